```python
import math
import jax, jax.numpy as jnp
from jax import lax
import numpy as np

D_MODEL = 1024
BATCH = 8
SEQ = 4096
DEPTH = 4

N_MIXERS = 3
N_A_LAYERS = (DEPTH + 2) // 3
N_B_LAYERS = (DEPTH + 1) // 3
N_C_LAYERS = DEPTH // 3
EPS = 1e-6

SB_HEAD_DIM = 64
SB_HEADS = D_MODEL // SB_HEAD_DIM
Q_BLOCK = 128

GM_CHUNK = 128
GM_HALF = 2 * D_MODEL
GM_GROUPS = 16
GM_GROUP_DIM = GM_HALF // GM_GROUPS

SSM_INNER = 2 * D_MODEL
SSM_HEAD_DIM = 64
SSM_HEADS = SSM_INNER // SSM_HEAD_DIM
SSM_GROUPS = 8
SSM_HPG = SSM_HEADS // SSM_GROUPS
SSM_STATE = 128
SSM_CONV = 4
SSM_CHUNK = 128
SSM_CONV_DIM = SSM_INNER + 2 * SSM_GROUPS * SSM_STATE
SSM_PROJ = SSM_INNER + SSM_CONV_DIM + SSM_HEADS

FFN_HIDDEN = -(-8 * D_MODEL // (3 * 256)) * 256

kernel_name = "hybrid_sb_gmlp_ssd_trunk"


def rmsnorm(x, g):
    xf = x.astype(jnp.float32)
    y = xf * lax.rsqrt(jnp.mean(xf * xf, axis=-1, keepdims=True) + EPS)
    return (y * g.astype(jnp.float32)).astype(x.dtype)


def stick_breaking_attention(h, w_qkv, q_gain, k_gain, w_o):
    B_, S, _ = h.shape
    qkv = (h @ w_qkv).reshape(B_, S, 3, SB_HEADS, SB_HEAD_DIM)
    q = rmsnorm(qkv[:, :, 0], q_gain).transpose(0, 2, 1, 3)
    k = rmsnorm(qkv[:, :, 1], k_gain).transpose(0, 2, 1, 3)
    v = qkv[:, :, 2].transpose(0, 2, 1, 3)
    scale = 1.0 / math.sqrt(SB_HEAD_DIM)
    nb = S // Q_BLOCK
    q_blocks = q.reshape(B_, SB_HEADS, nb, Q_BLOCK, SB_HEAD_DIM).transpose(2, 0, 1, 3, 4)
    key_pos = jnp.arange(S)

    def one_block(args):
        qb, start = args
        t = start + jnp.arange(Q_BLOCK)
        z = jnp.einsum('bhtd,bhsd->bhts', qb, k).astype(jnp.float32) * scale
        mask = key_pos[None, :] < t[:, None]
        log_beta = jax.nn.log_sigmoid(z)
        log_1m = jnp.where(mask, jax.nn.log_sigmoid(-z), 0.0)
        suffix = lax.cumsum(log_1m, axis=3, reverse=True) - log_1m
        w = jnp.where(mask, jnp.exp(log_beta + suffix), 0.0)
        return jnp.einsum('bhts,bhsd->bhtd', w.astype(v.dtype), v)

    out = lax.map(one_block, (q_blocks, jnp.arange(nb) * Q_BLOCK))
    out = out.transpose(1, 0, 3, 2, 4).reshape(B_, S, SB_HEADS * SB_HEAD_DIM)
    return out @ w_o


def chunked_gmlp(h, w_in, b_in, v_gain, w_s, b_s, w_out):
    B_, S, _ = h.shape
    zz = jax.nn.gelu(h @ w_in + b_in, approximate=False)
    u, v = zz[..., :GM_HALF], zz[..., GM_HALF:]
    v = rmsnorm(v, v_gain)
    nc = S // GM_CHUNK
    v = v.reshape(B_, nc, GM_CHUNK, GM_GROUPS, GM_GROUP_DIM)
    causal = jnp.tril(jnp.ones((GM_CHUNK, GM_CHUNK), dtype=bool))
    w_s_c = jnp.where(causal, w_s, 0.0).astype(v.dtype)
    mixed = jnp.einsum('gts,bnsgc->bntgc', w_s_c, v) + b_s.T[None, None, :, :, None]
    return (u * mixed.reshape(B_, S, GM_HALF)) @ w_out


def ssd_chunked(x, dt, A, Bm, Cm):
    B_, S, _, _ = x.shape
    nc = S // SSM_CHUNK
    L = SSM_CHUNK
    f32 = jnp.float32
    xdt = (x.astype(f32) * dt[..., None]).reshape(B_, nc, L, SSM_GROUPS, SSM_HPG, SSM_HEAD_DIM)
    a = (dt * A).reshape(B_, nc, L, SSM_GROUPS, SSM_HPG)
    Bc = Bm.astype(f32).reshape(B_, nc, L, SSM_GROUPS, SSM_STATE)
    Cc = Cm.astype(f32).reshape(B_, nc, L, SSM_GROUPS, SSM_STATE)
    a_cum = jnp.cumsum(a, axis=2)

    a_t = a_cum.transpose(0, 1, 3, 4, 2)
    seg = a_t[..., :, None] - a_t[..., None, :]
    causal = jnp.tril(jnp.ones((L, L), dtype=bool))
    decay = jnp.exp(jnp.where(causal, seg, -jnp.inf))
    cb = jnp.einsum('bclgn,bcsgn->bcgls', Cc, Bc)
    y_diag = jnp.einsum('bcgrls,bcsgrp->bclgrp', cb[:, :, :, None] * decay, xdt)

    decay_to_end = jnp.exp(a_cum[:, :, -1:] - a_cum)
    states = jnp.einsum('bclgn,bclgrp->bcgrpn', Bc, xdt * decay_to_end[..., None])
    chunk_decay = jnp.exp(a_cum[:, :, -1])

    def step(hstate, inp):
        st, dec = inp
        return hstate * dec[..., None, None] + st, hstate

    init = jnp.zeros((B_, SSM_GROUPS, SSM_HPG, SSM_HEAD_DIM, SSM_STATE), f32)
    _, prev = lax.scan(step, init, (states.transpose(1, 0, 2, 3, 4, 5),
                                    chunk_decay.transpose(1, 0, 2, 3)))
    prev = prev.transpose(1, 0, 2, 3, 4, 5)

    y_off = jnp.einsum('bclgn,bcgrpn->bclgrp', Cc, prev) * jnp.exp(a_cum)[..., None]
    y = (y_diag + y_off).reshape(B_, S, SSM_HEADS, SSM_HEAD_DIM)
    return y.astype(x.dtype)


def mamba2_mixer(h, w_in, conv_w, conv_b, dt_bias, a_log, d_skip, norm_gain, w_out):
    B_, S, _ = h.shape
    zxbcdt = h @ w_in
    z = zxbcdt[..., :SSM_INNER]
    xbc = zxbcdt[..., SSM_INNER:SSM_INNER + SSM_CONV_DIM]
    dt = zxbcdt[..., SSM_INNER + SSM_CONV_DIM:]
    xbc = lax.conv_general_dilated(
        xbc, conv_w[:, None, :].astype(xbc.dtype), window_strides=(1,),
        padding=[(SSM_CONV - 1, 0)], dimension_numbers=('NWC', 'WIO', 'NWC'),
        feature_group_count=SSM_CONV_DIM) + conv_b
    xbc = jax.nn.silu(xbc)
    xs = xbc[..., :SSM_INNER].reshape(B_, S, SSM_HEADS, SSM_HEAD_DIM)
    Bm = xbc[..., SSM_INNER:SSM_INNER + SSM_GROUPS * SSM_STATE].reshape(B_, S, SSM_GROUPS, SSM_STATE)
    Cm = xbc[..., SSM_INNER + SSM_GROUPS * SSM_STATE:].reshape(B_, S, SSM_GROUPS, SSM_STATE)
    dt = jax.nn.softplus(dt.astype(jnp.float32) + dt_bias.astype(jnp.float32))
    A = -jnp.exp(a_log.astype(jnp.float32))
    y = ssd_chunked(xs, dt, A, Bm, Cm)
    y = (y + xs * d_skip[:, None]).reshape(B_, S, SSM_INNER)
    yg = (y * jax.nn.silu(z)).astype(jnp.float32).reshape(B_, S, SSM_GROUPS, SSM_INNER // SSM_GROUPS)
    yg = yg * lax.rsqrt(jnp.mean(yg * yg, axis=-1, keepdims=True) + EPS)
    y = (yg.reshape(B_, S, SSM_INNER) * norm_gain.astype(jnp.float32)).astype(h.dtype)
    return y @ w_out


def swiglu(h, w_gu, w_down):
    gu = h @ w_gu
    return (jax.nn.silu(gu[..., :FFN_HIDDEN]) * gu[..., FFN_HIDDEN:]) @ w_down


def _fwd_setup_inputs(seed: int = 0) -> dict:
    key = jax.random.key(seed)
    ks = iter(jax.random.split(key, 40))

    def nrm(shape, scale):
        return jax.random.normal(next(ks), shape, jnp.float32) * scale

    def gain(shape):
        return 1.0 + nrm(shape, 0.02)

    D = D_MODEL
    x = nrm((BATCH, SEQ, D), 1.0)
    mix_norm = gain((DEPTH, D))
    ffn_norm = gain((DEPTH, D))
    sb_w_qkv = nrm((N_A_LAYERS, D, 3 * SB_HEADS * SB_HEAD_DIM), D ** -0.5)
    sb_q_gain = gain((N_A_LAYERS, SB_HEAD_DIM))
    sb_k_gain = gain((N_A_LAYERS, SB_HEAD_DIM))
    sb_w_o = nrm((N_A_LAYERS, SB_HEADS * SB_HEAD_DIM, D), (SB_HEADS * SB_HEAD_DIM) ** -0.5)
    gm_w_in = nrm((N_B_LAYERS, D, 2 * GM_HALF), D ** -0.5)
    gm_b_in = nrm((N_B_LAYERS, 2 * GM_HALF), 0.01)
    gm_v_gain = gain((N_B_LAYERS, GM_HALF))
    gm_w_s = nrm((N_B_LAYERS, GM_GROUPS, GM_CHUNK, GM_CHUNK), GM_CHUNK ** -0.5)
    gm_b_s = gain((N_B_LAYERS, GM_GROUPS, GM_CHUNK))
    gm_w_out = nrm((N_B_LAYERS, GM_HALF, D), GM_HALF ** -0.5)
    ssm_w_in = nrm((N_C_LAYERS, D, SSM_PROJ), D ** -0.5)
    ssm_conv_w = nrm((N_C_LAYERS, SSM_CONV, SSM_CONV_DIM), SSM_CONV ** -0.5)
    ssm_conv_b = nrm((N_C_LAYERS, SSM_CONV_DIM), 0.01)
    dt0 = jnp.exp(jax.random.uniform(next(ks), (N_C_LAYERS, SSM_HEADS), jnp.float32,
                                     math.log(1e-3), math.log(1e-1)))
    ssm_dt_bias = dt0 + jnp.log(-jnp.expm1(-dt0))
    ssm_a_log = jnp.log(jax.random.uniform(next(ks), (N_C_LAYERS, SSM_HEADS), jnp.float32, 1.0, 16.0))
    ssm_d = gain((N_C_LAYERS, SSM_HEADS))
    ssm_norm_gain = gain((N_C_LAYERS, SSM_INNER))
    ssm_w_out = nrm((N_C_LAYERS, SSM_INNER, D), SSM_INNER ** -0.5)
    ffn_w_gu = nrm((DEPTH, D, 2 * FFN_HIDDEN), D ** -0.5)
    ffn_w_down = nrm((DEPTH, FFN_HIDDEN, D), FFN_HIDDEN ** -0.5)
    return {
        "x": x, "mix_norm": mix_norm, "ffn_norm": ffn_norm,
        "sb_w_qkv": sb_w_qkv, "sb_q_gain": sb_q_gain, "sb_k_gain": sb_k_gain, "sb_w_o": sb_w_o,
        "gm_w_in": gm_w_in, "gm_b_in": gm_b_in, "gm_v_gain": gm_v_gain, "gm_w_s": gm_w_s,
        "gm_b_s": gm_b_s, "gm_w_out": gm_w_out,
        "ssm_w_in": ssm_w_in, "ssm_conv_w": ssm_conv_w, "ssm_conv_b": ssm_conv_b,
        "ssm_dt_bias": ssm_dt_bias, "ssm_a_log": ssm_a_log, "ssm_d": ssm_d,
        "ssm_norm_gain": ssm_norm_gain, "ssm_w_out": ssm_w_out,
        "ffn_w_gu": ffn_w_gu, "ffn_w_down": ffn_w_down,
    }


def _fwd_reference(x, mix_norm, ffn_norm,
              sb_w_qkv, sb_q_gain, sb_k_gain, sb_w_o,
              gm_w_in, gm_b_in, gm_v_gain, gm_w_s, gm_b_s, gm_w_out,
              ssm_w_in, ssm_conv_w, ssm_conv_b, ssm_dt_bias, ssm_a_log, ssm_d,
              ssm_norm_gain, ssm_w_out,
              ffn_w_gu, ffn_w_down):
    for i in range(DEPTH):
        h = rmsnorm(x, mix_norm[i])
        kind = i % N_MIXERS
        j = i // N_MIXERS
        if kind == 0:
            m = stick_breaking_attention(h, sb_w_qkv[j], sb_q_gain[j], sb_k_gain[j], sb_w_o[j])
        elif kind == 1:
            m = chunked_gmlp(h, gm_w_in[j], gm_b_in[j], gm_v_gain[j], gm_w_s[j], gm_b_s[j], gm_w_out[j])
        else:
            m = mamba2_mixer(h, ssm_w_in[j], ssm_conv_w[j], ssm_conv_b[j], ssm_dt_bias[j],
                             ssm_a_log[j], ssm_d[j], ssm_norm_gain[j], ssm_w_out[j])
        x = x + m
        x = x + swiglu(rmsnorm(x, ffn_norm[i]), ffn_w_gu[i], ffn_w_down[i])
    return x


import jax as _jax
import jax.numpy as _jnp

TWIN_FORMAT = 'train_step'
FWD_PARAMS = ['x', 'mix_norm', 'ffn_norm', 'sb_w_qkv', 'sb_q_gain', 'sb_k_gain', 'sb_w_o', 'gm_w_in', 'gm_b_in', 'gm_v_gain', 'gm_w_s', 'gm_b_s', 'gm_w_out', 'ssm_w_in', 'ssm_conv_w', 'ssm_conv_b', 'ssm_dt_bias', 'ssm_a_log', 'ssm_d', 'ssm_norm_gain', 'ssm_w_out', 'ffn_w_gu', 'ffn_w_down']
TWIN_WEIGHTS = ['mix_norm', 'ffn_norm', 'sb_w_qkv', 'sb_q_gain', 'sb_k_gain', 'sb_w_o', 'gm_w_in', 'gm_b_in', 'gm_v_gain', 'gm_w_s', 'gm_b_s', 'gm_w_out', 'ssm_w_in', 'ssm_conv_w', 'ssm_conv_b', 'ssm_dt_bias', 'ssm_a_log', 'ssm_d', 'ssm_norm_gain', 'ssm_w_out', 'ffn_w_gu', 'ffn_w_down']
TWIN_DIFF_INPUT = 'x'
TWIN_INPUTS = ['x', 'mix_norm', 'ffn_norm', 'sb_w_qkv', 'sb_q_gain', 'sb_k_gain', 'sb_w_o', 'gm_w_in', 'gm_b_in', 'gm_v_gain', 'gm_w_s', 'gm_b_s', 'gm_w_out', 'ssm_w_in', 'ssm_conv_w', 'ssm_conv_b', 'ssm_dt_bias', 'ssm_a_log', 'ssm_d', 'ssm_norm_gain', 'ssm_w_out', 'ffn_w_gu', 'ffn_w_down', 'loss_target', 'm_mix_norm', 'm_ffn_norm', 'm_sb_w_qkv', 'm_sb_q_gain', 'm_sb_k_gain', 'm_sb_w_o', 'm_gm_w_in', 'm_gm_b_in', 'm_gm_v_gain', 'm_gm_w_s', 'm_gm_b_s', 'm_gm_w_out', 'm_ssm_w_in', 'm_ssm_conv_w', 'm_ssm_conv_b', 'm_ssm_dt_bias', 'm_ssm_a_log', 'm_ssm_d', 'm_ssm_norm_gain', 'm_ssm_w_out', 'm_ffn_w_gu', 'm_ffn_w_down', 'v_mix_norm', 'v_ffn_norm', 'v_sb_w_qkv', 'v_sb_q_gain', 'v_sb_k_gain', 'v_sb_w_o', 'v_gm_w_in', 'v_gm_b_in', 'v_gm_v_gain', 'v_gm_w_s', 'v_gm_b_s', 'v_gm_w_out', 'v_ssm_w_in', 'v_ssm_conv_w', 'v_ssm_conv_b', 'v_ssm_dt_bias', 'v_ssm_a_log', 'v_ssm_d', 'v_ssm_norm_gain', 'v_ssm_w_out', 'v_ffn_w_gu', 'v_ffn_w_down']
TWIN_OUTPUTS = ['loss', 'grad_x', 'grad_mix_norm', 'grad_ffn_norm', 'grad_sb_w_qkv', 'grad_sb_q_gain', 'grad_sb_k_gain', 'grad_sb_w_o', 'grad_gm_w_in', 'grad_gm_b_in', 'grad_gm_v_gain', 'grad_gm_w_s', 'grad_gm_b_s', 'grad_gm_w_out', 'grad_ssm_w_in', 'grad_ssm_conv_w', 'grad_ssm_conv_b', 'grad_ssm_dt_bias', 'grad_ssm_a_log', 'grad_ssm_d', 'grad_ssm_norm_gain', 'grad_ssm_w_out', 'grad_ffn_w_gu', 'grad_ffn_w_down', 'delta_mix_norm', 'delta_ffn_norm', 'delta_sb_w_qkv', 'delta_sb_q_gain', 'delta_sb_k_gain', 'delta_sb_w_o', 'delta_gm_w_in', 'delta_gm_b_in', 'delta_gm_v_gain', 'delta_gm_w_s', 'delta_gm_b_s', 'delta_gm_w_out', 'delta_ssm_w_in', 'delta_ssm_conv_w', 'delta_ssm_conv_b', 'delta_ssm_dt_bias', 'delta_ssm_a_log', 'delta_ssm_d', 'delta_ssm_norm_gain', 'delta_ssm_w_out', 'delta_ffn_w_gu', 'delta_ffn_w_down', 'new_m_mix_norm', 'new_m_ffn_norm', 'new_m_sb_w_qkv', 'new_m_sb_q_gain', 'new_m_sb_k_gain', 'new_m_sb_w_o', 'new_m_gm_w_in', 'new_m_gm_b_in', 'new_m_gm_v_gain', 'new_m_gm_w_s', 'new_m_gm_b_s', 'new_m_gm_w_out', 'new_m_ssm_w_in', 'new_m_ssm_conv_w', 'new_m_ssm_conv_b', 'new_m_ssm_dt_bias', 'new_m_ssm_a_log', 'new_m_ssm_d', 'new_m_ssm_norm_gain', 'new_m_ssm_w_out', 'new_m_ffn_w_gu', 'new_m_ffn_w_down', 'new_v_mix_norm', 'new_v_ffn_norm', 'new_v_sb_w_qkv', 'new_v_sb_q_gain', 'new_v_sb_k_gain', 'new_v_sb_w_o', 'new_v_gm_w_in', 'new_v_gm_b_in', 'new_v_gm_v_gain', 'new_v_gm_w_s', 'new_v_gm_b_s', 'new_v_gm_w_out', 'new_v_ssm_w_in', 'new_v_ssm_conv_w', 'new_v_ssm_conv_b', 'new_v_ssm_dt_bias', 'new_v_ssm_a_log', 'new_v_ssm_d', 'new_v_ssm_norm_gain', 'new_v_ssm_w_out', 'new_v_ffn_w_gu', 'new_v_ffn_w_down']
TWIN_LEAF_KINDS = {'loss': 'loss', 'grad_x': 'grad_x', 'grad_mix_norm': 'grad_w', 'grad_ffn_norm': 'grad_w', 'grad_sb_w_qkv': 'grad_w', 'grad_sb_q_gain': 'grad_w', 'grad_sb_k_gain': 'grad_w', 'grad_sb_w_o': 'grad_w', 'grad_gm_w_in': 'grad_w', 'grad_gm_b_in': 'grad_w', 'grad_gm_v_gain': 'grad_w', 'grad_gm_w_s': 'grad_w', 'grad_gm_b_s': 'grad_w', 'grad_gm_w_out': 'grad_w', 'grad_ssm_w_in': 'grad_w', 'grad_ssm_conv_w': 'grad_w', 'grad_ssm_conv_b': 'grad_w', 'grad_ssm_dt_bias': 'grad_w', 'grad_ssm_a_log': 'grad_w', 'grad_ssm_d': 'grad_w', 'grad_ssm_norm_gain': 'grad_w', 'grad_ssm_w_out': 'grad_w', 'grad_ffn_w_gu': 'grad_w', 'grad_ffn_w_down': 'grad_w', 'delta_mix_norm': 'delta_w', 'delta_ffn_norm': 'delta_w', 'delta_sb_w_qkv': 'delta_w', 'delta_sb_q_gain': 'delta_w', 'delta_sb_k_gain': 'delta_w', 'delta_sb_w_o': 'delta_w', 'delta_gm_w_in': 'delta_w', 'delta_gm_b_in': 'delta_w', 'delta_gm_v_gain': 'delta_w', 'delta_gm_w_s': 'delta_w', 'delta_gm_b_s': 'delta_w', 'delta_gm_w_out': 'delta_w', 'delta_ssm_w_in': 'delta_w', 'delta_ssm_conv_w': 'delta_w', 'delta_ssm_conv_b': 'delta_w', 'delta_ssm_dt_bias': 'delta_w', 'delta_ssm_a_log': 'delta_w', 'delta_ssm_d': 'delta_w', 'delta_ssm_norm_gain': 'delta_w', 'delta_ssm_w_out': 'delta_w', 'delta_ffn_w_gu': 'delta_w', 'delta_ffn_w_down': 'delta_w', 'new_m_mix_norm': 'new_m', 'new_m_ffn_norm': 'new_m', 'new_m_sb_w_qkv': 'new_m', 'new_m_sb_q_gain': 'new_m', 'new_m_sb_k_gain': 'new_m', 'new_m_sb_w_o': 'new_m', 'new_m_gm_w_in': 'new_m', 'new_m_gm_b_in': 'new_m', 'new_m_gm_v_gain': 'new_m', 'new_m_gm_w_s': 'new_m', 'new_m_gm_b_s': 'new_m', 'new_m_gm_w_out': 'new_m', 'new_m_ssm_w_in': 'new_m', 'new_m_ssm_conv_w': 'new_m', 'new_m_ssm_conv_b': 'new_m', 'new_m_ssm_dt_bias': 'new_m', 'new_m_ssm_a_log': 'new_m', 'new_m_ssm_d': 'new_m', 'new_m_ssm_norm_gain': 'new_m', 'new_m_ssm_w_out': 'new_m', 'new_m_ffn_w_gu': 'new_m', 'new_m_ffn_w_down': 'new_m', 'new_v_mix_norm': 'new_v', 'new_v_ffn_norm': 'new_v', 'new_v_sb_w_qkv': 'new_v', 'new_v_sb_q_gain': 'new_v', 'new_v_sb_k_gain': 'new_v', 'new_v_sb_w_o': 'new_v', 'new_v_gm_w_in': 'new_v', 'new_v_gm_b_in': 'new_v', 'new_v_gm_v_gain': 'new_v', 'new_v_gm_w_s': 'new_v', 'new_v_gm_b_s': 'new_v', 'new_v_gm_w_out': 'new_v', 'new_v_ssm_w_in': 'new_v', 'new_v_ssm_conv_w': 'new_v', 'new_v_ssm_conv_b': 'new_v', 'new_v_ssm_dt_bias': 'new_v', 'new_v_ssm_a_log': 'new_v', 'new_v_ssm_d': 'new_v', 'new_v_ssm_norm_gain': 'new_v', 'new_v_ssm_w_out': 'new_v', 'new_v_ffn_w_gu': 'new_v', 'new_v_ffn_w_down': 'new_v'}


def _forward(args):
    return _fwd_reference(*[args[k] for k in FWD_PARAMS])


def _output_shape():
    out = _jax.eval_shape(lambda: _forward(_fwd_setup_inputs(0)))
    return out.shape, out.dtype

N_MICROBATCH = 1
ADAM_LR = 0.001
ADAM_B1 = 0.9
ADAM_B2 = 0.999
ADAM_EPS = 1e-08
ADAM_WD = 0.01
ADAM_STEP = 10
PER_EXAMPLE_BATCH_AXIS = {'x': 0, 'loss_target': 0}
SHARED_INPUTS = []
_WEIGHT_DTYPES = {'mix_norm': _jnp.float32, 'ffn_norm': _jnp.float32, 'sb_w_qkv': _jnp.float32, 'sb_q_gain': _jnp.float32, 'sb_k_gain': _jnp.float32, 'sb_w_o': _jnp.float32, 'gm_w_in': _jnp.float32, 'gm_b_in': _jnp.float32, 'gm_v_gain': _jnp.float32, 'gm_w_s': _jnp.float32, 'gm_b_s': _jnp.float32, 'gm_w_out': _jnp.float32, 'ssm_w_in': _jnp.float32, 'ssm_conv_w': _jnp.float32, 'ssm_conv_b': _jnp.float32, 'ssm_dt_bias': _jnp.float32, 'ssm_a_log': _jnp.float32, 'ssm_d': _jnp.float32, 'ssm_norm_gain': _jnp.float32, 'ssm_w_out': _jnp.float32, 'ffn_w_gu': _jnp.float32, 'ffn_w_down': _jnp.float32}
MOMENT_SCALE = {'mix_norm': 1.514705e+01, 'ffn_norm': 2.440780e+01, 'sb_w_qkv': 8.391113e-01, 'sb_q_gain': 2.914360e+01, 'sb_k_gain': 2.909717e+01, 'sb_w_o': 1.294676e+00, 'gm_w_in': 5.337572e-01, 'gm_b_in': 7.311729e+00, 'gm_v_gain': 3.183771e+00, 'gm_w_s': 1.954393e+00, 'gm_b_s': 6.945001e+00, 'gm_w_out': 4.273290e+00, 'ssm_w_in': 6.355316e-01, 'ssm_conv_w': 1.039154e+00, 'ssm_conv_b': 3.454516e+00, 'ssm_dt_bias': 7.020059e-01, 'ssm_a_log': 6.766963e+00, 'ssm_d': 7.121516e+00, 'ssm_norm_gain': 2.275592e+01, 'ssm_w_out': 2.998101e+00, 'ffn_w_gu': 4.344571e-01, 'ffn_w_down': 6.566771e-01}


def _to_microbatches(a, axis):
    t = _jnp.moveaxis(a, axis, 0)
    t = t.reshape((N_MICROBATCH, t.shape[0] // N_MICROBATCH) + t.shape[1:])
    return _jnp.moveaxis(t, 1, axis + 1)


def setup_inputs(seed: int = 0) -> dict:
    inp = _fwd_setup_inputs(seed)
    key = _jax.random.fold_in(_jax.random.key(seed), 7919)
    shape, _ = _output_shape()
    out = dict(inp)
    out["loss_target"] = _jax.random.normal(_jax.random.fold_in(key, 0), shape, _jnp.float32)
    for i, name in enumerate(TWIN_WEIGHTS):
        w = inp[name].astype(_jnp.float32)
        if MOMENT_SCALE is None:
            s = _jnp.sqrt(_jnp.mean(_jnp.square(w)) + 1e-30)
        else:
            s = MOMENT_SCALE[name]
        km, kv = _jax.random.split(_jax.random.fold_in(key, i + 1))
        out[name] = w
        out["m_" + name] = s * _jax.random.normal(km, w.shape, _jnp.float32)
        out["v_" + name] = (s * s) * _jax.random.uniform(kv, w.shape, _jnp.float32, 0.5, 1.5)
    if N_MICROBATCH > 1:
        for name, axis in PER_EXAMPLE_BATCH_AXIS.items():
            out[name] = _to_microbatches(out[name], axis)
    return {'x': out['x'], 'mix_norm': out['mix_norm'], 'ffn_norm': out['ffn_norm'], 'sb_w_qkv': out['sb_w_qkv'], 'sb_q_gain': out['sb_q_gain'], 'sb_k_gain': out['sb_k_gain'], 'sb_w_o': out['sb_w_o'], 'gm_w_in': out['gm_w_in'], 'gm_b_in': out['gm_b_in'], 'gm_v_gain': out['gm_v_gain'], 'gm_w_s': out['gm_w_s'], 'gm_b_s': out['gm_b_s'], 'gm_w_out': out['gm_w_out'], 'ssm_w_in': out['ssm_w_in'], 'ssm_conv_w': out['ssm_conv_w'], 'ssm_conv_b': out['ssm_conv_b'], 'ssm_dt_bias': out['ssm_dt_bias'], 'ssm_a_log': out['ssm_a_log'], 'ssm_d': out['ssm_d'], 'ssm_norm_gain': out['ssm_norm_gain'], 'ssm_w_out': out['ssm_w_out'], 'ffn_w_gu': out['ffn_w_gu'], 'ffn_w_down': out['ffn_w_down'], 'loss_target': out['loss_target'], 'm_mix_norm': out['m_mix_norm'], 'm_ffn_norm': out['m_ffn_norm'], 'm_sb_w_qkv': out['m_sb_w_qkv'], 'm_sb_q_gain': out['m_sb_q_gain'], 'm_sb_k_gain': out['m_sb_k_gain'], 'm_sb_w_o': out['m_sb_w_o'], 'm_gm_w_in': out['m_gm_w_in'], 'm_gm_b_in': out['m_gm_b_in'], 'm_gm_v_gain': out['m_gm_v_gain'], 'm_gm_w_s': out['m_gm_w_s'], 'm_gm_b_s': out['m_gm_b_s'], 'm_gm_w_out': out['m_gm_w_out'], 'm_ssm_w_in': out['m_ssm_w_in'], 'm_ssm_conv_w': out['m_ssm_conv_w'], 'm_ssm_conv_b': out['m_ssm_conv_b'], 'm_ssm_dt_bias': out['m_ssm_dt_bias'], 'm_ssm_a_log': out['m_ssm_a_log'], 'm_ssm_d': out['m_ssm_d'], 'm_ssm_norm_gain': out['m_ssm_norm_gain'], 'm_ssm_w_out': out['m_ssm_w_out'], 'm_ffn_w_gu': out['m_ffn_w_gu'], 'm_ffn_w_down': out['m_ffn_w_down'], 'v_mix_norm': out['v_mix_norm'], 'v_ffn_norm': out['v_ffn_norm'], 'v_sb_w_qkv': out['v_sb_w_qkv'], 'v_sb_q_gain': out['v_sb_q_gain'], 'v_sb_k_gain': out['v_sb_k_gain'], 'v_sb_w_o': out['v_sb_w_o'], 'v_gm_w_in': out['v_gm_w_in'], 'v_gm_b_in': out['v_gm_b_in'], 'v_gm_v_gain': out['v_gm_v_gain'], 'v_gm_w_s': out['v_gm_w_s'], 'v_gm_b_s': out['v_gm_b_s'], 'v_gm_w_out': out['v_gm_w_out'], 'v_ssm_w_in': out['v_ssm_w_in'], 'v_ssm_conv_w': out['v_ssm_conv_w'], 'v_ssm_conv_b': out['v_ssm_conv_b'], 'v_ssm_dt_bias': out['v_ssm_dt_bias'], 'v_ssm_a_log': out['v_ssm_a_log'], 'v_ssm_d': out['v_ssm_d'], 'v_ssm_norm_gain': out['v_ssm_norm_gain'], 'v_ssm_w_out': out['v_ssm_w_out'], 'v_ffn_w_gu': out['v_ffn_w_gu'], 'v_ffn_w_down': out['v_ffn_w_down']}


def _loss(weights, diff, rest, loss_target):
    with _jax.named_scope("forward"):
        args = {**rest, TWIN_DIFF_INPUT: diff, **{k: w.astype(_WEIGHT_DTYPES[k]) for k, w in weights.items()}}
        y = _forward(args)
    with _jax.named_scope("loss_head"):
        err = _jnp.square(y.astype(_jnp.float32) - loss_target)
        return 0.5 * _jnp.sum(_jnp.mean(err, axis=-1)) if err.ndim else 0.5 * err


def _adamw(w, g, m, v):
    m = ADAM_B1 * m + (1.0 - ADAM_B1) * g
    v = ADAM_B2 * v + (1.0 - ADAM_B2) * _jnp.square(g)
    m_hat = m / (1.0 - ADAM_B1 ** ADAM_STEP)
    v_hat = v / (1.0 - ADAM_B2 ** ADAM_STEP)
    delta = -ADAM_LR * (m_hat / (_jnp.sqrt(v_hat) + ADAM_EPS) + ADAM_WD * w)
    return delta, m, v


def reference(x, mix_norm, ffn_norm, sb_w_qkv, sb_q_gain, sb_k_gain, sb_w_o, gm_w_in, gm_b_in, gm_v_gain, gm_w_s, gm_b_s, gm_w_out, ssm_w_in, ssm_conv_w, ssm_conv_b, ssm_dt_bias, ssm_a_log, ssm_d, ssm_norm_gain, ssm_w_out, ffn_w_gu, ffn_w_down, loss_target, m_mix_norm, m_ffn_norm, m_sb_w_qkv, m_sb_q_gain, m_sb_k_gain, m_sb_w_o, m_gm_w_in, m_gm_b_in, m_gm_v_gain, m_gm_w_s, m_gm_b_s, m_gm_w_out, m_ssm_w_in, m_ssm_conv_w, m_ssm_conv_b, m_ssm_dt_bias, m_ssm_a_log, m_ssm_d, m_ssm_norm_gain, m_ssm_w_out, m_ffn_w_gu, m_ffn_w_down, v_mix_norm, v_ffn_norm, v_sb_w_qkv, v_sb_q_gain, v_sb_k_gain, v_sb_w_o, v_gm_w_in, v_gm_b_in, v_gm_v_gain, v_gm_w_s, v_gm_b_s, v_gm_w_out, v_ssm_w_in, v_ssm_conv_w, v_ssm_conv_b, v_ssm_dt_bias, v_ssm_a_log, v_ssm_d, v_ssm_norm_gain, v_ssm_w_out, v_ffn_w_gu, v_ffn_w_down):
    given = dict(x=x, mix_norm=mix_norm, ffn_norm=ffn_norm, sb_w_qkv=sb_w_qkv, sb_q_gain=sb_q_gain, sb_k_gain=sb_k_gain, sb_w_o=sb_w_o, gm_w_in=gm_w_in, gm_b_in=gm_b_in, gm_v_gain=gm_v_gain, gm_w_s=gm_w_s, gm_b_s=gm_b_s, gm_w_out=gm_w_out, ssm_w_in=ssm_w_in, ssm_conv_w=ssm_conv_w, ssm_conv_b=ssm_conv_b, ssm_dt_bias=ssm_dt_bias, ssm_a_log=ssm_a_log, ssm_d=ssm_d, ssm_norm_gain=ssm_norm_gain, ssm_w_out=ssm_w_out, ffn_w_gu=ffn_w_gu, ffn_w_down=ffn_w_down, loss_target=loss_target, m_mix_norm=m_mix_norm, m_ffn_norm=m_ffn_norm, m_sb_w_qkv=m_sb_w_qkv, m_sb_q_gain=m_sb_q_gain, m_sb_k_gain=m_sb_k_gain, m_sb_w_o=m_sb_w_o, m_gm_w_in=m_gm_w_in, m_gm_b_in=m_gm_b_in, m_gm_v_gain=m_gm_v_gain, m_gm_w_s=m_gm_w_s, m_gm_b_s=m_gm_b_s, m_gm_w_out=m_gm_w_out, m_ssm_w_in=m_ssm_w_in, m_ssm_conv_w=m_ssm_conv_w, m_ssm_conv_b=m_ssm_conv_b, m_ssm_dt_bias=m_ssm_dt_bias, m_ssm_a_log=m_ssm_a_log, m_ssm_d=m_ssm_d, m_ssm_norm_gain=m_ssm_norm_gain, m_ssm_w_out=m_ssm_w_out, m_ffn_w_gu=m_ffn_w_gu, m_ffn_w_down=m_ffn_w_down, v_mix_norm=v_mix_norm, v_ffn_norm=v_ffn_norm, v_sb_w_qkv=v_sb_w_qkv, v_sb_q_gain=v_sb_q_gain, v_sb_k_gain=v_sb_k_gain, v_sb_w_o=v_sb_w_o, v_gm_w_in=v_gm_w_in, v_gm_b_in=v_gm_b_in, v_gm_v_gain=v_gm_v_gain, v_gm_w_s=v_gm_w_s, v_gm_b_s=v_gm_b_s, v_gm_w_out=v_gm_w_out, v_ssm_w_in=v_ssm_w_in, v_ssm_conv_w=v_ssm_conv_w, v_ssm_conv_b=v_ssm_conv_b, v_ssm_dt_bias=v_ssm_dt_bias, v_ssm_a_log=v_ssm_a_log, v_ssm_d=v_ssm_d, v_ssm_norm_gain=v_ssm_norm_gain, v_ssm_w_out=v_ssm_w_out, v_ffn_w_gu=v_ffn_w_gu, v_ffn_w_down=v_ffn_w_down)
    weights = {n: given[n] for n in TWIN_WEIGHTS}
    shared = {n: given[n] for n in SHARED_INPUTS}
    per_example = {n: given[n] for n in ['x']}
    grad_fn = _jax.value_and_grad(_loss, argnums=(0, 1))

    def one_microbatch(ex, loss_target):
        ex = dict(ex)
        diff = ex.pop(TWIN_DIFF_INPUT)
        return grad_fn(weights, diff, {**shared, **ex}, loss_target)

    if N_MICROBATCH == 1:
        loss, (grad_w, grad_x) = one_microbatch(per_example, given["loss_target"])
    else:
        def body(carry, xs):
            loss_sum, grad_sum = carry
            l_k, (gw_k, gx_k) = one_microbatch(xs[0], xs[1])
            with _jax.named_scope("update"):
                return (loss_sum + l_k, _jax.tree.map(_jnp.add, grad_sum, gw_k)), gx_k

        init = (_jnp.zeros((), _jnp.float32), _jax.tree.map(_jnp.zeros_like, weights))
        (loss, grad_w), grad_x = _jax.lax.scan(body, init, (per_example, given["loss_target"]))
    with _jax.named_scope("update"):
        delta_w, new_m, new_v = {}, {}, {}
        for n in TWIN_WEIGHTS:
            delta_w[n], new_m[n], new_v[n] = _adamw(weights[n], grad_w[n], given["m_" + n], given["v_" + n])
    return (loss, grad_x, *[grad_w[n] for n in TWIN_WEIGHTS], *[delta_w[n] for n in TWIN_WEIGHTS],
            *[new_m[n] for n in TWIN_WEIGHTS], *[new_v[n] for n in TWIN_WEIGHTS])
```

```python
import functools
import math

import jax
import jax.numpy as jnp
from jax import lax
from jax.experimental import pallas as pl
from jax.experimental.pallas import tpu as pltpu

F32 = jnp.float32
BF16 = jnp.bfloat16
EPS = 1e-6
LANES = 128
SUBLANES = 8
VMEM_LIMIT = 56 * 1024 * 1024
HEAD = 64
CHUNK = 128
ADAM_LR, ADAM_B1, ADAM_B2, ADAM_EPS, ADAM_WD, ADAM_STEP = 0.001, 0.9, 0.999, 1e-08, 0.01, 10
MESH_ID = pl.DeviceIdType.MESH
ALL_AXES = ("x", "y", "c")


def _params(sem):
    return pltpu.CompilerParams(dimension_semantics=sem, vmem_limit_bytes=VMEM_LIMIT)


def _pick(n, cands):
    for c in cands:
        if n % c == 0:
            return c
    return n


def _dot(a, b, dims=((1,), (0,))):
    return lax.dot_general(a, b, (dims, ((), ())), preferred_element_type=F32)


def _dot_nt(a, b):
    return _dot(a, b, ((1,), (1,)))


def _dot_tn(a, b):
    return _dot(a, b, ((0,), (0,)))


def _split2(x):
    hi = x.astype(BF16)
    lo = (x - hi.astype(F32)).astype(BF16)
    return hi, lo


def _dot_x2(x, m):
    hi, lo = _split2(x)
    return _dot(hi, m) + _dot(lo, m)


def _dot_x3_left(m, x):
    h1 = x.astype(BF16)
    r1 = x - h1.astype(F32)
    h2 = r1.astype(BF16)
    h3 = (r1 - h2.astype(F32)).astype(BF16)
    return _dot(m, h1) + _dot(m, h2) + _dot(m, h3)


def _sigmoid(x):
    return 1.0 / (1.0 + jnp.exp(-x))


def _softplus(x):
    return jnp.maximum(x, 0.0) + jnp.log(1.0 + jnp.exp(-jnp.abs(x)))


def _colsum(x):
    return jnp.sum(x, axis=0, keepdims=True)


def _rowsum(x):
    return jnp.sum(x, axis=1, keepdims=True)


def _iota2(shape, dim):
    return lax.broadcasted_iota(jnp.int32, shape, dim)


def mm(a, b, *, ta=False, tb=False, add=None, bias=None, name):
    if ta:
        kk, m = a.shape
    else:
        m, kk = a.shape
    if tb:
        n, kb = b.shape
    else:
        kb, n = b.shape
    assert kk == kb, (a.shape, b.shape, ta, tb)
    tm = _pick(m, (512, 256, 128))
    tn = _pick(n, (512, 256, 128))
    tk = _pick(kk, (1024, 1408, 512, 256, 128))
    nk = kk // tk
    dims = ((0 if ta else 1,), (1 if tb else 0,))
    has_add, has_bias = add is not None, bias is not None

    def kern(*refs):
        a_ref, b_ref = refs[0], refs[1]
        rest = list(refs[2:])
        add_ref = rest.pop(0) if has_add else None
        bias_ref = rest.pop(0) if has_bias else None
        o_ref, acc_ref = rest
        k = pl.program_id(2)

        @pl.when(k == 0)
        def _():
            acc_ref[...] = jnp.zeros_like(acc_ref)

        acc_ref[...] += _dot(a_ref[...].astype(BF16), b_ref[...].astype(BF16), dims)

        @pl.when(k == nk - 1)
        def _():
            r = acc_ref[...]
            if has_add:
                r = r + add_ref[...]
            if has_bias:
                r = r + bias_ref[...]
            o_ref[...] = r

    a_spec = pl.BlockSpec((tk, tm), lambda i, j, k: (k, i)) if ta else pl.BlockSpec((tm, tk), lambda i, j, k: (i, k))
    b_spec = pl.BlockSpec((tn, tk), lambda i, j, k: (j, k)) if tb else pl.BlockSpec((tk, tn), lambda i, j, k: (k, j))
    in_specs, args = [a_spec, b_spec], [a, b]
    if has_add:
        in_specs.append(pl.BlockSpec((tm, tn), lambda i, j, k: (i, j)))
        args.append(add)
    if has_bias:
        in_specs.append(pl.BlockSpec((1, tn), lambda i, j, k: (0, j)))
        args.append(bias)
    return pl.pallas_call(
        kern,
        name=name,
        grid=(m // tm, n // tn, nk),
        in_specs=in_specs,
        out_specs=pl.BlockSpec((tm, tn), lambda i, j, k: (i, j)),
        out_shape=jax.ShapeDtypeStruct((m, n), F32),
        scratch_shapes=[pltpu.VMEM((tm, tn), F32)],
        compiler_params=_params(("parallel", "parallel", "arbitrary")),
    )(*args)


def rowwise(fn, ins, outs, accs=(), *, tr, name):
    rows = [a for a, kind in ins if kind == "row"][0].shape[0]
    tr = min(tr, rows)
    assert rows % tr == 0 and tr % SUBLANES == 0, (rows, tr)
    n = rows // tr
    n_in, n_out = len(ins), len(outs)
    kinds = [kind for _, kind in ins]

    def kern(*refs):
        i = pl.program_id(0)
        vals = []
        for ref, kind in zip(refs[:n_in], kinds):
            v = ref[...]
            if kind == "prev":
                v = v * (i > 0).astype(v.dtype)
            elif kind == "next":
                v = v * (i < n - 1).astype(v.dtype)
            vals.append(v)
        res = fn(*vals)
        for ref, r in zip(refs[n_in:n_in + n_out], res[:n_out]):
            ref[...] = r.astype(ref.dtype)
        if accs:
            acc_refs = refs[n_in + n_out:]

            @pl.when(i == 0)
            def _():
                for ref in acc_refs:
                    ref[...] = jnp.zeros_like(ref)

            for ref, r in zip(acc_refs, res[n_out:]):
                ref[...] += r

    in_specs = []
    for a, kind in ins:
        if kind == "row":
            in_specs.append(pl.BlockSpec((tr, a.shape[1]), lambda i: (i, 0)))
        elif kind == "full":
            in_specs.append(pl.BlockSpec(a.shape, lambda i, nd=a.ndim: (0,) * nd))
        elif kind == "prev":
            in_specs.append(pl.BlockSpec((SUBLANES, a.shape[1]),
                                         lambda i: (jnp.maximum(i * (tr // SUBLANES) - 1, 0), 0)))
        else:
            in_specs.append(pl.BlockSpec((SUBLANES, a.shape[1]),
                                         lambda i: (jnp.minimum((i + 1) * (tr // SUBLANES), rows // SUBLANES - 1), 0)))
    out_specs = [pl.BlockSpec((tr, c), lambda i: (i, 0)) for c, _ in outs]
    out_specs += [pl.BlockSpec((r, c), lambda i: (0, 0)) for r, c in accs]
    out_shape = [jax.ShapeDtypeStruct((rows, c), dt) for c, dt in outs]
    out_shape += [jax.ShapeDtypeStruct((r, c), F32) for r, c in accs]
    res = pl.pallas_call(
        kern,
        name=name,
        grid=(n,),
        in_specs=in_specs,
        out_specs=out_specs,
        out_shape=out_shape,
        compiler_params=_params(("arbitrary",) if accs else ("parallel",)),
    )(*[a for a, _ in ins])
    return res


def rms_fwd(x, g, name):
    def fn(xv, gv):
        r = lax.rsqrt(jnp.mean(xv * xv, axis=1, keepdims=True) + EPS)
        return (xv * r * gv,)

    return rowwise(fn, [(x, "row"), (g, "full")], [(x.shape[1], BF16)], tr=512, name=name)[0]


def rms_bwd(x, g, dy, dres, name):
    def fn(xv, gv, dyv, drv):
        r = lax.rsqrt(jnp.mean(xv * xv, axis=1, keepdims=True) + EPS)
        xh = xv * r
        dyg = dyv * gv
        dx = drv + r * (dyg - xh * jnp.mean(dyg * xh, axis=1, keepdims=True))
        return dx, _colsum(dyv * xh)

    c = x.shape[1]
    return rowwise(fn, [(x, "row"), (g, "full"), (dy, "row"), (dres, "row")], [(c, F32)], [(1, c)], tr=256, name=name)


def swiglu_fwd(gu, name):
    hid = gu.shape[1] // 2

    def fn(v):
        g, u = v[:, :hid], v[:, hid:]
        return (g * _sigmoid(g) * u,)

    return rowwise(fn, [(gu, "row")], [(hid, BF16)], tr=256, name=name)[0]


def swiglu_bwd(gu, da, name):
    hid = gu.shape[1] // 2

    def fn(v, d):
        g, u = v[:, :hid], v[:, hid:]
        s = _sigmoid(g)
        dg = d * u * s * (1.0 + g * (1.0 - s))
        du = d * g * s
        return (jnp.concatenate([dg, du], axis=1),)

    return rowwise(fn, [(gu, "row"), (da, "row")], [(2 * hid, BF16)], tr=256, name=name)[0]


def loss_and_grad(y, t, name):
    d = y.shape[1]

    def fn(yv, tv):
        e = yv - tv
        part = jnp.sum(_colsum(e * e), axis=1, keepdims=True) * (0.5 / d)
        return e * (1.0 / d), jnp.broadcast_to(part, (SUBLANES, LANES))

    dy, acc = rowwise(fn, [(y, "row"), (t, "row")], [(d, F32)], [(SUBLANES, LANES)], tr=512, name=name)
    return acc[0, 0], dy


def adamw(w, g, m, v, name):
    def fn(wv, gv, mv, vv):
        m2 = ADAM_B1 * mv + (1.0 - ADAM_B1) * gv
        v2 = ADAM_B2 * vv + (1.0 - ADAM_B2) * (gv * gv)
        m_hat = m2 / (1.0 - ADAM_B1 ** ADAM_STEP)
        v_hat = v2 / (1.0 - ADAM_B2 ** ADAM_STEP)
        delta = -ADAM_LR * (m_hat / (jnp.sqrt(v_hat) + ADAM_EPS) + ADAM_WD * wv)
        return delta, m2, v2

    rows, c = w.shape
    tr = _pick(rows, (256, 128, 64, 32, 16, 8)) if rows % SUBLANES == 0 else rows
    if rows % SUBLANES:
        return _whole(fn, [w, g, m, v], [(w.shape, F32)] * 3, name=name)
    return rowwise(fn, [(w, "row"), (g, "row"), (m, "row"), (v, "row")], [(c, F32)] * 3, tr=tr, name=name)


def _whole(fn, ins, outs, *, name):
    n_in = len(ins)

    def kern(*refs):
        res = fn(*[r[...] for r in refs[:n_in]])
        for ref, r in zip(refs[n_in:], res):
            ref[...] = r.astype(ref.dtype)

    return pl.pallas_call(
        kern,
        name=name,
        out_shape=[jax.ShapeDtypeStruct(s, dt) for s, dt in outs],
        compiler_params=pltpu.CompilerParams(vmem_limit_bytes=VMEM_LIMIT),
    )(*ins)


def ffn_fwd(x, g, wgu, wdown, tag):
    h = rms_fwd(x, g, f"ffn_rms_{tag}")
    gu = mm(h, wgu, name=f"ffn_gu_{tag}")
    a = swiglu_fwd(gu, f"ffn_act_{tag}")
    xn = mm(a, wdown, add=x, name=f"ffn_down_{tag}")
    return xn, (x, h, gu, a)


def ffn_bwd(dxn, saved, g, wgu, wdown, tag):
    x, h, gu, a = saved
    da = mm(dxn, wdown, tb=True, name=f"ffn_da_{tag}")
    dwdown = mm(a, dxn, ta=True, name=f"ffn_dwdown_{tag}")
    dgu = swiglu_bwd(gu, da, f"ffn_dact_{tag}")
    dh = mm(dgu, wgu, tb=True, name=f"ffn_dh_{tag}")
    dwgu = mm(h, dgu, ta=True, name=f"ffn_dwgu_{tag}")
    dx, dg = rms_bwd(x, g, dh, dxn, f"ffn_drms_{tag}")
    return dx, dg, dwgu, dwdown


def _head_blockdiag(c):
    i = jnp.arange(c) // HEAD
    return (i[:, None] == i[None, :]).astype(BF16)


def qknorm_fwd(qkv, qg, kg, bd, name):
    d = qkv.shape[1] // 3
    scale = 1.0 / math.sqrt(HEAD)

    def fn(v, qgv, kgv, bdv):
        q, k, vv = v[:, :d], v[:, d:2 * d], v[:, 2 * d:]
        rq = lax.rsqrt(_dot_x2(q * q, bdv) * (1.0 / HEAD) + EPS)
        rk = lax.rsqrt(_dot_x2(k * k, bdv) * (1.0 / HEAD) + EPS)
        return q * rq * qgv * scale, k * rk * kgv, vv

    return rowwise(fn, [(qkv, "row"), (qg, "full"), (kg, "full"), (bd, "full")],
                   [(d, BF16), (d, BF16), (d, BF16)], tr=256, name=name)


def qknorm_bwd(qkv, dqs, dkn, dv, qg, kg, bd, name):
    d = qkv.shape[1] // 3
    scale = 1.0 / math.sqrt(HEAD)

    def one(xv, gv, dyv, bdv):
        r = lax.rsqrt(_dot_x2(xv * xv, bdv) * (1.0 / HEAD) + EPS)
        xh = xv * r
        dyg = dyv * gv
        dx = r * (dyg - xh * (_dot_x2(dyg * xh, bdv) * (1.0 / HEAD)))
        return dx, _colsum(dyv * xh)

    def fn(v, dqv, dkv, dvv, qgv, kgv, bdv):
        q, k = v[:, :d], v[:, d:2 * d]
        dq, dqg = one(q, qgv, dqv * scale, bdv)
        dk, dkg = one(k, kgv, dkv, bdv)
        return jnp.concatenate([dq, dk, dvv], axis=1), dqg, dkg

    return rowwise(fn, [(qkv, "row"), (dqs, "row"), (dkn, "row"), (dv, "row"), (qg, "full"), (kg, "full"), (bd, "full")],
                   [(3 * d, BF16)], [(1, d), (1, d)], tr=256, name=name)


def _sb_tile(qh, k, mask, tri_gt):
    z = _dot_nt(qh, k)
    sp = jnp.log(1.0 + jnp.exp(-jnp.abs(z)))
    lb = jnp.minimum(z, 0.0) - sp
    l1 = jnp.where(mask, lb - z, 0.0)
    suf = _dot_x2(l1, tri_gt)
    return lb, l1, suf


def sb_attn_fwd(qs, kn, vb, name):
    s, d = qs.shape
    t = CHUNK
    nq = s // t
    assert nq <= LANES

    def kern(q_ref, k_ref, v_ref, o_ref, rs_ref):
        i = pl.program_id(1)
        row, col = _iota2((t, t), 0), _iota2((t, t), 1)
        tri_gt = (row > col).astype(BF16)
        lane = _iota2((1, LANES), 1)
        halves = [(lane < HEAD).astype(BF16), (lane >= HEAD).astype(BF16)]
        q = q_ref[...]
        qh = [q * hm for hm in halves]

        def step(n, carry):
            r, rt, acc = list(carry[0:2]), list(carry[2:4]), carry[4]
            kb = i - n
            ks = pl.multiple_of(kb * t, t)
            k = k_ref[pl.ds(ks, t), :]
            v = v_ref[pl.ds(ks, t), :]
            mask = col < row + n * t
            at_kb = col + jnp.minimum(row, 0) == kb
            for hh in range(2):
                lb, l1, suf = _sb_tile(qh[hh], k, mask, tri_gt)
                w = jnp.where(mask, jnp.exp(lb + suf + r[hh]), 0.0)
                acc = acc + _dot(w.astype(BF16), v * halves[hh])
                rt[hh] = jnp.where(at_kb, r[hh], rt[hh])
                r[hh] = r[hh] + _rowsum(l1)
            return (r[0], r[1], rt[0], rt[1], acc)

        zt = jnp.zeros((t, LANES), F32)
        res = lax.fori_loop(0, i + 1, step, (zt, zt, zt, zt, zt))
        o_ref[...] = res[4]
        rs_ref[0] = res[2]
        rs_ref[1] = res[3]

    nh2 = d // LANES
    return pl.pallas_call(
        kern,
        name=name,
        grid=(nh2, nq),
        in_specs=[pl.BlockSpec((t, LANES), lambda h, i: (i, h)),
                  pl.BlockSpec((s, LANES), lambda h, i: (0, h)),
                  pl.BlockSpec((s, LANES), lambda h, i: (0, h))],
        out_specs=[pl.BlockSpec((t, LANES), lambda h, i: (i, h)),
                   pl.BlockSpec((None, 2, t, LANES), lambda h, i: (h, 0, i, 0))],
        out_shape=[jax.ShapeDtypeStruct((s, d), F32), jax.ShapeDtypeStruct((nh2, 2, s, LANES), F32)],
        compiler_params=_params(("parallel", "arbitrary")),
    )(qs, kn, vb)


def sb_attn_bwd(qs, kn, vb, rsave, do, name):
    s, d = qs.shape
    t = CHUNK
    nq = s // t

    def kern(q_ref, k_ref, v_ref, rs_ref, do_ref, dq_ref, dk_ref, dv_ref):
        i = pl.program_id(1)

        @pl.when(i == 0)
        def _():
            dk_ref[...] = jnp.zeros_like(dk_ref)
            dv_ref[...] = jnp.zeros_like(dv_ref)

        row, col = _iota2((t, t), 0), _iota2((t, t), 1)
        tri_gt = (row > col).astype(BF16)
        tri_lt = (row < col).astype(BF16)
        lane = _iota2((1, LANES), 1)
        halves = [(lane < HEAD).astype(BF16), (lane >= HEAD).astype(BF16)]
        q = q_ref[...]
        qh = [q * hm for hm in halves]
        dov = do_ref[...].astype(BF16)
        doh = [dov * hm for hm in halves]
        rtile = [rs_ref[0], rs_ref[1]]

        def step(kb, carry):
            ep, dq = list(carry[0:2]), carry[2]
            ks = pl.multiple_of(kb * t, t)
            k = k_ref[pl.ds(ks, t), :]
            v = v_ref[pl.ds(ks, t), :]
            mask = col < row + (i - kb) * t
            onehot = (lane == kb).astype(F32)
            dk_t = jnp.zeros((t, LANES), F32)
            dv_t = jnp.zeros((t, LANES), F32)
            for hh in range(2):
                lb, l1, suf = _sb_tile(qh[hh], k, mask, tri_gt)
                r = _rowsum(rtile[hh] * onehot)
                w = jnp.where(mask, jnp.exp(lb + suf + r), 0.0)
                e = _dot_nt(doh[hh], v) * w
                pe = ep[hh] + _dot_x2(e, tri_lt)
                beta = jnp.exp(lb)
                dz = jnp.where(mask, e * (1.0 - beta) - pe * beta, 0.0).astype(BF16)
                dq = dq + _dot(dz, k * halves[hh])
                dk_t = dk_t + _dot_tn(dz, qh[hh])
                dv_t = dv_t + _dot_tn(w.astype(BF16), doh[hh])
                ep[hh] = ep[hh] + _rowsum(e)
            dk_ref[pl.ds(ks, t), :] += dk_t
            dv_ref[pl.ds(ks, t), :] += dv_t
            return (ep[0], ep[1], dq)

        z1 = jnp.zeros((t, 1), F32)
        res = lax.fori_loop(0, i + 1, step, (z1, z1, jnp.zeros((t, LANES), F32)))
        dq_ref[...] = res[2]

    nh2 = d // LANES
    return pl.pallas_call(
        kern,
        name=name,
        grid=(nh2, nq),
        in_specs=[pl.BlockSpec((t, LANES), lambda h, i: (i, h)),
                  pl.BlockSpec((s, LANES), lambda h, i: (0, h)),
                  pl.BlockSpec((s, LANES), lambda h, i: (0, h)),
                  pl.BlockSpec((None, 2, t, LANES), lambda h, i: (h, 0, i, 0)),
                  pl.BlockSpec((t, LANES), lambda h, i: (i, h))],
        out_specs=[pl.BlockSpec((t, LANES), lambda h, i: (i, h)),
                   pl.BlockSpec((s, LANES), lambda h, i: (0, h)),
                   pl.BlockSpec((s, LANES), lambda h, i: (0, h))],
        out_shape=[jax.ShapeDtypeStruct((s, d), F32)] * 3,
        compiler_params=_params(("parallel", "arbitrary")),
    )(qs, kn, vb, rsave, do)


def sb_fwd(x, g, wqkv, qg, kg, wo, bd, tag):
    h = rms_fwd(x, g, f"sb_rms_{tag}")
    qkv = mm(h, wqkv, name=f"sb_qkv_{tag}")
    qs, kn, vb = qknorm_fwd(qkv, qg, kg, bd, f"sb_qknorm_{tag}")
    o, rsave = sb_attn_fwd(qs, kn, vb, f"sb_attn_{tag}")
    xn = mm(o, wo, add=x, name=f"sb_out_{tag}")
    return xn, (x, h, qkv, qs, kn, vb, rsave, o)


def sb_bwd(dxn, saved, g, wqkv, qg, kg, wo, bd, tag):
    x, h, qkv, qs, kn, vb, rsave, o = saved
    do = mm(dxn, wo, tb=True, name=f"sb_do_{tag}")
    dwo = mm(o, dxn, ta=True, name=f"sb_dwo_{tag}")
    dqs, dkn, dv = sb_attn_bwd(qs, kn, vb, rsave, do, f"sb_dattn_{tag}")
    dqkv, dqg, dkg = qknorm_bwd(qkv, dqs, dkn, dv, qg, kg, bd, f"sb_dqknorm_{tag}")
    dh = mm(dqkv, wqkv, tb=True, name=f"sb_dh_{tag}")
    dwqkv = mm(h, dqkv, ta=True, name=f"sb_dwqkv_{tag}")
    dx, dg = rms_bwd(x, g, dh, dxn, f"sb_drms_{tag}")
    nh = dqg.shape[1] // HEAD
    return dx, dg, dwqkv, dqg.reshape(nh, HEAD).sum(0), dkg.reshape(nh, HEAD).sum(0), dwo


def _gelu(x):
    return 0.5 * x * (1.0 + lax.erf(x * (1.0 / math.sqrt(2.0))))


def _gelu_grad(x):
    return 0.5 * (1.0 + lax.erf(x * (1.0 / math.sqrt(2.0)))) + x * jnp.exp(-0.5 * x * x) * (1.0 / math.sqrt(2.0 * math.pi))


def gm_act_fwd(pre, vg, name):
    half = pre.shape[1] // 2

    def fn(p, vgv):
        u = _gelu(p[:, :half])
        v = _gelu(p[:, half:])
        r = lax.rsqrt(jnp.mean(v * v, axis=1, keepdims=True) + EPS)
        return u, v * r * vgv

    return rowwise(fn, [(pre, "row"), (vg, "full")], [(half, F32), (half, BF16)], tr=256, name=name)


def gm_act_bwd(pre, du, dvn, vg, name):
    half = pre.shape[1] // 2

    def fn(p, duv, dvnv, vgv):
        pu, pv = p[:, :half], p[:, half:]
        v = _gelu(pv)
        r = lax.rsqrt(jnp.mean(v * v, axis=1, keepdims=True) + EPS)
        vh = v * r
        dyg = dvnv * vgv
        dv = r * (dyg - vh * jnp.mean(dyg * vh, axis=1, keepdims=True))
        dpre = jnp.concatenate([duv * _gelu_grad(pu), dv * _gelu_grad(pv)], axis=1)
        return dpre, _colsum(dvnv * vh), _colsum(dpre)

    return rowwise(fn, [(pre, "row"), (du, "row"), (dvn, "row"), (vg, "full")],
                   [(2 * half, BF16)], [(1, half), (1, 2 * half)], tr=256, name=name)


def gm_spatial_fwd(u, vn, wc, bst, name):
    s, c = u.shape
    t = CHUNK
    ng = c // LANES

    def kern(u_ref, v_ref, w_ref, b_ref, o_ref):
        for g in range(ng):
            sl = slice(g * LANES, (g + 1) * LANES)
            mixed = _dot(w_ref[g], v_ref[:, sl]) + b_ref[:, sl]
            o_ref[:, sl] = (u_ref[:, sl] * mixed).astype(BF16)

    return pl.pallas_call(
        kern,
        name=name,
        grid=(s // t,),
        in_specs=[pl.BlockSpec((t, c), lambda i: (i, 0)), pl.BlockSpec((t, c), lambda i: (i, 0)),
                  pl.BlockSpec(wc.shape, lambda i: (0, 0, 0)), pl.BlockSpec(bst.shape, lambda i: (0, 0))],
        out_specs=pl.BlockSpec((t, c), lambda i: (i, 0)),
        out_shape=jax.ShapeDtypeStruct((s, c), BF16),
        compiler_params=_params(("parallel",)),
    )(u, vn, wc, bst)


def gm_spatial_bwd(dgate, u, vn, wc, bst, name):
    s, c = u.shape
    t = CHUNK
    ng = c // LANES

    def kern(dg_ref, u_ref, v_ref, w_ref, b_ref, du_ref, dv_ref, dw_ref, db_ref):
        i = pl.program_id(0)

        @pl.when(i == 0)
        def _():
            dw_ref[...] = jnp.zeros_like(dw_ref)
            db_ref[...] = jnp.zeros_like(db_ref)

        for g in range(ng):
            sl = slice(g * LANES, (g + 1) * LANES)
            vg = v_ref[:, sl]
            dgv = dg_ref[:, sl]
            mixed = _dot(w_ref[g], vg) + b_ref[:, sl]
            du_ref[:, sl] = dgv * mixed
            dmix = dgv * u_ref[:, sl]
            dmb = dmix.astype(BF16)
            dv_ref[:, sl] = _dot_tn(w_ref[g], dmb)
            dw_ref[g] += _dot_nt(dmb, vg)
            db_ref[:, sl] += dmix

    return pl.pallas_call(
        kern,
        name=name,
        grid=(s // t,),
        in_specs=[pl.BlockSpec((t, c), lambda i: (i, 0))] * 3 +
                 [pl.BlockSpec(wc.shape, lambda i: (0, 0, 0)), pl.BlockSpec(bst.shape, lambda i: (0, 0))],
        out_specs=[pl.BlockSpec((t, c), lambda i: (i, 0)), pl.BlockSpec((t, c), lambda i: (i, 0)),
                   pl.BlockSpec(wc.shape, lambda i: (0, 0, 0)), pl.BlockSpec(bst.shape, lambda i: (0, 0))],
        out_shape=[jax.ShapeDtypeStruct((s, c), F32), jax.ShapeDtypeStruct((s, c), F32),
                   jax.ShapeDtypeStruct(wc.shape, F32), jax.ShapeDtypeStruct(bst.shape, F32)],
        compiler_params=_params(("arbitrary",)),
    )(dgate, u, vn, wc, bst)


def gm_fwd(x, g, w_in, b_in, vg, wc, bst, w_out, tag):
    h = rms_fwd(x, g, f"gm_rms_{tag}")
    pre = mm(h, w_in, bias=b_in, name=f"gm_in_{tag}")
    u, vn = gm_act_fwd(pre, vg, f"gm_act_{tag}")
    gate = gm_spatial_fwd(u, vn, wc, bst, f"gm_spatial_{tag}")
    xn = mm(gate, w_out, add=x, name=f"gm_out_{tag}")
    return xn, (x, h, pre, u, vn, gate)


def gm_bwd(dxn, saved, g, w_in, vg, wc, bst, w_out, tag):
    x, h, pre, u, vn, gate = saved
    dgate = mm(dxn, w_out, tb=True, name=f"gm_dgate_{tag}")
    dwout = mm(gate, dxn, ta=True, name=f"gm_dwout_{tag}")
    du, dvn, dws, dbst = gm_spatial_bwd(dgate, u, vn, wc, bst, f"gm_dspatial_{tag}")
    dpre, dvg, dbin = gm_act_bwd(pre, du, dvn, vg, f"gm_dact_{tag}")
    dh = mm(dpre, w_in, tb=True, name=f"gm_dh_{tag}")
    dwin = mm(h, dpre, ta=True, name=f"gm_dwin_{tag}")
    dx, dg = rms_bwd(x, g, dh, dxn, f"gm_drms_{tag}")
    ng = wc.shape[0]
    dws = jnp.where(jnp.tril(jnp.ones((CHUNK, CHUNK), bool)), dws, 0.0)
    dbs = dbst.reshape(CHUNK, ng, LANES).sum(-1).T
    return dx, dg, dwin, dbin, dvg, dws, dbs, dwout


def _conv_taps(xv, prev):
    cat = jnp.concatenate([prev, xv], axis=0)
    return [pltpu.roll(cat, sh, 0)[SUBLANES:] for sh in (3, 2, 1)] + [xv]


def conv_fwd(xbc, ws, b, d_inner, name):
    c = xbc.shape[1]
    nst = (c - d_inner) // 2

    def fn(xv, prev, w0, w1, w2, w3, bv):
        taps = _conv_taps(xv, prev)
        pre = bv + w0 * taps[0] + w1 * taps[1] + w2 * taps[2] + w3 * taps[3]
        out = pre * _sigmoid(pre)
        return out[:, :d_inner], out[:, d_inner:d_inner + nst], out[:, d_inner + nst:]

    return rowwise(fn, [(xbc, "row"), (xbc, "prev")] + [(w, "full") for w in ws] + [(b, "full")],
                   [(d_inner, F32), (nst, F32), (nst, F32)], tr=256, name=name)


def conv_bwd_pre(xbc, ws, b, dxs_a, dxs_b, db_m, dc_m, name):
    c = xbc.shape[1]

    def fn(xv, prev, w0, w1, w2, w3, bv, da, db2, dbm, dcm):
        taps = _conv_taps(xv, prev)
        pre = bv + w0 * taps[0] + w1 * taps[1] + w2 * taps[2] + w3 * taps[3]
        sg = _sigmoid(pre)
        dout = jnp.concatenate([da + db2, dbm, dcm], axis=1)
        dpre = dout * sg * (1.0 + pre * (1.0 - sg))
        return (dpre,) + tuple(_colsum(dpre * tp) for tp in taps) + (_colsum(dpre),)

    return rowwise(fn, [(xbc, "row"), (xbc, "prev")] + [(w, "full") for w in ws] +
                   [(b, "full"), (dxs_a, "row"), (dxs_b, "row"), (db_m, "row"), (dc_m, "row")],
                   [(c, F32)], [(1, c)] * 5, tr=256, name=name)


def conv_bwd_in(dpre, ws, name):
    c = dpre.shape[1]

    def fn(dv, nxt, w0, w1, w2, w3):
        cat = jnp.concatenate([dv, nxt], axis=0)
        n = cat.shape[0]
        up = [pltpu.roll(cat, n - sh, 0)[:dv.shape[0]] for sh in (1, 2, 3)]
        return (w3 * dv + w2 * up[0] + w1 * up[1] + w0 * up[2],)

    return rowwise(fn, [(dpre, "row"), (dpre, "next")] + [(w, "full") for w in ws], [(c, BF16)], tr=256, name=name)[0]


def ssd_pre(dtr, bias, alog, name):
    def fn(d, bv, al, tri):
        dt = _softplus(d + bv)
        a = dt * (-jnp.exp(al))
        return dt, _dot_x3_left(tri, a)

    tri = jnp.tril(jnp.ones((CHUNK, CHUNK), BF16))
    return rowwise(fn, [(dtr, "row"), (bias, "full"), (alog, "full"), (tri, "full")],
                   [(LANES, F32), (LANES, F32)], tr=CHUNK, name=name)


def _ssd_layouts(v, ngroups, hpg):
    s = v.shape[0]
    col = v[:, :ngroups * hpg].T.reshape(ngroups, hpg, s, 1)
    return jnp.broadcast_to(col, (ngroups, hpg, s, LANES))


def _ssd_rowform(acum, ngroups, hpg):
    s = acum.shape[0]
    nc = s // CHUNK
    a = acum[:, :ngroups * hpg].reshape(nc, CHUNK, ngroups, hpg).transpose(2, 0, 3, 1)
    last = jnp.broadcast_to(a[..., CHUNK - 1:], a.shape)
    return jnp.concatenate([a, last], axis=2)


def ssd_chunk_fwd(xs, bm, cm, col_a, col_dt, rowf, name):
    s, d_inner = xs.shape
    ln = CHUNK
    nc = s // ln
    ng, hpg = col_a.shape[0], col_a.shape[1]
    gw = d_inner // ng
    assert gw == hpg * HEAD and gw % LANES == 0 and bm.shape[1] == ng * LANES

    def kern(x_ref, b_ref, c_ref, ca_ref, cd_ref, rf_ref, y_ref, hp_ref, h_scr):
        c = pl.program_id(1)

        @pl.when(c == 0)
        def _():
            h_scr[...] = jnp.zeros_like(h_scr)

        bb = b_ref[...].astype(BF16)
        cbf = c_ref[...].astype(BF16)
        cb = _dot_nt(cbf, bb)
        causal = _iota2((ln, ln), 0) >= _iota2((ln, ln), 1)
        lane = _iota2((1, LANES), 1)
        ys = [jnp.zeros((ln, LANES), F32) for _ in range(gw // LANES)]
        for r in range(hpg):
            j, hf = divmod(r, LANES // HEAD)
            mh = ((lane >= HEAD * hf) & (lane < HEAD * (hf + 1))).astype(F32)
            ac = ca_ref[r]
            ar = rf_ref[pl.ds(r, 1), :]
            aend = rf_ref[pl.ds(4 + r, 1), :]
            dm = jnp.exp(jnp.minimum(ac - ar, 0.0))
            m = jnp.where(causal, cb * dm, 0.0).astype(BF16)
            xdt = x_ref[:, j * LANES:(j + 1) * LANES] * cd_ref[r] * mh
            h = h_scr[r]
            hp_ref[r] = h
            ys[j] = ys[j] + _dot(m, xdt.astype(BF16)) + _dot_nt(cbf, h.astype(BF16)) * jnp.exp(ac)
            dte = jnp.exp(aend - ac)
            h_scr[r] = jnp.exp(aend) * h + _dot_tn((xdt * dte).astype(BF16), bb)
        for j in range(gw // LANES):
            y_ref[:, j * LANES:(j + 1) * LANES] = ys[j]

    return pl.pallas_call(
        kern,
        name=name,
        grid=(ng, nc),
        in_specs=[pl.BlockSpec((ln, gw), lambda g, c: (c, g)),
                  pl.BlockSpec((ln, LANES), lambda g, c: (c, g)),
                  pl.BlockSpec((ln, LANES), lambda g, c: (c, g)),
                  pl.BlockSpec((None, hpg, ln, LANES), lambda g, c: (g, 0, c, 0)),
                  pl.BlockSpec((None, hpg, ln, LANES), lambda g, c: (g, 0, c, 0)),
                  pl.BlockSpec((None, None, 8, LANES), lambda g, c: (g, c, 0, 0))],
        out_specs=[pl.BlockSpec((ln, gw), lambda g, c: (c, g)),
                   pl.BlockSpec((None, None, hpg, LANES, LANES), lambda g, c: (g, c, 0, 0, 0))],
        out_shape=[jax.ShapeDtypeStruct((s, d_inner), F32),
                   jax.ShapeDtypeStruct((ng, nc, hpg, LANES, LANES), F32)],
        scratch_shapes=[pltpu.VMEM((hpg, LANES, LANES), F32)],
        compiler_params=_params(("parallel", "arbitrary")),
    )(xs, bm, cm, col_a, col_dt, rowf)


def ssd_chunk_bwd(xs, bm, cm, col_a, col_dt, rowf, hprev, dy, name):
    s, d_inner = xs.shape
    ln = CHUNK
    nc = s // ln
    ng, hpg = col_a.shape[0], col_a.shape[1]
    gw = d_inner // ng

    def kern(x_ref, b_ref, c_ref, ca_ref, cd_ref, rf_ref, hp_ref, dy_ref,
             dx_ref, db_ref, dc_ref, ddt_ref, da_ref, dh_scr):
        c = pl.program_id(1)

        @pl.when(c == 0)
        def _():
            dh_scr[...] = jnp.zeros_like(dh_scr)

        bb = b_ref[...].astype(BF16)
        cbf = c_ref[...].astype(BF16)
        cb = _dot_nt(cbf, bb)
        row, col = _iota2((ln, ln), 0), _iota2((ln, ln), 1)
        causal = row >= col
        tri_ge = (col >= row).astype(BF16)
        ones = jnp.ones((ln, LANES), BF16)
        lane = _iota2((1, LANES), 1)
        last_row = (_iota2((ln, 1), 0) == ln - 1).astype(F32)
        dcb = jnp.zeros((ln, ln), F32)
        d_b = jnp.zeros((ln, LANES), F32)
        d_c = jnp.zeros((ln, LANES), F32)
        dxs = [jnp.zeros((ln, LANES), F32) for _ in range(gw // LANES)]
        for r in range(hpg):
            j, hf = divmod(r, LANES // HEAD)
            mh = ((lane >= HEAD * hf) & (lane < HEAD * (hf + 1))).astype(F32)
            ac = ca_ref[r]
            dt = cd_ref[r]
            ar = rf_ref[pl.ds(r, 1), :]
            aend = rf_ref[pl.ds(4 + r, 1), :]
            dm = jnp.where(causal, jnp.exp(jnp.minimum(ac - ar, 0.0)), 0.0)
            m = cb * dm
            mb = m.astype(BF16)
            xp = x_ref[:, j * LANES:(j + 1) * LANES]
            xdt = xp * dt * mh
            xdtb = xdt.astype(BF16)
            dyp = dy_ref[:, j * LANES:(j + 1) * LANES] * mh
            dypb = dyp.astype(BF16)
            h = hp_ref[r]
            hb = h.astype(BF16)
            dh = dh_scr[r]
            dhb = dh.astype(BF16)
            e_in = jnp.exp(ac)
            dte = jnp.exp(aend - ac)
            eend = jnp.exp(aend)
            d_m = _dot_nt(dypb, xdtb)
            dcb = dcb + d_m * dm
            gm = d_m * m
            yoff_pre = _dot_nt(cbf, hb)
            bdh = _dot_nt(bb, dhb)
            dxdt = _dot_tn(mb, dypb) + bdh * dte
            t1 = _rowsum(xdt * bdh) * dte
            gh, gl = _split2(gm)
            dacum = (_rowsum(gm) - (_dot_tn(gh, ones) + _dot_tn(gl, ones))
                     + _rowsum(dyp * yoff_pre) * e_in - t1)
            end_term = _colsum(t1) + eend * jnp.sum(_colsum(dh * h), axis=1, keepdims=True)
            dacum = dacum + last_row * end_term
            da_ref[r] = _dot_x3_left(tri_ge, dacum)
            ddt_ref[r] = jnp.broadcast_to(_rowsum(dxdt * xp), (ln, LANES))
            dxs[j] = dxs[j] + dxdt * dt
            d_b = d_b + _dot((xdt * dte).astype(BF16), dhb)
            dye = (dyp * e_in).astype(BF16)
            d_c = d_c + _dot(dye, hb)
            dh_scr[r] = eend * dh + _dot_tn(dye, cbf)
        dcbb = dcb.astype(BF16)
        dc_ref[...] = d_c + _dot(dcbb, bb)
        db_ref[...] = d_b + _dot_tn(dcbb, cbf)
        for j in range(gw // LANES):
            dx_ref[:, j * LANES:(j + 1) * LANES] = dxs[j]

    rev = nc - 1
    colspec = pl.BlockSpec((None, hpg, ln, LANES), lambda g, c: (g, 0, rev - c, 0))
    return pl.pallas_call(
        kern,
        name=name,
        grid=(ng, nc),
        in_specs=[pl.BlockSpec((ln, gw), lambda g, c: (rev - c, g)),
                  pl.BlockSpec((ln, LANES), lambda g, c: (rev - c, g)),
                  pl.BlockSpec((ln, LANES), lambda g, c: (rev - c, g)),
                  colspec, colspec,
                  pl.BlockSpec((None, None, 8, LANES), lambda g, c: (g, rev - c, 0, 0)),
                  pl.BlockSpec((None, None, hpg, LANES, LANES), lambda g, c: (g, rev - c, 0, 0, 0)),
                  pl.BlockSpec((ln, gw), lambda g, c: (rev - c, g))],
        out_specs=[pl.BlockSpec((ln, gw), lambda g, c: (rev - c, g)),
                   pl.BlockSpec((ln, LANES), lambda g, c: (rev - c, g)),
                   pl.BlockSpec((ln, LANES), lambda g, c: (rev - c, g)),
                   colspec, colspec],
        out_shape=[jax.ShapeDtypeStruct((s, d_inner), F32),
                   jax.ShapeDtypeStruct(bm.shape, F32), jax.ShapeDtypeStruct(cm.shape, F32),
                   jax.ShapeDtypeStruct(col_a.shape, F32), jax.ShapeDtypeStruct(col_a.shape, F32)],
        scratch_shapes=[pltpu.VMEM((hpg, LANES, LANES), F32)],
        compiler_params=_params(("parallel", "arbitrary")),
    )(xs, bm, cm, col_a, col_dt, rowf, hprev, dy)


def gnorm_fwd(y, xs, z, dexp, gain, ngroups, name):
    c = y.shape[1]
    gw = c // ngroups

    def fn(yv, xv, zv, dv, gv):
        yg = (yv + xv * dv) * (zv * _sigmoid(zv))
        outs = []
        for k in range(ngroups):
            t = yg[:, k * gw:(k + 1) * gw]
            outs.append(t * lax.rsqrt(jnp.mean(t * t, axis=1, keepdims=True) + EPS))
        return (jnp.concatenate(outs, axis=1) * gv,)

    return rowwise(fn, [(y, "row"), (xs, "row"), (z, "row"), (dexp, "full"), (gain, "full")], [(c, BF16)], tr=256, name=name)[0]


def gnorm_bwd(dn, y, xs, z, dexp, gain, ngroups, name):
    c = y.shape[1]
    gw = c // ngroups

    def fn(dnv, yv, xv, zv, dv, gv):
        yd = yv + xv * dv
        sg = _sigmoid(zv)
        sz = zv * sg
        yg = yd * sz
        dng = dnv * gv
        dyg, yh = [], []
        for k in range(ngroups):
            sl = slice(k * gw, (k + 1) * gw)
            t = yg[:, sl]
            r = lax.rsqrt(jnp.mean(t * t, axis=1, keepdims=True) + EPS)
            th = t * r
            dyg.append(r * (dng[:, sl] - th * jnp.mean(dng[:, sl] * th, axis=1, keepdims=True)))
            yh.append(th)
        dyg = jnp.concatenate(dyg, axis=1)
        yh = jnp.concatenate(yh, axis=1)
        dyd = dyg * sz
        dz = dyg * yd * (sg * (1.0 + zv * (1.0 - sg)))
        return dyd, dyd * dv, dz, _colsum(dyd * xv), _colsum(dnv * yh)

    return rowwise(fn, [(dn, "row"), (y, "row"), (xs, "row"), (z, "row"), (dexp, "full"), (gain, "full")],
                   [(c, F32), (c, F32), (c, BF16)], [(1, c), (1, c)], tr=256, name=name)


def ssd_post(ddt, da, dt, dtr, bias, alog, name):
    def fn(ddtv, dav, dtv, dtrv, bv, al):
        a_neg = -jnp.exp(al)
        ddtr = (ddtv + dav * a_neg) * _sigmoid(dtrv + bv)
        return ddtr, _colsum(ddtr), _colsum(dav * dtv) * a_neg

    return rowwise(fn, [(ddt, "row"), (da, "row"), (dt, "row"), (dtr, "row"), (bias, "full"), (alog, "full")],
                   [(LANES, BF16)], [(1, LANES), (1, LANES)], tr=512, name=name)


def _from_colform(v, s):
    ng, hpg = v.shape[0], v.shape[1]
    flat = v[..., 0].reshape(ng * hpg, s).T
    return jnp.pad(flat, ((0, 0), (0, LANES - ng * hpg)))


def ssm_fwd(x, g, p, tag):
    ng, hpg, d_inner = p["ng"], p["hpg"], p["d_inner"]
    h = rms_fwd(x, g, f"ssm_rms_{tag}")
    z = mm(h, p["w_z"], name=f"ssm_inz_{tag}")
    xbc = mm(h, p["w_xbc"], name=f"ssm_inx_{tag}")
    dtr = mm(h, p["w_dt"], name=f"ssm_indt_{tag}")
    xs, bm, cm = conv_fwd(xbc, p["conv_w"], p["conv_b"], d_inner, f"ssm_conv_{tag}")
    dt, acum = ssd_pre(dtr, p["dt_bias"], p["a_log"], f"ssm_pre_{tag}")
    col_a, col_dt = _ssd_layouts(acum, ng, hpg), _ssd_layouts(dt, ng, hpg)
    rowf = _ssd_rowform(acum, ng, hpg)
    y, hprev = ssd_chunk_fwd(xs, bm, cm, col_a, col_dt, rowf, f"ssm_scan_{tag}")
    n = gnorm_fwd(y, xs, z, p["d_exp"], p["norm_gain"], ng, f"ssm_gnorm_{tag}")
    xn = mm(n, p["w_out"], add=x, name=f"ssm_out_{tag}")
    return xn, (x, h, z, xbc, dtr, xs, bm, cm, dt, col_a, col_dt, rowf, y, hprev, n)


def ssm_bwd(dxn, saved, g, p, tag):
    x, h, z, xbc, dtr, xs, bm, cm, dt, col_a, col_dt, rowf, y, hprev, n = saved
    ng, hpg, d_inner = p["ng"], p["hpg"], p["d_inner"]
    s = x.shape[0]
    dn = mm(dxn, p["w_out"], tb=True, name=f"ssm_dn_{tag}")
    dwout = mm(n, dxn, ta=True, name=f"ssm_dwout_{tag}")
    dy, dxs_skip, dz, dd_lane, dgain = gnorm_bwd(dn, y, xs, z, p["d_exp"], p["norm_gain"], ng, f"ssm_dgnorm_{tag}")
    dxs, dbm, dcm, ddt_c, da_c = ssd_chunk_bwd(xs, bm, cm, col_a, col_dt, rowf, hprev, dy, f"ssm_dscan_{tag}")
    ddtr, dbias, dalog = ssd_post(_from_colform(ddt_c, s), _from_colform(da_c, s), dt, dtr,
                                  p["dt_bias"], p["a_log"], f"ssm_post_{tag}")
    res = conv_bwd_pre(xbc, p["conv_w"], p["conv_b"], dxs, dxs_skip, dbm, dcm, f"ssm_dconv_{tag}")
    dpre, dconv_w, dconv_b = res[0], jnp.concatenate(res[1:5], axis=0), res[5]
    dxbc = conv_bwd_in(dpre, p["conv_w"], f"ssm_dconvin_{tag}")
    dh = mm(dz, p["w_z"], tb=True, name=f"ssm_dhz_{tag}")
    dh = mm(dxbc, p["w_xbc"], tb=True, add=dh, name=f"ssm_dhx_{tag}")
    dh = mm(ddtr, p["w_dt"], tb=True, add=dh, name=f"ssm_dhdt_{tag}")
    dwz = mm(h, dz, ta=True, name=f"ssm_dwz_{tag}")
    dwxbc = mm(h, dxbc, ta=True, name=f"ssm_dwxbc_{tag}")
    dwdt = mm(h, ddtr, ta=True, name=f"ssm_dwdt_{tag}")
    dx, dg = rms_bwd(x, g, dh, dxn, f"ssm_drms_{tag}")
    nh = ng * hpg
    dwin = jnp.concatenate([dwz, dwxbc, dwdt[:, :nh]], axis=1)
    dd = dd_lane.reshape(nh, HEAD).sum(-1)
    return dx, dg, dict(w_in=dwin, conv_w=dconv_w, conv_b=dconv_b, dt_bias=dbias[0, :nh], a_log=dalog[0, :nh],
                        d=dd, norm_gain=dgain, w_out=dwout)


def local_step(x, target, w):
    d = x.shape[1]
    depth = w["mix_norm"].shape[0]
    bd = _head_blockdiag(d)
    tril = jnp.tril(jnp.ones((CHUNK, CHUNK), bool))
    ssm_heads = w["ssm_dt_bias"].shape[1]
    d_inner = w["ssm_w_out"].shape[1]
    ng = w["ssm_norm_gain"].shape[1] // 256
    nstate = CHUNK

    def pad_lanes(v):
        return jnp.pad(v, ((0, 0), (0, LANES - v.shape[1])))

    def ssm_params(j):
        w_in = w["ssm_w_in"][j]
        cw = w["ssm_conv_w"][j]
        return dict(ng=ng, hpg=ssm_heads // ng, d_inner=d_inner,
                    w_z=w_in[:, :d_inner], w_xbc=w_in[:, d_inner:d_inner + d_inner + 2 * ng * nstate],
                    w_dt=pad_lanes(w_in[:, 2 * d_inner + 2 * ng * nstate:]),
                    conv_w=[cw[k:k + 1] for k in range(cw.shape[0])], conv_b=w["ssm_conv_b"][j:j + 1],
                    dt_bias=pad_lanes(w["ssm_dt_bias"][j:j + 1]), a_log=pad_lanes(w["ssm_a_log"][j:j + 1]),
                    d_exp=jnp.repeat(w["ssm_d"][j], HEAD)[None, :], norm_gain=w["ssm_norm_gain"][j:j + 1],
                    w_out=w["ssm_w_out"][j])

    def gm_params(j):
        wc = jnp.where(tril, w["gm_w_s"][j], 0.0).astype(BF16)
        bst = jnp.repeat(w["gm_b_s"][j].T, LANES, axis=1)
        return wc, bst

    def sb_gains(j):
        nh = d // HEAD
        return jnp.tile(w["sb_q_gain"][j], nh)[None, :], jnp.tile(w["sb_k_gain"][j], nh)[None, :]

    saved = []
    cur = x
    for i in range(depth):
        kind, j = i % 3, i // 3
        gmix = w["mix_norm"][i:i + 1]
        if kind == 0:
            qg, kg = sb_gains(j)
            cur, sv = sb_fwd(cur, gmix, w["sb_w_qkv"][j], qg, kg, w["sb_w_o"][j], bd, f"{i}")
        elif kind == 1:
            wc, bst = gm_params(j)
            cur, sv = gm_fwd(cur, gmix, w["gm_w_in"][j], w["gm_b_in"][j:j + 1], w["gm_v_gain"][j:j + 1], wc, bst,
                             w["gm_w_out"][j], f"{i}")
        else:
            cur, sv = ssm_fwd(cur, gmix, ssm_params(j), f"{i}")
        cur, sv2 = ffn_fwd(cur, w["ffn_norm"][i:i + 1], w["ffn_w_gu"][i], w["ffn_w_down"][i], f"{i}")
        saved.append((sv, sv2))

    loss, dcur = loss_and_grad(cur, target, "loss")

    grads = {k: [None] * v.shape[0] for k, v in w.items()}
    for i in reversed(range(depth)):
        kind, j = i % 3, i // 3
        sv, sv2 = saved[i]
        gmix = w["mix_norm"][i:i + 1]
        dcur, dgf, dwgu, dwdown = ffn_bwd(dcur, sv2, w["ffn_norm"][i:i + 1], w["ffn_w_gu"][i], w["ffn_w_down"][i], f"{i}")
        grads["ffn_norm"][i], grads["ffn_w_gu"][i], grads["ffn_w_down"][i] = dgf[0], dwgu, dwdown
        if kind == 0:
            qg, kg = sb_gains(j)
            dcur, dg, dwqkv, dqg, dkg, dwo = sb_bwd(dcur, sv, gmix, w["sb_w_qkv"][j], qg, kg, w["sb_w_o"][j], bd, f"{i}")
            grads["sb_w_qkv"][j], grads["sb_q_gain"][j], grads["sb_k_gain"][j], grads["sb_w_o"][j] = dwqkv, dqg, dkg, dwo
        elif kind == 1:
            wc, bst = gm_params(j)
            dcur, dg, dwin, dbin, dvg, dws, dbs, dwout = gm_bwd(dcur, sv, gmix, w["gm_w_in"][j], w["gm_v_gain"][j:j + 1],
                                                                 wc, bst, w["gm_w_out"][j], f"{i}")
            grads["gm_w_in"][j], grads["gm_b_in"][j], grads["gm_v_gain"][j] = dwin, dbin[0], dvg[0]
            grads["gm_w_s"][j], grads["gm_b_s"][j], grads["gm_w_out"][j] = dws, dbs, dwout
        else:
            dcur, dg, gs = ssm_bwd(dcur, sv, gmix, ssm_params(j), f"{i}")
            grads["ssm_w_in"][j], grads["ssm_conv_w"][j], grads["ssm_conv_b"][j] = gs["w_in"], gs["conv_w"], gs["conv_b"][0]
            grads["ssm_dt_bias"][j], grads["ssm_a_log"][j], grads["ssm_d"][j] = gs["dt_bias"], gs["a_log"], gs["d"]
            grads["ssm_norm_gain"][j], grads["ssm_w_out"][j] = gs["norm_gain"][0], gs["w_out"]
        grads["mix_norm"][i] = dg[0]
    grads = {k: jnp.stack(v) for k, v in grads.items()}
    return loss, dcur, grads


WEIGHTS = ["mix_norm", "ffn_norm", "sb_w_qkv", "sb_q_gain", "sb_k_gain", "sb_w_o", "gm_w_in", "gm_b_in", "gm_v_gain",
           "gm_w_s", "gm_b_s", "gm_w_out", "ssm_w_in", "ssm_conv_w", "ssm_conv_b", "ssm_dt_bias", "ssm_a_log", "ssm_d",
           "ssm_norm_gain", "ssm_w_out", "ffn_w_gu", "ffn_w_down"]
SHARDED = {"sb_w_qkv": 2, "sb_w_o": 1, "gm_w_in": 2, "gm_w_out": 1, "ssm_w_in": 2, "ssm_conv_w": 2, "ssm_conv_b": 1,
           "ssm_norm_gain": 1, "ssm_w_out": 1, "ffn_w_gu": 2, "ffn_w_down": 1}
EXACT = ("ssm_conv_w", "ssm_conv_b", "ssm_norm_gain")
REPLICATED = [n for n in WEIGHTS if n not in SHARDED]
N_CHIPS = 4
N_DEV = 8
PACK_COLS = 1024
PACK_ROW_ALIGN = 32


def _pack(pieces, dtype, align):
    flat = jnp.concatenate([p.reshape(-1).astype(dtype) for p in pieces])
    rows = -(-flat.shape[0] // (PACK_COLS * align)) * align
    flat = jnp.pad(flat, (0, rows * PACK_COLS - flat.shape[0]))
    return flat.reshape(rows, PACK_COLS)


def _unpack(flat, shapes):
    out, off = [], 0
    for shp in shapes:
        n = math.prod(shp)
        out.append(flat[off:off + n].reshape(shp))
        off += n
    return out


def _to_words(v):
    return lax.bitcast_convert_type(v.astype(F32), BF16)


def _from_words(v):
    return lax.bitcast_convert_type(v, F32)


ANY = pl.BlockSpec(memory_space=pl.ANY)


def _pos():
    return lax.axis_index("x"), lax.axis_index("y"), lax.axis_index("c")


def _remote(src, dst, send, recv, k, to):
    return pltpu.make_async_remote_copy(src_ref=src, dst_ref=dst, send_sem=send.at[k], recv_sem=recv.at[k],
                                        device_id=to, device_id_type=MESH_ID)


def gather_weights(wp):
    rows, cols = wp.shape
    half = rows // 2

    def body(w_ref, o_ref, send, recv, lsem):
        x, y, c = _pos()
        me, sibling = (x, y, c), (x, y, 1 - c)
        chips = [(1 - x, y), (x, 1 - y), (1 - x, 1 - y)]

        def part(chip, cc):
            return o_ref.at[2 * chip[0] + chip[1], pl.ds(cc * half, half), :]

        mine = pltpu.make_async_copy(w_ref, o_ref.at[2 * x + y], lsem)
        mine.start()
        first = [_remote(w_ref.at[pl.ds(c * half, half), :], part((x, y), c), send, recv, j, (*chip, c))
                 for j, chip in enumerate(chips)]
        for cp in first:
            cp.start()
        passed = [_remote(part(chip, c), part(chip, c), send, recv, 3 + j, sibling) for j, chip in enumerate(chips)]
        for j, chip in enumerate(chips):
            _remote(part(chip, c), part(chip, c), send, recv, j, me).wait_recv()
            passed[j].start()
        for j, chip in enumerate(chips):
            _remote(part(chip, 1 - c), part(chip, 1 - c), send, recv, 3 + j, me).wait_recv()
        for cp in first + passed:
            cp.wait_send()
        mine.wait()

    return pl.pallas_call(
        body, name="gather_weights",
        out_shape=jax.ShapeDtypeStruct((N_CHIPS, rows, cols), wp.dtype),
        in_specs=[ANY], out_specs=ANY,
        scratch_shapes=[pltpu.SemaphoreType.DMA((6,)), pltpu.SemaphoreType.DMA((6,)), pltpu.SemaphoreType.DMA],
    )(wp)


def swap_halves(gp):
    nch, rows, cols = gp.shape
    half = rows // 2

    def body(g_ref, r_ref, send, recv):
        x, y, c = _pos()
        cp = _remote(g_ref.at[:, pl.ds((1 - c) * half, half), :], r_ref, send, recv, 0, (x, y, 1 - c))
        cp.start()
        cp.wait()

    return pl.pallas_call(
        body, name="swap_halves",
        out_shape=jax.ShapeDtypeStruct((nch, half, cols), gp.dtype),
        in_specs=[ANY], out_specs=ANY,
        scratch_shapes=[pltpu.SemaphoreType.DMA((1,)), pltpu.SemaphoreType.DMA((1,))],
    )(gp)


def scatter_chunks(p):
    nch, half, cols = p.shape

    def body(p_ref, r_ref, send, recv):
        x, y, c = _pos()
        chips = [(1 - x, y), (x, 1 - y), (1 - x, 1 - y)]
        cps = [_remote(p_ref.at[2 * chip[0] + chip[1]], r_ref.at[j], send, recv, j, (*chip, c))
               for j, chip in enumerate(chips)]
        for cp in cps:
            cp.start()
        for cp in cps:
            cp.wait()

    return pl.pallas_call(
        body, name="scatter_chunks",
        out_shape=jax.ShapeDtypeStruct((N_CHIPS - 1, half, cols), p.dtype),
        in_specs=[ANY], out_specs=ANY,
        scratch_shapes=[pltpu.SemaphoreType.DMA((3,)), pltpu.SemaphoreType.DMA((3,))],
    )(p)


def join_halves(hv):
    half, cols = hv.shape

    def body(h_ref, o_ref, send, recv, lsem):
        x, y, c = _pos()
        mine = pltpu.make_async_copy(h_ref, o_ref.at[pl.ds(c * half, half), :], lsem)
        mine.start()
        cp = _remote(h_ref, o_ref.at[pl.ds(c * half, half), :], send, recv, 0, (x, y, 1 - c))
        cp.start()
        _remote(h_ref, o_ref.at[pl.ds((1 - c) * half, half), :], send, recv, 0, (x, y, c)).wait_recv()
        cp.wait_send()
        mine.wait()

    return pl.pallas_call(
        body, name="join_halves",
        out_shape=jax.ShapeDtypeStruct((2 * half, cols), hv.dtype),
        in_specs=[ANY], out_specs=ANY,
        scratch_shapes=[pltpu.SemaphoreType.DMA((1,)), pltpu.SemaphoreType.DMA((1,)), pltpu.SemaphoreType.DMA],
    )(hv)


def gather_small(sg):
    rows, cols = sg.shape

    def body(s_ref, o_ref, send, recv, lsem):
        x, y, c = _pos()
        mine = pltpu.make_async_copy(s_ref, o_ref.at[4 * x + 2 * y + c], lsem)
        mine.start()
        peers = []
        for msk in range(1, N_DEV):
            px = 1 - x if msk & 4 else x
            py = 1 - y if msk & 2 else y
            pc = 1 - c if msk & 1 else c
            peers.append((px, py, pc))
        cps = [_remote(s_ref, o_ref.at[4 * x + 2 * y + c], send, recv, k, peer) for k, peer in enumerate(peers)]
        for cp in cps:
            cp.start()
        for k, (px, py, pc) in enumerate(peers):
            _remote(s_ref, o_ref.at[4 * px + 2 * py + pc], send, recv, k, (x, y, c)).wait_recv()
        for cp in cps:
            cp.wait_send()
        mine.wait()

    return pl.pallas_call(
        body, name="gather_small",
        out_shape=jax.ShapeDtypeStruct((N_DEV, rows, cols), sg.dtype),
        in_specs=[ANY], out_specs=ANY,
        scratch_shapes=[pltpu.SemaphoreType.DMA((N_DEV - 1,)), pltpu.SemaphoreType.DMA((N_DEV - 1,)), pltpu.SemaphoreType.DMA],
    )(sg)


def add_arrays(arrs, name):
    def fn(*vs):
        acc = vs[0]
        for v in vs[1:]:
            acc = acc + v
        return (acc,)

    return rowwise(fn, [(a, "row") for a in arrs], [(arrs[0].shape[1], F32)], tr=504, name=name)[0]


def small_update(gath, w, m, v, name):
    def fn(*vs):
        g = vs[0]
        for t in vs[1:N_DEV]:
            g = g + t
        wv, mv, vv = vs[N_DEV:]
        m2 = ADAM_B1 * mv + (1.0 - ADAM_B1) * g
        v2 = ADAM_B2 * vv + (1.0 - ADAM_B2) * (g * g)
        m_hat = m2 / (1.0 - ADAM_B1 ** ADAM_STEP)
        v_hat = v2 / (1.0 - ADAM_B2 ** ADAM_STEP)
        return g, -ADAM_LR * (m_hat / (jnp.sqrt(v_hat) + ADAM_EPS) + ADAM_WD * wv), m2, v2

    c = w.shape[1]
    ins = [(gath[k], "row") for k in range(N_DEV)] + [(w, "row"), (m, "row"), (v, "row")]
    return rowwise(fn, ins, [(c, F32)] * 4, tr=w.shape[0], name=name)


def _step(ins):
    x, target = ins["x"][0], ins["loss_target"][0]
    cc = lax.axis_index("c")
    chip = 2 * lax.axis_index("x") + lax.axis_index("y")
    sharded = list(SHARDED)

    pieces = [_to_words(ins[n]) if n in EXACT else ins[n] for n in sharded]
    piece_shapes = [p.shape for p in pieces]
    gathered = gather_weights(_pack(pieces, BF16, PACK_ROW_ALIGN))
    full = {}
    per_chip = [_unpack(gathered[k].reshape(-1), piece_shapes) for k in range(N_CHIPS)]
    for t, n in enumerate(sharded):
        parts = [per_chip[k][t] for k in range(N_CHIPS)]
        if n in EXACT:
            parts = [_from_words(p) for p in parts]
        full[n] = jnp.concatenate(parts, axis=SHARDED[n])
    for n in REPLICATED:
        full[n] = ins[n]

    loss, dx, grads = local_step(x, target, full)
    loss = lax.psum(loss, ALL_AXES)

    def chunks_of(g, axis):
        return jnp.split(g, N_CHIPS, axis=axis)

    split = {n: chunks_of(grads[n], SHARDED[n]) for n in sharded}
    gp = jnp.stack([_pack([split[n][k] for n in sharded], F32, PACK_ROW_ALIGN) for k in range(N_CHIPS)])
    rows = gp.shape[1]
    half = rows // 2
    theirs = swap_halves(gp)
    mine = lax.dynamic_slice_in_dim(gp, cc * half, half, axis=1)
    part = add_arrays([mine.reshape(N_CHIPS * half, PACK_COLS), theirs.reshape(N_CHIPS * half, PACK_COLS)],
                      "sum_cores").reshape(N_CHIPS, half, PACK_COLS)
    others = scatter_chunks(part)
    own = lax.dynamic_index_in_dim(part, chip, axis=0, keepdims=False)
    ghalf = add_arrays([own, others[0], others[1], others[2]], "sum_chips")
    gshard = join_halves(ghalf)
    shard_shapes = [ins[n].shape for n in sharded]
    gsh = dict(zip(sharded, _unpack(gshard.reshape(-1), shard_shapes)))

    small_shapes = [ins[n].shape for n in REPLICATED]
    gath = gather_small(_pack([grads[n] for n in REPLICATED], F32, SUBLANES))
    packed = [_pack([ins[pre + n] for n in REPLICATED], F32, SUBLANES) for pre in ("", "m_", "v_")]
    res = small_update(gath, *packed, name="small_update")
    small = [dict(zip(REPLICATED, _unpack(r.reshape(-1), small_shapes))) for r in res]

    out_g, out_d, out_m, out_v = {}, {}, {}, {}
    for n in REPLICATED:
        out_g[n], out_d[n], out_m[n], out_v[n] = (s[n] for s in small)
    for n in sharded:
        shp = ins[n].shape
        two = (math.prod(shp[:-1]), shp[-1])
        d2, m2, v2 = adamw(ins[n].reshape(two), gsh[n].reshape(two), ins["m_" + n].reshape(two),
                           ins["v_" + n].reshape(two), f"adamw_{n}")
        out_g[n], out_d[n], out_m[n], out_v[n] = gsh[n], d2.reshape(shp), m2.reshape(shp), v2.reshape(shp)
    return (loss, dx[None], *[out_g[n] for n in WEIGHTS], *[out_d[n] for n in WEIGHTS],
            *[out_m[n] for n in WEIGHTS], *[out_v[n] for n in WEIGHTS])


def kernel(x, mix_norm, ffn_norm, sb_w_qkv, sb_q_gain, sb_k_gain, sb_w_o, gm_w_in, gm_b_in, gm_v_gain, gm_w_s, gm_b_s, gm_w_out, ssm_w_in, ssm_conv_w, ssm_conv_b, ssm_dt_bias, ssm_a_log, ssm_d, ssm_norm_gain, ssm_w_out, ffn_w_gu, ffn_w_down, loss_target, m_mix_norm, m_ffn_norm, m_sb_w_qkv, m_sb_q_gain, m_sb_k_gain, m_sb_w_o, m_gm_w_in, m_gm_b_in, m_gm_v_gain, m_gm_w_s, m_gm_b_s, m_gm_w_out, m_ssm_w_in, m_ssm_conv_w, m_ssm_conv_b, m_ssm_dt_bias, m_ssm_a_log, m_ssm_d, m_ssm_norm_gain, m_ssm_w_out, m_ffn_w_gu, m_ffn_w_down, v_mix_norm, v_ffn_norm, v_sb_w_qkv, v_sb_q_gain, v_sb_k_gain, v_sb_w_o, v_gm_w_in, v_gm_b_in, v_gm_v_gain, v_gm_w_s, v_gm_b_s, v_gm_w_out, v_ssm_w_in, v_ssm_conv_w, v_ssm_conv_b, v_ssm_dt_bias, v_ssm_a_log, v_ssm_d, v_ssm_norm_gain, v_ssm_w_out, v_ffn_w_gu, v_ffn_w_down):
    return _step(dict(locals()))
```

```python
import functools
import math

import jax
import jax.numpy as jnp
from jax import lax
from jax.experimental import pallas as pl
from jax.experimental.pallas import tpu as pltpu

F32 = jnp.float32
BF16 = jnp.bfloat16
EPS = 1e-6
LANES = 128
SUBLANES = 8
VMEM_LIMIT = 56 * 1024 * 1024
HEAD = 64
CHUNK = 128
SB_TQ, SB_TK = 256, 256
ADAM_LR, ADAM_B1, ADAM_B2, ADAM_EPS, ADAM_WD, ADAM_STEP = 0.001, 0.9, 0.999, 1e-08, 0.01, 10
MESH_ID = pl.DeviceIdType.MESH
ALL_AXES = ("x", "y", "c")


def _params(sem):
    return pltpu.CompilerParams(dimension_semantics=sem, vmem_limit_bytes=VMEM_LIMIT)


def _pick(n, cands):
    for c in cands:
        if n % c == 0:
            return c
    return n


def _dot(a, b, dims=((1,), (0,))):
    return lax.dot_general(a, b, (dims, ((), ())), preferred_element_type=F32)


def _dot_nt(a, b):
    return _dot(a, b, ((1,), (1,)))


def _dot_tn(a, b):
    return _dot(a, b, ((0,), (0,)))


def _split2(x):
    hi = x.astype(BF16)
    lo = (x - hi.astype(F32)).astype(BF16)
    return hi, lo


def _dot_x2(x, m):
    hi, lo = _split2(x)
    return _dot(hi, m) + _dot(lo, m)


def _dot_x3_left(m, x):
    h1 = x.astype(BF16)
    r1 = x - h1.astype(F32)
    h2 = r1.astype(BF16)
    h3 = (r1 - h2.astype(F32)).astype(BF16)
    return _dot(m, h1) + _dot(m, h2) + _dot(m, h3)


def _sigmoid(x):
    return 1.0 / (1.0 + jnp.exp(-x))


def _softplus(x):
    return jnp.maximum(x, 0.0) + jnp.log(1.0 + jnp.exp(-jnp.abs(x)))


def _colsum(x):
    return jnp.sum(x, axis=0, keepdims=True)


def _rowsum(x):
    return jnp.sum(x, axis=1, keepdims=True)


def _iota2(shape, dim):
    return lax.broadcasted_iota(jnp.int32, shape, dim)


def mm(a, b, *, ta=False, tb=False, add=None, bias=None, name):
    if ta:
        kk, m = a.shape
    else:
        m, kk = a.shape
    if tb:
        n, kb = b.shape
    else:
        kb, n = b.shape
    assert kk == kb, (a.shape, b.shape, ta, tb)
    tm = _pick(m, (512, 256, 128))
    tn = _pick(n, (512, 256, 128))
    tk = _pick(kk, (1024, 1408, 512, 256, 128))
    nk = kk // tk
    dims = ((0 if ta else 1,), (1 if tb else 0,))
    has_add, has_bias = add is not None, bias is not None

    def kern(*refs):
        a_ref, b_ref = refs[0], refs[1]
        rest = list(refs[2:])
        add_ref = rest.pop(0) if has_add else None
        bias_ref = rest.pop(0) if has_bias else None
        o_ref, acc_ref = rest
        k = pl.program_id(2)

        @pl.when(k == 0)
        def _():
            acc_ref[...] = jnp.zeros_like(acc_ref)

        acc_ref[...] += _dot(a_ref[...].astype(BF16), b_ref[...].astype(BF16), dims)

        @pl.when(k == nk - 1)
        def _():
            r = acc_ref[...]
            if has_add:
                r = r + add_ref[...]
            if has_bias:
                r = r + bias_ref[...]
            o_ref[...] = r

    a_spec = pl.BlockSpec((tk, tm), lambda i, j, k: (k, i)) if ta else pl.BlockSpec((tm, tk), lambda i, j, k: (i, k))
    b_spec = pl.BlockSpec((tn, tk), lambda i, j, k: (j, k)) if tb else pl.BlockSpec((tk, tn), lambda i, j, k: (k, j))
    in_specs, args = [a_spec, b_spec], [a, b]
    if has_add:
        in_specs.append(pl.BlockSpec((tm, tn), lambda i, j, k: (i, j)))
        args.append(add)
    if has_bias:
        in_specs.append(pl.BlockSpec((1, tn), lambda i, j, k: (0, j)))
        args.append(bias)
    return pl.pallas_call(
        kern,
        name=name,
        grid=(m // tm, n // tn, nk),
        in_specs=in_specs,
        out_specs=pl.BlockSpec((tm, tn), lambda i, j, k: (i, j)),
        out_shape=jax.ShapeDtypeStruct((m, n), F32),
        scratch_shapes=[pltpu.VMEM((tm, tn), F32)],
        compiler_params=_params(("parallel", "parallel", "arbitrary")),
    )(*args)


def rowwise(fn, ins, outs, accs=(), *, tr, name):
    rows = [a for a, kind in ins if kind == "row"][0].shape[0]
    tr = min(tr, rows)
    assert rows % tr == 0 and tr % SUBLANES == 0, (rows, tr)
    n = rows // tr
    n_in, n_out = len(ins), len(outs)
    kinds = [kind for _, kind in ins]

    def kern(*refs):
        i = pl.program_id(0)
        vals = []
        for ref, kind in zip(refs[:n_in], kinds):
            v = ref[...]
            if kind == "prev":
                v = v * (i > 0).astype(v.dtype)
            elif kind == "next":
                v = v * (i < n - 1).astype(v.dtype)
            vals.append(v)
        res = fn(*vals)
        for ref, r in zip(refs[n_in:n_in + n_out], res[:n_out]):
            ref[...] = r.astype(ref.dtype)
        if accs:
            acc_refs = refs[n_in + n_out:]

            @pl.when(i == 0)
            def _():
                for ref in acc_refs:
                    ref[...] = jnp.zeros_like(ref)

            for ref, r in zip(acc_refs, res[n_out:]):
                ref[...] += r

    in_specs = []
    for a, kind in ins:
        if kind == "row":
            in_specs.append(pl.BlockSpec((tr, a.shape[1]), lambda i: (i, 0)))
        elif kind == "full":
            in_specs.append(pl.BlockSpec(a.shape, lambda i, nd=a.ndim: (0,) * nd))
        elif kind == "prev":
            in_specs.append(pl.BlockSpec((SUBLANES, a.shape[1]),
                                         lambda i: (jnp.maximum(i * (tr // SUBLANES) - 1, 0), 0)))
        else:
            in_specs.append(pl.BlockSpec((SUBLANES, a.shape[1]),
                                         lambda i: (jnp.minimum((i + 1) * (tr // SUBLANES), rows // SUBLANES - 1), 0)))
    out_specs = [pl.BlockSpec((tr, c), lambda i: (i, 0)) for c, _ in outs]
    out_specs += [pl.BlockSpec((r, c), lambda i: (0, 0)) for r, c in accs]
    out_shape = [jax.ShapeDtypeStruct((rows, c), dt) for c, dt in outs]
    out_shape += [jax.ShapeDtypeStruct((r, c), F32) for r, c in accs]
    res = pl.pallas_call(
        kern,
        name=name,
        grid=(n,),
        in_specs=in_specs,
        out_specs=out_specs,
        out_shape=out_shape,
        compiler_params=_params(("arbitrary",) if accs else ("parallel",)),
    )(*[a for a, _ in ins])
    return res


def rms_fwd(x, g, name):
    def fn(xv, gv):
        r = lax.rsqrt(jnp.mean(xv * xv, axis=1, keepdims=True) + EPS)
        return (xv * r * gv,)

    return rowwise(fn, [(x, "row"), (g, "full")], [(x.shape[1], BF16)], tr=512, name=name)[0]


def rms_bwd(x, g, dy, dres, name):
    def fn(xv, gv, dyv, drv):
        r = lax.rsqrt(jnp.mean(xv * xv, axis=1, keepdims=True) + EPS)
        xh = xv * r
        dyg = dyv * gv
        dx = drv + r * (dyg - xh * jnp.mean(dyg * xh, axis=1, keepdims=True))
        return dx, _colsum(dyv * xh)

    c = x.shape[1]
    return rowwise(fn, [(x, "row"), (g, "full"), (dy, "row"), (dres, "row")], [(c, F32)], [(1, c)], tr=256, name=name)


def swiglu_fwd(gu, name):
    hid = gu.shape[1] // 2

    def fn(v):
        g, u = v[:, :hid], v[:, hid:]
        return (g * _sigmoid(g) * u,)

    return rowwise(fn, [(gu, "row")], [(hid, BF16)], tr=256, name=name)[0]


def swiglu_bwd(gu, da, name):
    hid = gu.shape[1] // 2

    def fn(v, d):
        g, u = v[:, :hid], v[:, hid:]
        s = _sigmoid(g)
        dg = d * u * s * (1.0 + g * (1.0 - s))
        du = d * g * s
        return (jnp.concatenate([dg, du], axis=1),)

    return rowwise(fn, [(gu, "row"), (da, "row")], [(2 * hid, BF16)], tr=256, name=name)[0]


def loss_and_grad(y, t, name):
    d = y.shape[1]

    def fn(yv, tv):
        e = yv - tv
        part = jnp.sum(_colsum(e * e), axis=1, keepdims=True) * (0.5 / d)
        return e * (1.0 / d), jnp.broadcast_to(part, (SUBLANES, LANES))

    dy, acc = rowwise(fn, [(y, "row"), (t, "row")], [(d, F32)], [(SUBLANES, LANES)], tr=512, name=name)
    return acc[0, 0], dy


def adamw(w, g, m, v, name):
    def fn(wv, gv, mv, vv):
        m2 = ADAM_B1 * mv + (1.0 - ADAM_B1) * gv
        v2 = ADAM_B2 * vv + (1.0 - ADAM_B2) * (gv * gv)
        m_hat = m2 / (1.0 - ADAM_B1 ** ADAM_STEP)
        v_hat = v2 / (1.0 - ADAM_B2 ** ADAM_STEP)
        delta = -ADAM_LR * (m_hat / (jnp.sqrt(v_hat) + ADAM_EPS) + ADAM_WD * wv)
        return delta, m2, v2

    rows, c = w.shape
    tr = _pick(rows, (256, 128, 64, 32, 16, 8)) if rows % SUBLANES == 0 else rows
    if rows % SUBLANES:
        return _whole(fn, [w, g, m, v], [(w.shape, F32)] * 3, name=name)
    return rowwise(fn, [(w, "row"), (g, "row"), (m, "row"), (v, "row")], [(c, F32)] * 3, tr=tr, name=name)


def _whole(fn, ins, outs, *, name):
    n_in = len(ins)

    def kern(*refs):
        res = fn(*[r[...] for r in refs[:n_in]])
        for ref, r in zip(refs[n_in:], res):
            ref[...] = r.astype(ref.dtype)

    return pl.pallas_call(
        kern,
        name=name,
        out_shape=[jax.ShapeDtypeStruct(s, dt) for s, dt in outs],
        compiler_params=pltpu.CompilerParams(vmem_limit_bytes=VMEM_LIMIT),
    )(*ins)


def ffn_fwd(x, g, wgu, wdown, tag):
    h = rms_fwd(x, g, f"ffn_rms_{tag}")
    gu = mm(h, wgu, name=f"ffn_gu_{tag}")
    a = swiglu_fwd(gu, f"ffn_act_{tag}")
    xn = mm(a, wdown, add=x, name=f"ffn_down_{tag}")
    return xn, (x, h, gu, a)


def ffn_bwd(dxn, saved, g, wgu, wdown, tag):
    x, h, gu, a = saved
    da = mm(dxn, wdown, tb=True, name=f"ffn_da_{tag}")
    dwdown = mm(a, dxn, ta=True, name=f"ffn_dwdown_{tag}")
    dgu = swiglu_bwd(gu, da, f"ffn_dact_{tag}")
    dh = mm(dgu, wgu, tb=True, name=f"ffn_dh_{tag}")
    dwgu = mm(h, dgu, ta=True, name=f"ffn_dwgu_{tag}")
    dx, dg = rms_bwd(x, g, dh, dxn, f"ffn_drms_{tag}")
    return dx, dg, dwgu, dwdown


def _head_blockdiag(c):
    i = jnp.arange(c) // HEAD
    return (i[:, None] == i[None, :]).astype(BF16)


def qknorm_fwd(qkv, qg, kg, bd, name):
    d = qkv.shape[1] // 3
    scale = 1.0 / math.sqrt(HEAD)

    def fn(v, qgv, kgv, bdv):
        q, k, vv = v[:, :d], v[:, d:2 * d], v[:, 2 * d:]
        rq = lax.rsqrt(_dot_x2(q * q, bdv) * (1.0 / HEAD) + EPS)
        rk = lax.rsqrt(_dot_x2(k * k, bdv) * (1.0 / HEAD) + EPS)
        return q * rq * qgv * scale, k * rk * kgv, vv

    return rowwise(fn, [(qkv, "row"), (qg, "full"), (kg, "full"), (bd, "full")],
                   [(d, BF16), (d, BF16), (d, BF16)], tr=256, name=name)


def qknorm_bwd(qkv, dqs, dkn, dv, qg, kg, bd, name):
    d = qkv.shape[1] // 3
    scale = 1.0 / math.sqrt(HEAD)

    def one(xv, gv, dyv, bdv):
        r = lax.rsqrt(_dot_x2(xv * xv, bdv) * (1.0 / HEAD) + EPS)
        xh = xv * r
        dyg = dyv * gv
        dx = r * (dyg - xh * (_dot_x2(dyg * xh, bdv) * (1.0 / HEAD)))
        return dx, _colsum(dyv * xh)

    def fn(v, dqv, dkv, dvv, qgv, kgv, bdv):
        q, k = v[:, :d], v[:, d:2 * d]
        dq, dqg = one(q, qgv, dqv * scale, bdv)
        dk, dkg = one(k, kgv, dkv, bdv)
        return jnp.concatenate([dq, dk, dvv], axis=1), dqg, dkg

    return rowwise(fn, [(qkv, "row"), (dqs, "row"), (dkn, "row"), (dv, "row"), (qg, "full"), (kg, "full"), (bd, "full")],
                   [(3 * d, BF16)], [(1, d), (1, d)], tr=256, name=name)


def _sb_tile(qh, k, mask, tri_gt):
    z = _dot_nt(qh, k)
    sp = jnp.log(1.0 + jnp.exp(-jnp.abs(z)))
    lb = jnp.minimum(z, 0.0) - sp
    l1 = jnp.where(mask, lb - z, 0.0)
    suf = _dot_x2(l1, tri_gt)
    return lb, l1, suf


def _sb_setup(tq, tk):
    row, col = _iota2((tq, tk), 0), _iota2((tq, tk), 1)
    lane = _iota2((1, LANES), 1)
    halves = [(lane < HEAD).astype(BF16), (lane >= HEAD).astype(BF16)]
    lane_q = _iota2((tq, LANES), 1) + jnp.minimum(_iota2((tq, LANES), 0), 0)
    return row, col, halves, lane_q


def sb_attn_fwd(qs, kn, vb, name):
    s, d = qs.shape
    tq, tk = min(SB_TQ, s), min(SB_TK, s)
    nq = s // tq
    assert s // tk <= LANES and s % tq == 0 and s % tk == 0

    def kern(q_ref, k_ref, v_ref, o_ref, rs_ref):
        i = pl.program_id(1)
        row, col, halves, lane_q = _sb_setup(tq, tk)
        tri_gt = (_iota2((tk, tk), 0) > _iota2((tk, tk), 1)).astype(BF16)
        q = q_ref[...]
        qh = [q * hm for hm in halves]
        o_ref[...] = jnp.zeros_like(o_ref)
        rs_ref[...] = jnp.zeros_like(rs_ref)
        nkb = (i + 1) * (tq // tk)

        def step(n, r):
            r = list(r)
            kb = nkb - 1 - n
            ks = pl.multiple_of(kb * tk, tk)
            k = k_ref[pl.ds(ks, tk), :]
            v = v_ref[pl.ds(ks, tk), :]
            mask = col < row + (i * tq - kb * tk)
            at_kb = lane_q == kb
            for hh in range(2):
                lb, l1, suf = _sb_tile(qh[hh], k, mask, tri_gt)
                w = jnp.where(mask, jnp.exp(lb + suf + r[hh]), 0.0)
                o_ref[...] += _dot(w.astype(BF16), v * halves[hh])
                rs_ref[hh] = jnp.where(at_kb, r[hh], rs_ref[hh])
                r[hh] = r[hh] + _rowsum(l1)
            return tuple(r)

        z1 = jnp.zeros((tq, 1), F32)
        lax.fori_loop(0, nkb, step, (z1, z1))

    nh2 = d // LANES
    return pl.pallas_call(
        kern,
        name=name,
        grid=(nh2, nq),
        in_specs=[pl.BlockSpec((tq, LANES), lambda h, i: (i, h)),
                  pl.BlockSpec((s, LANES), lambda h, i: (0, h)),
                  pl.BlockSpec((s, LANES), lambda h, i: (0, h))],
        out_specs=[pl.BlockSpec((tq, LANES), lambda h, i: (i, h)),
                   pl.BlockSpec((None, 2, tq, LANES), lambda h, i: (h, 0, i, 0))],
        out_shape=[jax.ShapeDtypeStruct((s, d), F32), jax.ShapeDtypeStruct((nh2, 2, s, LANES), F32)],
        compiler_params=_params(("parallel", "arbitrary")),
    )(qs, kn, vb)


def sb_attn_bwd(qs, kn, vb, rsave, do, name):
    s, d = qs.shape
    tq, tk = min(SB_TQ, s), min(SB_TK, s)
    nq = s // tq

    def kern(q_ref, k_ref, v_ref, rs_ref, do_ref, dq_ref, dk_ref, dv_ref):
        i = pl.program_id(1)

        @pl.when(i == 0)
        def _():
            dk_ref[...] = jnp.zeros_like(dk_ref)
            dv_ref[...] = jnp.zeros_like(dv_ref)

        row, col, halves, lane_q = _sb_setup(tq, tk)
        tri_gt = (_iota2((tk, tk), 0) > _iota2((tk, tk), 1)).astype(BF16)
        tri_lt = (_iota2((tk, tk), 0) < _iota2((tk, tk), 1)).astype(BF16)
        q = q_ref[...]
        qh = [q * hm for hm in halves]
        dov = do_ref[...].astype(BF16)
        doh = [dov * hm for hm in halves]
        dq_ref[...] = jnp.zeros_like(dq_ref)
        nkb = (i + 1) * (tq // tk)

        def step(kb, ep):
            ep = list(ep)
            ks = pl.multiple_of(kb * tk, tk)
            k = k_ref[pl.ds(ks, tk), :]
            v = v_ref[pl.ds(ks, tk), :]
            mask = col < row + (i * tq - kb * tk)
            at_kb = lane_q == kb
            for hh in range(2):
                lb, l1, suf = _sb_tile(qh[hh], k, mask, tri_gt)
                r = _rowsum(jnp.where(at_kb, rs_ref[hh], 0.0))
                w = jnp.where(mask, jnp.exp(lb + suf + r), 0.0)
                e = _dot_nt(doh[hh], v) * w
                pe = ep[hh] + _dot_x2(e, tri_lt)
                beta = jnp.exp(lb)
                dz = jnp.where(mask, e * (1.0 - beta) - pe * beta, 0.0).astype(BF16)
                dq_ref[...] += _dot(dz, k * halves[hh])
                dk_ref[pl.ds(ks, tk), :] += _dot_tn(dz, qh[hh])
                dv_ref[pl.ds(ks, tk), :] += _dot_tn(w.astype(BF16), doh[hh])
                ep[hh] = ep[hh] + _rowsum(e)
            return tuple(ep)

        z1 = jnp.zeros((tq, 1), F32)
        lax.fori_loop(0, nkb, step, (z1, z1))

    nh2 = d // LANES
    return pl.pallas_call(
        kern,
        name=name,
        grid=(nh2, nq),
        in_specs=[pl.BlockSpec((tq, LANES), lambda h, i: (i, h)),
                  pl.BlockSpec((s, LANES), lambda h, i: (0, h)),
                  pl.BlockSpec((s, LANES), lambda h, i: (0, h)),
                  pl.BlockSpec((None, 2, tq, LANES), lambda h, i: (h, 0, i, 0)),
                  pl.BlockSpec((tq, LANES), lambda h, i: (i, h))],
        out_specs=[pl.BlockSpec((tq, LANES), lambda h, i: (i, h)),
                   pl.BlockSpec((s, LANES), lambda h, i: (0, h)),
                   pl.BlockSpec((s, LANES), lambda h, i: (0, h))],
        out_shape=[jax.ShapeDtypeStruct((s, d), F32)] * 3,
        compiler_params=_params(("parallel", "arbitrary")),
    )(qs, kn, vb, rsave, do)


def sb_fwd(x, g, wqkv, qg, kg, wo, bd, tag):
    h = rms_fwd(x, g, f"sb_rms_{tag}")
    qkv = mm(h, wqkv, name=f"sb_qkv_{tag}")
    qs, kn, vb = qknorm_fwd(qkv, qg, kg, bd, f"sb_qknorm_{tag}")
    o, rsave = sb_attn_fwd(qs, kn, vb, f"sb_attn_{tag}")
    xn = mm(o, wo, add=x, name=f"sb_out_{tag}")
    return xn, (x, h, qkv, qs, kn, vb, rsave, o)


def sb_bwd(dxn, saved, g, wqkv, qg, kg, wo, bd, tag):
    x, h, qkv, qs, kn, vb, rsave, o = saved
    do = mm(dxn, wo, tb=True, name=f"sb_do_{tag}")
    dwo = mm(o, dxn, ta=True, name=f"sb_dwo_{tag}")
    dqs, dkn, dv = sb_attn_bwd(qs, kn, vb, rsave, do, f"sb_dattn_{tag}")
    dqkv, dqg, dkg = qknorm_bwd(qkv, dqs, dkn, dv, qg, kg, bd, f"sb_dqknorm_{tag}")
    dh = mm(dqkv, wqkv, tb=True, name=f"sb_dh_{tag}")
    dwqkv = mm(h, dqkv, ta=True, name=f"sb_dwqkv_{tag}")
    dx, dg = rms_bwd(x, g, dh, dxn, f"sb_drms_{tag}")
    nh = dqg.shape[1] // HEAD
    return dx, dg, dwqkv, dqg.reshape(nh, HEAD).sum(0), dkg.reshape(nh, HEAD).sum(0), dwo


def _gelu(x):
    return 0.5 * x * (1.0 + lax.erf(x * (1.0 / math.sqrt(2.0))))


def _gelu_grad(x):
    return 0.5 * (1.0 + lax.erf(x * (1.0 / math.sqrt(2.0)))) + x * jnp.exp(-0.5 * x * x) * (1.0 / math.sqrt(2.0 * math.pi))


def gm_act_fwd(pre, vg, name):
    half = pre.shape[1] // 2

    def fn(p, vgv):
        u = _gelu(p[:, :half])
        v = _gelu(p[:, half:])
        r = lax.rsqrt(jnp.mean(v * v, axis=1, keepdims=True) + EPS)
        return u, v * r * vgv

    return rowwise(fn, [(pre, "row"), (vg, "full")], [(half, F32), (half, BF16)], tr=256, name=name)


def gm_act_bwd(pre, du, dvn, vg, name):
    half = pre.shape[1] // 2

    def fn(p, duv, dvnv, vgv):
        pu, pv = p[:, :half], p[:, half:]
        v = _gelu(pv)
        r = lax.rsqrt(jnp.mean(v * v, axis=1, keepdims=True) + EPS)
        vh = v * r
        dyg = dvnv * vgv
        dv = r * (dyg - vh * jnp.mean(dyg * vh, axis=1, keepdims=True))
        dpre = jnp.concatenate([duv * _gelu_grad(pu), dv * _gelu_grad(pv)], axis=1)
        return dpre, _colsum(dvnv * vh), _colsum(dpre)

    return rowwise(fn, [(pre, "row"), (du, "row"), (dvn, "row"), (vg, "full")],
                   [(2 * half, BF16)], [(1, half), (1, 2 * half)], tr=256, name=name)


def gm_spatial_fwd(u, vn, wc, bst, name):
    s, c = u.shape
    t = CHUNK
    ng = c // LANES

    def kern(u_ref, v_ref, w_ref, b_ref, o_ref):
        for g in range(ng):
            sl = slice(g * LANES, (g + 1) * LANES)
            mixed = _dot(w_ref[g], v_ref[:, sl]) + b_ref[:, sl]
            o_ref[:, sl] = (u_ref[:, sl] * mixed).astype(BF16)

    return pl.pallas_call(
        kern,
        name=name,
        grid=(s // t,),
        in_specs=[pl.BlockSpec((t, c), lambda i: (i, 0)), pl.BlockSpec((t, c), lambda i: (i, 0)),
                  pl.BlockSpec(wc.shape, lambda i: (0, 0, 0)), pl.BlockSpec(bst.shape, lambda i: (0, 0))],
        out_specs=pl.BlockSpec((t, c), lambda i: (i, 0)),
        out_shape=jax.ShapeDtypeStruct((s, c), BF16),
        compiler_params=_params(("parallel",)),
    )(u, vn, wc, bst)


def gm_spatial_bwd(dgate, u, vn, wc, bst, name):
    s, c = u.shape
    t = CHUNK
    ng = c // LANES

    def kern(dg_ref, u_ref, v_ref, w_ref, b_ref, du_ref, dv_ref, dw_ref, db_ref):
        i = pl.program_id(0)

        @pl.when(i == 0)
        def _():
            dw_ref[...] = jnp.zeros_like(dw_ref)
            db_ref[...] = jnp.zeros_like(db_ref)

        for g in range(ng):
            sl = slice(g * LANES, (g + 1) * LANES)
            vg = v_ref[:, sl]
            dgv = dg_ref[:, sl]
            mixed = _dot(w_ref[g], vg) + b_ref[:, sl]
            du_ref[:, sl] = dgv * mixed
            dmix = dgv * u_ref[:, sl]
            dmb = dmix.astype(BF16)
            dv_ref[:, sl] = _dot_tn(w_ref[g], dmb)
            dw_ref[g] += _dot_nt(dmb, vg)
            db_ref[:, sl] += dmix

    return pl.pallas_call(
        kern,
        name=name,
        grid=(s // t,),
        in_specs=[pl.BlockSpec((t, c), lambda i: (i, 0))] * 3 +
                 [pl.BlockSpec(wc.shape, lambda i: (0, 0, 0)), pl.BlockSpec(bst.shape, lambda i: (0, 0))],
        out_specs=[pl.BlockSpec((t, c), lambda i: (i, 0)), pl.BlockSpec((t, c), lambda i: (i, 0)),
                   pl.BlockSpec(wc.shape, lambda i: (0, 0, 0)), pl.BlockSpec(bst.shape, lambda i: (0, 0))],
        out_shape=[jax.ShapeDtypeStruct((s, c), F32), jax.ShapeDtypeStruct((s, c), F32),
                   jax.ShapeDtypeStruct(wc.shape, F32), jax.ShapeDtypeStruct(bst.shape, F32)],
        compiler_params=_params(("arbitrary",)),
    )(dgate, u, vn, wc, bst)


def gm_fwd(x, g, w_in, b_in, vg, wc, bst, w_out, tag):
    h = rms_fwd(x, g, f"gm_rms_{tag}")
    pre = mm(h, w_in, bias=b_in, name=f"gm_in_{tag}")
    u, vn = gm_act_fwd(pre, vg, f"gm_act_{tag}")
    gate = gm_spatial_fwd(u, vn, wc, bst, f"gm_spatial_{tag}")
    xn = mm(gate, w_out, add=x, name=f"gm_out_{tag}")
    return xn, (x, h, pre, u, vn, gate)


def gm_bwd(dxn, saved, g, w_in, vg, wc, bst, w_out, tag):
    x, h, pre, u, vn, gate = saved
    dgate = mm(dxn, w_out, tb=True, name=f"gm_dgate_{tag}")
    dwout = mm(gate, dxn, ta=True, name=f"gm_dwout_{tag}")
    du, dvn, dws, dbst = gm_spatial_bwd(dgate, u, vn, wc, bst, f"gm_dspatial_{tag}")
    dpre, dvg, dbin = gm_act_bwd(pre, du, dvn, vg, f"gm_dact_{tag}")
    dh = mm(dpre, w_in, tb=True, name=f"gm_dh_{tag}")
    dwin = mm(h, dpre, ta=True, name=f"gm_dwin_{tag}")
    dx, dg = rms_bwd(x, g, dh, dxn, f"gm_drms_{tag}")
    ng = wc.shape[0]
    dws = jnp.where(jnp.tril(jnp.ones((CHUNK, CHUNK), bool)), dws, 0.0)
    dbs = dbst.reshape(CHUNK, ng, LANES).sum(-1).T
    return dx, dg, dwin, dbin, dvg, dws, dbs, dwout


def _conv_taps(xv, prev):
    cat = jnp.concatenate([prev, xv], axis=0)
    return [pltpu.roll(cat, sh, 0)[SUBLANES:] for sh in (3, 2, 1)] + [xv]


def conv_fwd(xbc, ws, b, d_inner, name):
    c = xbc.shape[1]
    nst = (c - d_inner) // 2

    def fn(xv, prev, w0, w1, w2, w3, bv):
        taps = _conv_taps(xv, prev)
        pre = bv + w0 * taps[0] + w1 * taps[1] + w2 * taps[2] + w3 * taps[3]
        out = pre * _sigmoid(pre)
        return out[:, :d_inner], out[:, d_inner:d_inner + nst], out[:, d_inner + nst:]

    return rowwise(fn, [(xbc, "row"), (xbc, "prev")] + [(w, "full") for w in ws] + [(b, "full")],
                   [(d_inner, F32), (nst, F32), (nst, F32)], tr=256, name=name)


def conv_bwd_pre(xbc, ws, b, dxs_a, dxs_b, db_m, dc_m, name):
    c = xbc.shape[1]

    def fn(xv, prev, w0, w1, w2, w3, bv, da, db2, dbm, dcm):
        taps = _conv_taps(xv, prev)
        pre = bv + w0 * taps[0] + w1 * taps[1] + w2 * taps[2] + w3 * taps[3]
        sg = _sigmoid(pre)
        dout = jnp.concatenate([da + db2, dbm, dcm], axis=1)
        dpre = dout * sg * (1.0 + pre * (1.0 - sg))
        return (dpre,) + tuple(_colsum(dpre * tp) for tp in taps) + (_colsum(dpre),)

    return rowwise(fn, [(xbc, "row"), (xbc, "prev")] + [(w, "full") for w in ws] +
                   [(b, "full"), (dxs_a, "row"), (dxs_b, "row"), (db_m, "row"), (dc_m, "row")],
                   [(c, F32)], [(1, c)] * 5, tr=256, name=name)


def conv_bwd_in(dpre, ws, name):
    c = dpre.shape[1]

    def fn(dv, nxt, w0, w1, w2, w3):
        cat = jnp.concatenate([dv, nxt], axis=0)
        n = cat.shape[0]
        up = [pltpu.roll(cat, n - sh, 0)[:dv.shape[0]] for sh in (1, 2, 3)]
        return (w3 * dv + w2 * up[0] + w1 * up[1] + w0 * up[2],)

    return rowwise(fn, [(dpre, "row"), (dpre, "next")] + [(w, "full") for w in ws], [(c, BF16)], tr=256, name=name)[0]


def ssd_pre(dtr, bias, alog, name):
    def fn(d, bv, al, tri):
        dt = _softplus(d + bv)
        a = dt * (-jnp.exp(al))
        return dt, _dot_x3_left(tri, a)

    tri = jnp.tril(jnp.ones((CHUNK, CHUNK), BF16))
    return rowwise(fn, [(dtr, "row"), (bias, "full"), (alog, "full"), (tri, "full")],
                   [(LANES, F32), (LANES, F32)], tr=CHUNK, name=name)


def _ssd_layouts(v, ngroups, hpg):
    s = v.shape[0]
    col = v[:, :ngroups * hpg].T.reshape(ngroups, hpg, s, 1)
    return jnp.broadcast_to(col, (ngroups, hpg, s, LANES))


def _ssd_rowform(acum, ngroups, hpg):
    s = acum.shape[0]
    nc = s // CHUNK
    a = acum[:, :ngroups * hpg].reshape(nc, CHUNK, ngroups, hpg).transpose(2, 0, 3, 1)
    last = jnp.broadcast_to(a[..., CHUNK - 1:], a.shape)
    return jnp.concatenate([a, last], axis=2)


def ssd_chunk_fwd(xs, bm, cm, col_a, col_dt, rowf, name):
    s, d_inner = xs.shape
    ln = CHUNK
    nc = s // ln
    ng, hpg = col_a.shape[0], col_a.shape[1]
    gw = d_inner // ng
    assert gw == hpg * HEAD and gw % LANES == 0 and bm.shape[1] == ng * LANES

    def kern(x_ref, b_ref, c_ref, ca_ref, cd_ref, rf_ref, y_ref, hp_ref, h_scr):
        c = pl.program_id(1)

        @pl.when(c == 0)
        def _():
            h_scr[...] = jnp.zeros_like(h_scr)

        bb = b_ref[...].astype(BF16)
        cbf = c_ref[...].astype(BF16)
        cb = _dot_nt(cbf, bb)
        causal = _iota2((ln, ln), 0) >= _iota2((ln, ln), 1)
        lane = _iota2((1, LANES), 1)
        ys = [jnp.zeros((ln, LANES), F32) for _ in range(gw // LANES)]
        for r in range(hpg):
            j, hf = divmod(r, LANES // HEAD)
            mh = ((lane >= HEAD * hf) & (lane < HEAD * (hf + 1))).astype(F32)
            ac = ca_ref[r]
            ar = rf_ref[pl.ds(r, 1), :]
            aend = rf_ref[pl.ds(4 + r, 1), :]
            dm = jnp.exp(jnp.minimum(ac - ar, 0.0))
            m = jnp.where(causal, cb * dm, 0.0).astype(BF16)
            xdt = x_ref[:, j * LANES:(j + 1) * LANES] * cd_ref[r] * mh
            h = h_scr[r]
            hp_ref[r] = h
            ys[j] = ys[j] + _dot(m, xdt.astype(BF16)) + _dot_nt(cbf, h.astype(BF16)) * jnp.exp(ac)
            dte = jnp.exp(aend - ac)
            h_scr[r] = jnp.exp(aend) * h + _dot_tn((xdt * dte).astype(BF16), bb)
        for j in range(gw // LANES):
            y_ref[:, j * LANES:(j + 1) * LANES] = ys[j]

    return pl.pallas_call(
        kern,
        name=name,
        grid=(ng, nc),
        in_specs=[pl.BlockSpec((ln, gw), lambda g, c: (c, g)),
                  pl.BlockSpec((ln, LANES), lambda g, c: (c, g)),
                  pl.BlockSpec((ln, LANES), lambda g, c: (c, g)),
                  pl.BlockSpec((None, hpg, ln, LANES), lambda g, c: (g, 0, c, 0)),
                  pl.BlockSpec((None, hpg, ln, LANES), lambda g, c: (g, 0, c, 0)),
                  pl.BlockSpec((None, None, 8, LANES), lambda g, c: (g, c, 0, 0))],
        out_specs=[pl.BlockSpec((ln, gw), lambda g, c: (c, g)),
                   pl.BlockSpec((None, None, hpg, LANES, LANES), lambda g, c: (g, c, 0, 0, 0))],
        out_shape=[jax.ShapeDtypeStruct((s, d_inner), F32),
                   jax.ShapeDtypeStruct((ng, nc, hpg, LANES, LANES), F32)],
        scratch_shapes=[pltpu.VMEM((hpg, LANES, LANES), F32)],
        compiler_params=_params(("parallel", "arbitrary")),
    )(xs, bm, cm, col_a, col_dt, rowf)


def ssd_chunk_bwd(xs, bm, cm, col_a, col_dt, rowf, hprev, dy, name):
    s, d_inner = xs.shape
    ln = CHUNK
    nc = s // ln
    ng, hpg = col_a.shape[0], col_a.shape[1]
    gw = d_inner // ng

    def kern(x_ref, b_ref, c_ref, ca_ref, cd_ref, rf_ref, hp_ref, dy_ref,
             dx_ref, db_ref, dc_ref, ddt_ref, da_ref, dh_scr):
        c = pl.program_id(1)

        @pl.when(c == 0)
        def _():
            dh_scr[...] = jnp.zeros_like(dh_scr)

        bb = b_ref[...].astype(BF16)
        cbf = c_ref[...].astype(BF16)
        cb = _dot_nt(cbf, bb)
        row, col = _iota2((ln, ln), 0), _iota2((ln, ln), 1)
        causal = row >= col
        tri_ge = (col >= row).astype(BF16)
        ones = jnp.ones((ln, LANES), BF16)
        lane = _iota2((1, LANES), 1)
        last_row = (_iota2((ln, 1), 0) == ln - 1).astype(F32)
        dcb = jnp.zeros((ln, ln), F32)
        d_b = jnp.zeros((ln, LANES), F32)
        d_c = jnp.zeros((ln, LANES), F32)
        dxs = [jnp.zeros((ln, LANES), F32) for _ in range(gw // LANES)]
        for r in range(hpg):
            j, hf = divmod(r, LANES // HEAD)
            mh = ((lane >= HEAD * hf) & (lane < HEAD * (hf + 1))).astype(F32)
            ac = ca_ref[r]
            dt = cd_ref[r]
            ar = rf_ref[pl.ds(r, 1), :]
            aend = rf_ref[pl.ds(4 + r, 1), :]
            dm = jnp.where(causal, jnp.exp(jnp.minimum(ac - ar, 0.0)), 0.0)
            m = cb * dm
            mb = m.astype(BF16)
            xp = x_ref[:, j * LANES:(j + 1) * LANES]
            xdt = xp * dt * mh
            xdtb = xdt.astype(BF16)
            dyp = dy_ref[:, j * LANES:(j + 1) * LANES] * mh
            dypb = dyp.astype(BF16)
            h = hp_ref[r]
            hb = h.astype(BF16)
            dh = dh_scr[r]
            dhb = dh.astype(BF16)
            e_in = jnp.exp(ac)
            dte = jnp.exp(aend - ac)
            eend = jnp.exp(aend)
            d_m = _dot_nt(dypb, xdtb)
            dcb = dcb + d_m * dm
            gm = d_m * m
            yoff_pre = _dot_nt(cbf, hb)
            bdh = _dot_nt(bb, dhb)
            dxdt = _dot_tn(mb, dypb) + bdh * dte
            t1 = _rowsum(xdt * bdh) * dte
            gh, gl = _split2(gm)
            dacum = (_rowsum(gm) - (_dot_tn(gh, ones) + _dot_tn(gl, ones))
                     + _rowsum(dyp * yoff_pre) * e_in - t1)
            end_term = _colsum(t1) + eend * jnp.sum(_colsum(dh * h), axis=1, keepdims=True)
            dacum = dacum + last_row * end_term
            da_ref[r] = _dot_x3_left(tri_ge, dacum)
            ddt_ref[r] = jnp.broadcast_to(_rowsum(dxdt * xp), (ln, LANES))
            dxs[j] = dxs[j] + dxdt * dt
            d_b = d_b + _dot((xdt * dte).astype(BF16), dhb)
            dye = (dyp * e_in).astype(BF16)
            d_c = d_c + _dot(dye, hb)
            dh_scr[r] = eend * dh + _dot_tn(dye, cbf)
        dcbb = dcb.astype(BF16)
        dc_ref[...] = d_c + _dot(dcbb, bb)
        db_ref[...] = d_b + _dot_tn(dcbb, cbf)
        for j in range(gw // LANES):
            dx_ref[:, j * LANES:(j + 1) * LANES] = dxs[j]

    rev = nc - 1
    colspec = pl.BlockSpec((None, hpg, ln, LANES), lambda g, c: (g, 0, rev - c, 0))
    return pl.pallas_call(
        kern,
        name=name,
        grid=(ng, nc),
        in_specs=[pl.BlockSpec((ln, gw), lambda g, c: (rev - c, g)),
                  pl.BlockSpec((ln, LANES), lambda g, c: (rev - c, g)),
                  pl.BlockSpec((ln, LANES), lambda g, c: (rev - c, g)),
                  colspec, colspec,
                  pl.BlockSpec((None, None, 8, LANES), lambda g, c: (g, rev - c, 0, 0)),
                  pl.BlockSpec((None, None, hpg, LANES, LANES), lambda g, c: (g, rev - c, 0, 0, 0)),
                  pl.BlockSpec((ln, gw), lambda g, c: (rev - c, g))],
        out_specs=[pl.BlockSpec((ln, gw), lambda g, c: (rev - c, g)),
                   pl.BlockSpec((ln, LANES), lambda g, c: (rev - c, g)),
                   pl.BlockSpec((ln, LANES), lambda g, c: (rev - c, g)),
                   colspec, colspec],
        out_shape=[jax.ShapeDtypeStruct((s, d_inner), F32),
                   jax.ShapeDtypeStruct(bm.shape, F32), jax.ShapeDtypeStruct(cm.shape, F32),
                   jax.ShapeDtypeStruct(col_a.shape, F32), jax.ShapeDtypeStruct(col_a.shape, F32)],
        scratch_shapes=[pltpu.VMEM((hpg, LANES, LANES), F32)],
        compiler_params=_params(("parallel", "arbitrary")),
    )(xs, bm, cm, col_a, col_dt, rowf, hprev, dy)


def gnorm_fwd(y, xs, z, dexp, gain, ngroups, name):
    c = y.shape[1]
    gw = c // ngroups

    def fn(yv, xv, zv, dv, gv):
        yg = (yv + xv * dv) * (zv * _sigmoid(zv))
        outs = []
        for k in range(ngroups):
            t = yg[:, k * gw:(k + 1) * gw]
            outs.append(t * lax.rsqrt(jnp.mean(t * t, axis=1, keepdims=True) + EPS))
        return (jnp.concatenate(outs, axis=1) * gv,)

    return rowwise(fn, [(y, "row"), (xs, "row"), (z, "row"), (dexp, "full"), (gain, "full")], [(c, BF16)], tr=256, name=name)[0]


def gnorm_bwd(dn, y, xs, z, dexp, gain, ngroups, name):
    c = y.shape[1]
    gw = c // ngroups

    def fn(dnv, yv, xv, zv, dv, gv):
        yd = yv + xv * dv
        sg = _sigmoid(zv)
        sz = zv * sg
        yg = yd * sz
        dng = dnv * gv
        dyg, yh = [], []
        for k in range(ngroups):
            sl = slice(k * gw, (k + 1) * gw)
            t = yg[:, sl]
            r = lax.rsqrt(jnp.mean(t * t, axis=1, keepdims=True) + EPS)
            th = t * r
            dyg.append(r * (dng[:, sl] - th * jnp.mean(dng[:, sl] * th, axis=1, keepdims=True)))
            yh.append(th)
        dyg = jnp.concatenate(dyg, axis=1)
        yh = jnp.concatenate(yh, axis=1)
        dyd = dyg * sz
        dz = dyg * yd * (sg * (1.0 + zv * (1.0 - sg)))
        return dyd, dyd * dv, dz, _colsum(dyd * xv), _colsum(dnv * yh)

    return rowwise(fn, [(dn, "row"), (y, "row"), (xs, "row"), (z, "row"), (dexp, "full"), (gain, "full")],
                   [(c, F32), (c, F32), (c, BF16)], [(1, c), (1, c)], tr=256, name=name)


def ssd_post(ddt, da, dt, dtr, bias, alog, name):
    def fn(ddtv, dav, dtv, dtrv, bv, al):
        a_neg = -jnp.exp(al)
        ddtr = (ddtv + dav * a_neg) * _sigmoid(dtrv + bv)
        return ddtr, _colsum(ddtr), _colsum(dav * dtv) * a_neg

    return rowwise(fn, [(ddt, "row"), (da, "row"), (dt, "row"), (dtr, "row"), (bias, "full"), (alog, "full")],
                   [(LANES, BF16)], [(1, LANES), (1, LANES)], tr=512, name=name)


def _from_colform(v, s):
    ng, hpg = v.shape[0], v.shape[1]
    flat = v[..., 0].reshape(ng * hpg, s).T
    return jnp.pad(flat, ((0, 0), (0, LANES - ng * hpg)))


def ssm_fwd(x, g, p, tag):
    ng, hpg, d_inner = p["ng"], p["hpg"], p["d_inner"]
    h = rms_fwd(x, g, f"ssm_rms_{tag}")
    z = mm(h, p["w_z"], name=f"ssm_inz_{tag}")
    xbc = mm(h, p["w_xbc"], name=f"ssm_inx_{tag}")
    dtr = mm(h, p["w_dt"], name=f"ssm_indt_{tag}")
    xs, bm, cm = conv_fwd(xbc, p["conv_w"], p["conv_b"], d_inner, f"ssm_conv_{tag}")
    dt, acum = ssd_pre(dtr, p["dt_bias"], p["a_log"], f"ssm_pre_{tag}")
    col_a, col_dt = _ssd_layouts(acum, ng, hpg), _ssd_layouts(dt, ng, hpg)
    rowf = _ssd_rowform(acum, ng, hpg)
    y, hprev = ssd_chunk_fwd(xs, bm, cm, col_a, col_dt, rowf, f"ssm_scan_{tag}")
    n = gnorm_fwd(y, xs, z, p["d_exp"], p["norm_gain"], ng, f"ssm_gnorm_{tag}")
    xn = mm(n, p["w_out"], add=x, name=f"ssm_out_{tag}")
    return xn, (x, h, z, xbc, dtr, xs, bm, cm, dt, col_a, col_dt, rowf, y, hprev, n)


def ssm_bwd(dxn, saved, g, p, tag):
    x, h, z, xbc, dtr, xs, bm, cm, dt, col_a, col_dt, rowf, y, hprev, n = saved
    ng, hpg, d_inner = p["ng"], p["hpg"], p["d_inner"]
    s = x.shape[0]
    dn = mm(dxn, p["w_out"], tb=True, name=f"ssm_dn_{tag}")
    dwout = mm(n, dxn, ta=True, name=f"ssm_dwout_{tag}")
    dy, dxs_skip, dz, dd_lane, dgain = gnorm_bwd(dn, y, xs, z, p["d_exp"], p["norm_gain"], ng, f"ssm_dgnorm_{tag}")
    dxs, dbm, dcm, ddt_c, da_c = ssd_chunk_bwd(xs, bm, cm, col_a, col_dt, rowf, hprev, dy, f"ssm_dscan_{tag}")
    ddtr, dbias, dalog = ssd_post(_from_colform(ddt_c, s), _from_colform(da_c, s), dt, dtr,
                                  p["dt_bias"], p["a_log"], f"ssm_post_{tag}")
    res = conv_bwd_pre(xbc, p["conv_w"], p["conv_b"], dxs, dxs_skip, dbm, dcm, f"ssm_dconv_{tag}")
    dpre, dconv_w, dconv_b = res[0], jnp.concatenate(res[1:5], axis=0), res[5]
    dxbc = conv_bwd_in(dpre, p["conv_w"], f"ssm_dconvin_{tag}")
    dh = mm(dz, p["w_z"], tb=True, name=f"ssm_dhz_{tag}")
    dh = mm(dxbc, p["w_xbc"], tb=True, add=dh, name=f"ssm_dhx_{tag}")
    dh = mm(ddtr, p["w_dt"], tb=True, add=dh, name=f"ssm_dhdt_{tag}")
    dwz = mm(h, dz, ta=True, name=f"ssm_dwz_{tag}")
    dwxbc = mm(h, dxbc, ta=True, name=f"ssm_dwxbc_{tag}")
    dwdt = mm(h, ddtr, ta=True, name=f"ssm_dwdt_{tag}")
    dx, dg = rms_bwd(x, g, dh, dxn, f"ssm_drms_{tag}")
    nh = ng * hpg
    dwin = jnp.concatenate([dwz, dwxbc, dwdt[:, :nh]], axis=1)
    dd = dd_lane.reshape(nh, HEAD).sum(-1)
    return dx, dg, dict(w_in=dwin, conv_w=dconv_w, conv_b=dconv_b, dt_bias=dbias[0, :nh], a_log=dalog[0, :nh],
                        d=dd, norm_gain=dgain, w_out=dwout)


def local_step(x, target, w):
    d = x.shape[1]
    depth = w["mix_norm"].shape[0]
    bd = _head_blockdiag(d)
    tril = jnp.tril(jnp.ones((CHUNK, CHUNK), bool))
    ssm_heads = w["ssm_dt_bias"].shape[1]
    d_inner = w["ssm_w_out"].shape[1]
    ng = w["ssm_norm_gain"].shape[1] // 256
    nstate = CHUNK

    def pad_lanes(v):
        return jnp.pad(v, ((0, 0), (0, LANES - v.shape[1])))

    def ssm_params(j):
        w_in = w["ssm_w_in"][j]
        cw = w["ssm_conv_w"][j]
        return dict(ng=ng, hpg=ssm_heads // ng, d_inner=d_inner,
                    w_z=w_in[:, :d_inner], w_xbc=w_in[:, d_inner:d_inner + d_inner + 2 * ng * nstate],
                    w_dt=pad_lanes(w_in[:, 2 * d_inner + 2 * ng * nstate:]),
                    conv_w=[cw[k:k + 1] for k in range(cw.shape[0])], conv_b=w["ssm_conv_b"][j:j + 1],
                    dt_bias=pad_lanes(w["ssm_dt_bias"][j:j + 1]), a_log=pad_lanes(w["ssm_a_log"][j:j + 1]),
                    d_exp=jnp.repeat(w["ssm_d"][j], HEAD)[None, :], norm_gain=w["ssm_norm_gain"][j:j + 1],
                    w_out=w["ssm_w_out"][j])

    def gm_params(j):
        wc = jnp.where(tril, w["gm_w_s"][j], 0.0).astype(BF16)
        bst = jnp.repeat(w["gm_b_s"][j].T, LANES, axis=1)
        return wc, bst

    def sb_gains(j):
        nh = d // HEAD
        return jnp.tile(w["sb_q_gain"][j], nh)[None, :], jnp.tile(w["sb_k_gain"][j], nh)[None, :]

    saved = []
    cur = x
    for i in range(depth):
        kind, j = i % 3, i // 3
        gmix = w["mix_norm"][i:i + 1]
        if kind == 0:
            qg, kg = sb_gains(j)
            cur, sv = sb_fwd(cur, gmix, w["sb_w_qkv"][j], qg, kg, w["sb_w_o"][j], bd, f"{i}")
        elif kind == 1:
            wc, bst = gm_params(j)
            cur, sv = gm_fwd(cur, gmix, w["gm_w_in"][j], w["gm_b_in"][j:j + 1], w["gm_v_gain"][j:j + 1], wc, bst,
                             w["gm_w_out"][j], f"{i}")
        else:
            cur, sv = ssm_fwd(cur, gmix, ssm_params(j), f"{i}")
        cur, sv2 = ffn_fwd(cur, w["ffn_norm"][i:i + 1], w["ffn_w_gu"][i], w["ffn_w_down"][i], f"{i}")
        saved.append((sv, sv2))

    loss, dcur = loss_and_grad(cur, target, "loss")

    grads = {k: [None] * v.shape[0] for k, v in w.items()}
    for i in reversed(range(depth)):
        kind, j = i % 3, i // 3
        sv, sv2 = saved[i]
        gmix = w["mix_norm"][i:i + 1]
        dcur, dgf, dwgu, dwdown = ffn_bwd(dcur, sv2, w["ffn_norm"][i:i + 1], w["ffn_w_gu"][i], w["ffn_w_down"][i], f"{i}")
        grads["ffn_norm"][i], grads["ffn_w_gu"][i], grads["ffn_w_down"][i] = dgf[0], dwgu, dwdown
        if kind == 0:
            qg, kg = sb_gains(j)
            dcur, dg, dwqkv, dqg, dkg, dwo = sb_bwd(dcur, sv, gmix, w["sb_w_qkv"][j], qg, kg, w["sb_w_o"][j], bd, f"{i}")
            grads["sb_w_qkv"][j], grads["sb_q_gain"][j], grads["sb_k_gain"][j], grads["sb_w_o"][j] = dwqkv, dqg, dkg, dwo
        elif kind == 1:
            wc, bst = gm_params(j)
            dcur, dg, dwin, dbin, dvg, dws, dbs, dwout = gm_bwd(dcur, sv, gmix, w["gm_w_in"][j], w["gm_v_gain"][j:j + 1],
                                                                 wc, bst, w["gm_w_out"][j], f"{i}")
            grads["gm_w_in"][j], grads["gm_b_in"][j], grads["gm_v_gain"][j] = dwin, dbin[0], dvg[0]
            grads["gm_w_s"][j], grads["gm_b_s"][j], grads["gm_w_out"][j] = dws, dbs, dwout
        else:
            dcur, dg, gs = ssm_bwd(dcur, sv, gmix, ssm_params(j), f"{i}")
            grads["ssm_w_in"][j], grads["ssm_conv_w"][j], grads["ssm_conv_b"][j] = gs["w_in"], gs["conv_w"], gs["conv_b"][0]
            grads["ssm_dt_bias"][j], grads["ssm_a_log"][j], grads["ssm_d"][j] = gs["dt_bias"], gs["a_log"], gs["d"]
            grads["ssm_norm_gain"][j], grads["ssm_w_out"][j] = gs["norm_gain"][0], gs["w_out"]
        grads["mix_norm"][i] = dg[0]
    grads = {k: jnp.stack(v) for k, v in grads.items()}
    return loss, dcur, grads


WEIGHTS = ["mix_norm", "ffn_norm", "sb_w_qkv", "sb_q_gain", "sb_k_gain", "sb_w_o", "gm_w_in", "gm_b_in", "gm_v_gain",
           "gm_w_s", "gm_b_s", "gm_w_out", "ssm_w_in", "ssm_conv_w", "ssm_conv_b", "ssm_dt_bias", "ssm_a_log", "ssm_d",
           "ssm_norm_gain", "ssm_w_out", "ffn_w_gu", "ffn_w_down"]
SHARDED = {"sb_w_qkv": 2, "sb_w_o": 1, "gm_w_in": 2, "gm_w_out": 1, "ssm_w_in": 2, "ssm_conv_w": 2, "ssm_conv_b": 1,
           "ssm_norm_gain": 1, "ssm_w_out": 1, "ffn_w_gu": 2, "ffn_w_down": 1}
EXACT = ("ssm_conv_w", "ssm_conv_b", "ssm_norm_gain")
REPLICATED = [n for n in WEIGHTS if n not in SHARDED]
N_CHIPS = 4
N_DEV = 8
PACK_COLS = 1024
PACK_ROW_ALIGN = 32


def _pack(pieces, dtype, align):
    flat = jnp.concatenate([p.reshape(-1).astype(dtype) for p in pieces])
    rows = -(-flat.shape[0] // (PACK_COLS * align)) * align
    flat = jnp.pad(flat, (0, rows * PACK_COLS - flat.shape[0]))
    return flat.reshape(rows, PACK_COLS)


def _unpack(flat, shapes):
    out, off = [], 0
    for shp in shapes:
        n = math.prod(shp)
        out.append(flat[off:off + n].reshape(shp))
        off += n
    return out


def _to_words(v):
    return lax.bitcast_convert_type(v.astype(F32), BF16)


def _from_words(v):
    return lax.bitcast_convert_type(v, F32)


ANY = pl.BlockSpec(memory_space=pl.ANY)


def _pos():
    return lax.axis_index("x"), lax.axis_index("y"), lax.axis_index("c")


def _remote(src, dst, send, recv, k, to):
    return pltpu.make_async_remote_copy(src_ref=src, dst_ref=dst, send_sem=send.at[k], recv_sem=recv.at[k],
                                        device_id=to, device_id_type=MESH_ID)


def gather_weights(wp):
    rows, cols = wp.shape
    half = rows // 2

    def body(w_ref, o_ref, send, recv, lsem):
        x, y, c = _pos()
        me, sibling = (x, y, c), (x, y, 1 - c)
        chips = [(1 - x, y), (x, 1 - y), (1 - x, 1 - y)]

        def part(chip, cc):
            return o_ref.at[2 * chip[0] + chip[1], pl.ds(cc * half, half), :]

        mine = pltpu.make_async_copy(w_ref, o_ref.at[2 * x + y], lsem)
        mine.start()
        first = [_remote(w_ref.at[pl.ds(c * half, half), :], part((x, y), c), send, recv, j, (*chip, c))
                 for j, chip in enumerate(chips)]
        for cp in first:
            cp.start()
        passed = [_remote(part(chip, c), part(chip, c), send, recv, 3 + j, sibling) for j, chip in enumerate(chips)]
        for j, chip in enumerate(chips):
            _remote(part(chip, c), part(chip, c), send, recv, j, me).wait_recv()
            passed[j].start()
        for j, chip in enumerate(chips):
            _remote(part(chip, 1 - c), part(chip, 1 - c), send, recv, 3 + j, me).wait_recv()
        for cp in first + passed:
            cp.wait_send()
        mine.wait()

    return pl.pallas_call(
        body, name="gather_weights",
        out_shape=jax.ShapeDtypeStruct((N_CHIPS, rows, cols), wp.dtype),
        in_specs=[ANY], out_specs=ANY,
        scratch_shapes=[pltpu.SemaphoreType.DMA((6,)), pltpu.SemaphoreType.DMA((6,)), pltpu.SemaphoreType.DMA],
    )(wp)


def swap_halves(gp):
    nch, rows, cols = gp.shape
    half = rows // 2

    def body(g_ref, r_ref, send, recv):
        x, y, c = _pos()
        cp = _remote(g_ref.at[:, pl.ds((1 - c) * half, half), :], r_ref, send, recv, 0, (x, y, 1 - c))
        cp.start()
        cp.wait()

    return pl.pallas_call(
        body, name="swap_halves",
        out_shape=jax.ShapeDtypeStruct((nch, half, cols), gp.dtype),
        in_specs=[ANY], out_specs=ANY,
        scratch_shapes=[pltpu.SemaphoreType.DMA((1,)), pltpu.SemaphoreType.DMA((1,))],
    )(gp)


def scatter_chunks(p):
    nch, half, cols = p.shape

    def body(p_ref, r_ref, send, recv):
        x, y, c = _pos()
        chips = [(1 - x, y), (x, 1 - y), (1 - x, 1 - y)]
        cps = [_remote(p_ref.at[2 * chip[0] + chip[1]], r_ref.at[j], send, recv, j, (*chip, c))
               for j, chip in enumerate(chips)]
        for cp in cps:
            cp.start()
        for cp in cps:
            cp.wait()

    return pl.pallas_call(
        body, name="scatter_chunks",
        out_shape=jax.ShapeDtypeStruct((N_CHIPS - 1, half, cols), p.dtype),
        in_specs=[ANY], out_specs=ANY,
        scratch_shapes=[pltpu.SemaphoreType.DMA((3,)), pltpu.SemaphoreType.DMA((3,))],
    )(p)


def join_halves(hv):
    half, cols = hv.shape

    def body(h_ref, o_ref, send, recv, lsem):
        x, y, c = _pos()
        mine = pltpu.make_async_copy(h_ref, o_ref.at[pl.ds(c * half, half), :], lsem)
        mine.start()
        cp = _remote(h_ref, o_ref.at[pl.ds(c * half, half), :], send, recv, 0, (x, y, 1 - c))
        cp.start()
        _remote(h_ref, o_ref.at[pl.ds((1 - c) * half, half), :], send, recv, 0, (x, y, c)).wait_recv()
        cp.wait_send()
        mine.wait()

    return pl.pallas_call(
        body, name="join_halves",
        out_shape=jax.ShapeDtypeStruct((2 * half, cols), hv.dtype),
        in_specs=[ANY], out_specs=ANY,
        scratch_shapes=[pltpu.SemaphoreType.DMA((1,)), pltpu.SemaphoreType.DMA((1,)), pltpu.SemaphoreType.DMA],
    )(hv)


def gather_small(sg):
    rows, cols = sg.shape

    def body(s_ref, o_ref, send, recv, lsem):
        x, y, c = _pos()
        mine = pltpu.make_async_copy(s_ref, o_ref.at[4 * x + 2 * y + c], lsem)
        mine.start()
        peers = []
        for msk in range(1, N_DEV):
            px = 1 - x if msk & 4 else x
            py = 1 - y if msk & 2 else y
            pc = 1 - c if msk & 1 else c
            peers.append((px, py, pc))
        cps = [_remote(s_ref, o_ref.at[4 * x + 2 * y + c], send, recv, k, peer) for k, peer in enumerate(peers)]
        for cp in cps:
            cp.start()
        for k, (px, py, pc) in enumerate(peers):
            _remote(s_ref, o_ref.at[4 * px + 2 * py + pc], send, recv, k, (x, y, c)).wait_recv()
        for cp in cps:
            cp.wait_send()
        mine.wait()

    return pl.pallas_call(
        body, name="gather_small",
        out_shape=jax.ShapeDtypeStruct((N_DEV, rows, cols), sg.dtype),
        in_specs=[ANY], out_specs=ANY,
        scratch_shapes=[pltpu.SemaphoreType.DMA((N_DEV - 1,)), pltpu.SemaphoreType.DMA((N_DEV - 1,)), pltpu.SemaphoreType.DMA],
    )(sg)


def add_arrays(arrs, name):
    def fn(*vs):
        acc = vs[0]
        for v in vs[1:]:
            acc = acc + v
        return (acc,)

    return rowwise(fn, [(a, "row") for a in arrs], [(arrs[0].shape[1], F32)], tr=504, name=name)[0]


def small_update(gath, w, m, v, name):
    def fn(*vs):
        g = vs[0]
        for t in vs[1:N_DEV]:
            g = g + t
        wv, mv, vv = vs[N_DEV:]
        m2 = ADAM_B1 * mv + (1.0 - ADAM_B1) * g
        v2 = ADAM_B2 * vv + (1.0 - ADAM_B2) * (g * g)
        m_hat = m2 / (1.0 - ADAM_B1 ** ADAM_STEP)
        v_hat = v2 / (1.0 - ADAM_B2 ** ADAM_STEP)
        return g, -ADAM_LR * (m_hat / (jnp.sqrt(v_hat) + ADAM_EPS) + ADAM_WD * wv), m2, v2

    c = w.shape[1]
    ins = [(gath[k], "row") for k in range(N_DEV)] + [(w, "row"), (m, "row"), (v, "row")]
    return rowwise(fn, ins, [(c, F32)] * 4, tr=w.shape[0], name=name)


def _step(ins):
    x, target = ins["x"][0], ins["loss_target"][0]
    cc = lax.axis_index("c")
    chip = 2 * lax.axis_index("x") + lax.axis_index("y")
    sharded = list(SHARDED)

    pieces = [_to_words(ins[n]) if n in EXACT else ins[n] for n in sharded]
    piece_shapes = [p.shape for p in pieces]
    gathered = gather_weights(_pack(pieces, BF16, PACK_ROW_ALIGN))
    full = {}
    per_chip = [_unpack(gathered[k].reshape(-1), piece_shapes) for k in range(N_CHIPS)]
    for t, n in enumerate(sharded):
        parts = [per_chip[k][t] for k in range(N_CHIPS)]
        if n in EXACT:
            parts = [_from_words(p) for p in parts]
        full[n] = jnp.concatenate(parts, axis=SHARDED[n])
    for n in REPLICATED:
        full[n] = ins[n]

    loss, dx, grads = local_step(x, target, full)
    loss = lax.psum(loss, ALL_AXES)

    def chunks_of(g, axis):
        return jnp.split(g, N_CHIPS, axis=axis)

    split = {n: chunks_of(grads[n], SHARDED[n]) for n in sharded}
    gp = jnp.stack([_pack([split[n][k] for n in sharded], F32, PACK_ROW_ALIGN) for k in range(N_CHIPS)])
    rows = gp.shape[1]
    half = rows // 2
    theirs = swap_halves(gp)
    mine = lax.dynamic_slice_in_dim(gp, cc * half, half, axis=1)
    part = add_arrays([mine.reshape(N_CHIPS * half, PACK_COLS), theirs.reshape(N_CHIPS * half, PACK_COLS)],
                      "sum_cores").reshape(N_CHIPS, half, PACK_COLS)
    others = scatter_chunks(part)
    own = lax.dynamic_index_in_dim(part, chip, axis=0, keepdims=False)
    ghalf = add_arrays([own, others[0], others[1], others[2]], "sum_chips")
    gshard = join_halves(ghalf)
    shard_shapes = [ins[n].shape for n in sharded]
    gsh = dict(zip(sharded, _unpack(gshard.reshape(-1), shard_shapes)))

    small_shapes = [ins[n].shape for n in REPLICATED]
    gath = gather_small(_pack([grads[n] for n in REPLICATED], F32, SUBLANES))
    packed = [_pack([ins[pre + n] for n in REPLICATED], F32, SUBLANES) for pre in ("", "m_", "v_")]
    res = small_update(gath, *packed, name="small_update")
    small = [dict(zip(REPLICATED, _unpack(r.reshape(-1), small_shapes))) for r in res]

    out_g, out_d, out_m, out_v = {}, {}, {}, {}
    for n in REPLICATED:
        out_g[n], out_d[n], out_m[n], out_v[n] = (s[n] for s in small)
    for n in sharded:
        shp = ins[n].shape
        two = (math.prod(shp[:-1]), shp[-1])
        d2, m2, v2 = adamw(ins[n].reshape(two), gsh[n].reshape(two), ins["m_" + n].reshape(two),
                           ins["v_" + n].reshape(two), f"adamw_{n}")
        out_g[n], out_d[n], out_m[n], out_v[n] = gsh[n], d2.reshape(shp), m2.reshape(shp), v2.reshape(shp)
    return (loss, dx[None], *[out_g[n] for n in WEIGHTS], *[out_d[n] for n in WEIGHTS],
            *[out_m[n] for n in WEIGHTS], *[out_v[n] for n in WEIGHTS])


def kernel(x, mix_norm, ffn_norm, sb_w_qkv, sb_q_gain, sb_k_gain, sb_w_o, gm_w_in, gm_b_in, gm_v_gain, gm_w_s, gm_b_s, gm_w_out, ssm_w_in, ssm_conv_w, ssm_conv_b, ssm_dt_bias, ssm_a_log, ssm_d, ssm_norm_gain, ssm_w_out, ffn_w_gu, ffn_w_down, loss_target, m_mix_norm, m_ffn_norm, m_sb_w_qkv, m_sb_q_gain, m_sb_k_gain, m_sb_w_o, m_gm_w_in, m_gm_b_in, m_gm_v_gain, m_gm_w_s, m_gm_b_s, m_gm_w_out, m_ssm_w_in, m_ssm_conv_w, m_ssm_conv_b, m_ssm_dt_bias, m_ssm_a_log, m_ssm_d, m_ssm_norm_gain, m_ssm_w_out, m_ffn_w_gu, m_ffn_w_down, v_mix_norm, v_ffn_norm, v_sb_w_qkv, v_sb_q_gain, v_sb_k_gain, v_sb_w_o, v_gm_w_in, v_gm_b_in, v_gm_v_gain, v_gm_w_s, v_gm_b_s, v_gm_w_out, v_ssm_w_in, v_ssm_conv_w, v_ssm_conv_b, v_ssm_dt_bias, v_ssm_a_log, v_ssm_d, v_ssm_norm_gain, v_ssm_w_out, v_ffn_w_gu, v_ffn_w_down):
    return _step(dict(locals()))
```

```python
import functools
import math

import jax
import jax.numpy as jnp
from jax import lax
from jax.experimental import pallas as pl
from jax.experimental.pallas import tpu as pltpu

F32 = jnp.float32
BF16 = jnp.bfloat16
EPS = 1e-6
LANES = 128
SUBLANES = 8
VMEM_LIMIT = 56 * 1024 * 1024
HEAD = 64
CHUNK = 128
SB_TQ, SB_TK = 256, 256
ADAM_LR, ADAM_B1, ADAM_B2, ADAM_EPS, ADAM_WD, ADAM_STEP = 0.001, 0.9, 0.999, 1e-08, 0.01, 10
MESH_ID = pl.DeviceIdType.MESH
ALL_AXES = ("x", "y", "c")


def _params(sem):
    return pltpu.CompilerParams(dimension_semantics=sem, vmem_limit_bytes=VMEM_LIMIT)


def _pick(n, cands):
    for c in cands:
        if n % c == 0:
            return c
    return n


def _dot(a, b, dims=((1,), (0,))):
    return lax.dot_general(a, b, (dims, ((), ())), preferred_element_type=F32)


def _dot_nt(a, b):
    return _dot(a, b, ((1,), (1,)))


def _dot_tn(a, b):
    return _dot(a, b, ((0,), (0,)))


def _split2(x):
    hi = x.astype(BF16)
    lo = (x - hi.astype(F32)).astype(BF16)
    return hi, lo


def _dot_x2(x, m):
    hi, lo = _split2(x)
    return _dot(hi, m) + _dot(lo, m)


def _dot_x3_left(m, x):
    h1 = x.astype(BF16)
    r1 = x - h1.astype(F32)
    h2 = r1.astype(BF16)
    h3 = (r1 - h2.astype(F32)).astype(BF16)
    return _dot(m, h1) + _dot(m, h2) + _dot(m, h3)


def _sigmoid(x):
    return 1.0 / (1.0 + jnp.exp(-x))


def _softplus(x):
    return jnp.maximum(x, 0.0) + jnp.log(1.0 + jnp.exp(-jnp.abs(x)))


def _colsum(x):
    return jnp.sum(x, axis=0, keepdims=True)


def _rowsum(x):
    return jnp.sum(x, axis=1, keepdims=True)


def _iota2(shape, dim):
    return lax.broadcasted_iota(jnp.int32, shape, dim)


def mm(a, b, *, ta=False, tb=False, add=None, bias=None, b_chunks=False, out_chunks=False, name):
    if ta:
        kk, m = a.shape
    else:
        m, kk = a.shape
    nch, wide = 1, None
    if b_chunks:
        nch, rows_b, wide = b.shape
        kb, n = (rows_b, nch * wide) if not tb else (nch * wide, rows_b)
    elif tb:
        n, kb = b.shape
    else:
        kb, n = b.shape
    if out_chunks:
        nch, wide = N_CHIPS, n // N_CHIPS
    assert kk == kb, (a.shape, b.shape, ta, tb)
    tm = _pick(m, (512, 256, 128))
    tn = _pick(wide if (wide and not tb) or out_chunks else n, (512, 256, 1408, 128))
    tk = _pick(wide if (wide and tb) else kk, (1024, 1408, 512, 256, 128))
    nk = kk // tk
    dims = ((0 if ta else 1,), (1 if tb else 0,))
    has_add, has_bias = add is not None, bias is not None

    def kern(*refs):
        a_ref, b_ref = refs[0], refs[1]
        rest = list(refs[2:])
        add_ref = rest.pop(0) if has_add else None
        bias_ref = rest.pop(0) if has_bias else None
        o_ref, acc_ref = rest
        k = pl.program_id(2)

        @pl.when(k == 0)
        def _():
            acc_ref[...] = jnp.zeros_like(acc_ref)

        acc_ref[...] += _dot(a_ref[...].astype(BF16), b_ref[...].astype(BF16), dims)

        @pl.when(k == nk - 1)
        def _():
            r = acc_ref[...]
            if has_add:
                r = r + add_ref[...]
            if has_bias:
                r = r + bias_ref[...]
            o_ref[...] = r

    a_spec = pl.BlockSpec((tk, tm), lambda i, j, k: (k, i)) if ta else pl.BlockSpec((tm, tk), lambda i, j, k: (i, k))
    if b_chunks and tb:
        per = wide // tk
        b_spec = pl.BlockSpec((None, tn, tk), lambda i, j, k: (k // per, j, k % per))
    elif b_chunks:
        per = wide // tn
        b_spec = pl.BlockSpec((None, tk, tn), lambda i, j, k: (j // per, k, j % per))
    elif tb:
        b_spec = pl.BlockSpec((tn, tk), lambda i, j, k: (j, k))
    else:
        b_spec = pl.BlockSpec((tk, tn), lambda i, j, k: (k, j))
    if out_chunks:
        per_o = wide // tn
        out_spec = pl.BlockSpec((None, tm, tn), lambda i, j, k: (j // per_o, i, j % per_o))
        out_shape = jax.ShapeDtypeStruct((nch, m, wide), F32)
    else:
        out_spec = pl.BlockSpec((tm, tn), lambda i, j, k: (i, j))
        out_shape = jax.ShapeDtypeStruct((m, n), F32)
    in_specs, args = [a_spec, b_spec], [a, b]
    if has_add:
        in_specs.append(pl.BlockSpec((tm, tn), lambda i, j, k: (i, j)))
        args.append(add)
    if has_bias:
        in_specs.append(pl.BlockSpec((1, tn), lambda i, j, k: (0, j)))
        args.append(bias)
    return pl.pallas_call(
        kern,
        name=name,
        grid=(m // tm, n // tn, nk),
        in_specs=in_specs,
        out_specs=out_spec,
        out_shape=out_shape,
        scratch_shapes=[pltpu.VMEM((tm, tn), F32)],
        compiler_params=_params(("parallel", "parallel", "arbitrary")),
    )(*args)


def rowwise(fn, ins, outs, accs=(), *, tr, name):
    rows = [a for a, kind in ins if kind == "row"][0].shape[0]
    tr = min(tr, rows)
    assert rows % tr == 0 and tr % SUBLANES == 0, (rows, tr)
    n = rows // tr
    n_in, n_out = len(ins), len(outs)
    kinds = [kind for _, kind in ins]

    def kern(*refs):
        i = pl.program_id(0)
        vals = []
        for ref, kind in zip(refs[:n_in], kinds):
            v = ref[...]
            if kind == "prev":
                v = v * (i > 0).astype(v.dtype)
            elif kind == "next":
                v = v * (i < n - 1).astype(v.dtype)
            vals.append(v)
        res = fn(*vals)
        for ref, r in zip(refs[n_in:n_in + n_out], res[:n_out]):
            ref[...] = r.astype(ref.dtype)
        if accs:
            acc_refs = refs[n_in + n_out:]

            @pl.when(i == 0)
            def _():
                for ref in acc_refs:
                    ref[...] = jnp.zeros_like(ref)

            for ref, r in zip(acc_refs, res[n_out:]):
                ref[...] += r

    in_specs = []
    for a, kind in ins:
        if kind == "row":
            in_specs.append(pl.BlockSpec((tr, a.shape[1]), lambda i: (i, 0)))
        elif kind == "full":
            in_specs.append(pl.BlockSpec(a.shape, lambda i, nd=a.ndim: (0,) * nd))
        elif kind == "prev":
            in_specs.append(pl.BlockSpec((SUBLANES, a.shape[1]),
                                         lambda i: (jnp.maximum(i * (tr // SUBLANES) - 1, 0), 0)))
        else:
            in_specs.append(pl.BlockSpec((SUBLANES, a.shape[1]),
                                         lambda i: (jnp.minimum((i + 1) * (tr // SUBLANES), rows // SUBLANES - 1), 0)))
    out_specs = [pl.BlockSpec((tr, c), lambda i: (i, 0)) for c, _ in outs]
    out_specs += [pl.BlockSpec((r, c), lambda i: (0, 0)) for r, c in accs]
    out_shape = [jax.ShapeDtypeStruct((rows, c), dt) for c, dt in outs]
    out_shape += [jax.ShapeDtypeStruct((r, c), F32) for r, c in accs]
    res = pl.pallas_call(
        kern,
        name=name,
        grid=(n,),
        in_specs=in_specs,
        out_specs=out_specs,
        out_shape=out_shape,
        compiler_params=_params(("arbitrary",) if accs else ("parallel",)),
    )(*[a for a, _ in ins])
    return res


def rms_fwd(x, g, name):
    def fn(xv, gv):
        r = lax.rsqrt(jnp.mean(xv * xv, axis=1, keepdims=True) + EPS)
        return (xv * r * gv,)

    return rowwise(fn, [(x, "row"), (g, "full")], [(x.shape[1], BF16)], tr=512, name=name)[0]


def rms_bwd(x, g, dy, dres, name):
    def fn(xv, gv, dyv, drv):
        r = lax.rsqrt(jnp.mean(xv * xv, axis=1, keepdims=True) + EPS)
        xh = xv * r
        dyg = dyv * gv
        dx = drv + r * (dyg - xh * jnp.mean(dyg * xh, axis=1, keepdims=True))
        return dx, _colsum(dyv * xh)

    c = x.shape[1]
    return rowwise(fn, [(x, "row"), (g, "full"), (dy, "row"), (dres, "row")], [(c, F32)], [(1, c)], tr=256, name=name)


def swiglu_fwd(gu, name):
    hid = gu.shape[1] // 2

    def fn(v):
        g, u = v[:, :hid], v[:, hid:]
        return (g * _sigmoid(g) * u,)

    return rowwise(fn, [(gu, "row")], [(hid, BF16)], tr=256, name=name)[0]


def swiglu_bwd(gu, da, name):
    hid = gu.shape[1] // 2

    def fn(v, d):
        g, u = v[:, :hid], v[:, hid:]
        s = _sigmoid(g)
        dg = d * u * s * (1.0 + g * (1.0 - s))
        du = d * g * s
        return (jnp.concatenate([dg, du], axis=1),)

    return rowwise(fn, [(gu, "row"), (da, "row")], [(2 * hid, BF16)], tr=256, name=name)[0]


def loss_and_grad(y, t, name):
    d = y.shape[1]

    def fn(yv, tv):
        e = yv - tv
        part = jnp.sum(_colsum(e * e), axis=1, keepdims=True) * (0.5 / d)
        return e * (1.0 / d), jnp.broadcast_to(part, (SUBLANES, LANES))

    dy, acc = rowwise(fn, [(y, "row"), (t, "row")], [(d, F32)], [(SUBLANES, LANES)], tr=512, name=name)
    return acc[0, 0], dy


def adamw(w, g, m, v, name):
    def fn(wv, gv, mv, vv):
        m2 = ADAM_B1 * mv + (1.0 - ADAM_B1) * gv
        v2 = ADAM_B2 * vv + (1.0 - ADAM_B2) * (gv * gv)
        m_hat = m2 / (1.0 - ADAM_B1 ** ADAM_STEP)
        v_hat = v2 / (1.0 - ADAM_B2 ** ADAM_STEP)
        delta = -ADAM_LR * (m_hat / (jnp.sqrt(v_hat) + ADAM_EPS) + ADAM_WD * wv)
        return delta, m2, v2

    rows, c = w.shape
    tr = _pick(rows, (256, 128, 64, 32, 16, 8)) if rows % SUBLANES == 0 else rows
    if rows % SUBLANES:
        return _whole(fn, [w, g, m, v], [(w.shape, F32)] * 3, name=name)
    return rowwise(fn, [(w, "row"), (g, "row"), (m, "row"), (v, "row")], [(c, F32)] * 3, tr=tr, name=name)


def _whole(fn, ins, outs, *, name):
    n_in = len(ins)

    def kern(*refs):
        res = fn(*[r[...] for r in refs[:n_in]])
        for ref, r in zip(refs[n_in:], res):
            ref[...] = r.astype(ref.dtype)

    return pl.pallas_call(
        kern,
        name=name,
        out_shape=[jax.ShapeDtypeStruct(s, dt) for s, dt in outs],
        compiler_params=pltpu.CompilerParams(vmem_limit_bytes=VMEM_LIMIT),
    )(*ins)


def ffn_fwd(x, g, wgu, wdown, tag):
    h = rms_fwd(x, g, f"ffn_rms_{tag}")
    gu = mm(h, wgu, b_chunks=True, name=f"ffn_gu_{tag}")
    a = swiglu_fwd(gu, f"ffn_act_{tag}")
    xn = mm(a, wdown, add=x, name=f"ffn_down_{tag}")
    return xn, (x, h, gu, a)


def ffn_bwd(dxn, saved, g, wgu, wdown, tag):
    x, h, gu, a = saved
    da = mm(dxn, wdown, tb=True, name=f"ffn_da_{tag}")
    dwdown = mm(a, dxn, ta=True, name=f"ffn_dwdown_{tag}")
    dgu = swiglu_bwd(gu, da, f"ffn_dact_{tag}")
    dh = mm(dgu, wgu, tb=True, b_chunks=True, name=f"ffn_dh_{tag}")
    dwgu = mm(h, dgu, ta=True, out_chunks=True, name=f"ffn_dwgu_{tag}")
    dx, dg = rms_bwd(x, g, dh, dxn, f"ffn_drms_{tag}")
    return dx, dg, dwgu, dwdown


def _head_blockdiag(c):
    i = jnp.arange(c) // HEAD
    return (i[:, None] == i[None, :]).astype(BF16)


def qknorm_fwd(qkv, qg, kg, bd, name):
    d = qkv.shape[1] // 3
    scale = 1.0 / math.sqrt(HEAD)

    def fn(v, qgv, kgv, bdv):
        q, k, vv = v[:, :d], v[:, d:2 * d], v[:, 2 * d:]
        rq = lax.rsqrt(_dot_x2(q * q, bdv) * (1.0 / HEAD) + EPS)
        rk = lax.rsqrt(_dot_x2(k * k, bdv) * (1.0 / HEAD) + EPS)
        return q * rq * qgv * scale, k * rk * kgv, vv

    return rowwise(fn, [(qkv, "row"), (qg, "full"), (kg, "full"), (bd, "full")],
                   [(d, BF16), (d, BF16), (d, BF16)], tr=256, name=name)


def qknorm_bwd(qkv, dqs, dkn, dv, qg, kg, bd, name):
    d = qkv.shape[1] // 3
    scale = 1.0 / math.sqrt(HEAD)

    def one(xv, gv, dyv, bdv):
        r = lax.rsqrt(_dot_x2(xv * xv, bdv) * (1.0 / HEAD) + EPS)
        xh = xv * r
        dyg = dyv * gv
        dx = r * (dyg - xh * (_dot_x2(dyg * xh, bdv) * (1.0 / HEAD)))
        return dx, _colsum(dyv * xh)

    def fn(v, dqv, dkv, dvv, qgv, kgv, bdv):
        q, k = v[:, :d], v[:, d:2 * d]
        dq, dqg = one(q, qgv, dqv * scale, bdv)
        dk, dkg = one(k, kgv, dkv, bdv)
        return jnp.concatenate([dq, dk, dvv], axis=1), dqg, dkg

    return rowwise(fn, [(qkv, "row"), (dqs, "row"), (dkn, "row"), (dv, "row"), (qg, "full"), (kg, "full"), (bd, "full")],
                   [(3 * d, BF16)], [(1, d), (1, d)], tr=256, name=name)


def _sb_tile(qh, k, mask, tri_gt):
    z = _dot_nt(qh, k)
    sp = jnp.log(1.0 + jnp.exp(-jnp.abs(z)))
    lb = jnp.minimum(z, 0.0) - sp
    l1 = jnp.where(mask, lb - z, 0.0)
    suf = _dot_x2(l1, tri_gt)
    return lb, l1, suf


def _sb_setup(tq, tk):
    row, col = _iota2((tq, tk), 0), _iota2((tq, tk), 1)
    lane = _iota2((1, LANES), 1)
    halves = [(lane < HEAD).astype(BF16), (lane >= HEAD).astype(BF16)]
    lane_q = _iota2((tq, LANES), 1) + jnp.minimum(_iota2((tq, LANES), 0), 0)
    return row, col, halves, lane_q


def sb_attn_fwd(qs, kn, vb, name):
    s, d = qs.shape
    tq, tk = min(SB_TQ, s), min(SB_TK, s)
    nq = s // tq
    assert s // tk <= LANES and s % tq == 0 and s % tk == 0

    def kern(q_ref, k_ref, v_ref, o_ref, rs_ref):
        i = pl.program_id(1)
        row, col, halves, lane_q = _sb_setup(tq, tk)
        tri_gt = (_iota2((tk, tk), 0) > _iota2((tk, tk), 1)).astype(BF16)
        q = q_ref[...]
        qh = [q * hm for hm in halves]
        o_ref[...] = jnp.zeros_like(o_ref)
        rs_ref[...] = jnp.zeros_like(rs_ref)
        nkb = (i + 1) * (tq // tk)

        def step(n, r):
            r = list(r)
            kb = nkb - 1 - n
            ks = pl.multiple_of(kb * tk, tk)
            k = k_ref[pl.ds(ks, tk), :]
            v = v_ref[pl.ds(ks, tk), :]
            mask = col < row + (i * tq - kb * tk)
            at_kb = lane_q == kb
            for hh in range(2):
                lb, l1, suf = _sb_tile(qh[hh], k, mask, tri_gt)
                w = jnp.where(mask, jnp.exp(lb + suf + r[hh]), 0.0)
                o_ref[...] += _dot(w.astype(BF16), v * halves[hh])
                rs_ref[hh] = jnp.where(at_kb, r[hh], rs_ref[hh])
                r[hh] = r[hh] + _rowsum(l1)
            return tuple(r)

        z1 = jnp.zeros((tq, 1), F32)
        lax.fori_loop(0, nkb, step, (z1, z1))

    nh2 = d // LANES
    return pl.pallas_call(
        kern,
        name=name,
        grid=(nh2, nq),
        in_specs=[pl.BlockSpec((tq, LANES), lambda h, i: (i, h)),
                  pl.BlockSpec((s, LANES), lambda h, i: (0, h)),
                  pl.BlockSpec((s, LANES), lambda h, i: (0, h))],
        out_specs=[pl.BlockSpec((tq, LANES), lambda h, i: (i, h)),
                   pl.BlockSpec((None, 2, tq, LANES), lambda h, i: (h, 0, i, 0))],
        out_shape=[jax.ShapeDtypeStruct((s, d), F32), jax.ShapeDtypeStruct((nh2, 2, s, LANES), F32)],
        compiler_params=_params(("parallel", "arbitrary")),
    )(qs, kn, vb)


def sb_attn_bwd(qs, kn, vb, rsave, do, name):
    s, d = qs.shape
    tq, tk = min(SB_TQ, s), min(SB_TK, s)
    nq = s // tq

    def kern(q_ref, k_ref, v_ref, rs_ref, do_ref, dq_ref, dk_ref, dv_ref):
        i = pl.program_id(1)

        @pl.when(i == 0)
        def _():
            dk_ref[...] = jnp.zeros_like(dk_ref)
            dv_ref[...] = jnp.zeros_like(dv_ref)

        row, col, halves, lane_q = _sb_setup(tq, tk)
        tri_gt = (_iota2((tk, tk), 0) > _iota2((tk, tk), 1)).astype(BF16)
        tri_lt = (_iota2((tk, tk), 0) < _iota2((tk, tk), 1)).astype(BF16)
        q = q_ref[...]
        qh = [q * hm for hm in halves]
        dov = do_ref[...].astype(BF16)
        doh = [dov * hm for hm in halves]
        dq_ref[...] = jnp.zeros_like(dq_ref)
        nkb = (i + 1) * (tq // tk)

        def step(kb, ep):
            ep = list(ep)
            ks = pl.multiple_of(kb * tk, tk)
            k = k_ref[pl.ds(ks, tk), :]
            v = v_ref[pl.ds(ks, tk), :]
            mask = col < row + (i * tq - kb * tk)
            at_kb = lane_q == kb
            for hh in range(2):
                lb, l1, suf = _sb_tile(qh[hh], k, mask, tri_gt)
                r = _rowsum(jnp.where(at_kb, rs_ref[hh], 0.0))
                w = jnp.where(mask, jnp.exp(lb + suf + r), 0.0)
                e = _dot_nt(doh[hh], v) * w
                pe = ep[hh] + _dot_x2(e, tri_lt)
                beta = jnp.exp(lb)
                dz = jnp.where(mask, e * (1.0 - beta) - pe * beta, 0.0).astype(BF16)
                dq_ref[...] += _dot(dz, k * halves[hh])
                dk_ref[pl.ds(ks, tk), :] += _dot_tn(dz, qh[hh])
                dv_ref[pl.ds(ks, tk), :] += _dot_tn(w.astype(BF16), doh[hh])
                ep[hh] = ep[hh] + _rowsum(e)
            return tuple(ep)

        z1 = jnp.zeros((tq, 1), F32)
        lax.fori_loop(0, nkb, step, (z1, z1))

    nh2 = d // LANES
    return pl.pallas_call(
        kern,
        name=name,
        grid=(nh2, nq),
        in_specs=[pl.BlockSpec((tq, LANES), lambda h, i: (i, h)),
                  pl.BlockSpec((s, LANES), lambda h, i: (0, h)),
                  pl.BlockSpec((s, LANES), lambda h, i: (0, h)),
                  pl.BlockSpec((None, 2, tq, LANES), lambda h, i: (h, 0, i, 0)),
                  pl.BlockSpec((tq, LANES), lambda h, i: (i, h))],
        out_specs=[pl.BlockSpec((tq, LANES), lambda h, i: (i, h)),
                   pl.BlockSpec((s, LANES), lambda h, i: (0, h)),
                   pl.BlockSpec((s, LANES), lambda h, i: (0, h))],
        out_shape=[jax.ShapeDtypeStruct((s, d), F32)] * 3,
        compiler_params=_params(("parallel", "arbitrary")),
    )(qs, kn, vb, rsave, do)


def sb_fwd(x, g, wqkv, qg, kg, wo, bd, tag):
    h = rms_fwd(x, g, f"sb_rms_{tag}")
    qkv = mm(h, wqkv, b_chunks=True, name=f"sb_qkv_{tag}")
    qs, kn, vb = qknorm_fwd(qkv, qg, kg, bd, f"sb_qknorm_{tag}")
    o, rsave = sb_attn_fwd(qs, kn, vb, f"sb_attn_{tag}")
    xn = mm(o, wo, add=x, name=f"sb_out_{tag}")
    return xn, (x, h, qkv, qs, kn, vb, rsave, o)


def sb_bwd(dxn, saved, g, wqkv, qg, kg, wo, bd, tag):
    x, h, qkv, qs, kn, vb, rsave, o = saved
    do = mm(dxn, wo, tb=True, name=f"sb_do_{tag}")
    dwo = mm(o, dxn, ta=True, name=f"sb_dwo_{tag}")
    dqs, dkn, dv = sb_attn_bwd(qs, kn, vb, rsave, do, f"sb_dattn_{tag}")
    dqkv, dqg, dkg = qknorm_bwd(qkv, dqs, dkn, dv, qg, kg, bd, f"sb_dqknorm_{tag}")
    dh = mm(dqkv, wqkv, tb=True, b_chunks=True, name=f"sb_dh_{tag}")
    dwqkv = mm(h, dqkv, ta=True, out_chunks=True, name=f"sb_dwqkv_{tag}")
    dx, dg = rms_bwd(x, g, dh, dxn, f"sb_drms_{tag}")
    nh = dqg.shape[1] // HEAD
    return dx, dg, dwqkv, dqg.reshape(nh, HEAD).sum(0), dkg.reshape(nh, HEAD).sum(0), dwo


def _gelu(x):
    return 0.5 * x * (1.0 + lax.erf(x * (1.0 / math.sqrt(2.0))))


def _gelu_grad(x):
    return 0.5 * (1.0 + lax.erf(x * (1.0 / math.sqrt(2.0)))) + x * jnp.exp(-0.5 * x * x) * (1.0 / math.sqrt(2.0 * math.pi))


def gm_act_fwd(pre, vg, name):
    half = pre.shape[1] // 2

    def fn(p, vgv):
        u = _gelu(p[:, :half])
        v = _gelu(p[:, half:])
        r = lax.rsqrt(jnp.mean(v * v, axis=1, keepdims=True) + EPS)
        return u, v * r * vgv

    return rowwise(fn, [(pre, "row"), (vg, "full")], [(half, F32), (half, BF16)], tr=256, name=name)


def gm_act_bwd(pre, du, dvn, vg, name):
    half = pre.shape[1] // 2

    def fn(p, duv, dvnv, vgv):
        pu, pv = p[:, :half], p[:, half:]
        v = _gelu(pv)
        r = lax.rsqrt(jnp.mean(v * v, axis=1, keepdims=True) + EPS)
        vh = v * r
        dyg = dvnv * vgv
        dv = r * (dyg - vh * jnp.mean(dyg * vh, axis=1, keepdims=True))
        dpre = jnp.concatenate([duv * _gelu_grad(pu), dv * _gelu_grad(pv)], axis=1)
        return dpre, _colsum(dvnv * vh), _colsum(dpre)

    return rowwise(fn, [(pre, "row"), (du, "row"), (dvn, "row"), (vg, "full")],
                   [(2 * half, BF16)], [(1, half), (1, 2 * half)], tr=256, name=name)


def gm_spatial_fwd(u, vn, wc, bst, name):
    s, c = u.shape
    t = CHUNK
    ng = c // LANES

    def kern(u_ref, v_ref, w_ref, b_ref, o_ref):
        for g in range(ng):
            sl = slice(g * LANES, (g + 1) * LANES)
            mixed = _dot(w_ref[g], v_ref[:, sl]) + b_ref[:, sl]
            o_ref[:, sl] = (u_ref[:, sl] * mixed).astype(BF16)

    return pl.pallas_call(
        kern,
        name=name,
        grid=(s // t,),
        in_specs=[pl.BlockSpec((t, c), lambda i: (i, 0)), pl.BlockSpec((t, c), lambda i: (i, 0)),
                  pl.BlockSpec(wc.shape, lambda i: (0, 0, 0)), pl.BlockSpec(bst.shape, lambda i: (0, 0))],
        out_specs=pl.BlockSpec((t, c), lambda i: (i, 0)),
        out_shape=jax.ShapeDtypeStruct((s, c), BF16),
        compiler_params=_params(("parallel",)),
    )(u, vn, wc, bst)


def gm_spatial_bwd(dgate, u, vn, wc, bst, name):
    s, c = u.shape
    t = CHUNK
    ng = c // LANES

    def kern(dg_ref, u_ref, v_ref, w_ref, b_ref, du_ref, dv_ref, dw_ref, db_ref):
        i = pl.program_id(0)

        @pl.when(i == 0)
        def _():
            dw_ref[...] = jnp.zeros_like(dw_ref)
            db_ref[...] = jnp.zeros_like(db_ref)

        for g in range(ng):
            sl = slice(g * LANES, (g + 1) * LANES)
            vg = v_ref[:, sl]
            dgv = dg_ref[:, sl]
            mixed = _dot(w_ref[g], vg) + b_ref[:, sl]
            du_ref[:, sl] = dgv * mixed
            dmix = dgv * u_ref[:, sl]
            dmb = dmix.astype(BF16)
            dv_ref[:, sl] = _dot_tn(w_ref[g], dmb)
            dw_ref[g] += _dot_nt(dmb, vg)
            db_ref[:, sl] += dmix

    return pl.pallas_call(
        kern,
        name=name,
        grid=(s // t,),
        in_specs=[pl.BlockSpec((t, c), lambda i: (i, 0))] * 3 +
                 [pl.BlockSpec(wc.shape, lambda i: (0, 0, 0)), pl.BlockSpec(bst.shape, lambda i: (0, 0))],
        out_specs=[pl.BlockSpec((t, c), lambda i: (i, 0)), pl.BlockSpec((t, c), lambda i: (i, 0)),
                   pl.BlockSpec(wc.shape, lambda i: (0, 0, 0)), pl.BlockSpec(bst.shape, lambda i: (0, 0))],
        out_shape=[jax.ShapeDtypeStruct((s, c), F32), jax.ShapeDtypeStruct((s, c), F32),
                   jax.ShapeDtypeStruct(wc.shape, F32), jax.ShapeDtypeStruct(bst.shape, F32)],
        compiler_params=_params(("arbitrary",)),
    )(dgate, u, vn, wc, bst)


def gm_fwd(x, g, w_in, b_in, vg, wc, bst, w_out, tag):
    h = rms_fwd(x, g, f"gm_rms_{tag}")
    pre = mm(h, w_in, bias=b_in, b_chunks=True, name=f"gm_in_{tag}")
    u, vn = gm_act_fwd(pre, vg, f"gm_act_{tag}")
    gate = gm_spatial_fwd(u, vn, wc, bst, f"gm_spatial_{tag}")
    xn = mm(gate, w_out, add=x, name=f"gm_out_{tag}")
    return xn, (x, h, pre, u, vn, gate)


def gm_bwd(dxn, saved, g, w_in, vg, wc, bst, w_out, tag):
    x, h, pre, u, vn, gate = saved
    dgate = mm(dxn, w_out, tb=True, name=f"gm_dgate_{tag}")
    dwout = mm(gate, dxn, ta=True, name=f"gm_dwout_{tag}")
    du, dvn, dws, dbst = gm_spatial_bwd(dgate, u, vn, wc, bst, f"gm_dspatial_{tag}")
    dpre, dvg, dbin = gm_act_bwd(pre, du, dvn, vg, f"gm_dact_{tag}")
    dh = mm(dpre, w_in, tb=True, b_chunks=True, name=f"gm_dh_{tag}")
    dwin = mm(h, dpre, ta=True, out_chunks=True, name=f"gm_dwin_{tag}")
    dx, dg = rms_bwd(x, g, dh, dxn, f"gm_drms_{tag}")
    ng = wc.shape[0]
    dws = jnp.where(jnp.tril(jnp.ones((CHUNK, CHUNK), bool)), dws, 0.0)
    dbs = dbst.reshape(CHUNK, ng, LANES).sum(-1).T
    return dx, dg, dwin, dbin, dvg, dws, dbs, dwout


def _conv_taps(xv, prev):
    cat = jnp.concatenate([prev, xv], axis=0)
    return [pltpu.roll(cat, sh, 0)[SUBLANES:] for sh in (3, 2, 1)] + [xv]


def conv_fwd(xbc, ws, b, d_inner, name):
    c = xbc.shape[1]
    nst = (c - d_inner) // 2

    def fn(xv, prev, w0, w1, w2, w3, bv):
        taps = _conv_taps(xv, prev)
        pre = bv + w0 * taps[0] + w1 * taps[1] + w2 * taps[2] + w3 * taps[3]
        out = pre * _sigmoid(pre)
        return out[:, :d_inner], out[:, d_inner:d_inner + nst], out[:, d_inner + nst:]

    return rowwise(fn, [(xbc, "row"), (xbc, "prev")] + [(w, "full") for w in ws] + [(b, "full")],
                   [(d_inner, F32), (nst, F32), (nst, F32)], tr=256, name=name)


def conv_bwd_pre(xbc, ws, b, dxs_a, dxs_b, db_m, dc_m, name):
    c = xbc.shape[1]

    def fn(xv, prev, w0, w1, w2, w3, bv, da, db2, dbm, dcm):
        taps = _conv_taps(xv, prev)
        pre = bv + w0 * taps[0] + w1 * taps[1] + w2 * taps[2] + w3 * taps[3]
        sg = _sigmoid(pre)
        dout = jnp.concatenate([da + db2, dbm, dcm], axis=1)
        dpre = dout * sg * (1.0 + pre * (1.0 - sg))
        return (dpre,) + tuple(_colsum(dpre * tp) for tp in taps) + (_colsum(dpre),)

    return rowwise(fn, [(xbc, "row"), (xbc, "prev")] + [(w, "full") for w in ws] +
                   [(b, "full"), (dxs_a, "row"), (dxs_b, "row"), (db_m, "row"), (dc_m, "row")],
                   [(c, F32)], [(1, c)] * 5, tr=256, name=name)


def conv_bwd_in(dpre, ws, name):
    c = dpre.shape[1]

    def fn(dv, nxt, w0, w1, w2, w3):
        cat = jnp.concatenate([dv, nxt], axis=0)
        n = cat.shape[0]
        up = [pltpu.roll(cat, n - sh, 0)[:dv.shape[0]] for sh in (1, 2, 3)]
        return (w3 * dv + w2 * up[0] + w1 * up[1] + w0 * up[2],)

    return rowwise(fn, [(dpre, "row"), (dpre, "next")] + [(w, "full") for w in ws], [(c, BF16)], tr=256, name=name)[0]


def ssd_pre(dtr, bias, alog, name):
    def fn(d, bv, al, tri):
        dt = _softplus(d + bv)
        a = dt * (-jnp.exp(al))
        return dt, _dot_x3_left(tri, a)

    tri = jnp.tril(jnp.ones((CHUNK, CHUNK), BF16))
    return rowwise(fn, [(dtr, "row"), (bias, "full"), (alog, "full"), (tri, "full")],
                   [(LANES, F32), (LANES, F32)], tr=CHUNK, name=name)


def _ssd_layouts(v, ngroups, hpg):
    s = v.shape[0]
    col = v[:, :ngroups * hpg].T.reshape(ngroups, hpg, s, 1)
    return jnp.broadcast_to(col, (ngroups, hpg, s, LANES))


def _ssd_rowform(acum, ngroups, hpg):
    s = acum.shape[0]
    nc = s // CHUNK
    a = acum[:, :ngroups * hpg].reshape(nc, CHUNK, ngroups, hpg).transpose(2, 0, 3, 1)
    last = jnp.broadcast_to(a[..., CHUNK - 1:], a.shape)
    return jnp.concatenate([a, last], axis=2)


def ssd_chunk_fwd(xs, bm, cm, col_a, col_dt, rowf, name):
    s, d_inner = xs.shape
    ln = CHUNK
    nc = s // ln
    ng, hpg = col_a.shape[0], col_a.shape[1]
    gw = d_inner // ng
    assert gw == hpg * HEAD and gw % LANES == 0 and bm.shape[1] == ng * LANES

    def kern(x_ref, b_ref, c_ref, ca_ref, cd_ref, rf_ref, y_ref, hp_ref, h_scr):
        c = pl.program_id(1)

        @pl.when(c == 0)
        def _():
            h_scr[...] = jnp.zeros_like(h_scr)

        bb = b_ref[...].astype(BF16)
        cbf = c_ref[...].astype(BF16)
        cb = _dot_nt(cbf, bb)
        causal = _iota2((ln, ln), 0) >= _iota2((ln, ln), 1)
        lane = _iota2((1, LANES), 1)
        ys = [jnp.zeros((ln, LANES), F32) for _ in range(gw // LANES)]
        for r in range(hpg):
            j, hf = divmod(r, LANES // HEAD)
            mh = ((lane >= HEAD * hf) & (lane < HEAD * (hf + 1))).astype(F32)
            ac = ca_ref[r]
            ar = rf_ref[pl.ds(r, 1), :]
            aend = rf_ref[pl.ds(4 + r, 1), :]
            dm = jnp.exp(jnp.minimum(ac - ar, 0.0))
            m = jnp.where(causal, cb * dm, 0.0).astype(BF16)
            xdt = x_ref[:, j * LANES:(j + 1) * LANES] * cd_ref[r] * mh
            h = h_scr[r]
            hp_ref[r] = h
            ys[j] = ys[j] + _dot(m, xdt.astype(BF16)) + _dot_nt(cbf, h.astype(BF16)) * jnp.exp(ac)
            dte = jnp.exp(aend - ac)
            h_scr[r] = jnp.exp(aend) * h + _dot_tn((xdt * dte).astype(BF16), bb)
        for j in range(gw // LANES):
            y_ref[:, j * LANES:(j + 1) * LANES] = ys[j]

    return pl.pallas_call(
        kern,
        name=name,
        grid=(ng, nc),
        in_specs=[pl.BlockSpec((ln, gw), lambda g, c: (c, g)),
                  pl.BlockSpec((ln, LANES), lambda g, c: (c, g)),
                  pl.BlockSpec((ln, LANES), lambda g, c: (c, g)),
                  pl.BlockSpec((None, hpg, ln, LANES), lambda g, c: (g, 0, c, 0)),
                  pl.BlockSpec((None, hpg, ln, LANES), lambda g, c: (g, 0, c, 0)),
                  pl.BlockSpec((None, None, 8, LANES), lambda g, c: (g, c, 0, 0))],
        out_specs=[pl.BlockSpec((ln, gw), lambda g, c: (c, g)),
                   pl.BlockSpec((None, None, hpg, LANES, LANES), lambda g, c: (g, c, 0, 0, 0))],
        out_shape=[jax.ShapeDtypeStruct((s, d_inner), F32),
                   jax.ShapeDtypeStruct((ng, nc, hpg, LANES, LANES), F32)],
        scratch_shapes=[pltpu.VMEM((hpg, LANES, LANES), F32)],
        compiler_params=_params(("parallel", "arbitrary")),
    )(xs, bm, cm, col_a, col_dt, rowf)


def ssd_chunk_bwd(xs, bm, cm, col_a, col_dt, rowf, hprev, dy, name):
    s, d_inner = xs.shape
    ln = CHUNK
    nc = s // ln
    ng, hpg = col_a.shape[0], col_a.shape[1]
    gw = d_inner // ng

    def kern(x_ref, b_ref, c_ref, ca_ref, cd_ref, rf_ref, hp_ref, dy_ref,
             dx_ref, db_ref, dc_ref, ddt_ref, da_ref, dh_scr):
        c = pl.program_id(1)

        @pl.when(c == 0)
        def _():
            dh_scr[...] = jnp.zeros_like(dh_scr)

        bb = b_ref[...].astype(BF16)
        cbf = c_ref[...].astype(BF16)
        cb = _dot_nt(cbf, bb)
        row, col = _iota2((ln, ln), 0), _iota2((ln, ln), 1)
        causal = row >= col
        tri_ge = (col >= row).astype(BF16)
        ones = jnp.ones((ln, LANES), BF16)
        lane = _iota2((1, LANES), 1)
        last_row = (_iota2((ln, 1), 0) == ln - 1).astype(F32)
        dcb = jnp.zeros((ln, ln), F32)
        d_b = jnp.zeros((ln, LANES), F32)
        d_c = jnp.zeros((ln, LANES), F32)
        dxs = [jnp.zeros((ln, LANES), F32) for _ in range(gw // LANES)]
        for r in range(hpg):
            j, hf = divmod(r, LANES // HEAD)
            mh = ((lane >= HEAD * hf) & (lane < HEAD * (hf + 1))).astype(F32)
            ac = ca_ref[r]
            dt = cd_ref[r]
            ar = rf_ref[pl.ds(r, 1), :]
            aend = rf_ref[pl.ds(4 + r, 1), :]
            dm = jnp.where(causal, jnp.exp(jnp.minimum(ac - ar, 0.0)), 0.0)
            m = cb * dm
            mb = m.astype(BF16)
            xp = x_ref[:, j * LANES:(j + 1) * LANES]
            xdt = xp * dt * mh
            xdtb = xdt.astype(BF16)
            dyp = dy_ref[:, j * LANES:(j + 1) * LANES] * mh
            dypb = dyp.astype(BF16)
            h = hp_ref[r]
            hb = h.astype(BF16)
            dh = dh_scr[r]
            dhb = dh.astype(BF16)
            e_in = jnp.exp(ac)
            dte = jnp.exp(aend - ac)
            eend = jnp.exp(aend)
            d_m = _dot_nt(dypb, xdtb)
            dcb = dcb + d_m * dm
            gm = d_m * m
            yoff_pre = _dot_nt(cbf, hb)
            bdh = _dot_nt(bb, dhb)
            dxdt = _dot_tn(mb, dypb) + bdh * dte
            t1 = _rowsum(xdt * bdh) * dte
            gh, gl = _split2(gm)
            dacum = (_rowsum(gm) - (_dot_tn(gh, ones) + _dot_tn(gl, ones))
                     + _rowsum(dyp * yoff_pre) * e_in - t1)
            end_term = _colsum(t1) + eend * jnp.sum(_colsum(dh * h), axis=1, keepdims=True)
            dacum = dacum + last_row * end_term
            da_ref[r] = _dot_x3_left(tri_ge, dacum)
            ddt_ref[r] = jnp.broadcast_to(_rowsum(dxdt * xp), (ln, LANES))
            dxs[j] = dxs[j] + dxdt * dt
            d_b = d_b + _dot((xdt * dte).astype(BF16), dhb)
            dye = (dyp * e_in).astype(BF16)
            d_c = d_c + _dot(dye, hb)
            dh_scr[r] = eend * dh + _dot_tn(dye, cbf)
        dcbb = dcb.astype(BF16)
        dc_ref[...] = d_c + _dot(dcbb, bb)
        db_ref[...] = d_b + _dot_tn(dcbb, cbf)
        for j in range(gw // LANES):
            dx_ref[:, j * LANES:(j + 1) * LANES] = dxs[j]

    rev = nc - 1
    colspec = pl.BlockSpec((None, hpg, ln, LANES), lambda g, c: (g, 0, rev - c, 0))
    return pl.pallas_call(
        kern,
        name=name,
        grid=(ng, nc),
        in_specs=[pl.BlockSpec((ln, gw), lambda g, c: (rev - c, g)),
                  pl.BlockSpec((ln, LANES), lambda g, c: (rev - c, g)),
                  pl.BlockSpec((ln, LANES), lambda g, c: (rev - c, g)),
                  colspec, colspec,
                  pl.BlockSpec((None, None, 8, LANES), lambda g, c: (g, rev - c, 0, 0)),
                  pl.BlockSpec((None, None, hpg, LANES, LANES), lambda g, c: (g, rev - c, 0, 0, 0)),
                  pl.BlockSpec((ln, gw), lambda g, c: (rev - c, g))],
        out_specs=[pl.BlockSpec((ln, gw), lambda g, c: (rev - c, g)),
                   pl.BlockSpec((ln, LANES), lambda g, c: (rev - c, g)),
                   pl.BlockSpec((ln, LANES), lambda g, c: (rev - c, g)),
                   colspec, colspec],
        out_shape=[jax.ShapeDtypeStruct((s, d_inner), F32),
                   jax.ShapeDtypeStruct(bm.shape, F32), jax.ShapeDtypeStruct(cm.shape, F32),
                   jax.ShapeDtypeStruct(col_a.shape, F32), jax.ShapeDtypeStruct(col_a.shape, F32)],
        scratch_shapes=[pltpu.VMEM((hpg, LANES, LANES), F32)],
        compiler_params=_params(("parallel", "arbitrary")),
    )(xs, bm, cm, col_a, col_dt, rowf, hprev, dy)


def gnorm_fwd(y, xs, z, dexp, gain, ngroups, name):
    c = y.shape[1]
    gw = c // ngroups

    def fn(yv, xv, zv, dv, gv):
        yg = (yv + xv * dv) * (zv * _sigmoid(zv))
        outs = []
        for k in range(ngroups):
            t = yg[:, k * gw:(k + 1) * gw]
            outs.append(t * lax.rsqrt(jnp.mean(t * t, axis=1, keepdims=True) + EPS))
        return (jnp.concatenate(outs, axis=1) * gv,)

    return rowwise(fn, [(y, "row"), (xs, "row"), (z, "row"), (dexp, "full"), (gain, "full")], [(c, BF16)], tr=256, name=name)[0]


def gnorm_bwd(dn, y, xs, z, dexp, gain, ngroups, name):
    c = y.shape[1]
    gw = c // ngroups

    def fn(dnv, yv, xv, zv, dv, gv):
        yd = yv + xv * dv
        sg = _sigmoid(zv)
        sz = zv * sg
        yg = yd * sz
        dng = dnv * gv
        dyg, yh = [], []
        for k in range(ngroups):
            sl = slice(k * gw, (k + 1) * gw)
            t = yg[:, sl]
            r = lax.rsqrt(jnp.mean(t * t, axis=1, keepdims=True) + EPS)
            th = t * r
            dyg.append(r * (dng[:, sl] - th * jnp.mean(dng[:, sl] * th, axis=1, keepdims=True)))
            yh.append(th)
        dyg = jnp.concatenate(dyg, axis=1)
        yh = jnp.concatenate(yh, axis=1)
        dyd = dyg * sz
        dz = dyg * yd * (sg * (1.0 + zv * (1.0 - sg)))
        return dyd, dyd * dv, dz, _colsum(dyd * xv), _colsum(dnv * yh)

    return rowwise(fn, [(dn, "row"), (y, "row"), (xs, "row"), (z, "row"), (dexp, "full"), (gain, "full")],
                   [(c, F32), (c, F32), (c, BF16)], [(1, c), (1, c)], tr=256, name=name)


def ssd_post(ddt, da, dt, dtr, bias, alog, name):
    def fn(ddtv, dav, dtv, dtrv, bv, al):
        a_neg = -jnp.exp(al)
        ddtr = (ddtv + dav * a_neg) * _sigmoid(dtrv + bv)
        return ddtr, _colsum(ddtr), _colsum(dav * dtv) * a_neg

    return rowwise(fn, [(ddt, "row"), (da, "row"), (dt, "row"), (dtr, "row"), (bias, "full"), (alog, "full")],
                   [(LANES, BF16)], [(1, LANES), (1, LANES)], tr=512, name=name)


def _from_colform(v, s):
    ng, hpg = v.shape[0], v.shape[1]
    flat = v[..., 0].reshape(ng * hpg, s).T
    return jnp.pad(flat, ((0, 0), (0, LANES - ng * hpg)))


def ssm_fwd(x, g, p, tag):
    ng, hpg, d_inner = p["ng"], p["hpg"], p["d_inner"]
    h = rms_fwd(x, g, f"ssm_rms_{tag}")
    z = mm(h, p["w_z"], name=f"ssm_inz_{tag}")
    xbc = mm(h, p["w_xbc"], name=f"ssm_inx_{tag}")
    dtr = mm(h, p["w_dt"], name=f"ssm_indt_{tag}")
    xs, bm, cm = conv_fwd(xbc, p["conv_w"], p["conv_b"], d_inner, f"ssm_conv_{tag}")
    dt, acum = ssd_pre(dtr, p["dt_bias"], p["a_log"], f"ssm_pre_{tag}")
    col_a, col_dt = _ssd_layouts(acum, ng, hpg), _ssd_layouts(dt, ng, hpg)
    rowf = _ssd_rowform(acum, ng, hpg)
    y, hprev = ssd_chunk_fwd(xs, bm, cm, col_a, col_dt, rowf, f"ssm_scan_{tag}")
    n = gnorm_fwd(y, xs, z, p["d_exp"], p["norm_gain"], ng, f"ssm_gnorm_{tag}")
    xn = mm(n, p["w_out"], add=x, name=f"ssm_out_{tag}")
    return xn, (x, h, z, xbc, dtr, xs, bm, cm, dt, col_a, col_dt, rowf, y, hprev, n)


def ssm_bwd(dxn, saved, g, p, tag):
    x, h, z, xbc, dtr, xs, bm, cm, dt, col_a, col_dt, rowf, y, hprev, n = saved
    ng, hpg, d_inner = p["ng"], p["hpg"], p["d_inner"]
    s = x.shape[0]
    dn = mm(dxn, p["w_out"], tb=True, name=f"ssm_dn_{tag}")
    dwout = mm(n, dxn, ta=True, name=f"ssm_dwout_{tag}")
    dy, dxs_skip, dz, dd_lane, dgain = gnorm_bwd(dn, y, xs, z, p["d_exp"], p["norm_gain"], ng, f"ssm_dgnorm_{tag}")
    dxs, dbm, dcm, ddt_c, da_c = ssd_chunk_bwd(xs, bm, cm, col_a, col_dt, rowf, hprev, dy, f"ssm_dscan_{tag}")
    ddtr, dbias, dalog = ssd_post(_from_colform(ddt_c, s), _from_colform(da_c, s), dt, dtr,
                                  p["dt_bias"], p["a_log"], f"ssm_post_{tag}")
    res = conv_bwd_pre(xbc, p["conv_w"], p["conv_b"], dxs, dxs_skip, dbm, dcm, f"ssm_dconv_{tag}")
    dpre, dconv_w, dconv_b = res[0], jnp.concatenate(res[1:5], axis=0), res[5]
    dxbc = conv_bwd_in(dpre, p["conv_w"], f"ssm_dconvin_{tag}")
    dh = mm(dz, p["w_z"], tb=True, name=f"ssm_dhz_{tag}")
    dh = mm(dxbc, p["w_xbc"], tb=True, add=dh, name=f"ssm_dhx_{tag}")
    dh = mm(ddtr, p["w_dt"], tb=True, add=dh, name=f"ssm_dhdt_{tag}")
    dwz = mm(h, dz, ta=True, name=f"ssm_dwz_{tag}")
    dwxbc = mm(h, dxbc, ta=True, name=f"ssm_dwxbc_{tag}")
    dwdt = mm(h, ddtr, ta=True, name=f"ssm_dwdt_{tag}")
    dx, dg = rms_bwd(x, g, dh, dxn, f"ssm_drms_{tag}")
    nh = ng * hpg
    dwin = jnp.concatenate([dwz, dwxbc, dwdt[:, :nh]], axis=1)
    dd = dd_lane.reshape(nh, HEAD).sum(-1)
    return dx, dg, dict(w_in=dwin, conv_w=dconv_w, conv_b=dconv_b, dt_bias=dbias[0, :nh], a_log=dalog[0, :nh],
                        d=dd, norm_gain=dgain, w_out=dwout)


def local_step(x, target, w):
    d = x.shape[1]
    depth = w["mix_norm"].shape[0]
    bd = _head_blockdiag(d)
    tril = jnp.tril(jnp.ones((CHUNK, CHUNK), bool))
    ssm_heads = w["ssm_dt_bias"].shape[1]
    d_inner = w["ssm_norm_gain"].shape[1]
    ng = w["ssm_norm_gain"].shape[1] // 256
    nstate = CHUNK

    def pad_lanes(v):
        return jnp.pad(v, ((0, 0), (0, LANES - v.shape[1])))

    def ssm_params(j):
        w_in = w["ssm_w_in"][j]
        cw = w["ssm_conv_w"][j]
        return dict(ng=ng, hpg=ssm_heads // ng, d_inner=d_inner,
                    w_z=w_in[:, :d_inner], w_xbc=w_in[:, d_inner:d_inner + d_inner + 2 * ng * nstate],
                    w_dt=pad_lanes(w_in[:, 2 * d_inner + 2 * ng * nstate:]),
                    conv_w=[cw[k:k + 1] for k in range(cw.shape[0])], conv_b=w["ssm_conv_b"][j:j + 1],
                    dt_bias=pad_lanes(w["ssm_dt_bias"][j:j + 1]), a_log=pad_lanes(w["ssm_a_log"][j:j + 1]),
                    d_exp=jnp.repeat(w["ssm_d"][j], HEAD)[None, :], norm_gain=w["ssm_norm_gain"][j:j + 1],
                    w_out=w["ssm_w_out"][j])

    def gm_params(j):
        wc = jnp.where(tril, w["gm_w_s"][j], 0.0).astype(BF16)
        bst = jnp.repeat(w["gm_b_s"][j].T, LANES, axis=1)
        return wc, bst

    def sb_gains(j):
        nh = d // HEAD
        return jnp.tile(w["sb_q_gain"][j], nh)[None, :], jnp.tile(w["sb_k_gain"][j], nh)[None, :]

    saved = []
    cur = x
    for i in range(depth):
        kind, j = i % 3, i // 3
        gmix = w["mix_norm"][i:i + 1]
        if kind == 0:
            qg, kg = sb_gains(j)
            cur, sv = sb_fwd(cur, gmix, w["sb_w_qkv"][j], qg, kg, w["sb_w_o"][j], bd, f"{i}")
        elif kind == 1:
            wc, bst = gm_params(j)
            cur, sv = gm_fwd(cur, gmix, w["gm_w_in"][j], w["gm_b_in"][j:j + 1], w["gm_v_gain"][j:j + 1], wc, bst,
                             w["gm_w_out"][j], f"{i}")
        else:
            cur, sv = ssm_fwd(cur, gmix, ssm_params(j), f"{i}")
        cur, sv2 = ffn_fwd(cur, w["ffn_norm"][i:i + 1], w["ffn_w_gu"][i], w["ffn_w_down"][i], f"{i}")
        saved.append((sv, sv2))

    loss, dcur = loss_and_grad(cur, target, "loss")

    grads = {k: [None] * len(v) for k, v in w.items()}
    for i in reversed(range(depth)):
        kind, j = i % 3, i // 3
        sv, sv2 = saved[i]
        gmix = w["mix_norm"][i:i + 1]
        dcur, dgf, dwgu, dwdown = ffn_bwd(dcur, sv2, w["ffn_norm"][i:i + 1], w["ffn_w_gu"][i], w["ffn_w_down"][i], f"{i}")
        grads["ffn_norm"][i], grads["ffn_w_gu"][i], grads["ffn_w_down"][i] = dgf[0], dwgu, dwdown
        if kind == 0:
            qg, kg = sb_gains(j)
            dcur, dg, dwqkv, dqg, dkg, dwo = sb_bwd(dcur, sv, gmix, w["sb_w_qkv"][j], qg, kg, w["sb_w_o"][j], bd, f"{i}")
            grads["sb_w_qkv"][j], grads["sb_q_gain"][j], grads["sb_k_gain"][j], grads["sb_w_o"][j] = dwqkv, dqg, dkg, dwo
        elif kind == 1:
            wc, bst = gm_params(j)
            dcur, dg, dwin, dbin, dvg, dws, dbs, dwout = gm_bwd(dcur, sv, gmix, w["gm_w_in"][j], w["gm_v_gain"][j:j + 1],
                                                                 wc, bst, w["gm_w_out"][j], f"{i}")
            grads["gm_w_in"][j], grads["gm_b_in"][j], grads["gm_v_gain"][j] = dwin, dbin[0], dvg[0]
            grads["gm_w_s"][j], grads["gm_b_s"][j], grads["gm_w_out"][j] = dws, dbs, dwout
        else:
            dcur, dg, gs = ssm_bwd(dcur, sv, gmix, ssm_params(j), f"{i}")
            grads["ssm_w_in"][j], grads["ssm_conv_w"][j], grads["ssm_conv_b"][j] = gs["w_in"], gs["conv_w"], gs["conv_b"][0]
            grads["ssm_dt_bias"][j], grads["ssm_a_log"][j], grads["ssm_d"][j] = gs["dt_bias"], gs["a_log"], gs["d"]
            grads["ssm_norm_gain"][j], grads["ssm_w_out"][j] = gs["norm_gain"][0], gs["w_out"]
        grads["mix_norm"][i] = dg[0]
    grads = {k: (v if k in MATRICES else jnp.stack(v)) for k, v in grads.items()}
    return loss, dcur, grads


WEIGHTS = ["mix_norm", "ffn_norm", "sb_w_qkv", "sb_q_gain", "sb_k_gain", "sb_w_o", "gm_w_in", "gm_b_in", "gm_v_gain",
           "gm_w_s", "gm_b_s", "gm_w_out", "ssm_w_in", "ssm_conv_w", "ssm_conv_b", "ssm_dt_bias", "ssm_a_log", "ssm_d",
           "ssm_norm_gain", "ssm_w_out", "ffn_w_gu", "ffn_w_down"]
SHARDED = {"sb_w_qkv": 2, "sb_w_o": 1, "gm_w_in": 2, "gm_w_out": 1, "ssm_w_in": 2, "ssm_conv_w": 2, "ssm_conv_b": 1,
           "ssm_norm_gain": 1, "ssm_w_out": 1, "ffn_w_gu": 2, "ffn_w_down": 1}
EXACT = ("ssm_conv_w", "ssm_conv_b", "ssm_norm_gain")
MATRICES = tuple(n for n in SHARDED if n not in EXACT)
COLUMN_BLOCKS = ("sb_w_qkv", "gm_w_in", "ffn_w_gu")
REPLICATED = [n for n in WEIGHTS if n not in SHARDED]
N_CHIPS = 4
N_DEV = 8
PACK_COLS = 1024


def _pack(pieces, dtype, align):
    flat = jnp.concatenate([p.reshape(-1).astype(dtype) for p in pieces])
    rows = -(-flat.shape[0] // (PACK_COLS * align)) * align
    flat = jnp.pad(flat, (0, rows * PACK_COLS - flat.shape[0]))
    return flat.reshape(rows, PACK_COLS)


def _unpack(flat, shapes):
    out, off = [], 0
    for shp in shapes:
        n = math.prod(shp)
        out.append(flat[off:off + n].reshape(shp))
        off += n
    return out


ANY = pl.BlockSpec(memory_space=pl.ANY)


def _pos():
    return lax.axis_index("x"), lax.axis_index("y"), lax.axis_index("c")


def _remote(src, dst, send, recv, k, to):
    return pltpu.make_async_remote_copy(src_ref=src, dst_ref=dst, send_sem=send.at[k], recv_sem=recv.at[k],
                                        device_id=to, device_id_type=MESH_ID)


def _comm_call(body, name, ins, out_shapes, nsem, nlocal=0):
    scratch = [pltpu.SemaphoreType.DMA((nsem,)), pltpu.SemaphoreType.DMA((nsem,))]
    if nlocal:
        scratch.append(pltpu.SemaphoreType.DMA((nlocal,)))
    return pl.pallas_call(
        body, name=name, out_shape=out_shapes,
        in_specs=[ANY] * len(ins), out_specs=[ANY] * len(out_shapes), scratch_shapes=scratch,
    )(*ins)


def gather_weights(shards, name):
    n = len(shards)

    def body(*refs):
        w_refs, o_refs = refs[:n], refs[n:2 * n]
        send, recv, lsem = refs[2 * n:]
        x, y, c = _pos()
        me, sibling = (x, y, c), (x, y, 1 - c)
        chips = [(1 - x, y), (x, 1 - y), (1 - x, 1 - y)]

        def part(u, chip, cc):
            half = shards[u].shape[0] // 2
            return o_refs[u].at[2 * chip[0] + chip[1], pl.ds(cc * half, half), :]

        mine = [pltpu.make_async_copy(w_refs[u], o_refs[u].at[2 * x + y], lsem.at[u]) for u in range(n)]
        for cp in mine:
            cp.start()
        first = []
        for u in range(n):
            half = shards[u].shape[0] // 2
            first += [_remote(w_refs[u].at[pl.ds(c * half, half), :], part(u, (x, y), c), send, recv, 6 * u + j, (*chip, c))
                      for j, chip in enumerate(chips)]
        for cp in first:
            cp.start()
        passed = []
        for u in range(n):
            for j, chip in enumerate(chips):
                _remote(part(u, chip, c), part(u, chip, c), send, recv, 6 * u + j, me).wait_recv()
                fw = _remote(part(u, chip, c), part(u, chip, c), send, recv, 6 * u + 3 + j, sibling)
                fw.start()
                passed.append(fw)
        for u in range(n):
            for j, chip in enumerate(chips):
                _remote(part(u, chip, 1 - c), part(u, chip, 1 - c), send, recv, 6 * u + 3 + j, me).wait_recv()
        for cp in first + passed:
            cp.wait_send()
        for cp in mine:
            cp.wait()

    outs = [jax.ShapeDtypeStruct((N_CHIPS,) + s.shape, s.dtype) for s in shards]
    return _comm_call(body, name, shards, outs, 6 * n, n)


def swap_halves(gps):
    n = len(gps)

    def body(*refs):
        g_refs, r_refs = refs[:n], refs[n:2 * n]
        send, recv = refs[2 * n:]
        x, y, c = _pos()
        cps = []
        for u in range(n):
            half = gps[u].shape[1] // 2
            cps.append(_remote(g_refs[u].at[:, pl.ds((1 - c) * half, half), :], r_refs[u], send, recv, u, (x, y, 1 - c)))
        for cp in cps:
            cp.start()
        for cp in cps:
            cp.wait()

    outs = [jax.ShapeDtypeStruct((g.shape[0], g.shape[1] // 2, g.shape[2]), g.dtype) for g in gps]
    return _comm_call(body, "swap_halves", gps, outs, n)


def scatter_chunks(parts):
    n = len(parts)

    def body(*refs):
        p_refs, r_refs = refs[:n], refs[n:2 * n]
        send, recv = refs[2 * n:]
        x, y, c = _pos()
        chips = [(1 - x, y), (x, 1 - y), (1 - x, 1 - y)]
        cps = [_remote(p_refs[u].at[2 * chip[0] + chip[1]], r_refs[u].at[j], send, recv, 3 * u + j, (*chip, c))
               for u in range(n) for j, chip in enumerate(chips)]
        for cp in cps:
            cp.start()
        for cp in cps:
            cp.wait()

    outs = [jax.ShapeDtypeStruct((N_CHIPS - 1,) + p.shape[1:], p.dtype) for p in parts]
    return _comm_call(body, "scatter_chunks", parts, outs, 3 * n)


def join_halves(hvs):
    n = len(hvs)

    def body(*refs):
        h_refs, o_refs = refs[:n], refs[n:2 * n]
        send, recv, lsem = refs[2 * n:]
        x, y, c = _pos()
        mine, cps = [], []
        for u in range(n):
            half = hvs[u].shape[0]
            mine.append(pltpu.make_async_copy(h_refs[u], o_refs[u].at[pl.ds(c * half, half), :], lsem.at[u]))
            cps.append(_remote(h_refs[u], o_refs[u].at[pl.ds(c * half, half), :], send, recv, u, (x, y, 1 - c)))
        for cp in mine + cps:
            cp.start()
        for u in range(n):
            half = hvs[u].shape[0]
            _remote(h_refs[u], o_refs[u].at[pl.ds((1 - c) * half, half), :], send, recv, u, (x, y, c)).wait_recv()
        for cp in cps:
            cp.wait_send()
        for cp in mine:
            cp.wait()

    outs = [jax.ShapeDtypeStruct((2 * h.shape[0], h.shape[1]), h.dtype) for h in hvs]
    return _comm_call(body, "join_halves", hvs, outs, n, n)


def gather_small(sg, name):
    rows, cols = sg.shape

    def body(s_ref, o_ref, send, recv, lsem):
        x, y, c = _pos()
        mine = pltpu.make_async_copy(s_ref, o_ref.at[4 * x + 2 * y + c], lsem)
        mine.start()
        peers = []
        for msk in range(1, N_DEV):
            px = 1 - x if msk & 4 else x
            py = 1 - y if msk & 2 else y
            pc = 1 - c if msk & 1 else c
            peers.append((px, py, pc))
        cps = [_remote(s_ref, o_ref.at[4 * x + 2 * y + c], send, recv, k, peer) for k, peer in enumerate(peers)]
        for cp in cps:
            cp.start()
        for k, (px, py, pc) in enumerate(peers):
            _remote(s_ref, o_ref.at[4 * px + 2 * py + pc], send, recv, k, (x, y, c)).wait_recv()
        for cp in cps:
            cp.wait_send()
        mine.wait()

    return pl.pallas_call(
        body, name=name,
        out_shape=jax.ShapeDtypeStruct((N_DEV, rows, cols), sg.dtype),
        in_specs=[ANY], out_specs=ANY,
        scratch_shapes=[pltpu.SemaphoreType.DMA((N_DEV - 1,)), pltpu.SemaphoreType.DMA((N_DEV - 1,)), pltpu.SemaphoreType.DMA],
    )(sg)


def sum_cores(gp, theirs, core, chip, name):
    nch, rows, cols = gp.shape
    half = rows // 2
    tr = _pick(half, (256, 176, 128, 64))
    nb = half // tr

    def kern(idx_ref, g_ref, t_ref, own_ref, all_ref):
        k = pl.program_id(1)
        s = g_ref[...] + t_ref[...]
        all_ref[...] = s.astype(BF16)

        @pl.when(k == idx_ref[1])
        def _():
            own_ref[...] = s

    grid_spec = pltpu.PrefetchScalarGridSpec(
        num_scalar_prefetch=1, grid=(nb, nch),
        in_specs=[pl.BlockSpec((None, tr, cols), lambda i, k, idx: (k, idx[0] * nb + i, 0)),
                  pl.BlockSpec((None, tr, cols), lambda i, k, idx: (k, i, 0))],
        out_specs=[pl.BlockSpec((tr, cols), lambda i, k, idx: (i, 0)),
                   pl.BlockSpec((None, tr, cols), lambda i, k, idx: (k, i, 0))])
    return pl.pallas_call(
        kern, name=name, grid_spec=grid_spec,
        out_shape=[jax.ShapeDtypeStruct((half, cols), F32), jax.ShapeDtypeStruct((nch, half, cols), BF16)],
        compiler_params=_params(("parallel", "arbitrary")),
    )(jnp.stack([core, chip]).astype(jnp.int32), gp, theirs)


def sum_chips(own, others, name):
    half, cols = own.shape
    tr = _pick(half, (256, 176, 128, 64))

    def kern(o_ref, a_ref, b_ref, c_ref, out_ref):
        out_ref[...] = ((o_ref[...] + a_ref[...].astype(F32)) + b_ref[...].astype(F32)) + c_ref[...].astype(F32)

    return pl.pallas_call(
        kern, name=name, grid=(half // tr,),
        in_specs=[pl.BlockSpec((tr, cols), lambda i: (i, 0))] +
                 [pl.BlockSpec((None, tr, cols), lambda i, j=j: (j, i, 0)) for j in range(N_CHIPS - 1)],
        out_specs=pl.BlockSpec((tr, cols), lambda i: (i, 0)),
        out_shape=jax.ShapeDtypeStruct((half, cols), F32),
        compiler_params=_params(("parallel",)),
    )(own, others, others, others)


def small_update(gath, w, m, v, name):
    def fn(*vs):
        g = vs[0]
        for t in vs[1:N_DEV]:
            g = g + t
        wv, mv, vv = vs[N_DEV:]
        m2 = ADAM_B1 * mv + (1.0 - ADAM_B1) * g
        v2 = ADAM_B2 * vv + (1.0 - ADAM_B2) * (g * g)
        m_hat = m2 / (1.0 - ADAM_B1 ** ADAM_STEP)
        v_hat = v2 / (1.0 - ADAM_B2 ** ADAM_STEP)
        return g, -ADAM_LR * (m_hat / (jnp.sqrt(v_hat) + ADAM_EPS) + ADAM_WD * wv), m2, v2

    c = w.shape[1]
    ins = [(gath[k], "row") for k in range(N_DEV)] + [(w, "row"), (m, "row"), (v, "row")]
    return rowwise(fn, ins, [(c, F32)] * 4, tr=w.shape[0] // 2, name=name)


def _step(ins):
    x, target = ins["x"][0], ins["loss_target"][0]
    core = lax.axis_index("c")
    chip = 2 * lax.axis_index("x") + lax.axis_index("y")
    units = [(n, l) for n in MATRICES for l in range(ins[n].shape[0])]

    def lane_pad(v):
        return jnp.pad(v, ((0, 0), (0, PACK_COLS - v.shape[1])))

    vec_rows = [ins["ssm_conv_w"][0], ins["ssm_conv_b"], lane_pad(ins["ssm_norm_gain"])]
    blk = jnp.concatenate(vec_rows + [jnp.zeros((SUBLANES - 6, PACK_COLS), F32)], axis=0)
    per_chip = gather_small(blk, "gather_vectors")[0::2]
    ngw = ins["ssm_norm_gain"].shape[1]
    full = {
        "ssm_conv_w": jnp.concatenate([per_chip[k, 0:4] for k in range(N_CHIPS)], axis=1)[None],
        "ssm_conv_b": jnp.concatenate([per_chip[k, 4:5] for k in range(N_CHIPS)], axis=1),
        "ssm_norm_gain": jnp.concatenate([per_chip[k, 5:6, :ngw] for k in range(N_CHIPS)], axis=1),
    }

    gathered = gather_weights([ins[n][l].astype(BF16) for n, l in units], "gather_weights")
    for n in MATRICES:
        full[n] = []
    for (n, l), g in zip(units, gathered):
        if n in COLUMN_BLOCKS:
            full[n].append(g)
        elif n == "ssm_w_in":
            full[n].append(jnp.concatenate([g[k] for k in range(N_CHIPS)], axis=1))
        else:
            full[n].append(g.reshape(-1, g.shape[-1]))
    for n in REPLICATED:
        full[n] = ins[n]

    loss, dx, grads = local_step(x, target, full)
    loss = lax.psum(loss, ALL_AXES)

    gps = []
    for n, l in units:
        g = grads[n][l]
        if n in COLUMN_BLOCKS:
            gps.append(g)
        elif n == "ssm_w_in":
            gps.append(jnp.stack(jnp.split(g, N_CHIPS, axis=1)))
        else:
            gps.append(g.reshape(N_CHIPS, -1, g.shape[-1]))
    theirs = swap_halves(gps)
    owns, alls = [], []
    for (n, l), g, t in zip(units, gps, theirs):
        own, every = sum_cores(g, t, core, chip, f"sum_cores_{n}_{l}")
        owns.append(own)
        alls.append(every)
    others = scatter_chunks(alls)
    ghalves = [sum_chips(o, ot, f"sum_chips_{n}_{l}") for (n, l), o, ot in zip(units, owns, others)]
    gshards = dict(zip(units, join_halves(ghalves)))

    small_shapes = [ins[n].shape for n in REPLICATED]
    vec_shapes = [grads[n].shape for n in EXACT]
    vec_pack = _pack([grads[n] for n in EXACT], F32, SUBLANES)
    gath = gather_small(jnp.concatenate([_pack([grads[n] for n in REPLICATED], F32, SUBLANES), vec_pack], axis=0),
                        "gather_small")
    packed = [jnp.concatenate([_pack([ins[pre + n] for n in REPLICATED], F32, SUBLANES), jnp.zeros_like(vec_pack)], axis=0)
              for pre in ("", "m_", "v_")]
    res = small_update(gath, *packed, name="small_update")
    nrep = res[0].shape[0] - vec_pack.shape[0]
    small = [dict(zip(REPLICATED, _unpack(r[:nrep].reshape(-1), small_shapes))) for r in res]
    vec_g = dict(zip(EXACT, _unpack(res[0][nrep:].reshape(-1), vec_shapes)))

    out_g, out_d, out_m, out_v = {}, {}, {}, {}
    for n in REPLICATED:
        out_g[n], out_d[n], out_m[n], out_v[n] = (s[n] for s in small)
    for n in SHARDED:
        shp = ins[n].shape
        if n in EXACT:
            g = lax.dynamic_slice_in_dim(vec_g[n], chip * shp[-1], shp[-1], axis=vec_g[n].ndim - 1)
        else:
            g = jnp.stack([gshards[(n, l)] for l in range(shp[0])])
        two = (math.prod(shp[:-1]), shp[-1])
        d2, m2, v2 = adamw(ins[n].reshape(two), g.reshape(two), ins["m_" + n].reshape(two),
                           ins["v_" + n].reshape(two), f"adamw_{n}")
        out_g[n], out_d[n], out_m[n], out_v[n] = g, d2.reshape(shp), m2.reshape(shp), v2.reshape(shp)
    return (loss, dx[None], *[out_g[n] for n in WEIGHTS], *[out_d[n] for n in WEIGHTS],
            *[out_m[n] for n in WEIGHTS], *[out_v[n] for n in WEIGHTS])


def kernel(x, mix_norm, ffn_norm, sb_w_qkv, sb_q_gain, sb_k_gain, sb_w_o, gm_w_in, gm_b_in, gm_v_gain, gm_w_s, gm_b_s, gm_w_out, ssm_w_in, ssm_conv_w, ssm_conv_b, ssm_dt_bias, ssm_a_log, ssm_d, ssm_norm_gain, ssm_w_out, ffn_w_gu, ffn_w_down, loss_target, m_mix_norm, m_ffn_norm, m_sb_w_qkv, m_sb_q_gain, m_sb_k_gain, m_sb_w_o, m_gm_w_in, m_gm_b_in, m_gm_v_gain, m_gm_w_s, m_gm_b_s, m_gm_w_out, m_ssm_w_in, m_ssm_conv_w, m_ssm_conv_b, m_ssm_dt_bias, m_ssm_a_log, m_ssm_d, m_ssm_norm_gain, m_ssm_w_out, m_ffn_w_gu, m_ffn_w_down, v_mix_norm, v_ffn_norm, v_sb_w_qkv, v_sb_q_gain, v_sb_k_gain, v_sb_w_o, v_gm_w_in, v_gm_b_in, v_gm_v_gain, v_gm_w_s, v_gm_b_s, v_gm_w_out, v_ssm_w_in, v_ssm_conv_w, v_ssm_conv_b, v_ssm_dt_bias, v_ssm_a_log, v_ssm_d, v_ssm_norm_gain, v_ssm_w_out, v_ffn_w_gu, v_ffn_w_down):
    return _step(dict(locals()))
```

```python
import functools
import math

import jax
import jax.numpy as jnp
from jax import lax
from jax.experimental import pallas as pl
from jax.experimental.pallas import tpu as pltpu

F32 = jnp.float32
BF16 = jnp.bfloat16
EPS = 1e-6
LANES = 128
SUBLANES = 8
VMEM_LIMIT = 56 * 1024 * 1024
HEAD = 64
CHUNK = 128
SB_TQ, SB_TK = 256, 256
SB_DEAD = -110.0
SB_UNSEEN = -1e30
ADAM_LR, ADAM_B1, ADAM_B2, ADAM_EPS, ADAM_WD, ADAM_STEP = 0.001, 0.9, 0.999, 1e-08, 0.01, 10
MESH_ID = pl.DeviceIdType.MESH
ALL_AXES = ("x", "y", "c")


def _params(sem):
    return pltpu.CompilerParams(dimension_semantics=sem, vmem_limit_bytes=VMEM_LIMIT)


def _pick(n, cands):
    for c in cands:
        if n % c == 0:
            return c
    return n


def _dot(a, b, dims=((1,), (0,))):
    return lax.dot_general(a, b, (dims, ((), ())), preferred_element_type=F32)


def _dot_nt(a, b):
    return _dot(a, b, ((1,), (1,)))


def _dot_tn(a, b):
    return _dot(a, b, ((0,), (0,)))


def _split2(x):
    hi = x.astype(BF16)
    lo = (x - hi.astype(F32)).astype(BF16)
    return hi, lo


def _dot_x2(x, m):
    hi, lo = _split2(x)
    return _dot(hi, m) + _dot(lo, m)


def _dot_x3_left(m, x):
    h1 = x.astype(BF16)
    r1 = x - h1.astype(F32)
    h2 = r1.astype(BF16)
    h3 = (r1 - h2.astype(F32)).astype(BF16)
    return _dot(m, h1) + _dot(m, h2) + _dot(m, h3)


def _sigmoid(x):
    return 1.0 / (1.0 + jnp.exp(-x))


def _softplus(x):
    return jnp.maximum(x, 0.0) + jnp.log(1.0 + jnp.exp(-jnp.abs(x)))


def _colsum(x):
    return jnp.sum(x, axis=0, keepdims=True)


def _rowsum(x):
    return jnp.sum(x, axis=1, keepdims=True)


def _iota2(shape, dim):
    return lax.broadcasted_iota(jnp.int32, shape, dim)


def mm(a, b, *, ta=False, tb=False, add=None, bias=None, b_chunks=False, out_chunks=False, name):
    if ta:
        kk, m = a.shape
    else:
        m, kk = a.shape
    nch, wide = 1, None
    if b_chunks:
        nch, rows_b, wide = b.shape
        kb, n = (rows_b, nch * wide) if not tb else (nch * wide, rows_b)
    elif tb:
        n, kb = b.shape
    else:
        kb, n = b.shape
    if out_chunks:
        nch, wide = N_CHIPS, n // N_CHIPS
    assert kk == kb, (a.shape, b.shape, ta, tb)
    tm = _pick(m, (512, 256, 128))
    tn = _pick(wide if (wide and not tb) or out_chunks else n, (512, 256, 1408, 128))
    tk = _pick(wide if (wide and tb) else kk, (1024, 1408, 512, 256, 128))
    nk = kk // tk
    dims = ((0 if ta else 1,), (1 if tb else 0,))
    has_add, has_bias = add is not None, bias is not None

    def kern(*refs):
        a_ref, b_ref = refs[0], refs[1]
        rest = list(refs[2:])
        add_ref = rest.pop(0) if has_add else None
        bias_ref = rest.pop(0) if has_bias else None
        o_ref, acc_ref = rest
        k = pl.program_id(2)

        @pl.when(k == 0)
        def _():
            acc_ref[...] = jnp.zeros_like(acc_ref)

        acc_ref[...] += _dot(a_ref[...].astype(BF16), b_ref[...].astype(BF16), dims)

        @pl.when(k == nk - 1)
        def _():
            r = acc_ref[...]
            if has_add:
                r = r + add_ref[...]
            if has_bias:
                r = r + bias_ref[...]
            o_ref[...] = r

    a_spec = pl.BlockSpec((tk, tm), lambda i, j, k: (k, i)) if ta else pl.BlockSpec((tm, tk), lambda i, j, k: (i, k))
    if b_chunks and tb:
        per = wide // tk
        b_spec = pl.BlockSpec((None, tn, tk), lambda i, j, k: (k // per, j, k % per))
    elif b_chunks:
        per = wide // tn
        b_spec = pl.BlockSpec((None, tk, tn), lambda i, j, k: (j // per, k, j % per))
    elif tb:
        b_spec = pl.BlockSpec((tn, tk), lambda i, j, k: (j, k))
    else:
        b_spec = pl.BlockSpec((tk, tn), lambda i, j, k: (k, j))
    if out_chunks:
        per_o = wide // tn
        out_spec = pl.BlockSpec((None, tm, tn), lambda i, j, k: (j // per_o, i, j % per_o))
        out_shape = jax.ShapeDtypeStruct((nch, m, wide), F32)
    else:
        out_spec = pl.BlockSpec((tm, tn), lambda i, j, k: (i, j))
        out_shape = jax.ShapeDtypeStruct((m, n), F32)
    in_specs, args = [a_spec, b_spec], [a, b]
    if has_add:
        in_specs.append(pl.BlockSpec((tm, tn), lambda i, j, k: (i, j)))
        args.append(add)
    if has_bias:
        in_specs.append(pl.BlockSpec((1, tn), lambda i, j, k: (0, j)))
        args.append(bias)
    return pl.pallas_call(
        kern,
        name=name,
        grid=(m // tm, n // tn, nk),
        in_specs=in_specs,
        out_specs=out_spec,
        out_shape=out_shape,
        scratch_shapes=[pltpu.VMEM((tm, tn), F32)],
        compiler_params=_params(("parallel", "parallel", "arbitrary")),
    )(*args)


def rowwise(fn, ins, outs, accs=(), *, tr, name):
    rows = [a for a, kind in ins if kind == "row"][0].shape[0]
    tr = min(tr, rows)
    assert rows % tr == 0 and tr % SUBLANES == 0, (rows, tr)
    n = rows // tr
    n_in, n_out = len(ins), len(outs)
    kinds = [kind for _, kind in ins]

    def kern(*refs):
        i = pl.program_id(0)
        vals = []
        for ref, kind in zip(refs[:n_in], kinds):
            v = ref[...]
            if kind == "prev":
                v = v * (i > 0).astype(v.dtype)
            elif kind == "next":
                v = v * (i < n - 1).astype(v.dtype)
            vals.append(v)
        res = fn(*vals)
        for ref, r in zip(refs[n_in:n_in + n_out], res[:n_out]):
            ref[...] = r.astype(ref.dtype)
        if accs:
            acc_refs = refs[n_in + n_out:]

            @pl.when(i == 0)
            def _():
                for ref in acc_refs:
                    ref[...] = jnp.zeros_like(ref)

            for ref, r in zip(acc_refs, res[n_out:]):
                ref[...] += r

    in_specs = []
    for a, kind in ins:
        if kind == "row":
            in_specs.append(pl.BlockSpec((tr, a.shape[1]), lambda i: (i, 0)))
        elif kind == "full":
            in_specs.append(pl.BlockSpec(a.shape, lambda i, nd=a.ndim: (0,) * nd))
        elif kind == "prev":
            in_specs.append(pl.BlockSpec((SUBLANES, a.shape[1]),
                                         lambda i: (jnp.maximum(i * (tr // SUBLANES) - 1, 0), 0)))
        else:
            in_specs.append(pl.BlockSpec((SUBLANES, a.shape[1]),
                                         lambda i: (jnp.minimum((i + 1) * (tr // SUBLANES), rows // SUBLANES - 1), 0)))
    out_specs = [pl.BlockSpec((tr, c), lambda i: (i, 0)) for c, _ in outs]
    out_specs += [pl.BlockSpec((r, c), lambda i: (0, 0)) for r, c in accs]
    out_shape = [jax.ShapeDtypeStruct((rows, c), dt) for c, dt in outs]
    out_shape += [jax.ShapeDtypeStruct((r, c), F32) for r, c in accs]
    res = pl.pallas_call(
        kern,
        name=name,
        grid=(n,),
        in_specs=in_specs,
        out_specs=out_specs,
        out_shape=out_shape,
        compiler_params=_params(("arbitrary",) if accs else ("parallel",)),
    )(*[a for a, _ in ins])
    return res


def rms_fwd(x, g, name):
    def fn(xv, gv):
        r = lax.rsqrt(jnp.mean(xv * xv, axis=1, keepdims=True) + EPS)
        return (xv * r * gv,)

    return rowwise(fn, [(x, "row"), (g, "full")], [(x.shape[1], BF16)], tr=512, name=name)[0]


def rms_bwd(x, g, dy, dres, name):
    def fn(xv, gv, dyv, drv):
        r = lax.rsqrt(jnp.mean(xv * xv, axis=1, keepdims=True) + EPS)
        xh = xv * r
        dyg = dyv * gv
        dx = drv + r * (dyg - xh * jnp.mean(dyg * xh, axis=1, keepdims=True))
        return dx, _colsum(dyv * xh)

    c = x.shape[1]
    return rowwise(fn, [(x, "row"), (g, "full"), (dy, "row"), (dres, "row")], [(c, F32)], [(1, c)], tr=256, name=name)


def swiglu_fwd(gu, name):
    hid = gu.shape[1] // 2

    def fn(v):
        g, u = v[:, :hid], v[:, hid:]
        return (g * _sigmoid(g) * u,)

    return rowwise(fn, [(gu, "row")], [(hid, BF16)], tr=256, name=name)[0]


def swiglu_bwd(gu, da, name):
    hid = gu.shape[1] // 2

    def fn(v, d):
        g, u = v[:, :hid], v[:, hid:]
        s = _sigmoid(g)
        dg = d * u * s * (1.0 + g * (1.0 - s))
        du = d * g * s
        return (jnp.concatenate([dg, du], axis=1),)

    return rowwise(fn, [(gu, "row"), (da, "row")], [(2 * hid, BF16)], tr=256, name=name)[0]


def loss_and_grad(y, t, name):
    d = y.shape[1]

    def fn(yv, tv):
        e = yv - tv
        part = jnp.sum(_colsum(e * e), axis=1, keepdims=True) * (0.5 / d)
        return e * (1.0 / d), jnp.broadcast_to(part, (SUBLANES, LANES))

    dy, acc = rowwise(fn, [(y, "row"), (t, "row")], [(d, F32)], [(SUBLANES, LANES)], tr=512, name=name)
    return acc[0, 0], dy


def adamw(w, g, m, v, name):
    def fn(wv, gv, mv, vv):
        m2 = ADAM_B1 * mv + (1.0 - ADAM_B1) * gv
        v2 = ADAM_B2 * vv + (1.0 - ADAM_B2) * (gv * gv)
        m_hat = m2 / (1.0 - ADAM_B1 ** ADAM_STEP)
        v_hat = v2 / (1.0 - ADAM_B2 ** ADAM_STEP)
        delta = -ADAM_LR * (m_hat / (jnp.sqrt(v_hat) + ADAM_EPS) + ADAM_WD * wv)
        return delta, m2, v2

    rows, c = w.shape
    tr = _pick(rows, (256, 128, 64, 32, 16, 8)) if rows % SUBLANES == 0 else rows
    if rows % SUBLANES:
        return _whole(fn, [w, g, m, v], [(w.shape, F32)] * 3, name=name)
    return rowwise(fn, [(w, "row"), (g, "row"), (m, "row"), (v, "row")], [(c, F32)] * 3, tr=tr, name=name)


def _whole(fn, ins, outs, *, name):
    n_in = len(ins)

    def kern(*refs):
        res = fn(*[r[...] for r in refs[:n_in]])
        for ref, r in zip(refs[n_in:], res):
            ref[...] = r.astype(ref.dtype)

    return pl.pallas_call(
        kern,
        name=name,
        out_shape=[jax.ShapeDtypeStruct(s, dt) for s, dt in outs],
        compiler_params=pltpu.CompilerParams(vmem_limit_bytes=VMEM_LIMIT),
    )(*ins)


def ffn_fwd(x, g, wgu, wdown, tag):
    h = rms_fwd(x, g, f"ffn_rms_{tag}")
    gu = mm(h, wgu, b_chunks=True, name=f"ffn_gu_{tag}")
    a = swiglu_fwd(gu, f"ffn_act_{tag}")
    xn = mm(a, wdown, add=x, name=f"ffn_down_{tag}")
    return xn, (x, h, gu, a)


def ffn_bwd(dxn, saved, g, wgu, wdown, tag):
    x, h, gu, a = saved
    da = mm(dxn, wdown, tb=True, name=f"ffn_da_{tag}")
    dwdown = mm(a, dxn, ta=True, name=f"ffn_dwdown_{tag}")
    dgu = swiglu_bwd(gu, da, f"ffn_dact_{tag}")
    dh = mm(dgu, wgu, tb=True, b_chunks=True, name=f"ffn_dh_{tag}")
    dwgu = mm(h, dgu, ta=True, out_chunks=True, name=f"ffn_dwgu_{tag}")
    dx, dg = rms_bwd(x, g, dh, dxn, f"ffn_drms_{tag}")
    return dx, dg, dwgu, dwdown


def _head_blockdiag(c):
    i = jnp.arange(c) // HEAD
    return (i[:, None] == i[None, :]).astype(BF16)


def qknorm_fwd(qkv, qg, kg, bd, name):
    d = qkv.shape[1] // 3
    scale = 1.0 / math.sqrt(HEAD)

    def fn(v, qgv, kgv, bdv):
        q, k, vv = v[:, :d], v[:, d:2 * d], v[:, 2 * d:]
        rq = lax.rsqrt(_dot_x2(q * q, bdv) * (1.0 / HEAD) + EPS)
        rk = lax.rsqrt(_dot_x2(k * k, bdv) * (1.0 / HEAD) + EPS)
        return q * rq * qgv * scale, k * rk * kgv, vv

    return rowwise(fn, [(qkv, "row"), (qg, "full"), (kg, "full"), (bd, "full")],
                   [(d, BF16), (d, BF16), (d, BF16)], tr=256, name=name)


def qknorm_bwd(qkv, dqs, dkn, dv, qg, kg, bd, name):
    d = qkv.shape[1] // 3
    scale = 1.0 / math.sqrt(HEAD)

    def one(xv, gv, dyv, bdv):
        r = lax.rsqrt(_dot_x2(xv * xv, bdv) * (1.0 / HEAD) + EPS)
        xh = xv * r
        dyg = dyv * gv
        dx = r * (dyg - xh * (_dot_x2(dyg * xh, bdv) * (1.0 / HEAD)))
        return dx, _colsum(dyv * xh)

    def fn(v, dqv, dkv, dvv, qgv, kgv, bdv):
        q, k = v[:, :d], v[:, d:2 * d]
        dq, dqg = one(q, qgv, dqv * scale, bdv)
        dk, dkg = one(k, kgv, dkv, bdv)
        return jnp.concatenate([dq, dk, dvv], axis=1), dqg, dkg

    return rowwise(fn, [(qkv, "row"), (dqs, "row"), (dkn, "row"), (dv, "row"), (qg, "full"), (kg, "full"), (bd, "full")],
                   [(3 * d, BF16)], [(1, d), (1, d)], tr=256, name=name)


def _sb_tile(qh, k, mask, tri_gt):
    z = _dot_nt(qh, k)
    sp = jnp.log(1.0 + jnp.exp(-jnp.abs(z)))
    lb = jnp.minimum(z, 0.0) - sp
    l1 = jnp.where(mask, lb - z, 0.0)
    suf = _dot_x2(l1, tri_gt)
    return lb, l1, suf


def _sb_setup(tq, tk):
    row, col = _iota2((tq, tk), 0), _iota2((tq, tk), 1)
    lane = _iota2((1, LANES), 1)
    halves = [(lane < HEAD).astype(BF16), (lane >= HEAD).astype(BF16)]
    lane_q = _iota2((tq, LANES), 1) + jnp.minimum(_iota2((tq, LANES), 0), 0)
    return row, col, halves, lane_q


def sb_attn_fwd(qs, kn, vb, name):
    s, d = qs.shape
    tq, tk = min(SB_TQ, s), min(SB_TK, s)
    nq = s // tq
    assert s // tk <= LANES and s % tq == 0 and s % tk == 0

    def kern(q_ref, k_ref, v_ref, o_ref, rs_ref):
        i = pl.program_id(1)
        row, col, halves, lane_q = _sb_setup(tq, tk)
        tri_gt = (_iota2((tk, tk), 0) > _iota2((tk, tk), 1)).astype(BF16)
        q = q_ref[...]
        qh = [q * hm for hm in halves]
        o_ref[...] = jnp.zeros_like(o_ref)
        rs_ref[...] = jnp.full(rs_ref.shape, SB_UNSEEN, F32)
        nkb = (i + 1) * (tq // tk)

        def more(st):
            return (st[0] < nkb) & (st[1] > SB_DEAD)

        def step(st):
            n, r = st[0], list(st[2:])
            kb = nkb - 1 - n
            ks = pl.multiple_of(kb * tk, tk)
            k = k_ref[pl.ds(ks, tk), :]
            v = v_ref[pl.ds(ks, tk), :]
            mask = col < row + (i * tq - kb * tk)
            at_kb = lane_q == kb
            for hh in range(2):
                lb, l1, suf = _sb_tile(qh[hh], k, mask, tri_gt)
                w = jnp.where(mask, jnp.exp(lb + suf + r[hh]), 0.0)
                o_ref[...] += _dot(w.astype(BF16), v * halves[hh])
                rs_ref[hh] = jnp.where(at_kb, r[hh], rs_ref[hh])
                r[hh] = r[hh] + _rowsum(l1)
            return (n + 1, jnp.maximum(jnp.max(r[0]), jnp.max(r[1])), r[0], r[1])

        z1 = jnp.zeros((tq, 1), F32)
        lax.while_loop(more, step, (jnp.int32(0), jnp.float32(0.0), z1, z1))

    nh2 = d // LANES
    return pl.pallas_call(
        kern,
        name=name,
        grid=(nh2, nq),
        in_specs=[pl.BlockSpec((tq, LANES), lambda h, i: (i, h)),
                  pl.BlockSpec((s, LANES), lambda h, i: (0, h)),
                  pl.BlockSpec((s, LANES), lambda h, i: (0, h))],
        out_specs=[pl.BlockSpec((tq, LANES), lambda h, i: (i, h)),
                   pl.BlockSpec((None, 2, tq, LANES), lambda h, i: (h, 0, i, 0))],
        out_shape=[jax.ShapeDtypeStruct((s, d), F32), jax.ShapeDtypeStruct((nh2, 2, s, LANES), F32)],
        compiler_params=_params(("parallel", "arbitrary")),
    )(qs, kn, vb)


def sb_attn_bwd(qs, kn, vb, rsave, do, name):
    s, d = qs.shape
    tq, tk = min(SB_TQ, s), min(SB_TK, s)
    nq = s // tq

    def kern(q_ref, k_ref, v_ref, rs_ref, do_ref, dq_ref, dk_ref, dv_ref):
        i = pl.program_id(1)

        @pl.when(i == 0)
        def _():
            dk_ref[...] = jnp.zeros_like(dk_ref)
            dv_ref[...] = jnp.zeros_like(dv_ref)

        row, col, halves, lane_q = _sb_setup(tq, tk)
        tri_gt = (_iota2((tk, tk), 0) > _iota2((tk, tk), 1)).astype(BF16)
        tri_lt = (_iota2((tk, tk), 0) < _iota2((tk, tk), 1)).astype(BF16)
        q = q_ref[...]
        qh = [q * hm for hm in halves]
        dov = do_ref[...].astype(BF16)
        doh = [dov * hm for hm in halves]
        dq_ref[...] = jnp.zeros_like(dq_ref)
        nkb = (i + 1) * (tq // tk)
        top = jnp.maximum(jnp.max(rs_ref[0], axis=0, keepdims=True), jnp.max(rs_ref[1], axis=0, keepdims=True))
        dead = (top <= SB_DEAD) & (_iota2((1, LANES), 1) < nkb)
        kstart = jnp.minimum(jnp.sum(dead.astype(F32)).astype(jnp.int32), nkb)

        def step(kb, ep):
            ep = list(ep)
            ks = pl.multiple_of(kb * tk, tk)
            k = k_ref[pl.ds(ks, tk), :]
            v = v_ref[pl.ds(ks, tk), :]
            mask = col < row + (i * tq - kb * tk)
            at_kb = lane_q == kb
            for hh in range(2):
                lb, l1, suf = _sb_tile(qh[hh], k, mask, tri_gt)
                r = _rowsum(jnp.where(at_kb, rs_ref[hh], 0.0))
                w = jnp.where(mask, jnp.exp(lb + suf + r), 0.0)
                e = _dot_nt(doh[hh], v) * w
                pe = ep[hh] + _dot_x2(e, tri_lt)
                beta = jnp.exp(lb)
                dz = jnp.where(mask, e * (1.0 - beta) - pe * beta, 0.0).astype(BF16)
                dq_ref[...] += _dot(dz, k * halves[hh])
                dk_ref[pl.ds(ks, tk), :] += _dot_tn(dz, qh[hh])
                dv_ref[pl.ds(ks, tk), :] += _dot_tn(w.astype(BF16), doh[hh])
                ep[hh] = ep[hh] + _rowsum(e)
            return tuple(ep)

        z1 = jnp.zeros((tq, 1), F32)
        lax.fori_loop(kstart, nkb, step, (z1, z1))

    nh2 = d // LANES
    return pl.pallas_call(
        kern,
        name=name,
        grid=(nh2, nq),
        in_specs=[pl.BlockSpec((tq, LANES), lambda h, i: (i, h)),
                  pl.BlockSpec((s, LANES), lambda h, i: (0, h)),
                  pl.BlockSpec((s, LANES), lambda h, i: (0, h)),
                  pl.BlockSpec((None, 2, tq, LANES), lambda h, i: (h, 0, i, 0)),
                  pl.BlockSpec((tq, LANES), lambda h, i: (i, h))],
        out_specs=[pl.BlockSpec((tq, LANES), lambda h, i: (i, h)),
                   pl.BlockSpec((s, LANES), lambda h, i: (0, h)),
                   pl.BlockSpec((s, LANES), lambda h, i: (0, h))],
        out_shape=[jax.ShapeDtypeStruct((s, d), F32)] * 3,
        compiler_params=_params(("parallel", "arbitrary")),
    )(qs, kn, vb, rsave, do)


def sb_fwd(x, g, wqkv, qg, kg, wo, bd, tag):
    h = rms_fwd(x, g, f"sb_rms_{tag}")
    qkv = mm(h, wqkv, b_chunks=True, name=f"sb_qkv_{tag}")
    qs, kn, vb = qknorm_fwd(qkv, qg, kg, bd, f"sb_qknorm_{tag}")
    o, rsave = sb_attn_fwd(qs, kn, vb, f"sb_attn_{tag}")
    xn = mm(o, wo, add=x, name=f"sb_out_{tag}")
    return xn, (x, h, qkv, qs, kn, vb, rsave, o)


def sb_bwd(dxn, saved, g, wqkv, qg, kg, wo, bd, tag):
    x, h, qkv, qs, kn, vb, rsave, o = saved
    do = mm(dxn, wo, tb=True, name=f"sb_do_{tag}")
    dwo = mm(o, dxn, ta=True, name=f"sb_dwo_{tag}")
    dqs, dkn, dv = sb_attn_bwd(qs, kn, vb, rsave, do, f"sb_dattn_{tag}")
    dqkv, dqg, dkg = qknorm_bwd(qkv, dqs, dkn, dv, qg, kg, bd, f"sb_dqknorm_{tag}")
    dh = mm(dqkv, wqkv, tb=True, b_chunks=True, name=f"sb_dh_{tag}")
    dwqkv = mm(h, dqkv, ta=True, out_chunks=True, name=f"sb_dwqkv_{tag}")
    dx, dg = rms_bwd(x, g, dh, dxn, f"sb_drms_{tag}")
    nh = dqg.shape[1] // HEAD
    return dx, dg, dwqkv, dqg.reshape(nh, HEAD).sum(0), dkg.reshape(nh, HEAD).sum(0), dwo


def _gelu(x):
    return 0.5 * x * (1.0 + lax.erf(x * (1.0 / math.sqrt(2.0))))


def _gelu_grad(x):
    return 0.5 * (1.0 + lax.erf(x * (1.0 / math.sqrt(2.0)))) + x * jnp.exp(-0.5 * x * x) * (1.0 / math.sqrt(2.0 * math.pi))


def gm_act_fwd(pre, vg, name):
    half = pre.shape[1] // 2

    def fn(p, vgv):
        u = _gelu(p[:, :half])
        v = _gelu(p[:, half:])
        r = lax.rsqrt(jnp.mean(v * v, axis=1, keepdims=True) + EPS)
        return u, v * r * vgv

    return rowwise(fn, [(pre, "row"), (vg, "full")], [(half, F32), (half, BF16)], tr=256, name=name)


def gm_act_bwd(pre, du, dvn, vg, name):
    half = pre.shape[1] // 2

    def fn(p, duv, dvnv, vgv):
        pu, pv = p[:, :half], p[:, half:]
        v = _gelu(pv)
        r = lax.rsqrt(jnp.mean(v * v, axis=1, keepdims=True) + EPS)
        vh = v * r
        dyg = dvnv * vgv
        dv = r * (dyg - vh * jnp.mean(dyg * vh, axis=1, keepdims=True))
        dpre = jnp.concatenate([duv * _gelu_grad(pu), dv * _gelu_grad(pv)], axis=1)
        return dpre, _colsum(dvnv * vh), _colsum(dpre)

    return rowwise(fn, [(pre, "row"), (du, "row"), (dvn, "row"), (vg, "full")],
                   [(2 * half, BF16)], [(1, half), (1, 2 * half)], tr=256, name=name)


def gm_spatial_fwd(u, vn, wc, bst, name):
    s, c = u.shape
    t = CHUNK
    ng = c // LANES

    def kern(u_ref, v_ref, w_ref, b_ref, o_ref):
        for g in range(ng):
            sl = slice(g * LANES, (g + 1) * LANES)
            mixed = _dot(w_ref[g], v_ref[:, sl]) + b_ref[:, sl]
            o_ref[:, sl] = (u_ref[:, sl] * mixed).astype(BF16)

    return pl.pallas_call(
        kern,
        name=name,
        grid=(s // t,),
        in_specs=[pl.BlockSpec((t, c), lambda i: (i, 0)), pl.BlockSpec((t, c), lambda i: (i, 0)),
                  pl.BlockSpec(wc.shape, lambda i: (0, 0, 0)), pl.BlockSpec(bst.shape, lambda i: (0, 0))],
        out_specs=pl.BlockSpec((t, c), lambda i: (i, 0)),
        out_shape=jax.ShapeDtypeStruct((s, c), BF16),
        compiler_params=_params(("parallel",)),
    )(u, vn, wc, bst)


def gm_spatial_bwd(dgate, u, vn, wc, bst, name):
    s, c = u.shape
    t = CHUNK
    ng = c // LANES

    def kern(dg_ref, u_ref, v_ref, w_ref, b_ref, du_ref, dv_ref, dw_ref, db_ref):
        i = pl.program_id(0)

        @pl.when(i == 0)
        def _():
            dw_ref[...] = jnp.zeros_like(dw_ref)
            db_ref[...] = jnp.zeros_like(db_ref)

        for g in range(ng):
            sl = slice(g * LANES, (g + 1) * LANES)
            vg = v_ref[:, sl]
            dgv = dg_ref[:, sl]
            mixed = _dot(w_ref[g], vg) + b_ref[:, sl]
            du_ref[:, sl] = dgv * mixed
            dmix = dgv * u_ref[:, sl]
            dmb = dmix.astype(BF16)
            dv_ref[:, sl] = _dot_tn(w_ref[g], dmb)
            dw_ref[g] += _dot_nt(dmb, vg)
            db_ref[:, sl] += dmix

    return pl.pallas_call(
        kern,
        name=name,
        grid=(s // t,),
        in_specs=[pl.BlockSpec((t, c), lambda i: (i, 0))] * 3 +
                 [pl.BlockSpec(wc.shape, lambda i: (0, 0, 0)), pl.BlockSpec(bst.shape, lambda i: (0, 0))],
        out_specs=[pl.BlockSpec((t, c), lambda i: (i, 0)), pl.BlockSpec((t, c), lambda i: (i, 0)),
                   pl.BlockSpec(wc.shape, lambda i: (0, 0, 0)), pl.BlockSpec(bst.shape, lambda i: (0, 0))],
        out_shape=[jax.ShapeDtypeStruct((s, c), F32), jax.ShapeDtypeStruct((s, c), F32),
                   jax.ShapeDtypeStruct(wc.shape, F32), jax.ShapeDtypeStruct(bst.shape, F32)],
        compiler_params=_params(("arbitrary",)),
    )(dgate, u, vn, wc, bst)


def gm_fwd(x, g, w_in, b_in, vg, wc, bst, w_out, tag):
    h = rms_fwd(x, g, f"gm_rms_{tag}")
    pre = mm(h, w_in, bias=b_in, b_chunks=True, name=f"gm_in_{tag}")
    u, vn = gm_act_fwd(pre, vg, f"gm_act_{tag}")
    gate = gm_spatial_fwd(u, vn, wc, bst, f"gm_spatial_{tag}")
    xn = mm(gate, w_out, add=x, name=f"gm_out_{tag}")
    return xn, (x, h, pre, u, vn, gate)


def gm_bwd(dxn, saved, g, w_in, vg, wc, bst, w_out, tag):
    x, h, pre, u, vn, gate = saved
    dgate = mm(dxn, w_out, tb=True, name=f"gm_dgate_{tag}")
    dwout = mm(gate, dxn, ta=True, name=f"gm_dwout_{tag}")
    du, dvn, dws, dbst = gm_spatial_bwd(dgate, u, vn, wc, bst, f"gm_dspatial_{tag}")
    dpre, dvg, dbin = gm_act_bwd(pre, du, dvn, vg, f"gm_dact_{tag}")
    dh = mm(dpre, w_in, tb=True, b_chunks=True, name=f"gm_dh_{tag}")
    dwin = mm(h, dpre, ta=True, out_chunks=True, name=f"gm_dwin_{tag}")
    dx, dg = rms_bwd(x, g, dh, dxn, f"gm_drms_{tag}")
    ng = wc.shape[0]
    dws = jnp.where(jnp.tril(jnp.ones((CHUNK, CHUNK), bool)), dws, 0.0)
    dbs = dbst.reshape(CHUNK, ng, LANES).sum(-1).T
    return dx, dg, dwin, dbin, dvg, dws, dbs, dwout


def _conv_taps(xv, prev):
    cat = jnp.concatenate([prev, xv], axis=0)
    return [pltpu.roll(cat, sh, 0)[SUBLANES:] for sh in (3, 2, 1)] + [xv]


def conv_fwd(xbc, ws, b, d_inner, name):
    c = xbc.shape[1]
    nst = (c - d_inner) // 2

    def fn(xv, prev, w0, w1, w2, w3, bv):
        taps = _conv_taps(xv, prev)
        pre = bv + w0 * taps[0] + w1 * taps[1] + w2 * taps[2] + w3 * taps[3]
        out = pre * _sigmoid(pre)
        return out[:, :d_inner], out[:, d_inner:d_inner + nst], out[:, d_inner + nst:]

    return rowwise(fn, [(xbc, "row"), (xbc, "prev")] + [(w, "full") for w in ws] + [(b, "full")],
                   [(d_inner, F32), (nst, F32), (nst, F32)], tr=256, name=name)


def conv_bwd_pre(xbc, ws, b, dxs_a, dxs_b, db_m, dc_m, name):
    c = xbc.shape[1]

    def fn(xv, prev, w0, w1, w2, w3, bv, da, db2, dbm, dcm):
        taps = _conv_taps(xv, prev)
        pre = bv + w0 * taps[0] + w1 * taps[1] + w2 * taps[2] + w3 * taps[3]
        sg = _sigmoid(pre)
        dout = jnp.concatenate([da + db2, dbm, dcm], axis=1)
        dpre = dout * sg * (1.0 + pre * (1.0 - sg))
        return (dpre,) + tuple(_colsum(dpre * tp) for tp in taps) + (_colsum(dpre),)

    return rowwise(fn, [(xbc, "row"), (xbc, "prev")] + [(w, "full") for w in ws] +
                   [(b, "full"), (dxs_a, "row"), (dxs_b, "row"), (db_m, "row"), (dc_m, "row")],
                   [(c, F32)], [(1, c)] * 5, tr=256, name=name)


def conv_bwd_in(dpre, ws, name):
    c = dpre.shape[1]

    def fn(dv, nxt, w0, w1, w2, w3):
        cat = jnp.concatenate([dv, nxt], axis=0)
        n = cat.shape[0]
        up = [pltpu.roll(cat, n - sh, 0)[:dv.shape[0]] for sh in (1, 2, 3)]
        return (w3 * dv + w2 * up[0] + w1 * up[1] + w0 * up[2],)

    return rowwise(fn, [(dpre, "row"), (dpre, "next")] + [(w, "full") for w in ws], [(c, BF16)], tr=256, name=name)[0]


def ssd_pre(dtr, bias, alog, name):
    def fn(d, bv, al, tri):
        dt = _softplus(d + bv)
        a = dt * (-jnp.exp(al))
        return dt, _dot_x3_left(tri, a)

    tri = jnp.tril(jnp.ones((CHUNK, CHUNK), BF16))
    return rowwise(fn, [(dtr, "row"), (bias, "full"), (alog, "full"), (tri, "full")],
                   [(LANES, F32), (LANES, F32)], tr=CHUNK, name=name)


def _ssd_layouts(v, ngroups, hpg):
    s = v.shape[0]
    col = v[:, :ngroups * hpg].T.reshape(ngroups, hpg, s, 1)
    return jnp.broadcast_to(col, (ngroups, hpg, s, LANES))


def _ssd_rowform(acum, ngroups, hpg):
    s = acum.shape[0]
    nc = s // CHUNK
    a = acum[:, :ngroups * hpg].reshape(nc, CHUNK, ngroups, hpg).transpose(2, 0, 3, 1)
    last = jnp.broadcast_to(a[..., CHUNK - 1:], a.shape)
    return jnp.concatenate([a, last], axis=2)


def ssd_chunk_fwd(xs, bm, cm, col_a, col_dt, rowf, name):
    s, d_inner = xs.shape
    ln = CHUNK
    nc = s // ln
    ng, hpg = col_a.shape[0], col_a.shape[1]
    gw = d_inner // ng
    assert gw == hpg * HEAD and gw % LANES == 0 and bm.shape[1] == ng * LANES

    def kern(x_ref, b_ref, c_ref, ca_ref, cd_ref, rf_ref, y_ref, hp_ref, h_scr):
        c = pl.program_id(1)

        @pl.when(c == 0)
        def _():
            h_scr[...] = jnp.zeros_like(h_scr)

        bb = b_ref[...].astype(BF16)
        cbf = c_ref[...].astype(BF16)
        cb = _dot_nt(cbf, bb)
        causal = _iota2((ln, ln), 0) >= _iota2((ln, ln), 1)
        lane = _iota2((1, LANES), 1)
        ys = [jnp.zeros((ln, LANES), F32) for _ in range(gw // LANES)]
        for r in range(hpg):
            j, hf = divmod(r, LANES // HEAD)
            mh = ((lane >= HEAD * hf) & (lane < HEAD * (hf + 1))).astype(F32)
            ac = ca_ref[r]
            ar = rf_ref[pl.ds(r, 1), :]
            aend = rf_ref[pl.ds(4 + r, 1), :]
            dm = jnp.exp(jnp.minimum(ac - ar, 0.0))
            m = jnp.where(causal, cb * dm, 0.0).astype(BF16)
            xdt = x_ref[:, j * LANES:(j + 1) * LANES] * cd_ref[r] * mh
            h = h_scr[r]
            hp_ref[r] = h
            ys[j] = ys[j] + _dot(m, xdt.astype(BF16)) + _dot_nt(cbf, h.astype(BF16)) * jnp.exp(ac)
            dte = jnp.exp(aend - ac)
            h_scr[r] = jnp.exp(aend) * h + _dot_tn((xdt * dte).astype(BF16), bb)
        for j in range(gw // LANES):
            y_ref[:, j * LANES:(j + 1) * LANES] = ys[j]

    return pl.pallas_call(
        kern,
        name=name,
        grid=(ng, nc),
        in_specs=[pl.BlockSpec((ln, gw), lambda g, c: (c, g)),
                  pl.BlockSpec((ln, LANES), lambda g, c: (c, g)),
                  pl.BlockSpec((ln, LANES), lambda g, c: (c, g)),
                  pl.BlockSpec((None, hpg, ln, LANES), lambda g, c: (g, 0, c, 0)),
                  pl.BlockSpec((None, hpg, ln, LANES), lambda g, c: (g, 0, c, 0)),
                  pl.BlockSpec((None, None, 8, LANES), lambda g, c: (g, c, 0, 0))],
        out_specs=[pl.BlockSpec((ln, gw), lambda g, c: (c, g)),
                   pl.BlockSpec((None, None, hpg, LANES, LANES), lambda g, c: (g, c, 0, 0, 0))],
        out_shape=[jax.ShapeDtypeStruct((s, d_inner), F32),
                   jax.ShapeDtypeStruct((ng, nc, hpg, LANES, LANES), F32)],
        scratch_shapes=[pltpu.VMEM((hpg, LANES, LANES), F32)],
        compiler_params=_params(("parallel", "arbitrary")),
    )(xs, bm, cm, col_a, col_dt, rowf)


def ssd_chunk_bwd(xs, bm, cm, col_a, col_dt, rowf, hprev, dy, name):
    s, d_inner = xs.shape
    ln = CHUNK
    nc = s // ln
    ng, hpg = col_a.shape[0], col_a.shape[1]
    gw = d_inner // ng

    def kern(x_ref, b_ref, c_ref, ca_ref, cd_ref, rf_ref, hp_ref, dy_ref,
             dx_ref, db_ref, dc_ref, ddt_ref, da_ref, dh_scr):
        c = pl.program_id(1)

        @pl.when(c == 0)
        def _():
            dh_scr[...] = jnp.zeros_like(dh_scr)

        bb = b_ref[...].astype(BF16)
        cbf = c_ref[...].astype(BF16)
        cb = _dot_nt(cbf, bb)
        row, col = _iota2((ln, ln), 0), _iota2((ln, ln), 1)
        causal = row >= col
        tri_ge = (col >= row).astype(BF16)
        ones = jnp.ones((ln, LANES), BF16)
        lane = _iota2((1, LANES), 1)
        last_row = (_iota2((ln, 1), 0) == ln - 1).astype(F32)
        dcb = jnp.zeros((ln, ln), F32)
        d_b = jnp.zeros((ln, LANES), F32)
        d_c = jnp.zeros((ln, LANES), F32)
        dxs = [jnp.zeros((ln, LANES), F32) for _ in range(gw // LANES)]
        for r in range(hpg):
            j, hf = divmod(r, LANES // HEAD)
            mh = ((lane >= HEAD * hf) & (lane < HEAD * (hf + 1))).astype(F32)
            ac = ca_ref[r]
            dt = cd_ref[r]
            ar = rf_ref[pl.ds(r, 1), :]
            aend = rf_ref[pl.ds(4 + r, 1), :]
            dm = jnp.where(causal, jnp.exp(jnp.minimum(ac - ar, 0.0)), 0.0)
            m = cb * dm
            mb = m.astype(BF16)
            xp = x_ref[:, j * LANES:(j + 1) * LANES]
            xdt = xp * dt * mh
            xdtb = xdt.astype(BF16)
            dyp = dy_ref[:, j * LANES:(j + 1) * LANES] * mh
            dypb = dyp.astype(BF16)
            h = hp_ref[r]
            hb = h.astype(BF16)
            dh = dh_scr[r]
            dhb = dh.astype(BF16)
            e_in = jnp.exp(ac)
            dte = jnp.exp(aend - ac)
            eend = jnp.exp(aend)
            d_m = _dot_nt(dypb, xdtb)
            dcb = dcb + d_m * dm
            gm = d_m * m
            yoff_pre = _dot_nt(cbf, hb)
            bdh = _dot_nt(bb, dhb)
            dxdt = _dot_tn(mb, dypb) + bdh * dte
            t1 = _rowsum(xdt * bdh) * dte
            gh, gl = _split2(gm)
            dacum = (_rowsum(gm) - (_dot_tn(gh, ones) + _dot_tn(gl, ones))
                     + _rowsum(dyp * yoff_pre) * e_in - t1)
            end_term = _colsum(t1) + eend * jnp.sum(_colsum(dh * h), axis=1, keepdims=True)
            dacum = dacum + last_row * end_term
            da_ref[r] = _dot_x3_left(tri_ge, dacum)
            ddt_ref[r] = jnp.broadcast_to(_rowsum(dxdt * xp), (ln, LANES))
            dxs[j] = dxs[j] + dxdt * dt
            d_b = d_b + _dot((xdt * dte).astype(BF16), dhb)
            dye = (dyp * e_in).astype(BF16)
            d_c = d_c + _dot(dye, hb)
            dh_scr[r] = eend * dh + _dot_tn(dye, cbf)
        dcbb = dcb.astype(BF16)
        dc_ref[...] = d_c + _dot(dcbb, bb)
        db_ref[...] = d_b + _dot_tn(dcbb, cbf)
        for j in range(gw // LANES):
            dx_ref[:, j * LANES:(j + 1) * LANES] = dxs[j]

    rev = nc - 1
    colspec = pl.BlockSpec((None, hpg, ln, LANES), lambda g, c: (g, 0, rev - c, 0))
    return pl.pallas_call(
        kern,
        name=name,
        grid=(ng, nc),
        in_specs=[pl.BlockSpec((ln, gw), lambda g, c: (rev - c, g)),
                  pl.BlockSpec((ln, LANES), lambda g, c: (rev - c, g)),
                  pl.BlockSpec((ln, LANES), lambda g, c: (rev - c, g)),
                  colspec, colspec,
                  pl.BlockSpec((None, None, 8, LANES), lambda g, c: (g, rev - c, 0, 0)),
                  pl.BlockSpec((None, None, hpg, LANES, LANES), lambda g, c: (g, rev - c, 0, 0, 0)),
                  pl.BlockSpec((ln, gw), lambda g, c: (rev - c, g))],
        out_specs=[pl.BlockSpec((ln, gw), lambda g, c: (rev - c, g)),
                   pl.BlockSpec((ln, LANES), lambda g, c: (rev - c, g)),
                   pl.BlockSpec((ln, LANES), lambda g, c: (rev - c, g)),
                   colspec, colspec],
        out_shape=[jax.ShapeDtypeStruct((s, d_inner), F32),
                   jax.ShapeDtypeStruct(bm.shape, F32), jax.ShapeDtypeStruct(cm.shape, F32),
                   jax.ShapeDtypeStruct(col_a.shape, F32), jax.ShapeDtypeStruct(col_a.shape, F32)],
        scratch_shapes=[pltpu.VMEM((hpg, LANES, LANES), F32)],
        compiler_params=_params(("parallel", "arbitrary")),
    )(xs, bm, cm, col_a, col_dt, rowf, hprev, dy)


def gnorm_fwd(y, xs, z, dexp, gain, ngroups, name):
    c = y.shape[1]
    gw = c // ngroups

    def fn(yv, xv, zv, dv, gv):
        yg = (yv + xv * dv) * (zv * _sigmoid(zv))
        outs = []
        for k in range(ngroups):
            t = yg[:, k * gw:(k + 1) * gw]
            outs.append(t * lax.rsqrt(jnp.mean(t * t, axis=1, keepdims=True) + EPS))
        return (jnp.concatenate(outs, axis=1) * gv,)

    return rowwise(fn, [(y, "row"), (xs, "row"), (z, "row"), (dexp, "full"), (gain, "full")], [(c, BF16)], tr=256, name=name)[0]


def gnorm_bwd(dn, y, xs, z, dexp, gain, ngroups, name):
    c = y.shape[1]
    gw = c // ngroups

    def fn(dnv, yv, xv, zv, dv, gv):
        yd = yv + xv * dv
        sg = _sigmoid(zv)
        sz = zv * sg
        yg = yd * sz
        dng = dnv * gv
        dyg, yh = [], []
        for k in range(ngroups):
            sl = slice(k * gw, (k + 1) * gw)
            t = yg[:, sl]
            r = lax.rsqrt(jnp.mean(t * t, axis=1, keepdims=True) + EPS)
            th = t * r
            dyg.append(r * (dng[:, sl] - th * jnp.mean(dng[:, sl] * th, axis=1, keepdims=True)))
            yh.append(th)
        dyg = jnp.concatenate(dyg, axis=1)
        yh = jnp.concatenate(yh, axis=1)
        dyd = dyg * sz
        dz = dyg * yd * (sg * (1.0 + zv * (1.0 - sg)))
        return dyd, dyd * dv, dz, _colsum(dyd * xv), _colsum(dnv * yh)

    return rowwise(fn, [(dn, "row"), (y, "row"), (xs, "row"), (z, "row"), (dexp, "full"), (gain, "full")],
                   [(c, F32), (c, F32), (c, BF16)], [(1, c), (1, c)], tr=256, name=name)


def ssd_post(ddt, da, dt, dtr, bias, alog, name):
    def fn(ddtv, dav, dtv, dtrv, bv, al):
        a_neg = -jnp.exp(al)
        ddtr = (ddtv + dav * a_neg) * _sigmoid(dtrv + bv)
        return ddtr, _colsum(ddtr), _colsum(dav * dtv) * a_neg

    return rowwise(fn, [(ddt, "row"), (da, "row"), (dt, "row"), (dtr, "row"), (bias, "full"), (alog, "full")],
                   [(LANES, BF16)], [(1, LANES), (1, LANES)], tr=512, name=name)


def _from_colform(v, s):
    ng, hpg = v.shape[0], v.shape[1]
    flat = v[..., 0].reshape(ng * hpg, s).T
    return jnp.pad(flat, ((0, 0), (0, LANES - ng * hpg)))


def ssm_fwd(x, g, p, tag):
    ng, hpg, d_inner = p["ng"], p["hpg"], p["d_inner"]
    h = rms_fwd(x, g, f"ssm_rms_{tag}")
    z = mm(h, p["w_z"], name=f"ssm_inz_{tag}")
    xbc = mm(h, p["w_xbc"], name=f"ssm_inx_{tag}")
    dtr = mm(h, p["w_dt"], name=f"ssm_indt_{tag}")
    xs, bm, cm = conv_fwd(xbc, p["conv_w"], p["conv_b"], d_inner, f"ssm_conv_{tag}")
    dt, acum = ssd_pre(dtr, p["dt_bias"], p["a_log"], f"ssm_pre_{tag}")
    col_a, col_dt = _ssd_layouts(acum, ng, hpg), _ssd_layouts(dt, ng, hpg)
    rowf = _ssd_rowform(acum, ng, hpg)
    y, hprev = ssd_chunk_fwd(xs, bm, cm, col_a, col_dt, rowf, f"ssm_scan_{tag}")
    n = gnorm_fwd(y, xs, z, p["d_exp"], p["norm_gain"], ng, f"ssm_gnorm_{tag}")
    xn = mm(n, p["w_out"], add=x, name=f"ssm_out_{tag}")
    return xn, (x, h, z, xbc, dtr, xs, bm, cm, dt, col_a, col_dt, rowf, y, hprev, n)


def ssm_bwd(dxn, saved, g, p, tag):
    x, h, z, xbc, dtr, xs, bm, cm, dt, col_a, col_dt, rowf, y, hprev, n = saved
    ng, hpg, d_inner = p["ng"], p["hpg"], p["d_inner"]
    s = x.shape[0]
    dn = mm(dxn, p["w_out"], tb=True, name=f"ssm_dn_{tag}")
    dwout = mm(n, dxn, ta=True, name=f"ssm_dwout_{tag}")
    dy, dxs_skip, dz, dd_lane, dgain = gnorm_bwd(dn, y, xs, z, p["d_exp"], p["norm_gain"], ng, f"ssm_dgnorm_{tag}")
    dxs, dbm, dcm, ddt_c, da_c = ssd_chunk_bwd(xs, bm, cm, col_a, col_dt, rowf, hprev, dy, f"ssm_dscan_{tag}")
    ddtr, dbias, dalog = ssd_post(_from_colform(ddt_c, s), _from_colform(da_c, s), dt, dtr,
                                  p["dt_bias"], p["a_log"], f"ssm_post_{tag}")
    res = conv_bwd_pre(xbc, p["conv_w"], p["conv_b"], dxs, dxs_skip, dbm, dcm, f"ssm_dconv_{tag}")
    dpre, dconv_w, dconv_b = res[0], jnp.concatenate(res[1:5], axis=0), res[5]
    dxbc = conv_bwd_in(dpre, p["conv_w"], f"ssm_dconvin_{tag}")
    dh = mm(dz, p["w_z"], tb=True, name=f"ssm_dhz_{tag}")
    dh = mm(dxbc, p["w_xbc"], tb=True, add=dh, name=f"ssm_dhx_{tag}")
    dh = mm(ddtr, p["w_dt"], tb=True, add=dh, name=f"ssm_dhdt_{tag}")
    dwz = mm(h, dz, ta=True, name=f"ssm_dwz_{tag}")
    dwxbc = mm(h, dxbc, ta=True, name=f"ssm_dwxbc_{tag}")
    dwdt = mm(h, ddtr, ta=True, name=f"ssm_dwdt_{tag}")
    dx, dg = rms_bwd(x, g, dh, dxn, f"ssm_drms_{tag}")
    nh = ng * hpg
    dwin = jnp.concatenate([dwz, dwxbc, dwdt[:, :nh]], axis=1)
    dd = dd_lane.reshape(nh, HEAD).sum(-1)
    return dx, dg, dict(w_in=dwin, conv_w=dconv_w, conv_b=dconv_b, dt_bias=dbias[0, :nh], a_log=dalog[0, :nh],
                        d=dd, norm_gain=dgain, w_out=dwout)


def local_step(x, target, w):
    d = x.shape[1]
    depth = w["mix_norm"].shape[0]
    bd = _head_blockdiag(d)
    tril = jnp.tril(jnp.ones((CHUNK, CHUNK), bool))
    ssm_heads = w["ssm_dt_bias"].shape[1]
    d_inner = w["ssm_norm_gain"].shape[1]
    ng = w["ssm_norm_gain"].shape[1] // 256
    nstate = CHUNK

    def pad_lanes(v):
        return jnp.pad(v, ((0, 0), (0, LANES - v.shape[1])))

    def ssm_params(j):
        w_in = w["ssm_w_in"][j]
        cw = w["ssm_conv_w"][j]
        return dict(ng=ng, hpg=ssm_heads // ng, d_inner=d_inner,
                    w_z=w_in[:, :d_inner], w_xbc=w_in[:, d_inner:d_inner + d_inner + 2 * ng * nstate],
                    w_dt=pad_lanes(w_in[:, 2 * d_inner + 2 * ng * nstate:]),
                    conv_w=[cw[k:k + 1] for k in range(cw.shape[0])], conv_b=w["ssm_conv_b"][j:j + 1],
                    dt_bias=pad_lanes(w["ssm_dt_bias"][j:j + 1]), a_log=pad_lanes(w["ssm_a_log"][j:j + 1]),
                    d_exp=jnp.repeat(w["ssm_d"][j], HEAD)[None, :], norm_gain=w["ssm_norm_gain"][j:j + 1],
                    w_out=w["ssm_w_out"][j])

    def gm_params(j):
        wc = jnp.where(tril, w["gm_w_s"][j], 0.0).astype(BF16)
        bst = jnp.repeat(w["gm_b_s"][j].T, LANES, axis=1)
        return wc, bst

    def sb_gains(j):
        nh = d // HEAD
        return jnp.tile(w["sb_q_gain"][j], nh)[None, :], jnp.tile(w["sb_k_gain"][j], nh)[None, :]

    saved = []
    cur = x
    for i in range(depth):
        kind, j = i % 3, i // 3
        gmix = w["mix_norm"][i:i + 1]
        if kind == 0:
            qg, kg = sb_gains(j)
            cur, sv = sb_fwd(cur, gmix, w["sb_w_qkv"][j], qg, kg, w["sb_w_o"][j], bd, f"{i}")
        elif kind == 1:
            wc, bst = gm_params(j)
            cur, sv = gm_fwd(cur, gmix, w["gm_w_in"][j], w["gm_b_in"][j:j + 1], w["gm_v_gain"][j:j + 1], wc, bst,
                             w["gm_w_out"][j], f"{i}")
        else:
            cur, sv = ssm_fwd(cur, gmix, ssm_params(j), f"{i}")
        cur, sv2 = ffn_fwd(cur, w["ffn_norm"][i:i + 1], w["ffn_w_gu"][i], w["ffn_w_down"][i], f"{i}")
        saved.append((sv, sv2))

    loss, dcur = loss_and_grad(cur, target, "loss")

    grads = {k: [None] * len(v) for k, v in w.items()}
    for i in reversed(range(depth)):
        kind, j = i % 3, i // 3
        sv, sv2 = saved[i]
        gmix = w["mix_norm"][i:i + 1]
        dcur, dgf, dwgu, dwdown = ffn_bwd(dcur, sv2, w["ffn_norm"][i:i + 1], w["ffn_w_gu"][i], w["ffn_w_down"][i], f"{i}")
        grads["ffn_norm"][i], grads["ffn_w_gu"][i], grads["ffn_w_down"][i] = dgf[0], dwgu, dwdown
        if kind == 0:
            qg, kg = sb_gains(j)
            dcur, dg, dwqkv, dqg, dkg, dwo = sb_bwd(dcur, sv, gmix, w["sb_w_qkv"][j], qg, kg, w["sb_w_o"][j], bd, f"{i}")
            grads["sb_w_qkv"][j], grads["sb_q_gain"][j], grads["sb_k_gain"][j], grads["sb_w_o"][j] = dwqkv, dqg, dkg, dwo
        elif kind == 1:
            wc, bst = gm_params(j)
            dcur, dg, dwin, dbin, dvg, dws, dbs, dwout = gm_bwd(dcur, sv, gmix, w["gm_w_in"][j], w["gm_v_gain"][j:j + 1],
                                                                 wc, bst, w["gm_w_out"][j], f"{i}")
            grads["gm_w_in"][j], grads["gm_b_in"][j], grads["gm_v_gain"][j] = dwin, dbin[0], dvg[0]
            grads["gm_w_s"][j], grads["gm_b_s"][j], grads["gm_w_out"][j] = dws, dbs, dwout
        else:
            dcur, dg, gs = ssm_bwd(dcur, sv, gmix, ssm_params(j), f"{i}")
            grads["ssm_w_in"][j], grads["ssm_conv_w"][j], grads["ssm_conv_b"][j] = gs["w_in"], gs["conv_w"], gs["conv_b"][0]
            grads["ssm_dt_bias"][j], grads["ssm_a_log"][j], grads["ssm_d"][j] = gs["dt_bias"], gs["a_log"], gs["d"]
            grads["ssm_norm_gain"][j], grads["ssm_w_out"][j] = gs["norm_gain"][0], gs["w_out"]
        grads["mix_norm"][i] = dg[0]
    grads = {k: (v if k in MATRICES else jnp.stack(v)) for k, v in grads.items()}
    return loss, dcur, grads


WEIGHTS = ["mix_norm", "ffn_norm", "sb_w_qkv", "sb_q_gain", "sb_k_gain", "sb_w_o", "gm_w_in", "gm_b_in", "gm_v_gain",
           "gm_w_s", "gm_b_s", "gm_w_out", "ssm_w_in", "ssm_conv_w", "ssm_conv_b", "ssm_dt_bias", "ssm_a_log", "ssm_d",
           "ssm_norm_gain", "ssm_w_out", "ffn_w_gu", "ffn_w_down"]
SHARDED = {"sb_w_qkv": 2, "sb_w_o": 1, "gm_w_in": 2, "gm_w_out": 1, "ssm_w_in": 2, "ssm_conv_w": 2, "ssm_conv_b": 1,
           "ssm_norm_gain": 1, "ssm_w_out": 1, "ffn_w_gu": 2, "ffn_w_down": 1}
EXACT = ("ssm_conv_w", "ssm_conv_b", "ssm_norm_gain")
MATRICES = tuple(n for n in SHARDED if n not in EXACT)
COLUMN_BLOCKS = ("sb_w_qkv", "gm_w_in", "ffn_w_gu")
REPLICATED = [n for n in WEIGHTS if n not in SHARDED]
N_CHIPS = 4
N_DEV = 8
PACK_COLS = 1024


def _pack(pieces, dtype, align):
    flat = jnp.concatenate([p.reshape(-1).astype(dtype) for p in pieces])
    rows = -(-flat.shape[0] // (PACK_COLS * align)) * align
    flat = jnp.pad(flat, (0, rows * PACK_COLS - flat.shape[0]))
    return flat.reshape(rows, PACK_COLS)


def _unpack(flat, shapes):
    out, off = [], 0
    for shp in shapes:
        n = math.prod(shp)
        out.append(flat[off:off + n].reshape(shp))
        off += n
    return out


ANY = pl.BlockSpec(memory_space=pl.ANY)


def _pos():
    return lax.axis_index("x"), lax.axis_index("y"), lax.axis_index("c")


def _remote(src, dst, send, recv, k, to):
    return pltpu.make_async_remote_copy(src_ref=src, dst_ref=dst, send_sem=send.at[k], recv_sem=recv.at[k],
                                        device_id=to, device_id_type=MESH_ID)


def _comm_call(body, name, ins, out_shapes, nsem, aliases=None):
    return pl.pallas_call(
        body, name=name, out_shape=out_shapes,
        in_specs=[ANY] * len(ins), out_specs=[ANY] * len(out_shapes),
        scratch_shapes=[pltpu.SemaphoreType.DMA((nsem,)), pltpu.SemaphoreType.DMA((nsem,))],
        input_output_aliases=aliases or {},
    )(*ins)


def stage_shard(w, chip, name):
    rows, cols = w.shape
    tr = _pick(rows, (256, 352, 128))

    def kern(idx_ref, w_ref, o_ref):
        o_ref[...] = w_ref[...].astype(BF16)

    grid_spec = pltpu.PrefetchScalarGridSpec(
        num_scalar_prefetch=1, grid=(rows // tr,),
        in_specs=[pl.BlockSpec((tr, cols), lambda i, idx: (i, 0))],
        out_specs=pl.BlockSpec((None, tr, cols), lambda i, idx: (idx[0], i, 0)))
    return pl.pallas_call(
        kern, name=name, grid_spec=grid_spec,
        out_shape=jax.ShapeDtypeStruct((N_CHIPS, rows, cols), BF16),
        compiler_params=_params(("parallel",)),
    )(jnp.reshape(chip, (1,)).astype(jnp.int32), w)


def gather_weights(staged, name):
    n = len(staged)

    def body(*refs):
        o_refs = refs[n:2 * n]
        send, recv = refs[2 * n:]
        x, y, c = _pos()
        me, sibling = (x, y, c), (x, y, 1 - c)
        chips = [(1 - x, y), (x, 1 - y), (1 - x, 1 - y)]

        def part(u, chip, cc):
            half = staged[u].shape[1] // 2
            return o_refs[u].at[2 * chip[0] + chip[1], pl.ds(cc * half, half), :]

        first = []
        for u in range(n):
            first += [_remote(part(u, (x, y), c), part(u, (x, y), c), send, recv, 6 * u + j, (*chip, c))
                      for j, chip in enumerate(chips)]
        for cp in first:
            cp.start()
        passed = []
        for u in range(n):
            for j, chip in enumerate(chips):
                _remote(part(u, chip, c), part(u, chip, c), send, recv, 6 * u + j, me).wait_recv()
                fw = _remote(part(u, chip, c), part(u, chip, c), send, recv, 6 * u + 3 + j, sibling)
                fw.start()
                passed.append(fw)
        for u in range(n):
            for j, chip in enumerate(chips):
                _remote(part(u, chip, 1 - c), part(u, chip, 1 - c), send, recv, 6 * u + 3 + j, me).wait_recv()
        for cp in first + passed:
            cp.wait_send()

    outs = [jax.ShapeDtypeStruct(s.shape, s.dtype) for s in staged]
    return _comm_call(body, name, staged, outs, 6 * n, aliases={u: u for u in range(n)})


def swap_halves(gps):
    n = len(gps)

    def body(*refs):
        g_refs, r_refs = refs[:n], refs[n:2 * n]
        send, recv = refs[2 * n:]
        x, y, c = _pos()
        cps = []
        for u in range(n):
            half = gps[u].shape[1] // 2
            cps.append(_remote(g_refs[u].at[:, pl.ds((1 - c) * half, half), :], r_refs[u], send, recv, u, (x, y, 1 - c)))
        for cp in cps:
            cp.start()
        for cp in cps:
            cp.wait()

    outs = [jax.ShapeDtypeStruct((g.shape[0], g.shape[1] // 2, g.shape[2]), g.dtype) for g in gps]
    return _comm_call(body, "swap_halves", gps, outs, n)


def scatter_chunks(parts):
    n = len(parts)

    def body(*refs):
        p_refs, r_refs = refs[:n], refs[n:2 * n]
        send, recv = refs[2 * n:]
        x, y, c = _pos()
        chips = [(1 - x, y), (x, 1 - y), (1 - x, 1 - y)]
        cps = [_remote(p_refs[u].at[2 * chip[0] + chip[1]], r_refs[u].at[j], send, recv, 3 * u + j, (*chip, c))
               for u in range(n) for j, chip in enumerate(chips)]
        for cp in cps:
            cp.start()
        for cp in cps:
            cp.wait()

    outs = [jax.ShapeDtypeStruct((N_CHIPS - 1,) + p.shape[1:], p.dtype) for p in parts]
    return _comm_call(body, "scatter_chunks", parts, outs, 3 * n)


def join_halves(bufs):
    n = len(bufs)

    def body(*refs):
        o_refs = refs[n:2 * n]
        send, recv = refs[2 * n:]
        x, y, c = _pos()

        def rows(u, cc):
            half = bufs[u].shape[0] // 2
            return o_refs[u].at[pl.ds(cc * half, half), :]

        cps = [_remote(rows(u, c), rows(u, c), send, recv, u, (x, y, 1 - c)) for u in range(n)]
        for cp in cps:
            cp.start()
        for u in range(n):
            _remote(rows(u, 1 - c), rows(u, 1 - c), send, recv, u, (x, y, c)).wait_recv()
        for cp in cps:
            cp.wait_send()

    outs = [jax.ShapeDtypeStruct(b.shape, b.dtype) for b in bufs]
    return _comm_call(body, "join_halves", bufs, outs, n, aliases={u: u for u in range(n)})


def gather_small(sg, name):
    rows, cols = sg.shape

    def body(s_ref, o_ref, send, recv, lsem):
        x, y, c = _pos()
        mine = pltpu.make_async_copy(s_ref, o_ref.at[4 * x + 2 * y + c], lsem)
        mine.start()
        peers = []
        for msk in range(1, N_DEV):
            px = 1 - x if msk & 4 else x
            py = 1 - y if msk & 2 else y
            pc = 1 - c if msk & 1 else c
            peers.append((px, py, pc))
        cps = [_remote(s_ref, o_ref.at[4 * x + 2 * y + c], send, recv, k, peer) for k, peer in enumerate(peers)]
        for cp in cps:
            cp.start()
        for k, (px, py, pc) in enumerate(peers):
            _remote(s_ref, o_ref.at[4 * px + 2 * py + pc], send, recv, k, (x, y, c)).wait_recv()
        for cp in cps:
            cp.wait_send()
        mine.wait()

    return pl.pallas_call(
        body, name=name,
        out_shape=jax.ShapeDtypeStruct((N_DEV, rows, cols), sg.dtype),
        in_specs=[ANY], out_specs=ANY,
        scratch_shapes=[pltpu.SemaphoreType.DMA((N_DEV - 1,)), pltpu.SemaphoreType.DMA((N_DEV - 1,)), pltpu.SemaphoreType.DMA],
    )(sg)


def sum_cores(gp, theirs, core, chip, name):
    nch, rows, cols = gp.shape
    half = rows // 2
    tr = _pick(half, (256, 176, 128, 64))
    nb = half // tr

    def kern(idx_ref, g_ref, t_ref, own_ref, all_ref):
        k = pl.program_id(1)
        s = g_ref[...] + t_ref[...]
        all_ref[...] = s.astype(BF16)

        @pl.when(k == idx_ref[1])
        def _():
            own_ref[...] = s

    grid_spec = pltpu.PrefetchScalarGridSpec(
        num_scalar_prefetch=1, grid=(nb, nch),
        in_specs=[pl.BlockSpec((None, tr, cols), lambda i, k, idx: (k, idx[0] * nb + i, 0)),
                  pl.BlockSpec((None, tr, cols), lambda i, k, idx: (k, i, 0))],
        out_specs=[pl.BlockSpec((tr, cols), lambda i, k, idx: (i, 0)),
                   pl.BlockSpec((None, tr, cols), lambda i, k, idx: (k, i, 0))])
    return pl.pallas_call(
        kern, name=name, grid_spec=grid_spec,
        out_shape=[jax.ShapeDtypeStruct((half, cols), F32), jax.ShapeDtypeStruct((nch, half, cols), BF16)],
        compiler_params=_params(("parallel", "arbitrary")),
    )(jnp.stack([core, chip]).astype(jnp.int32), gp, theirs)


def sum_chips(own, others, core, name):
    half, cols = own.shape
    tr = _pick(half, (256, 176, 128, 64))
    nb = half // tr

    def kern(idx_ref, o_ref, a_ref, b_ref, c_ref, out_ref):
        out_ref[...] = ((o_ref[...] + a_ref[...].astype(F32)) + b_ref[...].astype(F32)) + c_ref[...].astype(F32)

    grid_spec = pltpu.PrefetchScalarGridSpec(
        num_scalar_prefetch=1, grid=(nb,),
        in_specs=[pl.BlockSpec((tr, cols), lambda i, idx: (i, 0))] +
                 [pl.BlockSpec((None, tr, cols), lambda i, idx, j=j: (j, i, 0)) for j in range(N_CHIPS - 1)],
        out_specs=pl.BlockSpec((tr, cols), lambda i, idx: (idx[0] * nb + i, 0)))
    return pl.pallas_call(
        kern, name=name, grid_spec=grid_spec,
        out_shape=jax.ShapeDtypeStruct((2 * half, cols), F32),
        compiler_params=_params(("parallel",)),
    )(jnp.reshape(core, (1,)).astype(jnp.int32), own, others, others, others)


def small_update(gath, w, m, v, name):
    def fn(*vs):
        g = vs[0]
        for t in vs[1:N_DEV]:
            g = g + t
        wv, mv, vv = vs[N_DEV:]
        m2 = ADAM_B1 * mv + (1.0 - ADAM_B1) * g
        v2 = ADAM_B2 * vv + (1.0 - ADAM_B2) * (g * g)
        m_hat = m2 / (1.0 - ADAM_B1 ** ADAM_STEP)
        v_hat = v2 / (1.0 - ADAM_B2 ** ADAM_STEP)
        return g, -ADAM_LR * (m_hat / (jnp.sqrt(v_hat) + ADAM_EPS) + ADAM_WD * wv), m2, v2

    c = w.shape[1]
    ins = [(gath[k], "row") for k in range(N_DEV)] + [(w, "row"), (m, "row"), (v, "row")]
    return rowwise(fn, ins, [(c, F32)] * 4, tr=w.shape[0] // 2, name=name)


def _step(ins):
    x, target = ins["x"][0], ins["loss_target"][0]
    core = lax.axis_index("c")
    chip = 2 * lax.axis_index("x") + lax.axis_index("y")
    units = [(n, l) for n in MATRICES for l in range(ins[n].shape[0])]

    def lane_pad(v):
        return jnp.pad(v, ((0, 0), (0, PACK_COLS - v.shape[1])))

    vec_rows = [ins["ssm_conv_w"][0], ins["ssm_conv_b"], lane_pad(ins["ssm_norm_gain"])]
    blk = jnp.concatenate(vec_rows + [jnp.zeros((SUBLANES - 6, PACK_COLS), F32)], axis=0)
    per_chip = gather_small(blk, "gather_vectors")[0::2]
    ngw = ins["ssm_norm_gain"].shape[1]
    full = {
        "ssm_conv_w": jnp.concatenate([per_chip[k, 0:4] for k in range(N_CHIPS)], axis=1)[None],
        "ssm_conv_b": jnp.concatenate([per_chip[k, 4:5] for k in range(N_CHIPS)], axis=1),
        "ssm_norm_gain": jnp.concatenate([per_chip[k, 5:6, :ngw] for k in range(N_CHIPS)], axis=1),
    }

    gathered = gather_weights([stage_shard(ins[n][l], chip, f"stage_{n}_{l}") for n, l in units], "gather_weights")
    for n in MATRICES:
        full[n] = []
    for (n, l), g in zip(units, gathered):
        if n in COLUMN_BLOCKS:
            full[n].append(g)
        elif n == "ssm_w_in":
            full[n].append(jnp.concatenate([g[k] for k in range(N_CHIPS)], axis=1))
        else:
            full[n].append(g.reshape(-1, g.shape[-1]))
    for n in REPLICATED:
        full[n] = ins[n]

    loss, dx, grads = local_step(x, target, full)
    loss = lax.psum(loss, ALL_AXES)

    gps = []
    for n, l in units:
        g = grads[n][l]
        if n in COLUMN_BLOCKS:
            gps.append(g)
        elif n == "ssm_w_in":
            gps.append(jnp.stack(jnp.split(g, N_CHIPS, axis=1)))
        else:
            gps.append(g.reshape(N_CHIPS, -1, g.shape[-1]))
    theirs = swap_halves(gps)
    owns, alls = [], []
    for (n, l), g, t in zip(units, gps, theirs):
        own, every = sum_cores(g, t, core, chip, f"sum_cores_{n}_{l}")
        owns.append(own)
        alls.append(every)
    others = scatter_chunks(alls)
    ghalves = [sum_chips(o, ot, core, f"sum_chips_{n}_{l}") for (n, l), o, ot in zip(units, owns, others)]
    gshards = dict(zip(units, join_halves(ghalves)))

    small_shapes = [ins[n].shape for n in REPLICATED]
    vec_shapes = [grads[n].shape for n in EXACT]
    vec_pack = _pack([grads[n] for n in EXACT], F32, SUBLANES)
    gath = gather_small(jnp.concatenate([_pack([grads[n] for n in REPLICATED], F32, SUBLANES), vec_pack], axis=0),
                        "gather_small")
    packed = [jnp.concatenate([_pack([ins[pre + n] for n in REPLICATED], F32, SUBLANES), jnp.zeros_like(vec_pack)], axis=0)
              for pre in ("", "m_", "v_")]
    res = small_update(gath, *packed, name="small_update")
    nrep = res[0].shape[0] - vec_pack.shape[0]
    small = [dict(zip(REPLICATED, _unpack(r[:nrep].reshape(-1), small_shapes))) for r in res]
    vec_g = dict(zip(EXACT, _unpack(res[0][nrep:].reshape(-1), vec_shapes)))

    out_g, out_d, out_m, out_v = {}, {}, {}, {}
    for n in REPLICATED:
        out_g[n], out_d[n], out_m[n], out_v[n] = (s[n] for s in small)
    for n in SHARDED:
        shp = ins[n].shape
        if n in EXACT:
            g = lax.dynamic_slice_in_dim(vec_g[n], chip * shp[-1], shp[-1], axis=vec_g[n].ndim - 1)
        else:
            g = jnp.stack([gshards[(n, l)] for l in range(shp[0])])
        two = (math.prod(shp[:-1]), shp[-1])
        d2, m2, v2 = adamw(ins[n].reshape(two), g.reshape(two), ins["m_" + n].reshape(two),
                           ins["v_" + n].reshape(two), f"adamw_{n}")
        out_g[n], out_d[n], out_m[n], out_v[n] = g, d2.reshape(shp), m2.reshape(shp), v2.reshape(shp)
    return (loss, dx[None], *[out_g[n] for n in WEIGHTS], *[out_d[n] for n in WEIGHTS],
            *[out_m[n] for n in WEIGHTS], *[out_v[n] for n in WEIGHTS])


def kernel(x, mix_norm, ffn_norm, sb_w_qkv, sb_q_gain, sb_k_gain, sb_w_o, gm_w_in, gm_b_in, gm_v_gain, gm_w_s, gm_b_s, gm_w_out, ssm_w_in, ssm_conv_w, ssm_conv_b, ssm_dt_bias, ssm_a_log, ssm_d, ssm_norm_gain, ssm_w_out, ffn_w_gu, ffn_w_down, loss_target, m_mix_norm, m_ffn_norm, m_sb_w_qkv, m_sb_q_gain, m_sb_k_gain, m_sb_w_o, m_gm_w_in, m_gm_b_in, m_gm_v_gain, m_gm_w_s, m_gm_b_s, m_gm_w_out, m_ssm_w_in, m_ssm_conv_w, m_ssm_conv_b, m_ssm_dt_bias, m_ssm_a_log, m_ssm_d, m_ssm_norm_gain, m_ssm_w_out, m_ffn_w_gu, m_ffn_w_down, v_mix_norm, v_ffn_norm, v_sb_w_qkv, v_sb_q_gain, v_sb_k_gain, v_sb_w_o, v_gm_w_in, v_gm_b_in, v_gm_v_gain, v_gm_w_s, v_gm_b_s, v_gm_w_out, v_ssm_w_in, v_ssm_conv_w, v_ssm_conv_b, v_ssm_dt_bias, v_ssm_a_log, v_ssm_d, v_ssm_norm_gain, v_ssm_w_out, v_ffn_w_gu, v_ffn_w_down):
    return _step(dict(locals()))
```

```python
import functools
import math

import jax
import jax.numpy as jnp
from jax import lax
from jax.experimental import pallas as pl
from jax.experimental.pallas import tpu as pltpu

F32 = jnp.float32
BF16 = jnp.bfloat16
EPS = 1e-6
LANES = 128
SUBLANES = 8
VMEM_LIMIT = 56 * 1024 * 1024
HEAD = 64
CHUNK = 128
SB_TQ, SB_TK = 256, 256
SB_DEAD = -110.0
SB_UNSEEN = -1e30
ADAM_LR, ADAM_B1, ADAM_B2, ADAM_EPS, ADAM_WD, ADAM_STEP = 0.001, 0.9, 0.999, 1e-08, 0.01, 10
MESH_ID = pl.DeviceIdType.MESH
ALL_AXES = ("x", "y", "c")


def _params(sem):
    return pltpu.CompilerParams(dimension_semantics=sem, vmem_limit_bytes=VMEM_LIMIT)


def _pick(n, cands):
    for c in cands:
        if n % c == 0:
            return c
    return n


def _dot(a, b, dims=((1,), (0,))):
    return lax.dot_general(a, b, (dims, ((), ())), preferred_element_type=F32)


def _dot_nt(a, b):
    return _dot(a, b, ((1,), (1,)))


def _dot_tn(a, b):
    return _dot(a, b, ((0,), (0,)))


def _split2(x):
    hi = x.astype(BF16)
    lo = (x - hi.astype(F32)).astype(BF16)
    return hi, lo


def _dot_x2(x, m):
    hi, lo = _split2(x)
    return _dot(hi, m) + _dot(lo, m)


def _dot_x3_left(m, x):
    h1 = x.astype(BF16)
    r1 = x - h1.astype(F32)
    h2 = r1.astype(BF16)
    h3 = (r1 - h2.astype(F32)).astype(BF16)
    return _dot(m, h1) + _dot(m, h2) + _dot(m, h3)


def _sigmoid(x):
    return 1.0 / (1.0 + jnp.exp(-x))


def _softplus(x):
    return jnp.maximum(x, 0.0) + jnp.log(1.0 + jnp.exp(-jnp.abs(x)))


def _colsum(x):
    return jnp.sum(x, axis=0, keepdims=True)


def _rowsum(x):
    return jnp.sum(x, axis=1, keepdims=True)


def _iota2(shape, dim):
    return lax.broadcasted_iota(jnp.int32, shape, dim)


MM_VMEM_BUDGET = 40 * 1024 * 1024
MM_STEP_US = 0.35
MM_HBM_BYTES_PER_US = 3.0e6
MM_VMEM_BYTES_PER_US = 1.5e6
MM_FLOPS_PER_US = 9.0e8
MXU_DIM = 256


def _mm_tiles(m, n, kk, wn, wk, a_bytes, b_bytes, has_add):
    def divisors(total, cands):
        got = [c for c in cands if total % c == 0 and c <= total]
        return got or [total]

    best = None
    for tm in divisors(m, (1024, 512, 256, 128)):
        for tn in divisors(wn, (1024, 768, 1408, 512, 256, 128)):
            for tk in divisors(wk, (4096, 2816, 2048, 1408, 1024, 768, 512, 256, 128)):
                nk = kk // tk
                vmem = 2 * (tm * tk * a_bytes + tk * tn * b_bytes + tm * tn * 4 * (2 if has_add else 1))
                vmem += tm * tn * 4 if nk > 1 else 0
                if vmem > MM_VMEM_BUDGET:
                    continue
                steps = (m // tm) * (n // tn) * nk
                a_reads = 1 if nk == 1 else n // tn
                traffic = m * kk * a_bytes * a_reads + kk * n * b_bytes * (m // tm) + m * n * 4
                fill = min(1.0, tn / MXU_DIM) * min(1.0, tm / MXU_DIM)
                compute = 2.0 * m * n * kk / (MM_FLOPS_PER_US * fill)
                cost = steps * MM_STEP_US + max(compute, traffic / MM_HBM_BYTES_PER_US)
                if nk > 1:
                    cost += steps * tm * tn * 8 / MM_VMEM_BYTES_PER_US
                if best is None or cost < best[0]:
                    best = (cost, tm, tn, tk)
    return best[1:]


def mm(a, b, *, ta=False, tb=False, add=None, bias=None, b_chunks=False, out_chunks=False, name):
    if ta:
        kk, m = a.shape
    else:
        m, kk = a.shape
    nch, wide = 1, None
    if b_chunks:
        nch, rows_b, wide = b.shape
        kb, n = (rows_b, nch * wide) if not tb else (nch * wide, rows_b)
    elif tb:
        n, kb = b.shape
    else:
        kb, n = b.shape
    if out_chunks:
        nch, wide = N_CHIPS, n // N_CHIPS
    assert kk == kb, (a.shape, b.shape, ta, tb)
    has_add, has_bias = add is not None, bias is not None
    tm, tn, tk = _mm_tiles(m, n, kk, wide if (wide and not tb) or out_chunks else n, wide if (wide and tb) else kk,
                           a.dtype.itemsize, b.dtype.itemsize, has_add)
    nk = kk // tk
    dims = ((0 if ta else 1,), (1 if tb else 0,))

    def kern(*refs):
        a_ref, b_ref = refs[0], refs[1]
        rest = list(refs[2:])
        add_ref = rest.pop(0) if has_add else None
        bias_ref = rest.pop(0) if has_bias else None
        o_ref = rest[0]
        part = _dot(a_ref[...].astype(BF16), b_ref[...].astype(BF16), dims)

        def finish(r):
            if has_add:
                r = r + add_ref[...]
            if has_bias:
                r = r + bias_ref[...]
            o_ref[...] = r

        if nk == 1:
            finish(part)
        else:
            acc_ref = rest[1]
            k = pl.program_id(2)

            @pl.when(k == 0)
            def _():
                acc_ref[...] = part

            @pl.when((k > 0) & (k < nk - 1))
            def _():
                acc_ref[...] += part

            @pl.when(k == nk - 1)
            def _():
                finish(acc_ref[...] + part)

    a_spec = pl.BlockSpec((tk, tm), lambda i, j, k: (k, i)) if ta else pl.BlockSpec((tm, tk), lambda i, j, k: (i, k))
    if b_chunks and tb:
        per = wide // tk
        b_spec = pl.BlockSpec((None, tn, tk), lambda i, j, k: (k // per, j, k % per))
    elif b_chunks:
        per = wide // tn
        b_spec = pl.BlockSpec((None, tk, tn), lambda i, j, k: (j // per, k, j % per))
    elif tb:
        b_spec = pl.BlockSpec((tn, tk), lambda i, j, k: (j, k))
    else:
        b_spec = pl.BlockSpec((tk, tn), lambda i, j, k: (k, j))
    if out_chunks:
        per_o = wide // tn
        out_spec = pl.BlockSpec((None, tm, tn), lambda i, j, k: (j // per_o, i, j % per_o))
        out_shape = jax.ShapeDtypeStruct((nch, m, wide), F32)
    else:
        out_spec = pl.BlockSpec((tm, tn), lambda i, j, k: (i, j))
        out_shape = jax.ShapeDtypeStruct((m, n), F32)
    in_specs, args = [a_spec, b_spec], [a, b]
    if has_add:
        in_specs.append(pl.BlockSpec((tm, tn), lambda i, j, k: (i, j)))
        args.append(add)
    if has_bias:
        in_specs.append(pl.BlockSpec((1, tn), lambda i, j, k: (0, j)))
        args.append(bias)
    return pl.pallas_call(
        kern,
        name=name,
        grid=(m // tm, n // tn, nk),
        in_specs=in_specs,
        out_specs=out_spec,
        out_shape=out_shape,
        scratch_shapes=[pltpu.VMEM((tm, tn), F32)] if nk > 1 else [],
        compiler_params=_params(("parallel", "parallel", "arbitrary")),
    )(*args)


def rowwise(fn, ins, outs, accs=(), *, tr, name):
    rows = [a for a, kind in ins if kind == "row"][0].shape[0]
    tr = min(tr, rows)
    assert rows % tr == 0 and tr % SUBLANES == 0, (rows, tr)
    n = rows // tr
    n_in, n_out = len(ins), len(outs)
    kinds = [kind for _, kind in ins]

    def kern(*refs):
        i = pl.program_id(0)
        vals = []
        for ref, kind in zip(refs[:n_in], kinds):
            v = ref[...]
            if kind == "prev":
                v = v * (i > 0).astype(v.dtype)
            elif kind == "next":
                v = v * (i < n - 1).astype(v.dtype)
            vals.append(v)
        res = fn(*vals)
        for ref, r in zip(refs[n_in:n_in + n_out], res[:n_out]):
            ref[...] = r.astype(ref.dtype)
        if accs:
            acc_refs = refs[n_in + n_out:]

            @pl.when(i == 0)
            def _():
                for ref in acc_refs:
                    ref[...] = jnp.zeros_like(ref)

            for ref, r in zip(acc_refs, res[n_out:]):
                ref[...] += r

    in_specs = []
    for a, kind in ins:
        if kind == "row":
            in_specs.append(pl.BlockSpec((tr, a.shape[1]), lambda i: (i, 0)))
        elif kind == "full":
            in_specs.append(pl.BlockSpec(a.shape, lambda i, nd=a.ndim: (0,) * nd))
        elif kind == "prev":
            in_specs.append(pl.BlockSpec((SUBLANES, a.shape[1]),
                                         lambda i: (jnp.maximum(i * (tr // SUBLANES) - 1, 0), 0)))
        else:
            in_specs.append(pl.BlockSpec((SUBLANES, a.shape[1]),
                                         lambda i: (jnp.minimum((i + 1) * (tr // SUBLANES), rows // SUBLANES - 1), 0)))
    out_specs = [pl.BlockSpec((tr, c), lambda i: (i, 0)) for c, _ in outs]
    out_specs += [pl.BlockSpec((r, c), lambda i: (0, 0)) for r, c in accs]
    out_shape = [jax.ShapeDtypeStruct((rows, c), dt) for c, dt in outs]
    out_shape += [jax.ShapeDtypeStruct((r, c), F32) for r, c in accs]
    res = pl.pallas_call(
        kern,
        name=name,
        grid=(n,),
        in_specs=in_specs,
        out_specs=out_specs,
        out_shape=out_shape,
        compiler_params=_params(("arbitrary",) if accs else ("parallel",)),
    )(*[a for a, _ in ins])
    return res


def rms_fwd(x, g, name):
    def fn(xv, gv):
        r = lax.rsqrt(jnp.mean(xv * xv, axis=1, keepdims=True) + EPS)
        return (xv * r * gv,)

    return rowwise(fn, [(x, "row"), (g, "full")], [(x.shape[1], BF16)], tr=512, name=name)[0]


def rms_bwd(x, g, dy, dres, name):
    def fn(xv, gv, dyv, drv):
        r = lax.rsqrt(jnp.mean(xv * xv, axis=1, keepdims=True) + EPS)
        xh = xv * r
        dyg = dyv * gv
        dx = drv + r * (dyg - xh * jnp.mean(dyg * xh, axis=1, keepdims=True))
        return dx, dx, _colsum(dyv * xh)

    c = x.shape[1]
    dx, dxb, dg = rowwise(fn, [(x, "row"), (g, "full"), (dy, "row"), (dres, "row")], [(c, F32), (c, BF16)], [(1, c)],
                          tr=256, name=name)
    return (dx, dxb), dg


def swiglu_fwd(gu, name):
    hid = gu.shape[1] // 2

    def fn(v):
        g, u = v[:, :hid], v[:, hid:]
        return (g * _sigmoid(g) * u,)

    return rowwise(fn, [(gu, "row")], [(hid, BF16)], tr=256, name=name)[0]


def swiglu_bwd(gu, da, name):
    hid = gu.shape[1] // 2

    def fn(v, d):
        g, u = v[:, :hid], v[:, hid:]
        s = _sigmoid(g)
        dg = d * u * s * (1.0 + g * (1.0 - s))
        du = d * g * s
        return (jnp.concatenate([dg, du], axis=1),)

    return rowwise(fn, [(gu, "row"), (da, "row")], [(2 * hid, BF16)], tr=256, name=name)[0]


def loss_and_grad(y, t, name):
    d = y.shape[1]

    def fn(yv, tv):
        e = yv - tv
        part = jnp.sum(_colsum(e * e), axis=1, keepdims=True) * (0.5 / d)
        dy = e * (1.0 / d)
        return dy, dy, jnp.broadcast_to(part, (SUBLANES, LANES))

    dy, dyb, acc = rowwise(fn, [(y, "row"), (t, "row")], [(d, F32), (d, BF16)], [(SUBLANES, LANES)], tr=512, name=name)
    return acc[0, 0], (dy, dyb)


def adamw(w, g, m, v, name):
    def fn(wv, gv, mv, vv):
        m2 = ADAM_B1 * mv + (1.0 - ADAM_B1) * gv
        v2 = ADAM_B2 * vv + (1.0 - ADAM_B2) * (gv * gv)
        m_hat = m2 / (1.0 - ADAM_B1 ** ADAM_STEP)
        v_hat = v2 / (1.0 - ADAM_B2 ** ADAM_STEP)
        delta = -ADAM_LR * (m_hat / (jnp.sqrt(v_hat) + ADAM_EPS) + ADAM_WD * wv)
        return delta, m2, v2

    rows, c = w.shape
    tr = _pick(rows, (256, 128, 64, 32, 16, 8)) if rows % SUBLANES == 0 else rows
    if rows % SUBLANES:
        return _whole(fn, [w, g, m, v], [(w.shape, F32)] * 3, name=name)
    return rowwise(fn, [(w, "row"), (g, "row"), (m, "row"), (v, "row")], [(c, F32)] * 3, tr=tr, name=name)


def _whole(fn, ins, outs, *, name):
    n_in = len(ins)

    def kern(*refs):
        res = fn(*[r[...] for r in refs[:n_in]])
        for ref, r in zip(refs[n_in:], res):
            ref[...] = r.astype(ref.dtype)

    return pl.pallas_call(
        kern,
        name=name,
        out_shape=[jax.ShapeDtypeStruct(s, dt) for s, dt in outs],
        compiler_params=pltpu.CompilerParams(vmem_limit_bytes=VMEM_LIMIT),
    )(*ins)


def ffn_fwd(x, g, wgu, wdown, tag):
    h = rms_fwd(x, g, f"ffn_rms_{tag}")
    gu = mm(h, wgu, b_chunks=True, name=f"ffn_gu_{tag}")
    a = swiglu_fwd(gu, f"ffn_act_{tag}")
    xn = mm(a, wdown, add=x, name=f"ffn_down_{tag}")
    return xn, (x, h, gu, a)


def ffn_bwd(dxn, saved, g, wgu, wdown, tag):
    x, h, gu, a = saved
    dxn, dxb = dxn
    da = mm(dxb, wdown, tb=True, name=f"ffn_da_{tag}")
    dwdown = mm(a, dxb, ta=True, name=f"ffn_dwdown_{tag}")
    dgu = swiglu_bwd(gu, da, f"ffn_dact_{tag}")
    dh = mm(dgu, wgu, tb=True, b_chunks=True, name=f"ffn_dh_{tag}")
    dwgu = mm(h, dgu, ta=True, out_chunks=True, name=f"ffn_dwgu_{tag}")
    dx, dg = rms_bwd(x, g, dh, dxn, f"ffn_drms_{tag}")
    return dx, dg, dwgu, dwdown


def _head_blockdiag(c):
    i = jnp.arange(c) // HEAD
    return (i[:, None] == i[None, :]).astype(BF16)


def qknorm_fwd(qkv, qg, kg, bd, name):
    d = qkv.shape[1] // 3
    scale = 1.0 / math.sqrt(HEAD)

    def fn(v, qgv, kgv, bdv):
        q, k, vv = v[:, :d], v[:, d:2 * d], v[:, 2 * d:]
        rq = lax.rsqrt(_dot_x2(q * q, bdv) * (1.0 / HEAD) + EPS)
        rk = lax.rsqrt(_dot_x2(k * k, bdv) * (1.0 / HEAD) + EPS)
        return q * rq * qgv * scale, k * rk * kgv, vv

    return rowwise(fn, [(qkv, "row"), (qg, "full"), (kg, "full"), (bd, "full")],
                   [(d, BF16), (d, BF16), (d, BF16)], tr=256, name=name)


def qknorm_bwd(qkv, dqs, dkn, dv, qg, kg, bd, name):
    d = qkv.shape[1] // 3
    scale = 1.0 / math.sqrt(HEAD)

    def one(xv, gv, dyv, bdv):
        r = lax.rsqrt(_dot_x2(xv * xv, bdv) * (1.0 / HEAD) + EPS)
        xh = xv * r
        dyg = dyv * gv
        dx = r * (dyg - xh * (_dot_x2(dyg * xh, bdv) * (1.0 / HEAD)))
        return dx, _colsum(dyv * xh)

    def fn(v, dqv, dkv, dvv, qgv, kgv, bdv):
        q, k = v[:, :d], v[:, d:2 * d]
        dq, dqg = one(q, qgv, dqv * scale, bdv)
        dk, dkg = one(k, kgv, dkv, bdv)
        return jnp.concatenate([dq, dk, dvv], axis=1), dqg, dkg

    return rowwise(fn, [(qkv, "row"), (dqs, "row"), (dkn, "row"), (dv, "row"), (qg, "full"), (kg, "full"), (bd, "full")],
                   [(3 * d, BF16)], [(1, d), (1, d)], tr=256, name=name)


def _sb_tile(qh, k, mask, tri_gt):
    z = _dot_nt(qh, k)
    sp = jnp.log(1.0 + jnp.exp(-jnp.abs(z)))
    lb = jnp.minimum(z, 0.0) - sp
    l1 = jnp.where(mask, lb - z, 0.0)
    suf = _dot_x2(l1, tri_gt)
    return lb, l1, suf


def _sb_setup(tq, tk):
    row, col = _iota2((tq, tk), 0), _iota2((tq, tk), 1)
    lane = _iota2((1, LANES), 1)
    halves = [(lane < HEAD).astype(BF16), (lane >= HEAD).astype(BF16)]
    lane_q = _iota2((tq, LANES), 1) + jnp.minimum(_iota2((tq, LANES), 0), 0)
    return row, col, halves, lane_q


def sb_attn_fwd(qs, kn, vb, name):
    s, d = qs.shape
    tq, tk = min(SB_TQ, s), min(SB_TK, s)
    nq = s // tq
    assert s // tk <= LANES and s % tq == 0 and s % tk == 0

    def kern(q_ref, k_ref, v_ref, o_ref, rs_ref, acc_ref):
        i = pl.program_id(1)
        row, col, halves, lane_q = _sb_setup(tq, tk)
        tri_gt = (_iota2((tk, tk), 0) > _iota2((tk, tk), 1)).astype(BF16)
        q = q_ref[...]
        qh = [q * hm for hm in halves]
        acc_ref[...] = jnp.zeros_like(acc_ref)
        rs_ref[...] = jnp.full(rs_ref.shape, SB_UNSEEN, F32)
        nkb = (i + 1) * (tq // tk)

        def more(st):
            return (st[0] < nkb) & (st[1] > SB_DEAD)

        def step(st):
            n, r = st[0], list(st[2:])
            kb = nkb - 1 - n
            ks = pl.multiple_of(kb * tk, tk)
            k = k_ref[pl.ds(ks, tk), :]
            v = v_ref[pl.ds(ks, tk), :]
            mask = col < row + (i * tq - kb * tk)
            at_kb = lane_q == kb
            for hh in range(2):
                lb, l1, suf = _sb_tile(qh[hh], k, mask, tri_gt)
                w = jnp.where(mask, jnp.exp(lb + suf + r[hh]), 0.0)
                acc_ref[...] += _dot(w.astype(BF16), v * halves[hh])
                rs_ref[hh] = jnp.where(at_kb, r[hh], rs_ref[hh])
                r[hh] = r[hh] + _rowsum(l1)
            return (n + 1, jnp.maximum(jnp.max(r[0]), jnp.max(r[1])), r[0], r[1])

        z1 = jnp.zeros((tq, 1), F32)
        lax.while_loop(more, step, (jnp.int32(0), jnp.float32(0.0), z1, z1))
        o_ref[...] = acc_ref[...].astype(BF16)

    nh2 = d // LANES
    return pl.pallas_call(
        kern,
        name=name,
        grid=(nh2, nq),
        in_specs=[pl.BlockSpec((tq, LANES), lambda h, i: (i, h)),
                  pl.BlockSpec((s, LANES), lambda h, i: (0, h)),
                  pl.BlockSpec((s, LANES), lambda h, i: (0, h))],
        out_specs=[pl.BlockSpec((tq, LANES), lambda h, i: (i, h)),
                   pl.BlockSpec((None, 2, tq, LANES), lambda h, i: (h, 0, i, 0))],
        out_shape=[jax.ShapeDtypeStruct((s, d), BF16), jax.ShapeDtypeStruct((nh2, 2, s, LANES), F32)],
        scratch_shapes=[pltpu.VMEM((tq, LANES), F32)],
        compiler_params=_params(("parallel", "arbitrary")),
    )(qs, kn, vb)


def sb_attn_bwd(qs, kn, vb, rsave, do, name):
    s, d = qs.shape
    tq, tk = min(SB_TQ, s), min(SB_TK, s)
    nq = s // tq

    def kern(q_ref, k_ref, v_ref, rs_ref, do_ref, dq_ref, dk_ref, dv_ref):
        i = pl.program_id(1)

        @pl.when(i == 0)
        def _():
            dk_ref[...] = jnp.zeros_like(dk_ref)
            dv_ref[...] = jnp.zeros_like(dv_ref)

        row, col, halves, lane_q = _sb_setup(tq, tk)
        tri_gt = (_iota2((tk, tk), 0) > _iota2((tk, tk), 1)).astype(BF16)
        tri_lt = (_iota2((tk, tk), 0) < _iota2((tk, tk), 1)).astype(BF16)
        q = q_ref[...]
        qh = [q * hm for hm in halves]
        dov = do_ref[...].astype(BF16)
        doh = [dov * hm for hm in halves]
        dq_ref[...] = jnp.zeros_like(dq_ref)
        nkb = (i + 1) * (tq // tk)
        top = jnp.maximum(jnp.max(rs_ref[0], axis=0, keepdims=True), jnp.max(rs_ref[1], axis=0, keepdims=True))
        dead = (top <= SB_DEAD) & (_iota2((1, LANES), 1) < nkb)
        kstart = jnp.minimum(jnp.sum(dead.astype(F32)).astype(jnp.int32), nkb)

        def step(kb, ep):
            ep = list(ep)
            ks = pl.multiple_of(kb * tk, tk)
            k = k_ref[pl.ds(ks, tk), :]
            v = v_ref[pl.ds(ks, tk), :]
            mask = col < row + (i * tq - kb * tk)
            at_kb = lane_q == kb
            for hh in range(2):
                lb, l1, suf = _sb_tile(qh[hh], k, mask, tri_gt)
                r = _rowsum(jnp.where(at_kb, rs_ref[hh], 0.0))
                w = jnp.where(mask, jnp.exp(lb + suf + r), 0.0)
                e = _dot_nt(doh[hh], v) * w
                pe = ep[hh] + _dot_x2(e, tri_lt)
                beta = jnp.exp(lb)
                dz = jnp.where(mask, e * (1.0 - beta) - pe * beta, 0.0).astype(BF16)
                dq_ref[...] += _dot(dz, k * halves[hh])
                dk_ref[pl.ds(ks, tk), :] += _dot_tn(dz, qh[hh])
                dv_ref[pl.ds(ks, tk), :] += _dot_tn(w.astype(BF16), doh[hh])
                ep[hh] = ep[hh] + _rowsum(e)
            return tuple(ep)

        z1 = jnp.zeros((tq, 1), F32)
        lax.fori_loop(kstart, nkb, step, (z1, z1))

    nh2 = d // LANES
    return pl.pallas_call(
        kern,
        name=name,
        grid=(nh2, nq),
        in_specs=[pl.BlockSpec((tq, LANES), lambda h, i: (i, h)),
                  pl.BlockSpec((s, LANES), lambda h, i: (0, h)),
                  pl.BlockSpec((s, LANES), lambda h, i: (0, h)),
                  pl.BlockSpec((None, 2, tq, LANES), lambda h, i: (h, 0, i, 0)),
                  pl.BlockSpec((tq, LANES), lambda h, i: (i, h))],
        out_specs=[pl.BlockSpec((tq, LANES), lambda h, i: (i, h)),
                   pl.BlockSpec((s, LANES), lambda h, i: (0, h)),
                   pl.BlockSpec((s, LANES), lambda h, i: (0, h))],
        out_shape=[jax.ShapeDtypeStruct((s, d), F32)] * 3,
        compiler_params=_params(("parallel", "arbitrary")),
    )(qs, kn, vb, rsave, do)


def sb_fwd(x, g, wqkv, qg, kg, wo, bd, tag):
    h = rms_fwd(x, g, f"sb_rms_{tag}")
    qkv = mm(h, wqkv, b_chunks=True, name=f"sb_qkv_{tag}")
    qs, kn, vb = qknorm_fwd(qkv, qg, kg, bd, f"sb_qknorm_{tag}")
    o, rsave = sb_attn_fwd(qs, kn, vb, f"sb_attn_{tag}")
    xn = mm(o, wo, add=x, name=f"sb_out_{tag}")
    return xn, (x, h, qkv, qs, kn, vb, rsave, o)


def sb_bwd(dxn, saved, g, wqkv, qg, kg, wo, bd, tag):
    x, h, qkv, qs, kn, vb, rsave, o = saved
    dxn, dxb = dxn
    do = mm(dxb, wo, tb=True, name=f"sb_do_{tag}")
    dwo = mm(o, dxb, ta=True, name=f"sb_dwo_{tag}")
    dqs, dkn, dv = sb_attn_bwd(qs, kn, vb, rsave, do, f"sb_dattn_{tag}")
    dqkv, dqg, dkg = qknorm_bwd(qkv, dqs, dkn, dv, qg, kg, bd, f"sb_dqknorm_{tag}")
    dh = mm(dqkv, wqkv, tb=True, b_chunks=True, name=f"sb_dh_{tag}")
    dwqkv = mm(h, dqkv, ta=True, out_chunks=True, name=f"sb_dwqkv_{tag}")
    dx, dg = rms_bwd(x, g, dh, dxn, f"sb_drms_{tag}")
    nh = dqg.shape[1] // HEAD
    return dx, dg, dwqkv, dqg.reshape(nh, HEAD).sum(0), dkg.reshape(nh, HEAD).sum(0), dwo


def _gelu(x):
    return 0.5 * x * (1.0 + lax.erf(x * (1.0 / math.sqrt(2.0))))


def _gelu_grad(x):
    return 0.5 * (1.0 + lax.erf(x * (1.0 / math.sqrt(2.0)))) + x * jnp.exp(-0.5 * x * x) * (1.0 / math.sqrt(2.0 * math.pi))


def gm_act_fwd(pre, vg, name):
    half = pre.shape[1] // 2

    def fn(p, vgv):
        u = _gelu(p[:, :half])
        v = _gelu(p[:, half:])
        r = lax.rsqrt(jnp.mean(v * v, axis=1, keepdims=True) + EPS)
        return u, v * r * vgv

    return rowwise(fn, [(pre, "row"), (vg, "full")], [(half, F32), (half, BF16)], tr=256, name=name)


def gm_act_bwd(pre, du, dvn, vg, name):
    half = pre.shape[1] // 2

    def fn(p, duv, dvnv, vgv):
        pu, pv = p[:, :half], p[:, half:]
        v = _gelu(pv)
        r = lax.rsqrt(jnp.mean(v * v, axis=1, keepdims=True) + EPS)
        vh = v * r
        dyg = dvnv * vgv
        dv = r * (dyg - vh * jnp.mean(dyg * vh, axis=1, keepdims=True))
        dpre = jnp.concatenate([duv * _gelu_grad(pu), dv * _gelu_grad(pv)], axis=1)
        return dpre, _colsum(dvnv * vh), _colsum(dpre)

    return rowwise(fn, [(pre, "row"), (du, "row"), (dvn, "row"), (vg, "full")],
                   [(2 * half, BF16)], [(1, half), (1, 2 * half)], tr=256, name=name)


def gm_spatial_fwd(u, vn, wc, bst, name):
    s, c = u.shape
    t = CHUNK
    ng = c // LANES

    def kern(u_ref, v_ref, w_ref, b_ref, o_ref):
        for g in range(ng):
            sl = slice(g * LANES, (g + 1) * LANES)
            mixed = _dot(w_ref[g], v_ref[:, sl]) + b_ref[:, sl]
            o_ref[:, sl] = (u_ref[:, sl] * mixed).astype(BF16)

    return pl.pallas_call(
        kern,
        name=name,
        grid=(s // t,),
        in_specs=[pl.BlockSpec((t, c), lambda i: (i, 0)), pl.BlockSpec((t, c), lambda i: (i, 0)),
                  pl.BlockSpec(wc.shape, lambda i: (0, 0, 0)), pl.BlockSpec(bst.shape, lambda i: (0, 0))],
        out_specs=pl.BlockSpec((t, c), lambda i: (i, 0)),
        out_shape=jax.ShapeDtypeStruct((s, c), BF16),
        compiler_params=_params(("parallel",)),
    )(u, vn, wc, bst)


def gm_spatial_bwd(dgate, u, vn, wc, bst, name):
    s, c = u.shape
    t = CHUNK
    ng = c // LANES

    def kern(dg_ref, u_ref, v_ref, w_ref, b_ref, du_ref, dv_ref, dw_ref, db_ref):
        i = pl.program_id(0)

        @pl.when(i == 0)
        def _():
            dw_ref[...] = jnp.zeros_like(dw_ref)
            db_ref[...] = jnp.zeros_like(db_ref)

        for g in range(ng):
            sl = slice(g * LANES, (g + 1) * LANES)
            vg = v_ref[:, sl]
            dgv = dg_ref[:, sl]
            mixed = _dot(w_ref[g], vg) + b_ref[:, sl]
            du_ref[:, sl] = dgv * mixed
            dmix = dgv * u_ref[:, sl]
            dmb = dmix.astype(BF16)
            dv_ref[:, sl] = _dot_tn(w_ref[g], dmb)
            dw_ref[g] += _dot_nt(dmb, vg)
            db_ref[:, sl] += dmix

    return pl.pallas_call(
        kern,
        name=name,
        grid=(s // t,),
        in_specs=[pl.BlockSpec((t, c), lambda i: (i, 0))] * 3 +
                 [pl.BlockSpec(wc.shape, lambda i: (0, 0, 0)), pl.BlockSpec(bst.shape, lambda i: (0, 0))],
        out_specs=[pl.BlockSpec((t, c), lambda i: (i, 0)), pl.BlockSpec((t, c), lambda i: (i, 0)),
                   pl.BlockSpec(wc.shape, lambda i: (0, 0, 0)), pl.BlockSpec(bst.shape, lambda i: (0, 0))],
        out_shape=[jax.ShapeDtypeStruct((s, c), F32), jax.ShapeDtypeStruct((s, c), F32),
                   jax.ShapeDtypeStruct(wc.shape, F32), jax.ShapeDtypeStruct(bst.shape, F32)],
        compiler_params=_params(("arbitrary",)),
    )(dgate, u, vn, wc, bst)


def gm_fwd(x, g, w_in, b_in, vg, wc, bst, w_out, tag):
    h = rms_fwd(x, g, f"gm_rms_{tag}")
    pre = mm(h, w_in, bias=b_in, b_chunks=True, name=f"gm_in_{tag}")
    u, vn = gm_act_fwd(pre, vg, f"gm_act_{tag}")
    gate = gm_spatial_fwd(u, vn, wc, bst, f"gm_spatial_{tag}")
    xn = mm(gate, w_out, add=x, name=f"gm_out_{tag}")
    return xn, (x, h, pre, u, vn, gate)


def gm_bwd(dxn, saved, g, w_in, vg, wc, bst, w_out, tag):
    x, h, pre, u, vn, gate = saved
    dxn, dxb = dxn
    dgate = mm(dxb, w_out, tb=True, name=f"gm_dgate_{tag}")
    dwout = mm(gate, dxb, ta=True, name=f"gm_dwout_{tag}")
    du, dvn, dws, dbst = gm_spatial_bwd(dgate, u, vn, wc, bst, f"gm_dspatial_{tag}")
    dpre, dvg, dbin = gm_act_bwd(pre, du, dvn, vg, f"gm_dact_{tag}")
    dh = mm(dpre, w_in, tb=True, b_chunks=True, name=f"gm_dh_{tag}")
    dwin = mm(h, dpre, ta=True, out_chunks=True, name=f"gm_dwin_{tag}")
    dx, dg = rms_bwd(x, g, dh, dxn, f"gm_drms_{tag}")
    ng = wc.shape[0]
    dws = jnp.where(jnp.tril(jnp.ones((CHUNK, CHUNK), bool)), dws, 0.0)
    dbs = dbst.reshape(CHUNK, ng, LANES).sum(-1).T
    return dx, dg, dwin, dbin, dvg, dws, dbs, dwout


def _conv_taps(xv, prev):
    cat = jnp.concatenate([prev, xv], axis=0)
    return [pltpu.roll(cat, sh, 0)[SUBLANES:] for sh in (3, 2, 1)] + [xv]


def conv_fwd(xbc, ws, b, d_inner, name):
    c = xbc.shape[1]
    nst = (c - d_inner) // 2

    def fn(xv, prev, w0, w1, w2, w3, bv):
        taps = _conv_taps(xv, prev)
        pre = bv + w0 * taps[0] + w1 * taps[1] + w2 * taps[2] + w3 * taps[3]
        out = pre * _sigmoid(pre)
        return out[:, :d_inner], out[:, d_inner:d_inner + nst], out[:, d_inner + nst:]

    return rowwise(fn, [(xbc, "row"), (xbc, "prev")] + [(w, "full") for w in ws] + [(b, "full")],
                   [(d_inner, F32), (nst, F32), (nst, F32)], tr=256, name=name)


def conv_bwd_pre(xbc, ws, b, dxs_a, dxs_b, db_m, dc_m, name):
    c = xbc.shape[1]

    def fn(xv, prev, w0, w1, w2, w3, bv, da, db2, dbm, dcm):
        taps = _conv_taps(xv, prev)
        pre = bv + w0 * taps[0] + w1 * taps[1] + w2 * taps[2] + w3 * taps[3]
        sg = _sigmoid(pre)
        dout = jnp.concatenate([da + db2, dbm, dcm], axis=1)
        dpre = dout * sg * (1.0 + pre * (1.0 - sg))
        return (dpre,) + tuple(_colsum(dpre * tp) for tp in taps) + (_colsum(dpre),)

    return rowwise(fn, [(xbc, "row"), (xbc, "prev")] + [(w, "full") for w in ws] +
                   [(b, "full"), (dxs_a, "row"), (dxs_b, "row"), (db_m, "row"), (dc_m, "row")],
                   [(c, F32)], [(1, c)] * 5, tr=256, name=name)


def conv_bwd_in(dpre, ws, name):
    c = dpre.shape[1]

    def fn(dv, nxt, w0, w1, w2, w3):
        cat = jnp.concatenate([dv, nxt], axis=0)
        n = cat.shape[0]
        up = [pltpu.roll(cat, n - sh, 0)[:dv.shape[0]] for sh in (1, 2, 3)]
        return (w3 * dv + w2 * up[0] + w1 * up[1] + w0 * up[2],)

    return rowwise(fn, [(dpre, "row"), (dpre, "next")] + [(w, "full") for w in ws], [(c, BF16)], tr=256, name=name)[0]


def ssd_pre(dtr, bias, alog, name):
    def fn(d, bv, al, tri):
        dt = _softplus(d + bv)
        a = dt * (-jnp.exp(al))
        return dt, _dot_x3_left(tri, a)

    tri = jnp.tril(jnp.ones((CHUNK, CHUNK), BF16))
    return rowwise(fn, [(dtr, "row"), (bias, "full"), (alog, "full"), (tri, "full")],
                   [(LANES, F32), (LANES, F32)], tr=CHUNK, name=name)


def _ssd_layouts(v, ngroups, hpg):
    s = v.shape[0]
    col = v[:, :ngroups * hpg].T.reshape(ngroups, hpg, s, 1)
    return jnp.broadcast_to(col, (ngroups, hpg, s, LANES))


def _ssd_rowform(acum, ngroups, hpg):
    s = acum.shape[0]
    nc = s // CHUNK
    a = acum[:, :ngroups * hpg].reshape(nc, CHUNK, ngroups, hpg).transpose(2, 0, 3, 1)
    last = jnp.broadcast_to(a[..., CHUNK - 1:], a.shape)
    return jnp.concatenate([a, last], axis=2)


def ssd_chunk_fwd(xs, bm, cm, col_a, col_dt, rowf, name):
    s, d_inner = xs.shape
    ln = CHUNK
    nc = s // ln
    ng, hpg = col_a.shape[0], col_a.shape[1]
    gw = d_inner // ng
    assert gw == hpg * HEAD and gw % LANES == 0 and bm.shape[1] == ng * LANES

    def kern(x_ref, b_ref, c_ref, ca_ref, cd_ref, rf_ref, y_ref, hp_ref, h_scr):
        c = pl.program_id(1)

        @pl.when(c == 0)
        def _():
            h_scr[...] = jnp.zeros_like(h_scr)

        bb = b_ref[...].astype(BF16)
        cbf = c_ref[...].astype(BF16)
        cb = _dot_nt(cbf, bb)
        causal = _iota2((ln, ln), 0) >= _iota2((ln, ln), 1)
        lane = _iota2((1, LANES), 1)
        ys = [jnp.zeros((ln, LANES), F32) for _ in range(gw // LANES)]
        for r in range(hpg):
            j, hf = divmod(r, LANES // HEAD)
            mh = ((lane >= HEAD * hf) & (lane < HEAD * (hf + 1))).astype(F32)
            ac = ca_ref[r]
            ar = rf_ref[pl.ds(r, 1), :]
            aend = rf_ref[pl.ds(4 + r, 1), :]
            dm = jnp.exp(jnp.minimum(ac - ar, 0.0))
            m = jnp.where(causal, cb * dm, 0.0).astype(BF16)
            xdt = x_ref[:, j * LANES:(j + 1) * LANES] * cd_ref[r] * mh
            h = h_scr[r]
            hp_ref[r] = h
            ys[j] = ys[j] + _dot(m, xdt.astype(BF16)) + _dot_nt(cbf, h.astype(BF16)) * jnp.exp(ac)
            dte = jnp.exp(aend - ac)
            h_scr[r] = jnp.exp(aend) * h + _dot_tn((xdt * dte).astype(BF16), bb)
        for j in range(gw // LANES):
            y_ref[:, j * LANES:(j + 1) * LANES] = ys[j]

    return pl.pallas_call(
        kern,
        name=name,
        grid=(ng, nc),
        in_specs=[pl.BlockSpec((ln, gw), lambda g, c: (c, g)),
                  pl.BlockSpec((ln, LANES), lambda g, c: (c, g)),
                  pl.BlockSpec((ln, LANES), lambda g, c: (c, g)),
                  pl.BlockSpec((None, hpg, ln, LANES), lambda g, c: (g, 0, c, 0)),
                  pl.BlockSpec((None, hpg, ln, LANES), lambda g, c: (g, 0, c, 0)),
                  pl.BlockSpec((None, None, 8, LANES), lambda g, c: (g, c, 0, 0))],
        out_specs=[pl.BlockSpec((ln, gw), lambda g, c: (c, g)),
                   pl.BlockSpec((None, None, hpg, LANES, LANES), lambda g, c: (g, c, 0, 0, 0))],
        out_shape=[jax.ShapeDtypeStruct((s, d_inner), F32),
                   jax.ShapeDtypeStruct((ng, nc, hpg, LANES, LANES), F32)],
        scratch_shapes=[pltpu.VMEM((hpg, LANES, LANES), F32)],
        compiler_params=_params(("parallel", "arbitrary")),
    )(xs, bm, cm, col_a, col_dt, rowf)


def ssd_chunk_bwd(xs, bm, cm, col_a, col_dt, rowf, hprev, dy, name):
    s, d_inner = xs.shape
    ln = CHUNK
    nc = s // ln
    ng, hpg = col_a.shape[0], col_a.shape[1]
    gw = d_inner // ng

    def kern(x_ref, b_ref, c_ref, ca_ref, cd_ref, rf_ref, hp_ref, dy_ref,
             dx_ref, db_ref, dc_ref, ddt_ref, da_ref, dh_scr):
        c = pl.program_id(1)

        @pl.when(c == 0)
        def _():
            dh_scr[...] = jnp.zeros_like(dh_scr)

        bb = b_ref[...].astype(BF16)
        cbf = c_ref[...].astype(BF16)
        cb = _dot_nt(cbf, bb)
        row, col = _iota2((ln, ln), 0), _iota2((ln, ln), 1)
        causal = row >= col
        tri_ge = (col >= row).astype(BF16)
        ones = jnp.ones((ln, LANES), BF16)
        lane = _iota2((1, LANES), 1)
        last_row = (_iota2((ln, 1), 0) == ln - 1).astype(F32)
        dcb = jnp.zeros((ln, ln), F32)
        d_b = jnp.zeros((ln, LANES), F32)
        d_c = jnp.zeros((ln, LANES), F32)
        dxs = [jnp.zeros((ln, LANES), F32) for _ in range(gw // LANES)]
        for r in range(hpg):
            j, hf = divmod(r, LANES // HEAD)
            mh = ((lane >= HEAD * hf) & (lane < HEAD * (hf + 1))).astype(F32)
            ac = ca_ref[r]
            dt = cd_ref[r]
            ar = rf_ref[pl.ds(r, 1), :]
            aend = rf_ref[pl.ds(4 + r, 1), :]
            dm = jnp.where(causal, jnp.exp(jnp.minimum(ac - ar, 0.0)), 0.0)
            m = cb * dm
            mb = m.astype(BF16)
            xp = x_ref[:, j * LANES:(j + 1) * LANES]
            xdt = xp * dt * mh
            xdtb = xdt.astype(BF16)
            dyp = dy_ref[:, j * LANES:(j + 1) * LANES] * mh
            dypb = dyp.astype(BF16)
            h = hp_ref[r]
            hb = h.astype(BF16)
            dh = dh_scr[r]
            dhb = dh.astype(BF16)
            e_in = jnp.exp(ac)
            dte = jnp.exp(aend - ac)
            eend = jnp.exp(aend)
            d_m = _dot_nt(dypb, xdtb)
            dcb = dcb + d_m * dm
            gm = d_m * m
            yoff_pre = _dot_nt(cbf, hb)
            bdh = _dot_nt(bb, dhb)
            dxdt = _dot_tn(mb, dypb) + bdh * dte
            t1 = _rowsum(xdt * bdh) * dte
            gh, gl = _split2(gm)
            dacum = (_rowsum(gm) - (_dot_tn(gh, ones) + _dot_tn(gl, ones))
                     + _rowsum(dyp * yoff_pre) * e_in - t1)
            end_term = _colsum(t1) + eend * jnp.sum(_colsum(dh * h), axis=1, keepdims=True)
            dacum = dacum + last_row * end_term
            da_ref[r] = _dot_x3_left(tri_ge, dacum)
            ddt_ref[r] = jnp.broadcast_to(_rowsum(dxdt * xp), (ln, LANES))
            dxs[j] = dxs[j] + dxdt * dt
            d_b = d_b + _dot((xdt * dte).astype(BF16), dhb)
            dye = (dyp * e_in).astype(BF16)
            d_c = d_c + _dot(dye, hb)
            dh_scr[r] = eend * dh + _dot_tn(dye, cbf)
        dcbb = dcb.astype(BF16)
        dc_ref[...] = d_c + _dot(dcbb, bb)
        db_ref[...] = d_b + _dot_tn(dcbb, cbf)
        for j in range(gw // LANES):
            dx_ref[:, j * LANES:(j + 1) * LANES] = dxs[j]

    rev = nc - 1
    colspec = pl.BlockSpec((None, hpg, ln, LANES), lambda g, c: (g, 0, rev - c, 0))
    return pl.pallas_call(
        kern,
        name=name,
        grid=(ng, nc),
        in_specs=[pl.BlockSpec((ln, gw), lambda g, c: (rev - c, g)),
                  pl.BlockSpec((ln, LANES), lambda g, c: (rev - c, g)),
                  pl.BlockSpec((ln, LANES), lambda g, c: (rev - c, g)),
                  colspec, colspec,
                  pl.BlockSpec((None, None, 8, LANES), lambda g, c: (g, rev - c, 0, 0)),
                  pl.BlockSpec((None, None, hpg, LANES, LANES), lambda g, c: (g, rev - c, 0, 0, 0)),
                  pl.BlockSpec((ln, gw), lambda g, c: (rev - c, g))],
        out_specs=[pl.BlockSpec((ln, gw), lambda g, c: (rev - c, g)),
                   pl.BlockSpec((ln, LANES), lambda g, c: (rev - c, g)),
                   pl.BlockSpec((ln, LANES), lambda g, c: (rev - c, g)),
                   colspec, colspec],
        out_shape=[jax.ShapeDtypeStruct((s, d_inner), F32),
                   jax.ShapeDtypeStruct(bm.shape, F32), jax.ShapeDtypeStruct(cm.shape, F32),
                   jax.ShapeDtypeStruct(col_a.shape, F32), jax.ShapeDtypeStruct(col_a.shape, F32)],
        scratch_shapes=[pltpu.VMEM((hpg, LANES, LANES), F32)],
        compiler_params=_params(("parallel", "arbitrary")),
    )(xs, bm, cm, col_a, col_dt, rowf, hprev, dy)


def gnorm_fwd(y, xs, z, dexp, gain, ngroups, name):
    c = y.shape[1]
    gw = c // ngroups

    def fn(yv, xv, zv, dv, gv):
        yg = (yv + xv * dv) * (zv * _sigmoid(zv))
        outs = []
        for k in range(ngroups):
            t = yg[:, k * gw:(k + 1) * gw]
            outs.append(t * lax.rsqrt(jnp.mean(t * t, axis=1, keepdims=True) + EPS))
        return (jnp.concatenate(outs, axis=1) * gv,)

    return rowwise(fn, [(y, "row"), (xs, "row"), (z, "row"), (dexp, "full"), (gain, "full")], [(c, BF16)], tr=256, name=name)[0]


def gnorm_bwd(dn, y, xs, z, dexp, gain, ngroups, name):
    c = y.shape[1]
    gw = c // ngroups

    def fn(dnv, yv, xv, zv, dv, gv):
        yd = yv + xv * dv
        sg = _sigmoid(zv)
        sz = zv * sg
        yg = yd * sz
        dng = dnv * gv
        dyg, yh = [], []
        for k in range(ngroups):
            sl = slice(k * gw, (k + 1) * gw)
            t = yg[:, sl]
            r = lax.rsqrt(jnp.mean(t * t, axis=1, keepdims=True) + EPS)
            th = t * r
            dyg.append(r * (dng[:, sl] - th * jnp.mean(dng[:, sl] * th, axis=1, keepdims=True)))
            yh.append(th)
        dyg = jnp.concatenate(dyg, axis=1)
        yh = jnp.concatenate(yh, axis=1)
        dyd = dyg * sz
        dz = dyg * yd * (sg * (1.0 + zv * (1.0 - sg)))
        return dyd, dyd * dv, dz, _colsum(dyd * xv), _colsum(dnv * yh)

    return rowwise(fn, [(dn, "row"), (y, "row"), (xs, "row"), (z, "row"), (dexp, "full"), (gain, "full")],
                   [(c, F32), (c, F32), (c, BF16)], [(1, c), (1, c)], tr=256, name=name)


def ssd_post(ddt, da, dt, dtr, bias, alog, name):
    def fn(ddtv, dav, dtv, dtrv, bv, al):
        a_neg = -jnp.exp(al)
        ddtr = (ddtv + dav * a_neg) * _sigmoid(dtrv + bv)
        return ddtr, _colsum(ddtr), _colsum(dav * dtv) * a_neg

    return rowwise(fn, [(ddt, "row"), (da, "row"), (dt, "row"), (dtr, "row"), (bias, "full"), (alog, "full")],
                   [(LANES, BF16)], [(1, LANES), (1, LANES)], tr=512, name=name)


def _from_colform(v, s):
    ng, hpg = v.shape[0], v.shape[1]
    flat = v[..., 0].reshape(ng * hpg, s).T
    return jnp.pad(flat, ((0, 0), (0, LANES - ng * hpg)))


def ssm_fwd(x, g, p, tag):
    ng, hpg, d_inner = p["ng"], p["hpg"], p["d_inner"]
    h = rms_fwd(x, g, f"ssm_rms_{tag}")
    z = mm(h, p["w_z"], name=f"ssm_inz_{tag}")
    xbc = mm(h, p["w_xbc"], name=f"ssm_inx_{tag}")
    dtr = mm(h, p["w_dt"], name=f"ssm_indt_{tag}")
    xs, bm, cm = conv_fwd(xbc, p["conv_w"], p["conv_b"], d_inner, f"ssm_conv_{tag}")
    dt, acum = ssd_pre(dtr, p["dt_bias"], p["a_log"], f"ssm_pre_{tag}")
    col_a, col_dt = _ssd_layouts(acum, ng, hpg), _ssd_layouts(dt, ng, hpg)
    rowf = _ssd_rowform(acum, ng, hpg)
    y, hprev = ssd_chunk_fwd(xs, bm, cm, col_a, col_dt, rowf, f"ssm_scan_{tag}")
    n = gnorm_fwd(y, xs, z, p["d_exp"], p["norm_gain"], ng, f"ssm_gnorm_{tag}")
    xn = mm(n, p["w_out"], add=x, name=f"ssm_out_{tag}")
    return xn, (x, h, z, xbc, dtr, xs, bm, cm, dt, col_a, col_dt, rowf, y, hprev, n)


def ssm_bwd(dxn, saved, g, p, tag):
    x, h, z, xbc, dtr, xs, bm, cm, dt, col_a, col_dt, rowf, y, hprev, n = saved
    ng, hpg, d_inner = p["ng"], p["hpg"], p["d_inner"]
    s = x.shape[0]
    dxn, dxb = dxn
    dn = mm(dxb, p["w_out"], tb=True, name=f"ssm_dn_{tag}")
    dwout = mm(n, dxb, ta=True, name=f"ssm_dwout_{tag}")
    dy, dxs_skip, dz, dd_lane, dgain = gnorm_bwd(dn, y, xs, z, p["d_exp"], p["norm_gain"], ng, f"ssm_dgnorm_{tag}")
    dxs, dbm, dcm, ddt_c, da_c = ssd_chunk_bwd(xs, bm, cm, col_a, col_dt, rowf, hprev, dy, f"ssm_dscan_{tag}")
    ddtr, dbias, dalog = ssd_post(_from_colform(ddt_c, s), _from_colform(da_c, s), dt, dtr,
                                  p["dt_bias"], p["a_log"], f"ssm_post_{tag}")
    res = conv_bwd_pre(xbc, p["conv_w"], p["conv_b"], dxs, dxs_skip, dbm, dcm, f"ssm_dconv_{tag}")
    dpre, dconv_w, dconv_b = res[0], jnp.concatenate(res[1:5], axis=0), res[5]
    dxbc = conv_bwd_in(dpre, p["conv_w"], f"ssm_dconvin_{tag}")
    dh = mm(dz, p["w_z"], tb=True, name=f"ssm_dhz_{tag}")
    dh = mm(dxbc, p["w_xbc"], tb=True, add=dh, name=f"ssm_dhx_{tag}")
    dh = mm(ddtr, p["w_dt"], tb=True, add=dh, name=f"ssm_dhdt_{tag}")
    dwz = mm(h, dz, ta=True, name=f"ssm_dwz_{tag}")
    dwxbc = mm(h, dxbc, ta=True, name=f"ssm_dwxbc_{tag}")
    dwdt = mm(h, ddtr, ta=True, name=f"ssm_dwdt_{tag}")
    dx, dg = rms_bwd(x, g, dh, dxn, f"ssm_drms_{tag}")
    nh = ng * hpg
    dwin = jnp.concatenate([dwz, dwxbc, dwdt[:, :nh]], axis=1)
    dd = dd_lane.reshape(nh, HEAD).sum(-1)
    return dx, dg, dict(w_in=dwin, conv_w=dconv_w, conv_b=dconv_b, dt_bias=dbias[0, :nh], a_log=dalog[0, :nh],
                        d=dd, norm_gain=dgain, w_out=dwout)


def local_step(x, target, w):
    d = x.shape[1]
    depth = w["mix_norm"].shape[0]
    bd = _head_blockdiag(d)
    tril = jnp.tril(jnp.ones((CHUNK, CHUNK), bool))
    ssm_heads = w["ssm_dt_bias"].shape[1]
    d_inner = w["ssm_norm_gain"].shape[1]
    ng = w["ssm_norm_gain"].shape[1] // 256
    nstate = CHUNK

    def pad_lanes(v):
        return jnp.pad(v, ((0, 0), (0, LANES - v.shape[1])))

    def ssm_params(j):
        w_in = w["ssm_w_in"][j]
        cw = w["ssm_conv_w"][j]
        return dict(ng=ng, hpg=ssm_heads // ng, d_inner=d_inner,
                    w_z=w_in[:, :d_inner], w_xbc=w_in[:, d_inner:d_inner + d_inner + 2 * ng * nstate],
                    w_dt=pad_lanes(w_in[:, 2 * d_inner + 2 * ng * nstate:]),
                    conv_w=[cw[k:k + 1] for k in range(cw.shape[0])], conv_b=w["ssm_conv_b"][j:j + 1],
                    dt_bias=pad_lanes(w["ssm_dt_bias"][j:j + 1]), a_log=pad_lanes(w["ssm_a_log"][j:j + 1]),
                    d_exp=jnp.repeat(w["ssm_d"][j], HEAD)[None, :], norm_gain=w["ssm_norm_gain"][j:j + 1],
                    w_out=w["ssm_w_out"][j])

    def gm_params(j):
        wc = jnp.where(tril, w["gm_w_s"][j], 0.0).astype(BF16)
        bst = jnp.repeat(w["gm_b_s"][j].T, LANES, axis=1)
        return wc, bst

    def sb_gains(j):
        nh = d // HEAD
        return jnp.tile(w["sb_q_gain"][j], nh)[None, :], jnp.tile(w["sb_k_gain"][j], nh)[None, :]

    saved = []
    cur = x
    for i in range(depth):
        kind, j = i % 3, i // 3
        gmix = w["mix_norm"][i:i + 1]
        if kind == 0:
            qg, kg = sb_gains(j)
            cur, sv = sb_fwd(cur, gmix, w["sb_w_qkv"][j], qg, kg, w["sb_w_o"][j], bd, f"{i}")
        elif kind == 1:
            wc, bst = gm_params(j)
            cur, sv = gm_fwd(cur, gmix, w["gm_w_in"][j], w["gm_b_in"][j:j + 1], w["gm_v_gain"][j:j + 1], wc, bst,
                             w["gm_w_out"][j], f"{i}")
        else:
            cur, sv = ssm_fwd(cur, gmix, ssm_params(j), f"{i}")
        cur, sv2 = ffn_fwd(cur, w["ffn_norm"][i:i + 1], w["ffn_w_gu"][i], w["ffn_w_down"][i], f"{i}")
        saved.append((sv, sv2))

    loss, dcur = loss_and_grad(cur, target, "loss")

    grads = {k: [None] * len(v) for k, v in w.items()}
    for i in reversed(range(depth)):
        kind, j = i % 3, i // 3
        sv, sv2 = saved[i]
        gmix = w["mix_norm"][i:i + 1]
        dcur, dgf, dwgu, dwdown = ffn_bwd(dcur, sv2, w["ffn_norm"][i:i + 1], w["ffn_w_gu"][i], w["ffn_w_down"][i], f"{i}")
        grads["ffn_norm"][i], grads["ffn_w_gu"][i], grads["ffn_w_down"][i] = dgf[0], dwgu, dwdown
        if kind == 0:
            qg, kg = sb_gains(j)
            dcur, dg, dwqkv, dqg, dkg, dwo = sb_bwd(dcur, sv, gmix, w["sb_w_qkv"][j], qg, kg, w["sb_w_o"][j], bd, f"{i}")
            grads["sb_w_qkv"][j], grads["sb_q_gain"][j], grads["sb_k_gain"][j], grads["sb_w_o"][j] = dwqkv, dqg, dkg, dwo
        elif kind == 1:
            wc, bst = gm_params(j)
            dcur, dg, dwin, dbin, dvg, dws, dbs, dwout = gm_bwd(dcur, sv, gmix, w["gm_w_in"][j], w["gm_v_gain"][j:j + 1],
                                                                 wc, bst, w["gm_w_out"][j], f"{i}")
            grads["gm_w_in"][j], grads["gm_b_in"][j], grads["gm_v_gain"][j] = dwin, dbin[0], dvg[0]
            grads["gm_w_s"][j], grads["gm_b_s"][j], grads["gm_w_out"][j] = dws, dbs, dwout
        else:
            dcur, dg, gs = ssm_bwd(dcur, sv, gmix, ssm_params(j), f"{i}")
            grads["ssm_w_in"][j], grads["ssm_conv_w"][j], grads["ssm_conv_b"][j] = gs["w_in"], gs["conv_w"], gs["conv_b"][0]
            grads["ssm_dt_bias"][j], grads["ssm_a_log"][j], grads["ssm_d"][j] = gs["dt_bias"], gs["a_log"], gs["d"]
            grads["ssm_norm_gain"][j], grads["ssm_w_out"][j] = gs["norm_gain"][0], gs["w_out"]
        grads["mix_norm"][i] = dg[0]
    grads = {k: (v if k in MATRICES else jnp.stack(v)) for k, v in grads.items()}
    return loss, dcur[0], grads


WEIGHTS = ["mix_norm", "ffn_norm", "sb_w_qkv", "sb_q_gain", "sb_k_gain", "sb_w_o", "gm_w_in", "gm_b_in", "gm_v_gain",
           "gm_w_s", "gm_b_s", "gm_w_out", "ssm_w_in", "ssm_conv_w", "ssm_conv_b", "ssm_dt_bias", "ssm_a_log", "ssm_d",
           "ssm_norm_gain", "ssm_w_out", "ffn_w_gu", "ffn_w_down"]
SHARDED = {"sb_w_qkv": 2, "sb_w_o": 1, "gm_w_in": 2, "gm_w_out": 1, "ssm_w_in": 2, "ssm_conv_w": 2, "ssm_conv_b": 1,
           "ssm_norm_gain": 1, "ssm_w_out": 1, "ffn_w_gu": 2, "ffn_w_down": 1}
EXACT = ("ssm_conv_w", "ssm_conv_b", "ssm_norm_gain")
MATRICES = tuple(n for n in SHARDED if n not in EXACT)
COLUMN_BLOCKS = ("sb_w_qkv", "gm_w_in", "ffn_w_gu")
REPLICATED = [n for n in WEIGHTS if n not in SHARDED]
N_CHIPS = 4
N_DEV = 8
PACK_COLS = 1024


def _pack(pieces, dtype, align):
    flat = jnp.concatenate([p.reshape(-1).astype(dtype) for p in pieces])
    rows = -(-flat.shape[0] // (PACK_COLS * align)) * align
    flat = jnp.pad(flat, (0, rows * PACK_COLS - flat.shape[0]))
    return flat.reshape(rows, PACK_COLS)


def _unpack(flat, shapes):
    out, off = [], 0
    for shp in shapes:
        n = math.prod(shp)
        out.append(flat[off:off + n].reshape(shp))
        off += n
    return out


ANY = pl.BlockSpec(memory_space=pl.ANY)


def _pos():
    return lax.axis_index("x"), lax.axis_index("y"), lax.axis_index("c")


def _remote(src, dst, send, recv, k, to):
    return pltpu.make_async_remote_copy(src_ref=src, dst_ref=dst, send_sem=send.at[k], recv_sem=recv.at[k],
                                        device_id=to, device_id_type=MESH_ID)


def _comm_call(body, name, ins, out_shapes, nsem, aliases=None):
    return pl.pallas_call(
        body, name=name, out_shape=out_shapes,
        in_specs=[ANY] * len(ins), out_specs=[ANY] * len(out_shapes),
        scratch_shapes=[pltpu.SemaphoreType.DMA((nsem,)), pltpu.SemaphoreType.DMA((nsem,))],
        input_output_aliases=aliases or {},
    )(*ins)


def stage_shard(w, chip, name):
    rows, cols = w.shape
    tr = _pick(rows, (256, 352, 128))

    def kern(idx_ref, w_ref, o_ref):
        o_ref[...] = w_ref[...].astype(BF16)

    grid_spec = pltpu.PrefetchScalarGridSpec(
        num_scalar_prefetch=1, grid=(rows // tr,),
        in_specs=[pl.BlockSpec((tr, cols), lambda i, idx: (i, 0))],
        out_specs=pl.BlockSpec((None, tr, cols), lambda i, idx: (idx[0], i, 0)))
    return pl.pallas_call(
        kern, name=name, grid_spec=grid_spec,
        out_shape=jax.ShapeDtypeStruct((N_CHIPS, rows, cols), BF16),
        compiler_params=_params(("parallel",)),
    )(jnp.reshape(chip, (1,)).astype(jnp.int32), w)


def gather_weights(staged, name):
    n = len(staged)

    def body(*refs):
        o_refs = refs[n:2 * n]
        send, recv = refs[2 * n:]
        x, y, c = _pos()
        me, sibling = (x, y, c), (x, y, 1 - c)
        chips = [(1 - x, y), (x, 1 - y), (1 - x, 1 - y)]

        def part(u, chip, cc):
            half = staged[u].shape[1] // 2
            return o_refs[u].at[2 * chip[0] + chip[1], pl.ds(cc * half, half), :]

        first = []
        for u in range(n):
            first += [_remote(part(u, (x, y), c), part(u, (x, y), c), send, recv, 6 * u + j, (*chip, c))
                      for j, chip in enumerate(chips)]
        for cp in first:
            cp.start()
        passed = []
        for u in range(n):
            for j, chip in enumerate(chips):
                _remote(part(u, chip, c), part(u, chip, c), send, recv, 6 * u + j, me).wait_recv()
                fw = _remote(part(u, chip, c), part(u, chip, c), send, recv, 6 * u + 3 + j, sibling)
                fw.start()
                passed.append(fw)
        for u in range(n):
            for j, chip in enumerate(chips):
                _remote(part(u, chip, 1 - c), part(u, chip, 1 - c), send, recv, 6 * u + 3 + j, me).wait_recv()
        for cp in first + passed:
            cp.wait_send()

    outs = [jax.ShapeDtypeStruct(s.shape, s.dtype) for s in staged]
    return _comm_call(body, name, staged, outs, 6 * n, aliases={u: u for u in range(n)})


def swap_halves(gps):
    n = len(gps)

    def body(*refs):
        g_refs, r_refs = refs[:n], refs[n:2 * n]
        send, recv = refs[2 * n:]
        x, y, c = _pos()
        cps = []
        for u in range(n):
            half = gps[u].shape[1] // 2
            cps.append(_remote(g_refs[u].at[:, pl.ds((1 - c) * half, half), :], r_refs[u], send, recv, u, (x, y, 1 - c)))
        for cp in cps:
            cp.start()
        for cp in cps:
            cp.wait()

    outs = [jax.ShapeDtypeStruct((g.shape[0], g.shape[1] // 2, g.shape[2]), g.dtype) for g in gps]
    return _comm_call(body, "swap_halves", gps, outs, n)


def scatter_chunks(parts):
    n = len(parts)

    def body(*refs):
        p_refs, r_refs = refs[:n], refs[n:2 * n]
        send, recv = refs[2 * n:]
        x, y, c = _pos()
        chips = [(1 - x, y), (x, 1 - y), (1 - x, 1 - y)]
        cps = [_remote(p_refs[u].at[2 * chip[0] + chip[1]], r_refs[u].at[j], send, recv, 3 * u + j, (*chip, c))
               for u in range(n) for j, chip in enumerate(chips)]
        for cp in cps:
            cp.start()
        for cp in cps:
            cp.wait()

    outs = [jax.ShapeDtypeStruct((N_CHIPS - 1,) + p.shape[1:], p.dtype) for p in parts]
    return _comm_call(body, "scatter_chunks", parts, outs, 3 * n)


def join_halves(bufs):
    n = len(bufs)

    def body(*refs):
        o_refs = refs[n:2 * n]
        send, recv = refs[2 * n:]
        x, y, c = _pos()

        def rows(u, cc):
            half = bufs[u].shape[0] // 2
            return o_refs[u].at[pl.ds(cc * half, half), :]

        cps = [_remote(rows(u, c), rows(u, c), send, recv, u, (x, y, 1 - c)) for u in range(n)]
        for cp in cps:
            cp.start()
        for u in range(n):
            _remote(rows(u, 1 - c), rows(u, 1 - c), send, recv, u, (x, y, c)).wait_recv()
        for cp in cps:
            cp.wait_send()

    outs = [jax.ShapeDtypeStruct(b.shape, b.dtype) for b in bufs]
    return _comm_call(body, "join_halves", bufs, outs, n, aliases={u: u for u in range(n)})


def gather_small(sg, name):
    rows, cols = sg.shape

    def body(s_ref, o_ref, send, recv, lsem):
        x, y, c = _pos()
        mine = pltpu.make_async_copy(s_ref, o_ref.at[4 * x + 2 * y + c], lsem)
        mine.start()
        peers = []
        for msk in range(1, N_DEV):
            px = 1 - x if msk & 4 else x
            py = 1 - y if msk & 2 else y
            pc = 1 - c if msk & 1 else c
            peers.append((px, py, pc))
        cps = [_remote(s_ref, o_ref.at[4 * x + 2 * y + c], send, recv, k, peer) for k, peer in enumerate(peers)]
        for cp in cps:
            cp.start()
        for k, (px, py, pc) in enumerate(peers):
            _remote(s_ref, o_ref.at[4 * px + 2 * py + pc], send, recv, k, (x, y, c)).wait_recv()
        for cp in cps:
            cp.wait_send()
        mine.wait()

    return pl.pallas_call(
        body, name=name,
        out_shape=jax.ShapeDtypeStruct((N_DEV, rows, cols), sg.dtype),
        in_specs=[ANY], out_specs=ANY,
        scratch_shapes=[pltpu.SemaphoreType.DMA((N_DEV - 1,)), pltpu.SemaphoreType.DMA((N_DEV - 1,)), pltpu.SemaphoreType.DMA],
    )(sg)


def sum_cores(gp, theirs, core, chip, name):
    nch, rows, cols = gp.shape
    half = rows // 2
    tr = _pick(half, (256, 176, 128, 64))
    nb = half // tr

    def kern(idx_ref, g_ref, t_ref, own_ref, all_ref):
        k = pl.program_id(1)
        s = g_ref[...] + t_ref[...]
        all_ref[...] = s.astype(BF16)

        @pl.when(k == idx_ref[1])
        def _():
            own_ref[...] = s

    grid_spec = pltpu.PrefetchScalarGridSpec(
        num_scalar_prefetch=1, grid=(nb, nch),
        in_specs=[pl.BlockSpec((None, tr, cols), lambda i, k, idx: (k, idx[0] * nb + i, 0)),
                  pl.BlockSpec((None, tr, cols), lambda i, k, idx: (k, i, 0))],
        out_specs=[pl.BlockSpec((tr, cols), lambda i, k, idx: (i, 0)),
                   pl.BlockSpec((None, tr, cols), lambda i, k, idx: (k, i, 0))])
    return pl.pallas_call(
        kern, name=name, grid_spec=grid_spec,
        out_shape=[jax.ShapeDtypeStruct((half, cols), F32), jax.ShapeDtypeStruct((nch, half, cols), BF16)],
        compiler_params=_params(("parallel", "arbitrary")),
    )(jnp.stack([core, chip]).astype(jnp.int32), gp, theirs)


def sum_chips(own, others, core, name):
    half, cols = own.shape
    tr = _pick(half, (256, 176, 128, 64))
    nb = half // tr

    def kern(idx_ref, o_ref, a_ref, b_ref, c_ref, out_ref):
        out_ref[...] = ((o_ref[...] + a_ref[...].astype(F32)) + b_ref[...].astype(F32)) + c_ref[...].astype(F32)

    grid_spec = pltpu.PrefetchScalarGridSpec(
        num_scalar_prefetch=1, grid=(nb,),
        in_specs=[pl.BlockSpec((tr, cols), lambda i, idx: (i, 0))] +
                 [pl.BlockSpec((None, tr, cols), lambda i, idx, j=j: (j, i, 0)) for j in range(N_CHIPS - 1)],
        out_specs=pl.BlockSpec((tr, cols), lambda i, idx: (idx[0] * nb + i, 0)))
    return pl.pallas_call(
        kern, name=name, grid_spec=grid_spec,
        out_shape=jax.ShapeDtypeStruct((2 * half, cols), F32),
        compiler_params=_params(("parallel",)),
    )(jnp.reshape(core, (1,)).astype(jnp.int32), own, others, others, others)


def small_update(gath, w, m, v, name):
    def fn(*vs):
        g = vs[0]
        for t in vs[1:N_DEV]:
            g = g + t
        wv, mv, vv = vs[N_DEV:]
        m2 = ADAM_B1 * mv + (1.0 - ADAM_B1) * g
        v2 = ADAM_B2 * vv + (1.0 - ADAM_B2) * (g * g)
        m_hat = m2 / (1.0 - ADAM_B1 ** ADAM_STEP)
        v_hat = v2 / (1.0 - ADAM_B2 ** ADAM_STEP)
        return g, -ADAM_LR * (m_hat / (jnp.sqrt(v_hat) + ADAM_EPS) + ADAM_WD * wv), m2, v2

    c = w.shape[1]
    ins = [(gath[k], "row") for k in range(N_DEV)] + [(w, "row"), (m, "row"), (v, "row")]
    return rowwise(fn, ins, [(c, F32)] * 4, tr=w.shape[0] // 2, name=name)


def _step(ins):
    x, target = ins["x"][0], ins["loss_target"][0]
    core = lax.axis_index("c")
    chip = 2 * lax.axis_index("x") + lax.axis_index("y")
    units = [(n, l) for n in MATRICES for l in range(ins[n].shape[0])]

    def lane_pad(v):
        return jnp.pad(v, ((0, 0), (0, PACK_COLS - v.shape[1])))

    vec_rows = [ins["ssm_conv_w"][0], ins["ssm_conv_b"], lane_pad(ins["ssm_norm_gain"])]
    blk = jnp.concatenate(vec_rows + [jnp.zeros((SUBLANES - 6, PACK_COLS), F32)], axis=0)
    per_chip = gather_small(blk, "gather_vectors")[0::2]
    ngw = ins["ssm_norm_gain"].shape[1]
    full = {
        "ssm_conv_w": jnp.concatenate([per_chip[k, 0:4] for k in range(N_CHIPS)], axis=1)[None],
        "ssm_conv_b": jnp.concatenate([per_chip[k, 4:5] for k in range(N_CHIPS)], axis=1),
        "ssm_norm_gain": jnp.concatenate([per_chip[k, 5:6, :ngw] for k in range(N_CHIPS)], axis=1),
    }

    gathered = gather_weights([stage_shard(ins[n][l], chip, f"stage_{n}_{l}") for n, l in units], "gather_weights")
    for n in MATRICES:
        full[n] = []
    for (n, l), g in zip(units, gathered):
        if n in COLUMN_BLOCKS:
            full[n].append(g)
        elif n == "ssm_w_in":
            full[n].append(jnp.concatenate([g[k] for k in range(N_CHIPS)], axis=1))
        else:
            full[n].append(g.reshape(-1, g.shape[-1]))
    for n in REPLICATED:
        full[n] = ins[n]

    loss, dx, grads = local_step(x, target, full)
    loss = lax.psum(loss, ALL_AXES)

    gps = []
    for n, l in units:
        g = grads[n][l]
        if n in COLUMN_BLOCKS:
            gps.append(g)
        elif n == "ssm_w_in":
            gps.append(jnp.stack(jnp.split(g, N_CHIPS, axis=1)))
        else:
            gps.append(g.reshape(N_CHIPS, -1, g.shape[-1]))
    theirs = swap_halves(gps)
    owns, alls = [], []
    for (n, l), g, t in zip(units, gps, theirs):
        own, every = sum_cores(g, t, core, chip, f"sum_cores_{n}_{l}")
        owns.append(own)
        alls.append(every)
    others = scatter_chunks(alls)
    ghalves = [sum_chips(o, ot, core, f"sum_chips_{n}_{l}") for (n, l), o, ot in zip(units, owns, others)]
    gshards = dict(zip(units, join_halves(ghalves)))

    small_shapes = [ins[n].shape for n in REPLICATED]
    vec_shapes = [grads[n].shape for n in EXACT]
    vec_pack = _pack([grads[n] for n in EXACT], F32, SUBLANES)
    gath = gather_small(jnp.concatenate([_pack([grads[n] for n in REPLICATED], F32, SUBLANES), vec_pack], axis=0),
                        "gather_small")
    packed = [jnp.concatenate([_pack([ins[pre + n] for n in REPLICATED], F32, SUBLANES), jnp.zeros_like(vec_pack)], axis=0)
              for pre in ("", "m_", "v_")]
    res = small_update(gath, *packed, name="small_update")
    nrep = res[0].shape[0] - vec_pack.shape[0]
    small = [dict(zip(REPLICATED, _unpack(r[:nrep].reshape(-1), small_shapes))) for r in res]
    vec_g = dict(zip(EXACT, _unpack(res[0][nrep:].reshape(-1), vec_shapes)))

    out_g, out_d, out_m, out_v = {}, {}, {}, {}
    for n in REPLICATED:
        out_g[n], out_d[n], out_m[n], out_v[n] = (s[n] for s in small)
    for n in SHARDED:
        shp = ins[n].shape
        if n in EXACT:
            g = lax.dynamic_slice_in_dim(vec_g[n], chip * shp[-1], shp[-1], axis=vec_g[n].ndim - 1)
        else:
            g = jnp.stack([gshards[(n, l)] for l in range(shp[0])])
        two = (math.prod(shp[:-1]), shp[-1])
        d2, m2, v2 = adamw(ins[n].reshape(two), g.reshape(two), ins["m_" + n].reshape(two),
                           ins["v_" + n].reshape(two), f"adamw_{n}")
        out_g[n], out_d[n], out_m[n], out_v[n] = g, d2.reshape(shp), m2.reshape(shp), v2.reshape(shp)
    return (loss, dx[None], *[out_g[n] for n in WEIGHTS], *[out_d[n] for n in WEIGHTS],
            *[out_m[n] for n in WEIGHTS], *[out_v[n] for n in WEIGHTS])


def kernel(x, mix_norm, ffn_norm, sb_w_qkv, sb_q_gain, sb_k_gain, sb_w_o, gm_w_in, gm_b_in, gm_v_gain, gm_w_s, gm_b_s, gm_w_out, ssm_w_in, ssm_conv_w, ssm_conv_b, ssm_dt_bias, ssm_a_log, ssm_d, ssm_norm_gain, ssm_w_out, ffn_w_gu, ffn_w_down, loss_target, m_mix_norm, m_ffn_norm, m_sb_w_qkv, m_sb_q_gain, m_sb_k_gain, m_sb_w_o, m_gm_w_in, m_gm_b_in, m_gm_v_gain, m_gm_w_s, m_gm_b_s, m_gm_w_out, m_ssm_w_in, m_ssm_conv_w, m_ssm_conv_b, m_ssm_dt_bias, m_ssm_a_log, m_ssm_d, m_ssm_norm_gain, m_ssm_w_out, m_ffn_w_gu, m_ffn_w_down, v_mix_norm, v_ffn_norm, v_sb_w_qkv, v_sb_q_gain, v_sb_k_gain, v_sb_w_o, v_gm_w_in, v_gm_b_in, v_gm_v_gain, v_gm_w_s, v_gm_b_s, v_gm_w_out, v_ssm_w_in, v_ssm_conv_w, v_ssm_conv_b, v_ssm_dt_bias, v_ssm_a_log, v_ssm_d, v_ssm_norm_gain, v_ssm_w_out, v_ffn_w_gu, v_ffn_w_down):
    return _step(dict(locals()))
```

```python
import functools
import math

import jax
import jax.numpy as jnp
from jax import lax
from jax.experimental import pallas as pl
from jax.experimental.pallas import tpu as pltpu

F32 = jnp.float32
BF16 = jnp.bfloat16
EPS = 1e-6
LANES = 128
SUBLANES = 8
VMEM_LIMIT = 56 * 1024 * 1024
HEAD = 64
CHUNK = 128
SB_TQ, SB_TK = 256, 256
SB_DEAD = -110.0
SB_UNSEEN = -1e30
ADAM_LR, ADAM_B1, ADAM_B2, ADAM_EPS, ADAM_WD, ADAM_STEP = 0.001, 0.9, 0.999, 1e-08, 0.01, 10
MESH_ID = pl.DeviceIdType.MESH
ALL_AXES = ("x", "y", "c")


def _params(sem):
    return pltpu.CompilerParams(dimension_semantics=sem, vmem_limit_bytes=VMEM_LIMIT)


def _pick(n, cands):
    for c in cands:
        if n % c == 0:
            return c
    return n


def _dot(a, b, dims=((1,), (0,))):
    return lax.dot_general(a, b, (dims, ((), ())), preferred_element_type=F32)


def _dot_nt(a, b):
    return _dot(a, b, ((1,), (1,)))


def _dot_tn(a, b):
    return _dot(a, b, ((0,), (0,)))


def _split2(x):
    hi = x.astype(BF16)
    lo = (x - hi.astype(F32)).astype(BF16)
    return hi, lo


def _dot_x2(x, m):
    hi, lo = _split2(x)
    return _dot(hi, m) + _dot(lo, m)


def _dot_x3_left(m, x):
    h1 = x.astype(BF16)
    r1 = x - h1.astype(F32)
    h2 = r1.astype(BF16)
    h3 = (r1 - h2.astype(F32)).astype(BF16)
    return _dot(m, h1) + _dot(m, h2) + _dot(m, h3)


def _sigmoid(x):
    return 1.0 / (1.0 + jnp.exp(-x))


def _softplus(x):
    return jnp.maximum(x, 0.0) + jnp.log(1.0 + jnp.exp(-jnp.abs(x)))


def _colsum(x):
    return jnp.sum(x, axis=0, keepdims=True)


def _rowsum(x):
    return jnp.sum(x, axis=1, keepdims=True)


def _iota2(shape, dim):
    return lax.broadcasted_iota(jnp.int32, shape, dim)


MM_VMEM_BUDGET = 40 * 1024 * 1024
MM_STEP_US = 0.35
MM_HBM_BYTES_PER_US = 3.0e6
MM_VMEM_BYTES_PER_US = 1.5e6
MM_FLOPS_PER_US = 9.0e8
MXU_DIM = 256


def _mm_tiles(m, n, kk, wn, wk, a_bytes, b_bytes, has_add):
    def divisors(total, cands):
        got = [c for c in cands if total % c == 0 and c <= total]
        return got or [total]

    best = None
    for tm in divisors(m, (1024, 512, 256, 128)):
        for tn in divisors(wn, (1024, 768, 1408, 512, 256, 128)):
            for tk in divisors(wk, (4096, 2816, 2048, 1408, 1024, 768, 512, 256, 128)):
                nk = kk // tk
                vmem = 2 * (tm * tk * a_bytes + tk * tn * b_bytes + tm * tn * 4 * (2 if has_add else 1))
                vmem += tm * tn * 4 if nk > 1 else 0
                if vmem > MM_VMEM_BUDGET:
                    continue
                steps = (m // tm) * (n // tn) * nk
                a_reads = 1 if nk == 1 else n // tn
                traffic = m * kk * a_bytes * a_reads + kk * n * b_bytes * (m // tm) + m * n * 4
                fill = min(1.0, tn / MXU_DIM) * min(1.0, tm / MXU_DIM)
                compute = 2.0 * m * n * kk / (MM_FLOPS_PER_US * fill)
                cost = steps * MM_STEP_US + max(compute, traffic / MM_HBM_BYTES_PER_US)
                if nk > 1:
                    cost += steps * tm * tn * 8 / MM_VMEM_BYTES_PER_US
                if best is None or cost < best[0]:
                    best = (cost, tm, tn, tk)
    return best[1:]


def mm(a, b, *, ta=False, tb=False, add=None, bias=None, b_chunks=False, out_chunks=False, name):
    if ta:
        kk, m = a.shape
    else:
        m, kk = a.shape
    nch, wide = 1, None
    if b_chunks:
        nch, rows_b, wide = b.shape
        kb, n = (rows_b, nch * wide) if not tb else (nch * wide, rows_b)
    elif tb:
        n, kb = b.shape
    else:
        kb, n = b.shape
    if out_chunks:
        nch, wide = N_CHIPS, n // N_CHIPS
    assert kk == kb, (a.shape, b.shape, ta, tb)
    has_add, has_bias = add is not None, bias is not None
    tm, tn, tk = _mm_tiles(m, n, kk, wide if (wide and not tb) or out_chunks else n, wide if (wide and tb) else kk,
                           a.dtype.itemsize, b.dtype.itemsize, has_add)
    nk = kk // tk
    dims = ((0 if ta else 1,), (1 if tb else 0,))

    def kern(*refs):
        a_ref, b_ref = refs[0], refs[1]
        rest = list(refs[2:])
        add_ref = rest.pop(0) if has_add else None
        bias_ref = rest.pop(0) if has_bias else None
        o_ref = rest[0]
        part = _dot(a_ref[...].astype(BF16), b_ref[...].astype(BF16), dims)

        def finish(r):
            if has_add:
                r = r + add_ref[...]
            if has_bias:
                r = r + bias_ref[...]
            o_ref[...] = r

        if nk == 1:
            finish(part)
        else:
            acc_ref = rest[1]
            k = pl.program_id(2)

            @pl.when(k == 0)
            def _():
                acc_ref[...] = part

            @pl.when((k > 0) & (k < nk - 1))
            def _():
                acc_ref[...] += part

            @pl.when(k == nk - 1)
            def _():
                finish(acc_ref[...] + part)

    a_spec = pl.BlockSpec((tk, tm), lambda i, j, k: (k, i)) if ta else pl.BlockSpec((tm, tk), lambda i, j, k: (i, k))
    if b_chunks and tb:
        per = wide // tk
        b_spec = pl.BlockSpec((None, tn, tk), lambda i, j, k: (k // per, j, k % per))
    elif b_chunks:
        per = wide // tn
        b_spec = pl.BlockSpec((None, tk, tn), lambda i, j, k: (j // per, k, j % per))
    elif tb:
        b_spec = pl.BlockSpec((tn, tk), lambda i, j, k: (j, k))
    else:
        b_spec = pl.BlockSpec((tk, tn), lambda i, j, k: (k, j))
    if out_chunks:
        per_o = wide // tn
        out_spec = pl.BlockSpec((None, tm, tn), lambda i, j, k: (j // per_o, i, j % per_o))
        out_shape = jax.ShapeDtypeStruct((nch, m, wide), F32)
    else:
        out_spec = pl.BlockSpec((tm, tn), lambda i, j, k: (i, j))
        out_shape = jax.ShapeDtypeStruct((m, n), F32)
    in_specs, args = [a_spec, b_spec], [a, b]
    if has_add:
        in_specs.append(pl.BlockSpec((tm, tn), lambda i, j, k: (i, j)))
        args.append(add)
    if has_bias:
        in_specs.append(pl.BlockSpec((1, tn), lambda i, j, k: (0, j)))
        args.append(bias)
    return pl.pallas_call(
        kern,
        name=name,
        grid=(m // tm, n // tn, nk),
        in_specs=in_specs,
        out_specs=out_spec,
        out_shape=out_shape,
        scratch_shapes=[pltpu.VMEM((tm, tn), F32)] if nk > 1 else [],
        compiler_params=_params(("parallel", "parallel", "arbitrary")),
    )(*args)


def rowwise(fn, ins, outs, accs=(), *, tr, name):
    rows = [a for a, kind in ins if kind == "row"][0].shape[0]
    tr = min(tr, rows)
    assert rows % tr == 0 and tr % SUBLANES == 0, (rows, tr)
    n = rows // tr
    n_in, n_out = len(ins), len(outs)
    kinds = [kind for _, kind in ins]

    def kern(*refs):
        i = pl.program_id(0)
        vals = []
        for ref, kind in zip(refs[:n_in], kinds):
            v = ref[...]
            if kind == "prev":
                v = v * (i > 0).astype(v.dtype)
            elif kind == "next":
                v = v * (i < n - 1).astype(v.dtype)
            vals.append(v)
        res = fn(*vals)
        for ref, r in zip(refs[n_in:n_in + n_out], res[:n_out]):
            ref[...] = r.astype(ref.dtype)
        if accs:
            acc_refs = refs[n_in + n_out:]

            @pl.when(i == 0)
            def _():
                for ref in acc_refs:
                    ref[...] = jnp.zeros_like(ref)

            for ref, r in zip(acc_refs, res[n_out:]):
                ref[...] += r

    in_specs = []
    for a, kind in ins:
        if kind == "row":
            in_specs.append(pl.BlockSpec((tr, a.shape[1]), lambda i: (i, 0)))
        elif kind == "full":
            in_specs.append(pl.BlockSpec(a.shape, lambda i, nd=a.ndim: (0,) * nd))
        elif kind == "prev":
            in_specs.append(pl.BlockSpec((SUBLANES, a.shape[1]),
                                         lambda i: (jnp.maximum(i * (tr // SUBLANES) - 1, 0), 0)))
        else:
            in_specs.append(pl.BlockSpec((SUBLANES, a.shape[1]),
                                         lambda i: (jnp.minimum((i + 1) * (tr // SUBLANES), rows // SUBLANES - 1), 0)))
    out_specs = [pl.BlockSpec((tr, c), lambda i: (i, 0)) for c, _ in outs]
    out_specs += [pl.BlockSpec((r, c), lambda i: (0, 0)) for r, c in accs]
    out_shape = [jax.ShapeDtypeStruct((rows, c), dt) for c, dt in outs]
    out_shape += [jax.ShapeDtypeStruct((r, c), F32) for r, c in accs]
    res = pl.pallas_call(
        kern,
        name=name,
        grid=(n,),
        in_specs=in_specs,
        out_specs=out_specs,
        out_shape=out_shape,
        compiler_params=_params(("arbitrary",) if accs else ("parallel",)),
    )(*[a for a, _ in ins])
    return res


def rms_fwd(x, g, name):
    def fn(xv, gv):
        r = lax.rsqrt(jnp.mean(xv * xv, axis=1, keepdims=True) + EPS)
        return (xv * r * gv,)

    return rowwise(fn, [(x, "row"), (g, "full")], [(x.shape[1], BF16)], tr=512, name=name)[0]


def rms_bwd(x, g, dy, dres, name):
    def fn(xv, gv, dyv, drv):
        r = lax.rsqrt(jnp.mean(xv * xv, axis=1, keepdims=True) + EPS)
        xh = xv * r
        dyg = dyv * gv
        dx = drv + r * (dyg - xh * jnp.mean(dyg * xh, axis=1, keepdims=True))
        return dx, dx, _colsum(dyv * xh)

    c = x.shape[1]
    dx, dxb, dg = rowwise(fn, [(x, "row"), (g, "full"), (dy, "row"), (dres, "row")], [(c, F32), (c, BF16)], [(1, c)],
                          tr=256, name=name)
    return (dx, dxb), dg


def swiglu_fwd(gu, name):
    hid = gu.shape[1] // 2

    def fn(v):
        g, u = v[:, :hid], v[:, hid:]
        return (g * _sigmoid(g) * u,)

    return rowwise(fn, [(gu, "row")], [(hid, BF16)], tr=256, name=name)[0]


def swiglu_bwd(gu, da, name):
    hid = gu.shape[1] // 2

    def fn(v, d):
        g, u = v[:, :hid], v[:, hid:]
        s = _sigmoid(g)
        dg = d * u * s * (1.0 + g * (1.0 - s))
        du = d * g * s
        return (jnp.concatenate([dg, du], axis=1),)

    return rowwise(fn, [(gu, "row"), (da, "row")], [(2 * hid, BF16)], tr=256, name=name)[0]


def loss_and_grad(y, t, name):
    d = y.shape[1]

    def fn(yv, tv):
        e = yv - tv
        part = jnp.sum(_colsum(e * e), axis=1, keepdims=True) * (0.5 / d)
        dy = e * (1.0 / d)
        return dy, dy, jnp.broadcast_to(part, (SUBLANES, LANES))

    dy, dyb, acc = rowwise(fn, [(y, "row"), (t, "row")], [(d, F32), (d, BF16)], [(SUBLANES, LANES)], tr=512, name=name)
    return acc[0, 0], (dy, dyb)


def adamw(w, g, m, v, name):
    def fn(wv, gv, mv, vv):
        m2 = ADAM_B1 * mv + (1.0 - ADAM_B1) * gv
        v2 = ADAM_B2 * vv + (1.0 - ADAM_B2) * (gv * gv)
        m_hat = m2 / (1.0 - ADAM_B1 ** ADAM_STEP)
        v_hat = v2 / (1.0 - ADAM_B2 ** ADAM_STEP)
        delta = -ADAM_LR * (m_hat / (jnp.sqrt(v_hat) + ADAM_EPS) + ADAM_WD * wv)
        return delta, m2, v2

    rows, c = w.shape
    tr = _pick(rows, (256, 128, 64, 32, 16, 8)) if rows % SUBLANES == 0 else rows
    if rows % SUBLANES:
        return _whole(fn, [w, g, m, v], [(w.shape, F32)] * 3, name=name)
    return rowwise(fn, [(w, "row"), (g, "row"), (m, "row"), (v, "row")], [(c, F32)] * 3, tr=tr, name=name)


def _whole(fn, ins, outs, *, name):
    n_in = len(ins)

    def kern(*refs):
        res = fn(*[r[...] for r in refs[:n_in]])
        for ref, r in zip(refs[n_in:], res):
            ref[...] = r.astype(ref.dtype)

    return pl.pallas_call(
        kern,
        name=name,
        out_shape=[jax.ShapeDtypeStruct(s, dt) for s, dt in outs],
        compiler_params=pltpu.CompilerParams(vmem_limit_bytes=VMEM_LIMIT),
    )(*ins)


def ffn_fwd(x, g, wgu, wdown, tag):
    h = rms_fwd(x, g, f"ffn_rms_{tag}")
    gu = mm(h, wgu, b_chunks=True, name=f"ffn_gu_{tag}")
    a = swiglu_fwd(gu, f"ffn_act_{tag}")
    xn = mm(a, wdown, add=x, name=f"ffn_down_{tag}")
    return xn, (x, h, gu, a)


def ffn_bwd(dxn, saved, g, wgu, wdown, tag):
    x, h, gu, a = saved
    dxn, dxb = dxn
    da = mm(dxb, wdown, tb=True, name=f"ffn_da_{tag}")
    dwdown = mm(a, dxb, ta=True, name=f"ffn_dwdown_{tag}")
    dgu = swiglu_bwd(gu, da, f"ffn_dact_{tag}")
    dh = mm(dgu, wgu, tb=True, b_chunks=True, name=f"ffn_dh_{tag}")
    dwgu = mm(h, dgu, ta=True, out_chunks=True, name=f"ffn_dwgu_{tag}")
    dx, dg = rms_bwd(x, g, dh, dxn, f"ffn_drms_{tag}")
    return dx, dg, dwgu, dwdown


def _head_blockdiag(c):
    i = jnp.arange(c) // HEAD
    return (i[:, None] == i[None, :]).astype(BF16)


def qknorm_fwd(qkv, qg, kg, bd, name):
    d = qkv.shape[1] // 3
    scale = 1.0 / math.sqrt(HEAD)

    def fn(v, qgv, kgv, bdv):
        q, k, vv = v[:, :d], v[:, d:2 * d], v[:, 2 * d:]
        rq = lax.rsqrt(_dot_x2(q * q, bdv) * (1.0 / HEAD) + EPS)
        rk = lax.rsqrt(_dot_x2(k * k, bdv) * (1.0 / HEAD) + EPS)
        return q * rq * qgv * scale, k * rk * kgv, vv

    return rowwise(fn, [(qkv, "row"), (qg, "full"), (kg, "full"), (bd, "full")],
                   [(d, BF16), (d, BF16), (d, BF16)], tr=256, name=name)


def qknorm_bwd(qkv, dqs, dkn, dv, qg, kg, bd, name):
    d = qkv.shape[1] // 3
    scale = 1.0 / math.sqrt(HEAD)

    def one(xv, gv, dyv, bdv):
        r = lax.rsqrt(_dot_x2(xv * xv, bdv) * (1.0 / HEAD) + EPS)
        xh = xv * r
        dyg = dyv * gv
        dx = r * (dyg - xh * (_dot_x2(dyg * xh, bdv) * (1.0 / HEAD)))
        return dx, _colsum(dyv * xh)

    def fn(v, dqv, dkv, dvv, qgv, kgv, bdv):
        q, k = v[:, :d], v[:, d:2 * d]
        dq, dqg = one(q, qgv, dqv * scale, bdv)
        dk, dkg = one(k, kgv, dkv, bdv)
        return jnp.concatenate([dq, dk, dvv], axis=1), dqg, dkg

    return rowwise(fn, [(qkv, "row"), (dqs, "row"), (dkn, "row"), (dv, "row"), (qg, "full"), (kg, "full"), (bd, "full")],
                   [(3 * d, BF16)], [(1, d), (1, d)], tr=256, name=name)


def _sb_tile(qh, k, mask, tri_gt):
    z = _dot_nt(qh, k)
    sp = jnp.log(1.0 + jnp.exp(-jnp.abs(z)))
    lb = jnp.minimum(z, 0.0) - sp
    l1 = jnp.where(mask, lb - z, 0.0)
    suf = _dot_x2(l1, tri_gt)
    return lb, l1, suf


def _sb_setup(tq, tk):
    row, col = _iota2((tq, tk), 0), _iota2((tq, tk), 1)
    lane = _iota2((1, LANES), 1)
    halves = [(lane < HEAD).astype(BF16), (lane >= HEAD).astype(BF16)]
    lane_q = _iota2((tq, LANES), 1) + jnp.minimum(_iota2((tq, LANES), 0), 0)
    return row, col, halves, lane_q


def sb_attn_fwd(qs, kn, vb, name, side=None):
    s, d = qs.shape
    tq, tk = min(SB_TQ, s), min(SB_TK, s)
    nq = s // tq
    assert s // tk <= LANES and s % tq == 0 and s % tk == 0

    def kern(q_ref, k_ref, v_ref, o_ref, rs_ref, acc_ref):
        i = pl.program_id(1)
        row, col, halves, lane_q = _sb_setup(tq, tk)
        tri_gt = (_iota2((tk, tk), 0) > _iota2((tk, tk), 1)).astype(BF16)
        q = q_ref[...]
        qh = [q * hm for hm in halves]
        acc_ref[...] = jnp.zeros_like(acc_ref)
        rs_ref[...] = jnp.full(rs_ref.shape, SB_UNSEEN, F32)
        nkb = (i + 1) * (tq // tk)

        def more(st):
            return (st[0] < nkb) & (st[1] > SB_DEAD)

        def step(st):
            n, r = st[0], list(st[2:])
            kb = nkb - 1 - n
            ks = pl.multiple_of(kb * tk, tk)
            k = k_ref[pl.ds(ks, tk), :]
            v = v_ref[pl.ds(ks, tk), :]
            mask = col < row + (i * tq - kb * tk)
            at_kb = lane_q == kb
            for hh in range(2):
                lb, l1, suf = _sb_tile(qh[hh], k, mask, tri_gt)
                w = jnp.where(mask, jnp.exp(lb + suf + r[hh]), 0.0)
                acc_ref[...] += _dot(w.astype(BF16), v * halves[hh])
                rs_ref[hh] = jnp.where(at_kb, r[hh], rs_ref[hh])
                r[hh] = r[hh] + _rowsum(l1)
            return (n + 1, jnp.maximum(jnp.max(r[0]), jnp.max(r[1])), r[0], r[1])

        z1 = jnp.zeros((tq, 1), F32)
        lax.while_loop(more, step, (jnp.int32(0), jnp.float32(0.0), z1, z1))
        o_ref[...] = acc_ref[...].astype(BF16)

    nh2 = d // LANES
    return side_call(
        kern, side,
        name=name,
        grid=(nh2, nq),
        in_specs=[pl.BlockSpec((tq, LANES), lambda h, i: (i, h)),
                  pl.BlockSpec((s, LANES), lambda h, i: (0, h)),
                  pl.BlockSpec((s, LANES), lambda h, i: (0, h))],
        out_specs=[pl.BlockSpec((tq, LANES), lambda h, i: (i, h)),
                   pl.BlockSpec((None, 2, tq, LANES), lambda h, i: (h, 0, i, 0))],
        out_shape=[jax.ShapeDtypeStruct((s, d), BF16), jax.ShapeDtypeStruct((nh2, 2, s, LANES), F32)],
        scratch_shapes=[pltpu.VMEM((tq, LANES), F32)],
        args=(qs, kn, vb))


def sb_attn_bwd(qs, kn, vb, rsave, do, name, side=None):
    s, d = qs.shape
    tq, tk = min(SB_TQ, s), min(SB_TK, s)
    nq = s // tq

    def kern(q_ref, k_ref, v_ref, rs_ref, do_ref, dq_ref, dk_ref, dv_ref):
        i = pl.program_id(1)

        @pl.when(i == 0)
        def _():
            dk_ref[...] = jnp.zeros_like(dk_ref)
            dv_ref[...] = jnp.zeros_like(dv_ref)

        row, col, halves, lane_q = _sb_setup(tq, tk)
        tri_gt = (_iota2((tk, tk), 0) > _iota2((tk, tk), 1)).astype(BF16)
        tri_lt = (_iota2((tk, tk), 0) < _iota2((tk, tk), 1)).astype(BF16)
        q = q_ref[...]
        qh = [q * hm for hm in halves]
        dov = do_ref[...].astype(BF16)
        doh = [dov * hm for hm in halves]
        dq_ref[...] = jnp.zeros_like(dq_ref)
        nkb = (i + 1) * (tq // tk)
        top = jnp.maximum(jnp.max(rs_ref[0], axis=0, keepdims=True), jnp.max(rs_ref[1], axis=0, keepdims=True))
        dead = (top <= SB_DEAD) & (_iota2((1, LANES), 1) < nkb)
        kstart = jnp.minimum(jnp.sum(dead.astype(F32)).astype(jnp.int32), nkb)

        def step(kb, ep):
            ep = list(ep)
            ks = pl.multiple_of(kb * tk, tk)
            k = k_ref[pl.ds(ks, tk), :]
            v = v_ref[pl.ds(ks, tk), :]
            mask = col < row + (i * tq - kb * tk)
            at_kb = lane_q == kb
            for hh in range(2):
                lb, l1, suf = _sb_tile(qh[hh], k, mask, tri_gt)
                r = _rowsum(jnp.where(at_kb, rs_ref[hh], 0.0))
                w = jnp.where(mask, jnp.exp(lb + suf + r), 0.0)
                e = _dot_nt(doh[hh], v) * w
                pe = ep[hh] + _dot_x2(e, tri_lt)
                beta = jnp.exp(lb)
                dz = jnp.where(mask, e * (1.0 - beta) - pe * beta, 0.0).astype(BF16)
                dq_ref[...] += _dot(dz, k * halves[hh])
                dk_ref[pl.ds(ks, tk), :] += _dot_tn(dz, qh[hh])
                dv_ref[pl.ds(ks, tk), :] += _dot_tn(w.astype(BF16), doh[hh])
                ep[hh] = ep[hh] + _rowsum(e)
            return tuple(ep)

        z1 = jnp.zeros((tq, 1), F32)
        lax.fori_loop(kstart, nkb, step, (z1, z1))

    nh2 = d // LANES
    return side_call(
        kern, side,
        name=name,
        grid=(nh2, nq),
        in_specs=[pl.BlockSpec((tq, LANES), lambda h, i: (i, h)),
                  pl.BlockSpec((s, LANES), lambda h, i: (0, h)),
                  pl.BlockSpec((s, LANES), lambda h, i: (0, h)),
                  pl.BlockSpec((None, 2, tq, LANES), lambda h, i: (h, 0, i, 0)),
                  pl.BlockSpec((tq, LANES), lambda h, i: (i, h))],
        out_specs=[pl.BlockSpec((tq, LANES), lambda h, i: (i, h)),
                   pl.BlockSpec((s, LANES), lambda h, i: (0, h)),
                   pl.BlockSpec((s, LANES), lambda h, i: (0, h))],
        out_shape=[jax.ShapeDtypeStruct((s, d), F32)] * 3,
        scratch_shapes=[],
        args=(qs, kn, vb, rsave, do))


def _hooked(plan, tag, call, *args):
    side = plan.side(tag)
    outs, side_outs = call(*args, tag, side)
    if side is not None:
        plan.done(tag, side_outs)
    return outs


def sb_fwd(x, g, wqkv, qg, kg, wo, bd, tag, plan):
    h = rms_fwd(x, g, f"sb_rms_{tag}")
    qkv = mm(h, wqkv, b_chunks=True, name=f"sb_qkv_{tag}")
    qs, kn, vb = qknorm_fwd(qkv, qg, kg, bd, f"sb_qknorm_{tag}")
    o, rsave = _hooked(plan, f"sb_attn_{tag}", sb_attn_fwd, qs, kn, vb)
    xn = mm(o, wo, add=x, name=f"sb_out_{tag}")
    return xn, (x, h, qkv, qs, kn, vb, rsave, o)


def sb_bwd(dxn, saved, g, wqkv, qg, kg, wo, bd, tag, plan):
    x, h, qkv, qs, kn, vb, rsave, o = saved
    dxn, dxb = dxn
    do = mm(dxb, wo, tb=True, name=f"sb_do_{tag}")
    dwo = mm(o, dxb, ta=True, name=f"sb_dwo_{tag}")
    dqs, dkn, dv = _hooked(plan, f"sb_dattn_{tag}", sb_attn_bwd, qs, kn, vb, rsave, do)
    dqkv, dqg, dkg = qknorm_bwd(qkv, dqs, dkn, dv, qg, kg, bd, f"sb_dqknorm_{tag}")
    dh = mm(dqkv, wqkv, tb=True, b_chunks=True, name=f"sb_dh_{tag}")
    dwqkv = mm(h, dqkv, ta=True, out_chunks=True, name=f"sb_dwqkv_{tag}")
    dx, dg = rms_bwd(x, g, dh, dxn, f"sb_drms_{tag}")
    nh = dqg.shape[1] // HEAD
    return dx, dg, dwqkv, dqg.reshape(nh, HEAD).sum(0), dkg.reshape(nh, HEAD).sum(0), dwo


def _gelu(x):
    return 0.5 * x * (1.0 + lax.erf(x * (1.0 / math.sqrt(2.0))))


def _gelu_grad(x):
    return 0.5 * (1.0 + lax.erf(x * (1.0 / math.sqrt(2.0)))) + x * jnp.exp(-0.5 * x * x) * (1.0 / math.sqrt(2.0 * math.pi))


def gm_act_fwd(pre, vg, name):
    half = pre.shape[1] // 2

    def fn(p, vgv):
        u = _gelu(p[:, :half])
        v = _gelu(p[:, half:])
        r = lax.rsqrt(jnp.mean(v * v, axis=1, keepdims=True) + EPS)
        return u, v * r * vgv

    return rowwise(fn, [(pre, "row"), (vg, "full")], [(half, F32), (half, BF16)], tr=256, name=name)


def gm_act_bwd(pre, du, dvn, vg, name):
    half = pre.shape[1] // 2

    def fn(p, duv, dvnv, vgv):
        pu, pv = p[:, :half], p[:, half:]
        v = _gelu(pv)
        r = lax.rsqrt(jnp.mean(v * v, axis=1, keepdims=True) + EPS)
        vh = v * r
        dyg = dvnv * vgv
        dv = r * (dyg - vh * jnp.mean(dyg * vh, axis=1, keepdims=True))
        dpre = jnp.concatenate([duv * _gelu_grad(pu), dv * _gelu_grad(pv)], axis=1)
        return dpre, _colsum(dvnv * vh), _colsum(dpre)

    return rowwise(fn, [(pre, "row"), (du, "row"), (dvn, "row"), (vg, "full")],
                   [(2 * half, BF16)], [(1, half), (1, 2 * half)], tr=256, name=name)


def gm_spatial_fwd(u, vn, wc, bst, name):
    s, c = u.shape
    t = CHUNK
    ng = c // LANES

    def kern(u_ref, v_ref, w_ref, b_ref, o_ref):
        for g in range(ng):
            sl = slice(g * LANES, (g + 1) * LANES)
            mixed = _dot(w_ref[g], v_ref[:, sl]) + b_ref[:, sl]
            o_ref[:, sl] = (u_ref[:, sl] * mixed).astype(BF16)

    return pl.pallas_call(
        kern,
        name=name,
        grid=(s // t,),
        in_specs=[pl.BlockSpec((t, c), lambda i: (i, 0)), pl.BlockSpec((t, c), lambda i: (i, 0)),
                  pl.BlockSpec(wc.shape, lambda i: (0, 0, 0)), pl.BlockSpec(bst.shape, lambda i: (0, 0))],
        out_specs=pl.BlockSpec((t, c), lambda i: (i, 0)),
        out_shape=jax.ShapeDtypeStruct((s, c), BF16),
        compiler_params=_params(("parallel",)),
    )(u, vn, wc, bst)


def gm_spatial_bwd(dgate, u, vn, wc, bst, name):
    s, c = u.shape
    t = CHUNK
    ng = c // LANES

    def kern(dg_ref, u_ref, v_ref, w_ref, b_ref, du_ref, dv_ref, dw_ref, db_ref):
        i = pl.program_id(0)

        @pl.when(i == 0)
        def _():
            dw_ref[...] = jnp.zeros_like(dw_ref)
            db_ref[...] = jnp.zeros_like(db_ref)

        for g in range(ng):
            sl = slice(g * LANES, (g + 1) * LANES)
            vg = v_ref[:, sl]
            dgv = dg_ref[:, sl]
            mixed = _dot(w_ref[g], vg) + b_ref[:, sl]
            du_ref[:, sl] = dgv * mixed
            dmix = dgv * u_ref[:, sl]
            dmb = dmix.astype(BF16)
            dv_ref[:, sl] = _dot_tn(w_ref[g], dmb)
            dw_ref[g] += _dot_nt(dmb, vg)
            db_ref[:, sl] += dmix

    return pl.pallas_call(
        kern,
        name=name,
        grid=(s // t,),
        in_specs=[pl.BlockSpec((t, c), lambda i: (i, 0))] * 3 +
                 [pl.BlockSpec(wc.shape, lambda i: (0, 0, 0)), pl.BlockSpec(bst.shape, lambda i: (0, 0))],
        out_specs=[pl.BlockSpec((t, c), lambda i: (i, 0)), pl.BlockSpec((t, c), lambda i: (i, 0)),
                   pl.BlockSpec(wc.shape, lambda i: (0, 0, 0)), pl.BlockSpec(bst.shape, lambda i: (0, 0))],
        out_shape=[jax.ShapeDtypeStruct((s, c), F32), jax.ShapeDtypeStruct((s, c), F32),
                   jax.ShapeDtypeStruct(wc.shape, F32), jax.ShapeDtypeStruct(bst.shape, F32)],
        compiler_params=_params(("arbitrary",)),
    )(dgate, u, vn, wc, bst)


def gm_fwd(x, g, w_in, b_in, vg, wc, bst, w_out, tag):
    h = rms_fwd(x, g, f"gm_rms_{tag}")
    pre = mm(h, w_in, bias=b_in, b_chunks=True, name=f"gm_in_{tag}")
    u, vn = gm_act_fwd(pre, vg, f"gm_act_{tag}")
    gate = gm_spatial_fwd(u, vn, wc, bst, f"gm_spatial_{tag}")
    xn = mm(gate, w_out, add=x, name=f"gm_out_{tag}")
    return xn, (x, h, pre, u, vn, gate)


def gm_bwd(dxn, saved, g, w_in, vg, wc, bst, w_out, tag):
    x, h, pre, u, vn, gate = saved
    dxn, dxb = dxn
    dgate = mm(dxb, w_out, tb=True, name=f"gm_dgate_{tag}")
    dwout = mm(gate, dxb, ta=True, name=f"gm_dwout_{tag}")
    du, dvn, dws, dbst = gm_spatial_bwd(dgate, u, vn, wc, bst, f"gm_dspatial_{tag}")
    dpre, dvg, dbin = gm_act_bwd(pre, du, dvn, vg, f"gm_dact_{tag}")
    dh = mm(dpre, w_in, tb=True, b_chunks=True, name=f"gm_dh_{tag}")
    dwin = mm(h, dpre, ta=True, out_chunks=True, name=f"gm_dwin_{tag}")
    dx, dg = rms_bwd(x, g, dh, dxn, f"gm_drms_{tag}")
    ng = wc.shape[0]
    dws = jnp.where(jnp.tril(jnp.ones((CHUNK, CHUNK), bool)), dws, 0.0)
    dbs = dbst.reshape(CHUNK, ng, LANES).sum(-1).T
    return dx, dg, dwin, dbin, dvg, dws, dbs, dwout


def _conv_taps(xv, prev):
    cat = jnp.concatenate([prev, xv], axis=0)
    return [pltpu.roll(cat, sh, 0)[SUBLANES:] for sh in (3, 2, 1)] + [xv]


def conv_fwd(xbc, ws, b, d_inner, name):
    c = xbc.shape[1]
    nst = (c - d_inner) // 2

    def fn(xv, prev, w0, w1, w2, w3, bv):
        taps = _conv_taps(xv, prev)
        pre = bv + w0 * taps[0] + w1 * taps[1] + w2 * taps[2] + w3 * taps[3]
        out = pre * _sigmoid(pre)
        return out[:, :d_inner], out[:, d_inner:d_inner + nst], out[:, d_inner + nst:]

    return rowwise(fn, [(xbc, "row"), (xbc, "prev")] + [(w, "full") for w in ws] + [(b, "full")],
                   [(d_inner, F32), (nst, F32), (nst, F32)], tr=256, name=name)


def conv_bwd_pre(xbc, ws, b, dxs_a, dxs_b, db_m, dc_m, name):
    c = xbc.shape[1]

    def fn(xv, prev, w0, w1, w2, w3, bv, da, db2, dbm, dcm):
        taps = _conv_taps(xv, prev)
        pre = bv + w0 * taps[0] + w1 * taps[1] + w2 * taps[2] + w3 * taps[3]
        sg = _sigmoid(pre)
        dout = jnp.concatenate([da + db2, dbm, dcm], axis=1)
        dpre = dout * sg * (1.0 + pre * (1.0 - sg))
        return (dpre,) + tuple(_colsum(dpre * tp) for tp in taps) + (_colsum(dpre),)

    return rowwise(fn, [(xbc, "row"), (xbc, "prev")] + [(w, "full") for w in ws] +
                   [(b, "full"), (dxs_a, "row"), (dxs_b, "row"), (db_m, "row"), (dc_m, "row")],
                   [(c, F32)], [(1, c)] * 5, tr=256, name=name)


def conv_bwd_in(dpre, ws, name):
    c = dpre.shape[1]

    def fn(dv, nxt, w0, w1, w2, w3):
        cat = jnp.concatenate([dv, nxt], axis=0)
        n = cat.shape[0]
        up = [pltpu.roll(cat, n - sh, 0)[:dv.shape[0]] for sh in (1, 2, 3)]
        return (w3 * dv + w2 * up[0] + w1 * up[1] + w0 * up[2],)

    return rowwise(fn, [(dpre, "row"), (dpre, "next")] + [(w, "full") for w in ws], [(c, BF16)], tr=256, name=name)[0]


def ssd_pre(dtr, bias, alog, name):
    def fn(d, bv, al, tri):
        dt = _softplus(d + bv)
        a = dt * (-jnp.exp(al))
        return dt, _dot_x3_left(tri, a)

    tri = jnp.tril(jnp.ones((CHUNK, CHUNK), BF16))
    return rowwise(fn, [(dtr, "row"), (bias, "full"), (alog, "full"), (tri, "full")],
                   [(LANES, F32), (LANES, F32)], tr=CHUNK, name=name)


def _ssd_layouts(v, ngroups, hpg):
    s = v.shape[0]
    col = v[:, :ngroups * hpg].T.reshape(ngroups, hpg, s, 1)
    return jnp.broadcast_to(col, (ngroups, hpg, s, LANES))


def _ssd_rowform(acum, ngroups, hpg):
    s = acum.shape[0]
    nc = s // CHUNK
    a = acum[:, :ngroups * hpg].reshape(nc, CHUNK, ngroups, hpg).transpose(2, 0, 3, 1)
    last = jnp.broadcast_to(a[..., CHUNK - 1:], a.shape)
    return jnp.concatenate([a, last], axis=2)


def ssd_chunk_fwd(xs, bm, cm, col_a, col_dt, rowf, name, side=None):
    s, d_inner = xs.shape
    ln = CHUNK
    nc = s // ln
    ng, hpg = col_a.shape[0], col_a.shape[1]
    gw = d_inner // ng
    assert gw == hpg * HEAD and gw % LANES == 0 and bm.shape[1] == ng * LANES

    def kern(x_ref, b_ref, c_ref, ca_ref, cd_ref, rf_ref, y_ref, hp_ref, h_scr):
        c = pl.program_id(1)

        @pl.when(c == 0)
        def _():
            h_scr[...] = jnp.zeros_like(h_scr)

        bb = b_ref[...].astype(BF16)
        cbf = c_ref[...].astype(BF16)
        cb = _dot_nt(cbf, bb)
        causal = _iota2((ln, ln), 0) >= _iota2((ln, ln), 1)
        lane = _iota2((1, LANES), 1)
        ys = [jnp.zeros((ln, LANES), F32) for _ in range(gw // LANES)]
        for r in range(hpg):
            j, hf = divmod(r, LANES // HEAD)
            mh = ((lane >= HEAD * hf) & (lane < HEAD * (hf + 1))).astype(F32)
            ac = ca_ref[r]
            ar = rf_ref[pl.ds(r, 1), :]
            aend = rf_ref[pl.ds(4 + r, 1), :]
            dm = jnp.exp(jnp.minimum(ac - ar, 0.0))
            m = jnp.where(causal, cb * dm, 0.0).astype(BF16)
            xdt = x_ref[:, j * LANES:(j + 1) * LANES] * cd_ref[r] * mh
            h = h_scr[r]
            hp_ref[r] = h
            ys[j] = ys[j] + _dot(m, xdt.astype(BF16)) + _dot_nt(cbf, h.astype(BF16)) * jnp.exp(ac)
            dte = jnp.exp(aend - ac)
            h_scr[r] = jnp.exp(aend) * h + _dot_tn((xdt * dte).astype(BF16), bb)
        for j in range(gw // LANES):
            y_ref[:, j * LANES:(j + 1) * LANES] = ys[j]

    return side_call(
        kern, side,
        name=name,
        grid=(ng, nc),
        in_specs=[pl.BlockSpec((ln, gw), lambda g, c: (c, g)),
                  pl.BlockSpec((ln, LANES), lambda g, c: (c, g)),
                  pl.BlockSpec((ln, LANES), lambda g, c: (c, g)),
                  pl.BlockSpec((None, hpg, ln, LANES), lambda g, c: (g, 0, c, 0)),
                  pl.BlockSpec((None, hpg, ln, LANES), lambda g, c: (g, 0, c, 0)),
                  pl.BlockSpec((None, None, 8, LANES), lambda g, c: (g, c, 0, 0))],
        out_specs=[pl.BlockSpec((ln, gw), lambda g, c: (c, g)),
                   pl.BlockSpec((None, None, hpg, LANES, LANES), lambda g, c: (g, c, 0, 0, 0))],
        out_shape=[jax.ShapeDtypeStruct((s, d_inner), F32),
                   jax.ShapeDtypeStruct((ng, nc, hpg, LANES, LANES), F32)],
        scratch_shapes=[pltpu.VMEM((hpg, LANES, LANES), F32)],
        args=(xs, bm, cm, col_a, col_dt, rowf))


def ssd_chunk_bwd(xs, bm, cm, col_a, col_dt, rowf, hprev, dy, name, side=None):
    s, d_inner = xs.shape
    ln = CHUNK
    nc = s // ln
    ng, hpg = col_a.shape[0], col_a.shape[1]
    gw = d_inner // ng

    def kern(x_ref, b_ref, c_ref, ca_ref, cd_ref, rf_ref, hp_ref, dy_ref,
             dx_ref, db_ref, dc_ref, ddt_ref, da_ref, dh_scr):
        c = pl.program_id(1)

        @pl.when(c == 0)
        def _():
            dh_scr[...] = jnp.zeros_like(dh_scr)

        bb = b_ref[...].astype(BF16)
        cbf = c_ref[...].astype(BF16)
        cb = _dot_nt(cbf, bb)
        row, col = _iota2((ln, ln), 0), _iota2((ln, ln), 1)
        causal = row >= col
        tri_ge = (col >= row).astype(BF16)
        ones = jnp.ones((ln, LANES), BF16)
        lane = _iota2((1, LANES), 1)
        last_row = (_iota2((ln, 1), 0) == ln - 1).astype(F32)
        dcb = jnp.zeros((ln, ln), F32)
        d_b = jnp.zeros((ln, LANES), F32)
        d_c = jnp.zeros((ln, LANES), F32)
        dxs = [jnp.zeros((ln, LANES), F32) for _ in range(gw // LANES)]
        for r in range(hpg):
            j, hf = divmod(r, LANES // HEAD)
            mh = ((lane >= HEAD * hf) & (lane < HEAD * (hf + 1))).astype(F32)
            ac = ca_ref[r]
            dt = cd_ref[r]
            ar = rf_ref[pl.ds(r, 1), :]
            aend = rf_ref[pl.ds(4 + r, 1), :]
            dm = jnp.where(causal, jnp.exp(jnp.minimum(ac - ar, 0.0)), 0.0)
            m = cb * dm
            mb = m.astype(BF16)
            xp = x_ref[:, j * LANES:(j + 1) * LANES]
            xdt = xp * dt * mh
            xdtb = xdt.astype(BF16)
            dyp = dy_ref[:, j * LANES:(j + 1) * LANES] * mh
            dypb = dyp.astype(BF16)
            h = hp_ref[r]
            hb = h.astype(BF16)
            dh = dh_scr[r]
            dhb = dh.astype(BF16)
            e_in = jnp.exp(ac)
            dte = jnp.exp(aend - ac)
            eend = jnp.exp(aend)
            d_m = _dot_nt(dypb, xdtb)
            dcb = dcb + d_m * dm
            gm = d_m * m
            yoff_pre = _dot_nt(cbf, hb)
            bdh = _dot_nt(bb, dhb)
            dxdt = _dot_tn(mb, dypb) + bdh * dte
            t1 = _rowsum(xdt * bdh) * dte
            gh, gl = _split2(gm)
            dacum = (_rowsum(gm) - (_dot_tn(gh, ones) + _dot_tn(gl, ones))
                     + _rowsum(dyp * yoff_pre) * e_in - t1)
            end_term = _colsum(t1) + eend * jnp.sum(_colsum(dh * h), axis=1, keepdims=True)
            dacum = dacum + last_row * end_term
            da_ref[r] = _dot_x3_left(tri_ge, dacum)
            ddt_ref[r] = jnp.broadcast_to(_rowsum(dxdt * xp), (ln, LANES))
            dxs[j] = dxs[j] + dxdt * dt
            d_b = d_b + _dot((xdt * dte).astype(BF16), dhb)
            dye = (dyp * e_in).astype(BF16)
            d_c = d_c + _dot(dye, hb)
            dh_scr[r] = eend * dh + _dot_tn(dye, cbf)
        dcbb = dcb.astype(BF16)
        dc_ref[...] = d_c + _dot(dcbb, bb)
        db_ref[...] = d_b + _dot_tn(dcbb, cbf)
        for j in range(gw // LANES):
            dx_ref[:, j * LANES:(j + 1) * LANES] = dxs[j]

    rev = nc - 1
    colspec = pl.BlockSpec((None, hpg, ln, LANES), lambda g, c: (g, 0, rev - c, 0))
    return side_call(
        kern, side,
        name=name,
        grid=(ng, nc),
        in_specs=[pl.BlockSpec((ln, gw), lambda g, c: (rev - c, g)),
                  pl.BlockSpec((ln, LANES), lambda g, c: (rev - c, g)),
                  pl.BlockSpec((ln, LANES), lambda g, c: (rev - c, g)),
                  colspec, colspec,
                  pl.BlockSpec((None, None, 8, LANES), lambda g, c: (g, rev - c, 0, 0)),
                  pl.BlockSpec((None, None, hpg, LANES, LANES), lambda g, c: (g, rev - c, 0, 0, 0)),
                  pl.BlockSpec((ln, gw), lambda g, c: (rev - c, g))],
        out_specs=[pl.BlockSpec((ln, gw), lambda g, c: (rev - c, g)),
                   pl.BlockSpec((ln, LANES), lambda g, c: (rev - c, g)),
                   pl.BlockSpec((ln, LANES), lambda g, c: (rev - c, g)),
                   colspec, colspec],
        out_shape=[jax.ShapeDtypeStruct((s, d_inner), F32),
                   jax.ShapeDtypeStruct(bm.shape, F32), jax.ShapeDtypeStruct(cm.shape, F32),
                   jax.ShapeDtypeStruct(col_a.shape, F32), jax.ShapeDtypeStruct(col_a.shape, F32)],
        scratch_shapes=[pltpu.VMEM((hpg, LANES, LANES), F32)],
        args=(xs, bm, cm, col_a, col_dt, rowf, hprev, dy))


def gnorm_fwd(y, xs, z, dexp, gain, ngroups, name):
    c = y.shape[1]
    gw = c // ngroups

    def fn(yv, xv, zv, dv, gv):
        yg = (yv + xv * dv) * (zv * _sigmoid(zv))
        outs = []
        for k in range(ngroups):
            t = yg[:, k * gw:(k + 1) * gw]
            outs.append(t * lax.rsqrt(jnp.mean(t * t, axis=1, keepdims=True) + EPS))
        return (jnp.concatenate(outs, axis=1) * gv,)

    return rowwise(fn, [(y, "row"), (xs, "row"), (z, "row"), (dexp, "full"), (gain, "full")], [(c, BF16)], tr=256, name=name)[0]


def gnorm_bwd(dn, y, xs, z, dexp, gain, ngroups, name):
    c = y.shape[1]
    gw = c // ngroups

    def fn(dnv, yv, xv, zv, dv, gv):
        yd = yv + xv * dv
        sg = _sigmoid(zv)
        sz = zv * sg
        yg = yd * sz
        dng = dnv * gv
        dyg, yh = [], []
        for k in range(ngroups):
            sl = slice(k * gw, (k + 1) * gw)
            t = yg[:, sl]
            r = lax.rsqrt(jnp.mean(t * t, axis=1, keepdims=True) + EPS)
            th = t * r
            dyg.append(r * (dng[:, sl] - th * jnp.mean(dng[:, sl] * th, axis=1, keepdims=True)))
            yh.append(th)
        dyg = jnp.concatenate(dyg, axis=1)
        yh = jnp.concatenate(yh, axis=1)
        dyd = dyg * sz
        dz = dyg * yd * (sg * (1.0 + zv * (1.0 - sg)))
        return dyd, dyd * dv, dz, _colsum(dyd * xv), _colsum(dnv * yh)

    return rowwise(fn, [(dn, "row"), (y, "row"), (xs, "row"), (z, "row"), (dexp, "full"), (gain, "full")],
                   [(c, F32), (c, F32), (c, BF16)], [(1, c), (1, c)], tr=256, name=name)


def ssd_post(ddt, da, dt, dtr, bias, alog, name):
    def fn(ddtv, dav, dtv, dtrv, bv, al):
        a_neg = -jnp.exp(al)
        ddtr = (ddtv + dav * a_neg) * _sigmoid(dtrv + bv)
        return ddtr, _colsum(ddtr), _colsum(dav * dtv) * a_neg

    return rowwise(fn, [(ddt, "row"), (da, "row"), (dt, "row"), (dtr, "row"), (bias, "full"), (alog, "full")],
                   [(LANES, BF16)], [(1, LANES), (1, LANES)], tr=512, name=name)


def _from_colform(v, s):
    ng, hpg = v.shape[0], v.shape[1]
    flat = v[..., 0].reshape(ng * hpg, s).T
    return jnp.pad(flat, ((0, 0), (0, LANES - ng * hpg)))


def ssm_fwd(x, g, p, tag, plan):
    ng, hpg, d_inner = p["ng"], p["hpg"], p["d_inner"]
    h = rms_fwd(x, g, f"ssm_rms_{tag}")
    z = mm(h, p["w_z"], name=f"ssm_inz_{tag}")
    xbc = mm(h, p["w_xbc"], name=f"ssm_inx_{tag}")
    dtr = mm(h, p["w_dt"], name=f"ssm_indt_{tag}")
    xs, bm, cm = conv_fwd(xbc, p["conv_w"], p["conv_b"], d_inner, f"ssm_conv_{tag}")
    dt, acum = ssd_pre(dtr, p["dt_bias"], p["a_log"], f"ssm_pre_{tag}")
    col_a, col_dt = _ssd_layouts(acum, ng, hpg), _ssd_layouts(dt, ng, hpg)
    rowf = _ssd_rowform(acum, ng, hpg)
    y, hprev = _hooked(plan, f"ssm_scan_{tag}", ssd_chunk_fwd, xs, bm, cm, col_a, col_dt, rowf)
    n = gnorm_fwd(y, xs, z, p["d_exp"], p["norm_gain"], ng, f"ssm_gnorm_{tag}")
    xn = mm(n, p["w_out"], add=x, name=f"ssm_out_{tag}")
    return xn, (x, h, z, xbc, dtr, xs, bm, cm, dt, col_a, col_dt, rowf, y, hprev, n)


def ssm_bwd(dxn, saved, g, p, tag, plan):
    x, h, z, xbc, dtr, xs, bm, cm, dt, col_a, col_dt, rowf, y, hprev, n = saved
    ng, hpg, d_inner = p["ng"], p["hpg"], p["d_inner"]
    s = x.shape[0]
    dxn, dxb = dxn
    dn = mm(dxb, p["w_out"], tb=True, name=f"ssm_dn_{tag}")
    dwout = mm(n, dxb, ta=True, name=f"ssm_dwout_{tag}")
    dy, dxs_skip, dz, dd_lane, dgain = gnorm_bwd(dn, y, xs, z, p["d_exp"], p["norm_gain"], ng, f"ssm_dgnorm_{tag}")
    dxs, dbm, dcm, ddt_c, da_c = _hooked(plan, f"ssm_dscan_{tag}", ssd_chunk_bwd, xs, bm, cm, col_a, col_dt, rowf, hprev, dy)
    ddtr, dbias, dalog = ssd_post(_from_colform(ddt_c, s), _from_colform(da_c, s), dt, dtr,
                                  p["dt_bias"], p["a_log"], f"ssm_post_{tag}")
    res = conv_bwd_pre(xbc, p["conv_w"], p["conv_b"], dxs, dxs_skip, dbm, dcm, f"ssm_dconv_{tag}")
    dpre, dconv_w, dconv_b = res[0], jnp.concatenate(res[1:5], axis=0), res[5]
    dxbc = conv_bwd_in(dpre, p["conv_w"], f"ssm_dconvin_{tag}")
    dh = mm(dz, p["w_z"], tb=True, name=f"ssm_dhz_{tag}")
    dh = mm(dxbc, p["w_xbc"], tb=True, add=dh, name=f"ssm_dhx_{tag}")
    dh = mm(ddtr, p["w_dt"], tb=True, add=dh, name=f"ssm_dhdt_{tag}")
    dwz = mm(h, dz, ta=True, name=f"ssm_dwz_{tag}")
    dwxbc = mm(h, dxbc, ta=True, name=f"ssm_dwxbc_{tag}")
    dwdt = mm(h, ddtr, ta=True, name=f"ssm_dwdt_{tag}")
    dx, dg = rms_bwd(x, g, dh, dxn, f"ssm_drms_{tag}")
    nh = ng * hpg
    dwin = jnp.concatenate([dwz, dwxbc, dwdt[:, :nh]], axis=1)
    dd = dd_lane.reshape(nh, HEAD).sum(-1)
    return dx, dg, dict(w_in=dwin, conv_w=dconv_w, conv_b=dconv_b, dt_bias=dbias[0, :nh], a_log=dalog[0, :nh],
                        d=dd, norm_gain=dgain, w_out=dwout)


def local_step(x, target, w, plan):
    d = x.shape[1]
    depth = w["mix_norm"].shape[0]
    bd = _head_blockdiag(d)
    tril = jnp.tril(jnp.ones((CHUNK, CHUNK), bool))
    ssm_heads = w["ssm_dt_bias"].shape[1]
    d_inner = w["ssm_norm_gain"].shape[1]
    ng = w["ssm_norm_gain"].shape[1] // 256
    nstate = CHUNK

    def pad_lanes(v):
        return jnp.pad(v, ((0, 0), (0, LANES - v.shape[1])))

    def ssm_params(j):
        w_in = w["ssm_w_in"][j]
        cw = w["ssm_conv_w"][j]
        return dict(ng=ng, hpg=ssm_heads // ng, d_inner=d_inner,
                    w_z=w_in[:, :d_inner], w_xbc=w_in[:, d_inner:d_inner + d_inner + 2 * ng * nstate],
                    w_dt=pad_lanes(w_in[:, 2 * d_inner + 2 * ng * nstate:]),
                    conv_w=[cw[k:k + 1] for k in range(cw.shape[0])], conv_b=w["ssm_conv_b"][j:j + 1],
                    dt_bias=pad_lanes(w["ssm_dt_bias"][j:j + 1]), a_log=pad_lanes(w["ssm_a_log"][j:j + 1]),
                    d_exp=jnp.repeat(w["ssm_d"][j], HEAD)[None, :], norm_gain=w["ssm_norm_gain"][j:j + 1],
                    w_out=w["ssm_w_out"][j])

    def gm_params(j):
        wc = jnp.where(tril, w["gm_w_s"][j], 0.0).astype(BF16)
        bst = jnp.repeat(w["gm_b_s"][j].T, LANES, axis=1)
        return wc, bst

    def sb_gains(j):
        nh = d // HEAD
        return jnp.tile(w["sb_q_gain"][j], nh)[None, :], jnp.tile(w["sb_k_gain"][j], nh)[None, :]

    saved = []
    cur = x
    for i in range(depth):
        kind, j = i % 3, i // 3
        gmix = w["mix_norm"][i:i + 1]
        if kind == 0:
            qg, kg = sb_gains(j)
            cur, sv = sb_fwd(cur, gmix, w["sb_w_qkv"][j], qg, kg, w["sb_w_o"][j], bd, f"{i}", plan)
        elif kind == 1:
            wc, bst = gm_params(j)
            cur, sv = gm_fwd(cur, gmix, w["gm_w_in"][j], w["gm_b_in"][j:j + 1], w["gm_v_gain"][j:j + 1], wc, bst,
                             w["gm_w_out"][j], f"{i}")
        else:
            cur, sv = ssm_fwd(cur, gmix, ssm_params(j), f"{i}", plan)
        cur, sv2 = ffn_fwd(cur, w["ffn_norm"][i:i + 1], w["ffn_w_gu"][i], w["ffn_w_down"][i], f"{i}")
        saved.append((sv, sv2))

    loss, dcur = loss_and_grad(cur, target, "loss")

    grads = {k: [None] * len(v) for k, v in w.items()}
    for i in reversed(range(depth)):
        kind, j = i % 3, i // 3
        sv, sv2 = saved[i]
        gmix = w["mix_norm"][i:i + 1]
        dcur, dgf, dwgu, dwdown = ffn_bwd(dcur, sv2, w["ffn_norm"][i:i + 1], w["ffn_w_gu"][i], w["ffn_w_down"][i], f"{i}")
        grads["ffn_norm"][i], grads["ffn_w_gu"][i], grads["ffn_w_down"][i] = dgf[0], dwgu, dwdown
        if kind == 0:
            qg, kg = sb_gains(j)
            dcur, dg, dwqkv, dqg, dkg, dwo = sb_bwd(dcur, sv, gmix, w["sb_w_qkv"][j], qg, kg, w["sb_w_o"][j], bd, f"{i}", plan)
            grads["sb_w_qkv"][j], grads["sb_q_gain"][j], grads["sb_k_gain"][j], grads["sb_w_o"][j] = dwqkv, dqg, dkg, dwo
        elif kind == 1:
            wc, bst = gm_params(j)
            dcur, dg, dwin, dbin, dvg, dws, dbs, dwout = gm_bwd(dcur, sv, gmix, w["gm_w_in"][j], w["gm_v_gain"][j:j + 1],
                                                                 wc, bst, w["gm_w_out"][j], f"{i}")
            grads["gm_w_in"][j], grads["gm_b_in"][j], grads["gm_v_gain"][j] = dwin, dbin[0], dvg[0]
            grads["gm_w_s"][j], grads["gm_b_s"][j], grads["gm_w_out"][j] = dws, dbs, dwout
        else:
            dcur, dg, gs = ssm_bwd(dcur, sv, gmix, ssm_params(j), f"{i}", plan)
            grads["ssm_w_in"][j], grads["ssm_conv_w"][j], grads["ssm_conv_b"][j] = gs["w_in"], gs["conv_w"], gs["conv_b"][0]
            grads["ssm_dt_bias"][j], grads["ssm_a_log"][j], grads["ssm_d"][j] = gs["dt_bias"], gs["a_log"], gs["d"]
            grads["ssm_norm_gain"][j], grads["ssm_w_out"][j] = gs["norm_gain"][0], gs["w_out"]
        grads["mix_norm"][i] = dg[0]
        mixer = {0: ("sb_w_qkv", "sb_w_o"), 1: ("gm_w_in", "gm_w_out"), 2: ("ssm_w_in", "ssm_w_out")}[kind]
        plan.layer_done(i, {(n, l): grads[n][l] for n, l in [(mixer[0], j), (mixer[1], j), ("ffn_w_gu", i), ("ffn_w_down", i)]})
    grads = {k: (v if k in MATRICES else jnp.stack(v)) for k, v in grads.items()}
    return loss, dcur[0], grads


WEIGHTS = ["mix_norm", "ffn_norm", "sb_w_qkv", "sb_q_gain", "sb_k_gain", "sb_w_o", "gm_w_in", "gm_b_in", "gm_v_gain",
           "gm_w_s", "gm_b_s", "gm_w_out", "ssm_w_in", "ssm_conv_w", "ssm_conv_b", "ssm_dt_bias", "ssm_a_log", "ssm_d",
           "ssm_norm_gain", "ssm_w_out", "ffn_w_gu", "ffn_w_down"]
SHARDED = {"sb_w_qkv": 2, "sb_w_o": 1, "gm_w_in": 2, "gm_w_out": 1, "ssm_w_in": 2, "ssm_conv_w": 2, "ssm_conv_b": 1,
           "ssm_norm_gain": 1, "ssm_w_out": 1, "ffn_w_gu": 2, "ffn_w_down": 1}
EXACT = ("ssm_conv_w", "ssm_conv_b", "ssm_norm_gain")
MATRICES = tuple(n for n in SHARDED if n not in EXACT)
COLUMN_BLOCKS = ("sb_w_qkv", "gm_w_in", "ffn_w_gu")
REPLICATED = [n for n in WEIGHTS if n not in SHARDED]
N_CHIPS = 4
N_DEV = 8
PACK_COLS = 1024


def _pack(pieces, dtype, align):
    flat = jnp.concatenate([p.reshape(-1).astype(dtype) for p in pieces])
    rows = -(-flat.shape[0] // (PACK_COLS * align)) * align
    flat = jnp.pad(flat, (0, rows * PACK_COLS - flat.shape[0]))
    return flat.reshape(rows, PACK_COLS)


def _unpack(flat, shapes):
    out, off = [], 0
    for shp in shapes:
        n = math.prod(shp)
        out.append(flat[off:off + n].reshape(shp))
        off += n
    return out


ANY = pl.BlockSpec(memory_space=pl.ANY)


def _pos():
    return lax.axis_index("x"), lax.axis_index("y"), lax.axis_index("c")


def _remote(src, dst, send, recv, k, to):
    return pltpu.make_async_remote_copy(src_ref=src, dst_ref=dst, send_sem=send.at[k], recv_sem=recv.at[k],
                                        device_id=to, device_id_type=MESH_ID)


def _comm_call(body, name, ins, out_shapes, nsem, aliases=None):
    return pl.pallas_call(
        body, name=name, out_shape=out_shapes,
        in_specs=[ANY] * len(ins), out_specs=[ANY] * len(out_shapes),
        scratch_shapes=[pltpu.SemaphoreType.DMA((nsem,)), pltpu.SemaphoreType.DMA((nsem,))],
        input_output_aliases=aliases or {},
    )(*ins)


def stage_shard(w, chip, name):
    rows, cols = w.shape
    tr = _pick(rows, (256, 352, 128))

    def kern(idx_ref, w_ref, o_ref):
        o_ref[...] = w_ref[...].astype(BF16)

    grid_spec = pltpu.PrefetchScalarGridSpec(
        num_scalar_prefetch=1, grid=(rows // tr,),
        in_specs=[pl.BlockSpec((tr, cols), lambda i, idx: (i, 0))],
        out_specs=pl.BlockSpec((None, tr, cols), lambda i, idx: (idx[0], i, 0)))
    return pl.pallas_call(
        kern, name=name, grid_spec=grid_spec,
        out_shape=jax.ShapeDtypeStruct((N_CHIPS, rows, cols), BF16),
        compiler_params=_params(("parallel",)),
    )(jnp.reshape(chip, (1,)).astype(jnp.int32), w)


class Side:
    def __init__(self, arrays, out_shapes, aliases, nsem, start, finish):
        self.arrays, self.out_shapes, self.aliases, self.nsem = list(arrays), list(out_shapes), aliases, nsem
        self.start, self.finish = start, finish


def run_side(side, name):
    n_in, n_out = len(side.arrays), len(side.out_shapes)

    def body(*refs):
        ins, outs = refs[:n_in], refs[n_in:n_in + n_out]
        send, recv = refs[n_in + n_out:]
        side.start(ins, outs, send, recv)
        side.finish(ins, outs, send, recv)

    return _comm_call(body, name, side.arrays, side.out_shapes, side.nsem, aliases=side.aliases)


def side_call(kern, side, *, name, grid, in_specs, out_specs, out_shape, scratch_shapes, args):
    if side is None:
        res = pl.pallas_call(kern, name=name, grid=grid, in_specs=in_specs, out_specs=out_specs, out_shape=out_shape,
                             scratch_shapes=scratch_shapes,
                             compiler_params=_params(("parallel",) + ("arbitrary",) * (len(grid) - 1)))(*args)
        return list(res), []
    n_in, n_out, n_scr = len(in_specs), len(out_specs), len(scratch_shapes)
    s_in, s_out = len(side.arrays), len(side.out_shapes)

    def body(*refs):
        ins, refs = refs[:n_in], refs[n_in:]
        side_ins, refs = refs[:s_in], refs[s_in:]
        outs, refs = refs[:n_out], refs[n_out:]
        side_outs, refs = refs[:s_out], refs[s_out:]
        scr, (send, recv) = refs[:n_scr], refs[n_scr:]
        first, last = None, None
        for axis, size in enumerate(grid):
            at0, at1 = pl.program_id(axis) == 0, pl.program_id(axis) == size - 1
            first = at0 if first is None else first & at0
            last = at1 if last is None else last & at1

        @pl.when(first)
        def _():
            side.start(side_ins, side_outs, send, recv)

        kern(*ins, *outs, *scr)

        @pl.when(last)
        def _():
            side.finish(side_ins, side_outs, send, recv)

    res = pl.pallas_call(
        body, name=name, grid=grid,
        in_specs=list(in_specs) + [ANY] * s_in, out_specs=list(out_specs) + [ANY] * s_out,
        out_shape=list(out_shape) + side.out_shapes,
        scratch_shapes=list(scratch_shapes) + [pltpu.SemaphoreType.DMA((side.nsem,)), pltpu.SemaphoreType.DMA((side.nsem,))],
        input_output_aliases={n_in + a: n_out + b for a, b in side.aliases.items()},
        compiler_params=_params(("arbitrary",) * len(grid)),
    )(*args, *side.arrays)
    return list(res[:n_out]), list(res[n_out:])


def gather_side(staged):
    n = len(staged)

    def plan(o_refs, send, recv):
        x, y, c = _pos()
        chips = [(1 - x, y), (x, 1 - y), (1 - x, 1 - y)]

        def part(u, chip, cc):
            half = staged[u].shape[1] // 2
            return o_refs[u].at[2 * chip[0] + chip[1], pl.ds(cc * half, half), :]

        first = [_remote(part(u, (x, y), c), part(u, (x, y), c), send, recv, 6 * u + j, (*chip, c))
                 for u in range(n) for j, chip in enumerate(chips)]
        landed = [_remote(part(u, chip, c), part(u, chip, c), send, recv, 6 * u + j, (x, y, c))
                  for u in range(n) for j, chip in enumerate(chips)]
        passed = [_remote(part(u, chip, c), part(u, chip, c), send, recv, 6 * u + 3 + j, (x, y, 1 - c))
                  for u in range(n) for j, chip in enumerate(chips)]
        handed = [_remote(part(u, chip, 1 - c), part(u, chip, 1 - c), send, recv, 6 * u + 3 + j, (x, y, c))
                  for u in range(n) for j, chip in enumerate(chips)]
        return first, landed, passed, handed

    def start(ins, outs, send, recv):
        for cp in plan(outs, send, recv)[0]:
            cp.start()

    def finish(ins, outs, send, recv):
        first, landed, passed, handed = plan(outs, send, recv)
        for got, fw in zip(landed, passed):
            got.wait_recv()
            fw.start()
        for got in handed:
            got.wait_recv()
        for cp in first + passed:
            cp.wait_send()

    outs = [jax.ShapeDtypeStruct(s.shape, s.dtype) for s in staged]
    return Side(staged, outs, {u: u for u in range(n)}, 6 * n, start, finish)


def swap_halves(gps, name):
    n = len(gps)

    def body(*refs):
        g_refs, r_refs = refs[:n], refs[n:2 * n]
        send, recv = refs[2 * n:]
        x, y, c = _pos()
        cps = []
        for u in range(n):
            half = gps[u].shape[1] // 2
            cps.append(_remote(g_refs[u].at[:, pl.ds((1 - c) * half, half), :], r_refs[u], send, recv, u, (x, y, 1 - c)))
        for cp in cps:
            cp.start()
        for cp in cps:
            cp.wait()

    outs = [jax.ShapeDtypeStruct((g.shape[0], g.shape[1] // 2, g.shape[2]), g.dtype) for g in gps]
    return _comm_call(body, name, gps, outs, n)


def scatter_side(parts):
    n = len(parts)

    def plan(p_refs, r_refs, send, recv):
        x, y, c = _pos()
        chips = [(1 - x, y), (x, 1 - y), (1 - x, 1 - y)]
        return [_remote(p_refs[u].at[2 * chip[0] + chip[1]], r_refs[u].at[j], send, recv, 3 * u + j, (*chip, c))
                for u in range(n) for j, chip in enumerate(chips)]

    def start(ins, outs, send, recv):
        for cp in plan(ins, outs, send, recv):
            cp.start()

    def finish(ins, outs, send, recv):
        for cp in plan(ins, outs, send, recv):
            cp.wait()

    outs = [jax.ShapeDtypeStruct((N_CHIPS - 1,) + p.shape[1:], p.dtype) for p in parts]
    return Side(parts, outs, {}, 3 * n, start, finish)


def join_halves(bufs):
    n = len(bufs)

    def body(*refs):
        o_refs = refs[n:2 * n]
        send, recv = refs[2 * n:]
        x, y, c = _pos()

        def rows(u, cc):
            half = bufs[u].shape[0] // 2
            return o_refs[u].at[pl.ds(cc * half, half), :]

        cps = [_remote(rows(u, c), rows(u, c), send, recv, u, (x, y, 1 - c)) for u in range(n)]
        for cp in cps:
            cp.start()
        for u in range(n):
            _remote(rows(u, 1 - c), rows(u, 1 - c), send, recv, u, (x, y, c)).wait_recv()
        for cp in cps:
            cp.wait_send()

    outs = [jax.ShapeDtypeStruct(b.shape, b.dtype) for b in bufs]
    return _comm_call(body, "join_halves", bufs, outs, n, aliases={u: u for u in range(n)})


def gather_small(sg, name):
    rows, cols = sg.shape

    def body(s_ref, o_ref, send, recv, lsem):
        x, y, c = _pos()
        mine = pltpu.make_async_copy(s_ref, o_ref.at[4 * x + 2 * y + c], lsem)
        mine.start()
        peers = []
        for msk in range(1, N_DEV):
            px = 1 - x if msk & 4 else x
            py = 1 - y if msk & 2 else y
            pc = 1 - c if msk & 1 else c
            peers.append((px, py, pc))
        cps = [_remote(s_ref, o_ref.at[4 * x + 2 * y + c], send, recv, k, peer) for k, peer in enumerate(peers)]
        for cp in cps:
            cp.start()
        for k, (px, py, pc) in enumerate(peers):
            _remote(s_ref, o_ref.at[4 * px + 2 * py + pc], send, recv, k, (x, y, c)).wait_recv()
        for cp in cps:
            cp.wait_send()
        mine.wait()

    return pl.pallas_call(
        body, name=name,
        out_shape=jax.ShapeDtypeStruct((N_DEV, rows, cols), sg.dtype),
        in_specs=[ANY], out_specs=ANY,
        scratch_shapes=[pltpu.SemaphoreType.DMA((N_DEV - 1,)), pltpu.SemaphoreType.DMA((N_DEV - 1,)), pltpu.SemaphoreType.DMA],
    )(sg)


def sum_cores(gp, theirs, core, chip, name):
    nch, rows, cols = gp.shape
    half = rows // 2
    tr = _pick(half, (256, 176, 128, 64))
    nb = half // tr

    def kern(idx_ref, g_ref, t_ref, own_ref, all_ref):
        k = pl.program_id(1)
        s = g_ref[...] + t_ref[...]
        all_ref[...] = s.astype(BF16)

        @pl.when(k == idx_ref[1])
        def _():
            own_ref[...] = s

    grid_spec = pltpu.PrefetchScalarGridSpec(
        num_scalar_prefetch=1, grid=(nb, nch),
        in_specs=[pl.BlockSpec((None, tr, cols), lambda i, k, idx: (k, idx[0] * nb + i, 0)),
                  pl.BlockSpec((None, tr, cols), lambda i, k, idx: (k, i, 0))],
        out_specs=[pl.BlockSpec((tr, cols), lambda i, k, idx: (i, 0)),
                   pl.BlockSpec((None, tr, cols), lambda i, k, idx: (k, i, 0))])
    return pl.pallas_call(
        kern, name=name, grid_spec=grid_spec,
        out_shape=[jax.ShapeDtypeStruct((half, cols), F32), jax.ShapeDtypeStruct((nch, half, cols), BF16)],
        compiler_params=_params(("parallel", "arbitrary")),
    )(jnp.stack([core, chip]).astype(jnp.int32), gp, theirs)


def sum_chips(own, others, core, name):
    half, cols = own.shape
    tr = _pick(half, (256, 176, 128, 64))
    nb = half // tr

    def kern(idx_ref, o_ref, a_ref, b_ref, c_ref, out_ref):
        out_ref[...] = ((o_ref[...] + a_ref[...].astype(F32)) + b_ref[...].astype(F32)) + c_ref[...].astype(F32)

    grid_spec = pltpu.PrefetchScalarGridSpec(
        num_scalar_prefetch=1, grid=(nb,),
        in_specs=[pl.BlockSpec((tr, cols), lambda i, idx: (i, 0))] +
                 [pl.BlockSpec((None, tr, cols), lambda i, idx, j=j: (j, i, 0)) for j in range(N_CHIPS - 1)],
        out_specs=pl.BlockSpec((tr, cols), lambda i, idx: (idx[0] * nb + i, 0)))
    return pl.pallas_call(
        kern, name=name, grid_spec=grid_spec,
        out_shape=jax.ShapeDtypeStruct((2 * half, cols), F32),
        compiler_params=_params(("parallel",)),
    )(jnp.reshape(core, (1,)).astype(jnp.int32), own, others, others, others)


def small_update(gath, w, m, v, name):
    def fn(*vs):
        g = vs[0]
        for t in vs[1:N_DEV]:
            g = g + t
        wv, mv, vv = vs[N_DEV:]
        m2 = ADAM_B1 * mv + (1.0 - ADAM_B1) * g
        v2 = ADAM_B2 * vv + (1.0 - ADAM_B2) * (g * g)
        m_hat = m2 / (1.0 - ADAM_B1 ** ADAM_STEP)
        v_hat = v2 / (1.0 - ADAM_B2 ** ADAM_STEP)
        return g, -ADAM_LR * (m_hat / (jnp.sqrt(v_hat) + ADAM_EPS) + ADAM_WD * wv), m2, v2

    c = w.shape[1]
    ins = [(gath[k], "row") for k in range(N_DEV)] + [(w, "row"), (m, "row"), (v, "row")]
    return rowwise(fn, ins, [(c, F32)] * 4, tr=w.shape[0] // 2, name=name)


LAYER_UNITS = {
    0: [("sb_w_qkv", 0), ("sb_w_o", 0), ("ffn_w_gu", 0), ("ffn_w_down", 0)],
    1: [("gm_w_in", 0), ("gm_w_out", 0), ("ffn_w_gu", 1), ("ffn_w_down", 1)],
    2: [("ssm_w_in", 0), ("ssm_w_out", 0), ("ffn_w_gu", 2), ("ffn_w_down", 2)],
    3: [("sb_w_qkv", 1), ("sb_w_o", 1), ("ffn_w_gu", 3), ("ffn_w_down", 3)],
}
GATHER_AT = {"sb_attn_0": (1, 2), "ssm_scan_2": (3,)}
SCATTER_AT = {"ssm_dscan_2": (3,), "sb_dattn_0": (2, 1)}


class _Plan:
    def __init__(self, ins, core, chip):
        self.core, self.chip = core, chip
        self.staged = {(n, l): stage_shard(ins[n][l], chip, f"stage_{n}_{l}")
                       for i in LAYER_UNITS for n, l in LAYER_UNITS[i]}
        self.full = {n: [None] * ins[n].shape[0] for n in MATRICES}
        self.parts = {}
        self.halves = {}
        self._fill(LAYER_UNITS[0], run_side(gather_side([self.staged[u] for u in LAYER_UNITS[0]]), "gather_0"))

    @staticmethod
    def _units(layers):
        return [u for i in layers for u in LAYER_UNITS[i]]

    def _fill(self, units, gathered):
        for (n, l), g in zip(units, gathered):
            if n in COLUMN_BLOCKS:
                self.full[n][l] = g
            elif n == "ssm_w_in":
                self.full[n][l] = jnp.concatenate([g[k] for k in range(N_CHIPS)], axis=1)
            else:
                self.full[n][l] = g.reshape(-1, g.shape[-1])

    def _reduce(self, layers, others):
        owns = [own for i in layers for own in self.parts[i][0]]
        for (n, l), own, other in zip(self._units(layers), owns, others):
            self.halves[(n, l)] = sum_chips(own, other, self.core, f"sum_chips_{n}_{l}")

    def side(self, tag):
        if tag in GATHER_AT:
            return gather_side([self.staged[u] for u in self._units(GATHER_AT[tag])])
        if tag in SCATTER_AT:
            return scatter_side([a for i in SCATTER_AT[tag] for a in self.parts[i][1]])
        return None

    def done(self, tag, results):
        if tag in GATHER_AT:
            self._fill(self._units(GATHER_AT[tag]), results)
        else:
            self._reduce(SCATTER_AT[tag], results)

    def layer_done(self, i, grads):
        gps = []
        for n, l in LAYER_UNITS[i]:
            g = grads[(n, l)]
            if n in COLUMN_BLOCKS:
                gps.append(g)
            elif n == "ssm_w_in":
                gps.append(jnp.stack(jnp.split(g, N_CHIPS, axis=1)))
            else:
                gps.append(g.reshape(N_CHIPS, -1, g.shape[-1]))
        theirs = swap_halves(gps, f"swap_halves_{i}")
        sums = [sum_cores(g, t, self.core, self.chip, f"sum_cores_{n}_{l}")
                for (n, l), g, t in zip(LAYER_UNITS[i], gps, theirs)]
        self.parts[i] = ([s[0] for s in sums], [s[1] for s in sums])
        if not any(i in layers for layers in SCATTER_AT.values()):
            self._reduce((i,), run_side(scatter_side(self.parts[i][1]), f"scatter_{i}"))

    def shard_grads(self):
        units = self._units(sorted(LAYER_UNITS))
        return dict(zip(units, join_halves([self.halves[u] for u in units])))


def _step(ins):
    x, target = ins["x"][0], ins["loss_target"][0]
    core = lax.axis_index("c")
    chip = 2 * lax.axis_index("x") + lax.axis_index("y")

    def lane_pad(v):
        return jnp.pad(v, ((0, 0), (0, PACK_COLS - v.shape[1])))

    vec_rows = [ins["ssm_conv_w"][0], ins["ssm_conv_b"], lane_pad(ins["ssm_norm_gain"])]
    blk = jnp.concatenate(vec_rows + [jnp.zeros((SUBLANES - 6, PACK_COLS), F32)], axis=0)
    per_chip = gather_small(blk, "gather_vectors")[0::2]
    ngw = ins["ssm_norm_gain"].shape[1]
    full = {
        "ssm_conv_w": jnp.concatenate([per_chip[k, 0:4] for k in range(N_CHIPS)], axis=1)[None],
        "ssm_conv_b": jnp.concatenate([per_chip[k, 4:5] for k in range(N_CHIPS)], axis=1),
        "ssm_norm_gain": jnp.concatenate([per_chip[k, 5:6, :ngw] for k in range(N_CHIPS)], axis=1),
    }

    plan = _Plan(ins, core, chip)
    full.update(plan.full)
    for n in REPLICATED:
        full[n] = ins[n]

    loss, dx, grads = local_step(x, target, full, plan)
    loss = lax.psum(loss, ALL_AXES)
    gshards = plan.shard_grads()

    small_shapes = [ins[n].shape for n in REPLICATED]
    vec_shapes = [grads[n].shape for n in EXACT]
    vec_pack = _pack([grads[n] for n in EXACT], F32, SUBLANES)
    gath = gather_small(jnp.concatenate([_pack([grads[n] for n in REPLICATED], F32, SUBLANES), vec_pack], axis=0),
                        "gather_small")
    packed = [jnp.concatenate([_pack([ins[pre + n] for n in REPLICATED], F32, SUBLANES), jnp.zeros_like(vec_pack)], axis=0)
              for pre in ("", "m_", "v_")]
    res = small_update(gath, *packed, name="small_update")
    nrep = res[0].shape[0] - vec_pack.shape[0]
    small = [dict(zip(REPLICATED, _unpack(r[:nrep].reshape(-1), small_shapes))) for r in res]
    vec_g = dict(zip(EXACT, _unpack(res[0][nrep:].reshape(-1), vec_shapes)))

    out_g, out_d, out_m, out_v = {}, {}, {}, {}
    for n in REPLICATED:
        out_g[n], out_d[n], out_m[n], out_v[n] = (s[n] for s in small)
    for n in SHARDED:
        shp = ins[n].shape
        if n in EXACT:
            g = lax.dynamic_slice_in_dim(vec_g[n], chip * shp[-1], shp[-1], axis=vec_g[n].ndim - 1)
        else:
            g = jnp.stack([gshards[(n, l)] for l in range(shp[0])])
        two = (math.prod(shp[:-1]), shp[-1])
        d2, m2, v2 = adamw(ins[n].reshape(two), g.reshape(two), ins["m_" + n].reshape(two),
                           ins["v_" + n].reshape(two), f"adamw_{n}")
        out_g[n], out_d[n], out_m[n], out_v[n] = g, d2.reshape(shp), m2.reshape(shp), v2.reshape(shp)
    return (loss, dx[None], *[out_g[n] for n in WEIGHTS], *[out_d[n] for n in WEIGHTS],
            *[out_m[n] for n in WEIGHTS], *[out_v[n] for n in WEIGHTS])


def kernel(x, mix_norm, ffn_norm, sb_w_qkv, sb_q_gain, sb_k_gain, sb_w_o, gm_w_in, gm_b_in, gm_v_gain, gm_w_s, gm_b_s, gm_w_out, ssm_w_in, ssm_conv_w, ssm_conv_b, ssm_dt_bias, ssm_a_log, ssm_d, ssm_norm_gain, ssm_w_out, ffn_w_gu, ffn_w_down, loss_target, m_mix_norm, m_ffn_norm, m_sb_w_qkv, m_sb_q_gain, m_sb_k_gain, m_sb_w_o, m_gm_w_in, m_gm_b_in, m_gm_v_gain, m_gm_w_s, m_gm_b_s, m_gm_w_out, m_ssm_w_in, m_ssm_conv_w, m_ssm_conv_b, m_ssm_dt_bias, m_ssm_a_log, m_ssm_d, m_ssm_norm_gain, m_ssm_w_out, m_ffn_w_gu, m_ffn_w_down, v_mix_norm, v_ffn_norm, v_sb_w_qkv, v_sb_q_gain, v_sb_k_gain, v_sb_w_o, v_gm_w_in, v_gm_b_in, v_gm_v_gain, v_gm_w_s, v_gm_b_s, v_gm_w_out, v_ssm_w_in, v_ssm_conv_w, v_ssm_conv_b, v_ssm_dt_bias, v_ssm_a_log, v_ssm_d, v_ssm_norm_gain, v_ssm_w_out, v_ffn_w_gu, v_ffn_w_down):
    return _step(dict(locals()))
```

```python
import functools
import math

import jax
import jax.numpy as jnp
from jax import lax
from jax.experimental import pallas as pl
from jax.experimental.pallas import tpu as pltpu

F32 = jnp.float32
BF16 = jnp.bfloat16
EPS = 1e-6
LANES = 128
SUBLANES = 8
VMEM_LIMIT = 56 * 1024 * 1024
HEAD = 64
CHUNK = 128
SB_TQ, SB_TK = 256, 256
SB_DEAD = -110.0
SB_UNSEEN = -1e30
ADAM_LR, ADAM_B1, ADAM_B2, ADAM_EPS, ADAM_WD, ADAM_STEP = 0.001, 0.9, 0.999, 1e-08, 0.01, 10
MESH_ID = pl.DeviceIdType.MESH
ALL_AXES = ("x", "y", "c")


def _params(sem):
    return pltpu.CompilerParams(dimension_semantics=sem, vmem_limit_bytes=VMEM_LIMIT)


def _pick(n, cands):
    for c in cands:
        if n % c == 0:
            return c
    return n


def _dot(a, b, dims=((1,), (0,))):
    return lax.dot_general(a, b, (dims, ((), ())), preferred_element_type=F32)


def _dot_nt(a, b):
    return _dot(a, b, ((1,), (1,)))


def _dot_tn(a, b):
    return _dot(a, b, ((0,), (0,)))


def _split2(x):
    hi = x.astype(BF16)
    lo = (x - hi.astype(F32)).astype(BF16)
    return hi, lo


def _dot_x2(x, m):
    hi, lo = _split2(x)
    return _dot(hi, m) + _dot(lo, m)


def _dot_x3_left(m, x):
    h1 = x.astype(BF16)
    r1 = x - h1.astype(F32)
    h2 = r1.astype(BF16)
    h3 = (r1 - h2.astype(F32)).astype(BF16)
    return _dot(m, h1) + _dot(m, h2) + _dot(m, h3)


def _sigmoid(x):
    return 1.0 / (1.0 + jnp.exp(-x))


def _softplus(x):
    return jnp.maximum(x, 0.0) + jnp.log(1.0 + jnp.exp(-jnp.abs(x)))


def _colsum(x):
    return jnp.sum(x, axis=0, keepdims=True)


def _rowsum(x):
    return jnp.sum(x, axis=1, keepdims=True)


def _iota2(shape, dim):
    return lax.broadcasted_iota(jnp.int32, shape, dim)


MM_VMEM_BUDGET = 40 * 1024 * 1024
MM_STEP_US = 0.35
MM_HBM_BYTES_PER_US = 3.0e6
MM_VMEM_BYTES_PER_US = 1.5e6
MM_FLOPS_PER_US = 9.0e8
MXU_DIM = 256


def _mm_tiles(m, n, kk, wn, wk, a_bytes, b_bytes, has_add):
    def divisors(total, cands):
        got = [c for c in cands if total % c == 0 and c <= total]
        return got or [total]

    best = None
    for tm in divisors(m, (1024, 512, 256, 128)):
        for tn in divisors(wn, (1024, 768, 1408, 512, 256, 128)):
            for tk in divisors(wk, (4096, 2816, 2048, 1408, 1024, 768, 512, 256, 128)):
                nk = kk // tk
                vmem = 2 * (tm * tk * a_bytes + tk * tn * b_bytes + tm * tn * 4 * (2 if has_add else 1))
                vmem += tm * tn * 4 if nk > 1 else 0
                if vmem > MM_VMEM_BUDGET:
                    continue
                steps = (m // tm) * (n // tn) * nk
                a_reads = 1 if nk == 1 else n // tn
                traffic = m * kk * a_bytes * a_reads + kk * n * b_bytes * (m // tm) + m * n * 4
                fill = min(1.0, tn / MXU_DIM) * min(1.0, tm / MXU_DIM)
                compute = 2.0 * m * n * kk / (MM_FLOPS_PER_US * fill)
                cost = steps * MM_STEP_US + max(compute, traffic / MM_HBM_BYTES_PER_US)
                if nk > 1:
                    cost += steps * tm * tn * 8 / MM_VMEM_BYTES_PER_US
                if best is None or cost < best[0]:
                    best = (cost, tm, tn, tk)
    return best[1:]


def mm(a, b, *, ta=False, tb=False, add=None, bias=None, b_chunks=False, out_chunks=False, out_dtype=F32, name):
    if ta:
        kk, m = a.shape
    else:
        m, kk = a.shape
    nch, wide = 1, None
    if b_chunks:
        nch, rows_b, wide = b.shape
        kb, n = (rows_b, nch * wide) if not tb else (nch * wide, rows_b)
    elif tb:
        n, kb = b.shape
    else:
        kb, n = b.shape
    if out_chunks:
        nch, wide = N_CHIPS, n // N_CHIPS
    assert kk == kb, (a.shape, b.shape, ta, tb)
    has_add, has_bias = add is not None, bias is not None
    tm, tn, tk = _mm_tiles(m, n, kk, wide if (wide and not tb) or out_chunks else n, wide if (wide and tb) else kk,
                           a.dtype.itemsize, b.dtype.itemsize, has_add)
    nk = kk // tk
    dims = ((0 if ta else 1,), (1 if tb else 0,))

    def kern(*refs):
        a_ref, b_ref = refs[0], refs[1]
        rest = list(refs[2:])
        add_ref = rest.pop(0) if has_add else None
        bias_ref = rest.pop(0) if has_bias else None
        o_ref = rest[0]
        part = _dot(a_ref[...].astype(BF16), b_ref[...].astype(BF16), dims)

        def finish(r):
            if has_add:
                r = r + add_ref[...]
            if has_bias:
                r = r + bias_ref[...]
            o_ref[...] = r.astype(out_dtype)

        if nk == 1:
            finish(part)
        else:
            acc_ref = rest[1]
            k = pl.program_id(2)

            @pl.when(k == 0)
            def _():
                acc_ref[...] = part

            @pl.when((k > 0) & (k < nk - 1))
            def _():
                acc_ref[...] += part

            @pl.when(k == nk - 1)
            def _():
                finish(acc_ref[...] + part)

    a_spec = pl.BlockSpec((tk, tm), lambda i, j, k: (k, i)) if ta else pl.BlockSpec((tm, tk), lambda i, j, k: (i, k))
    if b_chunks and tb:
        per = wide // tk
        b_spec = pl.BlockSpec((None, tn, tk), lambda i, j, k: (k // per, j, k % per))
    elif b_chunks:
        per = wide // tn
        b_spec = pl.BlockSpec((None, tk, tn), lambda i, j, k: (j // per, k, j % per))
    elif tb:
        b_spec = pl.BlockSpec((tn, tk), lambda i, j, k: (j, k))
    else:
        b_spec = pl.BlockSpec((tk, tn), lambda i, j, k: (k, j))
    if out_chunks:
        per_o = wide // tn
        out_spec = pl.BlockSpec((None, tm, tn), lambda i, j, k: (j // per_o, i, j % per_o))
        out_shape = jax.ShapeDtypeStruct((nch, m, wide), out_dtype)
    else:
        out_spec = pl.BlockSpec((tm, tn), lambda i, j, k: (i, j))
        out_shape = jax.ShapeDtypeStruct((m, n), out_dtype)
    in_specs, args = [a_spec, b_spec], [a, b]
    if has_add:
        in_specs.append(pl.BlockSpec((tm, tn), lambda i, j, k: (i, j)))
        args.append(add)
    if has_bias:
        in_specs.append(pl.BlockSpec((1, tn), lambda i, j, k: (0, j)))
        args.append(bias)
    return pl.pallas_call(
        kern,
        name=name,
        grid=(m // tm, n // tn, nk),
        in_specs=in_specs,
        out_specs=out_spec,
        out_shape=out_shape,
        scratch_shapes=[pltpu.VMEM((tm, tn), F32)] if nk > 1 else [],
        compiler_params=_params(("parallel", "parallel", "arbitrary")),
    )(*args)


def rowwise(fn, ins, outs, accs=(), *, tr, name):
    rows = [a for a, kind in ins if kind == "row"][0].shape[0]
    tr = min(tr, rows)
    assert rows % tr == 0 and tr % SUBLANES == 0, (rows, tr)
    n = rows // tr
    n_in, n_out = len(ins), len(outs)
    kinds = [kind for _, kind in ins]

    def kern(*refs):
        i = pl.program_id(0)
        vals = []
        for ref, kind in zip(refs[:n_in], kinds):
            v = ref[...]
            if kind == "prev":
                v = v * (i > 0).astype(v.dtype)
            elif kind == "next":
                v = v * (i < n - 1).astype(v.dtype)
            vals.append(v)
        res = fn(*vals)
        for ref, r in zip(refs[n_in:n_in + n_out], res[:n_out]):
            ref[...] = r.astype(ref.dtype)
        if accs:
            acc_refs = refs[n_in + n_out:]

            @pl.when(i == 0)
            def _():
                for ref in acc_refs:
                    ref[...] = jnp.zeros_like(ref)

            for ref, r in zip(acc_refs, res[n_out:]):
                ref[...] += r

    in_specs = []
    for a, kind in ins:
        if kind == "row":
            in_specs.append(pl.BlockSpec((tr, a.shape[1]), lambda i: (i, 0)))
        elif kind == "full":
            in_specs.append(pl.BlockSpec(a.shape, lambda i, nd=a.ndim: (0,) * nd))
        elif kind == "prev":
            in_specs.append(pl.BlockSpec((SUBLANES, a.shape[1]),
                                         lambda i: (jnp.maximum(i * (tr // SUBLANES) - 1, 0), 0)))
        else:
            in_specs.append(pl.BlockSpec((SUBLANES, a.shape[1]),
                                         lambda i: (jnp.minimum((i + 1) * (tr // SUBLANES), rows // SUBLANES - 1), 0)))
    out_specs = [pl.BlockSpec((tr, c), lambda i: (i, 0)) for c, _ in outs]
    out_specs += [pl.BlockSpec((r, c), lambda i: (0, 0)) for r, c in accs]
    out_shape = [jax.ShapeDtypeStruct((rows, c), dt) for c, dt in outs]
    out_shape += [jax.ShapeDtypeStruct((r, c), F32) for r, c in accs]
    res = pl.pallas_call(
        kern,
        name=name,
        grid=(n,),
        in_specs=in_specs,
        out_specs=out_specs,
        out_shape=out_shape,
        compiler_params=_params(("arbitrary",) if accs else ("parallel",)),
    )(*[a for a, _ in ins])
    return res


def rms_fwd(x, g, name):
    def fn(xv, gv):
        r = lax.rsqrt(jnp.mean(xv * xv, axis=1, keepdims=True) + EPS)
        return (xv * r * gv,)

    return rowwise(fn, [(x, "row"), (g, "full")], [(x.shape[1], BF16)], tr=512, name=name)[0]


def rms_bwd(x, g, dy, dres, name):
    def fn(xv, gv, dyv, drv):
        r = lax.rsqrt(jnp.mean(xv * xv, axis=1, keepdims=True) + EPS)
        xh = xv * r
        dyg = dyv * gv
        dx = drv + r * (dyg - xh * jnp.mean(dyg * xh, axis=1, keepdims=True))
        return dx, dx, _colsum(dyv * xh)

    c = x.shape[1]
    dx, dxb, dg = rowwise(fn, [(x, "row"), (g, "full"), (dy, "row"), (dres, "row")], [(c, F32), (c, BF16)], [(1, c)],
                          tr=256, name=name)
    return (dx, dxb), dg


def swiglu_fwd(gu, name):
    hid = gu.shape[1] // 2

    def fn(v):
        g, u = v[:, :hid].astype(F32), v[:, hid:].astype(F32)
        return (g * _sigmoid(g) * u,)

    return rowwise(fn, [(gu, "row")], [(hid, BF16)], tr=256, name=name)[0]


def swiglu_bwd(gu, da, name):
    hid = gu.shape[1] // 2

    def fn(v, d):
        g, u = v[:, :hid].astype(F32), v[:, hid:].astype(F32)
        s = _sigmoid(g)
        dg = d * u * s * (1.0 + g * (1.0 - s))
        du = d * g * s
        return (jnp.concatenate([dg, du], axis=1),)

    return rowwise(fn, [(gu, "row"), (da, "row")], [(2 * hid, BF16)], tr=256, name=name)[0]


def loss_and_grad(y, t, name):
    d = y.shape[1]

    def fn(yv, tv):
        e = yv - tv
        part = jnp.sum(_colsum(e * e), axis=1, keepdims=True) * (0.5 / d)
        dy = e * (1.0 / d)
        return dy, dy, jnp.broadcast_to(part, (SUBLANES, LANES))

    dy, dyb, acc = rowwise(fn, [(y, "row"), (t, "row")], [(d, F32), (d, BF16)], [(SUBLANES, LANES)], tr=512, name=name)
    return acc[0, 0], (dy, dyb)


def adamw(w, g, m, v, name):
    def fn(wv, gv, mv, vv):
        m2 = ADAM_B1 * mv + (1.0 - ADAM_B1) * gv
        v2 = ADAM_B2 * vv + (1.0 - ADAM_B2) * (gv * gv)
        m_hat = m2 / (1.0 - ADAM_B1 ** ADAM_STEP)
        v_hat = v2 / (1.0 - ADAM_B2 ** ADAM_STEP)
        delta = -ADAM_LR * (m_hat / (jnp.sqrt(v_hat) + ADAM_EPS) + ADAM_WD * wv)
        return delta, m2, v2

    rows, c = w.shape
    tr = _pick(rows, (256, 128, 64, 32, 16, 8)) if rows % SUBLANES == 0 else rows
    if rows % SUBLANES:
        return _whole(fn, [w, g, m, v], [(w.shape, F32)] * 3, name=name)
    return rowwise(fn, [(w, "row"), (g, "row"), (m, "row"), (v, "row")], [(c, F32)] * 3, tr=tr, name=name)


def _whole(fn, ins, outs, *, name):
    n_in = len(ins)

    def kern(*refs):
        res = fn(*[r[...] for r in refs[:n_in]])
        for ref, r in zip(refs[n_in:], res):
            ref[...] = r.astype(ref.dtype)

    return pl.pallas_call(
        kern,
        name=name,
        out_shape=[jax.ShapeDtypeStruct(s, dt) for s, dt in outs],
        compiler_params=pltpu.CompilerParams(vmem_limit_bytes=VMEM_LIMIT),
    )(*ins)


def ffn_fwd(x, g, wgu, wdown, tag):
    h = rms_fwd(x, g, f"ffn_rms_{tag}")
    gu = mm(h, wgu, b_chunks=True, out_dtype=BF16, name=f"ffn_gu_{tag}")
    a = swiglu_fwd(gu, f"ffn_act_{tag}")
    xn = mm(a, wdown, add=x, name=f"ffn_down_{tag}")
    return xn, (x, h, gu, a)


def ffn_bwd(dxn, saved, g, wgu, wdown, tag):
    x, h, gu, a = saved
    dxn, dxb = dxn
    da = mm(dxb, wdown, tb=True, name=f"ffn_da_{tag}")
    dwdown = mm(a, dxb, ta=True, name=f"ffn_dwdown_{tag}")
    dgu = swiglu_bwd(gu, da, f"ffn_dact_{tag}")
    dh = mm(dgu, wgu, tb=True, b_chunks=True, name=f"ffn_dh_{tag}")
    dwgu = mm(h, dgu, ta=True, out_chunks=True, name=f"ffn_dwgu_{tag}")
    dx, dg = rms_bwd(x, g, dh, dxn, f"ffn_drms_{tag}")
    return dx, dg, dwgu, dwdown


def _head_blockdiag(c):
    i = jnp.arange(c) // HEAD
    return (i[:, None] == i[None, :]).astype(BF16)


def _head_sums(x, bd):
    return jnp.concatenate([_dot_x2(x[:, g * LANES:(g + 1) * LANES], bd) for g in range(x.shape[1] // LANES)], axis=1)


def qknorm_fwd(qkv, qg, kg, bd, name):
    d = qkv.shape[1] // 3
    scale = 1.0 / math.sqrt(HEAD)

    def fn(v, qgv, kgv, bdv):
        v = v.astype(F32)
        q, k, vv = v[:, :d], v[:, d:2 * d], v[:, 2 * d:]
        rq = lax.rsqrt(_head_sums(q * q, bdv) * (1.0 / HEAD) + EPS)
        rk = lax.rsqrt(_head_sums(k * k, bdv) * (1.0 / HEAD) + EPS)
        return q * rq * qgv * scale, k * rk * kgv, vv

    return rowwise(fn, [(qkv, "row"), (qg, "full"), (kg, "full"), (bd, "full")],
                   [(d, BF16), (d, BF16), (d, BF16)], tr=256, name=name)


def qknorm_bwd(qkv, dqs, dkn, dv, qg, kg, bd, name):
    d = qkv.shape[1] // 3
    scale = 1.0 / math.sqrt(HEAD)

    def one(xv, gv, dyv, bdv):
        r = lax.rsqrt(_head_sums(xv * xv, bdv) * (1.0 / HEAD) + EPS)
        xh = xv * r
        dyg = dyv * gv
        dx = r * (dyg - xh * (_head_sums(dyg * xh, bdv) * (1.0 / HEAD)))
        return dx, _colsum(dyv * xh)

    def fn(v, dqv, dkv, dvv, qgv, kgv, bdv):
        v = v.astype(F32)
        q, k = v[:, :d], v[:, d:2 * d]
        dq, dqg = one(q, qgv, dqv * scale, bdv)
        dk, dkg = one(k, kgv, dkv, bdv)
        return jnp.concatenate([dq, dk, dvv], axis=1), dqg, dkg

    return rowwise(fn, [(qkv, "row"), (dqs, "row"), (dkn, "row"), (dv, "row"), (qg, "full"), (kg, "full"), (bd, "full")],
                   [(3 * d, BF16)], [(1, d), (1, d)], tr=256, name=name)


def _sb_tile(qh, k, mask, tri_gt):
    z = _dot_nt(qh, k)
    sp = jnp.log(1.0 + jnp.exp(-jnp.abs(z)))
    lb = jnp.minimum(z, 0.0) - sp
    l1 = jnp.where(mask, lb - z, 0.0)
    suf = _dot_x2(l1, tri_gt)
    return lb, l1, suf


def _sb_setup(tq, tk):
    row, col = _iota2((tq, tk), 0), _iota2((tq, tk), 1)
    lane = _iota2((1, LANES), 1)
    halves = [(lane < HEAD).astype(BF16), (lane >= HEAD).astype(BF16)]
    lane_q = _iota2((tq, LANES), 1) + jnp.minimum(_iota2((tq, LANES), 0), 0)
    return row, col, halves, lane_q


def sb_attn_fwd(qs, kn, vb, name, side=None):
    s, d = qs.shape
    tq, tk = min(SB_TQ, s), min(SB_TK, s)
    nq = s // tq
    assert s // tk <= LANES and s % tq == 0 and s % tk == 0

    def kern(q_ref, k_ref, v_ref, o_ref, rs_ref, acc_ref):
        i = pl.program_id(1)
        row, col, halves, lane_q = _sb_setup(tq, tk)
        tri_gt = (_iota2((tk, tk), 0) > _iota2((tk, tk), 1)).astype(BF16)
        q = q_ref[...]
        qh = [q * hm for hm in halves]
        acc_ref[...] = jnp.zeros_like(acc_ref)
        rs_ref[...] = jnp.full(rs_ref.shape, SB_UNSEEN, F32)
        nkb = (i + 1) * (tq // tk)

        def more(st):
            return (st[0] < nkb) & (st[1] > SB_DEAD)

        def step(st):
            n, r = st[0], list(st[2:])
            kb = nkb - 1 - n
            ks = pl.multiple_of(kb * tk, tk)
            k = k_ref[pl.ds(ks, tk), :]
            v = v_ref[pl.ds(ks, tk), :]
            mask = col < row + (i * tq - kb * tk)
            at_kb = lane_q == kb
            for hh in range(2):
                lb, l1, suf = _sb_tile(qh[hh], k, mask, tri_gt)
                w = jnp.where(mask, jnp.exp(lb + suf + r[hh]), 0.0)
                acc_ref[...] += _dot(w.astype(BF16), v * halves[hh])
                rs_ref[hh] = jnp.where(at_kb, r[hh], rs_ref[hh])
                r[hh] = r[hh] + _rowsum(l1)
            return (n + 1, jnp.maximum(jnp.max(r[0]), jnp.max(r[1])), r[0], r[1])

        z1 = jnp.zeros((tq, 1), F32)
        lax.while_loop(more, step, (jnp.int32(0), jnp.float32(0.0), z1, z1))
        o_ref[...] = acc_ref[...].astype(BF16)

    nh2 = d // LANES
    return side_call(
        kern, side,
        name=name,
        grid=(nh2, nq),
        in_specs=[pl.BlockSpec((tq, LANES), lambda h, i: (i, h)),
                  pl.BlockSpec((s, LANES), lambda h, i: (0, h)),
                  pl.BlockSpec((s, LANES), lambda h, i: (0, h))],
        out_specs=[pl.BlockSpec((tq, LANES), lambda h, i: (i, h)),
                   pl.BlockSpec((None, 2, tq, LANES), lambda h, i: (h, 0, i, 0))],
        out_shape=[jax.ShapeDtypeStruct((s, d), BF16), jax.ShapeDtypeStruct((nh2, 2, s, LANES), F32)],
        scratch_shapes=[pltpu.VMEM((tq, LANES), F32)],
        args=(qs, kn, vb))


def sb_attn_bwd(qs, kn, vb, rsave, do, name, side=None):
    s, d = qs.shape
    tq, tk = min(SB_TQ, s), min(SB_TK, s)
    nq = s // tq

    def kern(q_ref, k_ref, v_ref, rs_ref, do_ref, dq_ref, dk_ref, dv_ref):
        i = pl.program_id(1)

        @pl.when(i == 0)
        def _():
            dk_ref[...] = jnp.zeros_like(dk_ref)
            dv_ref[...] = jnp.zeros_like(dv_ref)

        row, col, halves, lane_q = _sb_setup(tq, tk)
        tri_gt = (_iota2((tk, tk), 0) > _iota2((tk, tk), 1)).astype(BF16)
        tri_lt = (_iota2((tk, tk), 0) < _iota2((tk, tk), 1)).astype(BF16)
        q = q_ref[...]
        qh = [q * hm for hm in halves]
        dov = do_ref[...].astype(BF16)
        doh = [dov * hm for hm in halves]
        dq_ref[...] = jnp.zeros_like(dq_ref)
        nkb = (i + 1) * (tq // tk)
        top = jnp.maximum(jnp.max(rs_ref[0], axis=0, keepdims=True), jnp.max(rs_ref[1], axis=0, keepdims=True))
        dead = (top <= SB_DEAD) & (_iota2((1, LANES), 1) < nkb)
        kstart = jnp.minimum(jnp.sum(dead.astype(F32)).astype(jnp.int32), nkb)

        def step(kb, ep):
            ep = list(ep)
            ks = pl.multiple_of(kb * tk, tk)
            k = k_ref[pl.ds(ks, tk), :]
            v = v_ref[pl.ds(ks, tk), :]
            mask = col < row + (i * tq - kb * tk)
            at_kb = lane_q == kb
            for hh in range(2):
                lb, l1, suf = _sb_tile(qh[hh], k, mask, tri_gt)
                r = _rowsum(jnp.where(at_kb, rs_ref[hh], 0.0))
                lbm = jnp.where(mask, lb, SB_UNSEEN)
                w = jnp.exp(lbm + suf + r)
                e = _dot_nt(doh[hh], v) * w
                pe = ep[hh] + _dot(e.astype(BF16), tri_lt)
                beta = jnp.exp(lbm)
                dz = (e - beta * (e + pe)).astype(BF16)
                dq_ref[...] += _dot(dz, k * halves[hh])
                dk_ref[pl.ds(ks, tk), :] += _dot_tn(dz, qh[hh])
                dv_ref[pl.ds(ks, tk), :] += _dot_tn(w.astype(BF16), doh[hh])
                ep[hh] = ep[hh] + _rowsum(e)
            return tuple(ep)

        z1 = jnp.zeros((tq, 1), F32)
        lax.fori_loop(kstart, nkb, step, (z1, z1))

    nh2 = d // LANES
    return side_call(
        kern, side,
        name=name,
        grid=(nh2, nq),
        in_specs=[pl.BlockSpec((tq, LANES), lambda h, i: (i, h)),
                  pl.BlockSpec((s, LANES), lambda h, i: (0, h)),
                  pl.BlockSpec((s, LANES), lambda h, i: (0, h)),
                  pl.BlockSpec((None, 2, tq, LANES), lambda h, i: (h, 0, i, 0)),
                  pl.BlockSpec((tq, LANES), lambda h, i: (i, h))],
        out_specs=[pl.BlockSpec((tq, LANES), lambda h, i: (i, h)),
                   pl.BlockSpec((s, LANES), lambda h, i: (0, h)),
                   pl.BlockSpec((s, LANES), lambda h, i: (0, h))],
        out_shape=[jax.ShapeDtypeStruct((s, d), F32)] * 3,
        scratch_shapes=[],
        args=(qs, kn, vb, rsave, do))


def _hooked(plan, tag, call, *args):
    side = plan.side(tag)
    outs, side_outs = call(*args, tag, side)
    if side is not None:
        plan.done(tag, side_outs)
    return outs


def sb_fwd(x, g, wqkv, qg, kg, wo, bd, tag, plan):
    h = rms_fwd(x, g, f"sb_rms_{tag}")
    qkv = mm(h, wqkv, b_chunks=True, out_dtype=BF16, name=f"sb_qkv_{tag}")
    qs, kn, vb = qknorm_fwd(qkv, qg, kg, bd, f"sb_qknorm_{tag}")
    o, rsave = _hooked(plan, f"sb_attn_{tag}", sb_attn_fwd, qs, kn, vb)
    xn = mm(o, wo, add=x, name=f"sb_out_{tag}")
    return xn, (x, h, qkv, qs, kn, vb, rsave, o)


def sb_bwd(dxn, saved, g, wqkv, qg, kg, wo, bd, tag, plan):
    x, h, qkv, qs, kn, vb, rsave, o = saved
    dxn, dxb = dxn
    do = mm(dxb, wo, tb=True, name=f"sb_do_{tag}")
    dwo = mm(o, dxb, ta=True, name=f"sb_dwo_{tag}")
    dqs, dkn, dv = _hooked(plan, f"sb_dattn_{tag}", sb_attn_bwd, qs, kn, vb, rsave, do)
    dqkv, dqg, dkg = qknorm_bwd(qkv, dqs, dkn, dv, qg, kg, bd, f"sb_dqknorm_{tag}")
    dh = mm(dqkv, wqkv, tb=True, b_chunks=True, name=f"sb_dh_{tag}")
    dwqkv = mm(h, dqkv, ta=True, out_chunks=True, name=f"sb_dwqkv_{tag}")
    dx, dg = rms_bwd(x, g, dh, dxn, f"sb_drms_{tag}")
    nh = dqg.shape[1] // HEAD
    return dx, dg, dwqkv, dqg.reshape(nh, HEAD).sum(0), dkg.reshape(nh, HEAD).sum(0), dwo


def _gelu(x):
    return 0.5 * x * (1.0 + lax.erf(x * (1.0 / math.sqrt(2.0))))


def _gelu_grad(x):
    return 0.5 * (1.0 + lax.erf(x * (1.0 / math.sqrt(2.0)))) + x * jnp.exp(-0.5 * x * x) * (1.0 / math.sqrt(2.0 * math.pi))


def gm_act_fwd(pre, vg, name):
    half = pre.shape[1] // 2

    def fn(p, vgv):
        p = p.astype(F32)
        u = _gelu(p[:, :half])
        v = _gelu(p[:, half:])
        r = lax.rsqrt(jnp.mean(v * v, axis=1, keepdims=True) + EPS)
        return u, v * r * vgv

    return rowwise(fn, [(pre, "row"), (vg, "full")], [(half, F32), (half, BF16)], tr=256, name=name)


def gm_act_bwd(pre, du, dvn, vg, name):
    half = pre.shape[1] // 2

    def fn(p, duv, dvnv, vgv):
        p = p.astype(F32)
        pu, pv = p[:, :half], p[:, half:]
        v = _gelu(pv)
        r = lax.rsqrt(jnp.mean(v * v, axis=1, keepdims=True) + EPS)
        vh = v * r
        dyg = dvnv * vgv
        dv = r * (dyg - vh * jnp.mean(dyg * vh, axis=1, keepdims=True))
        dpre = jnp.concatenate([duv * _gelu_grad(pu), dv * _gelu_grad(pv)], axis=1)
        return dpre, _colsum(dvnv * vh), _colsum(dpre)

    return rowwise(fn, [(pre, "row"), (du, "row"), (dvn, "row"), (vg, "full")],
                   [(2 * half, BF16)], [(1, half), (1, 2 * half)], tr=256, name=name)


def gm_spatial_fwd(u, vn, wc, bst, name):
    s, c = u.shape
    t = CHUNK
    ng = c // LANES

    def kern(u_ref, v_ref, w_ref, b_ref, o_ref):
        for g in range(ng):
            sl = slice(g * LANES, (g + 1) * LANES)
            mixed = _dot(w_ref[g], v_ref[:, sl]) + b_ref[:, sl]
            o_ref[:, sl] = (u_ref[:, sl] * mixed).astype(BF16)

    return pl.pallas_call(
        kern,
        name=name,
        grid=(s // t,),
        in_specs=[pl.BlockSpec((t, c), lambda i: (i, 0)), pl.BlockSpec((t, c), lambda i: (i, 0)),
                  pl.BlockSpec(wc.shape, lambda i: (0, 0, 0)), pl.BlockSpec(bst.shape, lambda i: (0, 0))],
        out_specs=pl.BlockSpec((t, c), lambda i: (i, 0)),
        out_shape=jax.ShapeDtypeStruct((s, c), BF16),
        compiler_params=_params(("parallel",)),
    )(u, vn, wc, bst)


def gm_spatial_bwd(dgate, u, vn, wc, bst, name):
    s, c = u.shape
    t = CHUNK
    ng = c // LANES

    def kern(dg_ref, u_ref, v_ref, w_ref, b_ref, du_ref, dv_ref, dw_ref, db_ref):
        i = pl.program_id(0)

        @pl.when(i == 0)
        def _():
            dw_ref[...] = jnp.zeros_like(dw_ref)
            db_ref[...] = jnp.zeros_like(db_ref)

        for g in range(ng):
            sl = slice(g * LANES, (g + 1) * LANES)
            vg = v_ref[:, sl]
            dgv = dg_ref[:, sl]
            mixed = _dot(w_ref[g], vg) + b_ref[:, sl]
            du_ref[:, sl] = dgv * mixed
            dmix = dgv * u_ref[:, sl]
            dmb = dmix.astype(BF16)
            dv_ref[:, sl] = _dot_tn(w_ref[g], dmb)
            dw_ref[g] += _dot_nt(dmb, vg)
            db_ref[:, sl] += dmix

    return pl.pallas_call(
        kern,
        name=name,
        grid=(s // t,),
        in_specs=[pl.BlockSpec((t, c), lambda i: (i, 0))] * 3 +
                 [pl.BlockSpec(wc.shape, lambda i: (0, 0, 0)), pl.BlockSpec(bst.shape, lambda i: (0, 0))],
        out_specs=[pl.BlockSpec((t, c), lambda i: (i, 0)), pl.BlockSpec((t, c), lambda i: (i, 0)),
                   pl.BlockSpec(wc.shape, lambda i: (0, 0, 0)), pl.BlockSpec(bst.shape, lambda i: (0, 0))],
        out_shape=[jax.ShapeDtypeStruct((s, c), F32), jax.ShapeDtypeStruct((s, c), F32),
                   jax.ShapeDtypeStruct(wc.shape, F32), jax.ShapeDtypeStruct(bst.shape, F32)],
        compiler_params=_params(("arbitrary",)),
    )(dgate, u, vn, wc, bst)


def gm_fwd(x, g, w_in, b_in, vg, wc, bst, w_out, tag):
    h = rms_fwd(x, g, f"gm_rms_{tag}")
    pre = mm(h, w_in, bias=b_in, b_chunks=True, out_dtype=BF16, name=f"gm_in_{tag}")
    u, vn = gm_act_fwd(pre, vg, f"gm_act_{tag}")
    gate = gm_spatial_fwd(u, vn, wc, bst, f"gm_spatial_{tag}")
    xn = mm(gate, w_out, add=x, name=f"gm_out_{tag}")
    return xn, (x, h, pre, u, vn, gate)


def gm_bwd(dxn, saved, g, w_in, vg, wc, bst, w_out, tag):
    x, h, pre, u, vn, gate = saved
    dxn, dxb = dxn
    dgate = mm(dxb, w_out, tb=True, name=f"gm_dgate_{tag}")
    dwout = mm(gate, dxb, ta=True, name=f"gm_dwout_{tag}")
    du, dvn, dws, dbst = gm_spatial_bwd(dgate, u, vn, wc, bst, f"gm_dspatial_{tag}")
    dpre, dvg, dbin = gm_act_bwd(pre, du, dvn, vg, f"gm_dact_{tag}")
    dh = mm(dpre, w_in, tb=True, b_chunks=True, name=f"gm_dh_{tag}")
    dwin = mm(h, dpre, ta=True, out_chunks=True, name=f"gm_dwin_{tag}")
    dx, dg = rms_bwd(x, g, dh, dxn, f"gm_drms_{tag}")
    ng = wc.shape[0]
    dws = jnp.where(jnp.tril(jnp.ones((CHUNK, CHUNK), bool)), dws, 0.0)
    dbs = dbst.reshape(CHUNK, ng, LANES).sum(-1).T
    return dx, dg, dwin, dbin, dvg, dws, dbs, dwout


def _conv_taps(xv, prev):
    cat = jnp.concatenate([prev, xv], axis=0)
    return [pltpu.roll(cat, sh, 0)[SUBLANES:] for sh in (3, 2, 1)] + [xv]


def conv_fwd(xbc, ws, b, d_inner, name):
    c = xbc.shape[1]
    nst = (c - d_inner) // 2

    def fn(xv, prev, w0, w1, w2, w3, bv):
        taps = _conv_taps(xv, prev)
        pre = bv + w0 * taps[0] + w1 * taps[1] + w2 * taps[2] + w3 * taps[3]
        out = pre * _sigmoid(pre)
        return out[:, :d_inner], out[:, d_inner:d_inner + nst], out[:, d_inner + nst:]

    return rowwise(fn, [(xbc, "row"), (xbc, "prev")] + [(w, "full") for w in ws] + [(b, "full")],
                   [(d_inner, F32), (nst, F32), (nst, F32)], tr=256, name=name)


def conv_bwd_pre(xbc, ws, b, dxs_a, dxs_b, db_m, dc_m, name):
    c = xbc.shape[1]

    def fn(xv, prev, w0, w1, w2, w3, bv, da, db2, dbm, dcm):
        taps = _conv_taps(xv, prev)
        pre = bv + w0 * taps[0] + w1 * taps[1] + w2 * taps[2] + w3 * taps[3]
        sg = _sigmoid(pre)
        dout = jnp.concatenate([da + db2, dbm, dcm], axis=1)
        dpre = dout * sg * (1.0 + pre * (1.0 - sg))
        return (dpre,) + tuple(_colsum(dpre * tp) for tp in taps) + (_colsum(dpre),)

    return rowwise(fn, [(xbc, "row"), (xbc, "prev")] + [(w, "full") for w in ws] +
                   [(b, "full"), (dxs_a, "row"), (dxs_b, "row"), (db_m, "row"), (dc_m, "row")],
                   [(c, F32)], [(1, c)] * 5, tr=256, name=name)


def conv_bwd_in(dpre, ws, name):
    c = dpre.shape[1]

    def fn(dv, nxt, w0, w1, w2, w3):
        cat = jnp.concatenate([dv, nxt], axis=0)
        n = cat.shape[0]
        up = [pltpu.roll(cat, n - sh, 0)[:dv.shape[0]] for sh in (1, 2, 3)]
        return (w3 * dv + w2 * up[0] + w1 * up[1] + w0 * up[2],)

    return rowwise(fn, [(dpre, "row"), (dpre, "next")] + [(w, "full") for w in ws], [(c, BF16)], tr=256, name=name)[0]


def ssd_pre(dtr, bias, alog, name):
    def fn(d, bv, al, tri):
        dt = _softplus(d + bv)
        a = dt * (-jnp.exp(al))
        return dt, _dot_x3_left(tri, a)

    tri = jnp.tril(jnp.ones((CHUNK, CHUNK), BF16))
    return rowwise(fn, [(dtr, "row"), (bias, "full"), (alog, "full"), (tri, "full")],
                   [(LANES, F32), (LANES, F32)], tr=CHUNK, name=name)


def _ssd_layouts(v, ngroups, hpg):
    s = v.shape[0]
    col = v[:, :ngroups * hpg].T.reshape(ngroups, hpg, s, 1)
    return jnp.broadcast_to(col, (ngroups, hpg, s, LANES))


def _ssd_rowform(acum, ngroups, hpg):
    s = acum.shape[0]
    nc = s // CHUNK
    a = acum[:, :ngroups * hpg].reshape(nc, CHUNK, ngroups, hpg).transpose(2, 0, 3, 1)
    last = jnp.broadcast_to(a[..., CHUNK - 1:], a.shape)
    return jnp.concatenate([a, last], axis=2)


def ssd_chunk_fwd(xs, bm, cm, col_a, col_dt, rowf, name, side=None):
    s, d_inner = xs.shape
    ln = CHUNK
    nc = s // ln
    ng, hpg = col_a.shape[0], col_a.shape[1]
    gw = d_inner // ng
    assert gw == hpg * HEAD and gw % LANES == 0 and bm.shape[1] == ng * LANES

    def kern(x_ref, b_ref, c_ref, ca_ref, cd_ref, rf_ref, y_ref, hp_ref, h_scr):
        c = pl.program_id(1)

        @pl.when(c == 0)
        def _():
            h_scr[...] = jnp.zeros_like(h_scr)

        bb = b_ref[...].astype(BF16)
        cbf = c_ref[...].astype(BF16)
        cb = _dot_nt(cbf, bb)
        causal = _iota2((ln, ln), 0) >= _iota2((ln, ln), 1)
        lane = _iota2((1, LANES), 1)
        ys = [jnp.zeros((ln, LANES), F32) for _ in range(gw // LANES)]
        for r in range(hpg):
            j, hf = divmod(r, LANES // HEAD)
            mh = ((lane >= HEAD * hf) & (lane < HEAD * (hf + 1))).astype(F32)
            ac = ca_ref[r]
            ar = rf_ref[pl.ds(r, 1), :]
            aend = rf_ref[pl.ds(4 + r, 1), :]
            dm = jnp.exp(jnp.minimum(ac - ar, 0.0))
            m = jnp.where(causal, cb * dm, 0.0).astype(BF16)
            xdt = x_ref[:, j * LANES:(j + 1) * LANES] * cd_ref[r] * mh
            h = h_scr[r]
            hp_ref[r] = h
            ys[j] = ys[j] + _dot(m, xdt.astype(BF16)) + _dot_nt(cbf, h.astype(BF16)) * jnp.exp(ac)
            dte = jnp.exp(aend - ac)
            h_scr[r] = jnp.exp(aend) * h + _dot_tn((xdt * dte).astype(BF16), bb)
        for j in range(gw // LANES):
            y_ref[:, j * LANES:(j + 1) * LANES] = ys[j]

    return side_call(
        kern, side,
        name=name,
        grid=(ng, nc),
        in_specs=[pl.BlockSpec((ln, gw), lambda g, c: (c, g)),
                  pl.BlockSpec((ln, LANES), lambda g, c: (c, g)),
                  pl.BlockSpec((ln, LANES), lambda g, c: (c, g)),
                  pl.BlockSpec((None, hpg, ln, LANES), lambda g, c: (g, 0, c, 0)),
                  pl.BlockSpec((None, hpg, ln, LANES), lambda g, c: (g, 0, c, 0)),
                  pl.BlockSpec((None, None, 8, LANES), lambda g, c: (g, c, 0, 0))],
        out_specs=[pl.BlockSpec((ln, gw), lambda g, c: (c, g)),
                   pl.BlockSpec((None, None, hpg, LANES, LANES), lambda g, c: (g, c, 0, 0, 0))],
        out_shape=[jax.ShapeDtypeStruct((s, d_inner), F32),
                   jax.ShapeDtypeStruct((ng, nc, hpg, LANES, LANES), F32)],
        scratch_shapes=[pltpu.VMEM((hpg, LANES, LANES), F32)],
        args=(xs, bm, cm, col_a, col_dt, rowf))


def ssd_chunk_bwd(xs, bm, cm, col_a, col_dt, rowf, hprev, dy, name, side=None):
    s, d_inner = xs.shape
    ln = CHUNK
    nc = s // ln
    ng, hpg = col_a.shape[0], col_a.shape[1]
    gw = d_inner // ng

    def kern(x_ref, b_ref, c_ref, ca_ref, cd_ref, rf_ref, hp_ref, dy_ref,
             dx_ref, db_ref, dc_ref, ddt_ref, da_ref, dh_scr):
        c = pl.program_id(1)

        @pl.when(c == 0)
        def _():
            dh_scr[...] = jnp.zeros_like(dh_scr)

        bb = b_ref[...].astype(BF16)
        cbf = c_ref[...].astype(BF16)
        cb = _dot_nt(cbf, bb)
        row, col = _iota2((ln, ln), 0), _iota2((ln, ln), 1)
        causal = row >= col
        tri_ge = (col >= row).astype(BF16)
        ones = jnp.ones((ln, LANES), BF16)
        lane = _iota2((1, LANES), 1)
        last_row = (_iota2((ln, 1), 0) == ln - 1).astype(F32)
        dcb = jnp.zeros((ln, ln), F32)
        d_b = jnp.zeros((ln, LANES), F32)
        d_c = jnp.zeros((ln, LANES), F32)
        dxs = [jnp.zeros((ln, LANES), F32) for _ in range(gw // LANES)]
        for r in range(hpg):
            j, hf = divmod(r, LANES // HEAD)
            mh = ((lane >= HEAD * hf) & (lane < HEAD * (hf + 1))).astype(F32)
            ac = ca_ref[r]
            dt = cd_ref[r]
            ar = rf_ref[pl.ds(r, 1), :]
            aend = rf_ref[pl.ds(4 + r, 1), :]
            dm = jnp.where(causal, jnp.exp(jnp.minimum(ac - ar, 0.0)), 0.0)
            m = cb * dm
            mb = m.astype(BF16)
            xp = x_ref[:, j * LANES:(j + 1) * LANES]
            xdt = xp * dt * mh
            xdtb = xdt.astype(BF16)
            dyp = dy_ref[:, j * LANES:(j + 1) * LANES] * mh
            dypb = dyp.astype(BF16)
            h = hp_ref[r]
            hb = h.astype(BF16)
            dh = dh_scr[r]
            dhb = dh.astype(BF16)
            e_in = jnp.exp(ac)
            dte = jnp.exp(aend - ac)
            eend = jnp.exp(aend)
            d_m = _dot_nt(dypb, xdtb)
            dcb = dcb + d_m * dm
            gm = d_m * m
            yoff_pre = _dot_nt(cbf, hb)
            bdh = _dot_nt(bb, dhb)
            dxdt = _dot_tn(mb, dypb) + bdh * dte
            t1 = _rowsum(xdt * bdh) * dte
            gh, gl = _split2(gm)
            dacum = (_rowsum(gm) - (_dot_tn(gh, ones) + _dot_tn(gl, ones))
                     + _rowsum(dyp * yoff_pre) * e_in - t1)
            end_term = _colsum(t1) + eend * jnp.sum(_colsum(dh * h), axis=1, keepdims=True)
            dacum = dacum + last_row * end_term
            da_ref[r] = _dot_x3_left(tri_ge, dacum)
            ddt_ref[r] = jnp.broadcast_to(_rowsum(dxdt * xp), (ln, LANES))
            dxs[j] = dxs[j] + dxdt * dt
            d_b = d_b + _dot((xdt * dte).astype(BF16), dhb)
            dye = (dyp * e_in).astype(BF16)
            d_c = d_c + _dot(dye, hb)
            dh_scr[r] = eend * dh + _dot_tn(dye, cbf)
        dcbb = dcb.astype(BF16)
        dc_ref[...] = d_c + _dot(dcbb, bb)
        db_ref[...] = d_b + _dot_tn(dcbb, cbf)
        for j in range(gw // LANES):
            dx_ref[:, j * LANES:(j + 1) * LANES] = dxs[j]

    rev = nc - 1
    colspec = pl.BlockSpec((None, hpg, ln, LANES), lambda g, c: (g, 0, rev - c, 0))
    return side_call(
        kern, side,
        name=name,
        grid=(ng, nc),
        in_specs=[pl.BlockSpec((ln, gw), lambda g, c: (rev - c, g)),
                  pl.BlockSpec((ln, LANES), lambda g, c: (rev - c, g)),
                  pl.BlockSpec((ln, LANES), lambda g, c: (rev - c, g)),
                  colspec, colspec,
                  pl.BlockSpec((None, None, 8, LANES), lambda g, c: (g, rev - c, 0, 0)),
                  pl.BlockSpec((None, None, hpg, LANES, LANES), lambda g, c: (g, rev - c, 0, 0, 0)),
                  pl.BlockSpec((ln, gw), lambda g, c: (rev - c, g))],
        out_specs=[pl.BlockSpec((ln, gw), lambda g, c: (rev - c, g)),
                   pl.BlockSpec((ln, LANES), lambda g, c: (rev - c, g)),
                   pl.BlockSpec((ln, LANES), lambda g, c: (rev - c, g)),
                   colspec, colspec],
        out_shape=[jax.ShapeDtypeStruct((s, d_inner), F32),
                   jax.ShapeDtypeStruct(bm.shape, F32), jax.ShapeDtypeStruct(cm.shape, F32),
                   jax.ShapeDtypeStruct(col_a.shape, F32), jax.ShapeDtypeStruct(col_a.shape, F32)],
        scratch_shapes=[pltpu.VMEM((hpg, LANES, LANES), F32)],
        args=(xs, bm, cm, col_a, col_dt, rowf, hprev, dy))


def gnorm_fwd(y, xs, z, dexp, gain, ngroups, name):
    c = y.shape[1]
    gw = c // ngroups

    def fn(yv, xv, zv, dv, gv):
        yg = (yv + xv * dv) * (zv * _sigmoid(zv))
        outs = []
        for k in range(ngroups):
            t = yg[:, k * gw:(k + 1) * gw]
            outs.append(t * lax.rsqrt(jnp.mean(t * t, axis=1, keepdims=True) + EPS))
        return (jnp.concatenate(outs, axis=1) * gv,)

    return rowwise(fn, [(y, "row"), (xs, "row"), (z, "row"), (dexp, "full"), (gain, "full")], [(c, BF16)], tr=256, name=name)[0]


def gnorm_bwd(dn, y, xs, z, dexp, gain, ngroups, name):
    c = y.shape[1]
    gw = c // ngroups

    def fn(dnv, yv, xv, zv, dv, gv):
        yd = yv + xv * dv
        sg = _sigmoid(zv)
        sz = zv * sg
        yg = yd * sz
        dng = dnv * gv
        dyg, yh = [], []
        for k in range(ngroups):
            sl = slice(k * gw, (k + 1) * gw)
            t = yg[:, sl]
            r = lax.rsqrt(jnp.mean(t * t, axis=1, keepdims=True) + EPS)
            th = t * r
            dyg.append(r * (dng[:, sl] - th * jnp.mean(dng[:, sl] * th, axis=1, keepdims=True)))
            yh.append(th)
        dyg = jnp.concatenate(dyg, axis=1)
        yh = jnp.concatenate(yh, axis=1)
        dyd = dyg * sz
        dz = dyg * yd * (sg * (1.0 + zv * (1.0 - sg)))
        return dyd, dyd * dv, dz, _colsum(dyd * xv), _colsum(dnv * yh)

    return rowwise(fn, [(dn, "row"), (y, "row"), (xs, "row"), (z, "row"), (dexp, "full"), (gain, "full")],
                   [(c, F32), (c, F32), (c, BF16)], [(1, c), (1, c)], tr=256, name=name)


def ssd_post(ddt, da, dt, dtr, bias, alog, name):
    def fn(ddtv, dav, dtv, dtrv, bv, al):
        a_neg = -jnp.exp(al)
        ddtr = (ddtv + dav * a_neg) * _sigmoid(dtrv + bv)
        return ddtr, _colsum(ddtr), _colsum(dav * dtv) * a_neg

    return rowwise(fn, [(ddt, "row"), (da, "row"), (dt, "row"), (dtr, "row"), (bias, "full"), (alog, "full")],
                   [(LANES, BF16)], [(1, LANES), (1, LANES)], tr=512, name=name)


def _from_colform(v, s):
    ng, hpg = v.shape[0], v.shape[1]
    flat = v[..., 0].reshape(ng * hpg, s).T
    return jnp.pad(flat, ((0, 0), (0, LANES - ng * hpg)))


def ssm_fwd(x, g, p, tag, plan):
    ng, hpg, d_inner = p["ng"], p["hpg"], p["d_inner"]
    h = rms_fwd(x, g, f"ssm_rms_{tag}")
    z = mm(h, p["w_z"], name=f"ssm_inz_{tag}")
    xbc = mm(h, p["w_xbc"], name=f"ssm_inx_{tag}")
    dtr = mm(h, p["w_dt"], name=f"ssm_indt_{tag}")
    xs, bm, cm = conv_fwd(xbc, p["conv_w"], p["conv_b"], d_inner, f"ssm_conv_{tag}")
    dt, acum = ssd_pre(dtr, p["dt_bias"], p["a_log"], f"ssm_pre_{tag}")
    col_a, col_dt = _ssd_layouts(acum, ng, hpg), _ssd_layouts(dt, ng, hpg)
    rowf = _ssd_rowform(acum, ng, hpg)
    y, hprev = _hooked(plan, f"ssm_scan_{tag}", ssd_chunk_fwd, xs, bm, cm, col_a, col_dt, rowf)
    n = gnorm_fwd(y, xs, z, p["d_exp"], p["norm_gain"], ng, f"ssm_gnorm_{tag}")
    xn = mm(n, p["w_out"], add=x, name=f"ssm_out_{tag}")
    return xn, (x, h, z, xbc, dtr, xs, bm, cm, dt, col_a, col_dt, rowf, y, hprev, n)


def ssm_bwd(dxn, saved, g, p, tag, plan):
    x, h, z, xbc, dtr, xs, bm, cm, dt, col_a, col_dt, rowf, y, hprev, n = saved
    ng, hpg, d_inner = p["ng"], p["hpg"], p["d_inner"]
    s = x.shape[0]
    dxn, dxb = dxn
    dn = mm(dxb, p["w_out"], tb=True, name=f"ssm_dn_{tag}")
    dwout = mm(n, dxb, ta=True, name=f"ssm_dwout_{tag}")
    dy, dxs_skip, dz, dd_lane, dgain = gnorm_bwd(dn, y, xs, z, p["d_exp"], p["norm_gain"], ng, f"ssm_dgnorm_{tag}")
    dxs, dbm, dcm, ddt_c, da_c = _hooked(plan, f"ssm_dscan_{tag}", ssd_chunk_bwd, xs, bm, cm, col_a, col_dt, rowf, hprev, dy)
    ddtr, dbias, dalog = ssd_post(_from_colform(ddt_c, s), _from_colform(da_c, s), dt, dtr,
                                  p["dt_bias"], p["a_log"], f"ssm_post_{tag}")
    res = conv_bwd_pre(xbc, p["conv_w"], p["conv_b"], dxs, dxs_skip, dbm, dcm, f"ssm_dconv_{tag}")
    dpre, dconv_w, dconv_b = res[0], jnp.concatenate(res[1:5], axis=0), res[5]
    dxbc = conv_bwd_in(dpre, p["conv_w"], f"ssm_dconvin_{tag}")
    dh = mm(dz, p["w_z"], tb=True, name=f"ssm_dhz_{tag}")
    dh = mm(dxbc, p["w_xbc"], tb=True, add=dh, name=f"ssm_dhx_{tag}")
    dh = mm(ddtr, p["w_dt"], tb=True, add=dh, name=f"ssm_dhdt_{tag}")
    dwz = mm(h, dz, ta=True, name=f"ssm_dwz_{tag}")
    dwxbc = mm(h, dxbc, ta=True, name=f"ssm_dwxbc_{tag}")
    dwdt = mm(h, ddtr, ta=True, name=f"ssm_dwdt_{tag}")
    dx, dg = rms_bwd(x, g, dh, dxn, f"ssm_drms_{tag}")
    nh = ng * hpg
    dwin = jnp.concatenate([dwz, dwxbc, dwdt[:, :nh]], axis=1)
    dd = dd_lane.reshape(nh, HEAD).sum(-1)
    return dx, dg, dict(w_in=dwin, conv_w=dconv_w, conv_b=dconv_b, dt_bias=dbias[0, :nh], a_log=dalog[0, :nh],
                        d=dd, norm_gain=dgain, w_out=dwout)


def local_step(x, target, w, plan):
    d = x.shape[1]
    depth = w["mix_norm"].shape[0]
    bd = _head_blockdiag(LANES)
    tril = jnp.tril(jnp.ones((CHUNK, CHUNK), bool))
    ssm_heads = w["ssm_dt_bias"].shape[1]
    d_inner = w["ssm_norm_gain"].shape[1]
    ng = w["ssm_norm_gain"].shape[1] // 256
    nstate = CHUNK

    def pad_lanes(v):
        return jnp.pad(v, ((0, 0), (0, LANES - v.shape[1])))

    def ssm_params(j):
        w_in = w["ssm_w_in"][j]
        cw = w["ssm_conv_w"][j]
        return dict(ng=ng, hpg=ssm_heads // ng, d_inner=d_inner,
                    w_z=w_in[:, :d_inner], w_xbc=w_in[:, d_inner:d_inner + d_inner + 2 * ng * nstate],
                    w_dt=pad_lanes(w_in[:, 2 * d_inner + 2 * ng * nstate:]),
                    conv_w=[cw[k:k + 1] for k in range(cw.shape[0])], conv_b=w["ssm_conv_b"][j:j + 1],
                    dt_bias=pad_lanes(w["ssm_dt_bias"][j:j + 1]), a_log=pad_lanes(w["ssm_a_log"][j:j + 1]),
                    d_exp=jnp.repeat(w["ssm_d"][j], HEAD)[None, :], norm_gain=w["ssm_norm_gain"][j:j + 1],
                    w_out=w["ssm_w_out"][j])

    def gm_params(j):
        wc = jnp.where(tril, w["gm_w_s"][j], 0.0).astype(BF16)
        bst = jnp.repeat(w["gm_b_s"][j].T, LANES, axis=1)
        return wc, bst

    def sb_gains(j):
        nh = d // HEAD
        return jnp.tile(w["sb_q_gain"][j], nh)[None, :], jnp.tile(w["sb_k_gain"][j], nh)[None, :]

    saved = []
    cur = x
    for i in range(depth):
        kind, j = i % 3, i // 3
        gmix = w["mix_norm"][i:i + 1]
        if kind == 0:
            qg, kg = sb_gains(j)
            cur, sv = sb_fwd(cur, gmix, w["sb_w_qkv"][j], qg, kg, w["sb_w_o"][j], bd, f"{i}", plan)
        elif kind == 1:
            wc, bst = gm_params(j)
            cur, sv = gm_fwd(cur, gmix, w["gm_w_in"][j], w["gm_b_in"][j:j + 1], w["gm_v_gain"][j:j + 1], wc, bst,
                             w["gm_w_out"][j], f"{i}")
        else:
            cur, sv = ssm_fwd(cur, gmix, ssm_params(j), f"{i}", plan)
        cur, sv2 = ffn_fwd(cur, w["ffn_norm"][i:i + 1], w["ffn_w_gu"][i], w["ffn_w_down"][i], f"{i}")
        saved.append((sv, sv2))

    loss, dcur = loss_and_grad(cur, target, "loss")

    grads = {k: [None] * len(v) for k, v in w.items()}
    for i in reversed(range(depth)):
        kind, j = i % 3, i // 3
        sv, sv2 = saved[i]
        gmix = w["mix_norm"][i:i + 1]
        dcur, dgf, dwgu, dwdown = ffn_bwd(dcur, sv2, w["ffn_norm"][i:i + 1], w["ffn_w_gu"][i], w["ffn_w_down"][i], f"{i}")
        grads["ffn_norm"][i], grads["ffn_w_gu"][i], grads["ffn_w_down"][i] = dgf[0], dwgu, dwdown
        if kind == 0:
            qg, kg = sb_gains(j)
            dcur, dg, dwqkv, dqg, dkg, dwo = sb_bwd(dcur, sv, gmix, w["sb_w_qkv"][j], qg, kg, w["sb_w_o"][j], bd, f"{i}", plan)
            grads["sb_w_qkv"][j], grads["sb_q_gain"][j], grads["sb_k_gain"][j], grads["sb_w_o"][j] = dwqkv, dqg, dkg, dwo
        elif kind == 1:
            wc, bst = gm_params(j)
            dcur, dg, dwin, dbin, dvg, dws, dbs, dwout = gm_bwd(dcur, sv, gmix, w["gm_w_in"][j], w["gm_v_gain"][j:j + 1],
                                                                 wc, bst, w["gm_w_out"][j], f"{i}")
            grads["gm_w_in"][j], grads["gm_b_in"][j], grads["gm_v_gain"][j] = dwin, dbin[0], dvg[0]
            grads["gm_w_s"][j], grads["gm_b_s"][j], grads["gm_w_out"][j] = dws, dbs, dwout
        else:
            dcur, dg, gs = ssm_bwd(dcur, sv, gmix, ssm_params(j), f"{i}", plan)
            grads["ssm_w_in"][j], grads["ssm_conv_w"][j], grads["ssm_conv_b"][j] = gs["w_in"], gs["conv_w"], gs["conv_b"][0]
            grads["ssm_dt_bias"][j], grads["ssm_a_log"][j], grads["ssm_d"][j] = gs["dt_bias"], gs["a_log"], gs["d"]
            grads["ssm_norm_gain"][j], grads["ssm_w_out"][j] = gs["norm_gain"][0], gs["w_out"]
        grads["mix_norm"][i] = dg[0]
        mixer = {0: ("sb_w_qkv", "sb_w_o"), 1: ("gm_w_in", "gm_w_out"), 2: ("ssm_w_in", "ssm_w_out")}[kind]
        plan.layer_done(i, {(n, l): grads[n][l] for n, l in [(mixer[0], j), (mixer[1], j), ("ffn_w_gu", i), ("ffn_w_down", i)]})
    grads = {k: (v if k in MATRICES else jnp.stack(v)) for k, v in grads.items()}
    return loss, dcur[0], grads


WEIGHTS = ["mix_norm", "ffn_norm", "sb_w_qkv", "sb_q_gain", "sb_k_gain", "sb_w_o", "gm_w_in", "gm_b_in", "gm_v_gain",
           "gm_w_s", "gm_b_s", "gm_w_out", "ssm_w_in", "ssm_conv_w", "ssm_conv_b", "ssm_dt_bias", "ssm_a_log", "ssm_d",
           "ssm_norm_gain", "ssm_w_out", "ffn_w_gu", "ffn_w_down"]
SHARDED = {"sb_w_qkv": 2, "sb_w_o": 1, "gm_w_in": 2, "gm_w_out": 1, "ssm_w_in": 2, "ssm_conv_w": 2, "ssm_conv_b": 1,
           "ssm_norm_gain": 1, "ssm_w_out": 1, "ffn_w_gu": 2, "ffn_w_down": 1}
EXACT = ("ssm_conv_w", "ssm_conv_b", "ssm_norm_gain")
MATRICES = tuple(n for n in SHARDED if n not in EXACT)
COLUMN_BLOCKS = ("sb_w_qkv", "gm_w_in", "ffn_w_gu")
REPLICATED = [n for n in WEIGHTS if n not in SHARDED]
N_CHIPS = 4
N_DEV = 8
PACK_COLS = 1024


def _pack(pieces, dtype, align):
    flat = jnp.concatenate([p.reshape(-1).astype(dtype) for p in pieces])
    rows = -(-flat.shape[0] // (PACK_COLS * align)) * align
    flat = jnp.pad(flat, (0, rows * PACK_COLS - flat.shape[0]))
    return flat.reshape(rows, PACK_COLS)


def _unpack(flat, shapes):
    out, off = [], 0
    for shp in shapes:
        n = math.prod(shp)
        out.append(flat[off:off + n].reshape(shp))
        off += n
    return out


ANY = pl.BlockSpec(memory_space=pl.ANY)


def _pos():
    return lax.axis_index("x"), lax.axis_index("y"), lax.axis_index("c")


def _remote(src, dst, send, recv, k, to):
    return pltpu.make_async_remote_copy(src_ref=src, dst_ref=dst, send_sem=send.at[k], recv_sem=recv.at[k],
                                        device_id=to, device_id_type=MESH_ID)


def _comm_call(body, name, ins, out_shapes, nsem, aliases=None):
    return pl.pallas_call(
        body, name=name, out_shape=out_shapes,
        in_specs=[ANY] * len(ins), out_specs=[ANY] * len(out_shapes),
        scratch_shapes=[pltpu.SemaphoreType.DMA((nsem,)), pltpu.SemaphoreType.DMA((nsem,))],
        input_output_aliases=aliases or {},
    )(*ins)


def stage_shard(w, chip, name):
    rows, cols = w.shape
    tr = _pick(rows, (256, 352, 128))

    def kern(idx_ref, w_ref, o_ref):
        o_ref[...] = w_ref[...].astype(BF16)

    grid_spec = pltpu.PrefetchScalarGridSpec(
        num_scalar_prefetch=1, grid=(rows // tr,),
        in_specs=[pl.BlockSpec((tr, cols), lambda i, idx: (i, 0))],
        out_specs=pl.BlockSpec((None, tr, cols), lambda i, idx: (idx[0], i, 0)))
    return pl.pallas_call(
        kern, name=name, grid_spec=grid_spec,
        out_shape=jax.ShapeDtypeStruct((N_CHIPS, rows, cols), BF16),
        compiler_params=_params(("parallel",)),
    )(jnp.reshape(chip, (1,)).astype(jnp.int32), w)


class Side:
    def __init__(self, arrays, out_shapes, aliases, nsem, start, finish):
        self.arrays, self.out_shapes, self.aliases, self.nsem = list(arrays), list(out_shapes), aliases, nsem
        self.start, self.finish = start, finish


def run_side(side, name):
    n_in, n_out = len(side.arrays), len(side.out_shapes)

    def body(*refs):
        ins, outs = refs[:n_in], refs[n_in:n_in + n_out]
        send, recv = refs[n_in + n_out:]
        side.start(ins, outs, send, recv)
        side.finish(ins, outs, send, recv)

    return _comm_call(body, name, side.arrays, side.out_shapes, side.nsem, aliases=side.aliases)


def side_call(kern, side, *, name, grid, in_specs, out_specs, out_shape, scratch_shapes, args):
    if side is None:
        res = pl.pallas_call(kern, name=name, grid=grid, in_specs=in_specs, out_specs=out_specs, out_shape=out_shape,
                             scratch_shapes=scratch_shapes,
                             compiler_params=_params(("parallel",) + ("arbitrary",) * (len(grid) - 1)))(*args)
        return list(res), []
    n_in, n_out, n_scr = len(in_specs), len(out_specs), len(scratch_shapes)
    s_in, s_out = len(side.arrays), len(side.out_shapes)

    def body(*refs):
        ins, refs = refs[:n_in], refs[n_in:]
        side_ins, refs = refs[:s_in], refs[s_in:]
        outs, refs = refs[:n_out], refs[n_out:]
        side_outs, refs = refs[:s_out], refs[s_out:]
        scr, (send, recv) = refs[:n_scr], refs[n_scr:]
        first, last = None, None
        for axis, size in enumerate(grid):
            at0, at1 = pl.program_id(axis) == 0, pl.program_id(axis) == size - 1
            first = at0 if first is None else first & at0
            last = at1 if last is None else last & at1

        @pl.when(first)
        def _():
            side.start(side_ins, side_outs, send, recv)

        kern(*ins, *outs, *scr)

        @pl.when(last)
        def _():
            side.finish(side_ins, side_outs, send, recv)

    res = pl.pallas_call(
        body, name=name, grid=grid,
        in_specs=list(in_specs) + [ANY] * s_in, out_specs=list(out_specs) + [ANY] * s_out,
        out_shape=list(out_shape) + side.out_shapes,
        scratch_shapes=list(scratch_shapes) + [pltpu.SemaphoreType.DMA((side.nsem,)), pltpu.SemaphoreType.DMA((side.nsem,))],
        input_output_aliases={n_in + a: n_out + b for a, b in side.aliases.items()},
        compiler_params=_params(("arbitrary",) * len(grid)),
    )(*args, *side.arrays)
    return list(res[:n_out]), list(res[n_out:])


def gather_side(staged):
    n = len(staged)

    def plan(o_refs, send, recv):
        x, y, c = _pos()
        chips = [(1 - x, y), (x, 1 - y), (1 - x, 1 - y)]

        def part(u, chip, cc):
            half = staged[u].shape[1] // 2
            return o_refs[u].at[2 * chip[0] + chip[1], pl.ds(cc * half, half), :]

        first = [_remote(part(u, (x, y), c), part(u, (x, y), c), send, recv, 6 * u + j, (*chip, c))
                 for u in range(n) for j, chip in enumerate(chips)]
        landed = [_remote(part(u, chip, c), part(u, chip, c), send, recv, 6 * u + j, (x, y, c))
                  for u in range(n) for j, chip in enumerate(chips)]
        passed = [_remote(part(u, chip, c), part(u, chip, c), send, recv, 6 * u + 3 + j, (x, y, 1 - c))
                  for u in range(n) for j, chip in enumerate(chips)]
        handed = [_remote(part(u, chip, 1 - c), part(u, chip, 1 - c), send, recv, 6 * u + 3 + j, (x, y, c))
                  for u in range(n) for j, chip in enumerate(chips)]
        return first, landed, passed, handed

    def start(ins, outs, send, recv):
        for cp in plan(outs, send, recv)[0]:
            cp.start()

    def finish(ins, outs, send, recv):
        first, landed, passed, handed = plan(outs, send, recv)
        for got, fw in zip(landed, passed):
            got.wait_recv()
            fw.start()
        for got in handed:
            got.wait_recv()
        for cp in first + passed:
            cp.wait_send()

    outs = [jax.ShapeDtypeStruct(s.shape, s.dtype) for s in staged]
    return Side(staged, outs, {u: u for u in range(n)}, 6 * n, start, finish)


def swap_halves(gps, name):
    n = len(gps)

    def body(*refs):
        g_refs, r_refs = refs[:n], refs[n:2 * n]
        send, recv = refs[2 * n:]
        x, y, c = _pos()
        cps = []
        for u in range(n):
            half = gps[u].shape[1] // 2
            cps.append(_remote(g_refs[u].at[:, pl.ds((1 - c) * half, half), :], r_refs[u], send, recv, u, (x, y, 1 - c)))
        for cp in cps:
            cp.start()
        for cp in cps:
            cp.wait()

    outs = [jax.ShapeDtypeStruct((g.shape[0], g.shape[1] // 2, g.shape[2]), g.dtype) for g in gps]
    return _comm_call(body, name, gps, outs, n)


def scatter_side(parts):
    n = len(parts)

    def plan(p_refs, r_refs, send, recv):
        x, y, c = _pos()
        chips = [(1 - x, y), (x, 1 - y), (1 - x, 1 - y)]
        return [_remote(p_refs[u].at[2 * chip[0] + chip[1]], r_refs[u].at[j], send, recv, 3 * u + j, (*chip, c))
                for u in range(n) for j, chip in enumerate(chips)]

    def start(ins, outs, send, recv):
        for cp in plan(ins, outs, send, recv):
            cp.start()

    def finish(ins, outs, send, recv):
        for cp in plan(ins, outs, send, recv):
            cp.wait()

    outs = [jax.ShapeDtypeStruct((N_CHIPS - 1,) + p.shape[1:], p.dtype) for p in parts]
    return Side(parts, outs, {}, 3 * n, start, finish)


def join_halves(bufs):
    n = len(bufs)

    def body(*refs):
        o_refs = refs[n:2 * n]
        send, recv = refs[2 * n:]
        x, y, c = _pos()

        def rows(u, cc):
            half = bufs[u].shape[0] // 2
            return o_refs[u].at[pl.ds(cc * half, half), :]

        cps = [_remote(rows(u, c), rows(u, c), send, recv, u, (x, y, 1 - c)) for u in range(n)]
        for cp in cps:
            cp.start()
        for u in range(n):
            _remote(rows(u, 1 - c), rows(u, 1 - c), send, recv, u, (x, y, c)).wait_recv()
        for cp in cps:
            cp.wait_send()

    outs = [jax.ShapeDtypeStruct(b.shape, b.dtype) for b in bufs]
    return _comm_call(body, "join_halves", bufs, outs, n, aliases={u: u for u in range(n)})


def gather_small(sg, name):
    rows, cols = sg.shape

    def body(s_ref, o_ref, send, recv, lsem):
        x, y, c = _pos()
        me, sibling = (x, y, c), (x, y, 1 - c)
        chips = [(1 - x, y), (x, 1 - y), (1 - x, 1 - y)]

        def blk(px, py, pc):
            return o_ref.at[4 * px + 2 * py + pc]

        mine = pltpu.make_async_copy(s_ref, blk(*me), lsem)
        mine.start()
        first = [_remote(s_ref, blk(*me), send, recv, 0, sibling)]
        first += [_remote(s_ref, blk(*me), send, recv, 1 + j, (*chip, c)) for j, chip in enumerate(chips)]
        for cp in first:
            cp.start()
        passed = [_remote(blk(*chip, c), blk(*chip, c), send, recv, 4 + j, sibling) for j, chip in enumerate(chips)]
        for j, chip in enumerate(chips):
            _remote(blk(*chip, c), blk(*chip, c), send, recv, 1 + j, me).wait_recv()
            passed[j].start()
        _remote(blk(*sibling), blk(*sibling), send, recv, 0, me).wait_recv()
        for j, chip in enumerate(chips):
            _remote(blk(*chip, 1 - c), blk(*chip, 1 - c), send, recv, 4 + j, me).wait_recv()
        for cp in first + passed:
            cp.wait_send()
        mine.wait()

    return pl.pallas_call(
        body, name=name,
        out_shape=jax.ShapeDtypeStruct((N_DEV, rows, cols), sg.dtype),
        in_specs=[ANY], out_specs=ANY,
        scratch_shapes=[pltpu.SemaphoreType.DMA((N_DEV - 1,)), pltpu.SemaphoreType.DMA((N_DEV - 1,)), pltpu.SemaphoreType.DMA],
    )(sg)


def sum_cores(gp, theirs, core, chip, name):
    nch, rows, cols = gp.shape
    half = rows // 2
    tr = _pick(half, (256, 176, 128, 64))
    nb = half // tr

    def kern(idx_ref, g_ref, t_ref, own_ref, all_ref):
        k = pl.program_id(1)
        s = g_ref[...] + t_ref[...]
        all_ref[...] = s.astype(BF16)

        @pl.when(k == idx_ref[1])
        def _():
            own_ref[...] = s

    grid_spec = pltpu.PrefetchScalarGridSpec(
        num_scalar_prefetch=1, grid=(nb, nch),
        in_specs=[pl.BlockSpec((None, tr, cols), lambda i, k, idx: (k, idx[0] * nb + i, 0)),
                  pl.BlockSpec((None, tr, cols), lambda i, k, idx: (k, i, 0))],
        out_specs=[pl.BlockSpec((tr, cols), lambda i, k, idx: (i, 0)),
                   pl.BlockSpec((None, tr, cols), lambda i, k, idx: (k, i, 0))])
    return pl.pallas_call(
        kern, name=name, grid_spec=grid_spec,
        out_shape=[jax.ShapeDtypeStruct((half, cols), F32), jax.ShapeDtypeStruct((nch, half, cols), BF16)],
        compiler_params=_params(("parallel", "arbitrary")),
    )(jnp.stack([core, chip]).astype(jnp.int32), gp, theirs)


def sum_chips(own, others, core, name):
    half, cols = own.shape
    tr = _pick(half, (256, 176, 128, 64))
    nb = half // tr

    def kern(idx_ref, o_ref, a_ref, b_ref, c_ref, out_ref):
        out_ref[...] = ((o_ref[...] + a_ref[...].astype(F32)) + b_ref[...].astype(F32)) + c_ref[...].astype(F32)

    grid_spec = pltpu.PrefetchScalarGridSpec(
        num_scalar_prefetch=1, grid=(nb,),
        in_specs=[pl.BlockSpec((tr, cols), lambda i, idx: (i, 0))] +
                 [pl.BlockSpec((None, tr, cols), lambda i, idx, j=j: (j, i, 0)) for j in range(N_CHIPS - 1)],
        out_specs=pl.BlockSpec((tr, cols), lambda i, idx: (idx[0] * nb + i, 0)))
    return pl.pallas_call(
        kern, name=name, grid_spec=grid_spec,
        out_shape=jax.ShapeDtypeStruct((2 * half, cols), F32),
        compiler_params=_params(("parallel",)),
    )(jnp.reshape(core, (1,)).astype(jnp.int32), own, others, others, others)


def small_update(gath, w, m, v, name):
    def fn(*vs):
        g = vs[0]
        for t in vs[1:N_DEV]:
            g = g + t
        wv, mv, vv = vs[N_DEV:]
        m2 = ADAM_B1 * mv + (1.0 - ADAM_B1) * g
        v2 = ADAM_B2 * vv + (1.0 - ADAM_B2) * (g * g)
        m_hat = m2 / (1.0 - ADAM_B1 ** ADAM_STEP)
        v_hat = v2 / (1.0 - ADAM_B2 ** ADAM_STEP)
        return g, -ADAM_LR * (m_hat / (jnp.sqrt(v_hat) + ADAM_EPS) + ADAM_WD * wv), m2, v2

    c = w.shape[1]
    ins = [(gath[k], "row") for k in range(N_DEV)] + [(w, "row"), (m, "row"), (v, "row")]
    return rowwise(fn, ins, [(c, F32)] * 4, tr=w.shape[0] // 2, name=name)


LAYER_UNITS = {
    0: [("sb_w_qkv", 0), ("sb_w_o", 0), ("ffn_w_gu", 0), ("ffn_w_down", 0)],
    1: [("gm_w_in", 0), ("gm_w_out", 0), ("ffn_w_gu", 1), ("ffn_w_down", 1)],
    2: [("ssm_w_in", 0), ("ssm_w_out", 0), ("ffn_w_gu", 2), ("ffn_w_down", 2)],
    3: [("sb_w_qkv", 1), ("sb_w_o", 1), ("ffn_w_gu", 3), ("ffn_w_down", 3)],
}
GATHER_AT = {"sb_attn_0": (1, 2), "ssm_scan_2": (3,)}
SCATTER_AT = {"ssm_dscan_2": (3,), "sb_dattn_0": (2, 1)}


class _Plan:
    def __init__(self, ins, core, chip):
        self.core, self.chip = core, chip
        self.staged = {(n, l): stage_shard(ins[n][l], chip, f"stage_{n}_{l}")
                       for i in LAYER_UNITS for n, l in LAYER_UNITS[i]}
        self.full = {n: [None] * ins[n].shape[0] for n in MATRICES}
        self.parts = {}
        self.halves = {}
        self._fill(LAYER_UNITS[0], run_side(gather_side([self.staged[u] for u in LAYER_UNITS[0]]), "gather_0"))

    @staticmethod
    def _units(layers):
        return [u for i in layers for u in LAYER_UNITS[i]]

    def _fill(self, units, gathered):
        for (n, l), g in zip(units, gathered):
            if n in COLUMN_BLOCKS:
                self.full[n][l] = g
            elif n == "ssm_w_in":
                self.full[n][l] = jnp.concatenate([g[k] for k in range(N_CHIPS)], axis=1)
            else:
                self.full[n][l] = g.reshape(-1, g.shape[-1])

    def _reduce(self, layers, others):
        owns = [own for i in layers for own in self.parts[i][0]]
        for (n, l), own, other in zip(self._units(layers), owns, others):
            self.halves[(n, l)] = sum_chips(own, other, self.core, f"sum_chips_{n}_{l}")

    def side(self, tag):
        if tag in GATHER_AT:
            return gather_side([self.staged[u] for u in self._units(GATHER_AT[tag])])
        if tag in SCATTER_AT:
            return scatter_side([a for i in SCATTER_AT[tag] for a in self.parts[i][1]])
        return None

    def done(self, tag, results):
        if tag in GATHER_AT:
            self._fill(self._units(GATHER_AT[tag]), results)
        else:
            self._reduce(SCATTER_AT[tag], results)

    def layer_done(self, i, grads):
        gps = []
        for n, l in LAYER_UNITS[i]:
            g = grads[(n, l)]
            if n in COLUMN_BLOCKS:
                gps.append(g)
            elif n == "ssm_w_in":
                gps.append(jnp.stack(jnp.split(g, N_CHIPS, axis=1)))
            else:
                gps.append(g.reshape(N_CHIPS, -1, g.shape[-1]))
        theirs = swap_halves(gps, f"swap_halves_{i}")
        sums = [sum_cores(g, t, self.core, self.chip, f"sum_cores_{n}_{l}")
                for (n, l), g, t in zip(LAYER_UNITS[i], gps, theirs)]
        self.parts[i] = ([s[0] for s in sums], [s[1] for s in sums])
        if not any(i in layers for layers in SCATTER_AT.values()):
            self._reduce((i,), run_side(scatter_side(self.parts[i][1]), f"scatter_{i}"))

    def shard_grads(self):
        units = self._units(sorted(LAYER_UNITS))
        return dict(zip(units, join_halves([self.halves[u] for u in units])))


def _step(ins):
    x, target = ins["x"][0], ins["loss_target"][0]
    core = lax.axis_index("c")
    chip = 2 * lax.axis_index("x") + lax.axis_index("y")

    def lane_pad(v):
        return jnp.pad(v, ((0, 0), (0, PACK_COLS - v.shape[1])))

    vec_rows = [ins["ssm_conv_w"][0], ins["ssm_conv_b"], lane_pad(ins["ssm_norm_gain"])]
    blk = jnp.concatenate(vec_rows + [jnp.zeros((SUBLANES - 6, PACK_COLS), F32)], axis=0)
    per_chip = gather_small(blk, "gather_vectors")[0::2]
    ngw = ins["ssm_norm_gain"].shape[1]
    full = {
        "ssm_conv_w": jnp.concatenate([per_chip[k, 0:4] for k in range(N_CHIPS)], axis=1)[None],
        "ssm_conv_b": jnp.concatenate([per_chip[k, 4:5] for k in range(N_CHIPS)], axis=1),
        "ssm_norm_gain": jnp.concatenate([per_chip[k, 5:6, :ngw] for k in range(N_CHIPS)], axis=1),
    }

    plan = _Plan(ins, core, chip)
    full.update(plan.full)
    for n in REPLICATED:
        full[n] = ins[n]

    loss, dx, grads = local_step(x, target, full, plan)
    loss = lax.psum(loss, ALL_AXES)
    gshards = plan.shard_grads()

    small_shapes = [ins[n].shape for n in REPLICATED]
    vec_shapes = [grads[n].shape for n in EXACT]
    vec_pack = _pack([grads[n] for n in EXACT], F32, SUBLANES)
    gath = gather_small(jnp.concatenate([_pack([grads[n] for n in REPLICATED], F32, SUBLANES), vec_pack], axis=0),
                        "gather_small")
    packed = [jnp.concatenate([_pack([ins[pre + n] for n in REPLICATED], F32, SUBLANES), jnp.zeros_like(vec_pack)], axis=0)
              for pre in ("", "m_", "v_")]
    res = small_update(gath, *packed, name="small_update")
    nrep = res[0].shape[0] - vec_pack.shape[0]
    small = [dict(zip(REPLICATED, _unpack(r[:nrep].reshape(-1), small_shapes))) for r in res]
    vec_g = dict(zip(EXACT, _unpack(res[0][nrep:].reshape(-1), vec_shapes)))

    out_g, out_d, out_m, out_v = {}, {}, {}, {}
    for n in REPLICATED:
        out_g[n], out_d[n], out_m[n], out_v[n] = (s[n] for s in small)
    for n in SHARDED:
        shp = ins[n].shape
        if n in EXACT:
            g = lax.dynamic_slice_in_dim(vec_g[n], chip * shp[-1], shp[-1], axis=vec_g[n].ndim - 1)
        else:
            g = jnp.stack([gshards[(n, l)] for l in range(shp[0])])
        two = (math.prod(shp[:-1]), shp[-1])
        d2, m2, v2 = adamw(ins[n].reshape(two), g.reshape(two), ins["m_" + n].reshape(two),
                           ins["v_" + n].reshape(two), f"adamw_{n}")
        out_g[n], out_d[n], out_m[n], out_v[n] = g, d2.reshape(shp), m2.reshape(shp), v2.reshape(shp)
    return (loss, dx[None], *[out_g[n] for n in WEIGHTS], *[out_d[n] for n in WEIGHTS],
            *[out_m[n] for n in WEIGHTS], *[out_v[n] for n in WEIGHTS])


def kernel(x, mix_norm, ffn_norm, sb_w_qkv, sb_q_gain, sb_k_gain, sb_w_o, gm_w_in, gm_b_in, gm_v_gain, gm_w_s, gm_b_s, gm_w_out, ssm_w_in, ssm_conv_w, ssm_conv_b, ssm_dt_bias, ssm_a_log, ssm_d, ssm_norm_gain, ssm_w_out, ffn_w_gu, ffn_w_down, loss_target, m_mix_norm, m_ffn_norm, m_sb_w_qkv, m_sb_q_gain, m_sb_k_gain, m_sb_w_o, m_gm_w_in, m_gm_b_in, m_gm_v_gain, m_gm_w_s, m_gm_b_s, m_gm_w_out, m_ssm_w_in, m_ssm_conv_w, m_ssm_conv_b, m_ssm_dt_bias, m_ssm_a_log, m_ssm_d, m_ssm_norm_gain, m_ssm_w_out, m_ffn_w_gu, m_ffn_w_down, v_mix_norm, v_ffn_norm, v_sb_w_qkv, v_sb_q_gain, v_sb_k_gain, v_sb_w_o, v_gm_w_in, v_gm_b_in, v_gm_v_gain, v_gm_w_s, v_gm_b_s, v_gm_w_out, v_ssm_w_in, v_ssm_conv_w, v_ssm_conv_b, v_ssm_dt_bias, v_ssm_a_log, v_ssm_d, v_ssm_norm_gain, v_ssm_w_out, v_ffn_w_gu, v_ffn_w_down):
    return _step(dict(locals()))
```

```python
import functools
import math

import jax
import jax.numpy as jnp
from jax import lax
from jax.experimental import pallas as pl
from jax.experimental.pallas import tpu as pltpu

F32 = jnp.float32
BF16 = jnp.bfloat16
EPS = 1e-6
LANES = 128
SUBLANES = 8
VMEM_LIMIT = 56 * 1024 * 1024
HEAD = 64
CHUNK = 128
SB_TQ, SB_TK = 256, 256
SSD_SUB = 4
SB_DEAD = -110.0
SB_UNSEEN = -1e30
ADAM_LR, ADAM_B1, ADAM_B2, ADAM_EPS, ADAM_WD, ADAM_STEP = 0.001, 0.9, 0.999, 1e-08, 0.01, 10
MESH_ID = pl.DeviceIdType.MESH
ALL_AXES = ("x", "y", "c")


def _params(sem):
    return pltpu.CompilerParams(dimension_semantics=sem, vmem_limit_bytes=VMEM_LIMIT)


def _pick(n, cands):
    for c in cands:
        if n % c == 0:
            return c
    return n


def _dot(a, b, dims=((1,), (0,))):
    return lax.dot_general(a, b, (dims, ((), ())), preferred_element_type=F32)


def _dot_nt(a, b):
    return _dot(a, b, ((1,), (1,)))


def _dot_tn(a, b):
    return _dot(a, b, ((0,), (0,)))


def _split2(x):
    hi = x.astype(BF16)
    lo = (x - hi.astype(F32)).astype(BF16)
    return hi, lo


def _dot_x2(x, m):
    hi, lo = _split2(x)
    return _dot(hi, m) + _dot(lo, m)


def _dot_x3_left(m, x):
    h1 = x.astype(BF16)
    r1 = x - h1.astype(F32)
    h2 = r1.astype(BF16)
    h3 = (r1 - h2.astype(F32)).astype(BF16)
    return _dot(m, h1) + _dot(m, h2) + _dot(m, h3)


def _sigmoid(x):
    return 1.0 / (1.0 + jnp.exp(-x))


def _softplus(x):
    return jnp.maximum(x, 0.0) + jnp.log(1.0 + jnp.exp(-jnp.abs(x)))


def _colsum(x):
    return jnp.sum(x, axis=0, keepdims=True)


def _rowsum(x):
    return jnp.sum(x, axis=1, keepdims=True)


def _iota2(shape, dim):
    return lax.broadcasted_iota(jnp.int32, shape, dim)


MM_VMEM_BUDGET = 40 * 1024 * 1024
MM_STEP_US = 0.35
MM_HBM_BYTES_PER_US = 3.0e6
MM_VMEM_BYTES_PER_US = 1.5e6
MM_FLOPS_PER_US = 9.0e8
MXU_DIM = 256


def _mm_tiles(m, n, kk, wn, wk, a_bytes, b_bytes, has_add):
    def divisors(total, cands):
        got = [c for c in cands if total % c == 0 and c <= total]
        return got or [total]

    best = None
    for tm in divisors(m, (1024, 512, 256, 128)):
        for tn in divisors(wn, (1024, 768, 1408, 512, 256, 128)):
            for tk in divisors(wk, (4096, 2816, 2048, 1408, 1024, 768, 512, 256, 128)):
                nk = kk // tk
                vmem = 2 * (tm * tk * a_bytes + tk * tn * b_bytes + tm * tn * 4 * (2 if has_add else 1))
                vmem += tm * tn * 4 if nk > 1 else 0
                if vmem > MM_VMEM_BUDGET:
                    continue
                steps = (m // tm) * (n // tn) * nk
                a_reads = 1 if nk == 1 else n // tn
                traffic = m * kk * a_bytes * a_reads + kk * n * b_bytes * (m // tm) + m * n * 4
                fill = min(1.0, tn / MXU_DIM) * min(1.0, tm / MXU_DIM)
                compute = 2.0 * m * n * kk / (MM_FLOPS_PER_US * fill)
                cost = steps * MM_STEP_US + max(compute, traffic / MM_HBM_BYTES_PER_US)
                if nk > 1:
                    cost += steps * tm * tn * 8 / MM_VMEM_BYTES_PER_US
                if best is None or cost < best[0]:
                    best = (cost, tm, tn, tk)
    return best[1:]


def mm(a, b, *, ta=False, tb=False, add=None, bias=None, b_chunks=False, out_chunks=False, out_dtype=F32, name):
    if ta:
        kk, m = a.shape
    else:
        m, kk = a.shape
    nch, wide = 1, None
    if b_chunks:
        nch, rows_b, wide = b.shape
        kb, n = (rows_b, nch * wide) if not tb else (nch * wide, rows_b)
    elif tb:
        n, kb = b.shape
    else:
        kb, n = b.shape
    if out_chunks:
        nch, wide = N_CHIPS, n // N_CHIPS
    assert kk == kb, (a.shape, b.shape, ta, tb)
    has_add, has_bias = add is not None, bias is not None
    tm, tn, tk = _mm_tiles(m, n, kk, wide if (wide and not tb) or out_chunks else n, wide if (wide and tb) else kk,
                           a.dtype.itemsize, b.dtype.itemsize, has_add)
    nk = kk // tk
    dims = ((0 if ta else 1,), (1 if tb else 0,))

    def kern(*refs):
        a_ref, b_ref = refs[0], refs[1]
        rest = list(refs[2:])
        add_ref = rest.pop(0) if has_add else None
        bias_ref = rest.pop(0) if has_bias else None
        o_ref = rest[0]
        part = _dot(a_ref[...].astype(BF16), b_ref[...].astype(BF16), dims)

        def finish(r):
            if has_add:
                r = r + add_ref[...]
            if has_bias:
                r = r + bias_ref[...]
            o_ref[...] = r.astype(out_dtype)

        if nk == 1:
            finish(part)
        else:
            acc_ref = rest[1]
            k = pl.program_id(2)

            @pl.when(k == 0)
            def _():
                acc_ref[...] = part

            @pl.when((k > 0) & (k < nk - 1))
            def _():
                acc_ref[...] += part

            @pl.when(k == nk - 1)
            def _():
                finish(acc_ref[...] + part)

    a_spec = pl.BlockSpec((tk, tm), lambda i, j, k: (k, i)) if ta else pl.BlockSpec((tm, tk), lambda i, j, k: (i, k))
    if b_chunks and tb:
        per = wide // tk
        b_spec = pl.BlockSpec((None, tn, tk), lambda i, j, k: (k // per, j, k % per))
    elif b_chunks:
        per = wide // tn
        b_spec = pl.BlockSpec((None, tk, tn), lambda i, j, k: (j // per, k, j % per))
    elif tb:
        b_spec = pl.BlockSpec((tn, tk), lambda i, j, k: (j, k))
    else:
        b_spec = pl.BlockSpec((tk, tn), lambda i, j, k: (k, j))
    if out_chunks:
        per_o = wide // tn
        out_spec = pl.BlockSpec((None, tm, tn), lambda i, j, k: (j // per_o, i, j % per_o))
        out_shape = jax.ShapeDtypeStruct((nch, m, wide), out_dtype)
    else:
        out_spec = pl.BlockSpec((tm, tn), lambda i, j, k: (i, j))
        out_shape = jax.ShapeDtypeStruct((m, n), out_dtype)
    in_specs, args = [a_spec, b_spec], [a, b]
    if has_add:
        in_specs.append(pl.BlockSpec((tm, tn), lambda i, j, k: (i, j)))
        args.append(add)
    if has_bias:
        in_specs.append(pl.BlockSpec((1, tn), lambda i, j, k: (0, j)))
        args.append(bias)
    return pl.pallas_call(
        kern,
        name=name,
        grid=(m // tm, n // tn, nk),
        in_specs=in_specs,
        out_specs=out_spec,
        out_shape=out_shape,
        scratch_shapes=[pltpu.VMEM((tm, tn), F32)] if nk > 1 else [],
        compiler_params=_params(("parallel", "parallel", "arbitrary")),
    )(*args)


def rowwise(fn, ins, outs, accs=(), *, tr, name):
    rows = [a for a, kind in ins if kind == "row"][0].shape[0]
    tr = min(tr, rows)
    assert rows % tr == 0 and tr % SUBLANES == 0, (rows, tr)
    n = rows // tr
    n_in, n_out = len(ins), len(outs)
    kinds = [kind for _, kind in ins]

    def kern(*refs):
        i = pl.program_id(0)
        vals = []
        for ref, kind in zip(refs[:n_in], kinds):
            v = ref[...]
            if kind == "prev":
                v = v * (i > 0).astype(v.dtype)
            elif kind == "next":
                v = v * (i < n - 1).astype(v.dtype)
            vals.append(v)
        res = fn(*vals)
        for ref, r in zip(refs[n_in:n_in + n_out], res[:n_out]):
            ref[...] = r.astype(ref.dtype)
        if accs:
            acc_refs = refs[n_in + n_out:]

            @pl.when(i == 0)
            def _():
                for ref in acc_refs:
                    ref[...] = jnp.zeros_like(ref)

            for ref, r in zip(acc_refs, res[n_out:]):
                ref[...] += r

    in_specs = []
    for a, kind in ins:
        if kind == "row":
            in_specs.append(pl.BlockSpec((tr, a.shape[1]), lambda i: (i, 0)))
        elif kind == "full":
            in_specs.append(pl.BlockSpec(a.shape, lambda i, nd=a.ndim: (0,) * nd))
        elif kind == "prev":
            in_specs.append(pl.BlockSpec((SUBLANES, a.shape[1]),
                                         lambda i: (jnp.maximum(i * (tr // SUBLANES) - 1, 0), 0)))
        else:
            in_specs.append(pl.BlockSpec((SUBLANES, a.shape[1]),
                                         lambda i: (jnp.minimum((i + 1) * (tr // SUBLANES), rows // SUBLANES - 1), 0)))
    out_specs = [pl.BlockSpec((tr, c), lambda i: (i, 0)) for c, _ in outs]
    out_specs += [pl.BlockSpec((r, c), lambda i: (0, 0)) for r, c in accs]
    out_shape = [jax.ShapeDtypeStruct((rows, c), dt) for c, dt in outs]
    out_shape += [jax.ShapeDtypeStruct((r, c), F32) for r, c in accs]
    res = pl.pallas_call(
        kern,
        name=name,
        grid=(n,),
        in_specs=in_specs,
        out_specs=out_specs,
        out_shape=out_shape,
        compiler_params=_params(("arbitrary",) if accs else ("parallel",)),
    )(*[a for a, _ in ins])
    return res


def rms_fwd(x, g, name):
    def fn(xv, gv):
        r = lax.rsqrt(jnp.mean(xv * xv, axis=1, keepdims=True) + EPS)
        return (xv * r * gv,)

    return rowwise(fn, [(x, "row"), (g, "full")], [(x.shape[1], BF16)], tr=512, name=name)[0]


def rms_bwd(x, g, dy, dres, name):
    def fn(xv, gv, dyv, drv):
        r = lax.rsqrt(jnp.mean(xv * xv, axis=1, keepdims=True) + EPS)
        xh = xv * r
        dyg = dyv * gv
        dx = drv + r * (dyg - xh * jnp.mean(dyg * xh, axis=1, keepdims=True))
        return dx, dx, _colsum(dyv * xh)

    c = x.shape[1]
    dx, dxb, dg = rowwise(fn, [(x, "row"), (g, "full"), (dy, "row"), (dres, "row")], [(c, F32), (c, BF16)], [(1, c)],
                          tr=256, name=name)
    return (dx, dxb), dg


def swiglu_fwd(gu, name):
    hid = gu.shape[1] // 2

    def fn(v):
        g, u = v[:, :hid].astype(F32), v[:, hid:].astype(F32)
        return (g * _sigmoid(g) * u,)

    return rowwise(fn, [(gu, "row")], [(hid, BF16)], tr=256, name=name)[0]


def swiglu_bwd(gu, da, name):
    hid = gu.shape[1] // 2

    def fn(v, d):
        g, u = v[:, :hid].astype(F32), v[:, hid:].astype(F32)
        s = _sigmoid(g)
        dg = d * u * s * (1.0 + g * (1.0 - s))
        du = d * g * s
        return (jnp.concatenate([dg, du], axis=1),)

    return rowwise(fn, [(gu, "row"), (da, "row")], [(2 * hid, BF16)], tr=256, name=name)[0]


def loss_and_grad(y, t, name):
    d = y.shape[1]

    def fn(yv, tv):
        e = yv - tv
        part = jnp.sum(_colsum(e * e), axis=1, keepdims=True) * (0.5 / d)
        dy = e * (1.0 / d)
        return dy, dy, jnp.broadcast_to(part, (SUBLANES, LANES))

    dy, dyb, acc = rowwise(fn, [(y, "row"), (t, "row")], [(d, F32), (d, BF16)], [(SUBLANES, LANES)], tr=512, name=name)
    return acc[0, 0], (dy, dyb)


def adamw(w, g, m, v, name):
    def fn(wv, gv, mv, vv):
        m2 = ADAM_B1 * mv + (1.0 - ADAM_B1) * gv
        v2 = ADAM_B2 * vv + (1.0 - ADAM_B2) * (gv * gv)
        m_hat = m2 / (1.0 - ADAM_B1 ** ADAM_STEP)
        v_hat = v2 / (1.0 - ADAM_B2 ** ADAM_STEP)
        delta = -ADAM_LR * (m_hat / (jnp.sqrt(v_hat) + ADAM_EPS) + ADAM_WD * wv)
        return delta, m2, v2

    rows, c = w.shape
    tr = _pick(rows, (256, 128, 64, 32, 16, 8)) if rows % SUBLANES == 0 else rows
    if rows % SUBLANES:
        return _whole(fn, [w, g, m, v], [(w.shape, F32)] * 3, name=name)
    return rowwise(fn, [(w, "row"), (g, "row"), (m, "row"), (v, "row")], [(c, F32)] * 3, tr=tr, name=name)


def _whole(fn, ins, outs, *, name):
    n_in = len(ins)

    def kern(*refs):
        res = fn(*[r[...] for r in refs[:n_in]])
        for ref, r in zip(refs[n_in:], res):
            ref[...] = r.astype(ref.dtype)

    return pl.pallas_call(
        kern,
        name=name,
        out_shape=[jax.ShapeDtypeStruct(s, dt) for s, dt in outs],
        compiler_params=pltpu.CompilerParams(vmem_limit_bytes=VMEM_LIMIT),
    )(*ins)


def ffn_fwd(x, g, wgu, wdown, tag):
    h = rms_fwd(x, g, f"ffn_rms_{tag}")
    gu = mm(h, wgu, b_chunks=True, out_dtype=BF16, name=f"ffn_gu_{tag}")
    a = swiglu_fwd(gu, f"ffn_act_{tag}")
    xn = mm(a, wdown, add=x, name=f"ffn_down_{tag}")
    return xn, (x, h, gu, a)


def ffn_bwd(dxn, saved, g, wgu, wdown, tag):
    x, h, gu, a = saved
    dxn, dxb = dxn
    da = mm(dxb, wdown, tb=True, name=f"ffn_da_{tag}")
    dwdown = mm(a, dxb, ta=True, out_dtype=BF16, name=f"ffn_dwdown_{tag}")
    dgu = swiglu_bwd(gu, da, f"ffn_dact_{tag}")
    dh = mm(dgu, wgu, tb=True, b_chunks=True, name=f"ffn_dh_{tag}")
    dwgu = mm(h, dgu, ta=True, out_dtype=BF16, out_chunks=True, name=f"ffn_dwgu_{tag}")
    dx, dg = rms_bwd(x, g, dh, dxn, f"ffn_drms_{tag}")
    return dx, dg, dwgu, dwdown


def _head_blockdiag(c):
    i = jnp.arange(c) // HEAD
    return (i[:, None] == i[None, :]).astype(BF16)


def _head_sums(x, bd):
    return jnp.concatenate([_dot_x2(x[:, g * LANES:(g + 1) * LANES], bd) for g in range(x.shape[1] // LANES)], axis=1)


def qknorm_fwd(qkv, qg, kg, bd, name):
    d = qkv.shape[1] // 3
    scale = 1.0 / math.sqrt(HEAD)

    def fn(v, qgv, kgv, bdv):
        v = v.astype(F32)
        q, k, vv = v[:, :d], v[:, d:2 * d], v[:, 2 * d:]
        rq = lax.rsqrt(_head_sums(q * q, bdv) * (1.0 / HEAD) + EPS)
        rk = lax.rsqrt(_head_sums(k * k, bdv) * (1.0 / HEAD) + EPS)
        return q * rq * qgv * scale, k * rk * kgv, vv

    return rowwise(fn, [(qkv, "row"), (qg, "full"), (kg, "full"), (bd, "full")],
                   [(d, BF16), (d, BF16), (d, BF16)], tr=256, name=name)


def qknorm_bwd(qkv, dqs, dkn, dv, qg, kg, bd, name):
    d = qkv.shape[1] // 3
    scale = 1.0 / math.sqrt(HEAD)

    def one(xv, gv, dyv, bdv):
        r = lax.rsqrt(_head_sums(xv * xv, bdv) * (1.0 / HEAD) + EPS)
        xh = xv * r
        dyg = dyv * gv
        dx = r * (dyg - xh * (_head_sums(dyg * xh, bdv) * (1.0 / HEAD)))
        return dx, _colsum(dyv * xh)

    def fn(v, dqv, dkv, dvv, qgv, kgv, bdv):
        v = v.astype(F32)
        q, k = v[:, :d], v[:, d:2 * d]
        dq, dqg = one(q, qgv, dqv * scale, bdv)
        dk, dkg = one(k, kgv, dkv, bdv)
        return jnp.concatenate([dq, dk, dvv], axis=1), dqg, dkg

    return rowwise(fn, [(qkv, "row"), (dqs, "row"), (dkn, "row"), (dv, "row"), (qg, "full"), (kg, "full"), (bd, "full")],
                   [(3 * d, BF16)], [(1, d), (1, d)], tr=256, name=name)


def _sb_tile(qh, k, mask, tri_gt):
    z = _dot_nt(qh, k)
    sp = jnp.log(1.0 + jnp.exp(-jnp.abs(z)))
    lb = jnp.minimum(z, 0.0) - sp
    l1 = jnp.where(mask, lb - z, 0.0)
    suf = _dot_x2(l1, tri_gt)
    return lb, l1, suf


def _sb_setup(tq, tk):
    row, col = _iota2((tq, tk), 0), _iota2((tq, tk), 1)
    lane = _iota2((1, LANES), 1)
    halves = [(lane < HEAD).astype(BF16), (lane >= HEAD).astype(BF16)]
    lane_q = _iota2((tq, LANES), 1) + jnp.minimum(_iota2((tq, LANES), 0), 0)
    return row, col, halves, lane_q


def sb_attn_fwd(qs, kn, vb, name, side=None):
    s, d = qs.shape
    tq, tk = min(SB_TQ, s), min(SB_TK, s)
    nq = s // tq
    assert s // tk <= LANES and s % tq == 0 and s % tk == 0

    def kern(q_ref, k_ref, v_ref, o_ref, rs_ref, acc_ref):
        i = pl.program_id(1)
        row, col, halves, lane_q = _sb_setup(tq, tk)
        tri_gt = (_iota2((tk, tk), 0) > _iota2((tk, tk), 1)).astype(BF16)
        q = q_ref[...]
        qh = [q * hm for hm in halves]
        acc_ref[...] = jnp.zeros_like(acc_ref)
        rs_ref[...] = jnp.full(rs_ref.shape, SB_UNSEEN, F32)
        nkb = (i + 1) * (tq // tk)

        def more(st):
            return (st[0] < nkb) & (st[1] > SB_DEAD)

        def step(st):
            n, r = st[0], list(st[2:])
            kb = nkb - 1 - n
            ks = pl.multiple_of(kb * tk, tk)
            k = k_ref[pl.ds(ks, tk), :]
            v = v_ref[pl.ds(ks, tk), :]
            mask = col < row + (i * tq - kb * tk)
            at_kb = lane_q == kb
            for hh in range(2):
                lb, l1, suf = _sb_tile(qh[hh], k, mask, tri_gt)
                w = jnp.where(mask, jnp.exp(lb + suf + r[hh]), 0.0)
                acc_ref[...] += _dot(w.astype(BF16), v * halves[hh])
                rs_ref[hh] = jnp.where(at_kb, r[hh], rs_ref[hh])
                r[hh] = r[hh] + _rowsum(l1)
            return (n + 1, jnp.maximum(jnp.max(r[0]), jnp.max(r[1])), r[0], r[1])

        z1 = jnp.zeros((tq, 1), F32)
        lax.while_loop(more, step, (jnp.int32(0), jnp.float32(0.0), z1, z1))
        o_ref[...] = acc_ref[...].astype(BF16)

    nh2 = d // LANES
    return side_call(
        kern, side,
        name=name,
        grid=(nh2, nq),
        in_specs=[pl.BlockSpec((tq, LANES), lambda h, i: (i, h)),
                  pl.BlockSpec((s, LANES), lambda h, i: (0, h)),
                  pl.BlockSpec((s, LANES), lambda h, i: (0, h))],
        out_specs=[pl.BlockSpec((tq, LANES), lambda h, i: (i, h)),
                   pl.BlockSpec((None, 2, tq, LANES), lambda h, i: (h, 0, i, 0))],
        out_shape=[jax.ShapeDtypeStruct((s, d), BF16), jax.ShapeDtypeStruct((nh2, 2, s, LANES), F32)],
        scratch_shapes=[pltpu.VMEM((tq, LANES), F32)],
        args=(qs, kn, vb))


def sb_attn_bwd(qs, kn, vb, rsave, do, name, side=None):
    s, d = qs.shape
    tq, tk = min(SB_TQ, s), min(SB_TK, s)
    nq = s // tq

    def kern(q_ref, k_ref, v_ref, rs_ref, do_ref, dq_ref, dk_ref, dv_ref):
        i = pl.program_id(1)

        @pl.when(i == 0)
        def _():
            dk_ref[...] = jnp.zeros_like(dk_ref)
            dv_ref[...] = jnp.zeros_like(dv_ref)

        row, col, halves, lane_q = _sb_setup(tq, tk)
        tri_gt = (_iota2((tk, tk), 0) > _iota2((tk, tk), 1)).astype(BF16)
        tri_lt = (_iota2((tk, tk), 0) < _iota2((tk, tk), 1)).astype(BF16)
        q = q_ref[...]
        qh = [q * hm for hm in halves]
        dov = do_ref[...].astype(BF16)
        doh = [dov * hm for hm in halves]
        dq_ref[...] = jnp.zeros_like(dq_ref)
        nkb = (i + 1) * (tq // tk)
        top = jnp.maximum(jnp.max(rs_ref[0], axis=0, keepdims=True), jnp.max(rs_ref[1], axis=0, keepdims=True))
        dead = (top <= SB_DEAD) & (_iota2((1, LANES), 1) < nkb)
        kstart = jnp.minimum(jnp.sum(dead.astype(F32)).astype(jnp.int32), nkb)

        def step(kb, ep):
            ep = list(ep)
            ks = pl.multiple_of(kb * tk, tk)
            k = k_ref[pl.ds(ks, tk), :]
            v = v_ref[pl.ds(ks, tk), :]
            mask = col < row + (i * tq - kb * tk)
            at_kb = lane_q == kb
            for hh in range(2):
                lb, l1, suf = _sb_tile(qh[hh], k, mask, tri_gt)
                r = _rowsum(jnp.where(at_kb, rs_ref[hh], 0.0))
                lbm = jnp.where(mask, lb, SB_UNSEEN)
                w = jnp.exp(lbm + suf + r)
                e = _dot_nt(doh[hh], v) * w
                pe = ep[hh] + _dot(e.astype(BF16), tri_lt)
                beta = jnp.exp(lbm)
                dz = (e - beta * (e + pe)).astype(BF16)
                dq_ref[...] += _dot(dz, k * halves[hh])
                dk_ref[pl.ds(ks, tk), :] += _dot_tn(dz, qh[hh])
                dv_ref[pl.ds(ks, tk), :] += _dot_tn(w.astype(BF16), doh[hh])
                ep[hh] = ep[hh] + _rowsum(e)
            return tuple(ep)

        z1 = jnp.zeros((tq, 1), F32)
        lax.fori_loop(kstart, nkb, step, (z1, z1))

    nh2 = d // LANES
    return side_call(
        kern, side,
        name=name,
        grid=(nh2, nq),
        in_specs=[pl.BlockSpec((tq, LANES), lambda h, i: (i, h)),
                  pl.BlockSpec((s, LANES), lambda h, i: (0, h)),
                  pl.BlockSpec((s, LANES), lambda h, i: (0, h)),
                  pl.BlockSpec((None, 2, tq, LANES), lambda h, i: (h, 0, i, 0)),
                  pl.BlockSpec((tq, LANES), lambda h, i: (i, h))],
        out_specs=[pl.BlockSpec((tq, LANES), lambda h, i: (i, h)),
                   pl.BlockSpec((s, LANES), lambda h, i: (0, h)),
                   pl.BlockSpec((s, LANES), lambda h, i: (0, h))],
        out_shape=[jax.ShapeDtypeStruct((s, d), F32)] * 3,
        scratch_shapes=[],
        args=(qs, kn, vb, rsave, do))


def _hooked(plan, tag, call, *args):
    side = plan.side(tag)
    outs, side_outs = call(*args, tag, side)
    if side is not None:
        plan.done(tag, side_outs)
    return outs


def sb_fwd(x, g, wqkv, qg, kg, wo, bd, tag, plan):
    h = rms_fwd(x, g, f"sb_rms_{tag}")
    qkv = mm(h, wqkv, b_chunks=True, out_dtype=BF16, name=f"sb_qkv_{tag}")
    qs, kn, vb = qknorm_fwd(qkv, qg, kg, bd, f"sb_qknorm_{tag}")
    o, rsave = _hooked(plan, f"sb_attn_{tag}", sb_attn_fwd, qs, kn, vb)
    xn = mm(o, wo, add=x, name=f"sb_out_{tag}")
    return xn, (x, h, qkv, qs, kn, vb, rsave, o)


def sb_bwd(dxn, saved, g, wqkv, qg, kg, wo, bd, tag, plan):
    x, h, qkv, qs, kn, vb, rsave, o = saved
    dxn, dxb = dxn
    do = mm(dxb, wo, tb=True, name=f"sb_do_{tag}")
    dwo = mm(o, dxb, ta=True, out_dtype=BF16, name=f"sb_dwo_{tag}")
    dqs, dkn, dv = _hooked(plan, f"sb_dattn_{tag}", sb_attn_bwd, qs, kn, vb, rsave, do)
    dqkv, dqg, dkg = qknorm_bwd(qkv, dqs, dkn, dv, qg, kg, bd, f"sb_dqknorm_{tag}")
    dh = mm(dqkv, wqkv, tb=True, b_chunks=True, name=f"sb_dh_{tag}")
    dwqkv = mm(h, dqkv, ta=True, out_dtype=BF16, out_chunks=True, name=f"sb_dwqkv_{tag}")
    dx, dg = rms_bwd(x, g, dh, dxn, f"sb_drms_{tag}")
    nh = dqg.shape[1] // HEAD
    return dx, dg, dwqkv, dqg.reshape(nh, HEAD).sum(0), dkg.reshape(nh, HEAD).sum(0), dwo


def _gelu(x):
    return 0.5 * x * (1.0 + lax.erf(x * (1.0 / math.sqrt(2.0))))


def _gelu_grad(x):
    return 0.5 * (1.0 + lax.erf(x * (1.0 / math.sqrt(2.0)))) + x * jnp.exp(-0.5 * x * x) * (1.0 / math.sqrt(2.0 * math.pi))


def gm_act_fwd(pre, vg, name):
    half = pre.shape[1] // 2

    def fn(p, vgv):
        p = p.astype(F32)
        u = _gelu(p[:, :half])
        v = _gelu(p[:, half:])
        r = lax.rsqrt(jnp.mean(v * v, axis=1, keepdims=True) + EPS)
        return u, v * r * vgv

    return rowwise(fn, [(pre, "row"), (vg, "full")], [(half, F32), (half, BF16)], tr=256, name=name)


def gm_act_bwd(pre, du, dvn, vg, name):
    half = pre.shape[1] // 2

    def fn(p, duv, dvnv, vgv):
        p = p.astype(F32)
        pu, pv = p[:, :half], p[:, half:]
        v = _gelu(pv)
        r = lax.rsqrt(jnp.mean(v * v, axis=1, keepdims=True) + EPS)
        vh = v * r
        dyg = dvnv * vgv
        dv = r * (dyg - vh * jnp.mean(dyg * vh, axis=1, keepdims=True))
        dpre = jnp.concatenate([duv * _gelu_grad(pu), dv * _gelu_grad(pv)], axis=1)
        return dpre, _colsum(dvnv * vh), _colsum(dpre)

    return rowwise(fn, [(pre, "row"), (du, "row"), (dvn, "row"), (vg, "full")],
                   [(2 * half, BF16)], [(1, half), (1, 2 * half)], tr=256, name=name)


def gm_spatial_fwd(u, vn, wc, bst, name):
    s, c = u.shape
    t = CHUNK
    ng = c // LANES

    def kern(u_ref, v_ref, w_ref, b_ref, o_ref):
        for g in range(ng):
            sl = slice(g * LANES, (g + 1) * LANES)
            mixed = _dot(w_ref[g], v_ref[:, sl]) + b_ref[:, sl]
            o_ref[:, sl] = (u_ref[:, sl] * mixed).astype(BF16)

    return pl.pallas_call(
        kern,
        name=name,
        grid=(s // t,),
        in_specs=[pl.BlockSpec((t, c), lambda i: (i, 0)), pl.BlockSpec((t, c), lambda i: (i, 0)),
                  pl.BlockSpec(wc.shape, lambda i: (0, 0, 0)), pl.BlockSpec(bst.shape, lambda i: (0, 0))],
        out_specs=pl.BlockSpec((t, c), lambda i: (i, 0)),
        out_shape=jax.ShapeDtypeStruct((s, c), BF16),
        compiler_params=_params(("parallel",)),
    )(u, vn, wc, bst)


def gm_spatial_bwd(dgate, u, vn, wc, bst, name):
    s, c = u.shape
    t = CHUNK
    ng = c // LANES

    def kern(dg_ref, u_ref, v_ref, w_ref, b_ref, du_ref, dv_ref, dw_ref, db_ref):
        i = pl.program_id(0)

        @pl.when(i == 0)
        def _():
            dw_ref[...] = jnp.zeros_like(dw_ref)
            db_ref[...] = jnp.zeros_like(db_ref)

        for g in range(ng):
            sl = slice(g * LANES, (g + 1) * LANES)
            vg = v_ref[:, sl]
            dgv = dg_ref[:, sl]
            mixed = _dot(w_ref[g], vg) + b_ref[:, sl]
            du_ref[:, sl] = dgv * mixed
            dmix = dgv * u_ref[:, sl]
            dmb = dmix.astype(BF16)
            dv_ref[:, sl] = _dot_tn(w_ref[g], dmb)
            dw_ref[g] += _dot_nt(dmb, vg)
            db_ref[:, sl] += dmix

    return pl.pallas_call(
        kern,
        name=name,
        grid=(s // t,),
        in_specs=[pl.BlockSpec((t, c), lambda i: (i, 0))] * 3 +
                 [pl.BlockSpec(wc.shape, lambda i: (0, 0, 0)), pl.BlockSpec(bst.shape, lambda i: (0, 0))],
        out_specs=[pl.BlockSpec((t, c), lambda i: (i, 0)), pl.BlockSpec((t, c), lambda i: (i, 0)),
                   pl.BlockSpec(wc.shape, lambda i: (0, 0, 0)), pl.BlockSpec(bst.shape, lambda i: (0, 0))],
        out_shape=[jax.ShapeDtypeStruct((s, c), F32), jax.ShapeDtypeStruct((s, c), F32),
                   jax.ShapeDtypeStruct(wc.shape, F32), jax.ShapeDtypeStruct(bst.shape, F32)],
        compiler_params=_params(("arbitrary",)),
    )(dgate, u, vn, wc, bst)


def gm_fwd(x, g, w_in, b_in, vg, wc, bst, w_out, tag):
    h = rms_fwd(x, g, f"gm_rms_{tag}")
    pre = mm(h, w_in, bias=b_in, b_chunks=True, out_dtype=BF16, name=f"gm_in_{tag}")
    u, vn = gm_act_fwd(pre, vg, f"gm_act_{tag}")
    gate = gm_spatial_fwd(u, vn, wc, bst, f"gm_spatial_{tag}")
    xn = mm(gate, w_out, add=x, name=f"gm_out_{tag}")
    return xn, (x, h, pre, u, vn, gate)


def gm_bwd(dxn, saved, g, w_in, vg, wc, bst, w_out, tag):
    x, h, pre, u, vn, gate = saved
    dxn, dxb = dxn
    dgate = mm(dxb, w_out, tb=True, name=f"gm_dgate_{tag}")
    dwout = mm(gate, dxb, ta=True, out_dtype=BF16, name=f"gm_dwout_{tag}")
    du, dvn, dws, dbst = gm_spatial_bwd(dgate, u, vn, wc, bst, f"gm_dspatial_{tag}")
    dpre, dvg, dbin = gm_act_bwd(pre, du, dvn, vg, f"gm_dact_{tag}")
    dh = mm(dpre, w_in, tb=True, b_chunks=True, name=f"gm_dh_{tag}")
    dwin = mm(h, dpre, ta=True, out_dtype=BF16, out_chunks=True, name=f"gm_dwin_{tag}")
    dx, dg = rms_bwd(x, g, dh, dxn, f"gm_drms_{tag}")
    ng = wc.shape[0]
    dws = jnp.where(jnp.tril(jnp.ones((CHUNK, CHUNK), bool)), dws, 0.0)
    dbs = dbst.reshape(CHUNK, ng, LANES).sum(-1).T
    return dx, dg, dwin, dbin, dvg, dws, dbs, dwout


def _conv_taps(xv, prev):
    cat = jnp.concatenate([prev, xv], axis=0)
    return [pltpu.roll(cat, sh, 0)[SUBLANES:] for sh in (3, 2, 1)] + [xv]


def conv_fwd(xbc, ws, b, d_inner, name):
    c = xbc.shape[1]
    nst = (c - d_inner) // 2

    def fn(xv, prev, w0, w1, w2, w3, bv):
        taps = _conv_taps(xv, prev)
        pre = bv + w0 * taps[0] + w1 * taps[1] + w2 * taps[2] + w3 * taps[3]
        out = pre * _sigmoid(pre)
        return out[:, :d_inner], out[:, d_inner:d_inner + nst], out[:, d_inner + nst:]

    return rowwise(fn, [(xbc, "row"), (xbc, "prev")] + [(w, "full") for w in ws] + [(b, "full")],
                   [(d_inner, F32), (nst, F32), (nst, F32)], tr=256, name=name)


def conv_bwd_pre(xbc, ws, b, dxs_a, dxs_b, db_m, dc_m, name):
    c = xbc.shape[1]

    def fn(xv, prev, w0, w1, w2, w3, bv, da, db2, dbm, dcm):
        taps = _conv_taps(xv, prev)
        pre = bv + w0 * taps[0] + w1 * taps[1] + w2 * taps[2] + w3 * taps[3]
        sg = _sigmoid(pre)
        dout = jnp.concatenate([da + db2, dbm, dcm], axis=1)
        dpre = dout * sg * (1.0 + pre * (1.0 - sg))
        return (dpre,) + tuple(_colsum(dpre * tp) for tp in taps) + (_colsum(dpre),)

    return rowwise(fn, [(xbc, "row"), (xbc, "prev")] + [(w, "full") for w in ws] +
                   [(b, "full"), (dxs_a, "row"), (dxs_b, "row"), (db_m, "row"), (dc_m, "row")],
                   [(c, F32)], [(1, c)] * 5, tr=256, name=name)


def conv_bwd_in(dpre, ws, name):
    c = dpre.shape[1]

    def fn(dv, nxt, w0, w1, w2, w3):
        cat = jnp.concatenate([dv, nxt], axis=0)
        n = cat.shape[0]
        up = [pltpu.roll(cat, n - sh, 0)[:dv.shape[0]] for sh in (1, 2, 3)]
        return (w3 * dv + w2 * up[0] + w1 * up[1] + w0 * up[2],)

    return rowwise(fn, [(dpre, "row"), (dpre, "next")] + [(w, "full") for w in ws], [(c, BF16)], tr=256, name=name)[0]


def ssd_pre(dtr, bias, alog, name):
    def fn(d, bv, al, tri):
        dt = _softplus(d + bv)
        a = dt * (-jnp.exp(al))
        return dt, _dot_x3_left(tri, a)

    tri = jnp.tril(jnp.ones((CHUNK, CHUNK), BF16))
    return rowwise(fn, [(dtr, "row"), (bias, "full"), (alog, "full"), (tri, "full")],
                   [(LANES, F32), (LANES, F32)], tr=CHUNK, name=name)


def _ssd_layouts(v, ngroups, hpg):
    s = v.shape[0]
    col = v[:, :ngroups * hpg].T.reshape(ngroups, hpg, s, 1)
    return jnp.broadcast_to(col, (ngroups, hpg, s, LANES))


def _ssd_rowform(acum, ngroups, hpg):
    s = acum.shape[0]
    nc = s // CHUNK
    a = acum[:, :ngroups * hpg].reshape(nc, CHUNK, ngroups, hpg).transpose(2, 0, 3, 1)
    last = jnp.broadcast_to(a[..., CHUNK - 1:], a.shape)
    return jnp.concatenate([a, last], axis=2)


def ssd_chunk_fwd(xs, bm, cm, col_a, col_dt, rowf, name, side=None):
    s, d_inner = xs.shape
    ln = CHUNK
    nc = s // ln
    nsub = _pick(nc, (SSD_SUB, 2, 1))
    rows = nsub * ln
    ng, hpg = col_a.shape[0], col_a.shape[1]
    gw = d_inner // ng
    assert gw == hpg * HEAD and gw % LANES == 0 and bm.shape[1] == ng * LANES

    def kern(x_ref, b_ref, c_ref, ca_ref, cd_ref, rf_ref, y_ref, hp_ref, h_scr):
        @pl.when(pl.program_id(1) == 0)
        def _():
            h_scr[...] = jnp.zeros_like(h_scr)

        causal = _iota2((ln, ln), 0) >= _iota2((ln, ln), 1)
        lane = _iota2((1, LANES), 1)
        for sc in range(nsub):
            rs = slice(sc * ln, (sc + 1) * ln)
            bb = b_ref[rs, :].astype(BF16)
            cbf = c_ref[rs, :].astype(BF16)
            cb = _dot_nt(cbf, bb)
            ys = [jnp.zeros((ln, LANES), F32) for _ in range(gw // LANES)]
            for r in range(hpg):
                j, hf = divmod(r, LANES // HEAD)
                mh = ((lane >= HEAD * hf) & (lane < HEAD * (hf + 1))).astype(F32)
                ac = ca_ref[r, rs, :]
                ar = rf_ref[sc, pl.ds(r, 1), :]
                aend = rf_ref[sc, pl.ds(4 + r, 1), :]
                dm = jnp.exp(jnp.minimum(ac - ar, 0.0))
                m = jnp.where(causal, cb * dm, 0.0).astype(BF16)
                xdt = x_ref[rs, j * LANES:(j + 1) * LANES] * cd_ref[r, rs, :] * mh
                h = h_scr[r]
                hp_ref[sc, r] = h
                ys[j] = ys[j] + _dot(m, xdt.astype(BF16)) + _dot_nt(cbf, h.astype(BF16)) * jnp.exp(ac)
                dte = jnp.exp(aend - ac)
                h_scr[r] = jnp.exp(aend) * h + _dot_tn((xdt * dte).astype(BF16), bb)
            for j in range(gw // LANES):
                y_ref[rs, j * LANES:(j + 1) * LANES] = ys[j]

    colspec = pl.BlockSpec((None, hpg, rows, LANES), lambda g, c: (g, 0, c, 0))
    return side_call(
        kern, side,
        name=name,
        grid=(ng, nc // nsub),
        in_specs=[pl.BlockSpec((rows, gw), lambda g, c: (c, g)),
                  pl.BlockSpec((rows, LANES), lambda g, c: (c, g)),
                  pl.BlockSpec((rows, LANES), lambda g, c: (c, g)),
                  colspec, colspec,
                  pl.BlockSpec((None, nsub, 8, LANES), lambda g, c: (g, c, 0, 0))],
        out_specs=[pl.BlockSpec((rows, gw), lambda g, c: (c, g)),
                   pl.BlockSpec((None, nsub, hpg, LANES, LANES), lambda g, c: (g, c, 0, 0, 0))],
        out_shape=[jax.ShapeDtypeStruct((s, d_inner), F32),
                   jax.ShapeDtypeStruct((ng, nc, hpg, LANES, LANES), F32)],
        scratch_shapes=[pltpu.VMEM((hpg, LANES, LANES), F32)],
        args=(xs, bm, cm, col_a, col_dt, rowf))


def ssd_chunk_bwd(xs, bm, cm, col_a, col_dt, rowf, hprev, dy, name, side=None):
    s, d_inner = xs.shape
    ln = CHUNK
    nc = s // ln
    nsub = _pick(nc, (SSD_SUB, 2, 1))
    rows = nsub * ln
    ng, hpg = col_a.shape[0], col_a.shape[1]
    gw = d_inner // ng

    def kern(x_ref, b_ref, c_ref, ca_ref, cd_ref, rf_ref, hp_ref, dy_ref,
             dx_ref, db_ref, dc_ref, ddt_ref, da_ref, dh_scr):
        @pl.when(pl.program_id(1) == 0)
        def _():
            dh_scr[...] = jnp.zeros_like(dh_scr)

        row, col = _iota2((ln, ln), 0), _iota2((ln, ln), 1)
        causal = row >= col
        tri_ge = (col >= row).astype(BF16)
        ones = jnp.ones((ln, LANES), BF16)
        lane = _iota2((1, LANES), 1)
        last_row = (_iota2((ln, 1), 0) == ln - 1).astype(F32)
        for sc in reversed(range(nsub)):
            rs = slice(sc * ln, (sc + 1) * ln)
            bb = b_ref[rs, :].astype(BF16)
            cbf = c_ref[rs, :].astype(BF16)
            cb = _dot_nt(cbf, bb)
            dcb = jnp.zeros((ln, ln), F32)
            d_b = jnp.zeros((ln, LANES), F32)
            d_c = jnp.zeros((ln, LANES), F32)
            dxs = [jnp.zeros((ln, LANES), F32) for _ in range(gw // LANES)]
            for r in range(hpg):
                j, hf = divmod(r, LANES // HEAD)
                mh = ((lane >= HEAD * hf) & (lane < HEAD * (hf + 1))).astype(F32)
                ac = ca_ref[r, rs, :]
                dt = cd_ref[r, rs, :]
                ar = rf_ref[sc, pl.ds(r, 1), :]
                aend = rf_ref[sc, pl.ds(4 + r, 1), :]
                dm = jnp.where(causal, jnp.exp(jnp.minimum(ac - ar, 0.0)), 0.0)
                m = cb * dm
                mb = m.astype(BF16)
                xp = x_ref[rs, j * LANES:(j + 1) * LANES]
                xdt = xp * dt * mh
                xdtb = xdt.astype(BF16)
                dyp = dy_ref[rs, j * LANES:(j + 1) * LANES] * mh
                dypb = dyp.astype(BF16)
                h = hp_ref[sc, r]
                hb = h.astype(BF16)
                dh = dh_scr[r]
                dhb = dh.astype(BF16)
                e_in = jnp.exp(ac)
                dte = jnp.exp(aend - ac)
                eend = jnp.exp(aend)
                d_m = _dot_nt(dypb, xdtb)
                dcb = dcb + d_m * dm
                gm = d_m * m
                yoff_pre = _dot_nt(cbf, hb)
                bdh = _dot_nt(bb, dhb)
                dxdt = _dot_tn(mb, dypb) + bdh * dte
                t1 = _rowsum(xdt * bdh) * dte
                gh, gl = _split2(gm)
                dacum = (_rowsum(gm) - (_dot_tn(gh, ones) + _dot_tn(gl, ones))
                         + _rowsum(dyp * yoff_pre) * e_in - t1)
                end_term = _colsum(t1) + eend * jnp.sum(_colsum(dh * h), axis=1, keepdims=True)
                dacum = dacum + last_row * end_term
                da_ref[r, rs, :] = _dot_x3_left(tri_ge, dacum)
                ddt_ref[r, rs, :] = jnp.broadcast_to(_rowsum(dxdt * xp), (ln, LANES))
                dxs[j] = dxs[j] + dxdt * dt
                d_b = d_b + _dot((xdt * dte).astype(BF16), dhb)
                dye = (dyp * e_in).astype(BF16)
                d_c = d_c + _dot(dye, hb)
                dh_scr[r] = eend * dh + _dot_tn(dye, cbf)
            dcbb = dcb.astype(BF16)
            dc_ref[rs, :] = d_c + _dot(dcbb, bb)
            db_ref[rs, :] = d_b + _dot_tn(dcbb, cbf)
            for j in range(gw // LANES):
                dx_ref[rs, j * LANES:(j + 1) * LANES] = dxs[j]

    rev = nc // nsub - 1
    colspec = pl.BlockSpec((None, hpg, rows, LANES), lambda g, c: (g, 0, rev - c, 0))
    return side_call(
        kern, side,
        name=name,
        grid=(ng, nc // nsub),
        in_specs=[pl.BlockSpec((rows, gw), lambda g, c: (rev - c, g)),
                  pl.BlockSpec((rows, LANES), lambda g, c: (rev - c, g)),
                  pl.BlockSpec((rows, LANES), lambda g, c: (rev - c, g)),
                  colspec, colspec,
                  pl.BlockSpec((None, nsub, 8, LANES), lambda g, c: (g, rev - c, 0, 0)),
                  pl.BlockSpec((None, nsub, hpg, LANES, LANES), lambda g, c: (g, rev - c, 0, 0, 0)),
                  pl.BlockSpec((rows, gw), lambda g, c: (rev - c, g))],
        out_specs=[pl.BlockSpec((rows, gw), lambda g, c: (rev - c, g)),
                   pl.BlockSpec((rows, LANES), lambda g, c: (rev - c, g)),
                   pl.BlockSpec((rows, LANES), lambda g, c: (rev - c, g)),
                   colspec, colspec],
        out_shape=[jax.ShapeDtypeStruct((s, d_inner), F32),
                   jax.ShapeDtypeStruct(bm.shape, F32), jax.ShapeDtypeStruct(cm.shape, F32),
                   jax.ShapeDtypeStruct(col_a.shape, F32), jax.ShapeDtypeStruct(col_a.shape, F32)],
        scratch_shapes=[pltpu.VMEM((hpg, LANES, LANES), F32)],
        args=(xs, bm, cm, col_a, col_dt, rowf, hprev, dy))


def gnorm_fwd(y, xs, z, dexp, gain, ngroups, name):
    c = y.shape[1]
    gw = c // ngroups

    def fn(yv, xv, zv, dv, gv):
        yg = (yv + xv * dv) * (zv * _sigmoid(zv))
        outs = []
        for k in range(ngroups):
            t = yg[:, k * gw:(k + 1) * gw]
            outs.append(t * lax.rsqrt(jnp.mean(t * t, axis=1, keepdims=True) + EPS))
        return (jnp.concatenate(outs, axis=1) * gv,)

    return rowwise(fn, [(y, "row"), (xs, "row"), (z, "row"), (dexp, "full"), (gain, "full")], [(c, BF16)], tr=256, name=name)[0]


def gnorm_bwd(dn, y, xs, z, dexp, gain, ngroups, name):
    c = y.shape[1]
    gw = c // ngroups

    def fn(dnv, yv, xv, zv, dv, gv):
        yd = yv + xv * dv
        sg = _sigmoid(zv)
        sz = zv * sg
        yg = yd * sz
        dng = dnv * gv
        dyg, yh = [], []
        for k in range(ngroups):
            sl = slice(k * gw, (k + 1) * gw)
            t = yg[:, sl]
            r = lax.rsqrt(jnp.mean(t * t, axis=1, keepdims=True) + EPS)
            th = t * r
            dyg.append(r * (dng[:, sl] - th * jnp.mean(dng[:, sl] * th, axis=1, keepdims=True)))
            yh.append(th)
        dyg = jnp.concatenate(dyg, axis=1)
        yh = jnp.concatenate(yh, axis=1)
        dyd = dyg * sz
        dz = dyg * yd * (sg * (1.0 + zv * (1.0 - sg)))
        return dyd, dyd * dv, dz, _colsum(dyd * xv), _colsum(dnv * yh)

    return rowwise(fn, [(dn, "row"), (y, "row"), (xs, "row"), (z, "row"), (dexp, "full"), (gain, "full")],
                   [(c, F32), (c, F32), (c, BF16)], [(1, c), (1, c)], tr=256, name=name)


def ssd_post(ddt, da, dt, dtr, bias, alog, name):
    def fn(ddtv, dav, dtv, dtrv, bv, al):
        a_neg = -jnp.exp(al)
        ddtr = (ddtv + dav * a_neg) * _sigmoid(dtrv + bv)
        return ddtr, _colsum(ddtr), _colsum(dav * dtv) * a_neg

    return rowwise(fn, [(ddt, "row"), (da, "row"), (dt, "row"), (dtr, "row"), (bias, "full"), (alog, "full")],
                   [(LANES, BF16)], [(1, LANES), (1, LANES)], tr=512, name=name)


def _from_colform(v, s):
    ng, hpg = v.shape[0], v.shape[1]
    flat = v[..., 0].reshape(ng * hpg, s).T
    return jnp.pad(flat, ((0, 0), (0, LANES - ng * hpg)))


def ssm_fwd(x, g, p, tag, plan):
    ng, hpg, d_inner = p["ng"], p["hpg"], p["d_inner"]
    h = rms_fwd(x, g, f"ssm_rms_{tag}")
    z = mm(h, p["w_z"], name=f"ssm_inz_{tag}")
    xbc = mm(h, p["w_xbc"], name=f"ssm_inx_{tag}")
    dtr = mm(h, p["w_dt"], name=f"ssm_indt_{tag}")
    xs, bm, cm = conv_fwd(xbc, p["conv_w"], p["conv_b"], d_inner, f"ssm_conv_{tag}")
    dt, acum = ssd_pre(dtr, p["dt_bias"], p["a_log"], f"ssm_pre_{tag}")
    col_a, col_dt = _ssd_layouts(acum, ng, hpg), _ssd_layouts(dt, ng, hpg)
    rowf = _ssd_rowform(acum, ng, hpg)
    y, hprev = _hooked(plan, f"ssm_scan_{tag}", ssd_chunk_fwd, xs, bm, cm, col_a, col_dt, rowf)
    n = gnorm_fwd(y, xs, z, p["d_exp"], p["norm_gain"], ng, f"ssm_gnorm_{tag}")
    xn = mm(n, p["w_out"], add=x, name=f"ssm_out_{tag}")
    return xn, (x, h, z, xbc, dtr, xs, bm, cm, dt, col_a, col_dt, rowf, y, hprev, n)


def ssm_bwd(dxn, saved, g, p, tag, plan):
    x, h, z, xbc, dtr, xs, bm, cm, dt, col_a, col_dt, rowf, y, hprev, n = saved
    ng, hpg, d_inner = p["ng"], p["hpg"], p["d_inner"]
    s = x.shape[0]
    dxn, dxb = dxn
    dn = mm(dxb, p["w_out"], tb=True, name=f"ssm_dn_{tag}")
    dwout = mm(n, dxb, ta=True, out_dtype=BF16, name=f"ssm_dwout_{tag}")
    dy, dxs_skip, dz, dd_lane, dgain = gnorm_bwd(dn, y, xs, z, p["d_exp"], p["norm_gain"], ng, f"ssm_dgnorm_{tag}")
    dxs, dbm, dcm, ddt_c, da_c = _hooked(plan, f"ssm_dscan_{tag}", ssd_chunk_bwd, xs, bm, cm, col_a, col_dt, rowf, hprev, dy)
    ddtr, dbias, dalog = ssd_post(_from_colform(ddt_c, s), _from_colform(da_c, s), dt, dtr,
                                  p["dt_bias"], p["a_log"], f"ssm_post_{tag}")
    res = conv_bwd_pre(xbc, p["conv_w"], p["conv_b"], dxs, dxs_skip, dbm, dcm, f"ssm_dconv_{tag}")
    dpre, dconv_w, dconv_b = res[0], jnp.concatenate(res[1:5], axis=0), res[5]
    dxbc = conv_bwd_in(dpre, p["conv_w"], f"ssm_dconvin_{tag}")
    dh = mm(dz, p["w_z"], tb=True, name=f"ssm_dhz_{tag}")
    dh = mm(dxbc, p["w_xbc"], tb=True, add=dh, name=f"ssm_dhx_{tag}")
    dh = mm(ddtr, p["w_dt"], tb=True, add=dh, name=f"ssm_dhdt_{tag}")
    dwz = mm(h, dz, ta=True, out_dtype=BF16, name=f"ssm_dwz_{tag}")
    dwxbc = mm(h, dxbc, ta=True, out_dtype=BF16, name=f"ssm_dwxbc_{tag}")
    dwdt = mm(h, ddtr, ta=True, out_dtype=BF16, name=f"ssm_dwdt_{tag}")
    dx, dg = rms_bwd(x, g, dh, dxn, f"ssm_drms_{tag}")
    nh = ng * hpg
    dwin = jnp.concatenate([dwz, dwxbc, dwdt[:, :nh]], axis=1)
    dd = dd_lane.reshape(nh, HEAD).sum(-1)
    return dx, dg, dict(w_in=dwin, conv_w=dconv_w, conv_b=dconv_b, dt_bias=dbias[0, :nh], a_log=dalog[0, :nh],
                        d=dd, norm_gain=dgain, w_out=dwout)


def local_step(x, target, w, plan):
    d = x.shape[1]
    depth = w["mix_norm"].shape[0]
    bd = _head_blockdiag(LANES)
    tril = jnp.tril(jnp.ones((CHUNK, CHUNK), bool))
    ssm_heads = w["ssm_dt_bias"].shape[1]
    d_inner = w["ssm_norm_gain"].shape[1]
    ng = w["ssm_norm_gain"].shape[1] // 256
    nstate = CHUNK

    def pad_lanes(v):
        return jnp.pad(v, ((0, 0), (0, LANES - v.shape[1])))

    def ssm_params(j):
        w_in = w["ssm_w_in"][j]
        cw = w["ssm_conv_w"][j]
        return dict(ng=ng, hpg=ssm_heads // ng, d_inner=d_inner,
                    w_z=w_in[:, :d_inner], w_xbc=w_in[:, d_inner:d_inner + d_inner + 2 * ng * nstate],
                    w_dt=pad_lanes(w_in[:, 2 * d_inner + 2 * ng * nstate:]),
                    conv_w=[cw[k:k + 1] for k in range(cw.shape[0])], conv_b=w["ssm_conv_b"][j:j + 1],
                    dt_bias=pad_lanes(w["ssm_dt_bias"][j:j + 1]), a_log=pad_lanes(w["ssm_a_log"][j:j + 1]),
                    d_exp=jnp.repeat(w["ssm_d"][j], HEAD)[None, :], norm_gain=w["ssm_norm_gain"][j:j + 1],
                    w_out=w["ssm_w_out"][j])

    def gm_params(j):
        wc = jnp.where(tril, w["gm_w_s"][j], 0.0).astype(BF16)
        bst = jnp.repeat(w["gm_b_s"][j].T, LANES, axis=1)
        return wc, bst

    def sb_gains(j):
        nh = d // HEAD
        return jnp.tile(w["sb_q_gain"][j], nh)[None, :], jnp.tile(w["sb_k_gain"][j], nh)[None, :]

    saved = []
    cur = x
    for i in range(depth):
        kind, j = i % 3, i // 3
        gmix = w["mix_norm"][i:i + 1]
        if kind == 0:
            qg, kg = sb_gains(j)
            cur, sv = sb_fwd(cur, gmix, w["sb_w_qkv"][j], qg, kg, w["sb_w_o"][j], bd, f"{i}", plan)
        elif kind == 1:
            wc, bst = gm_params(j)
            cur, sv = gm_fwd(cur, gmix, w["gm_w_in"][j], w["gm_b_in"][j:j + 1], w["gm_v_gain"][j:j + 1], wc, bst,
                             w["gm_w_out"][j], f"{i}")
        else:
            cur, sv = ssm_fwd(cur, gmix, ssm_params(j), f"{i}", plan)
        cur, sv2 = ffn_fwd(cur, w["ffn_norm"][i:i + 1], w["ffn_w_gu"][i], w["ffn_w_down"][i], f"{i}")
        saved.append((sv, sv2))

    loss, dcur = loss_and_grad(cur, target, "loss")

    grads = {k: [None] * len(v) for k, v in w.items()}
    for i in reversed(range(depth)):
        kind, j = i % 3, i // 3
        sv, sv2 = saved[i]
        gmix = w["mix_norm"][i:i + 1]
        dcur, dgf, dwgu, dwdown = ffn_bwd(dcur, sv2, w["ffn_norm"][i:i + 1], w["ffn_w_gu"][i], w["ffn_w_down"][i], f"{i}")
        grads["ffn_norm"][i], grads["ffn_w_gu"][i], grads["ffn_w_down"][i] = dgf[0], dwgu, dwdown
        if kind == 0:
            qg, kg = sb_gains(j)
            dcur, dg, dwqkv, dqg, dkg, dwo = sb_bwd(dcur, sv, gmix, w["sb_w_qkv"][j], qg, kg, w["sb_w_o"][j], bd, f"{i}", plan)
            grads["sb_w_qkv"][j], grads["sb_q_gain"][j], grads["sb_k_gain"][j], grads["sb_w_o"][j] = dwqkv, dqg, dkg, dwo
        elif kind == 1:
            wc, bst = gm_params(j)
            dcur, dg, dwin, dbin, dvg, dws, dbs, dwout = gm_bwd(dcur, sv, gmix, w["gm_w_in"][j], w["gm_v_gain"][j:j + 1],
                                                                 wc, bst, w["gm_w_out"][j], f"{i}")
            grads["gm_w_in"][j], grads["gm_b_in"][j], grads["gm_v_gain"][j] = dwin, dbin[0], dvg[0]
            grads["gm_w_s"][j], grads["gm_b_s"][j], grads["gm_w_out"][j] = dws, dbs, dwout
        else:
            dcur, dg, gs = ssm_bwd(dcur, sv, gmix, ssm_params(j), f"{i}", plan)
            grads["ssm_w_in"][j], grads["ssm_conv_w"][j], grads["ssm_conv_b"][j] = gs["w_in"], gs["conv_w"], gs["conv_b"][0]
            grads["ssm_dt_bias"][j], grads["ssm_a_log"][j], grads["ssm_d"][j] = gs["dt_bias"], gs["a_log"], gs["d"]
            grads["ssm_norm_gain"][j], grads["ssm_w_out"][j] = gs["norm_gain"][0], gs["w_out"]
        grads["mix_norm"][i] = dg[0]
        mixer = {0: ("sb_w_qkv", "sb_w_o"), 1: ("gm_w_in", "gm_w_out"), 2: ("ssm_w_in", "ssm_w_out")}[kind]
        plan.layer_done(i, {(n, l): grads[n][l] for n, l in [(mixer[0], j), (mixer[1], j), ("ffn_w_gu", i), ("ffn_w_down", i)]})
    grads = {k: (v if k in MATRICES else jnp.stack(v)) for k, v in grads.items()}
    return loss, dcur[0], grads


WEIGHTS = ["mix_norm", "ffn_norm", "sb_w_qkv", "sb_q_gain", "sb_k_gain", "sb_w_o", "gm_w_in", "gm_b_in", "gm_v_gain",
           "gm_w_s", "gm_b_s", "gm_w_out", "ssm_w_in", "ssm_conv_w", "ssm_conv_b", "ssm_dt_bias", "ssm_a_log", "ssm_d",
           "ssm_norm_gain", "ssm_w_out", "ffn_w_gu", "ffn_w_down"]
SHARDED = {"sb_w_qkv": 2, "sb_w_o": 1, "gm_w_in": 2, "gm_w_out": 1, "ssm_w_in": 2, "ssm_conv_w": 2, "ssm_conv_b": 1,
           "ssm_norm_gain": 1, "ssm_w_out": 1, "ffn_w_gu": 2, "ffn_w_down": 1}
EXACT = ("ssm_conv_w", "ssm_conv_b", "ssm_norm_gain")
MATRICES = tuple(n for n in SHARDED if n not in EXACT)
COLUMN_BLOCKS = ("sb_w_qkv", "gm_w_in", "ffn_w_gu")
REPLICATED = [n for n in WEIGHTS if n not in SHARDED]
N_CHIPS = 4
N_DEV = 8
PACK_COLS = 1024


def _pack(pieces, dtype, align):
    flat = jnp.concatenate([p.reshape(-1).astype(dtype) for p in pieces])
    rows = -(-flat.shape[0] // (PACK_COLS * align)) * align
    flat = jnp.pad(flat, (0, rows * PACK_COLS - flat.shape[0]))
    return flat.reshape(rows, PACK_COLS)


def _unpack(flat, shapes):
    out, off = [], 0
    for shp in shapes:
        n = math.prod(shp)
        out.append(flat[off:off + n].reshape(shp))
        off += n
    return out


ANY = pl.BlockSpec(memory_space=pl.ANY)


def _pos():
    return lax.axis_index("x"), lax.axis_index("y"), lax.axis_index("c")


def _remote(src, dst, send, recv, k, to):
    return pltpu.make_async_remote_copy(src_ref=src, dst_ref=dst, send_sem=send.at[k], recv_sem=recv.at[k],
                                        device_id=to, device_id_type=MESH_ID)


def _comm_call(body, name, ins, out_shapes, nsem, aliases=None):
    return pl.pallas_call(
        body, name=name, out_shape=out_shapes,
        in_specs=[ANY] * len(ins), out_specs=[ANY] * len(out_shapes),
        scratch_shapes=[pltpu.SemaphoreType.DMA((nsem,)), pltpu.SemaphoreType.DMA((nsem,))],
        input_output_aliases=aliases or {},
    )(*ins)


def stage_shard(w, chip, name):
    rows, cols = w.shape
    tr = _pick(rows, (256, 352, 128))

    def kern(idx_ref, w_ref, o_ref):
        o_ref[...] = w_ref[...].astype(BF16)

    grid_spec = pltpu.PrefetchScalarGridSpec(
        num_scalar_prefetch=1, grid=(rows // tr,),
        in_specs=[pl.BlockSpec((tr, cols), lambda i, idx: (i, 0))],
        out_specs=pl.BlockSpec((None, tr, cols), lambda i, idx: (idx[0], i, 0)))
    return pl.pallas_call(
        kern, name=name, grid_spec=grid_spec,
        out_shape=jax.ShapeDtypeStruct((N_CHIPS, rows, cols), BF16),
        compiler_params=_params(("parallel",)),
    )(jnp.reshape(chip, (1,)).astype(jnp.int32), w)


class Side:
    def __init__(self, arrays, out_shapes, aliases, nsem, start, finish):
        self.arrays, self.out_shapes, self.aliases, self.nsem = list(arrays), list(out_shapes), aliases, nsem
        self.start, self.finish = start, finish


def run_side(side, name):
    n_in, n_out = len(side.arrays), len(side.out_shapes)

    def body(*refs):
        ins, outs = refs[:n_in], refs[n_in:n_in + n_out]
        send, recv = refs[n_in + n_out:]
        side.start(ins, outs, send, recv)
        side.finish(ins, outs, send, recv)

    return _comm_call(body, name, side.arrays, side.out_shapes, side.nsem, aliases=side.aliases)


def side_call(kern, side, *, name, grid, in_specs, out_specs, out_shape, scratch_shapes, args):
    if side is None:
        res = pl.pallas_call(kern, name=name, grid=grid, in_specs=in_specs, out_specs=out_specs, out_shape=out_shape,
                             scratch_shapes=scratch_shapes,
                             compiler_params=_params(("parallel",) + ("arbitrary",) * (len(grid) - 1)))(*args)
        return list(res), []
    n_in, n_out, n_scr = len(in_specs), len(out_specs), len(scratch_shapes)
    s_in, s_out = len(side.arrays), len(side.out_shapes)

    def body(*refs):
        ins, refs = refs[:n_in], refs[n_in:]
        side_ins, refs = refs[:s_in], refs[s_in:]
        outs, refs = refs[:n_out], refs[n_out:]
        side_outs, refs = refs[:s_out], refs[s_out:]
        scr, (send, recv) = refs[:n_scr], refs[n_scr:]
        first, last = None, None
        for axis, size in enumerate(grid):
            at0, at1 = pl.program_id(axis) == 0, pl.program_id(axis) == size - 1
            first = at0 if first is None else first & at0
            last = at1 if last is None else last & at1

        @pl.when(first)
        def _():
            side.start(side_ins, side_outs, send, recv)

        kern(*ins, *outs, *scr)

        @pl.when(last)
        def _():
            side.finish(side_ins, side_outs, send, recv)

    res = pl.pallas_call(
        body, name=name, grid=grid,
        in_specs=list(in_specs) + [ANY] * s_in, out_specs=list(out_specs) + [ANY] * s_out,
        out_shape=list(out_shape) + side.out_shapes,
        scratch_shapes=list(scratch_shapes) + [pltpu.SemaphoreType.DMA((side.nsem,)), pltpu.SemaphoreType.DMA((side.nsem,))],
        input_output_aliases={n_in + a: n_out + b for a, b in side.aliases.items()},
        compiler_params=_params(("arbitrary",) * len(grid)),
    )(*args, *side.arrays)
    return list(res[:n_out]), list(res[n_out:])


def gather_side(staged):
    n = len(staged)

    def plan(o_refs, send, recv):
        x, y, c = _pos()
        chips = [(1 - x, y), (x, 1 - y), (1 - x, 1 - y)]

        def part(u, chip, cc):
            half = staged[u].shape[1] // 2
            return o_refs[u].at[2 * chip[0] + chip[1], pl.ds(cc * half, half), :]

        first = [_remote(part(u, (x, y), c), part(u, (x, y), c), send, recv, 6 * u + j, (*chip, c))
                 for u in range(n) for j, chip in enumerate(chips)]
        landed = [_remote(part(u, chip, c), part(u, chip, c), send, recv, 6 * u + j, (x, y, c))
                  for u in range(n) for j, chip in enumerate(chips)]
        passed = [_remote(part(u, chip, c), part(u, chip, c), send, recv, 6 * u + 3 + j, (x, y, 1 - c))
                  for u in range(n) for j, chip in enumerate(chips)]
        handed = [_remote(part(u, chip, 1 - c), part(u, chip, 1 - c), send, recv, 6 * u + 3 + j, (x, y, c))
                  for u in range(n) for j, chip in enumerate(chips)]
        return first, landed, passed, handed

    def start(ins, outs, send, recv):
        for cp in plan(outs, send, recv)[0]:
            cp.start()

    def finish(ins, outs, send, recv):
        first, landed, passed, handed = plan(outs, send, recv)
        for got, fw in zip(landed, passed):
            got.wait_recv()
            fw.start()
        for got in handed:
            got.wait_recv()
        for cp in first + passed:
            cp.wait_send()

    outs = [jax.ShapeDtypeStruct(s.shape, s.dtype) for s in staged]
    return Side(staged, outs, {u: u for u in range(n)}, 6 * n, start, finish)


def swap_halves(gps, name):
    n = len(gps)

    def body(*refs):
        g_refs, r_refs = refs[:n], refs[n:2 * n]
        send, recv = refs[2 * n:]
        x, y, c = _pos()
        cps = []
        for u in range(n):
            half = gps[u].shape[1] // 2
            cps.append(_remote(g_refs[u].at[:, pl.ds((1 - c) * half, half), :], r_refs[u], send, recv, u, (x, y, 1 - c)))
        for cp in cps:
            cp.start()
        for cp in cps:
            cp.wait()

    outs = [jax.ShapeDtypeStruct((g.shape[0], g.shape[1] // 2, g.shape[2]), g.dtype) for g in gps]
    return _comm_call(body, name, gps, outs, n)


def scatter_side(parts):
    n = len(parts)

    def plan(p_refs, r_refs, send, recv):
        x, y, c = _pos()
        chips = [(1 - x, y), (x, 1 - y), (1 - x, 1 - y)]
        return [_remote(p_refs[u].at[2 * chip[0] + chip[1]], r_refs[u].at[j], send, recv, 3 * u + j, (*chip, c))
                for u in range(n) for j, chip in enumerate(chips)]

    def start(ins, outs, send, recv):
        for cp in plan(ins, outs, send, recv):
            cp.start()

    def finish(ins, outs, send, recv):
        for cp in plan(ins, outs, send, recv):
            cp.wait()

    outs = [jax.ShapeDtypeStruct((N_CHIPS - 1,) + p.shape[1:], p.dtype) for p in parts]
    return Side(parts, outs, {}, 3 * n, start, finish)


def join_halves(bufs):
    n = len(bufs)

    def body(*refs):
        o_refs = refs[n:2 * n]
        send, recv = refs[2 * n:]
        x, y, c = _pos()

        def rows(u, cc):
            half = bufs[u].shape[0] // 2
            return o_refs[u].at[pl.ds(cc * half, half), :]

        cps = [_remote(rows(u, c), rows(u, c), send, recv, u, (x, y, 1 - c)) for u in range(n)]
        for cp in cps:
            cp.start()
        for u in range(n):
            _remote(rows(u, 1 - c), rows(u, 1 - c), send, recv, u, (x, y, c)).wait_recv()
        for cp in cps:
            cp.wait_send()

    outs = [jax.ShapeDtypeStruct(b.shape, b.dtype) for b in bufs]
    return _comm_call(body, "join_halves", bufs, outs, n, aliases={u: u for u in range(n)})


def gather_small(sg, name):
    rows, cols = sg.shape

    def body(s_ref, o_ref, send, recv, lsem):
        x, y, c = _pos()
        me, sibling = (x, y, c), (x, y, 1 - c)
        chips = [(1 - x, y), (x, 1 - y), (1 - x, 1 - y)]

        def blk(px, py, pc):
            return o_ref.at[4 * px + 2 * py + pc]

        mine = pltpu.make_async_copy(s_ref, blk(*me), lsem)
        mine.start()
        first = [_remote(s_ref, blk(*me), send, recv, 0, sibling)]
        first += [_remote(s_ref, blk(*me), send, recv, 1 + j, (*chip, c)) for j, chip in enumerate(chips)]
        for cp in first:
            cp.start()
        passed = [_remote(blk(*chip, c), blk(*chip, c), send, recv, 4 + j, sibling) for j, chip in enumerate(chips)]
        for j, chip in enumerate(chips):
            _remote(blk(*chip, c), blk(*chip, c), send, recv, 1 + j, me).wait_recv()
            passed[j].start()
        _remote(blk(*sibling), blk(*sibling), send, recv, 0, me).wait_recv()
        for j, chip in enumerate(chips):
            _remote(blk(*chip, 1 - c), blk(*chip, 1 - c), send, recv, 4 + j, me).wait_recv()
        for cp in first + passed:
            cp.wait_send()
        mine.wait()

    return pl.pallas_call(
        body, name=name,
        out_shape=jax.ShapeDtypeStruct((N_DEV, rows, cols), sg.dtype),
        in_specs=[ANY], out_specs=ANY,
        scratch_shapes=[pltpu.SemaphoreType.DMA((N_DEV - 1,)), pltpu.SemaphoreType.DMA((N_DEV - 1,)), pltpu.SemaphoreType.DMA],
    )(sg)


def sum_cores(gp, theirs, core, chip, name):
    nch, rows, cols = gp.shape
    half = rows // 2
    tr = _pick(half, (256, 176, 128, 64))
    nb = half // tr

    def kern(idx_ref, g_ref, t_ref, own_ref, all_ref):
        k = pl.program_id(1)
        s = g_ref[...].astype(F32) + t_ref[...].astype(F32)
        all_ref[...] = s.astype(BF16)

        @pl.when(k == idx_ref[1])
        def _():
            own_ref[...] = s

    grid_spec = pltpu.PrefetchScalarGridSpec(
        num_scalar_prefetch=1, grid=(nb, nch),
        in_specs=[pl.BlockSpec((None, tr, cols), lambda i, k, idx: (k, idx[0] * nb + i, 0)),
                  pl.BlockSpec((None, tr, cols), lambda i, k, idx: (k, i, 0))],
        out_specs=[pl.BlockSpec((tr, cols), lambda i, k, idx: (i, 0)),
                   pl.BlockSpec((None, tr, cols), lambda i, k, idx: (k, i, 0))])
    return pl.pallas_call(
        kern, name=name, grid_spec=grid_spec,
        out_shape=[jax.ShapeDtypeStruct((half, cols), F32), jax.ShapeDtypeStruct((nch, half, cols), BF16)],
        compiler_params=_params(("parallel", "arbitrary")),
    )(jnp.stack([core, chip]).astype(jnp.int32), gp, theirs)


def sum_chips(own, others, core, name):
    half, cols = own.shape
    tr = _pick(half, (256, 176, 128, 64))
    nb = half // tr

    def kern(idx_ref, o_ref, a_ref, b_ref, c_ref, out_ref):
        out_ref[...] = ((o_ref[...] + a_ref[...].astype(F32)) + b_ref[...].astype(F32)) + c_ref[...].astype(F32)

    grid_spec = pltpu.PrefetchScalarGridSpec(
        num_scalar_prefetch=1, grid=(nb,),
        in_specs=[pl.BlockSpec((tr, cols), lambda i, idx: (i, 0))] +
                 [pl.BlockSpec((None, tr, cols), lambda i, idx, j=j: (j, i, 0)) for j in range(N_CHIPS - 1)],
        out_specs=pl.BlockSpec((tr, cols), lambda i, idx: (idx[0] * nb + i, 0)))
    return pl.pallas_call(
        kern, name=name, grid_spec=grid_spec,
        out_shape=jax.ShapeDtypeStruct((2 * half, cols), F32),
        compiler_params=_params(("parallel",)),
    )(jnp.reshape(core, (1,)).astype(jnp.int32), own, others, others, others)


def small_update(gath, w, m, v, name):
    def fn(*vs):
        g = vs[0]
        for t in vs[1:N_DEV]:
            g = g + t
        wv, mv, vv = vs[N_DEV:]
        m2 = ADAM_B1 * mv + (1.0 - ADAM_B1) * g
        v2 = ADAM_B2 * vv + (1.0 - ADAM_B2) * (g * g)
        m_hat = m2 / (1.0 - ADAM_B1 ** ADAM_STEP)
        v_hat = v2 / (1.0 - ADAM_B2 ** ADAM_STEP)
        return g, -ADAM_LR * (m_hat / (jnp.sqrt(v_hat) + ADAM_EPS) + ADAM_WD * wv), m2, v2

    c = w.shape[1]
    ins = [(gath[k], "row") for k in range(N_DEV)] + [(w, "row"), (m, "row"), (v, "row")]
    return rowwise(fn, ins, [(c, F32)] * 4, tr=w.shape[0] // 2, name=name)


LAYER_UNITS = {
    0: [("sb_w_qkv", 0), ("sb_w_o", 0), ("ffn_w_gu", 0), ("ffn_w_down", 0)],
    1: [("gm_w_in", 0), ("gm_w_out", 0), ("ffn_w_gu", 1), ("ffn_w_down", 1)],
    2: [("ssm_w_in", 0), ("ssm_w_out", 0), ("ffn_w_gu", 2), ("ffn_w_down", 2)],
    3: [("sb_w_qkv", 1), ("sb_w_o", 1), ("ffn_w_gu", 3), ("ffn_w_down", 3)],
}
GATHER_AT = {"sb_attn_0": (1, 2), "ssm_scan_2": (3,)}
SCATTER_AT = {"ssm_dscan_2": (3,), "sb_dattn_0": (2, 1)}


class _Plan:
    def __init__(self, ins, core, chip):
        self.core, self.chip = core, chip
        self.staged = {(n, l): stage_shard(ins[n][l], chip, f"stage_{n}_{l}")
                       for i in LAYER_UNITS for n, l in LAYER_UNITS[i]}
        self.full = {n: [None] * ins[n].shape[0] for n in MATRICES}
        self.parts = {}
        self.halves = {}
        self._fill(LAYER_UNITS[0], run_side(gather_side([self.staged[u] for u in LAYER_UNITS[0]]), "gather_0"))

    @staticmethod
    def _units(layers):
        return [u for i in layers for u in LAYER_UNITS[i]]

    def _fill(self, units, gathered):
        for (n, l), g in zip(units, gathered):
            if n in COLUMN_BLOCKS:
                self.full[n][l] = g
            elif n == "ssm_w_in":
                self.full[n][l] = jnp.concatenate([g[k] for k in range(N_CHIPS)], axis=1)
            else:
                self.full[n][l] = g.reshape(-1, g.shape[-1])

    def _reduce(self, layers, others):
        owns = [own for i in layers for own in self.parts[i][0]]
        for (n, l), own, other in zip(self._units(layers), owns, others):
            self.halves[(n, l)] = sum_chips(own, other, self.core, f"sum_chips_{n}_{l}")

    def side(self, tag):
        if tag in GATHER_AT:
            return gather_side([self.staged[u] for u in self._units(GATHER_AT[tag])])
        if tag in SCATTER_AT:
            return scatter_side([a for i in SCATTER_AT[tag] for a in self.parts[i][1]])
        return None

    def done(self, tag, results):
        if tag in GATHER_AT:
            self._fill(self._units(GATHER_AT[tag]), results)
        else:
            self._reduce(SCATTER_AT[tag], results)

    def layer_done(self, i, grads):
        gps = []
        for n, l in LAYER_UNITS[i]:
            g = grads[(n, l)]
            if n in COLUMN_BLOCKS:
                gps.append(g)
            elif n == "ssm_w_in":
                gps.append(jnp.stack(jnp.split(g, N_CHIPS, axis=1)))
            else:
                gps.append(g.reshape(N_CHIPS, -1, g.shape[-1]))
        theirs = swap_halves(gps, f"swap_halves_{i}")
        sums = [sum_cores(g, t, self.core, self.chip, f"sum_cores_{n}_{l}")
                for (n, l), g, t in zip(LAYER_UNITS[i], gps, theirs)]
        self.parts[i] = ([s[0] for s in sums], [s[1] for s in sums])
        if not any(i in layers for layers in SCATTER_AT.values()):
            self._reduce((i,), run_side(scatter_side(self.parts[i][1]), f"scatter_{i}"))

    def shard_grads(self):
        units = self._units(sorted(LAYER_UNITS))
        return dict(zip(units, join_halves([self.halves[u] for u in units])))


def _step(ins):
    x, target = ins["x"][0], ins["loss_target"][0]
    core = lax.axis_index("c")
    chip = 2 * lax.axis_index("x") + lax.axis_index("y")

    def lane_pad(v):
        return jnp.pad(v, ((0, 0), (0, PACK_COLS - v.shape[1])))

    vec_rows = [ins["ssm_conv_w"][0], ins["ssm_conv_b"], lane_pad(ins["ssm_norm_gain"])]
    blk = jnp.concatenate(vec_rows + [jnp.zeros((SUBLANES - 6, PACK_COLS), F32)], axis=0)
    per_chip = gather_small(blk, "gather_vectors")[0::2]
    ngw = ins["ssm_norm_gain"].shape[1]
    full = {
        "ssm_conv_w": jnp.concatenate([per_chip[k, 0:4] for k in range(N_CHIPS)], axis=1)[None],
        "ssm_conv_b": jnp.concatenate([per_chip[k, 4:5] for k in range(N_CHIPS)], axis=1),
        "ssm_norm_gain": jnp.concatenate([per_chip[k, 5:6, :ngw] for k in range(N_CHIPS)], axis=1),
    }

    plan = _Plan(ins, core, chip)
    full.update(plan.full)
    for n in REPLICATED:
        full[n] = ins[n]

    loss, dx, grads = local_step(x, target, full, plan)
    loss = lax.psum(loss, ALL_AXES)
    gshards = plan.shard_grads()

    small_shapes = [ins[n].shape for n in REPLICATED]
    vec_shapes = [grads[n].shape for n in EXACT]
    vec_pack = _pack([grads[n] for n in EXACT], F32, SUBLANES)
    gath = gather_small(jnp.concatenate([_pack([grads[n] for n in REPLICATED], F32, SUBLANES), vec_pack], axis=0),
                        "gather_small")
    packed = [jnp.concatenate([_pack([ins[pre + n] for n in REPLICATED], F32, SUBLANES), jnp.zeros_like(vec_pack)], axis=0)
              for pre in ("", "m_", "v_")]
    res = small_update(gath, *packed, name="small_update")
    nrep = res[0].shape[0] - vec_pack.shape[0]
    small = [dict(zip(REPLICATED, _unpack(r[:nrep].reshape(-1), small_shapes))) for r in res]
    vec_g = dict(zip(EXACT, _unpack(res[0][nrep:].reshape(-1), vec_shapes)))

    out_g, out_d, out_m, out_v = {}, {}, {}, {}
    for n in REPLICATED:
        out_g[n], out_d[n], out_m[n], out_v[n] = (s[n] for s in small)
    for n in SHARDED:
        shp = ins[n].shape
        if n in EXACT:
            g = lax.dynamic_slice_in_dim(vec_g[n], chip * shp[-1], shp[-1], axis=vec_g[n].ndim - 1)
        else:
            g = jnp.stack([gshards[(n, l)] for l in range(shp[0])])
        two = (math.prod(shp[:-1]), shp[-1])
        d2, m2, v2 = adamw(ins[n].reshape(two), g.reshape(two), ins["m_" + n].reshape(two),
                           ins["v_" + n].reshape(two), f"adamw_{n}")
        out_g[n], out_d[n], out_m[n], out_v[n] = g, d2.reshape(shp), m2.reshape(shp), v2.reshape(shp)
    return (loss, dx[None], *[out_g[n] for n in WEIGHTS], *[out_d[n] for n in WEIGHTS],
            *[out_m[n] for n in WEIGHTS], *[out_v[n] for n in WEIGHTS])


def kernel(x, mix_norm, ffn_norm, sb_w_qkv, sb_q_gain, sb_k_gain, sb_w_o, gm_w_in, gm_b_in, gm_v_gain, gm_w_s, gm_b_s, gm_w_out, ssm_w_in, ssm_conv_w, ssm_conv_b, ssm_dt_bias, ssm_a_log, ssm_d, ssm_norm_gain, ssm_w_out, ffn_w_gu, ffn_w_down, loss_target, m_mix_norm, m_ffn_norm, m_sb_w_qkv, m_sb_q_gain, m_sb_k_gain, m_sb_w_o, m_gm_w_in, m_gm_b_in, m_gm_v_gain, m_gm_w_s, m_gm_b_s, m_gm_w_out, m_ssm_w_in, m_ssm_conv_w, m_ssm_conv_b, m_ssm_dt_bias, m_ssm_a_log, m_ssm_d, m_ssm_norm_gain, m_ssm_w_out, m_ffn_w_gu, m_ffn_w_down, v_mix_norm, v_ffn_norm, v_sb_w_qkv, v_sb_q_gain, v_sb_k_gain, v_sb_w_o, v_gm_w_in, v_gm_b_in, v_gm_v_gain, v_gm_w_s, v_gm_b_s, v_gm_w_out, v_ssm_w_in, v_ssm_conv_w, v_ssm_conv_b, v_ssm_dt_bias, v_ssm_a_log, v_ssm_d, v_ssm_norm_gain, v_ssm_w_out, v_ffn_w_gu, v_ffn_w_down):
    return _step(dict(locals()))
```

```python
import functools
import math

import jax
import jax.numpy as jnp
from jax import lax
from jax.experimental import pallas as pl
from jax.experimental.pallas import tpu as pltpu

F32 = jnp.float32
BF16 = jnp.bfloat16
EPS = 1e-6
LANES = 128
SUBLANES = 8
VMEM_LIMIT = 56 * 1024 * 1024
HEAD = 64
CHUNK = 128
SB_TQ, SB_TK = 256, 256
SSD_SUB = 4
SB_DEAD = -110.0
SB_UNSEEN = -1e30
ADAM_LR, ADAM_B1, ADAM_B2, ADAM_EPS, ADAM_WD, ADAM_STEP = 0.001, 0.9, 0.999, 1e-08, 0.01, 10
MESH_ID = pl.DeviceIdType.MESH
ALL_AXES = ("x", "y", "c")


def _params(sem):
    return pltpu.CompilerParams(dimension_semantics=sem, vmem_limit_bytes=VMEM_LIMIT)


def _pick(n, cands):
    for c in cands:
        if n % c == 0:
            return c
    return n


def _dot(a, b, dims=((1,), (0,))):
    return lax.dot_general(a, b, (dims, ((), ())), preferred_element_type=F32)


def _dot_nt(a, b):
    return _dot(a, b, ((1,), (1,)))


def _dot_tn(a, b):
    return _dot(a, b, ((0,), (0,)))


def _split2(x):
    hi = x.astype(BF16)
    lo = (x - hi.astype(F32)).astype(BF16)
    return hi, lo


def _dot_x2(x, m):
    hi, lo = _split2(x)
    return _dot(hi, m) + _dot(lo, m)


def _dot_x3_left(m, x):
    h1 = x.astype(BF16)
    r1 = x - h1.astype(F32)
    h2 = r1.astype(BF16)
    h3 = (r1 - h2.astype(F32)).astype(BF16)
    return _dot(m, h1) + _dot(m, h2) + _dot(m, h3)


def _sigmoid(x):
    return 1.0 / (1.0 + jnp.exp(-x))


def _softplus(x):
    return jnp.maximum(x, 0.0) + jnp.log(1.0 + jnp.exp(-jnp.abs(x)))


def _colsum(x):
    return jnp.sum(x, axis=0, keepdims=True)


def _rowsum(x):
    return jnp.sum(x, axis=1, keepdims=True)


def _iota2(shape, dim):
    return lax.broadcasted_iota(jnp.int32, shape, dim)


MM_VMEM_BUDGET = 40 * 1024 * 1024
MM_STEP_US = 0.35
MM_HBM_BYTES_PER_US = 3.0e6
MM_VMEM_BYTES_PER_US = 1.5e6
MM_FLOPS_PER_US = 9.0e8
MXU_DIM = 256


def _mm_tiles(m, n, kk, wn, wk, a_bytes, b_bytes, has_add):
    def divisors(total, cands):
        got = [c for c in cands if total % c == 0 and c <= total]
        return got or [total]

    best = None
    for tm in divisors(m, (1024, 512, 256, 128)):
        for tn in divisors(wn, (1024, 768, 1408, 512, 256, 128)):
            for tk in divisors(wk, (4096, 2816, 2048, 1408, 1024, 768, 512, 256, 128)):
                nk = kk // tk
                vmem = 2 * (tm * tk * a_bytes + tk * tn * b_bytes + tm * tn * 4 * (2 if has_add else 1))
                vmem += tm * tn * 4 if nk > 1 else 0
                if vmem > MM_VMEM_BUDGET:
                    continue
                steps = (m // tm) * (n // tn) * nk
                a_reads = 1 if nk == 1 else n // tn
                traffic = m * kk * a_bytes * a_reads + kk * n * b_bytes * (m // tm) + m * n * 4
                fill = min(1.0, tn / MXU_DIM) * min(1.0, tm / MXU_DIM)
                compute = 2.0 * m * n * kk / (MM_FLOPS_PER_US * fill)
                cost = steps * MM_STEP_US + max(compute, traffic / MM_HBM_BYTES_PER_US)
                if nk > 1:
                    cost += steps * tm * tn * 8 / MM_VMEM_BYTES_PER_US
                if best is None or cost < best[0]:
                    best = (cost, tm, tn, tk)
    return best[1:]


def mm(a, b, *, ta=False, tb=False, add=None, bias=None, b_chunks=False, out_chunks=False, out_dtype=F32, name):
    if ta:
        kk, m = a.shape
    else:
        m, kk = a.shape
    nch, wide = 1, None
    if b_chunks:
        nch, rows_b, wide = b.shape
        kb, n = (rows_b, nch * wide) if not tb else (nch * wide, rows_b)
    elif tb:
        n, kb = b.shape
    else:
        kb, n = b.shape
    if out_chunks:
        nch, wide = N_CHIPS, n // N_CHIPS
    assert kk == kb, (a.shape, b.shape, ta, tb)
    has_add, has_bias = add is not None, bias is not None
    tm, tn, tk = _mm_tiles(m, n, kk, wide if (wide and not tb) or out_chunks else n, wide if (wide and tb) else kk,
                           a.dtype.itemsize, b.dtype.itemsize, has_add)
    nk = kk // tk
    dims = ((0 if ta else 1,), (1 if tb else 0,))

    def kern(*refs):
        a_ref, b_ref = refs[0], refs[1]
        rest = list(refs[2:])
        add_ref = rest.pop(0) if has_add else None
        bias_ref = rest.pop(0) if has_bias else None
        o_ref = rest[0]
        part = _dot(a_ref[...].astype(BF16), b_ref[...].astype(BF16), dims)

        def finish(r):
            if has_add:
                r = r + add_ref[...]
            if has_bias:
                r = r + bias_ref[...]
            o_ref[...] = r.astype(out_dtype)

        if nk == 1:
            finish(part)
        else:
            acc_ref = rest[1]
            k = pl.program_id(2)

            @pl.when(k == 0)
            def _():
                acc_ref[...] = part

            @pl.when((k > 0) & (k < nk - 1))
            def _():
                acc_ref[...] += part

            @pl.when(k == nk - 1)
            def _():
                finish(acc_ref[...] + part)

    a_spec = pl.BlockSpec((tk, tm), lambda i, j, k: (k, i)) if ta else pl.BlockSpec((tm, tk), lambda i, j, k: (i, k))
    if b_chunks and tb:
        per = wide // tk
        b_spec = pl.BlockSpec((None, tn, tk), lambda i, j, k: (k // per, j, k % per))
    elif b_chunks:
        per = wide // tn
        b_spec = pl.BlockSpec((None, tk, tn), lambda i, j, k: (j // per, k, j % per))
    elif tb:
        b_spec = pl.BlockSpec((tn, tk), lambda i, j, k: (j, k))
    else:
        b_spec = pl.BlockSpec((tk, tn), lambda i, j, k: (k, j))
    if out_chunks:
        per_o = wide // tn
        out_spec = pl.BlockSpec((None, tm, tn), lambda i, j, k: (j // per_o, i, j % per_o))
        out_shape = jax.ShapeDtypeStruct((nch, m, wide), out_dtype)
    else:
        out_spec = pl.BlockSpec((tm, tn), lambda i, j, k: (i, j))
        out_shape = jax.ShapeDtypeStruct((m, n), out_dtype)
    in_specs, args = [a_spec, b_spec], [a, b]
    if has_add:
        in_specs.append(pl.BlockSpec((tm, tn), lambda i, j, k: (i, j)))
        args.append(add)
    if has_bias:
        in_specs.append(pl.BlockSpec((1, tn), lambda i, j, k: (0, j)))
        args.append(bias)
    return pl.pallas_call(
        kern,
        name=name,
        grid=(m // tm, n // tn, nk),
        in_specs=in_specs,
        out_specs=out_spec,
        out_shape=out_shape,
        scratch_shapes=[pltpu.VMEM((tm, tn), F32)] if nk > 1 else [],
        compiler_params=_params(("parallel", "parallel", "arbitrary")),
    )(*args)


def rowwise(fn, ins, outs, accs=(), *, tr, name):
    rows = [a for a, kind in ins if kind == "row"][0].shape[0]
    tr = min(tr, rows)
    assert rows % tr == 0 and tr % SUBLANES == 0, (rows, tr)
    n = rows // tr
    n_in, n_out = len(ins), len(outs)
    kinds = [kind for _, kind in ins]

    def kern(*refs):
        i = pl.program_id(0)
        vals = []
        for ref, kind in zip(refs[:n_in], kinds):
            v = ref[...]
            if kind == "prev":
                v = v * (i > 0).astype(v.dtype)
            elif kind == "next":
                v = v * (i < n - 1).astype(v.dtype)
            vals.append(v)
        res = fn(*vals)
        for ref, r in zip(refs[n_in:n_in + n_out], res[:n_out]):
            ref[...] = r.astype(ref.dtype)
        if accs:
            acc_refs = refs[n_in + n_out:]

            @pl.when(i == 0)
            def _():
                for ref in acc_refs:
                    ref[...] = jnp.zeros_like(ref)

            for ref, r in zip(acc_refs, res[n_out:]):
                ref[...] += r

    in_specs = []
    for a, kind in ins:
        if kind == "row":
            in_specs.append(pl.BlockSpec((tr, a.shape[1]), lambda i: (i, 0)))
        elif kind == "full":
            in_specs.append(pl.BlockSpec(a.shape, lambda i, nd=a.ndim: (0,) * nd))
        elif kind == "prev":
            in_specs.append(pl.BlockSpec((SUBLANES, a.shape[1]),
                                         lambda i: (jnp.maximum(i * (tr // SUBLANES) - 1, 0), 0)))
        else:
            in_specs.append(pl.BlockSpec((SUBLANES, a.shape[1]),
                                         lambda i: (jnp.minimum((i + 1) * (tr // SUBLANES), rows // SUBLANES - 1), 0)))
    out_specs = [pl.BlockSpec((tr, c), lambda i: (i, 0)) for c, _ in outs]
    out_specs += [pl.BlockSpec((r, c), lambda i: (0, 0)) for r, c in accs]
    out_shape = [jax.ShapeDtypeStruct((rows, c), dt) for c, dt in outs]
    out_shape += [jax.ShapeDtypeStruct((r, c), F32) for r, c in accs]
    res = pl.pallas_call(
        kern,
        name=name,
        grid=(n,),
        in_specs=in_specs,
        out_specs=out_specs,
        out_shape=out_shape,
        compiler_params=_params(("arbitrary",) if accs else ("parallel",)),
    )(*[a for a, _ in ins])
    return res


def rms_fwd(x, g, name):
    def fn(xv, gv):
        r = lax.rsqrt(jnp.mean(xv * xv, axis=1, keepdims=True) + EPS)
        return (xv * r * gv,)

    return rowwise(fn, [(x, "row"), (g, "full")], [(x.shape[1], BF16)], tr=512, name=name)[0]


def rms_bwd(x, g, dy, dres, name):
    def fn(xv, gv, dyv, drv):
        r = lax.rsqrt(jnp.mean(xv * xv, axis=1, keepdims=True) + EPS)
        xh = xv * r
        dyg = dyv * gv
        dx = drv + r * (dyg - xh * jnp.mean(dyg * xh, axis=1, keepdims=True))
        return dx, dx, _colsum(dyv * xh)

    c = x.shape[1]
    dx, dxb, dg = rowwise(fn, [(x, "row"), (g, "full"), (dy, "row"), (dres, "row")], [(c, F32), (c, BF16)], [(1, c)],
                          tr=256, name=name)
    return (dx, dxb), dg


def swiglu_fwd(gu, name):
    hid = gu.shape[1] // 2

    def fn(v):
        g, u = v[:, :hid].astype(F32), v[:, hid:].astype(F32)
        return (g * _sigmoid(g) * u,)

    return rowwise(fn, [(gu, "row")], [(hid, BF16)], tr=256, name=name)[0]


def swiglu_bwd(gu, da, name):
    hid = gu.shape[1] // 2

    def fn(v, d):
        g, u = v[:, :hid].astype(F32), v[:, hid:].astype(F32)
        s = _sigmoid(g)
        dg = d * u * s * (1.0 + g * (1.0 - s))
        du = d * g * s
        return (jnp.concatenate([dg, du], axis=1),)

    return rowwise(fn, [(gu, "row"), (da, "row")], [(2 * hid, BF16)], tr=256, name=name)[0]


def loss_and_grad(y, t, name):
    d = y.shape[1]

    def fn(yv, tv):
        e = yv - tv
        part = jnp.sum(_colsum(e * e), axis=1, keepdims=True) * (0.5 / d)
        dy = e * (1.0 / d)
        return dy, dy, jnp.broadcast_to(part, (SUBLANES, LANES))

    dy, dyb, acc = rowwise(fn, [(y, "row"), (t, "row")], [(d, F32), (d, BF16)], [(SUBLANES, LANES)], tr=512, name=name)
    return acc[0, 0], (dy, dyb)


def adamw(w, g, m, v, name):
    def fn(wv, gv, mv, vv):
        m2 = ADAM_B1 * mv + (1.0 - ADAM_B1) * gv
        v2 = ADAM_B2 * vv + (1.0 - ADAM_B2) * (gv * gv)
        m_hat = m2 / (1.0 - ADAM_B1 ** ADAM_STEP)
        v_hat = v2 / (1.0 - ADAM_B2 ** ADAM_STEP)
        delta = -ADAM_LR * (m_hat / (jnp.sqrt(v_hat) + ADAM_EPS) + ADAM_WD * wv)
        return delta, m2, v2

    rows, c = w.shape
    tr = _pick(rows, (256, 128, 64, 32, 16, 8)) if rows % SUBLANES == 0 else rows
    if rows % SUBLANES:
        return _whole(fn, [w, g, m, v], [(w.shape, F32)] * 3, name=name)
    return rowwise(fn, [(w, "row"), (g, "row"), (m, "row"), (v, "row")], [(c, F32)] * 3, tr=tr, name=name)


def _whole(fn, ins, outs, *, name):
    n_in = len(ins)

    def kern(*refs):
        res = fn(*[r[...] for r in refs[:n_in]])
        for ref, r in zip(refs[n_in:], res):
            ref[...] = r.astype(ref.dtype)

    return pl.pallas_call(
        kern,
        name=name,
        out_shape=[jax.ShapeDtypeStruct(s, dt) for s, dt in outs],
        compiler_params=pltpu.CompilerParams(vmem_limit_bytes=VMEM_LIMIT),
    )(*ins)


def ffn_fwd(x, g, wgu, wdown, tag):
    h = rms_fwd(x, g, f"ffn_rms_{tag}")
    gu = mm(h, wgu, b_chunks=True, out_dtype=BF16, name=f"ffn_gu_{tag}")
    a = swiglu_fwd(gu, f"ffn_act_{tag}")
    xn = mm(a, wdown, add=x, name=f"ffn_down_{tag}")
    return xn, (x, h, gu, a)


def ffn_bwd(dxn, saved, g, wgu, wdown, tag):
    x, h, gu, a = saved
    dxn, dxb = dxn
    da = mm(dxb, wdown, tb=True, name=f"ffn_da_{tag}")
    dwdown = mm(a, dxb, ta=True, out_dtype=BF16, name=f"ffn_dwdown_{tag}")
    dgu = swiglu_bwd(gu, da, f"ffn_dact_{tag}")
    dh = mm(dgu, wgu, tb=True, b_chunks=True, name=f"ffn_dh_{tag}")
    dwgu = mm(h, dgu, ta=True, out_dtype=BF16, out_chunks=True, name=f"ffn_dwgu_{tag}")
    dx, dg = rms_bwd(x, g, dh, dxn, f"ffn_drms_{tag}")
    return dx, dg, dwgu, dwdown


def _head_blockdiag(c):
    i = jnp.arange(c) // HEAD
    return (i[:, None] == i[None, :]).astype(BF16)


def _head_sums(x, bd):
    return jnp.concatenate([_dot_x2(x[:, g * LANES:(g + 1) * LANES], bd) for g in range(x.shape[1] // LANES)], axis=1)


def qknorm_fwd(qkv, qg, kg, bd, name):
    d = qkv.shape[1] // 3
    scale = 1.0 / math.sqrt(HEAD)

    def fn(v, qgv, kgv, bdv):
        v = v.astype(F32)
        q, k, vv = v[:, :d], v[:, d:2 * d], v[:, 2 * d:]
        rq = lax.rsqrt(_head_sums(q * q, bdv) * (1.0 / HEAD) + EPS)
        rk = lax.rsqrt(_head_sums(k * k, bdv) * (1.0 / HEAD) + EPS)
        return q * rq * qgv * scale, k * rk * kgv, vv

    return rowwise(fn, [(qkv, "row"), (qg, "full"), (kg, "full"), (bd, "full")],
                   [(d, BF16), (d, BF16), (d, BF16)], tr=256, name=name)


def qknorm_bwd(qkv, dqs, dkn, dv, qg, kg, bd, name):
    d = qkv.shape[1] // 3
    scale = 1.0 / math.sqrt(HEAD)

    def one(xv, gv, dyv, bdv):
        r = lax.rsqrt(_head_sums(xv * xv, bdv) * (1.0 / HEAD) + EPS)
        xh = xv * r
        dyg = dyv * gv
        dx = r * (dyg - xh * (_head_sums(dyg * xh, bdv) * (1.0 / HEAD)))
        return dx, _colsum(dyv * xh)

    def fn(v, dqv, dkv, dvv, qgv, kgv, bdv):
        v = v.astype(F32)
        q, k = v[:, :d], v[:, d:2 * d]
        dq, dqg = one(q, qgv, dqv * scale, bdv)
        dk, dkg = one(k, kgv, dkv, bdv)
        return jnp.concatenate([dq, dk, dvv], axis=1), dqg, dkg

    return rowwise(fn, [(qkv, "row"), (dqs, "row"), (dkn, "row"), (dv, "row"), (qg, "full"), (kg, "full"), (bd, "full")],
                   [(3 * d, BF16)], [(1, d), (1, d)], tr=256, name=name)


def _sb_tile(qh, k, mask, tri_gt):
    z = _dot_nt(qh, k)
    sp = jnp.log(1.0 + jnp.exp(-jnp.abs(z)))
    lb = jnp.minimum(z, 0.0) - sp
    l1 = jnp.where(mask, lb - z, 0.0)
    suf = _dot_x2(l1, tri_gt)
    return lb, l1, suf


def _sb_setup(tq, tk):
    row, col = _iota2((tq, tk), 0), _iota2((tq, tk), 1)
    lane = _iota2((1, LANES), 1)
    halves = [(lane < HEAD).astype(BF16), (lane >= HEAD).astype(BF16)]
    lane_q = _iota2((tq, LANES), 1) + jnp.minimum(_iota2((tq, LANES), 0), 0)
    return row, col, halves, lane_q


def sb_attn_fwd(qs, kn, vb, name, side=None):
    s, d = qs.shape
    tq, tk = min(SB_TQ, s), min(SB_TK, s)
    nq = s // tq
    assert s // tk <= LANES and s % tq == 0 and s % tk == 0

    def kern(q_ref, k_ref, v_ref, o_ref, rs_ref, acc_ref):
        i = pl.program_id(1)
        row, col, halves, lane_q = _sb_setup(tq, tk)
        tri_gt = (_iota2((tk, tk), 0) > _iota2((tk, tk), 1)).astype(BF16)
        q = q_ref[...]
        qh = [q * hm for hm in halves]
        acc_ref[...] = jnp.zeros_like(acc_ref)
        rs_ref[...] = jnp.full(rs_ref.shape, SB_UNSEEN, F32)
        nkb = (i + 1) * (tq // tk)

        def more(st):
            return (st[0] < nkb) & (st[1] > SB_DEAD)

        def step(st):
            n, r = st[0], list(st[2:])
            kb = nkb - 1 - n
            ks = pl.multiple_of(kb * tk, tk)
            k = k_ref[pl.ds(ks, tk), :]
            v = v_ref[pl.ds(ks, tk), :]
            mask = col < row + (i * tq - kb * tk)
            at_kb = lane_q == kb
            for hh in range(2):
                lb, l1, suf = _sb_tile(qh[hh], k, mask, tri_gt)
                w = jnp.where(mask, jnp.exp(lb + suf + r[hh]), 0.0)
                acc_ref[...] += _dot(w.astype(BF16), v * halves[hh])
                rs_ref[hh] = jnp.where(at_kb, r[hh], rs_ref[hh])
                r[hh] = r[hh] + _rowsum(l1)
            return (n + 1, jnp.maximum(jnp.max(r[0]), jnp.max(r[1])), r[0], r[1])

        z1 = jnp.zeros((tq, 1), F32)
        lax.while_loop(more, step, (jnp.int32(0), jnp.float32(0.0), z1, z1))
        o_ref[...] = acc_ref[...].astype(BF16)

    nh2 = d // LANES
    return side_call(
        kern, side,
        name=name,
        grid=(nh2, nq),
        in_specs=[pl.BlockSpec((tq, LANES), lambda h, i: (i, h)),
                  pl.BlockSpec((s, LANES), lambda h, i: (0, h)),
                  pl.BlockSpec((s, LANES), lambda h, i: (0, h))],
        out_specs=[pl.BlockSpec((tq, LANES), lambda h, i: (i, h)),
                   pl.BlockSpec((None, 2, tq, LANES), lambda h, i: (h, 0, i, 0))],
        out_shape=[jax.ShapeDtypeStruct((s, d), BF16), jax.ShapeDtypeStruct((nh2, 2, s, LANES), F32)],
        scratch_shapes=[pltpu.VMEM((tq, LANES), F32)],
        args=(qs, kn, vb))


def sb_attn_bwd(qs, kn, vb, rsave, do, name, side=None):
    s, d = qs.shape
    tq, tk = min(SB_TQ, s), min(SB_TK, s)
    nq = s // tq

    def kern(q_ref, k_ref, v_ref, rs_ref, do_ref, dq_ref, dk_ref, dv_ref):
        i = pl.program_id(1)

        @pl.when(i == 0)
        def _():
            dk_ref[...] = jnp.zeros_like(dk_ref)
            dv_ref[...] = jnp.zeros_like(dv_ref)

        row, col, halves, lane_q = _sb_setup(tq, tk)
        tri_gt = (_iota2((tk, tk), 0) > _iota2((tk, tk), 1)).astype(BF16)
        tri_lt = (_iota2((tk, tk), 0) < _iota2((tk, tk), 1)).astype(BF16)
        q = q_ref[...]
        qh = [q * hm for hm in halves]
        dov = do_ref[...].astype(BF16)
        doh = [dov * hm for hm in halves]
        dq_ref[...] = jnp.zeros_like(dq_ref)
        nkb = (i + 1) * (tq // tk)
        top = jnp.maximum(jnp.max(rs_ref[0], axis=0, keepdims=True), jnp.max(rs_ref[1], axis=0, keepdims=True))
        dead = (top <= SB_DEAD) & (_iota2((1, LANES), 1) < nkb)
        kstart = jnp.minimum(jnp.sum(dead.astype(F32)).astype(jnp.int32), nkb)

        def step(kb, ep):
            ep = list(ep)
            ks = pl.multiple_of(kb * tk, tk)
            k = k_ref[pl.ds(ks, tk), :]
            v = v_ref[pl.ds(ks, tk), :]
            mask = col < row + (i * tq - kb * tk)
            at_kb = lane_q == kb
            for hh in range(2):
                lb, l1, suf = _sb_tile(qh[hh], k, mask, tri_gt)
                r = _rowsum(jnp.where(at_kb, rs_ref[hh], 0.0))
                lbm = jnp.where(mask, lb, SB_UNSEEN)
                w = jnp.exp(lbm + suf + r)
                e = _dot_nt(doh[hh], v) * w
                pe = ep[hh] + _dot(e.astype(BF16), tri_lt)
                beta = jnp.exp(lbm)
                dz = (e - beta * (e + pe)).astype(BF16)
                dq_ref[...] += _dot(dz, k * halves[hh])
                dk_ref[pl.ds(ks, tk), :] += _dot_tn(dz, qh[hh])
                dv_ref[pl.ds(ks, tk), :] += _dot_tn(w.astype(BF16), doh[hh])
                ep[hh] = ep[hh] + _rowsum(e)
            return tuple(ep)

        z1 = jnp.zeros((tq, 1), F32)
        lax.fori_loop(kstart, nkb, step, (z1, z1))

    nh2 = d // LANES
    return side_call(
        kern, side,
        name=name,
        grid=(nh2, nq),
        in_specs=[pl.BlockSpec((tq, LANES), lambda h, i: (i, h)),
                  pl.BlockSpec((s, LANES), lambda h, i: (0, h)),
                  pl.BlockSpec((s, LANES), lambda h, i: (0, h)),
                  pl.BlockSpec((None, 2, tq, LANES), lambda h, i: (h, 0, i, 0)),
                  pl.BlockSpec((tq, LANES), lambda h, i: (i, h))],
        out_specs=[pl.BlockSpec((tq, LANES), lambda h, i: (i, h)),
                   pl.BlockSpec((s, LANES), lambda h, i: (0, h)),
                   pl.BlockSpec((s, LANES), lambda h, i: (0, h))],
        out_shape=[jax.ShapeDtypeStruct((s, d), F32)] * 3,
        scratch_shapes=[],
        args=(qs, kn, vb, rsave, do))


def _hooked(plan, tag, call, *args):
    side = plan.side(tag)
    outs, side_outs = call(*args, tag, side)
    if side is not None:
        plan.done(tag, side_outs)
    return outs


def sb_fwd(x, g, wqkv, qg, kg, wo, bd, tag, plan):
    h = rms_fwd(x, g, f"sb_rms_{tag}")
    qkv = mm(h, wqkv, b_chunks=True, out_dtype=BF16, name=f"sb_qkv_{tag}")
    qs, kn, vb = qknorm_fwd(qkv, qg, kg, bd, f"sb_qknorm_{tag}")
    o, rsave = _hooked(plan, f"sb_attn_{tag}", sb_attn_fwd, qs, kn, vb)
    xn = mm(o, wo(), add=x, name=f"sb_out_{tag}")
    return xn, (x, h, qkv, qs, kn, vb, rsave, o)


def sb_bwd(dxn, saved, g, wqkv, qg, kg, wo, bd, tag, plan):
    x, h, qkv, qs, kn, vb, rsave, o = saved
    dxn, dxb = dxn
    do = mm(dxb, wo, tb=True, name=f"sb_do_{tag}")
    dwo = mm(o, dxb, ta=True, out_dtype=BF16, name=f"sb_dwo_{tag}")
    dqs, dkn, dv = _hooked(plan, f"sb_dattn_{tag}", sb_attn_bwd, qs, kn, vb, rsave, do)
    dqkv, dqg, dkg = qknorm_bwd(qkv, dqs, dkn, dv, qg, kg, bd, f"sb_dqknorm_{tag}")
    dh = mm(dqkv, wqkv, tb=True, b_chunks=True, name=f"sb_dh_{tag}")
    dwqkv = mm(h, dqkv, ta=True, out_dtype=BF16, out_chunks=True, name=f"sb_dwqkv_{tag}")
    dx, dg = rms_bwd(x, g, dh, dxn, f"sb_drms_{tag}")
    nh = dqg.shape[1] // HEAD
    return dx, dg, dwqkv, dqg.reshape(nh, HEAD).sum(0), dkg.reshape(nh, HEAD).sum(0), dwo


def _gelu(x):
    return 0.5 * x * (1.0 + lax.erf(x * (1.0 / math.sqrt(2.0))))


def _gelu_grad(x):
    return 0.5 * (1.0 + lax.erf(x * (1.0 / math.sqrt(2.0)))) + x * jnp.exp(-0.5 * x * x) * (1.0 / math.sqrt(2.0 * math.pi))


def gm_act_fwd(pre, vg, name):
    half = pre.shape[1] // 2

    def fn(p, vgv):
        p = p.astype(F32)
        u = _gelu(p[:, :half])
        v = _gelu(p[:, half:])
        r = lax.rsqrt(jnp.mean(v * v, axis=1, keepdims=True) + EPS)
        return u, v * r * vgv

    return rowwise(fn, [(pre, "row"), (vg, "full")], [(half, F32), (half, BF16)], tr=256, name=name)


def gm_act_bwd(pre, du, dvn, vg, name):
    half = pre.shape[1] // 2

    def fn(p, duv, dvnv, vgv):
        p = p.astype(F32)
        pu, pv = p[:, :half], p[:, half:]
        v = _gelu(pv)
        r = lax.rsqrt(jnp.mean(v * v, axis=1, keepdims=True) + EPS)
        vh = v * r
        dyg = dvnv * vgv
        dv = r * (dyg - vh * jnp.mean(dyg * vh, axis=1, keepdims=True))
        dpre = jnp.concatenate([duv * _gelu_grad(pu), dv * _gelu_grad(pv)], axis=1)
        return dpre, _colsum(dvnv * vh), _colsum(dpre)

    return rowwise(fn, [(pre, "row"), (du, "row"), (dvn, "row"), (vg, "full")],
                   [(2 * half, BF16)], [(1, half), (1, 2 * half)], tr=256, name=name)


def gm_spatial_fwd(u, vn, wc, bst, name):
    s, c = u.shape
    t = CHUNK
    ng = c // LANES

    def kern(u_ref, v_ref, w_ref, b_ref, o_ref):
        for g in range(ng):
            sl = slice(g * LANES, (g + 1) * LANES)
            mixed = _dot(w_ref[g], v_ref[:, sl]) + b_ref[:, sl]
            o_ref[:, sl] = (u_ref[:, sl] * mixed).astype(BF16)

    return pl.pallas_call(
        kern,
        name=name,
        grid=(s // t,),
        in_specs=[pl.BlockSpec((t, c), lambda i: (i, 0)), pl.BlockSpec((t, c), lambda i: (i, 0)),
                  pl.BlockSpec(wc.shape, lambda i: (0, 0, 0)), pl.BlockSpec(bst.shape, lambda i: (0, 0))],
        out_specs=pl.BlockSpec((t, c), lambda i: (i, 0)),
        out_shape=jax.ShapeDtypeStruct((s, c), BF16),
        compiler_params=_params(("parallel",)),
    )(u, vn, wc, bst)


def gm_spatial_bwd(dgate, u, vn, wc, bst, name):
    s, c = u.shape
    t = CHUNK
    ng = c // LANES

    def kern(dg_ref, u_ref, v_ref, w_ref, b_ref, du_ref, dv_ref, dw_ref, db_ref):
        i = pl.program_id(0)

        @pl.when(i == 0)
        def _():
            dw_ref[...] = jnp.zeros_like(dw_ref)
            db_ref[...] = jnp.zeros_like(db_ref)

        for g in range(ng):
            sl = slice(g * LANES, (g + 1) * LANES)
            vg = v_ref[:, sl]
            dgv = dg_ref[:, sl]
            mixed = _dot(w_ref[g], vg) + b_ref[:, sl]
            du_ref[:, sl] = dgv * mixed
            dmix = dgv * u_ref[:, sl]
            dmb = dmix.astype(BF16)
            dv_ref[:, sl] = _dot_tn(w_ref[g], dmb)
            dw_ref[g] += _dot_nt(dmb, vg)
            db_ref[:, sl] += dmix

    return pl.pallas_call(
        kern,
        name=name,
        grid=(s // t,),
        in_specs=[pl.BlockSpec((t, c), lambda i: (i, 0))] * 3 +
                 [pl.BlockSpec(wc.shape, lambda i: (0, 0, 0)), pl.BlockSpec(bst.shape, lambda i: (0, 0))],
        out_specs=[pl.BlockSpec((t, c), lambda i: (i, 0)), pl.BlockSpec((t, c), lambda i: (i, 0)),
                   pl.BlockSpec(wc.shape, lambda i: (0, 0, 0)), pl.BlockSpec(bst.shape, lambda i: (0, 0))],
        out_shape=[jax.ShapeDtypeStruct((s, c), F32), jax.ShapeDtypeStruct((s, c), F32),
                   jax.ShapeDtypeStruct(wc.shape, F32), jax.ShapeDtypeStruct(bst.shape, F32)],
        compiler_params=_params(("arbitrary",)),
    )(dgate, u, vn, wc, bst)


def gm_fwd(x, g, w_in, b_in, vg, wc, bst, w_out, tag):
    h = rms_fwd(x, g, f"gm_rms_{tag}")
    pre = mm(h, w_in, bias=b_in, b_chunks=True, out_dtype=BF16, name=f"gm_in_{tag}")
    u, vn = gm_act_fwd(pre, vg, f"gm_act_{tag}")
    gate = gm_spatial_fwd(u, vn, wc, bst, f"gm_spatial_{tag}")
    xn = mm(gate, w_out, add=x, name=f"gm_out_{tag}")
    return xn, (x, h, pre, u, vn, gate)


def gm_bwd(dxn, saved, g, w_in, vg, wc, bst, w_out, tag):
    x, h, pre, u, vn, gate = saved
    dxn, dxb = dxn
    dgate = mm(dxb, w_out, tb=True, name=f"gm_dgate_{tag}")
    dwout = mm(gate, dxb, ta=True, out_dtype=BF16, name=f"gm_dwout_{tag}")
    du, dvn, dws, dbst = gm_spatial_bwd(dgate, u, vn, wc, bst, f"gm_dspatial_{tag}")
    dpre, dvg, dbin = gm_act_bwd(pre, du, dvn, vg, f"gm_dact_{tag}")
    dh = mm(dpre, w_in, tb=True, b_chunks=True, name=f"gm_dh_{tag}")
    dwin = mm(h, dpre, ta=True, out_dtype=BF16, out_chunks=True, name=f"gm_dwin_{tag}")
    dx, dg = rms_bwd(x, g, dh, dxn, f"gm_drms_{tag}")
    ng = wc.shape[0]
    dws = jnp.where(jnp.tril(jnp.ones((CHUNK, CHUNK), bool)), dws, 0.0)
    dbs = dbst.reshape(CHUNK, ng, LANES).sum(-1).T
    return dx, dg, dwin, dbin, dvg, dws, dbs, dwout


def _conv_taps(xv, prev):
    cat = jnp.concatenate([prev, xv], axis=0)
    return [pltpu.roll(cat, sh, 0)[SUBLANES:] for sh in (3, 2, 1)] + [xv]


def conv_fwd(xbc, ws, b, d_inner, name):
    c = xbc.shape[1]
    nst = (c - d_inner) // 2

    def fn(xv, prev, w0, w1, w2, w3, bv):
        taps = _conv_taps(xv, prev)
        pre = bv + w0 * taps[0] + w1 * taps[1] + w2 * taps[2] + w3 * taps[3]
        out = pre * _sigmoid(pre)
        return out[:, :d_inner], out[:, d_inner:d_inner + nst], out[:, d_inner + nst:]

    return rowwise(fn, [(xbc, "row"), (xbc, "prev")] + [(w, "full") for w in ws] + [(b, "full")],
                   [(d_inner, F32), (nst, F32), (nst, F32)], tr=256, name=name)


def conv_bwd_pre(xbc, ws, b, dxs_a, dxs_b, db_m, dc_m, name):
    c = xbc.shape[1]

    def fn(xv, prev, w0, w1, w2, w3, bv, da, db2, dbm, dcm):
        taps = _conv_taps(xv, prev)
        pre = bv + w0 * taps[0] + w1 * taps[1] + w2 * taps[2] + w3 * taps[3]
        sg = _sigmoid(pre)
        dout = jnp.concatenate([da + db2, dbm, dcm], axis=1)
        dpre = dout * sg * (1.0 + pre * (1.0 - sg))
        return (dpre,) + tuple(_colsum(dpre * tp) for tp in taps) + (_colsum(dpre),)

    return rowwise(fn, [(xbc, "row"), (xbc, "prev")] + [(w, "full") for w in ws] +
                   [(b, "full"), (dxs_a, "row"), (dxs_b, "row"), (db_m, "row"), (dc_m, "row")],
                   [(c, F32)], [(1, c)] * 5, tr=256, name=name)


def conv_bwd_in(dpre, ws, name):
    c = dpre.shape[1]

    def fn(dv, nxt, w0, w1, w2, w3):
        cat = jnp.concatenate([dv, nxt], axis=0)
        n = cat.shape[0]
        up = [pltpu.roll(cat, n - sh, 0)[:dv.shape[0]] for sh in (1, 2, 3)]
        return (w3 * dv + w2 * up[0] + w1 * up[1] + w0 * up[2],)

    return rowwise(fn, [(dpre, "row"), (dpre, "next")] + [(w, "full") for w in ws], [(c, BF16)], tr=256, name=name)[0]


def ssd_pre(dtr, bias, alog, name):
    def fn(d, bv, al, tri):
        dt = _softplus(d + bv)
        a = dt * (-jnp.exp(al))
        return dt, _dot_x3_left(tri, a)

    tri = jnp.tril(jnp.ones((CHUNK, CHUNK), BF16))
    return rowwise(fn, [(dtr, "row"), (bias, "full"), (alog, "full"), (tri, "full")],
                   [(LANES, F32), (LANES, F32)], tr=CHUNK, name=name)


def _ssd_layouts(v, ngroups, hpg):
    s = v.shape[0]
    per = v[:, :ngroups * hpg].reshape(s, ngroups, hpg).transpose(1, 0, 2)
    return jnp.repeat(per, LANES // hpg, axis=2)


def _lane_pick(tile, r, hpg):
    sel = (_iota2((LANES, LANES), 0) == r * (LANES // hpg)).astype(BF16)
    h1 = tile.astype(BF16)
    r1 = tile - h1.astype(F32)
    h2 = r1.astype(BF16)
    h3 = (r1 - h2.astype(F32)).astype(BF16)
    return _dot(h1, sel) + _dot(h2, sel) + _dot(h3, sel)


def _lane_place(cols, hpg):
    lane, w = _iota2((1, LANES), 1), LANES // hpg
    out = cols[0] * (lane < w).astype(F32)
    for r in range(1, hpg):
        out = out + cols[r] * ((lane >= r * w) & (lane < (r + 1) * w)).astype(F32)
    return out


def _ssd_rowform(acum, ngroups, hpg):
    s = acum.shape[0]
    nc = s // CHUNK
    a = acum[:, :ngroups * hpg].reshape(nc, CHUNK, ngroups, hpg).transpose(2, 0, 3, 1)
    last = jnp.broadcast_to(a[..., CHUNK - 1:], a.shape)
    return jnp.concatenate([a, last], axis=2)


def ssd_chunk_fwd(xs, bm, cm, col_a, col_dt, rowf, name, side=None):
    s, d_inner = xs.shape
    ln = CHUNK
    nc = s // ln
    nsub = _pick(nc, (SSD_SUB, 2, 1))
    rows = nsub * ln
    ng = col_a.shape[0]
    gw = d_inner // ng
    hpg = gw // HEAD
    assert gw % LANES == 0 and bm.shape[1] == ng * LANES

    def kern(x_ref, b_ref, c_ref, ca_ref, cd_ref, rf_ref, y_ref, hp_ref, h_scr):
        @pl.when(pl.program_id(1) == 0)
        def _():
            h_scr[...] = jnp.zeros_like(h_scr)

        causal = _iota2((ln, ln), 0) >= _iota2((ln, ln), 1)
        lane = _iota2((1, LANES), 1)
        for sc in range(nsub):
            rs = slice(sc * ln, (sc + 1) * ln)
            bb = b_ref[rs, :].astype(BF16)
            cbf = c_ref[rs, :].astype(BF16)
            cb = _dot_nt(cbf, bb)
            ys = [jnp.zeros((ln, LANES), F32) for _ in range(gw // LANES)]
            for r in range(hpg):
                j, hf = divmod(r, LANES // HEAD)
                mh = ((lane >= HEAD * hf) & (lane < HEAD * (hf + 1))).astype(F32)
                ac = _lane_pick(ca_ref[rs, :], r, hpg)
                ar = rf_ref[sc, pl.ds(r, 1), :]
                aend = rf_ref[sc, pl.ds(4 + r, 1), :]
                dm = jnp.exp(jnp.minimum(ac - ar, 0.0))
                m = jnp.where(causal, cb * dm, 0.0).astype(BF16)
                xdt = x_ref[rs, j * LANES:(j + 1) * LANES] * _lane_pick(cd_ref[rs, :], r, hpg) * mh
                h = h_scr[r]
                hp_ref[sc, r] = h
                ys[j] = ys[j] + _dot(m, xdt.astype(BF16)) + _dot_nt(cbf, h.astype(BF16)) * jnp.exp(ac)
                dte = jnp.exp(aend - ac)
                h_scr[r] = jnp.exp(aend) * h + _dot_tn((xdt * dte).astype(BF16), bb)
            for j in range(gw // LANES):
                y_ref[rs, j * LANES:(j + 1) * LANES] = ys[j]

    colspec = pl.BlockSpec((None, rows, LANES), lambda g, c: (g, c, 0))
    return side_call(
        kern, side,
        name=name,
        grid=(ng, nc // nsub),
        in_specs=[pl.BlockSpec((rows, gw), lambda g, c: (c, g)),
                  pl.BlockSpec((rows, LANES), lambda g, c: (c, g)),
                  pl.BlockSpec((rows, LANES), lambda g, c: (c, g)),
                  colspec, colspec,
                  pl.BlockSpec((None, nsub, 8, LANES), lambda g, c: (g, c, 0, 0))],
        out_specs=[pl.BlockSpec((rows, gw), lambda g, c: (c, g)),
                   pl.BlockSpec((None, nsub, hpg, LANES, LANES), lambda g, c: (g, c, 0, 0, 0))],
        out_shape=[jax.ShapeDtypeStruct((s, d_inner), F32),
                   jax.ShapeDtypeStruct((ng, nc, hpg, LANES, LANES), F32)],
        scratch_shapes=[pltpu.VMEM((hpg, LANES, LANES), F32)],
        args=(xs, bm, cm, col_a, col_dt, rowf))


def ssd_chunk_bwd(xs, bm, cm, col_a, col_dt, rowf, hprev, dy, name, side=None):
    s, d_inner = xs.shape
    ln = CHUNK
    nc = s // ln
    nsub = _pick(nc, (SSD_SUB, 2, 1))
    rows = nsub * ln
    ng = col_a.shape[0]
    gw = d_inner // ng
    hpg = gw // HEAD

    def kern(x_ref, b_ref, c_ref, ca_ref, cd_ref, rf_ref, hp_ref, dy_ref,
             dx_ref, db_ref, dc_ref, ddt_ref, da_ref, dh_scr):
        @pl.when(pl.program_id(1) == 0)
        def _():
            dh_scr[...] = jnp.zeros_like(dh_scr)

        row, col = _iota2((ln, ln), 0), _iota2((ln, ln), 1)
        causal = row >= col
        tri_ge = (col >= row).astype(BF16)
        ones = jnp.ones((ln, LANES), BF16)
        lane = _iota2((1, LANES), 1)
        last_row = (_iota2((ln, 1), 0) == ln - 1).astype(F32)
        for sc in reversed(range(nsub)):
            rs = slice(sc * ln, (sc + 1) * ln)
            bb = b_ref[rs, :].astype(BF16)
            cbf = c_ref[rs, :].astype(BF16)
            cb = _dot_nt(cbf, bb)
            dcb = jnp.zeros((ln, ln), F32)
            d_b = jnp.zeros((ln, LANES), F32)
            d_c = jnp.zeros((ln, LANES), F32)
            dxs = [jnp.zeros((ln, LANES), F32) for _ in range(gw // LANES)]
            das, ddts = [], []
            for r in range(hpg):
                j, hf = divmod(r, LANES // HEAD)
                mh = ((lane >= HEAD * hf) & (lane < HEAD * (hf + 1))).astype(F32)
                ac = _lane_pick(ca_ref[rs, :], r, hpg)
                dt = _lane_pick(cd_ref[rs, :], r, hpg)
                ar = rf_ref[sc, pl.ds(r, 1), :]
                aend = rf_ref[sc, pl.ds(4 + r, 1), :]
                dm = jnp.where(causal, jnp.exp(jnp.minimum(ac - ar, 0.0)), 0.0)
                m = cb * dm
                mb = m.astype(BF16)
                xp = x_ref[rs, j * LANES:(j + 1) * LANES]
                xdt = xp * dt * mh
                xdtb = xdt.astype(BF16)
                dyp = dy_ref[rs, j * LANES:(j + 1) * LANES] * mh
                dypb = dyp.astype(BF16)
                h = hp_ref[sc, r]
                hb = h.astype(BF16)
                dh = dh_scr[r]
                dhb = dh.astype(BF16)
                e_in = jnp.exp(ac)
                dte = jnp.exp(aend - ac)
                eend = jnp.exp(aend)
                d_m = _dot_nt(dypb, xdtb)
                dcb = dcb + d_m * dm
                gm = d_m * m
                yoff_pre = _dot_nt(cbf, hb)
                bdh = _dot_nt(bb, dhb)
                dxdt = _dot_tn(mb, dypb) + bdh * dte
                t1 = _rowsum(xdt * bdh) * dte
                gh, gl = _split2(gm)
                dacum = (_rowsum(gm) - (_dot_tn(gh, ones) + _dot_tn(gl, ones))
                         + _rowsum(dyp * yoff_pre) * e_in - t1)
                end_term = _colsum(t1) + eend * jnp.sum(_colsum(dh * h), axis=1, keepdims=True)
                dacum = dacum + last_row * end_term
                das.append(_dot_x3_left(tri_ge, dacum))
                ddts.append(jnp.broadcast_to(_rowsum(dxdt * xp), (ln, LANES)))
                dxs[j] = dxs[j] + dxdt * dt
                d_b = d_b + _dot((xdt * dte).astype(BF16), dhb)
                dye = (dyp * e_in).astype(BF16)
                d_c = d_c + _dot(dye, hb)
                dh_scr[r] = eend * dh + _dot_tn(dye, cbf)
            da_ref[rs, :] = _lane_place(das, hpg)
            ddt_ref[rs, :] = _lane_place(ddts, hpg)
            dcbb = dcb.astype(BF16)
            dc_ref[rs, :] = d_c + _dot(dcbb, bb)
            db_ref[rs, :] = d_b + _dot_tn(dcbb, cbf)
            for j in range(gw // LANES):
                dx_ref[rs, j * LANES:(j + 1) * LANES] = dxs[j]

    rev = nc // nsub - 1
    colspec = pl.BlockSpec((None, rows, LANES), lambda g, c: (g, rev - c, 0))
    return side_call(
        kern, side,
        name=name,
        grid=(ng, nc // nsub),
        in_specs=[pl.BlockSpec((rows, gw), lambda g, c: (rev - c, g)),
                  pl.BlockSpec((rows, LANES), lambda g, c: (rev - c, g)),
                  pl.BlockSpec((rows, LANES), lambda g, c: (rev - c, g)),
                  colspec, colspec,
                  pl.BlockSpec((None, nsub, 8, LANES), lambda g, c: (g, rev - c, 0, 0)),
                  pl.BlockSpec((None, nsub, hpg, LANES, LANES), lambda g, c: (g, rev - c, 0, 0, 0)),
                  pl.BlockSpec((rows, gw), lambda g, c: (rev - c, g))],
        out_specs=[pl.BlockSpec((rows, gw), lambda g, c: (rev - c, g)),
                   pl.BlockSpec((rows, LANES), lambda g, c: (rev - c, g)),
                   pl.BlockSpec((rows, LANES), lambda g, c: (rev - c, g)),
                   colspec, colspec],
        out_shape=[jax.ShapeDtypeStruct((s, d_inner), F32),
                   jax.ShapeDtypeStruct(bm.shape, F32), jax.ShapeDtypeStruct(cm.shape, F32),
                   jax.ShapeDtypeStruct(col_a.shape, F32), jax.ShapeDtypeStruct(col_a.shape, F32)],
        scratch_shapes=[pltpu.VMEM((hpg, LANES, LANES), F32)],
        args=(xs, bm, cm, col_a, col_dt, rowf, hprev, dy))


def gnorm_fwd(y, xs, z, dexp, gain, ngroups, name):
    c = y.shape[1]
    gw = c // ngroups

    def fn(yv, xv, zv, dv, gv):
        yg = (yv + xv * dv) * (zv * _sigmoid(zv))
        outs = []
        for k in range(ngroups):
            t = yg[:, k * gw:(k + 1) * gw]
            outs.append(t * lax.rsqrt(jnp.mean(t * t, axis=1, keepdims=True) + EPS))
        return (jnp.concatenate(outs, axis=1) * gv,)

    return rowwise(fn, [(y, "row"), (xs, "row"), (z, "row"), (dexp, "full"), (gain, "full")], [(c, BF16)], tr=256, name=name)[0]


def gnorm_bwd(dn, y, xs, z, dexp, gain, ngroups, name):
    c = y.shape[1]
    gw = c // ngroups

    def fn(dnv, yv, xv, zv, dv, gv):
        yd = yv + xv * dv
        sg = _sigmoid(zv)
        sz = zv * sg
        yg = yd * sz
        dng = dnv * gv
        dyg, yh = [], []
        for k in range(ngroups):
            sl = slice(k * gw, (k + 1) * gw)
            t = yg[:, sl]
            r = lax.rsqrt(jnp.mean(t * t, axis=1, keepdims=True) + EPS)
            th = t * r
            dyg.append(r * (dng[:, sl] - th * jnp.mean(dng[:, sl] * th, axis=1, keepdims=True)))
            yh.append(th)
        dyg = jnp.concatenate(dyg, axis=1)
        yh = jnp.concatenate(yh, axis=1)
        dyd = dyg * sz
        dz = dyg * yd * (sg * (1.0 + zv * (1.0 - sg)))
        return dyd, dyd * dv, dz, _colsum(dyd * xv), _colsum(dnv * yh)

    return rowwise(fn, [(dn, "row"), (y, "row"), (xs, "row"), (z, "row"), (dexp, "full"), (gain, "full")],
                   [(c, F32), (c, F32), (c, BF16)], [(1, c), (1, c)], tr=256, name=name)


def ssd_post(ddt, da, dt, dtr, bias, alog, name):
    def fn(ddtv, dav, dtv, dtrv, bv, al):
        a_neg = -jnp.exp(al)
        ddtr = (ddtv + dav * a_neg) * _sigmoid(dtrv + bv)
        return ddtr, _colsum(ddtr), _colsum(dav * dtv) * a_neg

    return rowwise(fn, [(ddt, "row"), (da, "row"), (dt, "row"), (dtr, "row"), (bias, "full"), (alog, "full")],
                   [(LANES, BF16)], [(1, LANES), (1, LANES)], tr=512, name=name)


def _from_colform(v, hpg):
    ng, s = v.shape[0], v.shape[1]
    flat = v[:, :, ::LANES // hpg].transpose(1, 0, 2).reshape(s, ng * hpg)
    return jnp.pad(flat, ((0, 0), (0, LANES - ng * hpg)))


def ssm_fwd(x, g, p, tag, plan):
    ng, hpg, d_inner = p["ng"], p["hpg"], p["d_inner"]
    h = rms_fwd(x, g, f"ssm_rms_{tag}")
    z = mm(h, p["w_z"], name=f"ssm_inz_{tag}")
    xbc = mm(h, p["w_xbc"], name=f"ssm_inx_{tag}")
    dtr = mm(h, p["w_dt"], name=f"ssm_indt_{tag}")
    xs, bm, cm = conv_fwd(xbc, p["conv_w"], p["conv_b"], d_inner, f"ssm_conv_{tag}")
    dt, acum = ssd_pre(dtr, p["dt_bias"], p["a_log"], f"ssm_pre_{tag}")
    col_a, col_dt = _ssd_layouts(acum, ng, hpg), _ssd_layouts(dt, ng, hpg)
    rowf = _ssd_rowform(acum, ng, hpg)
    y, hprev = _hooked(plan, f"ssm_scan_{tag}", ssd_chunk_fwd, xs, bm, cm, col_a, col_dt, rowf)
    n = gnorm_fwd(y, xs, z, p["d_exp"], p["norm_gain"], ng, f"ssm_gnorm_{tag}")
    xn = mm(n, p["w_out"], add=x, name=f"ssm_out_{tag}")
    return xn, (x, h, z, xbc, dtr, xs, bm, cm, dt, col_a, col_dt, rowf, y, hprev, n)


def ssm_bwd(dxn, saved, g, p, tag, plan):
    x, h, z, xbc, dtr, xs, bm, cm, dt, col_a, col_dt, rowf, y, hprev, n = saved
    ng, hpg, d_inner = p["ng"], p["hpg"], p["d_inner"]
    s = x.shape[0]
    dxn, dxb = dxn
    dn = mm(dxb, p["w_out"], tb=True, name=f"ssm_dn_{tag}")
    dwout = mm(n, dxb, ta=True, out_dtype=BF16, name=f"ssm_dwout_{tag}")
    dy, dxs_skip, dz, dd_lane, dgain = gnorm_bwd(dn, y, xs, z, p["d_exp"], p["norm_gain"], ng, f"ssm_dgnorm_{tag}")
    dxs, dbm, dcm, ddt_c, da_c = _hooked(plan, f"ssm_dscan_{tag}", ssd_chunk_bwd, xs, bm, cm, col_a, col_dt, rowf, hprev, dy)
    ddtr, dbias, dalog = ssd_post(_from_colform(ddt_c, hpg), _from_colform(da_c, hpg), dt, dtr,
                                  p["dt_bias"], p["a_log"], f"ssm_post_{tag}")
    res = conv_bwd_pre(xbc, p["conv_w"], p["conv_b"], dxs, dxs_skip, dbm, dcm, f"ssm_dconv_{tag}")
    dpre, dconv_w, dconv_b = res[0], jnp.concatenate(res[1:5], axis=0), res[5]
    dxbc = conv_bwd_in(dpre, p["conv_w"], f"ssm_dconvin_{tag}")
    dh = mm(dz, p["w_z"], tb=True, name=f"ssm_dhz_{tag}")
    dh = mm(dxbc, p["w_xbc"], tb=True, add=dh, name=f"ssm_dhx_{tag}")
    dh = mm(ddtr, p["w_dt"], tb=True, add=dh, name=f"ssm_dhdt_{tag}")
    dwz = mm(h, dz, ta=True, out_dtype=BF16, name=f"ssm_dwz_{tag}")
    dwxbc = mm(h, dxbc, ta=True, out_dtype=BF16, name=f"ssm_dwxbc_{tag}")
    dwdt = mm(h, ddtr, ta=True, out_dtype=BF16, name=f"ssm_dwdt_{tag}")
    dx, dg = rms_bwd(x, g, dh, dxn, f"ssm_drms_{tag}")
    nh = ng * hpg
    dwin = jnp.concatenate([dwz, dwxbc, dwdt[:, :nh]], axis=1)
    dd = dd_lane.reshape(nh, HEAD).sum(-1)
    return dx, dg, dict(w_in=dwin, conv_w=dconv_w, conv_b=dconv_b, dt_bias=dbias[0, :nh], a_log=dalog[0, :nh],
                        d=dd, norm_gain=dgain, w_out=dwout)


def local_step(x, target, w, plan):
    d = x.shape[1]
    depth = w["mix_norm"].shape[0]
    bd = _head_blockdiag(LANES)
    tril = jnp.tril(jnp.ones((CHUNK, CHUNK), bool))
    ssm_heads = w["ssm_dt_bias"].shape[1]
    d_inner = w["ssm_norm_gain"].shape[1]
    ng = w["ssm_norm_gain"].shape[1] // 256
    nstate = CHUNK

    def pad_lanes(v):
        return jnp.pad(v, ((0, 0), (0, LANES - v.shape[1])))

    def ssm_params(j):
        w_in = w["ssm_w_in"][j]
        cw = w["ssm_conv_w"][j]
        return dict(ng=ng, hpg=ssm_heads // ng, d_inner=d_inner,
                    w_z=w_in[:, :d_inner], w_xbc=w_in[:, d_inner:d_inner + d_inner + 2 * ng * nstate],
                    w_dt=pad_lanes(w_in[:, 2 * d_inner + 2 * ng * nstate:]),
                    conv_w=[cw[k:k + 1] for k in range(cw.shape[0])], conv_b=w["ssm_conv_b"][j:j + 1],
                    dt_bias=pad_lanes(w["ssm_dt_bias"][j:j + 1]), a_log=pad_lanes(w["ssm_a_log"][j:j + 1]),
                    d_exp=jnp.repeat(w["ssm_d"][j], HEAD)[None, :], norm_gain=w["ssm_norm_gain"][j:j + 1],
                    w_out=w["ssm_w_out"][j])

    def gm_params(j):
        wc = jnp.where(tril, w["gm_w_s"][j], 0.0).astype(BF16)
        bst = jnp.repeat(w["gm_b_s"][j].T, LANES, axis=1)
        return wc, bst

    def sb_gains(j):
        nh = d // HEAD
        return jnp.tile(w["sb_q_gain"][j], nh)[None, :], jnp.tile(w["sb_k_gain"][j], nh)[None, :]

    saved = []
    cur = x
    for i in range(depth):
        kind, j = i % 3, i // 3
        gmix = w["mix_norm"][i:i + 1]
        if kind == 0:
            qg, kg = sb_gains(j)
            cur, sv = sb_fwd(cur, gmix, w["sb_w_qkv"][j], qg, kg, lambda j=j: w["sb_w_o"][j], bd, f"{i}", plan)
        elif kind == 1:
            wc, bst = gm_params(j)
            cur, sv = gm_fwd(cur, gmix, w["gm_w_in"][j], w["gm_b_in"][j:j + 1], w["gm_v_gain"][j:j + 1], wc, bst,
                             w["gm_w_out"][j], f"{i}")
        else:
            cur, sv = ssm_fwd(cur, gmix, ssm_params(j), f"{i}", plan)
        cur, sv2 = ffn_fwd(cur, w["ffn_norm"][i:i + 1], w["ffn_w_gu"][i], w["ffn_w_down"][i], f"{i}")
        saved.append((sv, sv2))

    loss, dcur = loss_and_grad(cur, target, "loss")

    grads = {k: [None] * len(v) for k, v in w.items()}
    for i in reversed(range(depth)):
        kind, j = i % 3, i // 3
        sv, sv2 = saved[i]
        gmix = w["mix_norm"][i:i + 1]
        dcur, dgf, dwgu, dwdown = ffn_bwd(dcur, sv2, w["ffn_norm"][i:i + 1], w["ffn_w_gu"][i], w["ffn_w_down"][i], f"{i}")
        grads["ffn_norm"][i], grads["ffn_w_gu"][i], grads["ffn_w_down"][i] = dgf[0], dwgu, dwdown
        plan.grads_ready({("ffn_w_gu", i): dwgu, ("ffn_w_down", i): dwdown})
        if kind == 0:
            qg, kg = sb_gains(j)
            dcur, dg, dwqkv, dqg, dkg, dwo = sb_bwd(dcur, sv, gmix, w["sb_w_qkv"][j], qg, kg, w["sb_w_o"][j], bd, f"{i}", plan)
            grads["sb_w_qkv"][j], grads["sb_q_gain"][j], grads["sb_k_gain"][j], grads["sb_w_o"][j] = dwqkv, dqg, dkg, dwo
        elif kind == 1:
            wc, bst = gm_params(j)
            dcur, dg, dwin, dbin, dvg, dws, dbs, dwout = gm_bwd(dcur, sv, gmix, w["gm_w_in"][j], w["gm_v_gain"][j:j + 1],
                                                                 wc, bst, w["gm_w_out"][j], f"{i}")
            grads["gm_w_in"][j], grads["gm_b_in"][j], grads["gm_v_gain"][j] = dwin, dbin[0], dvg[0]
            grads["gm_w_s"][j], grads["gm_b_s"][j], grads["gm_w_out"][j] = dws, dbs, dwout
        else:
            dcur, dg, gs = ssm_bwd(dcur, sv, gmix, ssm_params(j), f"{i}", plan)
            grads["ssm_w_in"][j], grads["ssm_conv_w"][j], grads["ssm_conv_b"][j] = gs["w_in"], gs["conv_w"], gs["conv_b"][0]
            grads["ssm_dt_bias"][j], grads["ssm_a_log"][j], grads["ssm_d"][j] = gs["dt_bias"], gs["a_log"], gs["d"]
            grads["ssm_norm_gain"][j], grads["ssm_w_out"][j] = gs["norm_gain"][0], gs["w_out"]
        grads["mix_norm"][i] = dg[0]
        mixer = {0: ("sb_w_qkv", "sb_w_o"), 1: ("gm_w_in", "gm_w_out"), 2: ("ssm_w_in", "ssm_w_out")}[kind]
        plan.grads_ready({(n, j): grads[n][j] for n in mixer})
    grads = {k: (v if k in MATRICES else jnp.stack(v)) for k, v in grads.items()}
    return loss, dcur[0], grads


WEIGHTS = ["mix_norm", "ffn_norm", "sb_w_qkv", "sb_q_gain", "sb_k_gain", "sb_w_o", "gm_w_in", "gm_b_in", "gm_v_gain",
           "gm_w_s", "gm_b_s", "gm_w_out", "ssm_w_in", "ssm_conv_w", "ssm_conv_b", "ssm_dt_bias", "ssm_a_log", "ssm_d",
           "ssm_norm_gain", "ssm_w_out", "ffn_w_gu", "ffn_w_down"]
SHARDED = {"sb_w_qkv": 2, "sb_w_o": 1, "gm_w_in": 2, "gm_w_out": 1, "ssm_w_in": 2, "ssm_conv_w": 2, "ssm_conv_b": 1,
           "ssm_norm_gain": 1, "ssm_w_out": 1, "ffn_w_gu": 2, "ffn_w_down": 1}
EXACT = ("ssm_conv_w", "ssm_conv_b", "ssm_norm_gain")
MATRICES = tuple(n for n in SHARDED if n not in EXACT)
COLUMN_BLOCKS = ("sb_w_qkv", "gm_w_in", "ffn_w_gu")
REPLICATED = [n for n in WEIGHTS if n not in SHARDED]
N_CHIPS = 4
N_DEV = 8
PACK_COLS = 1024


def _pack(pieces, dtype, align):
    flat = jnp.concatenate([p.reshape(-1).astype(dtype) for p in pieces])
    rows = -(-flat.shape[0] // (PACK_COLS * align)) * align
    flat = jnp.pad(flat, (0, rows * PACK_COLS - flat.shape[0]))
    return flat.reshape(rows, PACK_COLS)


def _unpack(flat, shapes):
    out, off = [], 0
    for shp in shapes:
        n = math.prod(shp)
        out.append(flat[off:off + n].reshape(shp))
        off += n
    return out


ANY = pl.BlockSpec(memory_space=pl.ANY)


def _pos():
    return lax.axis_index("x"), lax.axis_index("y"), lax.axis_index("c")


def _remote(src, dst, send, recv, k, to):
    return pltpu.make_async_remote_copy(src_ref=src, dst_ref=dst, send_sem=send.at[k], recv_sem=recv.at[k],
                                        device_id=to, device_id_type=MESH_ID)


def _comm_call(body, name, ins, out_shapes, nsem, aliases=None):
    return pl.pallas_call(
        body, name=name, out_shape=out_shapes,
        in_specs=[ANY] * len(ins), out_specs=[ANY] * len(out_shapes),
        scratch_shapes=[pltpu.SemaphoreType.DMA((nsem,)), pltpu.SemaphoreType.DMA((nsem,))],
        input_output_aliases=aliases or {},
    )(*ins)


def stage_shard(w, chip, name):
    rows, cols = w.shape
    tr = _pick(rows, (256, 352, 128))

    def kern(idx_ref, w_ref, o_ref):
        o_ref[...] = w_ref[...].astype(BF16)

    grid_spec = pltpu.PrefetchScalarGridSpec(
        num_scalar_prefetch=1, grid=(rows // tr,),
        in_specs=[pl.BlockSpec((tr, cols), lambda i, idx: (i, 0))],
        out_specs=pl.BlockSpec((None, tr, cols), lambda i, idx: (idx[0], i, 0)))
    return pl.pallas_call(
        kern, name=name, grid_spec=grid_spec,
        out_shape=jax.ShapeDtypeStruct((N_CHIPS, rows, cols), BF16),
        compiler_params=_params(("parallel",)),
    )(jnp.reshape(chip, (1,)).astype(jnp.int32), w)


class Side:
    def __init__(self, arrays, out_shapes, aliases, nsem, start, finish):
        self.arrays, self.out_shapes, self.aliases, self.nsem = list(arrays), list(out_shapes), aliases, nsem
        self.start, self.finish = start, finish


def run_side(side, name):
    n_in, n_out = len(side.arrays), len(side.out_shapes)

    def body(*refs):
        ins, outs = refs[:n_in], refs[n_in:n_in + n_out]
        send, recv = refs[n_in + n_out:]
        side.start(ins, outs, send, recv)
        side.finish(ins, outs, send, recv)

    return _comm_call(body, name, side.arrays, side.out_shapes, side.nsem, aliases=side.aliases)


def side_call(kern, side, *, name, grid, in_specs, out_specs, out_shape, scratch_shapes, args):
    if side is None:
        res = pl.pallas_call(kern, name=name, grid=grid, in_specs=in_specs, out_specs=out_specs, out_shape=out_shape,
                             scratch_shapes=scratch_shapes,
                             compiler_params=_params(("parallel",) + ("arbitrary",) * (len(grid) - 1)))(*args)
        return list(res), []
    n_in, n_out, n_scr = len(in_specs), len(out_specs), len(scratch_shapes)
    s_in, s_out = len(side.arrays), len(side.out_shapes)

    def body(*refs):
        ins, refs = refs[:n_in], refs[n_in:]
        side_ins, refs = refs[:s_in], refs[s_in:]
        outs, refs = refs[:n_out], refs[n_out:]
        side_outs, refs = refs[:s_out], refs[s_out:]
        scr, (send, recv) = refs[:n_scr], refs[n_scr:]
        first, last = None, None
        for axis, size in enumerate(grid):
            at0, at1 = pl.program_id(axis) == 0, pl.program_id(axis) == size - 1
            first = at0 if first is None else first & at0
            last = at1 if last is None else last & at1

        @pl.when(first)
        def _():
            side.start(side_ins, side_outs, send, recv)

        kern(*ins, *outs, *scr)

        @pl.when(last)
        def _():
            side.finish(side_ins, side_outs, send, recv)

    res = pl.pallas_call(
        body, name=name, grid=grid,
        in_specs=list(in_specs) + [ANY] * s_in, out_specs=list(out_specs) + [ANY] * s_out,
        out_shape=list(out_shape) + side.out_shapes,
        scratch_shapes=list(scratch_shapes) + [pltpu.SemaphoreType.DMA((side.nsem,)), pltpu.SemaphoreType.DMA((side.nsem,))],
        input_output_aliases={n_in + a: n_out + b for a, b in side.aliases.items()},
        compiler_params=_params(("arbitrary",) * len(grid)),
    )(*args, *side.arrays)
    return list(res[:n_out]), list(res[n_out:])


def gather_side(staged):
    n = len(staged)

    def plan(o_refs, send, recv):
        x, y, c = _pos()
        chips = [(1 - x, y), (x, 1 - y), (1 - x, 1 - y)]

        def part(u, chip, cc):
            half = staged[u].shape[1] // 2
            return o_refs[u].at[2 * chip[0] + chip[1], pl.ds(cc * half, half), :]

        first = [_remote(part(u, (x, y), c), part(u, (x, y), c), send, recv, 6 * u + j, (*chip, c))
                 for u in range(n) for j, chip in enumerate(chips)]
        landed = [_remote(part(u, chip, c), part(u, chip, c), send, recv, 6 * u + j, (x, y, c))
                  for u in range(n) for j, chip in enumerate(chips)]
        passed = [_remote(part(u, chip, c), part(u, chip, c), send, recv, 6 * u + 3 + j, (x, y, 1 - c))
                  for u in range(n) for j, chip in enumerate(chips)]
        handed = [_remote(part(u, chip, 1 - c), part(u, chip, 1 - c), send, recv, 6 * u + 3 + j, (x, y, c))
                  for u in range(n) for j, chip in enumerate(chips)]
        return first, landed, passed, handed

    def start(ins, outs, send, recv):
        for cp in plan(outs, send, recv)[0]:
            cp.start()

    def finish(ins, outs, send, recv):
        first, landed, passed, handed = plan(outs, send, recv)
        for got, fw in zip(landed, passed):
            got.wait_recv()
            fw.start()
        for got in handed:
            got.wait_recv()
        for cp in first + passed:
            cp.wait_send()

    outs = [jax.ShapeDtypeStruct(s.shape, s.dtype) for s in staged]
    return Side(staged, outs, {u: u for u in range(n)}, 6 * n, start, finish)


def swap_halves(gps, name):
    n = len(gps)

    def body(*refs):
        g_refs, r_refs = refs[:n], refs[n:2 * n]
        send, recv = refs[2 * n:]
        x, y, c = _pos()
        cps = []
        for u in range(n):
            half = gps[u].shape[1] // 2
            cps.append(_remote(g_refs[u].at[:, pl.ds((1 - c) * half, half), :], r_refs[u], send, recv, u, (x, y, 1 - c)))
        for cp in cps:
            cp.start()
        for cp in cps:
            cp.wait()

    outs = [jax.ShapeDtypeStruct((g.shape[0], g.shape[1] // 2, g.shape[2]), g.dtype) for g in gps]
    return _comm_call(body, name, gps, outs, n)


def scatter_side(parts):
    n = len(parts)

    def plan(p_refs, r_refs, send, recv):
        x, y, c = _pos()
        chips = [(1 - x, y), (x, 1 - y), (1 - x, 1 - y)]
        return [_remote(p_refs[u].at[2 * chip[0] + chip[1]], r_refs[u].at[j], send, recv, 3 * u + j, (*chip, c))
                for u in range(n) for j, chip in enumerate(chips)]

    def start(ins, outs, send, recv):
        for cp in plan(ins, outs, send, recv):
            cp.start()

    def finish(ins, outs, send, recv):
        for cp in plan(ins, outs, send, recv):
            cp.wait()

    outs = [jax.ShapeDtypeStruct((N_CHIPS - 1,) + p.shape[1:], p.dtype) for p in parts]
    return Side(parts, outs, {}, 3 * n, start, finish)


def join_halves(bufs):
    n = len(bufs)

    def body(*refs):
        o_refs = refs[n:2 * n]
        send, recv = refs[2 * n:]
        x, y, c = _pos()

        def rows(u, cc):
            half = bufs[u].shape[0] // 2
            return o_refs[u].at[pl.ds(cc * half, half), :]

        cps = [_remote(rows(u, c), rows(u, c), send, recv, u, (x, y, 1 - c)) for u in range(n)]
        for cp in cps:
            cp.start()
        for u in range(n):
            _remote(rows(u, 1 - c), rows(u, 1 - c), send, recv, u, (x, y, c)).wait_recv()
        for cp in cps:
            cp.wait_send()

    outs = [jax.ShapeDtypeStruct(b.shape, b.dtype) for b in bufs]
    return _comm_call(body, "join_halves", bufs, outs, n, aliases={u: u for u in range(n)})


def gather_small(sg, name):
    rows, cols = sg.shape

    def body(s_ref, o_ref, send, recv, lsem):
        x, y, c = _pos()
        me, sibling = (x, y, c), (x, y, 1 - c)
        chips = [(1 - x, y), (x, 1 - y), (1 - x, 1 - y)]

        def blk(px, py, pc):
            return o_ref.at[4 * px + 2 * py + pc]

        mine = pltpu.make_async_copy(s_ref, blk(*me), lsem)
        mine.start()
        first = [_remote(s_ref, blk(*me), send, recv, 0, sibling)]
        first += [_remote(s_ref, blk(*me), send, recv, 1 + j, (*chip, c)) for j, chip in enumerate(chips)]
        for cp in first:
            cp.start()
        passed = [_remote(blk(*chip, c), blk(*chip, c), send, recv, 4 + j, sibling) for j, chip in enumerate(chips)]
        for j, chip in enumerate(chips):
            _remote(blk(*chip, c), blk(*chip, c), send, recv, 1 + j, me).wait_recv()
            passed[j].start()
        _remote(blk(*sibling), blk(*sibling), send, recv, 0, me).wait_recv()
        for j, chip in enumerate(chips):
            _remote(blk(*chip, 1 - c), blk(*chip, 1 - c), send, recv, 4 + j, me).wait_recv()
        for cp in first + passed:
            cp.wait_send()
        mine.wait()

    return pl.pallas_call(
        body, name=name,
        out_shape=jax.ShapeDtypeStruct((N_DEV, rows, cols), sg.dtype),
        in_specs=[ANY], out_specs=ANY,
        scratch_shapes=[pltpu.SemaphoreType.DMA((N_DEV - 1,)), pltpu.SemaphoreType.DMA((N_DEV - 1,)), pltpu.SemaphoreType.DMA],
    )(sg)


def sum_cores(gp, theirs, core, chip, name):
    nch, rows, cols = gp.shape
    half = rows // 2
    tr = _pick(half, (256, 176, 128, 64))
    nb = half // tr

    def kern(idx_ref, g_ref, t_ref, own_ref, all_ref):
        k = pl.program_id(1)
        s = g_ref[...].astype(F32) + t_ref[...].astype(F32)
        all_ref[...] = s.astype(BF16)

        @pl.when(k == idx_ref[1])
        def _():
            own_ref[...] = s

    grid_spec = pltpu.PrefetchScalarGridSpec(
        num_scalar_prefetch=1, grid=(nb, nch),
        in_specs=[pl.BlockSpec((None, tr, cols), lambda i, k, idx: (k, idx[0] * nb + i, 0)),
                  pl.BlockSpec((None, tr, cols), lambda i, k, idx: (k, i, 0))],
        out_specs=[pl.BlockSpec((tr, cols), lambda i, k, idx: (i, 0)),
                   pl.BlockSpec((None, tr, cols), lambda i, k, idx: (k, i, 0))])
    return pl.pallas_call(
        kern, name=name, grid_spec=grid_spec,
        out_shape=[jax.ShapeDtypeStruct((half, cols), F32), jax.ShapeDtypeStruct((nch, half, cols), BF16)],
        compiler_params=_params(("parallel", "arbitrary")),
    )(jnp.stack([core, chip]).astype(jnp.int32), gp, theirs)


def sum_chips(own, others, core, name):
    half, cols = own.shape
    tr = _pick(half, (256, 176, 128, 64))
    nb = half // tr

    def kern(idx_ref, o_ref, a_ref, b_ref, c_ref, out_ref):
        out_ref[...] = ((o_ref[...] + a_ref[...].astype(F32)) + b_ref[...].astype(F32)) + c_ref[...].astype(F32)

    grid_spec = pltpu.PrefetchScalarGridSpec(
        num_scalar_prefetch=1, grid=(nb,),
        in_specs=[pl.BlockSpec((tr, cols), lambda i, idx: (i, 0))] +
                 [pl.BlockSpec((None, tr, cols), lambda i, idx, j=j: (j, i, 0)) for j in range(N_CHIPS - 1)],
        out_specs=pl.BlockSpec((tr, cols), lambda i, idx: (idx[0] * nb + i, 0)))
    return pl.pallas_call(
        kern, name=name, grid_spec=grid_spec,
        out_shape=jax.ShapeDtypeStruct((2 * half, cols), F32),
        compiler_params=_params(("parallel",)),
    )(jnp.reshape(core, (1,)).astype(jnp.int32), own, others, others, others)


def small_update(gath, w, m, v, name):
    def fn(*vs):
        g = vs[0]
        for t in vs[1:N_DEV]:
            g = g + t
        wv, mv, vv = vs[N_DEV:]
        m2 = ADAM_B1 * mv + (1.0 - ADAM_B1) * g
        v2 = ADAM_B2 * vv + (1.0 - ADAM_B2) * (g * g)
        m_hat = m2 / (1.0 - ADAM_B1 ** ADAM_STEP)
        v_hat = v2 / (1.0 - ADAM_B2 ** ADAM_STEP)
        return g, -ADAM_LR * (m_hat / (jnp.sqrt(v_hat) + ADAM_EPS) + ADAM_WD * wv), m2, v2

    c = w.shape[1]
    ins = [(gath[k], "row") for k in range(N_DEV)] + [(w, "row"), (m, "row"), (v, "row")]
    return rowwise(fn, ins, [(c, F32)] * 4, tr=w.shape[0] // 2, name=name)


_MIX = {0: [("sb_w_qkv", 0), ("sb_w_o", 0)], 1: [("gm_w_in", 0), ("gm_w_out", 0)],
        2: [("ssm_w_in", 0), ("ssm_w_out", 0)], 3: [("sb_w_qkv", 1), ("sb_w_o", 1)]}
_FFN = {i: [("ffn_w_gu", i), ("ffn_w_down", i)] for i in range(4)}
GATHER_FIRST = _MIX[0][:1]
GATHER_AT = {"sb_attn_0": _MIX[0][1:] + _FFN[0] + _MIX[1] + _FFN[1] + _MIX[2] + _FFN[2], "ssm_scan_2": _MIX[3] + _FFN[3]}
SCATTER_AT = {"ssm_dscan_2": _FFN[3] + _MIX[3] + _FFN[2], "sb_dattn_0": _MIX[2] + _FFN[1] + _MIX[1] + _FFN[0]}
SCATTER_LAST = _MIX[0]


class _Plan:
    def __init__(self, ins, core, chip):
        self.core, self.chip = core, chip
        self.staged = {(n, l): stage_shard(ins[n][l], chip, f"stage_{n}_{l}")
                       for n in MATRICES for l in range(ins[n].shape[0])}
        self.full = {n: [None] * ins[n].shape[0] for n in MATRICES}
        self.ready = {}
        self.parts = {}
        self.halves = {}
        self.swaps = 0
        self._fill(GATHER_FIRST, run_side(gather_side([self.staged[u] for u in GATHER_FIRST]), "gather_first"))

    def _fill(self, units, gathered):
        for (n, l), g in zip(units, gathered):
            if n in COLUMN_BLOCKS:
                self.full[n][l] = g
            elif n == "ssm_w_in":
                self.full[n][l] = jnp.concatenate([g[k] for k in range(N_CHIPS)], axis=1)
            else:
                self.full[n][l] = g.reshape(-1, g.shape[-1])

    def _prepare(self, units):
        gps = [self.ready[u] for u in units]
        theirs = swap_halves(gps, f"swap_halves_{self.swaps}")
        self.swaps += 1
        for (n, l), g, t in zip(units, gps, theirs):
            self.parts[(n, l)] = sum_cores(g, t, self.core, self.chip, f"sum_cores_{n}_{l}")

    def _reduce(self, units, others):
        for (n, l), other in zip(units, others):
            self.halves[(n, l)] = sum_chips(self.parts[(n, l)][0], other, self.core, f"sum_chips_{n}_{l}")

    def side(self, tag):
        if tag in GATHER_AT:
            return gather_side([self.staged[u] for u in GATHER_AT[tag]])
        if tag in SCATTER_AT:
            self._prepare(SCATTER_AT[tag])
            return scatter_side([self.parts[u][1] for u in SCATTER_AT[tag]])
        return None

    def done(self, tag, results):
        if tag in GATHER_AT:
            self._fill(GATHER_AT[tag], results)
        else:
            self._reduce(SCATTER_AT[tag], results)

    def grads_ready(self, grads):
        for (n, l), g in grads.items():
            if n in COLUMN_BLOCKS:
                self.ready[(n, l)] = g
            elif n == "ssm_w_in":
                self.ready[(n, l)] = jnp.stack(jnp.split(g, N_CHIPS, axis=1))
            else:
                self.ready[(n, l)] = g.reshape(N_CHIPS, -1, g.shape[-1])

    def shard_grads(self):
        self._prepare(SCATTER_LAST)
        self._reduce(SCATTER_LAST, run_side(scatter_side([self.parts[u][1] for u in SCATTER_LAST]), "scatter_last"))
        units = sorted(self.halves)
        return dict(zip(units, join_halves([self.halves[u] for u in units])))


def _step(ins):
    x, target = ins["x"][0], ins["loss_target"][0]
    core = lax.axis_index("c")
    chip = 2 * lax.axis_index("x") + lax.axis_index("y")

    def lane_pad(v):
        return jnp.pad(v, ((0, 0), (0, PACK_COLS - v.shape[1])))

    vec_rows = [ins["ssm_conv_w"][0], ins["ssm_conv_b"], lane_pad(ins["ssm_norm_gain"])]
    blk = jnp.concatenate(vec_rows + [jnp.zeros((SUBLANES - 6, PACK_COLS), F32)], axis=0)
    per_chip = gather_small(blk, "gather_vectors")[0::2]
    ngw = ins["ssm_norm_gain"].shape[1]
    full = {
        "ssm_conv_w": jnp.concatenate([per_chip[k, 0:4] for k in range(N_CHIPS)], axis=1)[None],
        "ssm_conv_b": jnp.concatenate([per_chip[k, 4:5] for k in range(N_CHIPS)], axis=1),
        "ssm_norm_gain": jnp.concatenate([per_chip[k, 5:6, :ngw] for k in range(N_CHIPS)], axis=1),
    }

    plan = _Plan(ins, core, chip)
    full.update(plan.full)
    for n in REPLICATED:
        full[n] = ins[n]

    loss, dx, grads = local_step(x, target, full, plan)
    loss = lax.psum(loss, ALL_AXES)
    gshards = plan.shard_grads()

    small_shapes = [ins[n].shape for n in REPLICATED]
    vec_shapes = [grads[n].shape for n in EXACT]
    vec_pack = _pack([grads[n] for n in EXACT], F32, SUBLANES)
    gath = gather_small(jnp.concatenate([_pack([grads[n] for n in REPLICATED], F32, SUBLANES), vec_pack], axis=0),
                        "gather_small")
    packed = [jnp.concatenate([_pack([ins[pre + n] for n in REPLICATED], F32, SUBLANES), jnp.zeros_like(vec_pack)], axis=0)
              for pre in ("", "m_", "v_")]
    res = small_update(gath, *packed, name="small_update")
    nrep = res[0].shape[0] - vec_pack.shape[0]
    small = [dict(zip(REPLICATED, _unpack(r[:nrep].reshape(-1), small_shapes))) for r in res]
    vec_g = dict(zip(EXACT, _unpack(res[0][nrep:].reshape(-1), vec_shapes)))

    out_g, out_d, out_m, out_v = {}, {}, {}, {}
    for n in REPLICATED:
        out_g[n], out_d[n], out_m[n], out_v[n] = (s[n] for s in small)
    for n in SHARDED:
        shp = ins[n].shape
        if n in EXACT:
            g = lax.dynamic_slice_in_dim(vec_g[n], chip * shp[-1], shp[-1], axis=vec_g[n].ndim - 1)
        else:
            g = jnp.stack([gshards[(n, l)] for l in range(shp[0])])
        two = (math.prod(shp[:-1]), shp[-1])
        d2, m2, v2 = adamw(ins[n].reshape(two), g.reshape(two), ins["m_" + n].reshape(two),
                           ins["v_" + n].reshape(two), f"adamw_{n}")
        out_g[n], out_d[n], out_m[n], out_v[n] = g, d2.reshape(shp), m2.reshape(shp), v2.reshape(shp)
    return (loss, dx[None], *[out_g[n] for n in WEIGHTS], *[out_d[n] for n in WEIGHTS],
            *[out_m[n] for n in WEIGHTS], *[out_v[n] for n in WEIGHTS])


def kernel(x, mix_norm, ffn_norm, sb_w_qkv, sb_q_gain, sb_k_gain, sb_w_o, gm_w_in, gm_b_in, gm_v_gain, gm_w_s, gm_b_s, gm_w_out, ssm_w_in, ssm_conv_w, ssm_conv_b, ssm_dt_bias, ssm_a_log, ssm_d, ssm_norm_gain, ssm_w_out, ffn_w_gu, ffn_w_down, loss_target, m_mix_norm, m_ffn_norm, m_sb_w_qkv, m_sb_q_gain, m_sb_k_gain, m_sb_w_o, m_gm_w_in, m_gm_b_in, m_gm_v_gain, m_gm_w_s, m_gm_b_s, m_gm_w_out, m_ssm_w_in, m_ssm_conv_w, m_ssm_conv_b, m_ssm_dt_bias, m_ssm_a_log, m_ssm_d, m_ssm_norm_gain, m_ssm_w_out, m_ffn_w_gu, m_ffn_w_down, v_mix_norm, v_ffn_norm, v_sb_w_qkv, v_sb_q_gain, v_sb_k_gain, v_sb_w_o, v_gm_w_in, v_gm_b_in, v_gm_v_gain, v_gm_w_s, v_gm_b_s, v_gm_w_out, v_ssm_w_in, v_ssm_conv_w, v_ssm_conv_b, v_ssm_dt_bias, v_ssm_a_log, v_ssm_d, v_ssm_norm_gain, v_ssm_w_out, v_ffn_w_gu, v_ffn_w_down):
    return _step(dict(locals()))
```

```python
import functools
import math

import jax
import jax.numpy as jnp
from jax import lax
from jax.experimental import pallas as pl
from jax.experimental.pallas import tpu as pltpu

F32 = jnp.float32
BF16 = jnp.bfloat16
EPS = 1e-6
LANES = 128
SUBLANES = 8
VMEM_LIMIT = 56 * 1024 * 1024
HEAD = 64
CHUNK = 128
SB_TQ, SB_TK = 256, 256
SSD_SUB = 4
SB_DEAD = -110.0
SB_UNSEEN = -1e30
ADAM_LR, ADAM_B1, ADAM_B2, ADAM_EPS, ADAM_WD, ADAM_STEP = 0.001, 0.9, 0.999, 1e-08, 0.01, 10
MESH_ID = pl.DeviceIdType.MESH
ALL_AXES = ("x", "y", "c")


def _params(sem):
    return pltpu.CompilerParams(dimension_semantics=sem, vmem_limit_bytes=VMEM_LIMIT)


def _pick(n, cands):
    for c in cands:
        if n % c == 0:
            return c
    return n


def _dot(a, b, dims=((1,), (0,))):
    return lax.dot_general(a, b, (dims, ((), ())), preferred_element_type=F32)


def _dot_nt(a, b):
    return _dot(a, b, ((1,), (1,)))


def _dot_tn(a, b):
    return _dot(a, b, ((0,), (0,)))


def _split2(x):
    hi = x.astype(BF16)
    lo = (x - hi.astype(F32)).astype(BF16)
    return hi, lo


def _dot_x2(x, m):
    hi, lo = _split2(x)
    return _dot(hi, m) + _dot(lo, m)


def _dot_x3_left(m, x):
    h1 = x.astype(BF16)
    r1 = x - h1.astype(F32)
    h2 = r1.astype(BF16)
    h3 = (r1 - h2.astype(F32)).astype(BF16)
    return _dot(m, h1) + _dot(m, h2) + _dot(m, h3)


def _sigmoid(x):
    return 1.0 / (1.0 + jnp.exp(-x))


def _softplus(x):
    return jnp.maximum(x, 0.0) + jnp.log(1.0 + jnp.exp(-jnp.abs(x)))


def _colsum(x):
    return jnp.sum(x, axis=0, keepdims=True)


def _rowsum(x):
    return jnp.sum(x, axis=1, keepdims=True)


def _iota2(shape, dim):
    return lax.broadcasted_iota(jnp.int32, shape, dim)


MM_VMEM_BUDGET = 40 * 1024 * 1024
MM_STEP_US = 0.35
MM_HBM_BYTES_PER_US = 3.0e6
MM_VMEM_BYTES_PER_US = 1.5e6
MM_FLOPS_PER_US = 9.0e8
MXU_DIM = 256


def _mm_tiles(m, n, kk, wn, wk, a_bytes, b_bytes, has_add):
    def divisors(total, cands):
        got = [c for c in cands if total % c == 0 and c <= total]
        return got or [total]

    best = None
    for tm in divisors(m, (1024, 512, 256, 128)):
        for tn in divisors(wn, (1024, 768, 1408, 512, 256, 128)):
            for tk in divisors(wk, (4096, 2816, 2048, 1408, 1024, 768, 512, 256, 128)):
                nk = kk // tk
                vmem = 2 * (tm * tk * a_bytes + tk * tn * b_bytes + tm * tn * 4 * (2 if has_add else 1))
                vmem += tm * tn * 4 if nk > 1 else 0
                if vmem > MM_VMEM_BUDGET:
                    continue
                steps = (m // tm) * (n // tn) * nk
                a_reads = 1 if nk == 1 else n // tn
                traffic = m * kk * a_bytes * a_reads + kk * n * b_bytes * (m // tm) + m * n * 4
                fill = min(1.0, tn / MXU_DIM) * min(1.0, tm / MXU_DIM)
                compute = 2.0 * m * n * kk / (MM_FLOPS_PER_US * fill)
                cost = steps * MM_STEP_US + max(compute, traffic / MM_HBM_BYTES_PER_US)
                if nk > 1:
                    cost += steps * tm * tn * 8 / MM_VMEM_BYTES_PER_US
                if best is None or cost < best[0]:
                    best = (cost, tm, tn, tk)
    return best[1:]


def mm(a, b, *, ta=False, tb=False, add=None, bias=None, b_chunks=False, out_chunks=False, out_dtype=F32, name,
       side=None):
    if ta:
        kk, m = a.shape
    else:
        m, kk = a.shape
    nch, wide = 1, None
    if b_chunks:
        nch, rows_b, wide = b.shape
        kb, n = (rows_b, nch * wide) if not tb else (nch * wide, rows_b)
    elif tb:
        n, kb = b.shape
    else:
        kb, n = b.shape
    if out_chunks:
        nch, wide = N_CHIPS, n // N_CHIPS
    assert kk == kb, (a.shape, b.shape, ta, tb)
    has_add, has_bias = add is not None, bias is not None
    tm, tn, tk = _mm_tiles(m, n, kk, wide if (wide and not tb) or out_chunks else n, wide if (wide and tb) else kk,
                           a.dtype.itemsize, b.dtype.itemsize, has_add)
    nk = kk // tk
    dims = ((0 if ta else 1,), (1 if tb else 0,))

    def kern(*refs):
        a_ref, b_ref = refs[0], refs[1]
        rest = list(refs[2:])
        add_ref = rest.pop(0) if has_add else None
        bias_ref = rest.pop(0) if has_bias else None
        o_ref = rest[0]
        part = _dot(a_ref[...].astype(BF16), b_ref[...].astype(BF16), dims)

        def finish(r):
            if has_add:
                r = r + add_ref[...]
            if has_bias:
                r = r + bias_ref[...]
            o_ref[...] = r.astype(out_dtype)

        if nk == 1:
            finish(part)
        else:
            acc_ref = rest[1]
            k = pl.program_id(2)

            @pl.when(k == 0)
            def _():
                acc_ref[...] = part

            @pl.when((k > 0) & (k < nk - 1))
            def _():
                acc_ref[...] += part

            @pl.when(k == nk - 1)
            def _():
                finish(acc_ref[...] + part)

    a_spec = pl.BlockSpec((tk, tm), lambda i, j, k: (k, i)) if ta else pl.BlockSpec((tm, tk), lambda i, j, k: (i, k))
    if b_chunks and tb:
        per = wide // tk
        b_spec = pl.BlockSpec((None, tn, tk), lambda i, j, k: (k // per, j, k % per))
    elif b_chunks:
        per = wide // tn
        b_spec = pl.BlockSpec((None, tk, tn), lambda i, j, k: (j // per, k, j % per))
    elif tb:
        b_spec = pl.BlockSpec((tn, tk), lambda i, j, k: (j, k))
    else:
        b_spec = pl.BlockSpec((tk, tn), lambda i, j, k: (k, j))
    if out_chunks:
        per_o = wide // tn
        out_spec = pl.BlockSpec((None, tm, tn), lambda i, j, k: (j // per_o, i, j % per_o))
        out_shape = jax.ShapeDtypeStruct((nch, m, wide), out_dtype)
    else:
        out_spec = pl.BlockSpec((tm, tn), lambda i, j, k: (i, j))
        out_shape = jax.ShapeDtypeStruct((m, n), out_dtype)
    in_specs, args = [a_spec, b_spec], [a, b]
    if has_add:
        in_specs.append(pl.BlockSpec((tm, tn), lambda i, j, k: (i, j)))
        args.append(add)
    if has_bias:
        in_specs.append(pl.BlockSpec((1, tn), lambda i, j, k: (0, j)))
        args.append(bias)
    (out,), side_outs = side_call(
        kern, side,
        name=name,
        grid=(m // tm, n // tn, nk),
        in_specs=in_specs,
        out_specs=[out_spec],
        out_shape=[out_shape],
        scratch_shapes=[pltpu.VMEM((tm, tn), F32)] if nk > 1 else [],
        args=args)
    return out if side is None else (out, side_outs)


def mm_hooked(plan, a, b, *, name, **kw):
    side = plan.side(name)
    if side is None:
        return mm(a, b, name=name, **kw)
    out, side_outs = mm(a, b, name=name, side=side, **kw)
    plan.done(name, side_outs)
    return out


def rowwise(fn, ins, outs, accs=(), *, tr, name):
    rows = [a for a, kind in ins if kind == "row"][0].shape[0]
    tr = min(tr, rows)
    assert rows % tr == 0 and tr % SUBLANES == 0, (rows, tr)
    n = rows // tr
    n_in, n_out = len(ins), len(outs)
    kinds = [kind for _, kind in ins]

    def kern(*refs):
        i = pl.program_id(0)
        vals = []
        for ref, kind in zip(refs[:n_in], kinds):
            v = ref[...]
            if kind == "prev":
                v = v * (i > 0).astype(v.dtype)
            elif kind == "next":
                v = v * (i < n - 1).astype(v.dtype)
            vals.append(v)
        res = fn(*vals)
        for ref, r in zip(refs[n_in:n_in + n_out], res[:n_out]):
            ref[...] = r.astype(ref.dtype)
        if accs:
            acc_refs = refs[n_in + n_out:]

            @pl.when(i == 0)
            def _():
                for ref in acc_refs:
                    ref[...] = jnp.zeros_like(ref)

            for ref, r in zip(acc_refs, res[n_out:]):
                ref[...] += r

    in_specs = []
    for a, kind in ins:
        if kind == "row":
            in_specs.append(pl.BlockSpec((tr, a.shape[1]), lambda i: (i, 0)))
        elif kind == "full":
            in_specs.append(pl.BlockSpec(a.shape, lambda i, nd=a.ndim: (0,) * nd))
        elif kind == "prev":
            in_specs.append(pl.BlockSpec((SUBLANES, a.shape[1]),
                                         lambda i: (jnp.maximum(i * (tr // SUBLANES) - 1, 0), 0)))
        else:
            in_specs.append(pl.BlockSpec((SUBLANES, a.shape[1]),
                                         lambda i: (jnp.minimum((i + 1) * (tr // SUBLANES), rows // SUBLANES - 1), 0)))
    out_specs = [pl.BlockSpec((tr, c), lambda i: (i, 0)) for c, _ in outs]
    out_specs += [pl.BlockSpec((r, c), lambda i: (0, 0)) for r, c in accs]
    out_shape = [jax.ShapeDtypeStruct((rows, c), dt) for c, dt in outs]
    out_shape += [jax.ShapeDtypeStruct((r, c), F32) for r, c in accs]
    res = pl.pallas_call(
        kern,
        name=name,
        grid=(n,),
        in_specs=in_specs,
        out_specs=out_specs,
        out_shape=out_shape,
        compiler_params=_params(("arbitrary",) if accs else ("parallel",)),
    )(*[a for a, _ in ins])
    return res


def rms_fwd(x, g, name):
    def fn(xv, gv):
        r = lax.rsqrt(jnp.mean(xv * xv, axis=1, keepdims=True) + EPS)
        return (xv * r * gv,)

    return rowwise(fn, [(x, "row"), (g, "full")], [(x.shape[1], BF16)], tr=512, name=name)[0]


def rms_bwd(x, g, dy, dres, name):
    def fn(xv, gv, dyv, drv):
        r = lax.rsqrt(jnp.mean(xv * xv, axis=1, keepdims=True) + EPS)
        xh = xv * r
        dyg = dyv * gv
        dx = drv + r * (dyg - xh * jnp.mean(dyg * xh, axis=1, keepdims=True))
        return dx, dx, _colsum(dyv * xh)

    c = x.shape[1]
    dx, dxb, dg = rowwise(fn, [(x, "row"), (g, "full"), (dy, "row"), (dres, "row")], [(c, F32), (c, BF16)], [(1, c)],
                          tr=256, name=name)
    return (dx, dxb), dg


def swiglu_fwd(gu, name):
    hid = gu.shape[1] // 2

    def fn(v):
        g, u = v[:, :hid].astype(F32), v[:, hid:].astype(F32)
        return (g * _sigmoid(g) * u,)

    return rowwise(fn, [(gu, "row")], [(hid, BF16)], tr=256, name=name)[0]


def swiglu_bwd(gu, da, name):
    hid = gu.shape[1] // 2

    def fn(v, d):
        g, u = v[:, :hid].astype(F32), v[:, hid:].astype(F32)
        s = _sigmoid(g)
        dg = d * u * s * (1.0 + g * (1.0 - s))
        du = d * g * s
        return (jnp.concatenate([dg, du], axis=1),)

    return rowwise(fn, [(gu, "row"), (da, "row")], [(2 * hid, BF16)], tr=256, name=name)[0]


def loss_and_grad(y, t, name):
    d = y.shape[1]

    def fn(yv, tv):
        e = yv - tv
        part = jnp.sum(_colsum(e * e), axis=1, keepdims=True) * (0.5 / d)
        dy = e * (1.0 / d)
        return dy, dy, jnp.broadcast_to(part, (SUBLANES, LANES))

    dy, dyb, acc = rowwise(fn, [(y, "row"), (t, "row")], [(d, F32), (d, BF16)], [(SUBLANES, LANES)], tr=512, name=name)
    return acc[0, 0], (dy, dyb)


def adamw(w, g, m, v, name):
    def fn(wv, gv, mv, vv):
        m2 = ADAM_B1 * mv + (1.0 - ADAM_B1) * gv
        v2 = ADAM_B2 * vv + (1.0 - ADAM_B2) * (gv * gv)
        m_hat = m2 / (1.0 - ADAM_B1 ** ADAM_STEP)
        v_hat = v2 / (1.0 - ADAM_B2 ** ADAM_STEP)
        delta = -ADAM_LR * (m_hat / (jnp.sqrt(v_hat) + ADAM_EPS) + ADAM_WD * wv)
        return delta, m2, v2

    rows, c = w.shape
    tr = _pick(rows, (256, 128, 64, 32, 16, 8)) if rows % SUBLANES == 0 else rows
    if rows % SUBLANES:
        return _whole(fn, [w, g, m, v], [(w.shape, F32)] * 3, name=name)
    return rowwise(fn, [(w, "row"), (g, "row"), (m, "row"), (v, "row")], [(c, F32)] * 3, tr=tr, name=name)


def _whole(fn, ins, outs, *, name):
    n_in = len(ins)

    def kern(*refs):
        res = fn(*[r[...] for r in refs[:n_in]])
        for ref, r in zip(refs[n_in:], res):
            ref[...] = r.astype(ref.dtype)

    return pl.pallas_call(
        kern,
        name=name,
        out_shape=[jax.ShapeDtypeStruct(s, dt) for s, dt in outs],
        compiler_params=pltpu.CompilerParams(vmem_limit_bytes=VMEM_LIMIT),
    )(*ins)


def ffn_fwd(x, g, wgu, wdown, tag, plan):
    h = rms_fwd(x, g, f"ffn_rms_{tag}")
    gu = mm_hooked(plan, h, wgu, b_chunks=True, out_dtype=BF16, name=f"ffn_gu_{tag}")
    a = swiglu_fwd(gu, f"ffn_act_{tag}")
    xn = mm_hooked(plan, a, wdown, add=x, name=f"ffn_down_{tag}")
    return xn, (x, h, gu, a)


def ffn_bwd(dxn, saved, g, wgu, wdown, tag):
    x, h, gu, a = saved
    dxn, dxb = dxn
    da = mm(dxb, wdown, tb=True, name=f"ffn_da_{tag}")
    dwdown = mm(a, dxb, ta=True, out_dtype=BF16, name=f"ffn_dwdown_{tag}")
    dgu = swiglu_bwd(gu, da, f"ffn_dact_{tag}")
    dh = mm(dgu, wgu, tb=True, b_chunks=True, name=f"ffn_dh_{tag}")
    dwgu = mm(h, dgu, ta=True, out_dtype=BF16, out_chunks=True, name=f"ffn_dwgu_{tag}")
    dx, dg = rms_bwd(x, g, dh, dxn, f"ffn_drms_{tag}")
    return dx, dg, dwgu, dwdown


def _head_blockdiag(c):
    i = jnp.arange(c) // HEAD
    return (i[:, None] == i[None, :]).astype(BF16)


def _head_sums(x, bd):
    return jnp.concatenate([_dot_x2(x[:, g * LANES:(g + 1) * LANES], bd) for g in range(x.shape[1] // LANES)], axis=1)


def qknorm_fwd(qkv, qg, kg, bd, name):
    d = qkv.shape[1] // 3
    scale = 1.0 / math.sqrt(HEAD)

    def fn(v, qgv, kgv, bdv):
        v = v.astype(F32)
        q, k, vv = v[:, :d], v[:, d:2 * d], v[:, 2 * d:]
        rq = lax.rsqrt(_head_sums(q * q, bdv) * (1.0 / HEAD) + EPS)
        rk = lax.rsqrt(_head_sums(k * k, bdv) * (1.0 / HEAD) + EPS)
        return q * rq * qgv * scale, k * rk * kgv, vv

    return rowwise(fn, [(qkv, "row"), (qg, "full"), (kg, "full"), (bd, "full")],
                   [(d, BF16), (d, BF16), (d, BF16)], tr=256, name=name)


def qknorm_bwd(qkv, dqs, dkn, dv, qg, kg, bd, name):
    d = qkv.shape[1] // 3
    scale = 1.0 / math.sqrt(HEAD)

    def one(xv, gv, dyv, bdv):
        r = lax.rsqrt(_head_sums(xv * xv, bdv) * (1.0 / HEAD) + EPS)
        xh = xv * r
        dyg = dyv * gv
        dx = r * (dyg - xh * (_head_sums(dyg * xh, bdv) * (1.0 / HEAD)))
        return dx, _colsum(dyv * xh)

    def fn(v, dqv, dkv, dvv, qgv, kgv, bdv):
        v = v.astype(F32)
        q, k = v[:, :d], v[:, d:2 * d]
        dq, dqg = one(q, qgv, dqv * scale, bdv)
        dk, dkg = one(k, kgv, dkv, bdv)
        return jnp.concatenate([dq, dk, dvv], axis=1), dqg, dkg

    return rowwise(fn, [(qkv, "row"), (dqs, "row"), (dkn, "row"), (dv, "row"), (qg, "full"), (kg, "full"), (bd, "full")],
                   [(3 * d, BF16)], [(1, d), (1, d)], tr=256, name=name)


def _sb_tile(qh, k, mask, tri_gt):
    z = _dot_nt(qh, k)
    sp = jnp.log(1.0 + jnp.exp(-jnp.abs(z)))
    lb = jnp.minimum(z, 0.0) - sp
    l1 = jnp.where(mask, lb - z, 0.0)
    suf = _dot_x2(l1, tri_gt)
    return lb, l1, suf


def _sb_setup(tq, tk):
    row, col = _iota2((tq, tk), 0), _iota2((tq, tk), 1)
    lane = _iota2((1, LANES), 1)
    halves = [(lane < HEAD).astype(BF16), (lane >= HEAD).astype(BF16)]
    lane_q = _iota2((tq, LANES), 1) + jnp.minimum(_iota2((tq, LANES), 0), 0)
    return row, col, halves, lane_q


def sb_attn_fwd(qs, kn, vb, name, side=None):
    s, d = qs.shape
    tq, tk = min(SB_TQ, s), min(SB_TK, s)
    nq = s // tq
    assert s // tk <= LANES and s % tq == 0 and s % tk == 0

    def kern(q_ref, k_ref, v_ref, o_ref, rs_ref, acc_ref):
        i = pl.program_id(1)
        row, col, halves, lane_q = _sb_setup(tq, tk)
        tri_gt = (_iota2((tk, tk), 0) > _iota2((tk, tk), 1)).astype(BF16)
        q = q_ref[...]
        qh = [q * hm for hm in halves]
        acc_ref[...] = jnp.zeros_like(acc_ref)
        rs_ref[...] = jnp.full(rs_ref.shape, SB_UNSEEN, F32)
        nkb = (i + 1) * (tq // tk)

        def more(st):
            return (st[0] < nkb) & (st[1] > SB_DEAD)

        def step(st):
            n, r = st[0], list(st[2:])
            kb = nkb - 1 - n
            ks = pl.multiple_of(kb * tk, tk)
            k = k_ref[pl.ds(ks, tk), :]
            v = v_ref[pl.ds(ks, tk), :]
            mask = col < row + (i * tq - kb * tk)
            at_kb = lane_q == kb
            for hh in range(2):
                lb, l1, suf = _sb_tile(qh[hh], k, mask, tri_gt)
                w = jnp.where(mask, jnp.exp(lb + suf + r[hh]), 0.0)
                acc_ref[...] += _dot(w.astype(BF16), v * halves[hh])
                rs_ref[hh] = jnp.where(at_kb, r[hh], rs_ref[hh])
                r[hh] = r[hh] + _rowsum(l1)
            return (n + 1, jnp.maximum(jnp.max(r[0]), jnp.max(r[1])), r[0], r[1])

        z1 = jnp.zeros((tq, 1), F32)
        lax.while_loop(more, step, (jnp.int32(0), jnp.float32(0.0), z1, z1))
        o_ref[...] = acc_ref[...].astype(BF16)

    nh2 = d // LANES
    return side_call(
        kern, side,
        name=name,
        grid=(nh2, nq),
        in_specs=[pl.BlockSpec((tq, LANES), lambda h, i: (i, h)),
                  pl.BlockSpec((s, LANES), lambda h, i: (0, h)),
                  pl.BlockSpec((s, LANES), lambda h, i: (0, h))],
        out_specs=[pl.BlockSpec((tq, LANES), lambda h, i: (i, h)),
                   pl.BlockSpec((None, 2, tq, LANES), lambda h, i: (h, 0, i, 0))],
        out_shape=[jax.ShapeDtypeStruct((s, d), BF16), jax.ShapeDtypeStruct((nh2, 2, s, LANES), F32)],
        scratch_shapes=[pltpu.VMEM((tq, LANES), F32)],
        args=(qs, kn, vb))


def sb_attn_bwd(qs, kn, vb, rsave, do, name, side=None):
    s, d = qs.shape
    tq, tk = min(SB_TQ, s), min(SB_TK, s)
    nq = s // tq

    def kern(q_ref, k_ref, v_ref, rs_ref, do_ref, dq_ref, dk_ref, dv_ref):
        i = pl.program_id(1)

        @pl.when(i == 0)
        def _():
            dk_ref[...] = jnp.zeros_like(dk_ref)
            dv_ref[...] = jnp.zeros_like(dv_ref)

        row, col, halves, lane_q = _sb_setup(tq, tk)
        tri_gt = (_iota2((tk, tk), 0) > _iota2((tk, tk), 1)).astype(BF16)
        tri_lt = (_iota2((tk, tk), 0) < _iota2((tk, tk), 1)).astype(BF16)
        q = q_ref[...]
        qh = [q * hm for hm in halves]
        dov = do_ref[...].astype(BF16)
        doh = [dov * hm for hm in halves]
        dq_ref[...] = jnp.zeros_like(dq_ref)
        nkb = (i + 1) * (tq // tk)
        top = jnp.maximum(jnp.max(rs_ref[0], axis=0, keepdims=True), jnp.max(rs_ref[1], axis=0, keepdims=True))
        dead = (top <= SB_DEAD) & (_iota2((1, LANES), 1) < nkb)
        kstart = jnp.minimum(jnp.sum(dead.astype(F32)).astype(jnp.int32), nkb)

        def step(kb, ep):
            ep = list(ep)
            ks = pl.multiple_of(kb * tk, tk)
            k = k_ref[pl.ds(ks, tk), :]
            v = v_ref[pl.ds(ks, tk), :]
            mask = col < row + (i * tq - kb * tk)
            at_kb = lane_q == kb
            for hh in range(2):
                lb, l1, suf = _sb_tile(qh[hh], k, mask, tri_gt)
                r = _rowsum(jnp.where(at_kb, rs_ref[hh], 0.0))
                lbm = jnp.where(mask, lb, SB_UNSEEN)
                w = jnp.exp(lbm + suf + r)
                e = _dot_nt(doh[hh], v) * w
                pe = ep[hh] + _dot(e.astype(BF16), tri_lt)
                beta = jnp.exp(lbm)
                dz = (e - beta * (e + pe)).astype(BF16)
                dq_ref[...] += _dot(dz, k * halves[hh])
                dk_ref[pl.ds(ks, tk), :] += _dot_tn(dz, qh[hh])
                dv_ref[pl.ds(ks, tk), :] += _dot_tn(w.astype(BF16), doh[hh])
                ep[hh] = ep[hh] + _rowsum(e)
            return tuple(ep)

        z1 = jnp.zeros((tq, 1), F32)
        lax.fori_loop(kstart, nkb, step, (z1, z1))

    nh2 = d // LANES
    return side_call(
        kern, side,
        name=name,
        grid=(nh2, nq),
        in_specs=[pl.BlockSpec((tq, LANES), lambda h, i: (i, h)),
                  pl.BlockSpec((s, LANES), lambda h, i: (0, h)),
                  pl.BlockSpec((s, LANES), lambda h, i: (0, h)),
                  pl.BlockSpec((None, 2, tq, LANES), lambda h, i: (h, 0, i, 0)),
                  pl.BlockSpec((tq, LANES), lambda h, i: (i, h))],
        out_specs=[pl.BlockSpec((tq, LANES), lambda h, i: (i, h)),
                   pl.BlockSpec((s, LANES), lambda h, i: (0, h)),
                   pl.BlockSpec((s, LANES), lambda h, i: (0, h))],
        out_shape=[jax.ShapeDtypeStruct((s, d), F32)] * 3,
        scratch_shapes=[],
        args=(qs, kn, vb, rsave, do))


def _hooked(plan, tag, call, *args):
    side = plan.side(tag)
    outs, side_outs = call(*args, tag, side)
    if side is not None:
        plan.done(tag, side_outs)
    return outs


def sb_fwd(x, g, wqkv, qg, kg, wo, bd, tag, plan):
    h = rms_fwd(x, g, f"sb_rms_{tag}")
    qkv = mm(h, wqkv, b_chunks=True, out_dtype=BF16, name=f"sb_qkv_{tag}")
    qs, kn, vb = qknorm_fwd(qkv, qg, kg, bd, f"sb_qknorm_{tag}")
    o, rsave = _hooked(plan, f"sb_attn_{tag}", sb_attn_fwd, qs, kn, vb)
    xn = mm(o, wo(), add=x, name=f"sb_out_{tag}")
    return xn, (x, h, qkv, qs, kn, vb, rsave, o)


def sb_bwd(dxn, saved, g, wqkv, qg, kg, wo, bd, tag, plan):
    x, h, qkv, qs, kn, vb, rsave, o = saved
    dxn, dxb = dxn
    do = mm(dxb, wo, tb=True, name=f"sb_do_{tag}")
    dwo = mm(o, dxb, ta=True, out_dtype=BF16, name=f"sb_dwo_{tag}")
    dqs, dkn, dv = _hooked(plan, f"sb_dattn_{tag}", sb_attn_bwd, qs, kn, vb, rsave, do)
    dqkv, dqg, dkg = qknorm_bwd(qkv, dqs, dkn, dv, qg, kg, bd, f"sb_dqknorm_{tag}")
    dh = mm(dqkv, wqkv, tb=True, b_chunks=True, name=f"sb_dh_{tag}")
    dwqkv = mm(h, dqkv, ta=True, out_dtype=BF16, out_chunks=True, name=f"sb_dwqkv_{tag}")
    dx, dg = rms_bwd(x, g, dh, dxn, f"sb_drms_{tag}")
    nh = dqg.shape[1] // HEAD
    return dx, dg, dwqkv, dqg.reshape(nh, HEAD).sum(0), dkg.reshape(nh, HEAD).sum(0), dwo


def _gelu(x):
    return 0.5 * x * (1.0 + lax.erf(x * (1.0 / math.sqrt(2.0))))


def _gelu_grad(x):
    return 0.5 * (1.0 + lax.erf(x * (1.0 / math.sqrt(2.0)))) + x * jnp.exp(-0.5 * x * x) * (1.0 / math.sqrt(2.0 * math.pi))


def gm_act_fwd(pre, vg, name):
    half = pre.shape[1] // 2

    def fn(p, vgv):
        p = p.astype(F32)
        u = _gelu(p[:, :half])
        v = _gelu(p[:, half:])
        r = lax.rsqrt(jnp.mean(v * v, axis=1, keepdims=True) + EPS)
        return u, v * r * vgv

    return rowwise(fn, [(pre, "row"), (vg, "full")], [(half, F32), (half, BF16)], tr=256, name=name)


def gm_act_bwd(pre, du, dvn, vg, name):
    half = pre.shape[1] // 2

    def fn(p, duv, dvnv, vgv):
        p = p.astype(F32)
        pu, pv = p[:, :half], p[:, half:]
        v = _gelu(pv)
        r = lax.rsqrt(jnp.mean(v * v, axis=1, keepdims=True) + EPS)
        vh = v * r
        dyg = dvnv * vgv
        dv = r * (dyg - vh * jnp.mean(dyg * vh, axis=1, keepdims=True))
        dpre = jnp.concatenate([duv * _gelu_grad(pu), dv * _gelu_grad(pv)], axis=1)
        return dpre, _colsum(dvnv * vh), _colsum(dpre)

    return rowwise(fn, [(pre, "row"), (du, "row"), (dvn, "row"), (vg, "full")],
                   [(2 * half, BF16)], [(1, half), (1, 2 * half)], tr=256, name=name)


def gm_spatial_fwd(u, vn, wc, bst, name):
    s, c = u.shape
    t = CHUNK
    ng = c // LANES

    def kern(u_ref, v_ref, w_ref, b_ref, o_ref):
        for g in range(ng):
            sl = slice(g * LANES, (g + 1) * LANES)
            mixed = _dot(w_ref[g], v_ref[:, sl]) + b_ref[:, sl]
            o_ref[:, sl] = (u_ref[:, sl] * mixed).astype(BF16)

    return pl.pallas_call(
        kern,
        name=name,
        grid=(s // t,),
        in_specs=[pl.BlockSpec((t, c), lambda i: (i, 0)), pl.BlockSpec((t, c), lambda i: (i, 0)),
                  pl.BlockSpec(wc.shape, lambda i: (0, 0, 0)), pl.BlockSpec(bst.shape, lambda i: (0, 0))],
        out_specs=pl.BlockSpec((t, c), lambda i: (i, 0)),
        out_shape=jax.ShapeDtypeStruct((s, c), BF16),
        compiler_params=_params(("parallel",)),
    )(u, vn, wc, bst)


def gm_spatial_bwd(dgate, u, vn, wc, bst, name):
    s, c = u.shape
    t = CHUNK
    ng = c // LANES

    def kern(dg_ref, u_ref, v_ref, w_ref, b_ref, du_ref, dv_ref, dw_ref, db_ref):
        i = pl.program_id(0)

        @pl.when(i == 0)
        def _():
            dw_ref[...] = jnp.zeros_like(dw_ref)
            db_ref[...] = jnp.zeros_like(db_ref)

        for g in range(ng):
            sl = slice(g * LANES, (g + 1) * LANES)
            vg = v_ref[:, sl]
            dgv = dg_ref[:, sl]
            mixed = _dot(w_ref[g], vg) + b_ref[:, sl]
            du_ref[:, sl] = dgv * mixed
            dmix = dgv * u_ref[:, sl]
            dmb = dmix.astype(BF16)
            dv_ref[:, sl] = _dot_tn(w_ref[g], dmb)
            dw_ref[g] += _dot_nt(dmb, vg)
            db_ref[:, sl] += dmix

    return pl.pallas_call(
        kern,
        name=name,
        grid=(s // t,),
        in_specs=[pl.BlockSpec((t, c), lambda i: (i, 0))] * 3 +
                 [pl.BlockSpec(wc.shape, lambda i: (0, 0, 0)), pl.BlockSpec(bst.shape, lambda i: (0, 0))],
        out_specs=[pl.BlockSpec((t, c), lambda i: (i, 0)), pl.BlockSpec((t, c), lambda i: (i, 0)),
                   pl.BlockSpec(wc.shape, lambda i: (0, 0, 0)), pl.BlockSpec(bst.shape, lambda i: (0, 0))],
        out_shape=[jax.ShapeDtypeStruct((s, c), F32), jax.ShapeDtypeStruct((s, c), F32),
                   jax.ShapeDtypeStruct(wc.shape, F32), jax.ShapeDtypeStruct(bst.shape, F32)],
        compiler_params=_params(("arbitrary",)),
    )(dgate, u, vn, wc, bst)


def gm_fwd(x, g, w_in, b_in, vg, wc, bst, w_out, tag):
    h = rms_fwd(x, g, f"gm_rms_{tag}")
    pre = mm(h, w_in, bias=b_in, b_chunks=True, out_dtype=BF16, name=f"gm_in_{tag}")
    u, vn = gm_act_fwd(pre, vg, f"gm_act_{tag}")
    gate = gm_spatial_fwd(u, vn, wc, bst, f"gm_spatial_{tag}")
    xn = mm(gate, w_out, add=x, name=f"gm_out_{tag}")
    return xn, (x, h, pre, u, vn, gate)


def gm_bwd(dxn, saved, g, w_in, vg, wc, bst, w_out, tag):
    x, h, pre, u, vn, gate = saved
    dxn, dxb = dxn
    dgate = mm(dxb, w_out, tb=True, name=f"gm_dgate_{tag}")
    dwout = mm(gate, dxb, ta=True, out_dtype=BF16, name=f"gm_dwout_{tag}")
    du, dvn, dws, dbst = gm_spatial_bwd(dgate, u, vn, wc, bst, f"gm_dspatial_{tag}")
    dpre, dvg, dbin = gm_act_bwd(pre, du, dvn, vg, f"gm_dact_{tag}")
    dh = mm(dpre, w_in, tb=True, b_chunks=True, name=f"gm_dh_{tag}")
    dwin = mm(h, dpre, ta=True, out_dtype=BF16, out_chunks=True, name=f"gm_dwin_{tag}")
    dx, dg = rms_bwd(x, g, dh, dxn, f"gm_drms_{tag}")
    ng = wc.shape[0]
    dws = jnp.where(jnp.tril(jnp.ones((CHUNK, CHUNK), bool)), dws, 0.0)
    dbs = dbst.reshape(CHUNK, ng, LANES).sum(-1).T
    return dx, dg, dwin, dbin, dvg, dws, dbs, dwout


def _conv_taps(xv, prev):
    cat = jnp.concatenate([prev, xv], axis=0)
    return [pltpu.roll(cat, sh, 0)[SUBLANES:] for sh in (3, 2, 1)] + [xv]


def conv_fwd(xbc, ws, b, d_inner, name):
    c = xbc.shape[1]
    nst = (c - d_inner) // 2

    def fn(xv, prev, w0, w1, w2, w3, bv):
        taps = _conv_taps(xv, prev)
        pre = bv + w0 * taps[0] + w1 * taps[1] + w2 * taps[2] + w3 * taps[3]
        out = pre * _sigmoid(pre)
        return out[:, :d_inner], out[:, d_inner:d_inner + nst], out[:, d_inner + nst:]

    return rowwise(fn, [(xbc, "row"), (xbc, "prev")] + [(w, "full") for w in ws] + [(b, "full")],
                   [(d_inner, F32), (nst, F32), (nst, F32)], tr=256, name=name)


def conv_bwd_pre(xbc, ws, b, dxs_a, dxs_b, db_m, dc_m, name):
    c = xbc.shape[1]

    def fn(xv, prev, w0, w1, w2, w3, bv, da, db2, dbm, dcm):
        taps = _conv_taps(xv, prev)
        pre = bv + w0 * taps[0] + w1 * taps[1] + w2 * taps[2] + w3 * taps[3]
        sg = _sigmoid(pre)
        dout = jnp.concatenate([da + db2, dbm, dcm], axis=1)
        dpre = dout * sg * (1.0 + pre * (1.0 - sg))
        return (dpre,) + tuple(_colsum(dpre * tp) for tp in taps) + (_colsum(dpre),)

    return rowwise(fn, [(xbc, "row"), (xbc, "prev")] + [(w, "full") for w in ws] +
                   [(b, "full"), (dxs_a, "row"), (dxs_b, "row"), (db_m, "row"), (dc_m, "row")],
                   [(c, F32)], [(1, c)] * 5, tr=256, name=name)


def conv_bwd_in(dpre, ws, name):
    c = dpre.shape[1]

    def fn(dv, nxt, w0, w1, w2, w3):
        cat = jnp.concatenate([dv, nxt], axis=0)
        n = cat.shape[0]
        up = [pltpu.roll(cat, n - sh, 0)[:dv.shape[0]] for sh in (1, 2, 3)]
        return (w3 * dv + w2 * up[0] + w1 * up[1] + w0 * up[2],)

    return rowwise(fn, [(dpre, "row"), (dpre, "next")] + [(w, "full") for w in ws], [(c, BF16)], tr=256, name=name)[0]


def ssd_pre(dtr, bias, alog, name):
    def fn(d, bv, al, tri):
        dt = _softplus(d + bv)
        a = dt * (-jnp.exp(al))
        return dt, _dot_x3_left(tri, a)

    tri = jnp.tril(jnp.ones((CHUNK, CHUNK), BF16))
    return rowwise(fn, [(dtr, "row"), (bias, "full"), (alog, "full"), (tri, "full")],
                   [(LANES, F32), (LANES, F32)], tr=CHUNK, name=name)


def _ssd_layouts(v, ngroups, hpg):
    s = v.shape[0]
    col = v[:, :ngroups * hpg].T.reshape(ngroups, hpg, s, 1)
    return jnp.broadcast_to(col, (ngroups, hpg, s, LANES))


def _ssd_rowform(acum, ngroups, hpg):
    s = acum.shape[0]
    nc = s // CHUNK
    a = acum[:, :ngroups * hpg].reshape(nc, CHUNK, ngroups, hpg).transpose(2, 0, 3, 1)
    last = jnp.broadcast_to(a[..., CHUNK - 1:], a.shape)
    return jnp.concatenate([a, last], axis=2)


def ssd_chunk_fwd(xs, bm, cm, col_a, col_dt, rowf, name, side=None):
    s, d_inner = xs.shape
    ln = CHUNK
    nc = s // ln
    nsub = _pick(nc, (SSD_SUB, 2, 1))
    rows = nsub * ln
    ng, hpg = col_a.shape[0], col_a.shape[1]
    gw = d_inner // ng
    assert gw == hpg * HEAD and gw % LANES == 0 and bm.shape[1] == ng * LANES

    def kern(x_ref, b_ref, c_ref, ca_ref, cd_ref, rf_ref, y_ref, hp_ref, h_scr):
        @pl.when(pl.program_id(1) == 0)
        def _():
            h_scr[...] = jnp.zeros_like(h_scr)

        causal = _iota2((ln, ln), 0) >= _iota2((ln, ln), 1)
        lane = _iota2((1, LANES), 1)
        for sc in range(nsub):
            rs = slice(sc * ln, (sc + 1) * ln)
            bb = b_ref[rs, :].astype(BF16)
            cbf = c_ref[rs, :].astype(BF16)
            cb = _dot_nt(cbf, bb)
            ys = [jnp.zeros((ln, LANES), F32) for _ in range(gw // LANES)]
            for r in range(hpg):
                j, hf = divmod(r, LANES // HEAD)
                mh = ((lane >= HEAD * hf) & (lane < HEAD * (hf + 1))).astype(F32)
                ac = ca_ref[r, rs, :]
                ar = rf_ref[sc, pl.ds(r, 1), :]
                aend = rf_ref[sc, pl.ds(4 + r, 1), :]
                dm = jnp.exp(jnp.minimum(ac - ar, 0.0))
                m = jnp.where(causal, cb * dm, 0.0).astype(BF16)
                xdt = x_ref[rs, j * LANES:(j + 1) * LANES] * cd_ref[r, rs, :] * mh
                h = h_scr[r]
                hp_ref[sc, r] = h
                ys[j] = ys[j] + _dot(m, xdt.astype(BF16)) + _dot_nt(cbf, h.astype(BF16)) * jnp.exp(ac)
                dte = jnp.exp(aend - ac)
                h_scr[r] = jnp.exp(aend) * h + _dot_tn((xdt * dte).astype(BF16), bb)
            for j in range(gw // LANES):
                y_ref[rs, j * LANES:(j + 1) * LANES] = ys[j]

    colspec = pl.BlockSpec((None, hpg, rows, LANES), lambda g, c: (g, 0, c, 0))
    return side_call(
        kern, side,
        name=name,
        grid=(ng, nc // nsub),
        in_specs=[pl.BlockSpec((rows, gw), lambda g, c: (c, g)),
                  pl.BlockSpec((rows, LANES), lambda g, c: (c, g)),
                  pl.BlockSpec((rows, LANES), lambda g, c: (c, g)),
                  colspec, colspec,
                  pl.BlockSpec((None, nsub, 8, LANES), lambda g, c: (g, c, 0, 0))],
        out_specs=[pl.BlockSpec((rows, gw), lambda g, c: (c, g)),
                   pl.BlockSpec((None, nsub, hpg, LANES, LANES), lambda g, c: (g, c, 0, 0, 0))],
        out_shape=[jax.ShapeDtypeStruct((s, d_inner), F32),
                   jax.ShapeDtypeStruct((ng, nc, hpg, LANES, LANES), F32)],
        scratch_shapes=[pltpu.VMEM((hpg, LANES, LANES), F32)],
        args=(xs, bm, cm, col_a, col_dt, rowf))


def ssd_chunk_bwd(xs, bm, cm, col_a, col_dt, rowf, hprev, dy, name, side=None):
    s, d_inner = xs.shape
    ln = CHUNK
    nc = s // ln
    nsub = _pick(nc, (SSD_SUB, 2, 1))
    rows = nsub * ln
    ng, hpg = col_a.shape[0], col_a.shape[1]
    gw = d_inner // ng

    def kern(x_ref, b_ref, c_ref, ca_ref, cd_ref, rf_ref, hp_ref, dy_ref,
             dx_ref, db_ref, dc_ref, ddt_ref, da_ref, dh_scr):
        @pl.when(pl.program_id(1) == 0)
        def _():
            dh_scr[...] = jnp.zeros_like(dh_scr)

        row, col = _iota2((ln, ln), 0), _iota2((ln, ln), 1)
        causal = row >= col
        tri_ge = (col >= row).astype(BF16)
        ones = jnp.ones((ln, LANES), BF16)
        lane = _iota2((1, LANES), 1)
        last_row = (_iota2((ln, 1), 0) == ln - 1).astype(F32)
        for sc in reversed(range(nsub)):
            rs = slice(sc * ln, (sc + 1) * ln)
            bb = b_ref[rs, :].astype(BF16)
            cbf = c_ref[rs, :].astype(BF16)
            cb = _dot_nt(cbf, bb)
            dcb = jnp.zeros((ln, ln), F32)
            d_b = jnp.zeros((ln, LANES), F32)
            d_c = jnp.zeros((ln, LANES), F32)
            dxs = [jnp.zeros((ln, LANES), F32) for _ in range(gw // LANES)]
            for r in range(hpg):
                j, hf = divmod(r, LANES // HEAD)
                mh = ((lane >= HEAD * hf) & (lane < HEAD * (hf + 1))).astype(F32)
                ac = ca_ref[r, rs, :]
                dt = cd_ref[r, rs, :]
                ar = rf_ref[sc, pl.ds(r, 1), :]
                aend = rf_ref[sc, pl.ds(4 + r, 1), :]
                dm = jnp.where(causal, jnp.exp(jnp.minimum(ac - ar, 0.0)), 0.0)
                m = cb * dm
                mb = m.astype(BF16)
                xp = x_ref[rs, j * LANES:(j + 1) * LANES]
                xdt = xp * dt * mh
                xdtb = xdt.astype(BF16)
                dyp = dy_ref[rs, j * LANES:(j + 1) * LANES] * mh
                dypb = dyp.astype(BF16)
                h = hp_ref[sc, r]
                hb = h.astype(BF16)
                dh = dh_scr[r]
                dhb = dh.astype(BF16)
                e_in = jnp.exp(ac)
                dte = jnp.exp(aend - ac)
                eend = jnp.exp(aend)
                d_m = _dot_nt(dypb, xdtb)
                dcb = dcb + d_m * dm
                gm = d_m * m
                yoff_pre = _dot_nt(cbf, hb)
                bdh = _dot_nt(bb, dhb)
                dxdt = _dot_tn(mb, dypb) + bdh * dte
                t1 = _rowsum(xdt * bdh) * dte
                gh, gl = _split2(gm)
                dacum = (_rowsum(gm) - (_dot_tn(gh, ones) + _dot_tn(gl, ones))
                         + _rowsum(dyp * yoff_pre) * e_in - t1)
                end_term = _colsum(t1) + eend * jnp.sum(_colsum(dh * h), axis=1, keepdims=True)
                dacum = dacum + last_row * end_term
                da_ref[r, rs, :] = _dot_x3_left(tri_ge, dacum)
                ddt_ref[r, rs, :] = jnp.broadcast_to(_rowsum(dxdt * xp), (ln, LANES))
                dxs[j] = dxs[j] + dxdt * dt
                d_b = d_b + _dot((xdt * dte).astype(BF16), dhb)
                dye = (dyp * e_in).astype(BF16)
                d_c = d_c + _dot(dye, hb)
                dh_scr[r] = eend * dh + _dot_tn(dye, cbf)
            dcbb = dcb.astype(BF16)
            dc_ref[rs, :] = d_c + _dot(dcbb, bb)
            db_ref[rs, :] = d_b + _dot_tn(dcbb, cbf)
            for j in range(gw // LANES):
                dx_ref[rs, j * LANES:(j + 1) * LANES] = dxs[j]

    rev = nc // nsub - 1
    colspec = pl.BlockSpec((None, hpg, rows, LANES), lambda g, c: (g, 0, rev - c, 0))
    return side_call(
        kern, side,
        name=name,
        grid=(ng, nc // nsub),
        in_specs=[pl.BlockSpec((rows, gw), lambda g, c: (rev - c, g)),
                  pl.BlockSpec((rows, LANES), lambda g, c: (rev - c, g)),
                  pl.BlockSpec((rows, LANES), lambda g, c: (rev - c, g)),
                  colspec, colspec,
                  pl.BlockSpec((None, nsub, 8, LANES), lambda g, c: (g, rev - c, 0, 0)),
                  pl.BlockSpec((None, nsub, hpg, LANES, LANES), lambda g, c: (g, rev - c, 0, 0, 0)),
                  pl.BlockSpec((rows, gw), lambda g, c: (rev - c, g))],
        out_specs=[pl.BlockSpec((rows, gw), lambda g, c: (rev - c, g)),
                   pl.BlockSpec((rows, LANES), lambda g, c: (rev - c, g)),
                   pl.BlockSpec((rows, LANES), lambda g, c: (rev - c, g)),
                   colspec, colspec],
        out_shape=[jax.ShapeDtypeStruct((s, d_inner), F32),
                   jax.ShapeDtypeStruct(bm.shape, F32), jax.ShapeDtypeStruct(cm.shape, F32),
                   jax.ShapeDtypeStruct(col_a.shape, F32), jax.ShapeDtypeStruct(col_a.shape, F32)],
        scratch_shapes=[pltpu.VMEM((hpg, LANES, LANES), F32)],
        args=(xs, bm, cm, col_a, col_dt, rowf, hprev, dy))


def gnorm_fwd(y, xs, z, dexp, gain, ngroups, name):
    c = y.shape[1]
    gw = c // ngroups

    def fn(yv, xv, zv, dv, gv):
        yg = (yv + xv * dv) * (zv * _sigmoid(zv))
        outs = []
        for k in range(ngroups):
            t = yg[:, k * gw:(k + 1) * gw]
            outs.append(t * lax.rsqrt(jnp.mean(t * t, axis=1, keepdims=True) + EPS))
        return (jnp.concatenate(outs, axis=1) * gv,)

    return rowwise(fn, [(y, "row"), (xs, "row"), (z, "row"), (dexp, "full"), (gain, "full")], [(c, BF16)], tr=256, name=name)[0]


def gnorm_bwd(dn, y, xs, z, dexp, gain, ngroups, name):
    c = y.shape[1]
    gw = c // ngroups

    def fn(dnv, yv, xv, zv, dv, gv):
        yd = yv + xv * dv
        sg = _sigmoid(zv)
        sz = zv * sg
        yg = yd * sz
        dng = dnv * gv
        dyg, yh = [], []
        for k in range(ngroups):
            sl = slice(k * gw, (k + 1) * gw)
            t = yg[:, sl]
            r = lax.rsqrt(jnp.mean(t * t, axis=1, keepdims=True) + EPS)
            th = t * r
            dyg.append(r * (dng[:, sl] - th * jnp.mean(dng[:, sl] * th, axis=1, keepdims=True)))
            yh.append(th)
        dyg = jnp.concatenate(dyg, axis=1)
        yh = jnp.concatenate(yh, axis=1)
        dyd = dyg * sz
        dz = dyg * yd * (sg * (1.0 + zv * (1.0 - sg)))
        return dyd, dyd * dv, dz, _colsum(dyd * xv), _colsum(dnv * yh)

    return rowwise(fn, [(dn, "row"), (y, "row"), (xs, "row"), (z, "row"), (dexp, "full"), (gain, "full")],
                   [(c, F32), (c, F32), (c, BF16)], [(1, c), (1, c)], tr=256, name=name)


def ssd_post(ddt, da, dt, dtr, bias, alog, name):
    def fn(ddtv, dav, dtv, dtrv, bv, al):
        a_neg = -jnp.exp(al)
        ddtr = (ddtv + dav * a_neg) * _sigmoid(dtrv + bv)
        return ddtr, _colsum(ddtr), _colsum(dav * dtv) * a_neg

    return rowwise(fn, [(ddt, "row"), (da, "row"), (dt, "row"), (dtr, "row"), (bias, "full"), (alog, "full")],
                   [(LANES, BF16)], [(1, LANES), (1, LANES)], tr=512, name=name)


def _from_colform(v, s):
    ng, hpg = v.shape[0], v.shape[1]
    flat = v[..., 0].reshape(ng * hpg, s).T
    return jnp.pad(flat, ((0, 0), (0, LANES - ng * hpg)))


def ssm_fwd(x, g, p, tag, plan):
    ng, hpg, d_inner = p["ng"], p["hpg"], p["d_inner"]
    h = rms_fwd(x, g, f"ssm_rms_{tag}")
    z = mm(h, p["w_z"], name=f"ssm_inz_{tag}")
    xbc = mm(h, p["w_xbc"], name=f"ssm_inx_{tag}")
    dtr = mm(h, p["w_dt"], name=f"ssm_indt_{tag}")
    xs, bm, cm = conv_fwd(xbc, p["conv_w"], p["conv_b"], d_inner, f"ssm_conv_{tag}")
    dt, acum = ssd_pre(dtr, p["dt_bias"], p["a_log"], f"ssm_pre_{tag}")
    col_a, col_dt = _ssd_layouts(acum, ng, hpg), _ssd_layouts(dt, ng, hpg)
    rowf = _ssd_rowform(acum, ng, hpg)
    y, hprev = _hooked(plan, f"ssm_scan_{tag}", ssd_chunk_fwd, xs, bm, cm, col_a, col_dt, rowf)
    n = gnorm_fwd(y, xs, z, p["d_exp"], p["norm_gain"], ng, f"ssm_gnorm_{tag}")
    xn = mm(n, p["w_out"], add=x, name=f"ssm_out_{tag}")
    return xn, (x, h, z, xbc, dtr, xs, bm, cm, dt, col_a, col_dt, rowf, y, hprev, n)


def ssm_bwd(dxn, saved, g, p, tag, plan):
    x, h, z, xbc, dtr, xs, bm, cm, dt, col_a, col_dt, rowf, y, hprev, n = saved
    ng, hpg, d_inner = p["ng"], p["hpg"], p["d_inner"]
    s = x.shape[0]
    dxn, dxb = dxn
    dn = mm(dxb, p["w_out"], tb=True, name=f"ssm_dn_{tag}")
    dwout = mm(n, dxb, ta=True, out_dtype=BF16, name=f"ssm_dwout_{tag}")
    dy, dxs_skip, dz, dd_lane, dgain = gnorm_bwd(dn, y, xs, z, p["d_exp"], p["norm_gain"], ng, f"ssm_dgnorm_{tag}")
    dxs, dbm, dcm, ddt_c, da_c = _hooked(plan, f"ssm_dscan_{tag}", ssd_chunk_bwd, xs, bm, cm, col_a, col_dt, rowf, hprev, dy)
    ddtr, dbias, dalog = ssd_post(_from_colform(ddt_c, s), _from_colform(da_c, s), dt, dtr,
                                  p["dt_bias"], p["a_log"], f"ssm_post_{tag}")
    res = conv_bwd_pre(xbc, p["conv_w"], p["conv_b"], dxs, dxs_skip, dbm, dcm, f"ssm_dconv_{tag}")
    dpre, dconv_w, dconv_b = res[0], jnp.concatenate(res[1:5], axis=0), res[5]
    dxbc = conv_bwd_in(dpre, p["conv_w"], f"ssm_dconvin_{tag}")
    dh = mm(dz, p["w_z"], tb=True, name=f"ssm_dhz_{tag}")
    dh = mm(dxbc, p["w_xbc"], tb=True, add=dh, name=f"ssm_dhx_{tag}")
    dh = mm(ddtr, p["w_dt"], tb=True, add=dh, name=f"ssm_dhdt_{tag}")
    dwz = mm(h, dz, ta=True, out_dtype=BF16, name=f"ssm_dwz_{tag}")
    dwxbc = mm(h, dxbc, ta=True, out_dtype=BF16, name=f"ssm_dwxbc_{tag}")
    dwdt = mm(h, ddtr, ta=True, out_dtype=BF16, name=f"ssm_dwdt_{tag}")
    dx, dg = rms_bwd(x, g, dh, dxn, f"ssm_drms_{tag}")
    nh = ng * hpg
    dwin = jnp.concatenate([dwz, dwxbc, dwdt[:, :nh]], axis=1)
    dd = dd_lane.reshape(nh, HEAD).sum(-1)
    return dx, dg, dict(w_in=dwin, conv_w=dconv_w, conv_b=dconv_b, dt_bias=dbias[0, :nh], a_log=dalog[0, :nh],
                        d=dd, norm_gain=dgain, w_out=dwout)


def local_step(x, target, w, plan):
    d = x.shape[1]
    depth = w["mix_norm"].shape[0]
    bd = _head_blockdiag(LANES)
    tril = jnp.tril(jnp.ones((CHUNK, CHUNK), bool))
    ssm_heads = w["ssm_dt_bias"].shape[1]
    d_inner = w["ssm_norm_gain"].shape[1]
    ng = w["ssm_norm_gain"].shape[1] // 256
    nstate = CHUNK

    def pad_lanes(v):
        return jnp.pad(v, ((0, 0), (0, LANES - v.shape[1])))

    def ssm_params(j):
        w_in = w["ssm_w_in"][j]
        cw = w["ssm_conv_w"][j]
        return dict(ng=ng, hpg=ssm_heads // ng, d_inner=d_inner,
                    w_z=w_in[:, :d_inner], w_xbc=w_in[:, d_inner:d_inner + d_inner + 2 * ng * nstate],
                    w_dt=pad_lanes(w_in[:, 2 * d_inner + 2 * ng * nstate:]),
                    conv_w=[cw[k:k + 1] for k in range(cw.shape[0])], conv_b=w["ssm_conv_b"][j:j + 1],
                    dt_bias=pad_lanes(w["ssm_dt_bias"][j:j + 1]), a_log=pad_lanes(w["ssm_a_log"][j:j + 1]),
                    d_exp=jnp.repeat(w["ssm_d"][j], HEAD)[None, :], norm_gain=w["ssm_norm_gain"][j:j + 1],
                    w_out=w["ssm_w_out"][j])

    def gm_params(j):
        wc = jnp.where(tril, w["gm_w_s"][j], 0.0).astype(BF16)
        bst = jnp.repeat(w["gm_b_s"][j].T, LANES, axis=1)
        return wc, bst

    def sb_gains(j):
        nh = d // HEAD
        return jnp.tile(w["sb_q_gain"][j], nh)[None, :], jnp.tile(w["sb_k_gain"][j], nh)[None, :]

    saved = []
    cur = x
    for i in range(depth):
        kind, j = i % 3, i // 3
        gmix = w["mix_norm"][i:i + 1]
        if kind == 0:
            qg, kg = sb_gains(j)
            cur, sv = sb_fwd(cur, gmix, w["sb_w_qkv"][j], qg, kg, lambda j=j: w["sb_w_o"][j], bd, f"{i}", plan)
        elif kind == 1:
            wc, bst = gm_params(j)
            cur, sv = gm_fwd(cur, gmix, w["gm_w_in"][j], w["gm_b_in"][j:j + 1], w["gm_v_gain"][j:j + 1], wc, bst,
                             w["gm_w_out"][j], f"{i}")
        else:
            cur, sv = ssm_fwd(cur, gmix, ssm_params(j), f"{i}", plan)
        cur, sv2 = ffn_fwd(cur, w["ffn_norm"][i:i + 1], w["ffn_w_gu"][i], w["ffn_w_down"][i], f"{i}", plan)
        saved.append((sv, sv2))

    loss, dcur = loss_and_grad(cur, target, "loss")

    grads = {k: [None] * len(v) for k, v in w.items()}
    for i in reversed(range(depth)):
        kind, j = i % 3, i // 3
        sv, sv2 = saved[i]
        gmix = w["mix_norm"][i:i + 1]
        dcur, dgf, dwgu, dwdown = ffn_bwd(dcur, sv2, w["ffn_norm"][i:i + 1], w["ffn_w_gu"][i], w["ffn_w_down"][i], f"{i}")
        grads["ffn_norm"][i], grads["ffn_w_gu"][i], grads["ffn_w_down"][i] = dgf[0], dwgu, dwdown
        plan.grads_ready({("ffn_w_gu", i): dwgu, ("ffn_w_down", i): dwdown})
        if kind == 0:
            qg, kg = sb_gains(j)
            dcur, dg, dwqkv, dqg, dkg, dwo = sb_bwd(dcur, sv, gmix, w["sb_w_qkv"][j], qg, kg, w["sb_w_o"][j], bd, f"{i}", plan)
            grads["sb_w_qkv"][j], grads["sb_q_gain"][j], grads["sb_k_gain"][j], grads["sb_w_o"][j] = dwqkv, dqg, dkg, dwo
        elif kind == 1:
            wc, bst = gm_params(j)
            dcur, dg, dwin, dbin, dvg, dws, dbs, dwout = gm_bwd(dcur, sv, gmix, w["gm_w_in"][j], w["gm_v_gain"][j:j + 1],
                                                                 wc, bst, w["gm_w_out"][j], f"{i}")
            grads["gm_w_in"][j], grads["gm_b_in"][j], grads["gm_v_gain"][j] = dwin, dbin[0], dvg[0]
            grads["gm_w_s"][j], grads["gm_b_s"][j], grads["gm_w_out"][j] = dws, dbs, dwout
        else:
            dcur, dg, gs = ssm_bwd(dcur, sv, gmix, ssm_params(j), f"{i}", plan)
            grads["ssm_w_in"][j], grads["ssm_conv_w"][j], grads["ssm_conv_b"][j] = gs["w_in"], gs["conv_w"], gs["conv_b"][0]
            grads["ssm_dt_bias"][j], grads["ssm_a_log"][j], grads["ssm_d"][j] = gs["dt_bias"], gs["a_log"], gs["d"]
            grads["ssm_norm_gain"][j], grads["ssm_w_out"][j] = gs["norm_gain"][0], gs["w_out"]
        grads["mix_norm"][i] = dg[0]
        mixer = {0: ("sb_w_qkv", "sb_w_o"), 1: ("gm_w_in", "gm_w_out"), 2: ("ssm_w_in", "ssm_w_out")}[kind]
        plan.grads_ready({(n, j): grads[n][j] for n in mixer})
    grads = {k: (v if k in MATRICES else jnp.stack(v)) for k, v in grads.items()}
    return loss, dcur[0], grads


WEIGHTS = ["mix_norm", "ffn_norm", "sb_w_qkv", "sb_q_gain", "sb_k_gain", "sb_w_o", "gm_w_in", "gm_b_in", "gm_v_gain",
           "gm_w_s", "gm_b_s", "gm_w_out", "ssm_w_in", "ssm_conv_w", "ssm_conv_b", "ssm_dt_bias", "ssm_a_log", "ssm_d",
           "ssm_norm_gain", "ssm_w_out", "ffn_w_gu", "ffn_w_down"]
SHARDED = {"sb_w_qkv": 2, "sb_w_o": 1, "gm_w_in": 2, "gm_w_out": 1, "ssm_w_in": 2, "ssm_conv_w": 2, "ssm_conv_b": 1,
           "ssm_norm_gain": 1, "ssm_w_out": 1, "ffn_w_gu": 2, "ffn_w_down": 1}
EXACT = ("ssm_conv_w", "ssm_conv_b", "ssm_norm_gain")
MATRICES = tuple(n for n in SHARDED if n not in EXACT)
COLUMN_BLOCKS = ("sb_w_qkv", "gm_w_in", "ffn_w_gu")
REPLICATED = [n for n in WEIGHTS if n not in SHARDED]
N_CHIPS = 4
N_DEV = 8
PACK_COLS = 1024


def _pack(pieces, dtype, align):
    flat = jnp.concatenate([p.reshape(-1).astype(dtype) for p in pieces])
    rows = -(-flat.shape[0] // (PACK_COLS * align)) * align
    flat = jnp.pad(flat, (0, rows * PACK_COLS - flat.shape[0]))
    return flat.reshape(rows, PACK_COLS)


def _unpack(flat, shapes):
    out, off = [], 0
    for shp in shapes:
        n = math.prod(shp)
        out.append(flat[off:off + n].reshape(shp))
        off += n
    return out


ANY = pl.BlockSpec(memory_space=pl.ANY)


def _pos():
    return lax.axis_index("x"), lax.axis_index("y"), lax.axis_index("c")


def _remote(src, dst, send, recv, k, to):
    return pltpu.make_async_remote_copy(src_ref=src, dst_ref=dst, send_sem=send.at[k], recv_sem=recv.at[k],
                                        device_id=to, device_id_type=MESH_ID)


def _comm_call(body, name, ins, out_shapes, nsem, aliases=None):
    return pl.pallas_call(
        body, name=name, out_shape=out_shapes,
        in_specs=[ANY] * len(ins), out_specs=[ANY] * len(out_shapes),
        scratch_shapes=[pltpu.SemaphoreType.DMA((nsem,)), pltpu.SemaphoreType.DMA((nsem,))],
        input_output_aliases=aliases or {},
    )(*ins)


def stage_shard(w, chip, name):
    rows, cols = w.shape
    tr = _pick(rows, (256, 352, 128))

    def kern(idx_ref, w_ref, o_ref):
        o_ref[...] = w_ref[...].astype(BF16)

    grid_spec = pltpu.PrefetchScalarGridSpec(
        num_scalar_prefetch=1, grid=(rows // tr,),
        in_specs=[pl.BlockSpec((tr, cols), lambda i, idx: (i, 0))],
        out_specs=pl.BlockSpec((None, tr, cols), lambda i, idx: (idx[0], i, 0)))
    return pl.pallas_call(
        kern, name=name, grid_spec=grid_spec,
        out_shape=jax.ShapeDtypeStruct((N_CHIPS, rows, cols), BF16),
        compiler_params=_params(("parallel",)),
    )(jnp.reshape(chip, (1,)).astype(jnp.int32), w)


class Side:
    def __init__(self, arrays, out_shapes, aliases, nsem, start, finish):
        self.arrays, self.out_shapes, self.aliases, self.nsem = list(arrays), list(out_shapes), aliases, nsem
        self.start, self.finish = start, finish


def run_side(side, name):
    n_in, n_out = len(side.arrays), len(side.out_shapes)

    def body(*refs):
        ins, outs = refs[:n_in], refs[n_in:n_in + n_out]
        send, recv = refs[n_in + n_out:]
        side.start(ins, outs, send, recv)
        side.finish(ins, outs, send, recv)

    return _comm_call(body, name, side.arrays, side.out_shapes, side.nsem, aliases=side.aliases)


def side_call(kern, side, *, name, grid, in_specs, out_specs, out_shape, scratch_shapes, args):
    if side is None:
        res = pl.pallas_call(kern, name=name, grid=grid, in_specs=in_specs, out_specs=out_specs, out_shape=out_shape,
                             scratch_shapes=scratch_shapes,
                             compiler_params=_params(("parallel",) + ("arbitrary",) * (len(grid) - 1)))(*args)
        return list(res), []
    n_in, n_out, n_scr = len(in_specs), len(out_specs), len(scratch_shapes)
    s_in, s_out = len(side.arrays), len(side.out_shapes)

    def body(*refs):
        ins, refs = refs[:n_in], refs[n_in:]
        side_ins, refs = refs[:s_in], refs[s_in:]
        outs, refs = refs[:n_out], refs[n_out:]
        side_outs, refs = refs[:s_out], refs[s_out:]
        scr, (send, recv) = refs[:n_scr], refs[n_scr:]
        first, last = None, None
        for axis, size in enumerate(grid):
            at0, at1 = pl.program_id(axis) == 0, pl.program_id(axis) == size - 1
            first = at0 if first is None else first & at0
            last = at1 if last is None else last & at1

        @pl.when(first)
        def _():
            side.start(side_ins, side_outs, send, recv)

        kern(*ins, *outs, *scr)

        @pl.when(last)
        def _():
            side.finish(side_ins, side_outs, send, recv)

    res = pl.pallas_call(
        body, name=name, grid=grid,
        in_specs=list(in_specs) + [ANY] * s_in, out_specs=list(out_specs) + [ANY] * s_out,
        out_shape=list(out_shape) + side.out_shapes,
        scratch_shapes=list(scratch_shapes) + [pltpu.SemaphoreType.DMA((side.nsem,)), pltpu.SemaphoreType.DMA((side.nsem,))],
        input_output_aliases={n_in + a: n_out + b for a, b in side.aliases.items()},
        compiler_params=_params(("arbitrary",) * len(grid)),
    )(*args, *side.arrays)
    return list(res[:n_out]), list(res[n_out:])


def gather_side(staged):
    n = len(staged)

    def plan(o_refs, send, recv):
        x, y, c = _pos()
        chips = [(1 - x, y), (x, 1 - y), (1 - x, 1 - y)]

        def part(u, chip, cc):
            half = staged[u].shape[1] // 2
            return o_refs[u].at[2 * chip[0] + chip[1], pl.ds(cc * half, half), :]

        first = [_remote(part(u, (x, y), c), part(u, (x, y), c), send, recv, 6 * u + j, (*chip, c))
                 for u in range(n) for j, chip in enumerate(chips)]
        landed = [_remote(part(u, chip, c), part(u, chip, c), send, recv, 6 * u + j, (x, y, c))
                  for u in range(n) for j, chip in enumerate(chips)]
        passed = [_remote(part(u, chip, c), part(u, chip, c), send, recv, 6 * u + 3 + j, (x, y, 1 - c))
                  for u in range(n) for j, chip in enumerate(chips)]
        handed = [_remote(part(u, chip, 1 - c), part(u, chip, 1 - c), send, recv, 6 * u + 3 + j, (x, y, c))
                  for u in range(n) for j, chip in enumerate(chips)]
        return first, landed, passed, handed

    def start(ins, outs, send, recv):
        for cp in plan(outs, send, recv)[0]:
            cp.start()

    def finish(ins, outs, send, recv):
        first, landed, passed, handed = plan(outs, send, recv)
        for got, fw in zip(landed, passed):
            got.wait_recv()
            fw.start()
        for got in handed:
            got.wait_recv()
        for cp in first + passed:
            cp.wait_send()

    outs = [jax.ShapeDtypeStruct(s.shape, s.dtype) for s in staged]
    return Side(staged, outs, {u: u for u in range(n)}, 6 * n, start, finish)


def swap_halves(gps, name):
    n = len(gps)

    def body(*refs):
        g_refs, r_refs = refs[:n], refs[n:2 * n]
        send, recv = refs[2 * n:]
        x, y, c = _pos()
        cps = []
        for u in range(n):
            half = gps[u].shape[1] // 2
            cps.append(_remote(g_refs[u].at[:, pl.ds((1 - c) * half, half), :], r_refs[u], send, recv, u, (x, y, 1 - c)))
        for cp in cps:
            cp.start()
        for cp in cps:
            cp.wait()

    outs = [jax.ShapeDtypeStruct((g.shape[0], g.shape[1] // 2, g.shape[2]), g.dtype) for g in gps]
    return _comm_call(body, name, gps, outs, n)


def scatter_side(parts):
    n = len(parts)

    def plan(p_refs, r_refs, send, recv):
        x, y, c = _pos()
        chips = [(1 - x, y), (x, 1 - y), (1 - x, 1 - y)]
        return [_remote(p_refs[u].at[2 * chip[0] + chip[1]], r_refs[u].at[j], send, recv, 3 * u + j, (*chip, c))
                for u in range(n) for j, chip in enumerate(chips)]

    def start(ins, outs, send, recv):
        for cp in plan(ins, outs, send, recv):
            cp.start()

    def finish(ins, outs, send, recv):
        for cp in plan(ins, outs, send, recv):
            cp.wait()

    outs = [jax.ShapeDtypeStruct((N_CHIPS - 1,) + p.shape[1:], p.dtype) for p in parts]
    return Side(parts, outs, {}, 3 * n, start, finish)


def join_halves(bufs):
    n = len(bufs)

    def body(*refs):
        o_refs = refs[n:2 * n]
        send, recv = refs[2 * n:]
        x, y, c = _pos()

        def rows(u, cc):
            half = bufs[u].shape[0] // 2
            return o_refs[u].at[pl.ds(cc * half, half), :]

        cps = [_remote(rows(u, c), rows(u, c), send, recv, u, (x, y, 1 - c)) for u in range(n)]
        for cp in cps:
            cp.start()
        for u in range(n):
            _remote(rows(u, 1 - c), rows(u, 1 - c), send, recv, u, (x, y, c)).wait_recv()
        for cp in cps:
            cp.wait_send()

    outs = [jax.ShapeDtypeStruct(b.shape, b.dtype) for b in bufs]
    return _comm_call(body, "join_halves", bufs, outs, n, aliases={u: u for u in range(n)})


def gather_small(sg, name):
    rows, cols = sg.shape

    def body(s_ref, o_ref, send, recv, lsem):
        x, y, c = _pos()
        me, sibling = (x, y, c), (x, y, 1 - c)
        chips = [(1 - x, y), (x, 1 - y), (1 - x, 1 - y)]

        def blk(px, py, pc):
            return o_ref.at[4 * px + 2 * py + pc]

        mine = pltpu.make_async_copy(s_ref, blk(*me), lsem)
        mine.start()
        first = [_remote(s_ref, blk(*me), send, recv, 0, sibling)]
        first += [_remote(s_ref, blk(*me), send, recv, 1 + j, (*chip, c)) for j, chip in enumerate(chips)]
        for cp in first:
            cp.start()
        passed = [_remote(blk(*chip, c), blk(*chip, c), send, recv, 4 + j, sibling) for j, chip in enumerate(chips)]
        for j, chip in enumerate(chips):
            _remote(blk(*chip, c), blk(*chip, c), send, recv, 1 + j, me).wait_recv()
            passed[j].start()
        _remote(blk(*sibling), blk(*sibling), send, recv, 0, me).wait_recv()
        for j, chip in enumerate(chips):
            _remote(blk(*chip, 1 - c), blk(*chip, 1 - c), send, recv, 4 + j, me).wait_recv()
        for cp in first + passed:
            cp.wait_send()
        mine.wait()

    return pl.pallas_call(
        body, name=name,
        out_shape=jax.ShapeDtypeStruct((N_DEV, rows, cols), sg.dtype),
        in_specs=[ANY], out_specs=ANY,
        scratch_shapes=[pltpu.SemaphoreType.DMA((N_DEV - 1,)), pltpu.SemaphoreType.DMA((N_DEV - 1,)), pltpu.SemaphoreType.DMA],
    )(sg)


def sum_cores(gp, theirs, core, chip, name):
    nch, rows, cols = gp.shape
    half = rows // 2
    tr = _pick(half, (256, 176, 128, 64))
    nb = half // tr

    def kern(idx_ref, g_ref, t_ref, own_ref, all_ref):
        k = pl.program_id(1)
        s = g_ref[...].astype(F32) + t_ref[...].astype(F32)
        all_ref[...] = s.astype(BF16)

        @pl.when(k == idx_ref[1])
        def _():
            own_ref[...] = s

    grid_spec = pltpu.PrefetchScalarGridSpec(
        num_scalar_prefetch=1, grid=(nb, nch),
        in_specs=[pl.BlockSpec((None, tr, cols), lambda i, k, idx: (k, idx[0] * nb + i, 0)),
                  pl.BlockSpec((None, tr, cols), lambda i, k, idx: (k, i, 0))],
        out_specs=[pl.BlockSpec((tr, cols), lambda i, k, idx: (i, 0)),
                   pl.BlockSpec((None, tr, cols), lambda i, k, idx: (k, i, 0))])
    return pl.pallas_call(
        kern, name=name, grid_spec=grid_spec,
        out_shape=[jax.ShapeDtypeStruct((half, cols), F32), jax.ShapeDtypeStruct((nch, half, cols), BF16)],
        compiler_params=_params(("parallel", "arbitrary")),
    )(jnp.stack([core, chip]).astype(jnp.int32), gp, theirs)


def sum_chips(own, others, core, name):
    half, cols = own.shape
    tr = _pick(half, (256, 176, 128, 64))
    nb = half // tr

    def kern(idx_ref, o_ref, a_ref, b_ref, c_ref, out_ref):
        out_ref[...] = ((o_ref[...] + a_ref[...].astype(F32)) + b_ref[...].astype(F32)) + c_ref[...].astype(F32)

    grid_spec = pltpu.PrefetchScalarGridSpec(
        num_scalar_prefetch=1, grid=(nb,),
        in_specs=[pl.BlockSpec((tr, cols), lambda i, idx: (i, 0))] +
                 [pl.BlockSpec((None, tr, cols), lambda i, idx, j=j: (j, i, 0)) for j in range(N_CHIPS - 1)],
        out_specs=pl.BlockSpec((tr, cols), lambda i, idx: (idx[0] * nb + i, 0)))
    return pl.pallas_call(
        kern, name=name, grid_spec=grid_spec,
        out_shape=jax.ShapeDtypeStruct((2 * half, cols), F32),
        compiler_params=_params(("parallel",)),
    )(jnp.reshape(core, (1,)).astype(jnp.int32), own, others, others, others)


def small_update(gath, w, m, v, name):
    def fn(*vs):
        g = vs[0]
        for t in vs[1:N_DEV]:
            g = g + t
        wv, mv, vv = vs[N_DEV:]
        m2 = ADAM_B1 * mv + (1.0 - ADAM_B1) * g
        v2 = ADAM_B2 * vv + (1.0 - ADAM_B2) * (g * g)
        m_hat = m2 / (1.0 - ADAM_B1 ** ADAM_STEP)
        v_hat = v2 / (1.0 - ADAM_B2 ** ADAM_STEP)
        return g, -ADAM_LR * (m_hat / (jnp.sqrt(v_hat) + ADAM_EPS) + ADAM_WD * wv), m2, v2

    c = w.shape[1]
    ins = [(gath[k], "row") for k in range(N_DEV)] + [(w, "row"), (m, "row"), (v, "row")]
    return rowwise(fn, ins, [(c, F32)] * 4, tr=w.shape[0] // 2, name=name)


_MIX = {0: [("sb_w_qkv", 0), ("sb_w_o", 0)], 1: [("gm_w_in", 0), ("gm_w_out", 0)],
        2: [("ssm_w_in", 0), ("ssm_w_out", 0)], 3: [("sb_w_qkv", 1), ("sb_w_o", 1)]}
_FFN = {i: [("ffn_w_gu", i), ("ffn_w_down", i)] for i in range(4)}
GATHER_FIRST = _MIX[0][:1]
GATHER_AT = {"sb_attn_0": _MIX[0][1:] + _FFN[0] + _MIX[1] + _FFN[1],
             "ffn_gu_0": _FFN[2][:1], "ffn_down_0": _FFN[2][1:], "ffn_gu_1": _MIX[2][:1], "ffn_down_1": _MIX[2][1:],
             "ssm_scan_2": _MIX[3] + _FFN[3]}
SCATTER_AT = {"ssm_dscan_2": _FFN[3] + _MIX[3] + _FFN[2], "sb_dattn_0": _MIX[2] + _FFN[1] + _MIX[1] + _FFN[0]}
SCATTER_LAST = _MIX[0]


class _Plan:
    def __init__(self, ins, core, chip):
        self.core, self.chip = core, chip
        self.staged = {(n, l): stage_shard(ins[n][l], chip, f"stage_{n}_{l}")
                       for n in MATRICES for l in range(ins[n].shape[0])}
        self.full = {n: [None] * ins[n].shape[0] for n in MATRICES}
        self.ready = {}
        self.parts = {}
        self.halves = {}
        self.swaps = 0
        self._fill(GATHER_FIRST, run_side(gather_side([self.staged[u] for u in GATHER_FIRST]), "gather_first"))

    def _fill(self, units, gathered):
        for (n, l), g in zip(units, gathered):
            if n in COLUMN_BLOCKS:
                self.full[n][l] = g
            elif n == "ssm_w_in":
                self.full[n][l] = jnp.concatenate([g[k] for k in range(N_CHIPS)], axis=1)
            else:
                self.full[n][l] = g.reshape(-1, g.shape[-1])

    def _prepare(self, units):
        gps = [self.ready[u] for u in units]
        theirs = swap_halves(gps, f"swap_halves_{self.swaps}")
        self.swaps += 1
        for (n, l), g, t in zip(units, gps, theirs):
            self.parts[(n, l)] = sum_cores(g, t, self.core, self.chip, f"sum_cores_{n}_{l}")

    def _reduce(self, units, others):
        for (n, l), other in zip(units, others):
            self.halves[(n, l)] = sum_chips(self.parts[(n, l)][0], other, self.core, f"sum_chips_{n}_{l}")

    def side(self, tag):
        if tag in GATHER_AT:
            return gather_side([self.staged[u] for u in GATHER_AT[tag]])
        if tag in SCATTER_AT:
            self._prepare(SCATTER_AT[tag])
            return scatter_side([self.parts[u][1] for u in SCATTER_AT[tag]])
        return None

    def done(self, tag, results):
        if tag in GATHER_AT:
            self._fill(GATHER_AT[tag], results)
        else:
            self._reduce(SCATTER_AT[tag], results)

    def grads_ready(self, grads):
        for (n, l), g in grads.items():
            if n in COLUMN_BLOCKS:
                self.ready[(n, l)] = g
            elif n == "ssm_w_in":
                self.ready[(n, l)] = jnp.stack(jnp.split(g, N_CHIPS, axis=1))
            else:
                self.ready[(n, l)] = g.reshape(N_CHIPS, -1, g.shape[-1])

    def shard_grads(self):
        self._prepare(SCATTER_LAST)
        self._reduce(SCATTER_LAST, run_side(scatter_side([self.parts[u][1] for u in SCATTER_LAST]), "scatter_last"))
        units = sorted(self.halves)
        return dict(zip(units, join_halves([self.halves[u] for u in units])))


def _step(ins):
    x, target = ins["x"][0], ins["loss_target"][0]
    core = lax.axis_index("c")
    chip = 2 * lax.axis_index("x") + lax.axis_index("y")

    def lane_pad(v):
        return jnp.pad(v, ((0, 0), (0, PACK_COLS - v.shape[1])))

    vec_rows = [ins["ssm_conv_w"][0], ins["ssm_conv_b"], lane_pad(ins["ssm_norm_gain"])]
    blk = jnp.concatenate(vec_rows + [jnp.zeros((SUBLANES - 6, PACK_COLS), F32)], axis=0)
    per_chip = gather_small(blk, "gather_vectors")[0::2]
    ngw = ins["ssm_norm_gain"].shape[1]
    full = {
        "ssm_conv_w": jnp.concatenate([per_chip[k, 0:4] for k in range(N_CHIPS)], axis=1)[None],
        "ssm_conv_b": jnp.concatenate([per_chip[k, 4:5] for k in range(N_CHIPS)], axis=1),
        "ssm_norm_gain": jnp.concatenate([per_chip[k, 5:6, :ngw] for k in range(N_CHIPS)], axis=1),
    }

    plan = _Plan(ins, core, chip)
    full.update(plan.full)
    for n in REPLICATED:
        full[n] = ins[n]

    loss, dx, grads = local_step(x, target, full, plan)
    loss = lax.psum(loss, ALL_AXES)
    gshards = plan.shard_grads()

    small_shapes = [ins[n].shape for n in REPLICATED]
    vec_shapes = [grads[n].shape for n in EXACT]
    vec_pack = _pack([grads[n] for n in EXACT], F32, SUBLANES)
    gath = gather_small(jnp.concatenate([_pack([grads[n] for n in REPLICATED], F32, SUBLANES), vec_pack], axis=0),
                        "gather_small")
    packed = [jnp.concatenate([_pack([ins[pre + n] for n in REPLICATED], F32, SUBLANES), jnp.zeros_like(vec_pack)], axis=0)
              for pre in ("", "m_", "v_")]
    res = small_update(gath, *packed, name="small_update")
    nrep = res[0].shape[0] - vec_pack.shape[0]
    small = [dict(zip(REPLICATED, _unpack(r[:nrep].reshape(-1), small_shapes))) for r in res]
    vec_g = dict(zip(EXACT, _unpack(res[0][nrep:].reshape(-1), vec_shapes)))

    out_g, out_d, out_m, out_v = {}, {}, {}, {}
    for n in REPLICATED:
        out_g[n], out_d[n], out_m[n], out_v[n] = (s[n] for s in small)
    for n in SHARDED:
        shp = ins[n].shape
        if n in EXACT:
            g = lax.dynamic_slice_in_dim(vec_g[n], chip * shp[-1], shp[-1], axis=vec_g[n].ndim - 1)
        else:
            g = jnp.stack([gshards[(n, l)] for l in range(shp[0])])
        two = (math.prod(shp[:-1]), shp[-1])
        d2, m2, v2 = adamw(ins[n].reshape(two), g.reshape(two), ins["m_" + n].reshape(two),
                           ins["v_" + n].reshape(two), f"adamw_{n}")
        out_g[n], out_d[n], out_m[n], out_v[n] = g, d2.reshape(shp), m2.reshape(shp), v2.reshape(shp)
    return (loss, dx[None], *[out_g[n] for n in WEIGHTS], *[out_d[n] for n in WEIGHTS],
            *[out_m[n] for n in WEIGHTS], *[out_v[n] for n in WEIGHTS])


def kernel(x, mix_norm, ffn_norm, sb_w_qkv, sb_q_gain, sb_k_gain, sb_w_o, gm_w_in, gm_b_in, gm_v_gain, gm_w_s, gm_b_s, gm_w_out, ssm_w_in, ssm_conv_w, ssm_conv_b, ssm_dt_bias, ssm_a_log, ssm_d, ssm_norm_gain, ssm_w_out, ffn_w_gu, ffn_w_down, loss_target, m_mix_norm, m_ffn_norm, m_sb_w_qkv, m_sb_q_gain, m_sb_k_gain, m_sb_w_o, m_gm_w_in, m_gm_b_in, m_gm_v_gain, m_gm_w_s, m_gm_b_s, m_gm_w_out, m_ssm_w_in, m_ssm_conv_w, m_ssm_conv_b, m_ssm_dt_bias, m_ssm_a_log, m_ssm_d, m_ssm_norm_gain, m_ssm_w_out, m_ffn_w_gu, m_ffn_w_down, v_mix_norm, v_ffn_norm, v_sb_w_qkv, v_sb_q_gain, v_sb_k_gain, v_sb_w_o, v_gm_w_in, v_gm_b_in, v_gm_v_gain, v_gm_w_s, v_gm_b_s, v_gm_w_out, v_ssm_w_in, v_ssm_conv_w, v_ssm_conv_b, v_ssm_dt_bias, v_ssm_a_log, v_ssm_d, v_ssm_norm_gain, v_ssm_w_out, v_ffn_w_gu, v_ffn_w_down):
    return _step(dict(locals()))
```

```python
import functools
import math

import jax
import jax.numpy as jnp
from jax import lax
from jax.experimental import pallas as pl
from jax.experimental.pallas import tpu as pltpu

F32 = jnp.float32
BF16 = jnp.bfloat16
EPS = 1e-6
LANES = 128
SUBLANES = 8
VMEM_LIMIT = 56 * 1024 * 1024
HEAD = 64
CHUNK = 128
SB_TQ, SB_TK = 256, 256
SSD_SUB = 4
SB_DEAD = -110.0
SB_UNSEEN = -1e30
ADAM_LR, ADAM_B1, ADAM_B2, ADAM_EPS, ADAM_WD, ADAM_STEP = 0.001, 0.9, 0.999, 1e-08, 0.01, 10
MESH_ID = pl.DeviceIdType.MESH
ALL_AXES = ("x", "y", "c")


def _params(sem):
    return pltpu.CompilerParams(dimension_semantics=sem, vmem_limit_bytes=VMEM_LIMIT)


def _pick(n, cands):
    for c in cands:
        if n % c == 0:
            return c
    return n


def _dot(a, b, dims=((1,), (0,))):
    return lax.dot_general(a, b, (dims, ((), ())), preferred_element_type=F32)


def _dot_nt(a, b):
    return _dot(a, b, ((1,), (1,)))


def _dot_tn(a, b):
    return _dot(a, b, ((0,), (0,)))


def _split2(x):
    hi = x.astype(BF16)
    lo = (x - hi.astype(F32)).astype(BF16)
    return hi, lo


def _dot_x2(x, m):
    hi, lo = _split2(x)
    return _dot(hi, m) + _dot(lo, m)


def _dot_x3_left(m, x):
    h1 = x.astype(BF16)
    r1 = x - h1.astype(F32)
    h2 = r1.astype(BF16)
    h3 = (r1 - h2.astype(F32)).astype(BF16)
    return _dot(m, h1) + _dot(m, h2) + _dot(m, h3)


def _sigmoid(x):
    return 1.0 / (1.0 + jnp.exp(-x))


def _softplus(x):
    return jnp.maximum(x, 0.0) + jnp.log(1.0 + jnp.exp(-jnp.abs(x)))


def _colsum(x):
    return jnp.sum(x, axis=0, keepdims=True)


def _rowsum(x):
    return jnp.sum(x, axis=1, keepdims=True)


def _iota2(shape, dim):
    return lax.broadcasted_iota(jnp.int32, shape, dim)


MM_VMEM_BUDGET = 40 * 1024 * 1024
MM_STEP_US = 0.35
MM_HBM_BYTES_PER_US = 3.0e6
MM_VMEM_BYTES_PER_US = 1.5e6
MM_FLOPS_PER_US = 9.0e8
MXU_DIM = 256


def _mm_tiles(m, n, kk, wn, wk, a_bytes, b_bytes, has_add):
    def divisors(total, cands):
        got = [c for c in cands if total % c == 0 and c <= total]
        return got or [total]

    best = None
    for tm in divisors(m, (1024, 512, 256, 128)):
        for tn in divisors(wn, (1024, 768, 1408, 512, 256, 128)):
            for tk in divisors(wk, (4096, 2816, 2048, 1408, 1024, 768, 512, 256, 128)):
                nk = kk // tk
                vmem = 2 * (tm * tk * a_bytes + tk * tn * b_bytes + tm * tn * 4 * (2 if has_add else 1))
                vmem += tm * tn * 4 if nk > 1 else 0
                if vmem > MM_VMEM_BUDGET:
                    continue
                steps = (m // tm) * (n // tn) * nk
                a_reads = 1 if nk == 1 else n // tn
                traffic = m * kk * a_bytes * a_reads + kk * n * b_bytes * (m // tm) + m * n * 4
                fill = min(1.0, tn / MXU_DIM) * min(1.0, tm / MXU_DIM)
                compute = 2.0 * m * n * kk / (MM_FLOPS_PER_US * fill)
                cost = steps * MM_STEP_US + max(compute, traffic / MM_HBM_BYTES_PER_US)
                if nk > 1:
                    cost += steps * tm * tn * 8 / MM_VMEM_BYTES_PER_US
                if best is None or cost < best[0]:
                    best = (cost, tm, tn, tk)
    return best[1:]


def mm(a, b, *, ta=False, tb=False, add=None, bias=None, a_chunks=False, b_chunks=False, out_chunks=False,
       out_dtype=F32, name, side=None):
    wa = None
    if a_chunks:
        m, wa = a.shape[1], a.shape[2]
        kk = a.shape[0] * wa
    elif ta:
        kk, m = a.shape
    else:
        m, kk = a.shape
    nch, wide = 1, None
    if b_chunks:
        nch, rows_b, wide = b.shape
        kb, n = (rows_b, nch * wide) if not tb else (nch * wide, rows_b)
    elif tb:
        n, kb = b.shape
    else:
        kb, n = b.shape
    wide_o = n // N_CHIPS if out_chunks else None
    assert kk == kb, (a.shape, b.shape, ta, tb)
    has_add, has_bias = add is not None, bias is not None
    wk = wide if (wide and tb) else kk
    wn = wide if (wide and not tb) else n
    tm, tn, tk = _mm_tiles(m, n, kk, math.gcd(wn, wide_o) if wide_o else wn, math.gcd(wk, wa) if wa else wk,
                           a.dtype.itemsize, b.dtype.itemsize, has_add)
    nk = kk // tk
    dims = ((0 if ta else 1,), (1 if tb else 0,))

    def kern(*refs):
        a_ref, b_ref = refs[0], refs[1]
        rest = list(refs[2:])
        add_ref = rest.pop(0) if has_add else None
        bias_ref = rest.pop(0) if has_bias else None
        o_ref = rest[0]
        part = _dot(a_ref[...].astype(BF16), b_ref[...].astype(BF16), dims)

        def finish(r):
            if has_add:
                r = r + add_ref[...]
            if has_bias:
                r = r + bias_ref[...]
            o_ref[...] = r.astype(out_dtype)

        if nk == 1:
            finish(part)
        else:
            acc_ref = rest[1]
            k = pl.program_id(2)

            @pl.when(k == 0)
            def _():
                acc_ref[...] = part

            @pl.when((k > 0) & (k < nk - 1))
            def _():
                acc_ref[...] += part

            @pl.when(k == nk - 1)
            def _():
                finish(acc_ref[...] + part)

    if a_chunks:
        per_a = wa // tk
        a_spec = pl.BlockSpec((None, tm, tk), lambda i, j, k: (k // per_a, i, k % per_a))
    elif ta:
        a_spec = pl.BlockSpec((tk, tm), lambda i, j, k: (k, i))
    else:
        a_spec = pl.BlockSpec((tm, tk), lambda i, j, k: (i, k))
    if b_chunks and tb:
        per = wide // tk
        b_spec = pl.BlockSpec((None, tn, tk), lambda i, j, k: (k // per, j, k % per))
    elif b_chunks:
        per = wide // tn
        b_spec = pl.BlockSpec((None, tk, tn), lambda i, j, k: (j // per, k, j % per))
    elif tb:
        b_spec = pl.BlockSpec((tn, tk), lambda i, j, k: (j, k))
    else:
        b_spec = pl.BlockSpec((tk, tn), lambda i, j, k: (k, j))
    if out_chunks:
        per_o = wide_o // tn
        out_spec = pl.BlockSpec((None, tm, tn), lambda i, j, k: (j // per_o, i, j % per_o))
        out_shape = jax.ShapeDtypeStruct((N_CHIPS, m, wide_o), out_dtype)
    else:
        out_spec = pl.BlockSpec((tm, tn), lambda i, j, k: (i, j))
        out_shape = jax.ShapeDtypeStruct((m, n), out_dtype)
    in_specs, args = [a_spec, b_spec], [a, b]
    if has_add:
        in_specs.append(pl.BlockSpec((tm, tn), lambda i, j, k: (i, j)))
        args.append(add)
    if has_bias:
        in_specs.append(pl.BlockSpec((1, tn), lambda i, j, k: (0, j)))
        args.append(bias)
    (out,), side_outs = side_call(
        kern, side,
        name=name,
        grid=(m // tm, n // tn, nk),
        in_specs=in_specs,
        out_specs=[out_spec],
        out_shape=[out_shape],
        scratch_shapes=[pltpu.VMEM((tm, tn), F32)] if nk > 1 else [],
        args=args)
    return out if side is None else (out, side_outs)


def mm_hooked(plan, a, b, *, name, **kw):
    side = plan.side(name)
    if side is None:
        return mm(a, b, name=name, **kw)
    out, side_outs = mm(a, b, name=name, side=side, **kw)
    plan.done(name, side_outs)
    return out


def rowwise(fn, ins, outs, accs=(), *, tr, name):
    rows = [a for a, kind in ins if kind == "row"][0].shape[0]
    tr = min(tr, rows)
    assert rows % tr == 0 and tr % SUBLANES == 0, (rows, tr)
    n = rows // tr
    n_in, n_out = len(ins), len(outs)
    kinds = [kind for _, kind in ins]

    def kern(*refs):
        i = pl.program_id(0)
        vals = []
        for ref, kind in zip(refs[:n_in], kinds):
            v = ref[...]
            if kind == "prev":
                v = v * (i > 0).astype(v.dtype)
            elif kind == "next":
                v = v * (i < n - 1).astype(v.dtype)
            vals.append(v)
        res = fn(*vals)
        for ref, r in zip(refs[n_in:n_in + n_out], res[:n_out]):
            ref[...] = r.astype(ref.dtype)
        if accs:
            acc_refs = refs[n_in + n_out:]

            @pl.when(i == 0)
            def _():
                for ref in acc_refs:
                    ref[...] = jnp.zeros_like(ref)

            for ref, r in zip(acc_refs, res[n_out:]):
                ref[...] += r

    in_specs = []
    for a, kind in ins:
        if kind == "row":
            in_specs.append(pl.BlockSpec((tr, a.shape[1]), lambda i: (i, 0)))
        elif kind == "full":
            in_specs.append(pl.BlockSpec(a.shape, lambda i, nd=a.ndim: (0,) * nd))
        elif kind == "prev":
            in_specs.append(pl.BlockSpec((SUBLANES, a.shape[1]),
                                         lambda i: (jnp.maximum(i * (tr // SUBLANES) - 1, 0), 0)))
        else:
            in_specs.append(pl.BlockSpec((SUBLANES, a.shape[1]),
                                         lambda i: (jnp.minimum((i + 1) * (tr // SUBLANES), rows // SUBLANES - 1), 0)))
    out_specs = [pl.BlockSpec((tr, c), lambda i: (i, 0)) for c, _ in outs]
    out_specs += [pl.BlockSpec((r, c), lambda i: (0, 0)) for r, c in accs]
    out_shape = [jax.ShapeDtypeStruct((rows, c), dt) for c, dt in outs]
    out_shape += [jax.ShapeDtypeStruct((r, c), F32) for r, c in accs]
    res = pl.pallas_call(
        kern,
        name=name,
        grid=(n,),
        in_specs=in_specs,
        out_specs=out_specs,
        out_shape=out_shape,
        compiler_params=_params(("arbitrary",) if accs else ("parallel",)),
    )(*[a for a, _ in ins])
    return res


def rms_fwd(x, g, name):
    def fn(xv, gv):
        r = lax.rsqrt(jnp.mean(xv * xv, axis=1, keepdims=True) + EPS)
        return (xv * r * gv,)

    return rowwise(fn, [(x, "row"), (g, "full")], [(x.shape[1], BF16)], tr=512, name=name)[0]


def rms_bwd(x, g, dy, dres, name):
    def fn(xv, gv, dyv, drv):
        r = lax.rsqrt(jnp.mean(xv * xv, axis=1, keepdims=True) + EPS)
        xh = xv * r
        dyg = dyv * gv
        dx = drv + r * (dyg - xh * jnp.mean(dyg * xh, axis=1, keepdims=True))
        return dx, dx, _colsum(dyv * xh)

    c = x.shape[1]
    dx, dxb, dg = rowwise(fn, [(x, "row"), (g, "full"), (dy, "row"), (dres, "row")], [(c, F32), (c, BF16)], [(1, c)],
                          tr=256, name=name)
    return (dx, dxb), dg


def ffn_up(h, wgu, name, side=None):
    s, d = h.shape
    nch, _, w = wgu.shape
    half = nch // 2
    tm = _pick(s, (512, 256, 128))

    def kern(h_ref, wg_ref, wu_ref, gu_ref, a_ref):
        hv = h_ref[...]
        g = _dot(hv, wg_ref[...])
        u = _dot(hv, wu_ref[...])
        gu_ref[0] = g.astype(BF16)
        gu_ref[1] = u.astype(BF16)
        a_ref[...] = (g * _sigmoid(g) * u).astype(BF16)

    return side_call(
        kern, side, name=name, grid=(s // tm, half),
        in_specs=[pl.BlockSpec((tm, d), lambda i, j: (i, 0)),
                  pl.BlockSpec((None, d, w), lambda i, j: (j, 0, 0)),
                  pl.BlockSpec((None, d, w), lambda i, j: (j + half, 0, 0))],
        out_specs=[pl.BlockSpec((2, tm, w), lambda i, j: (0, i, j)), pl.BlockSpec((tm, w), lambda i, j: (i, j))],
        out_shape=[jax.ShapeDtypeStruct((2, s, half * w), BF16), jax.ShapeDtypeStruct((s, half * w), BF16)],
        scratch_shapes=[], args=(h, wgu, wgu))


def ffn_dact(dxb, wdown, gu, name):
    s, d = dxb.shape
    hid = wdown.shape[0]
    tm = _pick(s, (512, 256, 128))
    tn = _pick(hid, (1408, 512, 256, 128))

    def kern(dx_ref, w_ref, gu_ref, o_ref):
        da = _dot_nt(dx_ref[...], w_ref[...])
        g, u = gu_ref[0].astype(F32), gu_ref[1].astype(F32)
        sg = _sigmoid(g)
        o_ref[0] = (da * u * sg * (1.0 + g * (1.0 - sg))).astype(BF16)
        o_ref[1] = (da * g * sg).astype(BF16)

    return pl.pallas_call(
        kern, name=name, grid=(s // tm, hid // tn),
        in_specs=[pl.BlockSpec((tm, d), lambda i, j: (i, 0)), pl.BlockSpec((tn, d), lambda i, j: (j, 0)),
                  pl.BlockSpec((2, tm, tn), lambda i, j: (0, i, j))],
        out_specs=pl.BlockSpec((2, tm, tn), lambda i, j: (0, i, j)),
        out_shape=jax.ShapeDtypeStruct((2, s, hid), BF16),
        compiler_params=_params(("parallel", "parallel")),
    )(dxb, wdown, gu)


def loss_and_grad(y, t, name):
    d = y.shape[1]

    def fn(yv, tv):
        e = yv - tv
        part = jnp.sum(_colsum(e * e), axis=1, keepdims=True) * (0.5 / d)
        dy = e * (1.0 / d)
        return dy, dy, jnp.broadcast_to(part, (SUBLANES, LANES))

    dy, dyb, acc = rowwise(fn, [(y, "row"), (t, "row")], [(d, F32), (d, BF16)], [(SUBLANES, LANES)], tr=512, name=name)
    return acc[0, 0], (dy, dyb)


def adamw(w, g, m, v, name):
    def fn(wv, gv, mv, vv):
        m2 = ADAM_B1 * mv + (1.0 - ADAM_B1) * gv
        v2 = ADAM_B2 * vv + (1.0 - ADAM_B2) * (gv * gv)
        m_hat = m2 / (1.0 - ADAM_B1 ** ADAM_STEP)
        v_hat = v2 / (1.0 - ADAM_B2 ** ADAM_STEP)
        delta = -ADAM_LR * (m_hat / (jnp.sqrt(v_hat) + ADAM_EPS) + ADAM_WD * wv)
        return delta, m2, v2

    rows, c = w.shape
    tr = _pick(rows, (256, 128, 64, 32, 16, 8)) if rows % SUBLANES == 0 else rows
    if rows % SUBLANES:
        return _whole(fn, [w, g, m, v], [(w.shape, F32)] * 3, name=name)
    return rowwise(fn, [(w, "row"), (g, "row"), (m, "row"), (v, "row")], [(c, F32)] * 3, tr=tr, name=name)


def _whole(fn, ins, outs, *, name):
    n_in = len(ins)

    def kern(*refs):
        res = fn(*[r[...] for r in refs[:n_in]])
        for ref, r in zip(refs[n_in:], res):
            ref[...] = r.astype(ref.dtype)

    return pl.pallas_call(
        kern,
        name=name,
        out_shape=[jax.ShapeDtypeStruct(s, dt) for s, dt in outs],
        compiler_params=pltpu.CompilerParams(vmem_limit_bytes=VMEM_LIMIT),
    )(*ins)


def ffn_fwd(x, g, wgu, wdown, tag, plan):
    h = rms_fwd(x, g, f"ffn_rms_{tag}")
    gu, a = _hooked(plan, f"ffn_gu_{tag}", ffn_up, h, wgu)
    xn = mm_hooked(plan, a, wdown, add=x, name=f"ffn_down_{tag}")
    return xn, (x, h, gu, a)


def ffn_bwd(dxn, saved, g, wgu, wdown, tag):
    x, h, gu, a = saved
    dxn, dxb = dxn
    dwdown = mm(a, dxb, ta=True, out_dtype=BF16, name=f"ffn_dwdown_{tag}")
    dgu = ffn_dact(dxb, wdown, gu, f"ffn_dact_{tag}")
    dh = mm(dgu, wgu, tb=True, a_chunks=True, b_chunks=True, name=f"ffn_dh_{tag}")
    dwgu = mm(h, dgu, ta=True, b_chunks=True, out_dtype=BF16, out_chunks=True, name=f"ffn_dwgu_{tag}")
    dx, dg = rms_bwd(x, g, dh, dxn, f"ffn_drms_{tag}")
    return dx, dg, dwgu, dwdown


def _head_blockdiag(c):
    i = jnp.arange(c) // HEAD
    return (i[:, None] == i[None, :]).astype(BF16)


def _head_sums(x, bd):
    return jnp.concatenate([_dot_x2(x[:, g * LANES:(g + 1) * LANES], bd) for g in range(x.shape[1] // LANES)], axis=1)


def qknorm_fwd(qkv, qg, kg, bd, name):
    d = qkv.shape[1] // 3
    scale = 1.0 / math.sqrt(HEAD)

    def fn(v, qgv, kgv, bdv):
        v = v.astype(F32)
        q, k, vv = v[:, :d], v[:, d:2 * d], v[:, 2 * d:]
        rq = lax.rsqrt(_head_sums(q * q, bdv) * (1.0 / HEAD) + EPS)
        rk = lax.rsqrt(_head_sums(k * k, bdv) * (1.0 / HEAD) + EPS)
        return q * rq * qgv * scale, k * rk * kgv, vv

    return rowwise(fn, [(qkv, "row"), (qg, "full"), (kg, "full"), (bd, "full")],
                   [(d, BF16), (d, BF16), (d, BF16)], tr=256, name=name)


def qknorm_bwd(qkv, dqs, dkn, dv, qg, kg, bd, name):
    d = qkv.shape[1] // 3
    scale = 1.0 / math.sqrt(HEAD)

    def one(xv, gv, dyv, bdv):
        r = lax.rsqrt(_head_sums(xv * xv, bdv) * (1.0 / HEAD) + EPS)
        xh = xv * r
        dyg = dyv * gv
        dx = r * (dyg - xh * (_head_sums(dyg * xh, bdv) * (1.0 / HEAD)))
        return dx, _colsum(dyv * xh)

    def fn(v, dqv, dkv, dvv, qgv, kgv, bdv):
        v = v.astype(F32)
        q, k = v[:, :d], v[:, d:2 * d]
        dq, dqg = one(q, qgv, dqv * scale, bdv)
        dk, dkg = one(k, kgv, dkv, bdv)
        return jnp.concatenate([dq, dk, dvv], axis=1), dqg, dkg

    return rowwise(fn, [(qkv, "row"), (dqs, "row"), (dkn, "row"), (dv, "row"), (qg, "full"), (kg, "full"), (bd, "full")],
                   [(3 * d, BF16)], [(1, d), (1, d)], tr=256, name=name)


def _sb_tile(qh, k, mask, tri_gt):
    z = _dot_nt(qh, k)
    sp = jnp.log(1.0 + jnp.exp(-jnp.abs(z)))
    lb = jnp.minimum(z, 0.0) - sp
    l1 = jnp.where(mask, lb - z, 0.0)
    suf = _dot(l1.astype(BF16), tri_gt)
    return lb, l1, suf


def _sb_setup(tq, tk):
    row, col = _iota2((tq, tk), 0), _iota2((tq, tk), 1)
    lane = _iota2((1, LANES), 1)
    halves = [(lane < HEAD).astype(BF16), (lane >= HEAD).astype(BF16)]
    lane_q = _iota2((tq, LANES), 1) + jnp.minimum(_iota2((tq, LANES), 0), 0)
    return row, col, halves, lane_q


def sb_attn_fwd(qs, kn, vb, name, side=None):
    s, d = qs.shape
    tq, tk = min(SB_TQ, s), min(SB_TK, s)
    nq = s // tq
    assert s // tk <= LANES and s % tq == 0 and s % tk == 0

    def kern(q_ref, k_ref, v_ref, o_ref, rs_ref, acc_ref):
        i = pl.program_id(1)
        row, col, halves, lane_q = _sb_setup(tq, tk)
        tri_gt = (_iota2((tk, tk), 0) > _iota2((tk, tk), 1)).astype(BF16)
        q = q_ref[...]
        qh = [q * hm for hm in halves]
        acc_ref[...] = jnp.zeros_like(acc_ref)
        rs_ref[...] = jnp.full(rs_ref.shape, SB_UNSEEN, F32)
        nkb = (i + 1) * (tq // tk)

        def more(st):
            return (st[0] < nkb) & (st[1] > SB_DEAD)

        def step(st):
            n, r = st[0], list(st[2:])
            kb = nkb - 1 - n
            ks = pl.multiple_of(kb * tk, tk)
            k = k_ref[pl.ds(ks, tk), :]
            v = v_ref[pl.ds(ks, tk), :]
            mask = col < row + (i * tq - kb * tk)
            at_kb = lane_q == kb
            for hh in range(2):
                lb, l1, suf = _sb_tile(qh[hh], k, mask, tri_gt)
                w = jnp.where(mask, jnp.exp(lb + suf + r[hh]), 0.0)
                acc_ref[...] += _dot(w.astype(BF16), v * halves[hh])
                rs_ref[hh] = jnp.where(at_kb, r[hh], rs_ref[hh])
                r[hh] = r[hh] + _rowsum(l1)
            return (n + 1, jnp.maximum(jnp.max(r[0]), jnp.max(r[1])), r[0], r[1])

        z1 = jnp.zeros((tq, 1), F32)
        lax.while_loop(more, step, (jnp.int32(0), jnp.float32(0.0), z1, z1))
        o_ref[...] = acc_ref[...].astype(BF16)

    nh2 = d // LANES
    return side_call(
        kern, side,
        name=name,
        grid=(nh2, nq),
        in_specs=[pl.BlockSpec((tq, LANES), lambda h, i: (i, h)),
                  pl.BlockSpec((s, LANES), lambda h, i: (0, h)),
                  pl.BlockSpec((s, LANES), lambda h, i: (0, h))],
        out_specs=[pl.BlockSpec((tq, LANES), lambda h, i: (i, h)),
                   pl.BlockSpec((None, 2, tq, LANES), lambda h, i: (h, 0, i, 0))],
        out_shape=[jax.ShapeDtypeStruct((s, d), BF16), jax.ShapeDtypeStruct((nh2, 2, s, LANES), F32)],
        scratch_shapes=[pltpu.VMEM((tq, LANES), F32)],
        args=(qs, kn, vb))


def sb_attn_bwd(qs, kn, vb, rsave, do, name, side=None):
    s, d = qs.shape
    tq, tk = min(SB_TQ, s), min(SB_TK, s)
    nq = s // tq

    def kern(q_ref, k_ref, v_ref, rs_ref, do_ref, dq_ref, dk_ref, dv_ref):
        i = pl.program_id(1)

        @pl.when(i == 0)
        def _():
            dk_ref[...] = jnp.zeros_like(dk_ref)
            dv_ref[...] = jnp.zeros_like(dv_ref)

        row, col, halves, lane_q = _sb_setup(tq, tk)
        tri_gt = (_iota2((tk, tk), 0) > _iota2((tk, tk), 1)).astype(BF16)
        tri_lt = (_iota2((tk, tk), 0) < _iota2((tk, tk), 1)).astype(BF16)
        q = q_ref[...]
        qh = [q * hm for hm in halves]
        dov = do_ref[...].astype(BF16)
        doh = [dov * hm for hm in halves]
        dq_ref[...] = jnp.zeros_like(dq_ref)
        nkb = (i + 1) * (tq // tk)
        top = jnp.maximum(jnp.max(rs_ref[0], axis=0, keepdims=True), jnp.max(rs_ref[1], axis=0, keepdims=True))
        dead = (top <= SB_DEAD) & (_iota2((1, LANES), 1) < nkb)
        kstart = jnp.minimum(jnp.sum(dead.astype(F32)).astype(jnp.int32), nkb)

        def step(kb, ep):
            ep = list(ep)
            ks = pl.multiple_of(kb * tk, tk)
            k = k_ref[pl.ds(ks, tk), :]
            v = v_ref[pl.ds(ks, tk), :]
            mask = col < row + (i * tq - kb * tk)
            at_kb = lane_q == kb
            for hh in range(2):
                lb, l1, suf = _sb_tile(qh[hh], k, mask, tri_gt)
                r = _rowsum(jnp.where(at_kb, rs_ref[hh], 0.0))
                lbm = jnp.where(mask, lb, SB_UNSEEN)
                w = jnp.exp(lbm + suf + r)
                e = _dot_nt(doh[hh], v) * w
                pe = ep[hh] + _dot(e.astype(BF16), tri_lt)
                beta = jnp.exp(lbm)
                dz = (e - beta * (e + pe)).astype(BF16)
                dq_ref[...] += _dot(dz, k * halves[hh])
                dk_ref[pl.ds(ks, tk), :] += _dot_tn(dz, qh[hh])
                dv_ref[pl.ds(ks, tk), :] += _dot_tn(w.astype(BF16), doh[hh])
                ep[hh] = ep[hh] + _rowsum(e)
            return tuple(ep)

        z1 = jnp.zeros((tq, 1), F32)
        lax.fori_loop(kstart, nkb, step, (z1, z1))

    nh2 = d // LANES
    return side_call(
        kern, side,
        name=name,
        grid=(nh2, nq),
        in_specs=[pl.BlockSpec((tq, LANES), lambda h, i: (i, h)),
                  pl.BlockSpec((s, LANES), lambda h, i: (0, h)),
                  pl.BlockSpec((s, LANES), lambda h, i: (0, h)),
                  pl.BlockSpec((None, 2, tq, LANES), lambda h, i: (h, 0, i, 0)),
                  pl.BlockSpec((tq, LANES), lambda h, i: (i, h))],
        out_specs=[pl.BlockSpec((tq, LANES), lambda h, i: (i, h)),
                   pl.BlockSpec((s, LANES), lambda h, i: (0, h)),
                   pl.BlockSpec((s, LANES), lambda h, i: (0, h))],
        out_shape=[jax.ShapeDtypeStruct((s, d), F32)] * 3,
        scratch_shapes=[],
        args=(qs, kn, vb, rsave, do))


def _hooked(plan, tag, call, *args):
    side = plan.side(tag)
    outs, side_outs = call(*args, tag, side)
    if side is not None:
        plan.done(tag, side_outs)
    return outs


def sb_fwd(x, g, wqkv, qg, kg, wo, bd, tag, plan):
    h = rms_fwd(x, g, f"sb_rms_{tag}")
    qkv = mm(h, wqkv, b_chunks=True, out_dtype=BF16, name=f"sb_qkv_{tag}")
    qs, kn, vb = qknorm_fwd(qkv, qg, kg, bd, f"sb_qknorm_{tag}")
    o, rsave = _hooked(plan, f"sb_attn_{tag}", sb_attn_fwd, qs, kn, vb)
    xn = mm(o, wo(), add=x, name=f"sb_out_{tag}")
    return xn, (x, h, qkv, qs, kn, vb, rsave, o)


def sb_bwd(dxn, saved, g, wqkv, qg, kg, wo, bd, tag, plan):
    x, h, qkv, qs, kn, vb, rsave, o = saved
    dxn, dxb = dxn
    do = mm(dxb, wo, tb=True, name=f"sb_do_{tag}")
    dwo = mm(o, dxb, ta=True, out_dtype=BF16, name=f"sb_dwo_{tag}")
    dqs, dkn, dv = _hooked(plan, f"sb_dattn_{tag}", sb_attn_bwd, qs, kn, vb, rsave, do)
    dqkv, dqg, dkg = qknorm_bwd(qkv, dqs, dkn, dv, qg, kg, bd, f"sb_dqknorm_{tag}")
    dh = mm(dqkv, wqkv, tb=True, b_chunks=True, name=f"sb_dh_{tag}")
    dwqkv = mm(h, dqkv, ta=True, out_dtype=BF16, out_chunks=True, name=f"sb_dwqkv_{tag}")
    dx, dg = rms_bwd(x, g, dh, dxn, f"sb_drms_{tag}")
    nh = dqg.shape[1] // HEAD
    return dx, dg, dwqkv, dqg.reshape(nh, HEAD).sum(0), dkg.reshape(nh, HEAD).sum(0), dwo


def _gelu(x):
    return 0.5 * x * (1.0 + lax.erf(x * (1.0 / math.sqrt(2.0))))


def _gelu_grad(x):
    return 0.5 * (1.0 + lax.erf(x * (1.0 / math.sqrt(2.0)))) + x * jnp.exp(-0.5 * x * x) * (1.0 / math.sqrt(2.0 * math.pi))


def gm_act_fwd(pre, vg, name):
    half = pre.shape[1] // 2

    def fn(p, vgv):
        p = p.astype(F32)
        u = _gelu(p[:, :half])
        v = _gelu(p[:, half:])
        r = lax.rsqrt(jnp.mean(v * v, axis=1, keepdims=True) + EPS)
        return u, v * r * vgv

    return rowwise(fn, [(pre, "row"), (vg, "full")], [(half, F32), (half, BF16)], tr=256, name=name)


def gm_act_bwd(pre, du, dvn, vg, name):
    half = pre.shape[1] // 2

    def fn(p, duv, dvnv, vgv):
        p = p.astype(F32)
        pu, pv = p[:, :half], p[:, half:]
        v = _gelu(pv)
        r = lax.rsqrt(jnp.mean(v * v, axis=1, keepdims=True) + EPS)
        vh = v * r
        dyg = dvnv * vgv
        dv = r * (dyg - vh * jnp.mean(dyg * vh, axis=1, keepdims=True))
        dpre = jnp.concatenate([duv * _gelu_grad(pu), dv * _gelu_grad(pv)], axis=1)
        return dpre, _colsum(dvnv * vh), _colsum(dpre)

    return rowwise(fn, [(pre, "row"), (du, "row"), (dvn, "row"), (vg, "full")],
                   [(2 * half, BF16)], [(1, half), (1, 2 * half)], tr=256, name=name)


def gm_spatial_fwd(u, vn, wc, bst, name):
    s, c = u.shape
    t = CHUNK
    ng = c // LANES

    def kern(u_ref, v_ref, w_ref, b_ref, o_ref):
        for g in range(ng):
            sl = slice(g * LANES, (g + 1) * LANES)
            mixed = _dot(w_ref[g], v_ref[:, sl]) + b_ref[:, sl]
            o_ref[:, sl] = (u_ref[:, sl] * mixed).astype(BF16)

    return pl.pallas_call(
        kern,
        name=name,
        grid=(s // t,),
        in_specs=[pl.BlockSpec((t, c), lambda i: (i, 0)), pl.BlockSpec((t, c), lambda i: (i, 0)),
                  pl.BlockSpec(wc.shape, lambda i: (0, 0, 0)), pl.BlockSpec(bst.shape, lambda i: (0, 0))],
        out_specs=pl.BlockSpec((t, c), lambda i: (i, 0)),
        out_shape=jax.ShapeDtypeStruct((s, c), BF16),
        compiler_params=_params(("parallel",)),
    )(u, vn, wc, bst)


def gm_spatial_bwd(dgate, u, vn, wc, bst, name):
    s, c = u.shape
    t = CHUNK
    ng = c // LANES

    def kern(dg_ref, u_ref, v_ref, w_ref, b_ref, du_ref, dv_ref, dw_ref, db_ref):
        i = pl.program_id(0)

        @pl.when(i == 0)
        def _():
            dw_ref[...] = jnp.zeros_like(dw_ref)
            db_ref[...] = jnp.zeros_like(db_ref)

        for g in range(ng):
            sl = slice(g * LANES, (g + 1) * LANES)
            vg = v_ref[:, sl]
            dgv = dg_ref[:, sl]
            mixed = _dot(w_ref[g], vg) + b_ref[:, sl]
            du_ref[:, sl] = dgv * mixed
            dmix = dgv * u_ref[:, sl]
            dmb = dmix.astype(BF16)
            dv_ref[:, sl] = _dot_tn(w_ref[g], dmb)
            dw_ref[g] += _dot_nt(dmb, vg)
            db_ref[:, sl] += dmix

    return pl.pallas_call(
        kern,
        name=name,
        grid=(s // t,),
        in_specs=[pl.BlockSpec((t, c), lambda i: (i, 0))] * 3 +
                 [pl.BlockSpec(wc.shape, lambda i: (0, 0, 0)), pl.BlockSpec(bst.shape, lambda i: (0, 0))],
        out_specs=[pl.BlockSpec((t, c), lambda i: (i, 0)), pl.BlockSpec((t, c), lambda i: (i, 0)),
                   pl.BlockSpec(wc.shape, lambda i: (0, 0, 0)), pl.BlockSpec(bst.shape, lambda i: (0, 0))],
        out_shape=[jax.ShapeDtypeStruct((s, c), F32), jax.ShapeDtypeStruct((s, c), F32),
                   jax.ShapeDtypeStruct(wc.shape, F32), jax.ShapeDtypeStruct(bst.shape, F32)],
        compiler_params=_params(("arbitrary",)),
    )(dgate, u, vn, wc, bst)


def gm_fwd(x, g, w_in, b_in, vg, wc, bst, w_out, tag):
    h = rms_fwd(x, g, f"gm_rms_{tag}")
    pre = mm(h, w_in, bias=b_in, b_chunks=True, out_dtype=BF16, name=f"gm_in_{tag}")
    u, vn = gm_act_fwd(pre, vg, f"gm_act_{tag}")
    gate = gm_spatial_fwd(u, vn, wc, bst, f"gm_spatial_{tag}")
    xn = mm(gate, w_out, add=x, name=f"gm_out_{tag}")
    return xn, (x, h, pre, u, vn, gate)


def gm_bwd(dxn, saved, g, w_in, vg, wc, bst, w_out, tag):
    x, h, pre, u, vn, gate = saved
    dxn, dxb = dxn
    dgate = mm(dxb, w_out, tb=True, name=f"gm_dgate_{tag}")
    dwout = mm(gate, dxb, ta=True, out_dtype=BF16, name=f"gm_dwout_{tag}")
    du, dvn, dws, dbst = gm_spatial_bwd(dgate, u, vn, wc, bst, f"gm_dspatial_{tag}")
    dpre, dvg, dbin = gm_act_bwd(pre, du, dvn, vg, f"gm_dact_{tag}")
    dh = mm(dpre, w_in, tb=True, b_chunks=True, name=f"gm_dh_{tag}")
    dwin = mm(h, dpre, ta=True, out_dtype=BF16, out_chunks=True, name=f"gm_dwin_{tag}")
    dx, dg = rms_bwd(x, g, dh, dxn, f"gm_drms_{tag}")
    ng = wc.shape[0]
    dws = jnp.where(jnp.tril(jnp.ones((CHUNK, CHUNK), bool)), dws, 0.0)
    dbs = dbst.reshape(CHUNK, ng, LANES).sum(-1).T
    return dx, dg, dwin, dbin, dvg, dws, dbs, dwout


def _conv_taps(xv, prev):
    cat = jnp.concatenate([prev, xv], axis=0)
    return [pltpu.roll(cat, sh, 0)[SUBLANES:] for sh in (3, 2, 1)] + [xv]


def conv_fwd(xbc, ws, b, d_inner, name):
    c = xbc.shape[1]
    nst = (c - d_inner) // 2

    def fn(xv, prev, w0, w1, w2, w3, bv):
        taps = _conv_taps(xv, prev)
        pre = bv + w0 * taps[0] + w1 * taps[1] + w2 * taps[2] + w3 * taps[3]
        out = pre * _sigmoid(pre)
        return out[:, :d_inner], out[:, d_inner:d_inner + nst], out[:, d_inner + nst:]

    return rowwise(fn, [(xbc, "row"), (xbc, "prev")] + [(w, "full") for w in ws] + [(b, "full")],
                   [(d_inner, F32), (nst, F32), (nst, F32)], tr=256, name=name)


def conv_bwd_pre(xbc, ws, b, dxs_a, dxs_b, db_m, dc_m, name):
    c = xbc.shape[1]

    def fn(xv, prev, w0, w1, w2, w3, bv, da, db2, dbm, dcm):
        taps = _conv_taps(xv, prev)
        pre = bv + w0 * taps[0] + w1 * taps[1] + w2 * taps[2] + w3 * taps[3]
        sg = _sigmoid(pre)
        dout = jnp.concatenate([da + db2, dbm, dcm], axis=1)
        dpre = dout * sg * (1.0 + pre * (1.0 - sg))
        return (dpre,) + tuple(_colsum(dpre * tp) for tp in taps) + (_colsum(dpre),)

    return rowwise(fn, [(xbc, "row"), (xbc, "prev")] + [(w, "full") for w in ws] +
                   [(b, "full"), (dxs_a, "row"), (dxs_b, "row"), (db_m, "row"), (dc_m, "row")],
                   [(c, F32)], [(1, c)] * 5, tr=256, name=name)


def conv_bwd_in(dpre, ws, name):
    c = dpre.shape[1]

    def fn(dv, nxt, w0, w1, w2, w3):
        cat = jnp.concatenate([dv, nxt], axis=0)
        n = cat.shape[0]
        up = [pltpu.roll(cat, n - sh, 0)[:dv.shape[0]] for sh in (1, 2, 3)]
        return (w3 * dv + w2 * up[0] + w1 * up[1] + w0 * up[2],)

    return rowwise(fn, [(dpre, "row"), (dpre, "next")] + [(w, "full") for w in ws], [(c, BF16)], tr=256, name=name)[0]


def ssd_pre(dtr, bias, alog, name):
    def fn(d, bv, al, tri):
        dt = _softplus(d + bv)
        a = dt * (-jnp.exp(al))
        return dt, _dot_x3_left(tri, a)

    tri = jnp.tril(jnp.ones((CHUNK, CHUNK), BF16))
    return rowwise(fn, [(dtr, "row"), (bias, "full"), (alog, "full"), (tri, "full")],
                   [(LANES, F32), (LANES, F32)], tr=CHUNK, name=name)


def _ssd_layouts(v, ngroups, hpg):
    s = v.shape[0]
    col = v[:, :ngroups * hpg].T.reshape(ngroups, hpg, s, 1)
    return jnp.broadcast_to(col, (ngroups, hpg, s, LANES))


def _ssd_rowform(acum, ngroups, hpg):
    s = acum.shape[0]
    nc = s // CHUNK
    a = acum[:, :ngroups * hpg].reshape(nc, CHUNK, ngroups, hpg).transpose(2, 0, 3, 1)
    last = jnp.broadcast_to(a[..., CHUNK - 1:], a.shape)
    return jnp.concatenate([a, last], axis=2)


def ssd_chunk_fwd(xs, bm, cm, col_a, col_dt, rowf, name, side=None):
    s, d_inner = xs.shape
    ln = CHUNK
    nc = s // ln
    nsub = _pick(nc, (SSD_SUB, 2, 1))
    rows = nsub * ln
    ng, hpg = col_a.shape[0], col_a.shape[1]
    gw = d_inner // ng
    assert gw == hpg * HEAD and gw % LANES == 0 and bm.shape[1] == ng * LANES

    def kern(x_ref, b_ref, c_ref, ca_ref, cd_ref, rf_ref, y_ref, hp_ref, h_scr):
        @pl.when(pl.program_id(1) == 0)
        def _():
            h_scr[...] = jnp.zeros_like(h_scr)

        causal = _iota2((ln, ln), 0) >= _iota2((ln, ln), 1)
        lane = _iota2((1, LANES), 1)
        for sc in range(nsub):
            rs = slice(sc * ln, (sc + 1) * ln)
            bb = b_ref[rs, :].astype(BF16)
            cbf = c_ref[rs, :].astype(BF16)
            cb = _dot_nt(cbf, bb)
            ys = [jnp.zeros((ln, LANES), F32) for _ in range(gw // LANES)]
            for r in range(hpg):
                j, hf = divmod(r, LANES // HEAD)
                mh = ((lane >= HEAD * hf) & (lane < HEAD * (hf + 1))).astype(F32)
                ac = ca_ref[r, rs, :]
                ar = rf_ref[sc, pl.ds(r, 1), :]
                aend = rf_ref[sc, pl.ds(4 + r, 1), :]
                dm = jnp.exp(jnp.minimum(ac - ar, 0.0))
                m = jnp.where(causal, cb * dm, 0.0).astype(BF16)
                xdt = x_ref[rs, j * LANES:(j + 1) * LANES] * cd_ref[r, rs, :] * mh
                h = h_scr[r]
                hp_ref[sc, r] = h
                ys[j] = ys[j] + _dot(m, xdt.astype(BF16)) + _dot_nt(cbf, h.astype(BF16)) * jnp.exp(ac)
                dte = jnp.exp(aend - ac)
                h_scr[r] = jnp.exp(aend) * h + _dot_tn((xdt * dte).astype(BF16), bb)
            for j in range(gw // LANES):
                y_ref[rs, j * LANES:(j + 1) * LANES] = ys[j]

    colspec = pl.BlockSpec((None, hpg, rows, LANES), lambda g, c: (g, 0, c, 0))
    return side_call(
        kern, side,
        name=name,
        grid=(ng, nc // nsub),
        in_specs=[pl.BlockSpec((rows, gw), lambda g, c: (c, g)),
                  pl.BlockSpec((rows, LANES), lambda g, c: (c, g)),
                  pl.BlockSpec((rows, LANES), lambda g, c: (c, g)),
                  colspec, colspec,
                  pl.BlockSpec((None, nsub, 8, LANES), lambda g, c: (g, c, 0, 0))],
        out_specs=[pl.BlockSpec((rows, gw), lambda g, c: (c, g)),
                   pl.BlockSpec((None, nsub, hpg, LANES, LANES), lambda g, c: (g, c, 0, 0, 0))],
        out_shape=[jax.ShapeDtypeStruct((s, d_inner), F32),
                   jax.ShapeDtypeStruct((ng, nc, hpg, LANES, LANES), F32)],
        scratch_shapes=[pltpu.VMEM((hpg, LANES, LANES), F32)],
        args=(xs, bm, cm, col_a, col_dt, rowf))


def ssd_chunk_bwd(xs, bm, cm, col_a, col_dt, rowf, hprev, dy, name, side=None):
    s, d_inner = xs.shape
    ln = CHUNK
    nc = s // ln
    nsub = _pick(nc, (SSD_SUB, 2, 1))
    rows = nsub * ln
    ng, hpg = col_a.shape[0], col_a.shape[1]
    gw = d_inner // ng

    def kern(x_ref, b_ref, c_ref, ca_ref, cd_ref, rf_ref, hp_ref, dy_ref,
             dx_ref, db_ref, dc_ref, ddt_ref, da_ref, dh_scr):
        @pl.when(pl.program_id(1) == 0)
        def _():
            dh_scr[...] = jnp.zeros_like(dh_scr)

        row, col = _iota2((ln, ln), 0), _iota2((ln, ln), 1)
        causal = row >= col
        tri_ge = (col >= row).astype(BF16)
        ones = jnp.ones((ln, LANES), BF16)
        lane = _iota2((1, LANES), 1)
        last_row = (_iota2((ln, 1), 0) == ln - 1).astype(F32)
        for sc in reversed(range(nsub)):
            rs = slice(sc * ln, (sc + 1) * ln)
            bb = b_ref[rs, :].astype(BF16)
            cbf = c_ref[rs, :].astype(BF16)
            cb = _dot_nt(cbf, bb)
            dcb = jnp.zeros((ln, ln), F32)
            d_b = jnp.zeros((ln, LANES), F32)
            d_c = jnp.zeros((ln, LANES), F32)
            dxs = [jnp.zeros((ln, LANES), F32) for _ in range(gw // LANES)]
            for r in range(hpg):
                j, hf = divmod(r, LANES // HEAD)
                mh = ((lane >= HEAD * hf) & (lane < HEAD * (hf + 1))).astype(F32)
                ac = ca_ref[r, rs, :]
                dt = cd_ref[r, rs, :]
                ar = rf_ref[sc, pl.ds(r, 1), :]
                aend = rf_ref[sc, pl.ds(4 + r, 1), :]
                dm = jnp.where(causal, jnp.exp(jnp.minimum(ac - ar, 0.0)), 0.0)
                m = cb * dm
                mb = m.astype(BF16)
                xp = x_ref[rs, j * LANES:(j + 1) * LANES]
                xdt = xp * dt * mh
                xdtb = xdt.astype(BF16)
                dyp = dy_ref[rs, j * LANES:(j + 1) * LANES] * mh
                dypb = dyp.astype(BF16)
                h = hp_ref[sc, r]
                hb = h.astype(BF16)
                dh = dh_scr[r]
                dhb = dh.astype(BF16)
                e_in = jnp.exp(ac)
                dte = jnp.exp(aend - ac)
                eend = jnp.exp(aend)
                d_m = _dot_nt(dypb, xdtb)
                dcb = dcb + d_m * dm
                gm = d_m * m
                yoff_pre = _dot_nt(cbf, hb)
                bdh = _dot_nt(bb, dhb)
                dxdt = _dot_tn(mb, dypb) + bdh * dte
                t1 = _rowsum(xdt * bdh) * dte
                gh, gl = _split2(gm)
                dacum = (_rowsum(gm) - (_dot_tn(gh, ones) + _dot_tn(gl, ones))
                         + _rowsum(dyp * yoff_pre) * e_in - t1)
                end_term = _colsum(t1) + eend * jnp.sum(_colsum(dh * h), axis=1, keepdims=True)
                dacum = dacum + last_row * end_term
                da_ref[r, rs, :] = _dot_x3_left(tri_ge, dacum)
                ddt_ref[r, rs, :] = jnp.broadcast_to(_rowsum(dxdt * xp), (ln, LANES))
                dxs[j] = dxs[j] + dxdt * dt
                d_b = d_b + _dot((xdt * dte).astype(BF16), dhb)
                dye = (dyp * e_in).astype(BF16)
                d_c = d_c + _dot(dye, hb)
                dh_scr[r] = eend * dh + _dot_tn(dye, cbf)
            dcbb = dcb.astype(BF16)
            dc_ref[rs, :] = d_c + _dot(dcbb, bb)
            db_ref[rs, :] = d_b + _dot_tn(dcbb, cbf)
            for j in range(gw // LANES):
                dx_ref[rs, j * LANES:(j + 1) * LANES] = dxs[j]

    rev = nc // nsub - 1
    colspec = pl.BlockSpec((None, hpg, rows, LANES), lambda g, c: (g, 0, rev - c, 0))
    return side_call(
        kern, side,
        name=name,
        grid=(ng, nc // nsub),
        in_specs=[pl.BlockSpec((rows, gw), lambda g, c: (rev - c, g)),
                  pl.BlockSpec((rows, LANES), lambda g, c: (rev - c, g)),
                  pl.BlockSpec((rows, LANES), lambda g, c: (rev - c, g)),
                  colspec, colspec,
                  pl.BlockSpec((None, nsub, 8, LANES), lambda g, c: (g, rev - c, 0, 0)),
                  pl.BlockSpec((None, nsub, hpg, LANES, LANES), lambda g, c: (g, rev - c, 0, 0, 0)),
                  pl.BlockSpec((rows, gw), lambda g, c: (rev - c, g))],
        out_specs=[pl.BlockSpec((rows, gw), lambda g, c: (rev - c, g)),
                   pl.BlockSpec((rows, LANES), lambda g, c: (rev - c, g)),
                   pl.BlockSpec((rows, LANES), lambda g, c: (rev - c, g)),
                   colspec, colspec],
        out_shape=[jax.ShapeDtypeStruct((s, d_inner), F32),
                   jax.ShapeDtypeStruct(bm.shape, F32), jax.ShapeDtypeStruct(cm.shape, F32),
                   jax.ShapeDtypeStruct(col_a.shape, F32), jax.ShapeDtypeStruct(col_a.shape, F32)],
        scratch_shapes=[pltpu.VMEM((hpg, LANES, LANES), F32)],
        args=(xs, bm, cm, col_a, col_dt, rowf, hprev, dy))


def gnorm_fwd(y, xs, z, dexp, gain, ngroups, name):
    c = y.shape[1]
    gw = c // ngroups

    def fn(yv, xv, zv, dv, gv):
        yg = (yv + xv * dv) * (zv * _sigmoid(zv))
        outs = []
        for k in range(ngroups):
            t = yg[:, k * gw:(k + 1) * gw]
            outs.append(t * lax.rsqrt(jnp.mean(t * t, axis=1, keepdims=True) + EPS))
        return (jnp.concatenate(outs, axis=1) * gv,)

    return rowwise(fn, [(y, "row"), (xs, "row"), (z, "row"), (dexp, "full"), (gain, "full")], [(c, BF16)], tr=256, name=name)[0]


def gnorm_bwd(dn, y, xs, z, dexp, gain, ngroups, name):
    c = y.shape[1]
    gw = c // ngroups

    def fn(dnv, yv, xv, zv, dv, gv):
        yd = yv + xv * dv
        sg = _sigmoid(zv)
        sz = zv * sg
        yg = yd * sz
        dng = dnv * gv
        dyg, yh = [], []
        for k in range(ngroups):
            sl = slice(k * gw, (k + 1) * gw)
            t = yg[:, sl]
            r = lax.rsqrt(jnp.mean(t * t, axis=1, keepdims=True) + EPS)
            th = t * r
            dyg.append(r * (dng[:, sl] - th * jnp.mean(dng[:, sl] * th, axis=1, keepdims=True)))
            yh.append(th)
        dyg = jnp.concatenate(dyg, axis=1)
        yh = jnp.concatenate(yh, axis=1)
        dyd = dyg * sz
        dz = dyg * yd * (sg * (1.0 + zv * (1.0 - sg)))
        return dyd, dyd * dv, dz, _colsum(dyd * xv), _colsum(dnv * yh)

    return rowwise(fn, [(dn, "row"), (y, "row"), (xs, "row"), (z, "row"), (dexp, "full"), (gain, "full")],
                   [(c, F32), (c, F32), (c, BF16)], [(1, c), (1, c)], tr=256, name=name)


def ssd_post(ddt, da, dt, dtr, bias, alog, name):
    def fn(ddtv, dav, dtv, dtrv, bv, al):
        a_neg = -jnp.exp(al)
        ddtr = (ddtv + dav * a_neg) * _sigmoid(dtrv + bv)
        return ddtr, _colsum(ddtr), _colsum(dav * dtv) * a_neg

    return rowwise(fn, [(ddt, "row"), (da, "row"), (dt, "row"), (dtr, "row"), (bias, "full"), (alog, "full")],
                   [(LANES, BF16)], [(1, LANES), (1, LANES)], tr=512, name=name)


def _from_colform(v, s):
    ng, hpg = v.shape[0], v.shape[1]
    flat = v[..., 0].reshape(ng * hpg, s).T
    return jnp.pad(flat, ((0, 0), (0, LANES - ng * hpg)))


def ssm_fwd(x, g, p, tag, plan):
    ng, hpg, d_inner = p["ng"], p["hpg"], p["d_inner"]
    h = rms_fwd(x, g, f"ssm_rms_{tag}")
    z = mm(h, p["w_z"], name=f"ssm_inz_{tag}")
    xbc = mm(h, p["w_xbc"], name=f"ssm_inx_{tag}")
    dtr = mm(h, p["w_dt"], name=f"ssm_indt_{tag}")
    xs, bm, cm = conv_fwd(xbc, p["conv_w"], p["conv_b"], d_inner, f"ssm_conv_{tag}")
    dt, acum = ssd_pre(dtr, p["dt_bias"], p["a_log"], f"ssm_pre_{tag}")
    col_a, col_dt = _ssd_layouts(acum, ng, hpg), _ssd_layouts(dt, ng, hpg)
    rowf = _ssd_rowform(acum, ng, hpg)
    y, hprev = _hooked(plan, f"ssm_scan_{tag}", ssd_chunk_fwd, xs, bm, cm, col_a, col_dt, rowf)
    n = gnorm_fwd(y, xs, z, p["d_exp"], p["norm_gain"], ng, f"ssm_gnorm_{tag}")
    xn = mm(n, p["w_out"], add=x, name=f"ssm_out_{tag}")
    return xn, (x, h, z, xbc, dtr, xs, bm, cm, dt, col_a, col_dt, rowf, y, hprev, n)


def ssm_bwd(dxn, saved, g, p, tag, plan):
    x, h, z, xbc, dtr, xs, bm, cm, dt, col_a, col_dt, rowf, y, hprev, n = saved
    ng, hpg, d_inner = p["ng"], p["hpg"], p["d_inner"]
    s = x.shape[0]
    dxn, dxb = dxn
    dn = mm(dxb, p["w_out"], tb=True, name=f"ssm_dn_{tag}")
    dwout = mm(n, dxb, ta=True, out_dtype=BF16, name=f"ssm_dwout_{tag}")
    dy, dxs_skip, dz, dd_lane, dgain = gnorm_bwd(dn, y, xs, z, p["d_exp"], p["norm_gain"], ng, f"ssm_dgnorm_{tag}")
    dxs, dbm, dcm, ddt_c, da_c = _hooked(plan, f"ssm_dscan_{tag}", ssd_chunk_bwd, xs, bm, cm, col_a, col_dt, rowf, hprev, dy)
    ddtr, dbias, dalog = ssd_post(_from_colform(ddt_c, s), _from_colform(da_c, s), dt, dtr,
                                  p["dt_bias"], p["a_log"], f"ssm_post_{tag}")
    res = conv_bwd_pre(xbc, p["conv_w"], p["conv_b"], dxs, dxs_skip, dbm, dcm, f"ssm_dconv_{tag}")
    dpre, dconv_w, dconv_b = res[0], jnp.concatenate(res[1:5], axis=0), res[5]
    dxbc = conv_bwd_in(dpre, p["conv_w"], f"ssm_dconvin_{tag}")
    dh = mm(dz, p["w_z"], tb=True, name=f"ssm_dhz_{tag}")
    dh = mm(dxbc, p["w_xbc"], tb=True, add=dh, name=f"ssm_dhx_{tag}")
    dh = mm(ddtr, p["w_dt"], tb=True, add=dh, name=f"ssm_dhdt_{tag}")
    dwz = mm(h, dz, ta=True, out_dtype=BF16, name=f"ssm_dwz_{tag}")
    dwxbc = mm(h, dxbc, ta=True, out_dtype=BF16, name=f"ssm_dwxbc_{tag}")
    dwdt = mm(h, ddtr, ta=True, out_dtype=BF16, name=f"ssm_dwdt_{tag}")
    dx, dg = rms_bwd(x, g, dh, dxn, f"ssm_drms_{tag}")
    nh = ng * hpg
    dwin = jnp.concatenate([dwz, dwxbc, dwdt[:, :nh]], axis=1)
    dd = dd_lane.reshape(nh, HEAD).sum(-1)
    return dx, dg, dict(w_in=dwin, conv_w=dconv_w, conv_b=dconv_b, dt_bias=dbias[0, :nh], a_log=dalog[0, :nh],
                        d=dd, norm_gain=dgain, w_out=dwout)


def local_step(x, target, w, plan):
    d = x.shape[1]
    depth = w["mix_norm"].shape[0]
    bd = _head_blockdiag(LANES)
    tril = jnp.tril(jnp.ones((CHUNK, CHUNK), bool))
    ssm_heads = w["ssm_dt_bias"].shape[1]
    d_inner = w["ssm_norm_gain"].shape[1]
    ng = w["ssm_norm_gain"].shape[1] // 256
    nstate = CHUNK

    def pad_lanes(v):
        return jnp.pad(v, ((0, 0), (0, LANES - v.shape[1])))

    def ssm_params(j):
        w_in = w["ssm_w_in"][j]
        cw = w["ssm_conv_w"][j]
        return dict(ng=ng, hpg=ssm_heads // ng, d_inner=d_inner,
                    w_z=w_in[:, :d_inner], w_xbc=w_in[:, d_inner:d_inner + d_inner + 2 * ng * nstate],
                    w_dt=pad_lanes(w_in[:, 2 * d_inner + 2 * ng * nstate:]),
                    conv_w=[cw[k:k + 1] for k in range(cw.shape[0])], conv_b=w["ssm_conv_b"][j:j + 1],
                    dt_bias=pad_lanes(w["ssm_dt_bias"][j:j + 1]), a_log=pad_lanes(w["ssm_a_log"][j:j + 1]),
                    d_exp=jnp.repeat(w["ssm_d"][j], HEAD)[None, :], norm_gain=w["ssm_norm_gain"][j:j + 1],
                    w_out=w["ssm_w_out"][j])

    def gm_params(j):
        wc = jnp.where(tril, w["gm_w_s"][j], 0.0).astype(BF16)
        bst = jnp.repeat(w["gm_b_s"][j].T, LANES, axis=1)
        return wc, bst

    def sb_gains(j):
        nh = d // HEAD
        return jnp.tile(w["sb_q_gain"][j], nh)[None, :], jnp.tile(w["sb_k_gain"][j], nh)[None, :]

    saved = []
    cur = x
    for i in range(depth):
        kind, j = i % 3, i // 3
        gmix = w["mix_norm"][i:i + 1]
        if kind == 0:
            qg, kg = sb_gains(j)
            cur, sv = sb_fwd(cur, gmix, w["sb_w_qkv"][j], qg, kg, lambda j=j: w["sb_w_o"][j], bd, f"{i}", plan)
        elif kind == 1:
            wc, bst = gm_params(j)
            cur, sv = gm_fwd(cur, gmix, w["gm_w_in"][j], w["gm_b_in"][j:j + 1], w["gm_v_gain"][j:j + 1], wc, bst,
                             w["gm_w_out"][j], f"{i}")
        else:
            cur, sv = ssm_fwd(cur, gmix, ssm_params(j), f"{i}", plan)
        cur, sv2 = ffn_fwd(cur, w["ffn_norm"][i:i + 1], w["ffn_w_gu"][i], w["ffn_w_down"][i], f"{i}", plan)
        saved.append((sv, sv2))

    loss, dcur = loss_and_grad(cur, target, "loss")

    grads = {k: [None] * len(v) for k, v in w.items()}
    for i in reversed(range(depth)):
        kind, j = i % 3, i // 3
        sv, sv2 = saved[i]
        gmix = w["mix_norm"][i:i + 1]
        dcur, dgf, dwgu, dwdown = ffn_bwd(dcur, sv2, w["ffn_norm"][i:i + 1], w["ffn_w_gu"][i], w["ffn_w_down"][i], f"{i}")
        grads["ffn_norm"][i], grads["ffn_w_gu"][i], grads["ffn_w_down"][i] = dgf[0], dwgu, dwdown
        plan.grads_ready({("ffn_w_gu", i): dwgu, ("ffn_w_down", i): dwdown})
        if kind == 0:
            qg, kg = sb_gains(j)
            dcur, dg, dwqkv, dqg, dkg, dwo = sb_bwd(dcur, sv, gmix, w["sb_w_qkv"][j], qg, kg, w["sb_w_o"][j], bd, f"{i}", plan)
            grads["sb_w_qkv"][j], grads["sb_q_gain"][j], grads["sb_k_gain"][j], grads["sb_w_o"][j] = dwqkv, dqg, dkg, dwo
        elif kind == 1:
            wc, bst = gm_params(j)
            dcur, dg, dwin, dbin, dvg, dws, dbs, dwout = gm_bwd(dcur, sv, gmix, w["gm_w_in"][j], w["gm_v_gain"][j:j + 1],
                                                                 wc, bst, w["gm_w_out"][j], f"{i}")
            grads["gm_w_in"][j], grads["gm_b_in"][j], grads["gm_v_gain"][j] = dwin, dbin[0], dvg[0]
            grads["gm_w_s"][j], grads["gm_b_s"][j], grads["gm_w_out"][j] = dws, dbs, dwout
        else:
            dcur, dg, gs = ssm_bwd(dcur, sv, gmix, ssm_params(j), f"{i}", plan)
            grads["ssm_w_in"][j], grads["ssm_conv_w"][j], grads["ssm_conv_b"][j] = gs["w_in"], gs["conv_w"], gs["conv_b"][0]
            grads["ssm_dt_bias"][j], grads["ssm_a_log"][j], grads["ssm_d"][j] = gs["dt_bias"], gs["a_log"], gs["d"]
            grads["ssm_norm_gain"][j], grads["ssm_w_out"][j] = gs["norm_gain"][0], gs["w_out"]
        grads["mix_norm"][i] = dg[0]
        mixer = {0: ("sb_w_qkv", "sb_w_o"), 1: ("gm_w_in", "gm_w_out"), 2: ("ssm_w_in", "ssm_w_out")}[kind]
        plan.grads_ready({(n, j): grads[n][j] for n in mixer})
    grads = {k: (v if k in MATRICES else jnp.stack(v)) for k, v in grads.items()}
    return loss, dcur[0], grads


WEIGHTS = ["mix_norm", "ffn_norm", "sb_w_qkv", "sb_q_gain", "sb_k_gain", "sb_w_o", "gm_w_in", "gm_b_in", "gm_v_gain",
           "gm_w_s", "gm_b_s", "gm_w_out", "ssm_w_in", "ssm_conv_w", "ssm_conv_b", "ssm_dt_bias", "ssm_a_log", "ssm_d",
           "ssm_norm_gain", "ssm_w_out", "ffn_w_gu", "ffn_w_down"]
SHARDED = {"sb_w_qkv": 2, "sb_w_o": 1, "gm_w_in": 2, "gm_w_out": 1, "ssm_w_in": 2, "ssm_conv_w": 2, "ssm_conv_b": 1,
           "ssm_norm_gain": 1, "ssm_w_out": 1, "ffn_w_gu": 2, "ffn_w_down": 1}
EXACT = ("ssm_conv_w", "ssm_conv_b", "ssm_norm_gain")
MATRICES = tuple(n for n in SHARDED if n not in EXACT)
COLUMN_BLOCKS = ("sb_w_qkv", "gm_w_in", "ffn_w_gu")
REPLICATED = [n for n in WEIGHTS if n not in SHARDED]
N_CHIPS = 4
N_DEV = 8
PACK_COLS = 1024


def _pack(pieces, dtype, align):
    flat = jnp.concatenate([p.reshape(-1).astype(dtype) for p in pieces])
    rows = -(-flat.shape[0] // (PACK_COLS * align)) * align
    flat = jnp.pad(flat, (0, rows * PACK_COLS - flat.shape[0]))
    return flat.reshape(rows, PACK_COLS)


def _unpack(flat, shapes):
    out, off = [], 0
    for shp in shapes:
        n = math.prod(shp)
        out.append(flat[off:off + n].reshape(shp))
        off += n
    return out


ANY = pl.BlockSpec(memory_space=pl.ANY)


def _pos():
    return lax.axis_index("x"), lax.axis_index("y"), lax.axis_index("c")


def _remote(src, dst, send, recv, k, to):
    return pltpu.make_async_remote_copy(src_ref=src, dst_ref=dst, send_sem=send.at[k], recv_sem=recv.at[k],
                                        device_id=to, device_id_type=MESH_ID)


def _comm_call(body, name, ins, out_shapes, nsem, aliases=None):
    return pl.pallas_call(
        body, name=name, out_shape=out_shapes,
        in_specs=[ANY] * len(ins), out_specs=[ANY] * len(out_shapes),
        scratch_shapes=[pltpu.SemaphoreType.DMA((nsem,)), pltpu.SemaphoreType.DMA((nsem,))],
        input_output_aliases=aliases or {},
    )(*ins)


def stage_shard(w, chip, name):
    rows, cols = w.shape
    tr = _pick(rows, (256, 352, 128))

    def kern(idx_ref, w_ref, o_ref):
        o_ref[...] = w_ref[...].astype(BF16)

    grid_spec = pltpu.PrefetchScalarGridSpec(
        num_scalar_prefetch=1, grid=(rows // tr,),
        in_specs=[pl.BlockSpec((tr, cols), lambda i, idx: (i, 0))],
        out_specs=pl.BlockSpec((None, tr, cols), lambda i, idx: (idx[0], i, 0)))
    return pl.pallas_call(
        kern, name=name, grid_spec=grid_spec,
        out_shape=jax.ShapeDtypeStruct((N_CHIPS, rows, cols), BF16),
        compiler_params=_params(("parallel",)),
    )(jnp.reshape(chip, (1,)).astype(jnp.int32), w)


class Side:
    def __init__(self, arrays, out_shapes, aliases, nsem, start, finish):
        self.arrays, self.out_shapes, self.aliases, self.nsem = list(arrays), list(out_shapes), aliases, nsem
        self.start, self.finish = start, finish


def run_side(side, name):
    n_in, n_out = len(side.arrays), len(side.out_shapes)

    def body(*refs):
        ins, outs = refs[:n_in], refs[n_in:n_in + n_out]
        send, recv = refs[n_in + n_out:]
        side.start(ins, outs, send, recv)
        side.finish(ins, outs, send, recv)

    return _comm_call(body, name, side.arrays, side.out_shapes, side.nsem, aliases=side.aliases)


def side_call(kern, side, *, name, grid, in_specs, out_specs, out_shape, scratch_shapes, args):
    if side is None:
        res = pl.pallas_call(kern, name=name, grid=grid, in_specs=in_specs, out_specs=out_specs, out_shape=out_shape,
                             scratch_shapes=scratch_shapes,
                             compiler_params=_params(("parallel",) + ("arbitrary",) * (len(grid) - 1)))(*args)
        return list(res), []
    n_in, n_out, n_scr = len(in_specs), len(out_specs), len(scratch_shapes)
    s_in, s_out = len(side.arrays), len(side.out_shapes)

    def body(*refs):
        ins, refs = refs[:n_in], refs[n_in:]
        side_ins, refs = refs[:s_in], refs[s_in:]
        outs, refs = refs[:n_out], refs[n_out:]
        side_outs, refs = refs[:s_out], refs[s_out:]
        scr, (send, recv) = refs[:n_scr], refs[n_scr:]
        first, last = None, None
        for axis, size in enumerate(grid):
            at0, at1 = pl.program_id(axis) == 0, pl.program_id(axis) == size - 1
            first = at0 if first is None else first & at0
            last = at1 if last is None else last & at1

        @pl.when(first)
        def _():
            side.start(side_ins, side_outs, send, recv)

        kern(*ins, *outs, *scr)

        @pl.when(last)
        def _():
            side.finish(side_ins, side_outs, send, recv)

    res = pl.pallas_call(
        body, name=name, grid=grid,
        in_specs=list(in_specs) + [ANY] * s_in, out_specs=list(out_specs) + [ANY] * s_out,
        out_shape=list(out_shape) + side.out_shapes,
        scratch_shapes=list(scratch_shapes) + [pltpu.SemaphoreType.DMA((side.nsem,)), pltpu.SemaphoreType.DMA((side.nsem,))],
        input_output_aliases={n_in + a: n_out + b for a, b in side.aliases.items()},
        compiler_params=_params(("arbitrary",) * len(grid)),
    )(*args, *side.arrays)
    return list(res[:n_out]), list(res[n_out:])


def gather_side(staged):
    n = len(staged)

    def plan(o_refs, send, recv):
        x, y, c = _pos()
        chips = [(1 - x, y), (x, 1 - y), (1 - x, 1 - y)]

        def part(u, chip, cc):
            half = staged[u].shape[1] // 2
            return o_refs[u].at[2 * chip[0] + chip[1], pl.ds(cc * half, half), :]

        first = [_remote(part(u, (x, y), c), part(u, (x, y), c), send, recv, 6 * u + j, (*chip, c))
                 for u in range(n) for j, chip in enumerate(chips)]
        landed = [_remote(part(u, chip, c), part(u, chip, c), send, recv, 6 * u + j, (x, y, c))
                  for u in range(n) for j, chip in enumerate(chips)]
        passed = [_remote(part(u, chip, c), part(u, chip, c), send, recv, 6 * u + 3 + j, (x, y, 1 - c))
                  for u in range(n) for j, chip in enumerate(chips)]
        handed = [_remote(part(u, chip, 1 - c), part(u, chip, 1 - c), send, recv, 6 * u + 3 + j, (x, y, c))
                  for u in range(n) for j, chip in enumerate(chips)]
        return first, landed, passed, handed

    def start(ins, outs, send, recv):
        for cp in plan(outs, send, recv)[0]:
            cp.start()

    def finish(ins, outs, send, recv):
        first, landed, passed, handed = plan(outs, send, recv)
        for got, fw in zip(landed, passed):
            got.wait_recv()
            fw.start()
        for got in handed:
            got.wait_recv()
        for cp in first + passed:
            cp.wait_send()

    outs = [jax.ShapeDtypeStruct(s.shape, s.dtype) for s in staged]
    return Side(staged, outs, {u: u for u in range(n)}, 6 * n, start, finish)


def swap_halves(gps, name):
    n = len(gps)

    def body(*refs):
        g_refs, r_refs = refs[:n], refs[n:2 * n]
        send, recv = refs[2 * n:]
        x, y, c = _pos()
        cps = []
        for u in range(n):
            half = gps[u].shape[1] // 2
            cps.append(_remote(g_refs[u].at[:, pl.ds((1 - c) * half, half), :], r_refs[u], send, recv, u, (x, y, 1 - c)))
        for cp in cps:
            cp.start()
        for cp in cps:
            cp.wait()

    outs = [jax.ShapeDtypeStruct((g.shape[0], g.shape[1] // 2, g.shape[2]), g.dtype) for g in gps]
    return _comm_call(body, name, gps, outs, n)


def scatter_side(parts):
    n = len(parts)

    def plan(p_refs, r_refs, send, recv):
        x, y, c = _pos()
        chips = [(1 - x, y), (x, 1 - y), (1 - x, 1 - y)]
        return [_remote(p_refs[u].at[2 * chip[0] + chip[1]], r_refs[u].at[j], send, recv, 3 * u + j, (*chip, c))
                for u in range(n) for j, chip in enumerate(chips)]

    def start(ins, outs, send, recv):
        for cp in plan(ins, outs, send, recv):
            cp.start()

    def finish(ins, outs, send, recv):
        for cp in plan(ins, outs, send, recv):
            cp.wait()

    outs = [jax.ShapeDtypeStruct((N_CHIPS - 1,) + p.shape[1:], p.dtype) for p in parts]
    return Side(parts, outs, {}, 3 * n, start, finish)


def join_halves(bufs):
    n = len(bufs)

    def body(*refs):
        o_refs = refs[n:2 * n]
        send, recv = refs[2 * n:]
        x, y, c = _pos()

        def rows(u, cc):
            half = bufs[u].shape[0] // 2
            return o_refs[u].at[pl.ds(cc * half, half), :]

        cps = [_remote(rows(u, c), rows(u, c), send, recv, u, (x, y, 1 - c)) for u in range(n)]
        for cp in cps:
            cp.start()
        for u in range(n):
            _remote(rows(u, 1 - c), rows(u, 1 - c), send, recv, u, (x, y, c)).wait_recv()
        for cp in cps:
            cp.wait_send()

    outs = [jax.ShapeDtypeStruct(b.shape, b.dtype) for b in bufs]
    return _comm_call(body, "join_halves", bufs, outs, n, aliases={u: u for u in range(n)})


def gather_small(sg, name):
    rows, cols = sg.shape

    def body(s_ref, o_ref, send, recv, lsem):
        x, y, c = _pos()
        me, sibling = (x, y, c), (x, y, 1 - c)
        chips = [(1 - x, y), (x, 1 - y), (1 - x, 1 - y)]

        def blk(px, py, pc):
            return o_ref.at[4 * px + 2 * py + pc]

        mine = pltpu.make_async_copy(s_ref, blk(*me), lsem)
        mine.start()
        first = [_remote(s_ref, blk(*me), send, recv, 0, sibling)]
        first += [_remote(s_ref, blk(*me), send, recv, 1 + j, (*chip, c)) for j, chip in enumerate(chips)]
        for cp in first:
            cp.start()
        passed = [_remote(blk(*chip, c), blk(*chip, c), send, recv, 4 + j, sibling) for j, chip in enumerate(chips)]
        for j, chip in enumerate(chips):
            _remote(blk(*chip, c), blk(*chip, c), send, recv, 1 + j, me).wait_recv()
            passed[j].start()
        _remote(blk(*sibling), blk(*sibling), send, recv, 0, me).wait_recv()
        for j, chip in enumerate(chips):
            _remote(blk(*chip, 1 - c), blk(*chip, 1 - c), send, recv, 4 + j, me).wait_recv()
        for cp in first + passed:
            cp.wait_send()
        mine.wait()

    return pl.pallas_call(
        body, name=name,
        out_shape=jax.ShapeDtypeStruct((N_DEV, rows, cols), sg.dtype),
        in_specs=[ANY], out_specs=ANY,
        scratch_shapes=[pltpu.SemaphoreType.DMA((N_DEV - 1,)), pltpu.SemaphoreType.DMA((N_DEV - 1,)), pltpu.SemaphoreType.DMA],
    )(sg)


def sum_cores(gp, theirs, core, chip, name):
    nch, rows, cols = gp.shape
    half = rows // 2
    tr = _pick(half, (256, 176, 128, 64))
    nb = half // tr

    def kern(idx_ref, g_ref, t_ref, own_ref, all_ref):
        k = pl.program_id(1)
        s = g_ref[...].astype(F32) + t_ref[...].astype(F32)
        all_ref[...] = s.astype(BF16)

        @pl.when(k == idx_ref[1])
        def _():
            own_ref[...] = s

    grid_spec = pltpu.PrefetchScalarGridSpec(
        num_scalar_prefetch=1, grid=(nb, nch),
        in_specs=[pl.BlockSpec((None, tr, cols), lambda i, k, idx: (k, idx[0] * nb + i, 0)),
                  pl.BlockSpec((None, tr, cols), lambda i, k, idx: (k, i, 0))],
        out_specs=[pl.BlockSpec((tr, cols), lambda i, k, idx: (i, 0)),
                   pl.BlockSpec((None, tr, cols), lambda i, k, idx: (k, i, 0))])
    return pl.pallas_call(
        kern, name=name, grid_spec=grid_spec,
        out_shape=[jax.ShapeDtypeStruct((half, cols), F32), jax.ShapeDtypeStruct((nch, half, cols), BF16)],
        compiler_params=_params(("parallel", "arbitrary")),
    )(jnp.stack([core, chip]).astype(jnp.int32), gp, theirs)


def sum_chips(own, others, core, name):
    half, cols = own.shape
    tr = _pick(half, (256, 176, 128, 64))
    nb = half // tr

    def kern(idx_ref, o_ref, a_ref, b_ref, c_ref, out_ref):
        out_ref[...] = ((o_ref[...] + a_ref[...].astype(F32)) + b_ref[...].astype(F32)) + c_ref[...].astype(F32)

    grid_spec = pltpu.PrefetchScalarGridSpec(
        num_scalar_prefetch=1, grid=(nb,),
        in_specs=[pl.BlockSpec((tr, cols), lambda i, idx: (i, 0))] +
                 [pl.BlockSpec((None, tr, cols), lambda i, idx, j=j: (j, i, 0)) for j in range(N_CHIPS - 1)],
        out_specs=pl.BlockSpec((tr, cols), lambda i, idx: (idx[0] * nb + i, 0)))
    return pl.pallas_call(
        kern, name=name, grid_spec=grid_spec,
        out_shape=jax.ShapeDtypeStruct((2 * half, cols), F32),
        compiler_params=_params(("parallel",)),
    )(jnp.reshape(core, (1,)).astype(jnp.int32), own, others, others, others)


def small_update(gath, w, m, v, name):
    def fn(*vs):
        g = vs[0]
        for t in vs[1:N_DEV]:
            g = g + t
        wv, mv, vv = vs[N_DEV:]
        m2 = ADAM_B1 * mv + (1.0 - ADAM_B1) * g
        v2 = ADAM_B2 * vv + (1.0 - ADAM_B2) * (g * g)
        m_hat = m2 / (1.0 - ADAM_B1 ** ADAM_STEP)
        v_hat = v2 / (1.0 - ADAM_B2 ** ADAM_STEP)
        return g, -ADAM_LR * (m_hat / (jnp.sqrt(v_hat) + ADAM_EPS) + ADAM_WD * wv), m2, v2

    c = w.shape[1]
    ins = [(gath[k], "row") for k in range(N_DEV)] + [(w, "row"), (m, "row"), (v, "row")]
    return rowwise(fn, ins, [(c, F32)] * 4, tr=w.shape[0] // 2, name=name)


_MIX = {0: [("sb_w_qkv", 0), ("sb_w_o", 0)], 1: [("gm_w_in", 0), ("gm_w_out", 0)],
        2: [("ssm_w_in", 0), ("ssm_w_out", 0)], 3: [("sb_w_qkv", 1), ("sb_w_o", 1)]}
_FFN = {i: [("ffn_w_gu", i), ("ffn_w_down", i)] for i in range(4)}
GATHER_FIRST = _MIX[0][:1]
GATHER_AT = {"sb_attn_0": _MIX[0][1:] + _FFN[0] + _MIX[1] + _FFN[1],
             "ffn_gu_0": _FFN[2][:1], "ffn_down_0": _FFN[2][1:], "ffn_gu_1": _MIX[2][:1], "ffn_down_1": _MIX[2][1:],
             "ssm_scan_2": _MIX[3] + _FFN[3]}
SCATTER_AT = {"ssm_dscan_2": _FFN[3] + _MIX[3] + _FFN[2], "sb_dattn_0": _MIX[2] + _FFN[1] + _MIX[1] + _FFN[0]}
SCATTER_LAST = _MIX[0]


class _Plan:
    def __init__(self, ins, core, chip):
        self.core, self.chip = core, chip
        self.staged = {(n, l): stage_shard(ins[n][l], chip, f"stage_{n}_{l}")
                       for n in MATRICES for l in range(ins[n].shape[0])}
        self.full = {n: [None] * ins[n].shape[0] for n in MATRICES}
        self.ready = {}
        self.parts = {}
        self.halves = {}
        self.swaps = 0
        self._fill(GATHER_FIRST, run_side(gather_side([self.staged[u] for u in GATHER_FIRST]), "gather_first"))

    def _fill(self, units, gathered):
        for (n, l), g in zip(units, gathered):
            if n in COLUMN_BLOCKS:
                self.full[n][l] = g
            elif n == "ssm_w_in":
                self.full[n][l] = jnp.concatenate([g[k] for k in range(N_CHIPS)], axis=1)
            else:
                self.full[n][l] = g.reshape(-1, g.shape[-1])

    def _prepare(self, units):
        gps = [self.ready[u] for u in units]
        theirs = swap_halves(gps, f"swap_halves_{self.swaps}")
        self.swaps += 1
        for (n, l), g, t in zip(units, gps, theirs):
            self.parts[(n, l)] = sum_cores(g, t, self.core, self.chip, f"sum_cores_{n}_{l}")

    def _reduce(self, units, others):
        for (n, l), other in zip(units, others):
            self.halves[(n, l)] = sum_chips(self.parts[(n, l)][0], other, self.core, f"sum_chips_{n}_{l}")

    def side(self, tag):
        if tag in GATHER_AT:
            return gather_side([self.staged[u] for u in GATHER_AT[tag]])
        if tag in SCATTER_AT:
            self._prepare(SCATTER_AT[tag])
            return scatter_side([self.parts[u][1] for u in SCATTER_AT[tag]])
        return None

    def done(self, tag, results):
        if tag in GATHER_AT:
            self._fill(GATHER_AT[tag], results)
        else:
            self._reduce(SCATTER_AT[tag], results)

    def grads_ready(self, grads):
        for (n, l), g in grads.items():
            if n in COLUMN_BLOCKS:
                self.ready[(n, l)] = g
            elif n == "ssm_w_in":
                self.ready[(n, l)] = jnp.stack(jnp.split(g, N_CHIPS, axis=1))
            else:
                self.ready[(n, l)] = g.reshape(N_CHIPS, -1, g.shape[-1])

    def shard_grads(self):
        self._prepare(SCATTER_LAST)
        self._reduce(SCATTER_LAST, run_side(scatter_side([self.parts[u][1] for u in SCATTER_LAST]), "scatter_last"))
        units = sorted(self.halves)
        return dict(zip(units, join_halves([self.halves[u] for u in units])))


def _step(ins):
    x, target = ins["x"][0], ins["loss_target"][0]
    core = lax.axis_index("c")
    chip = 2 * lax.axis_index("x") + lax.axis_index("y")

    def lane_pad(v):
        return jnp.pad(v, ((0, 0), (0, PACK_COLS - v.shape[1])))

    vec_rows = [ins["ssm_conv_w"][0], ins["ssm_conv_b"], lane_pad(ins["ssm_norm_gain"])]
    blk = jnp.concatenate(vec_rows + [jnp.zeros((SUBLANES - 6, PACK_COLS), F32)], axis=0)
    per_chip = gather_small(blk, "gather_vectors")[0::2]
    ngw = ins["ssm_norm_gain"].shape[1]
    full = {
        "ssm_conv_w": jnp.concatenate([per_chip[k, 0:4] for k in range(N_CHIPS)], axis=1)[None],
        "ssm_conv_b": jnp.concatenate([per_chip[k, 4:5] for k in range(N_CHIPS)], axis=1),
        "ssm_norm_gain": jnp.concatenate([per_chip[k, 5:6, :ngw] for k in range(N_CHIPS)], axis=1),
    }

    plan = _Plan(ins, core, chip)
    full.update(plan.full)
    for n in REPLICATED:
        full[n] = ins[n]

    loss, dx, grads = local_step(x, target, full, plan)
    loss = lax.psum(loss, ALL_AXES)
    gshards = plan.shard_grads()

    small_shapes = [ins[n].shape for n in REPLICATED]
    vec_shapes = [grads[n].shape for n in EXACT]
    vec_pack = _pack([grads[n] for n in EXACT], F32, SUBLANES)
    gath = gather_small(jnp.concatenate([_pack([grads[n] for n in REPLICATED], F32, SUBLANES), vec_pack], axis=0),
                        "gather_small")
    packed = [jnp.concatenate([_pack([ins[pre + n] for n in REPLICATED], F32, SUBLANES), jnp.zeros_like(vec_pack)], axis=0)
              for pre in ("", "m_", "v_")]
    res = small_update(gath, *packed, name="small_update")
    nrep = res[0].shape[0] - vec_pack.shape[0]
    small = [dict(zip(REPLICATED, _unpack(r[:nrep].reshape(-1), small_shapes))) for r in res]
    vec_g = dict(zip(EXACT, _unpack(res[0][nrep:].reshape(-1), vec_shapes)))

    out_g, out_d, out_m, out_v = {}, {}, {}, {}
    for n in REPLICATED:
        out_g[n], out_d[n], out_m[n], out_v[n] = (s[n] for s in small)
    for n in SHARDED:
        shp = ins[n].shape
        if n in EXACT:
            g = lax.dynamic_slice_in_dim(vec_g[n], chip * shp[-1], shp[-1], axis=vec_g[n].ndim - 1)
        else:
            g = jnp.stack([gshards[(n, l)] for l in range(shp[0])])
        two = (math.prod(shp[:-1]), shp[-1])
        d2, m2, v2 = adamw(ins[n].reshape(two), g.reshape(two), ins["m_" + n].reshape(two),
                           ins["v_" + n].reshape(two), f"adamw_{n}")
        out_g[n], out_d[n], out_m[n], out_v[n] = g, d2.reshape(shp), m2.reshape(shp), v2.reshape(shp)
    return (loss, dx[None], *[out_g[n] for n in WEIGHTS], *[out_d[n] for n in WEIGHTS],
            *[out_m[n] for n in WEIGHTS], *[out_v[n] for n in WEIGHTS])


def kernel(x, mix_norm, ffn_norm, sb_w_qkv, sb_q_gain, sb_k_gain, sb_w_o, gm_w_in, gm_b_in, gm_v_gain, gm_w_s, gm_b_s, gm_w_out, ssm_w_in, ssm_conv_w, ssm_conv_b, ssm_dt_bias, ssm_a_log, ssm_d, ssm_norm_gain, ssm_w_out, ffn_w_gu, ffn_w_down, loss_target, m_mix_norm, m_ffn_norm, m_sb_w_qkv, m_sb_q_gain, m_sb_k_gain, m_sb_w_o, m_gm_w_in, m_gm_b_in, m_gm_v_gain, m_gm_w_s, m_gm_b_s, m_gm_w_out, m_ssm_w_in, m_ssm_conv_w, m_ssm_conv_b, m_ssm_dt_bias, m_ssm_a_log, m_ssm_d, m_ssm_norm_gain, m_ssm_w_out, m_ffn_w_gu, m_ffn_w_down, v_mix_norm, v_ffn_norm, v_sb_w_qkv, v_sb_q_gain, v_sb_k_gain, v_sb_w_o, v_gm_w_in, v_gm_b_in, v_gm_v_gain, v_gm_w_s, v_gm_b_s, v_gm_w_out, v_ssm_w_in, v_ssm_conv_w, v_ssm_conv_b, v_ssm_dt_bias, v_ssm_a_log, v_ssm_d, v_ssm_norm_gain, v_ssm_w_out, v_ffn_w_gu, v_ffn_w_down):
    return _step(dict(locals()))
```

```python
import functools
import math

import jax
import jax.numpy as jnp
from jax import lax
from jax.experimental import pallas as pl
from jax.experimental.pallas import tpu as pltpu

F32 = jnp.float32
BF16 = jnp.bfloat16
EPS = 1e-6
LANES = 128
SUBLANES = 8
VMEM_LIMIT = 56 * 1024 * 1024
HEAD = 64
CHUNK = 128
SB_TQ, SB_TK = 256, 256
SSD_SUB = 8
SB_DEAD = -110.0
SB_UNSEEN = -1e30
ADAM_LR, ADAM_B1, ADAM_B2, ADAM_EPS, ADAM_WD, ADAM_STEP = 0.001, 0.9, 0.999, 1e-08, 0.01, 10
MESH_ID = pl.DeviceIdType.MESH
ALL_AXES = ("x", "y", "c")


def _params(sem):
    return pltpu.CompilerParams(dimension_semantics=sem, vmem_limit_bytes=VMEM_LIMIT)


def _pick(n, cands):
    for c in cands:
        if n % c == 0:
            return c
    return n


def _dot(a, b, dims=((1,), (0,))):
    return lax.dot_general(a, b, (dims, ((), ())), preferred_element_type=F32)


def _dot_nt(a, b):
    return _dot(a, b, ((1,), (1,)))


def _dot_tn(a, b):
    return _dot(a, b, ((0,), (0,)))


def _split2(x):
    hi = x.astype(BF16)
    lo = (x - hi.astype(F32)).astype(BF16)
    return hi, lo


def _dot_x2(x, m):
    hi, lo = _split2(x)
    return _dot(hi, m) + _dot(lo, m)


def _dot_x3_left(m, x):
    h1 = x.astype(BF16)
    r1 = x - h1.astype(F32)
    h2 = r1.astype(BF16)
    h3 = (r1 - h2.astype(F32)).astype(BF16)
    return _dot(m, h1) + _dot(m, h2) + _dot(m, h3)


def _sigmoid(x):
    return 1.0 / (1.0 + jnp.exp(-x))


def _softplus(x):
    return jnp.maximum(x, 0.0) + jnp.log(1.0 + jnp.exp(-jnp.abs(x)))


def _colsum(x):
    return jnp.sum(x, axis=0, keepdims=True)


def _rowsum(x):
    return jnp.sum(x, axis=1, keepdims=True)


def _iota2(shape, dim):
    return lax.broadcasted_iota(jnp.int32, shape, dim)


MM_VMEM_BUDGET = 40 * 1024 * 1024
MM_STEP_US = 0.35
MM_HBM_BYTES_PER_US = 3.0e6
MM_VMEM_BYTES_PER_US = 1.5e6
MM_FLOPS_PER_US = 9.0e8
MXU_DIM = 256


def _mm_tiles(m, n, kk, wn, wk, a_bytes, b_bytes, has_add):
    def divisors(total, cands):
        got = [c for c in cands if total % c == 0 and c <= total]
        return got or [total]

    best = None
    for tm in divisors(m, (1024, 512, 256, 128)):
        for tn in divisors(wn, (1024, 768, 1408, 512, 256, 128)):
            for tk in divisors(wk, (4096, 2816, 2048, 1408, 1024, 768, 512, 256, 128)):
                nk = kk // tk
                vmem = 2 * (tm * tk * a_bytes + tk * tn * b_bytes + tm * tn * 4 * (2 if has_add else 1))
                vmem += tm * tn * 4 if nk > 1 else 0
                if vmem > MM_VMEM_BUDGET:
                    continue
                steps = (m // tm) * (n // tn) * nk
                a_reads = 1 if nk == 1 else n // tn
                traffic = m * kk * a_bytes * a_reads + kk * n * b_bytes * (m // tm) + m * n * 4
                fill = min(1.0, tn / MXU_DIM) * min(1.0, tm / MXU_DIM)
                compute = 2.0 * m * n * kk / (MM_FLOPS_PER_US * fill)
                cost = steps * MM_STEP_US + max(compute, traffic / MM_HBM_BYTES_PER_US)
                if nk > 1:
                    cost += steps * tm * tn * 8 / MM_VMEM_BYTES_PER_US
                if best is None or cost < best[0]:
                    best = (cost, tm, tn, tk)
    return best[1:]


def mm(a, b, *, ta=False, tb=False, add=None, bias=None, a_chunks=False, b_chunks=False, out_chunks=False,
       out_dtype=F32, name, side=None):
    wa = None
    if a_chunks:
        m, wa = a.shape[1], a.shape[2]
        kk = a.shape[0] * wa
    elif ta:
        kk, m = a.shape
    else:
        m, kk = a.shape
    nch, wide = 1, None
    if b_chunks:
        nch, rows_b, wide = b.shape
        kb, n = (rows_b, nch * wide) if not tb else (nch * wide, rows_b)
    elif tb:
        n, kb = b.shape
    else:
        kb, n = b.shape
    wide_o = n // N_CHIPS if out_chunks else None
    assert kk == kb, (a.shape, b.shape, ta, tb)
    has_add, has_bias = add is not None, bias is not None
    wk = wide if (wide and tb) else kk
    wn = wide if (wide and not tb) else n
    tm, tn, tk = _mm_tiles(m, n, kk, math.gcd(wn, wide_o) if wide_o else wn, math.gcd(wk, wa) if wa else wk,
                           a.dtype.itemsize, b.dtype.itemsize, has_add)
    nk = kk // tk
    dims = ((0 if ta else 1,), (1 if tb else 0,))

    def kern(*refs):
        a_ref, b_ref = refs[0], refs[1]
        rest = list(refs[2:])
        add_ref = rest.pop(0) if has_add else None
        bias_ref = rest.pop(0) if has_bias else None
        o_ref = rest[0]
        part = _dot(a_ref[...].astype(BF16), b_ref[...].astype(BF16), dims)

        def finish(r):
            if has_add:
                r = r + add_ref[...]
            if has_bias:
                r = r + bias_ref[...]
            o_ref[...] = r.astype(out_dtype)

        if nk == 1:
            finish(part)
        else:
            acc_ref = rest[1]
            k = pl.program_id(2)

            @pl.when(k == 0)
            def _():
                acc_ref[...] = part

            @pl.when((k > 0) & (k < nk - 1))
            def _():
                acc_ref[...] += part

            @pl.when(k == nk - 1)
            def _():
                finish(acc_ref[...] + part)

    if a_chunks:
        per_a = wa // tk
        a_spec = pl.BlockSpec((None, tm, tk), lambda i, j, k: (k // per_a, i, k % per_a))
    elif ta:
        a_spec = pl.BlockSpec((tk, tm), lambda i, j, k: (k, i))
    else:
        a_spec = pl.BlockSpec((tm, tk), lambda i, j, k: (i, k))
    if b_chunks and tb:
        per = wide // tk
        b_spec = pl.BlockSpec((None, tn, tk), lambda i, j, k: (k // per, j, k % per))
    elif b_chunks:
        per = wide // tn
        b_spec = pl.BlockSpec((None, tk, tn), lambda i, j, k: (j // per, k, j % per))
    elif tb:
        b_spec = pl.BlockSpec((tn, tk), lambda i, j, k: (j, k))
    else:
        b_spec = pl.BlockSpec((tk, tn), lambda i, j, k: (k, j))
    if out_chunks:
        per_o = wide_o // tn
        out_spec = pl.BlockSpec((None, tm, tn), lambda i, j, k: (j // per_o, i, j % per_o))
        out_shape = jax.ShapeDtypeStruct((N_CHIPS, m, wide_o), out_dtype)
    else:
        out_spec = pl.BlockSpec((tm, tn), lambda i, j, k: (i, j))
        out_shape = jax.ShapeDtypeStruct((m, n), out_dtype)
    in_specs, args = [a_spec, b_spec], [a, b]
    if has_add:
        in_specs.append(pl.BlockSpec((tm, tn), lambda i, j, k: (i, j)))
        args.append(add)
    if has_bias:
        in_specs.append(pl.BlockSpec((1, tn), lambda i, j, k: (0, j)))
        args.append(bias)
    (out,), side_outs = side_call(
        kern, side,
        name=name,
        grid=(m // tm, n // tn, nk),
        in_specs=in_specs,
        out_specs=[out_spec],
        out_shape=[out_shape],
        scratch_shapes=[pltpu.VMEM((tm, tn), F32)] if nk > 1 else [],
        args=args)
    return out if side is None else (out, side_outs)


def mm_hooked(plan, a, b, *, name, **kw):
    side = plan.side(name)
    if side is None:
        return mm(a, b, name=name, **kw)
    out, side_outs = mm(a, b, name=name, side=side, **kw)
    plan.done(name, side_outs)
    return out


def rowwise(fn, ins, outs, accs=(), *, tr, name):
    rows = [a for a, kind in ins if kind == "row"][0].shape[0]
    tr = min(tr, rows)
    assert rows % tr == 0 and tr % SUBLANES == 0, (rows, tr)
    n = rows // tr
    n_in, n_out = len(ins), len(outs)
    kinds = [kind for _, kind in ins]

    def kern(*refs):
        i = pl.program_id(0)
        vals = []
        for ref, kind in zip(refs[:n_in], kinds):
            v = ref[...]
            if kind == "prev":
                v = v * (i > 0).astype(v.dtype)
            elif kind == "next":
                v = v * (i < n - 1).astype(v.dtype)
            vals.append(v)
        res = fn(*vals)
        for ref, r in zip(refs[n_in:n_in + n_out], res[:n_out]):
            ref[...] = r.astype(ref.dtype)
        if accs:
            acc_refs = refs[n_in + n_out:]

            @pl.when(i == 0)
            def _():
                for ref in acc_refs:
                    ref[...] = jnp.zeros_like(ref)

            for ref, r in zip(acc_refs, res[n_out:]):
                ref[...] += r

    in_specs = []
    for a, kind in ins:
        if kind == "row":
            in_specs.append(pl.BlockSpec((tr, a.shape[1]), lambda i: (i, 0)))
        elif kind == "full":
            in_specs.append(pl.BlockSpec(a.shape, lambda i, nd=a.ndim: (0,) * nd))
        elif kind == "prev":
            in_specs.append(pl.BlockSpec((SUBLANES, a.shape[1]),
                                         lambda i: (jnp.maximum(i * (tr // SUBLANES) - 1, 0), 0)))
        else:
            in_specs.append(pl.BlockSpec((SUBLANES, a.shape[1]),
                                         lambda i: (jnp.minimum((i + 1) * (tr // SUBLANES), rows // SUBLANES - 1), 0)))
    out_specs = [pl.BlockSpec((tr, c), lambda i: (i, 0)) for c, _ in outs]
    out_specs += [pl.BlockSpec((r, c), lambda i: (0, 0)) for r, c in accs]
    out_shape = [jax.ShapeDtypeStruct((rows, c), dt) for c, dt in outs]
    out_shape += [jax.ShapeDtypeStruct((r, c), F32) for r, c in accs]
    res = pl.pallas_call(
        kern,
        name=name,
        grid=(n,),
        in_specs=in_specs,
        out_specs=out_specs,
        out_shape=out_shape,
        compiler_params=_params(("arbitrary",) if accs else ("parallel",)),
    )(*[a for a, _ in ins])
    return res


def rms_fwd(x, g, name):
    def fn(xv, gv):
        r = lax.rsqrt(jnp.mean(xv * xv, axis=1, keepdims=True) + EPS)
        return (xv * r * gv,)

    return rowwise(fn, [(x, "row"), (g, "full")], [(x.shape[1], BF16)], tr=512, name=name)[0]


def rms_bwd(x, g, dy, dres, name):
    def fn(xv, gv, dyv, drv):
        r = lax.rsqrt(jnp.mean(xv * xv, axis=1, keepdims=True) + EPS)
        xh = xv * r
        dyg = dyv * gv
        dx = drv + r * (dyg - xh * jnp.mean(dyg * xh, axis=1, keepdims=True))
        return dx, dx, _colsum(dyv * xh)

    c = x.shape[1]
    dx, dxb, dg = rowwise(fn, [(x, "row"), (g, "full"), (dy, "row"), (dres, "row")], [(c, F32), (c, BF16)], [(1, c)],
                          tr=256, name=name)
    return (dx, dxb), dg


def ffn_up(h, wgu, name, side=None):
    s, d = h.shape
    nch, _, w = wgu.shape
    half = nch // 2
    tm = _pick(s, (512, 256, 128))

    def kern(h_ref, wg_ref, wu_ref, gu_ref, a_ref):
        hv = h_ref[...]
        g = _dot(hv, wg_ref[...])
        u = _dot(hv, wu_ref[...])
        gu_ref[0] = g.astype(BF16)
        gu_ref[1] = u.astype(BF16)
        a_ref[...] = (g * _sigmoid(g) * u).astype(BF16)

    return side_call(
        kern, side, name=name, grid=(s // tm, half),
        in_specs=[pl.BlockSpec((tm, d), lambda i, j: (i, 0)),
                  pl.BlockSpec((None, d, w), lambda i, j: (j, 0, 0)),
                  pl.BlockSpec((None, d, w), lambda i, j: (j + half, 0, 0))],
        out_specs=[pl.BlockSpec((2, tm, w), lambda i, j: (0, i, j)), pl.BlockSpec((tm, w), lambda i, j: (i, j))],
        out_shape=[jax.ShapeDtypeStruct((2, s, half * w), BF16), jax.ShapeDtypeStruct((s, half * w), BF16)],
        scratch_shapes=[], args=(h, wgu, wgu))


def ffn_dact(dxb, wdown, gu, name):
    s, d = dxb.shape
    hid = wdown.shape[0]
    tm = _pick(s, (512, 256, 128))
    tn = _pick(hid, (1408, 512, 256, 128))

    def kern(dx_ref, w_ref, gu_ref, o_ref):
        da = _dot_nt(dx_ref[...], w_ref[...])
        g, u = gu_ref[0].astype(F32), gu_ref[1].astype(F32)
        sg = _sigmoid(g)
        o_ref[0] = (da * u * sg * (1.0 + g * (1.0 - sg))).astype(BF16)
        o_ref[1] = (da * g * sg).astype(BF16)

    return pl.pallas_call(
        kern, name=name, grid=(s // tm, hid // tn),
        in_specs=[pl.BlockSpec((tm, d), lambda i, j: (i, 0)), pl.BlockSpec((tn, d), lambda i, j: (j, 0)),
                  pl.BlockSpec((2, tm, tn), lambda i, j: (0, i, j))],
        out_specs=pl.BlockSpec((2, tm, tn), lambda i, j: (0, i, j)),
        out_shape=jax.ShapeDtypeStruct((2, s, hid), BF16),
        compiler_params=_params(("parallel", "parallel")),
    )(dxb, wdown, gu)


def loss_and_grad(y, t, name):
    d = y.shape[1]

    def fn(yv, tv):
        e = yv - tv
        part = jnp.sum(_colsum(e * e), axis=1, keepdims=True) * (0.5 / d)
        dy = e * (1.0 / d)
        return dy, dy, jnp.broadcast_to(part, (SUBLANES, LANES))

    dy, dyb, acc = rowwise(fn, [(y, "row"), (t, "row")], [(d, F32), (d, BF16)], [(SUBLANES, LANES)], tr=512, name=name)
    return acc[0, 0], (dy, dyb)


def adamw(w, g, m, v, name):
    def fn(wv, gv, mv, vv):
        m2 = ADAM_B1 * mv + (1.0 - ADAM_B1) * gv
        v2 = ADAM_B2 * vv + (1.0 - ADAM_B2) * (gv * gv)
        m_hat = m2 / (1.0 - ADAM_B1 ** ADAM_STEP)
        v_hat = v2 / (1.0 - ADAM_B2 ** ADAM_STEP)
        delta = -ADAM_LR * (m_hat / (jnp.sqrt(v_hat) + ADAM_EPS) + ADAM_WD * wv)
        return delta, m2, v2

    rows, c = w.shape
    tr = _pick(rows, (256, 128, 64, 32, 16, 8)) if rows % SUBLANES == 0 else rows
    if rows % SUBLANES:
        return _whole(fn, [w, g, m, v], [(w.shape, F32)] * 3, name=name)
    return rowwise(fn, [(w, "row"), (g, "row"), (m, "row"), (v, "row")], [(c, F32)] * 3, tr=tr, name=name)


def _whole(fn, ins, outs, *, name):
    n_in = len(ins)

    def kern(*refs):
        res = fn(*[r[...] for r in refs[:n_in]])
        for ref, r in zip(refs[n_in:], res):
            ref[...] = r.astype(ref.dtype)

    return pl.pallas_call(
        kern,
        name=name,
        out_shape=[jax.ShapeDtypeStruct(s, dt) for s, dt in outs],
        compiler_params=pltpu.CompilerParams(vmem_limit_bytes=VMEM_LIMIT),
    )(*ins)


def ffn_fwd(x, g, wgu, wdown, tag, plan):
    h = rms_fwd(x, g, f"ffn_rms_{tag}")
    gu, a = _hooked(plan, f"ffn_gu_{tag}", ffn_up, h, wgu)
    xn = mm_hooked(plan, a, wdown, add=x, name=f"ffn_down_{tag}")
    return xn, (x, h, gu, a)


def ffn_bwd(dxn, saved, g, wgu, wdown, tag):
    x, h, gu, a = saved
    dxn, dxb = dxn
    dwdown = mm(a, dxb, ta=True, out_dtype=BF16, name=f"ffn_dwdown_{tag}")
    dgu = ffn_dact(dxb, wdown, gu, f"ffn_dact_{tag}")
    dh = mm(dgu, wgu, tb=True, a_chunks=True, b_chunks=True, name=f"ffn_dh_{tag}")
    dwgu = mm(h, dgu, ta=True, b_chunks=True, out_dtype=BF16, out_chunks=True, name=f"ffn_dwgu_{tag}")
    dx, dg = rms_bwd(x, g, dh, dxn, f"ffn_drms_{tag}")
    return dx, dg, dwgu, dwdown


def _head_blockdiag(c):
    i = jnp.arange(c) // HEAD
    return (i[:, None] == i[None, :]).astype(BF16)


def _head_sums(x, bd):
    return jnp.concatenate([_dot_x2(x[:, g * LANES:(g + 1) * LANES], bd) for g in range(x.shape[1] // LANES)], axis=1)


def qknorm_fwd(qkv, qg, kg, bd, name):
    d = qkv.shape[1] // 3
    scale = 1.0 / math.sqrt(HEAD)

    def fn(v, qgv, kgv, bdv):
        v = v.astype(F32)
        q, k, vv = v[:, :d], v[:, d:2 * d], v[:, 2 * d:]
        rq = lax.rsqrt(_head_sums(q * q, bdv) * (1.0 / HEAD) + EPS)
        rk = lax.rsqrt(_head_sums(k * k, bdv) * (1.0 / HEAD) + EPS)
        return q * rq * qgv * scale, k * rk * kgv, vv

    return rowwise(fn, [(qkv, "row"), (qg, "full"), (kg, "full"), (bd, "full")],
                   [(d, BF16), (d, BF16), (d, BF16)], tr=256, name=name)


def qknorm_bwd(qkv, dqs, dkn, dv, qg, kg, bd, name):
    d = qkv.shape[1] // 3
    scale = 1.0 / math.sqrt(HEAD)

    def one(xv, gv, dyv, bdv):
        r = lax.rsqrt(_head_sums(xv * xv, bdv) * (1.0 / HEAD) + EPS)
        xh = xv * r
        dyg = dyv * gv
        dx = r * (dyg - xh * (_head_sums(dyg * xh, bdv) * (1.0 / HEAD)))
        return dx, _colsum(dyv * xh)

    def fn(v, dqv, dkv, dvv, qgv, kgv, bdv):
        v = v.astype(F32)
        q, k = v[:, :d], v[:, d:2 * d]
        dq, dqg = one(q, qgv, dqv * scale, bdv)
        dk, dkg = one(k, kgv, dkv, bdv)
        return jnp.concatenate([dq, dk, dvv], axis=1), dqg, dkg

    return rowwise(fn, [(qkv, "row"), (dqs, "row"), (dkn, "row"), (dv, "row"), (qg, "full"), (kg, "full"), (bd, "full")],
                   [(3 * d, BF16)], [(1, d), (1, d)], tr=256, name=name)


def _sb_tile(qh, k, mask, tri_gt):
    z = _dot_nt(qh, k)
    sp = jnp.log(1.0 + jnp.exp(-jnp.abs(z)))
    lb = jnp.minimum(z, 0.0) - sp
    l1 = jnp.where(mask, lb - z, 0.0)
    suf = _dot(l1.astype(BF16), tri_gt)
    return lb, l1, suf


def _sb_tri(tk):
    i = jnp.arange(tk)
    return jnp.stack([i[:, None] > i[None, :], i[:, None] < i[None, :]]).astype(BF16)


def _sb_setup(tq, tk):
    row, col = _iota2((tq, tk), 0), _iota2((tq, tk), 1)
    lane = _iota2((1, LANES), 1)
    halves = [(lane < HEAD).astype(BF16), (lane >= HEAD).astype(BF16)]
    lane_q = _iota2((tq, LANES), 1) + jnp.minimum(_iota2((tq, LANES), 0), 0)
    return row, col, halves, lane_q


def sb_attn_fwd(qs, kn, vb, tri, name, side=None):
    s, d = qs.shape
    tq, tk = min(SB_TQ, s), min(SB_TK, s)
    nq = s // tq
    assert s // tk <= LANES and s % tq == 0 and s % tk == 0

    def kern(q_ref, k_ref, v_ref, tri_ref, o_ref, rs_ref, acc_ref):
        i = pl.program_id(1)
        row, col, halves, lane_q = _sb_setup(tq, tk)
        q = q_ref[...]
        qh = [q * hm for hm in halves]
        acc_ref[...] = jnp.zeros_like(acc_ref)
        rs_ref[...] = jnp.full(rs_ref.shape, SB_UNSEEN, F32)
        nkb = (i + 1) * (tq // tk)

        def more(st):
            return (st[0] < nkb) & (st[1] > SB_DEAD)

        def step(st):
            n, r = st[0], list(st[2:])
            kb = nkb - 1 - n
            ks = pl.multiple_of(kb * tk, tk)
            k = k_ref[pl.ds(ks, tk), :]
            v = v_ref[pl.ds(ks, tk), :]
            mask = col < row + (i * tq - kb * tk)
            at_kb = lane_q == kb
            for hh in range(2):
                lb, l1, suf = _sb_tile(qh[hh], k, mask, tri_ref[0])
                w = jnp.where(mask, jnp.exp(lb + suf + r[hh]), 0.0)
                acc_ref[...] += _dot(w.astype(BF16), v * halves[hh])
                rs_ref[hh] = jnp.where(at_kb, r[hh], rs_ref[hh])
                r[hh] = r[hh] + _rowsum(l1)
            return (n + 1, jnp.maximum(jnp.max(r[0]), jnp.max(r[1])), r[0], r[1])

        z1 = jnp.zeros((tq, 1), F32)
        lax.while_loop(more, step, (jnp.int32(0), jnp.float32(0.0), z1, z1))
        o_ref[...] = acc_ref[...].astype(BF16)

    nh2 = d // LANES
    return side_call(
        kern, side,
        name=name,
        grid=(nh2, nq),
        in_specs=[pl.BlockSpec((tq, LANES), lambda h, i: (i, h)),
                  pl.BlockSpec((s, LANES), lambda h, i: (0, h)),
                  pl.BlockSpec((s, LANES), lambda h, i: (0, h)),
                  pl.BlockSpec((2, tk, tk), lambda h, i: (0, 0, 0))],
        out_specs=[pl.BlockSpec((tq, LANES), lambda h, i: (i, h)),
                   pl.BlockSpec((None, 2, tq, LANES), lambda h, i: (h, 0, i, 0))],
        out_shape=[jax.ShapeDtypeStruct((s, d), BF16), jax.ShapeDtypeStruct((nh2, 2, s, LANES), F32)],
        scratch_shapes=[pltpu.VMEM((tq, LANES), F32)],
        args=(qs, kn, vb, tri))


def sb_attn_bwd(qs, kn, vb, rsave, do, tri, name, side=None):
    s, d = qs.shape
    tq, tk = min(SB_TQ, s), min(SB_TK, s)
    nq = s // tq

    def kern(q_ref, k_ref, v_ref, rs_ref, do_ref, tri_ref, dq_ref, dk_ref, dv_ref):
        i = pl.program_id(1)

        @pl.when(i == 0)
        def _():
            dk_ref[...] = jnp.zeros_like(dk_ref)
            dv_ref[...] = jnp.zeros_like(dv_ref)

        row, col, halves, lane_q = _sb_setup(tq, tk)
        q = q_ref[...]
        qh = [q * hm for hm in halves]
        dov = do_ref[...].astype(BF16)
        doh = [dov * hm for hm in halves]
        dq_ref[...] = jnp.zeros_like(dq_ref)
        nkb = (i + 1) * (tq // tk)
        top = jnp.maximum(jnp.max(rs_ref[0], axis=0, keepdims=True), jnp.max(rs_ref[1], axis=0, keepdims=True))
        dead = (top <= SB_DEAD) & (_iota2((1, LANES), 1) < nkb)
        kstart = jnp.minimum(jnp.sum(dead.astype(F32)).astype(jnp.int32), nkb)

        def step(kb, ep):
            ep = list(ep)
            ks = pl.multiple_of(kb * tk, tk)
            k = k_ref[pl.ds(ks, tk), :]
            v = v_ref[pl.ds(ks, tk), :]
            mask = col < row + (i * tq - kb * tk)
            at_kb = lane_q == kb
            for hh in range(2):
                lb, l1, suf = _sb_tile(qh[hh], k, mask, tri_ref[0])
                r = _rowsum(jnp.where(at_kb, rs_ref[hh], 0.0))
                lbm = jnp.where(mask, lb, SB_UNSEEN)
                w = jnp.exp(lbm + suf + r)
                e = _dot_nt(doh[hh], v) * w
                pe = ep[hh] + _dot(e.astype(BF16), tri_ref[1])
                beta = jnp.exp(lbm)
                dz = (e - beta * (e + pe)).astype(BF16)
                dq_ref[...] += _dot(dz, k * halves[hh])
                dk_ref[pl.ds(ks, tk), :] += _dot_tn(dz, qh[hh])
                dv_ref[pl.ds(ks, tk), :] += _dot_tn(w.astype(BF16), doh[hh])
                ep[hh] = ep[hh] + _rowsum(e)
            return tuple(ep)

        z1 = jnp.zeros((tq, 1), F32)
        lax.fori_loop(kstart, nkb, step, (z1, z1))

    nh2 = d // LANES
    return side_call(
        kern, side,
        name=name,
        grid=(nh2, nq),
        in_specs=[pl.BlockSpec((tq, LANES), lambda h, i: (i, h)),
                  pl.BlockSpec((s, LANES), lambda h, i: (0, h)),
                  pl.BlockSpec((s, LANES), lambda h, i: (0, h)),
                  pl.BlockSpec((None, 2, tq, LANES), lambda h, i: (h, 0, i, 0)),
                  pl.BlockSpec((tq, LANES), lambda h, i: (i, h)),
                  pl.BlockSpec((2, tk, tk), lambda h, i: (0, 0, 0))],
        out_specs=[pl.BlockSpec((tq, LANES), lambda h, i: (i, h)),
                   pl.BlockSpec((s, LANES), lambda h, i: (0, h)),
                   pl.BlockSpec((s, LANES), lambda h, i: (0, h))],
        out_shape=[jax.ShapeDtypeStruct((s, d), F32)] * 3,
        scratch_shapes=[],
        args=(qs, kn, vb, rsave, do, tri))


def _hooked(plan, tag, call, *args):
    side = plan.side(tag)
    outs, side_outs = call(*args, tag, side)
    if side is not None:
        plan.done(tag, side_outs)
    return outs


def sb_fwd(x, g, wqkv, qg, kg, wo, bd, tag, plan):
    h = rms_fwd(x, g, f"sb_rms_{tag}")
    qkv = mm(h, wqkv, b_chunks=True, out_dtype=BF16, name=f"sb_qkv_{tag}")
    qs, kn, vb = qknorm_fwd(qkv, qg, kg, bd, f"sb_qknorm_{tag}")
    o, rsave = _hooked(plan, f"sb_attn_{tag}", sb_attn_fwd, qs, kn, vb, _sb_tri(min(SB_TK, x.shape[0])))
    xn = mm(o, wo(), add=x, name=f"sb_out_{tag}")
    return xn, (x, h, qkv, qs, kn, vb, rsave, o)


def sb_bwd(dxn, saved, g, wqkv, qg, kg, wo, bd, tag, plan):
    x, h, qkv, qs, kn, vb, rsave, o = saved
    dxn, dxb = dxn
    do = mm(dxb, wo, tb=True, name=f"sb_do_{tag}")
    dwo = mm(o, dxb, ta=True, out_dtype=BF16, name=f"sb_dwo_{tag}")
    dqs, dkn, dv = _hooked(plan, f"sb_dattn_{tag}", sb_attn_bwd, qs, kn, vb, rsave, do, _sb_tri(min(SB_TK, x.shape[0])))
    dqkv, dqg, dkg = qknorm_bwd(qkv, dqs, dkn, dv, qg, kg, bd, f"sb_dqknorm_{tag}")
    dh = mm(dqkv, wqkv, tb=True, b_chunks=True, name=f"sb_dh_{tag}")
    dwqkv = mm(h, dqkv, ta=True, out_dtype=BF16, out_chunks=True, name=f"sb_dwqkv_{tag}")
    dx, dg = rms_bwd(x, g, dh, dxn, f"sb_drms_{tag}")
    nh = dqg.shape[1] // HEAD
    return dx, dg, dwqkv, dqg.reshape(nh, HEAD).sum(0), dkg.reshape(nh, HEAD).sum(0), dwo


def _gelu(x):
    return 0.5 * x * (1.0 + lax.erf(x * (1.0 / math.sqrt(2.0))))


def _gelu_grad(x):
    return 0.5 * (1.0 + lax.erf(x * (1.0 / math.sqrt(2.0)))) + x * jnp.exp(-0.5 * x * x) * (1.0 / math.sqrt(2.0 * math.pi))


def gm_act_fwd(pre, vg, name):
    half = pre.shape[1] // 2

    def fn(p, vgv):
        p = p.astype(F32)
        u = _gelu(p[:, :half])
        v = _gelu(p[:, half:])
        r = lax.rsqrt(jnp.mean(v * v, axis=1, keepdims=True) + EPS)
        return u, v * r * vgv

    return rowwise(fn, [(pre, "row"), (vg, "full")], [(half, F32), (half, BF16)], tr=256, name=name)


def gm_act_bwd(pre, du, dvn, vg, name):
    half = pre.shape[1] // 2

    def fn(p, duv, dvnv, vgv):
        p = p.astype(F32)
        pu, pv = p[:, :half], p[:, half:]
        v = _gelu(pv)
        r = lax.rsqrt(jnp.mean(v * v, axis=1, keepdims=True) + EPS)
        vh = v * r
        dyg = dvnv * vgv
        dv = r * (dyg - vh * jnp.mean(dyg * vh, axis=1, keepdims=True))
        dpre = jnp.concatenate([duv * _gelu_grad(pu), dv * _gelu_grad(pv)], axis=1)
        return dpre, _colsum(dvnv * vh), _colsum(dpre)

    return rowwise(fn, [(pre, "row"), (du, "row"), (dvn, "row"), (vg, "full")],
                   [(2 * half, BF16)], [(1, half), (1, 2 * half)], tr=256, name=name)


def gm_spatial_fwd(u, vn, wc, bst, name):
    s, c = u.shape
    t = CHUNK
    ng = c // LANES

    def kern(u_ref, v_ref, w_ref, b_ref, o_ref):
        for g in range(ng):
            sl = slice(g * LANES, (g + 1) * LANES)
            mixed = _dot(w_ref[g], v_ref[:, sl]) + b_ref[:, sl]
            o_ref[:, sl] = (u_ref[:, sl] * mixed).astype(BF16)

    return pl.pallas_call(
        kern,
        name=name,
        grid=(s // t,),
        in_specs=[pl.BlockSpec((t, c), lambda i: (i, 0)), pl.BlockSpec((t, c), lambda i: (i, 0)),
                  pl.BlockSpec(wc.shape, lambda i: (0, 0, 0)), pl.BlockSpec(bst.shape, lambda i: (0, 0))],
        out_specs=pl.BlockSpec((t, c), lambda i: (i, 0)),
        out_shape=jax.ShapeDtypeStruct((s, c), BF16),
        compiler_params=_params(("parallel",)),
    )(u, vn, wc, bst)


def gm_spatial_bwd(dgate, u, vn, wc, bst, name):
    s, c = u.shape
    t = CHUNK
    ng = c // LANES

    def kern(dg_ref, u_ref, v_ref, w_ref, b_ref, du_ref, dv_ref, dw_ref, db_ref):
        i = pl.program_id(0)

        @pl.when(i == 0)
        def _():
            dw_ref[...] = jnp.zeros_like(dw_ref)
            db_ref[...] = jnp.zeros_like(db_ref)

        for g in range(ng):
            sl = slice(g * LANES, (g + 1) * LANES)
            vg = v_ref[:, sl]
            dgv = dg_ref[:, sl]
            mixed = _dot(w_ref[g], vg) + b_ref[:, sl]
            du_ref[:, sl] = dgv * mixed
            dmix = dgv * u_ref[:, sl]
            dmb = dmix.astype(BF16)
            dv_ref[:, sl] = _dot_tn(w_ref[g], dmb)
            dw_ref[g] += _dot_nt(dmb, vg)
            db_ref[:, sl] += dmix

    return pl.pallas_call(
        kern,
        name=name,
        grid=(s // t,),
        in_specs=[pl.BlockSpec((t, c), lambda i: (i, 0))] * 3 +
                 [pl.BlockSpec(wc.shape, lambda i: (0, 0, 0)), pl.BlockSpec(bst.shape, lambda i: (0, 0))],
        out_specs=[pl.BlockSpec((t, c), lambda i: (i, 0)), pl.BlockSpec((t, c), lambda i: (i, 0)),
                   pl.BlockSpec(wc.shape, lambda i: (0, 0, 0)), pl.BlockSpec(bst.shape, lambda i: (0, 0))],
        out_shape=[jax.ShapeDtypeStruct((s, c), F32), jax.ShapeDtypeStruct((s, c), F32),
                   jax.ShapeDtypeStruct(wc.shape, F32), jax.ShapeDtypeStruct(bst.shape, F32)],
        compiler_params=_params(("arbitrary",)),
    )(dgate, u, vn, wc, bst)


def gm_fwd(x, g, w_in, b_in, vg, wc, bst, w_out, tag):
    h = rms_fwd(x, g, f"gm_rms_{tag}")
    pre = mm(h, w_in, bias=b_in, b_chunks=True, out_dtype=BF16, name=f"gm_in_{tag}")
    u, vn = gm_act_fwd(pre, vg, f"gm_act_{tag}")
    gate = gm_spatial_fwd(u, vn, wc, bst, f"gm_spatial_{tag}")
    xn = mm(gate, w_out, add=x, name=f"gm_out_{tag}")
    return xn, (x, h, pre, u, vn, gate)


def gm_bwd(dxn, saved, g, w_in, vg, wc, bst, w_out, tag):
    x, h, pre, u, vn, gate = saved
    dxn, dxb = dxn
    dgate = mm(dxb, w_out, tb=True, name=f"gm_dgate_{tag}")
    dwout = mm(gate, dxb, ta=True, out_dtype=BF16, name=f"gm_dwout_{tag}")
    du, dvn, dws, dbst = gm_spatial_bwd(dgate, u, vn, wc, bst, f"gm_dspatial_{tag}")
    dpre, dvg, dbin = gm_act_bwd(pre, du, dvn, vg, f"gm_dact_{tag}")
    dh = mm(dpre, w_in, tb=True, b_chunks=True, name=f"gm_dh_{tag}")
    dwin = mm(h, dpre, ta=True, out_dtype=BF16, out_chunks=True, name=f"gm_dwin_{tag}")
    dx, dg = rms_bwd(x, g, dh, dxn, f"gm_drms_{tag}")
    ng = wc.shape[0]
    dws = jnp.where(jnp.tril(jnp.ones((CHUNK, CHUNK), bool)), dws, 0.0)
    dbs = dbst.reshape(CHUNK, ng, LANES).sum(-1).T
    return dx, dg, dwin, dbin, dvg, dws, dbs, dwout


def _conv_taps(xv, prev):
    cat = jnp.concatenate([prev, xv], axis=0)
    return [pltpu.roll(cat, sh, 0)[SUBLANES:] for sh in (3, 2, 1)] + [xv]


def conv_fwd(xbc, ws, b, d_inner, name):
    c = xbc.shape[1]
    nst = (c - d_inner) // 2

    def fn(xv, prev, w0, w1, w2, w3, bv):
        taps = _conv_taps(xv, prev)
        pre = bv + w0 * taps[0] + w1 * taps[1] + w2 * taps[2] + w3 * taps[3]
        out = pre * _sigmoid(pre)
        return out[:, :d_inner], out[:, d_inner:d_inner + nst], out[:, d_inner + nst:]

    return rowwise(fn, [(xbc, "row"), (xbc, "prev")] + [(w, "full") for w in ws] + [(b, "full")],
                   [(d_inner, F32), (nst, F32), (nst, F32)], tr=256, name=name)


def conv_bwd_pre(xbc, ws, b, dxs_a, dxs_b, db_m, dc_m, name):
    c = xbc.shape[1]

    def fn(xv, prev, w0, w1, w2, w3, bv, da, db2, dbm, dcm):
        taps = _conv_taps(xv, prev)
        pre = bv + w0 * taps[0] + w1 * taps[1] + w2 * taps[2] + w3 * taps[3]
        sg = _sigmoid(pre)
        dout = jnp.concatenate([da + db2, dbm, dcm], axis=1)
        dpre = dout * sg * (1.0 + pre * (1.0 - sg))
        return (dpre,) + tuple(_colsum(dpre * tp) for tp in taps) + (_colsum(dpre),)

    return rowwise(fn, [(xbc, "row"), (xbc, "prev")] + [(w, "full") for w in ws] +
                   [(b, "full"), (dxs_a, "row"), (dxs_b, "row"), (db_m, "row"), (dc_m, "row")],
                   [(c, F32)], [(1, c)] * 5, tr=256, name=name)


def conv_bwd_in(dpre, ws, name):
    c = dpre.shape[1]

    def fn(dv, nxt, w0, w1, w2, w3):
        cat = jnp.concatenate([dv, nxt], axis=0)
        n = cat.shape[0]
        up = [pltpu.roll(cat, n - sh, 0)[:dv.shape[0]] for sh in (1, 2, 3)]
        return (w3 * dv + w2 * up[0] + w1 * up[1] + w0 * up[2],)

    return rowwise(fn, [(dpre, "row"), (dpre, "next")] + [(w, "full") for w in ws], [(c, BF16)], tr=256, name=name)[0]


def ssd_pre(dtr, bias, alog, name):
    def fn(d, bv, al, tri):
        dt = _softplus(d + bv)
        a = dt * (-jnp.exp(al))
        return dt, _dot_x3_left(tri, a)

    tri = jnp.tril(jnp.ones((CHUNK, CHUNK), BF16))
    return rowwise(fn, [(dtr, "row"), (bias, "full"), (alog, "full"), (tri, "full")],
                   [(LANES, F32), (LANES, F32)], tr=CHUNK, name=name)


def _ssd_layouts(v, ngroups, hpg):
    s = v.shape[0]
    col = v[:, :ngroups * hpg].T.reshape(ngroups, hpg, s, 1)
    return jnp.broadcast_to(col, (ngroups, hpg, s, LANES))


def _ssd_rowform(acum, ngroups, hpg):
    s = acum.shape[0]
    nc = s // CHUNK
    a = acum[:, :ngroups * hpg].reshape(nc, CHUNK, ngroups, hpg).transpose(2, 0, 3, 1)
    last = jnp.broadcast_to(a[..., CHUNK - 1:], a.shape)
    return jnp.concatenate([a, last], axis=2)


def ssd_chunk_fwd(xs, bm, cm, col_a, col_dt, rowf, name, side=None):
    s, d_inner = xs.shape
    ln = CHUNK
    nc = s // ln
    nsub = _pick(nc, (SSD_SUB, 2, 1))
    rows = nsub * ln
    ng, hpg = col_a.shape[0], col_a.shape[1]
    gw = d_inner // ng
    assert gw == hpg * HEAD and gw % LANES == 0 and bm.shape[1] == ng * LANES

    def kern(x_ref, b_ref, c_ref, ca_ref, cd_ref, rf_ref, y_ref, hp_ref, h_scr):
        @pl.when(pl.program_id(1) == 0)
        def _():
            h_scr[...] = jnp.zeros_like(h_scr)

        causal = _iota2((ln, ln), 0) >= _iota2((ln, ln), 1)
        lane = _iota2((1, LANES), 1)
        for sc in range(nsub):
            rs = slice(sc * ln, (sc + 1) * ln)
            bb = b_ref[rs, :].astype(BF16)
            cbf = c_ref[rs, :].astype(BF16)
            cb = _dot_nt(cbf, bb)
            ys = [jnp.zeros((ln, LANES), F32) for _ in range(gw // LANES)]
            for r in range(hpg):
                j, hf = divmod(r, LANES // HEAD)
                mh = ((lane >= HEAD * hf) & (lane < HEAD * (hf + 1))).astype(F32)
                ac = ca_ref[r, rs, :]
                ar = rf_ref[sc, pl.ds(r, 1), :]
                aend = rf_ref[sc, pl.ds(4 + r, 1), :]
                dm = jnp.exp(jnp.minimum(ac - ar, 0.0))
                m = jnp.where(causal, cb * dm, 0.0).astype(BF16)
                xdt = x_ref[rs, j * LANES:(j + 1) * LANES] * cd_ref[r, rs, :] * mh
                h = h_scr[r]
                hp_ref[sc, r] = h
                ys[j] = ys[j] + _dot(m, xdt.astype(BF16)) + _dot_nt(cbf, h.astype(BF16)) * jnp.exp(ac)
                dte = jnp.exp(aend - ac)
                h_scr[r] = jnp.exp(aend) * h + _dot_tn((xdt * dte).astype(BF16), bb)
            for j in range(gw // LANES):
                y_ref[rs, j * LANES:(j + 1) * LANES] = ys[j]

    colspec = pl.BlockSpec((None, hpg, rows, LANES), lambda g, c: (g, 0, c, 0))
    return side_call(
        kern, side,
        name=name,
        grid=(ng, nc // nsub),
        in_specs=[pl.BlockSpec((rows, gw), lambda g, c: (c, g)),
                  pl.BlockSpec((rows, LANES), lambda g, c: (c, g)),
                  pl.BlockSpec((rows, LANES), lambda g, c: (c, g)),
                  colspec, colspec,
                  pl.BlockSpec((None, nsub, 8, LANES), lambda g, c: (g, c, 0, 0))],
        out_specs=[pl.BlockSpec((rows, gw), lambda g, c: (c, g)),
                   pl.BlockSpec((None, nsub, hpg, LANES, LANES), lambda g, c: (g, c, 0, 0, 0))],
        out_shape=[jax.ShapeDtypeStruct((s, d_inner), F32),
                   jax.ShapeDtypeStruct((ng, nc, hpg, LANES, LANES), F32)],
        scratch_shapes=[pltpu.VMEM((hpg, LANES, LANES), F32)],
        args=(xs, bm, cm, col_a, col_dt, rowf))


def ssd_chunk_bwd(xs, bm, cm, col_a, col_dt, rowf, hprev, dy, name, side=None):
    s, d_inner = xs.shape
    ln = CHUNK
    nc = s // ln
    nsub = _pick(nc, (SSD_SUB, 2, 1))
    rows = nsub * ln
    ng, hpg = col_a.shape[0], col_a.shape[1]
    gw = d_inner // ng

    def kern(x_ref, b_ref, c_ref, ca_ref, cd_ref, rf_ref, hp_ref, dy_ref,
             dx_ref, db_ref, dc_ref, ddt_ref, da_ref, dh_scr):
        @pl.when(pl.program_id(1) == 0)
        def _():
            dh_scr[...] = jnp.zeros_like(dh_scr)

        row, col = _iota2((ln, ln), 0), _iota2((ln, ln), 1)
        causal = row >= col
        tri_ge = (col >= row).astype(BF16)
        ones = jnp.ones((ln, LANES), BF16)
        lane = _iota2((1, LANES), 1)
        last_row = (_iota2((ln, 1), 0) == ln - 1).astype(F32)
        for sc in reversed(range(nsub)):
            rs = slice(sc * ln, (sc + 1) * ln)
            bb = b_ref[rs, :].astype(BF16)
            cbf = c_ref[rs, :].astype(BF16)
            cb = _dot_nt(cbf, bb)
            dcb = jnp.zeros((ln, ln), F32)
            d_b = jnp.zeros((ln, LANES), F32)
            d_c = jnp.zeros((ln, LANES), F32)
            dxs = [jnp.zeros((ln, LANES), F32) for _ in range(gw // LANES)]
            for r in range(hpg):
                j, hf = divmod(r, LANES // HEAD)
                mh = ((lane >= HEAD * hf) & (lane < HEAD * (hf + 1))).astype(F32)
                ac = ca_ref[r, rs, :]
                dt = cd_ref[r, rs, :]
                ar = rf_ref[sc, pl.ds(r, 1), :]
                aend = rf_ref[sc, pl.ds(4 + r, 1), :]
                dm = jnp.where(causal, jnp.exp(jnp.minimum(ac - ar, 0.0)), 0.0)
                m = cb * dm
                mb = m.astype(BF16)
                xp = x_ref[rs, j * LANES:(j + 1) * LANES]
                xdt = xp * dt * mh
                xdtb = xdt.astype(BF16)
                dyp = dy_ref[rs, j * LANES:(j + 1) * LANES] * mh
                dypb = dyp.astype(BF16)
                h = hp_ref[sc, r]
                hb = h.astype(BF16)
                dh = dh_scr[r]
                dhb = dh.astype(BF16)
                e_in = jnp.exp(ac)
                dte = jnp.exp(aend - ac)
                eend = jnp.exp(aend)
                d_m = _dot_nt(dypb, xdtb)
                dcb = dcb + d_m * dm
                gm = d_m * m
                yoff_pre = _dot_nt(cbf, hb)
                bdh = _dot_nt(bb, dhb)
                dxdt = _dot_tn(mb, dypb) + bdh * dte
                t1 = _rowsum(xdt * bdh) * dte
                gh, gl = _split2(gm)
                dacum = (_rowsum(gm) - (_dot_tn(gh, ones) + _dot_tn(gl, ones))
                         + _rowsum(dyp * yoff_pre) * e_in - t1)
                end_term = _colsum(t1) + eend * jnp.sum(_colsum(dh * h), axis=1, keepdims=True)
                dacum = dacum + last_row * end_term
                da_ref[r, rs, :] = _dot_x3_left(tri_ge, dacum)
                ddt_ref[r, rs, :] = jnp.broadcast_to(_rowsum(dxdt * xp), (ln, LANES))
                dxs[j] = dxs[j] + dxdt * dt
                d_b = d_b + _dot((xdt * dte).astype(BF16), dhb)
                dye = (dyp * e_in).astype(BF16)
                d_c = d_c + _dot(dye, hb)
                dh_scr[r] = eend * dh + _dot_tn(dye, cbf)
            dcbb = dcb.astype(BF16)
            dc_ref[rs, :] = d_c + _dot(dcbb, bb)
            db_ref[rs, :] = d_b + _dot_tn(dcbb, cbf)
            for j in range(gw // LANES):
                dx_ref[rs, j * LANES:(j + 1) * LANES] = dxs[j]

    rev = nc // nsub - 1
    colspec = pl.BlockSpec((None, hpg, rows, LANES), lambda g, c: (g, 0, rev - c, 0))
    return side_call(
        kern, side,
        name=name,
        grid=(ng, nc // nsub),
        in_specs=[pl.BlockSpec((rows, gw), lambda g, c: (rev - c, g)),
                  pl.BlockSpec((rows, LANES), lambda g, c: (rev - c, g)),
                  pl.BlockSpec((rows, LANES), lambda g, c: (rev - c, g)),
                  colspec, colspec,
                  pl.BlockSpec((None, nsub, 8, LANES), lambda g, c: (g, rev - c, 0, 0)),
                  pl.BlockSpec((None, nsub, hpg, LANES, LANES), lambda g, c: (g, rev - c, 0, 0, 0)),
                  pl.BlockSpec((rows, gw), lambda g, c: (rev - c, g))],
        out_specs=[pl.BlockSpec((rows, gw), lambda g, c: (rev - c, g)),
                   pl.BlockSpec((rows, LANES), lambda g, c: (rev - c, g)),
                   pl.BlockSpec((rows, LANES), lambda g, c: (rev - c, g)),
                   colspec, colspec],
        out_shape=[jax.ShapeDtypeStruct((s, d_inner), F32),
                   jax.ShapeDtypeStruct(bm.shape, F32), jax.ShapeDtypeStruct(cm.shape, F32),
                   jax.ShapeDtypeStruct(col_a.shape, F32), jax.ShapeDtypeStruct(col_a.shape, F32)],
        scratch_shapes=[pltpu.VMEM((hpg, LANES, LANES), F32)],
        args=(xs, bm, cm, col_a, col_dt, rowf, hprev, dy))


def gnorm_fwd(y, xs, z, dexp, gain, ngroups, name):
    c = y.shape[1]
    gw = c // ngroups

    def fn(yv, xv, zv, dv, gv):
        yg = (yv + xv * dv) * (zv * _sigmoid(zv))
        outs = []
        for k in range(ngroups):
            t = yg[:, k * gw:(k + 1) * gw]
            outs.append(t * lax.rsqrt(jnp.mean(t * t, axis=1, keepdims=True) + EPS))
        return (jnp.concatenate(outs, axis=1) * gv,)

    return rowwise(fn, [(y, "row"), (xs, "row"), (z, "row"), (dexp, "full"), (gain, "full")], [(c, BF16)], tr=256, name=name)[0]


def gnorm_bwd(dn, y, xs, z, dexp, gain, ngroups, name):
    c = y.shape[1]
    gw = c // ngroups

    def fn(dnv, yv, xv, zv, dv, gv):
        yd = yv + xv * dv
        sg = _sigmoid(zv)
        sz = zv * sg
        yg = yd * sz
        dng = dnv * gv
        dyg, yh = [], []
        for k in range(ngroups):
            sl = slice(k * gw, (k + 1) * gw)
            t = yg[:, sl]
            r = lax.rsqrt(jnp.mean(t * t, axis=1, keepdims=True) + EPS)
            th = t * r
            dyg.append(r * (dng[:, sl] - th * jnp.mean(dng[:, sl] * th, axis=1, keepdims=True)))
            yh.append(th)
        dyg = jnp.concatenate(dyg, axis=1)
        yh = jnp.concatenate(yh, axis=1)
        dyd = dyg * sz
        dz = dyg * yd * (sg * (1.0 + zv * (1.0 - sg)))
        return dyd, dyd * dv, dz, _colsum(dyd * xv), _colsum(dnv * yh)

    return rowwise(fn, [(dn, "row"), (y, "row"), (xs, "row"), (z, "row"), (dexp, "full"), (gain, "full")],
                   [(c, F32), (c, F32), (c, BF16)], [(1, c), (1, c)], tr=256, name=name)


def ssd_post(ddt, da, dt, dtr, bias, alog, name):
    def fn(ddtv, dav, dtv, dtrv, bv, al):
        a_neg = -jnp.exp(al)
        ddtr = (ddtv + dav * a_neg) * _sigmoid(dtrv + bv)
        return ddtr, _colsum(ddtr), _colsum(dav * dtv) * a_neg

    return rowwise(fn, [(ddt, "row"), (da, "row"), (dt, "row"), (dtr, "row"), (bias, "full"), (alog, "full")],
                   [(LANES, BF16)], [(1, LANES), (1, LANES)], tr=512, name=name)


def _from_colform(v, s):
    ng, hpg = v.shape[0], v.shape[1]
    flat = v[..., 0].reshape(ng * hpg, s).T
    return jnp.pad(flat, ((0, 0), (0, LANES - ng * hpg)))


def ssm_fwd(x, g, p, tag, plan):
    ng, hpg, d_inner = p["ng"], p["hpg"], p["d_inner"]
    h = rms_fwd(x, g, f"ssm_rms_{tag}")
    z = mm(h, p["w_z"], name=f"ssm_inz_{tag}")
    xbc = mm(h, p["w_xbc"], name=f"ssm_inx_{tag}")
    dtr = mm(h, p["w_dt"], name=f"ssm_indt_{tag}")
    xs, bm, cm = conv_fwd(xbc, p["conv_w"], p["conv_b"], d_inner, f"ssm_conv_{tag}")
    dt, acum = ssd_pre(dtr, p["dt_bias"], p["a_log"], f"ssm_pre_{tag}")
    col_a, col_dt = _ssd_layouts(acum, ng, hpg), _ssd_layouts(dt, ng, hpg)
    rowf = _ssd_rowform(acum, ng, hpg)
    y, hprev = _hooked(plan, f"ssm_scan_{tag}", ssd_chunk_fwd, xs, bm, cm, col_a, col_dt, rowf)
    n = gnorm_fwd(y, xs, z, p["d_exp"], p["norm_gain"], ng, f"ssm_gnorm_{tag}")
    xn = mm(n, p["w_out"], add=x, name=f"ssm_out_{tag}")
    return xn, (x, h, z, xbc, dtr, xs, bm, cm, dt, col_a, col_dt, rowf, y, hprev, n)


def ssm_bwd(dxn, saved, g, p, tag, plan):
    x, h, z, xbc, dtr, xs, bm, cm, dt, col_a, col_dt, rowf, y, hprev, n = saved
    ng, hpg, d_inner = p["ng"], p["hpg"], p["d_inner"]
    s = x.shape[0]
    dxn, dxb = dxn
    dn = mm(dxb, p["w_out"], tb=True, name=f"ssm_dn_{tag}")
    dwout = mm(n, dxb, ta=True, out_dtype=BF16, name=f"ssm_dwout_{tag}")
    dy, dxs_skip, dz, dd_lane, dgain = gnorm_bwd(dn, y, xs, z, p["d_exp"], p["norm_gain"], ng, f"ssm_dgnorm_{tag}")
    dxs, dbm, dcm, ddt_c, da_c = _hooked(plan, f"ssm_dscan_{tag}", ssd_chunk_bwd, xs, bm, cm, col_a, col_dt, rowf, hprev, dy)
    ddtr, dbias, dalog = ssd_post(_from_colform(ddt_c, s), _from_colform(da_c, s), dt, dtr,
                                  p["dt_bias"], p["a_log"], f"ssm_post_{tag}")
    res = conv_bwd_pre(xbc, p["conv_w"], p["conv_b"], dxs, dxs_skip, dbm, dcm, f"ssm_dconv_{tag}")
    dpre, dconv_w, dconv_b = res[0], jnp.concatenate(res[1:5], axis=0), res[5]
    dxbc = conv_bwd_in(dpre, p["conv_w"], f"ssm_dconvin_{tag}")
    dh = mm(dz, p["w_z"], tb=True, name=f"ssm_dhz_{tag}")
    dh = mm(dxbc, p["w_xbc"], tb=True, add=dh, name=f"ssm_dhx_{tag}")
    dh = mm(ddtr, p["w_dt"], tb=True, add=dh, name=f"ssm_dhdt_{tag}")
    dwz = mm(h, dz, ta=True, out_dtype=BF16, name=f"ssm_dwz_{tag}")
    dwxbc = mm(h, dxbc, ta=True, out_dtype=BF16, name=f"ssm_dwxbc_{tag}")
    dwdt = mm(h, ddtr, ta=True, out_dtype=BF16, name=f"ssm_dwdt_{tag}")
    dx, dg = rms_bwd(x, g, dh, dxn, f"ssm_drms_{tag}")
    nh = ng * hpg
    dwin = jnp.concatenate([dwz, dwxbc, dwdt[:, :nh]], axis=1)
    dd = dd_lane.reshape(nh, HEAD).sum(-1)
    return dx, dg, dict(w_in=dwin, conv_w=dconv_w, conv_b=dconv_b, dt_bias=dbias[0, :nh], a_log=dalog[0, :nh],
                        d=dd, norm_gain=dgain, w_out=dwout)


def local_step(x, target, w, plan):
    d = x.shape[1]
    depth = w["mix_norm"].shape[0]
    bd = _head_blockdiag(LANES)
    tril = jnp.tril(jnp.ones((CHUNK, CHUNK), bool))
    ssm_heads = w["ssm_dt_bias"].shape[1]
    d_inner = w["ssm_norm_gain"].shape[1]
    ng = w["ssm_norm_gain"].shape[1] // 256
    nstate = CHUNK

    def pad_lanes(v):
        return jnp.pad(v, ((0, 0), (0, LANES - v.shape[1])))

    def ssm_params(j):
        w_in = w["ssm_w_in"][j]
        cw = w["ssm_conv_w"][j]
        return dict(ng=ng, hpg=ssm_heads // ng, d_inner=d_inner,
                    w_z=w_in[:, :d_inner], w_xbc=w_in[:, d_inner:d_inner + d_inner + 2 * ng * nstate],
                    w_dt=pad_lanes(w_in[:, 2 * d_inner + 2 * ng * nstate:]),
                    conv_w=[cw[k:k + 1] for k in range(cw.shape[0])], conv_b=w["ssm_conv_b"][j:j + 1],
                    dt_bias=pad_lanes(w["ssm_dt_bias"][j:j + 1]), a_log=pad_lanes(w["ssm_a_log"][j:j + 1]),
                    d_exp=jnp.repeat(w["ssm_d"][j], HEAD)[None, :], norm_gain=w["ssm_norm_gain"][j:j + 1],
                    w_out=w["ssm_w_out"][j])

    def gm_params(j):
        wc = jnp.where(tril, w["gm_w_s"][j], 0.0).astype(BF16)
        bst = jnp.repeat(w["gm_b_s"][j].T, LANES, axis=1)
        return wc, bst

    def sb_gains(j):
        nh = d // HEAD
        return jnp.tile(w["sb_q_gain"][j], nh)[None, :], jnp.tile(w["sb_k_gain"][j], nh)[None, :]

    saved = []
    cur = x
    for i in range(depth):
        kind, j = i % 3, i // 3
        gmix = w["mix_norm"][i:i + 1]
        if kind == 0:
            qg, kg = sb_gains(j)
            cur, sv = sb_fwd(cur, gmix, w["sb_w_qkv"][j], qg, kg, lambda j=j: w["sb_w_o"][j], bd, f"{i}", plan)
        elif kind == 1:
            wc, bst = gm_params(j)
            cur, sv = gm_fwd(cur, gmix, w["gm_w_in"][j], w["gm_b_in"][j:j + 1], w["gm_v_gain"][j:j + 1], wc, bst,
                             w["gm_w_out"][j], f"{i}")
        else:
            cur, sv = ssm_fwd(cur, gmix, ssm_params(j), f"{i}", plan)
        cur, sv2 = ffn_fwd(cur, w["ffn_norm"][i:i + 1], w["ffn_w_gu"][i], w["ffn_w_down"][i], f"{i}", plan)
        saved.append((sv, sv2))

    loss, dcur = loss_and_grad(cur, target, "loss")

    grads = {k: [None] * len(v) for k, v in w.items()}
    for i in reversed(range(depth)):
        kind, j = i % 3, i // 3
        sv, sv2 = saved[i]
        gmix = w["mix_norm"][i:i + 1]
        dcur, dgf, dwgu, dwdown = ffn_bwd(dcur, sv2, w["ffn_norm"][i:i + 1], w["ffn_w_gu"][i], w["ffn_w_down"][i], f"{i}")
        grads["ffn_norm"][i], grads["ffn_w_gu"][i], grads["ffn_w_down"][i] = dgf[0], dwgu, dwdown
        plan.grads_ready({("ffn_w_gu", i): dwgu, ("ffn_w_down", i): dwdown})
        if kind == 0:
            qg, kg = sb_gains(j)
            dcur, dg, dwqkv, dqg, dkg, dwo = sb_bwd(dcur, sv, gmix, w["sb_w_qkv"][j], qg, kg, w["sb_w_o"][j], bd, f"{i}", plan)
            grads["sb_w_qkv"][j], grads["sb_q_gain"][j], grads["sb_k_gain"][j], grads["sb_w_o"][j] = dwqkv, dqg, dkg, dwo
        elif kind == 1:
            wc, bst = gm_params(j)
            dcur, dg, dwin, dbin, dvg, dws, dbs, dwout = gm_bwd(dcur, sv, gmix, w["gm_w_in"][j], w["gm_v_gain"][j:j + 1],
                                                                 wc, bst, w["gm_w_out"][j], f"{i}")
            grads["gm_w_in"][j], grads["gm_b_in"][j], grads["gm_v_gain"][j] = dwin, dbin[0], dvg[0]
            grads["gm_w_s"][j], grads["gm_b_s"][j], grads["gm_w_out"][j] = dws, dbs, dwout
        else:
            dcur, dg, gs = ssm_bwd(dcur, sv, gmix, ssm_params(j), f"{i}", plan)
            grads["ssm_w_in"][j], grads["ssm_conv_w"][j], grads["ssm_conv_b"][j] = gs["w_in"], gs["conv_w"], gs["conv_b"][0]
            grads["ssm_dt_bias"][j], grads["ssm_a_log"][j], grads["ssm_d"][j] = gs["dt_bias"], gs["a_log"], gs["d"]
            grads["ssm_norm_gain"][j], grads["ssm_w_out"][j] = gs["norm_gain"][0], gs["w_out"]
        grads["mix_norm"][i] = dg[0]
        mixer = {0: ("sb_w_qkv", "sb_w_o"), 1: ("gm_w_in", "gm_w_out"), 2: ("ssm_w_in", "ssm_w_out")}[kind]
        plan.grads_ready({(n, j): grads[n][j] for n in mixer})
    grads = {k: (v if k in MATRICES else jnp.stack(v)) for k, v in grads.items()}
    return loss, dcur[0], grads


WEIGHTS = ["mix_norm", "ffn_norm", "sb_w_qkv", "sb_q_gain", "sb_k_gain", "sb_w_o", "gm_w_in", "gm_b_in", "gm_v_gain",
           "gm_w_s", "gm_b_s", "gm_w_out", "ssm_w_in", "ssm_conv_w", "ssm_conv_b", "ssm_dt_bias", "ssm_a_log", "ssm_d",
           "ssm_norm_gain", "ssm_w_out", "ffn_w_gu", "ffn_w_down"]
SHARDED = {"sb_w_qkv": 2, "sb_w_o": 1, "gm_w_in": 2, "gm_w_out": 1, "ssm_w_in": 2, "ssm_conv_w": 2, "ssm_conv_b": 1,
           "ssm_norm_gain": 1, "ssm_w_out": 1, "ffn_w_gu": 2, "ffn_w_down": 1}
EXACT = ("ssm_conv_w", "ssm_conv_b", "ssm_norm_gain")
MATRICES = tuple(n for n in SHARDED if n not in EXACT)
COLUMN_BLOCKS = ("sb_w_qkv", "gm_w_in", "ffn_w_gu")
REPLICATED = [n for n in WEIGHTS if n not in SHARDED]
N_CHIPS = 4
N_DEV = 8
PACK_COLS = 1024


def _pack(pieces, dtype, align):
    flat = jnp.concatenate([p.reshape(-1).astype(dtype) for p in pieces])
    rows = -(-flat.shape[0] // (PACK_COLS * align)) * align
    flat = jnp.pad(flat, (0, rows * PACK_COLS - flat.shape[0]))
    return flat.reshape(rows, PACK_COLS)


def _unpack(flat, shapes):
    out, off = [], 0
    for shp in shapes:
        n = math.prod(shp)
        out.append(flat[off:off + n].reshape(shp))
        off += n
    return out


ANY = pl.BlockSpec(memory_space=pl.ANY)


def _pos():
    return lax.axis_index("x"), lax.axis_index("y"), lax.axis_index("c")


def _remote(src, dst, send, recv, k, to):
    return pltpu.make_async_remote_copy(src_ref=src, dst_ref=dst, send_sem=send.at[k], recv_sem=recv.at[k],
                                        device_id=to, device_id_type=MESH_ID)


def _comm_call(body, name, ins, out_shapes, nsem, aliases=None):
    return pl.pallas_call(
        body, name=name, out_shape=out_shapes,
        in_specs=[ANY] * len(ins), out_specs=[ANY] * len(out_shapes),
        scratch_shapes=[pltpu.SemaphoreType.DMA((nsem,)), pltpu.SemaphoreType.DMA((nsem,))],
        input_output_aliases=aliases or {},
    )(*ins)


def stage_shard(w, chip, name):
    rows, cols = w.shape
    tr = _pick(rows, (256, 352, 128))

    def kern(idx_ref, w_ref, o_ref):
        o_ref[...] = w_ref[...].astype(BF16)

    grid_spec = pltpu.PrefetchScalarGridSpec(
        num_scalar_prefetch=1, grid=(rows // tr,),
        in_specs=[pl.BlockSpec((tr, cols), lambda i, idx: (i, 0))],
        out_specs=pl.BlockSpec((None, tr, cols), lambda i, idx: (idx[0], i, 0)))
    return pl.pallas_call(
        kern, name=name, grid_spec=grid_spec,
        out_shape=jax.ShapeDtypeStruct((N_CHIPS, rows, cols), BF16),
        compiler_params=_params(("parallel",)),
    )(jnp.reshape(chip, (1,)).astype(jnp.int32), w)


class Side:
    def __init__(self, arrays, out_shapes, aliases, nsem, start, finish):
        self.arrays, self.out_shapes, self.aliases, self.nsem = list(arrays), list(out_shapes), aliases, nsem
        self.start, self.finish = start, finish


def run_side(side, name):
    n_in, n_out = len(side.arrays), len(side.out_shapes)

    def body(*refs):
        ins, outs = refs[:n_in], refs[n_in:n_in + n_out]
        send, recv = refs[n_in + n_out:]
        side.start(ins, outs, send, recv)
        side.finish(ins, outs, send, recv)

    return _comm_call(body, name, side.arrays, side.out_shapes, side.nsem, aliases=side.aliases)


def side_call(kern, side, *, name, grid, in_specs, out_specs, out_shape, scratch_shapes, args):
    if side is None:
        res = pl.pallas_call(kern, name=name, grid=grid, in_specs=in_specs, out_specs=out_specs, out_shape=out_shape,
                             scratch_shapes=scratch_shapes,
                             compiler_params=_params(("parallel",) + ("arbitrary",) * (len(grid) - 1)))(*args)
        return list(res), []
    n_in, n_out, n_scr = len(in_specs), len(out_specs), len(scratch_shapes)
    s_in, s_out = len(side.arrays), len(side.out_shapes)

    def body(*refs):
        ins, refs = refs[:n_in], refs[n_in:]
        side_ins, refs = refs[:s_in], refs[s_in:]
        outs, refs = refs[:n_out], refs[n_out:]
        side_outs, refs = refs[:s_out], refs[s_out:]
        scr, (send, recv) = refs[:n_scr], refs[n_scr:]
        first, last = None, None
        for axis, size in enumerate(grid):
            at0, at1 = pl.program_id(axis) == 0, pl.program_id(axis) == size - 1
            first = at0 if first is None else first & at0
            last = at1 if last is None else last & at1

        @pl.when(first)
        def _():
            side.start(side_ins, side_outs, send, recv)

        kern(*ins, *outs, *scr)

        @pl.when(last)
        def _():
            side.finish(side_ins, side_outs, send, recv)

    res = pl.pallas_call(
        body, name=name, grid=grid,
        in_specs=list(in_specs) + [ANY] * s_in, out_specs=list(out_specs) + [ANY] * s_out,
        out_shape=list(out_shape) + side.out_shapes,
        scratch_shapes=list(scratch_shapes) + [pltpu.SemaphoreType.DMA((side.nsem,)), pltpu.SemaphoreType.DMA((side.nsem,))],
        input_output_aliases={n_in + a: n_out + b for a, b in side.aliases.items()},
        compiler_params=_params(("arbitrary",) * len(grid)),
    )(*args, *side.arrays)
    return list(res[:n_out]), list(res[n_out:])


def gather_side(staged):
    n = len(staged)

    def plan(o_refs, send, recv):
        x, y, c = _pos()
        chips = [(1 - x, y), (x, 1 - y), (1 - x, 1 - y)]

        def part(u, chip, cc):
            half = staged[u].shape[1] // 2
            return o_refs[u].at[2 * chip[0] + chip[1], pl.ds(cc * half, half), :]

        first = [_remote(part(u, (x, y), c), part(u, (x, y), c), send, recv, 6 * u + j, (*chip, c))
                 for u in range(n) for j, chip in enumerate(chips)]
        landed = [_remote(part(u, chip, c), part(u, chip, c), send, recv, 6 * u + j, (x, y, c))
                  for u in range(n) for j, chip in enumerate(chips)]
        passed = [_remote(part(u, chip, c), part(u, chip, c), send, recv, 6 * u + 3 + j, (x, y, 1 - c))
                  for u in range(n) for j, chip in enumerate(chips)]
        handed = [_remote(part(u, chip, 1 - c), part(u, chip, 1 - c), send, recv, 6 * u + 3 + j, (x, y, c))
                  for u in range(n) for j, chip in enumerate(chips)]
        return first, landed, passed, handed

    def start(ins, outs, send, recv):
        for cp in plan(outs, send, recv)[0]:
            cp.start()

    def finish(ins, outs, send, recv):
        first, landed, passed, handed = plan(outs, send, recv)
        for got, fw in zip(landed, passed):
            got.wait_recv()
            fw.start()
        for got in handed:
            got.wait_recv()
        for cp in first + passed:
            cp.wait_send()

    outs = [jax.ShapeDtypeStruct(s.shape, s.dtype) for s in staged]
    return Side(staged, outs, {u: u for u in range(n)}, 6 * n, start, finish)


def swap_halves(gps, name):
    n = len(gps)

    def body(*refs):
        g_refs, r_refs = refs[:n], refs[n:2 * n]
        send, recv = refs[2 * n:]
        x, y, c = _pos()
        cps = []
        for u in range(n):
            half = gps[u].shape[1] // 2
            cps.append(_remote(g_refs[u].at[:, pl.ds((1 - c) * half, half), :], r_refs[u], send, recv, u, (x, y, 1 - c)))
        for cp in cps:
            cp.start()
        for cp in cps:
            cp.wait()

    outs = [jax.ShapeDtypeStruct((g.shape[0], g.shape[1] // 2, g.shape[2]), g.dtype) for g in gps]
    return _comm_call(body, name, gps, outs, n)


def scatter_side(parts):
    n = len(parts)

    def plan(p_refs, r_refs, send, recv):
        x, y, c = _pos()
        chips = [(1 - x, y), (x, 1 - y), (1 - x, 1 - y)]
        return [_remote(p_refs[u].at[2 * chip[0] + chip[1]], r_refs[u].at[j], send, recv, 3 * u + j, (*chip, c))
                for u in range(n) for j, chip in enumerate(chips)]

    def start(ins, outs, send, recv):
        for cp in plan(ins, outs, send, recv):
            cp.start()

    def finish(ins, outs, send, recv):
        for cp in plan(ins, outs, send, recv):
            cp.wait()

    outs = [jax.ShapeDtypeStruct((N_CHIPS - 1,) + p.shape[1:], p.dtype) for p in parts]
    return Side(parts, outs, {}, 3 * n, start, finish)


def join_halves(bufs):
    n = len(bufs)

    def body(*refs):
        o_refs = refs[n:2 * n]
        send, recv = refs[2 * n:]
        x, y, c = _pos()

        def rows(u, cc):
            half = bufs[u].shape[0] // 2
            return o_refs[u].at[pl.ds(cc * half, half), :]

        cps = [_remote(rows(u, c), rows(u, c), send, recv, u, (x, y, 1 - c)) for u in range(n)]
        for cp in cps:
            cp.start()
        for u in range(n):
            _remote(rows(u, 1 - c), rows(u, 1 - c), send, recv, u, (x, y, c)).wait_recv()
        for cp in cps:
            cp.wait_send()

    outs = [jax.ShapeDtypeStruct(b.shape, b.dtype) for b in bufs]
    return _comm_call(body, "join_halves", bufs, outs, n, aliases={u: u for u in range(n)})


def gather_small(sg, name):
    rows, cols = sg.shape

    def body(s_ref, o_ref, send, recv, lsem):
        x, y, c = _pos()
        me, sibling = (x, y, c), (x, y, 1 - c)
        chips = [(1 - x, y), (x, 1 - y), (1 - x, 1 - y)]

        def blk(px, py, pc):
            return o_ref.at[4 * px + 2 * py + pc]

        mine = pltpu.make_async_copy(s_ref, blk(*me), lsem)
        mine.start()
        first = [_remote(s_ref, blk(*me), send, recv, 0, sibling)]
        first += [_remote(s_ref, blk(*me), send, recv, 1 + j, (*chip, c)) for j, chip in enumerate(chips)]
        for cp in first:
            cp.start()
        passed = [_remote(blk(*chip, c), blk(*chip, c), send, recv, 4 + j, sibling) for j, chip in enumerate(chips)]
        for j, chip in enumerate(chips):
            _remote(blk(*chip, c), blk(*chip, c), send, recv, 1 + j, me).wait_recv()
            passed[j].start()
        _remote(blk(*sibling), blk(*sibling), send, recv, 0, me).wait_recv()
        for j, chip in enumerate(chips):
            _remote(blk(*chip, 1 - c), blk(*chip, 1 - c), send, recv, 4 + j, me).wait_recv()
        for cp in first + passed:
            cp.wait_send()
        mine.wait()

    return pl.pallas_call(
        body, name=name,
        out_shape=jax.ShapeDtypeStruct((N_DEV, rows, cols), sg.dtype),
        in_specs=[ANY], out_specs=ANY,
        scratch_shapes=[pltpu.SemaphoreType.DMA((N_DEV - 1,)), pltpu.SemaphoreType.DMA((N_DEV - 1,)), pltpu.SemaphoreType.DMA],
    )(sg)


def sum_cores(gp, theirs, core, chip, name):
    nch, rows, cols = gp.shape
    half = rows // 2
    tr = _pick(half, (256, 176, 128, 64))
    nb = half // tr

    def kern(idx_ref, g_ref, t_ref, own_ref, all_ref):
        k = pl.program_id(1)
        s = g_ref[...].astype(F32) + t_ref[...].astype(F32)
        all_ref[...] = s.astype(BF16)

        @pl.when(k == idx_ref[1])
        def _():
            own_ref[...] = s

    grid_spec = pltpu.PrefetchScalarGridSpec(
        num_scalar_prefetch=1, grid=(nb, nch),
        in_specs=[pl.BlockSpec((None, tr, cols), lambda i, k, idx: (k, idx[0] * nb + i, 0)),
                  pl.BlockSpec((None, tr, cols), lambda i, k, idx: (k, i, 0))],
        out_specs=[pl.BlockSpec((tr, cols), lambda i, k, idx: (i, 0)),
                   pl.BlockSpec((None, tr, cols), lambda i, k, idx: (k, i, 0))])
    return pl.pallas_call(
        kern, name=name, grid_spec=grid_spec,
        out_shape=[jax.ShapeDtypeStruct((half, cols), F32), jax.ShapeDtypeStruct((nch, half, cols), BF16)],
        compiler_params=_params(("parallel", "arbitrary")),
    )(jnp.stack([core, chip]).astype(jnp.int32), gp, theirs)


def sum_chips(own, others, core, name):
    half, cols = own.shape
    tr = _pick(half, (256, 176, 128, 64))
    nb = half // tr

    def kern(idx_ref, o_ref, a_ref, b_ref, c_ref, out_ref):
        out_ref[...] = ((o_ref[...] + a_ref[...].astype(F32)) + b_ref[...].astype(F32)) + c_ref[...].astype(F32)

    grid_spec = pltpu.PrefetchScalarGridSpec(
        num_scalar_prefetch=1, grid=(nb,),
        in_specs=[pl.BlockSpec((tr, cols), lambda i, idx: (i, 0))] +
                 [pl.BlockSpec((None, tr, cols), lambda i, idx, j=j: (j, i, 0)) for j in range(N_CHIPS - 1)],
        out_specs=pl.BlockSpec((tr, cols), lambda i, idx: (idx[0] * nb + i, 0)))
    return pl.pallas_call(
        kern, name=name, grid_spec=grid_spec,
        out_shape=jax.ShapeDtypeStruct((2 * half, cols), F32),
        compiler_params=_params(("parallel",)),
    )(jnp.reshape(core, (1,)).astype(jnp.int32), own, others, others, others)


def small_update(gath, w, m, v, name):
    def fn(*vs):
        g = vs[0]
        for t in vs[1:N_DEV]:
            g = g + t
        wv, mv, vv = vs[N_DEV:]
        m2 = ADAM_B1 * mv + (1.0 - ADAM_B1) * g
        v2 = ADAM_B2 * vv + (1.0 - ADAM_B2) * (g * g)
        m_hat = m2 / (1.0 - ADAM_B1 ** ADAM_STEP)
        v_hat = v2 / (1.0 - ADAM_B2 ** ADAM_STEP)
        return g, -ADAM_LR * (m_hat / (jnp.sqrt(v_hat) + ADAM_EPS) + ADAM_WD * wv), m2, v2

    c = w.shape[1]
    ins = [(gath[k], "row") for k in range(N_DEV)] + [(w, "row"), (m, "row"), (v, "row")]
    return rowwise(fn, ins, [(c, F32)] * 4, tr=w.shape[0] // 2, name=name)


_MIX = {0: [("sb_w_qkv", 0), ("sb_w_o", 0)], 1: [("gm_w_in", 0), ("gm_w_out", 0)],
        2: [("ssm_w_in", 0), ("ssm_w_out", 0)], 3: [("sb_w_qkv", 1), ("sb_w_o", 1)]}
_FFN = {i: [("ffn_w_gu", i), ("ffn_w_down", i)] for i in range(4)}
GATHER_FIRST = _MIX[0][:1]
GATHER_AT = {"sb_attn_0": _MIX[0][1:] + _FFN[0] + _MIX[1] + _FFN[1],
             "ffn_gu_0": _FFN[2][:1], "ffn_down_0": _FFN[2][1:], "ffn_gu_1": _MIX[2][:1], "ffn_down_1": _MIX[2][1:],
             "ssm_scan_2": _MIX[3] + _FFN[3]}
SCATTER_AT = {"ssm_dscan_2": _FFN[3] + _MIX[3] + _FFN[2], "sb_dattn_0": _MIX[2] + _FFN[1] + _MIX[1] + _FFN[0]}
SCATTER_LAST = _MIX[0]


class _Plan:
    def __init__(self, ins, core, chip):
        self.core, self.chip = core, chip
        self.staged = {(n, l): stage_shard(ins[n][l], chip, f"stage_{n}_{l}")
                       for n in MATRICES for l in range(ins[n].shape[0])}
        self.full = {n: [None] * ins[n].shape[0] for n in MATRICES}
        self.ready = {}
        self.parts = {}
        self.halves = {}
        self.swaps = 0
        self._fill(GATHER_FIRST, run_side(gather_side([self.staged[u] for u in GATHER_FIRST]), "gather_first"))

    def _fill(self, units, gathered):
        for (n, l), g in zip(units, gathered):
            if n in COLUMN_BLOCKS:
                self.full[n][l] = g
            elif n == "ssm_w_in":
                self.full[n][l] = jnp.concatenate([g[k] for k in range(N_CHIPS)], axis=1)
            else:
                self.full[n][l] = g.reshape(-1, g.shape[-1])

    def _prepare(self, units):
        gps = [self.ready[u] for u in units]
        theirs = swap_halves(gps, f"swap_halves_{self.swaps}")
        self.swaps += 1
        for (n, l), g, t in zip(units, gps, theirs):
            self.parts[(n, l)] = sum_cores(g, t, self.core, self.chip, f"sum_cores_{n}_{l}")

    def _reduce(self, units, others):
        for (n, l), other in zip(units, others):
            self.halves[(n, l)] = sum_chips(self.parts[(n, l)][0], other, self.core, f"sum_chips_{n}_{l}")

    def side(self, tag):
        if tag in GATHER_AT:
            return gather_side([self.staged[u] for u in GATHER_AT[tag]])
        if tag in SCATTER_AT:
            self._prepare(SCATTER_AT[tag])
            return scatter_side([self.parts[u][1] for u in SCATTER_AT[tag]])
        return None

    def done(self, tag, results):
        if tag in GATHER_AT:
            self._fill(GATHER_AT[tag], results)
        else:
            self._reduce(SCATTER_AT[tag], results)

    def grads_ready(self, grads):
        for (n, l), g in grads.items():
            if n in COLUMN_BLOCKS:
                self.ready[(n, l)] = g
            elif n == "ssm_w_in":
                self.ready[(n, l)] = jnp.stack(jnp.split(g, N_CHIPS, axis=1))
            else:
                self.ready[(n, l)] = g.reshape(N_CHIPS, -1, g.shape[-1])

    def shard_grads(self):
        self._prepare(SCATTER_LAST)
        self._reduce(SCATTER_LAST, run_side(scatter_side([self.parts[u][1] for u in SCATTER_LAST]), "scatter_last"))
        units = sorted(self.halves)
        return dict(zip(units, join_halves([self.halves[u] for u in units])))


def _step(ins):
    x, target = ins["x"][0], ins["loss_target"][0]
    core = lax.axis_index("c")
    chip = 2 * lax.axis_index("x") + lax.axis_index("y")

    def lane_pad(v):
        return jnp.pad(v, ((0, 0), (0, PACK_COLS - v.shape[1])))

    vec_rows = [ins["ssm_conv_w"][0], ins["ssm_conv_b"], lane_pad(ins["ssm_norm_gain"])]
    blk = jnp.concatenate(vec_rows + [jnp.zeros((SUBLANES - 6, PACK_COLS), F32)], axis=0)
    per_chip = gather_small(blk, "gather_vectors")[0::2]
    ngw = ins["ssm_norm_gain"].shape[1]
    full = {
        "ssm_conv_w": jnp.concatenate([per_chip[k, 0:4] for k in range(N_CHIPS)], axis=1)[None],
        "ssm_conv_b": jnp.concatenate([per_chip[k, 4:5] for k in range(N_CHIPS)], axis=1),
        "ssm_norm_gain": jnp.concatenate([per_chip[k, 5:6, :ngw] for k in range(N_CHIPS)], axis=1),
    }

    plan = _Plan(ins, core, chip)
    full.update(plan.full)
    for n in REPLICATED:
        full[n] = ins[n]

    loss, dx, grads = local_step(x, target, full, plan)
    loss = lax.psum(loss, ALL_AXES)
    gshards = plan.shard_grads()

    small_shapes = [ins[n].shape for n in REPLICATED]
    vec_shapes = [grads[n].shape for n in EXACT]
    vec_pack = _pack([grads[n] for n in EXACT], F32, SUBLANES)
    gath = gather_small(jnp.concatenate([_pack([grads[n] for n in REPLICATED], F32, SUBLANES), vec_pack], axis=0),
                        "gather_small")
    packed = [jnp.concatenate([_pack([ins[pre + n] for n in REPLICATED], F32, SUBLANES), jnp.zeros_like(vec_pack)], axis=0)
              for pre in ("", "m_", "v_")]
    res = small_update(gath, *packed, name="small_update")
    nrep = res[0].shape[0] - vec_pack.shape[0]
    small = [dict(zip(REPLICATED, _unpack(r[:nrep].reshape(-1), small_shapes))) for r in res]
    vec_g = dict(zip(EXACT, _unpack(res[0][nrep:].reshape(-1), vec_shapes)))

    out_g, out_d, out_m, out_v = {}, {}, {}, {}
    for n in REPLICATED:
        out_g[n], out_d[n], out_m[n], out_v[n] = (s[n] for s in small)
    for n in SHARDED:
        shp = ins[n].shape
        if n in EXACT:
            g = lax.dynamic_slice_in_dim(vec_g[n], chip * shp[-1], shp[-1], axis=vec_g[n].ndim - 1)
        else:
            g = jnp.stack([gshards[(n, l)] for l in range(shp[0])])
        two = (math.prod(shp[:-1]), shp[-1])
        d2, m2, v2 = adamw(ins[n].reshape(two), g.reshape(two), ins["m_" + n].reshape(two),
                           ins["v_" + n].reshape(two), f"adamw_{n}")
        out_g[n], out_d[n], out_m[n], out_v[n] = g, d2.reshape(shp), m2.reshape(shp), v2.reshape(shp)
    return (loss, dx[None], *[out_g[n] for n in WEIGHTS], *[out_d[n] for n in WEIGHTS],
            *[out_m[n] for n in WEIGHTS], *[out_v[n] for n in WEIGHTS])


def kernel(x, mix_norm, ffn_norm, sb_w_qkv, sb_q_gain, sb_k_gain, sb_w_o, gm_w_in, gm_b_in, gm_v_gain, gm_w_s, gm_b_s, gm_w_out, ssm_w_in, ssm_conv_w, ssm_conv_b, ssm_dt_bias, ssm_a_log, ssm_d, ssm_norm_gain, ssm_w_out, ffn_w_gu, ffn_w_down, loss_target, m_mix_norm, m_ffn_norm, m_sb_w_qkv, m_sb_q_gain, m_sb_k_gain, m_sb_w_o, m_gm_w_in, m_gm_b_in, m_gm_v_gain, m_gm_w_s, m_gm_b_s, m_gm_w_out, m_ssm_w_in, m_ssm_conv_w, m_ssm_conv_b, m_ssm_dt_bias, m_ssm_a_log, m_ssm_d, m_ssm_norm_gain, m_ssm_w_out, m_ffn_w_gu, m_ffn_w_down, v_mix_norm, v_ffn_norm, v_sb_w_qkv, v_sb_q_gain, v_sb_k_gain, v_sb_w_o, v_gm_w_in, v_gm_b_in, v_gm_v_gain, v_gm_w_s, v_gm_b_s, v_gm_w_out, v_ssm_w_in, v_ssm_conv_w, v_ssm_conv_b, v_ssm_dt_bias, v_ssm_a_log, v_ssm_d, v_ssm_norm_gain, v_ssm_w_out, v_ffn_w_gu, v_ffn_w_down):
    return _step(dict(locals()))
```

```python
import functools
import math

import jax
import jax.numpy as jnp
from jax import lax
from jax.experimental import pallas as pl
from jax.experimental.pallas import tpu as pltpu

F32 = jnp.float32
BF16 = jnp.bfloat16
EPS = 1e-6
LANES = 128
SUBLANES = 8
VMEM_LIMIT = 56 * 1024 * 1024
HEAD = 64
CHUNK = 128
SB_TQ, SB_TK = 256, 256
SSD_SUB = 8
SB_KEEP = 2
SB_DEAD = -110.0
SB_UNSEEN = -1e30
ADAM_LR, ADAM_B1, ADAM_B2, ADAM_EPS, ADAM_WD, ADAM_STEP = 0.001, 0.9, 0.999, 1e-08, 0.01, 10
MESH_ID = pl.DeviceIdType.MESH
ALL_AXES = ("x", "y", "c")


def _params(sem):
    return pltpu.CompilerParams(dimension_semantics=sem, vmem_limit_bytes=VMEM_LIMIT)


def _pick(n, cands):
    for c in cands:
        if n % c == 0:
            return c
    return n


def _dot(a, b, dims=((1,), (0,))):
    return lax.dot_general(a, b, (dims, ((), ())), preferred_element_type=F32)


def _dot_nt(a, b):
    return _dot(a, b, ((1,), (1,)))


def _dot_tn(a, b):
    return _dot(a, b, ((0,), (0,)))


def _split2(x):
    hi = x.astype(BF16)
    lo = (x - hi.astype(F32)).astype(BF16)
    return hi, lo


def _dot_x2(x, m):
    hi, lo = _split2(x)
    return _dot(hi, m) + _dot(lo, m)


def _dot_x3_left(m, x):
    h1 = x.astype(BF16)
    r1 = x - h1.astype(F32)
    h2 = r1.astype(BF16)
    h3 = (r1 - h2.astype(F32)).astype(BF16)
    return _dot(m, h1) + _dot(m, h2) + _dot(m, h3)


def _sigmoid(x):
    return 1.0 / (1.0 + jnp.exp(-x))


def _softplus(x):
    return jnp.maximum(x, 0.0) + jnp.log(1.0 + jnp.exp(-jnp.abs(x)))


def _colsum(x):
    return jnp.sum(x, axis=0, keepdims=True)


def _rowsum(x):
    return jnp.sum(x, axis=1, keepdims=True)


def _iota2(shape, dim):
    return lax.broadcasted_iota(jnp.int32, shape, dim)


MM_VMEM_BUDGET = 40 * 1024 * 1024
MM_STEP_US = 0.35
MM_HBM_BYTES_PER_US = 3.0e6
MM_VMEM_BYTES_PER_US = 1.5e6
MM_FLOPS_PER_US = 9.0e8
MXU_DIM = 256


def _mm_tiles(m, n, kk, wn, wk, a_bytes, b_bytes, has_add):
    def divisors(total, cands):
        got = [c for c in cands if total % c == 0 and c <= total]
        return got or [total]

    best = None
    for tm in divisors(m, (1024, 512, 256, 128)):
        for tn in divisors(wn, (1024, 768, 1408, 512, 256, 128)):
            for tk in divisors(wk, (4096, 2816, 2048, 1408, 1024, 768, 512, 256, 128)):
                nk = kk // tk
                vmem = 2 * (tm * tk * a_bytes + tk * tn * b_bytes + tm * tn * 4 * (2 if has_add else 1))
                vmem += tm * tn * 4 if nk > 1 else 0
                if vmem > MM_VMEM_BUDGET:
                    continue
                steps = (m // tm) * (n // tn) * nk
                a_reads = 1 if nk == 1 else n // tn
                traffic = m * kk * a_bytes * a_reads + kk * n * b_bytes * (m // tm) + m * n * 4
                fill = min(1.0, tn / MXU_DIM) * min(1.0, tm / MXU_DIM)
                compute = 2.0 * m * n * kk / (MM_FLOPS_PER_US * fill)
                cost = steps * MM_STEP_US + max(compute, traffic / MM_HBM_BYTES_PER_US)
                if nk > 1:
                    cost += steps * tm * tn * 8 / MM_VMEM_BYTES_PER_US
                if best is None or cost < best[0]:
                    best = (cost, tm, tn, tk)
    return best[1:]


def mm(a, b, *, ta=False, tb=False, add=None, bias=None, a_chunks=False, b_chunks=False, out_chunks=False,
       out_dtype=F32, name, side=None):
    wa = None
    if a_chunks:
        m, wa = a.shape[1], a.shape[2]
        kk = a.shape[0] * wa
    elif ta:
        kk, m = a.shape
    else:
        m, kk = a.shape
    nch, wide = 1, None
    if b_chunks:
        nch, rows_b, wide = b.shape
        kb, n = (rows_b, nch * wide) if not tb else (nch * wide, rows_b)
    elif tb:
        n, kb = b.shape
    else:
        kb, n = b.shape
    wide_o = n // N_CHIPS if out_chunks else None
    assert kk == kb, (a.shape, b.shape, ta, tb)
    has_add, has_bias = add is not None, bias is not None
    wk = wide if (wide and tb) else kk
    wn = wide if (wide and not tb) else n
    tm, tn, tk = _mm_tiles(m, n, kk, math.gcd(wn, wide_o) if wide_o else wn, math.gcd(wk, wa) if wa else wk,
                           a.dtype.itemsize, b.dtype.itemsize, has_add)
    nk = kk // tk
    dims = ((0 if ta else 1,), (1 if tb else 0,))

    def kern(*refs):
        a_ref, b_ref = refs[0], refs[1]
        rest = list(refs[2:])
        add_ref = rest.pop(0) if has_add else None
        bias_ref = rest.pop(0) if has_bias else None
        o_ref = rest[0]
        part = _dot(a_ref[...].astype(BF16), b_ref[...].astype(BF16), dims)

        def finish(r):
            if has_add:
                r = r + add_ref[...]
            if has_bias:
                r = r + bias_ref[...]
            o_ref[...] = r.astype(out_dtype)

        if nk == 1:
            finish(part)
        else:
            acc_ref = rest[1]
            k = pl.program_id(2)

            @pl.when(k == 0)
            def _():
                acc_ref[...] = part

            @pl.when((k > 0) & (k < nk - 1))
            def _():
                acc_ref[...] += part

            @pl.when(k == nk - 1)
            def _():
                finish(acc_ref[...] + part)

    if a_chunks:
        per_a = wa // tk
        a_spec = pl.BlockSpec((None, tm, tk), lambda i, j, k: (k // per_a, i, k % per_a))
    elif ta:
        a_spec = pl.BlockSpec((tk, tm), lambda i, j, k: (k, i))
    else:
        a_spec = pl.BlockSpec((tm, tk), lambda i, j, k: (i, k))
    if b_chunks and tb:
        per = wide // tk
        b_spec = pl.BlockSpec((None, tn, tk), lambda i, j, k: (k // per, j, k % per))
    elif b_chunks:
        per = wide // tn
        b_spec = pl.BlockSpec((None, tk, tn), lambda i, j, k: (j // per, k, j % per))
    elif tb:
        b_spec = pl.BlockSpec((tn, tk), lambda i, j, k: (j, k))
    else:
        b_spec = pl.BlockSpec((tk, tn), lambda i, j, k: (k, j))
    if out_chunks:
        per_o = wide_o // tn
        out_spec = pl.BlockSpec((None, tm, tn), lambda i, j, k: (j // per_o, i, j % per_o))
        out_shape = jax.ShapeDtypeStruct((N_CHIPS, m, wide_o), out_dtype)
    else:
        out_spec = pl.BlockSpec((tm, tn), lambda i, j, k: (i, j))
        out_shape = jax.ShapeDtypeStruct((m, n), out_dtype)
    in_specs, args = [a_spec, b_spec], [a, b]
    if has_add:
        in_specs.append(pl.BlockSpec((tm, tn), lambda i, j, k: (i, j)))
        args.append(add)
    if has_bias:
        in_specs.append(pl.BlockSpec((1, tn), lambda i, j, k: (0, j)))
        args.append(bias)
    (out,), side_outs = side_call(
        kern, side,
        name=name,
        grid=(m // tm, n // tn, nk),
        in_specs=in_specs,
        out_specs=[out_spec],
        out_shape=[out_shape],
        scratch_shapes=[pltpu.VMEM((tm, tn), F32)] if nk > 1 else [],
        args=args)
    return out if side is None else (out, side_outs)


def mm_hooked(plan, a, b, *, name, **kw):
    side = plan.side(name)
    if side is None:
        return mm(a, b, name=name, **kw)
    out, side_outs = mm(a, b, name=name, side=side, **kw)
    plan.done(name, side_outs)
    return out


def rowwise(fn, ins, outs, accs=(), *, tr, name):
    rows = [a for a, kind in ins if kind == "row"][0].shape[0]
    tr = min(tr, rows)
    assert rows % tr == 0 and tr % SUBLANES == 0, (rows, tr)
    n = rows // tr
    n_in, n_out = len(ins), len(outs)
    kinds = [kind for _, kind in ins]

    def kern(*refs):
        i = pl.program_id(0)
        vals = []
        for ref, kind in zip(refs[:n_in], kinds):
            v = ref[...]
            if kind == "prev":
                v = v * (i > 0).astype(v.dtype)
            elif kind == "next":
                v = v * (i < n - 1).astype(v.dtype)
            vals.append(v)
        res = fn(*vals)
        for ref, r in zip(refs[n_in:n_in + n_out], res[:n_out]):
            ref[...] = r.astype(ref.dtype)
        if accs:
            acc_refs = refs[n_in + n_out:]

            @pl.when(i == 0)
            def _():
                for ref in acc_refs:
                    ref[...] = jnp.zeros_like(ref)

            for ref, r in zip(acc_refs, res[n_out:]):
                ref[...] += r

    in_specs = []
    for a, kind in ins:
        if kind == "row":
            in_specs.append(pl.BlockSpec((tr, a.shape[1]), lambda i: (i, 0)))
        elif kind == "full":
            in_specs.append(pl.BlockSpec(a.shape, lambda i, nd=a.ndim: (0,) * nd))
        elif kind == "prev":
            in_specs.append(pl.BlockSpec((SUBLANES, a.shape[1]),
                                         lambda i: (jnp.maximum(i * (tr // SUBLANES) - 1, 0), 0)))
        else:
            in_specs.append(pl.BlockSpec((SUBLANES, a.shape[1]),
                                         lambda i: (jnp.minimum((i + 1) * (tr // SUBLANES), rows // SUBLANES - 1), 0)))
    out_specs = [pl.BlockSpec((tr, c), lambda i: (i, 0)) for c, _ in outs]
    out_specs += [pl.BlockSpec((r, c), lambda i: (0, 0)) for r, c in accs]
    out_shape = [jax.ShapeDtypeStruct((rows, c), dt) for c, dt in outs]
    out_shape += [jax.ShapeDtypeStruct((r, c), F32) for r, c in accs]
    res = pl.pallas_call(
        kern,
        name=name,
        grid=(n,),
        in_specs=in_specs,
        out_specs=out_specs,
        out_shape=out_shape,
        compiler_params=_params(("arbitrary",) if accs else ("parallel",)),
    )(*[a for a, _ in ins])
    return res


def rms_fwd(x, g, name):
    def fn(xv, gv):
        r = lax.rsqrt(jnp.mean(xv * xv, axis=1, keepdims=True) + EPS)
        return (xv * r * gv,)

    return rowwise(fn, [(x, "row"), (g, "full")], [(x.shape[1], BF16)], tr=512, name=name)[0]


def rms_bwd(x, g, dy, dres, name):
    def fn(xv, gv, dyv, drv):
        r = lax.rsqrt(jnp.mean(xv * xv, axis=1, keepdims=True) + EPS)
        xh = xv * r
        dyg = dyv * gv
        dx = drv + r * (dyg - xh * jnp.mean(dyg * xh, axis=1, keepdims=True))
        return dx, dx, _colsum(dyv * xh)

    c = x.shape[1]
    dx, dxb, dg = rowwise(fn, [(x, "row"), (g, "full"), (dy, "row"), (dres, "row")], [(c, F32), (c, BF16)], [(1, c)],
                          tr=256, name=name)
    return (dx, dxb), dg


def ffn_up(h, wgu, name, side=None):
    s, d = h.shape
    nch, _, w = wgu.shape
    half = nch // 2
    tm = _pick(s, (512, 256, 128))

    def kern(h_ref, wg_ref, wu_ref, gu_ref, a_ref):
        hv = h_ref[...]
        g = _dot(hv, wg_ref[...])
        u = _dot(hv, wu_ref[...])
        gu_ref[0] = g.astype(BF16)
        gu_ref[1] = u.astype(BF16)
        a_ref[...] = (g * _sigmoid(g) * u).astype(BF16)

    return side_call(
        kern, side, name=name, grid=(s // tm, half),
        in_specs=[pl.BlockSpec((tm, d), lambda i, j: (i, 0)),
                  pl.BlockSpec((None, d, w), lambda i, j: (j, 0, 0)),
                  pl.BlockSpec((None, d, w), lambda i, j: (j + half, 0, 0))],
        out_specs=[pl.BlockSpec((2, tm, w), lambda i, j: (0, i, j)), pl.BlockSpec((tm, w), lambda i, j: (i, j))],
        out_shape=[jax.ShapeDtypeStruct((2, s, half * w), BF16), jax.ShapeDtypeStruct((s, half * w), BF16)],
        scratch_shapes=[], args=(h, wgu, wgu))


def ffn_dact(dxb, wdown, gu, name):
    s, d = dxb.shape
    hid = wdown.shape[0]
    tm = _pick(s, (512, 256, 128))
    tn = _pick(hid, (1408, 512, 256, 128))

    def kern(dx_ref, w_ref, gu_ref, o_ref):
        da = _dot_nt(dx_ref[...], w_ref[...])
        g, u = gu_ref[0].astype(F32), gu_ref[1].astype(F32)
        sg = _sigmoid(g)
        o_ref[0] = (da * u * sg * (1.0 + g * (1.0 - sg))).astype(BF16)
        o_ref[1] = (da * g * sg).astype(BF16)

    return pl.pallas_call(
        kern, name=name, grid=(s // tm, hid // tn),
        in_specs=[pl.BlockSpec((tm, d), lambda i, j: (i, 0)), pl.BlockSpec((tn, d), lambda i, j: (j, 0)),
                  pl.BlockSpec((2, tm, tn), lambda i, j: (0, i, j))],
        out_specs=pl.BlockSpec((2, tm, tn), lambda i, j: (0, i, j)),
        out_shape=jax.ShapeDtypeStruct((2, s, hid), BF16),
        compiler_params=_params(("parallel", "parallel")),
    )(dxb, wdown, gu)


def loss_and_grad(y, t, name):
    d = y.shape[1]

    def fn(yv, tv):
        e = yv - tv
        part = jnp.sum(_colsum(e * e), axis=1, keepdims=True) * (0.5 / d)
        dy = e * (1.0 / d)
        return dy, dy, jnp.broadcast_to(part, (SUBLANES, LANES))

    dy, dyb, acc = rowwise(fn, [(y, "row"), (t, "row")], [(d, F32), (d, BF16)], [(SUBLANES, LANES)], tr=512, name=name)
    return acc[0, 0], (dy, dyb)


def adamw(w, g, m, v, name):
    def fn(wv, gv, mv, vv):
        m2 = ADAM_B1 * mv + (1.0 - ADAM_B1) * gv
        v2 = ADAM_B2 * vv + (1.0 - ADAM_B2) * (gv * gv)
        m_hat = m2 / (1.0 - ADAM_B1 ** ADAM_STEP)
        v_hat = v2 / (1.0 - ADAM_B2 ** ADAM_STEP)
        delta = -ADAM_LR * (m_hat / (jnp.sqrt(v_hat) + ADAM_EPS) + ADAM_WD * wv)
        return delta, m2, v2

    rows, c = w.shape
    tr = _pick(rows, (256, 128, 64, 32, 16, 8)) if rows % SUBLANES == 0 else rows
    if rows % SUBLANES:
        return _whole(fn, [w, g, m, v], [(w.shape, F32)] * 3, name=name)
    return rowwise(fn, [(w, "row"), (g, "row"), (m, "row"), (v, "row")], [(c, F32)] * 3, tr=tr, name=name)


def _whole(fn, ins, outs, *, name):
    n_in = len(ins)

    def kern(*refs):
        res = fn(*[r[...] for r in refs[:n_in]])
        for ref, r in zip(refs[n_in:], res):
            ref[...] = r.astype(ref.dtype)

    return pl.pallas_call(
        kern,
        name=name,
        out_shape=[jax.ShapeDtypeStruct(s, dt) for s, dt in outs],
        compiler_params=pltpu.CompilerParams(vmem_limit_bytes=VMEM_LIMIT),
    )(*ins)


def ffn_fwd(x, g, wgu, wdown, tag, plan):
    h = rms_fwd(x, g, f"ffn_rms_{tag}")
    gu, a = _hooked(plan, f"ffn_gu_{tag}", ffn_up, h, wgu)
    xn = mm_hooked(plan, a, wdown, add=x, name=f"ffn_down_{tag}")
    return xn, (x, h, gu, a)


def ffn_bwd(dxn, saved, g, wgu, wdown, tag):
    x, h, gu, a = saved
    dxn, dxb = dxn
    dwdown = mm(a, dxb, ta=True, out_dtype=BF16, name=f"ffn_dwdown_{tag}")
    dgu = ffn_dact(dxb, wdown, gu, f"ffn_dact_{tag}")
    dh = mm(dgu, wgu, tb=True, a_chunks=True, b_chunks=True, name=f"ffn_dh_{tag}")
    dwgu = mm(h, dgu, ta=True, b_chunks=True, out_dtype=BF16, out_chunks=True, name=f"ffn_dwgu_{tag}")
    dx, dg = rms_bwd(x, g, dh, dxn, f"ffn_drms_{tag}")
    return dx, dg, dwgu, dwdown


def _head_blockdiag(c):
    i = jnp.arange(c) // HEAD
    return (i[:, None] == i[None, :]).astype(BF16)


def _head_sums(x, bd):
    return jnp.concatenate([_dot_x2(x[:, g * LANES:(g + 1) * LANES], bd) for g in range(x.shape[1] // LANES)], axis=1)


def qknorm_fwd(qkv, qg, kg, bd, name):
    d = qkv.shape[1] // 3
    scale = 1.0 / math.sqrt(HEAD)

    def fn(v, qgv, kgv, bdv):
        v = v.astype(F32)
        q, k, vv = v[:, :d], v[:, d:2 * d], v[:, 2 * d:]
        rq = lax.rsqrt(_head_sums(q * q, bdv) * (1.0 / HEAD) + EPS)
        rk = lax.rsqrt(_head_sums(k * k, bdv) * (1.0 / HEAD) + EPS)
        return q * rq * qgv * scale, k * rk * kgv, vv

    return rowwise(fn, [(qkv, "row"), (qg, "full"), (kg, "full"), (bd, "full")],
                   [(d, BF16), (d, BF16), (d, BF16)], tr=256, name=name)


def qknorm_bwd(qkv, dqs, dkn, dv, qg, kg, bd, name):
    d = qkv.shape[1] // 3
    scale = 1.0 / math.sqrt(HEAD)

    def one(xv, gv, dyv, bdv):
        r = lax.rsqrt(_head_sums(xv * xv, bdv) * (1.0 / HEAD) + EPS)
        xh = xv * r
        dyg = dyv * gv
        dx = r * (dyg - xh * (_head_sums(dyg * xh, bdv) * (1.0 / HEAD)))
        return dx, _colsum(dyv * xh)

    def fn(v, dqv, dkv, dvv, qgv, kgv, bdv):
        v = v.astype(F32)
        q, k = v[:, :d], v[:, d:2 * d]
        dq, dqg = one(q, qgv, dqv * scale, bdv)
        dk, dkg = one(k, kgv, dkv, bdv)
        return jnp.concatenate([dq, dk, dvv], axis=1), dqg, dkg

    return rowwise(fn, [(qkv, "row"), (dqs, "row"), (dkn, "row"), (dv, "row"), (qg, "full"), (kg, "full"), (bd, "full")],
                   [(3 * d, BF16)], [(1, d), (1, d)], tr=256, name=name)


def _sb_tile(qh, k, mask, tri_gt):
    z = _dot_nt(qh, k)
    sp = jnp.log(1.0 + jnp.exp(-jnp.abs(z)))
    lb = jnp.minimum(z, 0.0) - sp
    l1 = jnp.where(mask, lb - z, 0.0)
    suf = _dot(l1.astype(BF16), tri_gt)
    return lb, l1, suf


def _sb_tri(tk):
    i = jnp.arange(tk)
    return jnp.stack([i[:, None] > i[None, :], i[:, None] < i[None, :]]).astype(BF16)


def _sb_setup(tq, tk):
    row, col = _iota2((tq, tk), 0), _iota2((tq, tk), 1)
    lane = _iota2((1, LANES), 1)
    halves = [(lane < HEAD).astype(BF16), (lane >= HEAD).astype(BF16)]
    lane_q = _iota2((tq, LANES), 1) + jnp.minimum(_iota2((tq, LANES), 0), 0)
    return row, col, halves, lane_q


def sb_attn_fwd(qs, kn, vb, tri, name, side=None):
    s, d = qs.shape
    tq, tk = min(SB_TQ, s), min(SB_TK, s)
    nq = s // tq
    assert s // tk <= LANES and s % tq == 0 and s % tk == 0

    def kern(q_ref, k_ref, v_ref, tri_ref, o_ref, rs_ref, ws_ref, bs_ref, acc_ref):
        i = pl.program_id(1)
        row, col, halves, lane_q = _sb_setup(tq, tk)
        q = q_ref[...]
        qh = [q * hm for hm in halves]
        acc_ref[...] = jnp.zeros_like(acc_ref)
        rs_ref[...] = jnp.full(rs_ref.shape, SB_UNSEEN, F32)
        nkb = (i + 1) * (tq // tk)

        def more(st):
            return (st[0] < nkb) & (st[1] > SB_DEAD)

        def step(st):
            n, r = st[0], list(st[2:])
            kb = nkb - 1 - n
            ks = pl.multiple_of(kb * tk, tk)
            k = k_ref[pl.ds(ks, tk), :]
            v = v_ref[pl.ds(ks, tk), :]
            mask = col < row + (i * tq - kb * tk)
            at_kb = lane_q == kb
            for hh in range(2):
                lb, l1, suf = _sb_tile(qh[hh], k, mask, tri_ref[0])
                lbm = jnp.where(mask, lb, SB_UNSEEN)
                w = jnp.exp(lbm + suf + r[hh])
                wb = w.astype(BF16)

                @pl.when(n < SB_KEEP)
                def _():
                    ws_ref[hh, n] = wb
                    bs_ref[hh, n] = jnp.exp(lbm).astype(BF16)

                acc_ref[...] += _dot(wb, v * halves[hh])
                rs_ref[hh] = jnp.where(at_kb, r[hh], rs_ref[hh])
                r[hh] = r[hh] + _rowsum(l1)
            return (n + 1, jnp.maximum(jnp.max(r[0]), jnp.max(r[1])), r[0], r[1])

        z1 = jnp.zeros((tq, 1), F32)
        lax.while_loop(more, step, (jnp.int32(0), jnp.float32(0.0), z1, z1))
        o_ref[...] = acc_ref[...].astype(BF16)

    nh2 = d // LANES
    keep_spec = pl.BlockSpec((None, None, 2, SB_KEEP, tq, tk), lambda h, i: (h, i, 0, 0, 0, 0))
    return side_call(
        kern, side,
        name=name,
        grid=(nh2, nq),
        in_specs=[pl.BlockSpec((tq, LANES), lambda h, i: (i, h)),
                  pl.BlockSpec((s, LANES), lambda h, i: (0, h)),
                  pl.BlockSpec((s, LANES), lambda h, i: (0, h)),
                  pl.BlockSpec((2, tk, tk), lambda h, i: (0, 0, 0))],
        out_specs=[pl.BlockSpec((tq, LANES), lambda h, i: (i, h)),
                   pl.BlockSpec((None, 2, tq, LANES), lambda h, i: (h, 0, i, 0)),
                   keep_spec, keep_spec],
        out_shape=[jax.ShapeDtypeStruct((s, d), BF16), jax.ShapeDtypeStruct((nh2, 2, s, LANES), F32),
                   jax.ShapeDtypeStruct((nh2, nq, 2, SB_KEEP, tq, tk), BF16),
                   jax.ShapeDtypeStruct((nh2, nq, 2, SB_KEEP, tq, tk), BF16)],
        scratch_shapes=[pltpu.VMEM((tq, LANES), F32)],
        args=(qs, kn, vb, tri))


def sb_attn_bwd(qs, kn, vb, rsave, wkeep, bkeep, do, tri, name, side=None):
    s, d = qs.shape
    tq, tk = min(SB_TQ, s), min(SB_TK, s)
    nq = s // tq

    def kern(q_ref, k_ref, v_ref, rs_ref, do_ref, tri_ref, ws_ref, bs_ref, dq_ref, dk_ref, dv_ref, w_scr, b_scr):
        i = pl.program_id(1)

        @pl.when(i == 0)
        def _():
            dk_ref[...] = jnp.zeros_like(dk_ref)
            dv_ref[...] = jnp.zeros_like(dv_ref)

        row, col, halves, lane_q = _sb_setup(tq, tk)
        q = q_ref[...]
        qh = [q * hm for hm in halves]
        dov = do_ref[...].astype(BF16)
        doh = [dov * hm for hm in halves]
        dq_ref[...] = jnp.zeros_like(dq_ref)
        nkb = (i + 1) * (tq // tk)
        top = jnp.maximum(jnp.max(rs_ref[0], axis=0, keepdims=True), jnp.max(rs_ref[1], axis=0, keepdims=True))
        dead = (top <= SB_DEAD) & (_iota2((1, LANES), 1) < nkb)
        kstart = jnp.minimum(jnp.sum(dead.astype(F32)).astype(jnp.int32), nkb)

        def step(kb, ep):
            ep = list(ep)
            ks = pl.multiple_of(kb * tk, tk)
            k = k_ref[pl.ds(ks, tk), :]
            v = v_ref[pl.ds(ks, tk), :]
            n = nkb - 1 - kb
            for hh in range(2):
                @pl.when(n < SB_KEEP)
                def _():
                    w_scr[...] = ws_ref[hh, n].astype(F32)
                    b_scr[...] = bs_ref[hh, n].astype(F32)

                @pl.when(n >= SB_KEEP)
                def _():
                    mask = col < row + (i * tq - kb * tk)
                    lb, l1, suf = _sb_tile(qh[hh], k, mask, tri_ref[0])
                    r = _rowsum(jnp.where(lane_q == kb, rs_ref[hh], 0.0))
                    lbm = jnp.where(mask, lb, SB_UNSEEN)
                    w_scr[...] = jnp.exp(lbm + suf + r)
                    b_scr[...] = jnp.exp(lbm)

                w = w_scr[...]
                e = _dot_nt(doh[hh], v) * w
                pe = ep[hh] + _dot(e.astype(BF16), tri_ref[1])
                dz = (e - b_scr[...] * (e + pe)).astype(BF16)
                dq_ref[...] += _dot(dz, k * halves[hh])
                dk_ref[pl.ds(ks, tk), :] += _dot_tn(dz, qh[hh])
                dv_ref[pl.ds(ks, tk), :] += _dot_tn(w.astype(BF16), doh[hh])
                ep[hh] = ep[hh] + _rowsum(e)
            return tuple(ep)

        z1 = jnp.zeros((tq, 1), F32)
        lax.fori_loop(kstart, nkb, step, (z1, z1))

    nh2 = d // LANES
    keep_spec = pl.BlockSpec((None, None, 2, SB_KEEP, tq, tk), lambda h, i: (h, i, 0, 0, 0, 0))
    return side_call(
        kern, side,
        name=name,
        grid=(nh2, nq),
        in_specs=[pl.BlockSpec((tq, LANES), lambda h, i: (i, h)),
                  pl.BlockSpec((s, LANES), lambda h, i: (0, h)),
                  pl.BlockSpec((s, LANES), lambda h, i: (0, h)),
                  pl.BlockSpec((None, 2, tq, LANES), lambda h, i: (h, 0, i, 0)),
                  pl.BlockSpec((tq, LANES), lambda h, i: (i, h)),
                  pl.BlockSpec((2, tk, tk), lambda h, i: (0, 0, 0)),
                  keep_spec, keep_spec],
        out_specs=[pl.BlockSpec((tq, LANES), lambda h, i: (i, h)),
                   pl.BlockSpec((s, LANES), lambda h, i: (0, h)),
                   pl.BlockSpec((s, LANES), lambda h, i: (0, h))],
        out_shape=[jax.ShapeDtypeStruct((s, d), F32)] * 3,
        scratch_shapes=[pltpu.VMEM((tq, tk), F32), pltpu.VMEM((tq, tk), F32)],
        args=(qs, kn, vb, rsave, do, tri, wkeep, bkeep))


def _hooked(plan, tag, call, *args):
    side = plan.side(tag)
    outs, side_outs = call(*args, tag, side)
    if side is not None:
        plan.done(tag, side_outs)
    return outs


def sb_fwd(x, g, wqkv, qg, kg, wo, bd, tag, plan):
    h = rms_fwd(x, g, f"sb_rms_{tag}")
    qkv = mm(h, wqkv, b_chunks=True, out_dtype=BF16, name=f"sb_qkv_{tag}")
    qs, kn, vb = qknorm_fwd(qkv, qg, kg, bd, f"sb_qknorm_{tag}")
    o, rsave, wkeep, bkeep = _hooked(plan, f"sb_attn_{tag}", sb_attn_fwd, qs, kn, vb, _sb_tri(min(SB_TK, x.shape[0])))
    xn = mm(o, wo(), add=x, name=f"sb_out_{tag}")
    return xn, (x, h, qkv, qs, kn, vb, rsave, wkeep, bkeep, o)


def sb_bwd(dxn, saved, g, wqkv, qg, kg, wo, bd, tag, plan):
    x, h, qkv, qs, kn, vb, rsave, wkeep, bkeep, o = saved
    dxn, dxb = dxn
    do = mm(dxb, wo, tb=True, name=f"sb_do_{tag}")
    dwo = mm(o, dxb, ta=True, out_dtype=BF16, name=f"sb_dwo_{tag}")
    dqs, dkn, dv = _hooked(plan, f"sb_dattn_{tag}", sb_attn_bwd, qs, kn, vb, rsave, wkeep, bkeep, do,
                           _sb_tri(min(SB_TK, x.shape[0])))
    dqkv, dqg, dkg = qknorm_bwd(qkv, dqs, dkn, dv, qg, kg, bd, f"sb_dqknorm_{tag}")
    dh = mm(dqkv, wqkv, tb=True, b_chunks=True, name=f"sb_dh_{tag}")
    dwqkv = mm(h, dqkv, ta=True, out_dtype=BF16, out_chunks=True, name=f"sb_dwqkv_{tag}")
    dx, dg = rms_bwd(x, g, dh, dxn, f"sb_drms_{tag}")
    nh = dqg.shape[1] // HEAD
    return dx, dg, dwqkv, dqg.reshape(nh, HEAD).sum(0), dkg.reshape(nh, HEAD).sum(0), dwo


def _gelu(x):
    return 0.5 * x * (1.0 + lax.erf(x * (1.0 / math.sqrt(2.0))))


def _gelu_grad(x):
    return 0.5 * (1.0 + lax.erf(x * (1.0 / math.sqrt(2.0)))) + x * jnp.exp(-0.5 * x * x) * (1.0 / math.sqrt(2.0 * math.pi))


def gm_act_fwd(pre, vg, name):
    half = pre.shape[1] // 2

    def fn(p, vgv):
        p = p.astype(F32)
        u = _gelu(p[:, :half])
        v = _gelu(p[:, half:])
        r = lax.rsqrt(jnp.mean(v * v, axis=1, keepdims=True) + EPS)
        return u, v * r * vgv

    return rowwise(fn, [(pre, "row"), (vg, "full")], [(half, F32), (half, BF16)], tr=256, name=name)


def gm_act_bwd(pre, du, dvn, vg, name):
    half = pre.shape[1] // 2

    def fn(p, duv, dvnv, vgv):
        p = p.astype(F32)
        pu, pv = p[:, :half], p[:, half:]
        v = _gelu(pv)
        r = lax.rsqrt(jnp.mean(v * v, axis=1, keepdims=True) + EPS)
        vh = v * r
        dyg = dvnv * vgv
        dv = r * (dyg - vh * jnp.mean(dyg * vh, axis=1, keepdims=True))
        dpre = jnp.concatenate([duv * _gelu_grad(pu), dv * _gelu_grad(pv)], axis=1)
        return dpre, _colsum(dvnv * vh), _colsum(dpre)

    return rowwise(fn, [(pre, "row"), (du, "row"), (dvn, "row"), (vg, "full")],
                   [(2 * half, BF16)], [(1, half), (1, 2 * half)], tr=256, name=name)


def gm_spatial_fwd(u, vn, wc, bst, name):
    s, c = u.shape
    t = CHUNK
    ng = c // LANES

    def kern(u_ref, v_ref, w_ref, b_ref, o_ref):
        for g in range(ng):
            sl = slice(g * LANES, (g + 1) * LANES)
            mixed = _dot(w_ref[g], v_ref[:, sl]) + b_ref[:, sl]
            o_ref[:, sl] = (u_ref[:, sl] * mixed).astype(BF16)

    return pl.pallas_call(
        kern,
        name=name,
        grid=(s // t,),
        in_specs=[pl.BlockSpec((t, c), lambda i: (i, 0)), pl.BlockSpec((t, c), lambda i: (i, 0)),
                  pl.BlockSpec(wc.shape, lambda i: (0, 0, 0)), pl.BlockSpec(bst.shape, lambda i: (0, 0))],
        out_specs=pl.BlockSpec((t, c), lambda i: (i, 0)),
        out_shape=jax.ShapeDtypeStruct((s, c), BF16),
        compiler_params=_params(("parallel",)),
    )(u, vn, wc, bst)


def gm_spatial_bwd(dgate, u, vn, wc, bst, name):
    s, c = u.shape
    t = CHUNK
    ng = c // LANES

    def kern(dg_ref, u_ref, v_ref, w_ref, b_ref, du_ref, dv_ref, dw_ref, db_ref):
        i = pl.program_id(0)

        @pl.when(i == 0)
        def _():
            dw_ref[...] = jnp.zeros_like(dw_ref)
            db_ref[...] = jnp.zeros_like(db_ref)

        for g in range(ng):
            sl = slice(g * LANES, (g + 1) * LANES)
            vg = v_ref[:, sl]
            dgv = dg_ref[:, sl]
            mixed = _dot(w_ref[g], vg) + b_ref[:, sl]
            du_ref[:, sl] = dgv * mixed
            dmix = dgv * u_ref[:, sl]
            dmb = dmix.astype(BF16)
            dv_ref[:, sl] = _dot_tn(w_ref[g], dmb)
            dw_ref[g] += _dot_nt(dmb, vg)
            db_ref[:, sl] += dmix

    return pl.pallas_call(
        kern,
        name=name,
        grid=(s // t,),
        in_specs=[pl.BlockSpec((t, c), lambda i: (i, 0))] * 3 +
                 [pl.BlockSpec(wc.shape, lambda i: (0, 0, 0)), pl.BlockSpec(bst.shape, lambda i: (0, 0))],
        out_specs=[pl.BlockSpec((t, c), lambda i: (i, 0)), pl.BlockSpec((t, c), lambda i: (i, 0)),
                   pl.BlockSpec(wc.shape, lambda i: (0, 0, 0)), pl.BlockSpec(bst.shape, lambda i: (0, 0))],
        out_shape=[jax.ShapeDtypeStruct((s, c), F32), jax.ShapeDtypeStruct((s, c), F32),
                   jax.ShapeDtypeStruct(wc.shape, F32), jax.ShapeDtypeStruct(bst.shape, F32)],
        compiler_params=_params(("arbitrary",)),
    )(dgate, u, vn, wc, bst)


def gm_fwd(x, g, w_in, b_in, vg, wc, bst, w_out, tag):
    h = rms_fwd(x, g, f"gm_rms_{tag}")
    pre = mm(h, w_in, bias=b_in, b_chunks=True, out_dtype=BF16, name=f"gm_in_{tag}")
    u, vn = gm_act_fwd(pre, vg, f"gm_act_{tag}")
    gate = gm_spatial_fwd(u, vn, wc, bst, f"gm_spatial_{tag}")
    xn = mm(gate, w_out, add=x, name=f"gm_out_{tag}")
    return xn, (x, h, pre, u, vn, gate)


def gm_bwd(dxn, saved, g, w_in, vg, wc, bst, w_out, tag):
    x, h, pre, u, vn, gate = saved
    dxn, dxb = dxn
    dgate = mm(dxb, w_out, tb=True, name=f"gm_dgate_{tag}")
    dwout = mm(gate, dxb, ta=True, out_dtype=BF16, name=f"gm_dwout_{tag}")
    du, dvn, dws, dbst = gm_spatial_bwd(dgate, u, vn, wc, bst, f"gm_dspatial_{tag}")
    dpre, dvg, dbin = gm_act_bwd(pre, du, dvn, vg, f"gm_dact_{tag}")
    dh = mm(dpre, w_in, tb=True, b_chunks=True, name=f"gm_dh_{tag}")
    dwin = mm(h, dpre, ta=True, out_dtype=BF16, out_chunks=True, name=f"gm_dwin_{tag}")
    dx, dg = rms_bwd(x, g, dh, dxn, f"gm_drms_{tag}")
    ng = wc.shape[0]
    dws = jnp.where(jnp.tril(jnp.ones((CHUNK, CHUNK), bool)), dws, 0.0)
    dbs = dbst.reshape(CHUNK, ng, LANES).sum(-1).T
    return dx, dg, dwin, dbin, dvg, dws, dbs, dwout


def _conv_taps(xv, prev):
    cat = jnp.concatenate([prev, xv], axis=0)
    return [pltpu.roll(cat, sh, 0)[SUBLANES:] for sh in (3, 2, 1)] + [xv]


def conv_fwd(xbc, ws, b, d_inner, name):
    c = xbc.shape[1]
    nst = (c - d_inner) // 2

    def fn(xv, prev, w0, w1, w2, w3, bv):
        taps = _conv_taps(xv, prev)
        pre = bv + w0 * taps[0] + w1 * taps[1] + w2 * taps[2] + w3 * taps[3]
        out = pre * _sigmoid(pre)
        return out[:, :d_inner], out[:, d_inner:d_inner + nst], out[:, d_inner + nst:]

    return rowwise(fn, [(xbc, "row"), (xbc, "prev")] + [(w, "full") for w in ws] + [(b, "full")],
                   [(d_inner, F32), (nst, F32), (nst, F32)], tr=256, name=name)


def conv_bwd_pre(xbc, ws, b, dxs_a, dxs_b, db_m, dc_m, name):
    c = xbc.shape[1]

    def fn(xv, prev, w0, w1, w2, w3, bv, da, db2, dbm, dcm):
        taps = _conv_taps(xv, prev)
        pre = bv + w0 * taps[0] + w1 * taps[1] + w2 * taps[2] + w3 * taps[3]
        sg = _sigmoid(pre)
        dout = jnp.concatenate([da + db2, dbm, dcm], axis=1)
        dpre = dout * sg * (1.0 + pre * (1.0 - sg))
        return (dpre,) + tuple(_colsum(dpre * tp) for tp in taps) + (_colsum(dpre),)

    return rowwise(fn, [(xbc, "row"), (xbc, "prev")] + [(w, "full") for w in ws] +
                   [(b, "full"), (dxs_a, "row"), (dxs_b, "row"), (db_m, "row"), (dc_m, "row")],
                   [(c, F32)], [(1, c)] * 5, tr=256, name=name)


def conv_bwd_in(dpre, ws, name):
    c = dpre.shape[1]

    def fn(dv, nxt, w0, w1, w2, w3):
        cat = jnp.concatenate([dv, nxt], axis=0)
        n = cat.shape[0]
        up = [pltpu.roll(cat, n - sh, 0)[:dv.shape[0]] for sh in (1, 2, 3)]
        return (w3 * dv + w2 * up[0] + w1 * up[1] + w0 * up[2],)

    return rowwise(fn, [(dpre, "row"), (dpre, "next")] + [(w, "full") for w in ws], [(c, BF16)], tr=256, name=name)[0]


def ssd_pre(dtr, bias, alog, name):
    def fn(d, bv, al, tri):
        dt = _softplus(d + bv)
        a = dt * (-jnp.exp(al))
        return dt, _dot_x3_left(tri, a)

    tri = jnp.tril(jnp.ones((CHUNK, CHUNK), BF16))
    return rowwise(fn, [(dtr, "row"), (bias, "full"), (alog, "full"), (tri, "full")],
                   [(LANES, F32), (LANES, F32)], tr=CHUNK, name=name)


def _ssd_layouts(v, ngroups, hpg):
    s = v.shape[0]
    col = v[:, :ngroups * hpg].T.reshape(ngroups, hpg, s, 1)
    return jnp.broadcast_to(col, (ngroups, hpg, s, LANES))


def _ssd_rowform(acum, ngroups, hpg):
    s = acum.shape[0]
    nc = s // CHUNK
    a = acum[:, :ngroups * hpg].reshape(nc, CHUNK, ngroups, hpg).transpose(2, 0, 3, 1)
    last = jnp.broadcast_to(a[..., CHUNK - 1:], a.shape)
    return jnp.concatenate([a, last], axis=2)


def ssd_chunk_fwd(xs, bm, cm, col_a, col_dt, rowf, name, side=None):
    s, d_inner = xs.shape
    ln = CHUNK
    nc = s // ln
    nsub = _pick(nc, (SSD_SUB, 2, 1))
    rows = nsub * ln
    ng, hpg = col_a.shape[0], col_a.shape[1]
    gw = d_inner // ng
    assert gw == hpg * HEAD and gw % LANES == 0 and bm.shape[1] == ng * LANES

    def kern(x_ref, b_ref, c_ref, ca_ref, cd_ref, rf_ref, y_ref, hp_ref, h_scr):
        @pl.when(pl.program_id(1) == 0)
        def _():
            h_scr[...] = jnp.zeros_like(h_scr)

        causal = _iota2((ln, ln), 0) >= _iota2((ln, ln), 1)
        lane = _iota2((1, LANES), 1)
        for sc in range(nsub):
            rs = slice(sc * ln, (sc + 1) * ln)
            bb = b_ref[rs, :].astype(BF16)
            cbf = c_ref[rs, :].astype(BF16)
            cb = _dot_nt(cbf, bb)
            ys = [jnp.zeros((ln, LANES), F32) for _ in range(gw // LANES)]
            for r in range(hpg):
                j, hf = divmod(r, LANES // HEAD)
                mh = ((lane >= HEAD * hf) & (lane < HEAD * (hf + 1))).astype(F32)
                ac = ca_ref[r, rs, :]
                ar = rf_ref[sc, pl.ds(r, 1), :]
                aend = rf_ref[sc, pl.ds(4 + r, 1), :]
                dm = jnp.exp(jnp.minimum(ac - ar, 0.0))
                m = jnp.where(causal, cb * dm, 0.0).astype(BF16)
                xdt = x_ref[rs, j * LANES:(j + 1) * LANES] * cd_ref[r, rs, :] * mh
                h = h_scr[r]
                hp_ref[sc, r] = h
                ys[j] = ys[j] + _dot(m, xdt.astype(BF16)) + _dot_nt(cbf, h.astype(BF16)) * jnp.exp(ac)
                dte = jnp.exp(aend - ac)
                h_scr[r] = jnp.exp(aend) * h + _dot_tn((xdt * dte).astype(BF16), bb)
            for j in range(gw // LANES):
                y_ref[rs, j * LANES:(j + 1) * LANES] = ys[j]

    colspec = pl.BlockSpec((None, hpg, rows, LANES), lambda g, c: (g, 0, c, 0))
    return side_call(
        kern, side,
        name=name,
        grid=(ng, nc // nsub),
        in_specs=[pl.BlockSpec((rows, gw), lambda g, c: (c, g)),
                  pl.BlockSpec((rows, LANES), lambda g, c: (c, g)),
                  pl.BlockSpec((rows, LANES), lambda g, c: (c, g)),
                  colspec, colspec,
                  pl.BlockSpec((None, nsub, 8, LANES), lambda g, c: (g, c, 0, 0))],
        out_specs=[pl.BlockSpec((rows, gw), lambda g, c: (c, g)),
                   pl.BlockSpec((None, nsub, hpg, LANES, LANES), lambda g, c: (g, c, 0, 0, 0))],
        out_shape=[jax.ShapeDtypeStruct((s, d_inner), F32),
                   jax.ShapeDtypeStruct((ng, nc, hpg, LANES, LANES), F32)],
        scratch_shapes=[pltpu.VMEM((hpg, LANES, LANES), F32)],
        args=(xs, bm, cm, col_a, col_dt, rowf))


def ssd_chunk_bwd(xs, bm, cm, col_a, col_dt, rowf, hprev, dy, name, side=None):
    s, d_inner = xs.shape
    ln = CHUNK
    nc = s // ln
    nsub = _pick(nc, (SSD_SUB, 2, 1))
    rows = nsub * ln
    ng, hpg = col_a.shape[0], col_a.shape[1]
    gw = d_inner // ng

    def kern(x_ref, b_ref, c_ref, ca_ref, cd_ref, rf_ref, hp_ref, dy_ref,
             dx_ref, db_ref, dc_ref, ddt_ref, da_ref, dh_scr):
        @pl.when(pl.program_id(1) == 0)
        def _():
            dh_scr[...] = jnp.zeros_like(dh_scr)

        row, col = _iota2((ln, ln), 0), _iota2((ln, ln), 1)
        causal = row >= col
        tri_ge = (col >= row).astype(BF16)
        ones = jnp.ones((ln, LANES), BF16)
        lane = _iota2((1, LANES), 1)
        last_row = (_iota2((ln, 1), 0) == ln - 1).astype(F32)
        for sc in reversed(range(nsub)):
            rs = slice(sc * ln, (sc + 1) * ln)
            bb = b_ref[rs, :].astype(BF16)
            cbf = c_ref[rs, :].astype(BF16)
            cb = _dot_nt(cbf, bb)
            dcb = jnp.zeros((ln, ln), F32)
            d_b = jnp.zeros((ln, LANES), F32)
            d_c = jnp.zeros((ln, LANES), F32)
            dxs = [jnp.zeros((ln, LANES), F32) for _ in range(gw // LANES)]
            for r in range(hpg):
                j, hf = divmod(r, LANES // HEAD)
                mh = ((lane >= HEAD * hf) & (lane < HEAD * (hf + 1))).astype(F32)
                ac = ca_ref[r, rs, :]
                dt = cd_ref[r, rs, :]
                ar = rf_ref[sc, pl.ds(r, 1), :]
                aend = rf_ref[sc, pl.ds(4 + r, 1), :]
                dm = jnp.where(causal, jnp.exp(jnp.minimum(ac - ar, 0.0)), 0.0)
                m = cb * dm
                mb = m.astype(BF16)
                xp = x_ref[rs, j * LANES:(j + 1) * LANES]
                xdt = xp * dt * mh
                xdtb = xdt.astype(BF16)
                dyp = dy_ref[rs, j * LANES:(j + 1) * LANES] * mh
                dypb = dyp.astype(BF16)
                h = hp_ref[sc, r]
                hb = h.astype(BF16)
                dh = dh_scr[r]
                dhb = dh.astype(BF16)
                e_in = jnp.exp(ac)
                dte = jnp.exp(aend - ac)
                eend = jnp.exp(aend)
                d_m = _dot_nt(dypb, xdtb)
                dcb = dcb + d_m * dm
                gm = d_m * m
                yoff_pre = _dot_nt(cbf, hb)
                bdh = _dot_nt(bb, dhb)
                dxdt = _dot_tn(mb, dypb) + bdh * dte
                t1 = _rowsum(xdt * bdh) * dte
                gh, gl = _split2(gm)
                dacum = (_rowsum(gm) - (_dot_tn(gh, ones) + _dot_tn(gl, ones))
                         + _rowsum(dyp * yoff_pre) * e_in - t1)
                end_term = _colsum(t1) + eend * jnp.sum(_colsum(dh * h), axis=1, keepdims=True)
                dacum = dacum + last_row * end_term
                da_ref[r, rs, :] = _dot_x3_left(tri_ge, dacum)
                ddt_ref[r, rs, :] = jnp.broadcast_to(_rowsum(dxdt * xp), (ln, LANES))
                dxs[j] = dxs[j] + dxdt * dt
                d_b = d_b + _dot((xdt * dte).astype(BF16), dhb)
                dye = (dyp * e_in).astype(BF16)
                d_c = d_c + _dot(dye, hb)
                dh_scr[r] = eend * dh + _dot_tn(dye, cbf)
            dcbb = dcb.astype(BF16)
            dc_ref[rs, :] = d_c + _dot(dcbb, bb)
            db_ref[rs, :] = d_b + _dot_tn(dcbb, cbf)
            for j in range(gw // LANES):
                dx_ref[rs, j * LANES:(j + 1) * LANES] = dxs[j]

    rev = nc // nsub - 1
    colspec = pl.BlockSpec((None, hpg, rows, LANES), lambda g, c: (g, 0, rev - c, 0))
    return side_call(
        kern, side,
        name=name,
        grid=(ng, nc // nsub),
        in_specs=[pl.BlockSpec((rows, gw), lambda g, c: (rev - c, g)),
                  pl.BlockSpec((rows, LANES), lambda g, c: (rev - c, g)),
                  pl.BlockSpec((rows, LANES), lambda g, c: (rev - c, g)),
                  colspec, colspec,
                  pl.BlockSpec((None, nsub, 8, LANES), lambda g, c: (g, rev - c, 0, 0)),
                  pl.BlockSpec((None, nsub, hpg, LANES, LANES), lambda g, c: (g, rev - c, 0, 0, 0)),
                  pl.BlockSpec((rows, gw), lambda g, c: (rev - c, g))],
        out_specs=[pl.BlockSpec((rows, gw), lambda g, c: (rev - c, g)),
                   pl.BlockSpec((rows, LANES), lambda g, c: (rev - c, g)),
                   pl.BlockSpec((rows, LANES), lambda g, c: (rev - c, g)),
                   colspec, colspec],
        out_shape=[jax.ShapeDtypeStruct((s, d_inner), F32),
                   jax.ShapeDtypeStruct(bm.shape, F32), jax.ShapeDtypeStruct(cm.shape, F32),
                   jax.ShapeDtypeStruct(col_a.shape, F32), jax.ShapeDtypeStruct(col_a.shape, F32)],
        scratch_shapes=[pltpu.VMEM((hpg, LANES, LANES), F32)],
        args=(xs, bm, cm, col_a, col_dt, rowf, hprev, dy))


def gnorm_fwd(y, xs, z, dexp, gain, ngroups, name):
    c = y.shape[1]
    gw = c // ngroups

    def fn(yv, xv, zv, dv, gv):
        yg = (yv + xv * dv) * (zv * _sigmoid(zv))
        outs = []
        for k in range(ngroups):
            t = yg[:, k * gw:(k + 1) * gw]
            outs.append(t * lax.rsqrt(jnp.mean(t * t, axis=1, keepdims=True) + EPS))
        return (jnp.concatenate(outs, axis=1) * gv,)

    return rowwise(fn, [(y, "row"), (xs, "row"), (z, "row"), (dexp, "full"), (gain, "full")], [(c, BF16)], tr=256, name=name)[0]


def gnorm_bwd(dn, y, xs, z, dexp, gain, ngroups, name):
    c = y.shape[1]
    gw = c // ngroups

    def fn(dnv, yv, xv, zv, dv, gv):
        yd = yv + xv * dv
        sg = _sigmoid(zv)
        sz = zv * sg
        yg = yd * sz
        dng = dnv * gv
        dyg, yh = [], []
        for k in range(ngroups):
            sl = slice(k * gw, (k + 1) * gw)
            t = yg[:, sl]
            r = lax.rsqrt(jnp.mean(t * t, axis=1, keepdims=True) + EPS)
            th = t * r
            dyg.append(r * (dng[:, sl] - th * jnp.mean(dng[:, sl] * th, axis=1, keepdims=True)))
            yh.append(th)
        dyg = jnp.concatenate(dyg, axis=1)
        yh = jnp.concatenate(yh, axis=1)
        dyd = dyg * sz
        dz = dyg * yd * (sg * (1.0 + zv * (1.0 - sg)))
        return dyd, dyd * dv, dz, _colsum(dyd * xv), _colsum(dnv * yh)

    return rowwise(fn, [(dn, "row"), (y, "row"), (xs, "row"), (z, "row"), (dexp, "full"), (gain, "full")],
                   [(c, F32), (c, F32), (c, BF16)], [(1, c), (1, c)], tr=256, name=name)


def ssd_post(ddt, da, dt, dtr, bias, alog, name):
    def fn(ddtv, dav, dtv, dtrv, bv, al):
        a_neg = -jnp.exp(al)
        ddtr = (ddtv + dav * a_neg) * _sigmoid(dtrv + bv)
        return ddtr, _colsum(ddtr), _colsum(dav * dtv) * a_neg

    return rowwise(fn, [(ddt, "row"), (da, "row"), (dt, "row"), (dtr, "row"), (bias, "full"), (alog, "full")],
                   [(LANES, BF16)], [(1, LANES), (1, LANES)], tr=512, name=name)


def _from_colform(v, s):
    ng, hpg = v.shape[0], v.shape[1]
    flat = v[..., 0].reshape(ng * hpg, s).T
    return jnp.pad(flat, ((0, 0), (0, LANES - ng * hpg)))


def ssm_fwd(x, g, p, tag, plan):
    ng, hpg, d_inner = p["ng"], p["hpg"], p["d_inner"]
    h = rms_fwd(x, g, f"ssm_rms_{tag}")
    z = mm(h, p["w_z"], name=f"ssm_inz_{tag}")
    xbc = mm(h, p["w_xbc"], name=f"ssm_inx_{tag}")
    dtr = mm(h, p["w_dt"], name=f"ssm_indt_{tag}")
    xs, bm, cm = conv_fwd(xbc, p["conv_w"], p["conv_b"], d_inner, f"ssm_conv_{tag}")
    dt, acum = ssd_pre(dtr, p["dt_bias"], p["a_log"], f"ssm_pre_{tag}")
    col_a, col_dt = _ssd_layouts(acum, ng, hpg), _ssd_layouts(dt, ng, hpg)
    rowf = _ssd_rowform(acum, ng, hpg)
    y, hprev = _hooked(plan, f"ssm_scan_{tag}", ssd_chunk_fwd, xs, bm, cm, col_a, col_dt, rowf)
    n = gnorm_fwd(y, xs, z, p["d_exp"], p["norm_gain"], ng, f"ssm_gnorm_{tag}")
    xn = mm(n, p["w_out"], add=x, name=f"ssm_out_{tag}")
    return xn, (x, h, z, xbc, dtr, xs, bm, cm, dt, col_a, col_dt, rowf, y, hprev, n)


def ssm_bwd(dxn, saved, g, p, tag, plan):
    x, h, z, xbc, dtr, xs, bm, cm, dt, col_a, col_dt, rowf, y, hprev, n = saved
    ng, hpg, d_inner = p["ng"], p["hpg"], p["d_inner"]
    s = x.shape[0]
    dxn, dxb = dxn
    dn = mm(dxb, p["w_out"], tb=True, name=f"ssm_dn_{tag}")
    dwout = mm(n, dxb, ta=True, out_dtype=BF16, name=f"ssm_dwout_{tag}")
    dy, dxs_skip, dz, dd_lane, dgain = gnorm_bwd(dn, y, xs, z, p["d_exp"], p["norm_gain"], ng, f"ssm_dgnorm_{tag}")
    dxs, dbm, dcm, ddt_c, da_c = _hooked(plan, f"ssm_dscan_{tag}", ssd_chunk_bwd, xs, bm, cm, col_a, col_dt, rowf, hprev, dy)
    ddtr, dbias, dalog = ssd_post(_from_colform(ddt_c, s), _from_colform(da_c, s), dt, dtr,
                                  p["dt_bias"], p["a_log"], f"ssm_post_{tag}")
    res = conv_bwd_pre(xbc, p["conv_w"], p["conv_b"], dxs, dxs_skip, dbm, dcm, f"ssm_dconv_{tag}")
    dpre, dconv_w, dconv_b = res[0], jnp.concatenate(res[1:5], axis=0), res[5]
    dxbc = conv_bwd_in(dpre, p["conv_w"], f"ssm_dconvin_{tag}")
    dh = mm(dz, p["w_z"], tb=True, name=f"ssm_dhz_{tag}")
    dh = mm(dxbc, p["w_xbc"], tb=True, add=dh, name=f"ssm_dhx_{tag}")
    dh = mm(ddtr, p["w_dt"], tb=True, add=dh, name=f"ssm_dhdt_{tag}")
    dwz = mm(h, dz, ta=True, out_dtype=BF16, name=f"ssm_dwz_{tag}")
    dwxbc = mm(h, dxbc, ta=True, out_dtype=BF16, name=f"ssm_dwxbc_{tag}")
    dwdt = mm(h, ddtr, ta=True, out_dtype=BF16, name=f"ssm_dwdt_{tag}")
    dx, dg = rms_bwd(x, g, dh, dxn, f"ssm_drms_{tag}")
    nh = ng * hpg
    dwin = jnp.concatenate([dwz, dwxbc, dwdt[:, :nh]], axis=1)
    dd = dd_lane.reshape(nh, HEAD).sum(-1)
    return dx, dg, dict(w_in=dwin, conv_w=dconv_w, conv_b=dconv_b, dt_bias=dbias[0, :nh], a_log=dalog[0, :nh],
                        d=dd, norm_gain=dgain, w_out=dwout)


def local_step(x, target, w, plan):
    d = x.shape[1]
    depth = w["mix_norm"].shape[0]
    bd = _head_blockdiag(LANES)
    tril = jnp.tril(jnp.ones((CHUNK, CHUNK), bool))
    ssm_heads = w["ssm_dt_bias"].shape[1]
    d_inner = w["ssm_norm_gain"].shape[1]
    ng = w["ssm_norm_gain"].shape[1] // 256
    nstate = CHUNK

    def pad_lanes(v):
        return jnp.pad(v, ((0, 0), (0, LANES - v.shape[1])))

    def ssm_params(j):
        w_in = w["ssm_w_in"][j]
        cw = w["ssm_conv_w"][j]
        return dict(ng=ng, hpg=ssm_heads // ng, d_inner=d_inner,
                    w_z=w_in[:, :d_inner], w_xbc=w_in[:, d_inner:d_inner + d_inner + 2 * ng * nstate],
                    w_dt=pad_lanes(w_in[:, 2 * d_inner + 2 * ng * nstate:]),
                    conv_w=[cw[k:k + 1] for k in range(cw.shape[0])], conv_b=w["ssm_conv_b"][j:j + 1],
                    dt_bias=pad_lanes(w["ssm_dt_bias"][j:j + 1]), a_log=pad_lanes(w["ssm_a_log"][j:j + 1]),
                    d_exp=jnp.repeat(w["ssm_d"][j], HEAD)[None, :], norm_gain=w["ssm_norm_gain"][j:j + 1],
                    w_out=w["ssm_w_out"][j])

    def gm_params(j):
        wc = jnp.where(tril, w["gm_w_s"][j], 0.0).astype(BF16)
        bst = jnp.repeat(w["gm_b_s"][j].T, LANES, axis=1)
        return wc, bst

    def sb_gains(j):
        nh = d // HEAD
        return jnp.tile(w["sb_q_gain"][j], nh)[None, :], jnp.tile(w["sb_k_gain"][j], nh)[None, :]

    saved = []
    cur = x
    for i in range(depth):
        kind, j = i % 3, i // 3
        gmix = w["mix_norm"][i:i + 1]
        if kind == 0:
            qg, kg = sb_gains(j)
            cur, sv = sb_fwd(cur, gmix, w["sb_w_qkv"][j], qg, kg, lambda j=j: w["sb_w_o"][j], bd, f"{i}", plan)
        elif kind == 1:
            wc, bst = gm_params(j)
            cur, sv = gm_fwd(cur, gmix, w["gm_w_in"][j], w["gm_b_in"][j:j + 1], w["gm_v_gain"][j:j + 1], wc, bst,
                             w["gm_w_out"][j], f"{i}")
        else:
            cur, sv = ssm_fwd(cur, gmix, ssm_params(j), f"{i}", plan)
        cur, sv2 = ffn_fwd(cur, w["ffn_norm"][i:i + 1], w["ffn_w_gu"][i], w["ffn_w_down"][i], f"{i}", plan)
        saved.append((sv, sv2))

    loss, dcur = loss_and_grad(cur, target, "loss")

    grads = {k: [None] * len(v) for k, v in w.items()}
    for i in reversed(range(depth)):
        kind, j = i % 3, i // 3
        sv, sv2 = saved[i]
        gmix = w["mix_norm"][i:i + 1]
        dcur, dgf, dwgu, dwdown = ffn_bwd(dcur, sv2, w["ffn_norm"][i:i + 1], w["ffn_w_gu"][i], w["ffn_w_down"][i], f"{i}")
        grads["ffn_norm"][i], grads["ffn_w_gu"][i], grads["ffn_w_down"][i] = dgf[0], dwgu, dwdown
        plan.grads_ready({("ffn_w_gu", i): dwgu, ("ffn_w_down", i): dwdown})
        if kind == 0:
            qg, kg = sb_gains(j)
            dcur, dg, dwqkv, dqg, dkg, dwo = sb_bwd(dcur, sv, gmix, w["sb_w_qkv"][j], qg, kg, w["sb_w_o"][j], bd, f"{i}", plan)
            grads["sb_w_qkv"][j], grads["sb_q_gain"][j], grads["sb_k_gain"][j], grads["sb_w_o"][j] = dwqkv, dqg, dkg, dwo
        elif kind == 1:
            wc, bst = gm_params(j)
            dcur, dg, dwin, dbin, dvg, dws, dbs, dwout = gm_bwd(dcur, sv, gmix, w["gm_w_in"][j], w["gm_v_gain"][j:j + 1],
                                                                 wc, bst, w["gm_w_out"][j], f"{i}")
            grads["gm_w_in"][j], grads["gm_b_in"][j], grads["gm_v_gain"][j] = dwin, dbin[0], dvg[0]
            grads["gm_w_s"][j], grads["gm_b_s"][j], grads["gm_w_out"][j] = dws, dbs, dwout
        else:
            dcur, dg, gs = ssm_bwd(dcur, sv, gmix, ssm_params(j), f"{i}", plan)
            grads["ssm_w_in"][j], grads["ssm_conv_w"][j], grads["ssm_conv_b"][j] = gs["w_in"], gs["conv_w"], gs["conv_b"][0]
            grads["ssm_dt_bias"][j], grads["ssm_a_log"][j], grads["ssm_d"][j] = gs["dt_bias"], gs["a_log"], gs["d"]
            grads["ssm_norm_gain"][j], grads["ssm_w_out"][j] = gs["norm_gain"][0], gs["w_out"]
        grads["mix_norm"][i] = dg[0]
        mixer = {0: ("sb_w_qkv", "sb_w_o"), 1: ("gm_w_in", "gm_w_out"), 2: ("ssm_w_in", "ssm_w_out")}[kind]
        plan.grads_ready({(n, j): grads[n][j] for n in mixer})
    grads = {k: (v if k in MATRICES else jnp.stack(v)) for k, v in grads.items()}
    return loss, dcur[0], grads


WEIGHTS = ["mix_norm", "ffn_norm", "sb_w_qkv", "sb_q_gain", "sb_k_gain", "sb_w_o", "gm_w_in", "gm_b_in", "gm_v_gain",
           "gm_w_s", "gm_b_s", "gm_w_out", "ssm_w_in", "ssm_conv_w", "ssm_conv_b", "ssm_dt_bias", "ssm_a_log", "ssm_d",
           "ssm_norm_gain", "ssm_w_out", "ffn_w_gu", "ffn_w_down"]
SHARDED = {"sb_w_qkv": 2, "sb_w_o": 1, "gm_w_in": 2, "gm_w_out": 1, "ssm_w_in": 2, "ssm_conv_w": 2, "ssm_conv_b": 1,
           "ssm_norm_gain": 1, "ssm_w_out": 1, "ffn_w_gu": 2, "ffn_w_down": 1}
EXACT = ("ssm_conv_w", "ssm_conv_b", "ssm_norm_gain")
MATRICES = tuple(n for n in SHARDED if n not in EXACT)
COLUMN_BLOCKS = ("sb_w_qkv", "gm_w_in", "ffn_w_gu")
REPLICATED = [n for n in WEIGHTS if n not in SHARDED]
N_CHIPS = 4
N_DEV = 8
PACK_COLS = 1024


def _pack(pieces, dtype, align):
    flat = jnp.concatenate([p.reshape(-1).astype(dtype) for p in pieces])
    rows = -(-flat.shape[0] // (PACK_COLS * align)) * align
    flat = jnp.pad(flat, (0, rows * PACK_COLS - flat.shape[0]))
    return flat.reshape(rows, PACK_COLS)


def _unpack(flat, shapes):
    out, off = [], 0
    for shp in shapes:
        n = math.prod(shp)
        out.append(flat[off:off + n].reshape(shp))
        off += n
    return out


ANY = pl.BlockSpec(memory_space=pl.ANY)


def _pos():
    return lax.axis_index("x"), lax.axis_index("y"), lax.axis_index("c")


def _remote(src, dst, send, recv, k, to):
    return pltpu.make_async_remote_copy(src_ref=src, dst_ref=dst, send_sem=send.at[k], recv_sem=recv.at[k],
                                        device_id=to, device_id_type=MESH_ID)


def _comm_call(body, name, ins, out_shapes, nsem, aliases=None):
    return pl.pallas_call(
        body, name=name, out_shape=out_shapes,
        in_specs=[ANY] * len(ins), out_specs=[ANY] * len(out_shapes),
        scratch_shapes=[pltpu.SemaphoreType.DMA((nsem,)), pltpu.SemaphoreType.DMA((nsem,))],
        input_output_aliases=aliases or {},
    )(*ins)


def stage_shard(w, chip, name):
    rows, cols = w.shape
    tr = _pick(rows, (256, 352, 128))

    def kern(idx_ref, w_ref, o_ref):
        o_ref[...] = w_ref[...].astype(BF16)

    grid_spec = pltpu.PrefetchScalarGridSpec(
        num_scalar_prefetch=1, grid=(rows // tr,),
        in_specs=[pl.BlockSpec((tr, cols), lambda i, idx: (i, 0))],
        out_specs=pl.BlockSpec((None, tr, cols), lambda i, idx: (idx[0], i, 0)))
    return pl.pallas_call(
        kern, name=name, grid_spec=grid_spec,
        out_shape=jax.ShapeDtypeStruct((N_CHIPS, rows, cols), BF16),
        compiler_params=_params(("parallel",)),
    )(jnp.reshape(chip, (1,)).astype(jnp.int32), w)


class Side:
    def __init__(self, arrays, out_shapes, aliases, nsem, start, finish):
        self.arrays, self.out_shapes, self.aliases, self.nsem = list(arrays), list(out_shapes), aliases, nsem
        self.start, self.finish = start, finish


def run_side(side, name):
    n_in, n_out = len(side.arrays), len(side.out_shapes)

    def body(*refs):
        ins, outs = refs[:n_in], refs[n_in:n_in + n_out]
        send, recv = refs[n_in + n_out:]
        side.start(ins, outs, send, recv)
        side.finish(ins, outs, send, recv)

    return _comm_call(body, name, side.arrays, side.out_shapes, side.nsem, aliases=side.aliases)


def side_call(kern, side, *, name, grid, in_specs, out_specs, out_shape, scratch_shapes, args):
    if side is None:
        res = pl.pallas_call(kern, name=name, grid=grid, in_specs=in_specs, out_specs=out_specs, out_shape=out_shape,
                             scratch_shapes=scratch_shapes,
                             compiler_params=_params(("parallel",) + ("arbitrary",) * (len(grid) - 1)))(*args)
        return list(res), []
    n_in, n_out, n_scr = len(in_specs), len(out_specs), len(scratch_shapes)
    s_in, s_out = len(side.arrays), len(side.out_shapes)

    def body(*refs):
        ins, refs = refs[:n_in], refs[n_in:]
        side_ins, refs = refs[:s_in], refs[s_in:]
        outs, refs = refs[:n_out], refs[n_out:]
        side_outs, refs = refs[:s_out], refs[s_out:]
        scr, (send, recv) = refs[:n_scr], refs[n_scr:]
        first, last = None, None
        for axis, size in enumerate(grid):
            at0, at1 = pl.program_id(axis) == 0, pl.program_id(axis) == size - 1
            first = at0 if first is None else first & at0
            last = at1 if last is None else last & at1

        @pl.when(first)
        def _():
            side.start(side_ins, side_outs, send, recv)

        kern(*ins, *outs, *scr)

        @pl.when(last)
        def _():
            side.finish(side_ins, side_outs, send, recv)

    res = pl.pallas_call(
        body, name=name, grid=grid,
        in_specs=list(in_specs) + [ANY] * s_in, out_specs=list(out_specs) + [ANY] * s_out,
        out_shape=list(out_shape) + side.out_shapes,
        scratch_shapes=list(scratch_shapes) + [pltpu.SemaphoreType.DMA((side.nsem,)), pltpu.SemaphoreType.DMA((side.nsem,))],
        input_output_aliases={n_in + a: n_out + b for a, b in side.aliases.items()},
        compiler_params=_params(("arbitrary",) * len(grid)),
    )(*args, *side.arrays)
    return list(res[:n_out]), list(res[n_out:])


def gather_side(staged):
    n = len(staged)

    def plan(o_refs, send, recv):
        x, y, c = _pos()
        chips = [(1 - x, y), (x, 1 - y), (1 - x, 1 - y)]

        def part(u, chip, cc):
            half = staged[u].shape[1] // 2
            return o_refs[u].at[2 * chip[0] + chip[1], pl.ds(cc * half, half), :]

        first = [_remote(part(u, (x, y), c), part(u, (x, y), c), send, recv, 6 * u + j, (*chip, c))
                 for u in range(n) for j, chip in enumerate(chips)]
        landed = [_remote(part(u, chip, c), part(u, chip, c), send, recv, 6 * u + j, (x, y, c))
                  for u in range(n) for j, chip in enumerate(chips)]
        passed = [_remote(part(u, chip, c), part(u, chip, c), send, recv, 6 * u + 3 + j, (x, y, 1 - c))
                  for u in range(n) for j, chip in enumerate(chips)]
        handed = [_remote(part(u, chip, 1 - c), part(u, chip, 1 - c), send, recv, 6 * u + 3 + j, (x, y, c))
                  for u in range(n) for j, chip in enumerate(chips)]
        return first, landed, passed, handed

    def start(ins, outs, send, recv):
        for cp in plan(outs, send, recv)[0]:
            cp.start()

    def finish(ins, outs, send, recv):
        first, landed, passed, handed = plan(outs, send, recv)
        for got, fw in zip(landed, passed):
            got.wait_recv()
            fw.start()
        for got in handed:
            got.wait_recv()
        for cp in first + passed:
            cp.wait_send()

    outs = [jax.ShapeDtypeStruct(s.shape, s.dtype) for s in staged]
    return Side(staged, outs, {u: u for u in range(n)}, 6 * n, start, finish)


def swap_halves(gps, name):
    n = len(gps)

    def body(*refs):
        g_refs, r_refs = refs[:n], refs[n:2 * n]
        send, recv = refs[2 * n:]
        x, y, c = _pos()
        cps = []
        for u in range(n):
            half = gps[u].shape[1] // 2
            cps.append(_remote(g_refs[u].at[:, pl.ds((1 - c) * half, half), :], r_refs[u], send, recv, u, (x, y, 1 - c)))
        for cp in cps:
            cp.start()
        for cp in cps:
            cp.wait()

    outs = [jax.ShapeDtypeStruct((g.shape[0], g.shape[1] // 2, g.shape[2]), g.dtype) for g in gps]
    return _comm_call(body, name, gps, outs, n)


def scatter_side(parts):
    n = len(parts)

    def plan(p_refs, r_refs, send, recv):
        x, y, c = _pos()
        chips = [(1 - x, y), (x, 1 - y), (1 - x, 1 - y)]
        return [_remote(p_refs[u].at[2 * chip[0] + chip[1]], r_refs[u].at[j], send, recv, 3 * u + j, (*chip, c))
                for u in range(n) for j, chip in enumerate(chips)]

    def start(ins, outs, send, recv):
        for cp in plan(ins, outs, send, recv):
            cp.start()

    def finish(ins, outs, send, recv):
        for cp in plan(ins, outs, send, recv):
            cp.wait()

    outs = [jax.ShapeDtypeStruct((N_CHIPS - 1,) + p.shape[1:], p.dtype) for p in parts]
    return Side(parts, outs, {}, 3 * n, start, finish)


def join_halves(bufs):
    n = len(bufs)

    def body(*refs):
        o_refs = refs[n:2 * n]
        send, recv = refs[2 * n:]
        x, y, c = _pos()

        def rows(u, cc):
            half = bufs[u].shape[0] // 2
            return o_refs[u].at[pl.ds(cc * half, half), :]

        cps = [_remote(rows(u, c), rows(u, c), send, recv, u, (x, y, 1 - c)) for u in range(n)]
        for cp in cps:
            cp.start()
        for u in range(n):
            _remote(rows(u, 1 - c), rows(u, 1 - c), send, recv, u, (x, y, c)).wait_recv()
        for cp in cps:
            cp.wait_send()

    outs = [jax.ShapeDtypeStruct(b.shape, b.dtype) for b in bufs]
    return _comm_call(body, "join_halves", bufs, outs, n, aliases={u: u for u in range(n)})


def gather_small(sg, name):
    rows, cols = sg.shape

    def body(s_ref, o_ref, send, recv, lsem):
        x, y, c = _pos()
        me, sibling = (x, y, c), (x, y, 1 - c)
        chips = [(1 - x, y), (x, 1 - y), (1 - x, 1 - y)]

        def blk(px, py, pc):
            return o_ref.at[4 * px + 2 * py + pc]

        mine = pltpu.make_async_copy(s_ref, blk(*me), lsem)
        mine.start()
        first = [_remote(s_ref, blk(*me), send, recv, 0, sibling)]
        first += [_remote(s_ref, blk(*me), send, recv, 1 + j, (*chip, c)) for j, chip in enumerate(chips)]
        for cp in first:
            cp.start()
        passed = [_remote(blk(*chip, c), blk(*chip, c), send, recv, 4 + j, sibling) for j, chip in enumerate(chips)]
        for j, chip in enumerate(chips):
            _remote(blk(*chip, c), blk(*chip, c), send, recv, 1 + j, me).wait_recv()
            passed[j].start()
        _remote(blk(*sibling), blk(*sibling), send, recv, 0, me).wait_recv()
        for j, chip in enumerate(chips):
            _remote(blk(*chip, 1 - c), blk(*chip, 1 - c), send, recv, 4 + j, me).wait_recv()
        for cp in first + passed:
            cp.wait_send()
        mine.wait()

    return pl.pallas_call(
        body, name=name,
        out_shape=jax.ShapeDtypeStruct((N_DEV, rows, cols), sg.dtype),
        in_specs=[ANY], out_specs=ANY,
        scratch_shapes=[pltpu.SemaphoreType.DMA((N_DEV - 1,)), pltpu.SemaphoreType.DMA((N_DEV - 1,)), pltpu.SemaphoreType.DMA],
    )(sg)


def sum_cores(gp, theirs, core, chip, name):
    nch, rows, cols = gp.shape
    half = rows // 2
    tr = _pick(half, (256, 176, 128, 64))
    nb = half // tr

    def kern(idx_ref, g_ref, t_ref, own_ref, all_ref):
        k = pl.program_id(1)
        s = g_ref[...].astype(F32) + t_ref[...].astype(F32)
        all_ref[...] = s.astype(BF16)

        @pl.when(k == idx_ref[1])
        def _():
            own_ref[...] = s

    grid_spec = pltpu.PrefetchScalarGridSpec(
        num_scalar_prefetch=1, grid=(nb, nch),
        in_specs=[pl.BlockSpec((None, tr, cols), lambda i, k, idx: (k, idx[0] * nb + i, 0)),
                  pl.BlockSpec((None, tr, cols), lambda i, k, idx: (k, i, 0))],
        out_specs=[pl.BlockSpec((tr, cols), lambda i, k, idx: (i, 0)),
                   pl.BlockSpec((None, tr, cols), lambda i, k, idx: (k, i, 0))])
    return pl.pallas_call(
        kern, name=name, grid_spec=grid_spec,
        out_shape=[jax.ShapeDtypeStruct((half, cols), F32), jax.ShapeDtypeStruct((nch, half, cols), BF16)],
        compiler_params=_params(("parallel", "arbitrary")),
    )(jnp.stack([core, chip]).astype(jnp.int32), gp, theirs)


def sum_chips(own, others, core, name):
    half, cols = own.shape
    tr = _pick(half, (256, 176, 128, 64))
    nb = half // tr

    def kern(idx_ref, o_ref, a_ref, b_ref, c_ref, out_ref):
        out_ref[...] = ((o_ref[...] + a_ref[...].astype(F32)) + b_ref[...].astype(F32)) + c_ref[...].astype(F32)

    grid_spec = pltpu.PrefetchScalarGridSpec(
        num_scalar_prefetch=1, grid=(nb,),
        in_specs=[pl.BlockSpec((tr, cols), lambda i, idx: (i, 0))] +
                 [pl.BlockSpec((None, tr, cols), lambda i, idx, j=j: (j, i, 0)) for j in range(N_CHIPS - 1)],
        out_specs=pl.BlockSpec((tr, cols), lambda i, idx: (idx[0] * nb + i, 0)))
    return pl.pallas_call(
        kern, name=name, grid_spec=grid_spec,
        out_shape=jax.ShapeDtypeStruct((2 * half, cols), F32),
        compiler_params=_params(("parallel",)),
    )(jnp.reshape(core, (1,)).astype(jnp.int32), own, others, others, others)


def small_update(gath, w, m, v, name):
    def fn(*vs):
        g = vs[0]
        for t in vs[1:N_DEV]:
            g = g + t
        wv, mv, vv = vs[N_DEV:]
        m2 = ADAM_B1 * mv + (1.0 - ADAM_B1) * g
        v2 = ADAM_B2 * vv + (1.0 - ADAM_B2) * (g * g)
        m_hat = m2 / (1.0 - ADAM_B1 ** ADAM_STEP)
        v_hat = v2 / (1.0 - ADAM_B2 ** ADAM_STEP)
        return g, -ADAM_LR * (m_hat / (jnp.sqrt(v_hat) + ADAM_EPS) + ADAM_WD * wv), m2, v2

    c = w.shape[1]
    ins = [(gath[k], "row") for k in range(N_DEV)] + [(w, "row"), (m, "row"), (v, "row")]
    return rowwise(fn, ins, [(c, F32)] * 4, tr=w.shape[0] // 2, name=name)


_MIX = {0: [("sb_w_qkv", 0), ("sb_w_o", 0)], 1: [("gm_w_in", 0), ("gm_w_out", 0)],
        2: [("ssm_w_in", 0), ("ssm_w_out", 0)], 3: [("sb_w_qkv", 1), ("sb_w_o", 1)]}
_FFN = {i: [("ffn_w_gu", i), ("ffn_w_down", i)] for i in range(4)}
GATHER_FIRST = _MIX[0][:1]
GATHER_AT = {"sb_attn_0": _MIX[0][1:] + _FFN[0] + _MIX[1] + _FFN[1],
             "ffn_gu_0": _FFN[2][:1], "ffn_down_0": _FFN[2][1:], "ffn_gu_1": _MIX[2][:1], "ffn_down_1": _MIX[2][1:],
             "ssm_scan_2": _MIX[3] + _FFN[3]}
SCATTER_AT = {"ssm_dscan_2": _FFN[3] + _MIX[3] + _FFN[2], "sb_dattn_0": _MIX[2] + _FFN[1] + _MIX[1] + _FFN[0]}
SCATTER_LAST = _MIX[0]


class _Plan:
    def __init__(self, ins, core, chip):
        self.core, self.chip = core, chip
        self.staged = {(n, l): stage_shard(ins[n][l], chip, f"stage_{n}_{l}")
                       for n in MATRICES for l in range(ins[n].shape[0])}
        self.full = {n: [None] * ins[n].shape[0] for n in MATRICES}
        self.ready = {}
        self.parts = {}
        self.halves = {}
        self.swaps = 0
        self._fill(GATHER_FIRST, run_side(gather_side([self.staged[u] for u in GATHER_FIRST]), "gather_first"))

    def _fill(self, units, gathered):
        for (n, l), g in zip(units, gathered):
            if n in COLUMN_BLOCKS:
                self.full[n][l] = g
            elif n == "ssm_w_in":
                self.full[n][l] = jnp.concatenate([g[k] for k in range(N_CHIPS)], axis=1)
            else:
                self.full[n][l] = g.reshape(-1, g.shape[-1])

    def _prepare(self, units):
        gps = [self.ready[u] for u in units]
        theirs = swap_halves(gps, f"swap_halves_{self.swaps}")
        self.swaps += 1
        for (n, l), g, t in zip(units, gps, theirs):
            self.parts[(n, l)] = sum_cores(g, t, self.core, self.chip, f"sum_cores_{n}_{l}")

    def _reduce(self, units, others):
        for (n, l), other in zip(units, others):
            self.halves[(n, l)] = sum_chips(self.parts[(n, l)][0], other, self.core, f"sum_chips_{n}_{l}")

    def side(self, tag):
        if tag in GATHER_AT:
            return gather_side([self.staged[u] for u in GATHER_AT[tag]])
        if tag in SCATTER_AT:
            self._prepare(SCATTER_AT[tag])
            return scatter_side([self.parts[u][1] for u in SCATTER_AT[tag]])
        return None

    def done(self, tag, results):
        if tag in GATHER_AT:
            self._fill(GATHER_AT[tag], results)
        else:
            self._reduce(SCATTER_AT[tag], results)

    def grads_ready(self, grads):
        for (n, l), g in grads.items():
            if n in COLUMN_BLOCKS:
                self.ready[(n, l)] = g
            elif n == "ssm_w_in":
                self.ready[(n, l)] = jnp.stack(jnp.split(g, N_CHIPS, axis=1))
            else:
                self.ready[(n, l)] = g.reshape(N_CHIPS, -1, g.shape[-1])

    def shard_grads(self):
        self._prepare(SCATTER_LAST)
        self._reduce(SCATTER_LAST, run_side(scatter_side([self.parts[u][1] for u in SCATTER_LAST]), "scatter_last"))
        units = sorted(self.halves)
        return dict(zip(units, join_halves([self.halves[u] for u in units])))


def _step(ins):
    x, target = ins["x"][0], ins["loss_target"][0]
    core = lax.axis_index("c")
    chip = 2 * lax.axis_index("x") + lax.axis_index("y")

    def lane_pad(v):
        return jnp.pad(v, ((0, 0), (0, PACK_COLS - v.shape[1])))

    vec_rows = [ins["ssm_conv_w"][0], ins["ssm_conv_b"], lane_pad(ins["ssm_norm_gain"])]
    blk = jnp.concatenate(vec_rows + [jnp.zeros((SUBLANES - 6, PACK_COLS), F32)], axis=0)
    per_chip = gather_small(blk, "gather_vectors")[0::2]
    ngw = ins["ssm_norm_gain"].shape[1]
    full = {
        "ssm_conv_w": jnp.concatenate([per_chip[k, 0:4] for k in range(N_CHIPS)], axis=1)[None],
        "ssm_conv_b": jnp.concatenate([per_chip[k, 4:5] for k in range(N_CHIPS)], axis=1),
        "ssm_norm_gain": jnp.concatenate([per_chip[k, 5:6, :ngw] for k in range(N_CHIPS)], axis=1),
    }

    plan = _Plan(ins, core, chip)
    full.update(plan.full)
    for n in REPLICATED:
        full[n] = ins[n]

    loss, dx, grads = local_step(x, target, full, plan)
    loss = lax.psum(loss, ALL_AXES)
    gshards = plan.shard_grads()

    small_shapes = [ins[n].shape for n in REPLICATED]
    vec_shapes = [grads[n].shape for n in EXACT]
    vec_pack = _pack([grads[n] for n in EXACT], F32, SUBLANES)
    gath = gather_small(jnp.concatenate([_pack([grads[n] for n in REPLICATED], F32, SUBLANES), vec_pack], axis=0),
                        "gather_small")
    packed = [jnp.concatenate([_pack([ins[pre + n] for n in REPLICATED], F32, SUBLANES), jnp.zeros_like(vec_pack)], axis=0)
              for pre in ("", "m_", "v_")]
    res = small_update(gath, *packed, name="small_update")
    nrep = res[0].shape[0] - vec_pack.shape[0]
    small = [dict(zip(REPLICATED, _unpack(r[:nrep].reshape(-1), small_shapes))) for r in res]
    vec_g = dict(zip(EXACT, _unpack(res[0][nrep:].reshape(-1), vec_shapes)))

    out_g, out_d, out_m, out_v = {}, {}, {}, {}
    for n in REPLICATED:
        out_g[n], out_d[n], out_m[n], out_v[n] = (s[n] for s in small)
    for n in SHARDED:
        shp = ins[n].shape
        if n in EXACT:
            g = lax.dynamic_slice_in_dim(vec_g[n], chip * shp[-1], shp[-1], axis=vec_g[n].ndim - 1)
        else:
            g = jnp.stack([gshards[(n, l)] for l in range(shp[0])])
        two = (math.prod(shp[:-1]), shp[-1])
        d2, m2, v2 = adamw(ins[n].reshape(two), g.reshape(two), ins["m_" + n].reshape(two),
                           ins["v_" + n].reshape(two), f"adamw_{n}")
        out_g[n], out_d[n], out_m[n], out_v[n] = g, d2.reshape(shp), m2.reshape(shp), v2.reshape(shp)
    return (loss, dx[None], *[out_g[n] for n in WEIGHTS], *[out_d[n] for n in WEIGHTS],
            *[out_m[n] for n in WEIGHTS], *[out_v[n] for n in WEIGHTS])


def kernel(x, mix_norm, ffn_norm, sb_w_qkv, sb_q_gain, sb_k_gain, sb_w_o, gm_w_in, gm_b_in, gm_v_gain, gm_w_s, gm_b_s, gm_w_out, ssm_w_in, ssm_conv_w, ssm_conv_b, ssm_dt_bias, ssm_a_log, ssm_d, ssm_norm_gain, ssm_w_out, ffn_w_gu, ffn_w_down, loss_target, m_mix_norm, m_ffn_norm, m_sb_w_qkv, m_sb_q_gain, m_sb_k_gain, m_sb_w_o, m_gm_w_in, m_gm_b_in, m_gm_v_gain, m_gm_w_s, m_gm_b_s, m_gm_w_out, m_ssm_w_in, m_ssm_conv_w, m_ssm_conv_b, m_ssm_dt_bias, m_ssm_a_log, m_ssm_d, m_ssm_norm_gain, m_ssm_w_out, m_ffn_w_gu, m_ffn_w_down, v_mix_norm, v_ffn_norm, v_sb_w_qkv, v_sb_q_gain, v_sb_k_gain, v_sb_w_o, v_gm_w_in, v_gm_b_in, v_gm_v_gain, v_gm_w_s, v_gm_b_s, v_gm_w_out, v_ssm_w_in, v_ssm_conv_w, v_ssm_conv_b, v_ssm_dt_bias, v_ssm_a_log, v_ssm_d, v_ssm_norm_gain, v_ssm_w_out, v_ffn_w_gu, v_ffn_w_down):
    return _step(dict(locals()))
```

```python
import functools
import math

import jax
import jax.numpy as jnp
from jax import lax
from jax.experimental import pallas as pl
from jax.experimental.pallas import tpu as pltpu

F32 = jnp.float32
BF16 = jnp.bfloat16
EPS = 1e-6
LANES = 128
SUBLANES = 8
VMEM_LIMIT = 56 * 1024 * 1024
HEAD = 64
CHUNK = 128
SB_TQ, SB_TK = 256, 256
SSD_SUB = 8
SB_DEAD = -110.0
SB_UNSEEN = -1e30
ADAM_LR, ADAM_B1, ADAM_B2, ADAM_EPS, ADAM_WD, ADAM_STEP = 0.001, 0.9, 0.999, 1e-08, 0.01, 10
MESH_ID = pl.DeviceIdType.MESH
ALL_AXES = ("x", "y", "c")


def _params(sem):
    return pltpu.CompilerParams(dimension_semantics=sem, vmem_limit_bytes=VMEM_LIMIT)


def _pick(n, cands):
    for c in cands:
        if n % c == 0:
            return c
    return n


def _dot(a, b, dims=((1,), (0,))):
    return lax.dot_general(a, b, (dims, ((), ())), preferred_element_type=F32)


def _dot_nt(a, b):
    return _dot(a, b, ((1,), (1,)))


def _dot_tn(a, b):
    return _dot(a, b, ((0,), (0,)))


def _split2(x):
    hi = x.astype(BF16)
    lo = (x - hi.astype(F32)).astype(BF16)
    return hi, lo


def _dot_x2(x, m):
    hi, lo = _split2(x)
    return _dot(hi, m) + _dot(lo, m)


def _dot_x3_left(m, x):
    h1 = x.astype(BF16)
    r1 = x - h1.astype(F32)
    h2 = r1.astype(BF16)
    h3 = (r1 - h2.astype(F32)).astype(BF16)
    return _dot(m, h1) + _dot(m, h2) + _dot(m, h3)


def _sigmoid(x):
    return 1.0 / (1.0 + jnp.exp(-x))


def _softplus(x):
    return jnp.maximum(x, 0.0) + jnp.log(1.0 + jnp.exp(-jnp.abs(x)))


def _colsum(x):
    return jnp.sum(x, axis=0, keepdims=True)


def _rowsum(x):
    return jnp.sum(x, axis=1, keepdims=True)


def _iota2(shape, dim):
    return lax.broadcasted_iota(jnp.int32, shape, dim)


MM_VMEM_BUDGET = 40 * 1024 * 1024
MM_STEP_US = 0.35
MM_HBM_BYTES_PER_US = 3.0e6
MM_VMEM_BYTES_PER_US = 1.5e6
MM_FLOPS_PER_US = 9.0e8
MXU_DIM = 256


def _mm_tiles(m, n, kk, wn, wk, a_bytes, b_bytes, has_add):
    def divisors(total, cands):
        got = [c for c in cands if total % c == 0 and c <= total]
        return got or [total]

    best = None
    for tm in divisors(m, (1024, 512, 256, 128)):
        for tn in divisors(wn, (1024, 768, 1408, 512, 256, 128)):
            for tk in divisors(wk, (4096, 2816, 2048, 1408, 1024, 768, 512, 256, 128)):
                nk = kk // tk
                vmem = 2 * (tm * tk * a_bytes + tk * tn * b_bytes + tm * tn * 4 * (2 if has_add else 1))
                vmem += tm * tn * 4 if nk > 1 else 0
                if vmem > MM_VMEM_BUDGET:
                    continue
                steps = (m // tm) * (n // tn) * nk
                a_reads = 1 if nk == 1 else n // tn
                traffic = m * kk * a_bytes * a_reads + kk * n * b_bytes * (m // tm) + m * n * 4
                fill = min(1.0, tn / MXU_DIM) * min(1.0, tm / MXU_DIM)
                compute = 2.0 * m * n * kk / (MM_FLOPS_PER_US * fill)
                cost = steps * MM_STEP_US + max(compute, traffic / MM_HBM_BYTES_PER_US)
                if nk > 1:
                    cost += steps * tm * tn * 8 / MM_VMEM_BYTES_PER_US
                if best is None or cost < best[0]:
                    best = (cost, tm, tn, tk)
    return best[1:]


def mm(a, b, *, ta=False, tb=False, add=None, bias=None, a_chunks=False, b_chunks=False, out_chunks=False,
       out_dtype=F32, name, side=None):
    wa = None
    if a_chunks:
        m, wa = a.shape[1], a.shape[2]
        kk = a.shape[0] * wa
    elif ta:
        kk, m = a.shape
    else:
        m, kk = a.shape
    nch, wide = 1, None
    if b_chunks:
        nch, rows_b, wide = b.shape
        kb, n = (rows_b, nch * wide) if not tb else (nch * wide, rows_b)
    elif tb:
        n, kb = b.shape
    else:
        kb, n = b.shape
    wide_o = n // N_CHIPS if out_chunks else None
    assert kk == kb, (a.shape, b.shape, ta, tb)
    has_add, has_bias = add is not None, bias is not None
    wk = wide if (wide and tb) else kk
    wn = wide if (wide and not tb) else n
    tm, tn, tk = _mm_tiles(m, n, kk, math.gcd(wn, wide_o) if wide_o else wn, math.gcd(wk, wa) if wa else wk,
                           a.dtype.itemsize, b.dtype.itemsize, has_add)
    nk = kk // tk
    dims = ((0 if ta else 1,), (1 if tb else 0,))

    def kern(*refs):
        a_ref, b_ref = refs[0], refs[1]
        rest = list(refs[2:])
        add_ref = rest.pop(0) if has_add else None
        bias_ref = rest.pop(0) if has_bias else None
        o_ref = rest[0]
        part = _dot(a_ref[...].astype(BF16), b_ref[...].astype(BF16), dims)

        def finish(r):
            if has_add:
                r = r + add_ref[...]
            if has_bias:
                r = r + bias_ref[...]
            o_ref[...] = r.astype(out_dtype)

        if nk == 1:
            finish(part)
        else:
            acc_ref = rest[1]
            k = pl.program_id(2)

            @pl.when(k == 0)
            def _():
                acc_ref[...] = part

            @pl.when((k > 0) & (k < nk - 1))
            def _():
                acc_ref[...] += part

            @pl.when(k == nk - 1)
            def _():
                finish(acc_ref[...] + part)

    if a_chunks:
        per_a = wa // tk
        a_spec = pl.BlockSpec((None, tm, tk), lambda i, j, k: (k // per_a, i, k % per_a))
    elif ta:
        a_spec = pl.BlockSpec((tk, tm), lambda i, j, k: (k, i))
    else:
        a_spec = pl.BlockSpec((tm, tk), lambda i, j, k: (i, k))
    if b_chunks and tb:
        per = wide // tk
        b_spec = pl.BlockSpec((None, tn, tk), lambda i, j, k: (k // per, j, k % per))
    elif b_chunks:
        per = wide // tn
        b_spec = pl.BlockSpec((None, tk, tn), lambda i, j, k: (j // per, k, j % per))
    elif tb:
        b_spec = pl.BlockSpec((tn, tk), lambda i, j, k: (j, k))
    else:
        b_spec = pl.BlockSpec((tk, tn), lambda i, j, k: (k, j))
    if out_chunks:
        per_o = wide_o // tn
        out_spec = pl.BlockSpec((None, tm, tn), lambda i, j, k: (j // per_o, i, j % per_o))
        out_shape = jax.ShapeDtypeStruct((N_CHIPS, m, wide_o), out_dtype)
    else:
        out_spec = pl.BlockSpec((tm, tn), lambda i, j, k: (i, j))
        out_shape = jax.ShapeDtypeStruct((m, n), out_dtype)
    in_specs, args = [a_spec, b_spec], [a, b]
    if has_add:
        in_specs.append(pl.BlockSpec((tm, tn), lambda i, j, k: (i, j)))
        args.append(add)
    if has_bias:
        in_specs.append(pl.BlockSpec((1, tn), lambda i, j, k: (0, j)))
        args.append(bias)
    (out,), side_outs = side_call(
        kern, side,
        name=name,
        grid=(m // tm, n // tn, nk),
        in_specs=in_specs,
        out_specs=[out_spec],
        out_shape=[out_shape],
        scratch_shapes=[pltpu.VMEM((tm, tn), F32)] if nk > 1 else [],
        args=args)
    return out if side is None else (out, side_outs)


def mm_hooked(plan, a, b, *, name, **kw):
    side = plan.side(name)
    if side is None:
        return mm(a, b, name=name, **kw)
    out, side_outs = mm(a, b, name=name, side=side, **kw)
    plan.done(name, side_outs)
    return out


def rowwise(fn, ins, outs, accs=(), *, tr, name):
    rows = [a for a, kind in ins if kind == "row"][0].shape[0]
    tr = min(tr, rows)
    assert rows % tr == 0 and tr % SUBLANES == 0, (rows, tr)
    n = rows // tr
    n_in, n_out = len(ins), len(outs)
    kinds = [kind for _, kind in ins]

    def kern(*refs):
        i = pl.program_id(0)
        vals = []
        for ref, kind in zip(refs[:n_in], kinds):
            v = ref[...]
            if kind == "prev":
                v = v * (i > 0).astype(v.dtype)
            elif kind == "next":
                v = v * (i < n - 1).astype(v.dtype)
            vals.append(v)
        res = fn(*vals)
        for ref, r in zip(refs[n_in:n_in + n_out], res[:n_out]):
            ref[...] = r.astype(ref.dtype)
        if accs:
            acc_refs = refs[n_in + n_out:]

            @pl.when(i == 0)
            def _():
                for ref in acc_refs:
                    ref[...] = jnp.zeros_like(ref)

            for ref, r in zip(acc_refs, res[n_out:]):
                ref[...] += r

    in_specs = []
    for a, kind in ins:
        if kind == "row":
            in_specs.append(pl.BlockSpec((tr, a.shape[1]), lambda i: (i, 0)))
        elif kind == "full":
            in_specs.append(pl.BlockSpec(a.shape, lambda i, nd=a.ndim: (0,) * nd))
        elif kind == "prev":
            in_specs.append(pl.BlockSpec((SUBLANES, a.shape[1]),
                                         lambda i: (jnp.maximum(i * (tr // SUBLANES) - 1, 0), 0)))
        else:
            in_specs.append(pl.BlockSpec((SUBLANES, a.shape[1]),
                                         lambda i: (jnp.minimum((i + 1) * (tr // SUBLANES), rows // SUBLANES - 1), 0)))
    out_specs = [pl.BlockSpec((tr, c), lambda i: (i, 0)) for c, _ in outs]
    out_specs += [pl.BlockSpec((r, c), lambda i: (0, 0)) for r, c in accs]
    out_shape = [jax.ShapeDtypeStruct((rows, c), dt) for c, dt in outs]
    out_shape += [jax.ShapeDtypeStruct((r, c), F32) for r, c in accs]
    res = pl.pallas_call(
        kern,
        name=name,
        grid=(n,),
        in_specs=in_specs,
        out_specs=out_specs,
        out_shape=out_shape,
        compiler_params=_params(("arbitrary",) if accs else ("parallel",)),
    )(*[a for a, _ in ins])
    return res


def rms_fwd(x, g, name):
    def fn(xv, gv):
        r = lax.rsqrt(jnp.mean(xv * xv, axis=1, keepdims=True) + EPS)
        return (xv * r * gv,)

    return rowwise(fn, [(x, "row"), (g, "full")], [(x.shape[1], BF16)], tr=512, name=name)[0]


def rms_bwd(x, g, dy, dres, name):
    def fn(xv, gv, dyv, drv):
        r = lax.rsqrt(jnp.mean(xv * xv, axis=1, keepdims=True) + EPS)
        xh = xv * r
        dyg = dyv * gv
        dx = drv + r * (dyg - xh * jnp.mean(dyg * xh, axis=1, keepdims=True))
        return dx, dx, _colsum(dyv * xh)

    c = x.shape[1]
    dx, dxb, dg = rowwise(fn, [(x, "row"), (g, "full"), (dy, "row"), (dres, "row")], [(c, F32), (c, BF16)], [(1, c)],
                          tr=256, name=name)
    return (dx, dxb), dg


def ffn_up(h, wgu, name, side=None):
    s, d = h.shape
    nch, _, w = wgu.shape
    half = nch // 2
    tm = _pick(s, (512, 256, 128))

    def kern(h_ref, wg_ref, wu_ref, gu_ref, a_ref):
        hv = h_ref[...]
        g = _dot(hv, wg_ref[...])
        u = _dot(hv, wu_ref[...])
        gu_ref[0] = g.astype(BF16)
        gu_ref[1] = u.astype(BF16)
        a_ref[...] = (g * _sigmoid(g) * u).astype(BF16)

    return side_call(
        kern, side, name=name, grid=(s // tm, half),
        in_specs=[pl.BlockSpec((tm, d), lambda i, j: (i, 0)),
                  pl.BlockSpec((None, d, w), lambda i, j: (j, 0, 0)),
                  pl.BlockSpec((None, d, w), lambda i, j: (j + half, 0, 0))],
        out_specs=[pl.BlockSpec((2, tm, w), lambda i, j: (0, i, j)), pl.BlockSpec((tm, w), lambda i, j: (i, j))],
        out_shape=[jax.ShapeDtypeStruct((2, s, half * w), BF16), jax.ShapeDtypeStruct((s, half * w), BF16)],
        scratch_shapes=[], args=(h, wgu, wgu))


def ffn_dact(dxb, wdown, gu, name):
    s, d = dxb.shape
    hid = wdown.shape[0]
    tm = _pick(s, (512, 256, 128))
    tn = _pick(hid, (1408, 512, 256, 128))

    def kern(dx_ref, w_ref, gu_ref, o_ref):
        da = _dot_nt(dx_ref[...], w_ref[...])
        g, u = gu_ref[0].astype(F32), gu_ref[1].astype(F32)
        sg = _sigmoid(g)
        o_ref[0] = (da * u * sg * (1.0 + g * (1.0 - sg))).astype(BF16)
        o_ref[1] = (da * g * sg).astype(BF16)

    return pl.pallas_call(
        kern, name=name, grid=(s // tm, hid // tn),
        in_specs=[pl.BlockSpec((tm, d), lambda i, j: (i, 0)), pl.BlockSpec((tn, d), lambda i, j: (j, 0)),
                  pl.BlockSpec((2, tm, tn), lambda i, j: (0, i, j))],
        out_specs=pl.BlockSpec((2, tm, tn), lambda i, j: (0, i, j)),
        out_shape=jax.ShapeDtypeStruct((2, s, hid), BF16),
        compiler_params=_params(("parallel", "parallel")),
    )(dxb, wdown, gu)


def loss_and_grad(y, t, name):
    d = y.shape[1]

    def fn(yv, tv):
        e = yv - tv
        part = jnp.sum(_colsum(e * e), axis=1, keepdims=True) * (0.5 / d)
        dy = e * (1.0 / d)
        return dy, dy, jnp.broadcast_to(part, (SUBLANES, LANES))

    dy, dyb, acc = rowwise(fn, [(y, "row"), (t, "row")], [(d, F32), (d, BF16)], [(SUBLANES, LANES)], tr=512, name=name)
    return acc[0, 0], (dy, dyb)


def adamw(w, g, m, v, name):
    def fn(wv, gv, mv, vv):
        m2 = ADAM_B1 * mv + (1.0 - ADAM_B1) * gv
        v2 = ADAM_B2 * vv + (1.0 - ADAM_B2) * (gv * gv)
        m_hat = m2 / (1.0 - ADAM_B1 ** ADAM_STEP)
        v_hat = v2 / (1.0 - ADAM_B2 ** ADAM_STEP)
        delta = -ADAM_LR * (m_hat / (jnp.sqrt(v_hat) + ADAM_EPS) + ADAM_WD * wv)
        return delta, m2, v2

    rows, c = w.shape
    tr = _pick(rows, (256, 128, 64, 32, 16, 8)) if rows % SUBLANES == 0 else rows
    if rows % SUBLANES:
        return _whole(fn, [w, g, m, v], [(w.shape, F32)] * 3, name=name)
    return rowwise(fn, [(w, "row"), (g, "row"), (m, "row"), (v, "row")], [(c, F32)] * 3, tr=tr, name=name)


def _whole(fn, ins, outs, *, name):
    n_in = len(ins)

    def kern(*refs):
        res = fn(*[r[...] for r in refs[:n_in]])
        for ref, r in zip(refs[n_in:], res):
            ref[...] = r.astype(ref.dtype)

    return pl.pallas_call(
        kern,
        name=name,
        out_shape=[jax.ShapeDtypeStruct(s, dt) for s, dt in outs],
        compiler_params=pltpu.CompilerParams(vmem_limit_bytes=VMEM_LIMIT),
    )(*ins)


def ffn_fwd(x, g, wgu, wdown, tag, plan):
    h = rms_fwd(x, g, f"ffn_rms_{tag}")
    gu, a = _hooked(plan, f"ffn_gu_{tag}", ffn_up, h, wgu)
    xn = mm_hooked(plan, a, wdown, add=x, name=f"ffn_down_{tag}")
    return xn, (x, h, gu, a)


def ffn_bwd(dxn, saved, g, wgu, wdown, tag):
    x, h, gu, a = saved
    dxn, dxb = dxn
    dwdown = mm(a, dxb, ta=True, out_dtype=BF16, name=f"ffn_dwdown_{tag}")
    dgu = ffn_dact(dxb, wdown, gu, f"ffn_dact_{tag}")
    dh = mm(dgu, wgu, tb=True, a_chunks=True, b_chunks=True, name=f"ffn_dh_{tag}")
    dwgu = mm(h, dgu, ta=True, b_chunks=True, out_dtype=BF16, out_chunks=True, name=f"ffn_dwgu_{tag}")
    dx, dg = rms_bwd(x, g, dh, dxn, f"ffn_drms_{tag}")
    return dx, dg, dwgu, dwdown


def _head_blockdiag(c):
    i = jnp.arange(c) // HEAD
    return (i[:, None] == i[None, :]).astype(BF16)


def _head_sums(x, bd):
    return jnp.concatenate([_dot_x2(x[:, g * LANES:(g + 1) * LANES], bd) for g in range(x.shape[1] // LANES)], axis=1)


def qknorm_fwd(qkv, qg, kg, bd, name):
    d = qkv.shape[1] // 3
    scale = 1.0 / math.sqrt(HEAD)

    def fn(v, qgv, kgv, bdv):
        v = v.astype(F32)
        q, k, vv = v[:, :d], v[:, d:2 * d], v[:, 2 * d:]
        rq = lax.rsqrt(_head_sums(q * q, bdv) * (1.0 / HEAD) + EPS)
        rk = lax.rsqrt(_head_sums(k * k, bdv) * (1.0 / HEAD) + EPS)
        return q * rq * qgv * scale, k * rk * kgv, vv

    return rowwise(fn, [(qkv, "row"), (qg, "full"), (kg, "full"), (bd, "full")],
                   [(d, BF16), (d, BF16), (d, BF16)], tr=256, name=name)


def qknorm_bwd(qkv, dqs, dkn, dv, qg, kg, bd, name):
    d = qkv.shape[1] // 3
    scale = 1.0 / math.sqrt(HEAD)

    def one(xv, gv, dyv, bdv):
        r = lax.rsqrt(_head_sums(xv * xv, bdv) * (1.0 / HEAD) + EPS)
        xh = xv * r
        dyg = dyv * gv
        dx = r * (dyg - xh * (_head_sums(dyg * xh, bdv) * (1.0 / HEAD)))
        return dx, _colsum(dyv * xh)

    def fn(v, dqv, dkv, dvv, qgv, kgv, bdv):
        v = v.astype(F32)
        q, k = v[:, :d], v[:, d:2 * d]
        dq, dqg = one(q, qgv, dqv * scale, bdv)
        dk, dkg = one(k, kgv, dkv, bdv)
        return jnp.concatenate([dq, dk, dvv], axis=1), dqg, dkg

    return rowwise(fn, [(qkv, "row"), (dqs, "row"), (dkn, "row"), (dv, "row"), (qg, "full"), (kg, "full"), (bd, "full")],
                   [(3 * d, BF16)], [(1, d), (1, d)], tr=256, name=name)


def _sb_tile(qh, k, mask, tri_gt):
    z = _dot_nt(qh, k)
    sp = jnp.log(1.0 + jnp.exp(-jnp.abs(z)))
    lb = jnp.minimum(z, 0.0) - sp
    l1 = jnp.where(mask, lb - z, 0.0)
    suf = _dot(l1.astype(BF16), tri_gt)
    return lb, l1, suf


def _sb_tri(tk):
    i = jnp.arange(tk)
    return jnp.stack([i[:, None] > i[None, :], i[:, None] < i[None, :]]).astype(BF16)


def _sb_setup(tq, tk):
    row, col = _iota2((tq, tk), 0), _iota2((tq, tk), 1)
    lane = _iota2((1, LANES), 1)
    halves = [(lane < HEAD).astype(BF16), (lane >= HEAD).astype(BF16)]
    lane_q = _iota2((tq, LANES), 1) + jnp.minimum(_iota2((tq, LANES), 0), 0)
    return row, col, halves, lane_q


def sb_attn_fwd(qs, kn, vb, tri, name, side=None):
    s, d = qs.shape
    tq, tk = min(SB_TQ, s), min(SB_TK, s)
    nq = s // tq
    assert s // tk <= LANES and s % tq == 0 and s % tk == 0

    def kern(q_ref, k_ref, v_ref, tri_ref, o_ref, rs_ref, acc_ref):
        i = pl.program_id(1)
        row, col, halves, lane_q = _sb_setup(tq, tk)
        q = q_ref[...]
        qh = [q * hm for hm in halves]
        acc_ref[...] = jnp.zeros_like(acc_ref)
        rs_ref[...] = jnp.full(rs_ref.shape, SB_UNSEEN, F32)
        nkb = (i + 1) * (tq // tk)

        def more(st):
            return (st[0] < nkb) & (st[1] > SB_DEAD)

        def step(st):
            n, r = st[0], list(st[2:])
            kb = nkb - 1 - n
            ks = pl.multiple_of(kb * tk, tk)
            k = k_ref[pl.ds(ks, tk), :]
            v = v_ref[pl.ds(ks, tk), :]
            mask = col < row + (i * tq - kb * tk)
            at_kb = lane_q == kb
            for hh in range(2):
                lb, l1, suf = _sb_tile(qh[hh], k, mask, tri_ref[0])
                w = jnp.where(mask, jnp.exp(lb + suf + r[hh]), 0.0)
                acc_ref[...] += _dot(w.astype(BF16), v * halves[hh])
                rs_ref[hh] = jnp.where(at_kb, r[hh], rs_ref[hh])
                r[hh] = r[hh] + _rowsum(l1)
            return (n + 1, jnp.maximum(jnp.max(r[0]), jnp.max(r[1])), r[0], r[1])

        z1 = jnp.zeros((tq, 1), F32)
        lax.while_loop(more, step, (jnp.int32(0), jnp.float32(0.0), z1, z1))
        o_ref[...] = acc_ref[...].astype(BF16)

    nh2 = d // LANES
    return side_call(
        kern, side,
        name=name,
        grid=(nh2, nq),
        in_specs=[pl.BlockSpec((tq, LANES), lambda h, i: (i, h)),
                  pl.BlockSpec((s, LANES), lambda h, i: (0, h)),
                  pl.BlockSpec((s, LANES), lambda h, i: (0, h)),
                  pl.BlockSpec((2, tk, tk), lambda h, i: (0, 0, 0))],
        out_specs=[pl.BlockSpec((tq, LANES), lambda h, i: (i, h)),
                   pl.BlockSpec((None, 2, tq, LANES), lambda h, i: (h, 0, i, 0))],
        out_shape=[jax.ShapeDtypeStruct((s, d), BF16), jax.ShapeDtypeStruct((nh2, 2, s, LANES), F32)],
        scratch_shapes=[pltpu.VMEM((tq, LANES), F32)],
        args=(qs, kn, vb, tri))


def sb_attn_bwd(qs, kn, vb, rsave, do, tri, name, side=None):
    s, d = qs.shape
    tq, tk = min(SB_TQ, s), min(SB_TK, s)
    nq = s // tq

    def kern(q_ref, k_ref, v_ref, rs_ref, do_ref, tri_ref, dq_ref, dk_ref, dv_ref):
        i = pl.program_id(1)

        @pl.when(i == 0)
        def _():
            dk_ref[...] = jnp.zeros_like(dk_ref)
            dv_ref[...] = jnp.zeros_like(dv_ref)

        row, col, halves, lane_q = _sb_setup(tq, tk)
        q = q_ref[...]
        qh = [q * hm for hm in halves]
        dov = do_ref[...].astype(BF16)
        doh = [dov * hm for hm in halves]
        dq_ref[...] = jnp.zeros_like(dq_ref)
        nkb = (i + 1) * (tq // tk)
        top = jnp.maximum(jnp.max(rs_ref[0], axis=0, keepdims=True), jnp.max(rs_ref[1], axis=0, keepdims=True))
        dead = (top <= SB_DEAD) & (_iota2((1, LANES), 1) < nkb)
        kstart = jnp.minimum(jnp.sum(dead.astype(F32)).astype(jnp.int32), nkb)

        def step(kb, ep):
            ep = list(ep)
            ks = pl.multiple_of(kb * tk, tk)
            k = k_ref[pl.ds(ks, tk), :]
            v = v_ref[pl.ds(ks, tk), :]
            mask = col < row + (i * tq - kb * tk)
            at_kb = lane_q == kb
            for hh in range(2):
                lb, l1, suf = _sb_tile(qh[hh], k, mask, tri_ref[0])
                r = _rowsum(jnp.where(at_kb, rs_ref[hh], 0.0))
                lbm = jnp.where(mask, lb, SB_UNSEEN)
                w = jnp.exp(lbm + suf + r)
                e = _dot_nt(doh[hh], v) * w
                pe = ep[hh] + _dot(e.astype(BF16), tri_ref[1])
                beta = jnp.exp(lbm)
                dz = (e - beta * (e + pe)).astype(BF16)
                dq_ref[...] += _dot(dz, k * halves[hh])
                dk_ref[pl.ds(ks, tk), :] += _dot_tn(dz, qh[hh])
                dv_ref[pl.ds(ks, tk), :] += _dot_tn(w.astype(BF16), doh[hh])
                ep[hh] = ep[hh] + _rowsum(e)
            return tuple(ep)

        z1 = jnp.zeros((tq, 1), F32)
        lax.fori_loop(kstart, nkb, step, (z1, z1))

    nh2 = d // LANES
    return side_call(
        kern, side,
        name=name,
        grid=(nh2, nq),
        in_specs=[pl.BlockSpec((tq, LANES), lambda h, i: (i, h)),
                  pl.BlockSpec((s, LANES), lambda h, i: (0, h)),
                  pl.BlockSpec((s, LANES), lambda h, i: (0, h)),
                  pl.BlockSpec((None, 2, tq, LANES), lambda h, i: (h, 0, i, 0)),
                  pl.BlockSpec((tq, LANES), lambda h, i: (i, h)),
                  pl.BlockSpec((2, tk, tk), lambda h, i: (0, 0, 0))],
        out_specs=[pl.BlockSpec((tq, LANES), lambda h, i: (i, h)),
                   pl.BlockSpec((s, LANES), lambda h, i: (0, h)),
                   pl.BlockSpec((s, LANES), lambda h, i: (0, h))],
        out_shape=[jax.ShapeDtypeStruct((s, d), F32)] * 3,
        scratch_shapes=[],
        args=(qs, kn, vb, rsave, do, tri))


def _hooked(plan, tag, call, *args):
    side = plan.side(tag)
    outs, side_outs = call(*args, tag, side)
    if side is not None:
        plan.done(tag, side_outs)
    return outs


def sb_fwd(x, g, wqkv, qg, kg, wo, bd, tag, plan):
    h = rms_fwd(x, g, f"sb_rms_{tag}")
    qkv = mm(h, wqkv, b_chunks=True, out_dtype=BF16, name=f"sb_qkv_{tag}")
    qs, kn, vb = qknorm_fwd(qkv, qg, kg, bd, f"sb_qknorm_{tag}")
    o, rsave = _hooked(plan, f"sb_attn_{tag}", sb_attn_fwd, qs, kn, vb, _sb_tri(min(SB_TK, x.shape[0])))
    xn = mm(o, wo(), add=x, name=f"sb_out_{tag}")
    return xn, (x, h, qkv, qs, kn, vb, rsave, o)


def sb_bwd(dxn, saved, g, wqkv, qg, kg, wo, bd, tag, plan):
    x, h, qkv, qs, kn, vb, rsave, o = saved
    dxn, dxb = dxn
    do = mm(dxb, wo, tb=True, name=f"sb_do_{tag}")
    dwo = mm(o, dxb, ta=True, out_dtype=BF16, name=f"sb_dwo_{tag}")
    dqs, dkn, dv = _hooked(plan, f"sb_dattn_{tag}", sb_attn_bwd, qs, kn, vb, rsave, do, _sb_tri(min(SB_TK, x.shape[0])))
    dqkv, dqg, dkg = qknorm_bwd(qkv, dqs, dkn, dv, qg, kg, bd, f"sb_dqknorm_{tag}")
    dh = mm(dqkv, wqkv, tb=True, b_chunks=True, name=f"sb_dh_{tag}")
    dwqkv = mm(h, dqkv, ta=True, out_dtype=BF16, out_chunks=True, name=f"sb_dwqkv_{tag}")
    dx, dg = rms_bwd(x, g, dh, dxn, f"sb_drms_{tag}")
    nh = dqg.shape[1] // HEAD
    return dx, dg, dwqkv, dqg.reshape(nh, HEAD).sum(0), dkg.reshape(nh, HEAD).sum(0), dwo


def _gelu(x):
    return 0.5 * x * (1.0 + lax.erf(x * (1.0 / math.sqrt(2.0))))


def _gelu_grad(x):
    return 0.5 * (1.0 + lax.erf(x * (1.0 / math.sqrt(2.0)))) + x * jnp.exp(-0.5 * x * x) * (1.0 / math.sqrt(2.0 * math.pi))


def gm_act_fwd(pre, vg, name):
    half = pre.shape[1] // 2

    def fn(p, vgv):
        p = p.astype(F32)
        u = _gelu(p[:, :half])
        v = _gelu(p[:, half:])
        r = lax.rsqrt(jnp.mean(v * v, axis=1, keepdims=True) + EPS)
        return u, v * r * vgv

    return rowwise(fn, [(pre, "row"), (vg, "full")], [(half, F32), (half, BF16)], tr=256, name=name)


def gm_act_bwd(pre, du, dvn, vg, name):
    half = pre.shape[1] // 2

    def fn(p, duv, dvnv, vgv):
        p = p.astype(F32)
        pu, pv = p[:, :half], p[:, half:]
        v = _gelu(pv)
        r = lax.rsqrt(jnp.mean(v * v, axis=1, keepdims=True) + EPS)
        vh = v * r
        dyg = dvnv * vgv
        dv = r * (dyg - vh * jnp.mean(dyg * vh, axis=1, keepdims=True))
        dpre = jnp.concatenate([duv * _gelu_grad(pu), dv * _gelu_grad(pv)], axis=1)
        return dpre, _colsum(dvnv * vh), _colsum(dpre)

    return rowwise(fn, [(pre, "row"), (du, "row"), (dvn, "row"), (vg, "full")],
                   [(2 * half, BF16)], [(1, half), (1, 2 * half)], tr=256, name=name)


def gm_spatial_fwd(u, vn, wc, bst, name):
    s, c = u.shape
    t = CHUNK
    ng = c // LANES

    def kern(u_ref, v_ref, w_ref, b_ref, o_ref):
        for g in range(ng):
            sl = slice(g * LANES, (g + 1) * LANES)
            mixed = _dot(w_ref[g], v_ref[:, sl]) + b_ref[:, sl]
            o_ref[:, sl] = (u_ref[:, sl] * mixed).astype(BF16)

    return pl.pallas_call(
        kern,
        name=name,
        grid=(s // t,),
        in_specs=[pl.BlockSpec((t, c), lambda i: (i, 0)), pl.BlockSpec((t, c), lambda i: (i, 0)),
                  pl.BlockSpec(wc.shape, lambda i: (0, 0, 0)), pl.BlockSpec(bst.shape, lambda i: (0, 0))],
        out_specs=pl.BlockSpec((t, c), lambda i: (i, 0)),
        out_shape=jax.ShapeDtypeStruct((s, c), BF16),
        compiler_params=_params(("parallel",)),
    )(u, vn, wc, bst)


def gm_spatial_bwd(dgate, u, vn, wc, bst, name):
    s, c = u.shape
    t = CHUNK
    ng = c // LANES

    def kern(dg_ref, u_ref, v_ref, w_ref, b_ref, du_ref, dv_ref, dw_ref, db_ref):
        i = pl.program_id(0)

        @pl.when(i == 0)
        def _():
            dw_ref[...] = jnp.zeros_like(dw_ref)
            db_ref[...] = jnp.zeros_like(db_ref)

        for g in range(ng):
            sl = slice(g * LANES, (g + 1) * LANES)
            vg = v_ref[:, sl]
            dgv = dg_ref[:, sl]
            mixed = _dot(w_ref[g], vg) + b_ref[:, sl]
            du_ref[:, sl] = dgv * mixed
            dmix = dgv * u_ref[:, sl]
            dmb = dmix.astype(BF16)
            dv_ref[:, sl] = _dot_tn(w_ref[g], dmb)
            dw_ref[g] += _dot_nt(dmb, vg)
            db_ref[:, sl] += dmix

    return pl.pallas_call(
        kern,
        name=name,
        grid=(s // t,),
        in_specs=[pl.BlockSpec((t, c), lambda i: (i, 0))] * 3 +
                 [pl.BlockSpec(wc.shape, lambda i: (0, 0, 0)), pl.BlockSpec(bst.shape, lambda i: (0, 0))],
        out_specs=[pl.BlockSpec((t, c), lambda i: (i, 0)), pl.BlockSpec((t, c), lambda i: (i, 0)),
                   pl.BlockSpec(wc.shape, lambda i: (0, 0, 0)), pl.BlockSpec(bst.shape, lambda i: (0, 0))],
        out_shape=[jax.ShapeDtypeStruct((s, c), F32), jax.ShapeDtypeStruct((s, c), F32),
                   jax.ShapeDtypeStruct(wc.shape, F32), jax.ShapeDtypeStruct(bst.shape, F32)],
        compiler_params=_params(("arbitrary",)),
    )(dgate, u, vn, wc, bst)


def gm_fwd(x, g, w_in, b_in, vg, wc, bst, w_out, tag):
    h = rms_fwd(x, g, f"gm_rms_{tag}")
    pre = mm(h, w_in, bias=b_in, b_chunks=True, out_dtype=BF16, name=f"gm_in_{tag}")
    u, vn = gm_act_fwd(pre, vg, f"gm_act_{tag}")
    gate = gm_spatial_fwd(u, vn, wc, bst, f"gm_spatial_{tag}")
    xn = mm(gate, w_out, add=x, name=f"gm_out_{tag}")
    return xn, (x, h, pre, u, vn, gate)


def gm_bwd(dxn, saved, g, w_in, vg, wc, bst, w_out, tag):
    x, h, pre, u, vn, gate = saved
    dxn, dxb = dxn
    dgate = mm(dxb, w_out, tb=True, name=f"gm_dgate_{tag}")
    dwout = mm(gate, dxb, ta=True, out_dtype=BF16, name=f"gm_dwout_{tag}")
    du, dvn, dws, dbst = gm_spatial_bwd(dgate, u, vn, wc, bst, f"gm_dspatial_{tag}")
    dpre, dvg, dbin = gm_act_bwd(pre, du, dvn, vg, f"gm_dact_{tag}")
    dh = mm(dpre, w_in, tb=True, b_chunks=True, name=f"gm_dh_{tag}")
    dwin = mm(h, dpre, ta=True, out_dtype=BF16, out_chunks=True, name=f"gm_dwin_{tag}")
    dx, dg = rms_bwd(x, g, dh, dxn, f"gm_drms_{tag}")
    ng = wc.shape[0]
    dws = jnp.where(jnp.tril(jnp.ones((CHUNK, CHUNK), bool)), dws, 0.0)
    dbs = dbst.reshape(CHUNK, ng, LANES).sum(-1).T
    return dx, dg, dwin, dbin, dvg, dws, dbs, dwout


def _conv_taps(xv, prev):
    cat = jnp.concatenate([prev, xv], axis=0)
    return [pltpu.roll(cat, sh, 0)[SUBLANES:] for sh in (3, 2, 1)] + [xv]


def conv_fwd(xbc, ws, b, d_inner, name):
    c = xbc.shape[1]
    nst = (c - d_inner) // 2

    def fn(xv, prev, w0, w1, w2, w3, bv):
        taps = _conv_taps(xv, prev)
        pre = bv + w0 * taps[0] + w1 * taps[1] + w2 * taps[2] + w3 * taps[3]
        out = pre * _sigmoid(pre)
        return out[:, :d_inner], out[:, d_inner:d_inner + nst], out[:, d_inner + nst:]

    return rowwise(fn, [(xbc, "row"), (xbc, "prev")] + [(w, "full") for w in ws] + [(b, "full")],
                   [(d_inner, F32), (nst, F32), (nst, F32)], tr=256, name=name)


def conv_bwd_pre(xbc, ws, b, dxs_a, dxs_b, db_m, dc_m, name):
    c = xbc.shape[1]

    def fn(xv, prev, w0, w1, w2, w3, bv, da, db2, dbm, dcm):
        taps = _conv_taps(xv, prev)
        pre = bv + w0 * taps[0] + w1 * taps[1] + w2 * taps[2] + w3 * taps[3]
        sg = _sigmoid(pre)
        dout = jnp.concatenate([da + db2, dbm, dcm], axis=1)
        dpre = dout * sg * (1.0 + pre * (1.0 - sg))
        return (dpre,) + tuple(_colsum(dpre * tp) for tp in taps) + (_colsum(dpre),)

    return rowwise(fn, [(xbc, "row"), (xbc, "prev")] + [(w, "full") for w in ws] +
                   [(b, "full"), (dxs_a, "row"), (dxs_b, "row"), (db_m, "row"), (dc_m, "row")],
                   [(c, F32)], [(1, c)] * 5, tr=256, name=name)


def conv_bwd_in(dpre, ws, name):
    c = dpre.shape[1]

    def fn(dv, nxt, w0, w1, w2, w3):
        cat = jnp.concatenate([dv, nxt], axis=0)
        n = cat.shape[0]
        up = [pltpu.roll(cat, n - sh, 0)[:dv.shape[0]] for sh in (1, 2, 3)]
        return (w3 * dv + w2 * up[0] + w1 * up[1] + w0 * up[2],)

    return rowwise(fn, [(dpre, "row"), (dpre, "next")] + [(w, "full") for w in ws], [(c, BF16)], tr=256, name=name)[0]


def ssd_pre(dtr, bias, alog, name):
    def fn(d, bv, al, tri):
        dt = _softplus(d + bv)
        a = dt * (-jnp.exp(al))
        return dt, _dot_x3_left(tri, a)

    tri = jnp.tril(jnp.ones((CHUNK, CHUNK), BF16))
    return rowwise(fn, [(dtr, "row"), (bias, "full"), (alog, "full"), (tri, "full")],
                   [(LANES, F32), (LANES, F32)], tr=CHUNK, name=name)


def _ssd_layouts(v, ngroups, hpg):
    s = v.shape[0]
    col = v[:, :ngroups * hpg].T.reshape(ngroups, hpg, s, 1)
    return jnp.broadcast_to(col, (ngroups, hpg, s, LANES))


def _ssd_rowform(acum, ngroups, hpg):
    s = acum.shape[0]
    nc = s // CHUNK
    a = acum[:, :ngroups * hpg].reshape(nc, CHUNK, ngroups, hpg).transpose(2, 0, 3, 1)
    last = jnp.broadcast_to(a[..., CHUNK - 1:], a.shape)
    return jnp.concatenate([a, last], axis=2)


def ssd_chunk_fwd(xs, bm, cm, col_a, col_dt, rowf, name, side=None):
    s, d_inner = xs.shape
    ln = CHUNK
    nc = s // ln
    nsub = _pick(nc, (SSD_SUB, 2, 1))
    rows = nsub * ln
    ng, hpg = col_a.shape[0], col_a.shape[1]
    gw = d_inner // ng
    assert gw == hpg * HEAD and gw % LANES == 0 and bm.shape[1] == ng * LANES

    def kern(x_ref, b_ref, c_ref, ca_ref, cd_ref, rf_ref, y_ref, hp_ref, h_scr):
        @pl.when(pl.program_id(1) == 0)
        def _():
            h_scr[...] = jnp.zeros_like(h_scr)

        causal = _iota2((ln, ln), 0) >= _iota2((ln, ln), 1)
        lane = _iota2((1, LANES), 1)
        for sc in range(nsub):
            rs = slice(sc * ln, (sc + 1) * ln)
            bb = b_ref[rs, :].astype(BF16)
            cbf = c_ref[rs, :].astype(BF16)
            cb = _dot_nt(cbf, bb)
            ys = [jnp.zeros((ln, LANES), F32) for _ in range(gw // LANES)]
            for r in range(hpg):
                j, hf = divmod(r, LANES // HEAD)
                mh = ((lane >= HEAD * hf) & (lane < HEAD * (hf + 1))).astype(F32)
                ac = ca_ref[r, rs, :]
                ar = rf_ref[sc, pl.ds(r, 1), :]
                aend = rf_ref[sc, pl.ds(4 + r, 1), :]
                dm = jnp.exp(jnp.minimum(ac - ar, 0.0))
                m = jnp.where(causal, cb * dm, 0.0).astype(BF16)
                xdt = x_ref[rs, j * LANES:(j + 1) * LANES] * cd_ref[r, rs, :] * mh
                h = h_scr[r]
                hp_ref[sc, r] = h
                ys[j] = ys[j] + _dot(m, xdt.astype(BF16)) + _dot_nt(cbf, h.astype(BF16)) * jnp.exp(ac)
                dte = jnp.exp(aend - ac)
                h_scr[r] = jnp.exp(aend) * h + _dot_tn((xdt * dte).astype(BF16), bb)
            for j in range(gw // LANES):
                y_ref[rs, j * LANES:(j + 1) * LANES] = ys[j]

    colspec = pl.BlockSpec((None, hpg, rows, LANES), lambda g, c: (g, 0, c, 0))
    return side_call(
        kern, side,
        name=name,
        grid=(ng, nc // nsub),
        in_specs=[pl.BlockSpec((rows, gw), lambda g, c: (c, g)),
                  pl.BlockSpec((rows, LANES), lambda g, c: (c, g)),
                  pl.BlockSpec((rows, LANES), lambda g, c: (c, g)),
                  colspec, colspec,
                  pl.BlockSpec((None, nsub, 8, LANES), lambda g, c: (g, c, 0, 0))],
        out_specs=[pl.BlockSpec((rows, gw), lambda g, c: (c, g)),
                   pl.BlockSpec((None, nsub, hpg, LANES, LANES), lambda g, c: (g, c, 0, 0, 0))],
        out_shape=[jax.ShapeDtypeStruct((s, d_inner), F32),
                   jax.ShapeDtypeStruct((ng, nc, hpg, LANES, LANES), F32)],
        scratch_shapes=[pltpu.VMEM((hpg, LANES, LANES), F32)],
        args=(xs, bm, cm, col_a, col_dt, rowf))


def ssd_chunk_bwd(xs, bm, cm, col_a, col_dt, rowf, hprev, dy, name, side=None):
    s, d_inner = xs.shape
    ln = CHUNK
    nc = s // ln
    nsub = _pick(nc, (SSD_SUB, 2, 1))
    rows = nsub * ln
    ng, hpg = col_a.shape[0], col_a.shape[1]
    gw = d_inner // ng

    def kern(x_ref, b_ref, c_ref, ca_ref, cd_ref, rf_ref, hp_ref, dy_ref,
             dx_ref, db_ref, dc_ref, ddt_ref, da_ref, dh_scr):
        @pl.when(pl.program_id(1) == 0)
        def _():
            dh_scr[...] = jnp.zeros_like(dh_scr)

        row, col = _iota2((ln, ln), 0), _iota2((ln, ln), 1)
        causal = row >= col
        tri_ge = (col >= row).astype(BF16)
        ones = jnp.ones((ln, LANES), BF16)
        lane = _iota2((1, LANES), 1)
        last_row = (_iota2((ln, 1), 0) == ln - 1).astype(F32)
        for sc in reversed(range(nsub)):
            rs = slice(sc * ln, (sc + 1) * ln)
            bb = b_ref[rs, :].astype(BF16)
            cbf = c_ref[rs, :].astype(BF16)
            cb = _dot_nt(cbf, bb)
            dcb = jnp.zeros((ln, ln), F32)
            d_b = jnp.zeros((ln, LANES), F32)
            d_c = jnp.zeros((ln, LANES), F32)
            dxs = [jnp.zeros((ln, LANES), F32) for _ in range(gw // LANES)]
            for r in range(hpg):
                j, hf = divmod(r, LANES // HEAD)
                mh = ((lane >= HEAD * hf) & (lane < HEAD * (hf + 1))).astype(F32)
                ac = ca_ref[r, rs, :]
                dt = cd_ref[r, rs, :]
                ar = rf_ref[sc, pl.ds(r, 1), :]
                aend = rf_ref[sc, pl.ds(4 + r, 1), :]
                dm = jnp.where(causal, jnp.exp(jnp.minimum(ac - ar, 0.0)), 0.0)
                m = cb * dm
                mb = m.astype(BF16)
                xp = x_ref[rs, j * LANES:(j + 1) * LANES]
                xdt = xp * dt * mh
                xdtb = xdt.astype(BF16)
                dyp = dy_ref[rs, j * LANES:(j + 1) * LANES] * mh
                dypb = dyp.astype(BF16)
                h = hp_ref[sc, r]
                hb = h.astype(BF16)
                dh = dh_scr[r]
                dhb = dh.astype(BF16)
                e_in = jnp.exp(ac)
                dte = jnp.exp(aend - ac)
                eend = jnp.exp(aend)
                d_m = _dot_nt(dypb, xdtb)
                dcb = dcb + d_m * dm
                gm = d_m * m
                yoff_pre = _dot_nt(cbf, hb)
                bdh = _dot_nt(bb, dhb)
                dxdt = _dot_tn(mb, dypb) + bdh * dte
                t1 = _rowsum(xdt * bdh) * dte
                gh, gl = _split2(gm)
                dacum = (_rowsum(gm) - (_dot_tn(gh, ones) + _dot_tn(gl, ones))
                         + _rowsum(dyp * yoff_pre) * e_in - t1)
                end_term = _colsum(t1) + eend * jnp.sum(_colsum(dh * h), axis=1, keepdims=True)
                dacum = dacum + last_row * end_term
                da_ref[r, rs, :] = _dot_x3_left(tri_ge, dacum)
                ddt_ref[r, rs, :] = jnp.broadcast_to(_rowsum(dxdt * xp), (ln, LANES))
                dxs[j] = dxs[j] + dxdt * dt
                d_b = d_b + _dot((xdt * dte).astype(BF16), dhb)
                dye = (dyp * e_in).astype(BF16)
                d_c = d_c + _dot(dye, hb)
                dh_scr[r] = eend * dh + _dot_tn(dye, cbf)
            dcbb = dcb.astype(BF16)
            dc_ref[rs, :] = d_c + _dot(dcbb, bb)
            db_ref[rs, :] = d_b + _dot_tn(dcbb, cbf)
            for j in range(gw // LANES):
                dx_ref[rs, j * LANES:(j + 1) * LANES] = dxs[j]

    rev = nc // nsub - 1
    colspec = pl.BlockSpec((None, hpg, rows, LANES), lambda g, c: (g, 0, rev - c, 0))
    return side_call(
        kern, side,
        name=name,
        grid=(ng, nc // nsub),
        in_specs=[pl.BlockSpec((rows, gw), lambda g, c: (rev - c, g)),
                  pl.BlockSpec((rows, LANES), lambda g, c: (rev - c, g)),
                  pl.BlockSpec((rows, LANES), lambda g, c: (rev - c, g)),
                  colspec, colspec,
                  pl.BlockSpec((None, nsub, 8, LANES), lambda g, c: (g, rev - c, 0, 0)),
                  pl.BlockSpec((None, nsub, hpg, LANES, LANES), lambda g, c: (g, rev - c, 0, 0, 0)),
                  pl.BlockSpec((rows, gw), lambda g, c: (rev - c, g))],
        out_specs=[pl.BlockSpec((rows, gw), lambda g, c: (rev - c, g)),
                   pl.BlockSpec((rows, LANES), lambda g, c: (rev - c, g)),
                   pl.BlockSpec((rows, LANES), lambda g, c: (rev - c, g)),
                   colspec, colspec],
        out_shape=[jax.ShapeDtypeStruct((s, d_inner), F32),
                   jax.ShapeDtypeStruct(bm.shape, F32), jax.ShapeDtypeStruct(cm.shape, F32),
                   jax.ShapeDtypeStruct(col_a.shape, F32), jax.ShapeDtypeStruct(col_a.shape, F32)],
        scratch_shapes=[pltpu.VMEM((hpg, LANES, LANES), F32)],
        args=(xs, bm, cm, col_a, col_dt, rowf, hprev, dy))


def gnorm_fwd(y, xs, z, dexp, gain, ngroups, name):
    c = y.shape[1]
    gw = c // ngroups

    def fn(yv, xv, zv, dv, gv):
        yg = (yv + xv * dv) * (zv * _sigmoid(zv))
        outs = []
        for k in range(ngroups):
            t = yg[:, k * gw:(k + 1) * gw]
            outs.append(t * lax.rsqrt(jnp.mean(t * t, axis=1, keepdims=True) + EPS))
        return (jnp.concatenate(outs, axis=1) * gv,)

    return rowwise(fn, [(y, "row"), (xs, "row"), (z, "row"), (dexp, "full"), (gain, "full")], [(c, BF16)], tr=256, name=name)[0]


def gnorm_bwd(dn, y, xs, z, dexp, gain, ngroups, name):
    c = y.shape[1]
    gw = c // ngroups

    def fn(dnv, yv, xv, zv, dv, gv):
        yd = yv + xv * dv
        sg = _sigmoid(zv)
        sz = zv * sg
        yg = yd * sz
        dng = dnv * gv
        dyg, yh = [], []
        for k in range(ngroups):
            sl = slice(k * gw, (k + 1) * gw)
            t = yg[:, sl]
            r = lax.rsqrt(jnp.mean(t * t, axis=1, keepdims=True) + EPS)
            th = t * r
            dyg.append(r * (dng[:, sl] - th * jnp.mean(dng[:, sl] * th, axis=1, keepdims=True)))
            yh.append(th)
        dyg = jnp.concatenate(dyg, axis=1)
        yh = jnp.concatenate(yh, axis=1)
        dyd = dyg * sz
        dz = dyg * yd * (sg * (1.0 + zv * (1.0 - sg)))
        return dyd, dyd * dv, dz, _colsum(dyd * xv), _colsum(dnv * yh)

    return rowwise(fn, [(dn, "row"), (y, "row"), (xs, "row"), (z, "row"), (dexp, "full"), (gain, "full")],
                   [(c, F32), (c, F32), (c, BF16)], [(1, c), (1, c)], tr=256, name=name)


def ssd_post(ddt, da, dt, dtr, bias, alog, name):
    def fn(ddtv, dav, dtv, dtrv, bv, al):
        a_neg = -jnp.exp(al)
        ddtr = (ddtv + dav * a_neg) * _sigmoid(dtrv + bv)
        return ddtr, _colsum(ddtr), _colsum(dav * dtv) * a_neg

    return rowwise(fn, [(ddt, "row"), (da, "row"), (dt, "row"), (dtr, "row"), (bias, "full"), (alog, "full")],
                   [(LANES, BF16)], [(1, LANES), (1, LANES)], tr=512, name=name)


def _from_colform(v, s):
    ng, hpg = v.shape[0], v.shape[1]
    flat = v[..., 0].reshape(ng * hpg, s).T
    return jnp.pad(flat, ((0, 0), (0, LANES - ng * hpg)))


def ssm_fwd(x, g, p, tag, plan):
    ng, hpg, d_inner = p["ng"], p["hpg"], p["d_inner"]
    h = rms_fwd(x, g, f"ssm_rms_{tag}")
    z = mm(h, p["w_z"], name=f"ssm_inz_{tag}")
    xbc = mm(h, p["w_xbc"], name=f"ssm_inx_{tag}")
    dtr = mm(h, p["w_dt"], name=f"ssm_indt_{tag}")
    xs, bm, cm = conv_fwd(xbc, p["conv_w"], p["conv_b"], d_inner, f"ssm_conv_{tag}")
    dt, acum = ssd_pre(dtr, p["dt_bias"], p["a_log"], f"ssm_pre_{tag}")
    col_a, col_dt = _ssd_layouts(acum, ng, hpg), _ssd_layouts(dt, ng, hpg)
    rowf = _ssd_rowform(acum, ng, hpg)
    y, hprev = _hooked(plan, f"ssm_scan_{tag}", ssd_chunk_fwd, xs, bm, cm, col_a, col_dt, rowf)
    n = gnorm_fwd(y, xs, z, p["d_exp"], p["norm_gain"], ng, f"ssm_gnorm_{tag}")
    xn = mm(n, p["w_out"], add=x, name=f"ssm_out_{tag}")
    return xn, (x, h, z, xbc, dtr, xs, bm, cm, dt, col_a, col_dt, rowf, y, hprev, n)


def ssm_bwd(dxn, saved, g, p, tag, plan):
    x, h, z, xbc, dtr, xs, bm, cm, dt, col_a, col_dt, rowf, y, hprev, n = saved
    ng, hpg, d_inner = p["ng"], p["hpg"], p["d_inner"]
    s = x.shape[0]
    dxn, dxb = dxn
    dn = mm(dxb, p["w_out"], tb=True, name=f"ssm_dn_{tag}")
    dwout = mm(n, dxb, ta=True, out_dtype=BF16, name=f"ssm_dwout_{tag}")
    dy, dxs_skip, dz, dd_lane, dgain = gnorm_bwd(dn, y, xs, z, p["d_exp"], p["norm_gain"], ng, f"ssm_dgnorm_{tag}")
    dxs, dbm, dcm, ddt_c, da_c = _hooked(plan, f"ssm_dscan_{tag}", ssd_chunk_bwd, xs, bm, cm, col_a, col_dt, rowf, hprev, dy)
    ddtr, dbias, dalog = ssd_post(_from_colform(ddt_c, s), _from_colform(da_c, s), dt, dtr,
                                  p["dt_bias"], p["a_log"], f"ssm_post_{tag}")
    res = conv_bwd_pre(xbc, p["conv_w"], p["conv_b"], dxs, dxs_skip, dbm, dcm, f"ssm_dconv_{tag}")
    dpre, dconv_w, dconv_b = res[0], jnp.concatenate(res[1:5], axis=0), res[5]
    dxbc = conv_bwd_in(dpre, p["conv_w"], f"ssm_dconvin_{tag}")
    dh = mm(dz, p["w_z"], tb=True, name=f"ssm_dhz_{tag}")
    dh = mm(dxbc, p["w_xbc"], tb=True, add=dh, name=f"ssm_dhx_{tag}")
    dh = mm(ddtr, p["w_dt"], tb=True, add=dh, name=f"ssm_dhdt_{tag}")
    dwz = mm(h, dz, ta=True, out_dtype=BF16, name=f"ssm_dwz_{tag}")
    dwxbc = mm(h, dxbc, ta=True, out_dtype=BF16, name=f"ssm_dwxbc_{tag}")
    dwdt = mm(h, ddtr, ta=True, out_dtype=BF16, name=f"ssm_dwdt_{tag}")
    dx, dg = rms_bwd(x, g, dh, dxn, f"ssm_drms_{tag}")
    nh = ng * hpg
    dwin = jnp.concatenate([dwz, dwxbc, dwdt[:, :nh]], axis=1)
    dd = dd_lane.reshape(nh, HEAD).sum(-1)
    return dx, dg, dict(w_in=dwin, conv_w=dconv_w, conv_b=dconv_b, dt_bias=dbias[0, :nh], a_log=dalog[0, :nh],
                        d=dd, norm_gain=dgain, w_out=dwout)


def local_step(x, target, w, plan):
    d = x.shape[1]
    depth = w["mix_norm"].shape[0]
    bd = _head_blockdiag(LANES)
    tril = jnp.tril(jnp.ones((CHUNK, CHUNK), bool))
    ssm_heads = w["ssm_dt_bias"].shape[1]
    d_inner = w["ssm_norm_gain"].shape[1]
    ng = w["ssm_norm_gain"].shape[1] // 256
    nstate = CHUNK

    def pad_lanes(v):
        return jnp.pad(v, ((0, 0), (0, LANES - v.shape[1])))

    def ssm_params(j):
        w_in = w["ssm_w_in"][j]
        cw = w["ssm_conv_w"][j]
        return dict(ng=ng, hpg=ssm_heads // ng, d_inner=d_inner,
                    w_z=w_in[:, :d_inner], w_xbc=w_in[:, d_inner:d_inner + d_inner + 2 * ng * nstate],
                    w_dt=pad_lanes(w_in[:, 2 * d_inner + 2 * ng * nstate:]),
                    conv_w=[cw[k:k + 1] for k in range(cw.shape[0])], conv_b=w["ssm_conv_b"][j:j + 1],
                    dt_bias=pad_lanes(w["ssm_dt_bias"][j:j + 1]), a_log=pad_lanes(w["ssm_a_log"][j:j + 1]),
                    d_exp=jnp.repeat(w["ssm_d"][j], HEAD)[None, :], norm_gain=w["ssm_norm_gain"][j:j + 1],
                    w_out=w["ssm_w_out"][j])

    def gm_params(j):
        wc = jnp.where(tril, w["gm_w_s"][j], 0.0).astype(BF16)
        bst = jnp.repeat(w["gm_b_s"][j].T, LANES, axis=1)
        return wc, bst

    def sb_gains(j):
        nh = d // HEAD
        return jnp.tile(w["sb_q_gain"][j], nh)[None, :], jnp.tile(w["sb_k_gain"][j], nh)[None, :]

    saved = []
    cur = x
    for i in range(depth):
        kind, j = i % 3, i // 3
        gmix = w["mix_norm"][i:i + 1]
        if kind == 0:
            qg, kg = sb_gains(j)
            cur, sv = sb_fwd(cur, gmix, w["sb_w_qkv"][j], qg, kg, lambda j=j: w["sb_w_o"][j], bd, f"{i}", plan)
        elif kind == 1:
            wc, bst = gm_params(j)
            cur, sv = gm_fwd(cur, gmix, w["gm_w_in"][j], w["gm_b_in"][j:j + 1], w["gm_v_gain"][j:j + 1], wc, bst,
                             w["gm_w_out"][j], f"{i}")
        else:
            cur, sv = ssm_fwd(cur, gmix, ssm_params(j), f"{i}", plan)
        cur, sv2 = ffn_fwd(cur, w["ffn_norm"][i:i + 1], w["ffn_w_gu"][i], w["ffn_w_down"][i], f"{i}", plan)
        saved.append((sv, sv2))

    loss, dcur = loss_and_grad(cur, target, "loss")

    grads = {k: [None] * len(v) for k, v in w.items()}
    for i in reversed(range(depth)):
        kind, j = i % 3, i // 3
        sv, sv2 = saved[i]
        gmix = w["mix_norm"][i:i + 1]
        dcur, dgf, dwgu, dwdown = ffn_bwd(dcur, sv2, w["ffn_norm"][i:i + 1], w["ffn_w_gu"][i], w["ffn_w_down"][i], f"{i}")
        grads["ffn_norm"][i], grads["ffn_w_gu"][i], grads["ffn_w_down"][i] = dgf[0], dwgu, dwdown
        plan.grads_ready({("ffn_w_gu", i): dwgu, ("ffn_w_down", i): dwdown})
        if kind == 0:
            qg, kg = sb_gains(j)
            dcur, dg, dwqkv, dqg, dkg, dwo = sb_bwd(dcur, sv, gmix, w["sb_w_qkv"][j], qg, kg, w["sb_w_o"][j], bd, f"{i}", plan)
            grads["sb_w_qkv"][j], grads["sb_q_gain"][j], grads["sb_k_gain"][j], grads["sb_w_o"][j] = dwqkv, dqg, dkg, dwo
        elif kind == 1:
            wc, bst = gm_params(j)
            dcur, dg, dwin, dbin, dvg, dws, dbs, dwout = gm_bwd(dcur, sv, gmix, w["gm_w_in"][j], w["gm_v_gain"][j:j + 1],
                                                                 wc, bst, w["gm_w_out"][j], f"{i}")
            grads["gm_w_in"][j], grads["gm_b_in"][j], grads["gm_v_gain"][j] = dwin, dbin[0], dvg[0]
            grads["gm_w_s"][j], grads["gm_b_s"][j], grads["gm_w_out"][j] = dws, dbs, dwout
        else:
            dcur, dg, gs = ssm_bwd(dcur, sv, gmix, ssm_params(j), f"{i}", plan)
            grads["ssm_w_in"][j], grads["ssm_conv_w"][j], grads["ssm_conv_b"][j] = gs["w_in"], gs["conv_w"], gs["conv_b"][0]
            grads["ssm_dt_bias"][j], grads["ssm_a_log"][j], grads["ssm_d"][j] = gs["dt_bias"], gs["a_log"], gs["d"]
            grads["ssm_norm_gain"][j], grads["ssm_w_out"][j] = gs["norm_gain"][0], gs["w_out"]
        grads["mix_norm"][i] = dg[0]
        mixer = {0: ("sb_w_qkv", "sb_w_o"), 1: ("gm_w_in", "gm_w_out"), 2: ("ssm_w_in", "ssm_w_out")}[kind]
        plan.grads_ready({(n, j): grads[n][j] for n in mixer})
    grads = {k: (v if k in MATRICES else jnp.stack(v)) for k, v in grads.items()}
    return loss, dcur[0], grads


WEIGHTS = ["mix_norm", "ffn_norm", "sb_w_qkv", "sb_q_gain", "sb_k_gain", "sb_w_o", "gm_w_in", "gm_b_in", "gm_v_gain",
           "gm_w_s", "gm_b_s", "gm_w_out", "ssm_w_in", "ssm_conv_w", "ssm_conv_b", "ssm_dt_bias", "ssm_a_log", "ssm_d",
           "ssm_norm_gain", "ssm_w_out", "ffn_w_gu", "ffn_w_down"]
SHARDED = {"sb_w_qkv": 2, "sb_w_o": 1, "gm_w_in": 2, "gm_w_out": 1, "ssm_w_in": 2, "ssm_conv_w": 2, "ssm_conv_b": 1,
           "ssm_norm_gain": 1, "ssm_w_out": 1, "ffn_w_gu": 2, "ffn_w_down": 1}
EXACT = ("ssm_conv_w", "ssm_conv_b", "ssm_norm_gain")
MATRICES = tuple(n for n in SHARDED if n not in EXACT)
COLUMN_BLOCKS = ("sb_w_qkv", "gm_w_in", "ffn_w_gu")
REPLICATED = [n for n in WEIGHTS if n not in SHARDED]
N_CHIPS = 4
N_DEV = 8
PACK_COLS = 1024


def _pack(pieces, dtype, align):
    flat = jnp.concatenate([p.reshape(-1).astype(dtype) for p in pieces])
    rows = -(-flat.shape[0] // (PACK_COLS * align)) * align
    flat = jnp.pad(flat, (0, rows * PACK_COLS - flat.shape[0]))
    return flat.reshape(rows, PACK_COLS)


def _unpack(flat, shapes):
    out, off = [], 0
    for shp in shapes:
        n = math.prod(shp)
        out.append(flat[off:off + n].reshape(shp))
        off += n
    return out


ANY = pl.BlockSpec(memory_space=pl.ANY)


def _pos():
    return lax.axis_index("x"), lax.axis_index("y"), lax.axis_index("c")


def _remote(src, dst, send, recv, k, to):
    return pltpu.make_async_remote_copy(src_ref=src, dst_ref=dst, send_sem=send.at[k], recv_sem=recv.at[k],
                                        device_id=to, device_id_type=MESH_ID)


def _comm_call(body, name, ins, out_shapes, nsem, aliases=None):
    return pl.pallas_call(
        body, name=name, out_shape=out_shapes,
        in_specs=[ANY] * len(ins), out_specs=[ANY] * len(out_shapes),
        scratch_shapes=[pltpu.SemaphoreType.DMA((nsem,)), pltpu.SemaphoreType.DMA((nsem,))],
        input_output_aliases=aliases or {},
    )(*ins)


def stage_shard(w, chip, name):
    rows, cols = w.shape
    tr = _pick(rows, (256, 352, 128))

    def kern(idx_ref, w_ref, o_ref):
        o_ref[...] = w_ref[...].astype(BF16)

    grid_spec = pltpu.PrefetchScalarGridSpec(
        num_scalar_prefetch=1, grid=(rows // tr,),
        in_specs=[pl.BlockSpec((tr, cols), lambda i, idx: (i, 0))],
        out_specs=pl.BlockSpec((None, tr, cols), lambda i, idx: (idx[0], i, 0)))
    return pl.pallas_call(
        kern, name=name, grid_spec=grid_spec,
        out_shape=jax.ShapeDtypeStruct((N_CHIPS, rows, cols), BF16),
        compiler_params=_params(("parallel",)),
    )(jnp.reshape(chip, (1,)).astype(jnp.int32), w)


class Side:
    def __init__(self, arrays, out_shapes, aliases, nsem, start, finish):
        self.arrays, self.out_shapes, self.aliases, self.nsem = list(arrays), list(out_shapes), aliases, nsem
        self.start, self.finish = start, finish


def run_side(side, name):
    n_in, n_out = len(side.arrays), len(side.out_shapes)

    def body(*refs):
        ins, outs = refs[:n_in], refs[n_in:n_in + n_out]
        send, recv = refs[n_in + n_out:]
        side.start(ins, outs, send, recv)
        side.finish(ins, outs, send, recv)

    return _comm_call(body, name, side.arrays, side.out_shapes, side.nsem, aliases=side.aliases)


def side_call(kern, side, *, name, grid, in_specs, out_specs, out_shape, scratch_shapes, args):
    if side is None:
        res = pl.pallas_call(kern, name=name, grid=grid, in_specs=in_specs, out_specs=out_specs, out_shape=out_shape,
                             scratch_shapes=scratch_shapes,
                             compiler_params=_params(("parallel",) + ("arbitrary",) * (len(grid) - 1)))(*args)
        return list(res), []
    n_in, n_out, n_scr = len(in_specs), len(out_specs), len(scratch_shapes)
    s_in, s_out = len(side.arrays), len(side.out_shapes)

    def body(*refs):
        ins, refs = refs[:n_in], refs[n_in:]
        side_ins, refs = refs[:s_in], refs[s_in:]
        outs, refs = refs[:n_out], refs[n_out:]
        side_outs, refs = refs[:s_out], refs[s_out:]
        scr, (send, recv) = refs[:n_scr], refs[n_scr:]
        first, last = None, None
        for axis, size in enumerate(grid):
            at0, at1 = pl.program_id(axis) == 0, pl.program_id(axis) == size - 1
            first = at0 if first is None else first & at0
            last = at1 if last is None else last & at1

        @pl.when(first)
        def _():
            side.start(side_ins, side_outs, send, recv)

        kern(*ins, *outs, *scr)

        @pl.when(last)
        def _():
            side.finish(side_ins, side_outs, send, recv)

    res = pl.pallas_call(
        body, name=name, grid=grid,
        in_specs=list(in_specs) + [ANY] * s_in, out_specs=list(out_specs) + [ANY] * s_out,
        out_shape=list(out_shape) + side.out_shapes,
        scratch_shapes=list(scratch_shapes) + [pltpu.SemaphoreType.DMA((side.nsem,)), pltpu.SemaphoreType.DMA((side.nsem,))],
        input_output_aliases={n_in + a: n_out + b for a, b in side.aliases.items()},
        compiler_params=_params(("arbitrary",) * len(grid)),
    )(*args, *side.arrays)
    return list(res[:n_out]), list(res[n_out:])


def gather_side(staged):
    n = len(staged)

    def plan(o_refs, send, recv):
        x, y, c = _pos()
        chips = [(1 - x, y), (x, 1 - y), (1 - x, 1 - y)]

        def part(u, chip, cc):
            half = staged[u].shape[1] // 2
            return o_refs[u].at[2 * chip[0] + chip[1], pl.ds(cc * half, half), :]

        first = [_remote(part(u, (x, y), c), part(u, (x, y), c), send, recv, 6 * u + j, (*chip, c))
                 for u in range(n) for j, chip in enumerate(chips)]
        landed = [_remote(part(u, chip, c), part(u, chip, c), send, recv, 6 * u + j, (x, y, c))
                  for u in range(n) for j, chip in enumerate(chips)]
        passed = [_remote(part(u, chip, c), part(u, chip, c), send, recv, 6 * u + 3 + j, (x, y, 1 - c))
                  for u in range(n) for j, chip in enumerate(chips)]
        handed = [_remote(part(u, chip, 1 - c), part(u, chip, 1 - c), send, recv, 6 * u + 3 + j, (x, y, c))
                  for u in range(n) for j, chip in enumerate(chips)]
        return first, landed, passed, handed

    def start(ins, outs, send, recv):
        for cp in plan(outs, send, recv)[0]:
            cp.start()

    def finish(ins, outs, send, recv):
        first, landed, passed, handed = plan(outs, send, recv)
        for got, fw in zip(landed, passed):
            got.wait_recv()
            fw.start()
        for got in handed:
            got.wait_recv()
        for cp in first + passed:
            cp.wait_send()

    outs = [jax.ShapeDtypeStruct(s.shape, s.dtype) for s in staged]
    return Side(staged, outs, {u: u for u in range(n)}, 6 * n, start, finish)


def swap_halves(gps, name):
    n = len(gps)

    def body(*refs):
        g_refs, r_refs = refs[:n], refs[n:2 * n]
        send, recv = refs[2 * n:]
        x, y, c = _pos()
        cps = []
        for u in range(n):
            half = gps[u].shape[1] // 2
            cps.append(_remote(g_refs[u].at[:, pl.ds((1 - c) * half, half), :], r_refs[u], send, recv, u, (x, y, 1 - c)))
        for cp in cps:
            cp.start()
        for cp in cps:
            cp.wait()

    outs = [jax.ShapeDtypeStruct((g.shape[0], g.shape[1] // 2, g.shape[2]), g.dtype) for g in gps]
    return _comm_call(body, name, gps, outs, n)


def scatter_side(parts):
    n = len(parts)

    def plan(p_refs, r_refs, send, recv):
        x, y, c = _pos()
        chips = [(1 - x, y), (x, 1 - y), (1 - x, 1 - y)]
        return [_remote(p_refs[u].at[2 * chip[0] + chip[1]], r_refs[u].at[j], send, recv, 3 * u + j, (*chip, c))
                for u in range(n) for j, chip in enumerate(chips)]

    def start(ins, outs, send, recv):
        for cp in plan(ins, outs, send, recv):
            cp.start()

    def finish(ins, outs, send, recv):
        for cp in plan(ins, outs, send, recv):
            cp.wait()

    outs = [jax.ShapeDtypeStruct((N_CHIPS - 1,) + p.shape[1:], p.dtype) for p in parts]
    return Side(parts, outs, {}, 3 * n, start, finish)


def join_halves(bufs):
    n = len(bufs)

    def body(*refs):
        o_refs = refs[n:2 * n]
        send, recv = refs[2 * n:]
        x, y, c = _pos()

        def rows(u, cc):
            half = bufs[u].shape[0] // 2
            return o_refs[u].at[pl.ds(cc * half, half), :]

        cps = [_remote(rows(u, c), rows(u, c), send, recv, u, (x, y, 1 - c)) for u in range(n)]
        for cp in cps:
            cp.start()
        for u in range(n):
            _remote(rows(u, 1 - c), rows(u, 1 - c), send, recv, u, (x, y, c)).wait_recv()
        for cp in cps:
            cp.wait_send()

    outs = [jax.ShapeDtypeStruct(b.shape, b.dtype) for b in bufs]
    return _comm_call(body, "join_halves", bufs, outs, n, aliases={u: u for u in range(n)})


def gather_small(sg, name):
    rows, cols = sg.shape

    def body(s_ref, o_ref, send, recv, lsem):
        x, y, c = _pos()
        me, sibling = (x, y, c), (x, y, 1 - c)
        chips = [(1 - x, y), (x, 1 - y), (1 - x, 1 - y)]

        def blk(px, py, pc):
            return o_ref.at[4 * px + 2 * py + pc]

        mine = pltpu.make_async_copy(s_ref, blk(*me), lsem)
        mine.start()
        first = [_remote(s_ref, blk(*me), send, recv, 0, sibling)]
        first += [_remote(s_ref, blk(*me), send, recv, 1 + j, (*chip, c)) for j, chip in enumerate(chips)]
        for cp in first:
            cp.start()
        passed = [_remote(blk(*chip, c), blk(*chip, c), send, recv, 4 + j, sibling) for j, chip in enumerate(chips)]
        for j, chip in enumerate(chips):
            _remote(blk(*chip, c), blk(*chip, c), send, recv, 1 + j, me).wait_recv()
            passed[j].start()
        _remote(blk(*sibling), blk(*sibling), send, recv, 0, me).wait_recv()
        for j, chip in enumerate(chips):
            _remote(blk(*chip, 1 - c), blk(*chip, 1 - c), send, recv, 4 + j, me).wait_recv()
        for cp in first + passed:
            cp.wait_send()
        mine.wait()

    return pl.pallas_call(
        body, name=name,
        out_shape=jax.ShapeDtypeStruct((N_DEV, rows, cols), sg.dtype),
        in_specs=[ANY], out_specs=ANY,
        scratch_shapes=[pltpu.SemaphoreType.DMA((N_DEV - 1,)), pltpu.SemaphoreType.DMA((N_DEV - 1,)), pltpu.SemaphoreType.DMA],
    )(sg)


def sum_cores(gp, theirs, core, chip, name):
    nch, rows, cols = gp.shape
    half = rows // 2
    tr = _pick(half, (256, 176, 128, 64))
    nb = half // tr

    def kern(idx_ref, g_ref, t_ref, own_ref, all_ref):
        k = pl.program_id(1)
        s = g_ref[...].astype(F32) + t_ref[...].astype(F32)
        all_ref[...] = s.astype(BF16)

        @pl.when(k == idx_ref[1])
        def _():
            own_ref[...] = s

    grid_spec = pltpu.PrefetchScalarGridSpec(
        num_scalar_prefetch=1, grid=(nb, nch),
        in_specs=[pl.BlockSpec((None, tr, cols), lambda i, k, idx: (k, idx[0] * nb + i, 0)),
                  pl.BlockSpec((None, tr, cols), lambda i, k, idx: (k, i, 0))],
        out_specs=[pl.BlockSpec((tr, cols), lambda i, k, idx: (i, 0)),
                   pl.BlockSpec((None, tr, cols), lambda i, k, idx: (k, i, 0))])
    return pl.pallas_call(
        kern, name=name, grid_spec=grid_spec,
        out_shape=[jax.ShapeDtypeStruct((half, cols), F32), jax.ShapeDtypeStruct((nch, half, cols), BF16)],
        compiler_params=_params(("parallel", "arbitrary")),
    )(jnp.stack([core, chip]).astype(jnp.int32), gp, theirs)


def sum_chips(own, others, core, name):
    half, cols = own.shape
    tr = _pick(half, (256, 176, 128, 64))
    nb = half // tr

    def kern(idx_ref, o_ref, a_ref, b_ref, c_ref, out_ref):
        out_ref[...] = ((o_ref[...] + a_ref[...].astype(F32)) + b_ref[...].astype(F32)) + c_ref[...].astype(F32)

    grid_spec = pltpu.PrefetchScalarGridSpec(
        num_scalar_prefetch=1, grid=(nb,),
        in_specs=[pl.BlockSpec((tr, cols), lambda i, idx: (i, 0))] +
                 [pl.BlockSpec((None, tr, cols), lambda i, idx, j=j: (j, i, 0)) for j in range(N_CHIPS - 1)],
        out_specs=pl.BlockSpec((tr, cols), lambda i, idx: (idx[0] * nb + i, 0)))
    return pl.pallas_call(
        kern, name=name, grid_spec=grid_spec,
        out_shape=jax.ShapeDtypeStruct((2 * half, cols), F32),
        compiler_params=_params(("parallel",)),
    )(jnp.reshape(core, (1,)).astype(jnp.int32), own, others, others, others)


def small_update(gath, w, m, v, name):
    def fn(*vs):
        g = vs[0]
        for t in vs[1:N_DEV]:
            g = g + t
        wv, mv, vv = vs[N_DEV:]
        m2 = ADAM_B1 * mv + (1.0 - ADAM_B1) * g
        v2 = ADAM_B2 * vv + (1.0 - ADAM_B2) * (g * g)
        m_hat = m2 / (1.0 - ADAM_B1 ** ADAM_STEP)
        v_hat = v2 / (1.0 - ADAM_B2 ** ADAM_STEP)
        return g, -ADAM_LR * (m_hat / (jnp.sqrt(v_hat) + ADAM_EPS) + ADAM_WD * wv), m2, v2

    c = w.shape[1]
    ins = [(gath[k], "row") for k in range(N_DEV)] + [(w, "row"), (m, "row"), (v, "row")]
    return rowwise(fn, ins, [(c, F32)] * 4, tr=w.shape[0] // 2, name=name)


_MIX = {0: [("sb_w_qkv", 0), ("sb_w_o", 0)], 1: [("gm_w_in", 0), ("gm_w_out", 0)],
        2: [("ssm_w_in", 0), ("ssm_w_out", 0)], 3: [("sb_w_qkv", 1), ("sb_w_o", 1)]}
_FFN = {i: [("ffn_w_gu", i), ("ffn_w_down", i)] for i in range(4)}
GATHER_FIRST = _MIX[0][:1]
GATHER_AT = {"sb_attn_0": _MIX[0][1:] + _FFN[0] + _FFN[1],
             "ffn_gu_0": _MIX[1], "ffn_down_0": _MIX[2][1:], "ffn_gu_1": _MIX[2][:1], "ffn_down_1": _FFN[2][1:],
             "ssm_scan_2": _FFN[2][:1] + _MIX[3] + _FFN[3][1:], "ffn_gu_2": _FFN[3][:1]}
SCATTER_AT = {"ssm_dscan_2": _FFN[3] + _MIX[3] + _FFN[2], "sb_dattn_0": _MIX[2] + _FFN[1] + _MIX[1] + _FFN[0]}
SCATTER_LAST = _MIX[0]


class _Plan:
    def __init__(self, ins, core, chip):
        self.core, self.chip = core, chip
        self.staged = {(n, l): stage_shard(ins[n][l], chip, f"stage_{n}_{l}")
                       for n in MATRICES for l in range(ins[n].shape[0])}
        self.full = {n: [None] * ins[n].shape[0] for n in MATRICES}
        self.ready = {}
        self.parts = {}
        self.halves = {}
        self.swaps = 0
        self._fill(GATHER_FIRST, run_side(gather_side([self.staged[u] for u in GATHER_FIRST]), "gather_first"))

    def _fill(self, units, gathered):
        for (n, l), g in zip(units, gathered):
            if n in COLUMN_BLOCKS:
                self.full[n][l] = g
            elif n == "ssm_w_in":
                self.full[n][l] = jnp.concatenate([g[k] for k in range(N_CHIPS)], axis=1)
            else:
                self.full[n][l] = g.reshape(-1, g.shape[-1])

    def _prepare(self, units):
        gps = [self.ready[u] for u in units]
        theirs = swap_halves(gps, f"swap_halves_{self.swaps}")
        self.swaps += 1
        for (n, l), g, t in zip(units, gps, theirs):
            self.parts[(n, l)] = sum_cores(g, t, self.core, self.chip, f"sum_cores_{n}_{l}")

    def _reduce(self, units, others):
        for (n, l), other in zip(units, others):
            self.halves[(n, l)] = sum_chips(self.parts[(n, l)][0], other, self.core, f"sum_chips_{n}_{l}")

    def side(self, tag):
        if tag in GATHER_AT:
            return gather_side([self.staged[u] for u in GATHER_AT[tag]])
        if tag in SCATTER_AT:
            self._prepare(SCATTER_AT[tag])
            return scatter_side([self.parts[u][1] for u in SCATTER_AT[tag]])
        return None

    def done(self, tag, results):
        if tag in GATHER_AT:
            self._fill(GATHER_AT[tag], results)
        else:
            self._reduce(SCATTER_AT[tag], results)

    def grads_ready(self, grads):
        for (n, l), g in grads.items():
            if n in COLUMN_BLOCKS:
                self.ready[(n, l)] = g
            elif n == "ssm_w_in":
                self.ready[(n, l)] = jnp.stack(jnp.split(g, N_CHIPS, axis=1))
            else:
                self.ready[(n, l)] = g.reshape(N_CHIPS, -1, g.shape[-1])

    def shard_grads(self):
        self._prepare(SCATTER_LAST)
        self._reduce(SCATTER_LAST, run_side(scatter_side([self.parts[u][1] for u in SCATTER_LAST]), "scatter_last"))
        units = sorted(self.halves)
        return dict(zip(units, join_halves([self.halves[u] for u in units])))


def _step(ins):
    x, target = ins["x"][0], ins["loss_target"][0]
    core = lax.axis_index("c")
    chip = 2 * lax.axis_index("x") + lax.axis_index("y")

    def lane_pad(v):
        return jnp.pad(v, ((0, 0), (0, PACK_COLS - v.shape[1])))

    vec_rows = [ins["ssm_conv_w"][0], ins["ssm_conv_b"], lane_pad(ins["ssm_norm_gain"])]
    blk = jnp.concatenate(vec_rows + [jnp.zeros((SUBLANES - 6, PACK_COLS), F32)], axis=0)
    per_chip = gather_small(blk, "gather_vectors")[0::2]
    ngw = ins["ssm_norm_gain"].shape[1]
    full = {
        "ssm_conv_w": jnp.concatenate([per_chip[k, 0:4] for k in range(N_CHIPS)], axis=1)[None],
        "ssm_conv_b": jnp.concatenate([per_chip[k, 4:5] for k in range(N_CHIPS)], axis=1),
        "ssm_norm_gain": jnp.concatenate([per_chip[k, 5:6, :ngw] for k in range(N_CHIPS)], axis=1),
    }

    plan = _Plan(ins, core, chip)
    full.update(plan.full)
    for n in REPLICATED:
        full[n] = ins[n]

    loss, dx, grads = local_step(x, target, full, plan)
    loss = lax.psum(loss, ALL_AXES)
    gshards = plan.shard_grads()

    small_shapes = [ins[n].shape for n in REPLICATED]
    vec_shapes = [grads[n].shape for n in EXACT]
    vec_pack = _pack([grads[n] for n in EXACT], F32, SUBLANES)
    gath = gather_small(jnp.concatenate([_pack([grads[n] for n in REPLICATED], F32, SUBLANES), vec_pack], axis=0),
                        "gather_small")
    packed = [jnp.concatenate([_pack([ins[pre + n] for n in REPLICATED], F32, SUBLANES), jnp.zeros_like(vec_pack)], axis=0)
              for pre in ("", "m_", "v_")]
    res = small_update(gath, *packed, name="small_update")
    nrep = res[0].shape[0] - vec_pack.shape[0]
    small = [dict(zip(REPLICATED, _unpack(r[:nrep].reshape(-1), small_shapes))) for r in res]
    vec_g = dict(zip(EXACT, _unpack(res[0][nrep:].reshape(-1), vec_shapes)))

    out_g, out_d, out_m, out_v = {}, {}, {}, {}
    for n in REPLICATED:
        out_g[n], out_d[n], out_m[n], out_v[n] = (s[n] for s in small)
    for n in SHARDED:
        shp = ins[n].shape
        if n in EXACT:
            g = lax.dynamic_slice_in_dim(vec_g[n], chip * shp[-1], shp[-1], axis=vec_g[n].ndim - 1)
        else:
            g = jnp.stack([gshards[(n, l)] for l in range(shp[0])])
        two = (math.prod(shp[:-1]), shp[-1])
        d2, m2, v2 = adamw(ins[n].reshape(two), g.reshape(two), ins["m_" + n].reshape(two),
                           ins["v_" + n].reshape(two), f"adamw_{n}")
        out_g[n], out_d[n], out_m[n], out_v[n] = g, d2.reshape(shp), m2.reshape(shp), v2.reshape(shp)
    return (loss, dx[None], *[out_g[n] for n in WEIGHTS], *[out_d[n] for n in WEIGHTS],
            *[out_m[n] for n in WEIGHTS], *[out_v[n] for n in WEIGHTS])


def kernel(x, mix_norm, ffn_norm, sb_w_qkv, sb_q_gain, sb_k_gain, sb_w_o, gm_w_in, gm_b_in, gm_v_gain, gm_w_s, gm_b_s, gm_w_out, ssm_w_in, ssm_conv_w, ssm_conv_b, ssm_dt_bias, ssm_a_log, ssm_d, ssm_norm_gain, ssm_w_out, ffn_w_gu, ffn_w_down, loss_target, m_mix_norm, m_ffn_norm, m_sb_w_qkv, m_sb_q_gain, m_sb_k_gain, m_sb_w_o, m_gm_w_in, m_gm_b_in, m_gm_v_gain, m_gm_w_s, m_gm_b_s, m_gm_w_out, m_ssm_w_in, m_ssm_conv_w, m_ssm_conv_b, m_ssm_dt_bias, m_ssm_a_log, m_ssm_d, m_ssm_norm_gain, m_ssm_w_out, m_ffn_w_gu, m_ffn_w_down, v_mix_norm, v_ffn_norm, v_sb_w_qkv, v_sb_q_gain, v_sb_k_gain, v_sb_w_o, v_gm_w_in, v_gm_b_in, v_gm_v_gain, v_gm_w_s, v_gm_b_s, v_gm_w_out, v_ssm_w_in, v_ssm_conv_w, v_ssm_conv_b, v_ssm_dt_bias, v_ssm_a_log, v_ssm_d, v_ssm_norm_gain, v_ssm_w_out, v_ffn_w_gu, v_ffn_w_down):
    return _step(dict(locals()))
```

```python
import functools
import math

import jax
import jax.numpy as jnp
from jax import lax
from jax.experimental import pallas as pl
from jax.experimental.pallas import tpu as pltpu

F32 = jnp.float32
BF16 = jnp.bfloat16
EPS = 1e-6
LANES = 128
SUBLANES = 8
VMEM_LIMIT = 56 * 1024 * 1024
HEAD = 64
CHUNK = 128
SB_TQ, SB_TK = 256, 256
SSD_SUB = 8
SB_DEAD = -110.0
SB_UNSEEN = -1e30
ADAM_LR, ADAM_B1, ADAM_B2, ADAM_EPS, ADAM_WD, ADAM_STEP = 0.001, 0.9, 0.999, 1e-08, 0.01, 10
MESH_ID = pl.DeviceIdType.MESH
ALL_AXES = ("x", "y", "c")


def _params(sem):
    return pltpu.CompilerParams(dimension_semantics=sem, vmem_limit_bytes=VMEM_LIMIT)


def _pick(n, cands):
    for c in cands:
        if n % c == 0:
            return c
    return n


def _dot(a, b, dims=((1,), (0,))):
    return lax.dot_general(a, b, (dims, ((), ())), preferred_element_type=F32)


def _dot_nt(a, b):
    return _dot(a, b, ((1,), (1,)))


def _dot_tn(a, b):
    return _dot(a, b, ((0,), (0,)))


def _split2(x):
    hi = x.astype(BF16)
    lo = (x - hi.astype(F32)).astype(BF16)
    return hi, lo


def _dot_x2(x, m):
    hi, lo = _split2(x)
    return _dot(hi, m) + _dot(lo, m)


def _dot_x3_left(m, x):
    h1 = x.astype(BF16)
    r1 = x - h1.astype(F32)
    h2 = r1.astype(BF16)
    h3 = (r1 - h2.astype(F32)).astype(BF16)
    return _dot(m, h1) + _dot(m, h2) + _dot(m, h3)


def _sigmoid(x):
    return 1.0 / (1.0 + jnp.exp(-x))


def _softplus(x):
    return jnp.maximum(x, 0.0) + jnp.log(1.0 + jnp.exp(-jnp.abs(x)))


def _colsum(x):
    return jnp.sum(x, axis=0, keepdims=True)


def _rowsum(x):
    return jnp.sum(x, axis=1, keepdims=True)


def _iota2(shape, dim):
    return lax.broadcasted_iota(jnp.int32, shape, dim)


MM_VMEM_BUDGET = 40 * 1024 * 1024
MM_STEP_US = 0.35
MM_HBM_BYTES_PER_US = 3.0e6
MM_VMEM_BYTES_PER_US = 1.5e6
MM_FLOPS_PER_US = 9.0e8
MXU_DIM = 256


def _mm_tiles(m, n, kk, wn, wk, a_bytes, b_bytes, has_add):
    def divisors(total, cands):
        got = [c for c in cands if total % c == 0 and c <= total]
        return got or [total]

    best = None
    for tm in divisors(m, (1024, 512, 256, 128)):
        for tn in divisors(wn, (1024, 768, 1408, 512, 256, 128)):
            for tk in divisors(wk, (4096, 2816, 2048, 1408, 1024, 768, 512, 256, 128)):
                nk = kk // tk
                vmem = 2 * (tm * tk * a_bytes + tk * tn * b_bytes + tm * tn * 4 * (2 if has_add else 1))
                vmem += tm * tn * 4 if nk > 1 else 0
                if vmem > MM_VMEM_BUDGET:
                    continue
                steps = (m // tm) * (n // tn) * nk
                a_reads = 1 if nk == 1 else n // tn
                traffic = m * kk * a_bytes * a_reads + kk * n * b_bytes * (m // tm) + m * n * 4
                fill = min(1.0, tn / MXU_DIM) * min(1.0, tm / MXU_DIM)
                compute = 2.0 * m * n * kk / (MM_FLOPS_PER_US * fill)
                cost = steps * MM_STEP_US + max(compute, traffic / MM_HBM_BYTES_PER_US)
                if nk > 1:
                    cost += steps * tm * tn * 8 / MM_VMEM_BYTES_PER_US
                if best is None or cost < best[0]:
                    best = (cost, tm, tn, tk)
    return best[1:]


def mm(a, b, *, ta=False, tb=False, add=None, bias=None, a_chunks=False, b_chunks=False, out_chunks=False,
       out_dtype=F32, name, side=None):
    wa = None
    if a_chunks:
        m, wa = a.shape[1], a.shape[2]
        kk = a.shape[0] * wa
    elif ta:
        kk, m = a.shape
    else:
        m, kk = a.shape
    nch, wide = 1, None
    if b_chunks:
        nch, rows_b, wide = b.shape
        kb, n = (rows_b, nch * wide) if not tb else (nch * wide, rows_b)
    elif tb:
        n, kb = b.shape
    else:
        kb, n = b.shape
    wide_o = n // N_CHIPS if out_chunks else None
    assert kk == kb, (a.shape, b.shape, ta, tb)
    has_add, has_bias = add is not None, bias is not None
    wk = wide if (wide and tb) else kk
    wn = wide if (wide and not tb) else n
    tm, tn, tk = _mm_tiles(m, n, kk, math.gcd(wn, wide_o) if wide_o else wn, math.gcd(wk, wa) if wa else wk,
                           a.dtype.itemsize, b.dtype.itemsize, has_add)
    nk = kk // tk
    dims = ((0 if ta else 1,), (1 if tb else 0,))

    def kern(*refs):
        a_ref, b_ref = refs[0], refs[1]
        rest = list(refs[2:])
        add_ref = rest.pop(0) if has_add else None
        bias_ref = rest.pop(0) if has_bias else None
        o_ref = rest[0]
        part = _dot(a_ref[...].astype(BF16), b_ref[...].astype(BF16), dims)

        def finish(r):
            if has_add:
                r = r + add_ref[...]
            if has_bias:
                r = r + bias_ref[...]
            o_ref[...] = r.astype(out_dtype)

        if nk == 1:
            finish(part)
        else:
            acc_ref = rest[1]
            k = pl.program_id(2)

            @pl.when(k == 0)
            def _():
                acc_ref[...] = part

            @pl.when((k > 0) & (k < nk - 1))
            def _():
                acc_ref[...] += part

            @pl.when(k == nk - 1)
            def _():
                finish(acc_ref[...] + part)

    if a_chunks:
        per_a = wa // tk
        a_spec = pl.BlockSpec((None, tm, tk), lambda i, j, k: (k // per_a, i, k % per_a))
    elif ta:
        a_spec = pl.BlockSpec((tk, tm), lambda i, j, k: (k, i))
    else:
        a_spec = pl.BlockSpec((tm, tk), lambda i, j, k: (i, k))
    if b_chunks and tb:
        per = wide // tk
        b_spec = pl.BlockSpec((None, tn, tk), lambda i, j, k: (k // per, j, k % per))
    elif b_chunks:
        per = wide // tn
        b_spec = pl.BlockSpec((None, tk, tn), lambda i, j, k: (j // per, k, j % per))
    elif tb:
        b_spec = pl.BlockSpec((tn, tk), lambda i, j, k: (j, k))
    else:
        b_spec = pl.BlockSpec((tk, tn), lambda i, j, k: (k, j))
    if out_chunks:
        per_o = wide_o // tn
        out_spec = pl.BlockSpec((None, tm, tn), lambda i, j, k: (j // per_o, i, j % per_o))
        out_shape = jax.ShapeDtypeStruct((N_CHIPS, m, wide_o), out_dtype)
    else:
        out_spec = pl.BlockSpec((tm, tn), lambda i, j, k: (i, j))
        out_shape = jax.ShapeDtypeStruct((m, n), out_dtype)
    in_specs, args = [a_spec, b_spec], [a, b]
    if has_add:
        in_specs.append(pl.BlockSpec((tm, tn), lambda i, j, k: (i, j)))
        args.append(add)
    if has_bias:
        in_specs.append(pl.BlockSpec((1, tn), lambda i, j, k: (0, j)))
        args.append(bias)
    (out,), side_outs = side_call(
        kern, side,
        name=name,
        grid=(m // tm, n // tn, nk),
        in_specs=in_specs,
        out_specs=[out_spec],
        out_shape=[out_shape],
        scratch_shapes=[pltpu.VMEM((tm, tn), F32)] if nk > 1 else [],
        args=args)
    return out if side is None else (out, side_outs)


def mm_hooked(plan, a, b, *, name, **kw):
    side = plan.side(name)
    if side is None:
        return mm(a, b, name=name, **kw)
    out, side_outs = mm(a, b, name=name, side=side, **kw)
    plan.done(name, side_outs)
    return out


def rowwise(fn, ins, outs, accs=(), *, tr, name, side=None):
    rows = [a for a, kind in ins if kind == "row"][0].shape[0]
    tr = min(tr, rows)
    assert rows % tr == 0 and tr % SUBLANES == 0, (rows, tr)
    n = rows // tr
    n_in, n_out = len(ins), len(outs)
    kinds = [kind for _, kind in ins]

    def kern(*refs):
        i = pl.program_id(0)
        vals = []
        for ref, kind in zip(refs[:n_in], kinds):
            v = ref[...]
            if kind == "prev":
                v = v * (i > 0).astype(v.dtype)
            elif kind == "next":
                v = v * (i < n - 1).astype(v.dtype)
            vals.append(v)
        res = fn(*vals)
        for ref, r in zip(refs[n_in:n_in + n_out], res[:n_out]):
            ref[...] = r.astype(ref.dtype)
        if accs:
            acc_refs = refs[n_in + n_out:]

            @pl.when(i == 0)
            def _():
                for ref in acc_refs:
                    ref[...] = jnp.zeros_like(ref)

            for ref, r in zip(acc_refs, res[n_out:]):
                ref[...] += r

    in_specs = []
    for a, kind in ins:
        if kind == "row":
            in_specs.append(pl.BlockSpec((tr, a.shape[1]), lambda i: (i, 0)))
        elif kind == "full":
            in_specs.append(pl.BlockSpec(a.shape, lambda i, nd=a.ndim: (0,) * nd))
        elif kind == "prev":
            in_specs.append(pl.BlockSpec((SUBLANES, a.shape[1]),
                                         lambda i: (jnp.maximum(i * (tr // SUBLANES) - 1, 0), 0)))
        else:
            in_specs.append(pl.BlockSpec((SUBLANES, a.shape[1]),
                                         lambda i: (jnp.minimum((i + 1) * (tr // SUBLANES), rows // SUBLANES - 1), 0)))
    out_specs = [pl.BlockSpec((tr, c), lambda i: (i, 0)) for c, _ in outs]
    out_specs += [pl.BlockSpec((r, c), lambda i: (0, 0)) for r, c in accs]
    out_shape = [jax.ShapeDtypeStruct((rows, c), dt) for c, dt in outs]
    out_shape += [jax.ShapeDtypeStruct((r, c), F32) for r, c in accs]
    if side is not None:
        return side_call(kern, side, name=name, grid=(n,), in_specs=in_specs, out_specs=out_specs, out_shape=out_shape,
                         scratch_shapes=[], args=[a for a, _ in ins])
    res = pl.pallas_call(
        kern,
        name=name,
        grid=(n,),
        in_specs=in_specs,
        out_specs=out_specs,
        out_shape=out_shape,
        compiler_params=_params(("arbitrary",) if accs else ("parallel",)),
    )(*[a for a, _ in ins])
    return res


def rms_fwd(x, g, name):
    def fn(xv, gv):
        r = lax.rsqrt(jnp.mean(xv * xv, axis=1, keepdims=True) + EPS)
        return (xv * r * gv,)

    return rowwise(fn, [(x, "row"), (g, "full")], [(x.shape[1], BF16)], tr=512, name=name)[0]


def rms_bwd(x, g, dy, dres, name):
    def fn(xv, gv, dyv, drv):
        r = lax.rsqrt(jnp.mean(xv * xv, axis=1, keepdims=True) + EPS)
        xh = xv * r
        dyg = dyv * gv
        dx = drv + r * (dyg - xh * jnp.mean(dyg * xh, axis=1, keepdims=True))
        return dx, dx, _colsum(dyv * xh)

    c = x.shape[1]
    dx, dxb, dg = rowwise(fn, [(x, "row"), (g, "full"), (dy, "row"), (dres, "row")], [(c, F32), (c, BF16)], [(1, c)],
                          tr=256, name=name)
    return (dx, dxb), dg


def ffn_up(h, wgu, name, side=None):
    s, d = h.shape
    nch, _, w = wgu.shape
    half = nch // 2
    tm = _pick(s, (512, 256, 128))

    def kern(h_ref, wg_ref, wu_ref, gu_ref, a_ref):
        hv = h_ref[...]
        g = _dot(hv, wg_ref[...])
        u = _dot(hv, wu_ref[...])
        gu_ref[0] = g.astype(BF16)
        gu_ref[1] = u.astype(BF16)
        a_ref[...] = (g * _sigmoid(g) * u).astype(BF16)

    return side_call(
        kern, side, name=name, grid=(s // tm, half),
        in_specs=[pl.BlockSpec((tm, d), lambda i, j: (i, 0)),
                  pl.BlockSpec((None, d, w), lambda i, j: (j, 0, 0)),
                  pl.BlockSpec((None, d, w), lambda i, j: (j + half, 0, 0))],
        out_specs=[pl.BlockSpec((2, tm, w), lambda i, j: (0, i, j)), pl.BlockSpec((tm, w), lambda i, j: (i, j))],
        out_shape=[jax.ShapeDtypeStruct((2, s, half * w), BF16), jax.ShapeDtypeStruct((s, half * w), BF16)],
        scratch_shapes=[], args=(h, wgu, wgu))


def ffn_dact(dxb, wdown, gu, name):
    s, d = dxb.shape
    hid = wdown.shape[0]
    tm = _pick(s, (512, 256, 128))
    tn = _pick(hid, (1408, 512, 256, 128))

    def kern(dx_ref, w_ref, gu_ref, o_ref):
        da = _dot_nt(dx_ref[...], w_ref[...])
        g, u = gu_ref[0].astype(F32), gu_ref[1].astype(F32)
        sg = _sigmoid(g)
        o_ref[0] = (da * u * sg * (1.0 + g * (1.0 - sg))).astype(BF16)
        o_ref[1] = (da * g * sg).astype(BF16)

    return pl.pallas_call(
        kern, name=name, grid=(s // tm, hid // tn),
        in_specs=[pl.BlockSpec((tm, d), lambda i, j: (i, 0)), pl.BlockSpec((tn, d), lambda i, j: (j, 0)),
                  pl.BlockSpec((2, tm, tn), lambda i, j: (0, i, j))],
        out_specs=pl.BlockSpec((2, tm, tn), lambda i, j: (0, i, j)),
        out_shape=jax.ShapeDtypeStruct((2, s, hid), BF16),
        compiler_params=_params(("parallel", "parallel")),
    )(dxb, wdown, gu)


def loss_and_grad(y, t, name):
    d = y.shape[1]

    def fn(yv, tv):
        e = yv - tv
        part = jnp.sum(_colsum(e * e), axis=1, keepdims=True) * (0.5 / d)
        dy = e * (1.0 / d)
        return dy, dy, jnp.broadcast_to(part, (SUBLANES, LANES))

    dy, dyb, acc = rowwise(fn, [(y, "row"), (t, "row")], [(d, F32), (d, BF16)], [(SUBLANES, LANES)], tr=512, name=name)
    return acc[0, 0], (dy, dyb)


def adamw(w, g, m, v, name, side=None):
    def fn(wv, gv, mv, vv):
        m2 = ADAM_B1 * mv + (1.0 - ADAM_B1) * gv
        v2 = ADAM_B2 * vv + (1.0 - ADAM_B2) * (gv * gv)
        m_hat = m2 / (1.0 - ADAM_B1 ** ADAM_STEP)
        v_hat = v2 / (1.0 - ADAM_B2 ** ADAM_STEP)
        delta = -ADAM_LR * (m_hat / (jnp.sqrt(v_hat) + ADAM_EPS) + ADAM_WD * wv)
        return delta, m2, v2

    rows, c = w.shape
    tr = _pick(rows, (256, 128, 64, 32, 16, 8)) if rows % SUBLANES == 0 else rows
    if rows % SUBLANES:
        return _whole(fn, [w, g, m, v], [(w.shape, F32)] * 3, name=name)
    return rowwise(fn, [(w, "row"), (g, "row"), (m, "row"), (v, "row")], [(c, F32)] * 3, tr=tr, name=name, side=side)


def _whole(fn, ins, outs, *, name):
    n_in = len(ins)

    def kern(*refs):
        res = fn(*[r[...] for r in refs[:n_in]])
        for ref, r in zip(refs[n_in:], res):
            ref[...] = r.astype(ref.dtype)

    return pl.pallas_call(
        kern,
        name=name,
        out_shape=[jax.ShapeDtypeStruct(s, dt) for s, dt in outs],
        compiler_params=pltpu.CompilerParams(vmem_limit_bytes=VMEM_LIMIT),
    )(*ins)


def ffn_fwd(x, g, wgu, wdown, tag, plan):
    h = rms_fwd(x, g, f"ffn_rms_{tag}")
    gu, a = _hooked(plan, f"ffn_gu_{tag}", ffn_up, h, wgu)
    xn = mm_hooked(plan, a, wdown, add=x, name=f"ffn_down_{tag}")
    return xn, (x, h, gu, a)


def ffn_bwd(dxn, saved, g, wgu, wdown, tag):
    x, h, gu, a = saved
    dxn, dxb = dxn
    dwdown = mm(a, dxb, ta=True, out_dtype=BF16, name=f"ffn_dwdown_{tag}")
    dgu = ffn_dact(dxb, wdown, gu, f"ffn_dact_{tag}")
    dh = mm(dgu, wgu, tb=True, a_chunks=True, b_chunks=True, name=f"ffn_dh_{tag}")
    dwgu = mm(h, dgu, ta=True, b_chunks=True, out_dtype=BF16, out_chunks=True, name=f"ffn_dwgu_{tag}")
    dx, dg = rms_bwd(x, g, dh, dxn, f"ffn_drms_{tag}")
    return dx, dg, dwgu, dwdown


def _head_blockdiag(c):
    i = jnp.arange(c) // HEAD
    return (i[:, None] == i[None, :]).astype(BF16)


def _head_sums(x, bd):
    return jnp.concatenate([_dot_x2(x[:, g * LANES:(g + 1) * LANES], bd) for g in range(x.shape[1] // LANES)], axis=1)


def qknorm_fwd(qkv, qg, kg, bd, name):
    d = qkv.shape[1] // 3
    scale = 1.0 / math.sqrt(HEAD)

    def fn(v, qgv, kgv, bdv):
        v = v.astype(F32)
        q, k, vv = v[:, :d], v[:, d:2 * d], v[:, 2 * d:]
        rq = lax.rsqrt(_head_sums(q * q, bdv) * (1.0 / HEAD) + EPS)
        rk = lax.rsqrt(_head_sums(k * k, bdv) * (1.0 / HEAD) + EPS)
        return q * rq * qgv * scale, k * rk * kgv, vv

    return rowwise(fn, [(qkv, "row"), (qg, "full"), (kg, "full"), (bd, "full")],
                   [(d, BF16), (d, BF16), (d, BF16)], tr=256, name=name)


def qknorm_bwd(qkv, dqs, dkn, dv, qg, kg, bd, name):
    d = qkv.shape[1] // 3
    scale = 1.0 / math.sqrt(HEAD)

    def one(xv, gv, dyv, bdv):
        r = lax.rsqrt(_head_sums(xv * xv, bdv) * (1.0 / HEAD) + EPS)
        xh = xv * r
        dyg = dyv * gv
        dx = r * (dyg - xh * (_head_sums(dyg * xh, bdv) * (1.0 / HEAD)))
        return dx, _colsum(dyv * xh)

    def fn(v, dqv, dkv, dvv, qgv, kgv, bdv):
        v = v.astype(F32)
        q, k = v[:, :d], v[:, d:2 * d]
        dq, dqg = one(q, qgv, dqv * scale, bdv)
        dk, dkg = one(k, kgv, dkv, bdv)
        return jnp.concatenate([dq, dk, dvv], axis=1), dqg, dkg

    return rowwise(fn, [(qkv, "row"), (dqs, "row"), (dkn, "row"), (dv, "row"), (qg, "full"), (kg, "full"), (bd, "full")],
                   [(3 * d, BF16)], [(1, d), (1, d)], tr=256, name=name)


def _sb_tile(qh, k, mask, tri_gt):
    z = _dot_nt(qh, k)
    sp = jnp.log(1.0 + jnp.exp(-jnp.abs(z)))
    lb = jnp.minimum(z, 0.0) - sp
    l1 = jnp.where(mask, lb - z, 0.0)
    suf = _dot(l1.astype(BF16), tri_gt)
    return lb, l1, suf


def _sb_tri(tk):
    i = jnp.arange(tk)
    return jnp.stack([i[:, None] > i[None, :], i[:, None] < i[None, :]]).astype(BF16)


def _sb_setup(tq, tk):
    row, col = _iota2((tq, tk), 0), _iota2((tq, tk), 1)
    lane = _iota2((1, LANES), 1)
    halves = [(lane < HEAD).astype(BF16), (lane >= HEAD).astype(BF16)]
    lane_q = _iota2((tq, LANES), 1) + jnp.minimum(_iota2((tq, LANES), 0), 0)
    return row, col, halves, lane_q


def sb_attn_fwd(qs, kn, vb, tri, name, side=None):
    s, d = qs.shape
    tq, tk = min(SB_TQ, s), min(SB_TK, s)
    nq = s // tq
    assert s // tk <= LANES and s % tq == 0 and s % tk == 0

    def kern(q_ref, k_ref, v_ref, tri_ref, o_ref, rs_ref, acc_ref):
        i = pl.program_id(1)
        row, col, halves, lane_q = _sb_setup(tq, tk)
        q = q_ref[...]
        qh = [q * hm for hm in halves]
        acc_ref[...] = jnp.zeros_like(acc_ref)
        rs_ref[...] = jnp.full(rs_ref.shape, SB_UNSEEN, F32)
        nkb = (i + 1) * (tq // tk)

        def more(st):
            return (st[0] < nkb) & (st[1] > SB_DEAD)

        def step(st):
            n, r = st[0], list(st[2:])
            kb = nkb - 1 - n
            ks = pl.multiple_of(kb * tk, tk)
            k = k_ref[pl.ds(ks, tk), :]
            v = v_ref[pl.ds(ks, tk), :]
            mask = col < row + (i * tq - kb * tk)
            at_kb = lane_q == kb
            for hh in range(2):
                lb, l1, suf = _sb_tile(qh[hh], k, mask, tri_ref[0])
                w = jnp.where(mask, jnp.exp(lb + suf + r[hh]), 0.0)
                acc_ref[...] += _dot(w.astype(BF16), v * halves[hh])
                rs_ref[hh] = jnp.where(at_kb, r[hh], rs_ref[hh])
                r[hh] = r[hh] + _rowsum(l1)
            return (n + 1, jnp.maximum(jnp.max(r[0]), jnp.max(r[1])), r[0], r[1])

        z1 = jnp.zeros((tq, 1), F32)
        lax.while_loop(more, step, (jnp.int32(0), jnp.float32(0.0), z1, z1))
        o_ref[...] = acc_ref[...].astype(BF16)

    nh2 = d // LANES
    return side_call(
        kern, side,
        name=name,
        grid=(nh2, nq),
        in_specs=[pl.BlockSpec((tq, LANES), lambda h, i: (i, h)),
                  pl.BlockSpec((s, LANES), lambda h, i: (0, h)),
                  pl.BlockSpec((s, LANES), lambda h, i: (0, h)),
                  pl.BlockSpec((2, tk, tk), lambda h, i: (0, 0, 0))],
        out_specs=[pl.BlockSpec((tq, LANES), lambda h, i: (i, h)),
                   pl.BlockSpec((None, 2, tq, LANES), lambda h, i: (h, 0, i, 0))],
        out_shape=[jax.ShapeDtypeStruct((s, d), BF16), jax.ShapeDtypeStruct((nh2, 2, s, LANES), F32)],
        scratch_shapes=[pltpu.VMEM((tq, LANES), F32)],
        args=(qs, kn, vb, tri))


def sb_attn_bwd(qs, kn, vb, rsave, do, tri, name, side=None):
    s, d = qs.shape
    tq, tk = min(SB_TQ, s), min(SB_TK, s)
    nq = s // tq

    def kern(q_ref, k_ref, v_ref, rs_ref, do_ref, tri_ref, dq_ref, dk_ref, dv_ref):
        i = pl.program_id(1)

        @pl.when(i == 0)
        def _():
            dk_ref[...] = jnp.zeros_like(dk_ref)
            dv_ref[...] = jnp.zeros_like(dv_ref)

        row, col, halves, lane_q = _sb_setup(tq, tk)
        q = q_ref[...]
        qh = [q * hm for hm in halves]
        dov = do_ref[...].astype(BF16)
        doh = [dov * hm for hm in halves]
        dq_ref[...] = jnp.zeros_like(dq_ref)
        nkb = (i + 1) * (tq // tk)
        top = jnp.maximum(jnp.max(rs_ref[0], axis=0, keepdims=True), jnp.max(rs_ref[1], axis=0, keepdims=True))
        dead = (top <= SB_DEAD) & (_iota2((1, LANES), 1) < nkb)
        kstart = jnp.minimum(jnp.sum(dead.astype(F32)).astype(jnp.int32), nkb)

        def step(kb, ep):
            ep = list(ep)
            ks = pl.multiple_of(kb * tk, tk)
            k = k_ref[pl.ds(ks, tk), :]
            v = v_ref[pl.ds(ks, tk), :]
            mask = col < row + (i * tq - kb * tk)
            at_kb = lane_q == kb
            for hh in range(2):
                lb, l1, suf = _sb_tile(qh[hh], k, mask, tri_ref[0])
                r = _rowsum(jnp.where(at_kb, rs_ref[hh], 0.0))
                lbm = jnp.where(mask, lb, SB_UNSEEN)
                w = jnp.exp(lbm + suf + r)
                e = _dot_nt(doh[hh], v) * w
                pe = ep[hh] + _dot(e.astype(BF16), tri_ref[1])
                beta = jnp.exp(lbm)
                dz = (e - beta * (e + pe)).astype(BF16)
                dq_ref[...] += _dot(dz, k * halves[hh])
                dk_ref[pl.ds(ks, tk), :] += _dot_tn(dz, qh[hh])
                dv_ref[pl.ds(ks, tk), :] += _dot_tn(w.astype(BF16), doh[hh])
                ep[hh] = ep[hh] + _rowsum(e)
            return tuple(ep)

        z1 = jnp.zeros((tq, 1), F32)
        lax.fori_loop(kstart, nkb, step, (z1, z1))

    nh2 = d // LANES
    return side_call(
        kern, side,
        name=name,
        grid=(nh2, nq),
        in_specs=[pl.BlockSpec((tq, LANES), lambda h, i: (i, h)),
                  pl.BlockSpec((s, LANES), lambda h, i: (0, h)),
                  pl.BlockSpec((s, LANES), lambda h, i: (0, h)),
                  pl.BlockSpec((None, 2, tq, LANES), lambda h, i: (h, 0, i, 0)),
                  pl.BlockSpec((tq, LANES), lambda h, i: (i, h)),
                  pl.BlockSpec((2, tk, tk), lambda h, i: (0, 0, 0))],
        out_specs=[pl.BlockSpec((tq, LANES), lambda h, i: (i, h)),
                   pl.BlockSpec((s, LANES), lambda h, i: (0, h)),
                   pl.BlockSpec((s, LANES), lambda h, i: (0, h))],
        out_shape=[jax.ShapeDtypeStruct((s, d), F32)] * 3,
        scratch_shapes=[],
        args=(qs, kn, vb, rsave, do, tri))


def _hooked(plan, tag, call, *args):
    side = plan.side(tag)
    outs, side_outs = call(*args, tag, side)
    if side is not None:
        plan.done(tag, side_outs)
    return outs


def sb_fwd(x, g, wqkv, qg, kg, wo, bd, tag, plan):
    h = rms_fwd(x, g, f"sb_rms_{tag}")
    qkv = mm(h, wqkv, b_chunks=True, out_dtype=BF16, name=f"sb_qkv_{tag}")
    qs, kn, vb = qknorm_fwd(qkv, qg, kg, bd, f"sb_qknorm_{tag}")
    o, rsave = _hooked(plan, f"sb_attn_{tag}", sb_attn_fwd, qs, kn, vb, _sb_tri(min(SB_TK, x.shape[0])))
    xn = mm(o, wo(), add=x, name=f"sb_out_{tag}")
    return xn, (x, h, qkv, qs, kn, vb, rsave, o)


def sb_bwd(dxn, saved, g, wqkv, qg, kg, wo, bd, tag, plan):
    x, h, qkv, qs, kn, vb, rsave, o = saved
    dxn, dxb = dxn
    do = mm(dxb, wo, tb=True, name=f"sb_do_{tag}")
    dwo = mm(o, dxb, ta=True, out_dtype=BF16, name=f"sb_dwo_{tag}")
    dqs, dkn, dv = _hooked(plan, f"sb_dattn_{tag}", sb_attn_bwd, qs, kn, vb, rsave, do, _sb_tri(min(SB_TK, x.shape[0])))
    dqkv, dqg, dkg = qknorm_bwd(qkv, dqs, dkn, dv, qg, kg, bd, f"sb_dqknorm_{tag}")
    dh = mm(dqkv, wqkv, tb=True, b_chunks=True, name=f"sb_dh_{tag}")
    dwqkv = mm(h, dqkv, ta=True, out_dtype=BF16, out_chunks=True, name=f"sb_dwqkv_{tag}")
    dx, dg = rms_bwd(x, g, dh, dxn, f"sb_drms_{tag}")
    nh = dqg.shape[1] // HEAD
    return dx, dg, dwqkv, dqg.reshape(nh, HEAD).sum(0), dkg.reshape(nh, HEAD).sum(0), dwo


def _gelu(x):
    return 0.5 * x * (1.0 + lax.erf(x * (1.0 / math.sqrt(2.0))))


def _gelu_grad(x):
    return 0.5 * (1.0 + lax.erf(x * (1.0 / math.sqrt(2.0)))) + x * jnp.exp(-0.5 * x * x) * (1.0 / math.sqrt(2.0 * math.pi))


def gm_act_fwd(pre, vg, name):
    half = pre.shape[1] // 2

    def fn(p, vgv):
        p = p.astype(F32)
        u = _gelu(p[:, :half])
        v = _gelu(p[:, half:])
        r = lax.rsqrt(jnp.mean(v * v, axis=1, keepdims=True) + EPS)
        return u, v * r * vgv

    return rowwise(fn, [(pre, "row"), (vg, "full")], [(half, F32), (half, BF16)], tr=256, name=name)


def gm_act_bwd(pre, du, dvn, vg, name):
    half = pre.shape[1] // 2

    def fn(p, duv, dvnv, vgv):
        p = p.astype(F32)
        pu, pv = p[:, :half], p[:, half:]
        v = _gelu(pv)
        r = lax.rsqrt(jnp.mean(v * v, axis=1, keepdims=True) + EPS)
        vh = v * r
        dyg = dvnv * vgv
        dv = r * (dyg - vh * jnp.mean(dyg * vh, axis=1, keepdims=True))
        dpre = jnp.concatenate([duv * _gelu_grad(pu), dv * _gelu_grad(pv)], axis=1)
        return dpre, _colsum(dvnv * vh), _colsum(dpre)

    return rowwise(fn, [(pre, "row"), (du, "row"), (dvn, "row"), (vg, "full")],
                   [(2 * half, BF16)], [(1, half), (1, 2 * half)], tr=256, name=name)


def gm_spatial_fwd(u, vn, wc, bst, name):
    s, c = u.shape
    t = CHUNK
    ng = c // LANES

    def kern(u_ref, v_ref, w_ref, b_ref, o_ref):
        for g in range(ng):
            sl = slice(g * LANES, (g + 1) * LANES)
            mixed = _dot(w_ref[g], v_ref[:, sl]) + b_ref[:, sl]
            o_ref[:, sl] = (u_ref[:, sl] * mixed).astype(BF16)

    return pl.pallas_call(
        kern,
        name=name,
        grid=(s // t,),
        in_specs=[pl.BlockSpec((t, c), lambda i: (i, 0)), pl.BlockSpec((t, c), lambda i: (i, 0)),
                  pl.BlockSpec(wc.shape, lambda i: (0, 0, 0)), pl.BlockSpec(bst.shape, lambda i: (0, 0))],
        out_specs=pl.BlockSpec((t, c), lambda i: (i, 0)),
        out_shape=jax.ShapeDtypeStruct((s, c), BF16),
        compiler_params=_params(("parallel",)),
    )(u, vn, wc, bst)


def gm_spatial_bwd(dgate, u, vn, wc, bst, name):
    s, c = u.shape
    t = CHUNK
    ng = c // LANES

    def kern(dg_ref, u_ref, v_ref, w_ref, b_ref, du_ref, dv_ref, dw_ref, db_ref):
        i = pl.program_id(0)

        @pl.when(i == 0)
        def _():
            dw_ref[...] = jnp.zeros_like(dw_ref)
            db_ref[...] = jnp.zeros_like(db_ref)

        for g in range(ng):
            sl = slice(g * LANES, (g + 1) * LANES)
            vg = v_ref[:, sl]
            dgv = dg_ref[:, sl]
            mixed = _dot(w_ref[g], vg) + b_ref[:, sl]
            du_ref[:, sl] = dgv * mixed
            dmix = dgv * u_ref[:, sl]
            dmb = dmix.astype(BF16)
            dv_ref[:, sl] = _dot_tn(w_ref[g], dmb)
            dw_ref[g] += _dot_nt(dmb, vg)
            db_ref[:, sl] += dmix

    return pl.pallas_call(
        kern,
        name=name,
        grid=(s // t,),
        in_specs=[pl.BlockSpec((t, c), lambda i: (i, 0))] * 3 +
                 [pl.BlockSpec(wc.shape, lambda i: (0, 0, 0)), pl.BlockSpec(bst.shape, lambda i: (0, 0))],
        out_specs=[pl.BlockSpec((t, c), lambda i: (i, 0)), pl.BlockSpec((t, c), lambda i: (i, 0)),
                   pl.BlockSpec(wc.shape, lambda i: (0, 0, 0)), pl.BlockSpec(bst.shape, lambda i: (0, 0))],
        out_shape=[jax.ShapeDtypeStruct((s, c), F32), jax.ShapeDtypeStruct((s, c), F32),
                   jax.ShapeDtypeStruct(wc.shape, F32), jax.ShapeDtypeStruct(bst.shape, F32)],
        compiler_params=_params(("arbitrary",)),
    )(dgate, u, vn, wc, bst)


def gm_fwd(x, g, w_in, b_in, vg, wc, bst, w_out, tag):
    h = rms_fwd(x, g, f"gm_rms_{tag}")
    pre = mm(h, w_in, bias=b_in, b_chunks=True, out_dtype=BF16, name=f"gm_in_{tag}")
    u, vn = gm_act_fwd(pre, vg, f"gm_act_{tag}")
    gate = gm_spatial_fwd(u, vn, wc, bst, f"gm_spatial_{tag}")
    xn = mm(gate, w_out, add=x, name=f"gm_out_{tag}")
    return xn, (x, h, pre, u, vn, gate)


def gm_bwd(dxn, saved, g, w_in, vg, wc, bst, w_out, tag):
    x, h, pre, u, vn, gate = saved
    dxn, dxb = dxn
    dgate = mm(dxb, w_out, tb=True, name=f"gm_dgate_{tag}")
    dwout = mm(gate, dxb, ta=True, out_dtype=BF16, name=f"gm_dwout_{tag}")
    du, dvn, dws, dbst = gm_spatial_bwd(dgate, u, vn, wc, bst, f"gm_dspatial_{tag}")
    dpre, dvg, dbin = gm_act_bwd(pre, du, dvn, vg, f"gm_dact_{tag}")
    dh = mm(dpre, w_in, tb=True, b_chunks=True, name=f"gm_dh_{tag}")
    dwin = mm(h, dpre, ta=True, out_dtype=BF16, out_chunks=True, name=f"gm_dwin_{tag}")
    dx, dg = rms_bwd(x, g, dh, dxn, f"gm_drms_{tag}")
    ng = wc.shape[0]
    dws = jnp.where(jnp.tril(jnp.ones((CHUNK, CHUNK), bool)), dws, 0.0)
    dbs = dbst.reshape(CHUNK, ng, LANES).sum(-1).T
    return dx, dg, dwin, dbin, dvg, dws, dbs, dwout


def _conv_taps(xv, prev):
    cat = jnp.concatenate([prev, xv], axis=0)
    return [pltpu.roll(cat, sh, 0)[SUBLANES:] for sh in (3, 2, 1)] + [xv]


def conv_fwd(xbc, ws, b, d_inner, name):
    c = xbc.shape[1]
    nst = (c - d_inner) // 2

    def fn(xv, prev, w0, w1, w2, w3, bv):
        taps = _conv_taps(xv, prev)
        pre = bv + w0 * taps[0] + w1 * taps[1] + w2 * taps[2] + w3 * taps[3]
        out = pre * _sigmoid(pre)
        return out[:, :d_inner], out[:, d_inner:d_inner + nst], out[:, d_inner + nst:]

    return rowwise(fn, [(xbc, "row"), (xbc, "prev")] + [(w, "full") for w in ws] + [(b, "full")],
                   [(d_inner, F32), (nst, F32), (nst, F32)], tr=256, name=name)


def conv_bwd_pre(xbc, ws, b, dxs_a, dxs_b, db_m, dc_m, name):
    c = xbc.shape[1]

    def fn(xv, prev, w0, w1, w2, w3, bv, da, db2, dbm, dcm):
        taps = _conv_taps(xv, prev)
        pre = bv + w0 * taps[0] + w1 * taps[1] + w2 * taps[2] + w3 * taps[3]
        sg = _sigmoid(pre)
        dout = jnp.concatenate([da + db2, dbm, dcm], axis=1)
        dpre = dout * sg * (1.0 + pre * (1.0 - sg))
        return (dpre,) + tuple(_colsum(dpre * tp) for tp in taps) + (_colsum(dpre),)

    return rowwise(fn, [(xbc, "row"), (xbc, "prev")] + [(w, "full") for w in ws] +
                   [(b, "full"), (dxs_a, "row"), (dxs_b, "row"), (db_m, "row"), (dc_m, "row")],
                   [(c, F32)], [(1, c)] * 5, tr=256, name=name)


def conv_bwd_in(dpre, ws, name):
    c = dpre.shape[1]

    def fn(dv, nxt, w0, w1, w2, w3):
        cat = jnp.concatenate([dv, nxt], axis=0)
        n = cat.shape[0]
        up = [pltpu.roll(cat, n - sh, 0)[:dv.shape[0]] for sh in (1, 2, 3)]
        return (w3 * dv + w2 * up[0] + w1 * up[1] + w0 * up[2],)

    return rowwise(fn, [(dpre, "row"), (dpre, "next")] + [(w, "full") for w in ws], [(c, BF16)], tr=256, name=name)[0]


def ssd_pre(dtr, bias, alog, name):
    def fn(d, bv, al, tri):
        dt = _softplus(d + bv)
        a = dt * (-jnp.exp(al))
        return dt, _dot_x3_left(tri, a)

    tri = jnp.tril(jnp.ones((CHUNK, CHUNK), BF16))
    return rowwise(fn, [(dtr, "row"), (bias, "full"), (alog, "full"), (tri, "full")],
                   [(LANES, F32), (LANES, F32)], tr=CHUNK, name=name)


def _ssd_layouts(v, ngroups, hpg):
    s = v.shape[0]
    col = v[:, :ngroups * hpg].T.reshape(ngroups, hpg, s, 1)
    return jnp.broadcast_to(col, (ngroups, hpg, s, LANES))


def _ssd_rowform(acum, ngroups, hpg):
    s = acum.shape[0]
    nc = s // CHUNK
    a = acum[:, :ngroups * hpg].reshape(nc, CHUNK, ngroups, hpg).transpose(2, 0, 3, 1)
    last = jnp.broadcast_to(a[..., CHUNK - 1:], a.shape)
    return jnp.concatenate([a, last], axis=2)


def ssd_chunk_fwd(xs, bm, cm, col_a, col_dt, rowf, name, side=None):
    s, d_inner = xs.shape
    ln = CHUNK
    nc = s // ln
    nsub = _pick(nc, (SSD_SUB, 2, 1))
    rows = nsub * ln
    ng, hpg = col_a.shape[0], col_a.shape[1]
    gw = d_inner // ng
    assert gw == hpg * HEAD and gw % LANES == 0 and bm.shape[1] == ng * LANES

    def kern(x_ref, b_ref, c_ref, ca_ref, cd_ref, rf_ref, y_ref, hp_ref, h_scr):
        @pl.when(pl.program_id(1) == 0)
        def _():
            h_scr[...] = jnp.zeros_like(h_scr)

        causal = _iota2((ln, ln), 0) >= _iota2((ln, ln), 1)
        lane = _iota2((1, LANES), 1)
        for sc in range(nsub):
            rs = slice(sc * ln, (sc + 1) * ln)
            bb = b_ref[rs, :].astype(BF16)
            cbf = c_ref[rs, :].astype(BF16)
            cb = _dot_nt(cbf, bb)
            ys = [jnp.zeros((ln, LANES), F32) for _ in range(gw // LANES)]
            for r in range(hpg):
                j, hf = divmod(r, LANES // HEAD)
                mh = ((lane >= HEAD * hf) & (lane < HEAD * (hf + 1))).astype(F32)
                ac = ca_ref[r, rs, :]
                ar = rf_ref[sc, pl.ds(r, 1), :]
                aend = rf_ref[sc, pl.ds(4 + r, 1), :]
                dm = jnp.exp(jnp.minimum(ac - ar, 0.0))
                m = jnp.where(causal, cb * dm, 0.0).astype(BF16)
                xdt = x_ref[rs, j * LANES:(j + 1) * LANES] * cd_ref[r, rs, :] * mh
                h = h_scr[r]
                hp_ref[sc, r] = h
                ys[j] = ys[j] + _dot(m, xdt.astype(BF16)) + _dot_nt(cbf, h.astype(BF16)) * jnp.exp(ac)
                dte = jnp.exp(aend - ac)
                h_scr[r] = jnp.exp(aend) * h + _dot_tn((xdt * dte).astype(BF16), bb)
            for j in range(gw // LANES):
                y_ref[rs, j * LANES:(j + 1) * LANES] = ys[j]

    colspec = pl.BlockSpec((None, hpg, rows, LANES), lambda g, c: (g, 0, c, 0))
    return side_call(
        kern, side,
        name=name,
        grid=(ng, nc // nsub),
        in_specs=[pl.BlockSpec((rows, gw), lambda g, c: (c, g)),
                  pl.BlockSpec((rows, LANES), lambda g, c: (c, g)),
                  pl.BlockSpec((rows, LANES), lambda g, c: (c, g)),
                  colspec, colspec,
                  pl.BlockSpec((None, nsub, 8, LANES), lambda g, c: (g, c, 0, 0))],
        out_specs=[pl.BlockSpec((rows, gw), lambda g, c: (c, g)),
                   pl.BlockSpec((None, nsub, hpg, LANES, LANES), lambda g, c: (g, c, 0, 0, 0))],
        out_shape=[jax.ShapeDtypeStruct((s, d_inner), F32),
                   jax.ShapeDtypeStruct((ng, nc, hpg, LANES, LANES), F32)],
        scratch_shapes=[pltpu.VMEM((hpg, LANES, LANES), F32)],
        args=(xs, bm, cm, col_a, col_dt, rowf))


def ssd_chunk_bwd(xs, bm, cm, col_a, col_dt, rowf, hprev, dy, name, side=None):
    s, d_inner = xs.shape
    ln = CHUNK
    nc = s // ln
    nsub = _pick(nc, (SSD_SUB, 2, 1))
    rows = nsub * ln
    ng, hpg = col_a.shape[0], col_a.shape[1]
    gw = d_inner // ng

    def kern(x_ref, b_ref, c_ref, ca_ref, cd_ref, rf_ref, hp_ref, dy_ref,
             dx_ref, db_ref, dc_ref, ddt_ref, da_ref, dh_scr):
        @pl.when(pl.program_id(1) == 0)
        def _():
            dh_scr[...] = jnp.zeros_like(dh_scr)

        row, col = _iota2((ln, ln), 0), _iota2((ln, ln), 1)
        causal = row >= col
        tri_ge = (col >= row).astype(BF16)
        ones = jnp.ones((ln, LANES), BF16)
        lane = _iota2((1, LANES), 1)
        last_row = (_iota2((ln, 1), 0) == ln - 1).astype(F32)
        for sc in reversed(range(nsub)):
            rs = slice(sc * ln, (sc + 1) * ln)
            bb = b_ref[rs, :].astype(BF16)
            cbf = c_ref[rs, :].astype(BF16)
            cb = _dot_nt(cbf, bb)
            dcb = jnp.zeros((ln, ln), F32)
            d_b = jnp.zeros((ln, LANES), F32)
            d_c = jnp.zeros((ln, LANES), F32)
            dxs = [jnp.zeros((ln, LANES), F32) for _ in range(gw // LANES)]
            for r in range(hpg):
                j, hf = divmod(r, LANES // HEAD)
                mh = ((lane >= HEAD * hf) & (lane < HEAD * (hf + 1))).astype(F32)
                ac = ca_ref[r, rs, :]
                dt = cd_ref[r, rs, :]
                ar = rf_ref[sc, pl.ds(r, 1), :]
                aend = rf_ref[sc, pl.ds(4 + r, 1), :]
                dm = jnp.where(causal, jnp.exp(jnp.minimum(ac - ar, 0.0)), 0.0)
                m = cb * dm
                mb = m.astype(BF16)
                xp = x_ref[rs, j * LANES:(j + 1) * LANES]
                xdt = xp * dt * mh
                xdtb = xdt.astype(BF16)
                dyp = dy_ref[rs, j * LANES:(j + 1) * LANES] * mh
                dypb = dyp.astype(BF16)
                h = hp_ref[sc, r]
                hb = h.astype(BF16)
                dh = dh_scr[r]
                dhb = dh.astype(BF16)
                e_in = jnp.exp(ac)
                dte = jnp.exp(aend - ac)
                eend = jnp.exp(aend)
                d_m = _dot_nt(dypb, xdtb)
                dcb = dcb + d_m * dm
                gm = d_m * m
                yoff_pre = _dot_nt(cbf, hb)
                bdh = _dot_nt(bb, dhb)
                dxdt = _dot_tn(mb, dypb) + bdh * dte
                t1 = _rowsum(xdt * bdh) * dte
                gh, gl = _split2(gm)
                dacum = (_rowsum(gm) - (_dot_tn(gh, ones) + _dot_tn(gl, ones))
                         + _rowsum(dyp * yoff_pre) * e_in - t1)
                end_term = _colsum(t1) + eend * jnp.sum(_colsum(dh * h), axis=1, keepdims=True)
                dacum = dacum + last_row * end_term
                da_ref[r, rs, :] = _dot_x3_left(tri_ge, dacum)
                ddt_ref[r, rs, :] = jnp.broadcast_to(_rowsum(dxdt * xp), (ln, LANES))
                dxs[j] = dxs[j] + dxdt * dt
                d_b = d_b + _dot((xdt * dte).astype(BF16), dhb)
                dye = (dyp * e_in).astype(BF16)
                d_c = d_c + _dot(dye, hb)
                dh_scr[r] = eend * dh + _dot_tn(dye, cbf)
            dcbb = dcb.astype(BF16)
            dc_ref[rs, :] = d_c + _dot(dcbb, bb)
            db_ref[rs, :] = d_b + _dot_tn(dcbb, cbf)
            for j in range(gw // LANES):
                dx_ref[rs, j * LANES:(j + 1) * LANES] = dxs[j]

    rev = nc // nsub - 1
    colspec = pl.BlockSpec((None, hpg, rows, LANES), lambda g, c: (g, 0, rev - c, 0))
    return side_call(
        kern, side,
        name=name,
        grid=(ng, nc // nsub),
        in_specs=[pl.BlockSpec((rows, gw), lambda g, c: (rev - c, g)),
                  pl.BlockSpec((rows, LANES), lambda g, c: (rev - c, g)),
                  pl.BlockSpec((rows, LANES), lambda g, c: (rev - c, g)),
                  colspec, colspec,
                  pl.BlockSpec((None, nsub, 8, LANES), lambda g, c: (g, rev - c, 0, 0)),
                  pl.BlockSpec((None, nsub, hpg, LANES, LANES), lambda g, c: (g, rev - c, 0, 0, 0)),
                  pl.BlockSpec((rows, gw), lambda g, c: (rev - c, g))],
        out_specs=[pl.BlockSpec((rows, gw), lambda g, c: (rev - c, g)),
                   pl.BlockSpec((rows, LANES), lambda g, c: (rev - c, g)),
                   pl.BlockSpec((rows, LANES), lambda g, c: (rev - c, g)),
                   colspec, colspec],
        out_shape=[jax.ShapeDtypeStruct((s, d_inner), F32),
                   jax.ShapeDtypeStruct(bm.shape, F32), jax.ShapeDtypeStruct(cm.shape, F32),
                   jax.ShapeDtypeStruct(col_a.shape, F32), jax.ShapeDtypeStruct(col_a.shape, F32)],
        scratch_shapes=[pltpu.VMEM((hpg, LANES, LANES), F32)],
        args=(xs, bm, cm, col_a, col_dt, rowf, hprev, dy))


def gnorm_fwd(y, xs, z, dexp, gain, ngroups, name):
    c = y.shape[1]
    gw = c // ngroups

    def fn(yv, xv, zv, dv, gv):
        yg = (yv + xv * dv) * (zv * _sigmoid(zv))
        outs = []
        for k in range(ngroups):
            t = yg[:, k * gw:(k + 1) * gw]
            outs.append(t * lax.rsqrt(jnp.mean(t * t, axis=1, keepdims=True) + EPS))
        return (jnp.concatenate(outs, axis=1) * gv,)

    return rowwise(fn, [(y, "row"), (xs, "row"), (z, "row"), (dexp, "full"), (gain, "full")], [(c, BF16)], tr=256, name=name)[0]


def gnorm_bwd(dn, y, xs, z, dexp, gain, ngroups, name):
    c = y.shape[1]
    gw = c // ngroups

    def fn(dnv, yv, xv, zv, dv, gv):
        yd = yv + xv * dv
        sg = _sigmoid(zv)
        sz = zv * sg
        yg = yd * sz
        dng = dnv * gv
        dyg, yh = [], []
        for k in range(ngroups):
            sl = slice(k * gw, (k + 1) * gw)
            t = yg[:, sl]
            r = lax.rsqrt(jnp.mean(t * t, axis=1, keepdims=True) + EPS)
            th = t * r
            dyg.append(r * (dng[:, sl] - th * jnp.mean(dng[:, sl] * th, axis=1, keepdims=True)))
            yh.append(th)
        dyg = jnp.concatenate(dyg, axis=1)
        yh = jnp.concatenate(yh, axis=1)
        dyd = dyg * sz
        dz = dyg * yd * (sg * (1.0 + zv * (1.0 - sg)))
        return dyd, dyd * dv, dz, _colsum(dyd * xv), _colsum(dnv * yh)

    return rowwise(fn, [(dn, "row"), (y, "row"), (xs, "row"), (z, "row"), (dexp, "full"), (gain, "full")],
                   [(c, F32), (c, F32), (c, BF16)], [(1, c), (1, c)], tr=256, name=name)


def ssd_post(ddt, da, dt, dtr, bias, alog, name):
    def fn(ddtv, dav, dtv, dtrv, bv, al):
        a_neg = -jnp.exp(al)
        ddtr = (ddtv + dav * a_neg) * _sigmoid(dtrv + bv)
        return ddtr, _colsum(ddtr), _colsum(dav * dtv) * a_neg

    return rowwise(fn, [(ddt, "row"), (da, "row"), (dt, "row"), (dtr, "row"), (bias, "full"), (alog, "full")],
                   [(LANES, BF16)], [(1, LANES), (1, LANES)], tr=512, name=name)


def _from_colform(v, s):
    ng, hpg = v.shape[0], v.shape[1]
    flat = v[..., 0].reshape(ng * hpg, s).T
    return jnp.pad(flat, ((0, 0), (0, LANES - ng * hpg)))


def ssm_fwd(x, g, p, tag, plan):
    ng, hpg, d_inner = p["ng"], p["hpg"], p["d_inner"]
    h = rms_fwd(x, g, f"ssm_rms_{tag}")
    z = mm(h, p["w_z"], name=f"ssm_inz_{tag}")
    xbc = mm(h, p["w_xbc"], name=f"ssm_inx_{tag}")
    dtr = mm(h, p["w_dt"], name=f"ssm_indt_{tag}")
    xs, bm, cm = conv_fwd(xbc, p["conv_w"], p["conv_b"], d_inner, f"ssm_conv_{tag}")
    dt, acum = ssd_pre(dtr, p["dt_bias"], p["a_log"], f"ssm_pre_{tag}")
    col_a, col_dt = _ssd_layouts(acum, ng, hpg), _ssd_layouts(dt, ng, hpg)
    rowf = _ssd_rowform(acum, ng, hpg)
    y, hprev = _hooked(plan, f"ssm_scan_{tag}", ssd_chunk_fwd, xs, bm, cm, col_a, col_dt, rowf)
    n = gnorm_fwd(y, xs, z, p["d_exp"], p["norm_gain"], ng, f"ssm_gnorm_{tag}")
    xn = mm(n, p["w_out"], add=x, name=f"ssm_out_{tag}")
    return xn, (x, h, z, xbc, dtr, xs, bm, cm, dt, col_a, col_dt, rowf, y, hprev, n)


def ssm_bwd(dxn, saved, g, p, tag, plan):
    x, h, z, xbc, dtr, xs, bm, cm, dt, col_a, col_dt, rowf, y, hprev, n = saved
    ng, hpg, d_inner = p["ng"], p["hpg"], p["d_inner"]
    s = x.shape[0]
    dxn, dxb = dxn
    dn = mm(dxb, p["w_out"], tb=True, name=f"ssm_dn_{tag}")
    dwout = mm(n, dxb, ta=True, out_dtype=BF16, name=f"ssm_dwout_{tag}")
    dy, dxs_skip, dz, dd_lane, dgain = gnorm_bwd(dn, y, xs, z, p["d_exp"], p["norm_gain"], ng, f"ssm_dgnorm_{tag}")
    dxs, dbm, dcm, ddt_c, da_c = _hooked(plan, f"ssm_dscan_{tag}", ssd_chunk_bwd, xs, bm, cm, col_a, col_dt, rowf, hprev, dy)
    ddtr, dbias, dalog = ssd_post(_from_colform(ddt_c, s), _from_colform(da_c, s), dt, dtr,
                                  p["dt_bias"], p["a_log"], f"ssm_post_{tag}")
    res = conv_bwd_pre(xbc, p["conv_w"], p["conv_b"], dxs, dxs_skip, dbm, dcm, f"ssm_dconv_{tag}")
    dpre, dconv_w, dconv_b = res[0], jnp.concatenate(res[1:5], axis=0), res[5]
    dxbc = conv_bwd_in(dpre, p["conv_w"], f"ssm_dconvin_{tag}")
    dh = mm(dz, p["w_z"], tb=True, name=f"ssm_dhz_{tag}")
    dh = mm(dxbc, p["w_xbc"], tb=True, add=dh, name=f"ssm_dhx_{tag}")
    dh = mm(ddtr, p["w_dt"], tb=True, add=dh, name=f"ssm_dhdt_{tag}")
    dwz = mm(h, dz, ta=True, out_dtype=BF16, name=f"ssm_dwz_{tag}")
    dwxbc = mm(h, dxbc, ta=True, out_dtype=BF16, name=f"ssm_dwxbc_{tag}")
    dwdt = mm(h, ddtr, ta=True, out_dtype=BF16, name=f"ssm_dwdt_{tag}")
    dx, dg = rms_bwd(x, g, dh, dxn, f"ssm_drms_{tag}")
    nh = ng * hpg
    dwin = jnp.concatenate([dwz, dwxbc, dwdt[:, :nh]], axis=1)
    dd = dd_lane.reshape(nh, HEAD).sum(-1)
    return dx, dg, dict(w_in=dwin, conv_w=dconv_w, conv_b=dconv_b, dt_bias=dbias[0, :nh], a_log=dalog[0, :nh],
                        d=dd, norm_gain=dgain, w_out=dwout)


def local_step(x, target, w, plan):
    d = x.shape[1]
    depth = w["mix_norm"].shape[0]
    bd = _head_blockdiag(LANES)
    tril = jnp.tril(jnp.ones((CHUNK, CHUNK), bool))
    ssm_heads = w["ssm_dt_bias"].shape[1]
    d_inner = w["ssm_norm_gain"].shape[1]
    ng = w["ssm_norm_gain"].shape[1] // 256
    nstate = CHUNK

    def pad_lanes(v):
        return jnp.pad(v, ((0, 0), (0, LANES - v.shape[1])))

    def ssm_params(j):
        w_in = w["ssm_w_in"][j]
        cw = w["ssm_conv_w"][j]
        return dict(ng=ng, hpg=ssm_heads // ng, d_inner=d_inner,
                    w_z=w_in[:, :d_inner], w_xbc=w_in[:, d_inner:d_inner + d_inner + 2 * ng * nstate],
                    w_dt=pad_lanes(w_in[:, 2 * d_inner + 2 * ng * nstate:]),
                    conv_w=[cw[k:k + 1] for k in range(cw.shape[0])], conv_b=w["ssm_conv_b"][j:j + 1],
                    dt_bias=pad_lanes(w["ssm_dt_bias"][j:j + 1]), a_log=pad_lanes(w["ssm_a_log"][j:j + 1]),
                    d_exp=jnp.repeat(w["ssm_d"][j], HEAD)[None, :], norm_gain=w["ssm_norm_gain"][j:j + 1],
                    w_out=w["ssm_w_out"][j])

    def gm_params(j):
        wc = jnp.where(tril, w["gm_w_s"][j], 0.0).astype(BF16)
        bst = jnp.repeat(w["gm_b_s"][j].T, LANES, axis=1)
        return wc, bst

    def sb_gains(j):
        nh = d // HEAD
        return jnp.tile(w["sb_q_gain"][j], nh)[None, :], jnp.tile(w["sb_k_gain"][j], nh)[None, :]

    saved = []
    cur = x
    for i in range(depth):
        kind, j = i % 3, i // 3
        gmix = w["mix_norm"][i:i + 1]
        if kind == 0:
            qg, kg = sb_gains(j)
            cur, sv = sb_fwd(cur, gmix, w["sb_w_qkv"][j], qg, kg, lambda j=j: w["sb_w_o"][j], bd, f"{i}", plan)
        elif kind == 1:
            wc, bst = gm_params(j)
            cur, sv = gm_fwd(cur, gmix, w["gm_w_in"][j], w["gm_b_in"][j:j + 1], w["gm_v_gain"][j:j + 1], wc, bst,
                             w["gm_w_out"][j], f"{i}")
        else:
            cur, sv = ssm_fwd(cur, gmix, ssm_params(j), f"{i}", plan)
        cur, sv2 = ffn_fwd(cur, w["ffn_norm"][i:i + 1], w["ffn_w_gu"][i], w["ffn_w_down"][i], f"{i}", plan)
        saved.append((sv, sv2))

    loss, dcur = loss_and_grad(cur, target, "loss")

    grads = {k: [None] * len(v) for k, v in w.items()}
    for i in reversed(range(depth)):
        kind, j = i % 3, i // 3
        sv, sv2 = saved[i]
        gmix = w["mix_norm"][i:i + 1]
        dcur, dgf, dwgu, dwdown = ffn_bwd(dcur, sv2, w["ffn_norm"][i:i + 1], w["ffn_w_gu"][i], w["ffn_w_down"][i], f"{i}")
        grads["ffn_norm"][i], grads["ffn_w_gu"][i], grads["ffn_w_down"][i] = dgf[0], dwgu, dwdown
        plan.grads_ready({("ffn_w_gu", i): dwgu, ("ffn_w_down", i): dwdown})
        if kind == 0:
            qg, kg = sb_gains(j)
            dcur, dg, dwqkv, dqg, dkg, dwo = sb_bwd(dcur, sv, gmix, w["sb_w_qkv"][j], qg, kg, w["sb_w_o"][j], bd, f"{i}", plan)
            grads["sb_w_qkv"][j], grads["sb_q_gain"][j], grads["sb_k_gain"][j], grads["sb_w_o"][j] = dwqkv, dqg, dkg, dwo
        elif kind == 1:
            wc, bst = gm_params(j)
            dcur, dg, dwin, dbin, dvg, dws, dbs, dwout = gm_bwd(dcur, sv, gmix, w["gm_w_in"][j], w["gm_v_gain"][j:j + 1],
                                                                 wc, bst, w["gm_w_out"][j], f"{i}")
            grads["gm_w_in"][j], grads["gm_b_in"][j], grads["gm_v_gain"][j] = dwin, dbin[0], dvg[0]
            grads["gm_w_s"][j], grads["gm_b_s"][j], grads["gm_w_out"][j] = dws, dbs, dwout
        else:
            dcur, dg, gs = ssm_bwd(dcur, sv, gmix, ssm_params(j), f"{i}", plan)
            grads["ssm_w_in"][j], grads["ssm_conv_w"][j], grads["ssm_conv_b"][j] = gs["w_in"], gs["conv_w"], gs["conv_b"][0]
            grads["ssm_dt_bias"][j], grads["ssm_a_log"][j], grads["ssm_d"][j] = gs["dt_bias"], gs["a_log"], gs["d"]
            grads["ssm_norm_gain"][j], grads["ssm_w_out"][j] = gs["norm_gain"][0], gs["w_out"]
        grads["mix_norm"][i] = dg[0]
        mixer = {0: ("sb_w_qkv", "sb_w_o"), 1: ("gm_w_in", "gm_w_out"), 2: ("ssm_w_in", "ssm_w_out")}[kind]
        plan.grads_ready({(n, j): grads[n][j] for n in mixer})
    grads = {k: (v if k in MATRICES else jnp.stack(v)) for k, v in grads.items()}
    return loss, dcur[0], grads


WEIGHTS = ["mix_norm", "ffn_norm", "sb_w_qkv", "sb_q_gain", "sb_k_gain", "sb_w_o", "gm_w_in", "gm_b_in", "gm_v_gain",
           "gm_w_s", "gm_b_s", "gm_w_out", "ssm_w_in", "ssm_conv_w", "ssm_conv_b", "ssm_dt_bias", "ssm_a_log", "ssm_d",
           "ssm_norm_gain", "ssm_w_out", "ffn_w_gu", "ffn_w_down"]
SHARDED = {"sb_w_qkv": 2, "sb_w_o": 1, "gm_w_in": 2, "gm_w_out": 1, "ssm_w_in": 2, "ssm_conv_w": 2, "ssm_conv_b": 1,
           "ssm_norm_gain": 1, "ssm_w_out": 1, "ffn_w_gu": 2, "ffn_w_down": 1}
EXACT = ("ssm_conv_w", "ssm_conv_b", "ssm_norm_gain")
MATRICES = tuple(n for n in SHARDED if n not in EXACT)
COLUMN_BLOCKS = ("sb_w_qkv", "gm_w_in", "ffn_w_gu")
REPLICATED = [n for n in WEIGHTS if n not in SHARDED]
N_CHIPS = 4
N_DEV = 8
PACK_COLS = 1024


def _pack(pieces, dtype, align):
    flat = jnp.concatenate([p.reshape(-1).astype(dtype) for p in pieces])
    rows = -(-flat.shape[0] // (PACK_COLS * align)) * align
    flat = jnp.pad(flat, (0, rows * PACK_COLS - flat.shape[0]))
    return flat.reshape(rows, PACK_COLS)


def _unpack(flat, shapes):
    out, off = [], 0
    for shp in shapes:
        n = math.prod(shp)
        out.append(flat[off:off + n].reshape(shp))
        off += n
    return out


ANY = pl.BlockSpec(memory_space=pl.ANY)


def _pos():
    return lax.axis_index("x"), lax.axis_index("y"), lax.axis_index("c")


def _remote(src, dst, send, recv, k, to):
    return pltpu.make_async_remote_copy(src_ref=src, dst_ref=dst, send_sem=send.at[k], recv_sem=recv.at[k],
                                        device_id=to, device_id_type=MESH_ID)


def _comm_call(body, name, ins, out_shapes, nsem, aliases=None):
    return pl.pallas_call(
        body, name=name, out_shape=out_shapes,
        in_specs=[ANY] * len(ins), out_specs=[ANY] * len(out_shapes),
        scratch_shapes=[pltpu.SemaphoreType.DMA((nsem,)), pltpu.SemaphoreType.DMA((nsem,))],
        input_output_aliases=aliases or {},
    )(*ins)


def stage_shard(w, chip, name):
    rows, cols = w.shape
    tr = _pick(rows, (256, 352, 128))

    def kern(idx_ref, w_ref, o_ref):
        o_ref[...] = w_ref[...].astype(BF16)

    grid_spec = pltpu.PrefetchScalarGridSpec(
        num_scalar_prefetch=1, grid=(rows // tr,),
        in_specs=[pl.BlockSpec((tr, cols), lambda i, idx: (i, 0))],
        out_specs=pl.BlockSpec((None, tr, cols), lambda i, idx: (idx[0], i, 0)))
    return pl.pallas_call(
        kern, name=name, grid_spec=grid_spec,
        out_shape=jax.ShapeDtypeStruct((N_CHIPS, rows, cols), BF16),
        compiler_params=_params(("parallel",)),
    )(jnp.reshape(chip, (1,)).astype(jnp.int32), w)


class Side:
    def __init__(self, arrays, out_shapes, aliases, nsem, start, finish):
        self.arrays, self.out_shapes, self.aliases, self.nsem = list(arrays), list(out_shapes), aliases, nsem
        self.start, self.finish = start, finish


def run_side(side, name):
    n_in, n_out = len(side.arrays), len(side.out_shapes)

    def body(*refs):
        ins, outs = refs[:n_in], refs[n_in:n_in + n_out]
        send, recv = refs[n_in + n_out:]
        side.start(ins, outs, send, recv)
        side.finish(ins, outs, send, recv)

    return _comm_call(body, name, side.arrays, side.out_shapes, side.nsem, aliases=side.aliases)


def side_call(kern, side, *, name, grid, in_specs, out_specs, out_shape, scratch_shapes, args):
    if side is None:
        res = pl.pallas_call(kern, name=name, grid=grid, in_specs=in_specs, out_specs=out_specs, out_shape=out_shape,
                             scratch_shapes=scratch_shapes,
                             compiler_params=_params(("parallel",) + ("arbitrary",) * (len(grid) - 1)))(*args)
        return list(res), []
    n_in, n_out, n_scr = len(in_specs), len(out_specs), len(scratch_shapes)
    s_in, s_out = len(side.arrays), len(side.out_shapes)

    def body(*refs):
        ins, refs = refs[:n_in], refs[n_in:]
        side_ins, refs = refs[:s_in], refs[s_in:]
        outs, refs = refs[:n_out], refs[n_out:]
        side_outs, refs = refs[:s_out], refs[s_out:]
        scr, (send, recv) = refs[:n_scr], refs[n_scr:]
        first, last = None, None
        for axis, size in enumerate(grid):
            at0, at1 = pl.program_id(axis) == 0, pl.program_id(axis) == size - 1
            first = at0 if first is None else first & at0
            last = at1 if last is None else last & at1

        @pl.when(first)
        def _():
            side.start(side_ins, side_outs, send, recv)

        kern(*ins, *outs, *scr)

        @pl.when(last)
        def _():
            side.finish(side_ins, side_outs, send, recv)

    res = pl.pallas_call(
        body, name=name, grid=grid,
        in_specs=list(in_specs) + [ANY] * s_in, out_specs=list(out_specs) + [ANY] * s_out,
        out_shape=list(out_shape) + side.out_shapes,
        scratch_shapes=list(scratch_shapes) + [pltpu.SemaphoreType.DMA((side.nsem,)), pltpu.SemaphoreType.DMA((side.nsem,))],
        input_output_aliases={n_in + a: n_out + b for a, b in side.aliases.items()},
        compiler_params=_params(("arbitrary",) * len(grid)),
    )(*args, *side.arrays)
    return list(res[:n_out]), list(res[n_out:])


def gather_side(staged):
    n = len(staged)

    def plan(o_refs, send, recv):
        x, y, c = _pos()
        chips = [(1 - x, y), (x, 1 - y), (1 - x, 1 - y)]

        def part(u, chip, cc):
            half = staged[u].shape[1] // 2
            return o_refs[u].at[2 * chip[0] + chip[1], pl.ds(cc * half, half), :]

        first = [_remote(part(u, (x, y), c), part(u, (x, y), c), send, recv, 6 * u + j, (*chip, c))
                 for u in range(n) for j, chip in enumerate(chips)]
        landed = [_remote(part(u, chip, c), part(u, chip, c), send, recv, 6 * u + j, (x, y, c))
                  for u in range(n) for j, chip in enumerate(chips)]
        passed = [_remote(part(u, chip, c), part(u, chip, c), send, recv, 6 * u + 3 + j, (x, y, 1 - c))
                  for u in range(n) for j, chip in enumerate(chips)]
        handed = [_remote(part(u, chip, 1 - c), part(u, chip, 1 - c), send, recv, 6 * u + 3 + j, (x, y, c))
                  for u in range(n) for j, chip in enumerate(chips)]
        return first, landed, passed, handed

    def start(ins, outs, send, recv):
        for cp in plan(outs, send, recv)[0]:
            cp.start()

    def finish(ins, outs, send, recv):
        first, landed, passed, handed = plan(outs, send, recv)
        for got, fw in zip(landed, passed):
            got.wait_recv()
            fw.start()
        for got in handed:
            got.wait_recv()
        for cp in first + passed:
            cp.wait_send()

    outs = [jax.ShapeDtypeStruct(s.shape, s.dtype) for s in staged]
    return Side(staged, outs, {u: u for u in range(n)}, 6 * n, start, finish)


def swap_halves(gps, name):
    n = len(gps)

    def body(*refs):
        g_refs, r_refs = refs[:n], refs[n:2 * n]
        send, recv = refs[2 * n:]
        x, y, c = _pos()
        cps = []
        for u in range(n):
            half = gps[u].shape[1] // 2
            cps.append(_remote(g_refs[u].at[:, pl.ds((1 - c) * half, half), :], r_refs[u], send, recv, u, (x, y, 1 - c)))
        for cp in cps:
            cp.start()
        for cp in cps:
            cp.wait()

    outs = [jax.ShapeDtypeStruct((g.shape[0], g.shape[1] // 2, g.shape[2]), g.dtype) for g in gps]
    return _comm_call(body, name, gps, outs, n)


def scatter_side(parts):
    n = len(parts)

    def plan(p_refs, r_refs, send, recv):
        x, y, c = _pos()
        chips = [(1 - x, y), (x, 1 - y), (1 - x, 1 - y)]
        return [_remote(p_refs[u].at[2 * chip[0] + chip[1]], r_refs[u].at[j], send, recv, 3 * u + j, (*chip, c))
                for u in range(n) for j, chip in enumerate(chips)]

    def start(ins, outs, send, recv):
        for cp in plan(ins, outs, send, recv):
            cp.start()

    def finish(ins, outs, send, recv):
        for cp in plan(ins, outs, send, recv):
            cp.wait()

    outs = [jax.ShapeDtypeStruct((N_CHIPS - 1,) + p.shape[1:], p.dtype) for p in parts]
    return Side(parts, outs, {}, 3 * n, start, finish)


def join_halves(bufs, name):
    n = len(bufs)

    def body(*refs):
        o_refs = refs[n:2 * n]
        send, recv = refs[2 * n:]
        x, y, c = _pos()

        def rows(u, cc):
            half = bufs[u].shape[0] // 2
            return o_refs[u].at[pl.ds(cc * half, half), :]

        cps = [_remote(rows(u, c), rows(u, c), send, recv, u, (x, y, 1 - c)) for u in range(n)]
        for cp in cps:
            cp.start()
        for u in range(n):
            _remote(rows(u, 1 - c), rows(u, 1 - c), send, recv, u, (x, y, c)).wait_recv()
        for cp in cps:
            cp.wait_send()

    outs = [jax.ShapeDtypeStruct(b.shape, b.dtype) for b in bufs]
    return _comm_call(body, name, bufs, outs, n, aliases={u: u for u in range(n)})


def small_side(sg):
    rows, cols = sg.shape
    local = N_DEV - 1

    def plan(s_ref, o_ref, send, recv):
        x, y, c = _pos()
        me, sibling = (x, y, c), (x, y, 1 - c)
        chips = [(1 - x, y), (x, 1 - y), (1 - x, 1 - y)]

        def blk(px, py, pc):
            return o_ref.at[4 * px + 2 * py + pc]

        mine = pltpu.make_async_copy(s_ref, blk(*me), send.at[local])
        first = [_remote(s_ref, blk(*me), send, recv, 0, sibling)]
        first += [_remote(s_ref, blk(*me), send, recv, 1 + j, (*chip, c)) for j, chip in enumerate(chips)]
        landed = [_remote(blk(*chip, c), blk(*chip, c), send, recv, 1 + j, me) for j, chip in enumerate(chips)]
        passed = [_remote(blk(*chip, c), blk(*chip, c), send, recv, 4 + j, sibling) for j, chip in enumerate(chips)]
        handed = [_remote(blk(*sibling), blk(*sibling), send, recv, 0, me)]
        handed += [_remote(blk(*chip, 1 - c), blk(*chip, 1 - c), send, recv, 4 + j, me) for j, chip in enumerate(chips)]
        return mine, first, landed, passed, handed

    def start(ins, outs, send, recv):
        mine, first = plan(ins[0], outs[0], send, recv)[:2]
        mine.start()
        for cp in first:
            cp.start()

    def finish(ins, outs, send, recv):
        mine, first, landed, passed, handed = plan(ins[0], outs[0], send, recv)
        for got, fw in zip(landed, passed):
            got.wait_recv()
            fw.start()
        for got in handed:
            got.wait_recv()
        for cp in first + passed:
            cp.wait_send()
        mine.wait()

    return Side([sg], [jax.ShapeDtypeStruct((N_DEV, rows, cols), sg.dtype)], {}, N_DEV, start, finish)


def gather_small(sg, name):
    return run_side(small_side(sg), name)[0]


def sum_cores(gp, theirs, core, chip, name):
    nch, rows, cols = gp.shape
    half = rows // 2
    tr = _pick(half, (256, 176, 128, 64))
    nb = half // tr

    def kern(idx_ref, g_ref, t_ref, own_ref, all_ref):
        k = pl.program_id(1)
        s = g_ref[...].astype(F32) + t_ref[...].astype(F32)
        all_ref[...] = s.astype(BF16)

        @pl.when(k == idx_ref[1])
        def _():
            own_ref[...] = s

    grid_spec = pltpu.PrefetchScalarGridSpec(
        num_scalar_prefetch=1, grid=(nb, nch),
        in_specs=[pl.BlockSpec((None, tr, cols), lambda i, k, idx: (k, idx[0] * nb + i, 0)),
                  pl.BlockSpec((None, tr, cols), lambda i, k, idx: (k, i, 0))],
        out_specs=[pl.BlockSpec((tr, cols), lambda i, k, idx: (i, 0)),
                   pl.BlockSpec((None, tr, cols), lambda i, k, idx: (k, i, 0))])
    return pl.pallas_call(
        kern, name=name, grid_spec=grid_spec,
        out_shape=[jax.ShapeDtypeStruct((half, cols), F32), jax.ShapeDtypeStruct((nch, half, cols), BF16)],
        compiler_params=_params(("parallel", "arbitrary")),
    )(jnp.stack([core, chip]).astype(jnp.int32), gp, theirs)


def sum_chips(own, others, core, name):
    half, cols = own.shape
    tr = _pick(half, (256, 176, 128, 64))
    nb = half // tr

    def kern(idx_ref, o_ref, a_ref, b_ref, c_ref, out_ref):
        out_ref[...] = ((o_ref[...] + a_ref[...].astype(F32)) + b_ref[...].astype(F32)) + c_ref[...].astype(F32)

    grid_spec = pltpu.PrefetchScalarGridSpec(
        num_scalar_prefetch=1, grid=(nb,),
        in_specs=[pl.BlockSpec((tr, cols), lambda i, idx: (i, 0))] +
                 [pl.BlockSpec((None, tr, cols), lambda i, idx, j=j: (j, i, 0)) for j in range(N_CHIPS - 1)],
        out_specs=pl.BlockSpec((tr, cols), lambda i, idx: (idx[0] * nb + i, 0)))
    return pl.pallas_call(
        kern, name=name, grid_spec=grid_spec,
        out_shape=jax.ShapeDtypeStruct((2 * half, cols), F32),
        compiler_params=_params(("parallel",)),
    )(jnp.reshape(core, (1,)).astype(jnp.int32), own, others, others, others)


def small_update(gath, w, m, v, name):
    def fn(*vs):
        g = vs[0]
        for t in vs[1:N_DEV]:
            g = g + t
        wv, mv, vv = vs[N_DEV:]
        m2 = ADAM_B1 * mv + (1.0 - ADAM_B1) * g
        v2 = ADAM_B2 * vv + (1.0 - ADAM_B2) * (g * g)
        m_hat = m2 / (1.0 - ADAM_B1 ** ADAM_STEP)
        v_hat = v2 / (1.0 - ADAM_B2 ** ADAM_STEP)
        return g, -ADAM_LR * (m_hat / (jnp.sqrt(v_hat) + ADAM_EPS) + ADAM_WD * wv), m2, v2

    c = w.shape[1]
    ins = [(gath[k], "row") for k in range(N_DEV)] + [(w, "row"), (m, "row"), (v, "row")]
    return rowwise(fn, ins, [(c, F32)] * 4, tr=w.shape[0] // 2, name=name)


_MIX = {0: [("sb_w_qkv", 0), ("sb_w_o", 0)], 1: [("gm_w_in", 0), ("gm_w_out", 0)],
        2: [("ssm_w_in", 0), ("ssm_w_out", 0)], 3: [("sb_w_qkv", 1), ("sb_w_o", 1)]}
_FFN = {i: [("ffn_w_gu", i), ("ffn_w_down", i)] for i in range(4)}
GATHER_FIRST = _MIX[0][:1]
GATHER_AT = {"sb_attn_0": _MIX[0][1:] + _FFN[0] + _FFN[1],
             "ffn_gu_0": _MIX[1], "ffn_down_0": _MIX[2][1:], "ffn_gu_1": _MIX[2][:1], "ffn_down_1": _FFN[2][1:],
             "ssm_scan_2": _FFN[2][:1] + _MIX[3] + _FFN[3][1:], "ffn_gu_2": _FFN[3][:1]}
SCATTER_AT = {"ssm_dscan_2": _FFN[3] + _MIX[3] + _FFN[2], "sb_dattn_0": _MIX[2] + _FFN[1] + _MIX[1] + _FFN[0]}
SCATTER_LAST = _MIX[0]


class _Plan:
    def __init__(self, ins, core, chip):
        self.core, self.chip = core, chip
        self.staged = {(n, l): stage_shard(ins[n][l], chip, f"stage_{n}_{l}")
                       for n in MATRICES for l in range(ins[n].shape[0])}
        self.full = {n: [None] * ins[n].shape[0] for n in MATRICES}
        self.ready = {}
        self.parts = {}
        self.halves = {}
        self.swaps = 0
        self._fill(GATHER_FIRST, run_side(gather_side([self.staged[u] for u in GATHER_FIRST]), "gather_first"))

    def _fill(self, units, gathered):
        for (n, l), g in zip(units, gathered):
            if n in COLUMN_BLOCKS:
                self.full[n][l] = g
            elif n == "ssm_w_in":
                self.full[n][l] = jnp.concatenate([g[k] for k in range(N_CHIPS)], axis=1)
            else:
                self.full[n][l] = g.reshape(-1, g.shape[-1])

    def _prepare(self, units):
        gps = [self.ready[u] for u in units]
        theirs = swap_halves(gps, f"swap_halves_{self.swaps}")
        self.swaps += 1
        for (n, l), g, t in zip(units, gps, theirs):
            self.parts[(n, l)] = sum_cores(g, t, self.core, self.chip, f"sum_cores_{n}_{l}")

    def _reduce(self, units, others):
        for (n, l), other in zip(units, others):
            self.halves[(n, l)] = sum_chips(self.parts[(n, l)][0], other, self.core, f"sum_chips_{n}_{l}")

    def side(self, tag):
        if tag in GATHER_AT:
            return gather_side([self.staged[u] for u in GATHER_AT[tag]])
        if tag in SCATTER_AT:
            self._prepare(SCATTER_AT[tag])
            return scatter_side([self.parts[u][1] for u in SCATTER_AT[tag]])
        return None

    def done(self, tag, results):
        if tag in GATHER_AT:
            self._fill(GATHER_AT[tag], results)
        else:
            self._reduce(SCATTER_AT[tag], results)

    def grads_ready(self, grads):
        for (n, l), g in grads.items():
            if n in COLUMN_BLOCKS:
                self.ready[(n, l)] = g
            elif n == "ssm_w_in":
                self.ready[(n, l)] = jnp.stack(jnp.split(g, N_CHIPS, axis=1))
            else:
                self.ready[(n, l)] = g.reshape(N_CHIPS, -1, g.shape[-1])

    def join_ready(self):
        units = sorted(self.halves)
        return dict(zip(units, join_halves([self.halves[u] for u in units], "join_halves")))

    def last_side(self):
        self._prepare(SCATTER_LAST)
        return scatter_side([self.parts[u][1] for u in SCATTER_LAST])

    def last_done(self, results):
        self.halves = {}
        self._reduce(SCATTER_LAST, results)
        return dict(zip(SCATTER_LAST, join_halves([self.halves[u] for u in SCATTER_LAST], "join_last")))


def _step(ins):
    x, target = ins["x"][0], ins["loss_target"][0]
    core = lax.axis_index("c")
    chip = 2 * lax.axis_index("x") + lax.axis_index("y")

    def lane_pad(v):
        return jnp.pad(v, ((0, 0), (0, PACK_COLS - v.shape[1])))

    vec_rows = [ins["ssm_conv_w"][0], ins["ssm_conv_b"], lane_pad(ins["ssm_norm_gain"])]
    blk = jnp.concatenate(vec_rows + [jnp.zeros((SUBLANES - 6, PACK_COLS), F32)], axis=0)
    per_chip = gather_small(blk, "gather_vectors")[0::2]
    ngw = ins["ssm_norm_gain"].shape[1]
    full = {
        "ssm_conv_w": jnp.concatenate([per_chip[k, 0:4] for k in range(N_CHIPS)], axis=1)[None],
        "ssm_conv_b": jnp.concatenate([per_chip[k, 4:5] for k in range(N_CHIPS)], axis=1),
        "ssm_norm_gain": jnp.concatenate([per_chip[k, 5:6, :ngw] for k in range(N_CHIPS)], axis=1),
    }

    plan = _Plan(ins, core, chip)
    full.update(plan.full)
    for n in REPLICATED:
        full[n] = ins[n]

    loss, dx, grads = local_step(x, target, full, plan)
    loss = lax.psum(loss, ALL_AXES)
    gshards = plan.join_ready()

    small_shapes = [ins[n].shape for n in REPLICATED]
    vec_shapes = [grads[n].shape for n in EXACT]
    vec_pack = _pack([grads[n] for n in EXACT], F32, SUBLANES)
    small_grads = jnp.concatenate([_pack([grads[n] for n in REPLICATED], F32, SUBLANES), vec_pack], axis=0)
    out_g, out_d, out_m, out_v = {}, {}, {}, {}

    def update(n, g, side=None):
        shp = ins[n].shape
        two = (math.prod(shp[:-1]), shp[-1])
        res = adamw(ins[n].reshape(two), g.reshape(two), ins["m_" + n].reshape(two), ins["v_" + n].reshape(two),
                    f"adamw_{n}", side=side)
        (d2, m2, v2), carried = res if side is not None else (res, None)
        out_g[n], out_d[n], out_m[n], out_v[n] = g, d2.reshape(shp), m2.reshape(shp), v2.reshape(shp)
        return carried

    def stacked(n):
        return jnp.stack([gshards[(n, l)] for l in range(ins[n].shape[0])])

    gath = update("ffn_w_gu", stacked("ffn_w_gu"), small_side(small_grads))[0]
    gshards.update(plan.last_done(update("ffn_w_down", stacked("ffn_w_down"), plan.last_side())))
    packed = [jnp.concatenate([_pack([ins[pre + n] for n in REPLICATED], F32, SUBLANES), jnp.zeros_like(vec_pack)], axis=0)
              for pre in ("", "m_", "v_")]
    res = small_update(gath, *packed, name="small_update")
    nrep = res[0].shape[0] - vec_pack.shape[0]
    small = [dict(zip(REPLICATED, _unpack(r[:nrep].reshape(-1), small_shapes))) for r in res]
    vec_g = dict(zip(EXACT, _unpack(res[0][nrep:].reshape(-1), vec_shapes)))
    for n in REPLICATED:
        out_g[n], out_d[n], out_m[n], out_v[n] = (s[n] for s in small)
    for n in SHARDED:
        if n in EXACT:
            width = ins[n].shape[-1]
            update(n, lax.dynamic_slice_in_dim(vec_g[n], chip * width, width, axis=vec_g[n].ndim - 1))
        elif n not in out_g:
            update(n, stacked(n))
    return (loss, dx[None], *[out_g[n] for n in WEIGHTS], *[out_d[n] for n in WEIGHTS],
            *[out_m[n] for n in WEIGHTS], *[out_v[n] for n in WEIGHTS])


def kernel(x, mix_norm, ffn_norm, sb_w_qkv, sb_q_gain, sb_k_gain, sb_w_o, gm_w_in, gm_b_in, gm_v_gain, gm_w_s, gm_b_s, gm_w_out, ssm_w_in, ssm_conv_w, ssm_conv_b, ssm_dt_bias, ssm_a_log, ssm_d, ssm_norm_gain, ssm_w_out, ffn_w_gu, ffn_w_down, loss_target, m_mix_norm, m_ffn_norm, m_sb_w_qkv, m_sb_q_gain, m_sb_k_gain, m_sb_w_o, m_gm_w_in, m_gm_b_in, m_gm_v_gain, m_gm_w_s, m_gm_b_s, m_gm_w_out, m_ssm_w_in, m_ssm_conv_w, m_ssm_conv_b, m_ssm_dt_bias, m_ssm_a_log, m_ssm_d, m_ssm_norm_gain, m_ssm_w_out, m_ffn_w_gu, m_ffn_w_down, v_mix_norm, v_ffn_norm, v_sb_w_qkv, v_sb_q_gain, v_sb_k_gain, v_sb_w_o, v_gm_w_in, v_gm_b_in, v_gm_v_gain, v_gm_w_s, v_gm_b_s, v_gm_w_out, v_ssm_w_in, v_ssm_conv_w, v_ssm_conv_b, v_ssm_dt_bias, v_ssm_a_log, v_ssm_d, v_ssm_norm_gain, v_ssm_w_out, v_ffn_w_gu, v_ffn_w_down):
    return _step(dict(locals()))
```

```python
import functools
import math

import jax
import jax.numpy as jnp
from jax import lax
from jax.experimental import pallas as pl
from jax.experimental.pallas import tpu as pltpu

F32 = jnp.float32
BF16 = jnp.bfloat16
EPS = 1e-6
LANES = 128
SUBLANES = 8
VMEM_LIMIT = 56 * 1024 * 1024
HEAD = 64
CHUNK = 128
SB_TQ, SB_TK = 256, 256
SSD_SUB = 8
SB_DEAD = -110.0
SB_UNSEEN = -1e30
ADAM_LR, ADAM_B1, ADAM_B2, ADAM_EPS, ADAM_WD, ADAM_STEP = 0.001, 0.9, 0.999, 1e-08, 0.01, 10
MESH_ID = pl.DeviceIdType.MESH
ALL_AXES = ("x", "y", "c")


def _params(sem):
    return pltpu.CompilerParams(dimension_semantics=sem, vmem_limit_bytes=VMEM_LIMIT)


def _pick(n, cands):
    for c in cands:
        if n % c == 0:
            return c
    return n


def _dot(a, b, dims=((1,), (0,))):
    return lax.dot_general(a, b, (dims, ((), ())), preferred_element_type=F32)


def _dot_nt(a, b):
    return _dot(a, b, ((1,), (1,)))


def _dot_tn(a, b):
    return _dot(a, b, ((0,), (0,)))


def _split2(x):
    hi = x.astype(BF16)
    lo = (x - hi.astype(F32)).astype(BF16)
    return hi, lo


def _dot_x2(x, m):
    hi, lo = _split2(x)
    return _dot(hi, m) + _dot(lo, m)


def _dot_x3_left(m, x):
    h1 = x.astype(BF16)
    r1 = x - h1.astype(F32)
    h2 = r1.astype(BF16)
    h3 = (r1 - h2.astype(F32)).astype(BF16)
    return _dot(m, h1) + _dot(m, h2) + _dot(m, h3)


def _sigmoid(x):
    return 1.0 / (1.0 + jnp.exp(-x))


def _softplus(x):
    return jnp.maximum(x, 0.0) + jnp.log(1.0 + jnp.exp(-jnp.abs(x)))


def _colsum(x):
    return jnp.sum(x, axis=0, keepdims=True)


def _rowsum(x):
    return jnp.sum(x, axis=1, keepdims=True)


def _iota2(shape, dim):
    return lax.broadcasted_iota(jnp.int32, shape, dim)


MM_VMEM_BUDGET = 40 * 1024 * 1024
MM_STEP_US = 0.35
MM_HBM_BYTES_PER_US = 3.0e6
MM_VMEM_BYTES_PER_US = 1.5e6
MM_FLOPS_PER_US = 9.0e8
MXU_DIM = 256


def _mm_tiles(m, n, kk, wn, wk, a_bytes, b_bytes, has_add):
    def divisors(total, cands):
        got = [c for c in cands if total % c == 0 and c <= total]
        return got or [total]

    best = None
    for tm in divisors(m, (1024, 512, 256, 128)):
        for tn in divisors(wn, (1024, 768, 1408, 512, 256, 128)):
            for tk in divisors(wk, (4096, 2816, 2048, 1408, 1024, 768, 512, 256, 128)):
                nk = kk // tk
                vmem = 2 * (tm * tk * a_bytes + tk * tn * b_bytes + tm * tn * 4 * (2 if has_add else 1))
                vmem += tm * tn * 4 if nk > 1 else 0
                if vmem > MM_VMEM_BUDGET:
                    continue
                steps = (m // tm) * (n // tn) * nk
                a_reads = 1 if nk == 1 else n // tn
                traffic = m * kk * a_bytes * a_reads + kk * n * b_bytes * (m // tm) + m * n * 4
                fill = min(1.0, tn / MXU_DIM) * min(1.0, tm / MXU_DIM)
                compute = 2.0 * m * n * kk / (MM_FLOPS_PER_US * fill)
                cost = steps * MM_STEP_US + max(compute, traffic / MM_HBM_BYTES_PER_US)
                if nk > 1:
                    cost += steps * tm * tn * 8 / MM_VMEM_BYTES_PER_US
                if best is None or cost < best[0]:
                    best = (cost, tm, tn, tk)
    return best[1:]


def mm(a, b, *, ta=False, tb=False, add=None, bias=None, a_chunks=False, b_chunks=False, out_chunks=False,
       out_dtype=F32, name, side=None):
    wa = None
    if a_chunks:
        m, wa = a.shape[1], a.shape[2]
        kk = a.shape[0] * wa
    elif ta:
        kk, m = a.shape
    else:
        m, kk = a.shape
    nch, wide = 1, None
    if b_chunks:
        nch, rows_b, wide = b.shape
        kb, n = (rows_b, nch * wide) if not tb else (nch * wide, rows_b)
    elif tb:
        n, kb = b.shape
    else:
        kb, n = b.shape
    wide_o = n // N_CHIPS if out_chunks else None
    assert kk == kb, (a.shape, b.shape, ta, tb)
    has_add, has_bias = add is not None, bias is not None
    wk = wide if (wide and tb) else kk
    wn = wide if (wide and not tb) else n
    tm, tn, tk = _mm_tiles(m, n, kk, math.gcd(wn, wide_o) if wide_o else wn, math.gcd(wk, wa) if wa else wk,
                           a.dtype.itemsize, b.dtype.itemsize, has_add)
    nk = kk // tk
    dims = ((0 if ta else 1,), (1 if tb else 0,))

    def kern(*refs):
        a_ref, b_ref = refs[0], refs[1]
        rest = list(refs[2:])
        add_ref = rest.pop(0) if has_add else None
        bias_ref = rest.pop(0) if has_bias else None
        o_ref = rest[0]
        part = _dot(a_ref[...].astype(BF16), b_ref[...].astype(BF16), dims)

        def finish(r):
            if has_add:
                r = r + add_ref[...]
            if has_bias:
                r = r + bias_ref[...]
            o_ref[...] = r.astype(out_dtype)

        if nk == 1:
            finish(part)
        else:
            acc_ref = rest[1]
            k = pl.program_id(2)

            @pl.when(k == 0)
            def _():
                acc_ref[...] = part

            @pl.when((k > 0) & (k < nk - 1))
            def _():
                acc_ref[...] += part

            @pl.when(k == nk - 1)
            def _():
                finish(acc_ref[...] + part)

    if a_chunks:
        per_a = wa // tk
        a_spec = pl.BlockSpec((None, tm, tk), lambda i, j, k: (k // per_a, i, k % per_a))
    elif ta:
        a_spec = pl.BlockSpec((tk, tm), lambda i, j, k: (k, i))
    else:
        a_spec = pl.BlockSpec((tm, tk), lambda i, j, k: (i, k))
    if b_chunks and tb:
        per = wide // tk
        b_spec = pl.BlockSpec((None, tn, tk), lambda i, j, k: (k // per, j, k % per))
    elif b_chunks:
        per = wide // tn
        b_spec = pl.BlockSpec((None, tk, tn), lambda i, j, k: (j // per, k, j % per))
    elif tb:
        b_spec = pl.BlockSpec((tn, tk), lambda i, j, k: (j, k))
    else:
        b_spec = pl.BlockSpec((tk, tn), lambda i, j, k: (k, j))
    if out_chunks:
        per_o = wide_o // tn
        out_spec = pl.BlockSpec((None, tm, tn), lambda i, j, k: (j // per_o, i, j % per_o))
        out_shape = jax.ShapeDtypeStruct((N_CHIPS, m, wide_o), out_dtype)
    else:
        out_spec = pl.BlockSpec((tm, tn), lambda i, j, k: (i, j))
        out_shape = jax.ShapeDtypeStruct((m, n), out_dtype)
    in_specs, args = [a_spec, b_spec], [a, b]
    if has_add:
        in_specs.append(pl.BlockSpec((tm, tn), lambda i, j, k: (i, j)))
        args.append(add)
    if has_bias:
        in_specs.append(pl.BlockSpec((1, tn), lambda i, j, k: (0, j)))
        args.append(bias)
    (out,), side_outs = side_call(
        kern, side,
        name=name,
        grid=(m // tm, n // tn, nk),
        in_specs=in_specs,
        out_specs=[out_spec],
        out_shape=[out_shape],
        scratch_shapes=[pltpu.VMEM((tm, tn), F32)] if nk > 1 else [],
        args=args)
    return out if side is None else (out, side_outs)


def mm_hooked(plan, a, b, *, name, **kw):
    side = plan.side(name)
    if side is None:
        return mm(a, b, name=name, **kw)
    out, side_outs = mm(a, b, name=name, side=side, **kw)
    plan.done(name, side_outs)
    return out


def rowwise(fn, ins, outs, accs=(), *, tr, name):
    rows = [a for a, kind in ins if kind == "row"][0].shape[0]
    tr = min(tr, rows)
    assert rows % tr == 0 and tr % SUBLANES == 0, (rows, tr)
    n = rows // tr
    n_in, n_out = len(ins), len(outs)
    kinds = [kind for _, kind in ins]

    def kern(*refs):
        i = pl.program_id(0)
        vals = []
        for ref, kind in zip(refs[:n_in], kinds):
            v = ref[...]
            if kind == "prev":
                v = v * (i > 0).astype(v.dtype)
            elif kind == "next":
                v = v * (i < n - 1).astype(v.dtype)
            vals.append(v)
        res = fn(*vals)
        for ref, r in zip(refs[n_in:n_in + n_out], res[:n_out]):
            ref[...] = r.astype(ref.dtype)
        if accs:
            acc_refs = refs[n_in + n_out:]

            @pl.when(i == 0)
            def _():
                for ref in acc_refs:
                    ref[...] = jnp.zeros_like(ref)

            for ref, r in zip(acc_refs, res[n_out:]):
                ref[...] += r

    in_specs = []
    for a, kind in ins:
        if kind == "row":
            in_specs.append(pl.BlockSpec((tr, a.shape[1]), lambda i: (i, 0)))
        elif kind == "full":
            in_specs.append(pl.BlockSpec(a.shape, lambda i, nd=a.ndim: (0,) * nd))
        elif kind == "prev":
            in_specs.append(pl.BlockSpec((SUBLANES, a.shape[1]),
                                         lambda i: (jnp.maximum(i * (tr // SUBLANES) - 1, 0), 0)))
        else:
            in_specs.append(pl.BlockSpec((SUBLANES, a.shape[1]),
                                         lambda i: (jnp.minimum((i + 1) * (tr // SUBLANES), rows // SUBLANES - 1), 0)))
    out_specs = [pl.BlockSpec((tr, c), lambda i: (i, 0)) for c, _ in outs]
    out_specs += [pl.BlockSpec((r, c), lambda i: (0, 0)) for r, c in accs]
    out_shape = [jax.ShapeDtypeStruct((rows, c), dt) for c, dt in outs]
    out_shape += [jax.ShapeDtypeStruct((r, c), F32) for r, c in accs]
    res = pl.pallas_call(
        kern,
        name=name,
        grid=(n,),
        in_specs=in_specs,
        out_specs=out_specs,
        out_shape=out_shape,
        compiler_params=_params(("arbitrary",) if accs else ("parallel",)),
    )(*[a for a, _ in ins])
    return res


def rms_fwd(x, g, name):
    def fn(xv, gv):
        r = lax.rsqrt(jnp.mean(xv * xv, axis=1, keepdims=True) + EPS)
        return (xv * r * gv,)

    return rowwise(fn, [(x, "row"), (g, "full")], [(x.shape[1], BF16)], tr=1024, name=name)[0]


def rms_bwd(x, g, dy, dres, name):
    def fn(xv, gv, dyv, drv):
        r = lax.rsqrt(jnp.mean(xv * xv, axis=1, keepdims=True) + EPS)
        xh = xv * r
        dyg = dyv * gv
        dx = drv + r * (dyg - xh * jnp.mean(dyg * xh, axis=1, keepdims=True))
        return dx, dx, _colsum(dyv * xh)

    c = x.shape[1]
    dx, dxb, dg = rowwise(fn, [(x, "row"), (g, "full"), (dy, "row"), (dres, "row")], [(c, F32), (c, BF16)], [(1, c)],
                          tr=512, name=name)
    return (dx, dxb), dg


def ffn_up(h, wgu, name, side=None):
    s, d = h.shape
    nch, _, w = wgu.shape
    half = nch // 2
    tm = _pick(s, (512, 256, 128))

    def kern(h_ref, wg_ref, wu_ref, gu_ref, a_ref):
        hv = h_ref[...]
        g = _dot(hv, wg_ref[...])
        u = _dot(hv, wu_ref[...])
        gu_ref[0] = g.astype(BF16)
        gu_ref[1] = u.astype(BF16)
        a_ref[...] = (g * _sigmoid(g) * u).astype(BF16)

    return side_call(
        kern, side, name=name, grid=(s // tm, half),
        in_specs=[pl.BlockSpec((tm, d), lambda i, j: (i, 0)),
                  pl.BlockSpec((None, d, w), lambda i, j: (j, 0, 0)),
                  pl.BlockSpec((None, d, w), lambda i, j: (j + half, 0, 0))],
        out_specs=[pl.BlockSpec((2, tm, w), lambda i, j: (0, i, j)), pl.BlockSpec((tm, w), lambda i, j: (i, j))],
        out_shape=[jax.ShapeDtypeStruct((2, s, half * w), BF16), jax.ShapeDtypeStruct((s, half * w), BF16)],
        scratch_shapes=[], args=(h, wgu, wgu))


def ffn_dact(dxb, wdown, gu, name):
    s, d = dxb.shape
    hid = wdown.shape[0]
    tm = _pick(s, (512, 256, 128))
    tn = _pick(hid, (1408, 512, 256, 128))

    def kern(dx_ref, w_ref, gu_ref, o_ref):
        da = _dot_nt(dx_ref[...], w_ref[...])
        g, u = gu_ref[0].astype(F32), gu_ref[1].astype(F32)
        sg = _sigmoid(g)
        o_ref[0] = (da * u * sg * (1.0 + g * (1.0 - sg))).astype(BF16)
        o_ref[1] = (da * g * sg).astype(BF16)

    return pl.pallas_call(
        kern, name=name, grid=(s // tm, hid // tn),
        in_specs=[pl.BlockSpec((tm, d), lambda i, j: (i, 0)), pl.BlockSpec((tn, d), lambda i, j: (j, 0)),
                  pl.BlockSpec((2, tm, tn), lambda i, j: (0, i, j))],
        out_specs=pl.BlockSpec((2, tm, tn), lambda i, j: (0, i, j)),
        out_shape=jax.ShapeDtypeStruct((2, s, hid), BF16),
        compiler_params=_params(("parallel", "parallel")),
    )(dxb, wdown, gu)


def loss_and_grad(y, t, name):
    d = y.shape[1]

    def fn(yv, tv):
        e = yv - tv
        part = jnp.sum(_colsum(e * e), axis=1, keepdims=True) * (0.5 / d)
        dy = e * (1.0 / d)
        return dy, dy, jnp.broadcast_to(part, (SUBLANES, LANES))

    dy, dyb, acc = rowwise(fn, [(y, "row"), (t, "row")], [(d, F32), (d, BF16)], [(SUBLANES, LANES)], tr=1024, name=name)
    return acc[0, 0], (dy, dyb)


def adamw(w, g, m, v, name):
    def fn(wv, gv, mv, vv):
        m2 = ADAM_B1 * mv + (1.0 - ADAM_B1) * gv
        v2 = ADAM_B2 * vv + (1.0 - ADAM_B2) * (gv * gv)
        m_hat = m2 / (1.0 - ADAM_B1 ** ADAM_STEP)
        v_hat = v2 / (1.0 - ADAM_B2 ** ADAM_STEP)
        delta = -ADAM_LR * (m_hat / (jnp.sqrt(v_hat) + ADAM_EPS) + ADAM_WD * wv)
        return delta, m2, v2

    rows, c = w.shape
    tr = _pick(rows, (512, 256, 128, 64, 32, 16, 8)) if rows % SUBLANES == 0 else rows
    if rows % SUBLANES:
        return _whole(fn, [w, g, m, v], [(w.shape, F32)] * 3, name=name)
    return rowwise(fn, [(w, "row"), (g, "row"), (m, "row"), (v, "row")], [(c, F32)] * 3, tr=tr, name=name)


def _whole(fn, ins, outs, *, name):
    n_in = len(ins)

    def kern(*refs):
        res = fn(*[r[...] for r in refs[:n_in]])
        for ref, r in zip(refs[n_in:], res):
            ref[...] = r.astype(ref.dtype)

    return pl.pallas_call(
        kern,
        name=name,
        out_shape=[jax.ShapeDtypeStruct(s, dt) for s, dt in outs],
        compiler_params=pltpu.CompilerParams(vmem_limit_bytes=VMEM_LIMIT),
    )(*ins)


def ffn_fwd(x, g, wgu, wdown, tag, plan):
    h = rms_fwd(x, g, f"ffn_rms_{tag}")
    gu, a = _hooked(plan, f"ffn_gu_{tag}", ffn_up, h, wgu)
    xn = mm_hooked(plan, a, wdown, add=x, name=f"ffn_down_{tag}")
    return xn, (x, h, gu, a)


def ffn_bwd(dxn, saved, g, wgu, wdown, tag):
    x, h, gu, a = saved
    dxn, dxb = dxn
    dwdown = mm(a, dxb, ta=True, out_dtype=BF16, name=f"ffn_dwdown_{tag}")
    dgu = ffn_dact(dxb, wdown, gu, f"ffn_dact_{tag}")
    dh = mm(dgu, wgu, tb=True, a_chunks=True, b_chunks=True, name=f"ffn_dh_{tag}")
    dwgu = mm(h, dgu, ta=True, b_chunks=True, out_dtype=BF16, out_chunks=True, name=f"ffn_dwgu_{tag}")
    dx, dg = rms_bwd(x, g, dh, dxn, f"ffn_drms_{tag}")
    return dx, dg, dwgu, dwdown


def _head_blockdiag(c):
    i = jnp.arange(c) // HEAD
    return (i[:, None] == i[None, :]).astype(BF16)


def _head_sums(x, bd):
    return jnp.concatenate([_dot_x2(x[:, g * LANES:(g + 1) * LANES], bd) for g in range(x.shape[1] // LANES)], axis=1)


def qknorm_fwd(qkv, qg, kg, bd, name):
    d = qkv.shape[1] // 3
    scale = 1.0 / math.sqrt(HEAD)

    def fn(v, qgv, kgv, bdv):
        v = v.astype(F32)
        q, k, vv = v[:, :d], v[:, d:2 * d], v[:, 2 * d:]
        rq = lax.rsqrt(_head_sums(q * q, bdv) * (1.0 / HEAD) + EPS)
        rk = lax.rsqrt(_head_sums(k * k, bdv) * (1.0 / HEAD) + EPS)
        return q * rq * qgv * scale, k * rk * kgv, vv

    return rowwise(fn, [(qkv, "row"), (qg, "full"), (kg, "full"), (bd, "full")],
                   [(d, BF16), (d, BF16), (d, BF16)], tr=512, name=name)


def qknorm_bwd(qkv, dqs, dkn, dv, qg, kg, bd, name):
    d = qkv.shape[1] // 3
    scale = 1.0 / math.sqrt(HEAD)

    def one(xv, gv, dyv, bdv):
        r = lax.rsqrt(_head_sums(xv * xv, bdv) * (1.0 / HEAD) + EPS)
        xh = xv * r
        dyg = dyv * gv
        dx = r * (dyg - xh * (_head_sums(dyg * xh, bdv) * (1.0 / HEAD)))
        return dx, _colsum(dyv * xh)

    def fn(v, dqv, dkv, dvv, qgv, kgv, bdv):
        v = v.astype(F32)
        q, k = v[:, :d], v[:, d:2 * d]
        dq, dqg = one(q, qgv, dqv * scale, bdv)
        dk, dkg = one(k, kgv, dkv, bdv)
        return jnp.concatenate([dq, dk, dvv], axis=1), dqg, dkg

    return rowwise(fn, [(qkv, "row"), (dqs, "row"), (dkn, "row"), (dv, "row"), (qg, "full"), (kg, "full"), (bd, "full")],
                   [(3 * d, BF16)], [(1, d), (1, d)], tr=256, name=name)


def _sb_tile(qh, k, mask, tri_gt):
    z = _dot_nt(qh, k)
    sp = jnp.log(1.0 + jnp.exp(-jnp.abs(z)))
    lb = jnp.minimum(z, 0.0) - sp
    l1 = jnp.where(mask, lb - z, 0.0)
    suf = _dot(l1.astype(BF16), tri_gt)
    return lb, l1, suf


def _sb_tri(tk):
    i = jnp.arange(tk)
    return jnp.stack([i[:, None] > i[None, :], i[:, None] < i[None, :]]).astype(BF16)


def _sb_setup(tq, tk):
    row, col = _iota2((tq, tk), 0), _iota2((tq, tk), 1)
    lane = _iota2((1, LANES), 1)
    halves = [(lane < HEAD).astype(BF16), (lane >= HEAD).astype(BF16)]
    lane_q = _iota2((tq, LANES), 1) + jnp.minimum(_iota2((tq, LANES), 0), 0)
    return row, col, halves, lane_q


def sb_attn_fwd(qs, kn, vb, tri, name, side=None):
    s, d = qs.shape
    tq, tk = min(SB_TQ, s), min(SB_TK, s)
    nq = s // tq
    assert s // tk <= LANES and s % tq == 0 and s % tk == 0

    def kern(q_ref, k_ref, v_ref, tri_ref, o_ref, rs_ref, acc_ref):
        i = pl.program_id(1)
        row, col, halves, lane_q = _sb_setup(tq, tk)
        q = q_ref[...]
        qh = [q * hm for hm in halves]
        acc_ref[...] = jnp.zeros_like(acc_ref)
        rs_ref[...] = jnp.full(rs_ref.shape, SB_UNSEEN, F32)
        nkb = (i + 1) * (tq // tk)

        def more(st):
            return (st[0] < nkb) & (st[1] > SB_DEAD)

        def step(st):
            n, r = st[0], list(st[2:])
            kb = nkb - 1 - n
            ks = pl.multiple_of(kb * tk, tk)
            k = k_ref[pl.ds(ks, tk), :]
            v = v_ref[pl.ds(ks, tk), :]
            mask = col < row + (i * tq - kb * tk)
            at_kb = lane_q == kb
            for hh in range(2):
                lb, l1, suf = _sb_tile(qh[hh], k, mask, tri_ref[0])
                w = jnp.where(mask, jnp.exp(lb + suf + r[hh]), 0.0)
                acc_ref[...] += _dot(w.astype(BF16), v * halves[hh])
                rs_ref[hh] = jnp.where(at_kb, r[hh], rs_ref[hh])
                r[hh] = r[hh] + _rowsum(l1)
            return (n + 1, jnp.maximum(jnp.max(r[0]), jnp.max(r[1])), r[0], r[1])

        z1 = jnp.zeros((tq, 1), F32)
        lax.while_loop(more, step, (jnp.int32(0), jnp.float32(0.0), z1, z1))
        o_ref[...] = acc_ref[...].astype(BF16)

    nh2 = d // LANES
    return side_call(
        kern, side,
        name=name,
        grid=(nh2, nq),
        in_specs=[pl.BlockSpec((tq, LANES), lambda h, i: (i, h)),
                  pl.BlockSpec((s, LANES), lambda h, i: (0, h)),
                  pl.BlockSpec((s, LANES), lambda h, i: (0, h)),
                  pl.BlockSpec((2, tk, tk), lambda h, i: (0, 0, 0))],
        out_specs=[pl.BlockSpec((tq, LANES), lambda h, i: (i, h)),
                   pl.BlockSpec((None, 2, tq, LANES), lambda h, i: (h, 0, i, 0))],
        out_shape=[jax.ShapeDtypeStruct((s, d), BF16), jax.ShapeDtypeStruct((nh2, 2, s, LANES), F32)],
        scratch_shapes=[pltpu.VMEM((tq, LANES), F32)],
        args=(qs, kn, vb, tri))


def sb_attn_bwd(qs, kn, vb, rsave, do, tri, name, side=None):
    s, d = qs.shape
    tq, tk = min(SB_TQ, s), min(SB_TK, s)
    nq = s // tq

    def kern(q_ref, k_ref, v_ref, rs_ref, do_ref, tri_ref, dq_ref, dk_ref, dv_ref):
        i = pl.program_id(1)

        @pl.when(i == 0)
        def _():
            dk_ref[...] = jnp.zeros_like(dk_ref)
            dv_ref[...] = jnp.zeros_like(dv_ref)

        row, col, halves, lane_q = _sb_setup(tq, tk)
        q = q_ref[...]
        qh = [q * hm for hm in halves]
        dov = do_ref[...].astype(BF16)
        doh = [dov * hm for hm in halves]
        dq_ref[...] = jnp.zeros_like(dq_ref)
        nkb = (i + 1) * (tq // tk)
        top = jnp.maximum(jnp.max(rs_ref[0], axis=0, keepdims=True), jnp.max(rs_ref[1], axis=0, keepdims=True))
        dead = (top <= SB_DEAD) & (_iota2((1, LANES), 1) < nkb)
        kstart = jnp.minimum(jnp.sum(dead.astype(F32)).astype(jnp.int32), nkb)

        def step(kb, ep):
            ep = list(ep)
            ks = pl.multiple_of(kb * tk, tk)
            k = k_ref[pl.ds(ks, tk), :]
            v = v_ref[pl.ds(ks, tk), :]
            mask = col < row + (i * tq - kb * tk)
            at_kb = lane_q == kb
            for hh in range(2):
                lb, l1, suf = _sb_tile(qh[hh], k, mask, tri_ref[0])
                r = _rowsum(jnp.where(at_kb, rs_ref[hh], 0.0))
                lbm = jnp.where(mask, lb, SB_UNSEEN)
                w = jnp.exp(lbm + suf + r)
                e = _dot_nt(doh[hh], v) * w
                pe = ep[hh] + _dot(e.astype(BF16), tri_ref[1])
                beta = jnp.exp(lbm)
                dz = (e - beta * (e + pe)).astype(BF16)
                dq_ref[...] += _dot(dz, k * halves[hh])
                dk_ref[pl.ds(ks, tk), :] += _dot_tn(dz, qh[hh])
                dv_ref[pl.ds(ks, tk), :] += _dot_tn(w.astype(BF16), doh[hh])
                ep[hh] = ep[hh] + _rowsum(e)
            return tuple(ep)

        z1 = jnp.zeros((tq, 1), F32)
        lax.fori_loop(kstart, nkb, step, (z1, z1))

    nh2 = d // LANES
    return side_call(
        kern, side,
        name=name,
        grid=(nh2, nq),
        in_specs=[pl.BlockSpec((tq, LANES), lambda h, i: (i, h)),
                  pl.BlockSpec((s, LANES), lambda h, i: (0, h)),
                  pl.BlockSpec((s, LANES), lambda h, i: (0, h)),
                  pl.BlockSpec((None, 2, tq, LANES), lambda h, i: (h, 0, i, 0)),
                  pl.BlockSpec((tq, LANES), lambda h, i: (i, h)),
                  pl.BlockSpec((2, tk, tk), lambda h, i: (0, 0, 0))],
        out_specs=[pl.BlockSpec((tq, LANES), lambda h, i: (i, h)),
                   pl.BlockSpec((s, LANES), lambda h, i: (0, h)),
                   pl.BlockSpec((s, LANES), lambda h, i: (0, h))],
        out_shape=[jax.ShapeDtypeStruct((s, d), F32)] * 3,
        scratch_shapes=[],
        args=(qs, kn, vb, rsave, do, tri))


def _hooked(plan, tag, call, *args):
    side = plan.side(tag)
    outs, side_outs = call(*args, tag, side)
    if side is not None:
        plan.done(tag, side_outs)
    return outs


def sb_fwd(x, g, wqkv, qg, kg, wo, bd, tag, plan):
    h = rms_fwd(x, g, f"sb_rms_{tag}")
    qkv = mm(h, wqkv, b_chunks=True, out_dtype=BF16, name=f"sb_qkv_{tag}")
    qs, kn, vb = qknorm_fwd(qkv, qg, kg, bd, f"sb_qknorm_{tag}")
    o, rsave = _hooked(plan, f"sb_attn_{tag}", sb_attn_fwd, qs, kn, vb, _sb_tri(min(SB_TK, x.shape[0])))
    xn = mm(o, wo(), add=x, name=f"sb_out_{tag}")
    return xn, (x, h, qkv, qs, kn, vb, rsave, o)


def sb_bwd(dxn, saved, g, wqkv, qg, kg, wo, bd, tag, plan):
    x, h, qkv, qs, kn, vb, rsave, o = saved
    dxn, dxb = dxn
    do = mm(dxb, wo, tb=True, name=f"sb_do_{tag}")
    dwo = mm(o, dxb, ta=True, out_dtype=BF16, name=f"sb_dwo_{tag}")
    dqs, dkn, dv = _hooked(plan, f"sb_dattn_{tag}", sb_attn_bwd, qs, kn, vb, rsave, do, _sb_tri(min(SB_TK, x.shape[0])))
    dqkv, dqg, dkg = qknorm_bwd(qkv, dqs, dkn, dv, qg, kg, bd, f"sb_dqknorm_{tag}")
    dh = mm(dqkv, wqkv, tb=True, b_chunks=True, name=f"sb_dh_{tag}")
    dwqkv = mm(h, dqkv, ta=True, out_dtype=BF16, out_chunks=True, name=f"sb_dwqkv_{tag}")
    dx, dg = rms_bwd(x, g, dh, dxn, f"sb_drms_{tag}")
    nh = dqg.shape[1] // HEAD
    return dx, dg, dwqkv, dqg.reshape(nh, HEAD).sum(0), dkg.reshape(nh, HEAD).sum(0), dwo


def _gelu(x):
    return 0.5 * x * (1.0 + lax.erf(x * (1.0 / math.sqrt(2.0))))


def _gelu_grad(x):
    return 0.5 * (1.0 + lax.erf(x * (1.0 / math.sqrt(2.0)))) + x * jnp.exp(-0.5 * x * x) * (1.0 / math.sqrt(2.0 * math.pi))


def gm_act_fwd(pre, vg, name):
    half = pre.shape[1] // 2

    def fn(p, vgv):
        p = p.astype(F32)
        u = _gelu(p[:, :half])
        v = _gelu(p[:, half:])
        r = lax.rsqrt(jnp.mean(v * v, axis=1, keepdims=True) + EPS)
        return u, v * r * vgv

    return rowwise(fn, [(pre, "row"), (vg, "full")], [(half, F32), (half, BF16)], tr=256, name=name)


def gm_act_bwd(pre, du, dvn, vg, name):
    half = pre.shape[1] // 2

    def fn(p, duv, dvnv, vgv):
        p = p.astype(F32)
        pu, pv = p[:, :half], p[:, half:]
        v = _gelu(pv)
        r = lax.rsqrt(jnp.mean(v * v, axis=1, keepdims=True) + EPS)
        vh = v * r
        dyg = dvnv * vgv
        dv = r * (dyg - vh * jnp.mean(dyg * vh, axis=1, keepdims=True))
        dpre = jnp.concatenate([duv * _gelu_grad(pu), dv * _gelu_grad(pv)], axis=1)
        return dpre, _colsum(dvnv * vh), _colsum(dpre)

    return rowwise(fn, [(pre, "row"), (du, "row"), (dvn, "row"), (vg, "full")],
                   [(2 * half, BF16)], [(1, half), (1, 2 * half)], tr=256, name=name)


def gm_spatial_fwd(u, vn, wc, bst, name):
    s, c = u.shape
    t = CHUNK
    ng = c // LANES

    def kern(u_ref, v_ref, w_ref, b_ref, o_ref):
        for g in range(ng):
            sl = slice(g * LANES, (g + 1) * LANES)
            mixed = _dot(w_ref[g], v_ref[:, sl]) + b_ref[:, sl]
            o_ref[:, sl] = (u_ref[:, sl] * mixed).astype(BF16)

    return pl.pallas_call(
        kern,
        name=name,
        grid=(s // t,),
        in_specs=[pl.BlockSpec((t, c), lambda i: (i, 0)), pl.BlockSpec((t, c), lambda i: (i, 0)),
                  pl.BlockSpec(wc.shape, lambda i: (0, 0, 0)), pl.BlockSpec(bst.shape, lambda i: (0, 0))],
        out_specs=pl.BlockSpec((t, c), lambda i: (i, 0)),
        out_shape=jax.ShapeDtypeStruct((s, c), BF16),
        compiler_params=_params(("parallel",)),
    )(u, vn, wc, bst)


def gm_spatial_bwd(dgate, u, vn, wc, bst, name):
    s, c = u.shape
    t = CHUNK
    ng = c // LANES

    def kern(dg_ref, u_ref, v_ref, w_ref, b_ref, du_ref, dv_ref, dw_ref, db_ref):
        i = pl.program_id(0)

        @pl.when(i == 0)
        def _():
            dw_ref[...] = jnp.zeros_like(dw_ref)
            db_ref[...] = jnp.zeros_like(db_ref)

        for g in range(ng):
            sl = slice(g * LANES, (g + 1) * LANES)
            vg = v_ref[:, sl]
            dgv = dg_ref[:, sl]
            mixed = _dot(w_ref[g], vg) + b_ref[:, sl]
            du_ref[:, sl] = dgv * mixed
            dmix = dgv * u_ref[:, sl]
            dmb = dmix.astype(BF16)
            dv_ref[:, sl] = _dot_tn(w_ref[g], dmb)
            dw_ref[g] += _dot_nt(dmb, vg)
            db_ref[:, sl] += dmix

    return pl.pallas_call(
        kern,
        name=name,
        grid=(s // t,),
        in_specs=[pl.BlockSpec((t, c), lambda i: (i, 0))] * 3 +
                 [pl.BlockSpec(wc.shape, lambda i: (0, 0, 0)), pl.BlockSpec(bst.shape, lambda i: (0, 0))],
        out_specs=[pl.BlockSpec((t, c), lambda i: (i, 0)), pl.BlockSpec((t, c), lambda i: (i, 0)),
                   pl.BlockSpec(wc.shape, lambda i: (0, 0, 0)), pl.BlockSpec(bst.shape, lambda i: (0, 0))],
        out_shape=[jax.ShapeDtypeStruct((s, c), F32), jax.ShapeDtypeStruct((s, c), F32),
                   jax.ShapeDtypeStruct(wc.shape, F32), jax.ShapeDtypeStruct(bst.shape, F32)],
        compiler_params=_params(("arbitrary",)),
    )(dgate, u, vn, wc, bst)


def gm_fwd(x, g, w_in, b_in, vg, wc, bst, w_out, tag):
    h = rms_fwd(x, g, f"gm_rms_{tag}")
    pre = mm(h, w_in, bias=b_in, b_chunks=True, out_dtype=BF16, name=f"gm_in_{tag}")
    u, vn = gm_act_fwd(pre, vg, f"gm_act_{tag}")
    gate = gm_spatial_fwd(u, vn, wc, bst, f"gm_spatial_{tag}")
    xn = mm(gate, w_out, add=x, name=f"gm_out_{tag}")
    return xn, (x, h, pre, u, vn, gate)


def gm_bwd(dxn, saved, g, w_in, vg, wc, bst, w_out, tag):
    x, h, pre, u, vn, gate = saved
    dxn, dxb = dxn
    dgate = mm(dxb, w_out, tb=True, name=f"gm_dgate_{tag}")
    dwout = mm(gate, dxb, ta=True, out_dtype=BF16, name=f"gm_dwout_{tag}")
    du, dvn, dws, dbst = gm_spatial_bwd(dgate, u, vn, wc, bst, f"gm_dspatial_{tag}")
    dpre, dvg, dbin = gm_act_bwd(pre, du, dvn, vg, f"gm_dact_{tag}")
    dh = mm(dpre, w_in, tb=True, b_chunks=True, name=f"gm_dh_{tag}")
    dwin = mm(h, dpre, ta=True, out_dtype=BF16, out_chunks=True, name=f"gm_dwin_{tag}")
    dx, dg = rms_bwd(x, g, dh, dxn, f"gm_drms_{tag}")
    ng = wc.shape[0]
    dws = jnp.where(jnp.tril(jnp.ones((CHUNK, CHUNK), bool)), dws, 0.0)
    dbs = dbst.reshape(CHUNK, ng, LANES).sum(-1).T
    return dx, dg, dwin, dbin, dvg, dws, dbs, dwout


def _conv_taps(xv, prev):
    cat = jnp.concatenate([prev, xv], axis=0)
    return [pltpu.roll(cat, sh, 0)[SUBLANES:] for sh in (3, 2, 1)] + [xv]


def conv_fwd(xbc, ws, b, d_inner, name):
    c = xbc.shape[1]
    nst = (c - d_inner) // 2

    def fn(xv, prev, w0, w1, w2, w3, bv):
        taps = _conv_taps(xv, prev)
        pre = bv + w0 * taps[0] + w1 * taps[1] + w2 * taps[2] + w3 * taps[3]
        out = pre * _sigmoid(pre)
        return out[:, :d_inner], out[:, d_inner:d_inner + nst], out[:, d_inner + nst:]

    return rowwise(fn, [(xbc, "row"), (xbc, "prev")] + [(w, "full") for w in ws] + [(b, "full")],
                   [(d_inner, F32), (nst, F32), (nst, F32)], tr=256, name=name)


def conv_bwd_pre(xbc, ws, b, dxs_a, dxs_b, db_m, dc_m, name):
    c = xbc.shape[1]

    def fn(xv, prev, w0, w1, w2, w3, bv, da, db2, dbm, dcm):
        taps = _conv_taps(xv, prev)
        pre = bv + w0 * taps[0] + w1 * taps[1] + w2 * taps[2] + w3 * taps[3]
        sg = _sigmoid(pre)
        dout = jnp.concatenate([da + db2, dbm, dcm], axis=1)
        dpre = dout * sg * (1.0 + pre * (1.0 - sg))
        return (dpre,) + tuple(_colsum(dpre * tp) for tp in taps) + (_colsum(dpre),)

    return rowwise(fn, [(xbc, "row"), (xbc, "prev")] + [(w, "full") for w in ws] +
                   [(b, "full"), (dxs_a, "row"), (dxs_b, "row"), (db_m, "row"), (dc_m, "row")],
                   [(c, F32)], [(1, c)] * 5, tr=256, name=name)


def conv_bwd_in(dpre, ws, name):
    c = dpre.shape[1]

    def fn(dv, nxt, w0, w1, w2, w3):
        cat = jnp.concatenate([dv, nxt], axis=0)
        n = cat.shape[0]
        up = [pltpu.roll(cat, n - sh, 0)[:dv.shape[0]] for sh in (1, 2, 3)]
        return (w3 * dv + w2 * up[0] + w1 * up[1] + w0 * up[2],)

    return rowwise(fn, [(dpre, "row"), (dpre, "next")] + [(w, "full") for w in ws], [(c, BF16)], tr=256, name=name)[0]


def ssd_pre(dtr, bias, alog, name):
    def fn(d, bv, al, tri):
        dt = _softplus(d + bv)
        a = dt * (-jnp.exp(al))
        return dt, _dot_x3_left(tri, a)

    tri = jnp.tril(jnp.ones((CHUNK, CHUNK), BF16))
    return rowwise(fn, [(dtr, "row"), (bias, "full"), (alog, "full"), (tri, "full")],
                   [(LANES, F32), (LANES, F32)], tr=CHUNK, name=name)


def _ssd_layouts(v, ngroups, hpg):
    s = v.shape[0]
    col = v[:, :ngroups * hpg].T.reshape(ngroups, hpg, s, 1)
    return jnp.broadcast_to(col, (ngroups, hpg, s, LANES))


def _ssd_rowform(acum, ngroups, hpg):
    s = acum.shape[0]
    nc = s // CHUNK
    a = acum[:, :ngroups * hpg].reshape(nc, CHUNK, ngroups, hpg).transpose(2, 0, 3, 1)
    last = jnp.broadcast_to(a[..., CHUNK - 1:], a.shape)
    return jnp.concatenate([a, last], axis=2)


def ssd_chunk_fwd(xs, bm, cm, col_a, col_dt, rowf, name, side=None):
    s, d_inner = xs.shape
    ln = CHUNK
    nc = s // ln
    nsub = _pick(nc, (SSD_SUB, 2, 1))
    rows = nsub * ln
    ng, hpg = col_a.shape[0], col_a.shape[1]
    gw = d_inner // ng
    assert gw == hpg * HEAD and gw % LANES == 0 and bm.shape[1] == ng * LANES

    def kern(x_ref, b_ref, c_ref, ca_ref, cd_ref, rf_ref, y_ref, hp_ref, h_scr):
        @pl.when(pl.program_id(1) == 0)
        def _():
            h_scr[...] = jnp.zeros_like(h_scr)

        causal = _iota2((ln, ln), 0) >= _iota2((ln, ln), 1)
        lane = _iota2((1, LANES), 1)
        for sc in range(nsub):
            rs = slice(sc * ln, (sc + 1) * ln)
            bb = b_ref[rs, :].astype(BF16)
            cbf = c_ref[rs, :].astype(BF16)
            cb = _dot_nt(cbf, bb)
            ys = [jnp.zeros((ln, LANES), F32) for _ in range(gw // LANES)]
            for r in range(hpg):
                j, hf = divmod(r, LANES // HEAD)
                mh = ((lane >= HEAD * hf) & (lane < HEAD * (hf + 1))).astype(F32)
                ac = ca_ref[r, rs, :]
                ar = rf_ref[sc, pl.ds(r, 1), :]
                aend = rf_ref[sc, pl.ds(4 + r, 1), :]
                dm = jnp.exp(jnp.minimum(ac - ar, 0.0))
                m = jnp.where(causal, cb * dm, 0.0).astype(BF16)
                xdt = x_ref[rs, j * LANES:(j + 1) * LANES] * cd_ref[r, rs, :] * mh
                h = h_scr[r]
                hp_ref[sc, r] = h
                ys[j] = ys[j] + _dot(m, xdt.astype(BF16)) + _dot_nt(cbf, h.astype(BF16)) * jnp.exp(ac)
                dte = jnp.exp(aend - ac)
                h_scr[r] = jnp.exp(aend) * h + _dot_tn((xdt * dte).astype(BF16), bb)
            for j in range(gw // LANES):
                y_ref[rs, j * LANES:(j + 1) * LANES] = ys[j]

    colspec = pl.BlockSpec((None, hpg, rows, LANES), lambda g, c: (g, 0, c, 0))
    return side_call(
        kern, side,
        name=name,
        grid=(ng, nc // nsub),
        in_specs=[pl.BlockSpec((rows, gw), lambda g, c: (c, g)),
                  pl.BlockSpec((rows, LANES), lambda g, c: (c, g)),
                  pl.BlockSpec((rows, LANES), lambda g, c: (c, g)),
                  colspec, colspec,
                  pl.BlockSpec((None, nsub, 8, LANES), lambda g, c: (g, c, 0, 0))],
        out_specs=[pl.BlockSpec((rows, gw), lambda g, c: (c, g)),
                   pl.BlockSpec((None, nsub, hpg, LANES, LANES), lambda g, c: (g, c, 0, 0, 0))],
        out_shape=[jax.ShapeDtypeStruct((s, d_inner), F32),
                   jax.ShapeDtypeStruct((ng, nc, hpg, LANES, LANES), F32)],
        scratch_shapes=[pltpu.VMEM((hpg, LANES, LANES), F32)],
        args=(xs, bm, cm, col_a, col_dt, rowf))


def ssd_chunk_bwd(xs, bm, cm, col_a, col_dt, rowf, hprev, dy, name, side=None):
    s, d_inner = xs.shape
    ln = CHUNK
    nc = s // ln
    nsub = _pick(nc, (SSD_SUB, 2, 1))
    rows = nsub * ln
    ng, hpg = col_a.shape[0], col_a.shape[1]
    gw = d_inner // ng

    def kern(x_ref, b_ref, c_ref, ca_ref, cd_ref, rf_ref, hp_ref, dy_ref,
             dx_ref, db_ref, dc_ref, ddt_ref, da_ref, dh_scr):
        @pl.when(pl.program_id(1) == 0)
        def _():
            dh_scr[...] = jnp.zeros_like(dh_scr)

        row, col = _iota2((ln, ln), 0), _iota2((ln, ln), 1)
        causal = row >= col
        tri_ge = (col >= row).astype(BF16)
        ones = jnp.ones((ln, LANES), BF16)
        lane = _iota2((1, LANES), 1)
        last_row = (_iota2((ln, 1), 0) == ln - 1).astype(F32)
        for sc in reversed(range(nsub)):
            rs = slice(sc * ln, (sc + 1) * ln)
            bb = b_ref[rs, :].astype(BF16)
            cbf = c_ref[rs, :].astype(BF16)
            cb = _dot_nt(cbf, bb)
            dcb = jnp.zeros((ln, ln), F32)
            d_b = jnp.zeros((ln, LANES), F32)
            d_c = jnp.zeros((ln, LANES), F32)
            dxs = [jnp.zeros((ln, LANES), F32) for _ in range(gw // LANES)]
            for r in range(hpg):
                j, hf = divmod(r, LANES // HEAD)
                mh = ((lane >= HEAD * hf) & (lane < HEAD * (hf + 1))).astype(F32)
                ac = ca_ref[r, rs, :]
                dt = cd_ref[r, rs, :]
                ar = rf_ref[sc, pl.ds(r, 1), :]
                aend = rf_ref[sc, pl.ds(4 + r, 1), :]
                dm = jnp.where(causal, jnp.exp(jnp.minimum(ac - ar, 0.0)), 0.0)
                m = cb * dm
                mb = m.astype(BF16)
                xp = x_ref[rs, j * LANES:(j + 1) * LANES]
                xdt = xp * dt * mh
                xdtb = xdt.astype(BF16)
                dyp = dy_ref[rs, j * LANES:(j + 1) * LANES] * mh
                dypb = dyp.astype(BF16)
                h = hp_ref[sc, r]
                hb = h.astype(BF16)
                dh = dh_scr[r]
                dhb = dh.astype(BF16)
                e_in = jnp.exp(ac)
                dte = jnp.exp(aend - ac)
                eend = jnp.exp(aend)
                d_m = _dot_nt(dypb, xdtb)
                dcb = dcb + d_m * dm
                gm = d_m * m
                yoff_pre = _dot_nt(cbf, hb)
                bdh = _dot_nt(bb, dhb)
                dxdt = _dot_tn(mb, dypb) + bdh * dte
                t1 = _rowsum(xdt * bdh) * dte
                gh, gl = _split2(gm)
                dacum = (_rowsum(gm) - (_dot_tn(gh, ones) + _dot_tn(gl, ones))
                         + _rowsum(dyp * yoff_pre) * e_in - t1)
                end_term = _colsum(t1) + eend * jnp.sum(_colsum(dh * h), axis=1, keepdims=True)
                dacum = dacum + last_row * end_term
                da_ref[r, rs, :] = _dot_x3_left(tri_ge, dacum)
                ddt_ref[r, rs, :] = jnp.broadcast_to(_rowsum(dxdt * xp), (ln, LANES))
                dxs[j] = dxs[j] + dxdt * dt
                d_b = d_b + _dot((xdt * dte).astype(BF16), dhb)
                dye = (dyp * e_in).astype(BF16)
                d_c = d_c + _dot(dye, hb)
                dh_scr[r] = eend * dh + _dot_tn(dye, cbf)
            dcbb = dcb.astype(BF16)
            dc_ref[rs, :] = d_c + _dot(dcbb, bb)
            db_ref[rs, :] = d_b + _dot_tn(dcbb, cbf)
            for j in range(gw // LANES):
                dx_ref[rs, j * LANES:(j + 1) * LANES] = dxs[j]

    rev = nc // nsub - 1
    colspec = pl.BlockSpec((None, hpg, rows, LANES), lambda g, c: (g, 0, rev - c, 0))
    return side_call(
        kern, side,
        name=name,
        grid=(ng, nc // nsub),
        in_specs=[pl.BlockSpec((rows, gw), lambda g, c: (rev - c, g)),
                  pl.BlockSpec((rows, LANES), lambda g, c: (rev - c, g)),
                  pl.BlockSpec((rows, LANES), lambda g, c: (rev - c, g)),
                  colspec, colspec,
                  pl.BlockSpec((None, nsub, 8, LANES), lambda g, c: (g, rev - c, 0, 0)),
                  pl.BlockSpec((None, nsub, hpg, LANES, LANES), lambda g, c: (g, rev - c, 0, 0, 0)),
                  pl.BlockSpec((rows, gw), lambda g, c: (rev - c, g))],
        out_specs=[pl.BlockSpec((rows, gw), lambda g, c: (rev - c, g)),
                   pl.BlockSpec((rows, LANES), lambda g, c: (rev - c, g)),
                   pl.BlockSpec((rows, LANES), lambda g, c: (rev - c, g)),
                   colspec, colspec],
        out_shape=[jax.ShapeDtypeStruct((s, d_inner), F32),
                   jax.ShapeDtypeStruct(bm.shape, F32), jax.ShapeDtypeStruct(cm.shape, F32),
                   jax.ShapeDtypeStruct(col_a.shape, F32), jax.ShapeDtypeStruct(col_a.shape, F32)],
        scratch_shapes=[pltpu.VMEM((hpg, LANES, LANES), F32)],
        args=(xs, bm, cm, col_a, col_dt, rowf, hprev, dy))


def gnorm_fwd(y, xs, z, dexp, gain, ngroups, name):
    c = y.shape[1]
    gw = c // ngroups

    def fn(yv, xv, zv, dv, gv):
        yg = (yv + xv * dv) * (zv * _sigmoid(zv))
        outs = []
        for k in range(ngroups):
            t = yg[:, k * gw:(k + 1) * gw]
            outs.append(t * lax.rsqrt(jnp.mean(t * t, axis=1, keepdims=True) + EPS))
        return (jnp.concatenate(outs, axis=1) * gv,)

    return rowwise(fn, [(y, "row"), (xs, "row"), (z, "row"), (dexp, "full"), (gain, "full")], [(c, BF16)], tr=256, name=name)[0]


def gnorm_bwd(dn, y, xs, z, dexp, gain, ngroups, name):
    c = y.shape[1]
    gw = c // ngroups

    def fn(dnv, yv, xv, zv, dv, gv):
        yd = yv + xv * dv
        sg = _sigmoid(zv)
        sz = zv * sg
        yg = yd * sz
        dng = dnv * gv
        dyg, yh = [], []
        for k in range(ngroups):
            sl = slice(k * gw, (k + 1) * gw)
            t = yg[:, sl]
            r = lax.rsqrt(jnp.mean(t * t, axis=1, keepdims=True) + EPS)
            th = t * r
            dyg.append(r * (dng[:, sl] - th * jnp.mean(dng[:, sl] * th, axis=1, keepdims=True)))
            yh.append(th)
        dyg = jnp.concatenate(dyg, axis=1)
        yh = jnp.concatenate(yh, axis=1)
        dyd = dyg * sz
        dz = dyg * yd * (sg * (1.0 + zv * (1.0 - sg)))
        return dyd, dyd * dv, dz, _colsum(dyd * xv), _colsum(dnv * yh)

    return rowwise(fn, [(dn, "row"), (y, "row"), (xs, "row"), (z, "row"), (dexp, "full"), (gain, "full")],
                   [(c, F32), (c, F32), (c, BF16)], [(1, c), (1, c)], tr=256, name=name)


def ssd_post(ddt, da, dt, dtr, bias, alog, name):
    def fn(ddtv, dav, dtv, dtrv, bv, al):
        a_neg = -jnp.exp(al)
        ddtr = (ddtv + dav * a_neg) * _sigmoid(dtrv + bv)
        return ddtr, _colsum(ddtr), _colsum(dav * dtv) * a_neg

    return rowwise(fn, [(ddt, "row"), (da, "row"), (dt, "row"), (dtr, "row"), (bias, "full"), (alog, "full")],
                   [(LANES, BF16)], [(1, LANES), (1, LANES)], tr=512, name=name)


def _from_colform(v, s):
    ng, hpg = v.shape[0], v.shape[1]
    flat = v[..., 0].reshape(ng * hpg, s).T
    return jnp.pad(flat, ((0, 0), (0, LANES - ng * hpg)))


def ssm_fwd(x, g, p, tag, plan):
    ng, hpg, d_inner = p["ng"], p["hpg"], p["d_inner"]
    h = rms_fwd(x, g, f"ssm_rms_{tag}")
    z = mm(h, p["w_z"], name=f"ssm_inz_{tag}")
    xbc = mm(h, p["w_xbc"], name=f"ssm_inx_{tag}")
    dtr = mm(h, p["w_dt"], name=f"ssm_indt_{tag}")
    xs, bm, cm = conv_fwd(xbc, p["conv_w"], p["conv_b"], d_inner, f"ssm_conv_{tag}")
    dt, acum = ssd_pre(dtr, p["dt_bias"], p["a_log"], f"ssm_pre_{tag}")
    col_a, col_dt = _ssd_layouts(acum, ng, hpg), _ssd_layouts(dt, ng, hpg)
    rowf = _ssd_rowform(acum, ng, hpg)
    y, hprev = _hooked(plan, f"ssm_scan_{tag}", ssd_chunk_fwd, xs, bm, cm, col_a, col_dt, rowf)
    n = gnorm_fwd(y, xs, z, p["d_exp"], p["norm_gain"], ng, f"ssm_gnorm_{tag}")
    xn = mm(n, p["w_out"], add=x, name=f"ssm_out_{tag}")
    return xn, (x, h, z, xbc, dtr, xs, bm, cm, dt, col_a, col_dt, rowf, y, hprev, n)


def ssm_bwd(dxn, saved, g, p, tag, plan):
    x, h, z, xbc, dtr, xs, bm, cm, dt, col_a, col_dt, rowf, y, hprev, n = saved
    ng, hpg, d_inner = p["ng"], p["hpg"], p["d_inner"]
    s = x.shape[0]
    dxn, dxb = dxn
    dn = mm(dxb, p["w_out"], tb=True, name=f"ssm_dn_{tag}")
    dwout = mm(n, dxb, ta=True, out_dtype=BF16, name=f"ssm_dwout_{tag}")
    dy, dxs_skip, dz, dd_lane, dgain = gnorm_bwd(dn, y, xs, z, p["d_exp"], p["norm_gain"], ng, f"ssm_dgnorm_{tag}")
    dxs, dbm, dcm, ddt_c, da_c = _hooked(plan, f"ssm_dscan_{tag}", ssd_chunk_bwd, xs, bm, cm, col_a, col_dt, rowf, hprev, dy)
    ddtr, dbias, dalog = ssd_post(_from_colform(ddt_c, s), _from_colform(da_c, s), dt, dtr,
                                  p["dt_bias"], p["a_log"], f"ssm_post_{tag}")
    res = conv_bwd_pre(xbc, p["conv_w"], p["conv_b"], dxs, dxs_skip, dbm, dcm, f"ssm_dconv_{tag}")
    dpre, dconv_w, dconv_b = res[0], jnp.concatenate(res[1:5], axis=0), res[5]
    dxbc = conv_bwd_in(dpre, p["conv_w"], f"ssm_dconvin_{tag}")
    dh = mm(dz, p["w_z"], tb=True, name=f"ssm_dhz_{tag}")
    dh = mm(dxbc, p["w_xbc"], tb=True, add=dh, name=f"ssm_dhx_{tag}")
    dh = mm(ddtr, p["w_dt"], tb=True, add=dh, name=f"ssm_dhdt_{tag}")
    dwz = mm(h, dz, ta=True, out_dtype=BF16, name=f"ssm_dwz_{tag}")
    dwxbc = mm(h, dxbc, ta=True, out_dtype=BF16, name=f"ssm_dwxbc_{tag}")
    dwdt = mm(h, ddtr, ta=True, out_dtype=BF16, name=f"ssm_dwdt_{tag}")
    dx, dg = rms_bwd(x, g, dh, dxn, f"ssm_drms_{tag}")
    nh = ng * hpg
    dwin = jnp.concatenate([dwz, dwxbc, dwdt[:, :nh]], axis=1)
    dd = dd_lane.reshape(nh, HEAD).sum(-1)
    return dx, dg, dict(w_in=dwin, conv_w=dconv_w, conv_b=dconv_b, dt_bias=dbias[0, :nh], a_log=dalog[0, :nh],
                        d=dd, norm_gain=dgain, w_out=dwout)


def local_step(x, target, w, plan):
    d = x.shape[1]
    depth = w["mix_norm"].shape[0]
    bd = _head_blockdiag(LANES)
    tril = jnp.tril(jnp.ones((CHUNK, CHUNK), bool))
    ssm_heads = w["ssm_dt_bias"].shape[1]
    d_inner = w["ssm_norm_gain"].shape[1]
    ng = w["ssm_norm_gain"].shape[1] // 256
    nstate = CHUNK

    def pad_lanes(v):
        return jnp.pad(v, ((0, 0), (0, LANES - v.shape[1])))

    def ssm_params(j):
        w_in = w["ssm_w_in"][j]
        cw = w["ssm_conv_w"][j]
        return dict(ng=ng, hpg=ssm_heads // ng, d_inner=d_inner,
                    w_z=w_in[:, :d_inner], w_xbc=w_in[:, d_inner:d_inner + d_inner + 2 * ng * nstate],
                    w_dt=pad_lanes(w_in[:, 2 * d_inner + 2 * ng * nstate:]),
                    conv_w=[cw[k:k + 1] for k in range(cw.shape[0])], conv_b=w["ssm_conv_b"][j:j + 1],
                    dt_bias=pad_lanes(w["ssm_dt_bias"][j:j + 1]), a_log=pad_lanes(w["ssm_a_log"][j:j + 1]),
                    d_exp=jnp.repeat(w["ssm_d"][j], HEAD)[None, :], norm_gain=w["ssm_norm_gain"][j:j + 1],
                    w_out=w["ssm_w_out"][j])

    def gm_params(j):
        wc = jnp.where(tril, w["gm_w_s"][j], 0.0).astype(BF16)
        bst = jnp.repeat(w["gm_b_s"][j].T, LANES, axis=1)
        return wc, bst

    def sb_gains(j):
        nh = d // HEAD
        return jnp.tile(w["sb_q_gain"][j], nh)[None, :], jnp.tile(w["sb_k_gain"][j], nh)[None, :]

    saved = []
    cur = x
    for i in range(depth):
        kind, j = i % 3, i // 3
        gmix = w["mix_norm"][i:i + 1]
        if kind == 0:
            qg, kg = sb_gains(j)
            cur, sv = sb_fwd(cur, gmix, w["sb_w_qkv"][j], qg, kg, lambda j=j: w["sb_w_o"][j], bd, f"{i}", plan)
        elif kind == 1:
            wc, bst = gm_params(j)
            cur, sv = gm_fwd(cur, gmix, w["gm_w_in"][j], w["gm_b_in"][j:j + 1], w["gm_v_gain"][j:j + 1], wc, bst,
                             w["gm_w_out"][j], f"{i}")
        else:
            cur, sv = ssm_fwd(cur, gmix, ssm_params(j), f"{i}", plan)
        cur, sv2 = ffn_fwd(cur, w["ffn_norm"][i:i + 1], w["ffn_w_gu"][i], w["ffn_w_down"][i], f"{i}", plan)
        saved.append((sv, sv2))

    loss, dcur = loss_and_grad(cur, target, "loss")

    grads = {k: [None] * len(v) for k, v in w.items()}
    for i in reversed(range(depth)):
        kind, j = i % 3, i // 3
        sv, sv2 = saved[i]
        gmix = w["mix_norm"][i:i + 1]
        dcur, dgf, dwgu, dwdown = ffn_bwd(dcur, sv2, w["ffn_norm"][i:i + 1], w["ffn_w_gu"][i], w["ffn_w_down"][i], f"{i}")
        grads["ffn_norm"][i], grads["ffn_w_gu"][i], grads["ffn_w_down"][i] = dgf[0], dwgu, dwdown
        plan.grads_ready({("ffn_w_gu", i): dwgu, ("ffn_w_down", i): dwdown})
        if kind == 0:
            qg, kg = sb_gains(j)
            dcur, dg, dwqkv, dqg, dkg, dwo = sb_bwd(dcur, sv, gmix, w["sb_w_qkv"][j], qg, kg, w["sb_w_o"][j], bd, f"{i}", plan)
            grads["sb_w_qkv"][j], grads["sb_q_gain"][j], grads["sb_k_gain"][j], grads["sb_w_o"][j] = dwqkv, dqg, dkg, dwo
        elif kind == 1:
            wc, bst = gm_params(j)
            dcur, dg, dwin, dbin, dvg, dws, dbs, dwout = gm_bwd(dcur, sv, gmix, w["gm_w_in"][j], w["gm_v_gain"][j:j + 1],
                                                                 wc, bst, w["gm_w_out"][j], f"{i}")
            grads["gm_w_in"][j], grads["gm_b_in"][j], grads["gm_v_gain"][j] = dwin, dbin[0], dvg[0]
            grads["gm_w_s"][j], grads["gm_b_s"][j], grads["gm_w_out"][j] = dws, dbs, dwout
        else:
            dcur, dg, gs = ssm_bwd(dcur, sv, gmix, ssm_params(j), f"{i}", plan)
            grads["ssm_w_in"][j], grads["ssm_conv_w"][j], grads["ssm_conv_b"][j] = gs["w_in"], gs["conv_w"], gs["conv_b"][0]
            grads["ssm_dt_bias"][j], grads["ssm_a_log"][j], grads["ssm_d"][j] = gs["dt_bias"], gs["a_log"], gs["d"]
            grads["ssm_norm_gain"][j], grads["ssm_w_out"][j] = gs["norm_gain"][0], gs["w_out"]
        grads["mix_norm"][i] = dg[0]
        mixer = {0: ("sb_w_qkv", "sb_w_o"), 1: ("gm_w_in", "gm_w_out"), 2: ("ssm_w_in", "ssm_w_out")}[kind]
        plan.grads_ready({(n, j): grads[n][j] for n in mixer})
    grads = {k: (v if k in MATRICES else jnp.stack(v)) for k, v in grads.items()}
    return loss, dcur[0], grads


WEIGHTS = ["mix_norm", "ffn_norm", "sb_w_qkv", "sb_q_gain", "sb_k_gain", "sb_w_o", "gm_w_in", "gm_b_in", "gm_v_gain",
           "gm_w_s", "gm_b_s", "gm_w_out", "ssm_w_in", "ssm_conv_w", "ssm_conv_b", "ssm_dt_bias", "ssm_a_log", "ssm_d",
           "ssm_norm_gain", "ssm_w_out", "ffn_w_gu", "ffn_w_down"]
SHARDED = {"sb_w_qkv": 2, "sb_w_o": 1, "gm_w_in": 2, "gm_w_out": 1, "ssm_w_in": 2, "ssm_conv_w": 2, "ssm_conv_b": 1,
           "ssm_norm_gain": 1, "ssm_w_out": 1, "ffn_w_gu": 2, "ffn_w_down": 1}
EXACT = ("ssm_conv_w", "ssm_conv_b", "ssm_norm_gain")
MATRICES = tuple(n for n in SHARDED if n not in EXACT)
COLUMN_BLOCKS = ("sb_w_qkv", "gm_w_in", "ffn_w_gu")
REPLICATED = [n for n in WEIGHTS if n not in SHARDED]
N_CHIPS = 4
N_DEV = 8
PACK_COLS = 1024


def _pack(pieces, dtype, align):
    flat = jnp.concatenate([p.reshape(-1).astype(dtype) for p in pieces])
    rows = -(-flat.shape[0] // (PACK_COLS * align)) * align
    flat = jnp.pad(flat, (0, rows * PACK_COLS - flat.shape[0]))
    return flat.reshape(rows, PACK_COLS)


def _unpack(flat, shapes):
    out, off = [], 0
    for shp in shapes:
        n = math.prod(shp)
        out.append(flat[off:off + n].reshape(shp))
        off += n
    return out


ANY = pl.BlockSpec(memory_space=pl.ANY)


def _pos():
    return lax.axis_index("x"), lax.axis_index("y"), lax.axis_index("c")


def _remote(src, dst, send, recv, k, to):
    return pltpu.make_async_remote_copy(src_ref=src, dst_ref=dst, send_sem=send.at[k], recv_sem=recv.at[k],
                                        device_id=to, device_id_type=MESH_ID)


def _comm_call(body, name, ins, out_shapes, nsem, aliases=None):
    return pl.pallas_call(
        body, name=name, out_shape=out_shapes,
        in_specs=[ANY] * len(ins), out_specs=[ANY] * len(out_shapes),
        scratch_shapes=[pltpu.SemaphoreType.DMA((nsem,)), pltpu.SemaphoreType.DMA((nsem,))],
        input_output_aliases=aliases or {},
    )(*ins)


def stage_shard(w, chip, name):
    rows, cols = w.shape
    tr = _pick(rows, (256, 352, 128))

    def kern(idx_ref, w_ref, o_ref):
        o_ref[...] = w_ref[...].astype(BF16)

    grid_spec = pltpu.PrefetchScalarGridSpec(
        num_scalar_prefetch=1, grid=(rows // tr,),
        in_specs=[pl.BlockSpec((tr, cols), lambda i, idx: (i, 0))],
        out_specs=pl.BlockSpec((None, tr, cols), lambda i, idx: (idx[0], i, 0)))
    return pl.pallas_call(
        kern, name=name, grid_spec=grid_spec,
        out_shape=jax.ShapeDtypeStruct((N_CHIPS, rows, cols), BF16),
        compiler_params=_params(("parallel",)),
    )(jnp.reshape(chip, (1,)).astype(jnp.int32), w)


class Side:
    def __init__(self, arrays, out_shapes, aliases, nsem, start, finish):
        self.arrays, self.out_shapes, self.aliases, self.nsem = list(arrays), list(out_shapes), aliases, nsem
        self.start, self.finish = start, finish


def run_side(side, name):
    n_in, n_out = len(side.arrays), len(side.out_shapes)

    def body(*refs):
        ins, outs = refs[:n_in], refs[n_in:n_in + n_out]
        send, recv = refs[n_in + n_out:]
        side.start(ins, outs, send, recv)
        side.finish(ins, outs, send, recv)

    return _comm_call(body, name, side.arrays, side.out_shapes, side.nsem, aliases=side.aliases)


def side_call(kern, side, *, name, grid, in_specs, out_specs, out_shape, scratch_shapes, args):
    if side is None:
        res = pl.pallas_call(kern, name=name, grid=grid, in_specs=in_specs, out_specs=out_specs, out_shape=out_shape,
                             scratch_shapes=scratch_shapes,
                             compiler_params=_params(("parallel",) + ("arbitrary",) * (len(grid) - 1)))(*args)
        return list(res), []
    n_in, n_out, n_scr = len(in_specs), len(out_specs), len(scratch_shapes)
    s_in, s_out = len(side.arrays), len(side.out_shapes)

    def body(*refs):
        ins, refs = refs[:n_in], refs[n_in:]
        side_ins, refs = refs[:s_in], refs[s_in:]
        outs, refs = refs[:n_out], refs[n_out:]
        side_outs, refs = refs[:s_out], refs[s_out:]
        scr, (send, recv) = refs[:n_scr], refs[n_scr:]
        first, last = None, None
        for axis, size in enumerate(grid):
            at0, at1 = pl.program_id(axis) == 0, pl.program_id(axis) == size - 1
            first = at0 if first is None else first & at0
            last = at1 if last is None else last & at1

        @pl.when(first)
        def _():
            side.start(side_ins, side_outs, send, recv)

        kern(*ins, *outs, *scr)

        @pl.when(last)
        def _():
            side.finish(side_ins, side_outs, send, recv)

    res = pl.pallas_call(
        body, name=name, grid=grid,
        in_specs=list(in_specs) + [ANY] * s_in, out_specs=list(out_specs) + [ANY] * s_out,
        out_shape=list(out_shape) + side.out_shapes,
        scratch_shapes=list(scratch_shapes) + [pltpu.SemaphoreType.DMA((side.nsem,)), pltpu.SemaphoreType.DMA((side.nsem,))],
        input_output_aliases={n_in + a: n_out + b for a, b in side.aliases.items()},
        compiler_params=_params(("arbitrary",) * len(grid)),
    )(*args, *side.arrays)
    return list(res[:n_out]), list(res[n_out:])


def gather_side(staged):
    n = len(staged)

    def plan(o_refs, send, recv):
        x, y, c = _pos()
        chips = [(1 - x, y), (x, 1 - y), (1 - x, 1 - y)]

        def part(u, chip, cc):
            half = staged[u].shape[1] // 2
            return o_refs[u].at[2 * chip[0] + chip[1], pl.ds(cc * half, half), :]

        first = [_remote(part(u, (x, y), c), part(u, (x, y), c), send, recv, 6 * u + j, (*chip, c))
                 for u in range(n) for j, chip in enumerate(chips)]
        landed = [_remote(part(u, chip, c), part(u, chip, c), send, recv, 6 * u + j, (x, y, c))
                  for u in range(n) for j, chip in enumerate(chips)]
        passed = [_remote(part(u, chip, c), part(u, chip, c), send, recv, 6 * u + 3 + j, (x, y, 1 - c))
                  for u in range(n) for j, chip in enumerate(chips)]
        handed = [_remote(part(u, chip, 1 - c), part(u, chip, 1 - c), send, recv, 6 * u + 3 + j, (x, y, c))
                  for u in range(n) for j, chip in enumerate(chips)]
        return first, landed, passed, handed

    def start(ins, outs, send, recv):
        for cp in plan(outs, send, recv)[0]:
            cp.start()

    def finish(ins, outs, send, recv):
        first, landed, passed, handed = plan(outs, send, recv)
        for got, fw in zip(landed, passed):
            got.wait_recv()
            fw.start()
        for got in handed:
            got.wait_recv()
        for cp in first + passed:
            cp.wait_send()

    outs = [jax.ShapeDtypeStruct(s.shape, s.dtype) for s in staged]
    return Side(staged, outs, {u: u for u in range(n)}, 6 * n, start, finish)


def swap_halves(gps, name):
    n = len(gps)

    def body(*refs):
        g_refs, r_refs = refs[:n], refs[n:2 * n]
        send, recv = refs[2 * n:]
        x, y, c = _pos()
        cps = []
        for u in range(n):
            half = gps[u].shape[1] // 2
            cps.append(_remote(g_refs[u].at[:, pl.ds((1 - c) * half, half), :], r_refs[u], send, recv, u, (x, y, 1 - c)))
        for cp in cps:
            cp.start()
        for cp in cps:
            cp.wait()

    outs = [jax.ShapeDtypeStruct((g.shape[0], g.shape[1] // 2, g.shape[2]), g.dtype) for g in gps]
    return _comm_call(body, name, gps, outs, n)


def scatter_side(parts):
    n = len(parts)

    def plan(p_refs, r_refs, send, recv):
        x, y, c = _pos()
        chips = [(1 - x, y), (x, 1 - y), (1 - x, 1 - y)]
        return [_remote(p_refs[u].at[2 * chip[0] + chip[1]], r_refs[u].at[j], send, recv, 3 * u + j, (*chip, c))
                for u in range(n) for j, chip in enumerate(chips)]

    def start(ins, outs, send, recv):
        for cp in plan(ins, outs, send, recv):
            cp.start()

    def finish(ins, outs, send, recv):
        for cp in plan(ins, outs, send, recv):
            cp.wait()

    outs = [jax.ShapeDtypeStruct((N_CHIPS - 1,) + p.shape[1:], p.dtype) for p in parts]
    return Side(parts, outs, {}, 3 * n, start, finish)


def join_halves(bufs):
    n = len(bufs)

    def body(*refs):
        o_refs = refs[n:2 * n]
        send, recv = refs[2 * n:]
        x, y, c = _pos()

        def rows(u, cc):
            half = bufs[u].shape[0] // 2
            return o_refs[u].at[pl.ds(cc * half, half), :]

        cps = [_remote(rows(u, c), rows(u, c), send, recv, u, (x, y, 1 - c)) for u in range(n)]
        for cp in cps:
            cp.start()
        for u in range(n):
            _remote(rows(u, 1 - c), rows(u, 1 - c), send, recv, u, (x, y, c)).wait_recv()
        for cp in cps:
            cp.wait_send()

    outs = [jax.ShapeDtypeStruct(b.shape, b.dtype) for b in bufs]
    return _comm_call(body, "join_halves", bufs, outs, n, aliases={u: u for u in range(n)})


def gather_small(sg, name):
    rows, cols = sg.shape

    def body(s_ref, o_ref, send, recv, lsem):
        x, y, c = _pos()
        me, sibling = (x, y, c), (x, y, 1 - c)
        chips = [(1 - x, y), (x, 1 - y), (1 - x, 1 - y)]

        def blk(px, py, pc):
            return o_ref.at[4 * px + 2 * py + pc]

        mine = pltpu.make_async_copy(s_ref, blk(*me), lsem)
        mine.start()
        first = [_remote(s_ref, blk(*me), send, recv, 0, sibling)]
        first += [_remote(s_ref, blk(*me), send, recv, 1 + j, (*chip, c)) for j, chip in enumerate(chips)]
        for cp in first:
            cp.start()
        passed = [_remote(blk(*chip, c), blk(*chip, c), send, recv, 4 + j, sibling) for j, chip in enumerate(chips)]
        for j, chip in enumerate(chips):
            _remote(blk(*chip, c), blk(*chip, c), send, recv, 1 + j, me).wait_recv()
            passed[j].start()
        _remote(blk(*sibling), blk(*sibling), send, recv, 0, me).wait_recv()
        for j, chip in enumerate(chips):
            _remote(blk(*chip, 1 - c), blk(*chip, 1 - c), send, recv, 4 + j, me).wait_recv()
        for cp in first + passed:
            cp.wait_send()
        mine.wait()

    return pl.pallas_call(
        body, name=name,
        out_shape=jax.ShapeDtypeStruct((N_DEV, rows, cols), sg.dtype),
        in_specs=[ANY], out_specs=ANY,
        scratch_shapes=[pltpu.SemaphoreType.DMA((N_DEV - 1,)), pltpu.SemaphoreType.DMA((N_DEV - 1,)), pltpu.SemaphoreType.DMA],
    )(sg)


def sum_cores(gp, theirs, core, chip, name):
    nch, rows, cols = gp.shape
    half = rows // 2
    tr = _pick(half, (256, 176, 128, 64))
    nb = half // tr

    def kern(idx_ref, g_ref, t_ref, own_ref, all_ref):
        k = pl.program_id(1)
        s = g_ref[...].astype(F32) + t_ref[...].astype(F32)
        all_ref[...] = s.astype(BF16)

        @pl.when(k == idx_ref[1])
        def _():
            own_ref[...] = s

    grid_spec = pltpu.PrefetchScalarGridSpec(
        num_scalar_prefetch=1, grid=(nb, nch),
        in_specs=[pl.BlockSpec((None, tr, cols), lambda i, k, idx: (k, idx[0] * nb + i, 0)),
                  pl.BlockSpec((None, tr, cols), lambda i, k, idx: (k, i, 0))],
        out_specs=[pl.BlockSpec((tr, cols), lambda i, k, idx: (i, 0)),
                   pl.BlockSpec((None, tr, cols), lambda i, k, idx: (k, i, 0))])
    return pl.pallas_call(
        kern, name=name, grid_spec=grid_spec,
        out_shape=[jax.ShapeDtypeStruct((half, cols), F32), jax.ShapeDtypeStruct((nch, half, cols), BF16)],
        compiler_params=_params(("parallel", "arbitrary")),
    )(jnp.stack([core, chip]).astype(jnp.int32), gp, theirs)


def sum_chips(own, others, core, name):
    half, cols = own.shape
    tr = _pick(half, (256, 176, 128, 64))
    nb = half // tr

    def kern(idx_ref, o_ref, a_ref, b_ref, c_ref, out_ref):
        out_ref[...] = ((o_ref[...] + a_ref[...].astype(F32)) + b_ref[...].astype(F32)) + c_ref[...].astype(F32)

    grid_spec = pltpu.PrefetchScalarGridSpec(
        num_scalar_prefetch=1, grid=(nb,),
        in_specs=[pl.BlockSpec((tr, cols), lambda i, idx: (i, 0))] +
                 [pl.BlockSpec((None, tr, cols), lambda i, idx, j=j: (j, i, 0)) for j in range(N_CHIPS - 1)],
        out_specs=pl.BlockSpec((tr, cols), lambda i, idx: (idx[0] * nb + i, 0)))
    return pl.pallas_call(
        kern, name=name, grid_spec=grid_spec,
        out_shape=jax.ShapeDtypeStruct((2 * half, cols), F32),
        compiler_params=_params(("parallel",)),
    )(jnp.reshape(core, (1,)).astype(jnp.int32), own, others, others, others)


def small_update(gath, w, m, v, name):
    def fn(*vs):
        g = vs[0]
        for t in vs[1:N_DEV]:
            g = g + t
        wv, mv, vv = vs[N_DEV:]
        m2 = ADAM_B1 * mv + (1.0 - ADAM_B1) * g
        v2 = ADAM_B2 * vv + (1.0 - ADAM_B2) * (g * g)
        m_hat = m2 / (1.0 - ADAM_B1 ** ADAM_STEP)
        v_hat = v2 / (1.0 - ADAM_B2 ** ADAM_STEP)
        return g, -ADAM_LR * (m_hat / (jnp.sqrt(v_hat) + ADAM_EPS) + ADAM_WD * wv), m2, v2

    c = w.shape[1]
    ins = [(gath[k], "row") for k in range(N_DEV)] + [(w, "row"), (m, "row"), (v, "row")]
    return rowwise(fn, ins, [(c, F32)] * 4, tr=w.shape[0] // 2, name=name)


_MIX = {0: [("sb_w_qkv", 0), ("sb_w_o", 0)], 1: [("gm_w_in", 0), ("gm_w_out", 0)],
        2: [("ssm_w_in", 0), ("ssm_w_out", 0)], 3: [("sb_w_qkv", 1), ("sb_w_o", 1)]}
_FFN = {i: [("ffn_w_gu", i), ("ffn_w_down", i)] for i in range(4)}
GATHER_FIRST = _MIX[0][:1]
GATHER_AT = {"sb_attn_0": _MIX[0][1:] + _FFN[0] + _FFN[1],
             "ffn_gu_0": _MIX[1], "ffn_down_0": _MIX[2][1:], "ffn_gu_1": _MIX[2][:1], "ffn_down_1": _FFN[2][1:],
             "ssm_scan_2": _FFN[2][:1] + _MIX[3] + _FFN[3][1:], "ffn_gu_2": _FFN[3][:1]}
SCATTER_AT = {"ssm_dscan_2": _FFN[3] + _MIX[3] + _FFN[2], "sb_dattn_0": _MIX[2] + _FFN[1] + _MIX[1] + _FFN[0]}
SCATTER_LAST = _MIX[0]


class _Plan:
    def __init__(self, ins, core, chip):
        self.core, self.chip = core, chip
        self.staged = {(n, l): stage_shard(ins[n][l], chip, f"stage_{n}_{l}")
                       for n in MATRICES for l in range(ins[n].shape[0])}
        self.full = {n: [None] * ins[n].shape[0] for n in MATRICES}
        self.ready = {}
        self.parts = {}
        self.halves = {}
        self.swaps = 0
        self._fill(GATHER_FIRST, run_side(gather_side([self.staged[u] for u in GATHER_FIRST]), "gather_first"))

    def _fill(self, units, gathered):
        for (n, l), g in zip(units, gathered):
            if n in COLUMN_BLOCKS:
                self.full[n][l] = g
            elif n == "ssm_w_in":
                self.full[n][l] = jnp.concatenate([g[k] for k in range(N_CHIPS)], axis=1)
            else:
                self.full[n][l] = g.reshape(-1, g.shape[-1])

    def _prepare(self, units):
        gps = [self.ready[u] for u in units]
        theirs = swap_halves(gps, f"swap_halves_{self.swaps}")
        self.swaps += 1
        for (n, l), g, t in zip(units, gps, theirs):
            self.parts[(n, l)] = sum_cores(g, t, self.core, self.chip, f"sum_cores_{n}_{l}")

    def _reduce(self, units, others):
        for (n, l), other in zip(units, others):
            self.halves[(n, l)] = sum_chips(self.parts[(n, l)][0], other, self.core, f"sum_chips_{n}_{l}")

    def side(self, tag):
        if tag in GATHER_AT:
            return gather_side([self.staged[u] for u in GATHER_AT[tag]])
        if tag in SCATTER_AT:
            self._prepare(SCATTER_AT[tag])
            return scatter_side([self.parts[u][1] for u in SCATTER_AT[tag]])
        return None

    def done(self, tag, results):
        if tag in GATHER_AT:
            self._fill(GATHER_AT[tag], results)
        else:
            self._reduce(SCATTER_AT[tag], results)

    def grads_ready(self, grads):
        for (n, l), g in grads.items():
            if n in COLUMN_BLOCKS:
                self.ready[(n, l)] = g
            elif n == "ssm_w_in":
                self.ready[(n, l)] = jnp.stack(jnp.split(g, N_CHIPS, axis=1))
            else:
                self.ready[(n, l)] = g.reshape(N_CHIPS, -1, g.shape[-1])

    def shard_grads(self):
        self._prepare(SCATTER_LAST)
        self._reduce(SCATTER_LAST, run_side(scatter_side([self.parts[u][1] for u in SCATTER_LAST]), "scatter_last"))
        units = sorted(self.halves)
        return dict(zip(units, join_halves([self.halves[u] for u in units])))


def _step(ins):
    x, target = ins["x"][0], ins["loss_target"][0]
    core = lax.axis_index("c")
    chip = 2 * lax.axis_index("x") + lax.axis_index("y")

    def lane_pad(v):
        return jnp.pad(v, ((0, 0), (0, PACK_COLS - v.shape[1])))

    vec_rows = [ins["ssm_conv_w"][0], ins["ssm_conv_b"], lane_pad(ins["ssm_norm_gain"])]
    blk = jnp.concatenate(vec_rows + [jnp.zeros((SUBLANES - 6, PACK_COLS), F32)], axis=0)
    per_chip = gather_small(blk, "gather_vectors")[0::2]
    ngw = ins["ssm_norm_gain"].shape[1]
    full = {
        "ssm_conv_w": jnp.concatenate([per_chip[k, 0:4] for k in range(N_CHIPS)], axis=1)[None],
        "ssm_conv_b": jnp.concatenate([per_chip[k, 4:5] for k in range(N_CHIPS)], axis=1),
        "ssm_norm_gain": jnp.concatenate([per_chip[k, 5:6, :ngw] for k in range(N_CHIPS)], axis=1),
    }

    plan = _Plan(ins, core, chip)
    full.update(plan.full)
    for n in REPLICATED:
        full[n] = ins[n]

    loss, dx, grads = local_step(x, target, full, plan)
    loss = lax.psum(loss, ALL_AXES)
    gshards = plan.shard_grads()

    small_shapes = [ins[n].shape for n in REPLICATED]
    vec_shapes = [grads[n].shape for n in EXACT]
    vec_pack = _pack([grads[n] for n in EXACT], F32, SUBLANES)
    gath = gather_small(jnp.concatenate([_pack([grads[n] for n in REPLICATED], F32, SUBLANES), vec_pack], axis=0),
                        "gather_small")
    packed = [jnp.concatenate([_pack([ins[pre + n] for n in REPLICATED], F32, SUBLANES), jnp.zeros_like(vec_pack)], axis=0)
              for pre in ("", "m_", "v_")]
    res = small_update(gath, *packed, name="small_update")
    nrep = res[0].shape[0] - vec_pack.shape[0]
    small = [dict(zip(REPLICATED, _unpack(r[:nrep].reshape(-1), small_shapes))) for r in res]
    vec_g = dict(zip(EXACT, _unpack(res[0][nrep:].reshape(-1), vec_shapes)))

    out_g, out_d, out_m, out_v = {}, {}, {}, {}
    for n in REPLICATED:
        out_g[n], out_d[n], out_m[n], out_v[n] = (s[n] for s in small)
    for n in SHARDED:
        shp = ins[n].shape
        if n in EXACT:
            g = lax.dynamic_slice_in_dim(vec_g[n], chip * shp[-1], shp[-1], axis=vec_g[n].ndim - 1)
        else:
            g = jnp.stack([gshards[(n, l)] for l in range(shp[0])])
        two = (math.prod(shp[:-1]), shp[-1])
        d2, m2, v2 = adamw(ins[n].reshape(two), g.reshape(two), ins["m_" + n].reshape(two),
                           ins["v_" + n].reshape(two), f"adamw_{n}")
        out_g[n], out_d[n], out_m[n], out_v[n] = g, d2.reshape(shp), m2.reshape(shp), v2.reshape(shp)
    return (loss, dx[None], *[out_g[n] for n in WEIGHTS], *[out_d[n] for n in WEIGHTS],
            *[out_m[n] for n in WEIGHTS], *[out_v[n] for n in WEIGHTS])


def kernel(x, mix_norm, ffn_norm, sb_w_qkv, sb_q_gain, sb_k_gain, sb_w_o, gm_w_in, gm_b_in, gm_v_gain, gm_w_s, gm_b_s, gm_w_out, ssm_w_in, ssm_conv_w, ssm_conv_b, ssm_dt_bias, ssm_a_log, ssm_d, ssm_norm_gain, ssm_w_out, ffn_w_gu, ffn_w_down, loss_target, m_mix_norm, m_ffn_norm, m_sb_w_qkv, m_sb_q_gain, m_sb_k_gain, m_sb_w_o, m_gm_w_in, m_gm_b_in, m_gm_v_gain, m_gm_w_s, m_gm_b_s, m_gm_w_out, m_ssm_w_in, m_ssm_conv_w, m_ssm_conv_b, m_ssm_dt_bias, m_ssm_a_log, m_ssm_d, m_ssm_norm_gain, m_ssm_w_out, m_ffn_w_gu, m_ffn_w_down, v_mix_norm, v_ffn_norm, v_sb_w_qkv, v_sb_q_gain, v_sb_k_gain, v_sb_w_o, v_gm_w_in, v_gm_b_in, v_gm_v_gain, v_gm_w_s, v_gm_b_s, v_gm_w_out, v_ssm_w_in, v_ssm_conv_w, v_ssm_conv_b, v_ssm_dt_bias, v_ssm_a_log, v_ssm_d, v_ssm_norm_gain, v_ssm_w_out, v_ffn_w_gu, v_ffn_w_down):
    return _step(dict(locals()))
```

```python
import functools
import math

import jax
import jax.numpy as jnp
from jax import lax
from jax.experimental import pallas as pl
from jax.experimental.pallas import tpu as pltpu

F32 = jnp.float32
BF16 = jnp.bfloat16
EPS = 1e-6
LANES = 128
SUBLANES = 8
VMEM_LIMIT = 56 * 1024 * 1024
HEAD = 64
CHUNK = 128
SB_TQ, SB_TK = 256, 256
SSD_SUB = 8
SB_DEAD = -110.0
SB_UNSEEN = -1e30
ADAM_LR, ADAM_B1, ADAM_B2, ADAM_EPS, ADAM_WD, ADAM_STEP = 0.001, 0.9, 0.999, 1e-08, 0.01, 10
MESH_ID = pl.DeviceIdType.MESH
ALL_AXES = ("x", "y", "c")


def _params(sem):
    return pltpu.CompilerParams(dimension_semantics=sem, vmem_limit_bytes=VMEM_LIMIT)


def _pick(n, cands):
    for c in cands:
        if n % c == 0:
            return c
    return n


def _dot(a, b, dims=((1,), (0,))):
    return lax.dot_general(a, b, (dims, ((), ())), preferred_element_type=F32)


def _dot_nt(a, b):
    return _dot(a, b, ((1,), (1,)))


def _dot_tn(a, b):
    return _dot(a, b, ((0,), (0,)))


def _split2(x):
    hi = x.astype(BF16)
    lo = (x - hi.astype(F32)).astype(BF16)
    return hi, lo


def _dot_x2(x, m):
    hi, lo = _split2(x)
    return _dot(hi, m) + _dot(lo, m)


def _dot_x3_left(m, x):
    h1 = x.astype(BF16)
    r1 = x - h1.astype(F32)
    h2 = r1.astype(BF16)
    h3 = (r1 - h2.astype(F32)).astype(BF16)
    return _dot(m, h1) + _dot(m, h2) + _dot(m, h3)


def _sigmoid(x):
    return 1.0 / (1.0 + jnp.exp(-x))


def _softplus(x):
    return jnp.maximum(x, 0.0) + jnp.log(1.0 + jnp.exp(-jnp.abs(x)))


def _colsum(x):
    return jnp.sum(x, axis=0, keepdims=True)


def _rowsum(x):
    return jnp.sum(x, axis=1, keepdims=True)


def _iota2(shape, dim):
    return lax.broadcasted_iota(jnp.int32, shape, dim)


MM_VMEM_BUDGET = 40 * 1024 * 1024
MM_STEP_US = 0.35
MM_HBM_BYTES_PER_US = 3.0e6
MM_VMEM_BYTES_PER_US = 1.5e6
MM_FLOPS_PER_US = 9.0e8
MXU_DIM = 256


def _mm_tiles(m, n, kk, wn, wk, a_bytes, b_bytes, has_add):
    def divisors(total, cands):
        got = [c for c in cands if total % c == 0 and c <= total]
        return got or [total]

    best = None
    for tm in divisors(m, (1024, 512, 256, 128)):
        for tn in divisors(wn, (1024, 768, 1408, 512, 256, 128)):
            for tk in divisors(wk, (4096, 2816, 2048, 1408, 1024, 768, 512, 256, 128)):
                nk = kk // tk
                vmem = 2 * (tm * tk * a_bytes + tk * tn * b_bytes + tm * tn * 4 * (2 if has_add else 1))
                vmem += tm * tn * 4 if nk > 1 else 0
                if vmem > MM_VMEM_BUDGET:
                    continue
                steps = (m // tm) * (n // tn) * nk
                a_reads = 1 if nk == 1 else n // tn
                traffic = m * kk * a_bytes * a_reads + kk * n * b_bytes * (m // tm) + m * n * 4
                fill = min(1.0, tn / MXU_DIM) * min(1.0, tm / MXU_DIM)
                compute = 2.0 * m * n * kk / (MM_FLOPS_PER_US * fill)
                cost = steps * MM_STEP_US + max(compute, traffic / MM_HBM_BYTES_PER_US)
                if nk > 1:
                    cost += steps * tm * tn * 8 / MM_VMEM_BYTES_PER_US
                if best is None or cost < best[0]:
                    best = (cost, tm, tn, tk)
    return best[1:]


def mm(a, b, *, ta=False, tb=False, add=None, bias=None, a_chunks=False, b_chunks=False, out_chunks=False,
       out_dtype=F32, name, side=None):
    wa = None
    if a_chunks:
        m, wa = a.shape[1], a.shape[2]
        kk = a.shape[0] * wa
    elif ta:
        kk, m = a.shape
    else:
        m, kk = a.shape
    nch, wide = 1, None
    if b_chunks:
        nch, rows_b, wide = b.shape
        kb, n = (rows_b, nch * wide) if not tb else (nch * wide, rows_b)
    elif tb:
        n, kb = b.shape
    else:
        kb, n = b.shape
    wide_o = n // N_CHIPS if out_chunks else None
    assert kk == kb, (a.shape, b.shape, ta, tb)
    has_add, has_bias = add is not None, bias is not None
    wk = wide if (wide and tb) else kk
    wn = wide if (wide and not tb) else n
    tm, tn, tk = _mm_tiles(m, n, kk, math.gcd(wn, wide_o) if wide_o else wn, math.gcd(wk, wa) if wa else wk,
                           a.dtype.itemsize, b.dtype.itemsize, has_add)
    nk = kk // tk
    dims = ((0 if ta else 1,), (1 if tb else 0,))

    def kern(*refs):
        a_ref, b_ref = refs[0], refs[1]
        rest = list(refs[2:])
        add_ref = rest.pop(0) if has_add else None
        bias_ref = rest.pop(0) if has_bias else None
        o_ref = rest[0]
        part = _dot(a_ref[...].astype(BF16), b_ref[...].astype(BF16), dims)

        def finish(r):
            if has_add:
                r = r + add_ref[...]
            if has_bias:
                r = r + bias_ref[...]
            o_ref[...] = r.astype(out_dtype)

        if nk == 1:
            finish(part)
        else:
            acc_ref = rest[1]
            k = pl.program_id(2)

            @pl.when(k == 0)
            def _():
                acc_ref[...] = part

            @pl.when((k > 0) & (k < nk - 1))
            def _():
                acc_ref[...] += part

            @pl.when(k == nk - 1)
            def _():
                finish(acc_ref[...] + part)

    if a_chunks:
        per_a = wa // tk
        a_spec = pl.BlockSpec((None, tm, tk), lambda i, j, k: (k // per_a, i, k % per_a))
    elif ta:
        a_spec = pl.BlockSpec((tk, tm), lambda i, j, k: (k, i))
    else:
        a_spec = pl.BlockSpec((tm, tk), lambda i, j, k: (i, k))
    if b_chunks and tb:
        per = wide // tk
        b_spec = pl.BlockSpec((None, tn, tk), lambda i, j, k: (k // per, j, k % per))
    elif b_chunks:
        per = wide // tn
        b_spec = pl.BlockSpec((None, tk, tn), lambda i, j, k: (j // per, k, j % per))
    elif tb:
        b_spec = pl.BlockSpec((tn, tk), lambda i, j, k: (j, k))
    else:
        b_spec = pl.BlockSpec((tk, tn), lambda i, j, k: (k, j))
    if out_chunks:
        per_o = wide_o // tn
        out_spec = pl.BlockSpec((None, tm, tn), lambda i, j, k: (j // per_o, i, j % per_o))
        out_shape = jax.ShapeDtypeStruct((N_CHIPS, m, wide_o), out_dtype)
    else:
        out_spec = pl.BlockSpec((tm, tn), lambda i, j, k: (i, j))
        out_shape = jax.ShapeDtypeStruct((m, n), out_dtype)
    in_specs, args = [a_spec, b_spec], [a, b]
    if has_add:
        in_specs.append(pl.BlockSpec((tm, tn), lambda i, j, k: (i, j)))
        args.append(add)
    if has_bias:
        in_specs.append(pl.BlockSpec((1, tn), lambda i, j, k: (0, j)))
        args.append(bias)
    (out,), side_outs = side_call(
        kern, side,
        name=name,
        grid=(m // tm, n // tn, nk),
        in_specs=in_specs,
        out_specs=[out_spec],
        out_shape=[out_shape],
        scratch_shapes=[pltpu.VMEM((tm, tn), F32)] if nk > 1 else [],
        args=args)
    return out if side is None else (out, side_outs)


def mm_hooked(plan, a, b, *, name, **kw):
    side = plan.side(name)
    if side is None:
        return mm(a, b, name=name, **kw)
    out, side_outs = mm(a, b, name=name, side=side, **kw)
    plan.done(name, side_outs)
    return out


def rowwise(fn, ins, outs, accs=(), *, tr, name):
    rows = [a for a, kind in ins if kind == "row"][0].shape[0]
    tr = min(tr, rows)
    assert rows % tr == 0 and tr % SUBLANES == 0, (rows, tr)
    n = rows // tr
    n_in, n_out = len(ins), len(outs)
    kinds = [kind for _, kind in ins]

    def kern(*refs):
        i = pl.program_id(0)
        vals = []
        for ref, kind in zip(refs[:n_in], kinds):
            v = ref[...]
            if kind == "prev":
                v = v * (i > 0).astype(v.dtype)
            elif kind == "next":
                v = v * (i < n - 1).astype(v.dtype)
            vals.append(v)
        res = fn(*vals)
        for ref, r in zip(refs[n_in:n_in + n_out], res[:n_out]):
            ref[...] = r.astype(ref.dtype)
        if accs:
            acc_refs = refs[n_in + n_out:]

            @pl.when(i == 0)
            def _():
                for ref in acc_refs:
                    ref[...] = jnp.zeros_like(ref)

            for ref, r in zip(acc_refs, res[n_out:]):
                ref[...] += r

    in_specs = []
    for a, kind in ins:
        if kind == "row":
            in_specs.append(pl.BlockSpec((tr, a.shape[1]), lambda i: (i, 0)))
        elif kind == "full":
            in_specs.append(pl.BlockSpec(a.shape, lambda i, nd=a.ndim: (0,) * nd))
        elif kind == "prev":
            in_specs.append(pl.BlockSpec((SUBLANES, a.shape[1]),
                                         lambda i: (jnp.maximum(i * (tr // SUBLANES) - 1, 0), 0)))
        else:
            in_specs.append(pl.BlockSpec((SUBLANES, a.shape[1]),
                                         lambda i: (jnp.minimum((i + 1) * (tr // SUBLANES), rows // SUBLANES - 1), 0)))
    out_specs = [pl.BlockSpec((tr, c), lambda i: (i, 0)) for c, _ in outs]
    out_specs += [pl.BlockSpec((r, c), lambda i: (0, 0)) for r, c in accs]
    out_shape = [jax.ShapeDtypeStruct((rows, c), dt) for c, dt in outs]
    out_shape += [jax.ShapeDtypeStruct((r, c), F32) for r, c in accs]
    res = pl.pallas_call(
        kern,
        name=name,
        grid=(n,),
        in_specs=in_specs,
        out_specs=out_specs,
        out_shape=out_shape,
        compiler_params=_params(("arbitrary",) if accs else ("parallel",)),
    )(*[a for a, _ in ins])
    return res


def rms_fwd(x, g, name):
    def fn(xv, gv):
        r = lax.rsqrt(jnp.mean(xv * xv, axis=1, keepdims=True) + EPS)
        return (xv * r * gv,)

    return rowwise(fn, [(x, "row"), (g, "full")], [(x.shape[1], BF16)], tr=1024, name=name)[0]


def rms_bwd(x, g, dy, dres, name):
    def fn(xv, gv, dyv, drv):
        r = lax.rsqrt(jnp.mean(xv * xv, axis=1, keepdims=True) + EPS)
        xh = xv * r
        dyg = dyv * gv
        dx = drv + r * (dyg - xh * jnp.mean(dyg * xh, axis=1, keepdims=True))
        return dx, dx, _colsum(dyv * xh)

    c = x.shape[1]
    dx, dxb, dg = rowwise(fn, [(x, "row"), (g, "full"), (dy, "row"), (dres, "row")], [(c, F32), (c, BF16)], [(1, c)],
                          tr=512, name=name)
    return (dx, dxb), dg


def ffn_up(h, wgu, name, side=None):
    s, d = h.shape
    nch, _, w = wgu.shape
    half = nch // 2
    tm = _pick(s, (512, 256, 128))

    def kern(h_ref, wg_ref, wu_ref, gu_ref, a_ref):
        hv = h_ref[...]
        g = _dot(hv, wg_ref[...])
        u = _dot(hv, wu_ref[...])
        gu_ref[0] = g.astype(BF16)
        gu_ref[1] = u.astype(BF16)
        a_ref[...] = (g * _sigmoid(g) * u).astype(BF16)

    return side_call(
        kern, side, name=name, grid=(s // tm, half),
        in_specs=[pl.BlockSpec((tm, d), lambda i, j: (i, 0)),
                  pl.BlockSpec((None, d, w), lambda i, j: (j, 0, 0)),
                  pl.BlockSpec((None, d, w), lambda i, j: (j + half, 0, 0))],
        out_specs=[pl.BlockSpec((2, tm, w), lambda i, j: (0, i, j)), pl.BlockSpec((tm, w), lambda i, j: (i, j))],
        out_shape=[jax.ShapeDtypeStruct((2, s, half * w), BF16), jax.ShapeDtypeStruct((s, half * w), BF16)],
        scratch_shapes=[], args=(h, wgu, wgu))


def ffn_dact(dxb, wdown, gu, name):
    s, d = dxb.shape
    hid = wdown.shape[0]
    tm = _pick(s, (512, 256, 128))
    tn = _pick(hid, (1408, 512, 256, 128))

    def kern(dx_ref, w_ref, gu_ref, o_ref):
        da = _dot_nt(dx_ref[...], w_ref[...])
        g, u = gu_ref[0].astype(F32), gu_ref[1].astype(F32)
        sg = _sigmoid(g)
        o_ref[0] = (da * u * sg * (1.0 + g * (1.0 - sg))).astype(BF16)
        o_ref[1] = (da * g * sg).astype(BF16)

    return pl.pallas_call(
        kern, name=name, grid=(s // tm, hid // tn),
        in_specs=[pl.BlockSpec((tm, d), lambda i, j: (i, 0)), pl.BlockSpec((tn, d), lambda i, j: (j, 0)),
                  pl.BlockSpec((2, tm, tn), lambda i, j: (0, i, j))],
        out_specs=pl.BlockSpec((2, tm, tn), lambda i, j: (0, i, j)),
        out_shape=jax.ShapeDtypeStruct((2, s, hid), BF16),
        compiler_params=_params(("parallel", "parallel")),
    )(dxb, wdown, gu)


def loss_and_grad(y, t, name):
    d = y.shape[1]

    def fn(yv, tv):
        e = yv - tv
        part = jnp.sum(_colsum(e * e), axis=1, keepdims=True) * (0.5 / d)
        dy = e * (1.0 / d)
        return dy, dy, jnp.broadcast_to(part, (SUBLANES, LANES))

    dy, dyb, acc = rowwise(fn, [(y, "row"), (t, "row")], [(d, F32), (d, BF16)], [(SUBLANES, LANES)], tr=1024, name=name)
    return acc[0, 0], (dy, dyb)


def adamw(w, g, m, v, name):
    def fn(wv, gv, mv, vv):
        m2 = ADAM_B1 * mv + (1.0 - ADAM_B1) * gv
        v2 = ADAM_B2 * vv + (1.0 - ADAM_B2) * (gv * gv)
        m_hat = m2 / (1.0 - ADAM_B1 ** ADAM_STEP)
        v_hat = v2 / (1.0 - ADAM_B2 ** ADAM_STEP)
        delta = -ADAM_LR * (m_hat / (jnp.sqrt(v_hat) + ADAM_EPS) + ADAM_WD * wv)
        return delta, m2, v2

    rows, c = w.shape
    tr = _pick(rows, (512, 256, 128, 64, 32, 16, 8)) if rows % SUBLANES == 0 else rows
    if rows % SUBLANES:
        return _whole(fn, [w, g, m, v], [(w.shape, F32)] * 3, name=name)
    return rowwise(fn, [(w, "row"), (g, "row"), (m, "row"), (v, "row")], [(c, F32)] * 3, tr=tr, name=name)


def _whole(fn, ins, outs, *, name):
    n_in = len(ins)

    def kern(*refs):
        res = fn(*[r[...] for r in refs[:n_in]])
        for ref, r in zip(refs[n_in:], res):
            ref[...] = r.astype(ref.dtype)

    return pl.pallas_call(
        kern,
        name=name,
        out_shape=[jax.ShapeDtypeStruct(s, dt) for s, dt in outs],
        compiler_params=pltpu.CompilerParams(vmem_limit_bytes=VMEM_LIMIT),
    )(*ins)


def ffn_fwd(x, g, wgu, wdown, tag, plan):
    h = rms_fwd(x, g, f"ffn_rms_{tag}")
    gu, a = _hooked(plan, f"ffn_gu_{tag}", ffn_up, h, wgu)
    xn = mm_hooked(plan, a, wdown, add=x, name=f"ffn_down_{tag}")
    return xn, (x, h, gu, a)


def ffn_bwd(dxn, saved, g, wgu, wdown, tag):
    x, h, gu, a = saved
    dxn, dxb = dxn
    dwdown = mm(a, dxb, ta=True, out_dtype=BF16, name=f"ffn_dwdown_{tag}")
    dgu = ffn_dact(dxb, wdown, gu, f"ffn_dact_{tag}")
    dh = mm(dgu, wgu, tb=True, a_chunks=True, b_chunks=True, name=f"ffn_dh_{tag}")
    dwgu = mm(h, dgu, ta=True, b_chunks=True, out_dtype=BF16, out_chunks=True, name=f"ffn_dwgu_{tag}")
    dx, dg = rms_bwd(x, g, dh, dxn, f"ffn_drms_{tag}")
    return dx, dg, dwgu, dwdown


def _head_blockdiag(c):
    i = jnp.arange(c) // HEAD
    return (i[:, None] == i[None, :]).astype(BF16)


def _head_sums(x, bd):
    return jnp.concatenate([_dot_x2(x[:, g * LANES:(g + 1) * LANES], bd) for g in range(x.shape[1] // LANES)], axis=1)


def qknorm_fwd(qkv, qg, kg, bd, name):
    d = qkv.shape[1] // 3
    scale = 1.0 / math.sqrt(HEAD)

    def fn(v, qgv, kgv, bdv):
        v = v.astype(F32)
        q, k, vv = v[:, :d], v[:, d:2 * d], v[:, 2 * d:]
        rq = lax.rsqrt(_head_sums(q * q, bdv) * (1.0 / HEAD) + EPS)
        rk = lax.rsqrt(_head_sums(k * k, bdv) * (1.0 / HEAD) + EPS)
        return q * rq * qgv * scale, k * rk * kgv, vv

    return rowwise(fn, [(qkv, "row"), (qg, "full"), (kg, "full"), (bd, "full")],
                   [(d, BF16), (d, BF16), (d, BF16)], tr=512, name=name)


def qknorm_bwd(qkv, dqs, dkn, dv, qg, kg, bd, name):
    d = qkv.shape[1] // 3
    scale = 1.0 / math.sqrt(HEAD)

    def one(xv, gv, dyv, bdv):
        r = lax.rsqrt(_head_sums(xv * xv, bdv) * (1.0 / HEAD) + EPS)
        xh = xv * r
        dyg = dyv * gv
        dx = r * (dyg - xh * (_head_sums(dyg * xh, bdv) * (1.0 / HEAD)))
        return dx, _colsum(dyv * xh)

    def fn(v, dqv, dkv, dvv, qgv, kgv, bdv):
        v = v.astype(F32)
        q, k = v[:, :d], v[:, d:2 * d]
        dq, dqg = one(q, qgv, dqv * scale, bdv)
        dk, dkg = one(k, kgv, dkv, bdv)
        return jnp.concatenate([dq, dk, dvv], axis=1), dqg, dkg

    return rowwise(fn, [(qkv, "row"), (dqs, "row"), (dkn, "row"), (dv, "row"), (qg, "full"), (kg, "full"), (bd, "full")],
                   [(3 * d, BF16)], [(1, d), (1, d)], tr=512, name=name)


def _sb_tile(qh, k, mask, tri_gt):
    z = _dot_nt(qh, k)
    sp = jnp.log(1.0 + jnp.exp(-jnp.abs(z)))
    lb = jnp.minimum(z, 0.0) - sp
    l1 = jnp.where(mask, lb - z, 0.0)
    suf = _dot(l1.astype(BF16), tri_gt)
    return lb, l1, suf


def _sb_tri(tk):
    i = jnp.arange(tk)
    return jnp.stack([i[:, None] > i[None, :], i[:, None] < i[None, :]]).astype(BF16)


def _sb_setup(tq, tk):
    row, col = _iota2((tq, tk), 0), _iota2((tq, tk), 1)
    lane = _iota2((1, LANES), 1)
    halves = [(lane < HEAD).astype(BF16), (lane >= HEAD).astype(BF16)]
    lane_q = _iota2((tq, LANES), 1) + jnp.minimum(_iota2((tq, LANES), 0), 0)
    return row, col, halves, lane_q


def sb_attn_fwd(qs, kn, vb, tri, name, side=None):
    s, d = qs.shape
    tq, tk = min(SB_TQ, s), min(SB_TK, s)
    nq = s // tq
    assert s // tk <= LANES and s % tq == 0 and s % tk == 0

    def kern(q_ref, k_ref, v_ref, tri_ref, o_ref, rs_ref, acc_ref):
        i = pl.program_id(1)
        row, col, halves, lane_q = _sb_setup(tq, tk)
        q = q_ref[...]
        qh = [q * hm for hm in halves]
        acc_ref[...] = jnp.zeros_like(acc_ref)
        rs_ref[...] = jnp.full(rs_ref.shape, SB_UNSEEN, F32)
        nkb = (i + 1) * (tq // tk)

        def more(st):
            return (st[0] < nkb) & (st[1] > SB_DEAD)

        def step(st):
            n, r = st[0], list(st[2:])
            kb = nkb - 1 - n
            ks = pl.multiple_of(kb * tk, tk)
            k = k_ref[pl.ds(ks, tk), :]
            v = v_ref[pl.ds(ks, tk), :]
            mask = col < row + (i * tq - kb * tk)
            at_kb = lane_q == kb
            for hh in range(2):
                lb, l1, suf = _sb_tile(qh[hh], k, mask, tri_ref[0])
                w = jnp.where(mask, jnp.exp(lb + suf + r[hh]), 0.0)
                acc_ref[...] += _dot(w.astype(BF16), v * halves[hh])
                rs_ref[hh] = jnp.where(at_kb, r[hh], rs_ref[hh])
                r[hh] = r[hh] + _rowsum(l1)
            return (n + 1, jnp.maximum(jnp.max(r[0]), jnp.max(r[1])), r[0], r[1])

        z1 = jnp.zeros((tq, 1), F32)
        lax.while_loop(more, step, (jnp.int32(0), jnp.float32(0.0), z1, z1))
        o_ref[...] = acc_ref[...].astype(BF16)

    nh2 = d // LANES
    return side_call(
        kern, side,
        name=name,
        grid=(nh2, nq),
        in_specs=[pl.BlockSpec((tq, LANES), lambda h, i: (i, h)),
                  pl.BlockSpec((s, LANES), lambda h, i: (0, h)),
                  pl.BlockSpec((s, LANES), lambda h, i: (0, h)),
                  pl.BlockSpec((2, tk, tk), lambda h, i: (0, 0, 0))],
        out_specs=[pl.BlockSpec((tq, LANES), lambda h, i: (i, h)),
                   pl.BlockSpec((None, 2, tq, LANES), lambda h, i: (h, 0, i, 0))],
        out_shape=[jax.ShapeDtypeStruct((s, d), BF16), jax.ShapeDtypeStruct((nh2, 2, s, LANES), F32)],
        scratch_shapes=[pltpu.VMEM((tq, LANES), F32)],
        args=(qs, kn, vb, tri))


def sb_attn_bwd(qs, kn, vb, rsave, do, tri, name, side=None):
    s, d = qs.shape
    tq, tk = min(SB_TQ, s), min(SB_TK, s)
    nq = s // tq

    def kern(q_ref, k_ref, v_ref, rs_ref, do_ref, tri_ref, dq_ref, dk_ref, dv_ref):
        i = pl.program_id(1)

        @pl.when(i == 0)
        def _():
            dk_ref[...] = jnp.zeros_like(dk_ref)
            dv_ref[...] = jnp.zeros_like(dv_ref)

        row, col, halves, lane_q = _sb_setup(tq, tk)
        q = q_ref[...]
        qh = [q * hm for hm in halves]
        dov = do_ref[...].astype(BF16)
        doh = [dov * hm for hm in halves]
        dq_ref[...] = jnp.zeros_like(dq_ref)
        nkb = (i + 1) * (tq // tk)
        top = jnp.maximum(jnp.max(rs_ref[0], axis=0, keepdims=True), jnp.max(rs_ref[1], axis=0, keepdims=True))
        dead = (top <= SB_DEAD) & (_iota2((1, LANES), 1) < nkb)
        kstart = jnp.minimum(jnp.sum(dead.astype(F32)).astype(jnp.int32), nkb)

        def step(kb, ep):
            ep = list(ep)
            ks = pl.multiple_of(kb * tk, tk)
            k = k_ref[pl.ds(ks, tk), :]
            v = v_ref[pl.ds(ks, tk), :]
            mask = col < row + (i * tq - kb * tk)
            at_kb = lane_q == kb
            for hh in range(2):
                lb, l1, suf = _sb_tile(qh[hh], k, mask, tri_ref[0])
                r = _rowsum(jnp.where(at_kb, rs_ref[hh], 0.0))
                lbm = jnp.where(mask, lb, SB_UNSEEN)
                w = jnp.exp(lbm + suf + r)
                e = _dot_nt(doh[hh], v) * w
                pe = ep[hh] + _dot(e.astype(BF16), tri_ref[1])
                beta = jnp.exp(lbm)
                dz = (e - beta * (e + pe)).astype(BF16)
                dq_ref[...] += _dot(dz, k * halves[hh])
                dk_ref[pl.ds(ks, tk), :] += _dot_tn(dz, qh[hh])
                dv_ref[pl.ds(ks, tk), :] += _dot_tn(w.astype(BF16), doh[hh])
                ep[hh] = ep[hh] + _rowsum(e)
            return tuple(ep)

        z1 = jnp.zeros((tq, 1), F32)
        lax.fori_loop(kstart, nkb, step, (z1, z1))

    nh2 = d // LANES
    return side_call(
        kern, side,
        name=name,
        grid=(nh2, nq),
        in_specs=[pl.BlockSpec((tq, LANES), lambda h, i: (i, h)),
                  pl.BlockSpec((s, LANES), lambda h, i: (0, h)),
                  pl.BlockSpec((s, LANES), lambda h, i: (0, h)),
                  pl.BlockSpec((None, 2, tq, LANES), lambda h, i: (h, 0, i, 0)),
                  pl.BlockSpec((tq, LANES), lambda h, i: (i, h)),
                  pl.BlockSpec((2, tk, tk), lambda h, i: (0, 0, 0))],
        out_specs=[pl.BlockSpec((tq, LANES), lambda h, i: (i, h)),
                   pl.BlockSpec((s, LANES), lambda h, i: (0, h)),
                   pl.BlockSpec((s, LANES), lambda h, i: (0, h))],
        out_shape=[jax.ShapeDtypeStruct((s, d), F32)] * 3,
        scratch_shapes=[],
        args=(qs, kn, vb, rsave, do, tri))


def _hooked(plan, tag, call, *args):
    side = plan.side(tag)
    outs, side_outs = call(*args, tag, side)
    if side is not None:
        plan.done(tag, side_outs)
    return outs


def sb_fwd(x, g, wqkv, qg, kg, wo, bd, tag, plan):
    h = rms_fwd(x, g, f"sb_rms_{tag}")
    qkv = mm(h, wqkv, b_chunks=True, out_dtype=BF16, name=f"sb_qkv_{tag}")
    qs, kn, vb = qknorm_fwd(qkv, qg, kg, bd, f"sb_qknorm_{tag}")
    o, rsave = _hooked(plan, f"sb_attn_{tag}", sb_attn_fwd, qs, kn, vb, _sb_tri(min(SB_TK, x.shape[0])))
    xn = mm(o, wo(), add=x, name=f"sb_out_{tag}")
    return xn, (x, h, qkv, qs, kn, vb, rsave, o)


def sb_bwd(dxn, saved, g, wqkv, qg, kg, wo, bd, tag, plan):
    x, h, qkv, qs, kn, vb, rsave, o = saved
    dxn, dxb = dxn
    do = mm(dxb, wo, tb=True, name=f"sb_do_{tag}")
    dwo = mm(o, dxb, ta=True, out_dtype=BF16, name=f"sb_dwo_{tag}")
    dqs, dkn, dv = _hooked(plan, f"sb_dattn_{tag}", sb_attn_bwd, qs, kn, vb, rsave, do, _sb_tri(min(SB_TK, x.shape[0])))
    dqkv, dqg, dkg = qknorm_bwd(qkv, dqs, dkn, dv, qg, kg, bd, f"sb_dqknorm_{tag}")
    dh = mm(dqkv, wqkv, tb=True, b_chunks=True, name=f"sb_dh_{tag}")
    dwqkv = mm(h, dqkv, ta=True, out_dtype=BF16, out_chunks=True, name=f"sb_dwqkv_{tag}")
    dx, dg = rms_bwd(x, g, dh, dxn, f"sb_drms_{tag}")
    nh = dqg.shape[1] // HEAD
    return dx, dg, dwqkv, dqg.reshape(nh, HEAD).sum(0), dkg.reshape(nh, HEAD).sum(0), dwo


def _gelu(x):
    return 0.5 * x * (1.0 + lax.erf(x * (1.0 / math.sqrt(2.0))))


def _gelu_grad(x):
    return 0.5 * (1.0 + lax.erf(x * (1.0 / math.sqrt(2.0)))) + x * jnp.exp(-0.5 * x * x) * (1.0 / math.sqrt(2.0 * math.pi))


def gm_act_fwd(pre, vg, name):
    half = pre.shape[1] // 2

    def fn(p, vgv):
        p = p.astype(F32)
        u = _gelu(p[:, :half])
        v = _gelu(p[:, half:])
        r = lax.rsqrt(jnp.mean(v * v, axis=1, keepdims=True) + EPS)
        return u, v * r * vgv

    return rowwise(fn, [(pre, "row"), (vg, "full")], [(half, F32), (half, BF16)], tr=512, name=name)


def gm_act_bwd(pre, du, dvn, vg, name):
    half = pre.shape[1] // 2

    def fn(p, duv, dvnv, vgv):
        p = p.astype(F32)
        pu, pv = p[:, :half], p[:, half:]
        v = _gelu(pv)
        r = lax.rsqrt(jnp.mean(v * v, axis=1, keepdims=True) + EPS)
        vh = v * r
        dyg = dvnv * vgv
        dv = r * (dyg - vh * jnp.mean(dyg * vh, axis=1, keepdims=True))
        dpre = jnp.concatenate([duv * _gelu_grad(pu), dv * _gelu_grad(pv)], axis=1)
        return dpre, _colsum(dvnv * vh), _colsum(dpre)

    return rowwise(fn, [(pre, "row"), (du, "row"), (dvn, "row"), (vg, "full")],
                   [(2 * half, BF16)], [(1, half), (1, 2 * half)], tr=256, name=name)


def gm_spatial_fwd(u, vn, wc, bst, name):
    s, c = u.shape
    t = CHUNK
    ng = c // LANES

    def kern(u_ref, v_ref, w_ref, b_ref, o_ref):
        for g in range(ng):
            sl = slice(g * LANES, (g + 1) * LANES)
            mixed = _dot(w_ref[g], v_ref[:, sl]) + b_ref[:, sl]
            o_ref[:, sl] = (u_ref[:, sl] * mixed).astype(BF16)

    return pl.pallas_call(
        kern,
        name=name,
        grid=(s // t,),
        in_specs=[pl.BlockSpec((t, c), lambda i: (i, 0)), pl.BlockSpec((t, c), lambda i: (i, 0)),
                  pl.BlockSpec(wc.shape, lambda i: (0, 0, 0)), pl.BlockSpec(bst.shape, lambda i: (0, 0))],
        out_specs=pl.BlockSpec((t, c), lambda i: (i, 0)),
        out_shape=jax.ShapeDtypeStruct((s, c), BF16),
        compiler_params=_params(("parallel",)),
    )(u, vn, wc, bst)


def gm_spatial_bwd(dgate, u, vn, wc, bst, name):
    s, c = u.shape
    t = CHUNK
    ng = c // LANES

    def kern(dg_ref, u_ref, v_ref, w_ref, b_ref, du_ref, dv_ref, dw_ref, db_ref):
        i = pl.program_id(0)

        @pl.when(i == 0)
        def _():
            dw_ref[...] = jnp.zeros_like(dw_ref)
            db_ref[...] = jnp.zeros_like(db_ref)

        for g in range(ng):
            sl = slice(g * LANES, (g + 1) * LANES)
            vg = v_ref[:, sl]
            dgv = dg_ref[:, sl]
            mixed = _dot(w_ref[g], vg) + b_ref[:, sl]
            du_ref[:, sl] = dgv * mixed
            dmix = dgv * u_ref[:, sl]
            dmb = dmix.astype(BF16)
            dv_ref[:, sl] = _dot_tn(w_ref[g], dmb)
            dw_ref[g] += _dot_nt(dmb, vg)
            db_ref[:, sl] += dmix

    return pl.pallas_call(
        kern,
        name=name,
        grid=(s // t,),
        in_specs=[pl.BlockSpec((t, c), lambda i: (i, 0))] * 3 +
                 [pl.BlockSpec(wc.shape, lambda i: (0, 0, 0)), pl.BlockSpec(bst.shape, lambda i: (0, 0))],
        out_specs=[pl.BlockSpec((t, c), lambda i: (i, 0)), pl.BlockSpec((t, c), lambda i: (i, 0)),
                   pl.BlockSpec(wc.shape, lambda i: (0, 0, 0)), pl.BlockSpec(bst.shape, lambda i: (0, 0))],
        out_shape=[jax.ShapeDtypeStruct((s, c), F32), jax.ShapeDtypeStruct((s, c), F32),
                   jax.ShapeDtypeStruct(wc.shape, F32), jax.ShapeDtypeStruct(bst.shape, F32)],
        compiler_params=_params(("arbitrary",)),
    )(dgate, u, vn, wc, bst)


def gm_fwd(x, g, w_in, b_in, vg, wc, bst, w_out, tag):
    h = rms_fwd(x, g, f"gm_rms_{tag}")
    pre = mm(h, w_in, bias=b_in, b_chunks=True, out_dtype=BF16, name=f"gm_in_{tag}")
    u, vn = gm_act_fwd(pre, vg, f"gm_act_{tag}")
    gate = gm_spatial_fwd(u, vn, wc, bst, f"gm_spatial_{tag}")
    xn = mm(gate, w_out, add=x, name=f"gm_out_{tag}")
    return xn, (x, h, pre, u, vn, gate)


def gm_bwd(dxn, saved, g, w_in, vg, wc, bst, w_out, tag):
    x, h, pre, u, vn, gate = saved
    dxn, dxb = dxn
    dgate = mm(dxb, w_out, tb=True, name=f"gm_dgate_{tag}")
    dwout = mm(gate, dxb, ta=True, out_dtype=BF16, name=f"gm_dwout_{tag}")
    du, dvn, dws, dbst = gm_spatial_bwd(dgate, u, vn, wc, bst, f"gm_dspatial_{tag}")
    dpre, dvg, dbin = gm_act_bwd(pre, du, dvn, vg, f"gm_dact_{tag}")
    dh = mm(dpre, w_in, tb=True, b_chunks=True, name=f"gm_dh_{tag}")
    dwin = mm(h, dpre, ta=True, out_dtype=BF16, out_chunks=True, name=f"gm_dwin_{tag}")
    dx, dg = rms_bwd(x, g, dh, dxn, f"gm_drms_{tag}")
    ng = wc.shape[0]
    dws = jnp.where(jnp.tril(jnp.ones((CHUNK, CHUNK), bool)), dws, 0.0)
    dbs = dbst.reshape(CHUNK, ng, LANES).sum(-1).T
    return dx, dg, dwin, dbin, dvg, dws, dbs, dwout


def _conv_taps(xv, prev):
    cat = jnp.concatenate([prev, xv], axis=0)
    return [pltpu.roll(cat, sh, 0)[SUBLANES:] for sh in (3, 2, 1)] + [xv]


def conv_fwd(xbc, ws, b, d_inner, name):
    c = xbc.shape[1]
    nst = (c - d_inner) // 2

    def fn(xv, prev, w0, w1, w2, w3, bv):
        taps = _conv_taps(xv, prev)
        pre = bv + w0 * taps[0] + w1 * taps[1] + w2 * taps[2] + w3 * taps[3]
        out = pre * _sigmoid(pre)
        return out[:, :d_inner], out[:, d_inner:d_inner + nst], out[:, d_inner + nst:]

    return rowwise(fn, [(xbc, "row"), (xbc, "prev")] + [(w, "full") for w in ws] + [(b, "full")],
                   [(d_inner, F32), (nst, F32), (nst, F32)], tr=512, name=name)


def conv_bwd_pre(xbc, ws, b, dxs_a, dxs_b, db_m, dc_m, name):
    c = xbc.shape[1]

    def fn(xv, prev, w0, w1, w2, w3, bv, da, db2, dbm, dcm):
        taps = _conv_taps(xv, prev)
        pre = bv + w0 * taps[0] + w1 * taps[1] + w2 * taps[2] + w3 * taps[3]
        sg = _sigmoid(pre)
        dout = jnp.concatenate([da + db2, dbm, dcm], axis=1)
        dpre = dout * sg * (1.0 + pre * (1.0 - sg))
        return (dpre,) + tuple(_colsum(dpre * tp) for tp in taps) + (_colsum(dpre),)

    return rowwise(fn, [(xbc, "row"), (xbc, "prev")] + [(w, "full") for w in ws] +
                   [(b, "full"), (dxs_a, "row"), (dxs_b, "row"), (db_m, "row"), (dc_m, "row")],
                   [(c, F32)], [(1, c)] * 5, tr=256, name=name)


def conv_bwd_in(dpre, ws, name):
    c = dpre.shape[1]

    def fn(dv, nxt, w0, w1, w2, w3):
        cat = jnp.concatenate([dv, nxt], axis=0)
        n = cat.shape[0]
        up = [pltpu.roll(cat, n - sh, 0)[:dv.shape[0]] for sh in (1, 2, 3)]
        return (w3 * dv + w2 * up[0] + w1 * up[1] + w0 * up[2],)

    return rowwise(fn, [(dpre, "row"), (dpre, "next")] + [(w, "full") for w in ws], [(c, BF16)], tr=512, name=name)[0]


def ssd_pre(dtr, bias, alog, name):
    def fn(d, bv, al, tri):
        dt = _softplus(d + bv)
        a = dt * (-jnp.exp(al))
        return dt, _dot_x3_left(tri, a)

    tri = jnp.tril(jnp.ones((CHUNK, CHUNK), BF16))
    return rowwise(fn, [(dtr, "row"), (bias, "full"), (alog, "full"), (tri, "full")],
                   [(LANES, F32), (LANES, F32)], tr=CHUNK, name=name)


def _ssd_layouts(v, ngroups, hpg):
    s = v.shape[0]
    col = v[:, :ngroups * hpg].T.reshape(ngroups, hpg, s, 1)
    return jnp.broadcast_to(col, (ngroups, hpg, s, LANES))


def _ssd_rowform(acum, ngroups, hpg):
    s = acum.shape[0]
    nc = s // CHUNK
    a = acum[:, :ngroups * hpg].reshape(nc, CHUNK, ngroups, hpg).transpose(2, 0, 3, 1)
    last = jnp.broadcast_to(a[..., CHUNK - 1:], a.shape)
    return jnp.concatenate([a, last], axis=2)


def ssd_chunk_fwd(xs, bm, cm, col_a, col_dt, rowf, name, side=None):
    s, d_inner = xs.shape
    ln = CHUNK
    nc = s // ln
    nsub = _pick(nc, (SSD_SUB, 2, 1))
    rows = nsub * ln
    ng, hpg = col_a.shape[0], col_a.shape[1]
    gw = d_inner // ng
    assert gw == hpg * HEAD and gw % LANES == 0 and bm.shape[1] == ng * LANES

    def kern(x_ref, b_ref, c_ref, ca_ref, cd_ref, rf_ref, y_ref, hp_ref, h_scr):
        @pl.when(pl.program_id(1) == 0)
        def _():
            h_scr[...] = jnp.zeros_like(h_scr)

        causal = _iota2((ln, ln), 0) >= _iota2((ln, ln), 1)
        lane = _iota2((1, LANES), 1)
        for sc in range(nsub):
            rs = slice(sc * ln, (sc + 1) * ln)
            bb = b_ref[rs, :].astype(BF16)
            cbf = c_ref[rs, :].astype(BF16)
            cb = _dot_nt(cbf, bb)
            ys = [jnp.zeros((ln, LANES), F32) for _ in range(gw // LANES)]
            for r in range(hpg):
                j, hf = divmod(r, LANES // HEAD)
                mh = ((lane >= HEAD * hf) & (lane < HEAD * (hf + 1))).astype(F32)
                ac = ca_ref[r, rs, :]
                ar = rf_ref[sc, pl.ds(r, 1), :]
                aend = rf_ref[sc, pl.ds(4 + r, 1), :]
                dm = jnp.exp(jnp.minimum(ac - ar, 0.0))
                m = jnp.where(causal, cb * dm, 0.0).astype(BF16)
                xdt = x_ref[rs, j * LANES:(j + 1) * LANES] * cd_ref[r, rs, :] * mh
                h = h_scr[r]
                hp_ref[sc, r] = h
                ys[j] = ys[j] + _dot(m, xdt.astype(BF16)) + _dot_nt(cbf, h.astype(BF16)) * jnp.exp(ac)
                dte = jnp.exp(aend - ac)
                h_scr[r] = jnp.exp(aend) * h + _dot_tn((xdt * dte).astype(BF16), bb)
            for j in range(gw // LANES):
                y_ref[rs, j * LANES:(j + 1) * LANES] = ys[j]

    colspec = pl.BlockSpec((None, hpg, rows, LANES), lambda g, c: (g, 0, c, 0))
    return side_call(
        kern, side,
        name=name,
        grid=(ng, nc // nsub),
        in_specs=[pl.BlockSpec((rows, gw), lambda g, c: (c, g)),
                  pl.BlockSpec((rows, LANES), lambda g, c: (c, g)),
                  pl.BlockSpec((rows, LANES), lambda g, c: (c, g)),
                  colspec, colspec,
                  pl.BlockSpec((None, nsub, 8, LANES), lambda g, c: (g, c, 0, 0))],
        out_specs=[pl.BlockSpec((rows, gw), lambda g, c: (c, g)),
                   pl.BlockSpec((None, nsub, hpg, LANES, LANES), lambda g, c: (g, c, 0, 0, 0))],
        out_shape=[jax.ShapeDtypeStruct((s, d_inner), F32),
                   jax.ShapeDtypeStruct((ng, nc, hpg, LANES, LANES), F32)],
        scratch_shapes=[pltpu.VMEM((hpg, LANES, LANES), F32)],
        args=(xs, bm, cm, col_a, col_dt, rowf))


def ssd_chunk_bwd(xs, bm, cm, col_a, col_dt, rowf, hprev, dy, name, side=None):
    s, d_inner = xs.shape
    ln = CHUNK
    nc = s // ln
    nsub = _pick(nc, (SSD_SUB, 2, 1))
    rows = nsub * ln
    ng, hpg = col_a.shape[0], col_a.shape[1]
    gw = d_inner // ng

    def kern(x_ref, b_ref, c_ref, ca_ref, cd_ref, rf_ref, hp_ref, dy_ref,
             dx_ref, db_ref, dc_ref, ddt_ref, da_ref, dh_scr):
        @pl.when(pl.program_id(1) == 0)
        def _():
            dh_scr[...] = jnp.zeros_like(dh_scr)

        row, col = _iota2((ln, ln), 0), _iota2((ln, ln), 1)
        causal = row >= col
        tri_ge = (col >= row).astype(BF16)
        ones = jnp.ones((ln, LANES), BF16)
        lane = _iota2((1, LANES), 1)
        last_row = (_iota2((ln, 1), 0) == ln - 1).astype(F32)
        for sc in reversed(range(nsub)):
            rs = slice(sc * ln, (sc + 1) * ln)
            bb = b_ref[rs, :].astype(BF16)
            cbf = c_ref[rs, :].astype(BF16)
            cb = _dot_nt(cbf, bb)
            dcb = jnp.zeros((ln, ln), F32)
            d_b = jnp.zeros((ln, LANES), F32)
            d_c = jnp.zeros((ln, LANES), F32)
            dxs = [jnp.zeros((ln, LANES), F32) for _ in range(gw // LANES)]
            for r in range(hpg):
                j, hf = divmod(r, LANES // HEAD)
                mh = ((lane >= HEAD * hf) & (lane < HEAD * (hf + 1))).astype(F32)
                ac = ca_ref[r, rs, :]
                dt = cd_ref[r, rs, :]
                ar = rf_ref[sc, pl.ds(r, 1), :]
                aend = rf_ref[sc, pl.ds(4 + r, 1), :]
                dm = jnp.where(causal, jnp.exp(jnp.minimum(ac - ar, 0.0)), 0.0)
                m = cb * dm
                mb = m.astype(BF16)
                xp = x_ref[rs, j * LANES:(j + 1) * LANES]
                xdt = xp * dt * mh
                xdtb = xdt.astype(BF16)
                dyp = dy_ref[rs, j * LANES:(j + 1) * LANES] * mh
                dypb = dyp.astype(BF16)
                h = hp_ref[sc, r]
                hb = h.astype(BF16)
                dh = dh_scr[r]
                dhb = dh.astype(BF16)
                e_in = jnp.exp(ac)
                dte = jnp.exp(aend - ac)
                eend = jnp.exp(aend)
                d_m = _dot_nt(dypb, xdtb)
                dcb = dcb + d_m * dm
                gm = d_m * m
                yoff_pre = _dot_nt(cbf, hb)
                bdh = _dot_nt(bb, dhb)
                dxdt = _dot_tn(mb, dypb) + bdh * dte
                t1 = _rowsum(xdt * bdh) * dte
                gh, gl = _split2(gm)
                dacum = (_rowsum(gm) - (_dot_tn(gh, ones) + _dot_tn(gl, ones))
                         + _rowsum(dyp * yoff_pre) * e_in - t1)
                end_term = _colsum(t1) + eend * jnp.sum(_colsum(dh * h), axis=1, keepdims=True)
                dacum = dacum + last_row * end_term
                da_ref[r, rs, :] = _dot_x3_left(tri_ge, dacum)
                ddt_ref[r, rs, :] = jnp.broadcast_to(_rowsum(dxdt * xp), (ln, LANES))
                dxs[j] = dxs[j] + dxdt * dt
                d_b = d_b + _dot((xdt * dte).astype(BF16), dhb)
                dye = (dyp * e_in).astype(BF16)
                d_c = d_c + _dot(dye, hb)
                dh_scr[r] = eend * dh + _dot_tn(dye, cbf)
            dcbb = dcb.astype(BF16)
            dc_ref[rs, :] = d_c + _dot(dcbb, bb)
            db_ref[rs, :] = d_b + _dot_tn(dcbb, cbf)
            for j in range(gw // LANES):
                dx_ref[rs, j * LANES:(j + 1) * LANES] = dxs[j]

    rev = nc // nsub - 1
    colspec = pl.BlockSpec((None, hpg, rows, LANES), lambda g, c: (g, 0, rev - c, 0))
    return side_call(
        kern, side,
        name=name,
        grid=(ng, nc // nsub),
        in_specs=[pl.BlockSpec((rows, gw), lambda g, c: (rev - c, g)),
                  pl.BlockSpec((rows, LANES), lambda g, c: (rev - c, g)),
                  pl.BlockSpec((rows, LANES), lambda g, c: (rev - c, g)),
                  colspec, colspec,
                  pl.BlockSpec((None, nsub, 8, LANES), lambda g, c: (g, rev - c, 0, 0)),
                  pl.BlockSpec((None, nsub, hpg, LANES, LANES), lambda g, c: (g, rev - c, 0, 0, 0)),
                  pl.BlockSpec((rows, gw), lambda g, c: (rev - c, g))],
        out_specs=[pl.BlockSpec((rows, gw), lambda g, c: (rev - c, g)),
                   pl.BlockSpec((rows, LANES), lambda g, c: (rev - c, g)),
                   pl.BlockSpec((rows, LANES), lambda g, c: (rev - c, g)),
                   colspec, colspec],
        out_shape=[jax.ShapeDtypeStruct((s, d_inner), F32),
                   jax.ShapeDtypeStruct(bm.shape, F32), jax.ShapeDtypeStruct(cm.shape, F32),
                   jax.ShapeDtypeStruct(col_a.shape, F32), jax.ShapeDtypeStruct(col_a.shape, F32)],
        scratch_shapes=[pltpu.VMEM((hpg, LANES, LANES), F32)],
        args=(xs, bm, cm, col_a, col_dt, rowf, hprev, dy))


def gnorm_fwd(y, xs, z, dexp, gain, ngroups, name):
    c = y.shape[1]
    gw = c // ngroups

    def fn(yv, xv, zv, dv, gv):
        yg = (yv + xv * dv) * (zv * _sigmoid(zv))
        outs = []
        for k in range(ngroups):
            t = yg[:, k * gw:(k + 1) * gw]
            outs.append(t * lax.rsqrt(jnp.mean(t * t, axis=1, keepdims=True) + EPS))
        return (jnp.concatenate(outs, axis=1) * gv,)

    return rowwise(fn, [(y, "row"), (xs, "row"), (z, "row"), (dexp, "full"), (gain, "full")], [(c, BF16)], tr=512, name=name)[0]


def gnorm_bwd(dn, y, xs, z, dexp, gain, ngroups, name):
    c = y.shape[1]
    gw = c // ngroups

    def fn(dnv, yv, xv, zv, dv, gv):
        yd = yv + xv * dv
        sg = _sigmoid(zv)
        sz = zv * sg
        yg = yd * sz
        dng = dnv * gv
        dyg, yh = [], []
        for k in range(ngroups):
            sl = slice(k * gw, (k + 1) * gw)
            t = yg[:, sl]
            r = lax.rsqrt(jnp.mean(t * t, axis=1, keepdims=True) + EPS)
            th = t * r
            dyg.append(r * (dng[:, sl] - th * jnp.mean(dng[:, sl] * th, axis=1, keepdims=True)))
            yh.append(th)
        dyg = jnp.concatenate(dyg, axis=1)
        yh = jnp.concatenate(yh, axis=1)
        dyd = dyg * sz
        dz = dyg * yd * (sg * (1.0 + zv * (1.0 - sg)))
        return dyd, dyd * dv, dz, _colsum(dyd * xv), _colsum(dnv * yh)

    return rowwise(fn, [(dn, "row"), (y, "row"), (xs, "row"), (z, "row"), (dexp, "full"), (gain, "full")],
                   [(c, F32), (c, F32), (c, BF16)], [(1, c), (1, c)], tr=256, name=name)


def ssd_post(ddt, da, dt, dtr, bias, alog, name):
    def fn(ddtv, dav, dtv, dtrv, bv, al):
        a_neg = -jnp.exp(al)
        ddtr = (ddtv + dav * a_neg) * _sigmoid(dtrv + bv)
        return ddtr, _colsum(ddtr), _colsum(dav * dtv) * a_neg

    return rowwise(fn, [(ddt, "row"), (da, "row"), (dt, "row"), (dtr, "row"), (bias, "full"), (alog, "full")],
                   [(LANES, BF16)], [(1, LANES), (1, LANES)], tr=512, name=name)


def _from_colform(v, s):
    ng, hpg = v.shape[0], v.shape[1]
    flat = v[..., 0].reshape(ng * hpg, s).T
    return jnp.pad(flat, ((0, 0), (0, LANES - ng * hpg)))


def ssm_fwd(x, g, p, tag, plan):
    ng, hpg, d_inner = p["ng"], p["hpg"], p["d_inner"]
    h = rms_fwd(x, g, f"ssm_rms_{tag}")
    z = mm(h, p["w_z"], name=f"ssm_inz_{tag}")
    xbc = mm(h, p["w_xbc"], name=f"ssm_inx_{tag}")
    dtr = mm(h, p["w_dt"], name=f"ssm_indt_{tag}")
    xs, bm, cm = conv_fwd(xbc, p["conv_w"], p["conv_b"], d_inner, f"ssm_conv_{tag}")
    dt, acum = ssd_pre(dtr, p["dt_bias"], p["a_log"], f"ssm_pre_{tag}")
    col_a, col_dt = _ssd_layouts(acum, ng, hpg), _ssd_layouts(dt, ng, hpg)
    rowf = _ssd_rowform(acum, ng, hpg)
    y, hprev = _hooked(plan, f"ssm_scan_{tag}", ssd_chunk_fwd, xs, bm, cm, col_a, col_dt, rowf)
    n = gnorm_fwd(y, xs, z, p["d_exp"], p["norm_gain"], ng, f"ssm_gnorm_{tag}")
    xn = mm(n, p["w_out"], add=x, name=f"ssm_out_{tag}")
    return xn, (x, h, z, xbc, dtr, xs, bm, cm, dt, col_a, col_dt, rowf, y, hprev, n)


def ssm_bwd(dxn, saved, g, p, tag, plan):
    x, h, z, xbc, dtr, xs, bm, cm, dt, col_a, col_dt, rowf, y, hprev, n = saved
    ng, hpg, d_inner = p["ng"], p["hpg"], p["d_inner"]
    s = x.shape[0]
    dxn, dxb = dxn
    dn = mm(dxb, p["w_out"], tb=True, name=f"ssm_dn_{tag}")
    dwout = mm(n, dxb, ta=True, out_dtype=BF16, name=f"ssm_dwout_{tag}")
    dy, dxs_skip, dz, dd_lane, dgain = gnorm_bwd(dn, y, xs, z, p["d_exp"], p["norm_gain"], ng, f"ssm_dgnorm_{tag}")
    dxs, dbm, dcm, ddt_c, da_c = _hooked(plan, f"ssm_dscan_{tag}", ssd_chunk_bwd, xs, bm, cm, col_a, col_dt, rowf, hprev, dy)
    ddtr, dbias, dalog = ssd_post(_from_colform(ddt_c, s), _from_colform(da_c, s), dt, dtr,
                                  p["dt_bias"], p["a_log"], f"ssm_post_{tag}")
    res = conv_bwd_pre(xbc, p["conv_w"], p["conv_b"], dxs, dxs_skip, dbm, dcm, f"ssm_dconv_{tag}")
    dpre, dconv_w, dconv_b = res[0], jnp.concatenate(res[1:5], axis=0), res[5]
    dxbc = conv_bwd_in(dpre, p["conv_w"], f"ssm_dconvin_{tag}")
    dh = mm(dz, p["w_z"], tb=True, name=f"ssm_dhz_{tag}")
    dh = mm(dxbc, p["w_xbc"], tb=True, add=dh, name=f"ssm_dhx_{tag}")
    dh = mm(ddtr, p["w_dt"], tb=True, add=dh, name=f"ssm_dhdt_{tag}")
    dwz = mm(h, dz, ta=True, out_dtype=BF16, name=f"ssm_dwz_{tag}")
    dwxbc = mm(h, dxbc, ta=True, out_dtype=BF16, name=f"ssm_dwxbc_{tag}")
    dwdt = mm(h, ddtr, ta=True, out_dtype=BF16, name=f"ssm_dwdt_{tag}")
    dx, dg = rms_bwd(x, g, dh, dxn, f"ssm_drms_{tag}")
    nh = ng * hpg
    dwin = jnp.concatenate([dwz, dwxbc, dwdt[:, :nh]], axis=1)
    dd = dd_lane.reshape(nh, HEAD).sum(-1)
    return dx, dg, dict(w_in=dwin, conv_w=dconv_w, conv_b=dconv_b, dt_bias=dbias[0, :nh], a_log=dalog[0, :nh],
                        d=dd, norm_gain=dgain, w_out=dwout)


def local_step(x, target, w, plan):
    d = x.shape[1]
    depth = w["mix_norm"].shape[0]
    bd = _head_blockdiag(LANES)
    tril = jnp.tril(jnp.ones((CHUNK, CHUNK), bool))
    ssm_heads = w["ssm_dt_bias"].shape[1]
    d_inner = w["ssm_norm_gain"].shape[1]
    ng = w["ssm_norm_gain"].shape[1] // 256
    nstate = CHUNK

    def pad_lanes(v):
        return jnp.pad(v, ((0, 0), (0, LANES - v.shape[1])))

    def ssm_params(j):
        w_in = w["ssm_w_in"][j]
        cw = w["ssm_conv_w"][j]
        return dict(ng=ng, hpg=ssm_heads // ng, d_inner=d_inner,
                    w_z=w_in[:, :d_inner], w_xbc=w_in[:, d_inner:d_inner + d_inner + 2 * ng * nstate],
                    w_dt=pad_lanes(w_in[:, 2 * d_inner + 2 * ng * nstate:]),
                    conv_w=[cw[k:k + 1] for k in range(cw.shape[0])], conv_b=w["ssm_conv_b"][j:j + 1],
                    dt_bias=pad_lanes(w["ssm_dt_bias"][j:j + 1]), a_log=pad_lanes(w["ssm_a_log"][j:j + 1]),
                    d_exp=jnp.repeat(w["ssm_d"][j], HEAD)[None, :], norm_gain=w["ssm_norm_gain"][j:j + 1],
                    w_out=w["ssm_w_out"][j])

    def gm_params(j):
        wc = jnp.where(tril, w["gm_w_s"][j], 0.0).astype(BF16)
        bst = jnp.repeat(w["gm_b_s"][j].T, LANES, axis=1)
        return wc, bst

    def sb_gains(j):
        nh = d // HEAD
        return jnp.tile(w["sb_q_gain"][j], nh)[None, :], jnp.tile(w["sb_k_gain"][j], nh)[None, :]

    saved = []
    cur = x
    for i in range(depth):
        kind, j = i % 3, i // 3
        gmix = w["mix_norm"][i:i + 1]
        if kind == 0:
            qg, kg = sb_gains(j)
            cur, sv = sb_fwd(cur, gmix, w["sb_w_qkv"][j], qg, kg, lambda j=j: w["sb_w_o"][j], bd, f"{i}", plan)
        elif kind == 1:
            wc, bst = gm_params(j)
            cur, sv = gm_fwd(cur, gmix, w["gm_w_in"][j], w["gm_b_in"][j:j + 1], w["gm_v_gain"][j:j + 1], wc, bst,
                             w["gm_w_out"][j], f"{i}")
        else:
            cur, sv = ssm_fwd(cur, gmix, ssm_params(j), f"{i}", plan)
        cur, sv2 = ffn_fwd(cur, w["ffn_norm"][i:i + 1], w["ffn_w_gu"][i], w["ffn_w_down"][i], f"{i}", plan)
        saved.append((sv, sv2))

    loss, dcur = loss_and_grad(cur, target, "loss")

    grads = {k: [None] * len(v) for k, v in w.items()}
    for i in reversed(range(depth)):
        kind, j = i % 3, i // 3
        sv, sv2 = saved[i]
        gmix = w["mix_norm"][i:i + 1]
        dcur, dgf, dwgu, dwdown = ffn_bwd(dcur, sv2, w["ffn_norm"][i:i + 1], w["ffn_w_gu"][i], w["ffn_w_down"][i], f"{i}")
        grads["ffn_norm"][i], grads["ffn_w_gu"][i], grads["ffn_w_down"][i] = dgf[0], dwgu, dwdown
        plan.grads_ready({("ffn_w_gu", i): dwgu, ("ffn_w_down", i): dwdown})
        if kind == 0:
            qg, kg = sb_gains(j)
            dcur, dg, dwqkv, dqg, dkg, dwo = sb_bwd(dcur, sv, gmix, w["sb_w_qkv"][j], qg, kg, w["sb_w_o"][j], bd, f"{i}", plan)
            grads["sb_w_qkv"][j], grads["sb_q_gain"][j], grads["sb_k_gain"][j], grads["sb_w_o"][j] = dwqkv, dqg, dkg, dwo
        elif kind == 1:
            wc, bst = gm_params(j)
            dcur, dg, dwin, dbin, dvg, dws, dbs, dwout = gm_bwd(dcur, sv, gmix, w["gm_w_in"][j], w["gm_v_gain"][j:j + 1],
                                                                 wc, bst, w["gm_w_out"][j], f"{i}")
            grads["gm_w_in"][j], grads["gm_b_in"][j], grads["gm_v_gain"][j] = dwin, dbin[0], dvg[0]
            grads["gm_w_s"][j], grads["gm_b_s"][j], grads["gm_w_out"][j] = dws, dbs, dwout
        else:
            dcur, dg, gs = ssm_bwd(dcur, sv, gmix, ssm_params(j), f"{i}", plan)
            grads["ssm_w_in"][j], grads["ssm_conv_w"][j], grads["ssm_conv_b"][j] = gs["w_in"], gs["conv_w"], gs["conv_b"][0]
            grads["ssm_dt_bias"][j], grads["ssm_a_log"][j], grads["ssm_d"][j] = gs["dt_bias"], gs["a_log"], gs["d"]
            grads["ssm_norm_gain"][j], grads["ssm_w_out"][j] = gs["norm_gain"][0], gs["w_out"]
        grads["mix_norm"][i] = dg[0]
        mixer = {0: ("sb_w_qkv", "sb_w_o"), 1: ("gm_w_in", "gm_w_out"), 2: ("ssm_w_in", "ssm_w_out")}[kind]
        plan.grads_ready({(n, j): grads[n][j] for n in mixer})
    grads = {k: (v if k in MATRICES else jnp.stack(v)) for k, v in grads.items()}
    return loss, dcur[0], grads


WEIGHTS = ["mix_norm", "ffn_norm", "sb_w_qkv", "sb_q_gain", "sb_k_gain", "sb_w_o", "gm_w_in", "gm_b_in", "gm_v_gain",
           "gm_w_s", "gm_b_s", "gm_w_out", "ssm_w_in", "ssm_conv_w", "ssm_conv_b", "ssm_dt_bias", "ssm_a_log", "ssm_d",
           "ssm_norm_gain", "ssm_w_out", "ffn_w_gu", "ffn_w_down"]
SHARDED = {"sb_w_qkv": 2, "sb_w_o": 1, "gm_w_in": 2, "gm_w_out": 1, "ssm_w_in": 2, "ssm_conv_w": 2, "ssm_conv_b": 1,
           "ssm_norm_gain": 1, "ssm_w_out": 1, "ffn_w_gu": 2, "ffn_w_down": 1}
EXACT = ("ssm_conv_w", "ssm_conv_b", "ssm_norm_gain")
MATRICES = tuple(n for n in SHARDED if n not in EXACT)
COLUMN_BLOCKS = ("sb_w_qkv", "gm_w_in", "ffn_w_gu")
REPLICATED = [n for n in WEIGHTS if n not in SHARDED]
N_CHIPS = 4
N_DEV = 8
PACK_COLS = 1024


def _pack(pieces, dtype, align):
    flat = jnp.concatenate([p.reshape(-1).astype(dtype) for p in pieces])
    rows = -(-flat.shape[0] // (PACK_COLS * align)) * align
    flat = jnp.pad(flat, (0, rows * PACK_COLS - flat.shape[0]))
    return flat.reshape(rows, PACK_COLS)


def _unpack(flat, shapes):
    out, off = [], 0
    for shp in shapes:
        n = math.prod(shp)
        out.append(flat[off:off + n].reshape(shp))
        off += n
    return out


ANY = pl.BlockSpec(memory_space=pl.ANY)


def _pos():
    return lax.axis_index("x"), lax.axis_index("y"), lax.axis_index("c")


def _remote(src, dst, send, recv, k, to):
    return pltpu.make_async_remote_copy(src_ref=src, dst_ref=dst, send_sem=send.at[k], recv_sem=recv.at[k],
                                        device_id=to, device_id_type=MESH_ID)


def _comm_call(body, name, ins, out_shapes, nsem, aliases=None):
    return pl.pallas_call(
        body, name=name, out_shape=out_shapes,
        in_specs=[ANY] * len(ins), out_specs=[ANY] * len(out_shapes),
        scratch_shapes=[pltpu.SemaphoreType.DMA((nsem,)), pltpu.SemaphoreType.DMA((nsem,))],
        input_output_aliases=aliases or {},
    )(*ins)


def stage_shard(w, chip, name):
    rows, cols = w.shape
    tr = _pick(rows, (512, 352, 256, 128))

    def kern(idx_ref, w_ref, o_ref):
        o_ref[...] = w_ref[...].astype(BF16)

    grid_spec = pltpu.PrefetchScalarGridSpec(
        num_scalar_prefetch=1, grid=(rows // tr,),
        in_specs=[pl.BlockSpec((tr, cols), lambda i, idx: (i, 0))],
        out_specs=pl.BlockSpec((None, tr, cols), lambda i, idx: (idx[0], i, 0)))
    return pl.pallas_call(
        kern, name=name, grid_spec=grid_spec,
        out_shape=jax.ShapeDtypeStruct((N_CHIPS, rows, cols), BF16),
        compiler_params=_params(("parallel",)),
    )(jnp.reshape(chip, (1,)).astype(jnp.int32), w)


class Side:
    def __init__(self, arrays, out_shapes, aliases, nsem, start, finish):
        self.arrays, self.out_shapes, self.aliases, self.nsem = list(arrays), list(out_shapes), aliases, nsem
        self.start, self.finish = start, finish


def run_side(side, name):
    n_in, n_out = len(side.arrays), len(side.out_shapes)

    def body(*refs):
        ins, outs = refs[:n_in], refs[n_in:n_in + n_out]
        send, recv = refs[n_in + n_out:]
        side.start(ins, outs, send, recv)
        side.finish(ins, outs, send, recv)

    return _comm_call(body, name, side.arrays, side.out_shapes, side.nsem, aliases=side.aliases)


def side_call(kern, side, *, name, grid, in_specs, out_specs, out_shape, scratch_shapes, args):
    if side is None:
        res = pl.pallas_call(kern, name=name, grid=grid, in_specs=in_specs, out_specs=out_specs, out_shape=out_shape,
                             scratch_shapes=scratch_shapes,
                             compiler_params=_params(("parallel",) + ("arbitrary",) * (len(grid) - 1)))(*args)
        return list(res), []
    n_in, n_out, n_scr = len(in_specs), len(out_specs), len(scratch_shapes)
    s_in, s_out = len(side.arrays), len(side.out_shapes)

    def body(*refs):
        ins, refs = refs[:n_in], refs[n_in:]
        side_ins, refs = refs[:s_in], refs[s_in:]
        outs, refs = refs[:n_out], refs[n_out:]
        side_outs, refs = refs[:s_out], refs[s_out:]
        scr, (send, recv) = refs[:n_scr], refs[n_scr:]
        first, last = None, None
        for axis, size in enumerate(grid):
            at0, at1 = pl.program_id(axis) == 0, pl.program_id(axis) == size - 1
            first = at0 if first is None else first & at0
            last = at1 if last is None else last & at1

        @pl.when(first)
        def _():
            side.start(side_ins, side_outs, send, recv)

        kern(*ins, *outs, *scr)

        @pl.when(last)
        def _():
            side.finish(side_ins, side_outs, send, recv)

    res = pl.pallas_call(
        body, name=name, grid=grid,
        in_specs=list(in_specs) + [ANY] * s_in, out_specs=list(out_specs) + [ANY] * s_out,
        out_shape=list(out_shape) + side.out_shapes,
        scratch_shapes=list(scratch_shapes) + [pltpu.SemaphoreType.DMA((side.nsem,)), pltpu.SemaphoreType.DMA((side.nsem,))],
        input_output_aliases={n_in + a: n_out + b for a, b in side.aliases.items()},
        compiler_params=_params(("arbitrary",) * len(grid)),
    )(*args, *side.arrays)
    return list(res[:n_out]), list(res[n_out:])


def gather_side(staged):
    n = len(staged)

    def plan(o_refs, send, recv):
        x, y, c = _pos()
        chips = [(1 - x, y), (x, 1 - y), (1 - x, 1 - y)]

        def part(u, chip, cc):
            half = staged[u].shape[1] // 2
            return o_refs[u].at[2 * chip[0] + chip[1], pl.ds(cc * half, half), :]

        first = [_remote(part(u, (x, y), c), part(u, (x, y), c), send, recv, 6 * u + j, (*chip, c))
                 for u in range(n) for j, chip in enumerate(chips)]
        landed = [_remote(part(u, chip, c), part(u, chip, c), send, recv, 6 * u + j, (x, y, c))
                  for u in range(n) for j, chip in enumerate(chips)]
        passed = [_remote(part(u, chip, c), part(u, chip, c), send, recv, 6 * u + 3 + j, (x, y, 1 - c))
                  for u in range(n) for j, chip in enumerate(chips)]
        handed = [_remote(part(u, chip, 1 - c), part(u, chip, 1 - c), send, recv, 6 * u + 3 + j, (x, y, c))
                  for u in range(n) for j, chip in enumerate(chips)]
        return first, landed, passed, handed

    def start(ins, outs, send, recv):
        for cp in plan(outs, send, recv)[0]:
            cp.start()

    def finish(ins, outs, send, recv):
        first, landed, passed, handed = plan(outs, send, recv)
        for got, fw in zip(landed, passed):
            got.wait_recv()
            fw.start()
        for got in handed:
            got.wait_recv()
        for cp in first + passed:
            cp.wait_send()

    outs = [jax.ShapeDtypeStruct(s.shape, s.dtype) for s in staged]
    return Side(staged, outs, {u: u for u in range(n)}, 6 * n, start, finish)


def swap_halves(gps, name):
    n = len(gps)

    def body(*refs):
        g_refs, r_refs = refs[:n], refs[n:2 * n]
        send, recv = refs[2 * n:]
        x, y, c = _pos()
        cps = []
        for u in range(n):
            half = gps[u].shape[1] // 2
            cps.append(_remote(g_refs[u].at[:, pl.ds((1 - c) * half, half), :], r_refs[u], send, recv, u, (x, y, 1 - c)))
        for cp in cps:
            cp.start()
        for cp in cps:
            cp.wait()

    outs = [jax.ShapeDtypeStruct((g.shape[0], g.shape[1] // 2, g.shape[2]), g.dtype) for g in gps]
    return _comm_call(body, name, gps, outs, n)


def scatter_side(parts):
    n = len(parts)

    def plan(p_refs, r_refs, send, recv):
        x, y, c = _pos()
        chips = [(1 - x, y), (x, 1 - y), (1 - x, 1 - y)]
        return [_remote(p_refs[u].at[2 * chip[0] + chip[1]], r_refs[u].at[j], send, recv, 3 * u + j, (*chip, c))
                for u in range(n) for j, chip in enumerate(chips)]

    def start(ins, outs, send, recv):
        for cp in plan(ins, outs, send, recv):
            cp.start()

    def finish(ins, outs, send, recv):
        for cp in plan(ins, outs, send, recv):
            cp.wait()

    outs = [jax.ShapeDtypeStruct((N_CHIPS - 1,) + p.shape[1:], p.dtype) for p in parts]
    return Side(parts, outs, {}, 3 * n, start, finish)


def join_halves(bufs):
    n = len(bufs)

    def body(*refs):
        o_refs = refs[n:2 * n]
        send, recv = refs[2 * n:]
        x, y, c = _pos()

        def rows(u, cc):
            half = bufs[u].shape[0] // 2
            return o_refs[u].at[pl.ds(cc * half, half), :]

        cps = [_remote(rows(u, c), rows(u, c), send, recv, u, (x, y, 1 - c)) for u in range(n)]
        for cp in cps:
            cp.start()
        for u in range(n):
            _remote(rows(u, 1 - c), rows(u, 1 - c), send, recv, u, (x, y, c)).wait_recv()
        for cp in cps:
            cp.wait_send()

    outs = [jax.ShapeDtypeStruct(b.shape, b.dtype) for b in bufs]
    return _comm_call(body, "join_halves", bufs, outs, n, aliases={u: u for u in range(n)})


def gather_small(sg, name):
    rows, cols = sg.shape

    def body(s_ref, o_ref, send, recv, lsem):
        x, y, c = _pos()
        me, sibling = (x, y, c), (x, y, 1 - c)
        chips = [(1 - x, y), (x, 1 - y), (1 - x, 1 - y)]

        def blk(px, py, pc):
            return o_ref.at[4 * px + 2 * py + pc]

        mine = pltpu.make_async_copy(s_ref, blk(*me), lsem)
        mine.start()
        first = [_remote(s_ref, blk(*me), send, recv, 0, sibling)]
        first += [_remote(s_ref, blk(*me), send, recv, 1 + j, (*chip, c)) for j, chip in enumerate(chips)]
        for cp in first:
            cp.start()
        passed = [_remote(blk(*chip, c), blk(*chip, c), send, recv, 4 + j, sibling) for j, chip in enumerate(chips)]
        for j, chip in enumerate(chips):
            _remote(blk(*chip, c), blk(*chip, c), send, recv, 1 + j, me).wait_recv()
            passed[j].start()
        _remote(blk(*sibling), blk(*sibling), send, recv, 0, me).wait_recv()
        for j, chip in enumerate(chips):
            _remote(blk(*chip, 1 - c), blk(*chip, 1 - c), send, recv, 4 + j, me).wait_recv()
        for cp in first + passed:
            cp.wait_send()
        mine.wait()

    return pl.pallas_call(
        body, name=name,
        out_shape=jax.ShapeDtypeStruct((N_DEV, rows, cols), sg.dtype),
        in_specs=[ANY], out_specs=ANY,
        scratch_shapes=[pltpu.SemaphoreType.DMA((N_DEV - 1,)), pltpu.SemaphoreType.DMA((N_DEV - 1,)), pltpu.SemaphoreType.DMA],
    )(sg)


def sum_cores(gp, theirs, core, chip, name):
    nch, rows, cols = gp.shape
    half = rows // 2
    tr = _pick(half, (512, 352, 256, 176, 128, 64))
    nb = half // tr

    def kern(idx_ref, g_ref, t_ref, own_ref, all_ref):
        k = pl.program_id(1)
        s = g_ref[...].astype(F32) + t_ref[...].astype(F32)
        all_ref[...] = s.astype(BF16)

        @pl.when(k == idx_ref[1])
        def _():
            own_ref[...] = s

    grid_spec = pltpu.PrefetchScalarGridSpec(
        num_scalar_prefetch=1, grid=(nb, nch),
        in_specs=[pl.BlockSpec((None, tr, cols), lambda i, k, idx: (k, idx[0] * nb + i, 0)),
                  pl.BlockSpec((None, tr, cols), lambda i, k, idx: (k, i, 0))],
        out_specs=[pl.BlockSpec((tr, cols), lambda i, k, idx: (i, 0)),
                   pl.BlockSpec((None, tr, cols), lambda i, k, idx: (k, i, 0))])
    return pl.pallas_call(
        kern, name=name, grid_spec=grid_spec,
        out_shape=[jax.ShapeDtypeStruct((half, cols), F32), jax.ShapeDtypeStruct((nch, half, cols), BF16)],
        compiler_params=_params(("parallel", "arbitrary")),
    )(jnp.stack([core, chip]).astype(jnp.int32), gp, theirs)


def sum_chips(own, others, core, name):
    half, cols = own.shape
    tr = _pick(half, (512, 352, 256, 176, 128, 64))
    nb = half // tr

    def kern(idx_ref, o_ref, a_ref, b_ref, c_ref, out_ref):
        out_ref[...] = ((o_ref[...] + a_ref[...].astype(F32)) + b_ref[...].astype(F32)) + c_ref[...].astype(F32)

    grid_spec = pltpu.PrefetchScalarGridSpec(
        num_scalar_prefetch=1, grid=(nb,),
        in_specs=[pl.BlockSpec((tr, cols), lambda i, idx: (i, 0))] +
                 [pl.BlockSpec((None, tr, cols), lambda i, idx, j=j: (j, i, 0)) for j in range(N_CHIPS - 1)],
        out_specs=pl.BlockSpec((tr, cols), lambda i, idx: (idx[0] * nb + i, 0)))
    return pl.pallas_call(
        kern, name=name, grid_spec=grid_spec,
        out_shape=jax.ShapeDtypeStruct((2 * half, cols), F32),
        compiler_params=_params(("parallel",)),
    )(jnp.reshape(core, (1,)).astype(jnp.int32), own, others, others, others)


def small_update(gath, w, m, v, name):
    def fn(*vs):
        g = vs[0]
        for t in vs[1:N_DEV]:
            g = g + t
        wv, mv, vv = vs[N_DEV:]
        m2 = ADAM_B1 * mv + (1.0 - ADAM_B1) * g
        v2 = ADAM_B2 * vv + (1.0 - ADAM_B2) * (g * g)
        m_hat = m2 / (1.0 - ADAM_B1 ** ADAM_STEP)
        v_hat = v2 / (1.0 - ADAM_B2 ** ADAM_STEP)
        return g, -ADAM_LR * (m_hat / (jnp.sqrt(v_hat) + ADAM_EPS) + ADAM_WD * wv), m2, v2

    c = w.shape[1]
    ins = [(gath[k], "row") for k in range(N_DEV)] + [(w, "row"), (m, "row"), (v, "row")]
    return rowwise(fn, ins, [(c, F32)] * 4, tr=w.shape[0] // 2, name=name)


_MIX = {0: [("sb_w_qkv", 0), ("sb_w_o", 0)], 1: [("gm_w_in", 0), ("gm_w_out", 0)],
        2: [("ssm_w_in", 0), ("ssm_w_out", 0)], 3: [("sb_w_qkv", 1), ("sb_w_o", 1)]}
_FFN = {i: [("ffn_w_gu", i), ("ffn_w_down", i)] for i in range(4)}
GATHER_FIRST = _MIX[0][:1]
GATHER_AT = {"sb_attn_0": _MIX[0][1:] + _FFN[0] + _FFN[1],
             "ffn_gu_0": _MIX[1], "ffn_down_0": _MIX[2][1:], "ffn_gu_1": _MIX[2][:1], "ffn_down_1": _FFN[2][1:],
             "ssm_scan_2": _FFN[2][:1] + _MIX[3] + _FFN[3][1:], "ffn_gu_2": _FFN[3][:1]}
SCATTER_AT = {"ssm_dscan_2": _FFN[3] + _MIX[3] + _FFN[2], "sb_dattn_0": _MIX[2] + _FFN[1] + _MIX[1] + _FFN[0]}
SCATTER_LAST = _MIX[0]


class _Plan:
    def __init__(self, ins, core, chip):
        self.core, self.chip = core, chip
        self.staged = {(n, l): stage_shard(ins[n][l], chip, f"stage_{n}_{l}")
                       for n in MATRICES for l in range(ins[n].shape[0])}
        self.full = {n: [None] * ins[n].shape[0] for n in MATRICES}
        self.ready = {}
        self.parts = {}
        self.halves = {}
        self.swaps = 0
        self._fill(GATHER_FIRST, run_side(gather_side([self.staged[u] for u in GATHER_FIRST]), "gather_first"))

    def _fill(self, units, gathered):
        for (n, l), g in zip(units, gathered):
            if n in COLUMN_BLOCKS:
                self.full[n][l] = g
            elif n == "ssm_w_in":
                self.full[n][l] = jnp.concatenate([g[k] for k in range(N_CHIPS)], axis=1)
            else:
                self.full[n][l] = g.reshape(-1, g.shape[-1])

    def _prepare(self, units):
        gps = [self.ready[u] for u in units]
        theirs = swap_halves(gps, f"swap_halves_{self.swaps}")
        self.swaps += 1
        for (n, l), g, t in zip(units, gps, theirs):
            self.parts[(n, l)] = sum_cores(g, t, self.core, self.chip, f"sum_cores_{n}_{l}")

    def _reduce(self, units, others):
        for (n, l), other in zip(units, others):
            self.halves[(n, l)] = sum_chips(self.parts[(n, l)][0], other, self.core, f"sum_chips_{n}_{l}")

    def side(self, tag):
        if tag in GATHER_AT:
            return gather_side([self.staged[u] for u in GATHER_AT[tag]])
        if tag in SCATTER_AT:
            self._prepare(SCATTER_AT[tag])
            return scatter_side([self.parts[u][1] for u in SCATTER_AT[tag]])
        return None

    def done(self, tag, results):
        if tag in GATHER_AT:
            self._fill(GATHER_AT[tag], results)
        else:
            self._reduce(SCATTER_AT[tag], results)

    def grads_ready(self, grads):
        for (n, l), g in grads.items():
            if n in COLUMN_BLOCKS:
                self.ready[(n, l)] = g
            elif n == "ssm_w_in":
                self.ready[(n, l)] = jnp.stack(jnp.split(g, N_CHIPS, axis=1))
            else:
                self.ready[(n, l)] = g.reshape(N_CHIPS, -1, g.shape[-1])

    def shard_grads(self):
        self._prepare(SCATTER_LAST)
        self._reduce(SCATTER_LAST, run_side(scatter_side([self.parts[u][1] for u in SCATTER_LAST]), "scatter_last"))
        units = sorted(self.halves)
        return dict(zip(units, join_halves([self.halves[u] for u in units])))


def _step(ins):
    x, target = ins["x"][0], ins["loss_target"][0]
    core = lax.axis_index("c")
    chip = 2 * lax.axis_index("x") + lax.axis_index("y")

    def lane_pad(v):
        return jnp.pad(v, ((0, 0), (0, PACK_COLS - v.shape[1])))

    vec_rows = [ins["ssm_conv_w"][0], ins["ssm_conv_b"], lane_pad(ins["ssm_norm_gain"])]
    blk = jnp.concatenate(vec_rows + [jnp.zeros((SUBLANES - 6, PACK_COLS), F32)], axis=0)
    per_chip = gather_small(blk, "gather_vectors")[0::2]
    ngw = ins["ssm_norm_gain"].shape[1]
    full = {
        "ssm_conv_w": jnp.concatenate([per_chip[k, 0:4] for k in range(N_CHIPS)], axis=1)[None],
        "ssm_conv_b": jnp.concatenate([per_chip[k, 4:5] for k in range(N_CHIPS)], axis=1),
        "ssm_norm_gain": jnp.concatenate([per_chip[k, 5:6, :ngw] for k in range(N_CHIPS)], axis=1),
    }

    plan = _Plan(ins, core, chip)
    full.update(plan.full)
    for n in REPLICATED:
        full[n] = ins[n]

    loss, dx, grads = local_step(x, target, full, plan)
    loss = lax.psum(loss, ALL_AXES)
    gshards = plan.shard_grads()

    small_shapes = [ins[n].shape for n in REPLICATED]
    vec_shapes = [grads[n].shape for n in EXACT]
    vec_pack = _pack([grads[n] for n in EXACT], F32, SUBLANES)
    gath = gather_small(jnp.concatenate([_pack([grads[n] for n in REPLICATED], F32, SUBLANES), vec_pack], axis=0),
                        "gather_small")
    packed = [jnp.concatenate([_pack([ins[pre + n] for n in REPLICATED], F32, SUBLANES), jnp.zeros_like(vec_pack)], axis=0)
              for pre in ("", "m_", "v_")]
    res = small_update(gath, *packed, name="small_update")
    nrep = res[0].shape[0] - vec_pack.shape[0]
    small = [dict(zip(REPLICATED, _unpack(r[:nrep].reshape(-1), small_shapes))) for r in res]
    vec_g = dict(zip(EXACT, _unpack(res[0][nrep:].reshape(-1), vec_shapes)))

    out_g, out_d, out_m, out_v = {}, {}, {}, {}
    for n in REPLICATED:
        out_g[n], out_d[n], out_m[n], out_v[n] = (s[n] for s in small)
    for n in SHARDED:
        shp = ins[n].shape
        if n in EXACT:
            g = lax.dynamic_slice_in_dim(vec_g[n], chip * shp[-1], shp[-1], axis=vec_g[n].ndim - 1)
        else:
            g = jnp.stack([gshards[(n, l)] for l in range(shp[0])])
        two = (math.prod(shp[:-1]), shp[-1])
        d2, m2, v2 = adamw(ins[n].reshape(two), g.reshape(two), ins["m_" + n].reshape(two),
                           ins["v_" + n].reshape(two), f"adamw_{n}")
        out_g[n], out_d[n], out_m[n], out_v[n] = g, d2.reshape(shp), m2.reshape(shp), v2.reshape(shp)
    return (loss, dx[None], *[out_g[n] for n in WEIGHTS], *[out_d[n] for n in WEIGHTS],
            *[out_m[n] for n in WEIGHTS], *[out_v[n] for n in WEIGHTS])


def kernel(x, mix_norm, ffn_norm, sb_w_qkv, sb_q_gain, sb_k_gain, sb_w_o, gm_w_in, gm_b_in, gm_v_gain, gm_w_s, gm_b_s, gm_w_out, ssm_w_in, ssm_conv_w, ssm_conv_b, ssm_dt_bias, ssm_a_log, ssm_d, ssm_norm_gain, ssm_w_out, ffn_w_gu, ffn_w_down, loss_target, m_mix_norm, m_ffn_norm, m_sb_w_qkv, m_sb_q_gain, m_sb_k_gain, m_sb_w_o, m_gm_w_in, m_gm_b_in, m_gm_v_gain, m_gm_w_s, m_gm_b_s, m_gm_w_out, m_ssm_w_in, m_ssm_conv_w, m_ssm_conv_b, m_ssm_dt_bias, m_ssm_a_log, m_ssm_d, m_ssm_norm_gain, m_ssm_w_out, m_ffn_w_gu, m_ffn_w_down, v_mix_norm, v_ffn_norm, v_sb_w_qkv, v_sb_q_gain, v_sb_k_gain, v_sb_w_o, v_gm_w_in, v_gm_b_in, v_gm_v_gain, v_gm_w_s, v_gm_b_s, v_gm_w_out, v_ssm_w_in, v_ssm_conv_w, v_ssm_conv_b, v_ssm_dt_bias, v_ssm_a_log, v_ssm_d, v_ssm_norm_gain, v_ssm_w_out, v_ffn_w_gu, v_ffn_w_down):
    return _step(dict(locals()))
```

```python
import functools
import math

import jax
import jax.numpy as jnp
from jax import lax
from jax.experimental import pallas as pl
from jax.experimental.pallas import tpu as pltpu

F32 = jnp.float32
BF16 = jnp.bfloat16
EPS = 1e-6
LANES = 128
SUBLANES = 8
VMEM_LIMIT = 56 * 1024 * 1024
HEAD = 64
CHUNK = 128
SB_TQ, SB_TK = 256, 256
SSD_SUB = 8
SB_DEAD = -110.0
SB_UNSEEN = -1e30
ADAM_LR, ADAM_B1, ADAM_B2, ADAM_EPS, ADAM_WD, ADAM_STEP = 0.001, 0.9, 0.999, 1e-08, 0.01, 10
MESH_ID = pl.DeviceIdType.MESH
ALL_AXES = ("x", "y", "c")


def _params(sem):
    return pltpu.CompilerParams(dimension_semantics=sem, vmem_limit_bytes=VMEM_LIMIT)


def _pick(n, cands):
    for c in cands:
        if n % c == 0:
            return c
    return n


def _dot(a, b, dims=((1,), (0,))):
    return lax.dot_general(a, b, (dims, ((), ())), preferred_element_type=F32)


def _dot_nt(a, b):
    return _dot(a, b, ((1,), (1,)))


def _dot_tn(a, b):
    return _dot(a, b, ((0,), (0,)))


def _split2(x):
    hi = x.astype(BF16)
    lo = (x - hi.astype(F32)).astype(BF16)
    return hi, lo


def _dot_x2(x, m):
    hi, lo = _split2(x)
    return _dot(hi, m) + _dot(lo, m)


def _dot_x3_left(m, x):
    h1 = x.astype(BF16)
    r1 = x - h1.astype(F32)
    h2 = r1.astype(BF16)
    h3 = (r1 - h2.astype(F32)).astype(BF16)
    return _dot(m, h1) + _dot(m, h2) + _dot(m, h3)


def _sigmoid(x):
    return 1.0 / (1.0 + jnp.exp(-x))


def _softplus(x):
    return jnp.maximum(x, 0.0) + jnp.log(1.0 + jnp.exp(-jnp.abs(x)))


def _colsum(x):
    return jnp.sum(x, axis=0, keepdims=True)


def _rowsum(x):
    return jnp.sum(x, axis=1, keepdims=True)


def _iota2(shape, dim):
    return lax.broadcasted_iota(jnp.int32, shape, dim)


MM_VMEM_BUDGET = 40 * 1024 * 1024
MM_STEP_US = 0.35
MM_HBM_BYTES_PER_US = 3.0e6
MM_VMEM_BYTES_PER_US = 1.5e6
MM_FLOPS_PER_US = 9.0e8
MXU_DIM = 256


def _mm_tiles(m, n, kk, wn, wk, a_bytes, b_bytes, has_add):
    def divisors(total, cands):
        got = [c for c in cands if total % c == 0 and c <= total]
        return got or [total]

    best = None
    for tm in divisors(m, (1024, 512, 256, 128)):
        for tn in divisors(wn, (1024, 768, 1408, 512, 256, 128)):
            for tk in divisors(wk, (4096, 2816, 2048, 1408, 1024, 768, 512, 256, 128)):
                nk = kk // tk
                vmem = 2 * (tm * tk * a_bytes + tk * tn * b_bytes + tm * tn * 4 * (2 if has_add else 1))
                vmem += tm * tn * 4 if nk > 1 else 0
                if vmem > MM_VMEM_BUDGET:
                    continue
                steps = (m // tm) * (n // tn) * nk
                a_reads = 1 if nk == 1 else n // tn
                traffic = m * kk * a_bytes * a_reads + kk * n * b_bytes * (m // tm) + m * n * 4
                fill = min(1.0, tn / MXU_DIM) * min(1.0, tm / MXU_DIM)
                compute = 2.0 * m * n * kk / (MM_FLOPS_PER_US * fill)
                cost = steps * MM_STEP_US + max(compute, traffic / MM_HBM_BYTES_PER_US)
                if nk > 1:
                    cost += steps * tm * tn * 8 / MM_VMEM_BYTES_PER_US
                if best is None or cost < best[0]:
                    best = (cost, tm, tn, tk)
    return best[1:]


def mm(a, b, *, ta=False, tb=False, add=None, bias=None, a_chunks=False, b_chunks=False, out_chunks=False,
       out_dtype=F32, name, side=None):
    wa = None
    if a_chunks:
        m, wa = a.shape[1], a.shape[2]
        kk = a.shape[0] * wa
    elif ta:
        kk, m = a.shape
    else:
        m, kk = a.shape
    nch, wide = 1, None
    if b_chunks:
        nch, rows_b, wide = b.shape
        kb, n = (rows_b, nch * wide) if not tb else (nch * wide, rows_b)
    elif tb:
        n, kb = b.shape
    else:
        kb, n = b.shape
    wide_o = n // N_CHIPS if out_chunks else None
    assert kk == kb, (a.shape, b.shape, ta, tb)
    has_add, has_bias = add is not None, bias is not None
    wk = wide if (wide and tb) else kk
    wn = wide if (wide and not tb) else n
    tm, tn, tk = _mm_tiles(m, n, kk, math.gcd(wn, wide_o) if wide_o else wn, math.gcd(wk, wa) if wa else wk,
                           a.dtype.itemsize, b.dtype.itemsize, has_add)
    nk = kk // tk
    dims = ((0 if ta else 1,), (1 if tb else 0,))

    def kern(*refs):
        a_ref, b_ref = refs[0], refs[1]
        rest = list(refs[2:])
        add_ref = rest.pop(0) if has_add else None
        bias_ref = rest.pop(0) if has_bias else None
        o_ref = rest[0]
        part = _dot(a_ref[...].astype(BF16), b_ref[...].astype(BF16), dims)

        def finish(r):
            if has_add:
                r = r + add_ref[...]
            if has_bias:
                r = r + bias_ref[...]
            o_ref[...] = r.astype(out_dtype)

        if nk == 1:
            finish(part)
        else:
            acc_ref = rest[1]
            k = pl.program_id(2)

            @pl.when(k == 0)
            def _():
                acc_ref[...] = part

            @pl.when((k > 0) & (k < nk - 1))
            def _():
                acc_ref[...] += part

            @pl.when(k == nk - 1)
            def _():
                finish(acc_ref[...] + part)

    if a_chunks:
        per_a = wa // tk
        a_spec = pl.BlockSpec((None, tm, tk), lambda i, j, k: (k // per_a, i, k % per_a))
    elif ta:
        a_spec = pl.BlockSpec((tk, tm), lambda i, j, k: (k, i))
    else:
        a_spec = pl.BlockSpec((tm, tk), lambda i, j, k: (i, k))
    if b_chunks and tb:
        per = wide // tk
        b_spec = pl.BlockSpec((None, tn, tk), lambda i, j, k: (k // per, j, k % per))
    elif b_chunks:
        per = wide // tn
        b_spec = pl.BlockSpec((None, tk, tn), lambda i, j, k: (j // per, k, j % per))
    elif tb:
        b_spec = pl.BlockSpec((tn, tk), lambda i, j, k: (j, k))
    else:
        b_spec = pl.BlockSpec((tk, tn), lambda i, j, k: (k, j))
    if out_chunks:
        per_o = wide_o // tn
        out_spec = pl.BlockSpec((None, tm, tn), lambda i, j, k: (j // per_o, i, j % per_o))
        out_shape = jax.ShapeDtypeStruct((N_CHIPS, m, wide_o), out_dtype)
    else:
        out_spec = pl.BlockSpec((tm, tn), lambda i, j, k: (i, j))
        out_shape = jax.ShapeDtypeStruct((m, n), out_dtype)
    in_specs, args = [a_spec, b_spec], [a, b]
    if has_add:
        in_specs.append(pl.BlockSpec((tm, tn), lambda i, j, k: (i, j)))
        args.append(add)
    if has_bias:
        in_specs.append(pl.BlockSpec((1, tn), lambda i, j, k: (0, j)))
        args.append(bias)
    (out,), side_outs = side_call(
        kern, side,
        name=name,
        grid=(m // tm, n // tn, nk),
        in_specs=in_specs,
        out_specs=[out_spec],
        out_shape=[out_shape],
        scratch_shapes=[pltpu.VMEM((tm, tn), F32)] if nk > 1 else [],
        args=args)
    return out if side is None else (out, side_outs)


def mm_hooked(plan, a, b, *, name, **kw):
    side = plan.side(name)
    if side is None:
        return mm(a, b, name=name, **kw)
    out, side_outs = mm(a, b, name=name, side=side, **kw)
    plan.done(name, side_outs)
    return out


def rowwise(fn, ins, outs, accs=(), *, tr, name):
    rows = [a for a, kind in ins if kind == "row"][0].shape[0]
    tr = min(tr, rows)
    assert rows % tr == 0 and tr % SUBLANES == 0, (rows, tr)
    n = rows // tr
    n_in, n_out = len(ins), len(outs)
    kinds = [kind for _, kind in ins]

    def kern(*refs):
        i = pl.program_id(0)
        vals = []
        for ref, kind in zip(refs[:n_in], kinds):
            v = ref[...]
            if kind == "prev":
                v = v * (i > 0).astype(v.dtype)
            elif kind == "next":
                v = v * (i < n - 1).astype(v.dtype)
            vals.append(v)
        res = fn(*vals)
        for ref, r in zip(refs[n_in:n_in + n_out], res[:n_out]):
            ref[...] = r.astype(ref.dtype)
        if accs:
            acc_refs = refs[n_in + n_out:]

            @pl.when(i == 0)
            def _():
                for ref in acc_refs:
                    ref[...] = jnp.zeros_like(ref)

            for ref, r in zip(acc_refs, res[n_out:]):
                ref[...] += r

    in_specs = []
    for a, kind in ins:
        if kind == "row":
            in_specs.append(pl.BlockSpec((tr, a.shape[1]), lambda i: (i, 0)))
        elif kind == "full":
            in_specs.append(pl.BlockSpec(a.shape, lambda i, nd=a.ndim: (0,) * nd))
        elif kind == "prev":
            in_specs.append(pl.BlockSpec((SUBLANES, a.shape[1]),
                                         lambda i: (jnp.maximum(i * (tr // SUBLANES) - 1, 0), 0)))
        else:
            in_specs.append(pl.BlockSpec((SUBLANES, a.shape[1]),
                                         lambda i: (jnp.minimum((i + 1) * (tr // SUBLANES), rows // SUBLANES - 1), 0)))
    out_specs = [pl.BlockSpec((tr, c), lambda i: (i, 0)) for c, _ in outs]
    out_specs += [pl.BlockSpec((r, c), lambda i: (0, 0)) for r, c in accs]
    out_shape = [jax.ShapeDtypeStruct((rows, c), dt) for c, dt in outs]
    out_shape += [jax.ShapeDtypeStruct((r, c), F32) for r, c in accs]
    res = pl.pallas_call(
        kern,
        name=name,
        grid=(n,),
        in_specs=in_specs,
        out_specs=out_specs,
        out_shape=out_shape,
        compiler_params=_params(("arbitrary",) if accs else ("parallel",)),
    )(*[a for a, _ in ins])
    return res


def rms_fwd(x, g, name):
    def fn(xv, gv):
        r = lax.rsqrt(jnp.mean(xv * xv, axis=1, keepdims=True) + EPS)
        return (xv * r * gv,)

    return rowwise(fn, [(x, "row"), (g, "full")], [(x.shape[1], BF16)], tr=1024, name=name)[0]


def rms_bwd(x, g, dy, dres, name):
    def fn(xv, gv, dyv, drv):
        r = lax.rsqrt(jnp.mean(xv * xv, axis=1, keepdims=True) + EPS)
        xh = xv * r
        dyg = dyv * gv
        dx = drv + r * (dyg - xh * jnp.mean(dyg * xh, axis=1, keepdims=True))
        return dx, dx, _colsum(dyv * xh)

    c = x.shape[1]
    dx, dxb, dg = rowwise(fn, [(x, "row"), (g, "full"), (dy, "row"), (dres, "row")], [(c, F32), (c, BF16)], [(1, c)],
                          tr=512, name=name)
    return (dx, dxb), dg


def ffn_up(h, wgu, name, side=None):
    s, d = h.shape
    nch, _, w = wgu.shape
    half = nch // 2
    tm = _pick(s, (512, 256, 128))

    def kern(h_ref, wg_ref, wu_ref, gu_ref, a_ref):
        hv = h_ref[...]
        g = _dot(hv, wg_ref[...])
        u = _dot(hv, wu_ref[...])
        gu_ref[0] = g.astype(BF16)
        gu_ref[1] = u.astype(BF16)
        a_ref[...] = (g * _sigmoid(g) * u).astype(BF16)

    return side_call(
        kern, side, name=name, grid=(s // tm, half),
        in_specs=[pl.BlockSpec((tm, d), lambda i, j: (i, 0)),
                  pl.BlockSpec((None, d, w), lambda i, j: (j, 0, 0)),
                  pl.BlockSpec((None, d, w), lambda i, j: (j + half, 0, 0))],
        out_specs=[pl.BlockSpec((2, tm, w), lambda i, j: (0, i, j)), pl.BlockSpec((tm, w), lambda i, j: (i, j))],
        out_shape=[jax.ShapeDtypeStruct((2, s, half * w), BF16), jax.ShapeDtypeStruct((s, half * w), BF16)],
        scratch_shapes=[], args=(h, wgu, wgu))


def ffn_dact(dxb, wdown, gu, name):
    s, d = dxb.shape
    hid = wdown.shape[0]
    tm = _pick(s, (512, 256, 128))
    tn = _pick(hid, (1408, 512, 256, 128))

    def kern(dx_ref, w_ref, gu_ref, o_ref):
        da = _dot_nt(dx_ref[...], w_ref[...])
        g, u = gu_ref[0].astype(F32), gu_ref[1].astype(F32)
        sg = _sigmoid(g)
        o_ref[0] = (da * u * sg * (1.0 + g * (1.0 - sg))).astype(BF16)
        o_ref[1] = (da * g * sg).astype(BF16)

    return pl.pallas_call(
        kern, name=name, grid=(s // tm, hid // tn),
        in_specs=[pl.BlockSpec((tm, d), lambda i, j: (i, 0)), pl.BlockSpec((tn, d), lambda i, j: (j, 0)),
                  pl.BlockSpec((2, tm, tn), lambda i, j: (0, i, j))],
        out_specs=pl.BlockSpec((2, tm, tn), lambda i, j: (0, i, j)),
        out_shape=jax.ShapeDtypeStruct((2, s, hid), BF16),
        compiler_params=_params(("parallel", "parallel")),
    )(dxb, wdown, gu)


def loss_and_grad(y, t, name):
    d = y.shape[1]

    def fn(yv, tv):
        e = yv - tv
        part = jnp.sum(_colsum(e * e), axis=1, keepdims=True) * (0.5 / d)
        dy = e * (1.0 / d)
        return dy, dy, jnp.broadcast_to(part, (SUBLANES, LANES))

    dy, dyb, acc = rowwise(fn, [(y, "row"), (t, "row")], [(d, F32), (d, BF16)], [(SUBLANES, LANES)], tr=1024, name=name)
    return acc[0, 0], (dy, dyb)


def adamw(w, g, m, v, name):
    def fn(wv, gv, mv, vv):
        m2 = ADAM_B1 * mv + (1.0 - ADAM_B1) * gv
        v2 = ADAM_B2 * vv + (1.0 - ADAM_B2) * (gv * gv)
        m_hat = m2 / (1.0 - ADAM_B1 ** ADAM_STEP)
        v_hat = v2 / (1.0 - ADAM_B2 ** ADAM_STEP)
        delta = -ADAM_LR * (m_hat / (jnp.sqrt(v_hat) + ADAM_EPS) + ADAM_WD * wv)
        return delta, m2, v2

    rows, c = w.shape
    tr = _pick(rows, (512, 256, 128, 64, 32, 16, 8)) if rows % SUBLANES == 0 else rows
    if rows % SUBLANES:
        return _whole(fn, [w, g, m, v], [(w.shape, F32)] * 3, name=name)
    return rowwise(fn, [(w, "row"), (g, "row"), (m, "row"), (v, "row")], [(c, F32)] * 3, tr=tr, name=name)


def _whole(fn, ins, outs, *, name):
    n_in = len(ins)

    def kern(*refs):
        res = fn(*[r[...] for r in refs[:n_in]])
        for ref, r in zip(refs[n_in:], res):
            ref[...] = r.astype(ref.dtype)

    return pl.pallas_call(
        kern,
        name=name,
        out_shape=[jax.ShapeDtypeStruct(s, dt) for s, dt in outs],
        compiler_params=pltpu.CompilerParams(vmem_limit_bytes=VMEM_LIMIT),
    )(*ins)


def ffn_fwd(x, g, wgu, wdown, tag, plan):
    h = rms_fwd(x, g, f"ffn_rms_{tag}")
    gu, a = _hooked(plan, f"ffn_gu_{tag}", ffn_up, h, wgu)
    xn = mm_hooked(plan, a, wdown, add=x, name=f"ffn_down_{tag}")
    return xn, (x, h, gu, a)


def ffn_bwd(dxn, saved, g, wgu, wdown, tag):
    x, h, gu, a = saved
    dxn, dxb = dxn
    dwdown = mm(a, dxb, ta=True, out_dtype=BF16, name=f"ffn_dwdown_{tag}")
    dgu = ffn_dact(dxb, wdown, gu, f"ffn_dact_{tag}")
    dh = mm(dgu, wgu, tb=True, a_chunks=True, b_chunks=True, name=f"ffn_dh_{tag}")
    dwgu = mm(h, dgu, ta=True, b_chunks=True, out_dtype=BF16, out_chunks=True, name=f"ffn_dwgu_{tag}")
    dx, dg = rms_bwd(x, g, dh, dxn, f"ffn_drms_{tag}")
    return dx, dg, dwgu, dwdown


def _head_blockdiag(c):
    i = jnp.arange(c) // HEAD
    return (i[:, None] == i[None, :]).astype(BF16)


def _head_sums(x, bd):
    return jnp.concatenate([_dot_x2(x[:, g * LANES:(g + 1) * LANES], bd) for g in range(x.shape[1] // LANES)], axis=1)


def qknorm_fwd(qkv, qg, kg, bd, name):
    d = qkv.shape[1] // 3
    scale = 1.0 / math.sqrt(HEAD)

    def fn(v, qgv, kgv, bdv):
        v = v.astype(F32)
        q, k, vv = v[:, :d], v[:, d:2 * d], v[:, 2 * d:]
        rq = lax.rsqrt(_head_sums(q * q, bdv) * (1.0 / HEAD) + EPS)
        rk = lax.rsqrt(_head_sums(k * k, bdv) * (1.0 / HEAD) + EPS)
        return q * rq * qgv * scale, k * rk * kgv, vv

    return rowwise(fn, [(qkv, "row"), (qg, "full"), (kg, "full"), (bd, "full")],
                   [(d, BF16), (d, BF16), (d, BF16)], tr=512, name=name)


def qknorm_bwd(qkv, dqs, dkn, dv, qg, kg, bd, name):
    d = qkv.shape[1] // 3
    scale = 1.0 / math.sqrt(HEAD)

    def one(xv, gv, dyv, bdv):
        r = lax.rsqrt(_head_sums(xv * xv, bdv) * (1.0 / HEAD) + EPS)
        xh = xv * r
        dyg = dyv * gv
        dx = r * (dyg - xh * (_head_sums(dyg * xh, bdv) * (1.0 / HEAD)))
        return dx, _colsum(dyv * xh)

    def fn(v, dqv, dkv, dvv, qgv, kgv, bdv):
        v = v.astype(F32)
        q, k = v[:, :d], v[:, d:2 * d]
        dq, dqg = one(q, qgv, dqv * scale, bdv)
        dk, dkg = one(k, kgv, dkv, bdv)
        return jnp.concatenate([dq, dk, dvv], axis=1), dqg, dkg

    return rowwise(fn, [(qkv, "row"), (dqs, "row"), (dkn, "row"), (dv, "row"), (qg, "full"), (kg, "full"), (bd, "full")],
                   [(3 * d, BF16)], [(1, d), (1, d)], tr=512, name=name)


def _sb_tile(qh, k, mask, tri_gt):
    z = _dot_nt(qh, k)
    sp = jnp.log(1.0 + jnp.exp(-jnp.abs(z)))
    lb = jnp.minimum(z, 0.0) - sp
    l1 = jnp.where(mask, lb - z, 0.0)
    suf = _dot(l1.astype(BF16), tri_gt)
    return lb, l1, suf


def _sb_tri(tk):
    i = jnp.arange(tk)
    return jnp.stack([i[:, None] > i[None, :], i[:, None] < i[None, :]]).astype(BF16)


def _sb_setup(tq, tk):
    row, col = _iota2((tq, tk), 0), _iota2((tq, tk), 1)
    lane = _iota2((1, LANES), 1)
    halves = [(lane < HEAD).astype(BF16), (lane >= HEAD).astype(BF16)]
    lane_q = _iota2((tq, LANES), 1) + jnp.minimum(_iota2((tq, LANES), 0), 0)
    return row, col, halves, lane_q


def sb_attn_fwd(qs, kn, vb, tri, name, side=None):
    s, d = qs.shape
    tq, tk = min(SB_TQ, s), min(SB_TK, s)
    nq = s // tq
    assert s // tk <= LANES and s % tq == 0 and s % tk == 0

    def kern(q_ref, k_ref, v_ref, tri_ref, o_ref, rs_ref, acc_ref):
        i = pl.program_id(1)
        row, col, halves, lane_q = _sb_setup(tq, tk)
        q = q_ref[...]
        qh = [q * hm for hm in halves]
        acc_ref[...] = jnp.zeros_like(acc_ref)
        rs_ref[...] = jnp.full(rs_ref.shape, SB_UNSEEN, F32)
        nkb = (i + 1) * (tq // tk)

        def more(st):
            return (st[0] < nkb) & (st[1] > SB_DEAD)

        def step(st):
            n, r = st[0], list(st[2:])
            kb = nkb - 1 - n
            ks = pl.multiple_of(kb * tk, tk)
            k = k_ref[pl.ds(ks, tk), :]
            v = v_ref[pl.ds(ks, tk), :]
            mask = col < row + (i * tq - kb * tk)
            at_kb = lane_q == kb
            for hh in range(2):
                lb, l1, suf = _sb_tile(qh[hh], k, mask, tri_ref[0])
                w = jnp.where(mask, jnp.exp(lb + suf + r[hh]), 0.0)
                acc_ref[...] += _dot(w.astype(BF16), v * halves[hh])
                rs_ref[hh] = jnp.where(at_kb, r[hh], rs_ref[hh])
                r[hh] = r[hh] + _rowsum(l1)
            return (n + 1, jnp.maximum(jnp.max(r[0]), jnp.max(r[1])), r[0], r[1])

        z1 = jnp.zeros((tq, 1), F32)
        lax.while_loop(more, step, (jnp.int32(0), jnp.float32(0.0), z1, z1))
        o_ref[...] = acc_ref[...].astype(BF16)

    nh2 = d // LANES
    return side_call(
        kern, side,
        name=name,
        grid=(nh2, nq),
        in_specs=[pl.BlockSpec((tq, LANES), lambda h, i: (i, h)),
                  pl.BlockSpec((s, LANES), lambda h, i: (0, h)),
                  pl.BlockSpec((s, LANES), lambda h, i: (0, h)),
                  pl.BlockSpec((2, tk, tk), lambda h, i: (0, 0, 0))],
        out_specs=[pl.BlockSpec((tq, LANES), lambda h, i: (i, h)),
                   pl.BlockSpec((None, 2, tq, LANES), lambda h, i: (h, 0, i, 0))],
        out_shape=[jax.ShapeDtypeStruct((s, d), BF16), jax.ShapeDtypeStruct((nh2, 2, s, LANES), F32)],
        scratch_shapes=[pltpu.VMEM((tq, LANES), F32)],
        args=(qs, kn, vb, tri))


def sb_attn_bwd(qs, kn, vb, rsave, do, tri, name, side=None):
    s, d = qs.shape
    tq, tk = min(SB_TQ, s), min(SB_TK, s)
    nq = s // tq

    def kern(q_ref, k_ref, v_ref, rs_ref, do_ref, tri_ref, dq_ref, dk_ref, dv_ref):
        i = pl.program_id(1)

        @pl.when(i == 0)
        def _():
            dk_ref[...] = jnp.zeros_like(dk_ref)
            dv_ref[...] = jnp.zeros_like(dv_ref)

        row, col, halves, lane_q = _sb_setup(tq, tk)
        q = q_ref[...]
        qh = [q * hm for hm in halves]
        dov = do_ref[...].astype(BF16)
        doh = [dov * hm for hm in halves]
        dq_ref[...] = jnp.zeros_like(dq_ref)
        nkb = (i + 1) * (tq // tk)
        top = jnp.maximum(jnp.max(rs_ref[0], axis=0, keepdims=True), jnp.max(rs_ref[1], axis=0, keepdims=True))
        dead = (top <= SB_DEAD) & (_iota2((1, LANES), 1) < nkb)
        kstart = jnp.minimum(jnp.sum(dead.astype(F32)).astype(jnp.int32), nkb)

        def step(kb, ep):
            ep = list(ep)
            ks = pl.multiple_of(kb * tk, tk)
            k = k_ref[pl.ds(ks, tk), :]
            v = v_ref[pl.ds(ks, tk), :]
            mask = col < row + (i * tq - kb * tk)
            at_kb = lane_q == kb
            for hh in range(2):
                lb, l1, suf = _sb_tile(qh[hh], k, mask, tri_ref[0])
                r = _rowsum(jnp.where(at_kb, rs_ref[hh], 0.0))
                lbm = jnp.where(mask, lb, SB_UNSEEN)
                w = jnp.exp(lbm + suf + r)
                e = _dot_nt(doh[hh], v) * w
                pe = ep[hh] + _dot(e.astype(BF16), tri_ref[1])
                beta = jnp.exp(lbm)
                dz = (e - beta * (e + pe)).astype(BF16)
                dq_ref[...] += _dot(dz, k * halves[hh])
                dk_ref[pl.ds(ks, tk), :] += _dot_tn(dz, qh[hh])
                dv_ref[pl.ds(ks, tk), :] += _dot_tn(w.astype(BF16), doh[hh])
                ep[hh] = ep[hh] + _rowsum(e)
            return tuple(ep)

        z1 = jnp.zeros((tq, 1), F32)
        lax.fori_loop(kstart, nkb, step, (z1, z1))

    nh2 = d // LANES
    return side_call(
        kern, side,
        name=name,
        grid=(nh2, nq),
        in_specs=[pl.BlockSpec((tq, LANES), lambda h, i: (i, h)),
                  pl.BlockSpec((s, LANES), lambda h, i: (0, h)),
                  pl.BlockSpec((s, LANES), lambda h, i: (0, h)),
                  pl.BlockSpec((None, 2, tq, LANES), lambda h, i: (h, 0, i, 0)),
                  pl.BlockSpec((tq, LANES), lambda h, i: (i, h)),
                  pl.BlockSpec((2, tk, tk), lambda h, i: (0, 0, 0))],
        out_specs=[pl.BlockSpec((tq, LANES), lambda h, i: (i, h)),
                   pl.BlockSpec((s, LANES), lambda h, i: (0, h)),
                   pl.BlockSpec((s, LANES), lambda h, i: (0, h))],
        out_shape=[jax.ShapeDtypeStruct((s, d), F32)] * 3,
        scratch_shapes=[],
        args=(qs, kn, vb, rsave, do, tri))


def _hooked(plan, tag, call, *args):
    side = plan.side(tag)
    outs, side_outs = call(*args, tag, side)
    if side is not None:
        plan.done(tag, side_outs)
    return outs


def sb_fwd(x, g, wqkv, qg, kg, wo, bd, tag, plan):
    h = rms_fwd(x, g, f"sb_rms_{tag}")
    qkv = mm(h, wqkv, b_chunks=True, out_dtype=BF16, name=f"sb_qkv_{tag}")
    qs, kn, vb = qknorm_fwd(qkv, qg, kg, bd, f"sb_qknorm_{tag}")
    o, rsave = _hooked(plan, f"sb_attn_{tag}", sb_attn_fwd, qs, kn, vb, _sb_tri(min(SB_TK, x.shape[0])))
    xn = mm(o, wo(), add=x, name=f"sb_out_{tag}")
    return xn, (x, h, qkv, qs, kn, vb, rsave, o)


def sb_bwd(dxn, saved, g, wqkv, qg, kg, wo, bd, tag, plan):
    x, h, qkv, qs, kn, vb, rsave, o = saved
    dxn, dxb = dxn
    do = mm(dxb, wo, tb=True, name=f"sb_do_{tag}")
    dwo = mm(o, dxb, ta=True, out_dtype=BF16, name=f"sb_dwo_{tag}")
    dqs, dkn, dv = _hooked(plan, f"sb_dattn_{tag}", sb_attn_bwd, qs, kn, vb, rsave, do, _sb_tri(min(SB_TK, x.shape[0])))
    dqkv, dqg, dkg = qknorm_bwd(qkv, dqs, dkn, dv, qg, kg, bd, f"sb_dqknorm_{tag}")
    dh = mm(dqkv, wqkv, tb=True, b_chunks=True, name=f"sb_dh_{tag}")
    dwqkv = mm(h, dqkv, ta=True, out_dtype=BF16, out_chunks=True, name=f"sb_dwqkv_{tag}")
    dx, dg = rms_bwd(x, g, dh, dxn, f"sb_drms_{tag}")
    nh = dqg.shape[1] // HEAD
    return dx, dg, dwqkv, dqg.reshape(nh, HEAD).sum(0), dkg.reshape(nh, HEAD).sum(0), dwo


def _gelu(x):
    return 0.5 * x * (1.0 + lax.erf(x * (1.0 / math.sqrt(2.0))))


def _gelu_grad(x):
    return 0.5 * (1.0 + lax.erf(x * (1.0 / math.sqrt(2.0)))) + x * jnp.exp(-0.5 * x * x) * (1.0 / math.sqrt(2.0 * math.pi))


def gm_act_fwd(pre, vg, name):
    half = pre.shape[1] // 2

    def fn(p, vgv):
        p = p.astype(F32)
        u = _gelu(p[:, :half])
        v = _gelu(p[:, half:])
        r = lax.rsqrt(jnp.mean(v * v, axis=1, keepdims=True) + EPS)
        return u, v * r * vgv

    return rowwise(fn, [(pre, "row"), (vg, "full")], [(half, F32), (half, BF16)], tr=512, name=name)


def gm_act_bwd(pre, du, dvn, vg, name):
    half = pre.shape[1] // 2

    def fn(p, duv, dvnv, vgv):
        p = p.astype(F32)
        pu, pv = p[:, :half], p[:, half:]
        v = _gelu(pv)
        r = lax.rsqrt(jnp.mean(v * v, axis=1, keepdims=True) + EPS)
        vh = v * r
        dyg = dvnv * vgv
        dv = r * (dyg - vh * jnp.mean(dyg * vh, axis=1, keepdims=True))
        dpre = jnp.concatenate([duv * _gelu_grad(pu), dv * _gelu_grad(pv)], axis=1)
        return dpre, _colsum(dvnv * vh), _colsum(dpre)

    return rowwise(fn, [(pre, "row"), (du, "row"), (dvn, "row"), (vg, "full")],
                   [(2 * half, BF16)], [(1, half), (1, 2 * half)], tr=256, name=name)


def gm_spatial_fwd(u, vn, wc, bst, name):
    s, c = u.shape
    t = CHUNK
    ng = c // LANES

    def kern(u_ref, v_ref, w_ref, b_ref, o_ref):
        for g in range(ng):
            sl = slice(g * LANES, (g + 1) * LANES)
            mixed = _dot(w_ref[g], v_ref[:, sl]) + b_ref[:, sl]
            o_ref[:, sl] = (u_ref[:, sl] * mixed).astype(BF16)

    return pl.pallas_call(
        kern,
        name=name,
        grid=(s // t,),
        in_specs=[pl.BlockSpec((t, c), lambda i: (i, 0)), pl.BlockSpec((t, c), lambda i: (i, 0)),
                  pl.BlockSpec(wc.shape, lambda i: (0, 0, 0)), pl.BlockSpec(bst.shape, lambda i: (0, 0))],
        out_specs=pl.BlockSpec((t, c), lambda i: (i, 0)),
        out_shape=jax.ShapeDtypeStruct((s, c), BF16),
        compiler_params=_params(("parallel",)),
    )(u, vn, wc, bst)


def gm_spatial_bwd(dgate, u, vn, wc, bst, name):
    s, c = u.shape
    t = CHUNK
    ng = c // LANES

    def kern(dg_ref, u_ref, v_ref, w_ref, b_ref, du_ref, dv_ref, dw_ref, db_ref):
        i = pl.program_id(0)

        @pl.when(i == 0)
        def _():
            dw_ref[...] = jnp.zeros_like(dw_ref)
            db_ref[...] = jnp.zeros_like(db_ref)

        for g in range(ng):
            sl = slice(g * LANES, (g + 1) * LANES)
            vg = v_ref[:, sl]
            dgv = dg_ref[:, sl]
            mixed = _dot(w_ref[g], vg) + b_ref[:, sl]
            du_ref[:, sl] = dgv * mixed
            dmix = dgv * u_ref[:, sl]
            dmb = dmix.astype(BF16)
            dv_ref[:, sl] = _dot_tn(w_ref[g], dmb)
            dw_ref[g] += _dot_nt(dmb, vg)
            db_ref[:, sl] += dmix

    return pl.pallas_call(
        kern,
        name=name,
        grid=(s // t,),
        in_specs=[pl.BlockSpec((t, c), lambda i: (i, 0))] * 3 +
                 [pl.BlockSpec(wc.shape, lambda i: (0, 0, 0)), pl.BlockSpec(bst.shape, lambda i: (0, 0))],
        out_specs=[pl.BlockSpec((t, c), lambda i: (i, 0)), pl.BlockSpec((t, c), lambda i: (i, 0)),
                   pl.BlockSpec(wc.shape, lambda i: (0, 0, 0)), pl.BlockSpec(bst.shape, lambda i: (0, 0))],
        out_shape=[jax.ShapeDtypeStruct((s, c), F32), jax.ShapeDtypeStruct((s, c), F32),
                   jax.ShapeDtypeStruct(wc.shape, F32), jax.ShapeDtypeStruct(bst.shape, F32)],
        compiler_params=_params(("arbitrary",)),
    )(dgate, u, vn, wc, bst)


def gm_fwd(x, g, w_in, b_in, vg, wc, bst, w_out, tag):
    h = rms_fwd(x, g, f"gm_rms_{tag}")
    pre = mm(h, w_in, bias=b_in, b_chunks=True, out_dtype=BF16, name=f"gm_in_{tag}")
    u, vn = gm_act_fwd(pre, vg, f"gm_act_{tag}")
    gate = gm_spatial_fwd(u, vn, wc, bst, f"gm_spatial_{tag}")
    xn = mm(gate, w_out, add=x, name=f"gm_out_{tag}")
    return xn, (x, h, pre, u, vn, gate)


def gm_bwd(dxn, saved, g, w_in, vg, wc, bst, w_out, tag):
    x, h, pre, u, vn, gate = saved
    dxn, dxb = dxn
    dgate = mm(dxb, w_out, tb=True, name=f"gm_dgate_{tag}")
    dwout = mm(gate, dxb, ta=True, out_dtype=BF16, name=f"gm_dwout_{tag}")
    du, dvn, dws, dbst = gm_spatial_bwd(dgate, u, vn, wc, bst, f"gm_dspatial_{tag}")
    dpre, dvg, dbin = gm_act_bwd(pre, du, dvn, vg, f"gm_dact_{tag}")
    dh = mm(dpre, w_in, tb=True, b_chunks=True, name=f"gm_dh_{tag}")
    dwin = mm(h, dpre, ta=True, out_dtype=BF16, out_chunks=True, name=f"gm_dwin_{tag}")
    dx, dg = rms_bwd(x, g, dh, dxn, f"gm_drms_{tag}")
    ng = wc.shape[0]
    dws = jnp.where(jnp.tril(jnp.ones((CHUNK, CHUNK), bool)), dws, 0.0)
    dbs = dbst.reshape(CHUNK, ng, LANES).sum(-1).T
    return dx, dg, dwin, dbin, dvg, dws, dbs, dwout


def _conv_taps(xv, prev):
    cat = jnp.concatenate([prev, xv], axis=0)
    return [pltpu.roll(cat, sh, 0)[SUBLANES:] for sh in (3, 2, 1)] + [xv]


def conv_fwd(xbc, ws, b, d_inner, name):
    c = xbc.shape[1]
    nst = (c - d_inner) // 2

    def fn(xv, prev, w0, w1, w2, w3, bv):
        taps = _conv_taps(xv, prev)
        pre = bv + w0 * taps[0] + w1 * taps[1] + w2 * taps[2] + w3 * taps[3]
        out = pre * _sigmoid(pre)
        return out[:, :d_inner], out[:, d_inner:d_inner + nst], out[:, d_inner + nst:]

    return rowwise(fn, [(xbc, "row"), (xbc, "prev")] + [(w, "full") for w in ws] + [(b, "full")],
                   [(d_inner, F32), (nst, F32), (nst, F32)], tr=512, name=name)


def conv_bwd_pre(xbc, ws, b, dxs_a, dxs_b, db_m, dc_m, name):
    c = xbc.shape[1]

    def fn(xv, prev, w0, w1, w2, w3, bv, da, db2, dbm, dcm):
        taps = _conv_taps(xv, prev)
        pre = bv + w0 * taps[0] + w1 * taps[1] + w2 * taps[2] + w3 * taps[3]
        sg = _sigmoid(pre)
        dout = jnp.concatenate([da + db2, dbm, dcm], axis=1)
        dpre = dout * sg * (1.0 + pre * (1.0 - sg))
        return (dpre,) + tuple(_colsum(dpre * tp) for tp in taps) + (_colsum(dpre),)

    return rowwise(fn, [(xbc, "row"), (xbc, "prev")] + [(w, "full") for w in ws] +
                   [(b, "full"), (dxs_a, "row"), (dxs_b, "row"), (db_m, "row"), (dc_m, "row")],
                   [(c, F32)], [(1, c)] * 5, tr=256, name=name)


def conv_bwd_in(dpre, ws, name):
    c = dpre.shape[1]

    def fn(dv, nxt, w0, w1, w2, w3):
        cat = jnp.concatenate([dv, nxt], axis=0)
        n = cat.shape[0]
        up = [pltpu.roll(cat, n - sh, 0)[:dv.shape[0]] for sh in (1, 2, 3)]
        return (w3 * dv + w2 * up[0] + w1 * up[1] + w0 * up[2],)

    return rowwise(fn, [(dpre, "row"), (dpre, "next")] + [(w, "full") for w in ws], [(c, BF16)], tr=512, name=name)[0]


def ssd_pre(dtr, bias, alog, name):
    def fn(d, bv, al, tri):
        dt = _softplus(d + bv)
        a = dt * (-jnp.exp(al))
        return dt, _dot_x3_left(tri, a)

    tri = jnp.tril(jnp.ones((CHUNK, CHUNK), BF16))
    return rowwise(fn, [(dtr, "row"), (bias, "full"), (alog, "full"), (tri, "full")],
                   [(LANES, F32), (LANES, F32)], tr=CHUNK, name=name)


def _ssd_layouts(v, ngroups, hpg):
    s = v.shape[0]
    col = v[:, :ngroups * hpg].T.reshape(ngroups, hpg, s, 1)
    return jnp.broadcast_to(col, (ngroups, hpg, s, LANES))


def _ssd_rowform(acum, ngroups, hpg):
    s = acum.shape[0]
    nc = s // CHUNK
    a = acum[:, :ngroups * hpg].reshape(nc, CHUNK, ngroups, hpg).transpose(2, 0, 3, 1)
    last = jnp.broadcast_to(a[..., CHUNK - 1:], a.shape)
    return jnp.concatenate([a, last], axis=2)


def ssd_chunk_fwd(xs, bm, cm, col_a, col_dt, rowf, name, side=None):
    s, d_inner = xs.shape
    ln = CHUNK
    nc = s // ln
    nsub = _pick(nc, (SSD_SUB, 2, 1))
    rows = nsub * ln
    ng, hpg = col_a.shape[0], col_a.shape[1]
    gw = d_inner // ng
    assert gw == hpg * HEAD and gw % LANES == 0 and bm.shape[1] == ng * LANES

    def kern(x_ref, b_ref, c_ref, ca_ref, cd_ref, rf_ref, y_ref, hp_ref, h_scr):
        @pl.when(pl.program_id(1) == 0)
        def _():
            h_scr[...] = jnp.zeros_like(h_scr)

        causal = _iota2((ln, ln), 0) >= _iota2((ln, ln), 1)
        lane = _iota2((1, LANES), 1)
        for sc in range(nsub):
            rs = slice(sc * ln, (sc + 1) * ln)
            bb = b_ref[rs, :].astype(BF16)
            cbf = c_ref[rs, :].astype(BF16)
            cb = _dot_nt(cbf, bb)
            ys = [jnp.zeros((ln, LANES), F32) for _ in range(gw // LANES)]
            for r in range(hpg):
                j, hf = divmod(r, LANES // HEAD)
                mh = ((lane >= HEAD * hf) & (lane < HEAD * (hf + 1))).astype(F32)
                ac = ca_ref[r, rs, :]
                ar = rf_ref[sc, pl.ds(r, 1), :]
                aend = rf_ref[sc, pl.ds(4 + r, 1), :]
                dm = jnp.exp(jnp.minimum(ac - ar, 0.0))
                m = jnp.where(causal, cb * dm, 0.0).astype(BF16)
                xdt = x_ref[rs, j * LANES:(j + 1) * LANES] * cd_ref[r, rs, :] * mh
                h = h_scr[r]
                hp_ref[sc, r] = h
                ys[j] = ys[j] + _dot(m, xdt.astype(BF16)) + _dot_nt(cbf, h.astype(BF16)) * jnp.exp(ac)
                dte = jnp.exp(aend - ac)
                h_scr[r] = jnp.exp(aend) * h + _dot_tn((xdt * dte).astype(BF16), bb)
            for j in range(gw // LANES):
                y_ref[rs, j * LANES:(j + 1) * LANES] = ys[j]

    colspec = pl.BlockSpec((None, hpg, rows, LANES), lambda g, c: (g, 0, c, 0))
    return side_call(
        kern, side,
        name=name,
        grid=(ng, nc // nsub),
        in_specs=[pl.BlockSpec((rows, gw), lambda g, c: (c, g)),
                  pl.BlockSpec((rows, LANES), lambda g, c: (c, g)),
                  pl.BlockSpec((rows, LANES), lambda g, c: (c, g)),
                  colspec, colspec,
                  pl.BlockSpec((None, nsub, 8, LANES), lambda g, c: (g, c, 0, 0))],
        out_specs=[pl.BlockSpec((rows, gw), lambda g, c: (c, g)),
                   pl.BlockSpec((None, nsub, hpg, LANES, LANES), lambda g, c: (g, c, 0, 0, 0))],
        out_shape=[jax.ShapeDtypeStruct((s, d_inner), F32),
                   jax.ShapeDtypeStruct((ng, nc, hpg, LANES, LANES), F32)],
        scratch_shapes=[pltpu.VMEM((hpg, LANES, LANES), F32)],
        args=(xs, bm, cm, col_a, col_dt, rowf))


def ssd_chunk_bwd(xs, bm, cm, col_a, col_dt, rowf, hprev, dy, name, side=None):
    s, d_inner = xs.shape
    ln = CHUNK
    nc = s // ln
    nsub = _pick(nc, (SSD_SUB, 2, 1))
    rows = nsub * ln
    ng, hpg = col_a.shape[0], col_a.shape[1]
    gw = d_inner // ng

    def kern(x_ref, b_ref, c_ref, ca_ref, cd_ref, rf_ref, hp_ref, dy_ref,
             dx_ref, db_ref, dc_ref, ddt_ref, da_ref, dh_scr):
        @pl.when(pl.program_id(1) == 0)
        def _():
            dh_scr[...] = jnp.zeros_like(dh_scr)

        row, col = _iota2((ln, ln), 0), _iota2((ln, ln), 1)
        causal = row >= col
        tri_ge = (col >= row).astype(BF16)
        ones = jnp.ones((ln, LANES), BF16)
        lane = _iota2((1, LANES), 1)
        last_row = (_iota2((ln, 1), 0) == ln - 1).astype(F32)
        for sc in reversed(range(nsub)):
            rs = slice(sc * ln, (sc + 1) * ln)
            bb = b_ref[rs, :].astype(BF16)
            cbf = c_ref[rs, :].astype(BF16)
            cb = _dot_nt(cbf, bb)
            dcb = jnp.zeros((ln, ln), F32)
            d_b = jnp.zeros((ln, LANES), F32)
            d_c = jnp.zeros((ln, LANES), F32)
            dxs = [jnp.zeros((ln, LANES), F32) for _ in range(gw // LANES)]
            for r in range(hpg):
                j, hf = divmod(r, LANES // HEAD)
                mh = ((lane >= HEAD * hf) & (lane < HEAD * (hf + 1))).astype(F32)
                ac = ca_ref[r, rs, :]
                dt = cd_ref[r, rs, :]
                ar = rf_ref[sc, pl.ds(r, 1), :]
                aend = rf_ref[sc, pl.ds(4 + r, 1), :]
                dm = jnp.where(causal, jnp.exp(jnp.minimum(ac - ar, 0.0)), 0.0)
                m = cb * dm
                mb = m.astype(BF16)
                xp = x_ref[rs, j * LANES:(j + 1) * LANES]
                xdt = xp * dt * mh
                xdtb = xdt.astype(BF16)
                dyp = dy_ref[rs, j * LANES:(j + 1) * LANES] * mh
                dypb = dyp.astype(BF16)
                h = hp_ref[sc, r]
                hb = h.astype(BF16)
                dh = dh_scr[r]
                dhb = dh.astype(BF16)
                e_in = jnp.exp(ac)
                dte = jnp.exp(aend - ac)
                eend = jnp.exp(aend)
                d_m = _dot_nt(dypb, xdtb)
                dcb = dcb + d_m * dm
                gm = d_m * m
                yoff_pre = _dot_nt(cbf, hb)
                bdh = _dot_nt(bb, dhb)
                dxdt = _dot_tn(mb, dypb) + bdh * dte
                t1 = _rowsum(xdt * bdh) * dte
                gh, gl = _split2(gm)
                dacum = (_rowsum(gm) - (_dot_tn(gh, ones) + _dot_tn(gl, ones))
                         + _rowsum(dyp * yoff_pre) * e_in - t1)
                end_term = _colsum(t1) + eend * jnp.sum(_colsum(dh * h), axis=1, keepdims=True)
                dacum = dacum + last_row * end_term
                da_ref[r, rs, :] = _dot_x3_left(tri_ge, dacum)
                ddt_ref[r, rs, :] = jnp.broadcast_to(_rowsum(dxdt * xp), (ln, LANES))
                dxs[j] = dxs[j] + dxdt * dt
                d_b = d_b + _dot((xdt * dte).astype(BF16), dhb)
                dye = (dyp * e_in).astype(BF16)
                d_c = d_c + _dot(dye, hb)
                dh_scr[r] = eend * dh + _dot_tn(dye, cbf)
            dcbb = dcb.astype(BF16)
            dc_ref[rs, :] = d_c + _dot(dcbb, bb)
            db_ref[rs, :] = d_b + _dot_tn(dcbb, cbf)
            for j in range(gw // LANES):
                dx_ref[rs, j * LANES:(j + 1) * LANES] = dxs[j]

    rev = nc // nsub - 1
    colspec = pl.BlockSpec((None, hpg, rows, LANES), lambda g, c: (g, 0, rev - c, 0))
    return side_call(
        kern, side,
        name=name,
        grid=(ng, nc // nsub),
        in_specs=[pl.BlockSpec((rows, gw), lambda g, c: (rev - c, g)),
                  pl.BlockSpec((rows, LANES), lambda g, c: (rev - c, g)),
                  pl.BlockSpec((rows, LANES), lambda g, c: (rev - c, g)),
                  colspec, colspec,
                  pl.BlockSpec((None, nsub, 8, LANES), lambda g, c: (g, rev - c, 0, 0)),
                  pl.BlockSpec((None, nsub, hpg, LANES, LANES), lambda g, c: (g, rev - c, 0, 0, 0)),
                  pl.BlockSpec((rows, gw), lambda g, c: (rev - c, g))],
        out_specs=[pl.BlockSpec((rows, gw), lambda g, c: (rev - c, g)),
                   pl.BlockSpec((rows, LANES), lambda g, c: (rev - c, g)),
                   pl.BlockSpec((rows, LANES), lambda g, c: (rev - c, g)),
                   colspec, colspec],
        out_shape=[jax.ShapeDtypeStruct((s, d_inner), F32),
                   jax.ShapeDtypeStruct(bm.shape, F32), jax.ShapeDtypeStruct(cm.shape, F32),
                   jax.ShapeDtypeStruct(col_a.shape, F32), jax.ShapeDtypeStruct(col_a.shape, F32)],
        scratch_shapes=[pltpu.VMEM((hpg, LANES, LANES), F32)],
        args=(xs, bm, cm, col_a, col_dt, rowf, hprev, dy))


def gnorm_fwd(y, xs, z, dexp, gain, ngroups, name):
    c = y.shape[1]
    gw = c // ngroups

    def fn(yv, xv, zv, dv, gv):
        yg = (yv + xv * dv) * (zv * _sigmoid(zv))
        outs = []
        for k in range(ngroups):
            t = yg[:, k * gw:(k + 1) * gw]
            outs.append(t * lax.rsqrt(jnp.mean(t * t, axis=1, keepdims=True) + EPS))
        return (jnp.concatenate(outs, axis=1) * gv,)

    return rowwise(fn, [(y, "row"), (xs, "row"), (z, "row"), (dexp, "full"), (gain, "full")], [(c, BF16)], tr=512, name=name)[0]


def gnorm_bwd(dn, y, xs, z, dexp, gain, ngroups, name):
    c = y.shape[1]
    gw = c // ngroups

    def fn(dnv, yv, xv, zv, dv, gv):
        yd = yv + xv * dv
        sg = _sigmoid(zv)
        sz = zv * sg
        yg = yd * sz
        dng = dnv * gv
        dyg, yh = [], []
        for k in range(ngroups):
            sl = slice(k * gw, (k + 1) * gw)
            t = yg[:, sl]
            r = lax.rsqrt(jnp.mean(t * t, axis=1, keepdims=True) + EPS)
            th = t * r
            dyg.append(r * (dng[:, sl] - th * jnp.mean(dng[:, sl] * th, axis=1, keepdims=True)))
            yh.append(th)
        dyg = jnp.concatenate(dyg, axis=1)
        yh = jnp.concatenate(yh, axis=1)
        dyd = dyg * sz
        dz = dyg * yd * (sg * (1.0 + zv * (1.0 - sg)))
        return dyd, dyd * dv, dz, _colsum(dyd * xv), _colsum(dnv * yh)

    return rowwise(fn, [(dn, "row"), (y, "row"), (xs, "row"), (z, "row"), (dexp, "full"), (gain, "full")],
                   [(c, F32), (c, F32), (c, BF16)], [(1, c), (1, c)], tr=256, name=name)


def ssd_post(ddt, da, dt, dtr, bias, alog, name):
    def fn(ddtv, dav, dtv, dtrv, bv, al):
        a_neg = -jnp.exp(al)
        ddtr = (ddtv + dav * a_neg) * _sigmoid(dtrv + bv)
        return ddtr, _colsum(ddtr), _colsum(dav * dtv) * a_neg

    return rowwise(fn, [(ddt, "row"), (da, "row"), (dt, "row"), (dtr, "row"), (bias, "full"), (alog, "full")],
                   [(LANES, BF16)], [(1, LANES), (1, LANES)], tr=512, name=name)


def _from_colform(v, s):
    ng, hpg = v.shape[0], v.shape[1]
    flat = v[..., 0].reshape(ng * hpg, s).T
    return jnp.pad(flat, ((0, 0), (0, LANES - ng * hpg)))


def ssm_fwd(x, g, p, tag, plan):
    ng, hpg, d_inner = p["ng"], p["hpg"], p["d_inner"]
    h = rms_fwd(x, g, f"ssm_rms_{tag}")
    z = mm(h, p["w_z"], name=f"ssm_inz_{tag}")
    xbc = mm(h, p["w_xbc"], name=f"ssm_inx_{tag}")
    dtr = mm(h, p["w_dt"], name=f"ssm_indt_{tag}")
    xs, bm, cm = conv_fwd(xbc, p["conv_w"], p["conv_b"], d_inner, f"ssm_conv_{tag}")
    dt, acum = ssd_pre(dtr, p["dt_bias"], p["a_log"], f"ssm_pre_{tag}")
    col_a, col_dt = _ssd_layouts(acum, ng, hpg), _ssd_layouts(dt, ng, hpg)
    rowf = _ssd_rowform(acum, ng, hpg)
    y, hprev = _hooked(plan, f"ssm_scan_{tag}", ssd_chunk_fwd, xs, bm, cm, col_a, col_dt, rowf)
    n = gnorm_fwd(y, xs, z, p["d_exp"], p["norm_gain"], ng, f"ssm_gnorm_{tag}")
    xn = mm(n, p["w_out"], add=x, name=f"ssm_out_{tag}")
    return xn, (x, h, z, xbc, dtr, xs, bm, cm, dt, col_a, col_dt, rowf, y, hprev, n)


def ssm_bwd(dxn, saved, g, p, tag, plan):
    x, h, z, xbc, dtr, xs, bm, cm, dt, col_a, col_dt, rowf, y, hprev, n = saved
    ng, hpg, d_inner = p["ng"], p["hpg"], p["d_inner"]
    s = x.shape[0]
    dxn, dxb = dxn
    dn = mm(dxb, p["w_out"], tb=True, name=f"ssm_dn_{tag}")
    dwout = mm(n, dxb, ta=True, out_dtype=BF16, name=f"ssm_dwout_{tag}")
    dy, dxs_skip, dz, dd_lane, dgain = gnorm_bwd(dn, y, xs, z, p["d_exp"], p["norm_gain"], ng, f"ssm_dgnorm_{tag}")
    dxs, dbm, dcm, ddt_c, da_c = _hooked(plan, f"ssm_dscan_{tag}", ssd_chunk_bwd, xs, bm, cm, col_a, col_dt, rowf, hprev, dy)
    ddtr, dbias, dalog = ssd_post(_from_colform(ddt_c, s), _from_colform(da_c, s), dt, dtr,
                                  p["dt_bias"], p["a_log"], f"ssm_post_{tag}")
    res = conv_bwd_pre(xbc, p["conv_w"], p["conv_b"], dxs, dxs_skip, dbm, dcm, f"ssm_dconv_{tag}")
    dpre, dconv_w, dconv_b = res[0], jnp.concatenate(res[1:5], axis=0), res[5]
    dxbc = conv_bwd_in(dpre, p["conv_w"], f"ssm_dconvin_{tag}")
    dh = mm(dz, p["w_z"], tb=True, name=f"ssm_dhz_{tag}")
    dh = mm(dxbc, p["w_xbc"], tb=True, add=dh, name=f"ssm_dhx_{tag}")
    dh = mm(ddtr, p["w_dt"], tb=True, add=dh, name=f"ssm_dhdt_{tag}")
    dwz = mm(h, dz, ta=True, out_dtype=BF16, name=f"ssm_dwz_{tag}")
    dwxbc = mm(h, dxbc, ta=True, out_dtype=BF16, name=f"ssm_dwxbc_{tag}")
    dwdt = mm(h, ddtr, ta=True, out_dtype=BF16, name=f"ssm_dwdt_{tag}")
    dx, dg = rms_bwd(x, g, dh, dxn, f"ssm_drms_{tag}")
    nh = ng * hpg
    dwin = jnp.concatenate([dwz, dwxbc, dwdt[:, :nh]], axis=1)
    dd = dd_lane.reshape(nh, HEAD).sum(-1)
    return dx, dg, dict(w_in=dwin, conv_w=dconv_w, conv_b=dconv_b, dt_bias=dbias[0, :nh], a_log=dalog[0, :nh],
                        d=dd, norm_gain=dgain, w_out=dwout)


def local_step(x, target, w, plan):
    d = x.shape[1]
    depth = w["mix_norm"].shape[0]
    bd = _head_blockdiag(LANES)
    tril = jnp.tril(jnp.ones((CHUNK, CHUNK), bool))
    ssm_heads = w["ssm_dt_bias"].shape[1]
    d_inner = w["ssm_norm_gain"].shape[1]
    ng = w["ssm_norm_gain"].shape[1] // 256
    nstate = CHUNK

    def pad_lanes(v):
        return jnp.pad(v, ((0, 0), (0, LANES - v.shape[1])))

    def ssm_params(j):
        w_in = w["ssm_w_in"][j]
        cw = w["ssm_conv_w"][j]
        return dict(ng=ng, hpg=ssm_heads // ng, d_inner=d_inner,
                    w_z=w_in[:, :d_inner], w_xbc=w_in[:, d_inner:d_inner + d_inner + 2 * ng * nstate],
                    w_dt=pad_lanes(w_in[:, 2 * d_inner + 2 * ng * nstate:]),
                    conv_w=[cw[k:k + 1] for k in range(cw.shape[0])], conv_b=w["ssm_conv_b"][j:j + 1],
                    dt_bias=pad_lanes(w["ssm_dt_bias"][j:j + 1]), a_log=pad_lanes(w["ssm_a_log"][j:j + 1]),
                    d_exp=jnp.repeat(w["ssm_d"][j], HEAD)[None, :], norm_gain=w["ssm_norm_gain"][j:j + 1],
                    w_out=w["ssm_w_out"][j])

    def gm_params(j):
        wc = jnp.where(tril, w["gm_w_s"][j], 0.0).astype(BF16)
        bst = jnp.repeat(w["gm_b_s"][j].T, LANES, axis=1)
        return wc, bst

    def sb_gains(j):
        nh = d // HEAD
        return jnp.tile(w["sb_q_gain"][j], nh)[None, :], jnp.tile(w["sb_k_gain"][j], nh)[None, :]

    saved = []
    cur = x
    for i in range(depth):
        kind, j = i % 3, i // 3
        gmix = w["mix_norm"][i:i + 1]
        if kind == 0:
            qg, kg = sb_gains(j)
            cur, sv = sb_fwd(cur, gmix, w["sb_w_qkv"][j], qg, kg, lambda j=j: w["sb_w_o"][j], bd, f"{i}", plan)
        elif kind == 1:
            wc, bst = gm_params(j)
            cur, sv = gm_fwd(cur, gmix, w["gm_w_in"][j], w["gm_b_in"][j:j + 1], w["gm_v_gain"][j:j + 1], wc, bst,
                             w["gm_w_out"][j], f"{i}")
        else:
            cur, sv = ssm_fwd(cur, gmix, ssm_params(j), f"{i}", plan)
        cur, sv2 = ffn_fwd(cur, w["ffn_norm"][i:i + 1], w["ffn_w_gu"][i], w["ffn_w_down"][i], f"{i}", plan)
        saved.append((sv, sv2))

    loss, dcur = loss_and_grad(cur, target, "loss")

    grads = {k: [None] * len(v) for k, v in w.items()}
    for i in reversed(range(depth)):
        kind, j = i % 3, i // 3
        sv, sv2 = saved[i]
        gmix = w["mix_norm"][i:i + 1]
        dcur, dgf, dwgu, dwdown = ffn_bwd(dcur, sv2, w["ffn_norm"][i:i + 1], w["ffn_w_gu"][i], w["ffn_w_down"][i], f"{i}")
        grads["ffn_norm"][i], grads["ffn_w_gu"][i], grads["ffn_w_down"][i] = dgf[0], dwgu, dwdown
        plan.grads_ready({("ffn_w_gu", i): dwgu, ("ffn_w_down", i): dwdown})
        if kind == 0:
            qg, kg = sb_gains(j)
            dcur, dg, dwqkv, dqg, dkg, dwo = sb_bwd(dcur, sv, gmix, w["sb_w_qkv"][j], qg, kg, w["sb_w_o"][j], bd, f"{i}", plan)
            grads["sb_w_qkv"][j], grads["sb_q_gain"][j], grads["sb_k_gain"][j], grads["sb_w_o"][j] = dwqkv, dqg, dkg, dwo
        elif kind == 1:
            wc, bst = gm_params(j)
            dcur, dg, dwin, dbin, dvg, dws, dbs, dwout = gm_bwd(dcur, sv, gmix, w["gm_w_in"][j], w["gm_v_gain"][j:j + 1],
                                                                 wc, bst, w["gm_w_out"][j], f"{i}")
            grads["gm_w_in"][j], grads["gm_b_in"][j], grads["gm_v_gain"][j] = dwin, dbin[0], dvg[0]
            grads["gm_w_s"][j], grads["gm_b_s"][j], grads["gm_w_out"][j] = dws, dbs, dwout
        else:
            dcur, dg, gs = ssm_bwd(dcur, sv, gmix, ssm_params(j), f"{i}", plan)
            grads["ssm_w_in"][j], grads["ssm_conv_w"][j], grads["ssm_conv_b"][j] = gs["w_in"], gs["conv_w"], gs["conv_b"][0]
            grads["ssm_dt_bias"][j], grads["ssm_a_log"][j], grads["ssm_d"][j] = gs["dt_bias"], gs["a_log"], gs["d"]
            grads["ssm_norm_gain"][j], grads["ssm_w_out"][j] = gs["norm_gain"][0], gs["w_out"]
        grads["mix_norm"][i] = dg[0]
        mixer = {0: ("sb_w_qkv", "sb_w_o"), 1: ("gm_w_in", "gm_w_out"), 2: ("ssm_w_in", "ssm_w_out")}[kind]
        plan.grads_ready({(n, j): grads[n][j] for n in mixer})
    grads = {k: (v if k in MATRICES else jnp.stack(v)) for k, v in grads.items()}
    return loss, dcur[0], grads


WEIGHTS = ["mix_norm", "ffn_norm", "sb_w_qkv", "sb_q_gain", "sb_k_gain", "sb_w_o", "gm_w_in", "gm_b_in", "gm_v_gain",
           "gm_w_s", "gm_b_s", "gm_w_out", "ssm_w_in", "ssm_conv_w", "ssm_conv_b", "ssm_dt_bias", "ssm_a_log", "ssm_d",
           "ssm_norm_gain", "ssm_w_out", "ffn_w_gu", "ffn_w_down"]
SHARDED = {"sb_w_qkv": 2, "sb_w_o": 1, "gm_w_in": 2, "gm_w_out": 1, "ssm_w_in": 2, "ssm_conv_w": 2, "ssm_conv_b": 1,
           "ssm_norm_gain": 1, "ssm_w_out": 1, "ffn_w_gu": 2, "ffn_w_down": 1}
EXACT = ("ssm_conv_w", "ssm_conv_b", "ssm_norm_gain")
MATRICES = tuple(n for n in SHARDED if n not in EXACT)
COLUMN_BLOCKS = ("sb_w_qkv", "gm_w_in", "ffn_w_gu")
REPLICATED = [n for n in WEIGHTS if n not in SHARDED]
N_CHIPS = 4
N_DEV = 8
PACK_COLS = 1024


def _pack(pieces, dtype, align):
    flat = jnp.concatenate([p.reshape(-1).astype(dtype) for p in pieces])
    rows = -(-flat.shape[0] // (PACK_COLS * align)) * align
    flat = jnp.pad(flat, (0, rows * PACK_COLS - flat.shape[0]))
    return flat.reshape(rows, PACK_COLS)


def _unpack(flat, shapes):
    out, off = [], 0
    for shp in shapes:
        n = math.prod(shp)
        out.append(flat[off:off + n].reshape(shp))
        off += n
    return out


ANY = pl.BlockSpec(memory_space=pl.ANY)


def _pos():
    return lax.axis_index("x"), lax.axis_index("y"), lax.axis_index("c")


def _remote(src, dst, send, recv, k, to):
    return pltpu.make_async_remote_copy(src_ref=src, dst_ref=dst, send_sem=send.at[k], recv_sem=recv.at[k],
                                        device_id=to, device_id_type=MESH_ID)


def _comm_call(body, name, ins, out_shapes, nsem, aliases=None):
    return pl.pallas_call(
        body, name=name, out_shape=out_shapes,
        in_specs=[ANY] * len(ins), out_specs=[ANY] * len(out_shapes),
        scratch_shapes=[pltpu.SemaphoreType.DMA((nsem,)), pltpu.SemaphoreType.DMA((nsem,))],
        input_output_aliases=aliases or {},
    )(*ins)


def stage_shard(w, layer, chip, name):
    _, rows, cols = w.shape
    tr = _pick(rows, (512, 352, 256, 128))

    def kern(idx_ref, w_ref, o_ref):
        o_ref[...] = w_ref[...].astype(BF16)

    grid_spec = pltpu.PrefetchScalarGridSpec(
        num_scalar_prefetch=1, grid=(rows // tr,),
        in_specs=[pl.BlockSpec((None, tr, cols), lambda i, idx: (layer, i, 0))],
        out_specs=pl.BlockSpec((None, tr, cols), lambda i, idx: (idx[0], i, 0)))
    return pl.pallas_call(
        kern, name=name, grid_spec=grid_spec,
        out_shape=jax.ShapeDtypeStruct((N_CHIPS, rows, cols), BF16),
        compiler_params=_params(("parallel",)),
    )(jnp.reshape(chip, (1,)).astype(jnp.int32), w)


class Side:
    def __init__(self, arrays, out_shapes, aliases, nsem, start, finish):
        self.arrays, self.out_shapes, self.aliases, self.nsem = list(arrays), list(out_shapes), aliases, nsem
        self.start, self.finish = start, finish


def run_side(side, name):
    n_in, n_out = len(side.arrays), len(side.out_shapes)

    def body(*refs):
        ins, outs = refs[:n_in], refs[n_in:n_in + n_out]
        send, recv = refs[n_in + n_out:]
        side.start(ins, outs, send, recv)
        side.finish(ins, outs, send, recv)

    return _comm_call(body, name, side.arrays, side.out_shapes, side.nsem, aliases=side.aliases)


def side_call(kern, side, *, name, grid, in_specs, out_specs, out_shape, scratch_shapes, args):
    if side is None:
        res = pl.pallas_call(kern, name=name, grid=grid, in_specs=in_specs, out_specs=out_specs, out_shape=out_shape,
                             scratch_shapes=scratch_shapes,
                             compiler_params=_params(("parallel",) + ("arbitrary",) * (len(grid) - 1)))(*args)
        return list(res), []
    n_in, n_out, n_scr = len(in_specs), len(out_specs), len(scratch_shapes)
    s_in, s_out = len(side.arrays), len(side.out_shapes)

    def body(*refs):
        ins, refs = refs[:n_in], refs[n_in:]
        side_ins, refs = refs[:s_in], refs[s_in:]
        outs, refs = refs[:n_out], refs[n_out:]
        side_outs, refs = refs[:s_out], refs[s_out:]
        scr, (send, recv) = refs[:n_scr], refs[n_scr:]
        first, last = None, None
        for axis, size in enumerate(grid):
            at0, at1 = pl.program_id(axis) == 0, pl.program_id(axis) == size - 1
            first = at0 if first is None else first & at0
            last = at1 if last is None else last & at1

        @pl.when(first)
        def _():
            side.start(side_ins, side_outs, send, recv)

        kern(*ins, *outs, *scr)

        @pl.when(last)
        def _():
            side.finish(side_ins, side_outs, send, recv)

    res = pl.pallas_call(
        body, name=name, grid=grid,
        in_specs=list(in_specs) + [ANY] * s_in, out_specs=list(out_specs) + [ANY] * s_out,
        out_shape=list(out_shape) + side.out_shapes,
        scratch_shapes=list(scratch_shapes) + [pltpu.SemaphoreType.DMA((side.nsem,)), pltpu.SemaphoreType.DMA((side.nsem,))],
        input_output_aliases={n_in + a: n_out + b for a, b in side.aliases.items()},
        compiler_params=_params(("arbitrary",) * len(grid)),
    )(*args, *side.arrays)
    return list(res[:n_out]), list(res[n_out:])


def gather_side(staged):
    n = len(staged)

    def plan(o_refs, send, recv):
        x, y, c = _pos()
        chips = [(1 - x, y), (x, 1 - y), (1 - x, 1 - y)]

        def part(u, chip, cc):
            half = staged[u].shape[1] // 2
            return o_refs[u].at[2 * chip[0] + chip[1], pl.ds(cc * half, half), :]

        first = [_remote(part(u, (x, y), c), part(u, (x, y), c), send, recv, 6 * u + j, (*chip, c))
                 for u in range(n) for j, chip in enumerate(chips)]
        landed = [_remote(part(u, chip, c), part(u, chip, c), send, recv, 6 * u + j, (x, y, c))
                  for u in range(n) for j, chip in enumerate(chips)]
        passed = [_remote(part(u, chip, c), part(u, chip, c), send, recv, 6 * u + 3 + j, (x, y, 1 - c))
                  for u in range(n) for j, chip in enumerate(chips)]
        handed = [_remote(part(u, chip, 1 - c), part(u, chip, 1 - c), send, recv, 6 * u + 3 + j, (x, y, c))
                  for u in range(n) for j, chip in enumerate(chips)]
        return first, landed, passed, handed

    def start(ins, outs, send, recv):
        for cp in plan(outs, send, recv)[0]:
            cp.start()

    def finish(ins, outs, send, recv):
        first, landed, passed, handed = plan(outs, send, recv)
        for got, fw in zip(landed, passed):
            got.wait_recv()
            fw.start()
        for got in handed:
            got.wait_recv()
        for cp in first + passed:
            cp.wait_send()

    outs = [jax.ShapeDtypeStruct(s.shape, s.dtype) for s in staged]
    return Side(staged, outs, {u: u for u in range(n)}, 6 * n, start, finish)


def swap_halves(gps, name):
    n = len(gps)

    def body(*refs):
        g_refs, r_refs = refs[:n], refs[n:2 * n]
        send, recv = refs[2 * n:]
        x, y, c = _pos()
        cps = []
        for u in range(n):
            half = gps[u].shape[1] // 2
            cps.append(_remote(g_refs[u].at[:, pl.ds((1 - c) * half, half), :], r_refs[u], send, recv, u, (x, y, 1 - c)))
        for cp in cps:
            cp.start()
        for cp in cps:
            cp.wait()

    outs = [jax.ShapeDtypeStruct((g.shape[0], g.shape[1] // 2, g.shape[2]), g.dtype) for g in gps]
    return _comm_call(body, name, gps, outs, n)


def scatter_side(parts):
    n = len(parts)

    def plan(p_refs, r_refs, send, recv):
        x, y, c = _pos()
        chips = [(1 - x, y), (x, 1 - y), (1 - x, 1 - y)]
        return [_remote(p_refs[u].at[2 * chip[0] + chip[1]], r_refs[u].at[j], send, recv, 3 * u + j, (*chip, c))
                for u in range(n) for j, chip in enumerate(chips)]

    def start(ins, outs, send, recv):
        for cp in plan(ins, outs, send, recv):
            cp.start()

    def finish(ins, outs, send, recv):
        for cp in plan(ins, outs, send, recv):
            cp.wait()

    outs = [jax.ShapeDtypeStruct((N_CHIPS - 1,) + p.shape[1:], p.dtype) for p in parts]
    return Side(parts, outs, {}, 3 * n, start, finish)


def join_halves(bufs):
    n = len(bufs)

    def body(*refs):
        o_refs = refs[n:2 * n]
        send, recv = refs[2 * n:]
        x, y, c = _pos()

        def rows(u, cc):
            half = bufs[u].shape[1] // 2
            return o_refs[u].at[:, pl.ds(cc * half, half), :]

        cps = [_remote(rows(u, c), rows(u, c), send, recv, u, (x, y, 1 - c)) for u in range(n)]
        for cp in cps:
            cp.start()
        for u in range(n):
            _remote(rows(u, 1 - c), rows(u, 1 - c), send, recv, u, (x, y, c)).wait_recv()
        for cp in cps:
            cp.wait_send()

    outs = [jax.ShapeDtypeStruct(b.shape, b.dtype) for b in bufs]
    return _comm_call(body, "join_halves", bufs, outs, n, aliases={u: u for u in range(n)})


def gather_small(sg, name):
    rows, cols = sg.shape

    def body(s_ref, o_ref, send, recv, lsem):
        x, y, c = _pos()
        me, sibling = (x, y, c), (x, y, 1 - c)
        chips = [(1 - x, y), (x, 1 - y), (1 - x, 1 - y)]

        def blk(px, py, pc):
            return o_ref.at[4 * px + 2 * py + pc]

        mine = pltpu.make_async_copy(s_ref, blk(*me), lsem)
        mine.start()
        first = [_remote(s_ref, blk(*me), send, recv, 0, sibling)]
        first += [_remote(s_ref, blk(*me), send, recv, 1 + j, (*chip, c)) for j, chip in enumerate(chips)]
        for cp in first:
            cp.start()
        passed = [_remote(blk(*chip, c), blk(*chip, c), send, recv, 4 + j, sibling) for j, chip in enumerate(chips)]
        for j, chip in enumerate(chips):
            _remote(blk(*chip, c), blk(*chip, c), send, recv, 1 + j, me).wait_recv()
            passed[j].start()
        _remote(blk(*sibling), blk(*sibling), send, recv, 0, me).wait_recv()
        for j, chip in enumerate(chips):
            _remote(blk(*chip, 1 - c), blk(*chip, 1 - c), send, recv, 4 + j, me).wait_recv()
        for cp in first + passed:
            cp.wait_send()
        mine.wait()

    return pl.pallas_call(
        body, name=name,
        out_shape=jax.ShapeDtypeStruct((N_DEV, rows, cols), sg.dtype),
        in_specs=[ANY], out_specs=ANY,
        scratch_shapes=[pltpu.SemaphoreType.DMA((N_DEV - 1,)), pltpu.SemaphoreType.DMA((N_DEV - 1,)), pltpu.SemaphoreType.DMA],
    )(sg)


def sum_cores(gp, theirs, core, chip, name):
    nch, rows, cols = gp.shape
    half = rows // 2
    tr = _pick(half, (512, 352, 256, 176, 128, 64))
    nb = half // tr

    def kern(idx_ref, g_ref, t_ref, own_ref, all_ref):
        k = pl.program_id(1)
        s = g_ref[...].astype(F32) + t_ref[...].astype(F32)
        all_ref[...] = s.astype(BF16)

        @pl.when(k == idx_ref[1])
        def _():
            own_ref[...] = s

    grid_spec = pltpu.PrefetchScalarGridSpec(
        num_scalar_prefetch=1, grid=(nb, nch),
        in_specs=[pl.BlockSpec((None, tr, cols), lambda i, k, idx: (k, idx[0] * nb + i, 0)),
                  pl.BlockSpec((None, tr, cols), lambda i, k, idx: (k, i, 0))],
        out_specs=[pl.BlockSpec((tr, cols), lambda i, k, idx: (i, 0)),
                   pl.BlockSpec((None, tr, cols), lambda i, k, idx: (k, i, 0))])
    return pl.pallas_call(
        kern, name=name, grid_spec=grid_spec,
        out_shape=[jax.ShapeDtypeStruct((half, cols), F32), jax.ShapeDtypeStruct((nch, half, cols), BF16)],
        compiler_params=_params(("parallel", "arbitrary")),
    )(jnp.stack([core, chip]).astype(jnp.int32), gp, theirs)


def sum_chips(own, others, core, layer, nlayers, into, name):
    half, cols = own.shape
    tr = _pick(half, (512, 352, 256, 176, 128, 64))
    nb = half // tr

    def kern(idx_ref, o_ref, a_ref, b_ref, c_ref, *rest):
        out_ref = rest[-1]
        out_ref[...] = ((o_ref[...] + a_ref[...].astype(F32)) + b_ref[...].astype(F32)) + c_ref[...].astype(F32)

    grid_spec = pltpu.PrefetchScalarGridSpec(
        num_scalar_prefetch=1, grid=(nb,),
        in_specs=[pl.BlockSpec((tr, cols), lambda i, idx: (i, 0))] +
                 [pl.BlockSpec((None, tr, cols), lambda i, idx, j=j: (j, i, 0)) for j in range(N_CHIPS - 1)] +
                 ([] if into is None else [pl.BlockSpec(memory_space=pl.ANY)]),
        out_specs=pl.BlockSpec((None, tr, cols), lambda i, idx: (layer, idx[0] * nb + i, 0)))
    args = [jnp.reshape(core, (1,)).astype(jnp.int32), own, others, others, others] + ([] if into is None else [into])
    return pl.pallas_call(
        kern, name=name, grid_spec=grid_spec,
        out_shape=jax.ShapeDtypeStruct((nlayers, 2 * half, cols), F32),
        input_output_aliases={} if into is None else {len(args) - 1: 0},
        compiler_params=_params(("parallel",)),
    )(*args)


def small_update(gath, w, m, v, name):
    def fn(*vs):
        g = vs[0]
        for t in vs[1:N_DEV]:
            g = g + t
        wv, mv, vv = vs[N_DEV:]
        m2 = ADAM_B1 * mv + (1.0 - ADAM_B1) * g
        v2 = ADAM_B2 * vv + (1.0 - ADAM_B2) * (g * g)
        m_hat = m2 / (1.0 - ADAM_B1 ** ADAM_STEP)
        v_hat = v2 / (1.0 - ADAM_B2 ** ADAM_STEP)
        return g, -ADAM_LR * (m_hat / (jnp.sqrt(v_hat) + ADAM_EPS) + ADAM_WD * wv), m2, v2

    c = w.shape[1]
    ins = [(gath[k], "row") for k in range(N_DEV)] + [(w, "row"), (m, "row"), (v, "row")]
    return rowwise(fn, ins, [(c, F32)] * 4, tr=w.shape[0] // 2, name=name)


_MIX = {0: [("sb_w_qkv", 0), ("sb_w_o", 0)], 1: [("gm_w_in", 0), ("gm_w_out", 0)],
        2: [("ssm_w_in", 0), ("ssm_w_out", 0)], 3: [("sb_w_qkv", 1), ("sb_w_o", 1)]}
_FFN = {i: [("ffn_w_gu", i), ("ffn_w_down", i)] for i in range(4)}
GATHER_FIRST = _MIX[0][:1]
GATHER_AT = {"sb_attn_0": _MIX[0][1:] + _FFN[0] + _FFN[1],
             "ffn_gu_0": _MIX[1], "ffn_down_0": _MIX[2][1:], "ffn_gu_1": _MIX[2][:1], "ffn_down_1": _FFN[2][1:],
             "ssm_scan_2": _FFN[2][:1] + _MIX[3] + _FFN[3][1:], "ffn_gu_2": _FFN[3][:1]}
SCATTER_AT = {"ssm_dscan_2": _FFN[3] + _MIX[3] + _FFN[2], "sb_dattn_0": _MIX[2] + _FFN[1] + _MIX[1] + _FFN[0]}
SCATTER_LAST = _MIX[0]


class _Plan:
    def __init__(self, ins, core, chip):
        self.core, self.chip = core, chip
        self.staged = {(n, l): stage_shard(ins[n], l, chip, f"stage_{n}_{l}")
                       for n in MATRICES for l in range(ins[n].shape[0])}
        self.full = {n: [None] * ins[n].shape[0] for n in MATRICES}
        self.ready = {}
        self.parts = {}
        self.halves = {}
        self.layers = {n: ins[n].shape[0] for n in MATRICES}
        self.swaps = 0
        self._fill(GATHER_FIRST, run_side(gather_side([self.staged[u] for u in GATHER_FIRST]), "gather_first"))

    def _fill(self, units, gathered):
        for (n, l), g in zip(units, gathered):
            if n in COLUMN_BLOCKS:
                self.full[n][l] = g
            elif n == "ssm_w_in":
                self.full[n][l] = jnp.concatenate([g[k] for k in range(N_CHIPS)], axis=1)
            else:
                self.full[n][l] = g.reshape(-1, g.shape[-1])

    def _prepare(self, units):
        gps = [self.ready[u] for u in units]
        theirs = swap_halves(gps, f"swap_halves_{self.swaps}")
        self.swaps += 1
        for (n, l), g, t in zip(units, gps, theirs):
            self.parts[(n, l)] = sum_cores(g, t, self.core, self.chip, f"sum_cores_{n}_{l}")

    def _reduce(self, units, others):
        for (n, l), other in zip(units, others):
            self.halves[n] = sum_chips(self.parts[(n, l)][0], other, self.core, l, self.layers[n], self.halves.get(n),
                                       f"sum_chips_{n}_{l}")

    def side(self, tag):
        if tag in GATHER_AT:
            return gather_side([self.staged[u] for u in GATHER_AT[tag]])
        if tag in SCATTER_AT:
            self._prepare(SCATTER_AT[tag])
            return scatter_side([self.parts[u][1] for u in SCATTER_AT[tag]])
        return None

    def done(self, tag, results):
        if tag in GATHER_AT:
            self._fill(GATHER_AT[tag], results)
        else:
            self._reduce(SCATTER_AT[tag], results)

    def grads_ready(self, grads):
        for (n, l), g in grads.items():
            if n in COLUMN_BLOCKS:
                self.ready[(n, l)] = g
            elif n == "ssm_w_in":
                self.ready[(n, l)] = jnp.stack(jnp.split(g, N_CHIPS, axis=1))
            else:
                self.ready[(n, l)] = g.reshape(N_CHIPS, -1, g.shape[-1])

    def shard_grads(self):
        self._prepare(SCATTER_LAST)
        self._reduce(SCATTER_LAST, run_side(scatter_side([self.parts[u][1] for u in SCATTER_LAST]), "scatter_last"))
        names = sorted(self.halves)
        return dict(zip(names, join_halves([self.halves[n] for n in names])))


def _step(ins):
    x, target = ins["x"][0], ins["loss_target"][0]
    core = lax.axis_index("c")
    chip = 2 * lax.axis_index("x") + lax.axis_index("y")

    def lane_pad(v):
        return jnp.pad(v, ((0, 0), (0, PACK_COLS - v.shape[1])))

    vec_rows = [ins["ssm_conv_w"][0], ins["ssm_conv_b"], lane_pad(ins["ssm_norm_gain"])]
    blk = jnp.concatenate(vec_rows + [jnp.zeros((SUBLANES - 6, PACK_COLS), F32)], axis=0)
    per_chip = gather_small(blk, "gather_vectors")[0::2]
    ngw = ins["ssm_norm_gain"].shape[1]
    full = {
        "ssm_conv_w": jnp.concatenate([per_chip[k, 0:4] for k in range(N_CHIPS)], axis=1)[None],
        "ssm_conv_b": jnp.concatenate([per_chip[k, 4:5] for k in range(N_CHIPS)], axis=1),
        "ssm_norm_gain": jnp.concatenate([per_chip[k, 5:6, :ngw] for k in range(N_CHIPS)], axis=1),
    }

    plan = _Plan(ins, core, chip)
    full.update(plan.full)
    for n in REPLICATED:
        full[n] = ins[n]

    loss, dx, grads = local_step(x, target, full, plan)
    loss = lax.psum(loss, ALL_AXES)
    gshards = plan.shard_grads()

    small_shapes = [ins[n].shape for n in REPLICATED]
    vec_shapes = [grads[n].shape for n in EXACT]
    vec_pack = _pack([grads[n] for n in EXACT], F32, SUBLANES)
    gath = gather_small(jnp.concatenate([_pack([grads[n] for n in REPLICATED], F32, SUBLANES), vec_pack], axis=0),
                        "gather_small")
    packed = [jnp.concatenate([_pack([ins[pre + n] for n in REPLICATED], F32, SUBLANES), jnp.zeros_like(vec_pack)], axis=0)
              for pre in ("", "m_", "v_")]
    res = small_update(gath, *packed, name="small_update")
    nrep = res[0].shape[0] - vec_pack.shape[0]
    small = [dict(zip(REPLICATED, _unpack(r[:nrep].reshape(-1), small_shapes))) for r in res]
    vec_g = dict(zip(EXACT, _unpack(res[0][nrep:].reshape(-1), vec_shapes)))

    out_g, out_d, out_m, out_v = {}, {}, {}, {}
    for n in REPLICATED:
        out_g[n], out_d[n], out_m[n], out_v[n] = (s[n] for s in small)
    for n in SHARDED:
        shp = ins[n].shape
        if n in EXACT:
            g = lax.dynamic_slice_in_dim(vec_g[n], chip * shp[-1], shp[-1], axis=vec_g[n].ndim - 1)
        else:
            g = gshards[n]
        two = (math.prod(shp[:-1]), shp[-1])
        d2, m2, v2 = adamw(ins[n].reshape(two), g.reshape(two), ins["m_" + n].reshape(two),
                           ins["v_" + n].reshape(two), f"adamw_{n}")
        out_g[n], out_d[n], out_m[n], out_v[n] = g, d2.reshape(shp), m2.reshape(shp), v2.reshape(shp)
    return (loss, dx[None], *[out_g[n] for n in WEIGHTS], *[out_d[n] for n in WEIGHTS],
            *[out_m[n] for n in WEIGHTS], *[out_v[n] for n in WEIGHTS])


def kernel(x, mix_norm, ffn_norm, sb_w_qkv, sb_q_gain, sb_k_gain, sb_w_o, gm_w_in, gm_b_in, gm_v_gain, gm_w_s, gm_b_s, gm_w_out, ssm_w_in, ssm_conv_w, ssm_conv_b, ssm_dt_bias, ssm_a_log, ssm_d, ssm_norm_gain, ssm_w_out, ffn_w_gu, ffn_w_down, loss_target, m_mix_norm, m_ffn_norm, m_sb_w_qkv, m_sb_q_gain, m_sb_k_gain, m_sb_w_o, m_gm_w_in, m_gm_b_in, m_gm_v_gain, m_gm_w_s, m_gm_b_s, m_gm_w_out, m_ssm_w_in, m_ssm_conv_w, m_ssm_conv_b, m_ssm_dt_bias, m_ssm_a_log, m_ssm_d, m_ssm_norm_gain, m_ssm_w_out, m_ffn_w_gu, m_ffn_w_down, v_mix_norm, v_ffn_norm, v_sb_w_qkv, v_sb_q_gain, v_sb_k_gain, v_sb_w_o, v_gm_w_in, v_gm_b_in, v_gm_v_gain, v_gm_w_s, v_gm_b_s, v_gm_w_out, v_ssm_w_in, v_ssm_conv_w, v_ssm_conv_b, v_ssm_dt_bias, v_ssm_a_log, v_ssm_d, v_ssm_norm_gain, v_ssm_w_out, v_ffn_w_gu, v_ffn_w_down):
    return _step(dict(locals()))
```

```python
import functools
import math

import jax
import jax.numpy as jnp
from jax import lax
from jax.experimental import pallas as pl
from jax.experimental.pallas import tpu as pltpu

F32 = jnp.float32
BF16 = jnp.bfloat16
EPS = 1e-6
LANES = 128
SUBLANES = 8
VMEM_LIMIT = 56 * 1024 * 1024
HEAD = 64
CHUNK = 128
SB_TQ, SB_TK = 256, 256
SSD_SUB = 8
SB_DEAD = -110.0
SB_UNSEEN = -1e30
ADAM_LR, ADAM_B1, ADAM_B2, ADAM_EPS, ADAM_WD, ADAM_STEP = 0.001, 0.9, 0.999, 1e-08, 0.01, 10
MESH_ID = pl.DeviceIdType.MESH
ALL_AXES = ("x", "y", "c")


def _params(sem):
    return pltpu.CompilerParams(dimension_semantics=sem, vmem_limit_bytes=VMEM_LIMIT)


def _pick(n, cands):
    for c in cands:
        if n % c == 0:
            return c
    return n


def _dot(a, b, dims=((1,), (0,))):
    return lax.dot_general(a, b, (dims, ((), ())), preferred_element_type=F32)


def _dot_nt(a, b):
    return _dot(a, b, ((1,), (1,)))


def _dot_tn(a, b):
    return _dot(a, b, ((0,), (0,)))


def _split2(x):
    hi = x.astype(BF16)
    lo = (x - hi.astype(F32)).astype(BF16)
    return hi, lo


def _dot_x2(x, m):
    hi, lo = _split2(x)
    return _dot(hi, m) + _dot(lo, m)


def _dot_x3_left(m, x):
    h1 = x.astype(BF16)
    r1 = x - h1.astype(F32)
    h2 = r1.astype(BF16)
    h3 = (r1 - h2.astype(F32)).astype(BF16)
    return _dot(m, h1) + _dot(m, h2) + _dot(m, h3)


def _sigmoid(x):
    return 1.0 / (1.0 + jnp.exp(-x))


def _softplus(x):
    return jnp.maximum(x, 0.0) + jnp.log(1.0 + jnp.exp(-jnp.abs(x)))


def _colsum(x):
    return jnp.sum(x, axis=0, keepdims=True)


def _rowsum(x):
    return jnp.sum(x, axis=1, keepdims=True)


def _iota2(shape, dim):
    return lax.broadcasted_iota(jnp.int32, shape, dim)


MM_VMEM_BUDGET = 40 * 1024 * 1024
MM_STEP_US = 0.35
MM_HBM_BYTES_PER_US = 3.0e6
MM_VMEM_BYTES_PER_US = 1.5e6
MM_FLOPS_PER_US = 9.0e8
MXU_DIM = 256


def _mm_tiles(m, n, kk, wn, wk, a_bytes, b_bytes, has_add):
    def divisors(total, cands):
        got = [c for c in cands if total % c == 0 and c <= total]
        return got or [total]

    best = None
    for tm in divisors(m, (1024, 512, 256, 128)):
        for tn in divisors(wn, (1024, 768, 1408, 512, 256, 128)):
            for tk in divisors(wk, (4096, 2816, 2048, 1408, 1024, 768, 512, 256, 128)):
                nk = kk // tk
                vmem = 2 * (tm * tk * a_bytes + tk * tn * b_bytes + tm * tn * 4 * (2 if has_add else 1))
                vmem += tm * tn * 4 if nk > 1 else 0
                if vmem > MM_VMEM_BUDGET:
                    continue
                steps = (m // tm) * (n // tn) * nk
                a_reads = 1 if nk == 1 else n // tn
                traffic = m * kk * a_bytes * a_reads + kk * n * b_bytes * (m // tm) + m * n * 4
                fill = min(1.0, tn / MXU_DIM) * min(1.0, tm / MXU_DIM)
                compute = 2.0 * m * n * kk / (MM_FLOPS_PER_US * fill)
                cost = steps * MM_STEP_US + max(compute, traffic / MM_HBM_BYTES_PER_US)
                if nk > 1:
                    cost += steps * tm * tn * 8 / MM_VMEM_BYTES_PER_US
                if best is None or cost < best[0]:
                    best = (cost, tm, tn, tk)
    return best[1:]


def mm(a, b, *, ta=False, tb=False, add=None, bias=None, a_chunks=False, b_chunks=False, out_chunks=False,
       out_dtype=F32, name, side=None):
    wa = None
    if a_chunks:
        m, wa = a.shape[1], a.shape[2]
        kk = a.shape[0] * wa
    elif ta:
        kk, m = a.shape
    else:
        m, kk = a.shape
    nch, wide = 1, None
    if b_chunks:
        nch, rows_b, wide = b.shape
        kb, n = (rows_b, nch * wide) if not tb else (nch * wide, rows_b)
    elif tb:
        n, kb = b.shape
    else:
        kb, n = b.shape
    wide_o = n // N_CHIPS if out_chunks else None
    assert kk == kb, (a.shape, b.shape, ta, tb)
    has_add, has_bias = add is not None, bias is not None
    wk = wide if (wide and tb) else kk
    wn = wide if (wide and not tb) else n
    tm, tn, tk = _mm_tiles(m, n, kk, math.gcd(wn, wide_o) if wide_o else wn, math.gcd(wk, wa) if wa else wk,
                           a.dtype.itemsize, b.dtype.itemsize, has_add)
    nk = kk // tk
    dims = ((0 if ta else 1,), (1 if tb else 0,))

    def kern(*refs):
        a_ref, b_ref = refs[0], refs[1]
        rest = list(refs[2:])
        add_ref = rest.pop(0) if has_add else None
        bias_ref = rest.pop(0) if has_bias else None
        o_ref = rest[0]
        part = _dot(a_ref[...].astype(BF16), b_ref[...].astype(BF16), dims)

        def finish(r):
            if has_add:
                r = r + add_ref[...]
            if has_bias:
                r = r + bias_ref[...]
            o_ref[...] = r.astype(out_dtype)

        if nk == 1:
            finish(part)
        else:
            acc_ref = rest[1]
            k = pl.program_id(2)

            @pl.when(k == 0)
            def _():
                acc_ref[...] = part

            @pl.when((k > 0) & (k < nk - 1))
            def _():
                acc_ref[...] += part

            @pl.when(k == nk - 1)
            def _():
                finish(acc_ref[...] + part)

    if a_chunks:
        per_a = wa // tk
        a_spec = pl.BlockSpec((None, tm, tk), lambda i, j, k: (k // per_a, i, k % per_a))
    elif ta:
        a_spec = pl.BlockSpec((tk, tm), lambda i, j, k: (k, i))
    else:
        a_spec = pl.BlockSpec((tm, tk), lambda i, j, k: (i, k))
    if b_chunks and tb:
        per = wide // tk
        b_spec = pl.BlockSpec((None, tn, tk), lambda i, j, k: (k // per, j, k % per))
    elif b_chunks:
        per = wide // tn
        b_spec = pl.BlockSpec((None, tk, tn), lambda i, j, k: (j // per, k, j % per))
    elif tb:
        b_spec = pl.BlockSpec((tn, tk), lambda i, j, k: (j, k))
    else:
        b_spec = pl.BlockSpec((tk, tn), lambda i, j, k: (k, j))
    if out_chunks:
        per_o = wide_o // tn
        out_spec = pl.BlockSpec((None, tm, tn), lambda i, j, k: (j // per_o, i, j % per_o))
        out_shape = jax.ShapeDtypeStruct((N_CHIPS, m, wide_o), out_dtype)
    else:
        out_spec = pl.BlockSpec((tm, tn), lambda i, j, k: (i, j))
        out_shape = jax.ShapeDtypeStruct((m, n), out_dtype)
    in_specs, args = [a_spec, b_spec], [a, b]
    if has_add:
        in_specs.append(pl.BlockSpec((tm, tn), lambda i, j, k: (i, j)))
        args.append(add)
    if has_bias:
        in_specs.append(pl.BlockSpec((1, tn), lambda i, j, k: (0, j)))
        args.append(bias)
    (out,), side_outs = side_call(
        kern, side,
        name=name,
        grid=(m // tm, n // tn, nk),
        in_specs=in_specs,
        out_specs=[out_spec],
        out_shape=[out_shape],
        scratch_shapes=[pltpu.VMEM((tm, tn), F32)] if nk > 1 else [],
        args=args)
    return out if side is None else (out, side_outs)


def mm_hooked(plan, a, b, *, name, **kw):
    side = plan.side(name)
    if side is None:
        return mm(a, b, name=name, **kw)
    out, side_outs = mm(a, b, name=name, side=side, **kw)
    plan.done(name, side_outs)
    return out


def rowwise(fn, ins, outs, accs=(), *, tr, name):
    rows = [a for a, kind in ins if kind == "row"][0].shape[0]
    tr = min(tr, rows)
    assert rows % tr == 0 and tr % SUBLANES == 0, (rows, tr)
    n = rows // tr
    n_in, n_out = len(ins), len(outs)
    kinds = [kind for _, kind in ins]

    def kern(*refs):
        i = pl.program_id(0)
        vals = []
        for ref, kind in zip(refs[:n_in], kinds):
            v = ref[...]
            if kind == "prev":
                v = v * (i > 0).astype(v.dtype)
            elif kind == "next":
                v = v * (i < n - 1).astype(v.dtype)
            vals.append(v)
        res = fn(*vals)
        for ref, r in zip(refs[n_in:n_in + n_out], res[:n_out]):
            ref[...] = r.astype(ref.dtype)
        if accs:
            acc_refs = refs[n_in + n_out:]

            @pl.when(i == 0)
            def _():
                for ref in acc_refs:
                    ref[...] = jnp.zeros_like(ref)

            for ref, r in zip(acc_refs, res[n_out:]):
                ref[...] += r

    in_specs = []
    for a, kind in ins:
        if kind == "row":
            in_specs.append(pl.BlockSpec((tr, a.shape[1]), lambda i: (i, 0)))
        elif kind == "full":
            in_specs.append(pl.BlockSpec(a.shape, lambda i, nd=a.ndim: (0,) * nd))
        elif kind == "prev":
            in_specs.append(pl.BlockSpec((SUBLANES, a.shape[1]),
                                         lambda i: (jnp.maximum(i * (tr // SUBLANES) - 1, 0), 0)))
        else:
            in_specs.append(pl.BlockSpec((SUBLANES, a.shape[1]),
                                         lambda i: (jnp.minimum((i + 1) * (tr // SUBLANES), rows // SUBLANES - 1), 0)))
    out_specs = [pl.BlockSpec((tr, c), lambda i: (i, 0)) for c, _ in outs]
    out_specs += [pl.BlockSpec((r, c), lambda i: (0, 0)) for r, c in accs]
    out_shape = [jax.ShapeDtypeStruct((rows, c), dt) for c, dt in outs]
    out_shape += [jax.ShapeDtypeStruct((r, c), F32) for r, c in accs]
    res = pl.pallas_call(
        kern,
        name=name,
        grid=(n,),
        in_specs=in_specs,
        out_specs=out_specs,
        out_shape=out_shape,
        compiler_params=_params(("arbitrary",) if accs else ("parallel",)),
    )(*[a for a, _ in ins])
    return res


def rms_fwd(x, g, name):
    def fn(xv, gv):
        r = lax.rsqrt(jnp.mean(xv * xv, axis=1, keepdims=True) + EPS)
        return (xv * r * gv,)

    return rowwise(fn, [(x, "row"), (g, "full")], [(x.shape[1], BF16)], tr=1024, name=name)[0]


def rms_bwd(x, g, dy, dres, name):
    def fn(xv, gv, dyv, drv):
        r = lax.rsqrt(jnp.mean(xv * xv, axis=1, keepdims=True) + EPS)
        xh = xv * r
        dyg = dyv * gv
        dx = drv + r * (dyg - xh * jnp.mean(dyg * xh, axis=1, keepdims=True))
        return dx, dx, _colsum(dyv * xh)

    c = x.shape[1]
    dx, dxb, dg = rowwise(fn, [(x, "row"), (g, "full"), (dy, "row"), (dres, "row")], [(c, F32), (c, BF16)], [(1, c)],
                          tr=512, name=name)
    return (dx, dxb), dg


def ffn_up(h, wgu, name, side=None):
    s, d = h.shape
    nch, _, w = wgu.shape
    half = nch // 2
    tm = _pick(s, (512, 256, 128))

    def kern(h_ref, wg_ref, wu_ref, gu_ref, a_ref):
        hv = h_ref[...]
        g = _dot(hv, wg_ref[...])
        u = _dot(hv, wu_ref[...])
        gu_ref[0] = g.astype(BF16)
        gu_ref[1] = u.astype(BF16)
        a_ref[...] = (g * _sigmoid(g) * u).astype(BF16)

    return side_call(
        kern, side, name=name, grid=(s // tm, half),
        in_specs=[pl.BlockSpec((tm, d), lambda i, j: (i, 0)),
                  pl.BlockSpec((None, d, w), lambda i, j: (j, 0, 0)),
                  pl.BlockSpec((None, d, w), lambda i, j: (j + half, 0, 0))],
        out_specs=[pl.BlockSpec((2, tm, w), lambda i, j: (0, i, j)), pl.BlockSpec((tm, w), lambda i, j: (i, j))],
        out_shape=[jax.ShapeDtypeStruct((2, s, half * w), BF16), jax.ShapeDtypeStruct((s, half * w), BF16)],
        scratch_shapes=[], args=(h, wgu, wgu))


def ffn_dact(dxb, wdown, gu, name):
    s, d = dxb.shape
    hid = wdown.shape[0]
    tm = _pick(s, (512, 256, 128))
    tn = _pick(hid, (1408, 512, 256, 128))

    def kern(dx_ref, w_ref, gu_ref, o_ref):
        da = _dot_nt(dx_ref[...], w_ref[...])
        g, u = gu_ref[0].astype(F32), gu_ref[1].astype(F32)
        sg = _sigmoid(g)
        o_ref[0] = (da * u * sg * (1.0 + g * (1.0 - sg))).astype(BF16)
        o_ref[1] = (da * g * sg).astype(BF16)

    return pl.pallas_call(
        kern, name=name, grid=(s // tm, hid // tn),
        in_specs=[pl.BlockSpec((tm, d), lambda i, j: (i, 0)), pl.BlockSpec((tn, d), lambda i, j: (j, 0)),
                  pl.BlockSpec((2, tm, tn), lambda i, j: (0, i, j))],
        out_specs=pl.BlockSpec((2, tm, tn), lambda i, j: (0, i, j)),
        out_shape=jax.ShapeDtypeStruct((2, s, hid), BF16),
        compiler_params=_params(("parallel", "parallel")),
    )(dxb, wdown, gu)


def loss_and_grad(y, t, name):
    d = y.shape[1]

    def fn(yv, tv):
        e = yv - tv
        part = jnp.sum(_colsum(e * e), axis=1, keepdims=True) * (0.5 / d)
        dy = e * (1.0 / d)
        return dy, dy, jnp.broadcast_to(part, (SUBLANES, LANES))

    dy, dyb, acc = rowwise(fn, [(y, "row"), (t, "row")], [(d, F32), (d, BF16)], [(SUBLANES, LANES)], tr=1024, name=name)
    return acc[0, 0], (dy, dyb)


def adamw(w, g, m, v, name):
    def fn(wv, gv, mv, vv):
        m2 = ADAM_B1 * mv + (1.0 - ADAM_B1) * gv
        v2 = ADAM_B2 * vv + (1.0 - ADAM_B2) * (gv * gv)
        m_hat = m2 / (1.0 - ADAM_B1 ** ADAM_STEP)
        v_hat = v2 / (1.0 - ADAM_B2 ** ADAM_STEP)
        delta = -ADAM_LR * (m_hat / (jnp.sqrt(v_hat) + ADAM_EPS) + ADAM_WD * wv)
        return delta, m2, v2, gv

    rows, c = w.shape
    tr = _pick(rows, (512, 256, 128, 64, 32, 16, 8)) if rows % SUBLANES == 0 else rows
    if rows % SUBLANES:
        return _whole(fn, [w, g, m, v], [(w.shape, F32)] * 4, name=name)
    if 2 * 8 * tr * c * 4 > MM_VMEM_BUDGET:
        tr //= 2
    return rowwise(fn, [(w, "row"), (g, "row"), (m, "row"), (v, "row")], [(c, F32)] * 4, tr=tr, name=name)


def _whole(fn, ins, outs, *, name):
    n_in = len(ins)

    def kern(*refs):
        res = fn(*[r[...] for r in refs[:n_in]])
        for ref, r in zip(refs[n_in:], res):
            ref[...] = r.astype(ref.dtype)

    return pl.pallas_call(
        kern,
        name=name,
        out_shape=[jax.ShapeDtypeStruct(s, dt) for s, dt in outs],
        compiler_params=pltpu.CompilerParams(vmem_limit_bytes=VMEM_LIMIT),
    )(*ins)


def ffn_fwd(x, g, wgu, wdown, tag, plan):
    h = rms_fwd(x, g, f"ffn_rms_{tag}")
    gu, a = _hooked(plan, f"ffn_gu_{tag}", ffn_up, h, wgu)
    xn = mm_hooked(plan, a, wdown, add=x, name=f"ffn_down_{tag}")
    return xn, (x, h, gu, a)


def ffn_bwd(dxn, saved, g, wgu, wdown, tag):
    x, h, gu, a = saved
    dxn, dxb = dxn
    dwdown = mm(a, dxb, ta=True, out_dtype=BF16, name=f"ffn_dwdown_{tag}")
    dgu = ffn_dact(dxb, wdown, gu, f"ffn_dact_{tag}")
    dh = mm(dgu, wgu, tb=True, a_chunks=True, b_chunks=True, name=f"ffn_dh_{tag}")
    dwgu = mm(h, dgu, ta=True, b_chunks=True, out_dtype=BF16, out_chunks=True, name=f"ffn_dwgu_{tag}")
    dx, dg = rms_bwd(x, g, dh, dxn, f"ffn_drms_{tag}")
    return dx, dg, dwgu, dwdown


def _head_blockdiag(c):
    i = jnp.arange(c) // HEAD
    return (i[:, None] == i[None, :]).astype(BF16)


def _head_sums(x, bd):
    return jnp.concatenate([_dot_x2(x[:, g * LANES:(g + 1) * LANES], bd) for g in range(x.shape[1] // LANES)], axis=1)


def qknorm_fwd(qkv, qg, kg, bd, name):
    d = qkv.shape[1] // 3
    scale = 1.0 / math.sqrt(HEAD)

    def fn(v, qgv, kgv, bdv):
        v = v.astype(F32)
        q, k, vv = v[:, :d], v[:, d:2 * d], v[:, 2 * d:]
        rq = lax.rsqrt(_head_sums(q * q, bdv) * (1.0 / HEAD) + EPS)
        rk = lax.rsqrt(_head_sums(k * k, bdv) * (1.0 / HEAD) + EPS)
        return q * rq * qgv * scale, k * rk * kgv, vv

    return rowwise(fn, [(qkv, "row"), (qg, "full"), (kg, "full"), (bd, "full")],
                   [(d, BF16), (d, BF16), (d, BF16)], tr=512, name=name)


def qknorm_bwd(qkv, dqs, dkn, dv, qg, kg, bd, name):
    d = qkv.shape[1] // 3
    scale = 1.0 / math.sqrt(HEAD)

    def one(xv, gv, dyv, bdv):
        r = lax.rsqrt(_head_sums(xv * xv, bdv) * (1.0 / HEAD) + EPS)
        xh = xv * r
        dyg = dyv * gv
        dx = r * (dyg - xh * (_head_sums(dyg * xh, bdv) * (1.0 / HEAD)))
        return dx, _colsum(dyv * xh)

    def fn(v, dqv, dkv, dvv, qgv, kgv, bdv):
        v = v.astype(F32)
        q, k = v[:, :d], v[:, d:2 * d]
        dq, dqg = one(q, qgv, dqv * scale, bdv)
        dk, dkg = one(k, kgv, dkv, bdv)
        return jnp.concatenate([dq, dk, dvv], axis=1), dqg, dkg

    return rowwise(fn, [(qkv, "row"), (dqs, "row"), (dkn, "row"), (dv, "row"), (qg, "full"), (kg, "full"), (bd, "full")],
                   [(3 * d, BF16)], [(1, d), (1, d)], tr=512, name=name)


def _sb_tile(qh, k, mask, tri_gt):
    z = _dot_nt(qh, k)
    sp = jnp.log(1.0 + jnp.exp(-jnp.abs(z)))
    lb = jnp.minimum(z, 0.0) - sp
    l1 = jnp.where(mask, lb - z, 0.0)
    suf = _dot(l1.astype(BF16), tri_gt)
    return lb, l1, suf


def _sb_tri(tk):
    i = jnp.arange(tk)
    return jnp.stack([i[:, None] > i[None, :], i[:, None] < i[None, :]]).astype(BF16)


def _sb_setup(tq, tk):
    row, col = _iota2((tq, tk), 0), _iota2((tq, tk), 1)
    lane = _iota2((1, LANES), 1)
    halves = [(lane < HEAD).astype(BF16), (lane >= HEAD).astype(BF16)]
    lane_q = _iota2((tq, LANES), 1) + jnp.minimum(_iota2((tq, LANES), 0), 0)
    return row, col, halves, lane_q


def sb_attn_fwd(qs, kn, vb, tri, name, side=None):
    s, d = qs.shape
    tq, tk = min(SB_TQ, s), min(SB_TK, s)
    nq = s // tq
    assert s // tk <= LANES and s % tq == 0 and s % tk == 0

    def kern(q_ref, k_ref, v_ref, tri_ref, o_ref, rs_ref, acc_ref):
        i = pl.program_id(1)
        row, col, halves, lane_q = _sb_setup(tq, tk)
        q = q_ref[...]
        qh = [q * hm for hm in halves]
        acc_ref[...] = jnp.zeros_like(acc_ref)
        rs_ref[...] = jnp.full(rs_ref.shape, SB_UNSEEN, F32)
        nkb = (i + 1) * (tq // tk)

        def more(st):
            return (st[0] < nkb) & (st[1] > SB_DEAD)

        def step(st):
            n, r = st[0], list(st[2:])
            kb = nkb - 1 - n
            ks = pl.multiple_of(kb * tk, tk)
            k = k_ref[pl.ds(ks, tk), :]
            v = v_ref[pl.ds(ks, tk), :]
            mask = col < row + (i * tq - kb * tk)
            at_kb = lane_q == kb
            for hh in range(2):
                lb, l1, suf = _sb_tile(qh[hh], k, mask, tri_ref[0])
                w = jnp.where(mask, jnp.exp(lb + suf + r[hh]), 0.0)
                acc_ref[...] += _dot(w.astype(BF16), v * halves[hh])
                rs_ref[hh] = jnp.where(at_kb, r[hh], rs_ref[hh])
                r[hh] = r[hh] + _rowsum(l1)
            return (n + 1, jnp.maximum(jnp.max(r[0]), jnp.max(r[1])), r[0], r[1])

        z1 = jnp.zeros((tq, 1), F32)
        lax.while_loop(more, step, (jnp.int32(0), jnp.float32(0.0), z1, z1))
        o_ref[...] = acc_ref[...].astype(BF16)

    nh2 = d // LANES
    return side_call(
        kern, side,
        name=name,
        grid=(nh2, nq),
        in_specs=[pl.BlockSpec((tq, LANES), lambda h, i: (i, h)),
                  pl.BlockSpec((s, LANES), lambda h, i: (0, h)),
                  pl.BlockSpec((s, LANES), lambda h, i: (0, h)),
                  pl.BlockSpec((2, tk, tk), lambda h, i: (0, 0, 0))],
        out_specs=[pl.BlockSpec((tq, LANES), lambda h, i: (i, h)),
                   pl.BlockSpec((None, 2, tq, LANES), lambda h, i: (h, 0, i, 0))],
        out_shape=[jax.ShapeDtypeStruct((s, d), BF16), jax.ShapeDtypeStruct((nh2, 2, s, LANES), F32)],
        scratch_shapes=[pltpu.VMEM((tq, LANES), F32)],
        args=(qs, kn, vb, tri))


def sb_attn_bwd(qs, kn, vb, rsave, do, tri, name, side=None):
    s, d = qs.shape
    tq, tk = min(SB_TQ, s), min(SB_TK, s)
    nq = s // tq

    def kern(q_ref, k_ref, v_ref, rs_ref, do_ref, tri_ref, dq_ref, dk_ref, dv_ref):
        i = pl.program_id(1)

        @pl.when(i == 0)
        def _():
            dk_ref[...] = jnp.zeros_like(dk_ref)
            dv_ref[...] = jnp.zeros_like(dv_ref)

        row, col, halves, lane_q = _sb_setup(tq, tk)
        q = q_ref[...]
        qh = [q * hm for hm in halves]
        dov = do_ref[...].astype(BF16)
        doh = [dov * hm for hm in halves]
        dq_ref[...] = jnp.zeros_like(dq_ref)
        nkb = (i + 1) * (tq // tk)
        top = jnp.maximum(jnp.max(rs_ref[0], axis=0, keepdims=True), jnp.max(rs_ref[1], axis=0, keepdims=True))
        dead = (top <= SB_DEAD) & (_iota2((1, LANES), 1) < nkb)
        kstart = jnp.minimum(jnp.sum(dead.astype(F32)).astype(jnp.int32), nkb)

        def step(kb, ep):
            ep = list(ep)
            ks = pl.multiple_of(kb * tk, tk)
            k = k_ref[pl.ds(ks, tk), :]
            v = v_ref[pl.ds(ks, tk), :]
            mask = col < row + (i * tq - kb * tk)
            at_kb = lane_q == kb
            for hh in range(2):
                lb, l1, suf = _sb_tile(qh[hh], k, mask, tri_ref[0])
                r = _rowsum(jnp.where(at_kb, rs_ref[hh], 0.0))
                lbm = jnp.where(mask, lb, SB_UNSEEN)
                w = jnp.exp(lbm + suf + r)
                e = _dot_nt(doh[hh], v) * w
                pe = ep[hh] + _dot(e.astype(BF16), tri_ref[1])
                beta = jnp.exp(lbm)
                dz = (e - beta * (e + pe)).astype(BF16)
                dq_ref[...] += _dot(dz, k * halves[hh])
                dk_ref[pl.ds(ks, tk), :] += _dot_tn(dz, qh[hh])
                dv_ref[pl.ds(ks, tk), :] += _dot_tn(w.astype(BF16), doh[hh])
                ep[hh] = ep[hh] + _rowsum(e)
            return tuple(ep)

        z1 = jnp.zeros((tq, 1), F32)
        lax.fori_loop(kstart, nkb, step, (z1, z1))

    nh2 = d // LANES
    return side_call(
        kern, side,
        name=name,
        grid=(nh2, nq),
        in_specs=[pl.BlockSpec((tq, LANES), lambda h, i: (i, h)),
                  pl.BlockSpec((s, LANES), lambda h, i: (0, h)),
                  pl.BlockSpec((s, LANES), lambda h, i: (0, h)),
                  pl.BlockSpec((None, 2, tq, LANES), lambda h, i: (h, 0, i, 0)),
                  pl.BlockSpec((tq, LANES), lambda h, i: (i, h)),
                  pl.BlockSpec((2, tk, tk), lambda h, i: (0, 0, 0))],
        out_specs=[pl.BlockSpec((tq, LANES), lambda h, i: (i, h)),
                   pl.BlockSpec((s, LANES), lambda h, i: (0, h)),
                   pl.BlockSpec((s, LANES), lambda h, i: (0, h))],
        out_shape=[jax.ShapeDtypeStruct((s, d), F32)] * 3,
        scratch_shapes=[],
        args=(qs, kn, vb, rsave, do, tri))


def _hooked(plan, tag, call, *args):
    side = plan.side(tag)
    outs, side_outs = call(*args, tag, side)
    if side is not None:
        plan.done(tag, side_outs)
    return outs


def sb_fwd(x, g, wqkv, qg, kg, wo, bd, tag, plan):
    h = rms_fwd(x, g, f"sb_rms_{tag}")
    qkv = mm(h, wqkv, b_chunks=True, out_dtype=BF16, name=f"sb_qkv_{tag}")
    qs, kn, vb = qknorm_fwd(qkv, qg, kg, bd, f"sb_qknorm_{tag}")
    o, rsave = _hooked(plan, f"sb_attn_{tag}", sb_attn_fwd, qs, kn, vb, _sb_tri(min(SB_TK, x.shape[0])))
    xn = mm(o, wo(), add=x, name=f"sb_out_{tag}")
    return xn, (x, h, qkv, qs, kn, vb, rsave, o)


def sb_bwd(dxn, saved, g, wqkv, qg, kg, wo, bd, tag, plan):
    x, h, qkv, qs, kn, vb, rsave, o = saved
    dxn, dxb = dxn
    do = mm(dxb, wo, tb=True, name=f"sb_do_{tag}")
    dwo = mm(o, dxb, ta=True, out_dtype=BF16, name=f"sb_dwo_{tag}")
    dqs, dkn, dv = _hooked(plan, f"sb_dattn_{tag}", sb_attn_bwd, qs, kn, vb, rsave, do, _sb_tri(min(SB_TK, x.shape[0])))
    dqkv, dqg, dkg = qknorm_bwd(qkv, dqs, dkn, dv, qg, kg, bd, f"sb_dqknorm_{tag}")
    dh = mm(dqkv, wqkv, tb=True, b_chunks=True, name=f"sb_dh_{tag}")
    dwqkv = mm(h, dqkv, ta=True, out_dtype=BF16, out_chunks=True, name=f"sb_dwqkv_{tag}")
    dx, dg = rms_bwd(x, g, dh, dxn, f"sb_drms_{tag}")
    nh = dqg.shape[1] // HEAD
    return dx, dg, dwqkv, dqg.reshape(nh, HEAD).sum(0), dkg.reshape(nh, HEAD).sum(0), dwo


def _gelu(x):
    return 0.5 * x * (1.0 + lax.erf(x * (1.0 / math.sqrt(2.0))))


def _gelu_grad(x):
    return 0.5 * (1.0 + lax.erf(x * (1.0 / math.sqrt(2.0)))) + x * jnp.exp(-0.5 * x * x) * (1.0 / math.sqrt(2.0 * math.pi))


def gm_act_fwd(pre, vg, name):
    half = pre.shape[1] // 2

    def fn(p, vgv):
        p = p.astype(F32)
        u = _gelu(p[:, :half])
        v = _gelu(p[:, half:])
        r = lax.rsqrt(jnp.mean(v * v, axis=1, keepdims=True) + EPS)
        return u, v * r * vgv

    return rowwise(fn, [(pre, "row"), (vg, "full")], [(half, F32), (half, BF16)], tr=512, name=name)


def gm_act_bwd(pre, du, dvn, vg, name):
    half = pre.shape[1] // 2

    def fn(p, duv, dvnv, vgv):
        p = p.astype(F32)
        pu, pv = p[:, :half], p[:, half:]
        v = _gelu(pv)
        r = lax.rsqrt(jnp.mean(v * v, axis=1, keepdims=True) + EPS)
        vh = v * r
        dyg = dvnv * vgv
        dv = r * (dyg - vh * jnp.mean(dyg * vh, axis=1, keepdims=True))
        dpre = jnp.concatenate([duv * _gelu_grad(pu), dv * _gelu_grad(pv)], axis=1)
        return dpre, _colsum(dvnv * vh), _colsum(dpre)

    return rowwise(fn, [(pre, "row"), (du, "row"), (dvn, "row"), (vg, "full")],
                   [(2 * half, BF16)], [(1, half), (1, 2 * half)], tr=256, name=name)


def gm_spatial_fwd(u, vn, wc, bst, name):
    s, c = u.shape
    t = CHUNK
    ng = c // LANES

    def kern(u_ref, v_ref, w_ref, b_ref, o_ref):
        for g in range(ng):
            sl = slice(g * LANES, (g + 1) * LANES)
            mixed = _dot(w_ref[g], v_ref[:, sl]) + b_ref[:, sl]
            o_ref[:, sl] = (u_ref[:, sl] * mixed).astype(BF16)

    return pl.pallas_call(
        kern,
        name=name,
        grid=(s // t,),
        in_specs=[pl.BlockSpec((t, c), lambda i: (i, 0)), pl.BlockSpec((t, c), lambda i: (i, 0)),
                  pl.BlockSpec(wc.shape, lambda i: (0, 0, 0)), pl.BlockSpec(bst.shape, lambda i: (0, 0))],
        out_specs=pl.BlockSpec((t, c), lambda i: (i, 0)),
        out_shape=jax.ShapeDtypeStruct((s, c), BF16),
        compiler_params=_params(("parallel",)),
    )(u, vn, wc, bst)


def gm_spatial_bwd(dgate, u, vn, wc, bst, name):
    s, c = u.shape
    t = CHUNK
    ng = c // LANES

    def kern(dg_ref, u_ref, v_ref, w_ref, b_ref, du_ref, dv_ref, dw_ref, db_ref):
        i = pl.program_id(0)

        @pl.when(i == 0)
        def _():
            dw_ref[...] = jnp.zeros_like(dw_ref)
            db_ref[...] = jnp.zeros_like(db_ref)

        for g in range(ng):
            sl = slice(g * LANES, (g + 1) * LANES)
            vg = v_ref[:, sl]
            dgv = dg_ref[:, sl]
            mixed = _dot(w_ref[g], vg) + b_ref[:, sl]
            du_ref[:, sl] = dgv * mixed
            dmix = dgv * u_ref[:, sl]
            dmb = dmix.astype(BF16)
            dv_ref[:, sl] = _dot_tn(w_ref[g], dmb)
            dw_ref[g] += _dot_nt(dmb, vg)
            db_ref[:, sl] += dmix

    return pl.pallas_call(
        kern,
        name=name,
        grid=(s // t,),
        in_specs=[pl.BlockSpec((t, c), lambda i: (i, 0))] * 3 +
                 [pl.BlockSpec(wc.shape, lambda i: (0, 0, 0)), pl.BlockSpec(bst.shape, lambda i: (0, 0))],
        out_specs=[pl.BlockSpec((t, c), lambda i: (i, 0)), pl.BlockSpec((t, c), lambda i: (i, 0)),
                   pl.BlockSpec(wc.shape, lambda i: (0, 0, 0)), pl.BlockSpec(bst.shape, lambda i: (0, 0))],
        out_shape=[jax.ShapeDtypeStruct((s, c), F32), jax.ShapeDtypeStruct((s, c), F32),
                   jax.ShapeDtypeStruct(wc.shape, F32), jax.ShapeDtypeStruct(bst.shape, F32)],
        compiler_params=_params(("arbitrary",)),
    )(dgate, u, vn, wc, bst)


def gm_fwd(x, g, w_in, b_in, vg, wc, bst, w_out, tag):
    h = rms_fwd(x, g, f"gm_rms_{tag}")
    pre = mm(h, w_in, bias=b_in, b_chunks=True, out_dtype=BF16, name=f"gm_in_{tag}")
    u, vn = gm_act_fwd(pre, vg, f"gm_act_{tag}")
    gate = gm_spatial_fwd(u, vn, wc, bst, f"gm_spatial_{tag}")
    xn = mm(gate, w_out, add=x, name=f"gm_out_{tag}")
    return xn, (x, h, pre, u, vn, gate)


def gm_bwd(dxn, saved, g, w_in, vg, wc, bst, w_out, tag):
    x, h, pre, u, vn, gate = saved
    dxn, dxb = dxn
    dgate = mm(dxb, w_out, tb=True, name=f"gm_dgate_{tag}")
    dwout = mm(gate, dxb, ta=True, out_dtype=BF16, name=f"gm_dwout_{tag}")
    du, dvn, dws, dbst = gm_spatial_bwd(dgate, u, vn, wc, bst, f"gm_dspatial_{tag}")
    dpre, dvg, dbin = gm_act_bwd(pre, du, dvn, vg, f"gm_dact_{tag}")
    dh = mm(dpre, w_in, tb=True, b_chunks=True, name=f"gm_dh_{tag}")
    dwin = mm(h, dpre, ta=True, out_dtype=BF16, out_chunks=True, name=f"gm_dwin_{tag}")
    dx, dg = rms_bwd(x, g, dh, dxn, f"gm_drms_{tag}")
    ng = wc.shape[0]
    dws = jnp.where(jnp.tril(jnp.ones((CHUNK, CHUNK), bool)), dws, 0.0)
    dbs = dbst.reshape(CHUNK, ng, LANES).sum(-1).T
    return dx, dg, dwin, dbin, dvg, dws, dbs, dwout


def _conv_taps(xv, prev):
    cat = jnp.concatenate([prev, xv], axis=0)
    return [pltpu.roll(cat, sh, 0)[SUBLANES:] for sh in (3, 2, 1)] + [xv]


def conv_fwd(xbc, ws, b, d_inner, name):
    c = xbc.shape[1]
    nst = (c - d_inner) // 2

    def fn(xv, prev, w0, w1, w2, w3, bv):
        taps = _conv_taps(xv, prev)
        pre = bv + w0 * taps[0] + w1 * taps[1] + w2 * taps[2] + w3 * taps[3]
        out = pre * _sigmoid(pre)
        return out[:, :d_inner], out[:, d_inner:d_inner + nst], out[:, d_inner + nst:]

    return rowwise(fn, [(xbc, "row"), (xbc, "prev")] + [(w, "full") for w in ws] + [(b, "full")],
                   [(d_inner, F32), (nst, F32), (nst, F32)], tr=512, name=name)


def conv_bwd_pre(xbc, ws, b, dxs_a, dxs_b, db_m, dc_m, name):
    c = xbc.shape[1]

    def fn(xv, prev, w0, w1, w2, w3, bv, da, db2, dbm, dcm):
        taps = _conv_taps(xv, prev)
        pre = bv + w0 * taps[0] + w1 * taps[1] + w2 * taps[2] + w3 * taps[3]
        sg = _sigmoid(pre)
        dout = jnp.concatenate([da + db2, dbm, dcm], axis=1)
        dpre = dout * sg * (1.0 + pre * (1.0 - sg))
        return (dpre,) + tuple(_colsum(dpre * tp) for tp in taps) + (_colsum(dpre),)

    return rowwise(fn, [(xbc, "row"), (xbc, "prev")] + [(w, "full") for w in ws] +
                   [(b, "full"), (dxs_a, "row"), (dxs_b, "row"), (db_m, "row"), (dc_m, "row")],
                   [(c, F32)], [(1, c)] * 5, tr=256, name=name)


def conv_bwd_in(dpre, ws, name):
    c = dpre.shape[1]

    def fn(dv, nxt, w0, w1, w2, w3):
        cat = jnp.concatenate([dv, nxt], axis=0)
        n = cat.shape[0]
        up = [pltpu.roll(cat, n - sh, 0)[:dv.shape[0]] for sh in (1, 2, 3)]
        return (w3 * dv + w2 * up[0] + w1 * up[1] + w0 * up[2],)

    return rowwise(fn, [(dpre, "row"), (dpre, "next")] + [(w, "full") for w in ws], [(c, BF16)], tr=512, name=name)[0]


def ssd_pre(dtr, bias, alog, name):
    def fn(d, bv, al, tri):
        dt = _softplus(d + bv)
        a = dt * (-jnp.exp(al))
        return dt, _dot_x3_left(tri, a)

    tri = jnp.tril(jnp.ones((CHUNK, CHUNK), BF16))
    return rowwise(fn, [(dtr, "row"), (bias, "full"), (alog, "full"), (tri, "full")],
                   [(LANES, F32), (LANES, F32)], tr=CHUNK, name=name)


def _ssd_layouts(v, ngroups, hpg):
    s = v.shape[0]
    col = v[:, :ngroups * hpg].T.reshape(ngroups, hpg, s, 1)
    return jnp.broadcast_to(col, (ngroups, hpg, s, LANES))


def _ssd_rowform(acum, ngroups, hpg):
    s = acum.shape[0]
    nc = s // CHUNK
    a = acum[:, :ngroups * hpg].reshape(nc, CHUNK, ngroups, hpg).transpose(2, 0, 3, 1)
    last = jnp.broadcast_to(a[..., CHUNK - 1:], a.shape)
    return jnp.concatenate([a, last], axis=2)


def ssd_chunk_fwd(xs, bm, cm, col_a, col_dt, rowf, name, side=None):
    s, d_inner = xs.shape
    ln = CHUNK
    nc = s // ln
    nsub = _pick(nc, (SSD_SUB, 2, 1))
    rows = nsub * ln
    ng, hpg = col_a.shape[0], col_a.shape[1]
    gw = d_inner // ng
    assert gw == hpg * HEAD and gw % LANES == 0 and bm.shape[1] == ng * LANES

    def kern(x_ref, b_ref, c_ref, ca_ref, cd_ref, rf_ref, y_ref, hp_ref, h_scr):
        @pl.when(pl.program_id(1) == 0)
        def _():
            h_scr[...] = jnp.zeros_like(h_scr)

        causal = _iota2((ln, ln), 0) >= _iota2((ln, ln), 1)
        lane = _iota2((1, LANES), 1)
        for sc in range(nsub):
            rs = slice(sc * ln, (sc + 1) * ln)
            bb = b_ref[rs, :].astype(BF16)
            cbf = c_ref[rs, :].astype(BF16)
            cb = _dot_nt(cbf, bb)
            ys = [jnp.zeros((ln, LANES), F32) for _ in range(gw // LANES)]
            for r in range(hpg):
                j, hf = divmod(r, LANES // HEAD)
                mh = ((lane >= HEAD * hf) & (lane < HEAD * (hf + 1))).astype(F32)
                ac = ca_ref[r, rs, :]
                ar = rf_ref[sc, pl.ds(r, 1), :]
                aend = rf_ref[sc, pl.ds(4 + r, 1), :]
                dm = jnp.exp(jnp.minimum(ac - ar, 0.0))
                m = jnp.where(causal, cb * dm, 0.0).astype(BF16)
                xdt = x_ref[rs, j * LANES:(j + 1) * LANES] * cd_ref[r, rs, :] * mh
                h = h_scr[r]
                hp_ref[sc, r] = h
                ys[j] = ys[j] + _dot(m, xdt.astype(BF16)) + _dot_nt(cbf, h.astype(BF16)) * jnp.exp(ac)
                dte = jnp.exp(aend - ac)
                h_scr[r] = jnp.exp(aend) * h + _dot_tn((xdt * dte).astype(BF16), bb)
            for j in range(gw // LANES):
                y_ref[rs, j * LANES:(j + 1) * LANES] = ys[j]

    colspec = pl.BlockSpec((None, hpg, rows, LANES), lambda g, c: (g, 0, c, 0))
    return side_call(
        kern, side,
        name=name,
        grid=(ng, nc // nsub),
        in_specs=[pl.BlockSpec((rows, gw), lambda g, c: (c, g)),
                  pl.BlockSpec((rows, LANES), lambda g, c: (c, g)),
                  pl.BlockSpec((rows, LANES), lambda g, c: (c, g)),
                  colspec, colspec,
                  pl.BlockSpec((None, nsub, 8, LANES), lambda g, c: (g, c, 0, 0))],
        out_specs=[pl.BlockSpec((rows, gw), lambda g, c: (c, g)),
                   pl.BlockSpec((None, nsub, hpg, LANES, LANES), lambda g, c: (g, c, 0, 0, 0))],
        out_shape=[jax.ShapeDtypeStruct((s, d_inner), F32),
                   jax.ShapeDtypeStruct((ng, nc, hpg, LANES, LANES), F32)],
        scratch_shapes=[pltpu.VMEM((hpg, LANES, LANES), F32)],
        args=(xs, bm, cm, col_a, col_dt, rowf))


def ssd_chunk_bwd(xs, bm, cm, col_a, col_dt, rowf, hprev, dy, name, side=None):
    s, d_inner = xs.shape
    ln = CHUNK
    nc = s // ln
    nsub = _pick(nc, (SSD_SUB, 2, 1))
    rows = nsub * ln
    ng, hpg = col_a.shape[0], col_a.shape[1]
    gw = d_inner // ng

    def kern(x_ref, b_ref, c_ref, ca_ref, cd_ref, rf_ref, hp_ref, dy_ref,
             dx_ref, db_ref, dc_ref, ddt_ref, da_ref, dh_scr):
        @pl.when(pl.program_id(1) == 0)
        def _():
            dh_scr[...] = jnp.zeros_like(dh_scr)

        row, col = _iota2((ln, ln), 0), _iota2((ln, ln), 1)
        causal = row >= col
        tri_ge = (col >= row).astype(BF16)
        ones = jnp.ones((ln, LANES), BF16)
        lane = _iota2((1, LANES), 1)
        last_row = (_iota2((ln, 1), 0) == ln - 1).astype(F32)
        for sc in reversed(range(nsub)):
            rs = slice(sc * ln, (sc + 1) * ln)
            bb = b_ref[rs, :].astype(BF16)
            cbf = c_ref[rs, :].astype(BF16)
            cb = _dot_nt(cbf, bb)
            dcb = jnp.zeros((ln, ln), F32)
            d_b = jnp.zeros((ln, LANES), F32)
            d_c = jnp.zeros((ln, LANES), F32)
            dxs = [jnp.zeros((ln, LANES), F32) for _ in range(gw // LANES)]
            for r in range(hpg):
                j, hf = divmod(r, LANES // HEAD)
                mh = ((lane >= HEAD * hf) & (lane < HEAD * (hf + 1))).astype(F32)
                ac = ca_ref[r, rs, :]
                dt = cd_ref[r, rs, :]
                ar = rf_ref[sc, pl.ds(r, 1), :]
                aend = rf_ref[sc, pl.ds(4 + r, 1), :]
                dm = jnp.where(causal, jnp.exp(jnp.minimum(ac - ar, 0.0)), 0.0)
                m = cb * dm
                mb = m.astype(BF16)
                xp = x_ref[rs, j * LANES:(j + 1) * LANES]
                xdt = xp * dt * mh
                xdtb = xdt.astype(BF16)
                dyp = dy_ref[rs, j * LANES:(j + 1) * LANES] * mh
                dypb = dyp.astype(BF16)
                h = hp_ref[sc, r]
                hb = h.astype(BF16)
                dh = dh_scr[r]
                dhb = dh.astype(BF16)
                e_in = jnp.exp(ac)
                dte = jnp.exp(aend - ac)
                eend = jnp.exp(aend)
                d_m = _dot_nt(dypb, xdtb)
                dcb = dcb + d_m * dm
                gm = d_m * m
                yoff_pre = _dot_nt(cbf, hb)
                bdh = _dot_nt(bb, dhb)
                dxdt = _dot_tn(mb, dypb) + bdh * dte
                t1 = _rowsum(xdt * bdh) * dte
                gh, gl = _split2(gm)
                dacum = (_rowsum(gm) - (_dot_tn(gh, ones) + _dot_tn(gl, ones))
                         + _rowsum(dyp * yoff_pre) * e_in - t1)
                end_term = _colsum(t1) + eend * jnp.sum(_colsum(dh * h), axis=1, keepdims=True)
                dacum = dacum + last_row * end_term
                da_ref[r, rs, :] = _dot_x3_left(tri_ge, dacum)
                ddt_ref[r, rs, :] = jnp.broadcast_to(_rowsum(dxdt * xp), (ln, LANES))
                dxs[j] = dxs[j] + dxdt * dt
                d_b = d_b + _dot((xdt * dte).astype(BF16), dhb)
                dye = (dyp * e_in).astype(BF16)
                d_c = d_c + _dot(dye, hb)
                dh_scr[r] = eend * dh + _dot_tn(dye, cbf)
            dcbb = dcb.astype(BF16)
            dc_ref[rs, :] = d_c + _dot(dcbb, bb)
            db_ref[rs, :] = d_b + _dot_tn(dcbb, cbf)
            for j in range(gw // LANES):
                dx_ref[rs, j * LANES:(j + 1) * LANES] = dxs[j]

    rev = nc // nsub - 1
    colspec = pl.BlockSpec((None, hpg, rows, LANES), lambda g, c: (g, 0, rev - c, 0))
    return side_call(
        kern, side,
        name=name,
        grid=(ng, nc // nsub),
        in_specs=[pl.BlockSpec((rows, gw), lambda g, c: (rev - c, g)),
                  pl.BlockSpec((rows, LANES), lambda g, c: (rev - c, g)),
                  pl.BlockSpec((rows, LANES), lambda g, c: (rev - c, g)),
                  colspec, colspec,
                  pl.BlockSpec((None, nsub, 8, LANES), lambda g, c: (g, rev - c, 0, 0)),
                  pl.BlockSpec((None, nsub, hpg, LANES, LANES), lambda g, c: (g, rev - c, 0, 0, 0)),
                  pl.BlockSpec((rows, gw), lambda g, c: (rev - c, g))],
        out_specs=[pl.BlockSpec((rows, gw), lambda g, c: (rev - c, g)),
                   pl.BlockSpec((rows, LANES), lambda g, c: (rev - c, g)),
                   pl.BlockSpec((rows, LANES), lambda g, c: (rev - c, g)),
                   colspec, colspec],
        out_shape=[jax.ShapeDtypeStruct((s, d_inner), F32),
                   jax.ShapeDtypeStruct(bm.shape, F32), jax.ShapeDtypeStruct(cm.shape, F32),
                   jax.ShapeDtypeStruct(col_a.shape, F32), jax.ShapeDtypeStruct(col_a.shape, F32)],
        scratch_shapes=[pltpu.VMEM((hpg, LANES, LANES), F32)],
        args=(xs, bm, cm, col_a, col_dt, rowf, hprev, dy))


def gnorm_fwd(y, xs, z, dexp, gain, ngroups, name):
    c = y.shape[1]
    gw = c // ngroups

    def fn(yv, xv, zv, dv, gv):
        yg = (yv + xv * dv) * (zv * _sigmoid(zv))
        outs = []
        for k in range(ngroups):
            t = yg[:, k * gw:(k + 1) * gw]
            outs.append(t * lax.rsqrt(jnp.mean(t * t, axis=1, keepdims=True) + EPS))
        return (jnp.concatenate(outs, axis=1) * gv,)

    return rowwise(fn, [(y, "row"), (xs, "row"), (z, "row"), (dexp, "full"), (gain, "full")], [(c, BF16)], tr=512, name=name)[0]


def gnorm_bwd(dn, y, xs, z, dexp, gain, ngroups, name):
    c = y.shape[1]
    gw = c // ngroups

    def fn(dnv, yv, xv, zv, dv, gv):
        yd = yv + xv * dv
        sg = _sigmoid(zv)
        sz = zv * sg
        yg = yd * sz
        dng = dnv * gv
        dyg, yh = [], []
        for k in range(ngroups):
            sl = slice(k * gw, (k + 1) * gw)
            t = yg[:, sl]
            r = lax.rsqrt(jnp.mean(t * t, axis=1, keepdims=True) + EPS)
            th = t * r
            dyg.append(r * (dng[:, sl] - th * jnp.mean(dng[:, sl] * th, axis=1, keepdims=True)))
            yh.append(th)
        dyg = jnp.concatenate(dyg, axis=1)
        yh = jnp.concatenate(yh, axis=1)
        dyd = dyg * sz
        dz = dyg * yd * (sg * (1.0 + zv * (1.0 - sg)))
        return dyd, dyd * dv, dz, _colsum(dyd * xv), _colsum(dnv * yh)

    return rowwise(fn, [(dn, "row"), (y, "row"), (xs, "row"), (z, "row"), (dexp, "full"), (gain, "full")],
                   [(c, F32), (c, F32), (c, BF16)], [(1, c), (1, c)], tr=256, name=name)


def ssd_post(ddt, da, dt, dtr, bias, alog, name):
    def fn(ddtv, dav, dtv, dtrv, bv, al):
        a_neg = -jnp.exp(al)
        ddtr = (ddtv + dav * a_neg) * _sigmoid(dtrv + bv)
        return ddtr, _colsum(ddtr), _colsum(dav * dtv) * a_neg

    return rowwise(fn, [(ddt, "row"), (da, "row"), (dt, "row"), (dtr, "row"), (bias, "full"), (alog, "full")],
                   [(LANES, BF16)], [(1, LANES), (1, LANES)], tr=512, name=name)


def _from_colform(v, s):
    ng, hpg = v.shape[0], v.shape[1]
    flat = v[..., 0].reshape(ng * hpg, s).T
    return jnp.pad(flat, ((0, 0), (0, LANES - ng * hpg)))


def ssm_fwd(x, g, p, tag, plan):
    ng, hpg, d_inner = p["ng"], p["hpg"], p["d_inner"]
    h = rms_fwd(x, g, f"ssm_rms_{tag}")
    z = mm(h, p["w_z"], name=f"ssm_inz_{tag}")
    xbc = mm(h, p["w_xbc"], name=f"ssm_inx_{tag}")
    dtr = mm(h, p["w_dt"], name=f"ssm_indt_{tag}")
    xs, bm, cm = conv_fwd(xbc, p["conv_w"], p["conv_b"], d_inner, f"ssm_conv_{tag}")
    dt, acum = ssd_pre(dtr, p["dt_bias"], p["a_log"], f"ssm_pre_{tag}")
    col_a, col_dt = _ssd_layouts(acum, ng, hpg), _ssd_layouts(dt, ng, hpg)
    rowf = _ssd_rowform(acum, ng, hpg)
    y, hprev = _hooked(plan, f"ssm_scan_{tag}", ssd_chunk_fwd, xs, bm, cm, col_a, col_dt, rowf)
    n = gnorm_fwd(y, xs, z, p["d_exp"], p["norm_gain"], ng, f"ssm_gnorm_{tag}")
    xn = mm(n, p["w_out"], add=x, name=f"ssm_out_{tag}")
    return xn, (x, h, z, xbc, dtr, xs, bm, cm, dt, col_a, col_dt, rowf, y, hprev, n)


def ssm_bwd(dxn, saved, g, p, tag, plan):
    x, h, z, xbc, dtr, xs, bm, cm, dt, col_a, col_dt, rowf, y, hprev, n = saved
    ng, hpg, d_inner = p["ng"], p["hpg"], p["d_inner"]
    s = x.shape[0]
    dxn, dxb = dxn
    dn = mm(dxb, p["w_out"], tb=True, name=f"ssm_dn_{tag}")
    dwout = mm(n, dxb, ta=True, out_dtype=BF16, name=f"ssm_dwout_{tag}")
    dy, dxs_skip, dz, dd_lane, dgain = gnorm_bwd(dn, y, xs, z, p["d_exp"], p["norm_gain"], ng, f"ssm_dgnorm_{tag}")
    dxs, dbm, dcm, ddt_c, da_c = _hooked(plan, f"ssm_dscan_{tag}", ssd_chunk_bwd, xs, bm, cm, col_a, col_dt, rowf, hprev, dy)
    ddtr, dbias, dalog = ssd_post(_from_colform(ddt_c, s), _from_colform(da_c, s), dt, dtr,
                                  p["dt_bias"], p["a_log"], f"ssm_post_{tag}")
    res = conv_bwd_pre(xbc, p["conv_w"], p["conv_b"], dxs, dxs_skip, dbm, dcm, f"ssm_dconv_{tag}")
    dpre, dconv_w, dconv_b = res[0], jnp.concatenate(res[1:5], axis=0), res[5]
    dxbc = conv_bwd_in(dpre, p["conv_w"], f"ssm_dconvin_{tag}")
    dh = mm(dz, p["w_z"], tb=True, name=f"ssm_dhz_{tag}")
    dh = mm(dxbc, p["w_xbc"], tb=True, add=dh, name=f"ssm_dhx_{tag}")
    dh = mm(ddtr, p["w_dt"], tb=True, add=dh, name=f"ssm_dhdt_{tag}")
    dwz = mm(h, dz, ta=True, out_dtype=BF16, name=f"ssm_dwz_{tag}")
    dwxbc = mm(h, dxbc, ta=True, out_dtype=BF16, name=f"ssm_dwxbc_{tag}")
    dwdt = mm(h, ddtr, ta=True, out_dtype=BF16, name=f"ssm_dwdt_{tag}")
    dx, dg = rms_bwd(x, g, dh, dxn, f"ssm_drms_{tag}")
    nh = ng * hpg
    dwin = jnp.concatenate([dwz, dwxbc, dwdt[:, :nh]], axis=1)
    dd = dd_lane.reshape(nh, HEAD).sum(-1)
    return dx, dg, dict(w_in=dwin, conv_w=dconv_w, conv_b=dconv_b, dt_bias=dbias[0, :nh], a_log=dalog[0, :nh],
                        d=dd, norm_gain=dgain, w_out=dwout)


def local_step(x, target, w, plan):
    d = x.shape[1]
    depth = w["mix_norm"].shape[0]
    bd = _head_blockdiag(LANES)
    tril = jnp.tril(jnp.ones((CHUNK, CHUNK), bool))
    ssm_heads = w["ssm_dt_bias"].shape[1]
    d_inner = w["ssm_norm_gain"].shape[1]
    ng = w["ssm_norm_gain"].shape[1] // 256
    nstate = CHUNK

    def pad_lanes(v):
        return jnp.pad(v, ((0, 0), (0, LANES - v.shape[1])))

    def ssm_params(j):
        w_in = w["ssm_w_in"][j]
        cw = w["ssm_conv_w"][j]
        return dict(ng=ng, hpg=ssm_heads // ng, d_inner=d_inner,
                    w_z=w_in[:, :d_inner], w_xbc=w_in[:, d_inner:d_inner + d_inner + 2 * ng * nstate],
                    w_dt=pad_lanes(w_in[:, 2 * d_inner + 2 * ng * nstate:]),
                    conv_w=[cw[k:k + 1] for k in range(cw.shape[0])], conv_b=w["ssm_conv_b"][j:j + 1],
                    dt_bias=pad_lanes(w["ssm_dt_bias"][j:j + 1]), a_log=pad_lanes(w["ssm_a_log"][j:j + 1]),
                    d_exp=jnp.repeat(w["ssm_d"][j], HEAD)[None, :], norm_gain=w["ssm_norm_gain"][j:j + 1],
                    w_out=w["ssm_w_out"][j])

    def gm_params(j):
        wc = jnp.where(tril, w["gm_w_s"][j], 0.0).astype(BF16)
        bst = jnp.repeat(w["gm_b_s"][j].T, LANES, axis=1)
        return wc, bst

    def sb_gains(j):
        nh = d // HEAD
        return jnp.tile(w["sb_q_gain"][j], nh)[None, :], jnp.tile(w["sb_k_gain"][j], nh)[None, :]

    saved = []
    cur = x
    for i in range(depth):
        kind, j = i % 3, i // 3
        gmix = w["mix_norm"][i:i + 1]
        if kind == 0:
            qg, kg = sb_gains(j)
            cur, sv = sb_fwd(cur, gmix, w["sb_w_qkv"][j], qg, kg, lambda j=j: w["sb_w_o"][j], bd, f"{i}", plan)
        elif kind == 1:
            wc, bst = gm_params(j)
            cur, sv = gm_fwd(cur, gmix, w["gm_w_in"][j], w["gm_b_in"][j:j + 1], w["gm_v_gain"][j:j + 1], wc, bst,
                             w["gm_w_out"][j], f"{i}")
        else:
            cur, sv = ssm_fwd(cur, gmix, ssm_params(j), f"{i}", plan)
        cur, sv2 = ffn_fwd(cur, w["ffn_norm"][i:i + 1], w["ffn_w_gu"][i], w["ffn_w_down"][i], f"{i}", plan)
        saved.append((sv, sv2))

    loss, dcur = loss_and_grad(cur, target, "loss")

    grads = {k: [None] * len(v) for k, v in w.items()}
    for i in reversed(range(depth)):
        kind, j = i % 3, i // 3
        sv, sv2 = saved[i]
        gmix = w["mix_norm"][i:i + 1]
        dcur, dgf, dwgu, dwdown = ffn_bwd(dcur, sv2, w["ffn_norm"][i:i + 1], w["ffn_w_gu"][i], w["ffn_w_down"][i], f"{i}")
        grads["ffn_norm"][i], grads["ffn_w_gu"][i], grads["ffn_w_down"][i] = dgf[0], dwgu, dwdown
        plan.grads_ready({("ffn_w_gu", i): dwgu, ("ffn_w_down", i): dwdown})
        if kind == 0:
            qg, kg = sb_gains(j)
            dcur, dg, dwqkv, dqg, dkg, dwo = sb_bwd(dcur, sv, gmix, w["sb_w_qkv"][j], qg, kg, w["sb_w_o"][j], bd, f"{i}", plan)
            grads["sb_w_qkv"][j], grads["sb_q_gain"][j], grads["sb_k_gain"][j], grads["sb_w_o"][j] = dwqkv, dqg, dkg, dwo
        elif kind == 1:
            wc, bst = gm_params(j)
            dcur, dg, dwin, dbin, dvg, dws, dbs, dwout = gm_bwd(dcur, sv, gmix, w["gm_w_in"][j], w["gm_v_gain"][j:j + 1],
                                                                 wc, bst, w["gm_w_out"][j], f"{i}")
            grads["gm_w_in"][j], grads["gm_b_in"][j], grads["gm_v_gain"][j] = dwin, dbin[0], dvg[0]
            grads["gm_w_s"][j], grads["gm_b_s"][j], grads["gm_w_out"][j] = dws, dbs, dwout
        else:
            dcur, dg, gs = ssm_bwd(dcur, sv, gmix, ssm_params(j), f"{i}", plan)
            grads["ssm_w_in"][j], grads["ssm_conv_w"][j], grads["ssm_conv_b"][j] = gs["w_in"], gs["conv_w"], gs["conv_b"][0]
            grads["ssm_dt_bias"][j], grads["ssm_a_log"][j], grads["ssm_d"][j] = gs["dt_bias"], gs["a_log"], gs["d"]
            grads["ssm_norm_gain"][j], grads["ssm_w_out"][j] = gs["norm_gain"][0], gs["w_out"]
        grads["mix_norm"][i] = dg[0]
        mixer = {0: ("sb_w_qkv", "sb_w_o"), 1: ("gm_w_in", "gm_w_out"), 2: ("ssm_w_in", "ssm_w_out")}[kind]
        plan.grads_ready({(n, j): grads[n][j] for n in mixer})
    grads = {k: (v if k in MATRICES else jnp.stack(v)) for k, v in grads.items()}
    return loss, dcur[0], grads


WEIGHTS = ["mix_norm", "ffn_norm", "sb_w_qkv", "sb_q_gain", "sb_k_gain", "sb_w_o", "gm_w_in", "gm_b_in", "gm_v_gain",
           "gm_w_s", "gm_b_s", "gm_w_out", "ssm_w_in", "ssm_conv_w", "ssm_conv_b", "ssm_dt_bias", "ssm_a_log", "ssm_d",
           "ssm_norm_gain", "ssm_w_out", "ffn_w_gu", "ffn_w_down"]
SHARDED = {"sb_w_qkv": 2, "sb_w_o": 1, "gm_w_in": 2, "gm_w_out": 1, "ssm_w_in": 2, "ssm_conv_w": 2, "ssm_conv_b": 1,
           "ssm_norm_gain": 1, "ssm_w_out": 1, "ffn_w_gu": 2, "ffn_w_down": 1}
EXACT = ("ssm_conv_w", "ssm_conv_b", "ssm_norm_gain")
MATRICES = tuple(n for n in SHARDED if n not in EXACT)
COLUMN_BLOCKS = ("sb_w_qkv", "gm_w_in", "ffn_w_gu")
REPLICATED = [n for n in WEIGHTS if n not in SHARDED]
N_CHIPS = 4
N_DEV = 8
PACK_COLS = 1024


def _pack(pieces, dtype, align):
    flat = jnp.concatenate([p.reshape(-1).astype(dtype) for p in pieces])
    rows = -(-flat.shape[0] // (PACK_COLS * align)) * align
    flat = jnp.pad(flat, (0, rows * PACK_COLS - flat.shape[0]))
    return flat.reshape(rows, PACK_COLS)


def _unpack(flat, shapes):
    out, off = [], 0
    for shp in shapes:
        n = math.prod(shp)
        out.append(flat[off:off + n].reshape(shp))
        off += n
    return out


ANY = pl.BlockSpec(memory_space=pl.ANY)


def _pos():
    return lax.axis_index("x"), lax.axis_index("y"), lax.axis_index("c")


def _remote(src, dst, send, recv, k, to):
    return pltpu.make_async_remote_copy(src_ref=src, dst_ref=dst, send_sem=send.at[k], recv_sem=recv.at[k],
                                        device_id=to, device_id_type=MESH_ID)


def _comm_call(body, name, ins, out_shapes, nsem, aliases=None):
    return pl.pallas_call(
        body, name=name, out_shape=out_shapes,
        in_specs=[ANY] * len(ins), out_specs=[ANY] * len(out_shapes),
        scratch_shapes=[pltpu.SemaphoreType.DMA((nsem,)), pltpu.SemaphoreType.DMA((nsem,))],
        input_output_aliases=aliases or {},
    )(*ins)


def stage_shard(w, layer, chip, name):
    _, rows, cols = w.shape
    tr = _pick(rows, (512, 352, 256, 128))

    def kern(idx_ref, w_ref, o_ref):
        o_ref[...] = w_ref[...].astype(BF16)

    grid_spec = pltpu.PrefetchScalarGridSpec(
        num_scalar_prefetch=1, grid=(rows // tr,),
        in_specs=[pl.BlockSpec((None, tr, cols), lambda i, idx: (layer, i, 0))],
        out_specs=pl.BlockSpec((None, tr, cols), lambda i, idx: (idx[0], i, 0)))
    return pl.pallas_call(
        kern, name=name, grid_spec=grid_spec,
        out_shape=jax.ShapeDtypeStruct((N_CHIPS, rows, cols), BF16),
        compiler_params=_params(("parallel",)),
    )(jnp.reshape(chip, (1,)).astype(jnp.int32), w)


class Side:
    def __init__(self, arrays, out_shapes, aliases, nsem, start, finish):
        self.arrays, self.out_shapes, self.aliases, self.nsem = list(arrays), list(out_shapes), aliases, nsem
        self.start, self.finish = start, finish


def run_side(side, name):
    n_in, n_out = len(side.arrays), len(side.out_shapes)

    def body(*refs):
        ins, outs = refs[:n_in], refs[n_in:n_in + n_out]
        send, recv = refs[n_in + n_out:]
        side.start(ins, outs, send, recv)
        side.finish(ins, outs, send, recv)

    return _comm_call(body, name, side.arrays, side.out_shapes, side.nsem, aliases=side.aliases)


def side_call(kern, side, *, name, grid, in_specs, out_specs, out_shape, scratch_shapes, args):
    if side is None:
        res = pl.pallas_call(kern, name=name, grid=grid, in_specs=in_specs, out_specs=out_specs, out_shape=out_shape,
                             scratch_shapes=scratch_shapes,
                             compiler_params=_params(("parallel",) + ("arbitrary",) * (len(grid) - 1)))(*args)
        return list(res), []
    n_in, n_out, n_scr = len(in_specs), len(out_specs), len(scratch_shapes)
    s_in, s_out = len(side.arrays), len(side.out_shapes)

    def body(*refs):
        ins, refs = refs[:n_in], refs[n_in:]
        side_ins, refs = refs[:s_in], refs[s_in:]
        outs, refs = refs[:n_out], refs[n_out:]
        side_outs, refs = refs[:s_out], refs[s_out:]
        scr, (send, recv) = refs[:n_scr], refs[n_scr:]
        first, last = None, None
        for axis, size in enumerate(grid):
            at0, at1 = pl.program_id(axis) == 0, pl.program_id(axis) == size - 1
            first = at0 if first is None else first & at0
            last = at1 if last is None else last & at1

        @pl.when(first)
        def _():
            side.start(side_ins, side_outs, send, recv)

        kern(*ins, *outs, *scr)

        @pl.when(last)
        def _():
            side.finish(side_ins, side_outs, send, recv)

    res = pl.pallas_call(
        body, name=name, grid=grid,
        in_specs=list(in_specs) + [ANY] * s_in, out_specs=list(out_specs) + [ANY] * s_out,
        out_shape=list(out_shape) + side.out_shapes,
        scratch_shapes=list(scratch_shapes) + [pltpu.SemaphoreType.DMA((side.nsem,)), pltpu.SemaphoreType.DMA((side.nsem,))],
        input_output_aliases={n_in + a: n_out + b for a, b in side.aliases.items()},
        compiler_params=_params(("arbitrary",) * len(grid)),
    )(*args, *side.arrays)
    return list(res[:n_out]), list(res[n_out:])


def gather_side(staged):
    n = len(staged)

    def plan(o_refs, send, recv):
        x, y, c = _pos()
        chips = [(1 - x, y), (x, 1 - y), (1 - x, 1 - y)]

        def part(u, chip, cc):
            half = staged[u].shape[1] // 2
            return o_refs[u].at[2 * chip[0] + chip[1], pl.ds(cc * half, half), :]

        first = [_remote(part(u, (x, y), c), part(u, (x, y), c), send, recv, 6 * u + j, (*chip, c))
                 for u in range(n) for j, chip in enumerate(chips)]
        landed = [_remote(part(u, chip, c), part(u, chip, c), send, recv, 6 * u + j, (x, y, c))
                  for u in range(n) for j, chip in enumerate(chips)]
        passed = [_remote(part(u, chip, c), part(u, chip, c), send, recv, 6 * u + 3 + j, (x, y, 1 - c))
                  for u in range(n) for j, chip in enumerate(chips)]
        handed = [_remote(part(u, chip, 1 - c), part(u, chip, 1 - c), send, recv, 6 * u + 3 + j, (x, y, c))
                  for u in range(n) for j, chip in enumerate(chips)]
        return first, landed, passed, handed

    def start(ins, outs, send, recv):
        for cp in plan(outs, send, recv)[0]:
            cp.start()

    def finish(ins, outs, send, recv):
        first, landed, passed, handed = plan(outs, send, recv)
        for got, fw in zip(landed, passed):
            got.wait_recv()
            fw.start()
        for got in handed:
            got.wait_recv()
        for cp in first + passed:
            cp.wait_send()

    outs = [jax.ShapeDtypeStruct(s.shape, s.dtype) for s in staged]
    return Side(staged, outs, {u: u for u in range(n)}, 6 * n, start, finish)


def swap_halves(gps, name):
    n = len(gps)

    def body(*refs):
        g_refs, r_refs = refs[:n], refs[n:2 * n]
        send, recv = refs[2 * n:]
        x, y, c = _pos()
        cps = []
        for u in range(n):
            half = gps[u].shape[1] // 2
            cps.append(_remote(g_refs[u].at[:, pl.ds((1 - c) * half, half), :], r_refs[u], send, recv, u, (x, y, 1 - c)))
        for cp in cps:
            cp.start()
        for cp in cps:
            cp.wait()

    outs = [jax.ShapeDtypeStruct((g.shape[0], g.shape[1] // 2, g.shape[2]), g.dtype) for g in gps]
    return _comm_call(body, name, gps, outs, n)


def scatter_side(parts):
    n = len(parts)

    def plan(p_refs, r_refs, send, recv):
        x, y, c = _pos()
        chips = [(1 - x, y), (x, 1 - y), (1 - x, 1 - y)]
        return [_remote(p_refs[u].at[2 * chip[0] + chip[1]], r_refs[u].at[j], send, recv, 3 * u + j, (*chip, c))
                for u in range(n) for j, chip in enumerate(chips)]

    def start(ins, outs, send, recv):
        for cp in plan(ins, outs, send, recv):
            cp.start()

    def finish(ins, outs, send, recv):
        for cp in plan(ins, outs, send, recv):
            cp.wait()

    outs = [jax.ShapeDtypeStruct((N_CHIPS - 1,) + p.shape[1:], p.dtype) for p in parts]
    return Side(parts, outs, {}, 3 * n, start, finish)


def join_halves(bufs):
    n = len(bufs)

    def body(*refs):
        o_refs = refs[n:2 * n]
        send, recv = refs[2 * n:]
        x, y, c = _pos()

        def rows(u, cc):
            half = bufs[u].shape[1] // 2
            return o_refs[u].at[:, pl.ds(cc * half, half), :]

        cps = [_remote(rows(u, c), rows(u, c), send, recv, u, (x, y, 1 - c)) for u in range(n)]
        for cp in cps:
            cp.start()
        for u in range(n):
            _remote(rows(u, 1 - c), rows(u, 1 - c), send, recv, u, (x, y, c)).wait_recv()
        for cp in cps:
            cp.wait_send()

    outs = [jax.ShapeDtypeStruct(b.shape, b.dtype) for b in bufs]
    return _comm_call(body, "join_halves", bufs, outs, n, aliases={u: u for u in range(n)})


def gather_small(sg, name):
    rows, cols = sg.shape

    def body(s_ref, o_ref, send, recv, lsem):
        x, y, c = _pos()
        me, sibling = (x, y, c), (x, y, 1 - c)
        chips = [(1 - x, y), (x, 1 - y), (1 - x, 1 - y)]

        def blk(px, py, pc):
            return o_ref.at[4 * px + 2 * py + pc]

        mine = pltpu.make_async_copy(s_ref, blk(*me), lsem)
        mine.start()
        first = [_remote(s_ref, blk(*me), send, recv, 0, sibling)]
        first += [_remote(s_ref, blk(*me), send, recv, 1 + j, (*chip, c)) for j, chip in enumerate(chips)]
        for cp in first:
            cp.start()
        passed = [_remote(blk(*chip, c), blk(*chip, c), send, recv, 4 + j, sibling) for j, chip in enumerate(chips)]
        for j, chip in enumerate(chips):
            _remote(blk(*chip, c), blk(*chip, c), send, recv, 1 + j, me).wait_recv()
            passed[j].start()
        _remote(blk(*sibling), blk(*sibling), send, recv, 0, me).wait_recv()
        for j, chip in enumerate(chips):
            _remote(blk(*chip, 1 - c), blk(*chip, 1 - c), send, recv, 4 + j, me).wait_recv()
        for cp in first + passed:
            cp.wait_send()
        mine.wait()

    return pl.pallas_call(
        body, name=name,
        out_shape=jax.ShapeDtypeStruct((N_DEV, rows, cols), sg.dtype),
        in_specs=[ANY], out_specs=ANY,
        scratch_shapes=[pltpu.SemaphoreType.DMA((N_DEV - 1,)), pltpu.SemaphoreType.DMA((N_DEV - 1,)), pltpu.SemaphoreType.DMA],
    )(sg)


def sum_cores(gp, theirs, core, chip, name):
    nch, rows, cols = gp.shape
    half = rows // 2
    tr = _pick(half, (512, 352, 256, 176, 128, 64))
    nb = half // tr

    def kern(idx_ref, g_ref, t_ref, own_ref, all_ref):
        k = pl.program_id(1)
        s = g_ref[...].astype(F32) + t_ref[...].astype(F32)
        all_ref[...] = s.astype(BF16)

        @pl.when(k == idx_ref[1])
        def _():
            own_ref[...] = s

    grid_spec = pltpu.PrefetchScalarGridSpec(
        num_scalar_prefetch=1, grid=(nb, nch),
        in_specs=[pl.BlockSpec((None, tr, cols), lambda i, k, idx: (k, idx[0] * nb + i, 0)),
                  pl.BlockSpec((None, tr, cols), lambda i, k, idx: (k, i, 0))],
        out_specs=[pl.BlockSpec((tr, cols), lambda i, k, idx: (i, 0)),
                   pl.BlockSpec((None, tr, cols), lambda i, k, idx: (k, i, 0))])
    return pl.pallas_call(
        kern, name=name, grid_spec=grid_spec,
        out_shape=[jax.ShapeDtypeStruct((half, cols), F32), jax.ShapeDtypeStruct((nch, half, cols), BF16)],
        compiler_params=_params(("parallel", "arbitrary")),
    )(jnp.stack([core, chip]).astype(jnp.int32), gp, theirs)


def sum_chips(own, others, core, layer, nlayers, into, name):
    half, cols = own.shape
    tr = _pick(half, (512, 352, 256, 176, 128, 64))
    nb = half // tr

    def kern(idx_ref, o_ref, a_ref, b_ref, c_ref, *rest):
        out_ref = rest[-1]
        out_ref[...] = ((o_ref[...] + a_ref[...].astype(F32)) + b_ref[...].astype(F32)) + c_ref[...].astype(F32)

    grid_spec = pltpu.PrefetchScalarGridSpec(
        num_scalar_prefetch=1, grid=(nb,),
        in_specs=[pl.BlockSpec((tr, cols), lambda i, idx: (i, 0))] +
                 [pl.BlockSpec((None, tr, cols), lambda i, idx, j=j: (j, i, 0)) for j in range(N_CHIPS - 1)] +
                 ([] if into is None else [pl.BlockSpec(memory_space=pl.ANY)]),
        out_specs=pl.BlockSpec((None, tr, cols), lambda i, idx: (layer, idx[0] * nb + i, 0)))
    args = [jnp.reshape(core, (1,)).astype(jnp.int32), own, others, others, others] + ([] if into is None else [into])
    return pl.pallas_call(
        kern, name=name, grid_spec=grid_spec,
        out_shape=jax.ShapeDtypeStruct((nlayers, 2 * half, cols), F32),
        input_output_aliases={} if into is None else {len(args) - 1: 0},
        compiler_params=_params(("parallel",)),
    )(*args)


def small_update(gath, w, m, v, name):
    def fn(*vs):
        g = vs[0]
        for t in vs[1:N_DEV]:
            g = g + t
        wv, mv, vv = vs[N_DEV:]
        m2 = ADAM_B1 * mv + (1.0 - ADAM_B1) * g
        v2 = ADAM_B2 * vv + (1.0 - ADAM_B2) * (g * g)
        m_hat = m2 / (1.0 - ADAM_B1 ** ADAM_STEP)
        v_hat = v2 / (1.0 - ADAM_B2 ** ADAM_STEP)
        return g, -ADAM_LR * (m_hat / (jnp.sqrt(v_hat) + ADAM_EPS) + ADAM_WD * wv), m2, v2

    c = w.shape[1]
    ins = [(gath[k], "row") for k in range(N_DEV)] + [(w, "row"), (m, "row"), (v, "row")]
    return rowwise(fn, ins, [(c, F32)] * 4, tr=w.shape[0] // 2, name=name)


_MIX = {0: [("sb_w_qkv", 0), ("sb_w_o", 0)], 1: [("gm_w_in", 0), ("gm_w_out", 0)],
        2: [("ssm_w_in", 0), ("ssm_w_out", 0)], 3: [("sb_w_qkv", 1), ("sb_w_o", 1)]}
_FFN = {i: [("ffn_w_gu", i), ("ffn_w_down", i)] for i in range(4)}
GATHER_FIRST = _MIX[0][:1]
GATHER_AT = {"sb_attn_0": _MIX[0][1:] + _FFN[0] + _FFN[1],
             "ffn_gu_0": _MIX[1], "ffn_down_0": _MIX[2][1:], "ffn_gu_1": _MIX[2][:1], "ffn_down_1": _FFN[2][1:],
             "ssm_scan_2": _FFN[2][:1] + _MIX[3] + _FFN[3][1:], "ffn_gu_2": _FFN[3][:1]}
SCATTER_AT = {"ssm_dscan_2": _FFN[3] + _MIX[3] + _FFN[2], "sb_dattn_0": _MIX[2] + _FFN[1] + _MIX[1] + _FFN[0]}
SCATTER_LAST = _MIX[0]


class _Plan:
    def __init__(self, ins, core, chip):
        self.core, self.chip = core, chip
        self.staged = {(n, l): stage_shard(ins[n], l, chip, f"stage_{n}_{l}")
                       for n in MATRICES for l in range(ins[n].shape[0])}
        self.full = {n: [None] * ins[n].shape[0] for n in MATRICES}
        self.ready = {}
        self.parts = {}
        self.halves = {}
        self.layers = {n: ins[n].shape[0] for n in MATRICES}
        self.swaps = 0
        self._fill(GATHER_FIRST, run_side(gather_side([self.staged[u] for u in GATHER_FIRST]), "gather_first"))

    def _fill(self, units, gathered):
        for (n, l), g in zip(units, gathered):
            if n in COLUMN_BLOCKS:
                self.full[n][l] = g
            elif n == "ssm_w_in":
                self.full[n][l] = jnp.concatenate([g[k] for k in range(N_CHIPS)], axis=1)
            else:
                self.full[n][l] = g.reshape(-1, g.shape[-1])

    def _prepare(self, units):
        gps = [self.ready[u] for u in units]
        theirs = swap_halves(gps, f"swap_halves_{self.swaps}")
        self.swaps += 1
        for (n, l), g, t in zip(units, gps, theirs):
            self.parts[(n, l)] = sum_cores(g, t, self.core, self.chip, f"sum_cores_{n}_{l}")

    def _reduce(self, units, others):
        for (n, l), other in zip(units, others):
            self.halves[n] = sum_chips(self.parts[(n, l)][0], other, self.core, l, self.layers[n], self.halves.get(n),
                                       f"sum_chips_{n}_{l}")

    def side(self, tag):
        if tag in GATHER_AT:
            return gather_side([self.staged[u] for u in GATHER_AT[tag]])
        if tag in SCATTER_AT:
            self._prepare(SCATTER_AT[tag])
            return scatter_side([self.parts[u][1] for u in SCATTER_AT[tag]])
        return None

    def done(self, tag, results):
        if tag in GATHER_AT:
            self._fill(GATHER_AT[tag], results)
        else:
            self._reduce(SCATTER_AT[tag], results)

    def grads_ready(self, grads):
        for (n, l), g in grads.items():
            if n in COLUMN_BLOCKS:
                self.ready[(n, l)] = g
            elif n == "ssm_w_in":
                self.ready[(n, l)] = jnp.stack(jnp.split(g, N_CHIPS, axis=1))
            else:
                self.ready[(n, l)] = g.reshape(N_CHIPS, -1, g.shape[-1])

    def shard_grads(self):
        self._prepare(SCATTER_LAST)
        self._reduce(SCATTER_LAST, run_side(scatter_side([self.parts[u][1] for u in SCATTER_LAST]), "scatter_last"))
        names = sorted(self.halves)
        return dict(zip(names, join_halves([self.halves[n] for n in names])))


def _step(ins):
    x, target = ins["x"][0], ins["loss_target"][0]
    core = lax.axis_index("c")
    chip = 2 * lax.axis_index("x") + lax.axis_index("y")

    def lane_pad(v):
        return jnp.pad(v, ((0, 0), (0, PACK_COLS - v.shape[1])))

    vec_rows = [ins["ssm_conv_w"][0], ins["ssm_conv_b"], lane_pad(ins["ssm_norm_gain"])]
    blk = jnp.concatenate(vec_rows + [jnp.zeros((SUBLANES - 6, PACK_COLS), F32)], axis=0)
    per_chip = gather_small(blk, "gather_vectors")[0::2]
    ngw = ins["ssm_norm_gain"].shape[1]
    full = {
        "ssm_conv_w": jnp.concatenate([per_chip[k, 0:4] for k in range(N_CHIPS)], axis=1)[None],
        "ssm_conv_b": jnp.concatenate([per_chip[k, 4:5] for k in range(N_CHIPS)], axis=1),
        "ssm_norm_gain": jnp.concatenate([per_chip[k, 5:6, :ngw] for k in range(N_CHIPS)], axis=1),
    }

    plan = _Plan(ins, core, chip)
    full.update(plan.full)
    for n in REPLICATED:
        full[n] = ins[n]

    loss, dx, grads = local_step(x, target, full, plan)
    loss = lax.psum(loss, ALL_AXES)
    gshards = plan.shard_grads()

    small_shapes = [ins[n].shape for n in REPLICATED]
    vec_shapes = [grads[n].shape for n in EXACT]
    vec_pack = _pack([grads[n] for n in EXACT], F32, SUBLANES)
    gath = gather_small(jnp.concatenate([_pack([grads[n] for n in REPLICATED], F32, SUBLANES), vec_pack], axis=0),
                        "gather_small")
    packed = [jnp.concatenate([_pack([ins[pre + n] for n in REPLICATED], F32, SUBLANES), jnp.zeros_like(vec_pack)], axis=0)
              for pre in ("", "m_", "v_")]
    res = small_update(gath, *packed, name="small_update")
    nrep = res[0].shape[0] - vec_pack.shape[0]
    small = [dict(zip(REPLICATED, _unpack(r[:nrep].reshape(-1), small_shapes))) for r in res]
    vec_g = dict(zip(EXACT, _unpack(res[0][nrep:].reshape(-1), vec_shapes)))

    out_g, out_d, out_m, out_v = {}, {}, {}, {}
    for n in REPLICATED:
        out_g[n], out_d[n], out_m[n], out_v[n] = (s[n] for s in small)
    for n in SHARDED:
        shp = ins[n].shape
        if n in EXACT:
            g = lax.dynamic_slice_in_dim(vec_g[n], chip * shp[-1], shp[-1], axis=vec_g[n].ndim - 1)
        else:
            g = gshards[n]
        two = (math.prod(shp[:-1]), shp[-1])
        d2, m2, v2, g2 = adamw(ins[n].reshape(two), g.reshape(two), ins["m_" + n].reshape(two),
                               ins["v_" + n].reshape(two), f"adamw_{n}")
        out_g[n], out_d[n], out_m[n], out_v[n] = g2.reshape(shp), d2.reshape(shp), m2.reshape(shp), v2.reshape(shp)
    return (loss, dx[None], *[out_g[n] for n in WEIGHTS], *[out_d[n] for n in WEIGHTS],
            *[out_m[n] for n in WEIGHTS], *[out_v[n] for n in WEIGHTS])


def kernel(x, mix_norm, ffn_norm, sb_w_qkv, sb_q_gain, sb_k_gain, sb_w_o, gm_w_in, gm_b_in, gm_v_gain, gm_w_s, gm_b_s, gm_w_out, ssm_w_in, ssm_conv_w, ssm_conv_b, ssm_dt_bias, ssm_a_log, ssm_d, ssm_norm_gain, ssm_w_out, ffn_w_gu, ffn_w_down, loss_target, m_mix_norm, m_ffn_norm, m_sb_w_qkv, m_sb_q_gain, m_sb_k_gain, m_sb_w_o, m_gm_w_in, m_gm_b_in, m_gm_v_gain, m_gm_w_s, m_gm_b_s, m_gm_w_out, m_ssm_w_in, m_ssm_conv_w, m_ssm_conv_b, m_ssm_dt_bias, m_ssm_a_log, m_ssm_d, m_ssm_norm_gain, m_ssm_w_out, m_ffn_w_gu, m_ffn_w_down, v_mix_norm, v_ffn_norm, v_sb_w_qkv, v_sb_q_gain, v_sb_k_gain, v_sb_w_o, v_gm_w_in, v_gm_b_in, v_gm_v_gain, v_gm_w_s, v_gm_b_s, v_gm_w_out, v_ssm_w_in, v_ssm_conv_w, v_ssm_conv_b, v_ssm_dt_bias, v_ssm_a_log, v_ssm_d, v_ssm_norm_gain, v_ssm_w_out, v_ffn_w_gu, v_ffn_w_down):
    return _step(dict(locals()))
```

```python
import functools
import math

import jax
import jax.numpy as jnp
from jax import lax
from jax.experimental import pallas as pl
from jax.experimental.pallas import tpu as pltpu

F32 = jnp.float32
BF16 = jnp.bfloat16
EPS = 1e-6
LANES = 128
SUBLANES = 8
VMEM_LIMIT = 56 * 1024 * 1024
HEAD = 64
CHUNK = 128
SB_TQ, SB_TK = 256, 256
SSD_SUB = 8
SB_DEAD = -110.0
SB_UNSEEN = -1e30
ADAM_LR, ADAM_B1, ADAM_B2, ADAM_EPS, ADAM_WD, ADAM_STEP = 0.001, 0.9, 0.999, 1e-08, 0.01, 10
MESH_ID = pl.DeviceIdType.MESH
ALL_AXES = ("x", "y", "c")


def _params(sem):
    return pltpu.CompilerParams(dimension_semantics=sem, vmem_limit_bytes=VMEM_LIMIT)


def _pick(n, cands):
    for c in cands:
        if n % c == 0:
            return c
    return n


def _dot(a, b, dims=((1,), (0,))):
    return lax.dot_general(a, b, (dims, ((), ())), preferred_element_type=F32)


def _dot_nt(a, b):
    return _dot(a, b, ((1,), (1,)))


def _dot_tn(a, b):
    return _dot(a, b, ((0,), (0,)))


def _split2(x):
    hi = x.astype(BF16)
    lo = (x - hi.astype(F32)).astype(BF16)
    return hi, lo


def _dot_x2(x, m):
    hi, lo = _split2(x)
    return _dot(hi, m) + _dot(lo, m)


def _dot_x3_left(m, x):
    h1 = x.astype(BF16)
    r1 = x - h1.astype(F32)
    h2 = r1.astype(BF16)
    h3 = (r1 - h2.astype(F32)).astype(BF16)
    return _dot(m, h1) + _dot(m, h2) + _dot(m, h3)


def _sigmoid(x):
    return 1.0 / (1.0 + jnp.exp(-x))


def _softplus(x):
    return jnp.maximum(x, 0.0) + jnp.log(1.0 + jnp.exp(-jnp.abs(x)))


def _colsum(x):
    return jnp.sum(x, axis=0, keepdims=True)


def _rowsum(x):
    return jnp.sum(x, axis=1, keepdims=True)


def _iota2(shape, dim):
    return lax.broadcasted_iota(jnp.int32, shape, dim)


MM_VMEM_BUDGET = 40 * 1024 * 1024
MM_STEP_US = 0.35
MM_HBM_BYTES_PER_US = 3.0e6
MM_VMEM_BYTES_PER_US = 1.5e6
MM_FLOPS_PER_US = 9.0e8
MXU_DIM = 256


def _mm_tiles(m, n, kk, wn, wk, a_bytes, b_bytes, has_add):
    def divisors(total, cands):
        got = [c for c in cands if total % c == 0 and c <= total]
        return got or [total]

    best = None
    for tm in divisors(m, (1024, 512, 256, 128)):
        for tn in divisors(wn, (1024, 768, 1408, 512, 256, 128)):
            for tk in divisors(wk, (4096, 2816, 2048, 1408, 1024, 768, 512, 256, 128)):
                nk = kk // tk
                vmem = 2 * (tm * tk * a_bytes + tk * tn * b_bytes + tm * tn * 4 * (2 if has_add else 1))
                vmem += tm * tn * 4 if nk > 1 else 0
                if vmem > MM_VMEM_BUDGET:
                    continue
                steps = (m // tm) * (n // tn) * nk
                a_reads = 1 if nk == 1 else n // tn
                traffic = m * kk * a_bytes * a_reads + kk * n * b_bytes * (m // tm) + m * n * 4
                fill = min(1.0, tn / MXU_DIM) * min(1.0, tm / MXU_DIM)
                compute = 2.0 * m * n * kk / (MM_FLOPS_PER_US * fill)
                cost = steps * MM_STEP_US + max(compute, traffic / MM_HBM_BYTES_PER_US)
                if nk > 1:
                    cost += steps * tm * tn * 8 / MM_VMEM_BYTES_PER_US
                if best is None or cost < best[0]:
                    best = (cost, tm, tn, tk)
    return best[1:]


def mm(a, b, *, ta=False, tb=False, add=None, bias=None, a_chunks=False, b_chunks=False, out_chunks=False,
       out_dtype=F32, name, side=None):
    wa = None
    if a_chunks:
        m, wa = a.shape[1], a.shape[2]
        kk = a.shape[0] * wa
    elif ta:
        kk, m = a.shape
    else:
        m, kk = a.shape
    nch, wide = 1, None
    if b_chunks:
        nch, rows_b, wide = b.shape
        kb, n = (rows_b, nch * wide) if not tb else (nch * wide, rows_b)
    elif tb:
        n, kb = b.shape
    else:
        kb, n = b.shape
    wide_o = n // N_CHIPS if out_chunks else None
    assert kk == kb, (a.shape, b.shape, ta, tb)
    has_add, has_bias = add is not None, bias is not None
    wk = wide if (wide and tb) else kk
    wn = wide if (wide and not tb) else n
    tm, tn, tk = _mm_tiles(m, n, kk, math.gcd(wn, wide_o) if wide_o else wn, math.gcd(wk, wa) if wa else wk,
                           a.dtype.itemsize, b.dtype.itemsize, has_add)
    nk = kk // tk
    dims = ((0 if ta else 1,), (1 if tb else 0,))

    def kern(*refs):
        a_ref, b_ref = refs[0], refs[1]
        rest = list(refs[2:])
        add_ref = rest.pop(0) if has_add else None
        bias_ref = rest.pop(0) if has_bias else None
        o_ref = rest[0]
        part = _dot(a_ref[...].astype(BF16), b_ref[...].astype(BF16), dims)

        def finish(r):
            if has_add:
                r = r + add_ref[...]
            if has_bias:
                r = r + bias_ref[...]
            o_ref[...] = r.astype(out_dtype)

        if nk == 1:
            finish(part)
        else:
            acc_ref = rest[1]
            k = pl.program_id(2)

            @pl.when(k == 0)
            def _():
                acc_ref[...] = part

            @pl.when((k > 0) & (k < nk - 1))
            def _():
                acc_ref[...] += part

            @pl.when(k == nk - 1)
            def _():
                finish(acc_ref[...] + part)

    if a_chunks:
        per_a = wa // tk
        a_spec = pl.BlockSpec((None, tm, tk), lambda i, j, k: (k // per_a, i, k % per_a))
    elif ta:
        a_spec = pl.BlockSpec((tk, tm), lambda i, j, k: (k, i))
    else:
        a_spec = pl.BlockSpec((tm, tk), lambda i, j, k: (i, k))
    if b_chunks and tb:
        per = wide // tk
        b_spec = pl.BlockSpec((None, tn, tk), lambda i, j, k: (k // per, j, k % per))
    elif b_chunks:
        per = wide // tn
        b_spec = pl.BlockSpec((None, tk, tn), lambda i, j, k: (j // per, k, j % per))
    elif tb:
        b_spec = pl.BlockSpec((tn, tk), lambda i, j, k: (j, k))
    else:
        b_spec = pl.BlockSpec((tk, tn), lambda i, j, k: (k, j))
    if out_chunks:
        per_o = wide_o // tn
        out_spec = pl.BlockSpec((None, tm, tn), lambda i, j, k: (j // per_o, i, j % per_o))
        out_shape = jax.ShapeDtypeStruct((N_CHIPS, m, wide_o), out_dtype)
    else:
        out_spec = pl.BlockSpec((tm, tn), lambda i, j, k: (i, j))
        out_shape = jax.ShapeDtypeStruct((m, n), out_dtype)
    in_specs, args = [a_spec, b_spec], [a, b]
    if has_add:
        in_specs.append(pl.BlockSpec((tm, tn), lambda i, j, k: (i, j)))
        args.append(add)
    if has_bias:
        in_specs.append(pl.BlockSpec((1, tn), lambda i, j, k: (0, j)))
        args.append(bias)
    (out,), side_outs = side_call(
        kern, side,
        name=name,
        grid=(m // tm, n // tn, nk),
        in_specs=in_specs,
        out_specs=[out_spec],
        out_shape=[out_shape],
        scratch_shapes=[pltpu.VMEM((tm, tn), F32)] if nk > 1 else [],
        args=args)
    return out if side is None else (out, side_outs)


def mm_hooked(plan, a, b, *, name, **kw):
    side = plan.side(name)
    if side is None:
        return mm(a, b, name=name, **kw)
    out, side_outs = mm(a, b, name=name, side=side, **kw)
    plan.done(name, side_outs)
    return out


def rowwise(fn, ins, outs, accs=(), *, tr, name):
    rows = [a for a, kind in ins if kind == "row"][0].shape[0]
    tr = min(tr, rows)
    assert rows % tr == 0 and tr % SUBLANES == 0, (rows, tr)
    n = rows // tr
    n_in, n_out = len(ins), len(outs)
    kinds = [kind for _, kind in ins]

    def kern(*refs):
        i = pl.program_id(0)
        vals = []
        for ref, kind in zip(refs[:n_in], kinds):
            v = ref[...]
            if kind == "prev":
                v = v * (i > 0).astype(v.dtype)
            elif kind == "next":
                v = v * (i < n - 1).astype(v.dtype)
            vals.append(v)
        res = fn(*vals)
        for ref, r in zip(refs[n_in:n_in + n_out], res[:n_out]):
            ref[...] = r.astype(ref.dtype)
        if accs:
            acc_refs = refs[n_in + n_out:]

            @pl.when(i == 0)
            def _():
                for ref in acc_refs:
                    ref[...] = jnp.zeros_like(ref)

            for ref, r in zip(acc_refs, res[n_out:]):
                ref[...] += r

    in_specs = []
    for a, kind in ins:
        if kind == "row":
            in_specs.append(pl.BlockSpec((tr, a.shape[1]), lambda i: (i, 0)))
        elif kind == "full":
            in_specs.append(pl.BlockSpec(a.shape, lambda i, nd=a.ndim: (0,) * nd))
        elif kind == "prev":
            in_specs.append(pl.BlockSpec((SUBLANES, a.shape[1]),
                                         lambda i: (jnp.maximum(i * (tr // SUBLANES) - 1, 0), 0)))
        else:
            in_specs.append(pl.BlockSpec((SUBLANES, a.shape[1]),
                                         lambda i: (jnp.minimum((i + 1) * (tr // SUBLANES), rows // SUBLANES - 1), 0)))
    out_specs = [pl.BlockSpec((tr, c), lambda i: (i, 0)) for c, _ in outs]
    out_specs += [pl.BlockSpec((r, c), lambda i: (0, 0)) for r, c in accs]
    out_shape = [jax.ShapeDtypeStruct((rows, c), dt) for c, dt in outs]
    out_shape += [jax.ShapeDtypeStruct((r, c), F32) for r, c in accs]
    res = pl.pallas_call(
        kern,
        name=name,
        grid=(n,),
        in_specs=in_specs,
        out_specs=out_specs,
        out_shape=out_shape,
        compiler_params=_params(("arbitrary",) if accs else ("parallel",)),
    )(*[a for a, _ in ins])
    return res


def rms_fwd(x, g, name):
    def fn(xv, gv):
        r = lax.rsqrt(jnp.mean(xv * xv, axis=1, keepdims=True) + EPS)
        return (xv * r * gv,)

    return rowwise(fn, [(x, "row"), (g, "full")], [(x.shape[1], BF16)], tr=1024, name=name)[0]


def rms_bwd(x, g, dy, dres, name):
    def fn(xv, gv, dyv, drv):
        r = lax.rsqrt(jnp.mean(xv * xv, axis=1, keepdims=True) + EPS)
        xh = xv * r
        dyg = dyv * gv
        dx = drv + r * (dyg - xh * jnp.mean(dyg * xh, axis=1, keepdims=True))
        return dx, dx, _colsum(dyv * xh)

    c = x.shape[1]
    dx, dxb, dg = rowwise(fn, [(x, "row"), (g, "full"), (dy, "row"), (dres, "row")], [(c, F32), (c, BF16)], [(1, c)],
                          tr=512, name=name)
    return (dx, dxb), dg


def ffn_up(h, wgu, name, side=None):
    s, d = h.shape
    nch, _, w = wgu.shape
    half = nch // 2
    tm = _pick(s, (512, 256, 128))

    def kern(h_ref, wg_ref, wu_ref, gu_ref, a_ref):
        hv = h_ref[...]
        g = _dot(hv, wg_ref[...])
        u = _dot(hv, wu_ref[...])
        gu_ref[0] = g.astype(BF16)
        gu_ref[1] = u.astype(BF16)
        a_ref[...] = (g * _sigmoid(g) * u).astype(BF16)

    return side_call(
        kern, side, name=name, grid=(s // tm, half),
        in_specs=[pl.BlockSpec((tm, d), lambda i, j: (i, 0)),
                  pl.BlockSpec((None, d, w), lambda i, j: (j, 0, 0)),
                  pl.BlockSpec((None, d, w), lambda i, j: (j + half, 0, 0))],
        out_specs=[pl.BlockSpec((2, tm, w), lambda i, j: (0, i, j)), pl.BlockSpec((tm, w), lambda i, j: (i, j))],
        out_shape=[jax.ShapeDtypeStruct((2, s, half * w), BF16), jax.ShapeDtypeStruct((s, half * w), BF16)],
        scratch_shapes=[], args=(h, wgu, wgu))


def ffn_dact(dxb, wdown, gu, name):
    s, d = dxb.shape
    hid = wdown.shape[0]
    tm = _pick(s, (512, 256, 128))
    tn = _pick(hid, (1408, 512, 256, 128))

    def kern(dx_ref, w_ref, gu_ref, o_ref):
        da = _dot_nt(dx_ref[...], w_ref[...])
        g, u = gu_ref[0].astype(F32), gu_ref[1].astype(F32)
        sg = _sigmoid(g)
        o_ref[0] = (da * u * sg * (1.0 + g * (1.0 - sg))).astype(BF16)
        o_ref[1] = (da * g * sg).astype(BF16)

    return pl.pallas_call(
        kern, name=name, grid=(s // tm, hid // tn),
        in_specs=[pl.BlockSpec((tm, d), lambda i, j: (i, 0)), pl.BlockSpec((tn, d), lambda i, j: (j, 0)),
                  pl.BlockSpec((2, tm, tn), lambda i, j: (0, i, j))],
        out_specs=pl.BlockSpec((2, tm, tn), lambda i, j: (0, i, j)),
        out_shape=jax.ShapeDtypeStruct((2, s, hid), BF16),
        compiler_params=_params(("parallel", "parallel")),
    )(dxb, wdown, gu)


def loss_and_grad(y, t, name):
    d = y.shape[1]

    def fn(yv, tv):
        e = yv - tv
        part = jnp.sum(_colsum(e * e), axis=1, keepdims=True) * (0.5 / d)
        dy = e * (1.0 / d)
        return dy, dy, jnp.broadcast_to(part, (SUBLANES, LANES))

    dy, dyb, acc = rowwise(fn, [(y, "row"), (t, "row")], [(d, F32), (d, BF16)], [(SUBLANES, LANES)], tr=1024, name=name)
    return acc[0, 0], (dy, dyb)


def adamw(w, g, m, v, name):
    def fn(wv, gv, mv, vv):
        m2 = ADAM_B1 * mv + (1.0 - ADAM_B1) * gv
        v2 = ADAM_B2 * vv + (1.0 - ADAM_B2) * (gv * gv)
        m_hat = m2 / (1.0 - ADAM_B1 ** ADAM_STEP)
        v_hat = v2 / (1.0 - ADAM_B2 ** ADAM_STEP)
        delta = -ADAM_LR * (m_hat / (jnp.sqrt(v_hat) + ADAM_EPS) + ADAM_WD * wv)
        return delta, m2, v2, gv

    rows, c = w.shape
    tr = _pick(rows, (512, 256, 128, 64, 32, 16, 8)) if rows % SUBLANES == 0 else rows
    if rows % SUBLANES:
        return _whole(fn, [w, g, m, v], [(w.shape, F32)] * 4, name=name)
    if 2 * 8 * tr * c * 4 > MM_VMEM_BUDGET:
        tr //= 2
    return rowwise(fn, [(w, "row"), (g, "row"), (m, "row"), (v, "row")], [(c, F32)] * 4, tr=tr, name=name)


def _whole(fn, ins, outs, *, name):
    n_in = len(ins)

    def kern(*refs):
        res = fn(*[r[...] for r in refs[:n_in]])
        for ref, r in zip(refs[n_in:], res):
            ref[...] = r.astype(ref.dtype)

    return pl.pallas_call(
        kern,
        name=name,
        out_shape=[jax.ShapeDtypeStruct(s, dt) for s, dt in outs],
        compiler_params=pltpu.CompilerParams(vmem_limit_bytes=VMEM_LIMIT),
    )(*ins)


def ffn_fwd(x, g, wgu, wdown, tag, plan):
    h = rms_fwd(x, g, f"ffn_rms_{tag}")
    gu, a = _hooked(plan, f"ffn_gu_{tag}", ffn_up, h, wgu)
    xn = mm_hooked(plan, a, wdown, add=x, name=f"ffn_down_{tag}")
    return xn, (x, h, gu, a)


def ffn_bwd(dxn, saved, g, wgu, wdown, tag):
    x, h, gu, a = saved
    dxn, dxb = dxn
    dwdown = mm(a, dxb, ta=True, out_dtype=BF16, name=f"ffn_dwdown_{tag}")
    dgu = ffn_dact(dxb, wdown, gu, f"ffn_dact_{tag}")
    dh = mm(dgu, wgu, tb=True, a_chunks=True, b_chunks=True, name=f"ffn_dh_{tag}")
    dwgu = mm(h, dgu, ta=True, b_chunks=True, out_dtype=BF16, out_chunks=True, name=f"ffn_dwgu_{tag}")
    dx, dg = rms_bwd(x, g, dh, dxn, f"ffn_drms_{tag}")
    return dx, dg, dwgu, dwdown


def _head_blockdiag(c):
    i = jnp.arange(c) // HEAD
    return (i[:, None] == i[None, :]).astype(BF16)


def _head_sums(x, bd):
    return jnp.concatenate([_dot_x2(x[:, g * LANES:(g + 1) * LANES], bd) for g in range(x.shape[1] // LANES)], axis=1)


def qknorm_fwd(qkv, qg, kg, bd, name):
    d = qkv.shape[1] // 3
    scale = 1.0 / math.sqrt(HEAD)

    def fn(v, qgv, kgv, bdv):
        v = v.astype(F32)
        q, k, vv = v[:, :d], v[:, d:2 * d], v[:, 2 * d:]
        rq = lax.rsqrt(_head_sums(q * q, bdv) * (1.0 / HEAD) + EPS)
        rk = lax.rsqrt(_head_sums(k * k, bdv) * (1.0 / HEAD) + EPS)
        return q * rq * qgv * scale, k * rk * kgv, vv

    return rowwise(fn, [(qkv, "row"), (qg, "full"), (kg, "full"), (bd, "full")],
                   [(d, BF16), (d, BF16), (d, BF16)], tr=512, name=name)


def qknorm_bwd(qkv, dqs, dkn, dv, qg, kg, bd, name):
    d = qkv.shape[1] // 3
    scale = 1.0 / math.sqrt(HEAD)

    def one(xv, gv, dyv, bdv):
        r = lax.rsqrt(_head_sums(xv * xv, bdv) * (1.0 / HEAD) + EPS)
        xh = xv * r
        dyg = dyv * gv
        dx = r * (dyg - xh * (_head_sums(dyg * xh, bdv) * (1.0 / HEAD)))
        return dx, _colsum(dyv * xh)

    def fn(v, dqv, dkv, dvv, qgv, kgv, bdv):
        v = v.astype(F32)
        q, k = v[:, :d], v[:, d:2 * d]
        dq, dqg = one(q, qgv, dqv * scale, bdv)
        dk, dkg = one(k, kgv, dkv, bdv)
        return jnp.concatenate([dq, dk, dvv], axis=1), dqg, dkg

    return rowwise(fn, [(qkv, "row"), (dqs, "row"), (dkn, "row"), (dv, "row"), (qg, "full"), (kg, "full"), (bd, "full")],
                   [(3 * d, BF16)], [(1, d), (1, d)], tr=512, name=name)


def _sb_tile(qh, k, mask, tri_gt):
    z = _dot_nt(qh, k)
    sp = jnp.log(1.0 + jnp.exp(-jnp.abs(z)))
    lb = jnp.minimum(z, 0.0) - sp
    l1 = jnp.where(mask, lb - z, 0.0)
    suf = _dot(l1.astype(BF16), tri_gt)
    return lb, l1, suf


def _sb_tri(tk):
    i = jnp.arange(tk)
    return jnp.stack([i[:, None] > i[None, :], i[:, None] < i[None, :]]).astype(BF16)


def _sb_setup(tq, tk):
    row, col = _iota2((tq, tk), 0), _iota2((tq, tk), 1)
    lane = _iota2((1, LANES), 1)
    halves = [(lane < HEAD).astype(BF16), (lane >= HEAD).astype(BF16)]
    lane_q = _iota2((tq, LANES), 1) + jnp.minimum(_iota2((tq, LANES), 0), 0)
    return row, col, halves, lane_q


def sb_attn_fwd(qs, kn, vb, tri, name, side=None):
    s, d = qs.shape
    tq, tk = min(SB_TQ, s), min(SB_TK, s)
    nq = s // tq
    assert s // tk <= LANES and s % tq == 0 and s % tk == 0

    def kern(q_ref, k_ref, v_ref, tri_ref, o_ref, rs_ref, acc_ref):
        i = pl.program_id(1)
        row, col, halves, lane_q = _sb_setup(tq, tk)
        q = q_ref[...]
        qh = [q * hm for hm in halves]
        acc_ref[...] = jnp.zeros_like(acc_ref)
        rs_ref[...] = jnp.full(rs_ref.shape, SB_UNSEEN, F32)
        nkb = (i + 1) * (tq // tk)

        def more(st):
            return (st[0] < nkb) & (st[1] > SB_DEAD)

        def step(st):
            n, r = st[0], list(st[2:])
            kb = nkb - 1 - n
            ks = pl.multiple_of(kb * tk, tk)
            k = k_ref[pl.ds(ks, tk), :]
            v = v_ref[pl.ds(ks, tk), :]
            mask = col < row + (i * tq - kb * tk)
            at_kb = lane_q == kb
            for hh in range(2):
                lb, l1, suf = _sb_tile(qh[hh], k, mask, tri_ref[0])
                w = jnp.where(mask, jnp.exp(lb + suf + r[hh]), 0.0)
                acc_ref[...] += _dot(w.astype(BF16), v * halves[hh])
                rs_ref[hh] = jnp.where(at_kb, r[hh], rs_ref[hh])
                r[hh] = r[hh] + _rowsum(l1)
            return (n + 1, jnp.maximum(jnp.max(r[0]), jnp.max(r[1])), r[0], r[1])

        z1 = jnp.zeros((tq, 1), F32)
        lax.while_loop(more, step, (jnp.int32(0), jnp.float32(0.0), z1, z1))
        o_ref[...] = acc_ref[...].astype(BF16)

    nh2 = d // LANES
    return side_call(
        kern, side,
        name=name,
        grid=(nh2, nq),
        in_specs=[pl.BlockSpec((tq, LANES), lambda h, i: (i, h)),
                  pl.BlockSpec((s, LANES), lambda h, i: (0, h)),
                  pl.BlockSpec((s, LANES), lambda h, i: (0, h)),
                  pl.BlockSpec((2, tk, tk), lambda h, i: (0, 0, 0))],
        out_specs=[pl.BlockSpec((tq, LANES), lambda h, i: (i, h)),
                   pl.BlockSpec((None, 2, tq, LANES), lambda h, i: (h, 0, i, 0))],
        out_shape=[jax.ShapeDtypeStruct((s, d), BF16), jax.ShapeDtypeStruct((nh2, 2, s, LANES), F32)],
        scratch_shapes=[pltpu.VMEM((tq, LANES), F32)],
        args=(qs, kn, vb, tri))


def sb_attn_bwd(qs, kn, vb, rsave, do, tri, name, side=None):
    s, d = qs.shape
    tq, tk = min(SB_TQ, s), min(SB_TK, s)
    nq = s // tq

    def kern(q_ref, k_ref, v_ref, rs_ref, do_ref, tri_ref, dq_ref, dk_ref, dv_ref):
        i = pl.program_id(1)

        @pl.when(i == 0)
        def _():
            dk_ref[...] = jnp.zeros_like(dk_ref)
            dv_ref[...] = jnp.zeros_like(dv_ref)

        row, col, halves, lane_q = _sb_setup(tq, tk)
        q = q_ref[...]
        qh = [q * hm for hm in halves]
        dov = do_ref[...].astype(BF16)
        doh = [dov * hm for hm in halves]
        dq_ref[...] = jnp.zeros_like(dq_ref)
        nkb = (i + 1) * (tq // tk)
        top = jnp.maximum(jnp.max(rs_ref[0], axis=0, keepdims=True), jnp.max(rs_ref[1], axis=0, keepdims=True))
        dead = (top <= SB_DEAD) & (_iota2((1, LANES), 1) < nkb)
        kstart = jnp.minimum(jnp.sum(dead.astype(F32)).astype(jnp.int32), nkb)

        def step(kb, ep):
            ep = list(ep)
            ks = pl.multiple_of(kb * tk, tk)
            k = k_ref[pl.ds(ks, tk), :]
            v = v_ref[pl.ds(ks, tk), :]
            mask = col < row + (i * tq - kb * tk)
            at_kb = lane_q == kb
            for hh in range(2):
                lb, l1, suf = _sb_tile(qh[hh], k, mask, tri_ref[0])
                r = _rowsum(jnp.where(at_kb, rs_ref[hh], 0.0))
                lbm = jnp.where(mask, lb, SB_UNSEEN)
                w = jnp.exp(lbm + suf + r)
                e = _dot_nt(doh[hh], v) * w
                pe = ep[hh] + _dot(e.astype(BF16), tri_ref[1])
                beta = jnp.exp(lbm)
                dz = (e - beta * (e + pe)).astype(BF16)
                dq_ref[...] += _dot(dz, k * halves[hh])
                dk_ref[pl.ds(ks, tk), :] += _dot_tn(dz, qh[hh])
                dv_ref[pl.ds(ks, tk), :] += _dot_tn(w.astype(BF16), doh[hh])
                ep[hh] = ep[hh] + _rowsum(e)
            return tuple(ep)

        z1 = jnp.zeros((tq, 1), F32)
        lax.fori_loop(kstart, nkb, step, (z1, z1))

    nh2 = d // LANES
    return side_call(
        kern, side,
        name=name,
        grid=(nh2, nq),
        in_specs=[pl.BlockSpec((tq, LANES), lambda h, i: (i, h)),
                  pl.BlockSpec((s, LANES), lambda h, i: (0, h)),
                  pl.BlockSpec((s, LANES), lambda h, i: (0, h)),
                  pl.BlockSpec((None, 2, tq, LANES), lambda h, i: (h, 0, i, 0)),
                  pl.BlockSpec((tq, LANES), lambda h, i: (i, h)),
                  pl.BlockSpec((2, tk, tk), lambda h, i: (0, 0, 0))],
        out_specs=[pl.BlockSpec((tq, LANES), lambda h, i: (i, h)),
                   pl.BlockSpec((s, LANES), lambda h, i: (0, h)),
                   pl.BlockSpec((s, LANES), lambda h, i: (0, h))],
        out_shape=[jax.ShapeDtypeStruct((s, d), F32)] * 3,
        scratch_shapes=[],
        args=(qs, kn, vb, rsave, do, tri))


def _hooked(plan, tag, call, *args):
    side = plan.side(tag)
    outs, side_outs = call(*args, tag, side)
    if side is not None:
        plan.done(tag, side_outs)
    return outs


def sb_fwd(x, g, wqkv, qg, kg, wo, bd, tag, plan):
    h = rms_fwd(x, g, f"sb_rms_{tag}")
    qkv = mm(h, wqkv, b_chunks=True, out_dtype=BF16, name=f"sb_qkv_{tag}")
    qs, kn, vb = qknorm_fwd(qkv, qg, kg, bd, f"sb_qknorm_{tag}")
    o, rsave = _hooked(plan, f"sb_attn_{tag}", sb_attn_fwd, qs, kn, vb, _sb_tri(min(SB_TK, x.shape[0])))
    xn = mm(o, wo(), add=x, name=f"sb_out_{tag}")
    return xn, (x, h, qkv, qs, kn, vb, rsave, o)


def sb_bwd(dxn, saved, g, wqkv, qg, kg, wo, bd, tag, plan):
    x, h, qkv, qs, kn, vb, rsave, o = saved
    dxn, dxb = dxn
    do = mm(dxb, wo, tb=True, name=f"sb_do_{tag}")
    dwo = mm(o, dxb, ta=True, out_dtype=BF16, name=f"sb_dwo_{tag}")
    dqs, dkn, dv = _hooked(plan, f"sb_dattn_{tag}", sb_attn_bwd, qs, kn, vb, rsave, do, _sb_tri(min(SB_TK, x.shape[0])))
    dqkv, dqg, dkg = qknorm_bwd(qkv, dqs, dkn, dv, qg, kg, bd, f"sb_dqknorm_{tag}")
    dh = mm(dqkv, wqkv, tb=True, b_chunks=True, name=f"sb_dh_{tag}")
    dwqkv = mm(h, dqkv, ta=True, out_dtype=BF16, out_chunks=True, name=f"sb_dwqkv_{tag}")
    dx, dg = rms_bwd(x, g, dh, dxn, f"sb_drms_{tag}")
    nh = dqg.shape[1] // HEAD
    return dx, dg, dwqkv, dqg.reshape(nh, HEAD).sum(0), dkg.reshape(nh, HEAD).sum(0), dwo


def _gelu(x):
    return 0.5 * x * (1.0 + lax.erf(x * (1.0 / math.sqrt(2.0))))


def _gelu_grad(x):
    return 0.5 * (1.0 + lax.erf(x * (1.0 / math.sqrt(2.0)))) + x * jnp.exp(-0.5 * x * x) * (1.0 / math.sqrt(2.0 * math.pi))


def gm_act_fwd(pre, vg, name):
    half = pre.shape[1] // 2

    def fn(p, vgv):
        p = p.astype(F32)
        u = _gelu(p[:, :half])
        v = _gelu(p[:, half:])
        r = lax.rsqrt(jnp.mean(v * v, axis=1, keepdims=True) + EPS)
        return u, v * r * vgv

    return rowwise(fn, [(pre, "row"), (vg, "full")], [(half, F32), (half, BF16)], tr=512, name=name)


def gm_act_bwd(pre, du, dvn, vg, name):
    half = pre.shape[1] // 2

    def fn(p, duv, dvnv, vgv):
        p = p.astype(F32)
        pu, pv = p[:, :half], p[:, half:]
        v = _gelu(pv)
        r = lax.rsqrt(jnp.mean(v * v, axis=1, keepdims=True) + EPS)
        vh = v * r
        dyg = dvnv * vgv
        dv = r * (dyg - vh * jnp.mean(dyg * vh, axis=1, keepdims=True))
        dpre = jnp.concatenate([duv * _gelu_grad(pu), dv * _gelu_grad(pv)], axis=1)
        return dpre, _colsum(dvnv * vh), _colsum(dpre)

    return rowwise(fn, [(pre, "row"), (du, "row"), (dvn, "row"), (vg, "full")],
                   [(2 * half, BF16)], [(1, half), (1, 2 * half)], tr=256, name=name)


def gm_spatial_fwd(u, vn, wc, bst, name):
    s, c = u.shape
    t = CHUNK
    ng = c // LANES

    def kern(u_ref, v_ref, w_ref, b_ref, o_ref):
        for g in range(ng):
            sl = slice(g * LANES, (g + 1) * LANES)
            mixed = _dot(w_ref[g], v_ref[:, sl]) + b_ref[:, sl]
            o_ref[:, sl] = (u_ref[:, sl] * mixed).astype(BF16)

    return pl.pallas_call(
        kern,
        name=name,
        grid=(s // t,),
        in_specs=[pl.BlockSpec((t, c), lambda i: (i, 0)), pl.BlockSpec((t, c), lambda i: (i, 0)),
                  pl.BlockSpec(wc.shape, lambda i: (0, 0, 0)), pl.BlockSpec(bst.shape, lambda i: (0, 0))],
        out_specs=pl.BlockSpec((t, c), lambda i: (i, 0)),
        out_shape=jax.ShapeDtypeStruct((s, c), BF16),
        compiler_params=_params(("parallel",)),
    )(u, vn, wc, bst)


def gm_spatial_bwd(dgate, u, vn, wc, bst, name):
    s, c = u.shape
    t = CHUNK
    ng = c // LANES

    def kern(dg_ref, u_ref, v_ref, w_ref, b_ref, du_ref, dv_ref, dw_ref, db_ref):
        i = pl.program_id(0)

        @pl.when(i == 0)
        def _():
            dw_ref[...] = jnp.zeros_like(dw_ref)
            db_ref[...] = jnp.zeros_like(db_ref)

        for g in range(ng):
            sl = slice(g * LANES, (g + 1) * LANES)
            vg = v_ref[:, sl]
            dgv = dg_ref[:, sl]
            mixed = _dot(w_ref[g], vg) + b_ref[:, sl]
            du_ref[:, sl] = dgv * mixed
            dmix = dgv * u_ref[:, sl]
            dmb = dmix.astype(BF16)
            dv_ref[:, sl] = _dot_tn(w_ref[g], dmb)
            dw_ref[g] += _dot_nt(dmb, vg)
            db_ref[:, sl] += dmix

    return pl.pallas_call(
        kern,
        name=name,
        grid=(s // t,),
        in_specs=[pl.BlockSpec((t, c), lambda i: (i, 0))] * 3 +
                 [pl.BlockSpec(wc.shape, lambda i: (0, 0, 0)), pl.BlockSpec(bst.shape, lambda i: (0, 0))],
        out_specs=[pl.BlockSpec((t, c), lambda i: (i, 0)), pl.BlockSpec((t, c), lambda i: (i, 0)),
                   pl.BlockSpec(wc.shape, lambda i: (0, 0, 0)), pl.BlockSpec(bst.shape, lambda i: (0, 0))],
        out_shape=[jax.ShapeDtypeStruct((s, c), F32), jax.ShapeDtypeStruct((s, c), F32),
                   jax.ShapeDtypeStruct(wc.shape, F32), jax.ShapeDtypeStruct(bst.shape, F32)],
        compiler_params=_params(("arbitrary",)),
    )(dgate, u, vn, wc, bst)


def gm_fwd(x, g, w_in, b_in, vg, wc, bst, w_out, tag):
    h = rms_fwd(x, g, f"gm_rms_{tag}")
    pre = mm(h, w_in, bias=b_in, b_chunks=True, out_dtype=BF16, name=f"gm_in_{tag}")
    u, vn = gm_act_fwd(pre, vg, f"gm_act_{tag}")
    gate = gm_spatial_fwd(u, vn, wc, bst, f"gm_spatial_{tag}")
    xn = mm(gate, w_out, add=x, name=f"gm_out_{tag}")
    return xn, (x, h, pre, u, vn, gate)


def gm_bwd(dxn, saved, g, w_in, vg, wc, bst, w_out, tag):
    x, h, pre, u, vn, gate = saved
    dxn, dxb = dxn
    dgate = mm(dxb, w_out, tb=True, name=f"gm_dgate_{tag}")
    dwout = mm(gate, dxb, ta=True, out_dtype=BF16, name=f"gm_dwout_{tag}")
    du, dvn, dws, dbst = gm_spatial_bwd(dgate, u, vn, wc, bst, f"gm_dspatial_{tag}")
    dpre, dvg, dbin = gm_act_bwd(pre, du, dvn, vg, f"gm_dact_{tag}")
    dh = mm(dpre, w_in, tb=True, b_chunks=True, name=f"gm_dh_{tag}")
    dwin = mm(h, dpre, ta=True, out_dtype=BF16, out_chunks=True, name=f"gm_dwin_{tag}")
    dx, dg = rms_bwd(x, g, dh, dxn, f"gm_drms_{tag}")
    ng = wc.shape[0]
    dws = jnp.where(jnp.tril(jnp.ones((CHUNK, CHUNK), bool)), dws, 0.0)
    dbs = dbst.reshape(CHUNK, ng, LANES).sum(-1).T
    return dx, dg, dwin, dbin, dvg, dws, dbs, dwout


def _conv_taps(xv, prev):
    cat = jnp.concatenate([prev, xv], axis=0)
    return [pltpu.roll(cat, sh, 0)[SUBLANES:] for sh in (3, 2, 1)] + [xv]


def conv_fwd(xbc, ws, b, d_inner, name):
    c = xbc.shape[1]
    nst = (c - d_inner) // 2

    def fn(xv, prev, w0, w1, w2, w3, bv):
        taps = _conv_taps(xv, prev)
        pre = bv + w0 * taps[0] + w1 * taps[1] + w2 * taps[2] + w3 * taps[3]
        out = pre * _sigmoid(pre)
        return out[:, :d_inner], out[:, d_inner:d_inner + nst], out[:, d_inner + nst:]

    return rowwise(fn, [(xbc, "row"), (xbc, "prev")] + [(w, "full") for w in ws] + [(b, "full")],
                   [(d_inner, F32), (nst, F32), (nst, F32)], tr=512, name=name)


def conv_bwd_pre(xbc, ws, b, dxs_a, dxs_b, db_m, dc_m, name):
    c = xbc.shape[1]

    def fn(xv, prev, w0, w1, w2, w3, bv, da, db2, dbm, dcm):
        taps = _conv_taps(xv, prev)
        pre = bv + w0 * taps[0] + w1 * taps[1] + w2 * taps[2] + w3 * taps[3]
        sg = _sigmoid(pre)
        dout = jnp.concatenate([da + db2, dbm, dcm], axis=1)
        dpre = dout * sg * (1.0 + pre * (1.0 - sg))
        return (dpre,) + tuple(_colsum(dpre * tp) for tp in taps) + (_colsum(dpre),)

    return rowwise(fn, [(xbc, "row"), (xbc, "prev")] + [(w, "full") for w in ws] +
                   [(b, "full"), (dxs_a, "row"), (dxs_b, "row"), (db_m, "row"), (dc_m, "row")],
                   [(c, F32)], [(1, c)] * 5, tr=256, name=name)


def conv_bwd_in(dpre, ws, name):
    c = dpre.shape[1]

    def fn(dv, nxt, w0, w1, w2, w3):
        cat = jnp.concatenate([dv, nxt], axis=0)
        n = cat.shape[0]
        up = [pltpu.roll(cat, n - sh, 0)[:dv.shape[0]] for sh in (1, 2, 3)]
        return (w3 * dv + w2 * up[0] + w1 * up[1] + w0 * up[2],)

    return rowwise(fn, [(dpre, "row"), (dpre, "next")] + [(w, "full") for w in ws], [(c, BF16)], tr=512, name=name)[0]


def ssd_pre(dtr, bias, alog, name):
    def fn(d, bv, al, tri):
        dt = _softplus(d + bv)
        a = dt * (-jnp.exp(al))
        return dt, _dot_x3_left(tri, a)

    tri = jnp.tril(jnp.ones((CHUNK, CHUNK), BF16))
    return rowwise(fn, [(dtr, "row"), (bias, "full"), (alog, "full"), (tri, "full")],
                   [(LANES, F32), (LANES, F32)], tr=CHUNK, name=name)


def _ssd_layouts(v, ngroups, hpg):
    s = v.shape[0]
    col = v[:, :ngroups * hpg].T.reshape(ngroups, hpg, s, 1)
    return jnp.broadcast_to(col, (ngroups, hpg, s, LANES))


def _ssd_rowform(acum, ngroups, hpg):
    s = acum.shape[0]
    nc = s // CHUNK
    a = acum[:, :ngroups * hpg].reshape(nc, CHUNK, ngroups, hpg).transpose(2, 0, 3, 1)
    last = jnp.broadcast_to(a[..., CHUNK - 1:], a.shape)
    return jnp.concatenate([a, last], axis=2)


def ssd_chunk_fwd(xs, bm, cm, col_a, col_dt, rowf, name, side=None):
    s, d_inner = xs.shape
    ln = CHUNK
    nc = s // ln
    nsub = _pick(nc, (SSD_SUB, 2, 1))
    rows = nsub * ln
    ng, hpg = col_a.shape[0], col_a.shape[1]
    gw = d_inner // ng
    assert gw == hpg * HEAD and gw % LANES == 0 and bm.shape[1] == ng * LANES

    def kern(x_ref, b_ref, c_ref, ca_ref, cd_ref, rf_ref, y_ref, hp_ref, h_scr):
        @pl.when(pl.program_id(1) == 0)
        def _():
            h_scr[...] = jnp.zeros_like(h_scr)

        causal = _iota2((ln, ln), 0) >= _iota2((ln, ln), 1)
        lane = _iota2((1, LANES), 1)
        for sc in range(nsub):
            rs = slice(sc * ln, (sc + 1) * ln)
            bb = b_ref[rs, :].astype(BF16)
            cbf = c_ref[rs, :].astype(BF16)
            cb = _dot_nt(cbf, bb)
            ys = [jnp.zeros((ln, LANES), F32) for _ in range(gw // LANES)]
            for r in range(hpg):
                j, hf = divmod(r, LANES // HEAD)
                mh = ((lane >= HEAD * hf) & (lane < HEAD * (hf + 1))).astype(F32)
                ac = ca_ref[r, rs, :]
                ar = rf_ref[sc, pl.ds(r, 1), :]
                aend = rf_ref[sc, pl.ds(4 + r, 1), :]
                dm = jnp.exp(jnp.minimum(ac - ar, 0.0))
                m = jnp.where(causal, cb * dm, 0.0).astype(BF16)
                xdt = x_ref[rs, j * LANES:(j + 1) * LANES] * cd_ref[r, rs, :] * mh
                h = h_scr[r]
                hp_ref[sc, r] = h
                ys[j] = ys[j] + _dot(m, xdt.astype(BF16)) + _dot_nt(cbf, h.astype(BF16)) * jnp.exp(ac)
                dte = jnp.exp(aend - ac)
                h_scr[r] = jnp.exp(aend) * h + _dot_tn((xdt * dte).astype(BF16), bb)
            for j in range(gw // LANES):
                y_ref[rs, j * LANES:(j + 1) * LANES] = ys[j]

    colspec = pl.BlockSpec((None, hpg, rows, LANES), lambda g, c: (g, 0, c, 0))
    return side_call(
        kern, side,
        name=name,
        grid=(ng, nc // nsub),
        in_specs=[pl.BlockSpec((rows, gw), lambda g, c: (c, g)),
                  pl.BlockSpec((rows, LANES), lambda g, c: (c, g)),
                  pl.BlockSpec((rows, LANES), lambda g, c: (c, g)),
                  colspec, colspec,
                  pl.BlockSpec((None, nsub, 8, LANES), lambda g, c: (g, c, 0, 0))],
        out_specs=[pl.BlockSpec((rows, gw), lambda g, c: (c, g)),
                   pl.BlockSpec((None, nsub, hpg, LANES, LANES), lambda g, c: (g, c, 0, 0, 0))],
        out_shape=[jax.ShapeDtypeStruct((s, d_inner), F32),
                   jax.ShapeDtypeStruct((ng, nc, hpg, LANES, LANES), F32)],
        scratch_shapes=[pltpu.VMEM((hpg, LANES, LANES), F32)],
        args=(xs, bm, cm, col_a, col_dt, rowf))


def ssd_chunk_bwd(xs, bm, cm, col_a, col_dt, rowf, hprev, dy, name, side=None):
    s, d_inner = xs.shape
    ln = CHUNK
    nc = s // ln
    nsub = _pick(nc, (SSD_SUB, 2, 1))
    rows = nsub * ln
    ng, hpg = col_a.shape[0], col_a.shape[1]
    gw = d_inner // ng

    def kern(x_ref, b_ref, c_ref, ca_ref, cd_ref, rf_ref, hp_ref, dy_ref,
             dx_ref, db_ref, dc_ref, ddt_ref, da_ref, dh_scr):
        @pl.when(pl.program_id(1) == 0)
        def _():
            dh_scr[...] = jnp.zeros_like(dh_scr)

        row, col = _iota2((ln, ln), 0), _iota2((ln, ln), 1)
        causal = row >= col
        tri_ge = (col >= row).astype(BF16)
        ones = jnp.ones((ln, LANES), BF16)
        lane = _iota2((1, LANES), 1)
        last_row = (_iota2((ln, 1), 0) == ln - 1).astype(F32)
        for sc in reversed(range(nsub)):
            rs = slice(sc * ln, (sc + 1) * ln)
            bb = b_ref[rs, :].astype(BF16)
            cbf = c_ref[rs, :].astype(BF16)
            cb = _dot_nt(cbf, bb)
            dcb = jnp.zeros((ln, ln), F32)
            d_b = jnp.zeros((ln, LANES), F32)
            d_c = jnp.zeros((ln, LANES), F32)
            dxs = [jnp.zeros((ln, LANES), F32) for _ in range(gw // LANES)]
            for r in range(hpg):
                j, hf = divmod(r, LANES // HEAD)
                mh = ((lane >= HEAD * hf) & (lane < HEAD * (hf + 1))).astype(F32)
                ac = ca_ref[r, rs, :]
                dt = cd_ref[r, rs, :]
                ar = rf_ref[sc, pl.ds(r, 1), :]
                aend = rf_ref[sc, pl.ds(4 + r, 1), :]
                dm = jnp.where(causal, jnp.exp(jnp.minimum(ac - ar, 0.0)), 0.0)
                m = cb * dm
                mb = m.astype(BF16)
                xp = x_ref[rs, j * LANES:(j + 1) * LANES]
                xdt = xp * dt * mh
                xdtb = xdt.astype(BF16)
                dyp = dy_ref[rs, j * LANES:(j + 1) * LANES] * mh
                dypb = dyp.astype(BF16)
                h = hp_ref[sc, r]
                hb = h.astype(BF16)
                dh = dh_scr[r]
                dhb = dh.astype(BF16)
                e_in = jnp.exp(ac)
                dte = jnp.exp(aend - ac)
                eend = jnp.exp(aend)
                d_m = _dot_nt(dypb, xdtb)
                dcb = dcb + d_m * dm
                gm = d_m * m
                yoff_pre = _dot_nt(cbf, hb)
                bdh = _dot_nt(bb, dhb)
                dxdt = _dot_tn(mb, dypb) + bdh * dte
                t1 = _rowsum(xdt * bdh) * dte
                gh, gl = _split2(gm)
                dacum = (_rowsum(gm) - (_dot_tn(gh, ones) + _dot_tn(gl, ones))
                         + _rowsum(dyp * yoff_pre) * e_in - t1)
                end_term = _colsum(t1) + eend * jnp.sum(_colsum(dh * h), axis=1, keepdims=True)
                dacum = dacum + last_row * end_term
                da_ref[r, rs, :] = _dot_x3_left(tri_ge, dacum)
                ddt_ref[r, rs, :] = jnp.broadcast_to(_rowsum(dxdt * xp), (ln, LANES))
                dxs[j] = dxs[j] + dxdt * dt
                d_b = d_b + _dot((xdt * dte).astype(BF16), dhb)
                dye = (dyp * e_in).astype(BF16)
                d_c = d_c + _dot(dye, hb)
                dh_scr[r] = eend * dh + _dot_tn(dye, cbf)
            dcbb = dcb.astype(BF16)
            dc_ref[rs, :] = d_c + _dot(dcbb, bb)
            db_ref[rs, :] = d_b + _dot_tn(dcbb, cbf)
            for j in range(gw // LANES):
                dx_ref[rs, j * LANES:(j + 1) * LANES] = dxs[j]

    rev = nc // nsub - 1
    colspec = pl.BlockSpec((None, hpg, rows, LANES), lambda g, c: (g, 0, rev - c, 0))
    return side_call(
        kern, side,
        name=name,
        grid=(ng, nc // nsub),
        in_specs=[pl.BlockSpec((rows, gw), lambda g, c: (rev - c, g)),
                  pl.BlockSpec((rows, LANES), lambda g, c: (rev - c, g)),
                  pl.BlockSpec((rows, LANES), lambda g, c: (rev - c, g)),
                  colspec, colspec,
                  pl.BlockSpec((None, nsub, 8, LANES), lambda g, c: (g, rev - c, 0, 0)),
                  pl.BlockSpec((None, nsub, hpg, LANES, LANES), lambda g, c: (g, rev - c, 0, 0, 0)),
                  pl.BlockSpec((rows, gw), lambda g, c: (rev - c, g))],
        out_specs=[pl.BlockSpec((rows, gw), lambda g, c: (rev - c, g)),
                   pl.BlockSpec((rows, LANES), lambda g, c: (rev - c, g)),
                   pl.BlockSpec((rows, LANES), lambda g, c: (rev - c, g)),
                   colspec, colspec],
        out_shape=[jax.ShapeDtypeStruct((s, d_inner), F32),
                   jax.ShapeDtypeStruct(bm.shape, F32), jax.ShapeDtypeStruct(cm.shape, F32),
                   jax.ShapeDtypeStruct(col_a.shape, F32), jax.ShapeDtypeStruct(col_a.shape, F32)],
        scratch_shapes=[pltpu.VMEM((hpg, LANES, LANES), F32)],
        args=(xs, bm, cm, col_a, col_dt, rowf, hprev, dy))


def gnorm_fwd(y, xs, z, dexp, gain, ngroups, name):
    c = y.shape[1]
    gw = c // ngroups

    def fn(yv, xv, zv, dv, gv):
        yg = (yv + xv * dv) * (zv * _sigmoid(zv))
        outs = []
        for k in range(ngroups):
            t = yg[:, k * gw:(k + 1) * gw]
            outs.append(t * lax.rsqrt(jnp.mean(t * t, axis=1, keepdims=True) + EPS))
        return (jnp.concatenate(outs, axis=1) * gv,)

    return rowwise(fn, [(y, "row"), (xs, "row"), (z, "row"), (dexp, "full"), (gain, "full")], [(c, BF16)], tr=512, name=name)[0]


def gnorm_bwd(dn, y, xs, z, dexp, gain, ngroups, name):
    c = y.shape[1]
    gw = c // ngroups

    def fn(dnv, yv, xv, zv, dv, gv):
        yd = yv + xv * dv
        sg = _sigmoid(zv)
        sz = zv * sg
        yg = yd * sz
        dng = dnv * gv
        dyg, yh = [], []
        for k in range(ngroups):
            sl = slice(k * gw, (k + 1) * gw)
            t = yg[:, sl]
            r = lax.rsqrt(jnp.mean(t * t, axis=1, keepdims=True) + EPS)
            th = t * r
            dyg.append(r * (dng[:, sl] - th * jnp.mean(dng[:, sl] * th, axis=1, keepdims=True)))
            yh.append(th)
        dyg = jnp.concatenate(dyg, axis=1)
        yh = jnp.concatenate(yh, axis=1)
        dyd = dyg * sz
        dz = dyg * yd * (sg * (1.0 + zv * (1.0 - sg)))
        return dyd, dyd * dv, dz, _colsum(dyd * xv), _colsum(dnv * yh)

    return rowwise(fn, [(dn, "row"), (y, "row"), (xs, "row"), (z, "row"), (dexp, "full"), (gain, "full")],
                   [(c, F32), (c, F32), (c, BF16)], [(1, c), (1, c)], tr=256, name=name)


def ssd_post(ddt, da, dt, dtr, bias, alog, name):
    def fn(ddtv, dav, dtv, dtrv, bv, al):
        a_neg = -jnp.exp(al)
        ddtr = (ddtv + dav * a_neg) * _sigmoid(dtrv + bv)
        return ddtr, _colsum(ddtr), _colsum(dav * dtv) * a_neg

    return rowwise(fn, [(ddt, "row"), (da, "row"), (dt, "row"), (dtr, "row"), (bias, "full"), (alog, "full")],
                   [(LANES, BF16)], [(1, LANES), (1, LANES)], tr=512, name=name)


def _from_colform(v, s):
    ng, hpg = v.shape[0], v.shape[1]
    flat = v[..., 0].reshape(ng * hpg, s).T
    return jnp.pad(flat, ((0, 0), (0, LANES - ng * hpg)))


def ssm_fwd(x, g, p, tag, plan):
    ng, hpg, d_inner = p["ng"], p["hpg"], p["d_inner"]
    h = rms_fwd(x, g, f"ssm_rms_{tag}")
    z = mm(h, p["w_z"], name=f"ssm_inz_{tag}")
    xbc = mm(h, p["w_xbc"], name=f"ssm_inx_{tag}")
    dtr = mm(h, p["w_dt"], name=f"ssm_indt_{tag}")
    xs, bm, cm = conv_fwd(xbc, p["conv_w"], p["conv_b"], d_inner, f"ssm_conv_{tag}")
    dt, acum = ssd_pre(dtr, p["dt_bias"], p["a_log"], f"ssm_pre_{tag}")
    col_a, col_dt = _ssd_layouts(acum, ng, hpg), _ssd_layouts(dt, ng, hpg)
    rowf = _ssd_rowform(acum, ng, hpg)
    y, hprev = _hooked(plan, f"ssm_scan_{tag}", ssd_chunk_fwd, xs, bm, cm, col_a, col_dt, rowf)
    n = gnorm_fwd(y, xs, z, p["d_exp"], p["norm_gain"], ng, f"ssm_gnorm_{tag}")
    xn = mm(n, p["w_out"], add=x, name=f"ssm_out_{tag}")
    return xn, (x, h, z, xbc, dtr, xs, bm, cm, dt, col_a, col_dt, rowf, y, hprev, n)


def ssm_bwd(dxn, saved, g, p, tag, plan):
    x, h, z, xbc, dtr, xs, bm, cm, dt, col_a, col_dt, rowf, y, hprev, n = saved
    ng, hpg, d_inner = p["ng"], p["hpg"], p["d_inner"]
    s = x.shape[0]
    dxn, dxb = dxn
    dn = mm(dxb, p["w_out"], tb=True, name=f"ssm_dn_{tag}")
    dwout = mm(n, dxb, ta=True, out_dtype=BF16, name=f"ssm_dwout_{tag}")
    dy, dxs_skip, dz, dd_lane, dgain = gnorm_bwd(dn, y, xs, z, p["d_exp"], p["norm_gain"], ng, f"ssm_dgnorm_{tag}")
    dxs, dbm, dcm, ddt_c, da_c = _hooked(plan, f"ssm_dscan_{tag}", ssd_chunk_bwd, xs, bm, cm, col_a, col_dt, rowf, hprev, dy)
    ddtr, dbias, dalog = ssd_post(_from_colform(ddt_c, s), _from_colform(da_c, s), dt, dtr,
                                  p["dt_bias"], p["a_log"], f"ssm_post_{tag}")
    res = conv_bwd_pre(xbc, p["conv_w"], p["conv_b"], dxs, dxs_skip, dbm, dcm, f"ssm_dconv_{tag}")
    dpre, dconv_w, dconv_b = res[0], jnp.concatenate(res[1:5], axis=0), res[5]
    dxbc = conv_bwd_in(dpre, p["conv_w"], f"ssm_dconvin_{tag}")
    dh = mm(dz, p["w_z"], tb=True, name=f"ssm_dhz_{tag}")
    dh = mm(dxbc, p["w_xbc"], tb=True, add=dh, name=f"ssm_dhx_{tag}")
    dh = mm(ddtr, p["w_dt"], tb=True, add=dh, name=f"ssm_dhdt_{tag}")
    dwz = mm(h, dz, ta=True, out_dtype=BF16, name=f"ssm_dwz_{tag}")
    dwxbc = mm(h, dxbc, ta=True, out_dtype=BF16, name=f"ssm_dwxbc_{tag}")
    dwdt = mm(h, ddtr, ta=True, out_dtype=BF16, name=f"ssm_dwdt_{tag}")
    dx, dg = rms_bwd(x, g, dh, dxn, f"ssm_drms_{tag}")
    nh = ng * hpg
    dwin = jnp.concatenate([dwz, dwxbc, dwdt[:, :nh]], axis=1)
    dd = dd_lane.reshape(nh, HEAD).sum(-1)
    return dx, dg, dict(w_in=dwin, conv_w=dconv_w, conv_b=dconv_b, dt_bias=dbias[0, :nh], a_log=dalog[0, :nh],
                        d=dd, norm_gain=dgain, w_out=dwout)


def local_step(x, target, w, plan):
    d = x.shape[1]
    depth = w["mix_norm"].shape[0]
    bd = _head_blockdiag(LANES)
    tril = jnp.tril(jnp.ones((CHUNK, CHUNK), bool))
    ssm_heads = w["ssm_dt_bias"].shape[1]
    d_inner = w["ssm_norm_gain"].shape[1]
    ng = w["ssm_norm_gain"].shape[1] // 256
    nstate = CHUNK

    def pad_lanes(v):
        return jnp.pad(v, ((0, 0), (0, LANES - v.shape[1])))

    def ssm_params(j):
        w_in = w["ssm_w_in"][j]
        cw = w["ssm_conv_w"][j]
        return dict(ng=ng, hpg=ssm_heads // ng, d_inner=d_inner,
                    w_z=w_in[:, :d_inner], w_xbc=w_in[:, d_inner:d_inner + d_inner + 2 * ng * nstate],
                    w_dt=pad_lanes(w_in[:, 2 * d_inner + 2 * ng * nstate:]),
                    conv_w=[cw[k:k + 1] for k in range(cw.shape[0])], conv_b=w["ssm_conv_b"][j:j + 1],
                    dt_bias=pad_lanes(w["ssm_dt_bias"][j:j + 1]), a_log=pad_lanes(w["ssm_a_log"][j:j + 1]),
                    d_exp=jnp.repeat(w["ssm_d"][j], HEAD)[None, :], norm_gain=w["ssm_norm_gain"][j:j + 1],
                    w_out=w["ssm_w_out"][j])

    def gm_params(j):
        wc = jnp.where(tril, w["gm_w_s"][j], 0.0).astype(BF16)
        bst = jnp.repeat(w["gm_b_s"][j].T, LANES, axis=1)
        return wc, bst

    def sb_gains(j):
        nh = d // HEAD
        return jnp.tile(w["sb_q_gain"][j], nh)[None, :], jnp.tile(w["sb_k_gain"][j], nh)[None, :]

    saved = []
    cur = x
    for i in range(depth):
        kind, j = i % 3, i // 3
        gmix = w["mix_norm"][i:i + 1]
        if kind == 0:
            qg, kg = sb_gains(j)
            cur, sv = sb_fwd(cur, gmix, w["sb_w_qkv"][j], qg, kg, lambda j=j: w["sb_w_o"][j], bd, f"{i}", plan)
        elif kind == 1:
            wc, bst = gm_params(j)
            cur, sv = gm_fwd(cur, gmix, w["gm_w_in"][j], w["gm_b_in"][j:j + 1], w["gm_v_gain"][j:j + 1], wc, bst,
                             w["gm_w_out"][j], f"{i}")
        else:
            cur, sv = ssm_fwd(cur, gmix, ssm_params(j), f"{i}", plan)
        cur, sv2 = ffn_fwd(cur, w["ffn_norm"][i:i + 1], w["ffn_w_gu"][i], w["ffn_w_down"][i], f"{i}", plan)
        saved.append((sv, sv2))

    loss, dcur = loss_and_grad(cur, target, "loss")

    grads = {k: [None] * len(v) for k, v in w.items()}
    for i in reversed(range(depth)):
        kind, j = i % 3, i // 3
        sv, sv2 = saved[i]
        gmix = w["mix_norm"][i:i + 1]
        dcur, dgf, dwgu, dwdown = ffn_bwd(dcur, sv2, w["ffn_norm"][i:i + 1], w["ffn_w_gu"][i], w["ffn_w_down"][i], f"{i}")
        grads["ffn_norm"][i], grads["ffn_w_gu"][i], grads["ffn_w_down"][i] = dgf[0], dwgu, dwdown
        plan.grads_ready({("ffn_w_gu", i): dwgu, ("ffn_w_down", i): dwdown})
        if kind == 0:
            qg, kg = sb_gains(j)
            dcur, dg, dwqkv, dqg, dkg, dwo = sb_bwd(dcur, sv, gmix, w["sb_w_qkv"][j], qg, kg, w["sb_w_o"][j], bd, f"{i}", plan)
            grads["sb_w_qkv"][j], grads["sb_q_gain"][j], grads["sb_k_gain"][j], grads["sb_w_o"][j] = dwqkv, dqg, dkg, dwo
        elif kind == 1:
            wc, bst = gm_params(j)
            dcur, dg, dwin, dbin, dvg, dws, dbs, dwout = gm_bwd(dcur, sv, gmix, w["gm_w_in"][j], w["gm_v_gain"][j:j + 1],
                                                                 wc, bst, w["gm_w_out"][j], f"{i}")
            grads["gm_w_in"][j], grads["gm_b_in"][j], grads["gm_v_gain"][j] = dwin, dbin[0], dvg[0]
            grads["gm_w_s"][j], grads["gm_b_s"][j], grads["gm_w_out"][j] = dws, dbs, dwout
        else:
            dcur, dg, gs = ssm_bwd(dcur, sv, gmix, ssm_params(j), f"{i}", plan)
            grads["ssm_w_in"][j], grads["ssm_conv_w"][j], grads["ssm_conv_b"][j] = gs["w_in"], gs["conv_w"], gs["conv_b"][0]
            grads["ssm_dt_bias"][j], grads["ssm_a_log"][j], grads["ssm_d"][j] = gs["dt_bias"], gs["a_log"], gs["d"]
            grads["ssm_norm_gain"][j], grads["ssm_w_out"][j] = gs["norm_gain"][0], gs["w_out"]
        grads["mix_norm"][i] = dg[0]
        mixer = {0: ("sb_w_qkv", "sb_w_o"), 1: ("gm_w_in", "gm_w_out"), 2: ("ssm_w_in", "ssm_w_out")}[kind]
        plan.grads_ready({(n, j): grads[n][j] for n in mixer})
    grads = {k: (v if k in MATRICES else jnp.stack(v)) for k, v in grads.items()}
    return loss, dcur[0], grads


WEIGHTS = ["mix_norm", "ffn_norm", "sb_w_qkv", "sb_q_gain", "sb_k_gain", "sb_w_o", "gm_w_in", "gm_b_in", "gm_v_gain",
           "gm_w_s", "gm_b_s", "gm_w_out", "ssm_w_in", "ssm_conv_w", "ssm_conv_b", "ssm_dt_bias", "ssm_a_log", "ssm_d",
           "ssm_norm_gain", "ssm_w_out", "ffn_w_gu", "ffn_w_down"]
SHARDED = {"sb_w_qkv": 2, "sb_w_o": 1, "gm_w_in": 2, "gm_w_out": 1, "ssm_w_in": 2, "ssm_conv_w": 2, "ssm_conv_b": 1,
           "ssm_norm_gain": 1, "ssm_w_out": 1, "ffn_w_gu": 2, "ffn_w_down": 1}
EXACT = ("ssm_conv_w", "ssm_conv_b", "ssm_norm_gain")
MATRICES = tuple(n for n in SHARDED if n not in EXACT)
COLUMN_BLOCKS = ("sb_w_qkv", "gm_w_in", "ffn_w_gu")
REPLICATED = [n for n in WEIGHTS if n not in SHARDED]
N_CHIPS = 4
N_DEV = 8
PACK_COLS = 1024


def _pack(pieces, dtype, align):
    flat = jnp.concatenate([p.reshape(-1).astype(dtype) for p in pieces])
    rows = -(-flat.shape[0] // (PACK_COLS * align)) * align
    flat = jnp.pad(flat, (0, rows * PACK_COLS - flat.shape[0]))
    return flat.reshape(rows, PACK_COLS)


def _unpack(flat, shapes):
    out, off = [], 0
    for shp in shapes:
        n = math.prod(shp)
        out.append(flat[off:off + n].reshape(shp))
        off += n
    return out


ANY = pl.BlockSpec(memory_space=pl.ANY)


def _pos():
    return lax.axis_index("x"), lax.axis_index("y"), lax.axis_index("c")


def _remote(src, dst, send, recv, k, to):
    return pltpu.make_async_remote_copy(src_ref=src, dst_ref=dst, send_sem=send.at[k], recv_sem=recv.at[k],
                                        device_id=to, device_id_type=MESH_ID)


def _comm_call(body, name, ins, out_shapes, nsem, aliases=None):
    return pl.pallas_call(
        body, name=name, out_shape=out_shapes,
        in_specs=[ANY] * len(ins), out_specs=[ANY] * len(out_shapes),
        scratch_shapes=[pltpu.SemaphoreType.DMA((nsem,)), pltpu.SemaphoreType.DMA((nsem,))],
        input_output_aliases=aliases or {},
    )(*ins)


def stage_shard(w, layer, chip, name, dtype=BF16):
    _, rows, cols = w.shape
    tr = _pick(rows, (512, 352, 256, 128))

    def kern(idx_ref, w_ref, o_ref):
        o_ref[...] = w_ref[...].astype(dtype)

    grid_spec = pltpu.PrefetchScalarGridSpec(
        num_scalar_prefetch=1, grid=(rows // tr,),
        in_specs=[pl.BlockSpec((None, tr, cols), lambda i, idx: (layer, i, 0))],
        out_specs=pl.BlockSpec((None, tr, cols), lambda i, idx: (idx[0], i, 0)))
    return pl.pallas_call(
        kern, name=name, grid_spec=grid_spec,
        out_shape=jax.ShapeDtypeStruct((N_CHIPS, rows, cols), dtype),
        compiler_params=_params(("parallel",)),
    )(jnp.reshape(chip, (1,)).astype(jnp.int32), w)


class Side:
    def __init__(self, arrays, out_shapes, aliases, nsem, start, finish):
        self.arrays, self.out_shapes, self.aliases, self.nsem = list(arrays), list(out_shapes), aliases, nsem
        self.start, self.finish = start, finish


def run_side(side, name):
    n_in, n_out = len(side.arrays), len(side.out_shapes)

    def body(*refs):
        ins, outs = refs[:n_in], refs[n_in:n_in + n_out]
        send, recv = refs[n_in + n_out:]
        side.start(ins, outs, send, recv)
        side.finish(ins, outs, send, recv)

    return _comm_call(body, name, side.arrays, side.out_shapes, side.nsem, aliases=side.aliases)


def side_call(kern, side, *, name, grid, in_specs, out_specs, out_shape, scratch_shapes, args):
    if side is None:
        res = pl.pallas_call(kern, name=name, grid=grid, in_specs=in_specs, out_specs=out_specs, out_shape=out_shape,
                             scratch_shapes=scratch_shapes,
                             compiler_params=_params(("parallel",) + ("arbitrary",) * (len(grid) - 1)))(*args)
        return list(res), []
    n_in, n_out, n_scr = len(in_specs), len(out_specs), len(scratch_shapes)
    s_in, s_out = len(side.arrays), len(side.out_shapes)

    def body(*refs):
        ins, refs = refs[:n_in], refs[n_in:]
        side_ins, refs = refs[:s_in], refs[s_in:]
        outs, refs = refs[:n_out], refs[n_out:]
        side_outs, refs = refs[:s_out], refs[s_out:]
        scr, (send, recv) = refs[:n_scr], refs[n_scr:]
        first, last = None, None
        for axis, size in enumerate(grid):
            at0, at1 = pl.program_id(axis) == 0, pl.program_id(axis) == size - 1
            first = at0 if first is None else first & at0
            last = at1 if last is None else last & at1

        @pl.when(first)
        def _():
            side.start(side_ins, side_outs, send, recv)

        kern(*ins, *outs, *scr)

        @pl.when(last)
        def _():
            side.finish(side_ins, side_outs, send, recv)

    res = pl.pallas_call(
        body, name=name, grid=grid,
        in_specs=list(in_specs) + [ANY] * s_in, out_specs=list(out_specs) + [ANY] * s_out,
        out_shape=list(out_shape) + side.out_shapes,
        scratch_shapes=list(scratch_shapes) + [pltpu.SemaphoreType.DMA((side.nsem,)), pltpu.SemaphoreType.DMA((side.nsem,))],
        input_output_aliases={n_in + a: n_out + b for a, b in side.aliases.items()},
        compiler_params=_params(("arbitrary",) * len(grid)),
    )(*args, *side.arrays)
    return list(res[:n_out]), list(res[n_out:])


def gather_side(staged):
    n = len(staged)

    def plan(o_refs, send, recv):
        x, y, c = _pos()
        chips = [(1 - x, y), (x, 1 - y), (1 - x, 1 - y)]

        def part(u, chip, cc):
            half = staged[u].shape[1] // 2
            return o_refs[u].at[2 * chip[0] + chip[1], pl.ds(cc * half, half), :]

        first = [_remote(part(u, (x, y), c), part(u, (x, y), c), send, recv, 6 * u + j, (*chip, c))
                 for u in range(n) for j, chip in enumerate(chips)]
        landed = [_remote(part(u, chip, c), part(u, chip, c), send, recv, 6 * u + j, (x, y, c))
                  for u in range(n) for j, chip in enumerate(chips)]
        passed = [_remote(part(u, chip, c), part(u, chip, c), send, recv, 6 * u + 3 + j, (x, y, 1 - c))
                  for u in range(n) for j, chip in enumerate(chips)]
        handed = [_remote(part(u, chip, 1 - c), part(u, chip, 1 - c), send, recv, 6 * u + 3 + j, (x, y, c))
                  for u in range(n) for j, chip in enumerate(chips)]
        return first, landed, passed, handed

    def start(ins, outs, send, recv):
        for cp in plan(outs, send, recv)[0]:
            cp.start()

    def finish(ins, outs, send, recv):
        first, landed, passed, handed = plan(outs, send, recv)
        for got, fw in zip(landed, passed):
            got.wait_recv()
            fw.start()
        for got in handed:
            got.wait_recv()
        for cp in first + passed:
            cp.wait_send()

    outs = [jax.ShapeDtypeStruct(s.shape, s.dtype) for s in staged]
    return Side(staged, outs, {u: u for u in range(n)}, 6 * n, start, finish)


def swap_halves(gps, name):
    n = len(gps)

    def body(*refs):
        g_refs, r_refs = refs[:n], refs[n:2 * n]
        send, recv = refs[2 * n:]
        x, y, c = _pos()
        cps = []
        for u in range(n):
            half = gps[u].shape[1] // 2
            cps.append(_remote(g_refs[u].at[:, pl.ds((1 - c) * half, half), :], r_refs[u], send, recv, u, (x, y, 1 - c)))
        for cp in cps:
            cp.start()
        for cp in cps:
            cp.wait()

    outs = [jax.ShapeDtypeStruct((g.shape[0], g.shape[1] // 2, g.shape[2]), g.dtype) for g in gps]
    return _comm_call(body, name, gps, outs, n)


def scatter_side(parts):
    n = len(parts)

    def plan(p_refs, r_refs, send, recv):
        x, y, c = _pos()
        chips = [(1 - x, y), (x, 1 - y), (1 - x, 1 - y)]
        return [_remote(p_refs[u].at[2 * chip[0] + chip[1]], r_refs[u].at[j], send, recv, 3 * u + j, (*chip, c))
                for u in range(n) for j, chip in enumerate(chips)]

    def start(ins, outs, send, recv):
        for cp in plan(ins, outs, send, recv):
            cp.start()

    def finish(ins, outs, send, recv):
        for cp in plan(ins, outs, send, recv):
            cp.wait()

    outs = [jax.ShapeDtypeStruct((N_CHIPS - 1,) + p.shape[1:], p.dtype) for p in parts]
    return Side(parts, outs, {}, 3 * n, start, finish)


def join_halves(bufs):
    n = len(bufs)

    def body(*refs):
        o_refs = refs[n:2 * n]
        send, recv = refs[2 * n:]
        x, y, c = _pos()

        def rows(u, cc):
            half = bufs[u].shape[1] // 2
            return o_refs[u].at[:, pl.ds(cc * half, half), :]

        cps = [_remote(rows(u, c), rows(u, c), send, recv, u, (x, y, 1 - c)) for u in range(n)]
        for cp in cps:
            cp.start()
        for u in range(n):
            _remote(rows(u, 1 - c), rows(u, 1 - c), send, recv, u, (x, y, c)).wait_recv()
        for cp in cps:
            cp.wait_send()

    outs = [jax.ShapeDtypeStruct(b.shape, b.dtype) for b in bufs]
    return _comm_call(body, "join_halves", bufs, outs, n, aliases={u: u for u in range(n)})


def gather_small(sg, name):
    rows, cols = sg.shape

    def body(s_ref, o_ref, send, recv, lsem):
        x, y, c = _pos()
        me, sibling = (x, y, c), (x, y, 1 - c)
        chips = [(1 - x, y), (x, 1 - y), (1 - x, 1 - y)]

        def blk(px, py, pc):
            return o_ref.at[4 * px + 2 * py + pc]

        mine = pltpu.make_async_copy(s_ref, blk(*me), lsem)
        mine.start()
        first = [_remote(s_ref, blk(*me), send, recv, 0, sibling)]
        first += [_remote(s_ref, blk(*me), send, recv, 1 + j, (*chip, c)) for j, chip in enumerate(chips)]
        for cp in first:
            cp.start()
        passed = [_remote(blk(*chip, c), blk(*chip, c), send, recv, 4 + j, sibling) for j, chip in enumerate(chips)]
        for j, chip in enumerate(chips):
            _remote(blk(*chip, c), blk(*chip, c), send, recv, 1 + j, me).wait_recv()
            passed[j].start()
        _remote(blk(*sibling), blk(*sibling), send, recv, 0, me).wait_recv()
        for j, chip in enumerate(chips):
            _remote(blk(*chip, 1 - c), blk(*chip, 1 - c), send, recv, 4 + j, me).wait_recv()
        for cp in first + passed:
            cp.wait_send()
        mine.wait()

    return pl.pallas_call(
        body, name=name,
        out_shape=jax.ShapeDtypeStruct((N_DEV, rows, cols), sg.dtype),
        in_specs=[ANY], out_specs=ANY,
        scratch_shapes=[pltpu.SemaphoreType.DMA((N_DEV - 1,)), pltpu.SemaphoreType.DMA((N_DEV - 1,)), pltpu.SemaphoreType.DMA],
    )(sg)


def sum_cores(gp, theirs, core, chip, name):
    nch, rows, cols = gp.shape
    half = rows // 2
    tr = _pick(half, (512, 352, 256, 176, 128, 64))
    nb = half // tr

    def kern(idx_ref, g_ref, t_ref, own_ref, all_ref):
        k = pl.program_id(1)
        s = g_ref[...].astype(F32) + t_ref[...].astype(F32)
        all_ref[...] = s.astype(BF16)

        @pl.when(k == idx_ref[1])
        def _():
            own_ref[...] = s

    grid_spec = pltpu.PrefetchScalarGridSpec(
        num_scalar_prefetch=1, grid=(nb, nch),
        in_specs=[pl.BlockSpec((None, tr, cols), lambda i, k, idx: (k, idx[0] * nb + i, 0)),
                  pl.BlockSpec((None, tr, cols), lambda i, k, idx: (k, i, 0))],
        out_specs=[pl.BlockSpec((tr, cols), lambda i, k, idx: (i, 0)),
                   pl.BlockSpec((None, tr, cols), lambda i, k, idx: (k, i, 0))])
    return pl.pallas_call(
        kern, name=name, grid_spec=grid_spec,
        out_shape=[jax.ShapeDtypeStruct((half, cols), F32), jax.ShapeDtypeStruct((nch, half, cols), BF16)],
        compiler_params=_params(("parallel", "arbitrary")),
    )(jnp.stack([core, chip]).astype(jnp.int32), gp, theirs)


def sum_chips(own, others, core, layer, nlayers, into, name):
    half, cols = own.shape
    tr = _pick(half, (512, 352, 256, 176, 128, 64))
    nb = half // tr

    def kern(idx_ref, o_ref, a_ref, b_ref, c_ref, *rest):
        out_ref = rest[-1]
        out_ref[...] = ((o_ref[...] + a_ref[...].astype(F32)) + b_ref[...].astype(F32)) + c_ref[...].astype(F32)

    grid_spec = pltpu.PrefetchScalarGridSpec(
        num_scalar_prefetch=1, grid=(nb,),
        in_specs=[pl.BlockSpec((tr, cols), lambda i, idx: (i, 0))] +
                 [pl.BlockSpec((None, tr, cols), lambda i, idx, j=j: (j, i, 0)) for j in range(N_CHIPS - 1)] +
                 ([] if into is None else [pl.BlockSpec(memory_space=pl.ANY)]),
        out_specs=pl.BlockSpec((None, tr, cols), lambda i, idx: (layer, idx[0] * nb + i, 0)))
    args = [jnp.reshape(core, (1,)).astype(jnp.int32), own, others, others, others] + ([] if into is None else [into])
    return pl.pallas_call(
        kern, name=name, grid_spec=grid_spec,
        out_shape=jax.ShapeDtypeStruct((nlayers, 2 * half, cols), F32),
        input_output_aliases={} if into is None else {len(args) - 1: 0},
        compiler_params=_params(("parallel",)),
    )(*args)


def small_update(gath, w, m, v, name):
    def fn(*vs):
        g = vs[0]
        for t in vs[1:N_DEV]:
            g = g + t
        wv, mv, vv = vs[N_DEV:]
        m2 = ADAM_B1 * mv + (1.0 - ADAM_B1) * g
        v2 = ADAM_B2 * vv + (1.0 - ADAM_B2) * (g * g)
        m_hat = m2 / (1.0 - ADAM_B1 ** ADAM_STEP)
        v_hat = v2 / (1.0 - ADAM_B2 ** ADAM_STEP)
        return g, -ADAM_LR * (m_hat / (jnp.sqrt(v_hat) + ADAM_EPS) + ADAM_WD * wv), m2, v2

    c = w.shape[1]
    ins = [(gath[k], "row") for k in range(N_DEV)] + [(w, "row"), (m, "row"), (v, "row")]
    return rowwise(fn, ins, [(c, F32)] * 4, tr=w.shape[0] // 2, name=name)


_MIX = {0: [("sb_w_qkv", 0), ("sb_w_o", 0)], 1: [("gm_w_in", 0), ("gm_w_out", 0)],
        2: [("ssm_w_in", 0), ("ssm_w_out", 0)], 3: [("sb_w_qkv", 1), ("sb_w_o", 1)]}
_FFN = {i: [("ffn_w_gu", i), ("ffn_w_down", i)] for i in range(4)}
GATHER_FIRST = _MIX[0][:1]
GATHER_AT = {"sb_attn_0": _MIX[0][1:] + _FFN[0] + _FFN[1],
             "ffn_gu_0": _MIX[1], "ffn_down_0": _MIX[2][1:], "ffn_gu_1": _MIX[2][:1], "ffn_down_1": _FFN[2][1:],
             "ssm_scan_2": _FFN[2][:1] + _MIX[3] + _FFN[3][1:], "ffn_gu_2": _FFN[3][:1]}
SCATTER_AT = {"ssm_dscan_2": _FFN[3] + _MIX[3] + _FFN[2], "sb_dattn_0": _MIX[2] + _FFN[1] + _MIX[1] + _FFN[0]}
SCATTER_LAST = _MIX[0]


class _Plan:
    def __init__(self, ins, core, chip, vectors):
        self.core, self.chip = core, chip
        self.staged = {(n, l): stage_shard(ins[n], l, chip, f"stage_{n}_{l}")
                       for n in MATRICES for l in range(ins[n].shape[0])}
        self.full = {n: [None] * ins[n].shape[0] for n in MATRICES}
        self.ready = {}
        self.parts = {}
        self.halves = {}
        self.layers = {n: ins[n].shape[0] for n in MATRICES}
        self.swaps = 0
        first = [self.staged[u] for u in GATHER_FIRST] + [stage_shard(vectors, 0, chip, "stage_vectors", F32)]
        *gathered, self.vectors = run_side(gather_side(first), "gather_first")
        self._fill(GATHER_FIRST, gathered)

    def _fill(self, units, gathered):
        for (n, l), g in zip(units, gathered):
            if n in COLUMN_BLOCKS:
                self.full[n][l] = g
            elif n == "ssm_w_in":
                self.full[n][l] = jnp.concatenate([g[k] for k in range(N_CHIPS)], axis=1)
            else:
                self.full[n][l] = g.reshape(-1, g.shape[-1])

    def _prepare(self, units):
        gps = [self.ready[u] for u in units]
        theirs = swap_halves(gps, f"swap_halves_{self.swaps}")
        self.swaps += 1
        for (n, l), g, t in zip(units, gps, theirs):
            self.parts[(n, l)] = sum_cores(g, t, self.core, self.chip, f"sum_cores_{n}_{l}")

    def _reduce(self, units, others):
        for (n, l), other in zip(units, others):
            self.halves[n] = sum_chips(self.parts[(n, l)][0], other, self.core, l, self.layers[n], self.halves.get(n),
                                       f"sum_chips_{n}_{l}")

    def side(self, tag):
        if tag in GATHER_AT:
            return gather_side([self.staged[u] for u in GATHER_AT[tag]])
        if tag in SCATTER_AT:
            self._prepare(SCATTER_AT[tag])
            return scatter_side([self.parts[u][1] for u in SCATTER_AT[tag]])
        return None

    def done(self, tag, results):
        if tag in GATHER_AT:
            self._fill(GATHER_AT[tag], results)
        else:
            self._reduce(SCATTER_AT[tag], results)

    def grads_ready(self, grads):
        for (n, l), g in grads.items():
            if n in COLUMN_BLOCKS:
                self.ready[(n, l)] = g
            elif n == "ssm_w_in":
                self.ready[(n, l)] = jnp.stack(jnp.split(g, N_CHIPS, axis=1))
            else:
                self.ready[(n, l)] = g.reshape(N_CHIPS, -1, g.shape[-1])

    def shard_grads(self):
        self._prepare(SCATTER_LAST)
        self._reduce(SCATTER_LAST, run_side(scatter_side([self.parts[u][1] for u in SCATTER_LAST]), "scatter_last"))
        names = sorted(self.halves)
        return dict(zip(names, join_halves([self.halves[n] for n in names])))


def _step(ins):
    x, target = ins["x"][0], ins["loss_target"][0]
    core = lax.axis_index("c")
    chip = 2 * lax.axis_index("x") + lax.axis_index("y")

    def lane_pad(v):
        return jnp.pad(v, ((0, 0), (0, PACK_COLS - v.shape[1])))

    vec_rows = [ins["ssm_conv_w"][0], ins["ssm_conv_b"], lane_pad(ins["ssm_norm_gain"])]
    blk = jnp.concatenate(vec_rows + [jnp.zeros((2 * SUBLANES - 6, PACK_COLS), F32)], axis=0)
    plan = _Plan(ins, core, chip, blk[None])
    per_chip = plan.vectors
    ngw = ins["ssm_norm_gain"].shape[1]
    full = {
        "ssm_conv_w": jnp.concatenate([per_chip[k, 0:4] for k in range(N_CHIPS)], axis=1)[None],
        "ssm_conv_b": jnp.concatenate([per_chip[k, 4:5] for k in range(N_CHIPS)], axis=1),
        "ssm_norm_gain": jnp.concatenate([per_chip[k, 5:6, :ngw] for k in range(N_CHIPS)], axis=1),
    }

    full.update(plan.full)
    for n in REPLICATED:
        full[n] = ins[n]

    loss, dx, grads = local_step(x, target, full, plan)
    loss = lax.psum(loss, ALL_AXES)
    gshards = plan.shard_grads()

    small_shapes = [ins[n].shape for n in REPLICATED]
    vec_shapes = [grads[n].shape for n in EXACT]
    vec_pack = _pack([grads[n] for n in EXACT], F32, SUBLANES)
    gath = gather_small(jnp.concatenate([_pack([grads[n] for n in REPLICATED], F32, SUBLANES), vec_pack], axis=0),
                        "gather_small")
    packed = [jnp.concatenate([_pack([ins[pre + n] for n in REPLICATED], F32, SUBLANES), jnp.zeros_like(vec_pack)], axis=0)
              for pre in ("", "m_", "v_")]
    res = small_update(gath, *packed, name="small_update")
    nrep = res[0].shape[0] - vec_pack.shape[0]
    small = [dict(zip(REPLICATED, _unpack(r[:nrep].reshape(-1), small_shapes))) for r in res]
    vec_g = dict(zip(EXACT, _unpack(res[0][nrep:].reshape(-1), vec_shapes)))

    out_g, out_d, out_m, out_v = {}, {}, {}, {}
    for n in REPLICATED:
        out_g[n], out_d[n], out_m[n], out_v[n] = (s[n] for s in small)
    for n in SHARDED:
        shp = ins[n].shape
        if n in EXACT:
            g = lax.dynamic_slice_in_dim(vec_g[n], chip * shp[-1], shp[-1], axis=vec_g[n].ndim - 1)
        else:
            g = gshards[n]
        two = (math.prod(shp[:-1]), shp[-1])
        d2, m2, v2, g2 = adamw(ins[n].reshape(two), g.reshape(two), ins["m_" + n].reshape(two),
                               ins["v_" + n].reshape(two), f"adamw_{n}")
        out_g[n], out_d[n], out_m[n], out_v[n] = g2.reshape(shp), d2.reshape(shp), m2.reshape(shp), v2.reshape(shp)
    return (loss, dx[None], *[out_g[n] for n in WEIGHTS], *[out_d[n] for n in WEIGHTS],
            *[out_m[n] for n in WEIGHTS], *[out_v[n] for n in WEIGHTS])


def kernel(x, mix_norm, ffn_norm, sb_w_qkv, sb_q_gain, sb_k_gain, sb_w_o, gm_w_in, gm_b_in, gm_v_gain, gm_w_s, gm_b_s, gm_w_out, ssm_w_in, ssm_conv_w, ssm_conv_b, ssm_dt_bias, ssm_a_log, ssm_d, ssm_norm_gain, ssm_w_out, ffn_w_gu, ffn_w_down, loss_target, m_mix_norm, m_ffn_norm, m_sb_w_qkv, m_sb_q_gain, m_sb_k_gain, m_sb_w_o, m_gm_w_in, m_gm_b_in, m_gm_v_gain, m_gm_w_s, m_gm_b_s, m_gm_w_out, m_ssm_w_in, m_ssm_conv_w, m_ssm_conv_b, m_ssm_dt_bias, m_ssm_a_log, m_ssm_d, m_ssm_norm_gain, m_ssm_w_out, m_ffn_w_gu, m_ffn_w_down, v_mix_norm, v_ffn_norm, v_sb_w_qkv, v_sb_q_gain, v_sb_k_gain, v_sb_w_o, v_gm_w_in, v_gm_b_in, v_gm_v_gain, v_gm_w_s, v_gm_b_s, v_gm_w_out, v_ssm_w_in, v_ssm_conv_w, v_ssm_conv_b, v_ssm_dt_bias, v_ssm_a_log, v_ssm_d, v_ssm_norm_gain, v_ssm_w_out, v_ffn_w_gu, v_ffn_w_down):
    return _step(dict(locals()))
```

```python
import functools
import math

import jax
import jax.numpy as jnp
from jax import lax
from jax.experimental import pallas as pl
from jax.experimental.pallas import tpu as pltpu

F32 = jnp.float32
BF16 = jnp.bfloat16
EPS = 1e-6
LANES = 128
SUBLANES = 8
VMEM_LIMIT = 56 * 1024 * 1024
HEAD = 64
CHUNK = 128
SB_TQ, SB_TK = 256, 256
SSD_SUB = 8
SB_DEAD = -110.0
SB_UNSEEN = -1e30
ADAM_LR, ADAM_B1, ADAM_B2, ADAM_EPS, ADAM_WD, ADAM_STEP = 0.001, 0.9, 0.999, 1e-08, 0.01, 10
MESH_ID = pl.DeviceIdType.MESH
ALL_AXES = ("x", "y", "c")


def _params(sem):
    return pltpu.CompilerParams(dimension_semantics=sem, vmem_limit_bytes=VMEM_LIMIT)


def _pick(n, cands):
    for c in cands:
        if n % c == 0:
            return c
    return n


def _dot(a, b, dims=((1,), (0,))):
    return lax.dot_general(a, b, (dims, ((), ())), preferred_element_type=F32)


def _dot_nt(a, b):
    return _dot(a, b, ((1,), (1,)))


def _dot_tn(a, b):
    return _dot(a, b, ((0,), (0,)))


def _split2(x):
    hi = x.astype(BF16)
    lo = (x - hi.astype(F32)).astype(BF16)
    return hi, lo


def _dot_x2(x, m):
    hi, lo = _split2(x)
    return _dot(hi, m) + _dot(lo, m)


def _dot_x3_left(m, x):
    h1 = x.astype(BF16)
    r1 = x - h1.astype(F32)
    h2 = r1.astype(BF16)
    h3 = (r1 - h2.astype(F32)).astype(BF16)
    return _dot(m, h1) + _dot(m, h2) + _dot(m, h3)


def _sigmoid(x):
    return 1.0 / (1.0 + jnp.exp(-x))


def _softplus(x):
    return jnp.maximum(x, 0.0) + jnp.log(1.0 + jnp.exp(-jnp.abs(x)))


def _colsum(x):
    return jnp.sum(x, axis=0, keepdims=True)


def _rowsum(x):
    return jnp.sum(x, axis=1, keepdims=True)


def _iota2(shape, dim):
    return lax.broadcasted_iota(jnp.int32, shape, dim)


MM_VMEM_BUDGET = 40 * 1024 * 1024
MM_STEP_US = 0.35
MM_HBM_BYTES_PER_US = 3.0e6
MM_VMEM_BYTES_PER_US = 1.5e6
MM_FLOPS_PER_US = 9.0e8
MXU_DIM = 256


def _mm_tiles(m, n, kk, wn, wk, a_bytes, b_bytes, has_add):
    def divisors(total, cands):
        got = [c for c in cands if total % c == 0 and c <= total]
        return got or [total]

    best = None
    for tm in divisors(m, (1024, 512, 256, 128)):
        for tn in divisors(wn, (1024, 768, 1408, 512, 256, 128)):
            for tk in divisors(wk, (4096, 2816, 2048, 1408, 1024, 768, 512, 256, 128)):
                nk = kk // tk
                vmem = 2 * (tm * tk * a_bytes + tk * tn * b_bytes + tm * tn * 4 * (2 if has_add else 1))
                vmem += tm * tn * 4 if nk > 1 else 0
                if vmem > MM_VMEM_BUDGET:
                    continue
                steps = (m // tm) * (n // tn) * nk
                a_reads = 1 if nk == 1 else n // tn
                traffic = m * kk * a_bytes * a_reads + kk * n * b_bytes * (m // tm) + m * n * 4
                fill = min(1.0, tn / MXU_DIM) * min(1.0, tm / MXU_DIM)
                compute = 2.0 * m * n * kk / (MM_FLOPS_PER_US * fill)
                cost = steps * MM_STEP_US + max(compute, traffic / MM_HBM_BYTES_PER_US)
                if nk > 1:
                    cost += steps * tm * tn * 8 / MM_VMEM_BYTES_PER_US
                if best is None or cost < best[0]:
                    best = (cost, tm, tn, tk)
    return best[1:]


def mm(a, b, *, ta=False, tb=False, add=None, bias=None, a_chunks=False, b_chunks=False, out_chunks=False,
       out_dtype=F32, name, side=None):
    wa = None
    if a_chunks:
        m, wa = a.shape[1], a.shape[2]
        kk = a.shape[0] * wa
    elif ta:
        kk, m = a.shape
    else:
        m, kk = a.shape
    nch, wide = 1, None
    if b_chunks:
        nch, rows_b, wide = b.shape
        kb, n = (rows_b, nch * wide) if not tb else (nch * wide, rows_b)
    elif tb:
        n, kb = b.shape
    else:
        kb, n = b.shape
    wide_o = n // N_CHIPS if out_chunks else None
    assert kk == kb, (a.shape, b.shape, ta, tb)
    has_add, has_bias = add is not None, bias is not None
    wk = wide if (wide and tb) else kk
    wn = wide if (wide and not tb) else n
    tm, tn, tk = _mm_tiles(m, n, kk, math.gcd(wn, wide_o) if wide_o else wn, math.gcd(wk, wa) if wa else wk,
                           a.dtype.itemsize, b.dtype.itemsize, has_add)
    nk = kk // tk
    dims = ((0 if ta else 1,), (1 if tb else 0,))

    def kern(*refs):
        a_ref, b_ref = refs[0], refs[1]
        rest = list(refs[2:])
        add_ref = rest.pop(0) if has_add else None
        bias_ref = rest.pop(0) if has_bias else None
        o_ref = rest[0]
        part = _dot(a_ref[...].astype(BF16), b_ref[...].astype(BF16), dims)

        def finish(r):
            if has_add:
                r = r + add_ref[...]
            if has_bias:
                r = r + bias_ref[...]
            o_ref[...] = r.astype(out_dtype)

        if nk == 1:
            finish(part)
        else:
            acc_ref = rest[1]
            k = pl.program_id(2)

            @pl.when(k == 0)
            def _():
                acc_ref[...] = part

            @pl.when((k > 0) & (k < nk - 1))
            def _():
                acc_ref[...] += part

            @pl.when(k == nk - 1)
            def _():
                finish(acc_ref[...] + part)

    if a_chunks:
        per_a = wa // tk
        a_spec = pl.BlockSpec((None, tm, tk), lambda i, j, k: (k // per_a, i, k % per_a))
    elif ta:
        a_spec = pl.BlockSpec((tk, tm), lambda i, j, k: (k, i))
    else:
        a_spec = pl.BlockSpec((tm, tk), lambda i, j, k: (i, k))
    if b_chunks and tb:
        per = wide // tk
        b_spec = pl.BlockSpec((None, tn, tk), lambda i, j, k: (k // per, j, k % per))
    elif b_chunks:
        per = wide // tn
        b_spec = pl.BlockSpec((None, tk, tn), lambda i, j, k: (j // per, k, j % per))
    elif tb:
        b_spec = pl.BlockSpec((tn, tk), lambda i, j, k: (j, k))
    else:
        b_spec = pl.BlockSpec((tk, tn), lambda i, j, k: (k, j))
    if out_chunks:
        per_o = wide_o // tn
        out_spec = pl.BlockSpec((None, tm, tn), lambda i, j, k: (j // per_o, i, j % per_o))
        out_shape = jax.ShapeDtypeStruct((N_CHIPS, m, wide_o), out_dtype)
    else:
        out_spec = pl.BlockSpec((tm, tn), lambda i, j, k: (i, j))
        out_shape = jax.ShapeDtypeStruct((m, n), out_dtype)
    in_specs, args = [a_spec, b_spec], [a, b]
    if has_add:
        in_specs.append(pl.BlockSpec((tm, tn), lambda i, j, k: (i, j)))
        args.append(add)
    if has_bias:
        in_specs.append(pl.BlockSpec((1, tn), lambda i, j, k: (0, j)))
        args.append(bias)
    (out,), side_outs = side_call(
        kern, side,
        name=name,
        grid=(m // tm, n // tn, nk),
        in_specs=in_specs,
        out_specs=[out_spec],
        out_shape=[out_shape],
        scratch_shapes=[pltpu.VMEM((tm, tn), F32)] if nk > 1 else [],
        args=args)
    return out if side is None else (out, side_outs)


def mm_hooked(plan, a, b, *, name, **kw):
    side = plan.side(name)
    if side is None:
        return mm(a, b, name=name, **kw)
    out, side_outs = mm(a, b, name=name, side=side, **kw)
    plan.done(name, side_outs)
    return out


def rowwise(fn, ins, outs, accs=(), *, tr, name):
    rows = [a for a, kind in ins if kind == "row"][0].shape[0]
    tr = min(tr, rows)
    assert rows % tr == 0 and tr % SUBLANES == 0, (rows, tr)
    n = rows // tr
    n_in, n_out = len(ins), len(outs)
    kinds = [kind for _, kind in ins]

    def kern(*refs):
        i = pl.program_id(0)
        vals = []
        for ref, kind in zip(refs[:n_in], kinds):
            v = ref[...]
            if kind == "prev":
                v = v * (i > 0).astype(v.dtype)
            elif kind == "next":
                v = v * (i < n - 1).astype(v.dtype)
            vals.append(v)
        res = fn(*vals)
        for ref, r in zip(refs[n_in:n_in + n_out], res[:n_out]):
            ref[...] = r.astype(ref.dtype)
        if accs:
            acc_refs = refs[n_in + n_out:]

            @pl.when(i == 0)
            def _():
                for ref in acc_refs:
                    ref[...] = jnp.zeros_like(ref)

            for ref, r in zip(acc_refs, res[n_out:]):
                ref[...] += r

    in_specs = []
    for a, kind in ins:
        if kind == "row":
            in_specs.append(pl.BlockSpec((tr, a.shape[1]), lambda i: (i, 0)))
        elif kind == "full":
            in_specs.append(pl.BlockSpec(a.shape, lambda i, nd=a.ndim: (0,) * nd))
        elif kind == "prev":
            in_specs.append(pl.BlockSpec((SUBLANES, a.shape[1]),
                                         lambda i: (jnp.maximum(i * (tr // SUBLANES) - 1, 0), 0)))
        else:
            in_specs.append(pl.BlockSpec((SUBLANES, a.shape[1]),
                                         lambda i: (jnp.minimum((i + 1) * (tr // SUBLANES), rows // SUBLANES - 1), 0)))
    out_specs = [pl.BlockSpec((tr, c), lambda i: (i, 0)) for c, _ in outs]
    out_specs += [pl.BlockSpec((r, c), lambda i: (0, 0)) for r, c in accs]
    out_shape = [jax.ShapeDtypeStruct((rows, c), dt) for c, dt in outs]
    out_shape += [jax.ShapeDtypeStruct((r, c), F32) for r, c in accs]
    res = pl.pallas_call(
        kern,
        name=name,
        grid=(n,),
        in_specs=in_specs,
        out_specs=out_specs,
        out_shape=out_shape,
        compiler_params=_params(("arbitrary",) if accs else ("parallel",)),
    )(*[a for a, _ in ins])
    return res


def rms_fwd(x, g, name):
    def fn(xv, gv):
        r = lax.rsqrt(jnp.mean(xv * xv, axis=1, keepdims=True) + EPS)
        return (xv * r * gv,)

    return rowwise(fn, [(x, "row"), (g, "full")], [(x.shape[1], BF16)], tr=1024, name=name)[0]


def rms_bwd(x, g, dy, dres, name):
    def fn(xv, gv, dyv, drv):
        r = lax.rsqrt(jnp.mean(xv * xv, axis=1, keepdims=True) + EPS)
        xh = xv * r
        dyg = dyv * gv
        dx = drv + r * (dyg - xh * jnp.mean(dyg * xh, axis=1, keepdims=True))
        return dx, dx, _colsum(dyv * xh)

    c = x.shape[1]
    dx, dxb, dg = rowwise(fn, [(x, "row"), (g, "full"), (dy, "row"), (dres, "row")], [(c, F32), (c, BF16)], [(1, c)],
                          tr=512, name=name)
    return (dx, dxb), dg


def ffn_up(h, wgu, name, side=None):
    s, d = h.shape
    nch, _, w = wgu.shape
    half = nch // 2
    tm = _pick(s, (512, 256, 128))

    def kern(h_ref, wg_ref, wu_ref, gu_ref, a_ref):
        hv = h_ref[...]
        g = _dot(hv, wg_ref[...])
        u = _dot(hv, wu_ref[...])
        gu_ref[0] = g.astype(BF16)
        gu_ref[1] = u.astype(BF16)
        a_ref[...] = (g * _sigmoid(g) * u).astype(BF16)

    return side_call(
        kern, side, name=name, grid=(s // tm, half),
        in_specs=[pl.BlockSpec((tm, d), lambda i, j: (i, 0)),
                  pl.BlockSpec((None, d, w), lambda i, j: (j, 0, 0)),
                  pl.BlockSpec((None, d, w), lambda i, j: (j + half, 0, 0))],
        out_specs=[pl.BlockSpec((2, tm, w), lambda i, j: (0, i, j)), pl.BlockSpec((tm, w), lambda i, j: (i, j))],
        out_shape=[jax.ShapeDtypeStruct((2, s, half * w), BF16), jax.ShapeDtypeStruct((s, half * w), BF16)],
        scratch_shapes=[], args=(h, wgu, wgu))


def ffn_dact(dxb, wdown, gu, name):
    s, d = dxb.shape
    hid = wdown.shape[0]
    tm = _pick(s, (512, 256, 128))
    tn = _pick(hid, (1408, 512, 256, 128))

    def kern(dx_ref, w_ref, gu_ref, o_ref):
        da = _dot_nt(dx_ref[...], w_ref[...])
        g, u = gu_ref[0].astype(F32), gu_ref[1].astype(F32)
        sg = _sigmoid(g)
        o_ref[0] = (da * u * sg * (1.0 + g * (1.0 - sg))).astype(BF16)
        o_ref[1] = (da * g * sg).astype(BF16)

    return pl.pallas_call(
        kern, name=name, grid=(s // tm, hid // tn),
        in_specs=[pl.BlockSpec((tm, d), lambda i, j: (i, 0)), pl.BlockSpec((tn, d), lambda i, j: (j, 0)),
                  pl.BlockSpec((2, tm, tn), lambda i, j: (0, i, j))],
        out_specs=pl.BlockSpec((2, tm, tn), lambda i, j: (0, i, j)),
        out_shape=jax.ShapeDtypeStruct((2, s, hid), BF16),
        compiler_params=_params(("parallel", "parallel")),
    )(dxb, wdown, gu)


def loss_and_grad(y, t, name):
    d = y.shape[1]

    def fn(yv, tv):
        e = yv - tv
        part = jnp.sum(_colsum(e * e), axis=1, keepdims=True) * (0.5 / d)
        dy = e * (1.0 / d)
        return dy, dy, jnp.broadcast_to(part, (SUBLANES, LANES))

    dy, dyb, acc = rowwise(fn, [(y, "row"), (t, "row")], [(d, F32), (d, BF16)], [(SUBLANES, LANES)], tr=1024, name=name)
    return acc[0, 0], (dy, dyb)


def adamw(w, g, m, v, name):
    def fn(wv, gv, mv, vv):
        m2 = ADAM_B1 * mv + (1.0 - ADAM_B1) * gv
        v2 = ADAM_B2 * vv + (1.0 - ADAM_B2) * (gv * gv)
        m_hat = m2 / (1.0 - ADAM_B1 ** ADAM_STEP)
        v_hat = v2 / (1.0 - ADAM_B2 ** ADAM_STEP)
        delta = -ADAM_LR * (m_hat / (jnp.sqrt(v_hat) + ADAM_EPS) + ADAM_WD * wv)
        return delta, m2, v2, gv

    rows, c = w.shape
    tr = _pick(rows, (512, 256, 128, 64, 32, 16, 8)) if rows % SUBLANES == 0 else rows
    if rows % SUBLANES:
        return _whole(fn, [w, g, m, v], [(w.shape, F32)] * 4, name=name)
    if 2 * 8 * tr * c * 4 > MM_VMEM_BUDGET:
        tr //= 2
    return rowwise(fn, [(w, "row"), (g, "row"), (m, "row"), (v, "row")], [(c, F32)] * 4, tr=tr, name=name)


def _whole(fn, ins, outs, *, name):
    n_in = len(ins)

    def kern(*refs):
        res = fn(*[r[...] for r in refs[:n_in]])
        for ref, r in zip(refs[n_in:], res):
            ref[...] = r.astype(ref.dtype)

    return pl.pallas_call(
        kern,
        name=name,
        out_shape=[jax.ShapeDtypeStruct(s, dt) for s, dt in outs],
        compiler_params=pltpu.CompilerParams(vmem_limit_bytes=VMEM_LIMIT),
    )(*ins)


def ffn_fwd(x, g, wgu, wdown, tag, plan):
    h = rms_fwd(x, g, f"ffn_rms_{tag}")
    gu, a = _hooked(plan, f"ffn_gu_{tag}", ffn_up, h, wgu)
    xn = mm_hooked(plan, a, wdown, add=x, name=f"ffn_down_{tag}")
    return xn, (x, h, gu, a)


def ffn_bwd(dxn, saved, g, wgu, wdown, tag):
    x, h, gu, a = saved
    dxn, dxb = dxn
    dwdown = mm(a, dxb, ta=True, out_dtype=BF16, name=f"ffn_dwdown_{tag}")
    dgu = ffn_dact(dxb, wdown, gu, f"ffn_dact_{tag}")
    dh = mm(dgu, wgu, tb=True, a_chunks=True, b_chunks=True, name=f"ffn_dh_{tag}")
    dwgu = mm(h, dgu, ta=True, b_chunks=True, out_dtype=BF16, out_chunks=True, name=f"ffn_dwgu_{tag}")
    dx, dg = rms_bwd(x, g, dh, dxn, f"ffn_drms_{tag}")
    return dx, dg, dwgu, dwdown


def _head_blockdiag(c):
    i = jnp.arange(c) // HEAD
    return (i[:, None] == i[None, :]).astype(BF16)


def _head_sums(x, bd):
    return jnp.concatenate([_dot_x2(x[:, g * LANES:(g + 1) * LANES], bd) for g in range(x.shape[1] // LANES)], axis=1)


def qknorm_fwd(qkv, qg, kg, bd, name):
    d = qkv.shape[1] // 3
    scale = 1.0 / math.sqrt(HEAD)

    def fn(v, qgv, kgv, bdv):
        v = v.astype(F32)
        q, k, vv = v[:, :d], v[:, d:2 * d], v[:, 2 * d:]
        rq = lax.rsqrt(_head_sums(q * q, bdv) * (1.0 / HEAD) + EPS)
        rk = lax.rsqrt(_head_sums(k * k, bdv) * (1.0 / HEAD) + EPS)
        return q * rq * qgv * scale, k * rk * kgv, vv

    return rowwise(fn, [(qkv, "row"), (qg, "full"), (kg, "full"), (bd, "full")],
                   [(d, BF16), (d, BF16), (d, BF16)], tr=512, name=name)


def qknorm_bwd(qkv, dqs, dkn, dv, qg, kg, bd, name):
    d = qkv.shape[1] // 3
    scale = 1.0 / math.sqrt(HEAD)

    def one(xv, gv, dyv, bdv):
        r = lax.rsqrt(_head_sums(xv * xv, bdv) * (1.0 / HEAD) + EPS)
        xh = xv * r
        dyg = dyv * gv
        dx = r * (dyg - xh * (_head_sums(dyg * xh, bdv) * (1.0 / HEAD)))
        return dx, _colsum(dyv * xh)

    def fn(v, dqv, dkv, dvv, qgv, kgv, bdv):
        v = v.astype(F32)
        q, k = v[:, :d], v[:, d:2 * d]
        dq, dqg = one(q, qgv, dqv * scale, bdv)
        dk, dkg = one(k, kgv, dkv, bdv)
        return jnp.concatenate([dq, dk, dvv], axis=1), dqg, dkg

    return rowwise(fn, [(qkv, "row"), (dqs, "row"), (dkn, "row"), (dv, "row"), (qg, "full"), (kg, "full"), (bd, "full")],
                   [(3 * d, BF16)], [(1, d), (1, d)], tr=512, name=name)


def _sb_tile(qh, k, mask, tri_gt):
    z = _dot_nt(qh, k)
    sp = jnp.log(1.0 + jnp.exp(-jnp.abs(z)))
    lb = jnp.minimum(z, 0.0) - sp
    l1 = jnp.where(mask, lb - z, 0.0)
    suf = _dot(l1.astype(BF16), tri_gt)
    return lb, l1, suf


def _sb_tri(tk):
    i = jnp.arange(tk)
    return jnp.stack([i[:, None] > i[None, :], i[:, None] < i[None, :]]).astype(BF16)


def _sb_setup(tq, tk):
    row, col = _iota2((tq, tk), 0), _iota2((tq, tk), 1)
    lane = _iota2((1, LANES), 1)
    halves = [(lane < HEAD).astype(BF16), (lane >= HEAD).astype(BF16)]
    lane_q = _iota2((tq, LANES), 1) + jnp.minimum(_iota2((tq, LANES), 0), 0)
    return row, col, halves, lane_q


def sb_attn_fwd(qs, kn, vb, tri, name, side=None):
    s, d = qs.shape
    tq, tk = min(SB_TQ, s), min(SB_TK, s)
    nq = s // tq
    assert s // tk <= LANES and s % tq == 0 and s % tk == 0

    def kern(q_ref, k_ref, v_ref, tri_ref, o_ref, rs_ref, acc_ref):
        i = pl.program_id(1)
        row, col, halves, lane_q = _sb_setup(tq, tk)
        q = q_ref[...]
        qh = [q * hm for hm in halves]
        acc_ref[...] = jnp.zeros_like(acc_ref)
        rs_ref[...] = jnp.full(rs_ref.shape, SB_UNSEEN, F32)
        nkb = (i + 1) * (tq // tk)

        def more(st):
            return (st[0] < nkb) & (st[1] > SB_DEAD)

        def step(st):
            n, r = st[0], list(st[2:])
            kb = nkb - 1 - n
            ks = pl.multiple_of(kb * tk, tk)
            k = k_ref[pl.ds(ks, tk), :]
            v = v_ref[pl.ds(ks, tk), :]
            mask = col < row + (i * tq - kb * tk)
            at_kb = lane_q == kb
            for hh in range(2):
                lb, l1, suf = _sb_tile(qh[hh], k, mask, tri_ref[0])
                w = jnp.where(mask, jnp.exp(lb + suf + r[hh]), 0.0)
                acc_ref[...] += _dot(w.astype(BF16), v * halves[hh])
                rs_ref[hh] = jnp.where(at_kb, r[hh], rs_ref[hh])
                r[hh] = r[hh] + _rowsum(l1)
            return (n + 1, jnp.maximum(jnp.max(r[0]), jnp.max(r[1])), r[0], r[1])

        z1 = jnp.zeros((tq, 1), F32)
        lax.while_loop(more, step, (jnp.int32(0), jnp.float32(0.0), z1, z1))
        o_ref[...] = acc_ref[...].astype(BF16)

    nh2 = d // LANES
    return side_call(
        kern, side,
        name=name,
        grid=(nh2, nq),
        in_specs=[pl.BlockSpec((tq, LANES), lambda h, i: (i, h)),
                  pl.BlockSpec((s, LANES), lambda h, i: (0, h)),
                  pl.BlockSpec((s, LANES), lambda h, i: (0, h)),
                  pl.BlockSpec((2, tk, tk), lambda h, i: (0, 0, 0))],
        out_specs=[pl.BlockSpec((tq, LANES), lambda h, i: (i, h)),
                   pl.BlockSpec((None, 2, tq, LANES), lambda h, i: (h, 0, i, 0))],
        out_shape=[jax.ShapeDtypeStruct((s, d), BF16), jax.ShapeDtypeStruct((nh2, 2, s, LANES), F32)],
        scratch_shapes=[pltpu.VMEM((tq, LANES), F32)],
        args=(qs, kn, vb, tri))


def sb_attn_bwd(qs, kn, vb, rsave, do, tri, name, side=None):
    s, d = qs.shape
    tq, tk = min(SB_TQ, s), min(SB_TK, s)
    nq = s // tq

    def kern(q_ref, k_ref, v_ref, rs_ref, do_ref, tri_ref, dq_ref, dk_ref, dv_ref):
        i = pl.program_id(1)

        @pl.when(i == 0)
        def _():
            dk_ref[...] = jnp.zeros_like(dk_ref)
            dv_ref[...] = jnp.zeros_like(dv_ref)

        row, col, halves, lane_q = _sb_setup(tq, tk)
        q = q_ref[...]
        qh = [q * hm for hm in halves]
        dov = do_ref[...].astype(BF16)
        doh = [dov * hm for hm in halves]
        dq_ref[...] = jnp.zeros_like(dq_ref)
        nkb = (i + 1) * (tq // tk)
        top = jnp.maximum(jnp.max(rs_ref[0], axis=0, keepdims=True), jnp.max(rs_ref[1], axis=0, keepdims=True))
        dead = (top <= SB_DEAD) & (_iota2((1, LANES), 1) < nkb)
        kstart = jnp.minimum(jnp.sum(dead.astype(F32)).astype(jnp.int32), nkb)

        def step(kb, ep):
            ep = list(ep)
            ks = pl.multiple_of(kb * tk, tk)
            k = k_ref[pl.ds(ks, tk), :]
            v = v_ref[pl.ds(ks, tk), :]
            mask = col < row + (i * tq - kb * tk)
            at_kb = lane_q == kb
            for hh in range(2):
                lb, l1, suf = _sb_tile(qh[hh], k, mask, tri_ref[0])
                r = _rowsum(jnp.where(at_kb, rs_ref[hh], 0.0))
                lbm = jnp.where(mask, lb, SB_UNSEEN)
                w = jnp.exp(lbm + suf + r)
                e = _dot_nt(doh[hh], v) * w
                pe = ep[hh] + _dot(e.astype(BF16), tri_ref[1])
                beta = jnp.exp(lbm)
                dz = (e - beta * (e + pe)).astype(BF16)
                dq_ref[...] += _dot(dz, k * halves[hh])
                dk_ref[pl.ds(ks, tk), :] += _dot_tn(dz, qh[hh])
                dv_ref[pl.ds(ks, tk), :] += _dot_tn(w.astype(BF16), doh[hh])
                ep[hh] = ep[hh] + _rowsum(e)
            return tuple(ep)

        z1 = jnp.zeros((tq, 1), F32)
        lax.fori_loop(kstart, nkb, step, (z1, z1))

    nh2 = d // LANES
    return side_call(
        kern, side,
        name=name,
        grid=(nh2, nq),
        in_specs=[pl.BlockSpec((tq, LANES), lambda h, i: (i, h)),
                  pl.BlockSpec((s, LANES), lambda h, i: (0, h)),
                  pl.BlockSpec((s, LANES), lambda h, i: (0, h)),
                  pl.BlockSpec((None, 2, tq, LANES), lambda h, i: (h, 0, i, 0)),
                  pl.BlockSpec((tq, LANES), lambda h, i: (i, h)),
                  pl.BlockSpec((2, tk, tk), lambda h, i: (0, 0, 0))],
        out_specs=[pl.BlockSpec((tq, LANES), lambda h, i: (i, h)),
                   pl.BlockSpec((s, LANES), lambda h, i: (0, h)),
                   pl.BlockSpec((s, LANES), lambda h, i: (0, h))],
        out_shape=[jax.ShapeDtypeStruct((s, d), F32)] * 3,
        scratch_shapes=[],
        args=(qs, kn, vb, rsave, do, tri))


def _hooked(plan, tag, call, *args):
    side = plan.side(tag)
    outs, side_outs = call(*args, tag, side)
    if side is not None:
        plan.done(tag, side_outs)
    return outs


def sb_fwd(x, g, wqkv, qg, kg, wo, bd, tag, plan):
    h = rms_fwd(x, g, f"sb_rms_{tag}")
    qkv = mm(h, wqkv, b_chunks=True, out_dtype=BF16, name=f"sb_qkv_{tag}")
    qs, kn, vb = qknorm_fwd(qkv, qg, kg, bd, f"sb_qknorm_{tag}")
    o, rsave = _hooked(plan, f"sb_attn_{tag}", sb_attn_fwd, qs, kn, vb, _sb_tri(min(SB_TK, x.shape[0])))
    xn = mm(o, wo(), add=x, name=f"sb_out_{tag}")
    return xn, (x, h, qkv, qs, kn, vb, rsave, o)


def sb_bwd(dxn, saved, g, wqkv, qg, kg, wo, bd, tag, plan):
    x, h, qkv, qs, kn, vb, rsave, o = saved
    dxn, dxb = dxn
    do = mm(dxb, wo, tb=True, name=f"sb_do_{tag}")
    dwo = mm(o, dxb, ta=True, out_dtype=BF16, name=f"sb_dwo_{tag}")
    dqs, dkn, dv = _hooked(plan, f"sb_dattn_{tag}", sb_attn_bwd, qs, kn, vb, rsave, do, _sb_tri(min(SB_TK, x.shape[0])))
    dqkv, dqg, dkg = qknorm_bwd(qkv, dqs, dkn, dv, qg, kg, bd, f"sb_dqknorm_{tag}")
    dh = mm(dqkv, wqkv, tb=True, b_chunks=True, name=f"sb_dh_{tag}")
    dwqkv = mm(h, dqkv, ta=True, out_dtype=BF16, out_chunks=True, name=f"sb_dwqkv_{tag}")
    dx, dg = rms_bwd(x, g, dh, dxn, f"sb_drms_{tag}")
    nh = dqg.shape[1] // HEAD
    return dx, dg, dwqkv, dqg.reshape(nh, HEAD).sum(0), dkg.reshape(nh, HEAD).sum(0), dwo


def _gelu(x):
    return 0.5 * x * (1.0 + lax.erf(x * (1.0 / math.sqrt(2.0))))


def _gelu_grad(x):
    return 0.5 * (1.0 + lax.erf(x * (1.0 / math.sqrt(2.0)))) + x * jnp.exp(-0.5 * x * x) * (1.0 / math.sqrt(2.0 * math.pi))


def gm_act_fwd(pre, vg, name):
    half = pre.shape[1] // 2

    def fn(p, vgv):
        p = p.astype(F32)
        u = _gelu(p[:, :half])
        v = _gelu(p[:, half:])
        r = lax.rsqrt(jnp.mean(v * v, axis=1, keepdims=True) + EPS)
        return u, v * r * vgv

    return rowwise(fn, [(pre, "row"), (vg, "full")], [(half, F32), (half, BF16)], tr=512, name=name)


def gm_act_bwd(pre, du, dvn, vg, name):
    half = pre.shape[1] // 2

    def fn(p, duv, dvnv, vgv):
        p = p.astype(F32)
        pu, pv = p[:, :half], p[:, half:]
        v = _gelu(pv)
        r = lax.rsqrt(jnp.mean(v * v, axis=1, keepdims=True) + EPS)
        vh = v * r
        dyg = dvnv * vgv
        dv = r * (dyg - vh * jnp.mean(dyg * vh, axis=1, keepdims=True))
        dpre = jnp.concatenate([duv * _gelu_grad(pu), dv * _gelu_grad(pv)], axis=1)
        return dpre, _colsum(dvnv * vh), _colsum(dpre)

    return rowwise(fn, [(pre, "row"), (du, "row"), (dvn, "row"), (vg, "full")],
                   [(2 * half, BF16)], [(1, half), (1, 2 * half)], tr=256, name=name)


def gm_spatial_fwd(u, vn, wc, bst, name):
    s, c = u.shape
    t = CHUNK
    ng = c // LANES

    def kern(u_ref, v_ref, w_ref, b_ref, o_ref):
        for g in range(ng):
            sl = slice(g * LANES, (g + 1) * LANES)
            mixed = _dot(w_ref[g], v_ref[:, sl]) + b_ref[:, sl]
            o_ref[:, sl] = (u_ref[:, sl] * mixed).astype(BF16)

    return pl.pallas_call(
        kern,
        name=name,
        grid=(s // t,),
        in_specs=[pl.BlockSpec((t, c), lambda i: (i, 0)), pl.BlockSpec((t, c), lambda i: (i, 0)),
                  pl.BlockSpec(wc.shape, lambda i: (0, 0, 0)), pl.BlockSpec(bst.shape, lambda i: (0, 0))],
        out_specs=pl.BlockSpec((t, c), lambda i: (i, 0)),
        out_shape=jax.ShapeDtypeStruct((s, c), BF16),
        compiler_params=_params(("parallel",)),
    )(u, vn, wc, bst)


def gm_spatial_bwd(dgate, u, vn, wc, bst, name):
    s, c = u.shape
    t = CHUNK
    ng = c // LANES

    def kern(dg_ref, u_ref, v_ref, w_ref, b_ref, du_ref, dv_ref, dw_ref, db_ref):
        i = pl.program_id(0)

        @pl.when(i == 0)
        def _():
            dw_ref[...] = jnp.zeros_like(dw_ref)
            db_ref[...] = jnp.zeros_like(db_ref)

        for g in range(ng):
            sl = slice(g * LANES, (g + 1) * LANES)
            vg = v_ref[:, sl]
            dgv = dg_ref[:, sl]
            mixed = _dot(w_ref[g], vg) + b_ref[:, sl]
            du_ref[:, sl] = dgv * mixed
            dmix = dgv * u_ref[:, sl]
            dmb = dmix.astype(BF16)
            dv_ref[:, sl] = _dot_tn(w_ref[g], dmb)
            dw_ref[g] += _dot_nt(dmb, vg)
            db_ref[:, sl] += dmix

    return pl.pallas_call(
        kern,
        name=name,
        grid=(s // t,),
        in_specs=[pl.BlockSpec((t, c), lambda i: (i, 0))] * 3 +
                 [pl.BlockSpec(wc.shape, lambda i: (0, 0, 0)), pl.BlockSpec(bst.shape, lambda i: (0, 0))],
        out_specs=[pl.BlockSpec((t, c), lambda i: (i, 0)), pl.BlockSpec((t, c), lambda i: (i, 0)),
                   pl.BlockSpec(wc.shape, lambda i: (0, 0, 0)), pl.BlockSpec(bst.shape, lambda i: (0, 0))],
        out_shape=[jax.ShapeDtypeStruct((s, c), F32), jax.ShapeDtypeStruct((s, c), F32),
                   jax.ShapeDtypeStruct(wc.shape, F32), jax.ShapeDtypeStruct(bst.shape, F32)],
        compiler_params=_params(("arbitrary",)),
    )(dgate, u, vn, wc, bst)


def gm_fwd(x, g, w_in, b_in, vg, wc, bst, w_out, tag):
    h = rms_fwd(x, g, f"gm_rms_{tag}")
    pre = mm(h, w_in, bias=b_in, b_chunks=True, out_dtype=BF16, name=f"gm_in_{tag}")
    u, vn = gm_act_fwd(pre, vg, f"gm_act_{tag}")
    gate = gm_spatial_fwd(u, vn, wc, bst, f"gm_spatial_{tag}")
    xn = mm(gate, w_out, add=x, name=f"gm_out_{tag}")
    return xn, (x, h, pre, u, vn, gate)


def gm_bwd(dxn, saved, g, w_in, vg, wc, bst, w_out, tag):
    x, h, pre, u, vn, gate = saved
    dxn, dxb = dxn
    dgate = mm(dxb, w_out, tb=True, name=f"gm_dgate_{tag}")
    dwout = mm(gate, dxb, ta=True, out_dtype=BF16, name=f"gm_dwout_{tag}")
    du, dvn, dws, dbst = gm_spatial_bwd(dgate, u, vn, wc, bst, f"gm_dspatial_{tag}")
    dpre, dvg, dbin = gm_act_bwd(pre, du, dvn, vg, f"gm_dact_{tag}")
    dh = mm(dpre, w_in, tb=True, b_chunks=True, name=f"gm_dh_{tag}")
    dwin = mm(h, dpre, ta=True, out_dtype=BF16, out_chunks=True, name=f"gm_dwin_{tag}")
    dx, dg = rms_bwd(x, g, dh, dxn, f"gm_drms_{tag}")
    ng = wc.shape[0]
    dws = jnp.where(jnp.tril(jnp.ones((CHUNK, CHUNK), bool)), dws, 0.0)
    dbs = dbst.reshape(CHUNK, ng, LANES).sum(-1).T
    return dx, dg, dwin, dbin, dvg, dws, dbs, dwout


def _conv_taps(xv, prev):
    cat = jnp.concatenate([prev, xv], axis=0)
    return [pltpu.roll(cat, sh, 0)[SUBLANES:] for sh in (3, 2, 1)] + [xv]


def conv_fwd(xbc, ws, b, d_inner, name):
    c = xbc.shape[1]
    nst = (c - d_inner) // 2

    def fn(xv, prev, w0, w1, w2, w3, bv):
        taps = _conv_taps(xv, prev)
        pre = bv + w0 * taps[0] + w1 * taps[1] + w2 * taps[2] + w3 * taps[3]
        out = pre * _sigmoid(pre)
        return out[:, :d_inner], out[:, d_inner:d_inner + nst], out[:, d_inner + nst:]

    return rowwise(fn, [(xbc, "row"), (xbc, "prev")] + [(w, "full") for w in ws] + [(b, "full")],
                   [(d_inner, F32), (nst, F32), (nst, F32)], tr=512, name=name)


def conv_bwd_pre(xbc, ws, b, dxs_a, dxs_b, db_m, dc_m, name):
    c = xbc.shape[1]

    def fn(xv, prev, w0, w1, w2, w3, bv, da, db2, dbm, dcm):
        taps = _conv_taps(xv, prev)
        pre = bv + w0 * taps[0] + w1 * taps[1] + w2 * taps[2] + w3 * taps[3]
        sg = _sigmoid(pre)
        dout = jnp.concatenate([da + db2, dbm, dcm], axis=1)
        dpre = dout * sg * (1.0 + pre * (1.0 - sg))
        return (dpre,) + tuple(_colsum(dpre * tp) for tp in taps) + (_colsum(dpre),)

    return rowwise(fn, [(xbc, "row"), (xbc, "prev")] + [(w, "full") for w in ws] +
                   [(b, "full"), (dxs_a, "row"), (dxs_b, "row"), (db_m, "row"), (dc_m, "row")],
                   [(c, F32)], [(1, c)] * 5, tr=256, name=name)


def conv_bwd_in(dpre, ws, name):
    c = dpre.shape[1]

    def fn(dv, nxt, w0, w1, w2, w3):
        cat = jnp.concatenate([dv, nxt], axis=0)
        n = cat.shape[0]
        up = [pltpu.roll(cat, n - sh, 0)[:dv.shape[0]] for sh in (1, 2, 3)]
        return (w3 * dv + w2 * up[0] + w1 * up[1] + w0 * up[2],)

    return rowwise(fn, [(dpre, "row"), (dpre, "next")] + [(w, "full") for w in ws], [(c, BF16)], tr=512, name=name)[0]


def ssd_pre(dtr, bias, alog, name):
    def fn(d, bv, al, tri):
        dt = _softplus(d + bv)
        a = dt * (-jnp.exp(al))
        return dt, _dot_x3_left(tri, a)

    tri = jnp.tril(jnp.ones((CHUNK, CHUNK), BF16))
    return rowwise(fn, [(dtr, "row"), (bias, "full"), (alog, "full"), (tri, "full")],
                   [(LANES, F32), (LANES, F32)], tr=CHUNK, name=name)


def _ssd_layouts(v, ngroups, hpg):
    s = v.shape[0]
    col = v[:, :ngroups * hpg].T.reshape(ngroups, hpg, s, 1)
    return jnp.broadcast_to(col, (ngroups, hpg, s, LANES))


def _ssd_rowform(acum, ngroups, hpg):
    s = acum.shape[0]
    nc = s // CHUNK
    a = acum[:, :ngroups * hpg].reshape(nc, CHUNK, ngroups, hpg).transpose(2, 0, 3, 1)
    last = jnp.broadcast_to(a[..., CHUNK - 1:], a.shape)
    return jnp.concatenate([a, last], axis=2)


def ssd_chunk_fwd(xs, bm, cm, col_a, col_dt, rowf, name, side=None):
    s, d_inner = xs.shape
    ln = CHUNK
    nc = s // ln
    nsub = _pick(nc, (SSD_SUB, 2, 1))
    rows = nsub * ln
    ng, hpg = col_a.shape[0], col_a.shape[1]
    gw = d_inner // ng
    assert gw == hpg * HEAD and gw % LANES == 0 and bm.shape[1] == ng * LANES

    def kern(x_ref, b_ref, c_ref, ca_ref, cd_ref, rf_ref, y_ref, hp_ref, h_scr):
        @pl.when(pl.program_id(1) == 0)
        def _():
            h_scr[...] = jnp.zeros_like(h_scr)

        causal = _iota2((ln, ln), 0) >= _iota2((ln, ln), 1)
        lane = _iota2((1, LANES), 1)
        for sc in range(nsub):
            rs = slice(sc * ln, (sc + 1) * ln)
            bb = b_ref[rs, :].astype(BF16)
            cbf = c_ref[rs, :].astype(BF16)
            cb = _dot_nt(cbf, bb)
            ys = [jnp.zeros((ln, LANES), F32) for _ in range(gw // LANES)]
            for r in range(hpg):
                j, hf = divmod(r, LANES // HEAD)
                mh = ((lane >= HEAD * hf) & (lane < HEAD * (hf + 1))).astype(F32)
                ac = ca_ref[r, rs, :]
                ar = rf_ref[sc, pl.ds(r, 1), :]
                aend = rf_ref[sc, pl.ds(4 + r, 1), :]
                dm = jnp.exp(jnp.minimum(ac - ar, 0.0))
                m = jnp.where(causal, cb * dm, 0.0).astype(BF16)
                xdt = x_ref[rs, j * LANES:(j + 1) * LANES] * cd_ref[r, rs, :] * mh
                h = h_scr[r]
                hp_ref[sc, r] = h.astype(BF16)
                ys[j] = ys[j] + _dot(m, xdt.astype(BF16)) + _dot_nt(cbf, h.astype(BF16)) * jnp.exp(ac)
                dte = jnp.exp(aend - ac)
                h_scr[r] = jnp.exp(aend) * h + _dot_tn((xdt * dte).astype(BF16), bb)
            for j in range(gw // LANES):
                y_ref[rs, j * LANES:(j + 1) * LANES] = ys[j]

    colspec = pl.BlockSpec((None, hpg, rows, LANES), lambda g, c: (g, 0, c, 0))
    return side_call(
        kern, side,
        name=name,
        grid=(ng, nc // nsub),
        in_specs=[pl.BlockSpec((rows, gw), lambda g, c: (c, g)),
                  pl.BlockSpec((rows, LANES), lambda g, c: (c, g)),
                  pl.BlockSpec((rows, LANES), lambda g, c: (c, g)),
                  colspec, colspec,
                  pl.BlockSpec((None, nsub, 8, LANES), lambda g, c: (g, c, 0, 0))],
        out_specs=[pl.BlockSpec((rows, gw), lambda g, c: (c, g)),
                   pl.BlockSpec((None, nsub, hpg, LANES, LANES), lambda g, c: (g, c, 0, 0, 0))],
        out_shape=[jax.ShapeDtypeStruct((s, d_inner), F32),
                   jax.ShapeDtypeStruct((ng, nc, hpg, LANES, LANES), BF16)],
        scratch_shapes=[pltpu.VMEM((hpg, LANES, LANES), F32)],
        args=(xs, bm, cm, col_a, col_dt, rowf))


def ssd_chunk_bwd(xs, bm, cm, col_a, col_dt, rowf, hprev, dy, name, side=None):
    s, d_inner = xs.shape
    ln = CHUNK
    nc = s // ln
    nsub = _pick(nc, (SSD_SUB, 2, 1))
    rows = nsub * ln
    ng, hpg = col_a.shape[0], col_a.shape[1]
    gw = d_inner // ng

    def kern(x_ref, b_ref, c_ref, ca_ref, cd_ref, rf_ref, hp_ref, dy_ref,
             dx_ref, db_ref, dc_ref, ddt_ref, da_ref, dh_scr):
        @pl.when(pl.program_id(1) == 0)
        def _():
            dh_scr[...] = jnp.zeros_like(dh_scr)

        row, col = _iota2((ln, ln), 0), _iota2((ln, ln), 1)
        causal = row >= col
        tri_ge = (col >= row).astype(BF16)
        ones = jnp.ones((ln, LANES), BF16)
        lane = _iota2((1, LANES), 1)
        last_row = (_iota2((ln, 1), 0) == ln - 1).astype(F32)
        for sc in reversed(range(nsub)):
            rs = slice(sc * ln, (sc + 1) * ln)
            bb = b_ref[rs, :].astype(BF16)
            cbf = c_ref[rs, :].astype(BF16)
            cb = _dot_nt(cbf, bb)
            dcb = jnp.zeros((ln, ln), F32)
            d_b = jnp.zeros((ln, LANES), F32)
            d_c = jnp.zeros((ln, LANES), F32)
            dxs = [jnp.zeros((ln, LANES), F32) for _ in range(gw // LANES)]
            for r in range(hpg):
                j, hf = divmod(r, LANES // HEAD)
                mh = ((lane >= HEAD * hf) & (lane < HEAD * (hf + 1))).astype(F32)
                ac = ca_ref[r, rs, :]
                dt = cd_ref[r, rs, :]
                ar = rf_ref[sc, pl.ds(r, 1), :]
                aend = rf_ref[sc, pl.ds(4 + r, 1), :]
                dm = jnp.where(causal, jnp.exp(jnp.minimum(ac - ar, 0.0)), 0.0)
                m = cb * dm
                mb = m.astype(BF16)
                xp = x_ref[rs, j * LANES:(j + 1) * LANES]
                xdt = xp * dt * mh
                xdtb = xdt.astype(BF16)
                dyp = dy_ref[rs, j * LANES:(j + 1) * LANES] * mh
                dypb = dyp.astype(BF16)
                hb = hp_ref[sc, r]
                h = hb.astype(F32)
                dh = dh_scr[r]
                dhb = dh.astype(BF16)
                e_in = jnp.exp(ac)
                dte = jnp.exp(aend - ac)
                eend = jnp.exp(aend)
                d_m = _dot_nt(dypb, xdtb)
                dcb = dcb + d_m * dm
                gm = d_m * m
                yoff_pre = _dot_nt(cbf, hb)
                bdh = _dot_nt(bb, dhb)
                dxdt = _dot_tn(mb, dypb) + bdh * dte
                t1 = _rowsum(xdt * bdh) * dte
                gh, gl = _split2(gm)
                dacum = (_rowsum(gm) - (_dot_tn(gh, ones) + _dot_tn(gl, ones))
                         + _rowsum(dyp * yoff_pre) * e_in - t1)
                end_term = _colsum(t1) + eend * jnp.sum(_colsum(dh * h), axis=1, keepdims=True)
                dacum = dacum + last_row * end_term
                da_ref[r, rs, :] = _dot_x3_left(tri_ge, dacum)
                ddt_ref[r, rs, :] = jnp.broadcast_to(_rowsum(dxdt * xp), (ln, LANES))
                dxs[j] = dxs[j] + dxdt * dt
                d_b = d_b + _dot((xdt * dte).astype(BF16), dhb)
                dye = (dyp * e_in).astype(BF16)
                d_c = d_c + _dot(dye, hb)
                dh_scr[r] = eend * dh + _dot_tn(dye, cbf)
            dcbb = dcb.astype(BF16)
            dc_ref[rs, :] = d_c + _dot(dcbb, bb)
            db_ref[rs, :] = d_b + _dot_tn(dcbb, cbf)
            for j in range(gw // LANES):
                dx_ref[rs, j * LANES:(j + 1) * LANES] = dxs[j]

    rev = nc // nsub - 1
    colspec = pl.BlockSpec((None, hpg, rows, LANES), lambda g, c: (g, 0, rev - c, 0))
    return side_call(
        kern, side,
        name=name,
        grid=(ng, nc // nsub),
        in_specs=[pl.BlockSpec((rows, gw), lambda g, c: (rev - c, g)),
                  pl.BlockSpec((rows, LANES), lambda g, c: (rev - c, g)),
                  pl.BlockSpec((rows, LANES), lambda g, c: (rev - c, g)),
                  colspec, colspec,
                  pl.BlockSpec((None, nsub, 8, LANES), lambda g, c: (g, rev - c, 0, 0)),
                  pl.BlockSpec((None, nsub, hpg, LANES, LANES), lambda g, c: (g, rev - c, 0, 0, 0)),
                  pl.BlockSpec((rows, gw), lambda g, c: (rev - c, g))],
        out_specs=[pl.BlockSpec((rows, gw), lambda g, c: (rev - c, g)),
                   pl.BlockSpec((rows, LANES), lambda g, c: (rev - c, g)),
                   pl.BlockSpec((rows, LANES), lambda g, c: (rev - c, g)),
                   colspec, colspec],
        out_shape=[jax.ShapeDtypeStruct((s, d_inner), F32),
                   jax.ShapeDtypeStruct(bm.shape, F32), jax.ShapeDtypeStruct(cm.shape, F32),
                   jax.ShapeDtypeStruct(col_a.shape, F32), jax.ShapeDtypeStruct(col_a.shape, F32)],
        scratch_shapes=[pltpu.VMEM((hpg, LANES, LANES), F32)],
        args=(xs, bm, cm, col_a, col_dt, rowf, hprev, dy))


def gnorm_fwd(y, xs, z, dexp, gain, ngroups, name):
    c = y.shape[1]
    gw = c // ngroups

    def fn(yv, xv, zv, dv, gv):
        yg = (yv + xv * dv) * (zv * _sigmoid(zv))
        outs = []
        for k in range(ngroups):
            t = yg[:, k * gw:(k + 1) * gw]
            outs.append(t * lax.rsqrt(jnp.mean(t * t, axis=1, keepdims=True) + EPS))
        return (jnp.concatenate(outs, axis=1) * gv,)

    return rowwise(fn, [(y, "row"), (xs, "row"), (z, "row"), (dexp, "full"), (gain, "full")], [(c, BF16)], tr=512, name=name)[0]


def gnorm_bwd(dn, y, xs, z, dexp, gain, ngroups, name):
    c = y.shape[1]
    gw = c // ngroups

    def fn(dnv, yv, xv, zv, dv, gv):
        yd = yv + xv * dv
        sg = _sigmoid(zv)
        sz = zv * sg
        yg = yd * sz
        dng = dnv * gv
        dyg, yh = [], []
        for k in range(ngroups):
            sl = slice(k * gw, (k + 1) * gw)
            t = yg[:, sl]
            r = lax.rsqrt(jnp.mean(t * t, axis=1, keepdims=True) + EPS)
            th = t * r
            dyg.append(r * (dng[:, sl] - th * jnp.mean(dng[:, sl] * th, axis=1, keepdims=True)))
            yh.append(th)
        dyg = jnp.concatenate(dyg, axis=1)
        yh = jnp.concatenate(yh, axis=1)
        dyd = dyg * sz
        dz = dyg * yd * (sg * (1.0 + zv * (1.0 - sg)))
        return dyd, dyd * dv, dz, _colsum(dyd * xv), _colsum(dnv * yh)

    return rowwise(fn, [(dn, "row"), (y, "row"), (xs, "row"), (z, "row"), (dexp, "full"), (gain, "full")],
                   [(c, F32), (c, F32), (c, BF16)], [(1, c), (1, c)], tr=256, name=name)


def ssd_post(ddt, da, dt, dtr, bias, alog, name):
    def fn(ddtv, dav, dtv, dtrv, bv, al):
        a_neg = -jnp.exp(al)
        ddtr = (ddtv + dav * a_neg) * _sigmoid(dtrv + bv)
        return ddtr, _colsum(ddtr), _colsum(dav * dtv) * a_neg

    return rowwise(fn, [(ddt, "row"), (da, "row"), (dt, "row"), (dtr, "row"), (bias, "full"), (alog, "full")],
                   [(LANES, BF16)], [(1, LANES), (1, LANES)], tr=512, name=name)


def _from_colform(v, s):
    ng, hpg = v.shape[0], v.shape[1]
    flat = v[..., 0].reshape(ng * hpg, s).T
    return jnp.pad(flat, ((0, 0), (0, LANES - ng * hpg)))


def ssm_fwd(x, g, p, tag, plan):
    ng, hpg, d_inner = p["ng"], p["hpg"], p["d_inner"]
    h = rms_fwd(x, g, f"ssm_rms_{tag}")
    z = mm(h, p["w_z"], name=f"ssm_inz_{tag}")
    xbc = mm(h, p["w_xbc"], name=f"ssm_inx_{tag}")
    dtr = mm(h, p["w_dt"], name=f"ssm_indt_{tag}")
    xs, bm, cm = conv_fwd(xbc, p["conv_w"], p["conv_b"], d_inner, f"ssm_conv_{tag}")
    dt, acum = ssd_pre(dtr, p["dt_bias"], p["a_log"], f"ssm_pre_{tag}")
    col_a, col_dt = _ssd_layouts(acum, ng, hpg), _ssd_layouts(dt, ng, hpg)
    rowf = _ssd_rowform(acum, ng, hpg)
    y, hprev = _hooked(plan, f"ssm_scan_{tag}", ssd_chunk_fwd, xs, bm, cm, col_a, col_dt, rowf)
    n = gnorm_fwd(y, xs, z, p["d_exp"], p["norm_gain"], ng, f"ssm_gnorm_{tag}")
    xn = mm(n, p["w_out"], add=x, name=f"ssm_out_{tag}")
    return xn, (x, h, z, xbc, dtr, xs, bm, cm, dt, col_a, col_dt, rowf, y, hprev, n)


def ssm_bwd(dxn, saved, g, p, tag, plan):
    x, h, z, xbc, dtr, xs, bm, cm, dt, col_a, col_dt, rowf, y, hprev, n = saved
    ng, hpg, d_inner = p["ng"], p["hpg"], p["d_inner"]
    s = x.shape[0]
    dxn, dxb = dxn
    dn = mm(dxb, p["w_out"], tb=True, name=f"ssm_dn_{tag}")
    dwout = mm(n, dxb, ta=True, out_dtype=BF16, name=f"ssm_dwout_{tag}")
    dy, dxs_skip, dz, dd_lane, dgain = gnorm_bwd(dn, y, xs, z, p["d_exp"], p["norm_gain"], ng, f"ssm_dgnorm_{tag}")
    dxs, dbm, dcm, ddt_c, da_c = _hooked(plan, f"ssm_dscan_{tag}", ssd_chunk_bwd, xs, bm, cm, col_a, col_dt, rowf, hprev, dy)
    ddtr, dbias, dalog = ssd_post(_from_colform(ddt_c, s), _from_colform(da_c, s), dt, dtr,
                                  p["dt_bias"], p["a_log"], f"ssm_post_{tag}")
    res = conv_bwd_pre(xbc, p["conv_w"], p["conv_b"], dxs, dxs_skip, dbm, dcm, f"ssm_dconv_{tag}")
    dpre, dconv_w, dconv_b = res[0], jnp.concatenate(res[1:5], axis=0), res[5]
    dxbc = conv_bwd_in(dpre, p["conv_w"], f"ssm_dconvin_{tag}")
    dh = mm(dz, p["w_z"], tb=True, name=f"ssm_dhz_{tag}")
    dh = mm(dxbc, p["w_xbc"], tb=True, add=dh, name=f"ssm_dhx_{tag}")
    dh = mm(ddtr, p["w_dt"], tb=True, add=dh, name=f"ssm_dhdt_{tag}")
    dwz = mm(h, dz, ta=True, out_dtype=BF16, name=f"ssm_dwz_{tag}")
    dwxbc = mm(h, dxbc, ta=True, out_dtype=BF16, name=f"ssm_dwxbc_{tag}")
    dwdt = mm(h, ddtr, ta=True, out_dtype=BF16, name=f"ssm_dwdt_{tag}")
    dx, dg = rms_bwd(x, g, dh, dxn, f"ssm_drms_{tag}")
    nh = ng * hpg
    dwin = jnp.concatenate([dwz, dwxbc, dwdt[:, :nh]], axis=1)
    dd = dd_lane.reshape(nh, HEAD).sum(-1)
    return dx, dg, dict(w_in=dwin, conv_w=dconv_w, conv_b=dconv_b, dt_bias=dbias[0, :nh], a_log=dalog[0, :nh],
                        d=dd, norm_gain=dgain, w_out=dwout)


def local_step(x, target, w, plan):
    d = x.shape[1]
    depth = w["mix_norm"].shape[0]
    bd = _head_blockdiag(LANES)
    tril = jnp.tril(jnp.ones((CHUNK, CHUNK), bool))
    ssm_heads = w["ssm_dt_bias"].shape[1]
    d_inner = w["ssm_norm_gain"].shape[1]
    ng = w["ssm_norm_gain"].shape[1] // 256
    nstate = CHUNK

    def pad_lanes(v):
        return jnp.pad(v, ((0, 0), (0, LANES - v.shape[1])))

    def ssm_params(j):
        w_in = w["ssm_w_in"][j]
        cw = w["ssm_conv_w"][j]
        return dict(ng=ng, hpg=ssm_heads // ng, d_inner=d_inner,
                    w_z=w_in[:, :d_inner], w_xbc=w_in[:, d_inner:d_inner + d_inner + 2 * ng * nstate],
                    w_dt=pad_lanes(w_in[:, 2 * d_inner + 2 * ng * nstate:]),
                    conv_w=[cw[k:k + 1] for k in range(cw.shape[0])], conv_b=w["ssm_conv_b"][j:j + 1],
                    dt_bias=pad_lanes(w["ssm_dt_bias"][j:j + 1]), a_log=pad_lanes(w["ssm_a_log"][j:j + 1]),
                    d_exp=jnp.repeat(w["ssm_d"][j], HEAD)[None, :], norm_gain=w["ssm_norm_gain"][j:j + 1],
                    w_out=w["ssm_w_out"][j])

    def gm_params(j):
        wc = jnp.where(tril, w["gm_w_s"][j], 0.0).astype(BF16)
        bst = jnp.repeat(w["gm_b_s"][j].T, LANES, axis=1)
        return wc, bst

    def sb_gains(j):
        nh = d // HEAD
        return jnp.tile(w["sb_q_gain"][j], nh)[None, :], jnp.tile(w["sb_k_gain"][j], nh)[None, :]

    saved = []
    cur = x
    for i in range(depth):
        kind, j = i % 3, i // 3
        gmix = w["mix_norm"][i:i + 1]
        if kind == 0:
            qg, kg = sb_gains(j)
            cur, sv = sb_fwd(cur, gmix, w["sb_w_qkv"][j], qg, kg, lambda j=j: w["sb_w_o"][j], bd, f"{i}", plan)
        elif kind == 1:
            wc, bst = gm_params(j)
            cur, sv = gm_fwd(cur, gmix, w["gm_w_in"][j], w["gm_b_in"][j:j + 1], w["gm_v_gain"][j:j + 1], wc, bst,
                             w["gm_w_out"][j], f"{i}")
        else:
            cur, sv = ssm_fwd(cur, gmix, ssm_params(j), f"{i}", plan)
        cur, sv2 = ffn_fwd(cur, w["ffn_norm"][i:i + 1], w["ffn_w_gu"][i], w["ffn_w_down"][i], f"{i}", plan)
        saved.append((sv, sv2))

    loss, dcur = loss_and_grad(cur, target, "loss")

    grads = {k: [None] * len(v) for k, v in w.items()}
    for i in reversed(range(depth)):
        kind, j = i % 3, i // 3
        sv, sv2 = saved[i]
        gmix = w["mix_norm"][i:i + 1]
        dcur, dgf, dwgu, dwdown = ffn_bwd(dcur, sv2, w["ffn_norm"][i:i + 1], w["ffn_w_gu"][i], w["ffn_w_down"][i], f"{i}")
        grads["ffn_norm"][i], grads["ffn_w_gu"][i], grads["ffn_w_down"][i] = dgf[0], dwgu, dwdown
        plan.grads_ready({("ffn_w_gu", i): dwgu, ("ffn_w_down", i): dwdown})
        if kind == 0:
            qg, kg = sb_gains(j)
            dcur, dg, dwqkv, dqg, dkg, dwo = sb_bwd(dcur, sv, gmix, w["sb_w_qkv"][j], qg, kg, w["sb_w_o"][j], bd, f"{i}", plan)
            grads["sb_w_qkv"][j], grads["sb_q_gain"][j], grads["sb_k_gain"][j], grads["sb_w_o"][j] = dwqkv, dqg, dkg, dwo
        elif kind == 1:
            wc, bst = gm_params(j)
            dcur, dg, dwin, dbin, dvg, dws, dbs, dwout = gm_bwd(dcur, sv, gmix, w["gm_w_in"][j], w["gm_v_gain"][j:j + 1],
                                                                 wc, bst, w["gm_w_out"][j], f"{i}")
            grads["gm_w_in"][j], grads["gm_b_in"][j], grads["gm_v_gain"][j] = dwin, dbin[0], dvg[0]
            grads["gm_w_s"][j], grads["gm_b_s"][j], grads["gm_w_out"][j] = dws, dbs, dwout
        else:
            dcur, dg, gs = ssm_bwd(dcur, sv, gmix, ssm_params(j), f"{i}", plan)
            grads["ssm_w_in"][j], grads["ssm_conv_w"][j], grads["ssm_conv_b"][j] = gs["w_in"], gs["conv_w"], gs["conv_b"][0]
            grads["ssm_dt_bias"][j], grads["ssm_a_log"][j], grads["ssm_d"][j] = gs["dt_bias"], gs["a_log"], gs["d"]
            grads["ssm_norm_gain"][j], grads["ssm_w_out"][j] = gs["norm_gain"][0], gs["w_out"]
        grads["mix_norm"][i] = dg[0]
        mixer = {0: ("sb_w_qkv", "sb_w_o"), 1: ("gm_w_in", "gm_w_out"), 2: ("ssm_w_in", "ssm_w_out")}[kind]
        plan.grads_ready({(n, j): grads[n][j] for n in mixer})
    grads = {k: (v if k in MATRICES else jnp.stack(v)) for k, v in grads.items()}
    return loss, dcur[0], grads


WEIGHTS = ["mix_norm", "ffn_norm", "sb_w_qkv", "sb_q_gain", "sb_k_gain", "sb_w_o", "gm_w_in", "gm_b_in", "gm_v_gain",
           "gm_w_s", "gm_b_s", "gm_w_out", "ssm_w_in", "ssm_conv_w", "ssm_conv_b", "ssm_dt_bias", "ssm_a_log", "ssm_d",
           "ssm_norm_gain", "ssm_w_out", "ffn_w_gu", "ffn_w_down"]
SHARDED = {"sb_w_qkv": 2, "sb_w_o": 1, "gm_w_in": 2, "gm_w_out": 1, "ssm_w_in": 2, "ssm_conv_w": 2, "ssm_conv_b": 1,
           "ssm_norm_gain": 1, "ssm_w_out": 1, "ffn_w_gu": 2, "ffn_w_down": 1}
EXACT = ("ssm_conv_w", "ssm_conv_b", "ssm_norm_gain")
MATRICES = tuple(n for n in SHARDED if n not in EXACT)
COLUMN_BLOCKS = ("sb_w_qkv", "gm_w_in", "ffn_w_gu")
REPLICATED = [n for n in WEIGHTS if n not in SHARDED]
N_CHIPS = 4
N_DEV = 8
PACK_COLS = 1024


def _pack(pieces, dtype, align):
    flat = jnp.concatenate([p.reshape(-1).astype(dtype) for p in pieces])
    rows = -(-flat.shape[0] // (PACK_COLS * align)) * align
    flat = jnp.pad(flat, (0, rows * PACK_COLS - flat.shape[0]))
    return flat.reshape(rows, PACK_COLS)


def _unpack(flat, shapes):
    out, off = [], 0
    for shp in shapes:
        n = math.prod(shp)
        out.append(flat[off:off + n].reshape(shp))
        off += n
    return out


ANY = pl.BlockSpec(memory_space=pl.ANY)


def _pos():
    return lax.axis_index("x"), lax.axis_index("y"), lax.axis_index("c")


def _remote(src, dst, send, recv, k, to):
    return pltpu.make_async_remote_copy(src_ref=src, dst_ref=dst, send_sem=send.at[k], recv_sem=recv.at[k],
                                        device_id=to, device_id_type=MESH_ID)


def _comm_call(body, name, ins, out_shapes, nsem, aliases=None):
    return pl.pallas_call(
        body, name=name, out_shape=out_shapes,
        in_specs=[ANY] * len(ins), out_specs=[ANY] * len(out_shapes),
        scratch_shapes=[pltpu.SemaphoreType.DMA((nsem,)), pltpu.SemaphoreType.DMA((nsem,))],
        input_output_aliases=aliases or {},
    )(*ins)


def stage_shard(w, layer, chip, name, dtype=BF16):
    _, rows, cols = w.shape
    tr = _pick(rows, (512, 352, 256, 128))

    def kern(idx_ref, w_ref, o_ref):
        o_ref[...] = w_ref[...].astype(dtype)

    grid_spec = pltpu.PrefetchScalarGridSpec(
        num_scalar_prefetch=1, grid=(rows // tr,),
        in_specs=[pl.BlockSpec((None, tr, cols), lambda i, idx: (layer, i, 0))],
        out_specs=pl.BlockSpec((None, tr, cols), lambda i, idx: (idx[0], i, 0)))
    return pl.pallas_call(
        kern, name=name, grid_spec=grid_spec,
        out_shape=jax.ShapeDtypeStruct((N_CHIPS, rows, cols), dtype),
        compiler_params=_params(("parallel",)),
    )(jnp.reshape(chip, (1,)).astype(jnp.int32), w)


class Side:
    def __init__(self, arrays, out_shapes, aliases, nsem, start, finish):
        self.arrays, self.out_shapes, self.aliases, self.nsem = list(arrays), list(out_shapes), aliases, nsem
        self.start, self.finish = start, finish


def run_side(side, name):
    n_in, n_out = len(side.arrays), len(side.out_shapes)

    def body(*refs):
        ins, outs = refs[:n_in], refs[n_in:n_in + n_out]
        send, recv = refs[n_in + n_out:]
        side.start(ins, outs, send, recv)
        side.finish(ins, outs, send, recv)

    return _comm_call(body, name, side.arrays, side.out_shapes, side.nsem, aliases=side.aliases)


def side_call(kern, side, *, name, grid, in_specs, out_specs, out_shape, scratch_shapes, args):
    if side is None:
        res = pl.pallas_call(kern, name=name, grid=grid, in_specs=in_specs, out_specs=out_specs, out_shape=out_shape,
                             scratch_shapes=scratch_shapes,
                             compiler_params=_params(("parallel",) + ("arbitrary",) * (len(grid) - 1)))(*args)
        return list(res), []
    n_in, n_out, n_scr = len(in_specs), len(out_specs), len(scratch_shapes)
    s_in, s_out = len(side.arrays), len(side.out_shapes)

    def body(*refs):
        ins, refs = refs[:n_in], refs[n_in:]
        side_ins, refs = refs[:s_in], refs[s_in:]
        outs, refs = refs[:n_out], refs[n_out:]
        side_outs, refs = refs[:s_out], refs[s_out:]
        scr, (send, recv) = refs[:n_scr], refs[n_scr:]
        first, last = None, None
        for axis, size in enumerate(grid):
            at0, at1 = pl.program_id(axis) == 0, pl.program_id(axis) == size - 1
            first = at0 if first is None else first & at0
            last = at1 if last is None else last & at1

        @pl.when(first)
        def _():
            side.start(side_ins, side_outs, send, recv)

        kern(*ins, *outs, *scr)

        @pl.when(last)
        def _():
            side.finish(side_ins, side_outs, send, recv)

    res = pl.pallas_call(
        body, name=name, grid=grid,
        in_specs=list(in_specs) + [ANY] * s_in, out_specs=list(out_specs) + [ANY] * s_out,
        out_shape=list(out_shape) + side.out_shapes,
        scratch_shapes=list(scratch_shapes) + [pltpu.SemaphoreType.DMA((side.nsem,)), pltpu.SemaphoreType.DMA((side.nsem,))],
        input_output_aliases={n_in + a: n_out + b for a, b in side.aliases.items()},
        compiler_params=_params(("arbitrary",) * len(grid)),
    )(*args, *side.arrays)
    return list(res[:n_out]), list(res[n_out:])


def gather_side(staged):
    n = len(staged)

    def plan(o_refs, send, recv):
        x, y, c = _pos()
        chips = [(1 - x, y), (x, 1 - y), (1 - x, 1 - y)]

        def part(u, chip, cc):
            half = staged[u].shape[1] // 2
            return o_refs[u].at[2 * chip[0] + chip[1], pl.ds(cc * half, half), :]

        first = [_remote(part(u, (x, y), c), part(u, (x, y), c), send, recv, 6 * u + j, (*chip, c))
                 for u in range(n) for j, chip in enumerate(chips)]
        landed = [_remote(part(u, chip, c), part(u, chip, c), send, recv, 6 * u + j, (x, y, c))
                  for u in range(n) for j, chip in enumerate(chips)]
        passed = [_remote(part(u, chip, c), part(u, chip, c), send, recv, 6 * u + 3 + j, (x, y, 1 - c))
                  for u in range(n) for j, chip in enumerate(chips)]
        handed = [_remote(part(u, chip, 1 - c), part(u, chip, 1 - c), send, recv, 6 * u + 3 + j, (x, y, c))
                  for u in range(n) for j, chip in enumerate(chips)]
        return first, landed, passed, handed

    def start(ins, outs, send, recv):
        for cp in plan(outs, send, recv)[0]:
            cp.start()

    def finish(ins, outs, send, recv):
        first, landed, passed, handed = plan(outs, send, recv)
        for got, fw in zip(landed, passed):
            got.wait_recv()
            fw.start()
        for got in handed:
            got.wait_recv()
        for cp in first + passed:
            cp.wait_send()

    outs = [jax.ShapeDtypeStruct(s.shape, s.dtype) for s in staged]
    return Side(staged, outs, {u: u for u in range(n)}, 6 * n, start, finish)


def swap_halves(gps, name):
    n = len(gps)

    def body(*refs):
        g_refs, r_refs = refs[:n], refs[n:2 * n]
        send, recv = refs[2 * n:]
        x, y, c = _pos()
        cps = []
        for u in range(n):
            half = gps[u].shape[1] // 2
            cps.append(_remote(g_refs[u].at[:, pl.ds((1 - c) * half, half), :], r_refs[u], send, recv, u, (x, y, 1 - c)))
        for cp in cps:
            cp.start()
        for cp in cps:
            cp.wait()

    outs = [jax.ShapeDtypeStruct((g.shape[0], g.shape[1] // 2, g.shape[2]), g.dtype) for g in gps]
    return _comm_call(body, name, gps, outs, n)


def scatter_side(parts):
    n = len(parts)

    def plan(p_refs, r_refs, send, recv):
        x, y, c = _pos()
        chips = [(1 - x, y), (x, 1 - y), (1 - x, 1 - y)]
        return [_remote(p_refs[u].at[2 * chip[0] + chip[1]], r_refs[u].at[j], send, recv, 3 * u + j, (*chip, c))
                for u in range(n) for j, chip in enumerate(chips)]

    def start(ins, outs, send, recv):
        for cp in plan(ins, outs, send, recv):
            cp.start()

    def finish(ins, outs, send, recv):
        for cp in plan(ins, outs, send, recv):
            cp.wait()

    outs = [jax.ShapeDtypeStruct((N_CHIPS - 1,) + p.shape[1:], p.dtype) for p in parts]
    return Side(parts, outs, {}, 3 * n, start, finish)


def join_halves(bufs):
    n = len(bufs)

    def body(*refs):
        o_refs = refs[n:2 * n]
        send, recv = refs[2 * n:]
        x, y, c = _pos()

        def rows(u, cc):
            half = bufs[u].shape[1] // 2
            return o_refs[u].at[:, pl.ds(cc * half, half), :]

        cps = [_remote(rows(u, c), rows(u, c), send, recv, u, (x, y, 1 - c)) for u in range(n)]
        for cp in cps:
            cp.start()
        for u in range(n):
            _remote(rows(u, 1 - c), rows(u, 1 - c), send, recv, u, (x, y, c)).wait_recv()
        for cp in cps:
            cp.wait_send()

    outs = [jax.ShapeDtypeStruct(b.shape, b.dtype) for b in bufs]
    return _comm_call(body, "join_halves", bufs, outs, n, aliases={u: u for u in range(n)})


def gather_small(sg, name):
    rows, cols = sg.shape

    def body(s_ref, o_ref, send, recv, lsem):
        x, y, c = _pos()
        me, sibling = (x, y, c), (x, y, 1 - c)
        chips = [(1 - x, y), (x, 1 - y), (1 - x, 1 - y)]

        def blk(px, py, pc):
            return o_ref.at[4 * px + 2 * py + pc]

        mine = pltpu.make_async_copy(s_ref, blk(*me), lsem)
        mine.start()
        first = [_remote(s_ref, blk(*me), send, recv, 0, sibling)]
        first += [_remote(s_ref, blk(*me), send, recv, 1 + j, (*chip, c)) for j, chip in enumerate(chips)]
        for cp in first:
            cp.start()
        passed = [_remote(blk(*chip, c), blk(*chip, c), send, recv, 4 + j, sibling) for j, chip in enumerate(chips)]
        for j, chip in enumerate(chips):
            _remote(blk(*chip, c), blk(*chip, c), send, recv, 1 + j, me).wait_recv()
            passed[j].start()
        _remote(blk(*sibling), blk(*sibling), send, recv, 0, me).wait_recv()
        for j, chip in enumerate(chips):
            _remote(blk(*chip, 1 - c), blk(*chip, 1 - c), send, recv, 4 + j, me).wait_recv()
        for cp in first + passed:
            cp.wait_send()
        mine.wait()

    return pl.pallas_call(
        body, name=name,
        out_shape=jax.ShapeDtypeStruct((N_DEV, rows, cols), sg.dtype),
        in_specs=[ANY], out_specs=ANY,
        scratch_shapes=[pltpu.SemaphoreType.DMA((N_DEV - 1,)), pltpu.SemaphoreType.DMA((N_DEV - 1,)), pltpu.SemaphoreType.DMA],
    )(sg)


def sum_cores(gp, theirs, core, chip, name):
    nch, rows, cols = gp.shape
    half = rows // 2
    tr = _pick(half, (512, 352, 256, 176, 128, 64))
    nb = half // tr

    def kern(idx_ref, g_ref, t_ref, own_ref, all_ref):
        k = pl.program_id(1)
        s = g_ref[...].astype(F32) + t_ref[...].astype(F32)
        all_ref[...] = s.astype(BF16)

        @pl.when(k == idx_ref[1])
        def _():
            own_ref[...] = s

    grid_spec = pltpu.PrefetchScalarGridSpec(
        num_scalar_prefetch=1, grid=(nb, nch),
        in_specs=[pl.BlockSpec((None, tr, cols), lambda i, k, idx: (k, idx[0] * nb + i, 0)),
                  pl.BlockSpec((None, tr, cols), lambda i, k, idx: (k, i, 0))],
        out_specs=[pl.BlockSpec((tr, cols), lambda i, k, idx: (i, 0)),
                   pl.BlockSpec((None, tr, cols), lambda i, k, idx: (k, i, 0))])
    return pl.pallas_call(
        kern, name=name, grid_spec=grid_spec,
        out_shape=[jax.ShapeDtypeStruct((half, cols), F32), jax.ShapeDtypeStruct((nch, half, cols), BF16)],
        compiler_params=_params(("parallel", "arbitrary")),
    )(jnp.stack([core, chip]).astype(jnp.int32), gp, theirs)


def sum_chips(own, others, core, layer, nlayers, into, name):
    half, cols = own.shape
    tr = _pick(half, (512, 352, 256, 176, 128, 64))
    nb = half // tr

    def kern(idx_ref, o_ref, a_ref, b_ref, c_ref, *rest):
        out_ref = rest[-1]
        out_ref[...] = ((o_ref[...] + a_ref[...].astype(F32)) + b_ref[...].astype(F32)) + c_ref[...].astype(F32)

    grid_spec = pltpu.PrefetchScalarGridSpec(
        num_scalar_prefetch=1, grid=(nb,),
        in_specs=[pl.BlockSpec((tr, cols), lambda i, idx: (i, 0))] +
                 [pl.BlockSpec((None, tr, cols), lambda i, idx, j=j: (j, i, 0)) for j in range(N_CHIPS - 1)] +
                 ([] if into is None else [pl.BlockSpec(memory_space=pl.ANY)]),
        out_specs=pl.BlockSpec((None, tr, cols), lambda i, idx: (layer, idx[0] * nb + i, 0)))
    args = [jnp.reshape(core, (1,)).astype(jnp.int32), own, others, others, others] + ([] if into is None else [into])
    return pl.pallas_call(
        kern, name=name, grid_spec=grid_spec,
        out_shape=jax.ShapeDtypeStruct((nlayers, 2 * half, cols), F32),
        input_output_aliases={} if into is None else {len(args) - 1: 0},
        compiler_params=_params(("parallel",)),
    )(*args)


def small_update(gath, w, m, v, name):
    def fn(*vs):
        g = vs[0]
        for t in vs[1:N_DEV]:
            g = g + t
        wv, mv, vv = vs[N_DEV:]
        m2 = ADAM_B1 * mv + (1.0 - ADAM_B1) * g
        v2 = ADAM_B2 * vv + (1.0 - ADAM_B2) * (g * g)
        m_hat = m2 / (1.0 - ADAM_B1 ** ADAM_STEP)
        v_hat = v2 / (1.0 - ADAM_B2 ** ADAM_STEP)
        return g, -ADAM_LR * (m_hat / (jnp.sqrt(v_hat) + ADAM_EPS) + ADAM_WD * wv), m2, v2

    c = w.shape[1]
    ins = [(gath[k], "row") for k in range(N_DEV)] + [(w, "row"), (m, "row"), (v, "row")]
    return rowwise(fn, ins, [(c, F32)] * 4, tr=w.shape[0] // 2, name=name)


_MIX = {0: [("sb_w_qkv", 0), ("sb_w_o", 0)], 1: [("gm_w_in", 0), ("gm_w_out", 0)],
        2: [("ssm_w_in", 0), ("ssm_w_out", 0)], 3: [("sb_w_qkv", 1), ("sb_w_o", 1)]}
_FFN = {i: [("ffn_w_gu", i), ("ffn_w_down", i)] for i in range(4)}
GATHER_FIRST = _MIX[0][:1]
GATHER_AT = {"sb_attn_0": _MIX[0][1:] + _FFN[0] + _FFN[1],
             "ffn_gu_0": _MIX[1], "ffn_down_0": _MIX[2][1:], "ffn_gu_1": _MIX[2][:1], "ffn_down_1": _FFN[2][1:],
             "ssm_scan_2": _FFN[2][:1] + _MIX[3] + _FFN[3][1:], "ffn_gu_2": _FFN[3][:1]}
SCATTER_AT = {"ssm_dscan_2": _FFN[3] + _MIX[3] + _FFN[2], "sb_dattn_0": _MIX[2] + _FFN[1] + _MIX[1] + _FFN[0]}
SCATTER_LAST = _MIX[0]


class _Plan:
    def __init__(self, ins, core, chip, vectors):
        self.core, self.chip = core, chip
        self.staged = {(n, l): stage_shard(ins[n], l, chip, f"stage_{n}_{l}")
                       for n in MATRICES for l in range(ins[n].shape[0])}
        self.full = {n: [None] * ins[n].shape[0] for n in MATRICES}
        self.ready = {}
        self.parts = {}
        self.halves = {}
        self.layers = {n: ins[n].shape[0] for n in MATRICES}
        self.swaps = 0
        first = [self.staged[u] for u in GATHER_FIRST] + [stage_shard(vectors, 0, chip, "stage_vectors", F32)]
        *gathered, self.vectors = run_side(gather_side(first), "gather_first")
        self._fill(GATHER_FIRST, gathered)

    def _fill(self, units, gathered):
        for (n, l), g in zip(units, gathered):
            if n in COLUMN_BLOCKS:
                self.full[n][l] = g
            elif n == "ssm_w_in":
                self.full[n][l] = jnp.concatenate([g[k] for k in range(N_CHIPS)], axis=1)
            else:
                self.full[n][l] = g.reshape(-1, g.shape[-1])

    def _prepare(self, units):
        gps = [self.ready[u] for u in units]
        theirs = swap_halves(gps, f"swap_halves_{self.swaps}")
        self.swaps += 1
        for (n, l), g, t in zip(units, gps, theirs):
            self.parts[(n, l)] = sum_cores(g, t, self.core, self.chip, f"sum_cores_{n}_{l}")

    def _reduce(self, units, others):
        for (n, l), other in zip(units, others):
            self.halves[n] = sum_chips(self.parts[(n, l)][0], other, self.core, l, self.layers[n], self.halves.get(n),
                                       f"sum_chips_{n}_{l}")

    def side(self, tag):
        if tag in GATHER_AT:
            return gather_side([self.staged[u] for u in GATHER_AT[tag]])
        if tag in SCATTER_AT:
            self._prepare(SCATTER_AT[tag])
            return scatter_side([self.parts[u][1] for u in SCATTER_AT[tag]])
        return None

    def done(self, tag, results):
        if tag in GATHER_AT:
            self._fill(GATHER_AT[tag], results)
        else:
            self._reduce(SCATTER_AT[tag], results)

    def grads_ready(self, grads):
        for (n, l), g in grads.items():
            if n in COLUMN_BLOCKS:
                self.ready[(n, l)] = g
            elif n == "ssm_w_in":
                self.ready[(n, l)] = jnp.stack(jnp.split(g, N_CHIPS, axis=1))
            else:
                self.ready[(n, l)] = g.reshape(N_CHIPS, -1, g.shape[-1])

    def shard_grads(self):
        self._prepare(SCATTER_LAST)
        self._reduce(SCATTER_LAST, run_side(scatter_side([self.parts[u][1] for u in SCATTER_LAST]), "scatter_last"))
        names = sorted(self.halves)
        return dict(zip(names, join_halves([self.halves[n] for n in names])))


def _step(ins):
    x, target = ins["x"][0], ins["loss_target"][0]
    core = lax.axis_index("c")
    chip = 2 * lax.axis_index("x") + lax.axis_index("y")

    def lane_pad(v):
        return jnp.pad(v, ((0, 0), (0, PACK_COLS - v.shape[1])))

    vec_rows = [ins["ssm_conv_w"][0], ins["ssm_conv_b"], lane_pad(ins["ssm_norm_gain"])]
    blk = jnp.concatenate(vec_rows + [jnp.zeros((2 * SUBLANES - 6, PACK_COLS), F32)], axis=0)
    plan = _Plan(ins, core, chip, blk[None])
    per_chip = plan.vectors
    ngw = ins["ssm_norm_gain"].shape[1]
    full = {
        "ssm_conv_w": jnp.concatenate([per_chip[k, 0:4] for k in range(N_CHIPS)], axis=1)[None],
        "ssm_conv_b": jnp.concatenate([per_chip[k, 4:5] for k in range(N_CHIPS)], axis=1),
        "ssm_norm_gain": jnp.concatenate([per_chip[k, 5:6, :ngw] for k in range(N_CHIPS)], axis=1),
    }

    full.update(plan.full)
    for n in REPLICATED:
        full[n] = ins[n]

    loss, dx, grads = local_step(x, target, full, plan)
    loss = lax.psum(loss, ALL_AXES)
    gshards = plan.shard_grads()

    small_shapes = [ins[n].shape for n in REPLICATED]
    vec_shapes = [grads[n].shape for n in EXACT]
    vec_pack = _pack([grads[n] for n in EXACT], F32, SUBLANES)
    gath = gather_small(jnp.concatenate([_pack([grads[n] for n in REPLICATED], F32, SUBLANES), vec_pack], axis=0),
                        "gather_small")
    packed = [jnp.concatenate([_pack([ins[pre + n] for n in REPLICATED], F32, SUBLANES), jnp.zeros_like(vec_pack)], axis=0)
              for pre in ("", "m_", "v_")]
    res = small_update(gath, *packed, name="small_update")
    nrep = res[0].shape[0] - vec_pack.shape[0]
    small = [dict(zip(REPLICATED, _unpack(r[:nrep].reshape(-1), small_shapes))) for r in res]
    vec_g = dict(zip(EXACT, _unpack(res[0][nrep:].reshape(-1), vec_shapes)))

    out_g, out_d, out_m, out_v = {}, {}, {}, {}
    for n in REPLICATED:
        out_g[n], out_d[n], out_m[n], out_v[n] = (s[n] for s in small)
    for n in SHARDED:
        shp = ins[n].shape
        if n in EXACT:
            g = lax.dynamic_slice_in_dim(vec_g[n], chip * shp[-1], shp[-1], axis=vec_g[n].ndim - 1)
        else:
            g = gshards[n]
        two = (math.prod(shp[:-1]), shp[-1])
        d2, m2, v2, g2 = adamw(ins[n].reshape(two), g.reshape(two), ins["m_" + n].reshape(two),
                               ins["v_" + n].reshape(two), f"adamw_{n}")
        out_g[n], out_d[n], out_m[n], out_v[n] = g2.reshape(shp), d2.reshape(shp), m2.reshape(shp), v2.reshape(shp)
    return (loss, dx[None], *[out_g[n] for n in WEIGHTS], *[out_d[n] for n in WEIGHTS],
            *[out_m[n] for n in WEIGHTS], *[out_v[n] for n in WEIGHTS])


def kernel(x, mix_norm, ffn_norm, sb_w_qkv, sb_q_gain, sb_k_gain, sb_w_o, gm_w_in, gm_b_in, gm_v_gain, gm_w_s, gm_b_s, gm_w_out, ssm_w_in, ssm_conv_w, ssm_conv_b, ssm_dt_bias, ssm_a_log, ssm_d, ssm_norm_gain, ssm_w_out, ffn_w_gu, ffn_w_down, loss_target, m_mix_norm, m_ffn_norm, m_sb_w_qkv, m_sb_q_gain, m_sb_k_gain, m_sb_w_o, m_gm_w_in, m_gm_b_in, m_gm_v_gain, m_gm_w_s, m_gm_b_s, m_gm_w_out, m_ssm_w_in, m_ssm_conv_w, m_ssm_conv_b, m_ssm_dt_bias, m_ssm_a_log, m_ssm_d, m_ssm_norm_gain, m_ssm_w_out, m_ffn_w_gu, m_ffn_w_down, v_mix_norm, v_ffn_norm, v_sb_w_qkv, v_sb_q_gain, v_sb_k_gain, v_sb_w_o, v_gm_w_in, v_gm_b_in, v_gm_v_gain, v_gm_w_s, v_gm_b_s, v_gm_w_out, v_ssm_w_in, v_ssm_conv_w, v_ssm_conv_b, v_ssm_dt_bias, v_ssm_a_log, v_ssm_d, v_ssm_norm_gain, v_ssm_w_out, v_ffn_w_gu, v_ffn_w_down):
    return _step(dict(locals()))
```

```python
import functools
import math

import jax
import jax.numpy as jnp
from jax import lax
from jax.experimental import pallas as pl
from jax.experimental.pallas import tpu as pltpu

F32 = jnp.float32
BF16 = jnp.bfloat16
EPS = 1e-6
LANES = 128
SUBLANES = 8
VMEM_LIMIT = 56 * 1024 * 1024
HEAD = 64
CHUNK = 128
SB_TQ, SB_TK = 256, 256
SSD_SUB = 8
SB_DEAD = -110.0
SB_UNSEEN = -1e30
ADAM_LR, ADAM_B1, ADAM_B2, ADAM_EPS, ADAM_WD, ADAM_STEP = 0.001, 0.9, 0.999, 1e-08, 0.01, 10
MESH_ID = pl.DeviceIdType.MESH
ALL_AXES = ("x", "y", "c")


def _params(sem):
    return pltpu.CompilerParams(dimension_semantics=sem, vmem_limit_bytes=VMEM_LIMIT)


def _pick(n, cands):
    for c in cands:
        if n % c == 0:
            return c
    return n


def _dot(a, b, dims=((1,), (0,))):
    return lax.dot_general(a, b, (dims, ((), ())), preferred_element_type=F32)


def _dot_nt(a, b):
    return _dot(a, b, ((1,), (1,)))


def _dot_tn(a, b):
    return _dot(a, b, ((0,), (0,)))


def _split2(x):
    hi = x.astype(BF16)
    lo = (x - hi.astype(F32)).astype(BF16)
    return hi, lo


def _dot_x2(x, m):
    hi, lo = _split2(x)
    return _dot(hi, m) + _dot(lo, m)


def _dot_x3_left(m, x):
    h1 = x.astype(BF16)
    r1 = x - h1.astype(F32)
    h2 = r1.astype(BF16)
    h3 = (r1 - h2.astype(F32)).astype(BF16)
    return _dot(m, h1) + _dot(m, h2) + _dot(m, h3)


def _sigmoid(x):
    return 1.0 / (1.0 + jnp.exp(-x))


def _softplus(x):
    return jnp.maximum(x, 0.0) + jnp.log(1.0 + jnp.exp(-jnp.abs(x)))


def _colsum(x):
    return jnp.sum(x, axis=0, keepdims=True)


def _rowsum(x):
    return jnp.sum(x, axis=1, keepdims=True)


def _iota2(shape, dim):
    return lax.broadcasted_iota(jnp.int32, shape, dim)


MM_VMEM_BUDGET = 40 * 1024 * 1024
MM_STEP_US = 0.35
MM_HBM_BYTES_PER_US = 3.0e6
MM_VMEM_BYTES_PER_US = 1.5e6
MM_FLOPS_PER_US = 9.0e8
MXU_DIM = 256


def _mm_tiles(m, n, kk, wn, wk, a_bytes, b_bytes, has_add):
    def divisors(total, cands):
        got = [c for c in cands if total % c == 0 and c <= total]
        return got or [total]

    best = None
    for tm in divisors(m, (1024, 512, 256, 128)):
        for tn in divisors(wn, (1024, 768, 1408, 512, 256, 128)):
            for tk in divisors(wk, (4096, 2816, 2048, 1408, 1024, 768, 512, 256, 128)):
                nk = kk // tk
                vmem = 2 * (tm * tk * a_bytes + tk * tn * b_bytes + tm * tn * 4 * (2 if has_add else 1))
                vmem += tm * tn * 4 if nk > 1 else 0
                if vmem > MM_VMEM_BUDGET:
                    continue
                steps = (m // tm) * (n // tn) * nk
                a_reads = 1 if nk == 1 else n // tn
                traffic = m * kk * a_bytes * a_reads + kk * n * b_bytes * (m // tm) + m * n * 4
                fill = min(1.0, tn / MXU_DIM) * min(1.0, tm / MXU_DIM)
                compute = 2.0 * m * n * kk / (MM_FLOPS_PER_US * fill)
                cost = steps * MM_STEP_US + max(compute, traffic / MM_HBM_BYTES_PER_US)
                if nk > 1:
                    cost += steps * tm * tn * 8 / MM_VMEM_BYTES_PER_US
                if best is None or cost < best[0]:
                    best = (cost, tm, tn, tk)
    return best[1:]


def mm(a, b, *, ta=False, tb=False, add=None, bias=None, a_chunks=False, b_chunks=False, out_chunks=False,
       out_dtype=F32, name, side=None):
    wa = None
    if a_chunks:
        m, wa = a.shape[1], a.shape[2]
        kk = a.shape[0] * wa
    elif ta:
        kk, m = a.shape
    else:
        m, kk = a.shape
    nch, wide = 1, None
    if b_chunks:
        nch, rows_b, wide = b.shape
        kb, n = (rows_b, nch * wide) if not tb else (nch * wide, rows_b)
    elif tb:
        n, kb = b.shape
    else:
        kb, n = b.shape
    wide_o = n // N_CHIPS if out_chunks else None
    assert kk == kb, (a.shape, b.shape, ta, tb)
    has_add, has_bias = add is not None, bias is not None
    wk = wide if (wide and tb) else kk
    wn = wide if (wide and not tb) else n
    tm, tn, tk = _mm_tiles(m, n, kk, math.gcd(wn, wide_o) if wide_o else wn, math.gcd(wk, wa) if wa else wk,
                           a.dtype.itemsize, b.dtype.itemsize, has_add)
    nk = kk // tk
    dims = ((0 if ta else 1,), (1 if tb else 0,))

    def kern(*refs):
        a_ref, b_ref = refs[0], refs[1]
        rest = list(refs[2:])
        add_ref = rest.pop(0) if has_add else None
        bias_ref = rest.pop(0) if has_bias else None
        o_ref = rest[0]
        part = _dot(a_ref[...].astype(BF16), b_ref[...].astype(BF16), dims)

        def finish(r):
            if has_add:
                r = r + add_ref[...]
            if has_bias:
                r = r + bias_ref[...]
            o_ref[...] = r.astype(out_dtype)

        if nk == 1:
            finish(part)
        else:
            acc_ref = rest[1]
            k = pl.program_id(2)

            @pl.when(k == 0)
            def _():
                acc_ref[...] = part

            @pl.when((k > 0) & (k < nk - 1))
            def _():
                acc_ref[...] += part

            @pl.when(k == nk - 1)
            def _():
                finish(acc_ref[...] + part)

    if a_chunks:
        per_a = wa // tk
        a_spec = pl.BlockSpec((None, tm, tk), lambda i, j, k: (k // per_a, i, k % per_a))
    elif ta:
        a_spec = pl.BlockSpec((tk, tm), lambda i, j, k: (k, i))
    else:
        a_spec = pl.BlockSpec((tm, tk), lambda i, j, k: (i, k))
    if b_chunks and tb:
        per = wide // tk
        b_spec = pl.BlockSpec((None, tn, tk), lambda i, j, k: (k // per, j, k % per))
    elif b_chunks:
        per = wide // tn
        b_spec = pl.BlockSpec((None, tk, tn), lambda i, j, k: (j // per, k, j % per))
    elif tb:
        b_spec = pl.BlockSpec((tn, tk), lambda i, j, k: (j, k))
    else:
        b_spec = pl.BlockSpec((tk, tn), lambda i, j, k: (k, j))
    if out_chunks:
        per_o = wide_o // tn
        out_spec = pl.BlockSpec((None, tm, tn), lambda i, j, k: (j // per_o, i, j % per_o))
        out_shape = jax.ShapeDtypeStruct((N_CHIPS, m, wide_o), out_dtype)
    else:
        out_spec = pl.BlockSpec((tm, tn), lambda i, j, k: (i, j))
        out_shape = jax.ShapeDtypeStruct((m, n), out_dtype)
    in_specs, args = [a_spec, b_spec], [a, b]
    if has_add:
        in_specs.append(pl.BlockSpec((tm, tn), lambda i, j, k: (i, j)))
        args.append(add)
    if has_bias:
        in_specs.append(pl.BlockSpec((1, tn), lambda i, j, k: (0, j)))
        args.append(bias)
    (out,), side_outs = side_call(
        kern, side,
        name=name,
        grid=(m // tm, n // tn, nk),
        in_specs=in_specs,
        out_specs=[out_spec],
        out_shape=[out_shape],
        scratch_shapes=[pltpu.VMEM((tm, tn), F32)] if nk > 1 else [],
        args=args)
    return out if side is None else (out, side_outs)


def mm_hooked(plan, a, b, *, name, **kw):
    side = plan.side(name)
    if side is None:
        return mm(a, b, name=name, **kw)
    out, side_outs = mm(a, b, name=name, side=side, **kw)
    plan.done(name, side_outs)
    return out


def rowwise(fn, ins, outs, accs=(), *, tr, name):
    rows = [a for a, kind in ins if kind == "row"][0].shape[0]
    tr = min(tr, rows)
    assert rows % tr == 0 and tr % SUBLANES == 0, (rows, tr)
    n = rows // tr
    n_in, n_out = len(ins), len(outs)
    kinds = [kind for _, kind in ins]

    def kern(*refs):
        i = pl.program_id(0)
        vals = []
        for ref, kind in zip(refs[:n_in], kinds):
            v = ref[...]
            if kind == "prev":
                v = v * (i > 0).astype(v.dtype)
            elif kind == "next":
                v = v * (i < n - 1).astype(v.dtype)
            vals.append(v)
        res = fn(*vals)
        for ref, r in zip(refs[n_in:n_in + n_out], res[:n_out]):
            ref[...] = r.astype(ref.dtype)
        if accs:
            acc_refs = refs[n_in + n_out:]

            @pl.when(i == 0)
            def _():
                for ref in acc_refs:
                    ref[...] = jnp.zeros_like(ref)

            for ref, r in zip(acc_refs, res[n_out:]):
                ref[...] += r

    in_specs = []
    for a, kind in ins:
        if kind == "row":
            in_specs.append(pl.BlockSpec((tr, a.shape[1]), lambda i: (i, 0)))
        elif kind == "full":
            in_specs.append(pl.BlockSpec(a.shape, lambda i, nd=a.ndim: (0,) * nd))
        elif kind == "prev":
            in_specs.append(pl.BlockSpec((SUBLANES, a.shape[1]),
                                         lambda i: (jnp.maximum(i * (tr // SUBLANES) - 1, 0), 0)))
        else:
            in_specs.append(pl.BlockSpec((SUBLANES, a.shape[1]),
                                         lambda i: (jnp.minimum((i + 1) * (tr // SUBLANES), rows // SUBLANES - 1), 0)))
    out_specs = [pl.BlockSpec((tr, c), lambda i: (i, 0)) for c, _ in outs]
    out_specs += [pl.BlockSpec((r, c), lambda i: (0, 0)) for r, c in accs]
    out_shape = [jax.ShapeDtypeStruct((rows, c), dt) for c, dt in outs]
    out_shape += [jax.ShapeDtypeStruct((r, c), F32) for r, c in accs]
    res = pl.pallas_call(
        kern,
        name=name,
        grid=(n,),
        in_specs=in_specs,
        out_specs=out_specs,
        out_shape=out_shape,
        compiler_params=_params(("arbitrary",) if accs else ("parallel",)),
    )(*[a for a, _ in ins])
    return res


def rms_fwd(x, g, name):
    def fn(xv, gv):
        r = lax.rsqrt(jnp.mean(xv * xv, axis=1, keepdims=True) + EPS)
        return (xv * r * gv,)

    return rowwise(fn, [(x, "row"), (g, "full")], [(x.shape[1], BF16)], tr=1024, name=name)[0]


def rms_bwd(x, g, dy, dres, name):
    def fn(xv, gv, dyv, drv):
        r = lax.rsqrt(jnp.mean(xv * xv, axis=1, keepdims=True) + EPS)
        xh = xv * r
        dyg = dyv * gv
        dx = drv + r * (dyg - xh * jnp.mean(dyg * xh, axis=1, keepdims=True))
        return dx, dx, _colsum(dyv * xh)

    c = x.shape[1]
    dx, dxb, dg = rowwise(fn, [(x, "row"), (g, "full"), (dy, "row"), (dres, "row")], [(c, F32), (c, BF16)], [(1, c)],
                          tr=512, name=name)
    return (dx, dxb), dg


def ffn_up(h, wgu, name, side=None):
    s, d = h.shape
    nch, _, w = wgu.shape
    half = nch // 2
    tm = _pick(s, (512, 256, 128))

    def kern(h_ref, wg_ref, wu_ref, gu_ref, a_ref):
        hv = h_ref[...]
        g = _dot(hv, wg_ref[...])
        u = _dot(hv, wu_ref[...])
        gu_ref[0] = g.astype(BF16)
        gu_ref[1] = u.astype(BF16)
        a_ref[...] = (g * _sigmoid(g) * u).astype(BF16)

    return side_call(
        kern, side, name=name, grid=(s // tm, half),
        in_specs=[pl.BlockSpec((tm, d), lambda i, j: (i, 0)),
                  pl.BlockSpec((None, d, w), lambda i, j: (j, 0, 0)),
                  pl.BlockSpec((None, d, w), lambda i, j: (j + half, 0, 0))],
        out_specs=[pl.BlockSpec((2, tm, w), lambda i, j: (0, i, j)), pl.BlockSpec((tm, w), lambda i, j: (i, j))],
        out_shape=[jax.ShapeDtypeStruct((2, s, half * w), BF16), jax.ShapeDtypeStruct((s, half * w), BF16)],
        scratch_shapes=[], args=(h, wgu, wgu))


def ffn_dact(dxb, wdown, gu, name):
    s, d = dxb.shape
    hid = wdown.shape[0]
    tm = _pick(s, (512, 256, 128))
    tn = _pick(hid, (1408, 512, 256, 128))

    def kern(dx_ref, w_ref, gu_ref, o_ref):
        da = _dot_nt(dx_ref[...], w_ref[...])
        g, u = gu_ref[0].astype(F32), gu_ref[1].astype(F32)
        sg = _sigmoid(g)
        o_ref[0] = (da * u * sg * (1.0 + g * (1.0 - sg))).astype(BF16)
        o_ref[1] = (da * g * sg).astype(BF16)

    return pl.pallas_call(
        kern, name=name, grid=(s // tm, hid // tn),
        in_specs=[pl.BlockSpec((tm, d), lambda i, j: (i, 0)), pl.BlockSpec((tn, d), lambda i, j: (j, 0)),
                  pl.BlockSpec((2, tm, tn), lambda i, j: (0, i, j))],
        out_specs=pl.BlockSpec((2, tm, tn), lambda i, j: (0, i, j)),
        out_shape=jax.ShapeDtypeStruct((2, s, hid), BF16),
        compiler_params=_params(("parallel", "parallel")),
    )(dxb, wdown, gu)


def loss_and_grad(y, t, name):
    d = y.shape[1]

    def fn(yv, tv):
        e = yv - tv
        part = jnp.sum(_colsum(e * e), axis=1, keepdims=True) * (0.5 / d)
        dy = e * (1.0 / d)
        return dy, dy, jnp.broadcast_to(part, (SUBLANES, LANES))

    dy, dyb, acc = rowwise(fn, [(y, "row"), (t, "row")], [(d, F32), (d, BF16)], [(SUBLANES, LANES)], tr=1024, name=name)
    return acc[0, 0], (dy, dyb)


def adamw(w, g, m, v, name):
    def fn(wv, gv, mv, vv):
        m2 = ADAM_B1 * mv + (1.0 - ADAM_B1) * gv
        v2 = ADAM_B2 * vv + (1.0 - ADAM_B2) * (gv * gv)
        m_hat = m2 / (1.0 - ADAM_B1 ** ADAM_STEP)
        v_hat = v2 / (1.0 - ADAM_B2 ** ADAM_STEP)
        delta = -ADAM_LR * (m_hat / (jnp.sqrt(v_hat) + ADAM_EPS) + ADAM_WD * wv)
        return delta, m2, v2, gv

    rows, c = w.shape
    tr = _pick(rows, (512, 256, 128, 64, 32, 16, 8)) if rows % SUBLANES == 0 else rows
    if rows % SUBLANES:
        return _whole(fn, [w, g, m, v], [(w.shape, F32)] * 4, name=name)
    if 2 * 8 * tr * c * 4 > MM_VMEM_BUDGET:
        tr //= 2
    return rowwise(fn, [(w, "row"), (g, "row"), (m, "row"), (v, "row")], [(c, F32)] * 4, tr=tr, name=name)


def _whole(fn, ins, outs, *, name):
    n_in = len(ins)

    def kern(*refs):
        res = fn(*[r[...] for r in refs[:n_in]])
        for ref, r in zip(refs[n_in:], res):
            ref[...] = r.astype(ref.dtype)

    return pl.pallas_call(
        kern,
        name=name,
        out_shape=[jax.ShapeDtypeStruct(s, dt) for s, dt in outs],
        compiler_params=pltpu.CompilerParams(vmem_limit_bytes=VMEM_LIMIT),
    )(*ins)


def ffn_fwd(x, g, wgu, wdown, tag, plan):
    h = rms_fwd(x, g, f"ffn_rms_{tag}")
    gu, a = _hooked(plan, f"ffn_gu_{tag}", ffn_up, h, wgu)
    xn = mm_hooked(plan, a, wdown, add=x, name=f"ffn_down_{tag}")
    return xn, (x, h, gu, a)


def ffn_bwd(dxn, saved, g, wgu, wdown, tag):
    x, h, gu, a = saved
    dxn, dxb = dxn
    dwdown = mm(a, dxb, ta=True, out_dtype=BF16, name=f"ffn_dwdown_{tag}")
    dgu = ffn_dact(dxb, wdown, gu, f"ffn_dact_{tag}")
    dh = mm(dgu, wgu, tb=True, a_chunks=True, b_chunks=True, name=f"ffn_dh_{tag}")
    dwgu = mm(h, dgu, ta=True, b_chunks=True, out_dtype=BF16, out_chunks=True, name=f"ffn_dwgu_{tag}")
    dx, dg = rms_bwd(x, g, dh, dxn, f"ffn_drms_{tag}")
    return dx, dg, dwgu, dwdown


def _head_blockdiag(c):
    i = jnp.arange(c) // HEAD
    return (i[:, None] == i[None, :]).astype(BF16)


def _head_sums(x, bd):
    return jnp.concatenate([_dot_x2(x[:, g * LANES:(g + 1) * LANES], bd) for g in range(x.shape[1] // LANES)], axis=1)


def qknorm_fwd(qkv, qg, kg, bd, name):
    d = qkv.shape[1] // 3
    scale = 1.0 / math.sqrt(HEAD)

    def fn(v, qgv, kgv, bdv):
        v = v.astype(F32)
        q, k, vv = v[:, :d], v[:, d:2 * d], v[:, 2 * d:]
        rq = lax.rsqrt(_head_sums(q * q, bdv) * (1.0 / HEAD) + EPS)
        rk = lax.rsqrt(_head_sums(k * k, bdv) * (1.0 / HEAD) + EPS)
        return q * rq * qgv * scale, k * rk * kgv, vv

    return rowwise(fn, [(qkv, "row"), (qg, "full"), (kg, "full"), (bd, "full")],
                   [(d, BF16), (d, BF16), (d, BF16)], tr=512, name=name)


def qknorm_bwd(qkv, dqs, dkn, dv, qg, kg, bd, name):
    d = qkv.shape[1] // 3
    scale = 1.0 / math.sqrt(HEAD)

    def one(xv, gv, dyv, bdv):
        r = lax.rsqrt(_head_sums(xv * xv, bdv) * (1.0 / HEAD) + EPS)
        xh = xv * r
        dyg = dyv * gv
        dx = r * (dyg - xh * (_head_sums(dyg * xh, bdv) * (1.0 / HEAD)))
        return dx, _colsum(dyv * xh)

    def fn(v, dqv, dkv, dvv, qgv, kgv, bdv):
        v = v.astype(F32)
        q, k = v[:, :d], v[:, d:2 * d]
        dq, dqg = one(q, qgv, dqv * scale, bdv)
        dk, dkg = one(k, kgv, dkv, bdv)
        return jnp.concatenate([dq, dk, dvv], axis=1), dqg, dkg

    return rowwise(fn, [(qkv, "row"), (dqs, "row"), (dkn, "row"), (dv, "row"), (qg, "full"), (kg, "full"), (bd, "full")],
                   [(3 * d, BF16)], [(1, d), (1, d)], tr=512, name=name)


def _sb_tile(qh, k, mask, tri_gt):
    z = _dot_nt(qh, k)
    sp = jnp.log(1.0 + jnp.exp(-jnp.abs(z)))
    lb = jnp.minimum(z, 0.0) - sp
    l1 = jnp.where(mask, lb - z, 0.0)
    suf = _dot(l1.astype(BF16), tri_gt)
    return lb, l1, suf


def _sb_tri(tk):
    i = jnp.arange(tk)
    return jnp.stack([i[:, None] > i[None, :], i[:, None] < i[None, :]]).astype(BF16)


def _sb_setup(tq, tk):
    row, col = _iota2((tq, tk), 0), _iota2((tq, tk), 1)
    lane = _iota2((1, LANES), 1)
    halves = [(lane < HEAD).astype(BF16), (lane >= HEAD).astype(BF16)]
    lane_q = _iota2((tq, LANES), 1) + jnp.minimum(_iota2((tq, LANES), 0), 0)
    return row, col, halves, lane_q


def sb_attn_fwd(qs, kn, vb, tri, name, side=None):
    s, d = qs.shape
    tq, tk = min(SB_TQ, s), min(SB_TK, s)
    nq = s // tq
    assert s // tk <= LANES and s % tq == 0 and s % tk == 0

    def kern(q_ref, k_ref, v_ref, tri_ref, o_ref, rs_ref, acc_ref):
        i = pl.program_id(1)
        row, col, halves, lane_q = _sb_setup(tq, tk)
        q = q_ref[...]
        qh = [q * hm for hm in halves]
        acc_ref[...] = jnp.zeros_like(acc_ref)
        rs_ref[...] = jnp.full(rs_ref.shape, SB_UNSEEN, F32)
        nkb = (i + 1) * (tq // tk)

        def more(st):
            return (st[0] < nkb) & (st[1] > SB_DEAD)

        def step(st):
            n, r = st[0], list(st[2:])
            kb = nkb - 1 - n
            ks = pl.multiple_of(kb * tk, tk)
            k = k_ref[pl.ds(ks, tk), :]
            v = v_ref[pl.ds(ks, tk), :]
            mask = col < row + (i * tq - kb * tk)
            at_kb = lane_q == kb
            for hh in range(2):
                lb, l1, suf = _sb_tile(qh[hh], k, mask, tri_ref[0])
                w = jnp.where(mask, jnp.exp(lb + suf + r[hh]), 0.0)
                acc_ref[...] += _dot(w.astype(BF16), v * halves[hh])
                rs_ref[hh] = jnp.where(at_kb, r[hh], rs_ref[hh])
                r[hh] = r[hh] + _rowsum(l1)
            return (n + 1, jnp.maximum(jnp.max(r[0]), jnp.max(r[1])), r[0], r[1])

        z1 = jnp.zeros((tq, 1), F32)
        lax.while_loop(more, step, (jnp.int32(0), jnp.float32(0.0), z1, z1))
        o_ref[...] = acc_ref[...].astype(BF16)

    nh2 = d // LANES
    return side_call(
        kern, side,
        name=name,
        grid=(nh2, nq),
        in_specs=[pl.BlockSpec((tq, LANES), lambda h, i: (i, h)),
                  pl.BlockSpec((s, LANES), lambda h, i: (0, h)),
                  pl.BlockSpec((s, LANES), lambda h, i: (0, h)),
                  pl.BlockSpec((2, tk, tk), lambda h, i: (0, 0, 0))],
        out_specs=[pl.BlockSpec((tq, LANES), lambda h, i: (i, h)),
                   pl.BlockSpec((None, 2, tq, LANES), lambda h, i: (h, 0, i, 0))],
        out_shape=[jax.ShapeDtypeStruct((s, d), BF16), jax.ShapeDtypeStruct((nh2, 2, s, LANES), F32)],
        scratch_shapes=[pltpu.VMEM((tq, LANES), F32)],
        args=(qs, kn, vb, tri))


def sb_attn_bwd(qs, kn, vb, rsave, do, tri, name, side=None):
    s, d = qs.shape
    tq, tk = min(SB_TQ, s), min(SB_TK, s)
    nq = s // tq

    def kern(q_ref, k_ref, v_ref, rs_ref, do_ref, tri_ref, dq_ref, dk_ref, dv_ref):
        i = pl.program_id(1)

        @pl.when(i == 0)
        def _():
            dk_ref[...] = jnp.zeros_like(dk_ref)
            dv_ref[...] = jnp.zeros_like(dv_ref)

        row, col, halves, lane_q = _sb_setup(tq, tk)
        q = q_ref[...]
        qh = [q * hm for hm in halves]
        dov = do_ref[...].astype(BF16)
        doh = [dov * hm for hm in halves]
        dq_ref[...] = jnp.zeros_like(dq_ref)
        nkb = (i + 1) * (tq // tk)
        top = jnp.maximum(jnp.max(rs_ref[0], axis=0, keepdims=True), jnp.max(rs_ref[1], axis=0, keepdims=True))
        dead = (top <= SB_DEAD) & (_iota2((1, LANES), 1) < nkb)
        kstart = jnp.minimum(jnp.sum(dead.astype(F32)).astype(jnp.int32), nkb)

        def step(kb, ep):
            ep = list(ep)
            ks = pl.multiple_of(kb * tk, tk)
            k = k_ref[pl.ds(ks, tk), :]
            v = v_ref[pl.ds(ks, tk), :]
            mask = col < row + (i * tq - kb * tk)
            at_kb = lane_q == kb
            for hh in range(2):
                lb, l1, suf = _sb_tile(qh[hh], k, mask, tri_ref[0])
                r = _rowsum(jnp.where(at_kb, rs_ref[hh], 0.0))
                lbm = jnp.where(mask, lb, SB_UNSEEN)
                w = jnp.exp(lbm + suf + r)
                e = _dot_nt(doh[hh], v) * w
                pe = ep[hh] + _dot(e.astype(BF16), tri_ref[1])
                beta = jnp.exp(lbm)
                dz = (e - beta * (e + pe)).astype(BF16)
                dq_ref[...] += _dot(dz, k * halves[hh])
                dk_ref[pl.ds(ks, tk), :] += _dot_tn(dz, qh[hh])
                dv_ref[pl.ds(ks, tk), :] += _dot_tn(w.astype(BF16), doh[hh])
                ep[hh] = ep[hh] + _rowsum(e)
            return tuple(ep)

        z1 = jnp.zeros((tq, 1), F32)
        lax.fori_loop(kstart, nkb, step, (z1, z1))

    nh2 = d // LANES
    return side_call(
        kern, side,
        name=name,
        grid=(nh2, nq),
        in_specs=[pl.BlockSpec((tq, LANES), lambda h, i: (i, h)),
                  pl.BlockSpec((s, LANES), lambda h, i: (0, h)),
                  pl.BlockSpec((s, LANES), lambda h, i: (0, h)),
                  pl.BlockSpec((None, 2, tq, LANES), lambda h, i: (h, 0, i, 0)),
                  pl.BlockSpec((tq, LANES), lambda h, i: (i, h)),
                  pl.BlockSpec((2, tk, tk), lambda h, i: (0, 0, 0))],
        out_specs=[pl.BlockSpec((tq, LANES), lambda h, i: (i, h)),
                   pl.BlockSpec((s, LANES), lambda h, i: (0, h)),
                   pl.BlockSpec((s, LANES), lambda h, i: (0, h))],
        out_shape=[jax.ShapeDtypeStruct((s, d), F32)] * 3,
        scratch_shapes=[],
        args=(qs, kn, vb, rsave, do, tri))


def _hooked(plan, tag, call, *args):
    side = plan.side(tag)
    outs, side_outs = call(*args, tag, side)
    if side is not None:
        plan.done(tag, side_outs)
    return outs


def sb_fwd(x, g, wqkv, qg, kg, wo, bd, tag, plan):
    h = rms_fwd(x, g, f"sb_rms_{tag}")
    qkv = mm(h, wqkv, b_chunks=True, out_dtype=BF16, name=f"sb_qkv_{tag}")
    qs, kn, vb = qknorm_fwd(qkv, qg, kg, bd, f"sb_qknorm_{tag}")
    o, rsave = _hooked(plan, f"sb_attn_{tag}", sb_attn_fwd, qs, kn, vb, _sb_tri(min(SB_TK, x.shape[0])))
    xn = mm(o, wo(), add=x, name=f"sb_out_{tag}")
    return xn, (x, h, qkv, qs, kn, vb, rsave, o)


def sb_bwd(dxn, saved, g, wqkv, qg, kg, wo, bd, tag, plan):
    x, h, qkv, qs, kn, vb, rsave, o = saved
    dxn, dxb = dxn
    do = mm(dxb, wo, tb=True, name=f"sb_do_{tag}")
    dwo = mm(o, dxb, ta=True, out_dtype=BF16, name=f"sb_dwo_{tag}")
    dqs, dkn, dv = _hooked(plan, f"sb_dattn_{tag}", sb_attn_bwd, qs, kn, vb, rsave, do, _sb_tri(min(SB_TK, x.shape[0])))
    dqkv, dqg, dkg = qknorm_bwd(qkv, dqs, dkn, dv, qg, kg, bd, f"sb_dqknorm_{tag}")
    dh = mm(dqkv, wqkv, tb=True, b_chunks=True, name=f"sb_dh_{tag}")
    dwqkv = mm(h, dqkv, ta=True, out_dtype=BF16, out_chunks=True, name=f"sb_dwqkv_{tag}")
    dx, dg = rms_bwd(x, g, dh, dxn, f"sb_drms_{tag}")
    nh = dqg.shape[1] // HEAD
    return dx, dg, dwqkv, dqg.reshape(nh, HEAD).sum(0), dkg.reshape(nh, HEAD).sum(0), dwo


def _gelu(x):
    return 0.5 * x * (1.0 + lax.erf(x * (1.0 / math.sqrt(2.0))))


def _gelu_grad(x):
    return 0.5 * (1.0 + lax.erf(x * (1.0 / math.sqrt(2.0)))) + x * jnp.exp(-0.5 * x * x) * (1.0 / math.sqrt(2.0 * math.pi))


def gm_act_fwd(pre, vg, name):
    half = pre.shape[1] // 2

    def fn(p, vgv):
        p = p.astype(F32)
        u = _gelu(p[:, :half])
        v = _gelu(p[:, half:])
        r = lax.rsqrt(jnp.mean(v * v, axis=1, keepdims=True) + EPS)
        return u, v * r * vgv

    return rowwise(fn, [(pre, "row"), (vg, "full")], [(half, F32), (half, BF16)], tr=512, name=name)


def gm_act_bwd(pre, du, dvn, vg, name):
    half = pre.shape[1] // 2

    def fn(p, duv, dvnv, vgv):
        p = p.astype(F32)
        pu, pv = p[:, :half], p[:, half:]
        v = _gelu(pv)
        r = lax.rsqrt(jnp.mean(v * v, axis=1, keepdims=True) + EPS)
        vh = v * r
        dyg = dvnv * vgv
        dv = r * (dyg - vh * jnp.mean(dyg * vh, axis=1, keepdims=True))
        dpre = jnp.concatenate([duv * _gelu_grad(pu), dv * _gelu_grad(pv)], axis=1)
        return dpre, _colsum(dvnv * vh), _colsum(dpre)

    return rowwise(fn, [(pre, "row"), (du, "row"), (dvn, "row"), (vg, "full")],
                   [(2 * half, BF16)], [(1, half), (1, 2 * half)], tr=256, name=name)


def gm_spatial_fwd(u, vn, wc, bst, name):
    s, c = u.shape
    t = CHUNK
    ng = c // LANES

    def kern(u_ref, v_ref, w_ref, b_ref, o_ref):
        for g in range(ng):
            sl = slice(g * LANES, (g + 1) * LANES)
            mixed = _dot(w_ref[g], v_ref[:, sl]) + b_ref[:, sl]
            o_ref[:, sl] = (u_ref[:, sl] * mixed).astype(BF16)

    return pl.pallas_call(
        kern,
        name=name,
        grid=(s // t,),
        in_specs=[pl.BlockSpec((t, c), lambda i: (i, 0)), pl.BlockSpec((t, c), lambda i: (i, 0)),
                  pl.BlockSpec(wc.shape, lambda i: (0, 0, 0)), pl.BlockSpec(bst.shape, lambda i: (0, 0))],
        out_specs=pl.BlockSpec((t, c), lambda i: (i, 0)),
        out_shape=jax.ShapeDtypeStruct((s, c), BF16),
        compiler_params=_params(("parallel",)),
    )(u, vn, wc, bst)


def gm_spatial_bwd(dgate, u, vn, wc, bst, name):
    s, c = u.shape
    t = CHUNK
    ng = c // LANES

    def kern(dg_ref, u_ref, v_ref, w_ref, b_ref, du_ref, dv_ref, dw_ref, db_ref):
        i = pl.program_id(0)

        @pl.when(i == 0)
        def _():
            dw_ref[...] = jnp.zeros_like(dw_ref)
            db_ref[...] = jnp.zeros_like(db_ref)

        for g in range(ng):
            sl = slice(g * LANES, (g + 1) * LANES)
            vg = v_ref[:, sl]
            dgv = dg_ref[:, sl]
            mixed = _dot(w_ref[g], vg) + b_ref[:, sl]
            du_ref[:, sl] = dgv * mixed
            dmix = dgv * u_ref[:, sl]
            dmb = dmix.astype(BF16)
            dv_ref[:, sl] = _dot_tn(w_ref[g], dmb)
            dw_ref[g] += _dot_nt(dmb, vg)
            db_ref[:, sl] += dmix

    return pl.pallas_call(
        kern,
        name=name,
        grid=(s // t,),
        in_specs=[pl.BlockSpec((t, c), lambda i: (i, 0))] * 3 +
                 [pl.BlockSpec(wc.shape, lambda i: (0, 0, 0)), pl.BlockSpec(bst.shape, lambda i: (0, 0))],
        out_specs=[pl.BlockSpec((t, c), lambda i: (i, 0)), pl.BlockSpec((t, c), lambda i: (i, 0)),
                   pl.BlockSpec(wc.shape, lambda i: (0, 0, 0)), pl.BlockSpec(bst.shape, lambda i: (0, 0))],
        out_shape=[jax.ShapeDtypeStruct((s, c), F32), jax.ShapeDtypeStruct((s, c), F32),
                   jax.ShapeDtypeStruct(wc.shape, F32), jax.ShapeDtypeStruct(bst.shape, F32)],
        compiler_params=_params(("arbitrary",)),
    )(dgate, u, vn, wc, bst)


def gm_fwd(x, g, w_in, b_in, vg, wc, bst, w_out, tag):
    h = rms_fwd(x, g, f"gm_rms_{tag}")
    pre = mm(h, w_in, bias=b_in, b_chunks=True, out_dtype=BF16, name=f"gm_in_{tag}")
    u, vn = gm_act_fwd(pre, vg, f"gm_act_{tag}")
    gate = gm_spatial_fwd(u, vn, wc, bst, f"gm_spatial_{tag}")
    xn = mm(gate, w_out, add=x, name=f"gm_out_{tag}")
    return xn, (x, h, pre, u, vn, gate)


def gm_bwd(dxn, saved, g, w_in, vg, wc, bst, w_out, tag):
    x, h, pre, u, vn, gate = saved
    dxn, dxb = dxn
    dgate = mm(dxb, w_out, tb=True, name=f"gm_dgate_{tag}")
    dwout = mm(gate, dxb, ta=True, out_dtype=BF16, name=f"gm_dwout_{tag}")
    du, dvn, dws, dbst = gm_spatial_bwd(dgate, u, vn, wc, bst, f"gm_dspatial_{tag}")
    dpre, dvg, dbin = gm_act_bwd(pre, du, dvn, vg, f"gm_dact_{tag}")
    dh = mm(dpre, w_in, tb=True, b_chunks=True, name=f"gm_dh_{tag}")
    dwin = mm(h, dpre, ta=True, out_dtype=BF16, out_chunks=True, name=f"gm_dwin_{tag}")
    dx, dg = rms_bwd(x, g, dh, dxn, f"gm_drms_{tag}")
    ng = wc.shape[0]
    dws = jnp.where(jnp.tril(jnp.ones((CHUNK, CHUNK), bool)), dws, 0.0)
    dbs = dbst.reshape(CHUNK, ng, LANES).sum(-1).T
    return dx, dg, dwin, dbin, dvg, dws, dbs, dwout


def _conv_taps(xv, prev):
    cat = jnp.concatenate([prev, xv], axis=0)
    return [pltpu.roll(cat, sh, 0)[SUBLANES:] for sh in (3, 2, 1)] + [xv]


def conv_fwd(xbc, ws, b, d_inner, name):
    c = xbc.shape[1]
    nst = (c - d_inner) // 2

    def fn(xv, prev, w0, w1, w2, w3, bv):
        taps = _conv_taps(xv, prev)
        pre = bv + w0 * taps[0] + w1 * taps[1] + w2 * taps[2] + w3 * taps[3]
        out = pre * _sigmoid(pre)
        return out[:, :d_inner], out[:, d_inner:d_inner + nst], out[:, d_inner + nst:]

    return rowwise(fn, [(xbc, "row"), (xbc, "prev")] + [(w, "full") for w in ws] + [(b, "full")],
                   [(d_inner, F32), (nst, F32), (nst, F32)], tr=512, name=name)


def conv_bwd_pre(xbc, ws, b, dxs_a, dxs_b, db_m, dc_m, name):
    c = xbc.shape[1]

    def fn(xv, prev, w0, w1, w2, w3, bv, da, db2, dbm, dcm):
        taps = _conv_taps(xv, prev)
        pre = bv + w0 * taps[0] + w1 * taps[1] + w2 * taps[2] + w3 * taps[3]
        sg = _sigmoid(pre)
        dout = jnp.concatenate([da + db2, dbm, dcm], axis=1)
        dpre = dout * sg * (1.0 + pre * (1.0 - sg))
        return (dpre,) + tuple(_colsum(dpre * tp) for tp in taps) + (_colsum(dpre),)

    return rowwise(fn, [(xbc, "row"), (xbc, "prev")] + [(w, "full") for w in ws] +
                   [(b, "full"), (dxs_a, "row"), (dxs_b, "row"), (db_m, "row"), (dc_m, "row")],
                   [(c, F32)], [(1, c)] * 5, tr=256, name=name)


def conv_bwd_in(dpre, ws, name):
    c = dpre.shape[1]

    def fn(dv, nxt, w0, w1, w2, w3):
        cat = jnp.concatenate([dv, nxt], axis=0)
        n = cat.shape[0]
        up = [pltpu.roll(cat, n - sh, 0)[:dv.shape[0]] for sh in (1, 2, 3)]
        return (w3 * dv + w2 * up[0] + w1 * up[1] + w0 * up[2],)

    return rowwise(fn, [(dpre, "row"), (dpre, "next")] + [(w, "full") for w in ws], [(c, BF16)], tr=512, name=name)[0]


def ssd_pre(dtr, bias, alog, name):
    per = _pick(dtr.shape[0] // CHUNK, (4, 2, 1))

    def fn(d, bv, al, tri):
        dt = _softplus(d + bv)
        a = dt * (-jnp.exp(al))
        cums = [_dot_x3_left(tri, a[k * CHUNK:(k + 1) * CHUNK]) for k in range(per)]
        return dt, jnp.concatenate(cums, axis=0)

    tri = jnp.tril(jnp.ones((CHUNK, CHUNK), BF16))
    return rowwise(fn, [(dtr, "row"), (bias, "full"), (alog, "full"), (tri, "full")],
                   [(LANES, F32), (LANES, F32)], tr=per * CHUNK, name=name)


def _ssd_layouts(v, ngroups, hpg):
    s = v.shape[0]
    col = v[:, :ngroups * hpg].T.reshape(ngroups, hpg, s, 1)
    return jnp.broadcast_to(col, (ngroups, hpg, s, LANES))


def _ssd_rowform(acum, ngroups, hpg):
    s = acum.shape[0]
    nc = s // CHUNK
    a = acum[:, :ngroups * hpg].reshape(nc, CHUNK, ngroups, hpg).transpose(2, 0, 3, 1)
    last = jnp.broadcast_to(a[..., CHUNK - 1:], a.shape)
    return jnp.concatenate([a, last], axis=2)


def ssd_chunk_fwd(xs, bm, cm, col_a, col_dt, rowf, name, side=None):
    s, d_inner = xs.shape
    ln = CHUNK
    nc = s // ln
    nsub = _pick(nc, (SSD_SUB, 2, 1))
    rows = nsub * ln
    ng, hpg = col_a.shape[0], col_a.shape[1]
    gw = d_inner // ng
    assert gw == hpg * HEAD and gw % LANES == 0 and bm.shape[1] == ng * LANES

    def kern(x_ref, b_ref, c_ref, ca_ref, cd_ref, rf_ref, y_ref, hp_ref, h_scr):
        @pl.when(pl.program_id(1) == 0)
        def _():
            h_scr[...] = jnp.zeros_like(h_scr)

        causal = _iota2((ln, ln), 0) >= _iota2((ln, ln), 1)
        lane = _iota2((1, LANES), 1)
        for sc in range(nsub):
            rs = slice(sc * ln, (sc + 1) * ln)
            bb = b_ref[rs, :].astype(BF16)
            cbf = c_ref[rs, :].astype(BF16)
            cb = _dot_nt(cbf, bb)
            ys = [jnp.zeros((ln, LANES), F32) for _ in range(gw // LANES)]
            for r in range(hpg):
                j, hf = divmod(r, LANES // HEAD)
                mh = ((lane >= HEAD * hf) & (lane < HEAD * (hf + 1))).astype(F32)
                ac = ca_ref[r, rs, :]
                ar = rf_ref[sc, pl.ds(r, 1), :]
                aend = rf_ref[sc, pl.ds(4 + r, 1), :]
                dm = jnp.exp(jnp.minimum(ac - ar, 0.0))
                m = jnp.where(causal, cb * dm, 0.0).astype(BF16)
                xdt = x_ref[rs, j * LANES:(j + 1) * LANES] * cd_ref[r, rs, :] * mh
                h = h_scr[r]
                hp_ref[sc, r] = h.astype(BF16)
                ys[j] = ys[j] + _dot(m, xdt.astype(BF16)) + _dot_nt(cbf, h.astype(BF16)) * jnp.exp(ac)
                dte = jnp.exp(aend - ac)
                h_scr[r] = jnp.exp(aend) * h + _dot_tn((xdt * dte).astype(BF16), bb)
            for j in range(gw // LANES):
                y_ref[rs, j * LANES:(j + 1) * LANES] = ys[j]

    colspec = pl.BlockSpec((None, hpg, rows, LANES), lambda g, c: (g, 0, c, 0))
    return side_call(
        kern, side,
        name=name,
        grid=(ng, nc // nsub),
        in_specs=[pl.BlockSpec((rows, gw), lambda g, c: (c, g)),
                  pl.BlockSpec((rows, LANES), lambda g, c: (c, g)),
                  pl.BlockSpec((rows, LANES), lambda g, c: (c, g)),
                  colspec, colspec,
                  pl.BlockSpec((None, nsub, 8, LANES), lambda g, c: (g, c, 0, 0))],
        out_specs=[pl.BlockSpec((rows, gw), lambda g, c: (c, g)),
                   pl.BlockSpec((None, nsub, hpg, LANES, LANES), lambda g, c: (g, c, 0, 0, 0))],
        out_shape=[jax.ShapeDtypeStruct((s, d_inner), F32),
                   jax.ShapeDtypeStruct((ng, nc, hpg, LANES, LANES), BF16)],
        scratch_shapes=[pltpu.VMEM((hpg, LANES, LANES), F32)],
        args=(xs, bm, cm, col_a, col_dt, rowf))


def ssd_chunk_bwd(xs, bm, cm, col_a, col_dt, rowf, hprev, dy, name, side=None):
    s, d_inner = xs.shape
    ln = CHUNK
    nc = s // ln
    nsub = _pick(nc, (SSD_SUB, 2, 1))
    rows = nsub * ln
    ng, hpg = col_a.shape[0], col_a.shape[1]
    gw = d_inner // ng

    def kern(x_ref, b_ref, c_ref, ca_ref, cd_ref, rf_ref, hp_ref, dy_ref,
             dx_ref, db_ref, dc_ref, ddt_ref, da_ref, dh_scr):
        @pl.when(pl.program_id(1) == 0)
        def _():
            dh_scr[...] = jnp.zeros_like(dh_scr)

        row, col = _iota2((ln, ln), 0), _iota2((ln, ln), 1)
        causal = row >= col
        tri_ge = (col >= row).astype(BF16)
        ones = jnp.ones((ln, LANES), BF16)
        lane = _iota2((1, LANES), 1)
        last_row = (_iota2((ln, 1), 0) == ln - 1).astype(F32)
        for sc in reversed(range(nsub)):
            rs = slice(sc * ln, (sc + 1) * ln)
            bb = b_ref[rs, :].astype(BF16)
            cbf = c_ref[rs, :].astype(BF16)
            cb = _dot_nt(cbf, bb)
            dcb = jnp.zeros((ln, ln), F32)
            d_b = jnp.zeros((ln, LANES), F32)
            d_c = jnp.zeros((ln, LANES), F32)
            dxs = [jnp.zeros((ln, LANES), F32) for _ in range(gw // LANES)]
            for r in range(hpg):
                j, hf = divmod(r, LANES // HEAD)
                mh = ((lane >= HEAD * hf) & (lane < HEAD * (hf + 1))).astype(F32)
                ac = ca_ref[r, rs, :]
                dt = cd_ref[r, rs, :]
                ar = rf_ref[sc, pl.ds(r, 1), :]
                aend = rf_ref[sc, pl.ds(4 + r, 1), :]
                dm = jnp.where(causal, jnp.exp(jnp.minimum(ac - ar, 0.0)), 0.0)
                m = cb * dm
                mb = m.astype(BF16)
                xp = x_ref[rs, j * LANES:(j + 1) * LANES]
                xdt = xp * dt * mh
                xdtb = xdt.astype(BF16)
                dyp = dy_ref[rs, j * LANES:(j + 1) * LANES] * mh
                dypb = dyp.astype(BF16)
                hb = hp_ref[sc, r]
                h = hb.astype(F32)
                dh = dh_scr[r]
                dhb = dh.astype(BF16)
                e_in = jnp.exp(ac)
                dte = jnp.exp(aend - ac)
                eend = jnp.exp(aend)
                d_m = _dot_nt(dypb, xdtb)
                dcb = dcb + d_m * dm
                gm = d_m * m
                yoff_pre = _dot_nt(cbf, hb)
                bdh = _dot_nt(bb, dhb)
                dxdt = _dot_tn(mb, dypb) + bdh * dte
                t1 = _rowsum(xdt * bdh) * dte
                gh, gl = _split2(gm)
                dacum = (_rowsum(gm) - (_dot_tn(gh, ones) + _dot_tn(gl, ones))
                         + _rowsum(dyp * yoff_pre) * e_in - t1)
                end_term = _colsum(t1) + eend * jnp.sum(_colsum(dh * h), axis=1, keepdims=True)
                dacum = dacum + last_row * end_term
                da_ref[r, rs, :] = _dot_x3_left(tri_ge, dacum)
                ddt_ref[r, rs, :] = jnp.broadcast_to(_rowsum(dxdt * xp), (ln, LANES))
                dxs[j] = dxs[j] + dxdt * dt
                d_b = d_b + _dot((xdt * dte).astype(BF16), dhb)
                dye = (dyp * e_in).astype(BF16)
                d_c = d_c + _dot(dye, hb)
                dh_scr[r] = eend * dh + _dot_tn(dye, cbf)
            dcbb = dcb.astype(BF16)
            dc_ref[rs, :] = d_c + _dot(dcbb, bb)
            db_ref[rs, :] = d_b + _dot_tn(dcbb, cbf)
            for j in range(gw // LANES):
                dx_ref[rs, j * LANES:(j + 1) * LANES] = dxs[j]

    rev = nc // nsub - 1
    colspec = pl.BlockSpec((None, hpg, rows, LANES), lambda g, c: (g, 0, rev - c, 0))
    return side_call(
        kern, side,
        name=name,
        grid=(ng, nc // nsub),
        in_specs=[pl.BlockSpec((rows, gw), lambda g, c: (rev - c, g)),
                  pl.BlockSpec((rows, LANES), lambda g, c: (rev - c, g)),
                  pl.BlockSpec((rows, LANES), lambda g, c: (rev - c, g)),
                  colspec, colspec,
                  pl.BlockSpec((None, nsub, 8, LANES), lambda g, c: (g, rev - c, 0, 0)),
                  pl.BlockSpec((None, nsub, hpg, LANES, LANES), lambda g, c: (g, rev - c, 0, 0, 0)),
                  pl.BlockSpec((rows, gw), lambda g, c: (rev - c, g))],
        out_specs=[pl.BlockSpec((rows, gw), lambda g, c: (rev - c, g)),
                   pl.BlockSpec((rows, LANES), lambda g, c: (rev - c, g)),
                   pl.BlockSpec((rows, LANES), lambda g, c: (rev - c, g)),
                   colspec, colspec],
        out_shape=[jax.ShapeDtypeStruct((s, d_inner), F32),
                   jax.ShapeDtypeStruct(bm.shape, F32), jax.ShapeDtypeStruct(cm.shape, F32),
                   jax.ShapeDtypeStruct(col_a.shape, F32), jax.ShapeDtypeStruct(col_a.shape, F32)],
        scratch_shapes=[pltpu.VMEM((hpg, LANES, LANES), F32)],
        args=(xs, bm, cm, col_a, col_dt, rowf, hprev, dy))


def gnorm_fwd(y, xs, z, dexp, gain, ngroups, name):
    c = y.shape[1]
    gw = c // ngroups

    def fn(yv, xv, zv, dv, gv):
        yg = (yv + xv * dv) * (zv * _sigmoid(zv))
        outs = []
        for k in range(ngroups):
            t = yg[:, k * gw:(k + 1) * gw]
            outs.append(t * lax.rsqrt(jnp.mean(t * t, axis=1, keepdims=True) + EPS))
        return (jnp.concatenate(outs, axis=1) * gv,)

    return rowwise(fn, [(y, "row"), (xs, "row"), (z, "row"), (dexp, "full"), (gain, "full")], [(c, BF16)], tr=512, name=name)[0]


def gnorm_bwd(dn, y, xs, z, dexp, gain, ngroups, name):
    c = y.shape[1]
    gw = c // ngroups

    def fn(dnv, yv, xv, zv, dv, gv):
        yd = yv + xv * dv
        sg = _sigmoid(zv)
        sz = zv * sg
        yg = yd * sz
        dng = dnv * gv
        dyg, yh = [], []
        for k in range(ngroups):
            sl = slice(k * gw, (k + 1) * gw)
            t = yg[:, sl]
            r = lax.rsqrt(jnp.mean(t * t, axis=1, keepdims=True) + EPS)
            th = t * r
            dyg.append(r * (dng[:, sl] - th * jnp.mean(dng[:, sl] * th, axis=1, keepdims=True)))
            yh.append(th)
        dyg = jnp.concatenate(dyg, axis=1)
        yh = jnp.concatenate(yh, axis=1)
        dyd = dyg * sz
        dz = dyg * yd * (sg * (1.0 + zv * (1.0 - sg)))
        return dyd, dyd * dv, dz, _colsum(dyd * xv), _colsum(dnv * yh)

    return rowwise(fn, [(dn, "row"), (y, "row"), (xs, "row"), (z, "row"), (dexp, "full"), (gain, "full")],
                   [(c, F32), (c, F32), (c, BF16)], [(1, c), (1, c)], tr=256, name=name)


def ssd_post(ddt, da, dt, dtr, bias, alog, name):
    def fn(ddtv, dav, dtv, dtrv, bv, al):
        a_neg = -jnp.exp(al)
        ddtr = (ddtv + dav * a_neg) * _sigmoid(dtrv + bv)
        return ddtr, _colsum(ddtr), _colsum(dav * dtv) * a_neg

    return rowwise(fn, [(ddt, "row"), (da, "row"), (dt, "row"), (dtr, "row"), (bias, "full"), (alog, "full")],
                   [(LANES, BF16)], [(1, LANES), (1, LANES)], tr=1024, name=name)


def _from_colform(v, s):
    ng, hpg = v.shape[0], v.shape[1]
    flat = v[..., 0].reshape(ng * hpg, s).T
    return jnp.pad(flat, ((0, 0), (0, LANES - ng * hpg)))


def ssm_fwd(x, g, p, tag, plan):
    ng, hpg, d_inner = p["ng"], p["hpg"], p["d_inner"]
    h = rms_fwd(x, g, f"ssm_rms_{tag}")
    z = mm(h, p["w_z"], name=f"ssm_inz_{tag}")
    xbc = mm(h, p["w_xbc"], name=f"ssm_inx_{tag}")
    dtr = mm(h, p["w_dt"], name=f"ssm_indt_{tag}")
    xs, bm, cm = conv_fwd(xbc, p["conv_w"], p["conv_b"], d_inner, f"ssm_conv_{tag}")
    dt, acum = ssd_pre(dtr, p["dt_bias"], p["a_log"], f"ssm_pre_{tag}")
    col_a, col_dt = _ssd_layouts(acum, ng, hpg), _ssd_layouts(dt, ng, hpg)
    rowf = _ssd_rowform(acum, ng, hpg)
    y, hprev = _hooked(plan, f"ssm_scan_{tag}", ssd_chunk_fwd, xs, bm, cm, col_a, col_dt, rowf)
    n = gnorm_fwd(y, xs, z, p["d_exp"], p["norm_gain"], ng, f"ssm_gnorm_{tag}")
    xn = mm(n, p["w_out"], add=x, name=f"ssm_out_{tag}")
    return xn, (x, h, z, xbc, dtr, xs, bm, cm, dt, col_a, col_dt, rowf, y, hprev, n)


def ssm_bwd(dxn, saved, g, p, tag, plan):
    x, h, z, xbc, dtr, xs, bm, cm, dt, col_a, col_dt, rowf, y, hprev, n = saved
    ng, hpg, d_inner = p["ng"], p["hpg"], p["d_inner"]
    s = x.shape[0]
    dxn, dxb = dxn
    dn = mm(dxb, p["w_out"], tb=True, name=f"ssm_dn_{tag}")
    dwout = mm(n, dxb, ta=True, out_dtype=BF16, name=f"ssm_dwout_{tag}")
    dy, dxs_skip, dz, dd_lane, dgain = gnorm_bwd(dn, y, xs, z, p["d_exp"], p["norm_gain"], ng, f"ssm_dgnorm_{tag}")
    dxs, dbm, dcm, ddt_c, da_c = _hooked(plan, f"ssm_dscan_{tag}", ssd_chunk_bwd, xs, bm, cm, col_a, col_dt, rowf, hprev, dy)
    ddtr, dbias, dalog = ssd_post(_from_colform(ddt_c, s), _from_colform(da_c, s), dt, dtr,
                                  p["dt_bias"], p["a_log"], f"ssm_post_{tag}")
    res = conv_bwd_pre(xbc, p["conv_w"], p["conv_b"], dxs, dxs_skip, dbm, dcm, f"ssm_dconv_{tag}")
    dpre, dconv_w, dconv_b = res[0], jnp.concatenate(res[1:5], axis=0), res[5]
    dxbc = conv_bwd_in(dpre, p["conv_w"], f"ssm_dconvin_{tag}")
    dh = mm(dz, p["w_z"], tb=True, name=f"ssm_dhz_{tag}")
    dh = mm(dxbc, p["w_xbc"], tb=True, add=dh, name=f"ssm_dhx_{tag}")
    dh = mm(ddtr, p["w_dt"], tb=True, add=dh, name=f"ssm_dhdt_{tag}")
    dwz = mm(h, dz, ta=True, out_dtype=BF16, name=f"ssm_dwz_{tag}")
    dwxbc = mm(h, dxbc, ta=True, out_dtype=BF16, name=f"ssm_dwxbc_{tag}")
    dwdt = mm(h, ddtr, ta=True, out_dtype=BF16, name=f"ssm_dwdt_{tag}")
    dx, dg = rms_bwd(x, g, dh, dxn, f"ssm_drms_{tag}")
    nh = ng * hpg
    dwin = jnp.concatenate([dwz, dwxbc, dwdt[:, :nh]], axis=1)
    dd = dd_lane.reshape(nh, HEAD).sum(-1)
    return dx, dg, dict(w_in=dwin, conv_w=dconv_w, conv_b=dconv_b, dt_bias=dbias[0, :nh], a_log=dalog[0, :nh],
                        d=dd, norm_gain=dgain, w_out=dwout)


def local_step(x, target, w, plan):
    d = x.shape[1]
    depth = w["mix_norm"].shape[0]
    bd = _head_blockdiag(LANES)
    tril = jnp.tril(jnp.ones((CHUNK, CHUNK), bool))
    ssm_heads = w["ssm_dt_bias"].shape[1]
    d_inner = w["ssm_norm_gain"].shape[1]
    ng = w["ssm_norm_gain"].shape[1] // 256
    nstate = CHUNK

    def pad_lanes(v):
        return jnp.pad(v, ((0, 0), (0, LANES - v.shape[1])))

    def ssm_params(j):
        w_in = w["ssm_w_in"][j]
        cw = w["ssm_conv_w"][j]
        return dict(ng=ng, hpg=ssm_heads // ng, d_inner=d_inner,
                    w_z=w_in[:, :d_inner], w_xbc=w_in[:, d_inner:d_inner + d_inner + 2 * ng * nstate],
                    w_dt=pad_lanes(w_in[:, 2 * d_inner + 2 * ng * nstate:]),
                    conv_w=[cw[k:k + 1] for k in range(cw.shape[0])], conv_b=w["ssm_conv_b"][j:j + 1],
                    dt_bias=pad_lanes(w["ssm_dt_bias"][j:j + 1]), a_log=pad_lanes(w["ssm_a_log"][j:j + 1]),
                    d_exp=jnp.repeat(w["ssm_d"][j], HEAD)[None, :], norm_gain=w["ssm_norm_gain"][j:j + 1],
                    w_out=w["ssm_w_out"][j])

    def gm_params(j):
        wc = jnp.where(tril, w["gm_w_s"][j], 0.0).astype(BF16)
        bst = jnp.repeat(w["gm_b_s"][j].T, LANES, axis=1)
        return wc, bst

    def sb_gains(j):
        nh = d // HEAD
        return jnp.tile(w["sb_q_gain"][j], nh)[None, :], jnp.tile(w["sb_k_gain"][j], nh)[None, :]

    saved = []
    cur = x
    for i in range(depth):
        kind, j = i % 3, i // 3
        gmix = w["mix_norm"][i:i + 1]
        if kind == 0:
            qg, kg = sb_gains(j)
            cur, sv = sb_fwd(cur, gmix, w["sb_w_qkv"][j], qg, kg, lambda j=j: w["sb_w_o"][j], bd, f"{i}", plan)
        elif kind == 1:
            wc, bst = gm_params(j)
            cur, sv = gm_fwd(cur, gmix, w["gm_w_in"][j], w["gm_b_in"][j:j + 1], w["gm_v_gain"][j:j + 1], wc, bst,
                             w["gm_w_out"][j], f"{i}")
        else:
            cur, sv = ssm_fwd(cur, gmix, ssm_params(j), f"{i}", plan)
        cur, sv2 = ffn_fwd(cur, w["ffn_norm"][i:i + 1], w["ffn_w_gu"][i], w["ffn_w_down"][i], f"{i}", plan)
        saved.append((sv, sv2))

    loss, dcur = loss_and_grad(cur, target, "loss")

    grads = {k: [None] * len(v) for k, v in w.items()}
    for i in reversed(range(depth)):
        kind, j = i % 3, i // 3
        sv, sv2 = saved[i]
        gmix = w["mix_norm"][i:i + 1]
        dcur, dgf, dwgu, dwdown = ffn_bwd(dcur, sv2, w["ffn_norm"][i:i + 1], w["ffn_w_gu"][i], w["ffn_w_down"][i], f"{i}")
        grads["ffn_norm"][i], grads["ffn_w_gu"][i], grads["ffn_w_down"][i] = dgf[0], dwgu, dwdown
        plan.grads_ready({("ffn_w_gu", i): dwgu, ("ffn_w_down", i): dwdown})
        if kind == 0:
            qg, kg = sb_gains(j)
            dcur, dg, dwqkv, dqg, dkg, dwo = sb_bwd(dcur, sv, gmix, w["sb_w_qkv"][j], qg, kg, w["sb_w_o"][j], bd, f"{i}", plan)
            grads["sb_w_qkv"][j], grads["sb_q_gain"][j], grads["sb_k_gain"][j], grads["sb_w_o"][j] = dwqkv, dqg, dkg, dwo
        elif kind == 1:
            wc, bst = gm_params(j)
            dcur, dg, dwin, dbin, dvg, dws, dbs, dwout = gm_bwd(dcur, sv, gmix, w["gm_w_in"][j], w["gm_v_gain"][j:j + 1],
                                                                 wc, bst, w["gm_w_out"][j], f"{i}")
            grads["gm_w_in"][j], grads["gm_b_in"][j], grads["gm_v_gain"][j] = dwin, dbin[0], dvg[0]
            grads["gm_w_s"][j], grads["gm_b_s"][j], grads["gm_w_out"][j] = dws, dbs, dwout
        else:
            dcur, dg, gs = ssm_bwd(dcur, sv, gmix, ssm_params(j), f"{i}", plan)
            grads["ssm_w_in"][j], grads["ssm_conv_w"][j], grads["ssm_conv_b"][j] = gs["w_in"], gs["conv_w"], gs["conv_b"][0]
            grads["ssm_dt_bias"][j], grads["ssm_a_log"][j], grads["ssm_d"][j] = gs["dt_bias"], gs["a_log"], gs["d"]
            grads["ssm_norm_gain"][j], grads["ssm_w_out"][j] = gs["norm_gain"][0], gs["w_out"]
        grads["mix_norm"][i] = dg[0]
        mixer = {0: ("sb_w_qkv", "sb_w_o"), 1: ("gm_w_in", "gm_w_out"), 2: ("ssm_w_in", "ssm_w_out")}[kind]
        plan.grads_ready({(n, j): grads[n][j] for n in mixer})
    grads = {k: (v if k in MATRICES else jnp.stack(v)) for k, v in grads.items()}
    return loss, dcur[0], grads


WEIGHTS = ["mix_norm", "ffn_norm", "sb_w_qkv", "sb_q_gain", "sb_k_gain", "sb_w_o", "gm_w_in", "gm_b_in", "gm_v_gain",
           "gm_w_s", "gm_b_s", "gm_w_out", "ssm_w_in", "ssm_conv_w", "ssm_conv_b", "ssm_dt_bias", "ssm_a_log", "ssm_d",
           "ssm_norm_gain", "ssm_w_out", "ffn_w_gu", "ffn_w_down"]
SHARDED = {"sb_w_qkv": 2, "sb_w_o": 1, "gm_w_in": 2, "gm_w_out": 1, "ssm_w_in": 2, "ssm_conv_w": 2, "ssm_conv_b": 1,
           "ssm_norm_gain": 1, "ssm_w_out": 1, "ffn_w_gu": 2, "ffn_w_down": 1}
EXACT = ("ssm_conv_w", "ssm_conv_b", "ssm_norm_gain")
MATRICES = tuple(n for n in SHARDED if n not in EXACT)
COLUMN_BLOCKS = ("sb_w_qkv", "gm_w_in", "ffn_w_gu")
REPLICATED = [n for n in WEIGHTS if n not in SHARDED]
N_CHIPS = 4
N_DEV = 8
PACK_COLS = 1024


def _pack(pieces, dtype, align):
    flat = jnp.concatenate([p.reshape(-1).astype(dtype) for p in pieces])
    rows = -(-flat.shape[0] // (PACK_COLS * align)) * align
    flat = jnp.pad(flat, (0, rows * PACK_COLS - flat.shape[0]))
    return flat.reshape(rows, PACK_COLS)


def _unpack(flat, shapes):
    out, off = [], 0
    for shp in shapes:
        n = math.prod(shp)
        out.append(flat[off:off + n].reshape(shp))
        off += n
    return out


ANY = pl.BlockSpec(memory_space=pl.ANY)


def _pos():
    return lax.axis_index("x"), lax.axis_index("y"), lax.axis_index("c")


def _remote(src, dst, send, recv, k, to):
    return pltpu.make_async_remote_copy(src_ref=src, dst_ref=dst, send_sem=send.at[k], recv_sem=recv.at[k],
                                        device_id=to, device_id_type=MESH_ID)


def _comm_call(body, name, ins, out_shapes, nsem, aliases=None):
    return pl.pallas_call(
        body, name=name, out_shape=out_shapes,
        in_specs=[ANY] * len(ins), out_specs=[ANY] * len(out_shapes),
        scratch_shapes=[pltpu.SemaphoreType.DMA((nsem,)), pltpu.SemaphoreType.DMA((nsem,))],
        input_output_aliases=aliases or {},
    )(*ins)


def stage_shard(w, layer, chip, name, dtype=BF16):
    _, rows, cols = w.shape
    tr = _pick(rows, (512, 352, 256, 128))

    def kern(idx_ref, w_ref, o_ref):
        o_ref[...] = w_ref[...].astype(dtype)

    grid_spec = pltpu.PrefetchScalarGridSpec(
        num_scalar_prefetch=1, grid=(rows // tr,),
        in_specs=[pl.BlockSpec((None, tr, cols), lambda i, idx: (layer, i, 0))],
        out_specs=pl.BlockSpec((None, tr, cols), lambda i, idx: (idx[0], i, 0)))
    return pl.pallas_call(
        kern, name=name, grid_spec=grid_spec,
        out_shape=jax.ShapeDtypeStruct((N_CHIPS, rows, cols), dtype),
        compiler_params=_params(("parallel",)),
    )(jnp.reshape(chip, (1,)).astype(jnp.int32), w)


class Side:
    def __init__(self, arrays, out_shapes, aliases, nsem, start, finish):
        self.arrays, self.out_shapes, self.aliases, self.nsem = list(arrays), list(out_shapes), aliases, nsem
        self.start, self.finish = start, finish


def run_side(side, name):
    n_in, n_out = len(side.arrays), len(side.out_shapes)

    def body(*refs):
        ins, outs = refs[:n_in], refs[n_in:n_in + n_out]
        send, recv = refs[n_in + n_out:]
        side.start(ins, outs, send, recv)
        side.finish(ins, outs, send, recv)

    return _comm_call(body, name, side.arrays, side.out_shapes, side.nsem, aliases=side.aliases)


def side_call(kern, side, *, name, grid, in_specs, out_specs, out_shape, scratch_shapes, args):
    if side is None:
        res = pl.pallas_call(kern, name=name, grid=grid, in_specs=in_specs, out_specs=out_specs, out_shape=out_shape,
                             scratch_shapes=scratch_shapes,
                             compiler_params=_params(("parallel",) + ("arbitrary",) * (len(grid) - 1)))(*args)
        return list(res), []
    n_in, n_out, n_scr = len(in_specs), len(out_specs), len(scratch_shapes)
    s_in, s_out = len(side.arrays), len(side.out_shapes)

    def body(*refs):
        ins, refs = refs[:n_in], refs[n_in:]
        side_ins, refs = refs[:s_in], refs[s_in:]
        outs, refs = refs[:n_out], refs[n_out:]
        side_outs, refs = refs[:s_out], refs[s_out:]
        scr, (send, recv) = refs[:n_scr], refs[n_scr:]
        first, last = None, None
        for axis, size in enumerate(grid):
            at0, at1 = pl.program_id(axis) == 0, pl.program_id(axis) == size - 1
            first = at0 if first is None else first & at0
            last = at1 if last is None else last & at1

        @pl.when(first)
        def _():
            side.start(side_ins, side_outs, send, recv)

        kern(*ins, *outs, *scr)

        @pl.when(last)
        def _():
            side.finish(side_ins, side_outs, send, recv)

    res = pl.pallas_call(
        body, name=name, grid=grid,
        in_specs=list(in_specs) + [ANY] * s_in, out_specs=list(out_specs) + [ANY] * s_out,
        out_shape=list(out_shape) + side.out_shapes,
        scratch_shapes=list(scratch_shapes) + [pltpu.SemaphoreType.DMA((side.nsem,)), pltpu.SemaphoreType.DMA((side.nsem,))],
        input_output_aliases={n_in + a: n_out + b for a, b in side.aliases.items()},
        compiler_params=_params(("arbitrary",) * len(grid)),
    )(*args, *side.arrays)
    return list(res[:n_out]), list(res[n_out:])


def gather_side(staged):
    n = len(staged)

    def plan(o_refs, send, recv):
        x, y, c = _pos()
        chips = [(1 - x, y), (x, 1 - y), (1 - x, 1 - y)]

        def part(u, chip, cc):
            half = staged[u].shape[1] // 2
            return o_refs[u].at[2 * chip[0] + chip[1], pl.ds(cc * half, half), :]

        first = [_remote(part(u, (x, y), c), part(u, (x, y), c), send, recv, 6 * u + j, (*chip, c))
                 for u in range(n) for j, chip in enumerate(chips)]
        landed = [_remote(part(u, chip, c), part(u, chip, c), send, recv, 6 * u + j, (x, y, c))
                  for u in range(n) for j, chip in enumerate(chips)]
        passed = [_remote(part(u, chip, c), part(u, chip, c), send, recv, 6 * u + 3 + j, (x, y, 1 - c))
                  for u in range(n) for j, chip in enumerate(chips)]
        handed = [_remote(part(u, chip, 1 - c), part(u, chip, 1 - c), send, recv, 6 * u + 3 + j, (x, y, c))
                  for u in range(n) for j, chip in enumerate(chips)]
        return first, landed, passed, handed

    def start(ins, outs, send, recv):
        for cp in plan(outs, send, recv)[0]:
            cp.start()

    def finish(ins, outs, send, recv):
        first, landed, passed, handed = plan(outs, send, recv)
        for got, fw in zip(landed, passed):
            got.wait_recv()
            fw.start()
        for got in handed:
            got.wait_recv()
        for cp in first + passed:
            cp.wait_send()

    outs = [jax.ShapeDtypeStruct(s.shape, s.dtype) for s in staged]
    return Side(staged, outs, {u: u for u in range(n)}, 6 * n, start, finish)


def swap_halves(gps, name):
    n = len(gps)

    def body(*refs):
        g_refs, r_refs = refs[:n], refs[n:2 * n]
        send, recv = refs[2 * n:]
        x, y, c = _pos()
        cps = []
        for u in range(n):
            half = gps[u].shape[1] // 2
            cps.append(_remote(g_refs[u].at[:, pl.ds((1 - c) * half, half), :], r_refs[u], send, recv, u, (x, y, 1 - c)))
        for cp in cps:
            cp.start()
        for cp in cps:
            cp.wait()

    outs = [jax.ShapeDtypeStruct((g.shape[0], g.shape[1] // 2, g.shape[2]), g.dtype) for g in gps]
    return _comm_call(body, name, gps, outs, n)


def scatter_side(parts):
    n = len(parts)

    def plan(p_refs, r_refs, send, recv):
        x, y, c = _pos()
        chips = [(1 - x, y), (x, 1 - y), (1 - x, 1 - y)]
        return [_remote(p_refs[u].at[2 * chip[0] + chip[1]], r_refs[u].at[j], send, recv, 3 * u + j, (*chip, c))
                for u in range(n) for j, chip in enumerate(chips)]

    def start(ins, outs, send, recv):
        for cp in plan(ins, outs, send, recv):
            cp.start()

    def finish(ins, outs, send, recv):
        for cp in plan(ins, outs, send, recv):
            cp.wait()

    outs = [jax.ShapeDtypeStruct((N_CHIPS - 1,) + p.shape[1:], p.dtype) for p in parts]
    return Side(parts, outs, {}, 3 * n, start, finish)


def join_halves(bufs):
    n = len(bufs)

    def body(*refs):
        o_refs = refs[n:2 * n]
        send, recv = refs[2 * n:]
        x, y, c = _pos()

        def rows(u, cc):
            half = bufs[u].shape[1] // 2
            return o_refs[u].at[:, pl.ds(cc * half, half), :]

        cps = [_remote(rows(u, c), rows(u, c), send, recv, u, (x, y, 1 - c)) for u in range(n)]
        for cp in cps:
            cp.start()
        for u in range(n):
            _remote(rows(u, 1 - c), rows(u, 1 - c), send, recv, u, (x, y, c)).wait_recv()
        for cp in cps:
            cp.wait_send()

    outs = [jax.ShapeDtypeStruct(b.shape, b.dtype) for b in bufs]
    return _comm_call(body, "join_halves", bufs, outs, n, aliases={u: u for u in range(n)})


def gather_small(sg, name):
    rows, cols = sg.shape

    def body(s_ref, o_ref, send, recv, lsem):
        x, y, c = _pos()
        me, sibling = (x, y, c), (x, y, 1 - c)
        chips = [(1 - x, y), (x, 1 - y), (1 - x, 1 - y)]

        def blk(px, py, pc):
            return o_ref.at[4 * px + 2 * py + pc]

        mine = pltpu.make_async_copy(s_ref, blk(*me), lsem)
        mine.start()
        first = [_remote(s_ref, blk(*me), send, recv, 0, sibling)]
        first += [_remote(s_ref, blk(*me), send, recv, 1 + j, (*chip, c)) for j, chip in enumerate(chips)]
        for cp in first:
            cp.start()
        passed = [_remote(blk(*chip, c), blk(*chip, c), send, recv, 4 + j, sibling) for j, chip in enumerate(chips)]
        for j, chip in enumerate(chips):
            _remote(blk(*chip, c), blk(*chip, c), send, recv, 1 + j, me).wait_recv()
            passed[j].start()
        _remote(blk(*sibling), blk(*sibling), send, recv, 0, me).wait_recv()
        for j, chip in enumerate(chips):
            _remote(blk(*chip, 1 - c), blk(*chip, 1 - c), send, recv, 4 + j, me).wait_recv()
        for cp in first + passed:
            cp.wait_send()
        mine.wait()

    return pl.pallas_call(
        body, name=name,
        out_shape=jax.ShapeDtypeStruct((N_DEV, rows, cols), sg.dtype),
        in_specs=[ANY], out_specs=ANY,
        scratch_shapes=[pltpu.SemaphoreType.DMA((N_DEV - 1,)), pltpu.SemaphoreType.DMA((N_DEV - 1,)), pltpu.SemaphoreType.DMA],
    )(sg)


def sum_cores(gp, theirs, core, chip, name):
    nch, rows, cols = gp.shape
    half = rows // 2
    tr = _pick(half, (512, 352, 256, 176, 128, 64))
    nb = half // tr

    def kern(idx_ref, g_ref, t_ref, own_ref, all_ref):
        k = pl.program_id(1)
        s = g_ref[...].astype(F32) + t_ref[...].astype(F32)
        all_ref[...] = s.astype(BF16)

        @pl.when(k == idx_ref[1])
        def _():
            own_ref[...] = s

    grid_spec = pltpu.PrefetchScalarGridSpec(
        num_scalar_prefetch=1, grid=(nb, nch),
        in_specs=[pl.BlockSpec((None, tr, cols), lambda i, k, idx: (k, idx[0] * nb + i, 0)),
                  pl.BlockSpec((None, tr, cols), lambda i, k, idx: (k, i, 0))],
        out_specs=[pl.BlockSpec((tr, cols), lambda i, k, idx: (i, 0)),
                   pl.BlockSpec((None, tr, cols), lambda i, k, idx: (k, i, 0))])
    return pl.pallas_call(
        kern, name=name, grid_spec=grid_spec,
        out_shape=[jax.ShapeDtypeStruct((half, cols), F32), jax.ShapeDtypeStruct((nch, half, cols), BF16)],
        compiler_params=_params(("parallel", "arbitrary")),
    )(jnp.stack([core, chip]).astype(jnp.int32), gp, theirs)


def sum_chips(own, others, core, layer, nlayers, into, name):
    half, cols = own.shape
    tr = _pick(half, (512, 352, 256, 176, 128, 64))
    nb = half // tr

    def kern(idx_ref, o_ref, a_ref, b_ref, c_ref, *rest):
        out_ref = rest[-1]
        out_ref[...] = ((o_ref[...] + a_ref[...].astype(F32)) + b_ref[...].astype(F32)) + c_ref[...].astype(F32)

    grid_spec = pltpu.PrefetchScalarGridSpec(
        num_scalar_prefetch=1, grid=(nb,),
        in_specs=[pl.BlockSpec((tr, cols), lambda i, idx: (i, 0))] +
                 [pl.BlockSpec((None, tr, cols), lambda i, idx, j=j: (j, i, 0)) for j in range(N_CHIPS - 1)] +
                 ([] if into is None else [pl.BlockSpec(memory_space=pl.ANY)]),
        out_specs=pl.BlockSpec((None, tr, cols), lambda i, idx: (layer, idx[0] * nb + i, 0)))
    args = [jnp.reshape(core, (1,)).astype(jnp.int32), own, others, others, others] + ([] if into is None else [into])
    return pl.pallas_call(
        kern, name=name, grid_spec=grid_spec,
        out_shape=jax.ShapeDtypeStruct((nlayers, 2 * half, cols), F32),
        input_output_aliases={} if into is None else {len(args) - 1: 0},
        compiler_params=_params(("parallel",)),
    )(*args)


def small_update(gath, w, m, v, name):
    def fn(*vs):
        g = vs[0]
        for t in vs[1:N_DEV]:
            g = g + t
        wv, mv, vv = vs[N_DEV:]
        m2 = ADAM_B1 * mv + (1.0 - ADAM_B1) * g
        v2 = ADAM_B2 * vv + (1.0 - ADAM_B2) * (g * g)
        m_hat = m2 / (1.0 - ADAM_B1 ** ADAM_STEP)
        v_hat = v2 / (1.0 - ADAM_B2 ** ADAM_STEP)
        return g, -ADAM_LR * (m_hat / (jnp.sqrt(v_hat) + ADAM_EPS) + ADAM_WD * wv), m2, v2

    c = w.shape[1]
    ins = [(gath[k], "row") for k in range(N_DEV)] + [(w, "row"), (m, "row"), (v, "row")]
    return rowwise(fn, ins, [(c, F32)] * 4, tr=w.shape[0] // 2, name=name)


_MIX = {0: [("sb_w_qkv", 0), ("sb_w_o", 0)], 1: [("gm_w_in", 0), ("gm_w_out", 0)],
        2: [("ssm_w_in", 0), ("ssm_w_out", 0)], 3: [("sb_w_qkv", 1), ("sb_w_o", 1)]}
_FFN = {i: [("ffn_w_gu", i), ("ffn_w_down", i)] for i in range(4)}
GATHER_FIRST = _MIX[0][:1]
GATHER_AT = {"sb_attn_0": _MIX[0][1:] + _FFN[0] + _FFN[1],
             "ffn_gu_0": _MIX[1], "ffn_down_0": _MIX[2][1:], "ffn_gu_1": _MIX[2][:1], "ffn_down_1": _FFN[2][1:],
             "ssm_scan_2": _FFN[2][:1] + _MIX[3] + _FFN[3][1:], "ffn_gu_2": _FFN[3][:1]}
SCATTER_AT = {"ssm_dscan_2": _FFN[3] + _MIX[3] + _FFN[2], "sb_dattn_0": _MIX[2] + _FFN[1] + _MIX[1] + _FFN[0]}
SCATTER_LAST = _MIX[0]


class _Plan:
    def __init__(self, ins, core, chip, vectors):
        self.core, self.chip = core, chip
        self.staged = {(n, l): stage_shard(ins[n], l, chip, f"stage_{n}_{l}")
                       for n in MATRICES for l in range(ins[n].shape[0])}
        self.full = {n: [None] * ins[n].shape[0] for n in MATRICES}
        self.ready = {}
        self.parts = {}
        self.halves = {}
        self.layers = {n: ins[n].shape[0] for n in MATRICES}
        self.swaps = 0
        first = [self.staged[u] for u in GATHER_FIRST] + [stage_shard(vectors, 0, chip, "stage_vectors", F32)]
        *gathered, self.vectors = run_side(gather_side(first), "gather_first")
        self._fill(GATHER_FIRST, gathered)

    def _fill(self, units, gathered):
        for (n, l), g in zip(units, gathered):
            if n in COLUMN_BLOCKS:
                self.full[n][l] = g
            elif n == "ssm_w_in":
                self.full[n][l] = jnp.concatenate([g[k] for k in range(N_CHIPS)], axis=1)
            else:
                self.full[n][l] = g.reshape(-1, g.shape[-1])

    def _prepare(self, units):
        gps = [self.ready[u] for u in units]
        theirs = swap_halves(gps, f"swap_halves_{self.swaps}")
        self.swaps += 1
        for (n, l), g, t in zip(units, gps, theirs):
            self.parts[(n, l)] = sum_cores(g, t, self.core, self.chip, f"sum_cores_{n}_{l}")

    def _reduce(self, units, others):
        for (n, l), other in zip(units, others):
            self.halves[n] = sum_chips(self.parts[(n, l)][0], other, self.core, l, self.layers[n], self.halves.get(n),
                                       f"sum_chips_{n}_{l}")

    def side(self, tag):
        if tag in GATHER_AT:
            return gather_side([self.staged[u] for u in GATHER_AT[tag]])
        if tag in SCATTER_AT:
            self._prepare(SCATTER_AT[tag])
            return scatter_side([self.parts[u][1] for u in SCATTER_AT[tag]])
        return None

    def done(self, tag, results):
        if tag in GATHER_AT:
            self._fill(GATHER_AT[tag], results)
        else:
            self._reduce(SCATTER_AT[tag], results)

    def grads_ready(self, grads):
        for (n, l), g in grads.items():
            if n in COLUMN_BLOCKS:
                self.ready[(n, l)] = g
            elif n == "ssm_w_in":
                self.ready[(n, l)] = jnp.stack(jnp.split(g, N_CHIPS, axis=1))
            else:
                self.ready[(n, l)] = g.reshape(N_CHIPS, -1, g.shape[-1])

    def shard_grads(self):
        self._prepare(SCATTER_LAST)
        self._reduce(SCATTER_LAST, run_side(scatter_side([self.parts[u][1] for u in SCATTER_LAST]), "scatter_last"))
        names = sorted(self.halves)
        return dict(zip(names, join_halves([self.halves[n] for n in names])))


def _step(ins):
    x, target = ins["x"][0], ins["loss_target"][0]
    core = lax.axis_index("c")
    chip = 2 * lax.axis_index("x") + lax.axis_index("y")

    def lane_pad(v):
        return jnp.pad(v, ((0, 0), (0, PACK_COLS - v.shape[1])))

    vec_rows = [ins["ssm_conv_w"][0], ins["ssm_conv_b"], lane_pad(ins["ssm_norm_gain"])]
    blk = jnp.concatenate(vec_rows + [jnp.zeros((2 * SUBLANES - 6, PACK_COLS), F32)], axis=0)
    plan = _Plan(ins, core, chip, blk[None])
    per_chip = plan.vectors
    ngw = ins["ssm_norm_gain"].shape[1]
    full = {
        "ssm_conv_w": jnp.concatenate([per_chip[k, 0:4] for k in range(N_CHIPS)], axis=1)[None],
        "ssm_conv_b": jnp.concatenate([per_chip[k, 4:5] for k in range(N_CHIPS)], axis=1),
        "ssm_norm_gain": jnp.concatenate([per_chip[k, 5:6, :ngw] for k in range(N_CHIPS)], axis=1),
    }

    full.update(plan.full)
    for n in REPLICATED:
        full[n] = ins[n]

    loss, dx, grads = local_step(x, target, full, plan)
    loss = lax.psum(loss, ALL_AXES)
    gshards = plan.shard_grads()

    small_shapes = [ins[n].shape for n in REPLICATED]
    vec_shapes = [grads[n].shape for n in EXACT]
    vec_pack = _pack([grads[n] for n in EXACT], F32, SUBLANES)
    gath = gather_small(jnp.concatenate([_pack([grads[n] for n in REPLICATED], F32, SUBLANES), vec_pack], axis=0),
                        "gather_small")
    packed = [jnp.concatenate([_pack([ins[pre + n] for n in REPLICATED], F32, SUBLANES), jnp.zeros_like(vec_pack)], axis=0)
              for pre in ("", "m_", "v_")]
    res = small_update(gath, *packed, name="small_update")
    nrep = res[0].shape[0] - vec_pack.shape[0]
    small = [dict(zip(REPLICATED, _unpack(r[:nrep].reshape(-1), small_shapes))) for r in res]
    vec_g = dict(zip(EXACT, _unpack(res[0][nrep:].reshape(-1), vec_shapes)))

    out_g, out_d, out_m, out_v = {}, {}, {}, {}
    for n in REPLICATED:
        out_g[n], out_d[n], out_m[n], out_v[n] = (s[n] for s in small)
    for n in SHARDED:
        shp = ins[n].shape
        if n in EXACT:
            g = lax.dynamic_slice_in_dim(vec_g[n], chip * shp[-1], shp[-1], axis=vec_g[n].ndim - 1)
        else:
            g = gshards[n]
        two = (math.prod(shp[:-1]), shp[-1])
        d2, m2, v2, g2 = adamw(ins[n].reshape(two), g.reshape(two), ins["m_" + n].reshape(two),
                               ins["v_" + n].reshape(two), f"adamw_{n}")
        out_g[n], out_d[n], out_m[n], out_v[n] = g2.reshape(shp), d2.reshape(shp), m2.reshape(shp), v2.reshape(shp)
    return (loss, dx[None], *[out_g[n] for n in WEIGHTS], *[out_d[n] for n in WEIGHTS],
            *[out_m[n] for n in WEIGHTS], *[out_v[n] for n in WEIGHTS])


def kernel(x, mix_norm, ffn_norm, sb_w_qkv, sb_q_gain, sb_k_gain, sb_w_o, gm_w_in, gm_b_in, gm_v_gain, gm_w_s, gm_b_s, gm_w_out, ssm_w_in, ssm_conv_w, ssm_conv_b, ssm_dt_bias, ssm_a_log, ssm_d, ssm_norm_gain, ssm_w_out, ffn_w_gu, ffn_w_down, loss_target, m_mix_norm, m_ffn_norm, m_sb_w_qkv, m_sb_q_gain, m_sb_k_gain, m_sb_w_o, m_gm_w_in, m_gm_b_in, m_gm_v_gain, m_gm_w_s, m_gm_b_s, m_gm_w_out, m_ssm_w_in, m_ssm_conv_w, m_ssm_conv_b, m_ssm_dt_bias, m_ssm_a_log, m_ssm_d, m_ssm_norm_gain, m_ssm_w_out, m_ffn_w_gu, m_ffn_w_down, v_mix_norm, v_ffn_norm, v_sb_w_qkv, v_sb_q_gain, v_sb_k_gain, v_sb_w_o, v_gm_w_in, v_gm_b_in, v_gm_v_gain, v_gm_w_s, v_gm_b_s, v_gm_w_out, v_ssm_w_in, v_ssm_conv_w, v_ssm_conv_b, v_ssm_dt_bias, v_ssm_a_log, v_ssm_d, v_ssm_norm_gain, v_ssm_w_out, v_ffn_w_gu, v_ffn_w_down):
    return _step(dict(locals()))
```
